```python
import jax, jax.numpy as jnp
from jax import lax
import numpy as np

D_MODEL = 1024
BATCH = 8
SEQ = 8192
DEPTH = 1

D_RNN = 1024
RNN_BLOCKS = 8
RNN_BLOCK_W = D_RNN // RNN_BLOCKS
CONV_W = 4
LRU_C = 8.0
N_HEADS = 8
HEAD_DIM = 128
D_ATTN = N_HEADS * HEAD_DIM
ROT_DIM = HEAD_DIM // 4
ROPE_THETA = 500000.0
DILATED_GROUPS = ((128, 1), (512, 4), (2048, 16))
NEG_INF = -1e30
IN_WIDTHS = (D_RNN, D_RNN, D_ATTN, D_ATTN, D_ATTN, D_ATTN, D_MODEL, D_MODEL)
D_IN = sum(IN_WIDTHS)
IN_SPLITS = [int(s) for s in np.cumsum(IN_WIDTHS)[:-1]]
NORM_EPS = 1e-6

kernel_name = "hybrid_rglru_dilated_attn_block"


def rms_norm(x, g):
    xf = x.astype(jnp.float32)
    y = xf * lax.rsqrt(jnp.mean(xf * xf, axis=-1, keepdims=True) + NORM_EPS)
    return (y * g.astype(jnp.float32)).astype(x.dtype)


def causal_depthwise_conv(x, w, b):
    y = lax.conv_general_dilated(
        x, w[:, None, :], window_strides=(1,), padding=[(CONV_W - 1, 0)],
        dimension_numbers=("NWC", "WIO", "NWC"), feature_group_count=x.shape[-1])
    return y + b


def rg_lru(x, w_a, b_a, w_x, b_x, lam, positions):
    B, S, _ = x.shape
    xf = x.astype(jnp.float32)
    xh = xf.reshape(B, S, RNN_BLOCKS, RNN_BLOCK_W)
    r = jax.nn.sigmoid(jnp.einsum("bshi,hij->bshj", xh, w_a.astype(jnp.float32)) + b_a.astype(jnp.float32))
    i = jax.nn.sigmoid(jnp.einsum("bshi,hij->bshj", xh, w_x.astype(jnp.float32)) + b_x.astype(jnp.float32))
    r = r.reshape(B, S, D_RNN)
    i = i.reshape(B, S, D_RNN)
    log_a = -LRU_C * r * jax.nn.softplus(-lam.astype(jnp.float32))
    reset = (positions == 0)[..., None]
    a = jnp.where(reset, 0.0, jnp.exp(log_a))
    mult = jnp.where(reset, 1.0, jnp.sqrt(-jnp.expm1(2.0 * log_a)))
    bx = mult * i * xf

    def combine(left, right):
        a1, b1 = left
        a2, b2 = right
        return a1 * a2, a2 * b1 + b2

    _, h = lax.associative_scan(combine, (a, bx), axis=1)
    return h


def apply_partial_rope(t, cos, sin):
    tf = t.astype(jnp.float32)
    half = ROT_DIM // 2
    x1 = tf[..., :half]
    x2 = tf[..., half:ROT_DIM]
    return jnp.concatenate([x1 * cos - x2 * sin, x2 * cos + x1 * sin, tf[..., ROT_DIM:]], axis=-1)


def dilated_window_attention(q, k, v, window, dilation):
    B, S, H, Dh = q.shape
    blk = window // dilation
    span = blk * dilation
    s_pad = -(-S // span) * span
    nb = s_pad // span

    def to_blocks(t):
        t = jnp.pad(t, ((0, 0), (0, s_pad - S), (0, 0), (0, 0)))
        return t.reshape(B, nb, blk, dilation, H, Dh)

    qb, kb, vb = to_blocks(q), to_blocks(k), to_blocks(v)
    pad_prev = ((0, 0), (1, 0), (0, 0), (0, 0), (0, 0), (0, 0))
    kk = jnp.concatenate([jnp.pad(kb[:, :-1], pad_prev), kb], axis=2)
    vv = jnp.concatenate([jnp.pad(vb[:, :-1], pad_prev), vb], axis=2)
    s = jnp.einsum("bnqrhd,bnkrhd->bnrhqk", qb, kk) * (HEAD_DIM ** -0.5)
    qi = jnp.arange(blk)[:, None]
    ki = jnp.arange(2 * blk)[None, :]
    dist = blk + qi - ki
    band = (dist >= 0) & (dist <= blk)
    n_idx = jnp.arange(nb)[:, None, None]
    valid = band[None] & ((n_idx > 0) | (ki[None] >= blk))
    s = jnp.where(valid[None, :, None, None], s, NEG_INF)
    lse = jax.nn.logsumexp(s, axis=-1)
    p = jnp.exp(s - lse[..., None])
    o = jnp.einsum("bnrhqk,bnkrhd->bnqrhd", p, vv)
    o = o.reshape(B, s_pad, H, Dh)[:, :S]
    lse = lse.transpose(0, 1, 4, 2, 3).reshape(B, s_pad, H)[:, :S]
    return o, lse


def dilated_attention_mixture(q, k, v):
    outs, lses = [], []
    for window, dilation in DILATED_GROUPS:
        o, l = dilated_window_attention(q, k, v, window, dilation)
        outs.append(o)
        lses.append(l)
    w = jax.nn.softmax(jnp.stack(lses, axis=0), axis=0)
    return jnp.einsum("gbsh,gbshd->bshd", w, jnp.stack(outs, axis=0))


def _fwd_setup_inputs(seed: int = 0) -> dict:
    key = jax.random.key(seed)
    ks = jax.random.split(key, 20)
    f32 = jnp.float32
    nrm = lambda k, shape, scale: jax.random.normal(k, shape, f32) * scale
    x = jax.random.normal(ks[0], (BATCH, SEQ, D_MODEL), f32)
    c = jax.random.normal(ks[1], (BATCH, D_MODEL), f32)
    positions = jnp.broadcast_to(jnp.arange(SEQ, dtype=jnp.int32)[None, :], (BATCH, SEQ))
    g_norm = 1.0 + nrm(ks[2], (DEPTH, D_MODEL), 0.02)
    w_mod = nrm(ks[3], (DEPTH, D_MODEL, 3 * D_MODEL), 0.5 * D_MODEL ** -0.5)
    b_mod = nrm(ks[4], (DEPTH, 3 * D_MODEL), 0.01)
    w_in = nrm(ks[5], (DEPTH, D_MODEL, D_IN), D_MODEL ** -0.5)
    b_gate = nrm(ks[6], (DEPTH, 2 * D_MODEL), 0.01)
    conv_w = nrm(ks[7], (DEPTH, CONV_W, D_RNN), CONV_W ** -0.5)
    conv_b = nrm(ks[8], (DEPTH, D_RNN), 0.01)
    w_a = nrm(ks[9], (DEPTH, RNN_BLOCKS, RNN_BLOCK_W, RNN_BLOCK_W), RNN_BLOCK_W ** -0.5)
    b_a = nrm(ks[10], (DEPTH, RNN_BLOCKS, RNN_BLOCK_W), 0.01)
    w_x = nrm(ks[11], (DEPTH, RNN_BLOCKS, RNN_BLOCK_W, RNN_BLOCK_W), RNN_BLOCK_W ** -0.5)
    b_x = nrm(ks[12], (DEPTH, RNN_BLOCKS, RNN_BLOCK_W), 0.01)
    a0 = jax.random.uniform(ks[13], (DEPTH, D_RNN), f32, 0.9, 0.999)
    sig = a0 ** (1.0 / LRU_C)
    lam = jnp.log(sig) - jnp.log1p(-sig)
    w_out_rnn = nrm(ks[14], (DEPTH, D_RNN, D_MODEL), D_RNN ** -0.5)
    w_out_attn = nrm(ks[15], (DEPTH, D_ATTN, D_MODEL), D_ATTN ** -0.5)
    w_o = nrm(ks[16], (DEPTH, D_MODEL, D_MODEL), D_MODEL ** -0.5)
    g_final = 1.0 + nrm(ks[17], (D_MODEL,), 0.02)
    return {"x": x, "c": c, "positions": positions, "g_norm": g_norm, "w_mod": w_mod,
            "b_mod": b_mod, "w_in": w_in, "b_gate": b_gate, "conv_w": conv_w, "conv_b": conv_b,
            "w_a": w_a, "b_a": b_a, "w_x": w_x, "b_x": b_x, "lam": lam, "w_out_rnn": w_out_rnn,
            "w_out_attn": w_out_attn, "w_o": w_o, "g_final": g_final}


def _fwd_reference(x, c, positions, g_norm, w_mod, b_mod, w_in, b_gate, conv_w, conv_b,
              w_a, b_a, w_x, b_x, lam, w_out_rnn, w_out_attn, w_o, g_final):
    B, S, _ = x.shape
    dt = x.dtype
    inv_freq = ROPE_THETA ** (-jnp.arange(0, ROT_DIM, 2, dtype=jnp.float32) / ROT_DIM)
    ang = positions.astype(jnp.float32)[..., None] * inv_freq
    cos = jnp.cos(ang)[:, :, None, :]
    sin = jnp.sin(ang)[:, :, None, :]
    c_act = jax.nn.silu(c)
    for l in range(DEPTH):
        mod = c_act @ w_mod[l] + b_mod[l]
        shift, scale, gate = jnp.split(mod, 3, axis=-1)
        h = rms_norm(x, g_norm[l]) * (1.0 + scale[:, None, :]) + shift[:, None, :]
        proj = h @ w_in[l]
        x_rnn, z_rnn, q, k, v, z_attn, g_r, g_a = jnp.split(proj, IN_SPLITS, axis=-1)
        xc = causal_depthwise_conv(x_rnn, conv_w[l], conv_b[l])
        hr = rg_lru(xc, w_a[l], b_a[l], w_x[l], b_x[l], lam[l], positions)
        y_rnn = (hr * jax.nn.silu(z_rnn.astype(jnp.float32))).astype(dt) @ w_out_rnn[l]
        qh = apply_partial_rope(q.reshape(B, S, N_HEADS, HEAD_DIM), cos, sin)
        kh = apply_partial_rope(k.reshape(B, S, N_HEADS, HEAD_DIM), cos, sin)
        vh = v.reshape(B, S, N_HEADS, HEAD_DIM).astype(jnp.float32)
        o = dilated_attention_mixture(qh, kh, vh).reshape(B, S, D_ATTN)
        y_attn = (o * jax.nn.silu(z_attn.astype(jnp.float32))).astype(dt) @ w_out_attn[l]
        bg_r, bg_a = jnp.split(b_gate[l], 2, axis=-1)
        merged = jax.nn.sigmoid(g_r + bg_r) * y_rnn + jax.nn.sigmoid(g_a + bg_a) * y_attn
        x = x + gate[:, None, :] * (merged @ w_o[l])
    return rms_norm(x, g_final)


import jax as _jax
import jax.numpy as _jnp

TWIN_FORMAT = 'train_step'
FWD_PARAMS = ['x', 'c', 'positions', 'g_norm', 'w_mod', 'b_mod', 'w_in', 'b_gate', 'conv_w', 'conv_b', 'w_a', 'b_a', 'w_x', 'b_x', 'lam', 'w_out_rnn', 'w_out_attn', 'w_o', 'g_final']
TWIN_WEIGHTS = ['g_norm', 'w_mod', 'b_mod', 'w_in', 'b_gate', 'conv_w', 'conv_b', 'w_a', 'b_a', 'w_x', 'b_x', 'lam', 'w_out_rnn', 'w_out_attn', 'w_o', 'g_final']
TWIN_DIFF_INPUT = 'x'
TWIN_INPUTS = ['x', 'c', 'positions', 'g_norm', 'w_mod', 'b_mod', 'w_in', 'b_gate', 'conv_w', 'conv_b', 'w_a', 'b_a', 'w_x', 'b_x', 'lam', 'w_out_rnn', 'w_out_attn', 'w_o', 'g_final', 'loss_target', 'm_g_norm', 'm_w_mod', 'm_b_mod', 'm_w_in', 'm_b_gate', 'm_conv_w', 'm_conv_b', 'm_w_a', 'm_b_a', 'm_w_x', 'm_b_x', 'm_lam', 'm_w_out_rnn', 'm_w_out_attn', 'm_w_o', 'm_g_final', 'v_g_norm', 'v_w_mod', 'v_b_mod', 'v_w_in', 'v_b_gate', 'v_conv_w', 'v_conv_b', 'v_w_a', 'v_b_a', 'v_w_x', 'v_b_x', 'v_lam', 'v_w_out_rnn', 'v_w_out_attn', 'v_w_o', 'v_g_final']
TWIN_OUTPUTS = ['loss', 'grad_x', 'grad_g_norm', 'grad_w_mod', 'grad_b_mod', 'grad_w_in', 'grad_b_gate', 'grad_conv_w', 'grad_conv_b', 'grad_w_a', 'grad_b_a', 'grad_w_x', 'grad_b_x', 'grad_lam', 'grad_w_out_rnn', 'grad_w_out_attn', 'grad_w_o', 'grad_g_final', 'delta_g_norm', 'delta_w_mod', 'delta_b_mod', 'delta_w_in', 'delta_b_gate', 'delta_conv_w', 'delta_conv_b', 'delta_w_a', 'delta_b_a', 'delta_w_x', 'delta_b_x', 'delta_lam', 'delta_w_out_rnn', 'delta_w_out_attn', 'delta_w_o', 'delta_g_final', 'new_m_g_norm', 'new_m_w_mod', 'new_m_b_mod', 'new_m_w_in', 'new_m_b_gate', 'new_m_conv_w', 'new_m_conv_b', 'new_m_w_a', 'new_m_b_a', 'new_m_w_x', 'new_m_b_x', 'new_m_lam', 'new_m_w_out_rnn', 'new_m_w_out_attn', 'new_m_w_o', 'new_m_g_final', 'new_v_g_norm', 'new_v_w_mod', 'new_v_b_mod', 'new_v_w_in', 'new_v_b_gate', 'new_v_conv_w', 'new_v_conv_b', 'new_v_w_a', 'new_v_b_a', 'new_v_w_x', 'new_v_b_x', 'new_v_lam', 'new_v_w_out_rnn', 'new_v_w_out_attn', 'new_v_w_o', 'new_v_g_final']
TWIN_LEAF_KINDS = {'loss': 'loss', 'grad_x': 'grad_x', 'grad_g_norm': 'grad_w', 'grad_w_mod': 'grad_w', 'grad_b_mod': 'grad_w', 'grad_w_in': 'grad_w', 'grad_b_gate': 'grad_w', 'grad_conv_w': 'grad_w', 'grad_conv_b': 'grad_w', 'grad_w_a': 'grad_w', 'grad_b_a': 'grad_w', 'grad_w_x': 'grad_w', 'grad_b_x': 'grad_w', 'grad_lam': 'grad_w', 'grad_w_out_rnn': 'grad_w', 'grad_w_out_attn': 'grad_w', 'grad_w_o': 'grad_w', 'grad_g_final': 'grad_w', 'delta_g_norm': 'delta_w', 'delta_w_mod': 'delta_w', 'delta_b_mod': 'delta_w', 'delta_w_in': 'delta_w', 'delta_b_gate': 'delta_w', 'delta_conv_w': 'delta_w', 'delta_conv_b': 'delta_w', 'delta_w_a': 'delta_w', 'delta_b_a': 'delta_w', 'delta_w_x': 'delta_w', 'delta_b_x': 'delta_w', 'delta_lam': 'delta_w', 'delta_w_out_rnn': 'delta_w', 'delta_w_out_attn': 'delta_w', 'delta_w_o': 'delta_w', 'delta_g_final': 'delta_w', 'new_m_g_norm': 'new_m', 'new_m_w_mod': 'new_m', 'new_m_b_mod': 'new_m', 'new_m_w_in': 'new_m', 'new_m_b_gate': 'new_m', 'new_m_conv_w': 'new_m', 'new_m_conv_b': 'new_m', 'new_m_w_a': 'new_m', 'new_m_b_a': 'new_m', 'new_m_w_x': 'new_m', 'new_m_b_x': 'new_m', 'new_m_lam': 'new_m', 'new_m_w_out_rnn': 'new_m', 'new_m_w_out_attn': 'new_m', 'new_m_w_o': 'new_m', 'new_m_g_final': 'new_m', 'new_v_g_norm': 'new_v', 'new_v_w_mod': 'new_v', 'new_v_b_mod': 'new_v', 'new_v_w_in': 'new_v', 'new_v_b_gate': 'new_v', 'new_v_conv_w': 'new_v', 'new_v_conv_b': 'new_v', 'new_v_w_a': 'new_v', 'new_v_b_a': 'new_v', 'new_v_w_x': 'new_v', 'new_v_b_x': 'new_v', 'new_v_lam': 'new_v', 'new_v_w_out_rnn': 'new_v', 'new_v_w_out_attn': 'new_v', 'new_v_w_o': 'new_v', 'new_v_g_final': 'new_v'}


def _forward(args):
    return _fwd_reference(*[args[k] for k in FWD_PARAMS])


def _output_shape():
    out = _jax.eval_shape(lambda: _forward(_fwd_setup_inputs(0)))
    return out.shape, out.dtype

N_MICROBATCH = 1
ADAM_LR = 0.001
ADAM_B1 = 0.9
ADAM_B2 = 0.999
ADAM_EPS = 1e-08
ADAM_WD = 0.01
ADAM_STEP = 10
PER_EXAMPLE_BATCH_AXIS = {'x': 0, 'c': 0, 'positions': 0, 'loss_target': 0}
SHARED_INPUTS = []
_WEIGHT_DTYPES = {'g_norm': _jnp.float32, 'w_mod': _jnp.float32, 'b_mod': _jnp.float32, 'w_in': _jnp.float32, 'b_gate': _jnp.float32, 'conv_w': _jnp.float32, 'conv_b': _jnp.float32, 'w_a': _jnp.float32, 'b_a': _jnp.float32, 'w_x': _jnp.float32, 'b_x': _jnp.float32, 'lam': _jnp.float32, 'w_out_rnn': _jnp.float32, 'w_out_attn': _jnp.float32, 'w_o': _jnp.float32, 'g_final': _jnp.float32}
MOMENT_SCALE = {'g_norm': 9.917819e-02, 'w_mod': 1.378470e-01, 'b_mod': 2.234792e-01, 'w_in': 4.107958e-02, 'b_gate': 1.872678e-02, 'conv_w': 1.011874e-01, 'conv_b': 3.351325e-01, 'w_a': 8.394356e-03, 'b_a': 1.363135e-02, 'w_x': 1.719596e-02, 'b_x': 3.748110e-02, 'lam': 3.544274e-02, 'w_out_rnn': 7.792758e-02, 'w_out_attn': 1.137391e-02, 'w_o': 7.493246e-02, 'g_final': 6.408178e+01}


def _to_microbatches(a, axis):
    t = _jnp.moveaxis(a, axis, 0)
    t = t.reshape((N_MICROBATCH, t.shape[0] // N_MICROBATCH) + t.shape[1:])
    return _jnp.moveaxis(t, 1, axis + 1)


def setup_inputs(seed: int = 0) -> dict:
    inp = _fwd_setup_inputs(seed)
    key = _jax.random.fold_in(_jax.random.key(seed), 7919)
    shape, _ = _output_shape()
    out = dict(inp)
    out["loss_target"] = _jax.random.normal(_jax.random.fold_in(key, 0), shape, _jnp.float32)
    for i, name in enumerate(TWIN_WEIGHTS):
        w = inp[name].astype(_jnp.float32)
        if MOMENT_SCALE is None:
            s = _jnp.sqrt(_jnp.mean(_jnp.square(w)) + 1e-30)
        else:
            s = MOMENT_SCALE[name]
        km, kv = _jax.random.split(_jax.random.fold_in(key, i + 1))
        out[name] = w
        out["m_" + name] = s * _jax.random.normal(km, w.shape, _jnp.float32)
        out["v_" + name] = (s * s) * _jax.random.uniform(kv, w.shape, _jnp.float32, 0.5, 1.5)
    if N_MICROBATCH > 1:
        for name, axis in PER_EXAMPLE_BATCH_AXIS.items():
            out[name] = _to_microbatches(out[name], axis)
    return {'x': out['x'], 'c': out['c'], 'positions': out['positions'], 'g_norm': out['g_norm'], 'w_mod': out['w_mod'], 'b_mod': out['b_mod'], 'w_in': out['w_in'], 'b_gate': out['b_gate'], 'conv_w': out['conv_w'], 'conv_b': out['conv_b'], 'w_a': out['w_a'], 'b_a': out['b_a'], 'w_x': out['w_x'], 'b_x': out['b_x'], 'lam': out['lam'], 'w_out_rnn': out['w_out_rnn'], 'w_out_attn': out['w_out_attn'], 'w_o': out['w_o'], 'g_final': out['g_final'], 'loss_target': out['loss_target'], 'm_g_norm': out['m_g_norm'], 'm_w_mod': out['m_w_mod'], 'm_b_mod': out['m_b_mod'], 'm_w_in': out['m_w_in'], 'm_b_gate': out['m_b_gate'], 'm_conv_w': out['m_conv_w'], 'm_conv_b': out['m_conv_b'], 'm_w_a': out['m_w_a'], 'm_b_a': out['m_b_a'], 'm_w_x': out['m_w_x'], 'm_b_x': out['m_b_x'], 'm_lam': out['m_lam'], 'm_w_out_rnn': out['m_w_out_rnn'], 'm_w_out_attn': out['m_w_out_attn'], 'm_w_o': out['m_w_o'], 'm_g_final': out['m_g_final'], 'v_g_norm': out['v_g_norm'], 'v_w_mod': out['v_w_mod'], 'v_b_mod': out['v_b_mod'], 'v_w_in': out['v_w_in'], 'v_b_gate': out['v_b_gate'], 'v_conv_w': out['v_conv_w'], 'v_conv_b': out['v_conv_b'], 'v_w_a': out['v_w_a'], 'v_b_a': out['v_b_a'], 'v_w_x': out['v_w_x'], 'v_b_x': out['v_b_x'], 'v_lam': out['v_lam'], 'v_w_out_rnn': out['v_w_out_rnn'], 'v_w_out_attn': out['v_w_out_attn'], 'v_w_o': out['v_w_o'], 'v_g_final': out['v_g_final']}


def _loss(weights, diff, rest, loss_target):
    with _jax.named_scope("forward"):
        args = {**rest, TWIN_DIFF_INPUT: diff, **{k: w.astype(_WEIGHT_DTYPES[k]) for k, w in weights.items()}}
        y = _forward(args)
    with _jax.named_scope("loss_head"):
        err = _jnp.square(y.astype(_jnp.float32) - loss_target)
        return 0.5 * _jnp.sum(_jnp.mean(err, axis=-1)) if err.ndim else 0.5 * err


def _adamw(w, g, m, v):
    m = ADAM_B1 * m + (1.0 - ADAM_B1) * g
    v = ADAM_B2 * v + (1.0 - ADAM_B2) * _jnp.square(g)
    m_hat = m / (1.0 - ADAM_B1 ** ADAM_STEP)
    v_hat = v / (1.0 - ADAM_B2 ** ADAM_STEP)
    delta = -ADAM_LR * (m_hat / (_jnp.sqrt(v_hat) + ADAM_EPS) + ADAM_WD * w)
    return delta, m, v


def reference(x, c, positions, g_norm, w_mod, b_mod, w_in, b_gate, conv_w, conv_b, w_a, b_a, w_x, b_x, lam, w_out_rnn, w_out_attn, w_o, g_final, loss_target, m_g_norm, m_w_mod, m_b_mod, m_w_in, m_b_gate, m_conv_w, m_conv_b, m_w_a, m_b_a, m_w_x, m_b_x, m_lam, m_w_out_rnn, m_w_out_attn, m_w_o, m_g_final, v_g_norm, v_w_mod, v_b_mod, v_w_in, v_b_gate, v_conv_w, v_conv_b, v_w_a, v_b_a, v_w_x, v_b_x, v_lam, v_w_out_rnn, v_w_out_attn, v_w_o, v_g_final):
    given = dict(x=x, c=c, positions=positions, g_norm=g_norm, w_mod=w_mod, b_mod=b_mod, w_in=w_in, b_gate=b_gate, conv_w=conv_w, conv_b=conv_b, w_a=w_a, b_a=b_a, w_x=w_x, b_x=b_x, lam=lam, w_out_rnn=w_out_rnn, w_out_attn=w_out_attn, w_o=w_o, g_final=g_final, loss_target=loss_target, m_g_norm=m_g_norm, m_w_mod=m_w_mod, m_b_mod=m_b_mod, m_w_in=m_w_in, m_b_gate=m_b_gate, m_conv_w=m_conv_w, m_conv_b=m_conv_b, m_w_a=m_w_a, m_b_a=m_b_a, m_w_x=m_w_x, m_b_x=m_b_x, m_lam=m_lam, m_w_out_rnn=m_w_out_rnn, m_w_out_attn=m_w_out_attn, m_w_o=m_w_o, m_g_final=m_g_final, v_g_norm=v_g_norm, v_w_mod=v_w_mod, v_b_mod=v_b_mod, v_w_in=v_w_in, v_b_gate=v_b_gate, v_conv_w=v_conv_w, v_conv_b=v_conv_b, v_w_a=v_w_a, v_b_a=v_b_a, v_w_x=v_w_x, v_b_x=v_b_x, v_lam=v_lam, v_w_out_rnn=v_w_out_rnn, v_w_out_attn=v_w_out_attn, v_w_o=v_w_o, v_g_final=v_g_final)
    weights = {n: given[n] for n in TWIN_WEIGHTS}
    shared = {n: given[n] for n in SHARED_INPUTS}
    per_example = {n: given[n] for n in ['x', 'c', 'positions']}
    grad_fn = _jax.value_and_grad(_loss, argnums=(0, 1))

    def one_microbatch(ex, loss_target):
        ex = dict(ex)
        diff = ex.pop(TWIN_DIFF_INPUT)
        return grad_fn(weights, diff, {**shared, **ex}, loss_target)

    if N_MICROBATCH == 1:
        loss, (grad_w, grad_x) = one_microbatch(per_example, given["loss_target"])
    else:
        def body(carry, xs):
            loss_sum, grad_sum = carry
            l_k, (gw_k, gx_k) = one_microbatch(xs[0], xs[1])
            with _jax.named_scope("update"):
                return (loss_sum + l_k, _jax.tree.map(_jnp.add, grad_sum, gw_k)), gx_k

        init = (_jnp.zeros((), _jnp.float32), _jax.tree.map(_jnp.zeros_like, weights))
        (loss, grad_w), grad_x = _jax.lax.scan(body, init, (per_example, given["loss_target"]))
    with _jax.named_scope("update"):
        delta_w, new_m, new_v = {}, {}, {}
        for n in TWIN_WEIGHTS:
            delta_w[n], new_m[n], new_v[n] = _adamw(weights[n], grad_w[n], given["m_" + n], given["v_" + n])
    return (loss, grad_x, *[grad_w[n] for n in TWIN_WEIGHTS], *[delta_w[n] for n in TWIN_WEIGHTS],
            *[new_m[n] for n in TWIN_WEIGHTS], *[new_v[n] for n in TWIN_WEIGHTS])
```

```python
import jax
import jax.numpy as jnp
from jax import lax
from jax.experimental import pallas as pl
from jax.experimental.pallas import tpu as pltpu

F32, BF16 = jnp.float32, jnp.bfloat16
MESH = pl.DeviceIdType.MESH
HIGHEST = lax.Precision.HIGHEST

D = 1024
H = 8
DH = 128
PW = 2048
EPS = 1e-6
LRU_C = 8.0
SCALE = DH ** -0.5
NEG = -1e30
SPAN = 2048
UB = 128
DILATIONS = (1, 4, 16)
ROPE_THETA = 500000.0
ROT = 32

LR, B1, B2, ADAM_EPS, WD, STEP = 0.001, 0.9, 0.999, 1e-08, 0.01, 10

WROWS = 1024 + 384
NDEV = 8
AR_ROWS = 320
AR_CHUNK = AR_ROWS // NDEV


def _params(sem=None, vmem_mb=56):
    return pltpu.CompilerParams(dimension_semantics=sem, vmem_limit_bytes=vmem_mb * 2 ** 20)


def _coords():
    return lax.axis_index("x"), lax.axis_index("y"), lax.axis_index("c")


def _flip(v, bit):
    return 1 - v if bit else v


def _peer(k):
    x, y, c = _coords()
    return (_flip(x, (k >> 2) & 1), _flip(y, (k >> 1) & 1), _flip(c, k & 1))


def _my_index():
    x, y, c = _coords()
    return 4 * x + 2 * y + c


def _rcopy(src, dst, ssem, rsem, dev):
    return pltpu.make_async_remote_copy(src_ref=src, dst_ref=dst, send_sem=ssem, recv_sem=rsem,
                                        device_id=dev, device_id_type=MESH)


def _sigmoid(x):
    return jax.nn.sigmoid(x)


def _dot(a, b):
    return jnp.dot(a, b, preferred_element_type=F32)


def _dot_nt(a, b):
    return lax.dot_general(a, b, (((1,), (1,)), ((), ())), preferred_element_type=F32)


def _dot_tn(a, b):
    return lax.dot_general(a, b, (((0,), (0,)), ((), ())), preferred_element_type=F32)


def _colsum(a):
    return jnp.sum(a, axis=0, keepdims=True)


def _mod_fwd(c, w_mod_sh, b_mod4):
    def body(c_ref, w_ref, b_ref, call_ref, mod_ref, rows_ref, cmat_ref, s1, r1, s2, r2):
        x, y, _ = _coords()
        me = _my_index()
        j = 2 * x + y
        call_ref[me] = c_ref[...]
        sends = []
        for k in range(1, NDEV):
            cp = _rcopy(call_ref.at[me], call_ref.at[me], s1.at[k - 1], r1.at[k - 1], _peer(k))
            cp.start()
            sends.append(cp)
        for k in range(1, NDEV):
            pk = me ^ k
            _rcopy(call_ref.at[pk], call_ref.at[pk], s1.at[k - 1], r1.at[k - 1], _peer(k)).wait_recv()
        for b in range(NDEV):
            cmat_ref[pl.ds(b, 1), :] = call_ref[b]
        cm = cmat_ref[...]
        act = cm * _sigmoid(cm)
        mp = jnp.dot(act, w_ref[...], preferred_element_type=F32, precision=HIGHEST) + b_ref[j]
        for b in range(NDEV):
            rows_ref[b] = mp[b:b + 1]
        mod_ref[j] = rows_ref[me]
        for q, k in enumerate((2, 4, 6)):
            cp = _rcopy(rows_ref.at[me ^ k], mod_ref.at[j], s2.at[q], r2.at[q], _peer(k))
            cp.start()
            sends.append(cp)
        for q, k in enumerate((2, 4, 6)):
            jq = j ^ (k >> 1)
            _rcopy(rows_ref.at[me], mod_ref.at[jq], s2.at[q], r2.at[q], _peer(k)).wait_recv()
        for cp in sends:
            cp.wait_send()

    vm = pl.BlockSpec(memory_space=pltpu.VMEM)
    return pl.pallas_call(
        body, name="mod_fwd",
        out_shape=(jax.ShapeDtypeStruct((NDEV, 1, D), F32), jax.ShapeDtypeStruct((4, 1, 768), F32)),
        in_specs=[vm, vm, vm], out_specs=(vm, vm),
        scratch_shapes=[pltpu.VMEM((NDEV, 1, 768), F32), pltpu.VMEM((NDEV, D), F32),
                        pltpu.SemaphoreType.DMA((7,)), pltpu.SemaphoreType.DMA((7,)),
                        pltpu.SemaphoreType.DMA((3,)), pltpu.SemaphoreType.DMA((3,))],
        compiler_params=_params(),
    )(c, w_mod_sh, b_mod4)


def _mod_bwd(dmod4, c_all):
    def body(d_ref, call_ref, gw_ref, dall_ref, cmat_ref, dmat_ref, s1, r1):
        x, y, _ = _coords()
        me = _my_index()
        j = 2 * x + y
        dall_ref[me] = d_ref[...]
        sends = []
        for k in range(1, NDEV):
            cp = _rcopy(dall_ref.at[me], dall_ref.at[me], s1.at[k - 1], r1.at[k - 1], _peer(k))
            cp.start()
            sends.append(cp)
        for k in range(1, NDEV):
            pk = me ^ k
            _rcopy(dall_ref.at[pk], dall_ref.at[pk], s1.at[k - 1], r1.at[k - 1], _peer(k)).wait_recv()
        for cp in sends:
            cp.wait_send()
        for b in range(NDEV):
            cmat_ref[pl.ds(b, 1), :] = call_ref[b]
            dmat_ref[pl.ds(b, 1), :] = dall_ref[b, j]
        cm = cmat_ref[...]
        act = cm * _sigmoid(cm)
        gw_ref[...] = lax.dot_general(act, dmat_ref[...], (((0,), (0,)), ((), ())),
                                      preferred_element_type=F32, precision=HIGHEST)

    vm = pl.BlockSpec(memory_space=pltpu.VMEM)
    return pl.pallas_call(
        body, name="mod_bwd",
        out_shape=jax.ShapeDtypeStruct((D, 768), F32),
        in_specs=[vm, vm], out_specs=vm,
        scratch_shapes=[pltpu.VMEM((NDEV, 4, 1, 768), F32), pltpu.VMEM((NDEV, D), F32), pltpu.VMEM((NDEV, 768), F32),
                        pltpu.SemaphoreType.DMA((7,)), pltpu.SemaphoreType.DMA((7,))],
        compiler_params=_params(),
    )(dmod4, c_all)


def _gather_weights(wsh):
    rows, cols = wsh.shape
    half = rows // 2

    def body(w_ref, g_ref, lsem, ss, rs):
        x, y, c = _coords()
        j = 2 * x + y
        sib = (x, y, 1 - c)
        mine = pl.ds(c * half, half)
        theirs = pl.ds((1 - c) * half, half)
        local = pltpu.make_async_copy(w_ref, g_ref.at[j], lsem)
        local.start()
        sends = []
        chips = ((1, 0), (0, 1), (1, 1))
        for q, (kx, ky) in enumerate(chips):
            cp = _rcopy(w_ref.at[mine], g_ref.at[j, mine], ss.at[q], rs.at[q], (_flip(x, kx), _flip(y, ky), c))
            cp.start()
            sends.append(cp)
        for q, (kx, ky) in enumerate(chips):
            jq = j ^ (2 * kx + ky)
            _rcopy(w_ref.at[mine], g_ref.at[jq, mine], ss.at[q], rs.at[q], sib).wait_recv()
            cp = _rcopy(g_ref.at[jq, mine], g_ref.at[jq, mine], ss.at[3 + q], rs.at[3 + q], sib)
            cp.start()
            sends.append(cp)
        for q, (kx, ky) in enumerate(chips):
            jq = j ^ (2 * kx + ky)
            _rcopy(w_ref.at[mine], g_ref.at[jq, theirs], ss.at[3 + q], rs.at[3 + q], sib).wait_recv()
        for cp in sends:
            cp.wait_send()
        local.wait()

    any_ = pl.BlockSpec(memory_space=pl.ANY)
    return pl.pallas_call(
        body, name="gather_weights",
        out_shape=jax.ShapeDtypeStruct((4, rows, cols), wsh.dtype),
        in_specs=[any_], out_specs=any_,
        scratch_shapes=[pltpu.SemaphoreType.DMA, pltpu.SemaphoreType.DMA((6,)), pltpu.SemaphoreType.DMA((6,))],
        compiler_params=_params(),
    )(wsh)


def _gather_small(v):
    r, cdim = v.shape

    def body(v_ref, out_ref, ss, rs):
        me = _my_index()
        out_ref[me] = v_ref[...]
        sends = []
        for k in range(1, NDEV):
            cp = _rcopy(out_ref.at[me], out_ref.at[me], ss.at[k - 1], rs.at[k - 1], _peer(k))
            cp.start()
            sends.append(cp)
        for k in range(1, NDEV):
            pk = me ^ k
            _rcopy(out_ref.at[pk], out_ref.at[pk], ss.at[k - 1], rs.at[k - 1], _peer(k)).wait_recv()
        for cp in sends:
            cp.wait_send()

    vm = pl.BlockSpec(memory_space=pltpu.VMEM)
    return pl.pallas_call(
        body, name="gather_small",
        out_shape=jax.ShapeDtypeStruct((NDEV, r, cdim), v.dtype),
        in_specs=[vm], out_specs=vm,
        scratch_shapes=[pltpu.SemaphoreType.DMA((7,)), pltpu.SemaphoreType.DMA((7,))],
        compiler_params=_params(),
    )(v)


def _norm_inproj(x, gn, shift, scale, wg, tm=1024, tn=512):
    s = x.shape[0]
    npc = PW // tn

    def body(x_ref, gn_ref, sh_ref, sc_ref, w_ref, p_ref, h_ref, hs):
        @pl.when(pl.program_id(1) == 0)
        def _():
            xt = x_ref[...]
            rstd = lax.rsqrt(jnp.mean(xt * xt, axis=-1, keepdims=True) + EPS)
            h = (xt * rstd * gn_ref[...]) * (1.0 + sc_ref[...]) + sh_ref[...]
            hs[...] = h.astype(BF16)
            h_ref[...] = hs[...]

        p_ref[0] = _dot(hs[...], w_ref[0])

    row = pl.BlockSpec((1, D), lambda i, j: (0, 0))
    return pl.pallas_call(
        body, name="norm_inproj", grid=(s // tm, 4 * npc),
        in_specs=[pl.BlockSpec((tm, D), lambda i, j: (i, 0)), row, row, row,
                  pl.BlockSpec((1, D, tn), lambda i, j: (j // npc, 0, j % npc))],
        out_specs=(pl.BlockSpec((1, tm, tn), lambda i, j: (j // npc, i, j % npc)),
                   pl.BlockSpec((tm, D), lambda i, j: (i, 0))),
        out_shape=(jax.ShapeDtypeStruct((4, s, PW), F32), jax.ShapeDtypeStruct((s, D), BF16)),
        scratch_shapes=[pltpu.VMEM((tm, D), BF16)],
        compiler_params=_params(("parallel", "arbitrary")),
    )(x, gn, shift, scale, wg)


def _shift_down(prev8, cur, d):
    t = cur.shape[0]
    ext = jnp.concatenate([prev8, cur], axis=0)
    return pltpu.roll(ext, d, 0)[8:]


def _shift_up(cur, next8, d):
    t = cur.shape[0]
    ext = jnp.concatenate([cur, next8], axis=0)
    return pltpu.roll(ext, t + 8 - d, 0)[:t]


def _rnn_gates(xr, prev8, cw, cb, wa, ba, wx, bx, lam, reset):
    xc = cw[3:4] * xr + cb
    for d in (1, 2, 3):
        xc = xc + cw[3 - d:4 - d] * _shift_down(prev8, xr, d)
    xcb = xc.astype(BF16)
    r = _sigmoid(_dot(xcb, wa.astype(BF16)) + ba)
    ig = _sigmoid(_dot(xcb, wx.astype(BF16)) + bx)
    nl = -lam
    sp = jnp.maximum(nl, 0.0) + jnp.log1p(jnp.exp(-jnp.abs(nl)))
    log_a = (-LRU_C * r) * sp
    a = jnp.where(reset, 0.0, jnp.exp(log_a))
    mult = jnp.where(reset, 1.0, jnp.sqrt(1.0 - jnp.exp(2.0 * log_a)))
    return xc, r, ig, sp, a, mult


def _scan_down(a, b, t):
    rows = lax.broadcasted_iota(jnp.int32, a.shape, 0)
    d = 1
    while d < t:
        m = rows >= d
        a_s = pltpu.roll(a, d, 0)
        b_s = pltpu.roll(b, d, 0)
        b = jnp.where(m, a * b_s + b, b)
        a = jnp.where(m, a * a_s, a)
        d *= 2
    return a, b


def _scan_up(a, b, t):
    rows = lax.broadcasted_iota(jnp.int32, a.shape, 0)
    d = 1
    while d < t:
        m = rows < t - d
        a_s = pltpu.roll(a, t - d, 0)
        b_s = pltpu.roll(b, t - d, 0)
        b = jnp.where(m, a * b_s + b, b)
        a = jnp.where(m, a * a_s, a)
        d *= 2
    return a, b


def _rnn_fwd(p, pos, conv_w, conv_b, w_a, b_a, w_x, b_x, lam, tt=256):
    s = p.shape[1]
    nt = s // tt

    def body(xr_ref, z_ref, pos_ref, cw_ref, cb_ref, wa_ref, ba_ref, wx_ref, bx_ref, lam_ref,
             hr_ref, gr_ref, xprev, hprev):
        @pl.when(pl.program_id(1) == 0)
        def _():
            xprev[...] = jnp.zeros_like(xprev)
            hprev[...] = jnp.zeros_like(hprev)

        xr = xr_ref[0]
        z = z_ref[0]
        reset = pos_ref[...] == 0
        xc, r, ig, sp, a, mult = _rnn_gates(xr, xprev[...], cw_ref[...], cb_ref[...], wa_ref[0], ba_ref[0],
                                            wx_ref[0], bx_ref[0], lam_ref[...], reset)
        bx = mult * ig * xc
        a_cum, h0 = _scan_down(a, bx, tt)
        h = a_cum * hprev[0:1] + h0
        xprev[...] = xr[tt - 8:]
        hprev[...] = jnp.broadcast_to(h[tt - 1:tt], (8, DH))
        hr_ref[...] = h
        gr_ref[...] = (h * (z * _sigmoid(z))).astype(BF16)

    head_row = lambda hh, t: (0, hh)
    return pl.pallas_call(
        body, name="rnn_fwd", grid=(H, nt),
        in_specs=[pl.BlockSpec((1, tt, DH), lambda hh, t: (0, t, hh)),
                  pl.BlockSpec((1, tt, DH), lambda hh, t: (0, t, H + hh)),
                  pl.BlockSpec((tt, 1), lambda hh, t: (t, 0)),
                  pl.BlockSpec((4, DH), head_row), pl.BlockSpec((1, DH), head_row),
                  pl.BlockSpec((1, DH, DH), lambda hh, t: (hh, 0, 0)), pl.BlockSpec((1, 1, DH), lambda hh, t: (hh, 0, 0)),
                  pl.BlockSpec((1, DH, DH), lambda hh, t: (hh, 0, 0)), pl.BlockSpec((1, 1, DH), lambda hh, t: (hh, 0, 0)),
                  pl.BlockSpec((1, DH), head_row)],
        out_specs=(pl.BlockSpec((tt, DH), lambda hh, t: (t, hh)), pl.BlockSpec((tt, DH), lambda hh, t: (t, hh))),
        out_shape=(jax.ShapeDtypeStruct((s, D), F32), jax.ShapeDtypeStruct((s, D), BF16)),
        scratch_shapes=[pltpu.VMEM((8, DH), F32), pltpu.VMEM((8, DH), F32)],
        compiler_params=_params(("parallel", "arbitrary")),
    )(p, p, pos, conv_w, conv_b, w_a, b_a, w_x, b_x, lam)


def _rnn_bwd(p, hr, dgr, pos, conv_w, conv_b, w_a, b_a, w_x, b_x, lam, tt=256):
    s = p.shape[1]
    nt = s // tt
    t8 = tt // 8

    def body(xr_ref, z_ref, xp_ref, hr_ref, hp_ref, dg_ref, pos_ref, cw_ref, cb_ref, wa_ref, ba_ref, wx_ref, bx_ref,
             lam_ref, dxr_ref, dz_ref, gwa_ref, gba_ref, gwx_ref, gbx_ref, glam_ref, gcw_ref, gcb_ref,
             a_next, g_next, dxc_next):
        t = pl.program_id(1)
        has_prev = t < nt - 1

        @pl.when(t == 0)
        def _():
            a_next[...] = jnp.zeros_like(a_next)
            g_next[...] = jnp.zeros_like(g_next)
            dxc_next[...] = jnp.zeros_like(dxc_next)
            gwa_ref[...] = jnp.zeros_like(gwa_ref)
            gba_ref[...] = jnp.zeros_like(gba_ref)
            gwx_ref[...] = jnp.zeros_like(gwx_ref)
            gbx_ref[...] = jnp.zeros_like(gbx_ref)
            glam_ref[...] = jnp.zeros_like(glam_ref)
            gcw_ref[...] = jnp.zeros_like(gcw_ref)
            gcb_ref[...] = jnp.zeros_like(gcb_ref)

        xr = xr_ref[0]
        z = z_ref[0]
        hr_blk = hr_ref[...]
        dg = dg_ref[...]
        xprev = jnp.where(has_prev, xp_ref[0], 0.0)
        hprev8 = jnp.where(has_prev, hp_ref[...], 0.0)
        reset = pos_ref[...] == 0
        cw = cw_ref[...]
        wa = wa_ref[0]
        wx = wx_ref[0]
        lam_v = lam_ref[...]
        xc, r, ig, sp, a, mult = _rnn_gates(xr, xprev, cw, cb_ref[...], wa, ba_ref[0], wx, bx_ref[0], lam_v, reset)

        sz = _sigmoid(z)
        dh = dg * (z * sz)
        dz_ref[...] = (dg * hr_blk * (sz * (1.0 + z * (1.0 - sz)))).astype(BF16)

        an = _shift_up(a, a_next[...], 1)
        a_cum, g0 = _scan_up(an, dh, tt)
        g = g0 + a_cum * g_next[0:1]
        a_next[...] = jnp.broadcast_to(a[0:1], (8, DH))
        g_next[...] = jnp.broadcast_to(g[0:1], (8, DH))

        hm1 = _shift_down(hprev8, hr_blk, 1)
        da = g * hm1
        dmult = g * (ig * xc)
        di = g * (mult * xc)
        dxc = g * (mult * ig)
        dla = jnp.where(reset, 0.0, da * a - dmult * (a * a) / mult)
        dr = dla * (-LRU_C * sp)
        dsp = _colsum(dla * (-LRU_C * r))
        glam_ref[0] += dsp * (-_sigmoid(-lam_v))
        dpa = dr * r * (1.0 - r)
        dpx = di * ig * (1.0 - ig)
        dpab = dpa.astype(BF16)
        dpxb = dpx.astype(BF16)
        dxc = dxc + _dot_nt(dpab, wa.astype(BF16)) + _dot_nt(dpxb, wx.astype(BF16))
        xcb = xc.astype(BF16)
        gwa_ref[0] += _dot_tn(xcb, dpab)
        gwx_ref[0] += _dot_tn(xcb, dpxb)
        gba_ref[0] += _colsum(dpa)
        gbx_ref[0] += _colsum(dpx)

        dxr = cw[3:4] * dxc
        for d in (1, 2, 3):
            dxr = dxr + cw[3 - d:4 - d] * _shift_up(dxc, dxc_next[...], d)
        dxr_ref[...] = dxr.astype(BF16)
        dxc_next[...] = dxc[0:8]
        gcb_ref[0] += _colsum(dxc)
        gcw_ref[0, 3:4, :] += _colsum(xr * dxc)
        for d in (1, 2, 3):
            gcw_ref[0, 3 - d:4 - d, :] += _colsum(_shift_down(xprev, xr, d) * dxc)

    rt = lambda t: nt - 1 - t
    prev8 = lambda t: jnp.maximum(rt(t) * t8 - 1, 0)
    head_row = lambda hh, t: (0, hh)
    hsm = lambda hh, t: (hh, 0, 0)
    return pl.pallas_call(
        body, name="rnn_bwd", grid=(H, nt),
        in_specs=[pl.BlockSpec((1, tt, DH), lambda hh, t: (0, rt(t), hh)),
                  pl.BlockSpec((1, tt, DH), lambda hh, t: (0, rt(t), H + hh)),
                  pl.BlockSpec((1, 8, DH), lambda hh, t: (0, prev8(t), hh)),
                  pl.BlockSpec((tt, DH), lambda hh, t: (rt(t), hh)),
                  pl.BlockSpec((8, DH), lambda hh, t: (prev8(t), hh)),
                  pl.BlockSpec((tt, DH), lambda hh, t: (rt(t), hh)),
                  pl.BlockSpec((tt, 1), lambda hh, t: (rt(t), 0)),
                  pl.BlockSpec((4, DH), head_row), pl.BlockSpec((1, DH), head_row),
                  pl.BlockSpec((1, DH, DH), hsm), pl.BlockSpec((1, 1, DH), hsm),
                  pl.BlockSpec((1, DH, DH), hsm), pl.BlockSpec((1, 1, DH), hsm),
                  pl.BlockSpec((1, DH), head_row)],
        out_specs=(pl.BlockSpec((tt, DH), lambda hh, t: (rt(t), hh)), pl.BlockSpec((tt, DH), lambda hh, t: (rt(t), hh)),
                   pl.BlockSpec((1, DH, DH), hsm), pl.BlockSpec((1, 1, DH), hsm),
                   pl.BlockSpec((1, DH, DH), hsm), pl.BlockSpec((1, 1, DH), hsm),
                   pl.BlockSpec((1, 1, DH), hsm), pl.BlockSpec((1, 4, DH), hsm), pl.BlockSpec((1, 1, DH), hsm)),
        out_shape=(jax.ShapeDtypeStruct((s, D), BF16), jax.ShapeDtypeStruct((s, D), BF16),
                   jax.ShapeDtypeStruct((H, DH, DH), F32), jax.ShapeDtypeStruct((H, 1, DH), F32),
                   jax.ShapeDtypeStruct((H, DH, DH), F32), jax.ShapeDtypeStruct((H, 1, DH), F32),
                   jax.ShapeDtypeStruct((H, 1, DH), F32), jax.ShapeDtypeStruct((H, 4, DH), F32),
                   jax.ShapeDtypeStruct((H, 1, DH), F32)),
        scratch_shapes=[pltpu.VMEM((8, DH), F32), pltpu.VMEM((8, DH), F32), pltpu.VMEM((8, DH), F32)],
        compiler_params=_params(("parallel", "arbitrary")),
    )(p, p, p, hr, hr, dgr, pos, conv_w, conv_b, w_a, b_a, w_x, b_x, lam)


def _rope(t, c, sa, sb):
    return t * c + pltpu.roll(t, DH - ROT // 2, 1) * sa + pltpu.roll(t, ROT // 2, 1) * sb


def _rope_bwd(g, c, sa, sb):
    return g * c + pltpu.roll(g * sa, ROT // 2, 1) + pltpu.roll(g * sb, DH - ROT // 2, 1)


def _unit_bases(gi, u):
    dil = DILATIONS[gi]
    if dil == 1:
        return u * UB, SPAN + (u - 1) * UB, u == 0
    if dil == 4:
        blk, r = u // 4, u % 4
        return blk * 4 * UB + r, SPAN + (blk - 1) * 4 * UB + r, blk == 0
    return u, u, True


def _band_mask(first_in_span, has_prev):
    qi = lax.broadcasted_iota(jnp.int32, (UB, 2 * UB), 0)
    ki = lax.broadcasted_iota(jnp.int32, (UB, 2 * UB), 1)
    dist = UB + qi - ki
    band = (dist >= 0) & (dist <= UB)
    return band & ((ki >= UB) | jnp.logical_not(first_in_span) | has_prev)


def _attn_fwd(p, rc, rsa, rsb):
    s = p.shape[1]
    ns = s // SPAN
    nunit = SPAN // UB

    def body(q_ref, k_ref, v_ref, z_ref, c_ref, sa_ref, sb_ref, o_ref, lse_ref, ga_ref,
             qr, kf, vf, acc, mm, ll):
        n = pl.program_id(1)

        @pl.when(n == 0)
        def _():
            kf[0:SPAN] = jnp.zeros((SPAN, DH), F32)
            vf[0:SPAN] = jnp.zeros((SPAN, DH), F32)

        c, sa, sb = c_ref[...], sa_ref[...], sb_ref[...]
        qr[...] = _rope(q_ref[0], c, sa, sb)
        kf[SPAN:] = _rope(k_ref[0], c, sa, sb)
        vf[SPAN:] = v_ref[0]
        has_prev = n > 0

        for gi, dil in enumerate(DILATIONS):
            def unit(u, carry, gi=gi, dil=dil):
                qb0, kb0, first = _unit_bases(gi, u)
                qsl = pl.ds(qb0, UB, stride=dil) if dil > 1 else pl.ds(pl.multiple_of(qb0, UB), UB)
                ksl = pl.ds(kb0, 2 * UB, stride=dil) if dil > 1 else pl.ds(pl.multiple_of(kb0, UB), 2 * UB)
                qb = qr[qsl, :].astype(BF16)
                kb = kf[ksl, :].astype(BF16)
                vb = vf[ksl, :].astype(BF16)
                sc = _dot_nt(qb, kb) * SCALE
                sc = jnp.where(_band_mask(first, has_prev), sc, NEG)
                m = jnp.max(sc, axis=-1, keepdims=True)
                pr = jnp.exp(sc - m)
                l = jnp.sum(pr, axis=-1, keepdims=True)
                acc[gi, qsl, :] = _dot(pr.astype(BF16), vb)
                mm[gi, qsl, :] = jnp.broadcast_to(m, (UB, DH))
                ll[gi, qsl, :] = jnp.broadcast_to(l, (UB, DH))
                return carry

            lax.fori_loop(0, nunit, unit, 0)

        m_all = jnp.maximum(jnp.maximum(mm[0], mm[1]), mm[2])
        num = jnp.zeros((SPAN, DH), F32)
        den = jnp.zeros((SPAN, DH), F32)
        for gi in range(3):
            w = jnp.exp(mm[gi] - m_all)
            num = num + w * acc[gi]
            den = den + w * ll[gi]
        o = num / den
        o_ref[...] = o
        lse_ref[...] = m_all + jnp.log(den)
        z = z_ref[0]
        ga_ref[...] = (o * (z * _sigmoid(z))).astype(BF16)
        kf[0:SPAN] = kf[SPAN:]
        vf[0:SPAN] = vf[SPAN:]

    blk = lambda piece, off: pl.BlockSpec((1, SPAN, DH), lambda hh, n: (piece, n, off + hh))
    tab = pl.BlockSpec((SPAN, DH), lambda hh, n: (n, 0))
    outb = pl.BlockSpec((SPAN, DH), lambda hh, n: (n, hh))
    return pl.pallas_call(
        body, name="attn_fwd", grid=(H, ns),
        in_specs=[blk(1, 0), blk(1, H), blk(2, 0), blk(2, H), tab, tab, tab],
        out_specs=(outb, outb, outb),
        out_shape=(jax.ShapeDtypeStruct((s, D), F32), jax.ShapeDtypeStruct((s, D), F32),
                   jax.ShapeDtypeStruct((s, D), BF16)),
        scratch_shapes=[pltpu.VMEM((SPAN, DH), F32), pltpu.VMEM((2 * SPAN, DH), F32), pltpu.VMEM((2 * SPAN, DH), F32),
                        pltpu.VMEM((3, SPAN, DH), F32), pltpu.VMEM((3, SPAN, DH), F32), pltpu.VMEM((3, SPAN, DH), F32)],
        compiler_params=_params(("parallel", "arbitrary")),
    )(p, p, p, p, rc, rsa, rsb)


def _attn_bwd(p, o, lse, dga, rc, rsa, rsb):
    s = p.shape[1]
    ns = s // SPAN
    nunit = SPAN // UB

    def body(q_ref, k_ref, kp_ref, v_ref, vp_ref, z_ref, c_ref, sa_ref, sb_ref, cp_ref, sap_ref, sbp_ref,
             o_ref, lse_ref, dg_ref, dq_ref, dk_ref, dv_ref, dz_ref,
             qr, kf, vf, dof, dlt, dqa, dkf, dvf):
        step = pl.program_id(1)
        n = ns - 1 - step
        has_prev = n > 0

        @pl.when(step == 0)
        def _():
            dkf[...] = jnp.zeros_like(dkf)
            dvf[...] = jnp.zeros_like(dvf)

        @pl.when(step > 0)
        def _():
            dkf[SPAN:] = dkf[0:SPAN]
            dvf[SPAN:] = dvf[0:SPAN]
            dkf[0:SPAN] = jnp.zeros((SPAN, DH), F32)
            dvf[0:SPAN] = jnp.zeros((SPAN, DH), F32)

        c, sa, sb = c_ref[...], sa_ref[...], sb_ref[...]
        qr[...] = _rope(q_ref[0], c, sa, sb)
        kf[SPAN:] = _rope(k_ref[0], c, sa, sb)
        vf[SPAN:] = v_ref[0]
        kf[0:SPAN] = jnp.where(has_prev, _rope(kp_ref[0], cp_ref[...], sap_ref[...], sbp_ref[...]), 0.0)
        vf[0:SPAN] = jnp.where(has_prev, vp_ref[0], 0.0)
        z = z_ref[0]
        sz = _sigmoid(z)
        dg = dg_ref[...]
        ov = o_ref[...]
        do = dg * (z * sz)
        dz_ref[...] = (dg * ov * (sz * (1.0 + z * (1.0 - sz)))).astype(BF16)
        dof[...] = do
        dlt[...] = jnp.broadcast_to(jnp.sum(do * ov, axis=-1, keepdims=True), (SPAN, DH))
        dqa[...] = jnp.zeros_like(dqa)

        for gi, dil in enumerate(DILATIONS):
            def unit(u, carry, gi=gi, dil=dil):
                qb0, kb0, first = _unit_bases(gi, u)
                qsl = pl.ds(qb0, UB, stride=dil) if dil > 1 else pl.ds(pl.multiple_of(qb0, UB), UB)
                ksl = pl.ds(kb0, 2 * UB, stride=dil) if dil > 1 else pl.ds(pl.multiple_of(kb0, UB), 2 * UB)
                qb = qr[qsl, :].astype(BF16)
                kb = kf[ksl, :].astype(BF16)
                vb = vf[ksl, :].astype(BF16)
                dob = dof[qsl, :].astype(BF16)
                lse_b = lse_ref[qsl, :]
                dl_b = dlt[qsl, :]
                sc = _dot_nt(qb, kb) * SCALE
                pr = jnp.exp(sc - jnp.concatenate([lse_b, lse_b], axis=1))
                pr = jnp.where(_band_mask(first, has_prev), pr, 0.0)
                prb = pr.astype(BF16)
                dvf[ksl, :] += _dot_tn(prb, dob)
                dp = _dot_nt(dob, vb)
                ds = (pr * (dp - jnp.concatenate([dl_b, dl_b], axis=1)) * SCALE).astype(BF16)
                dqa[qsl, :] += _dot(ds, kb)
                dkf[ksl, :] += _dot_tn(ds, qb)
                return carry

            lax.fori_loop(0, nunit, unit, 0)

        dq_ref[...] = _rope_bwd(dqa[...], c, sa, sb).astype(BF16)
        dk_ref[...] = _rope_bwd(dkf[SPAN:], c, sa, sb).astype(BF16)
        dv_ref[...] = dvf[SPAN:].astype(BF16)

    rn = lambda n: ns - 1 - n
    pn = lambda n: jnp.maximum(ns - 2 - n, 0)
    blk = lambda piece, off: pl.BlockSpec((1, SPAN, DH), lambda hh, n: (piece, rn(n), off + hh))
    blkp = lambda piece, off: pl.BlockSpec((1, SPAN, DH), lambda hh, n: (piece, pn(n), off + hh))
    tab = pl.BlockSpec((SPAN, DH), lambda hh, n: (rn(n), 0))
    tabp = pl.BlockSpec((SPAN, DH), lambda hh, n: (pn(n), 0))
    io = pl.BlockSpec((SPAN, DH), lambda hh, n: (rn(n), hh))
    return pl.pallas_call(
        body, name="attn_bwd", grid=(H, ns),
        in_specs=[blk(1, 0), blk(1, H), blkp(1, H), blk(2, 0), blkp(2, 0), blk(2, H),
                  tab, tab, tab, tabp, tabp, tabp, io, io, io],
        out_specs=(io, io, io, io),
        out_shape=tuple(jax.ShapeDtypeStruct((s, D), BF16) for _ in range(4)),
        scratch_shapes=[pltpu.VMEM((SPAN, DH), F32), pltpu.VMEM((2 * SPAN, DH), F32), pltpu.VMEM((2 * SPAN, DH), F32),
                        pltpu.VMEM((SPAN, DH), F32), pltpu.VMEM((SPAN, DH), F32), pltpu.VMEM((SPAN, DH), F32),
                        pltpu.VMEM((2 * SPAN, DH), F32), pltpu.VMEM((2 * SPAN, DH), F32)],
        compiler_params=_params(("parallel", "arbitrary")),
    )(p, p, p, p, p, p, rc, rsa, rsb, rc, rsa, rsb, o, lse, dga)


def _tail(gr, ga, p, x, tgt, w3, b_gate, gate, g_final, tm=256):
    s = x.shape[0]
    nt = s // tm

    def body(gr_ref, ga_ref, pr_ref, pa_ref, x_ref, t_ref, bg_ref, gate_ref, gf_ref, w_hbm,
             dgr_ref, dga_ref, dc_ref, dx2_ref, vec_ref, go_hbm, w_s, acc_s, sem):
        i = pl.program_id(0)

        @pl.when(i == 0)
        def _():
            cp = pltpu.make_async_copy(w_hbm, w_s, sem.at[12])
            cp.start()
            acc_s[...] = jnp.zeros_like(acc_s)
            vec_ref[...] = jnp.zeros_like(vec_ref)
            cp.wait()

        grb = gr_ref[...]
        gab = ga_ref[...]
        bg = bg_ref[...]
        gate_v = gate_ref[...]
        gf = gf_ref[...]
        y_r = _dot(grb, w_s[0])
        y_a = _dot(gab, w_s[1])
        sr = _sigmoid(pr_ref[0] + bg[:, :D])
        sa = _sigmoid(pa_ref[0] + bg[:, D:])
        mb = (sr * y_r + sa * y_a).astype(BF16)
        u = _dot(mb, w_s[2])
        x2 = x_ref[...] + gate_v * u
        rstd = lax.rsqrt(jnp.mean(x2 * x2, axis=-1, keepdims=True) + EPS)
        xh = x2 * rstd
        e = xh * gf - t_ref[...]
        dy = e * (1.0 / D)
        dyg = dy * gf
        dx2 = rstd * (dyg - xh * jnp.mean(dyg * xh, axis=-1, keepdims=True))
        dx2_ref[...] = dx2
        dub = (dx2 * gate_v).astype(BF16)
        dm = _dot_nt(dub, w_s[2])
        dyr = (dm * sr).astype(BF16)
        dya = (dm * sa).astype(BF16)
        dpr = dm * y_r * (sr * (1.0 - sr))
        dpa = dm * y_a * (sa * (1.0 - sa))
        dc_ref[:, :D] = dpr.astype(BF16)
        dc_ref[:, D:] = dpa.astype(BF16)
        dgr_ref[...] = _dot_nt(dyr, w_s[0])
        dga_ref[...] = _dot_nt(dya, w_s[1])
        acc_s[0] += _dot_tn(grb, dyr)
        acc_s[1] += _dot_tn(gab, dya)
        acc_s[2] += _dot_tn(mb, dub)
        vec_ref[0:1, :] += _colsum(dy * xh)
        vec_ref[1:2, :] += _colsum(dx2 * u)
        vec_ref[2:3, :] += _colsum(dpr)
        vec_ref[3:4, :] += _colsum(dpa)
        vec_ref[4:5, :] += _colsum(e * e)

        @pl.when(i == nt - 1)
        def _():
            vec_ref[4:5, :] = jnp.broadcast_to(jnp.sum(vec_ref[4:5, :]) * (0.5 / D), (1, D))
            cps = []
            for w in range(3):
                for j in range(4):
                    cps.append(pltpu.make_async_copy(acc_s.at[w, pl.ds(256 * j, 256)],
                                                     go_hbm.at[j, pl.ds(256 * w, 256)], sem.at[4 * w + j]))
            for cp in cps:
                cp.start()
            for cp in cps:
                cp.wait()

    rowt = lambda i: (i, 0)
    row = lambda w: pl.BlockSpec((1, w), lambda i: (0, 0))
    any_ = pl.BlockSpec(memory_space=pl.ANY)
    return pl.pallas_call(
        body, name="tail", grid=(nt,),
        in_specs=[pl.BlockSpec((tm, D), rowt), pl.BlockSpec((tm, D), rowt),
                  pl.BlockSpec((1, tm, D), lambda i: (3, i, 0)), pl.BlockSpec((1, tm, D), lambda i: (3, i, 1)),
                  pl.BlockSpec((tm, D), rowt), pl.BlockSpec((tm, D), rowt),
                  row(2 * D), row(D), row(D), any_],
        out_specs=(pl.BlockSpec((tm, D), rowt), pl.BlockSpec((tm, D), rowt), pl.BlockSpec((tm, 2 * D), rowt),
                   pl.BlockSpec((tm, D), rowt), pl.BlockSpec((8, D), lambda i: (0, 0)), any_),
        out_shape=(jax.ShapeDtypeStruct((s, D), F32), jax.ShapeDtypeStruct((s, D), F32),
                   jax.ShapeDtypeStruct((s, 2 * D), BF16), jax.ShapeDtypeStruct((s, D), F32),
                   jax.ShapeDtypeStruct((8, D), F32), jax.ShapeDtypeStruct((4, 768, D), F32)),
        scratch_shapes=[pltpu.VMEM((3, D, D), BF16), pltpu.VMEM((3, D, D), F32), pltpu.SemaphoreType.DMA((13,))],
        compiler_params=_params(("arbitrary",)),
    )(gr, ga, p, p, x, tgt, b_gate, gate, g_final, w3)


def _pieces_steps(pieces):
    out, s0 = [], 0
    for a in pieces:
        n = a.shape[1] // D
        out.append((s0, n))
        s0 += n
    return out, s0


def _inproj_bwd_x(pieces, wg, x, dx2, gn, scale, tm=512):
    s = x.shape[0]
    steps, nk = _pieces_steps(pieces)
    npc = PW // D
    np_ = len(pieces)

    def body(*refs):
        d_refs = refs[:np_]
        w_ref, x_ref, dx2_ref, gn_ref, sc_ref, gx_ref, vec_ref, acc = refs[np_:]
        i, k = pl.program_id(0), pl.program_id(1)

        @pl.when(k == 0)
        def _():
            acc[...] = jnp.zeros_like(acc)

        @pl.when((i == 0) & (k == 0))
        def _():
            vec_ref[...] = jnp.zeros_like(vec_ref)

        for (s0, n), d_ref in zip(steps, d_refs):
            @pl.when((k >= s0) & (k < s0 + n))
            def _(d_ref=d_ref):
                acc[...] += _dot_nt(d_ref[...], w_ref[0])

        @pl.when(k == nk - 1)
        def _():
            dh = acc[...]
            xt = x_ref[...]
            rstd = lax.rsqrt(jnp.mean(xt * xt, axis=-1, keepdims=True) + EPS)
            xh = xt * rstd
            gn_v = gn_ref[...]
            sc1 = 1.0 + sc_ref[...]
            dhx = dh * xh
            vec_ref[0:1, :] += _colsum(dh)
            vec_ref[1:2, :] += _colsum(dhx) * gn_v
            vec_ref[2:3, :] += _colsum(dhx) * sc1
            dxh = dh * (gn_v * sc1)
            gx_ref[...] = rstd * (dxh - xh * jnp.mean(dxh * xh, axis=-1, keepdims=True)) + dx2_ref[...]

    def piece_spec(s0, n):
        return pl.BlockSpec((tm, D), lambda i, k: (i, jnp.clip(k - s0, 0, n - 1)))

    rowt = lambda i, k: (i, 0)
    row = pl.BlockSpec((1, D), lambda i, k: (0, 0))
    return pl.pallas_call(
        body, name="inproj_bwd_x", grid=(s // tm, nk),
        in_specs=[piece_spec(s0, n) for s0, n in steps] +
                 [pl.BlockSpec((1, D, D), lambda i, k: (k // npc, 0, k % npc)),
                  pl.BlockSpec((tm, D), rowt), pl.BlockSpec((tm, D), rowt), row, row],
        out_specs=(pl.BlockSpec((tm, D), rowt), pl.BlockSpec((8, D), lambda i, k: (0, 0))),
        out_shape=(jax.ShapeDtypeStruct((s, D), F32), jax.ShapeDtypeStruct((8, D), F32)),
        scratch_shapes=[pltpu.VMEM((tm, D), F32)],
        compiler_params=_params(("arbitrary", "arbitrary")),
    )(*pieces, wg, x, dx2, gn, scale)


def _inproj_bwd_w(pieces, hbf, tk=512):
    s = hbf.shape[0]
    steps, nk = _pieces_steps(pieces)
    npc = PW // D
    ns = s // tk
    np_ = len(pieces)

    def body(*refs):
        d_refs = refs[:np_]
        h_ref, g_ref = refs[np_:]
        cb, k = pl.program_id(0), pl.program_id(1)

        @pl.when(k == 0)
        def _():
            g_ref[...] = jnp.zeros_like(g_ref)

        for (s0, n), d_ref in zip(steps, d_refs):
            @pl.when((cb >= s0) & (cb < s0 + n))
            def _(d_ref=d_ref):
                g_ref[0] += _dot_tn(h_ref[...], d_ref[...])

    def piece_spec(s0, n):
        def imap(cb, k):
            active = (cb >= s0) & (cb < s0 + n)
            return (jnp.where(active, k, 0), jnp.clip(cb - s0, 0, n - 1))
        return pl.BlockSpec((tk, D), imap)

    return pl.pallas_call(
        body, name="inproj_bwd_w", grid=(nk, ns),
        in_specs=[piece_spec(s0, n) for s0, n in steps] + [pl.BlockSpec((tk, D), lambda cb, k: (k, 0))],
        out_specs=pl.BlockSpec((1, D, D), lambda cb, k: (cb // npc, 0, cb % npc)),
        out_shape=jax.ShapeDtypeStruct((4, D, PW), F32),
        compiler_params=_params(("parallel", "arbitrary")),
    )(*pieces, hbf)


def _pair_exchange(arrs):
    na = len(arrs)

    def body(*refs):
        a_refs = refs[:na]
        own_refs = refs[na:2 * na]
        rb_refs = refs[2 * na:3 * na]
        lsem, ss, rs = refs[3 * na:]
        x, y, c = _coords()
        sib = (x, y, 1 - c)
        cps = []
        for t, (a, own, rb) in enumerate(zip(a_refs, own_refs, rb_refs)):
            hr = a.shape[1] // 2
            lc = pltpu.make_async_copy(a.at[:, pl.ds(c * hr, hr), :], own, lsem.at[t])
            lc.start()
            rc = _rcopy(a.at[:, pl.ds((1 - c) * hr, hr), :], rb, ss.at[t], rs.at[t], sib)
            rc.start()
            cps.append((lc, rc))
        for lc, rc in cps:
            rc.wait_recv()
            rc.wait_send()
            lc.wait()

    any_ = pl.BlockSpec(memory_space=pl.ANY)
    halves = [jax.ShapeDtypeStruct((a.shape[0], a.shape[1] // 2, a.shape[2]), a.dtype) for a in arrs]
    outs = pl.pallas_call(
        body, name="pair_exchange",
        out_shape=tuple(halves + halves),
        in_specs=[any_] * na, out_specs=tuple([any_] * (2 * na)),
        scratch_shapes=[pltpu.SemaphoreType.DMA((na,)), pltpu.SemaphoreType.DMA((na,)), pltpu.SemaphoreType.DMA((na,))],
        compiler_params=_params(),
    )(*arrs)
    return outs[:na], outs[na:]


def _chip_scatter(arrs):
    na = len(arrs)

    def body(*refs):
        a_refs = refs[:na]
        o_refs = refs[na:2 * na]
        lsem, ss, rs = refs[2 * na:]
        x, y, c = _coords()
        j = 2 * x + y
        chips = ((1, 0), (0, 1), (1, 1))
        cps = []
        for t, (a, o) in enumerate(zip(a_refs, o_refs)):
            lc = pltpu.make_async_copy(a.at[j], o.at[3], lsem.at[t])
            lc.start()
            cps.append(lc)
            for q, (kx, ky) in enumerate(chips):
                jt = j ^ (2 * kx + ky)
                rc = _rcopy(a.at[jt], o.at[q], ss.at[3 * t + q], rs.at[3 * t + q], (_flip(x, kx), _flip(y, ky), c))
                rc.start()
                cps.append(rc)
        for cp in cps:
            cp.wait()

    any_ = pl.BlockSpec(memory_space=pl.ANY)
    return pl.pallas_call(
        body, name="chip_scatter",
        out_shape=tuple(jax.ShapeDtypeStruct(a.shape, a.dtype) for a in arrs),
        in_specs=[any_] * na, out_specs=tuple([any_] * na),
        scratch_shapes=[pltpu.SemaphoreType.DMA((na,)), pltpu.SemaphoreType.DMA((3 * na,)),
                        pltpu.SemaphoreType.DMA((3 * na,))],
        compiler_params=_params(),
    )(*arrs)


def _pair_allgather(arrs):
    na = len(arrs)

    def body(*refs):
        a_refs = refs[:na]
        o_refs = refs[na:2 * na]
        lsem, ss, rs = refs[2 * na:]
        x, y, c = _coords()
        sib = (x, y, 1 - c)
        cps = []
        for t, (a, o) in enumerate(zip(a_refs, o_refs)):
            lc = pltpu.make_async_copy(a, o.at[c], lsem.at[t])
            lc.start()
            rc = _rcopy(a, o.at[c], ss.at[t], rs.at[t], sib)
            rc.start()
            cps.append((lc, rc))
        for t, (lc, rc) in enumerate(cps):
            _rcopy(a_refs[t], o_refs[t].at[1 - c], ss.at[t], rs.at[t], sib).wait_recv()
            rc.wait_send()
            lc.wait()

    any_ = pl.BlockSpec(memory_space=pl.ANY)
    return pl.pallas_call(
        body, name="pair_allgather",
        out_shape=tuple(jax.ShapeDtypeStruct((2,) + a.shape, a.dtype) for a in arrs),
        in_specs=[any_] * na, out_specs=tuple([any_] * na),
        scratch_shapes=[pltpu.SemaphoreType.DMA((na,)), pltpu.SemaphoreType.DMA((na,)), pltpu.SemaphoreType.DMA((na,))],
        compiler_params=_params(),
    )(*arrs)


def _add2(a, b, tr=256):
    n, r, cdim = a.shape

    def body(a_ref, b_ref, o_ref):
        o_ref[...] = a_ref[...] + b_ref[...]

    spec = pl.BlockSpec((1, tr, cdim), lambda i, j: (i, j, 0))
    return pl.pallas_call(
        body, name="add2", grid=(n, r // tr), in_specs=[spec, spec], out_specs=spec,
        out_shape=jax.ShapeDtypeStruct(a.shape, a.dtype),
        compiler_params=_params(("parallel", "parallel")),
    )(a, b)


def _sum_slots(a, tr=256):
    _, r, cdim = a.shape

    def body(a_ref, o_ref):
        o_ref[...] = ((a_ref[3] + a_ref[0]) + a_ref[1]) + a_ref[2]

    return pl.pallas_call(
        body, name="sum_slots", grid=(r // tr,),
        in_specs=[pl.BlockSpec((4, tr, cdim), lambda i: (0, i, 0))],
        out_specs=pl.BlockSpec((tr, cdim), lambda i: (i, 0)),
        out_shape=jax.ShapeDtypeStruct((r, cdim), a.dtype),
        compiler_params=_params(("parallel",)),
    )(a)


def _allreduce_small(pack):
    def body(p_ref, out_ref, rbuf, s1, r1, s2, r2):
        me = _my_index()
        chunk = lambda d: pl.ds(pl.multiple_of(d * AR_CHUNK, 8), AR_CHUNK)
        sends = []
        for k in range(1, NDEV):
            cp = _rcopy(p_ref.at[chunk(me ^ k)], rbuf.at[me], s1.at[k - 1], r1.at[k - 1], _peer(k))
            cp.start()
            sends.append(cp)
        rbuf[me] = p_ref[chunk(me), :]
        for k in range(1, NDEV):
            _rcopy(p_ref.at[chunk(me)], rbuf.at[me ^ k], s1.at[k - 1], r1.at[k - 1], _peer(k)).wait_recv()
        tot = rbuf[0]
        for d in range(1, NDEV):
            tot = tot + rbuf[d]
        out_ref[chunk(me), :] = tot
        for k in range(1, NDEV):
            cp = _rcopy(out_ref.at[chunk(me)], out_ref.at[chunk(me)], s2.at[k - 1], r2.at[k - 1], _peer(k))
            cp.start()
            sends.append(cp)
        for k in range(1, NDEV):
            _rcopy(out_ref.at[chunk(me)], out_ref.at[chunk(me ^ k)], s2.at[k - 1], r2.at[k - 1], _peer(k)).wait_recv()
        for cp in sends:
            cp.wait_send()

    vm = pl.BlockSpec(memory_space=pltpu.VMEM)
    return pl.pallas_call(
        body, name="allreduce_small",
        out_shape=jax.ShapeDtypeStruct((AR_ROWS, D), F32),
        in_specs=[vm], out_specs=vm,
        scratch_shapes=[pltpu.VMEM((NDEV, AR_CHUNK, D), F32),
                        pltpu.SemaphoreType.DMA((7,)), pltpu.SemaphoreType.DMA((7,)),
                        pltpu.SemaphoreType.DMA((7,)), pltpu.SemaphoreType.DMA((7,))],
        compiler_params=_params(),
    )(pack)


def _adamw(w, g, m, v, tr):
    r, cdim = w.shape

    def body(w_ref, g_ref, m_ref, v_ref, d_ref, nm_ref, nv_ref):
        gv = g_ref[...]
        nm = B1 * m_ref[...] + (1.0 - B1) * gv
        nv = B2 * v_ref[...] + (1.0 - B2) * (gv * gv)
        m_hat = nm / (1.0 - B1 ** STEP)
        v_hat = nv / (1.0 - B2 ** STEP)
        d_ref[...] = -LR * (m_hat / (jnp.sqrt(v_hat) + ADAM_EPS) + WD * w_ref[...])
        nm_ref[...] = nm
        nv_ref[...] = nv

    spec = pl.BlockSpec((tr, cdim), lambda i: (i, 0))
    sd = jax.ShapeDtypeStruct((r, cdim), F32)
    return pl.pallas_call(
        body, name="adamw", grid=(r // tr,), in_specs=[spec] * 4, out_specs=(spec,) * 3, out_shape=(sd,) * 3,
        compiler_params=_params(("parallel",)),
    )(w, g, m, v)


def _rope_tables(positions):
    inv_freq = ROPE_THETA ** (-jnp.arange(0, ROT, 2, dtype=F32) / ROT)
    ang = positions.astype(F32)[:, None] * inv_freq
    cos, sin = jnp.cos(ang), jnp.sin(ang)
    n = positions.shape[0]
    half = ROT // 2
    rc = jnp.concatenate([cos, cos, jnp.ones((n, DH - ROT), F32)], axis=1)
    rsa = jnp.concatenate([-sin, jnp.zeros((n, DH - half), F32)], axis=1)
    rsb = jnp.concatenate([jnp.zeros((n, half), F32), sin, jnp.zeros((n, DH - ROT), F32)], axis=1)
    return rc, rsa, rsb


def kernel(x, c, positions, g_norm, w_mod, b_mod, w_in, b_gate, conv_w, conv_b, w_a, b_a, w_x, b_x, lam, w_out_rnn, w_out_attn, w_o, g_final, loss_target, m_g_norm, m_w_mod, m_b_mod, m_w_in, m_b_gate, m_conv_w, m_conv_b, m_w_a, m_b_a, m_w_x, m_b_x, m_lam, m_w_out_rnn, m_w_out_attn, m_w_o, m_g_final, v_g_norm, v_w_mod, v_b_mod, v_w_in, v_b_gate, v_conv_w, v_conv_b, v_w_a, v_b_a, v_w_x, v_b_x, v_lam, v_w_out_rnn, v_w_out_attn, v_w_o, v_g_final):
    s = x.shape[1]
    xi = lax.axis_index("x")
    yi = lax.axis_index("y")
    shard = 2 * xi + yi
    x2d = x[0]
    tgt = loss_target[0]
    pos = positions[0]

    c_all, mod4 = _mod_fwd(c, w_mod[0], b_mod.reshape(4, 1, 768))
    mod = mod4.reshape(1, 3 * D)
    shift, scale, gate = mod[:, :D], mod[:, D:2 * D], mod[:, 2 * D:]
    w3_sh = jnp.stack([w_out_rnn[0], w_out_attn[0], w_o[0]]).astype(BF16)
    wsh = jnp.concatenate([w_in[0].astype(BF16), w3_sh.reshape(384, PW)], axis=0)
    wg = _gather_weights(wsh)
    w3 = wg[:, D:, :].reshape(4, 3, 256, D).transpose(1, 0, 2, 3).reshape(3, D, D)
    conv_all = _gather_small(conv_w[0])
    conv_full = conv_all[0::2].transpose(1, 0, 2).reshape(4, D)

    p, hbf = _norm_inproj(x2d, g_norm, shift, scale, wg)
    rc, rsa, rsb = _rope_tables(pos)
    pos_col = pos.reshape(s, 1)
    b_a3, b_x3 = b_a.reshape(H, 1, DH), b_x.reshape(H, 1, DH)
    hr, gr = _rnn_fwd(p, pos_col, conv_full, conv_b, w_a[0], b_a3, w_x[0], b_x3, lam)
    o, lse, ga = _attn_fwd(p, rc, rsa, rsb)

    dgr, dga, dc, dx2, vec_t, g_out = _tail(gr, ga, p, x2d, tgt, w3, b_gate, gate, g_final.reshape(1, D))

    dxr, dzr, g_wa, g_ba, g_wx, g_bx, g_lam, g_cw, g_cb = _rnn_bwd(
        p, hr, dgr, pos_col, conv_full, conv_b, w_a[0], b_a3, w_x[0], b_x3, lam)
    dq, dk, dv, dza = _attn_bwd(p, o, lse, dga, rc, rsa, rsb)

    pieces = [dxr, dzr, dq, dk, dv, dza, dc]
    grad_x, vec_n = _inproj_bwd_x(pieces, wg, x2d, dx2, g_norm, scale)
    g_win = _inproj_bwd_w(pieces, hbf)

    (own_a, own_b), (rb_a, rb_b) = _pair_exchange([g_win, g_out])
    q_a, q_b = _add2(own_a, rb_a), _add2(own_b, rb_b, tr=128)
    r_a, r_b = _chip_scatter([q_a, q_b])
    f_a, f_b = _sum_slots(r_a), _sum_slots(r_b, tr=128)
    full_a, full_b = _pair_allgather([f_a, f_b])
    grad_w_in = full_a.reshape(D, PW)
    g3 = full_b.reshape(3, 256, D)

    dmod_row = jnp.concatenate([vec_n[0:1], vec_n[1:2], vec_t[1:2]], axis=1)
    pack = jnp.concatenate([
        vec_n[2:3],
        dmod_row.reshape(3, D),
        vec_t[2:4],
        g_cb.reshape(1, D),
        g_wa.reshape(128, D),
        g_ba.reshape(1, D),
        g_wx.reshape(128, D),
        g_bx.reshape(1, D),
        g_lam.reshape(1, D),
        vec_t[0:1],
        g_cw.transpose(1, 0, 2).reshape(4, D),
        vec_t[4:5],
        jnp.zeros((AR_ROWS - 272, D), F32)], axis=0)
    red = _allreduce_small(pack)
    loss = red[271, 0]
    grad_w_mod = _mod_bwd(dmod_row.reshape(4, 1, 768), c_all)
    g_conv_sh = lax.dynamic_slice_in_dim(red[267:271], shard * 256, 256, axis=1)

    def small_pack(g_norm_, b_mod_, b_gate_, conv_b_, w_a_, b_a_, w_x_, b_x_, lam_, g_final_, conv_w_):
        return jnp.concatenate([
            g_norm_.reshape(1, D), b_mod_.reshape(3, D), b_gate_.reshape(2, D), conv_b_.reshape(1, D),
            w_a_.reshape(128, D), b_a_.reshape(1, D), w_x_.reshape(128, D), b_x_.reshape(1, D),
            lam_.reshape(1, D), g_final_.reshape(1, D), conv_w_.reshape(1, D),
            jnp.zeros((4, D), F32)], axis=0)

    wp = small_pack(g_norm, b_mod, b_gate, conv_b, w_a, b_a, w_x, b_x, lam, g_final, conv_w)
    mp = small_pack(m_g_norm, m_b_mod, m_b_gate, m_conv_b, m_w_a, m_b_a, m_w_x, m_b_x, m_lam, m_g_final, m_conv_w)
    vp = small_pack(v_g_norm, v_b_mod, v_b_gate, v_conv_b, v_w_a, v_b_a, v_w_x, v_b_x, v_lam, v_g_final, v_conv_w)
    gp = jnp.concatenate([red[0:267], g_conv_sh.reshape(1, D), jnp.zeros((4, D), F32)], axis=0)
    small = _adamw(wp, gp, mp, vp, tr=136)

    def unpack(a):
        return dict(
            g_norm=a[0:1], b_mod=a[1:4].reshape(1, 3 * D), b_gate=a[4:6].reshape(1, 2 * D), conv_b=a[6:7],
            w_a=a[7:135].reshape(1, H, DH, DH), b_a=a[135:136].reshape(1, H, DH),
            w_x=a[136:264].reshape(1, H, DH, DH), b_x=a[264:265].reshape(1, H, DH), lam=a[265:266],
            g_final=a[266].reshape(D), conv_w=a[267:268].reshape(1, 4, 256))

    big_in = _adamw(w_in[0], grad_w_in, m_w_in[0], v_w_in[0], tr=256)
    big_mod = _adamw(w_mod[0], grad_w_mod, m_w_mod[0], v_w_mod[0], tr=256)
    w3f = jnp.concatenate([w_out_rnn[0], w_out_attn[0], w_o[0]], axis=0)
    m3f = jnp.concatenate([m_w_out_rnn[0], m_w_out_attn[0], m_w_o[0]], axis=0)
    v3f = jnp.concatenate([v_w_out_rnn[0], v_w_out_attn[0], v_w_o[0]], axis=0)
    big_out = _adamw(w3f, g3.reshape(768, D), m3f, v3f, tr=256)

    names = ["g_norm", "w_mod", "b_mod", "w_in", "b_gate", "conv_w", "conv_b", "w_a", "b_a", "w_x", "b_x", "lam",
             "w_out_rnn", "w_out_attn", "w_o", "g_final"]
    grads = unpack(gp)
    grads.update(w_mod=grad_w_mod[None], w_in=grad_w_in[None],
                 w_out_rnn=g3[0][None], w_out_attn=g3[1][None], w_o=g3[2][None])
    outs = [grads]
    for idx in range(3):
        d = unpack(small[idx])
        d.update(w_mod=big_mod[idx][None], w_in=big_in[idx][None],
                 w_out_rnn=big_out[idx][0:256][None], w_out_attn=big_out[idx][256:512][None],
                 w_o=big_out[idx][512:768][None])
        outs.append(d)
    flat = [d[n] for d in outs for n in names]
    return (loss, grad_x[None], *flat)
```

```python
import jax
import jax.numpy as jnp
from jax import lax
from jax.experimental import pallas as pl
from jax.experimental.pallas import tpu as pltpu

F32, BF16 = jnp.float32, jnp.bfloat16
MESH = pl.DeviceIdType.MESH
HIGHEST = lax.Precision.HIGHEST

D = 1024
H = 8
DH = 128
PW = 2048
EPS = 1e-6
LRU_C = 8.0
SCALE = DH ** -0.5
NEG = -1e30
SPAN = 2048
UB = 128
DILATIONS = (1, 4, 16)
ROPE_THETA = 500000.0
ROT = 32

LR, B1, B2, ADAM_EPS, WD, STEP = 0.001, 0.9, 0.999, 1e-08, 0.01, 10

WROWS = 1024 + 384
NDEV = 8
AR_ROWS = 320
AR_CHUNK = AR_ROWS // NDEV


def _params(sem=None, vmem_mb=56):
    return pltpu.CompilerParams(dimension_semantics=sem, vmem_limit_bytes=vmem_mb * 2 ** 20)


def _coords():
    return lax.axis_index("x"), lax.axis_index("y"), lax.axis_index("c")


def _flip(v, bit):
    return 1 - v if bit else v


def _peer(k):
    x, y, c = _coords()
    return (_flip(x, (k >> 2) & 1), _flip(y, (k >> 1) & 1), _flip(c, k & 1))


def _my_index():
    x, y, c = _coords()
    return 4 * x + 2 * y + c


def _rcopy(src, dst, ssem, rsem, dev):
    return pltpu.make_async_remote_copy(src_ref=src, dst_ref=dst, send_sem=ssem, recv_sem=rsem,
                                        device_id=dev, device_id_type=MESH)


def _sigmoid(x):
    return jax.nn.sigmoid(x)


def _dot(a, b):
    return jnp.dot(a, b, preferred_element_type=F32)


def _dot_nt(a, b):
    return lax.dot_general(a, b, (((1,), (1,)), ((), ())), preferred_element_type=F32)


def _dot_tn(a, b):
    return lax.dot_general(a, b, (((0,), (0,)), ((), ())), preferred_element_type=F32)


def _colsum(a):
    return jnp.sum(a, axis=0, keepdims=True)


def _mod_fwd(c, w_mod_sh, b_mod4):
    def body(c_ref, w_ref, b_ref, call_ref, mod_ref, rows_ref, cmat_ref, s1, r1, s2, r2):
        x, y, _ = _coords()
        me = _my_index()
        j = 2 * x + y
        call_ref[me] = c_ref[...]
        sends = []
        for k in range(1, NDEV):
            cp = _rcopy(call_ref.at[me], call_ref.at[me], s1.at[k - 1], r1.at[k - 1], _peer(k))
            cp.start()
            sends.append(cp)
        for k in range(1, NDEV):
            pk = me ^ k
            _rcopy(call_ref.at[pk], call_ref.at[pk], s1.at[k - 1], r1.at[k - 1], _peer(k)).wait_recv()
        for b in range(NDEV):
            cmat_ref[pl.ds(b, 1), :] = call_ref[b]
        cm = cmat_ref[...]
        act = cm * _sigmoid(cm)
        mp = jnp.dot(act, w_ref[...], preferred_element_type=F32, precision=HIGHEST) + b_ref[j]
        for b in range(NDEV):
            rows_ref[b] = mp[b:b + 1]
        mod_ref[j] = rows_ref[me]
        for q, k in enumerate((2, 4, 6)):
            cp = _rcopy(rows_ref.at[me ^ k], mod_ref.at[j], s2.at[q], r2.at[q], _peer(k))
            cp.start()
            sends.append(cp)
        for q, k in enumerate((2, 4, 6)):
            jq = j ^ (k >> 1)
            _rcopy(rows_ref.at[me], mod_ref.at[jq], s2.at[q], r2.at[q], _peer(k)).wait_recv()
        for cp in sends:
            cp.wait_send()

    vm = pl.BlockSpec(memory_space=pltpu.VMEM)
    return pl.pallas_call(
        body, name="mod_fwd",
        out_shape=(jax.ShapeDtypeStruct((NDEV, 1, D), F32), jax.ShapeDtypeStruct((4, 1, 768), F32)),
        in_specs=[vm, vm, vm], out_specs=(vm, vm),
        scratch_shapes=[pltpu.VMEM((NDEV, 1, 768), F32), pltpu.VMEM((NDEV, D), F32),
                        pltpu.SemaphoreType.DMA((7,)), pltpu.SemaphoreType.DMA((7,)),
                        pltpu.SemaphoreType.DMA((3,)), pltpu.SemaphoreType.DMA((3,))],
        compiler_params=_params(),
    )(c, w_mod_sh, b_mod4)


def _mod_bwd(dmod4, c_all):
    def body(d_ref, call_ref, gw_ref, dall_ref, cmat_ref, dmat_ref, s1, r1):
        x, y, _ = _coords()
        me = _my_index()
        j = 2 * x + y
        dall_ref[me] = d_ref[...]
        sends = []
        for k in range(1, NDEV):
            cp = _rcopy(dall_ref.at[me], dall_ref.at[me], s1.at[k - 1], r1.at[k - 1], _peer(k))
            cp.start()
            sends.append(cp)
        for k in range(1, NDEV):
            pk = me ^ k
            _rcopy(dall_ref.at[pk], dall_ref.at[pk], s1.at[k - 1], r1.at[k - 1], _peer(k)).wait_recv()
        for cp in sends:
            cp.wait_send()
        for b in range(NDEV):
            cmat_ref[pl.ds(b, 1), :] = call_ref[b]
            dmat_ref[pl.ds(b, 1), :] = dall_ref[b, j]
        cm = cmat_ref[...]
        act = cm * _sigmoid(cm)
        gw_ref[...] = lax.dot_general(act, dmat_ref[...], (((0,), (0,)), ((), ())),
                                      preferred_element_type=F32, precision=HIGHEST)

    vm = pl.BlockSpec(memory_space=pltpu.VMEM)
    return pl.pallas_call(
        body, name="mod_bwd",
        out_shape=jax.ShapeDtypeStruct((D, 768), F32),
        in_specs=[vm, vm], out_specs=vm,
        scratch_shapes=[pltpu.VMEM((NDEV, 4, 1, 768), F32), pltpu.VMEM((NDEV, D), F32), pltpu.VMEM((NDEV, 768), F32),
                        pltpu.SemaphoreType.DMA((7,)), pltpu.SemaphoreType.DMA((7,))],
        compiler_params=_params(),
    )(dmod4, c_all)


def _gather_weights(wsh):
    rows, cols = wsh.shape
    half = rows // 2
    nch = 4
    cr = half // nch

    def body(w_ref, g_ref, lsem, ss, rs):
        x, y, c = _coords()
        j = 2 * x + y
        sib = (x, y, 1 - c)
        local = pltpu.make_async_copy(w_ref, g_ref.at[j], lsem)
        local.start()
        sends = []
        chips = ((1, 0), (0, 1), (1, 1))
        mine = lambda n: pl.ds(c * half + n * cr, cr)
        theirs = lambda n: pl.ds((1 - c) * half + n * cr, cr)
        for n in range(nch):
            for q, (kx, ky) in enumerate(chips):
                e = nch * q + n
                cp = _rcopy(w_ref.at[mine(n)], g_ref.at[j, mine(n)], ss.at[e], rs.at[e],
                            (_flip(x, kx), _flip(y, ky), c))
                cp.start()
                sends.append(cp)
        for n in range(nch):
            for q, (kx, ky) in enumerate(chips):
                jq = j ^ (2 * kx + ky)
                e = nch * q + n
                _rcopy(w_ref.at[mine(n)], g_ref.at[jq, mine(n)], ss.at[e], rs.at[e], sib).wait_recv()
                cp = _rcopy(g_ref.at[jq, mine(n)], g_ref.at[jq, mine(n)], ss.at[3 * nch + e], rs.at[3 * nch + e], sib)
                cp.start()
                sends.append(cp)
        for n in range(nch):
            for q, (kx, ky) in enumerate(chips):
                jq = j ^ (2 * kx + ky)
                e = nch * q + n
                _rcopy(w_ref.at[mine(n)], g_ref.at[jq, theirs(n)], ss.at[3 * nch + e], rs.at[3 * nch + e], sib).wait_recv()
        for cp in sends:
            cp.wait_send()
        local.wait()

    any_ = pl.BlockSpec(memory_space=pl.ANY)
    return pl.pallas_call(
        body, name="gather_weights",
        out_shape=jax.ShapeDtypeStruct((4, rows, cols), wsh.dtype),
        in_specs=[any_], out_specs=any_,
        scratch_shapes=[pltpu.SemaphoreType.DMA, pltpu.SemaphoreType.DMA((6 * nch,)), pltpu.SemaphoreType.DMA((6 * nch,))],
        compiler_params=_params(),
    )(wsh)


def _gather_small(v):
    r, cdim = v.shape

    def body(v_ref, out_ref, ss, rs):
        me = _my_index()
        out_ref[me] = v_ref[...]
        sends = []
        for k in range(1, NDEV):
            cp = _rcopy(out_ref.at[me], out_ref.at[me], ss.at[k - 1], rs.at[k - 1], _peer(k))
            cp.start()
            sends.append(cp)
        for k in range(1, NDEV):
            pk = me ^ k
            _rcopy(out_ref.at[pk], out_ref.at[pk], ss.at[k - 1], rs.at[k - 1], _peer(k)).wait_recv()
        for cp in sends:
            cp.wait_send()

    vm = pl.BlockSpec(memory_space=pltpu.VMEM)
    return pl.pallas_call(
        body, name="gather_small",
        out_shape=jax.ShapeDtypeStruct((NDEV, r, cdim), v.dtype),
        in_specs=[vm], out_specs=vm,
        scratch_shapes=[pltpu.SemaphoreType.DMA((7,)), pltpu.SemaphoreType.DMA((7,))],
        compiler_params=_params(),
    )(v)


def _norm_inproj(x, gn, shift, scale, wg, tm=1024, tn=512):
    s = x.shape[0]
    npc = PW // tn

    def body(x_ref, gn_ref, sh_ref, sc_ref, w_ref, p_ref, h_ref, hs):
        @pl.when(pl.program_id(1) == 0)
        def _():
            xt = x_ref[...]
            rstd = lax.rsqrt(jnp.mean(xt * xt, axis=-1, keepdims=True) + EPS)
            h = (xt * rstd * gn_ref[...]) * (1.0 + sc_ref[...]) + sh_ref[...]
            hs[...] = h.astype(BF16)
            h_ref[...] = hs[...]

        p_ref[0] = _dot(hs[...], w_ref[0])

    row = pl.BlockSpec((1, D), lambda i, j: (0, 0))
    return pl.pallas_call(
        body, name="norm_inproj", grid=(s // tm, 4 * npc),
        in_specs=[pl.BlockSpec((tm, D), lambda i, j: (i, 0)), row, row, row,
                  pl.BlockSpec((1, D, tn), lambda i, j: (j // npc, 0, j % npc))],
        out_specs=(pl.BlockSpec((1, tm, tn), lambda i, j: (j // npc, i, j % npc)),
                   pl.BlockSpec((tm, D), lambda i, j: (i, 0))),
        out_shape=(jax.ShapeDtypeStruct((4, s, PW), F32), jax.ShapeDtypeStruct((s, D), BF16)),
        scratch_shapes=[pltpu.VMEM((tm, D), BF16)],
        compiler_params=_params(("parallel", "arbitrary")),
    )(x, gn, shift, scale, wg)


def _shift_down(prev8, cur, d):
    t = cur.shape[0]
    ext = jnp.concatenate([prev8, cur], axis=0)
    return pltpu.roll(ext, d, 0)[8:]


def _shift_up(cur, next8, d):
    t = cur.shape[0]
    ext = jnp.concatenate([cur, next8], axis=0)
    return pltpu.roll(ext, t + 8 - d, 0)[:t]


def _rnn_gates(xr, prev8, cw, cb, wa, ba, wx, bx, lam, reset):
    xc = cw[3:4] * xr + cb
    for d in (1, 2, 3):
        xc = xc + cw[3 - d:4 - d] * _shift_down(prev8, xr, d)
    xcb = xc.astype(BF16)
    r = _sigmoid(_dot(xcb, wa.astype(BF16)) + ba)
    ig = _sigmoid(_dot(xcb, wx.astype(BF16)) + bx)
    nl = -lam
    sp = jnp.maximum(nl, 0.0) + jnp.log1p(jnp.exp(-jnp.abs(nl)))
    log_a = (-LRU_C * r) * sp
    a = jnp.where(reset, 0.0, jnp.exp(log_a))
    mult = jnp.where(reset, 1.0, jnp.sqrt(1.0 - jnp.exp(2.0 * log_a)))
    return xc, r, ig, sp, a, mult


def _scan_down(a, b, t):
    rows = lax.broadcasted_iota(jnp.int32, a.shape, 0)
    d = 1
    while d < t:
        m = rows >= d
        a_s = pltpu.roll(a, d, 0)
        b_s = pltpu.roll(b, d, 0)
        b = jnp.where(m, a * b_s + b, b)
        a = jnp.where(m, a * a_s, a)
        d *= 2
    return a, b


def _scan_up(a, b, t):
    rows = lax.broadcasted_iota(jnp.int32, a.shape, 0)
    d = 1
    while d < t:
        m = rows < t - d
        a_s = pltpu.roll(a, t - d, 0)
        b_s = pltpu.roll(b, t - d, 0)
        b = jnp.where(m, a * b_s + b, b)
        a = jnp.where(m, a * a_s, a)
        d *= 2
    return a, b


def _rnn_fwd(p, pos, conv_w, conv_b, w_a, b_a, w_x, b_x, lam, tt=256):
    s = p.shape[1]
    nt = s // tt

    def body(xr_ref, z_ref, pos_ref, cw_ref, cb_ref, wa_ref, ba_ref, wx_ref, bx_ref, lam_ref,
             hr_ref, gr_ref, xprev, hprev):
        @pl.when(pl.program_id(1) == 0)
        def _():
            xprev[...] = jnp.zeros_like(xprev)
            hprev[...] = jnp.zeros_like(hprev)

        xr = xr_ref[0]
        z = z_ref[0]
        reset = pos_ref[...] == 0
        xc, r, ig, sp, a, mult = _rnn_gates(xr, xprev[...], cw_ref[...], cb_ref[...], wa_ref[0], ba_ref[0],
                                            wx_ref[0], bx_ref[0], lam_ref[...], reset)
        bx = mult * ig * xc
        a_cum, h0 = _scan_down(a, bx, tt)
        h = a_cum * hprev[0:1] + h0
        xprev[...] = xr[tt - 8:]
        hprev[...] = jnp.broadcast_to(h[tt - 1:tt], (8, DH))
        hr_ref[...] = h
        gr_ref[...] = (h * (z * _sigmoid(z))).astype(BF16)

    head_row = lambda hh, t: (0, hh)
    return pl.pallas_call(
        body, name="rnn_fwd", grid=(H, nt),
        in_specs=[pl.BlockSpec((1, tt, DH), lambda hh, t: (0, t, hh)),
                  pl.BlockSpec((1, tt, DH), lambda hh, t: (0, t, H + hh)),
                  pl.BlockSpec((tt, 1), lambda hh, t: (t, 0)),
                  pl.BlockSpec((4, DH), head_row), pl.BlockSpec((1, DH), head_row),
                  pl.BlockSpec((1, DH, DH), lambda hh, t: (hh, 0, 0)), pl.BlockSpec((1, 1, DH), lambda hh, t: (hh, 0, 0)),
                  pl.BlockSpec((1, DH, DH), lambda hh, t: (hh, 0, 0)), pl.BlockSpec((1, 1, DH), lambda hh, t: (hh, 0, 0)),
                  pl.BlockSpec((1, DH), head_row)],
        out_specs=(pl.BlockSpec((tt, DH), lambda hh, t: (t, hh)), pl.BlockSpec((tt, DH), lambda hh, t: (t, hh))),
        out_shape=(jax.ShapeDtypeStruct((s, D), F32), jax.ShapeDtypeStruct((s, D), BF16)),
        scratch_shapes=[pltpu.VMEM((8, DH), F32), pltpu.VMEM((8, DH), F32)],
        compiler_params=_params(("parallel", "arbitrary")),
    )(p, p, pos, conv_w, conv_b, w_a, b_a, w_x, b_x, lam)


def _rnn_bwd(p, hr, dgr, pos, conv_w, conv_b, w_a, b_a, w_x, b_x, lam, tt=256):
    s = p.shape[1]
    nt = s // tt
    t8 = tt // 8

    def body(xr_ref, z_ref, xp_ref, hr_ref, hp_ref, dg_ref, pos_ref, cw_ref, cb_ref, wa_ref, ba_ref, wx_ref, bx_ref,
             lam_ref, dxr_ref, dz_ref, gwa_ref, gba_ref, gwx_ref, gbx_ref, glam_ref, gcw_ref, gcb_ref,
             a_next, g_next, dxc_next):
        t = pl.program_id(1)
        has_prev = t < nt - 1

        @pl.when(t == 0)
        def _():
            a_next[...] = jnp.zeros_like(a_next)
            g_next[...] = jnp.zeros_like(g_next)
            dxc_next[...] = jnp.zeros_like(dxc_next)
            gwa_ref[...] = jnp.zeros_like(gwa_ref)
            gba_ref[...] = jnp.zeros_like(gba_ref)
            gwx_ref[...] = jnp.zeros_like(gwx_ref)
            gbx_ref[...] = jnp.zeros_like(gbx_ref)
            glam_ref[...] = jnp.zeros_like(glam_ref)
            gcw_ref[...] = jnp.zeros_like(gcw_ref)
            gcb_ref[...] = jnp.zeros_like(gcb_ref)

        xr = xr_ref[0]
        z = z_ref[0]
        hr_blk = hr_ref[...]
        dg = dg_ref[...]
        xprev = jnp.where(has_prev, xp_ref[0], 0.0)
        hprev8 = jnp.where(has_prev, hp_ref[...], 0.0)
        reset = pos_ref[...] == 0
        cw = cw_ref[...]
        wa = wa_ref[0]
        wx = wx_ref[0]
        lam_v = lam_ref[...]
        xc, r, ig, sp, a, mult = _rnn_gates(xr, xprev, cw, cb_ref[...], wa, ba_ref[0], wx, bx_ref[0], lam_v, reset)

        sz = _sigmoid(z)
        dh = dg * (z * sz)
        dz_ref[...] = (dg * hr_blk * (sz * (1.0 + z * (1.0 - sz)))).astype(BF16)

        an = _shift_up(a, a_next[...], 1)
        a_cum, g0 = _scan_up(an, dh, tt)
        g = g0 + a_cum * g_next[0:1]
        a_next[...] = jnp.broadcast_to(a[0:1], (8, DH))
        g_next[...] = jnp.broadcast_to(g[0:1], (8, DH))

        hm1 = _shift_down(hprev8, hr_blk, 1)
        da = g * hm1
        dmult = g * (ig * xc)
        di = g * (mult * xc)
        dxc = g * (mult * ig)
        dla = jnp.where(reset, 0.0, da * a - dmult * (a * a) / mult)
        dr = dla * (-LRU_C * sp)
        dsp = _colsum(dla * (-LRU_C * r))
        glam_ref[0] += dsp * (-_sigmoid(-lam_v))
        dpa = dr * r * (1.0 - r)
        dpx = di * ig * (1.0 - ig)
        dpab = dpa.astype(BF16)
        dpxb = dpx.astype(BF16)
        dxc = dxc + _dot_nt(dpab, wa.astype(BF16)) + _dot_nt(dpxb, wx.astype(BF16))
        xcb = xc.astype(BF16)
        gwa_ref[0] += _dot_tn(xcb, dpab)
        gwx_ref[0] += _dot_tn(xcb, dpxb)
        gba_ref[0] += _colsum(dpa)
        gbx_ref[0] += _colsum(dpx)

        dxr = cw[3:4] * dxc
        for d in (1, 2, 3):
            dxr = dxr + cw[3 - d:4 - d] * _shift_up(dxc, dxc_next[...], d)
        dxr_ref[...] = dxr.astype(BF16)
        dxc_next[...] = dxc[0:8]
        gcb_ref[0] += _colsum(dxc)
        gcw_ref[0, 3:4, :] += _colsum(xr * dxc)
        for d in (1, 2, 3):
            gcw_ref[0, 3 - d:4 - d, :] += _colsum(_shift_down(xprev, xr, d) * dxc)

    rt = lambda t: nt - 1 - t
    prev8 = lambda t: jnp.maximum(rt(t) * t8 - 1, 0)
    head_row = lambda hh, t: (0, hh)
    hsm = lambda hh, t: (hh, 0, 0)
    return pl.pallas_call(
        body, name="rnn_bwd", grid=(H, nt),
        in_specs=[pl.BlockSpec((1, tt, DH), lambda hh, t: (0, rt(t), hh)),
                  pl.BlockSpec((1, tt, DH), lambda hh, t: (0, rt(t), H + hh)),
                  pl.BlockSpec((1, 8, DH), lambda hh, t: (0, prev8(t), hh)),
                  pl.BlockSpec((tt, DH), lambda hh, t: (rt(t), hh)),
                  pl.BlockSpec((8, DH), lambda hh, t: (prev8(t), hh)),
                  pl.BlockSpec((tt, DH), lambda hh, t: (rt(t), hh)),
                  pl.BlockSpec((tt, 1), lambda hh, t: (rt(t), 0)),
                  pl.BlockSpec((4, DH), head_row), pl.BlockSpec((1, DH), head_row),
                  pl.BlockSpec((1, DH, DH), hsm), pl.BlockSpec((1, 1, DH), hsm),
                  pl.BlockSpec((1, DH, DH), hsm), pl.BlockSpec((1, 1, DH), hsm),
                  pl.BlockSpec((1, DH), head_row)],
        out_specs=(pl.BlockSpec((tt, DH), lambda hh, t: (rt(t), hh)), pl.BlockSpec((tt, DH), lambda hh, t: (rt(t), hh)),
                   pl.BlockSpec((1, DH, DH), hsm), pl.BlockSpec((1, 1, DH), hsm),
                   pl.BlockSpec((1, DH, DH), hsm), pl.BlockSpec((1, 1, DH), hsm),
                   pl.BlockSpec((1, 1, DH), hsm), pl.BlockSpec((1, 4, DH), hsm), pl.BlockSpec((1, 1, DH), hsm)),
        out_shape=(jax.ShapeDtypeStruct((s, D), BF16), jax.ShapeDtypeStruct((s, D), BF16),
                   jax.ShapeDtypeStruct((H, DH, DH), F32), jax.ShapeDtypeStruct((H, 1, DH), F32),
                   jax.ShapeDtypeStruct((H, DH, DH), F32), jax.ShapeDtypeStruct((H, 1, DH), F32),
                   jax.ShapeDtypeStruct((H, 1, DH), F32), jax.ShapeDtypeStruct((H, 4, DH), F32),
                   jax.ShapeDtypeStruct((H, 1, DH), F32)),
        scratch_shapes=[pltpu.VMEM((8, DH), F32), pltpu.VMEM((8, DH), F32), pltpu.VMEM((8, DH), F32)],
        compiler_params=_params(("parallel", "arbitrary")),
    )(p, p, p, hr, hr, dgr, pos, conv_w, conv_b, w_a, b_a, w_x, b_x, lam)


def _rope(t, c, sa, sb):
    return t * c + pltpu.roll(t, DH - ROT // 2, 1) * sa + pltpu.roll(t, ROT // 2, 1) * sb


def _rope_bwd(g, c, sa, sb):
    return g * c + pltpu.roll(g * sa, ROT // 2, 1) + pltpu.roll(g * sb, DH - ROT // 2, 1)


def _unit_bases(gi, u):
    dil = DILATIONS[gi]
    if dil == 1:
        return u * UB, SPAN + (u - 1) * UB, u == 0
    if dil == 4:
        blk, r = u // 4, u % 4
        return blk * 4 * UB + r, SPAN + (blk - 1) * 4 * UB + r, blk == 0
    return u, u, True


def _band_mask(first_in_span, has_prev):
    qi = lax.broadcasted_iota(jnp.int32, (UB, 2 * UB), 0)
    ki = lax.broadcasted_iota(jnp.int32, (UB, 2 * UB), 1)
    dist = UB + qi - ki
    band = (dist >= 0) & (dist <= UB)
    return band & ((ki >= UB) | jnp.logical_not(first_in_span) | has_prev)


def _attn_fwd(p, rc, rsa, rsb):
    s = p.shape[1]
    ns = s // SPAN
    nunit = SPAN // UB

    def body(q_ref, k_ref, v_ref, z_ref, c_ref, sa_ref, sb_ref, o_ref, lse_ref, ga_ref,
             qr, kf, vf, acc, mm, ll):
        n = pl.program_id(1)

        @pl.when(n == 0)
        def _():
            kf[0:SPAN] = jnp.zeros((SPAN, DH), F32)
            vf[0:SPAN] = jnp.zeros((SPAN, DH), F32)

        c, sa, sb = c_ref[...], sa_ref[...], sb_ref[...]
        qr[...] = _rope(q_ref[0], c, sa, sb)
        kf[SPAN:] = _rope(k_ref[0], c, sa, sb)
        vf[SPAN:] = v_ref[0]
        has_prev = n > 0

        for gi, dil in enumerate(DILATIONS):
            def unit(u, carry, gi=gi, dil=dil):
                qb0, kb0, first = _unit_bases(gi, u)
                qsl = pl.ds(qb0, UB, stride=dil) if dil > 1 else pl.ds(pl.multiple_of(qb0, UB), UB)
                ksl = pl.ds(kb0, 2 * UB, stride=dil) if dil > 1 else pl.ds(pl.multiple_of(kb0, UB), 2 * UB)
                qb = qr[qsl, :].astype(BF16)
                kb = kf[ksl, :].astype(BF16)
                vb = vf[ksl, :].astype(BF16)
                sc = _dot_nt(qb, kb) * SCALE
                sc = jnp.where(_band_mask(first, has_prev), sc, NEG)
                m = jnp.max(sc, axis=-1, keepdims=True)
                pr = jnp.exp(sc - m)
                l = jnp.sum(pr, axis=-1, keepdims=True)
                acc[gi, qsl, :] = _dot(pr.astype(BF16), vb)
                mm[gi, qsl, :] = jnp.broadcast_to(m, (UB, DH))
                ll[gi, qsl, :] = jnp.broadcast_to(l, (UB, DH))
                return carry

            lax.fori_loop(0, nunit, unit, 0)

        m_all = jnp.maximum(jnp.maximum(mm[0], mm[1]), mm[2])
        num = jnp.zeros((SPAN, DH), F32)
        den = jnp.zeros((SPAN, DH), F32)
        for gi in range(3):
            w = jnp.exp(mm[gi] - m_all)
            num = num + w * acc[gi]
            den = den + w * ll[gi]
        o = num / den
        o_ref[...] = o
        lse_ref[...] = m_all + jnp.log(den)
        z = z_ref[0]
        ga_ref[...] = (o * (z * _sigmoid(z))).astype(BF16)
        kf[0:SPAN] = kf[SPAN:]
        vf[0:SPAN] = vf[SPAN:]

    blk = lambda piece, off: pl.BlockSpec((1, SPAN, DH), lambda hh, n: (piece, n, off + hh))
    tab = pl.BlockSpec((SPAN, DH), lambda hh, n: (n, 0))
    outb = pl.BlockSpec((SPAN, DH), lambda hh, n: (n, hh))
    return pl.pallas_call(
        body, name="attn_fwd", grid=(H, ns),
        in_specs=[blk(1, 0), blk(1, H), blk(2, 0), blk(2, H), tab, tab, tab],
        out_specs=(outb, outb, outb),
        out_shape=(jax.ShapeDtypeStruct((s, D), F32), jax.ShapeDtypeStruct((s, D), F32),
                   jax.ShapeDtypeStruct((s, D), BF16)),
        scratch_shapes=[pltpu.VMEM((SPAN, DH), F32), pltpu.VMEM((2 * SPAN, DH), F32), pltpu.VMEM((2 * SPAN, DH), F32),
                        pltpu.VMEM((3, SPAN, DH), F32), pltpu.VMEM((3, SPAN, DH), F32), pltpu.VMEM((3, SPAN, DH), F32)],
        compiler_params=_params(("parallel", "arbitrary")),
    )(p, p, p, p, rc, rsa, rsb)


def _attn_bwd(p, o, lse, dga, rc, rsa, rsb):
    s = p.shape[1]
    ns = s // SPAN
    nunit = SPAN // UB

    def body(q_ref, k_ref, kp_ref, v_ref, vp_ref, z_ref, c_ref, sa_ref, sb_ref, cp_ref, sap_ref, sbp_ref,
             o_ref, lse_ref, dg_ref, dq_ref, dk_ref, dv_ref, dz_ref,
             qr, kf, vf, dof, dlt, dqa, dkf, dvf):
        step = pl.program_id(1)
        n = ns - 1 - step
        has_prev = n > 0

        @pl.when(step == 0)
        def _():
            dkf[...] = jnp.zeros_like(dkf)
            dvf[...] = jnp.zeros_like(dvf)

        @pl.when(step > 0)
        def _():
            dkf[SPAN:] = dkf[0:SPAN]
            dvf[SPAN:] = dvf[0:SPAN]
            dkf[0:SPAN] = jnp.zeros((SPAN, DH), F32)
            dvf[0:SPAN] = jnp.zeros((SPAN, DH), F32)

        c, sa, sb = c_ref[...], sa_ref[...], sb_ref[...]
        qr[...] = _rope(q_ref[0], c, sa, sb)
        kf[SPAN:] = _rope(k_ref[0], c, sa, sb)
        vf[SPAN:] = v_ref[0]
        kf[0:SPAN] = jnp.where(has_prev, _rope(kp_ref[0], cp_ref[...], sap_ref[...], sbp_ref[...]), 0.0)
        vf[0:SPAN] = jnp.where(has_prev, vp_ref[0], 0.0)
        z = z_ref[0]
        sz = _sigmoid(z)
        dg = dg_ref[...]
        ov = o_ref[...]
        do = dg * (z * sz)
        dz_ref[...] = (dg * ov * (sz * (1.0 + z * (1.0 - sz)))).astype(BF16)
        dof[...] = do
        dlt[...] = jnp.broadcast_to(jnp.sum(do * ov, axis=-1, keepdims=True), (SPAN, DH))
        dqa[...] = jnp.zeros_like(dqa)

        for gi, dil in enumerate(DILATIONS):
            def unit(u, carry, gi=gi, dil=dil):
                qb0, kb0, first = _unit_bases(gi, u)
                qsl = pl.ds(qb0, UB, stride=dil) if dil > 1 else pl.ds(pl.multiple_of(qb0, UB), UB)
                ksl = pl.ds(kb0, 2 * UB, stride=dil) if dil > 1 else pl.ds(pl.multiple_of(kb0, UB), 2 * UB)
                qb = qr[qsl, :].astype(BF16)
                kb = kf[ksl, :].astype(BF16)
                vb = vf[ksl, :].astype(BF16)
                dob = dof[qsl, :].astype(BF16)
                lse_b = lse_ref[qsl, :]
                dl_b = dlt[qsl, :]
                sc = _dot_nt(qb, kb) * SCALE
                pr = jnp.exp(sc - jnp.concatenate([lse_b, lse_b], axis=1))
                pr = jnp.where(_band_mask(first, has_prev), pr, 0.0)
                prb = pr.astype(BF16)
                dvf[ksl, :] += _dot_tn(prb, dob)
                dp = _dot_nt(dob, vb)
                ds = (pr * (dp - jnp.concatenate([dl_b, dl_b], axis=1)) * SCALE).astype(BF16)
                dqa[qsl, :] += _dot(ds, kb)
                dkf[ksl, :] += _dot_tn(ds, qb)
                return carry

            lax.fori_loop(0, nunit, unit, 0)

        dq_ref[...] = _rope_bwd(dqa[...], c, sa, sb).astype(BF16)
        dk_ref[...] = _rope_bwd(dkf[SPAN:], c, sa, sb).astype(BF16)
        dv_ref[...] = dvf[SPAN:].astype(BF16)

    rn = lambda n: ns - 1 - n
    pn = lambda n: jnp.maximum(ns - 2 - n, 0)
    blk = lambda piece, off: pl.BlockSpec((1, SPAN, DH), lambda hh, n: (piece, rn(n), off + hh))
    blkp = lambda piece, off: pl.BlockSpec((1, SPAN, DH), lambda hh, n: (piece, pn(n), off + hh))
    tab = pl.BlockSpec((SPAN, DH), lambda hh, n: (rn(n), 0))
    tabp = pl.BlockSpec((SPAN, DH), lambda hh, n: (pn(n), 0))
    io = pl.BlockSpec((SPAN, DH), lambda hh, n: (rn(n), hh))
    return pl.pallas_call(
        body, name="attn_bwd", grid=(H, ns),
        in_specs=[blk(1, 0), blk(1, H), blkp(1, H), blk(2, 0), blkp(2, 0), blk(2, H),
                  tab, tab, tab, tabp, tabp, tabp, io, io, io],
        out_specs=(io, io, io, io),
        out_shape=tuple(jax.ShapeDtypeStruct((s, D), BF16) for _ in range(4)),
        scratch_shapes=[pltpu.VMEM((SPAN, DH), F32), pltpu.VMEM((2 * SPAN, DH), F32), pltpu.VMEM((2 * SPAN, DH), F32),
                        pltpu.VMEM((SPAN, DH), F32), pltpu.VMEM((SPAN, DH), F32), pltpu.VMEM((SPAN, DH), F32),
                        pltpu.VMEM((2 * SPAN, DH), F32), pltpu.VMEM((2 * SPAN, DH), F32)],
        compiler_params=_params(("parallel", "arbitrary")),
    )(p, p, p, p, p, p, rc, rsa, rsb, rc, rsa, rsb, o, lse, dga)


def _tail(gr, ga, p, x, tgt, w3, b_gate, gate, g_final, tm=256):
    s = x.shape[0]
    nt = s // tm

    def body(gr_ref, ga_ref, pr_ref, pa_ref, x_ref, t_ref, bg_ref, gate_ref, gf_ref, w_hbm,
             dgr_ref, dga_ref, dc_ref, dx2_ref, vec_ref, go_hbm, w_s, acc_s, sem):
        i = pl.program_id(0)

        @pl.when(i == 0)
        def _():
            cp = pltpu.make_async_copy(w_hbm, w_s, sem.at[12])
            cp.start()
            acc_s[...] = jnp.zeros_like(acc_s)
            vec_ref[...] = jnp.zeros_like(vec_ref)
            cp.wait()

        grb = gr_ref[...]
        gab = ga_ref[...]
        bg = bg_ref[...]
        gate_v = gate_ref[...]
        gf = gf_ref[...]
        y_r = _dot(grb, w_s[0])
        y_a = _dot(gab, w_s[1])
        sr = _sigmoid(pr_ref[0] + bg[:, :D])
        sa = _sigmoid(pa_ref[0] + bg[:, D:])
        mb = (sr * y_r + sa * y_a).astype(BF16)
        u = _dot(mb, w_s[2])
        x2 = x_ref[...] + gate_v * u
        rstd = lax.rsqrt(jnp.mean(x2 * x2, axis=-1, keepdims=True) + EPS)
        xh = x2 * rstd
        e = xh * gf - t_ref[...]
        dy = e * (1.0 / D)
        dyg = dy * gf
        dx2 = rstd * (dyg - xh * jnp.mean(dyg * xh, axis=-1, keepdims=True))
        dx2_ref[...] = dx2
        dub = (dx2 * gate_v).astype(BF16)
        dm = _dot_nt(dub, w_s[2])
        dyr = (dm * sr).astype(BF16)
        dya = (dm * sa).astype(BF16)
        dpr = dm * y_r * (sr * (1.0 - sr))
        dpa = dm * y_a * (sa * (1.0 - sa))
        dc_ref[:, :D] = dpr.astype(BF16)
        dc_ref[:, D:] = dpa.astype(BF16)
        dgr_ref[...] = _dot_nt(dyr, w_s[0])
        dga_ref[...] = _dot_nt(dya, w_s[1])
        acc_s[0] += _dot_tn(grb, dyr)
        acc_s[1] += _dot_tn(gab, dya)
        acc_s[2] += _dot_tn(mb, dub)
        vec_ref[0:1, :] += _colsum(dy * xh)
        vec_ref[1:2, :] += _colsum(dx2 * u)
        vec_ref[2:3, :] += _colsum(dpr)
        vec_ref[3:4, :] += _colsum(dpa)
        vec_ref[4:5, :] += _colsum(e * e)

        @pl.when(i == nt - 1)
        def _():
            vec_ref[4:5, :] = jnp.broadcast_to(jnp.sum(vec_ref[4:5, :]) * (0.5 / D), (1, D))
            cps = []
            for w in range(3):
                for j in range(4):
                    cps.append(pltpu.make_async_copy(acc_s.at[w, pl.ds(256 * j, 256)],
                                                     go_hbm.at[j, pl.ds(256 * w, 256)], sem.at[4 * w + j]))
            for cp in cps:
                cp.start()
            for cp in cps:
                cp.wait()

    rowt = lambda i: (i, 0)
    row = lambda w: pl.BlockSpec((1, w), lambda i: (0, 0))
    any_ = pl.BlockSpec(memory_space=pl.ANY)
    return pl.pallas_call(
        body, name="tail", grid=(nt,),
        in_specs=[pl.BlockSpec((tm, D), rowt), pl.BlockSpec((tm, D), rowt),
                  pl.BlockSpec((1, tm, D), lambda i: (3, i, 0)), pl.BlockSpec((1, tm, D), lambda i: (3, i, 1)),
                  pl.BlockSpec((tm, D), rowt), pl.BlockSpec((tm, D), rowt),
                  row(2 * D), row(D), row(D), any_],
        out_specs=(pl.BlockSpec((tm, D), rowt), pl.BlockSpec((tm, D), rowt), pl.BlockSpec((tm, 2 * D), rowt),
                   pl.BlockSpec((tm, D), rowt), pl.BlockSpec((8, D), lambda i: (0, 0)), any_),
        out_shape=(jax.ShapeDtypeStruct((s, D), F32), jax.ShapeDtypeStruct((s, D), F32),
                   jax.ShapeDtypeStruct((s, 2 * D), BF16), jax.ShapeDtypeStruct((s, D), F32),
                   jax.ShapeDtypeStruct((8, D), F32), jax.ShapeDtypeStruct((4, 768, D), F32)),
        scratch_shapes=[pltpu.VMEM((3, D, D), BF16), pltpu.VMEM((3, D, D), F32), pltpu.SemaphoreType.DMA((13,))],
        compiler_params=_params(("arbitrary",)),
    )(gr, ga, p, p, x, tgt, b_gate, gate, g_final, w3)


def _pieces_steps(pieces):
    out, s0 = [], 0
    for a in pieces:
        n = a.shape[1] // D
        out.append((s0, n))
        s0 += n
    return out, s0


def _inproj_bwd_x(pieces, wg, x, dx2, gn, scale, tm=512):
    s = x.shape[0]
    steps, nk = _pieces_steps(pieces)
    npc = PW // D
    np_ = len(pieces)

    def body(*refs):
        d_refs = refs[:np_]
        w_ref, x_ref, dx2_ref, gn_ref, sc_ref, gx_ref, vec_ref, acc = refs[np_:]
        i, k = pl.program_id(0), pl.program_id(1)

        @pl.when(k == 0)
        def _():
            acc[...] = jnp.zeros_like(acc)

        @pl.when((i == 0) & (k == 0))
        def _():
            vec_ref[...] = jnp.zeros_like(vec_ref)

        for (s0, n), d_ref in zip(steps, d_refs):
            @pl.when((k >= s0) & (k < s0 + n))
            def _(d_ref=d_ref):
                acc[...] += _dot_nt(d_ref[...], w_ref[0])

        @pl.when(k == nk - 1)
        def _():
            dh = acc[...]
            xt = x_ref[...]
            rstd = lax.rsqrt(jnp.mean(xt * xt, axis=-1, keepdims=True) + EPS)
            xh = xt * rstd
            gn_v = gn_ref[...]
            sc1 = 1.0 + sc_ref[...]
            dhx = dh * xh
            vec_ref[0:1, :] += _colsum(dh)
            vec_ref[1:2, :] += _colsum(dhx) * gn_v
            vec_ref[2:3, :] += _colsum(dhx) * sc1
            dxh = dh * (gn_v * sc1)
            gx_ref[...] = rstd * (dxh - xh * jnp.mean(dxh * xh, axis=-1, keepdims=True)) + dx2_ref[...]

    def piece_spec(s0, n):
        return pl.BlockSpec((tm, D), lambda i, k: (i, jnp.clip(k - s0, 0, n - 1)))

    rowt = lambda i, k: (i, 0)
    row = pl.BlockSpec((1, D), lambda i, k: (0, 0))
    return pl.pallas_call(
        body, name="inproj_bwd_x", grid=(s // tm, nk),
        in_specs=[piece_spec(s0, n) for s0, n in steps] +
                 [pl.BlockSpec((1, D, D), lambda i, k: (k // npc, 0, k % npc)),
                  pl.BlockSpec((tm, D), rowt), pl.BlockSpec((tm, D), rowt), row, row],
        out_specs=(pl.BlockSpec((tm, D), rowt), pl.BlockSpec((8, D), lambda i, k: (0, 0))),
        out_shape=(jax.ShapeDtypeStruct((s, D), F32), jax.ShapeDtypeStruct((8, D), F32)),
        scratch_shapes=[pltpu.VMEM((tm, D), F32)],
        compiler_params=_params(("arbitrary", "arbitrary")),
    )(*pieces, wg, x, dx2, gn, scale)


def _inproj_bwd_w(pieces, hbf, tk=512):
    s = hbf.shape[0]
    steps, nk = _pieces_steps(pieces)
    npc = PW // D
    ns = s // tk
    np_ = len(pieces)

    def body(*refs):
        d_refs = refs[:np_]
        h_ref, g_ref = refs[np_:]
        cb, k = pl.program_id(0), pl.program_id(1)

        @pl.when(k == 0)
        def _():
            g_ref[...] = jnp.zeros_like(g_ref)

        for (s0, n), d_ref in zip(steps, d_refs):
            @pl.when((cb >= s0) & (cb < s0 + n))
            def _(d_ref=d_ref):
                g_ref[0] += _dot_tn(h_ref[...], d_ref[...])

    def piece_spec(s0, n):
        def imap(cb, k):
            active = (cb >= s0) & (cb < s0 + n)
            return (jnp.where(active, k, 0), jnp.clip(cb - s0, 0, n - 1))
        return pl.BlockSpec((tk, D), imap)

    return pl.pallas_call(
        body, name="inproj_bwd_w", grid=(nk, ns),
        in_specs=[piece_spec(s0, n) for s0, n in steps] + [pl.BlockSpec((tk, D), lambda cb, k: (k, 0))],
        out_specs=pl.BlockSpec((1, D, D), lambda cb, k: (cb // npc, 0, cb % npc)),
        out_shape=jax.ShapeDtypeStruct((4, D, PW), F32),
        compiler_params=_params(("parallel", "arbitrary")),
    )(*pieces, hbf)


D2D_CHUNK_BYTES = 512 * 1024


def _chunk_rows(a):
    return max(8, D2D_CHUNK_BYTES // (a.shape[-1] * a.dtype.itemsize))


def _pair_exchange(arrs):
    na = len(arrs)
    chunks = []
    for t, a in enumerate(arrs):
        hr = a.shape[1] // 2
        cr = _chunk_rows(a)
        chunks += [(t, j, r0, cr) for j in range(a.shape[0]) for r0 in range(0, hr, cr)]
    nch = len(chunks)

    def body(*refs):
        a_refs = refs[:na]
        own_refs = refs[na:2 * na]
        rb_refs = refs[2 * na:3 * na]
        lsem, ss, rs = refs[3 * na:]
        x, y, c = _coords()
        sib = (x, y, 1 - c)
        lcs, rcs = [], []
        for t, (a, own) in enumerate(zip(a_refs, own_refs)):
            hr = a.shape[1] // 2
            lc = pltpu.make_async_copy(a.at[:, pl.ds(c * hr, hr), :], own, lsem.at[t])
            lc.start()
            lcs.append(lc)
        for n, (t, j, r0, cr) in enumerate(chunks):
            hr = a_refs[t].shape[1] // 2
            rc = _rcopy(a_refs[t].at[j, pl.ds((1 - c) * hr + r0, cr), :], rb_refs[t].at[j, pl.ds(r0, cr), :],
                        ss.at[n], rs.at[n], sib)
            rc.start()
            rcs.append(rc)
        for rc in rcs:
            rc.wait_recv()
        for rc in rcs:
            rc.wait_send()
        for lc in lcs:
            lc.wait()

    any_ = pl.BlockSpec(memory_space=pl.ANY)
    halves = [jax.ShapeDtypeStruct((a.shape[0], a.shape[1] // 2, a.shape[2]), a.dtype) for a in arrs]
    outs = pl.pallas_call(
        body, name="pair_exchange",
        out_shape=tuple(halves + halves),
        in_specs=[any_] * na, out_specs=tuple([any_] * (2 * na)),
        scratch_shapes=[pltpu.SemaphoreType.DMA((na,)), pltpu.SemaphoreType.DMA((nch,)), pltpu.SemaphoreType.DMA((nch,))],
        compiler_params=_params(),
    )(*arrs)
    return outs[:na], outs[na:]


def _chip_scatter(arrs):
    na = len(arrs)

    def body(*refs):
        a_refs = refs[:na]
        o_refs = refs[na:2 * na]
        lsem, ss, rs = refs[2 * na:]
        x, y, c = _coords()
        j = 2 * x + y
        chips = ((1, 0), (0, 1), (1, 1))
        cps = []
        for t, (a, o) in enumerate(zip(a_refs, o_refs)):
            lc = pltpu.make_async_copy(a.at[j], o.at[3], lsem.at[t])
            lc.start()
            cps.append(lc)
            for q, (kx, ky) in enumerate(chips):
                jt = j ^ (2 * kx + ky)
                rc = _rcopy(a.at[jt], o.at[q], ss.at[3 * t + q], rs.at[3 * t + q], (_flip(x, kx), _flip(y, ky), c))
                rc.start()
                cps.append(rc)
        for cp in cps:
            cp.wait()

    any_ = pl.BlockSpec(memory_space=pl.ANY)
    return pl.pallas_call(
        body, name="chip_scatter",
        out_shape=tuple(jax.ShapeDtypeStruct(a.shape, a.dtype) for a in arrs),
        in_specs=[any_] * na, out_specs=tuple([any_] * na),
        scratch_shapes=[pltpu.SemaphoreType.DMA((na,)), pltpu.SemaphoreType.DMA((3 * na,)),
                        pltpu.SemaphoreType.DMA((3 * na,))],
        compiler_params=_params(),
    )(*arrs)


def _pair_allgather(arrs):
    na = len(arrs)
    chunks = []
    for t, a in enumerate(arrs):
        cr = _chunk_rows(a)
        chunks += [(t, r0, cr) for r0 in range(0, a.shape[0], cr)]
    nch = len(chunks)

    def body(*refs):
        a_refs = refs[:na]
        o_refs = refs[na:2 * na]
        lsem, ss, rs = refs[2 * na:]
        x, y, c = _coords()
        sib = (x, y, 1 - c)
        lcs, rcs = [], []
        for t, (a, o) in enumerate(zip(a_refs, o_refs)):
            lc = pltpu.make_async_copy(a, o.at[c], lsem.at[t])
            lc.start()
            lcs.append(lc)
        for n, (t, r0, cr) in enumerate(chunks):
            rows = pl.ds(r0, cr)
            rc = _rcopy(a_refs[t].at[rows, :], o_refs[t].at[c, rows, :], ss.at[n], rs.at[n], sib)
            rc.start()
            rcs.append(rc)
        for n, (t, r0, cr) in enumerate(chunks):
            rows = pl.ds(r0, cr)
            _rcopy(a_refs[t].at[rows, :], o_refs[t].at[1 - c, rows, :], ss.at[n], rs.at[n], sib).wait_recv()
        for rc in rcs:
            rc.wait_send()
        for lc in lcs:
            lc.wait()

    any_ = pl.BlockSpec(memory_space=pl.ANY)
    return pl.pallas_call(
        body, name="pair_allgather",
        out_shape=tuple(jax.ShapeDtypeStruct((2,) + a.shape, a.dtype) for a in arrs),
        in_specs=[any_] * na, out_specs=tuple([any_] * na),
        scratch_shapes=[pltpu.SemaphoreType.DMA((na,)), pltpu.SemaphoreType.DMA((nch,)), pltpu.SemaphoreType.DMA((nch,))],
        compiler_params=_params(),
    )(*arrs)


def _add2(a, b, tr=256):
    n, r, cdim = a.shape

    def body(a_ref, b_ref, o_ref):
        o_ref[...] = a_ref[...] + b_ref[...]

    spec = pl.BlockSpec((1, tr, cdim), lambda i, j: (i, j, 0))
    return pl.pallas_call(
        body, name="add2", grid=(n, r // tr), in_specs=[spec, spec], out_specs=spec,
        out_shape=jax.ShapeDtypeStruct(a.shape, a.dtype),
        compiler_params=_params(("parallel", "parallel")),
    )(a, b)


def _sum_slots(a, tr=256):
    _, r, cdim = a.shape

    def body(a_ref, o_ref):
        o_ref[...] = ((a_ref[3] + a_ref[0]) + a_ref[1]) + a_ref[2]

    return pl.pallas_call(
        body, name="sum_slots", grid=(r // tr,),
        in_specs=[pl.BlockSpec((4, tr, cdim), lambda i: (0, i, 0))],
        out_specs=pl.BlockSpec((tr, cdim), lambda i: (i, 0)),
        out_shape=jax.ShapeDtypeStruct((r, cdim), a.dtype),
        compiler_params=_params(("parallel",)),
    )(a)


def _allreduce_small(pack):
    def body(p_ref, out_ref, rbuf, s1, r1, s2, r2):
        me = _my_index()
        chunk = lambda d: pl.ds(pl.multiple_of(d * AR_CHUNK, 8), AR_CHUNK)
        sends = []
        for k in range(1, NDEV):
            cp = _rcopy(p_ref.at[chunk(me ^ k)], rbuf.at[me], s1.at[k - 1], r1.at[k - 1], _peer(k))
            cp.start()
            sends.append(cp)
        rbuf[me] = p_ref[chunk(me), :]
        for k in range(1, NDEV):
            _rcopy(p_ref.at[chunk(me)], rbuf.at[me ^ k], s1.at[k - 1], r1.at[k - 1], _peer(k)).wait_recv()
        tot = rbuf[0]
        for d in range(1, NDEV):
            tot = tot + rbuf[d]
        out_ref[chunk(me), :] = tot
        for k in range(1, NDEV):
            cp = _rcopy(out_ref.at[chunk(me)], out_ref.at[chunk(me)], s2.at[k - 1], r2.at[k - 1], _peer(k))
            cp.start()
            sends.append(cp)
        for k in range(1, NDEV):
            _rcopy(out_ref.at[chunk(me)], out_ref.at[chunk(me ^ k)], s2.at[k - 1], r2.at[k - 1], _peer(k)).wait_recv()
        for cp in sends:
            cp.wait_send()

    vm = pl.BlockSpec(memory_space=pltpu.VMEM)
    return pl.pallas_call(
        body, name="allreduce_small",
        out_shape=jax.ShapeDtypeStruct((AR_ROWS, D), F32),
        in_specs=[vm], out_specs=vm,
        scratch_shapes=[pltpu.VMEM((NDEV, AR_CHUNK, D), F32),
                        pltpu.SemaphoreType.DMA((7,)), pltpu.SemaphoreType.DMA((7,)),
                        pltpu.SemaphoreType.DMA((7,)), pltpu.SemaphoreType.DMA((7,))],
        compiler_params=_params(),
    )(pack)


def _adamw(w, g, m, v, tr):
    r, cdim = w.shape

    def body(w_ref, g_ref, m_ref, v_ref, d_ref, nm_ref, nv_ref):
        gv = g_ref[...]
        nm = B1 * m_ref[...] + (1.0 - B1) * gv
        nv = B2 * v_ref[...] + (1.0 - B2) * (gv * gv)
        m_hat = nm / (1.0 - B1 ** STEP)
        v_hat = nv / (1.0 - B2 ** STEP)
        d_ref[...] = -LR * (m_hat / (jnp.sqrt(v_hat) + ADAM_EPS) + WD * w_ref[...])
        nm_ref[...] = nm
        nv_ref[...] = nv

    spec = pl.BlockSpec((tr, cdim), lambda i: (i, 0))
    sd = jax.ShapeDtypeStruct((r, cdim), F32)
    return pl.pallas_call(
        body, name="adamw", grid=(r // tr,), in_specs=[spec] * 4, out_specs=(spec,) * 3, out_shape=(sd,) * 3,
        compiler_params=_params(("parallel",)),
    )(w, g, m, v)


def _rope_tables(positions):
    inv_freq = ROPE_THETA ** (-jnp.arange(0, ROT, 2, dtype=F32) / ROT)
    ang = positions.astype(F32)[:, None] * inv_freq
    cos, sin = jnp.cos(ang), jnp.sin(ang)
    n = positions.shape[0]
    half = ROT // 2
    rc = jnp.concatenate([cos, cos, jnp.ones((n, DH - ROT), F32)], axis=1)
    rsa = jnp.concatenate([-sin, jnp.zeros((n, DH - half), F32)], axis=1)
    rsb = jnp.concatenate([jnp.zeros((n, half), F32), sin, jnp.zeros((n, DH - ROT), F32)], axis=1)
    return rc, rsa, rsb


def kernel(x, c, positions, g_norm, w_mod, b_mod, w_in, b_gate, conv_w, conv_b, w_a, b_a, w_x, b_x, lam, w_out_rnn, w_out_attn, w_o, g_final, loss_target, m_g_norm, m_w_mod, m_b_mod, m_w_in, m_b_gate, m_conv_w, m_conv_b, m_w_a, m_b_a, m_w_x, m_b_x, m_lam, m_w_out_rnn, m_w_out_attn, m_w_o, m_g_final, v_g_norm, v_w_mod, v_b_mod, v_w_in, v_b_gate, v_conv_w, v_conv_b, v_w_a, v_b_a, v_w_x, v_b_x, v_lam, v_w_out_rnn, v_w_out_attn, v_w_o, v_g_final):
    s = x.shape[1]
    xi = lax.axis_index("x")
    yi = lax.axis_index("y")
    shard = 2 * xi + yi
    x2d = x[0]
    tgt = loss_target[0]
    pos = positions[0]

    c_all, mod4 = _mod_fwd(c, w_mod[0], b_mod.reshape(4, 1, 768))
    mod = mod4.reshape(1, 3 * D)
    shift, scale, gate = mod[:, :D], mod[:, D:2 * D], mod[:, 2 * D:]
    w3_sh = jnp.stack([w_out_rnn[0], w_out_attn[0], w_o[0]]).astype(BF16)
    wsh = jnp.concatenate([w_in[0].astype(BF16), w3_sh.reshape(384, PW)], axis=0)
    wg = _gather_weights(wsh)
    w3 = wg[:, D:, :].reshape(4, 3, 256, D).transpose(1, 0, 2, 3).reshape(3, D, D)
    conv_all = _gather_small(conv_w[0])
    conv_full = conv_all[0::2].transpose(1, 0, 2).reshape(4, D)

    p, hbf = _norm_inproj(x2d, g_norm, shift, scale, wg)
    rc, rsa, rsb = _rope_tables(pos)
    pos_col = pos.reshape(s, 1)
    b_a3, b_x3 = b_a.reshape(H, 1, DH), b_x.reshape(H, 1, DH)
    hr, gr = _rnn_fwd(p, pos_col, conv_full, conv_b, w_a[0], b_a3, w_x[0], b_x3, lam)
    o, lse, ga = _attn_fwd(p, rc, rsa, rsb)

    dgr, dga, dc, dx2, vec_t, g_out = _tail(gr, ga, p, x2d, tgt, w3, b_gate, gate, g_final.reshape(1, D))

    dxr, dzr, g_wa, g_ba, g_wx, g_bx, g_lam, g_cw, g_cb = _rnn_bwd(
        p, hr, dgr, pos_col, conv_full, conv_b, w_a[0], b_a3, w_x[0], b_x3, lam)
    dq, dk, dv, dza = _attn_bwd(p, o, lse, dga, rc, rsa, rsb)

    pieces = [dxr, dzr, dq, dk, dv, dza, dc]
    grad_x, vec_n = _inproj_bwd_x(pieces, wg, x2d, dx2, g_norm, scale)
    g_win = _inproj_bwd_w(pieces, hbf)

    (own_a, own_b), (rb_a, rb_b) = _pair_exchange([g_win, g_out])
    q_a, q_b = _add2(own_a, rb_a), _add2(own_b, rb_b, tr=128)
    r_a, r_b = _chip_scatter([q_a, q_b])
    f_a, f_b = _sum_slots(r_a), _sum_slots(r_b, tr=128)
    full_a, full_b = _pair_allgather([f_a, f_b])
    grad_w_in = full_a.reshape(D, PW)
    g3 = full_b.reshape(3, 256, D)

    dmod_row = jnp.concatenate([vec_n[0:1], vec_n[1:2], vec_t[1:2]], axis=1)
    pack = jnp.concatenate([
        vec_n[2:3],
        dmod_row.reshape(3, D),
        vec_t[2:4],
        g_cb.reshape(1, D),
        g_wa.reshape(128, D),
        g_ba.reshape(1, D),
        g_wx.reshape(128, D),
        g_bx.reshape(1, D),
        g_lam.reshape(1, D),
        vec_t[0:1],
        g_cw.transpose(1, 0, 2).reshape(4, D),
        vec_t[4:5],
        jnp.zeros((AR_ROWS - 272, D), F32)], axis=0)
    red = _allreduce_small(pack)
    loss = red[271, 0]
    grad_w_mod = _mod_bwd(dmod_row.reshape(4, 1, 768), c_all)
    g_conv_sh = lax.dynamic_slice_in_dim(red[267:271], shard * 256, 256, axis=1)

    def small_pack(g_norm_, b_mod_, b_gate_, conv_b_, w_a_, b_a_, w_x_, b_x_, lam_, g_final_, conv_w_):
        return jnp.concatenate([
            g_norm_.reshape(1, D), b_mod_.reshape(3, D), b_gate_.reshape(2, D), conv_b_.reshape(1, D),
            w_a_.reshape(128, D), b_a_.reshape(1, D), w_x_.reshape(128, D), b_x_.reshape(1, D),
            lam_.reshape(1, D), g_final_.reshape(1, D), conv_w_.reshape(1, D),
            jnp.zeros((4, D), F32)], axis=0)

    wp = small_pack(g_norm, b_mod, b_gate, conv_b, w_a, b_a, w_x, b_x, lam, g_final, conv_w)
    mp = small_pack(m_g_norm, m_b_mod, m_b_gate, m_conv_b, m_w_a, m_b_a, m_w_x, m_b_x, m_lam, m_g_final, m_conv_w)
    vp = small_pack(v_g_norm, v_b_mod, v_b_gate, v_conv_b, v_w_a, v_b_a, v_w_x, v_b_x, v_lam, v_g_final, v_conv_w)
    gp = jnp.concatenate([red[0:267], g_conv_sh.reshape(1, D), jnp.zeros((4, D), F32)], axis=0)
    small = _adamw(wp, gp, mp, vp, tr=136)

    def unpack(a):
        return dict(
            g_norm=a[0:1], b_mod=a[1:4].reshape(1, 3 * D), b_gate=a[4:6].reshape(1, 2 * D), conv_b=a[6:7],
            w_a=a[7:135].reshape(1, H, DH, DH), b_a=a[135:136].reshape(1, H, DH),
            w_x=a[136:264].reshape(1, H, DH, DH), b_x=a[264:265].reshape(1, H, DH), lam=a[265:266],
            g_final=a[266].reshape(D), conv_w=a[267:268].reshape(1, 4, 256))

    big_in = _adamw(w_in[0], grad_w_in, m_w_in[0], v_w_in[0], tr=256)
    big_mod = _adamw(w_mod[0], grad_w_mod, m_w_mod[0], v_w_mod[0], tr=256)
    w3f = jnp.concatenate([w_out_rnn[0], w_out_attn[0], w_o[0]], axis=0)
    m3f = jnp.concatenate([m_w_out_rnn[0], m_w_out_attn[0], m_w_o[0]], axis=0)
    v3f = jnp.concatenate([v_w_out_rnn[0], v_w_out_attn[0], v_w_o[0]], axis=0)
    big_out = _adamw(w3f, g3.reshape(768, D), m3f, v3f, tr=256)

    names = ["g_norm", "w_mod", "b_mod", "w_in", "b_gate", "conv_w", "conv_b", "w_a", "b_a", "w_x", "b_x", "lam",
             "w_out_rnn", "w_out_attn", "w_o", "g_final"]
    grads = unpack(gp)
    grads.update(w_mod=grad_w_mod[None], w_in=grad_w_in[None],
                 w_out_rnn=g3[0][None], w_out_attn=g3[1][None], w_o=g3[2][None])
    outs = [grads]
    for idx in range(3):
        d = unpack(small[idx])
        d.update(w_mod=big_mod[idx][None], w_in=big_in[idx][None],
                 w_out_rnn=big_out[idx][0:256][None], w_out_attn=big_out[idx][256:512][None],
                 w_o=big_out[idx][512:768][None])
        outs.append(d)
    flat = [d[n] for d in outs for n in names]
    return (loss, grad_x[None], *flat)
```

```python
import jax
import jax.numpy as jnp
from jax import lax
from jax.experimental import pallas as pl
from jax.experimental.pallas import tpu as pltpu

F32, BF16 = jnp.float32, jnp.bfloat16
MESH = pl.DeviceIdType.MESH
HIGHEST = lax.Precision.HIGHEST

D = 1024
H = 8
DH = 128
PW = 2048
EPS = 1e-6
LRU_C = 8.0
SCALE = DH ** -0.5
NEG = -1e30
SPAN = 2048
UB = 128
DILATIONS = (1, 4, 16)
ROPE_THETA = 500000.0
ROT = 32

LR, B1, B2, ADAM_EPS, WD, STEP = 0.001, 0.9, 0.999, 1e-08, 0.01, 10

WROWS = 1024 + 384
NDEV = 8
AR_ROWS = 320
AR_CHUNK = AR_ROWS // NDEV


def _params(sem=None, vmem_mb=56):
    return pltpu.CompilerParams(dimension_semantics=sem, vmem_limit_bytes=vmem_mb * 2 ** 20)


def _coords():
    return lax.axis_index("x"), lax.axis_index("y"), lax.axis_index("c")


def _flip(v, bit):
    return 1 - v if bit else v


def _peer(k):
    x, y, c = _coords()
    return (_flip(x, (k >> 2) & 1), _flip(y, (k >> 1) & 1), _flip(c, k & 1))


def _my_index():
    x, y, c = _coords()
    return 4 * x + 2 * y + c


def _rcopy(src, dst, ssem, rsem, dev):
    return pltpu.make_async_remote_copy(src_ref=src, dst_ref=dst, send_sem=ssem, recv_sem=rsem,
                                        device_id=dev, device_id_type=MESH)


def _sigmoid(x):
    return jax.nn.sigmoid(x)


def _dot(a, b):
    return jnp.dot(a, b, preferred_element_type=F32)


def _dot_nt(a, b):
    return lax.dot_general(a, b, (((1,), (1,)), ((), ())), preferred_element_type=F32)


def _dot_tn(a, b):
    return lax.dot_general(a, b, (((0,), (0,)), ((), ())), preferred_element_type=F32)


def _colsum(a):
    return jnp.sum(a, axis=0, keepdims=True)


def _mod_fwd(c, w_mod_sh, b_mod4):
    def body(c_ref, w_ref, b_ref, call_ref, mod_ref, rows_ref, cmat_ref, s1, r1, s2, r2):
        x, y, _ = _coords()
        me = _my_index()
        j = 2 * x + y
        call_ref[me] = c_ref[...]
        sends = []
        for k in range(1, NDEV):
            cp = _rcopy(call_ref.at[me], call_ref.at[me], s1.at[k - 1], r1.at[k - 1], _peer(k))
            cp.start()
            sends.append(cp)
        for k in range(1, NDEV):
            pk = me ^ k
            _rcopy(call_ref.at[pk], call_ref.at[pk], s1.at[k - 1], r1.at[k - 1], _peer(k)).wait_recv()
        for b in range(NDEV):
            cmat_ref[pl.ds(b, 1), :] = call_ref[b]
        cm = cmat_ref[...]
        act = cm * _sigmoid(cm)
        mp = jnp.dot(act, w_ref[...], preferred_element_type=F32, precision=HIGHEST) + b_ref[j]
        for b in range(NDEV):
            rows_ref[b] = mp[b:b + 1]
        mod_ref[j] = rows_ref[me]
        for q, k in enumerate((2, 4, 6)):
            cp = _rcopy(rows_ref.at[me ^ k], mod_ref.at[j], s2.at[q], r2.at[q], _peer(k))
            cp.start()
            sends.append(cp)
        for q, k in enumerate((2, 4, 6)):
            jq = j ^ (k >> 1)
            _rcopy(rows_ref.at[me], mod_ref.at[jq], s2.at[q], r2.at[q], _peer(k)).wait_recv()
        for cp in sends:
            cp.wait_send()

    vm = pl.BlockSpec(memory_space=pltpu.VMEM)
    return pl.pallas_call(
        body, name="mod_fwd",
        out_shape=(jax.ShapeDtypeStruct((NDEV, 1, D), F32), jax.ShapeDtypeStruct((4, 1, 768), F32)),
        in_specs=[vm, vm, vm], out_specs=(vm, vm),
        scratch_shapes=[pltpu.VMEM((NDEV, 1, 768), F32), pltpu.VMEM((NDEV, D), F32),
                        pltpu.SemaphoreType.DMA((7,)), pltpu.SemaphoreType.DMA((7,)),
                        pltpu.SemaphoreType.DMA((3,)), pltpu.SemaphoreType.DMA((3,))],
        compiler_params=_params(),
    )(c, w_mod_sh, b_mod4)


def _mod_bwd(dmod4, c_all):
    def body(d_ref, call_ref, gw_ref, dall_ref, cmat_ref, dmat_ref, s1, r1):
        x, y, _ = _coords()
        me = _my_index()
        j = 2 * x + y
        dall_ref[me] = d_ref[...]
        sends = []
        for k in range(1, NDEV):
            cp = _rcopy(dall_ref.at[me], dall_ref.at[me], s1.at[k - 1], r1.at[k - 1], _peer(k))
            cp.start()
            sends.append(cp)
        for k in range(1, NDEV):
            pk = me ^ k
            _rcopy(dall_ref.at[pk], dall_ref.at[pk], s1.at[k - 1], r1.at[k - 1], _peer(k)).wait_recv()
        for cp in sends:
            cp.wait_send()
        for b in range(NDEV):
            cmat_ref[pl.ds(b, 1), :] = call_ref[b]
            dmat_ref[pl.ds(b, 1), :] = dall_ref[b, j]
        cm = cmat_ref[...]
        act = cm * _sigmoid(cm)
        gw_ref[...] = lax.dot_general(act, dmat_ref[...], (((0,), (0,)), ((), ())),
                                      preferred_element_type=F32, precision=HIGHEST)

    vm = pl.BlockSpec(memory_space=pltpu.VMEM)
    return pl.pallas_call(
        body, name="mod_bwd",
        out_shape=jax.ShapeDtypeStruct((D, 768), F32),
        in_specs=[vm, vm], out_specs=vm,
        scratch_shapes=[pltpu.VMEM((NDEV, 4, 1, 768), F32), pltpu.VMEM((NDEV, D), F32), pltpu.VMEM((NDEV, 768), F32),
                        pltpu.SemaphoreType.DMA((7,)), pltpu.SemaphoreType.DMA((7,))],
        compiler_params=_params(),
    )(dmod4, c_all)


def _gather_weights(wsh):
    rows, cols = wsh.shape
    half = rows // 2
    nch = 4
    cr = half // nch

    def body(w_ref, g_ref, ss, rs):
        x, y, c = _coords()
        j = 2 * x + y
        sib = (x, y, 1 - c)
        sends = []
        chips = ((1, 0), (0, 1), (1, 1))
        mine = lambda n: pl.ds(c * half + n * cr, cr)
        theirs = lambda n: pl.ds((1 - c) * half + n * cr, cr)
        for n in range(nch):
            for q, (kx, ky) in enumerate(chips):
                e = nch * q + n
                cp = _rcopy(w_ref.at[mine(n)], g_ref.at[j, mine(n)], ss.at[e], rs.at[e],
                            (_flip(x, kx), _flip(y, ky), c))
                cp.start()
                sends.append(cp)
        for n in range(nch):
            for q, (kx, ky) in enumerate(chips):
                jq = j ^ (2 * kx + ky)
                e = nch * q + n
                _rcopy(w_ref.at[mine(n)], g_ref.at[jq, mine(n)], ss.at[e], rs.at[e], sib).wait_recv()
                cp = _rcopy(g_ref.at[jq, mine(n)], g_ref.at[jq, mine(n)], ss.at[3 * nch + e], rs.at[3 * nch + e], sib)
                cp.start()
                sends.append(cp)
        for n in range(nch):
            for q, (kx, ky) in enumerate(chips):
                jq = j ^ (2 * kx + ky)
                e = nch * q + n
                _rcopy(w_ref.at[mine(n)], g_ref.at[jq, theirs(n)], ss.at[3 * nch + e], rs.at[3 * nch + e], sib).wait_recv()
        for cp in sends:
            cp.wait_send()

    any_ = pl.BlockSpec(memory_space=pl.ANY)
    return pl.pallas_call(
        body, name="gather_weights",
        out_shape=jax.ShapeDtypeStruct((4, rows, cols), wsh.dtype),
        in_specs=[any_], out_specs=any_,
        scratch_shapes=[pltpu.SemaphoreType.DMA((6 * nch,)), pltpu.SemaphoreType.DMA((6 * nch,))],
        compiler_params=_params(),
    )(wsh)


def _gather_small(v):
    r, cdim = v.shape

    def body(v_ref, out_ref, ss, rs):
        me = _my_index()
        out_ref[me] = v_ref[...]
        sends = []
        for k in range(1, NDEV):
            cp = _rcopy(out_ref.at[me], out_ref.at[me], ss.at[k - 1], rs.at[k - 1], _peer(k))
            cp.start()
            sends.append(cp)
        for k in range(1, NDEV):
            pk = me ^ k
            _rcopy(out_ref.at[pk], out_ref.at[pk], ss.at[k - 1], rs.at[k - 1], _peer(k)).wait_recv()
        for cp in sends:
            cp.wait_send()

    vm = pl.BlockSpec(memory_space=pltpu.VMEM)
    return pl.pallas_call(
        body, name="gather_small",
        out_shape=jax.ShapeDtypeStruct((NDEV, r, cdim), v.dtype),
        in_specs=[vm], out_specs=vm,
        scratch_shapes=[pltpu.SemaphoreType.DMA((7,)), pltpu.SemaphoreType.DMA((7,))],
        compiler_params=_params(),
    )(v)


def _norm_inproj(x, gn, shift, scale, wg, tm=1024, tn=512):
    s = x.shape[0]
    npc = PW // tn

    def body(x_ref, gn_ref, sh_ref, sc_ref, w_ref, p_ref, h_ref, hs):
        @pl.when(pl.program_id(1) == 0)
        def _():
            xt = x_ref[...]
            rstd = lax.rsqrt(jnp.mean(xt * xt, axis=-1, keepdims=True) + EPS)
            h = (xt * rstd * gn_ref[...]) * (1.0 + sc_ref[...]) + sh_ref[...]
            hs[...] = h.astype(BF16)
            h_ref[...] = hs[...]

        p_ref[0] = _dot(hs[...], w_ref[0])

    row = pl.BlockSpec((1, D), lambda i, j: (0, 0))
    return pl.pallas_call(
        body, name="norm_inproj", grid=(s // tm, 4 * npc),
        in_specs=[pl.BlockSpec((tm, D), lambda i, j: (i, 0)), row, row, row,
                  pl.BlockSpec((1, D, tn), lambda i, j: (j // npc, 0, j % npc))],
        out_specs=(pl.BlockSpec((1, tm, tn), lambda i, j: (j // npc, i, j % npc)),
                   pl.BlockSpec((tm, D), lambda i, j: (i, 0))),
        out_shape=(jax.ShapeDtypeStruct((4, s, PW), F32), jax.ShapeDtypeStruct((s, D), BF16)),
        scratch_shapes=[pltpu.VMEM((tm, D), BF16)],
        compiler_params=_params(("parallel", "arbitrary")),
    )(x, gn, shift, scale, wg)


def _shift_down(prev8, cur, d):
    t = cur.shape[0]
    ext = jnp.concatenate([prev8, cur], axis=0)
    return pltpu.roll(ext, d, 0)[8:]


def _shift_up(cur, next8, d):
    t = cur.shape[0]
    ext = jnp.concatenate([cur, next8], axis=0)
    return pltpu.roll(ext, t + 8 - d, 0)[:t]


def _rnn_gates(xr, prev8, cw, cb, wa, ba, wx, bx, lam, reset):
    xc = cw[3:4] * xr + cb
    for d in (1, 2, 3):
        xc = xc + cw[3 - d:4 - d] * _shift_down(prev8, xr, d)
    xcb = xc.astype(BF16)
    r = _sigmoid(_dot(xcb, wa.astype(BF16)) + ba)
    ig = _sigmoid(_dot(xcb, wx.astype(BF16)) + bx)
    nl = -lam
    sp = jnp.maximum(nl, 0.0) + jnp.log1p(jnp.exp(-jnp.abs(nl)))
    log_a = (-LRU_C * r) * sp
    a = jnp.where(reset, 0.0, jnp.exp(log_a))
    mult = jnp.where(reset, 1.0, jnp.sqrt(1.0 - jnp.exp(2.0 * log_a)))
    return xc, r, ig, sp, a, mult


def _scan_down(a, b, t):
    rows = lax.broadcasted_iota(jnp.int32, a.shape, 0)
    d = 1
    while d < t:
        m = rows >= d
        a_s = pltpu.roll(a, d, 0)
        b_s = pltpu.roll(b, d, 0)
        b = jnp.where(m, a * b_s + b, b)
        a = jnp.where(m, a * a_s, a)
        d *= 2
    return a, b


def _scan_up(a, b, t):
    rows = lax.broadcasted_iota(jnp.int32, a.shape, 0)
    d = 1
    while d < t:
        m = rows < t - d
        a_s = pltpu.roll(a, t - d, 0)
        b_s = pltpu.roll(b, t - d, 0)
        b = jnp.where(m, a * b_s + b, b)
        a = jnp.where(m, a * a_s, a)
        d *= 2
    return a, b


def _rnn_fwd(p, pos, conv_w, conv_b, w_a, b_a, w_x, b_x, lam, tt=256):
    s = p.shape[1]
    nt = s // tt

    def body(xr_ref, z_ref, pos_ref, cw_ref, cb_ref, wa_ref, ba_ref, wx_ref, bx_ref, lam_ref,
             hr_ref, gr_ref, xprev, hprev):
        @pl.when(pl.program_id(1) == 0)
        def _():
            xprev[...] = jnp.zeros_like(xprev)
            hprev[...] = jnp.zeros_like(hprev)

        xr = xr_ref[0]
        z = z_ref[0]
        reset = pos_ref[...] == 0
        xc, r, ig, sp, a, mult = _rnn_gates(xr, xprev[...], cw_ref[...], cb_ref[...], wa_ref[0], ba_ref[0],
                                            wx_ref[0], bx_ref[0], lam_ref[...], reset)
        bx = mult * ig * xc
        a_cum, h0 = _scan_down(a, bx, tt)
        h = a_cum * hprev[0:1] + h0
        xprev[...] = xr[tt - 8:]
        hprev[...] = jnp.broadcast_to(h[tt - 1:tt], (8, DH))
        hr_ref[...] = h
        gr_ref[...] = (h * (z * _sigmoid(z))).astype(BF16)

    head_row = lambda hh, t: (0, hh)
    return pl.pallas_call(
        body, name="rnn_fwd", grid=(H, nt),
        in_specs=[pl.BlockSpec((1, tt, DH), lambda hh, t: (0, t, hh)),
                  pl.BlockSpec((1, tt, DH), lambda hh, t: (0, t, H + hh)),
                  pl.BlockSpec((tt, 1), lambda hh, t: (t, 0)),
                  pl.BlockSpec((4, DH), head_row), pl.BlockSpec((1, DH), head_row),
                  pl.BlockSpec((1, DH, DH), lambda hh, t: (hh, 0, 0)), pl.BlockSpec((1, 1, DH), lambda hh, t: (hh, 0, 0)),
                  pl.BlockSpec((1, DH, DH), lambda hh, t: (hh, 0, 0)), pl.BlockSpec((1, 1, DH), lambda hh, t: (hh, 0, 0)),
                  pl.BlockSpec((1, DH), head_row)],
        out_specs=(pl.BlockSpec((tt, DH), lambda hh, t: (t, hh)), pl.BlockSpec((tt, DH), lambda hh, t: (t, hh))),
        out_shape=(jax.ShapeDtypeStruct((s, D), F32), jax.ShapeDtypeStruct((s, D), BF16)),
        scratch_shapes=[pltpu.VMEM((8, DH), F32), pltpu.VMEM((8, DH), F32)],
        compiler_params=_params(("parallel", "arbitrary")),
    )(p, p, pos, conv_w, conv_b, w_a, b_a, w_x, b_x, lam)


def _rnn_bwd(p, hr, dgr, pos, conv_w, conv_b, w_a, b_a, w_x, b_x, lam, tt=256):
    s = p.shape[1]
    nt = s // tt
    t8 = tt // 8

    def body(xr_ref, z_ref, xp_ref, hr_ref, hp_ref, dg_ref, pos_ref, cw_ref, cb_ref, wa_ref, ba_ref, wx_ref, bx_ref,
             lam_ref, dxr_ref, dz_ref, gwa_ref, gba_ref, gwx_ref, gbx_ref, glam_ref, gcw_ref, gcb_ref,
             a_next, g_next, dxc_next):
        t = pl.program_id(1)
        has_prev = t < nt - 1

        @pl.when(t == 0)
        def _():
            a_next[...] = jnp.zeros_like(a_next)
            g_next[...] = jnp.zeros_like(g_next)
            dxc_next[...] = jnp.zeros_like(dxc_next)
            gwa_ref[...] = jnp.zeros_like(gwa_ref)
            gba_ref[...] = jnp.zeros_like(gba_ref)
            gwx_ref[...] = jnp.zeros_like(gwx_ref)
            gbx_ref[...] = jnp.zeros_like(gbx_ref)
            glam_ref[...] = jnp.zeros_like(glam_ref)
            gcw_ref[...] = jnp.zeros_like(gcw_ref)
            gcb_ref[...] = jnp.zeros_like(gcb_ref)

        xr = xr_ref[0]
        z = z_ref[0]
        hr_blk = hr_ref[...]
        dg = dg_ref[...]
        xprev = jnp.where(has_prev, xp_ref[0], 0.0)
        hprev8 = jnp.where(has_prev, hp_ref[...], 0.0)
        reset = pos_ref[...] == 0
        cw = cw_ref[...]
        wa = wa_ref[0]
        wx = wx_ref[0]
        lam_v = lam_ref[...]
        xc, r, ig, sp, a, mult = _rnn_gates(xr, xprev, cw, cb_ref[...], wa, ba_ref[0], wx, bx_ref[0], lam_v, reset)

        sz = _sigmoid(z)
        dh = dg * (z * sz)
        dz_ref[...] = (dg * hr_blk * (sz * (1.0 + z * (1.0 - sz)))).astype(BF16)

        an = _shift_up(a, a_next[...], 1)
        a_cum, g0 = _scan_up(an, dh, tt)
        g = g0 + a_cum * g_next[0:1]
        a_next[...] = jnp.broadcast_to(a[0:1], (8, DH))
        g_next[...] = jnp.broadcast_to(g[0:1], (8, DH))

        hm1 = _shift_down(hprev8, hr_blk, 1)
        da = g * hm1
        dmult = g * (ig * xc)
        di = g * (mult * xc)
        dxc = g * (mult * ig)
        dla = jnp.where(reset, 0.0, da * a - dmult * (a * a) / mult)
        dr = dla * (-LRU_C * sp)
        dsp = _colsum(dla * (-LRU_C * r))
        glam_ref[0] += dsp * (-_sigmoid(-lam_v))
        dpa = dr * r * (1.0 - r)
        dpx = di * ig * (1.0 - ig)
        dpab = dpa.astype(BF16)
        dpxb = dpx.astype(BF16)
        dxc = dxc + _dot_nt(dpab, wa.astype(BF16)) + _dot_nt(dpxb, wx.astype(BF16))
        xcb = xc.astype(BF16)
        gwa_ref[0] += _dot_tn(xcb, dpab)
        gwx_ref[0] += _dot_tn(xcb, dpxb)
        gba_ref[0] += _colsum(dpa)
        gbx_ref[0] += _colsum(dpx)

        dxr = cw[3:4] * dxc
        for d in (1, 2, 3):
            dxr = dxr + cw[3 - d:4 - d] * _shift_up(dxc, dxc_next[...], d)
        dxr_ref[...] = dxr.astype(BF16)
        dxc_next[...] = dxc[0:8]
        gcb_ref[0] += _colsum(dxc)
        gcw_ref[0, 3:4, :] += _colsum(xr * dxc)
        for d in (1, 2, 3):
            gcw_ref[0, 3 - d:4 - d, :] += _colsum(_shift_down(xprev, xr, d) * dxc)

    rt = lambda t: nt - 1 - t
    prev8 = lambda t: jnp.maximum(rt(t) * t8 - 1, 0)
    head_row = lambda hh, t: (0, hh)
    hsm = lambda hh, t: (hh, 0, 0)
    return pl.pallas_call(
        body, name="rnn_bwd", grid=(H, nt),
        in_specs=[pl.BlockSpec((1, tt, DH), lambda hh, t: (0, rt(t), hh)),
                  pl.BlockSpec((1, tt, DH), lambda hh, t: (0, rt(t), H + hh)),
                  pl.BlockSpec((1, 8, DH), lambda hh, t: (0, prev8(t), hh)),
                  pl.BlockSpec((tt, DH), lambda hh, t: (rt(t), hh)),
                  pl.BlockSpec((8, DH), lambda hh, t: (prev8(t), hh)),
                  pl.BlockSpec((tt, DH), lambda hh, t: (rt(t), hh)),
                  pl.BlockSpec((tt, 1), lambda hh, t: (rt(t), 0)),
                  pl.BlockSpec((4, DH), head_row), pl.BlockSpec((1, DH), head_row),
                  pl.BlockSpec((1, DH, DH), hsm), pl.BlockSpec((1, 1, DH), hsm),
                  pl.BlockSpec((1, DH, DH), hsm), pl.BlockSpec((1, 1, DH), hsm),
                  pl.BlockSpec((1, DH), head_row)],
        out_specs=(pl.BlockSpec((tt, DH), lambda hh, t: (rt(t), hh)), pl.BlockSpec((tt, DH), lambda hh, t: (rt(t), hh)),
                   pl.BlockSpec((1, DH, DH), hsm), pl.BlockSpec((1, 1, DH), hsm),
                   pl.BlockSpec((1, DH, DH), hsm), pl.BlockSpec((1, 1, DH), hsm),
                   pl.BlockSpec((1, 1, DH), hsm), pl.BlockSpec((1, 4, DH), hsm), pl.BlockSpec((1, 1, DH), hsm)),
        out_shape=(jax.ShapeDtypeStruct((s, D), BF16), jax.ShapeDtypeStruct((s, D), BF16),
                   jax.ShapeDtypeStruct((H, DH, DH), F32), jax.ShapeDtypeStruct((H, 1, DH), F32),
                   jax.ShapeDtypeStruct((H, DH, DH), F32), jax.ShapeDtypeStruct((H, 1, DH), F32),
                   jax.ShapeDtypeStruct((H, 1, DH), F32), jax.ShapeDtypeStruct((H, 4, DH), F32),
                   jax.ShapeDtypeStruct((H, 1, DH), F32)),
        scratch_shapes=[pltpu.VMEM((8, DH), F32), pltpu.VMEM((8, DH), F32), pltpu.VMEM((8, DH), F32)],
        compiler_params=_params(("parallel", "arbitrary")),
    )(p, p, p, hr, hr, dgr, pos, conv_w, conv_b, w_a, b_a, w_x, b_x, lam)


def _rope(t, c, sa, sb):
    return t * c + pltpu.roll(t, DH - ROT // 2, 1) * sa + pltpu.roll(t, ROT // 2, 1) * sb


def _rope_bwd(g, c, sa, sb):
    return g * c + pltpu.roll(g * sa, ROT // 2, 1) + pltpu.roll(g * sb, DH - ROT // 2, 1)


def _unit_bases(gi, u):
    dil = DILATIONS[gi]
    if dil == 1:
        return u * UB, SPAN + (u - 1) * UB, u == 0
    if dil == 4:
        blk, r = u // 4, u % 4
        return blk * 4 * UB + r, SPAN + (blk - 1) * 4 * UB + r, blk == 0
    return u, u, True


def _band_mask(first_in_span, has_prev):
    qi = lax.broadcasted_iota(jnp.int32, (UB, 2 * UB), 0)
    ki = lax.broadcasted_iota(jnp.int32, (UB, 2 * UB), 1)
    dist = UB + qi - ki
    band = (dist >= 0) & (dist <= UB)
    return band & ((ki >= UB) | jnp.logical_not(first_in_span) | has_prev)


def _attn_fwd(p, rc, rsa, rsb):
    s = p.shape[1]
    ns = s // SPAN
    nunit = SPAN // UB

    def body(q_ref, k_ref, v_ref, z_ref, c_ref, sa_ref, sb_ref, o_ref, lse_ref, ga_ref,
             qr, kf, vf, acc, mm, ll):
        n = pl.program_id(1)

        @pl.when(n == 0)
        def _():
            kf[0:SPAN] = jnp.zeros((SPAN, DH), F32)
            vf[0:SPAN] = jnp.zeros((SPAN, DH), F32)

        c, sa, sb = c_ref[...], sa_ref[...], sb_ref[...]
        qr[...] = _rope(q_ref[0], c, sa, sb)
        kf[SPAN:] = _rope(k_ref[0], c, sa, sb)
        vf[SPAN:] = v_ref[0]
        has_prev = n > 0

        for gi, dil in enumerate(DILATIONS):
            def unit(u, carry, gi=gi, dil=dil):
                qb0, kb0, first = _unit_bases(gi, u)
                qsl = pl.ds(qb0, UB, stride=dil) if dil > 1 else pl.ds(pl.multiple_of(qb0, UB), UB)
                ksl = pl.ds(kb0, 2 * UB, stride=dil) if dil > 1 else pl.ds(pl.multiple_of(kb0, UB), 2 * UB)
                qb = qr[qsl, :].astype(BF16)
                kb = kf[ksl, :].astype(BF16)
                vb = vf[ksl, :].astype(BF16)
                sc = _dot_nt(qb, kb) * SCALE
                sc = jnp.where(_band_mask(first, has_prev), sc, NEG)
                m = jnp.max(sc, axis=-1, keepdims=True)
                pr = jnp.exp(sc - m)
                l = jnp.sum(pr, axis=-1, keepdims=True)
                acc[gi, qsl, :] = _dot(pr.astype(BF16), vb)
                mm[gi, qsl, :] = jnp.broadcast_to(m, (UB, DH))
                ll[gi, qsl, :] = jnp.broadcast_to(l, (UB, DH))
                return carry

            lax.fori_loop(0, nunit, unit, 0)

        m_all = jnp.maximum(jnp.maximum(mm[0], mm[1]), mm[2])
        num = jnp.zeros((SPAN, DH), F32)
        den = jnp.zeros((SPAN, DH), F32)
        for gi in range(3):
            w = jnp.exp(mm[gi] - m_all)
            num = num + w * acc[gi]
            den = den + w * ll[gi]
        o = num / den
        o_ref[...] = o
        lse_ref[...] = m_all + jnp.log(den)
        z = z_ref[0]
        ga_ref[...] = (o * (z * _sigmoid(z))).astype(BF16)
        kf[0:SPAN] = kf[SPAN:]
        vf[0:SPAN] = vf[SPAN:]

    blk = lambda piece, off: pl.BlockSpec((1, SPAN, DH), lambda hh, n: (piece, n, off + hh))
    tab = pl.BlockSpec((SPAN, DH), lambda hh, n: (n, 0))
    outb = pl.BlockSpec((SPAN, DH), lambda hh, n: (n, hh))
    return pl.pallas_call(
        body, name="attn_fwd", grid=(H, ns),
        in_specs=[blk(1, 0), blk(1, H), blk(2, 0), blk(2, H), tab, tab, tab],
        out_specs=(outb, outb, outb),
        out_shape=(jax.ShapeDtypeStruct((s, D), F32), jax.ShapeDtypeStruct((s, D), F32),
                   jax.ShapeDtypeStruct((s, D), BF16)),
        scratch_shapes=[pltpu.VMEM((SPAN, DH), F32), pltpu.VMEM((2 * SPAN, DH), F32), pltpu.VMEM((2 * SPAN, DH), F32),
                        pltpu.VMEM((3, SPAN, DH), F32), pltpu.VMEM((3, SPAN, DH), F32), pltpu.VMEM((3, SPAN, DH), F32)],
        compiler_params=_params(("parallel", "arbitrary")),
    )(p, p, p, p, rc, rsa, rsb)


def _attn_bwd(p, o, lse, dga, rc, rsa, rsb):
    s = p.shape[1]
    ns = s // SPAN
    nunit = SPAN // UB

    def body(q_ref, k_ref, kp_ref, v_ref, vp_ref, z_ref, c_ref, sa_ref, sb_ref, cp_ref, sap_ref, sbp_ref,
             o_ref, lse_ref, dg_ref, dq_ref, dk_ref, dv_ref, dz_ref,
             qr, kf, vf, dof, dlt, dqa, dkf, dvf):
        step = pl.program_id(1)
        n = ns - 1 - step
        has_prev = n > 0

        @pl.when(step == 0)
        def _():
            dkf[...] = jnp.zeros_like(dkf)
            dvf[...] = jnp.zeros_like(dvf)

        @pl.when(step > 0)
        def _():
            dkf[SPAN:] = dkf[0:SPAN]
            dvf[SPAN:] = dvf[0:SPAN]
            dkf[0:SPAN] = jnp.zeros((SPAN, DH), F32)
            dvf[0:SPAN] = jnp.zeros((SPAN, DH), F32)

        c, sa, sb = c_ref[...], sa_ref[...], sb_ref[...]
        qr[...] = _rope(q_ref[0], c, sa, sb)
        kf[SPAN:] = _rope(k_ref[0], c, sa, sb)
        vf[SPAN:] = v_ref[0]
        kf[0:SPAN] = jnp.where(has_prev, _rope(kp_ref[0], cp_ref[...], sap_ref[...], sbp_ref[...]), 0.0)
        vf[0:SPAN] = jnp.where(has_prev, vp_ref[0], 0.0)
        z = z_ref[0]
        sz = _sigmoid(z)
        dg = dg_ref[...]
        ov = o_ref[...]
        do = dg * (z * sz)
        dz_ref[...] = (dg * ov * (sz * (1.0 + z * (1.0 - sz)))).astype(BF16)
        dof[...] = do
        dlt[...] = jnp.broadcast_to(jnp.sum(do * ov, axis=-1, keepdims=True), (SPAN, DH))
        dqa[...] = jnp.zeros_like(dqa)

        for gi, dil in enumerate(DILATIONS):
            def unit(u, carry, gi=gi, dil=dil):
                qb0, kb0, first = _unit_bases(gi, u)
                qsl = pl.ds(qb0, UB, stride=dil) if dil > 1 else pl.ds(pl.multiple_of(qb0, UB), UB)
                ksl = pl.ds(kb0, 2 * UB, stride=dil) if dil > 1 else pl.ds(pl.multiple_of(kb0, UB), 2 * UB)
                qb = qr[qsl, :].astype(BF16)
                kb = kf[ksl, :].astype(BF16)
                vb = vf[ksl, :].astype(BF16)
                dob = dof[qsl, :].astype(BF16)
                lse_b = lse_ref[qsl, :]
                dl_b = dlt[qsl, :]
                sc = _dot_nt(qb, kb) * SCALE
                pr = jnp.exp(sc - jnp.concatenate([lse_b, lse_b], axis=1))
                pr = jnp.where(_band_mask(first, has_prev), pr, 0.0)
                prb = pr.astype(BF16)
                dvf[ksl, :] += _dot_tn(prb, dob)
                dp = _dot_nt(dob, vb)
                ds = (pr * (dp - jnp.concatenate([dl_b, dl_b], axis=1)) * SCALE).astype(BF16)
                dqa[qsl, :] += _dot(ds, kb)
                dkf[ksl, :] += _dot_tn(ds, qb)
                return carry

            lax.fori_loop(0, nunit, unit, 0)

        dq_ref[...] = _rope_bwd(dqa[...], c, sa, sb).astype(BF16)
        dk_ref[...] = _rope_bwd(dkf[SPAN:], c, sa, sb).astype(BF16)
        dv_ref[...] = dvf[SPAN:].astype(BF16)

    rn = lambda n: ns - 1 - n
    pn = lambda n: jnp.maximum(ns - 2 - n, 0)
    blk = lambda piece, off: pl.BlockSpec((1, SPAN, DH), lambda hh, n: (piece, rn(n), off + hh))
    blkp = lambda piece, off: pl.BlockSpec((1, SPAN, DH), lambda hh, n: (piece, pn(n), off + hh))
    tab = pl.BlockSpec((SPAN, DH), lambda hh, n: (rn(n), 0))
    tabp = pl.BlockSpec((SPAN, DH), lambda hh, n: (pn(n), 0))
    io = pl.BlockSpec((SPAN, DH), lambda hh, n: (rn(n), hh))
    return pl.pallas_call(
        body, name="attn_bwd", grid=(H, ns),
        in_specs=[blk(1, 0), blk(1, H), blkp(1, H), blk(2, 0), blkp(2, 0), blk(2, H),
                  tab, tab, tab, tabp, tabp, tabp, io, io, io],
        out_specs=(io, io, io, io),
        out_shape=tuple(jax.ShapeDtypeStruct((s, D), BF16) for _ in range(4)),
        scratch_shapes=[pltpu.VMEM((SPAN, DH), F32), pltpu.VMEM((2 * SPAN, DH), F32), pltpu.VMEM((2 * SPAN, DH), F32),
                        pltpu.VMEM((SPAN, DH), F32), pltpu.VMEM((SPAN, DH), F32), pltpu.VMEM((SPAN, DH), F32),
                        pltpu.VMEM((2 * SPAN, DH), F32), pltpu.VMEM((2 * SPAN, DH), F32)],
        compiler_params=_params(("parallel", "arbitrary")),
    )(p, p, p, p, p, p, rc, rsa, rsb, rc, rsa, rsb, o, lse, dga)


def _tail(gr, ga, p, x, tgt, w3, b_gate, gate, g_final, tm=256):
    s = x.shape[0]
    nt = s // tm

    def body(gr_ref, ga_ref, pr_ref, pa_ref, x_ref, t_ref, bg_ref, gate_ref, gf_ref, w_hbm,
             dgr_ref, dga_ref, dc_ref, dx2_ref, vec_ref, go_hbm, w_s, acc_s, sem):
        i = pl.program_id(0)

        @pl.when(i == 0)
        def _():
            cp = pltpu.make_async_copy(w_hbm, w_s, sem.at[12])
            cp.start()
            acc_s[...] = jnp.zeros_like(acc_s)
            vec_ref[...] = jnp.zeros_like(vec_ref)
            cp.wait()

        grb = gr_ref[...]
        gab = ga_ref[...]
        bg = bg_ref[...]
        gate_v = gate_ref[...]
        gf = gf_ref[...]
        y_r = _dot(grb, w_s[0])
        y_a = _dot(gab, w_s[1])
        sr = _sigmoid(pr_ref[0] + bg[:, :D])
        sa = _sigmoid(pa_ref[0] + bg[:, D:])
        mb = (sr * y_r + sa * y_a).astype(BF16)
        u = _dot(mb, w_s[2])
        x2 = x_ref[...] + gate_v * u
        rstd = lax.rsqrt(jnp.mean(x2 * x2, axis=-1, keepdims=True) + EPS)
        xh = x2 * rstd
        e = xh * gf - t_ref[...]
        dy = e * (1.0 / D)
        dyg = dy * gf
        dx2 = rstd * (dyg - xh * jnp.mean(dyg * xh, axis=-1, keepdims=True))
        dx2_ref[...] = dx2
        dub = (dx2 * gate_v).astype(BF16)
        dm = _dot_nt(dub, w_s[2])
        dyr = (dm * sr).astype(BF16)
        dya = (dm * sa).astype(BF16)
        dpr = dm * y_r * (sr * (1.0 - sr))
        dpa = dm * y_a * (sa * (1.0 - sa))
        dc_ref[:, :D] = dpr.astype(BF16)
        dc_ref[:, D:] = dpa.astype(BF16)
        dgr_ref[...] = _dot_nt(dyr, w_s[0])
        dga_ref[...] = _dot_nt(dya, w_s[1])
        acc_s[0] += _dot_tn(grb, dyr)
        acc_s[1] += _dot_tn(gab, dya)
        acc_s[2] += _dot_tn(mb, dub)
        vec_ref[0:1, :] += _colsum(dy * xh)
        vec_ref[1:2, :] += _colsum(dx2 * u)
        vec_ref[2:3, :] += _colsum(dpr)
        vec_ref[3:4, :] += _colsum(dpa)
        vec_ref[4:5, :] += _colsum(e * e)

        @pl.when(i == nt - 1)
        def _():
            vec_ref[4:5, :] = jnp.broadcast_to(jnp.sum(vec_ref[4:5, :]) * (0.5 / D), (1, D))
            cps = []
            for w in range(3):
                for j in range(4):
                    cps.append(pltpu.make_async_copy(acc_s.at[w, pl.ds(256 * j, 256)],
                                                     go_hbm.at[j, pl.ds(256 * w, 256)], sem.at[4 * w + j]))
            for cp in cps:
                cp.start()
            for cp in cps:
                cp.wait()

    rowt = lambda i: (i, 0)
    row = lambda w: pl.BlockSpec((1, w), lambda i: (0, 0))
    any_ = pl.BlockSpec(memory_space=pl.ANY)
    return pl.pallas_call(
        body, name="tail", grid=(nt,),
        in_specs=[pl.BlockSpec((tm, D), rowt), pl.BlockSpec((tm, D), rowt),
                  pl.BlockSpec((1, tm, D), lambda i: (3, i, 0)), pl.BlockSpec((1, tm, D), lambda i: (3, i, 1)),
                  pl.BlockSpec((tm, D), rowt), pl.BlockSpec((tm, D), rowt),
                  row(2 * D), row(D), row(D), any_],
        out_specs=(pl.BlockSpec((tm, D), rowt), pl.BlockSpec((tm, D), rowt), pl.BlockSpec((tm, 2 * D), rowt),
                   pl.BlockSpec((tm, D), rowt), pl.BlockSpec((8, D), lambda i: (0, 0)), any_),
        out_shape=(jax.ShapeDtypeStruct((s, D), F32), jax.ShapeDtypeStruct((s, D), F32),
                   jax.ShapeDtypeStruct((s, 2 * D), BF16), jax.ShapeDtypeStruct((s, D), F32),
                   jax.ShapeDtypeStruct((8, D), F32), jax.ShapeDtypeStruct((4, 768, D), F32)),
        scratch_shapes=[pltpu.VMEM((3, D, D), BF16), pltpu.VMEM((3, D, D), F32), pltpu.SemaphoreType.DMA((13,))],
        compiler_params=_params(("arbitrary",)),
    )(gr, ga, p, p, x, tgt, b_gate, gate, g_final, w3)


def _pieces_steps(pieces):
    out, s0 = [], 0
    for a in pieces:
        n = a.shape[1] // D
        out.append((s0, n))
        s0 += n
    return out, s0


def _inproj_bwd_x(pieces, wg, x, dx2, gn, scale, tm=512):
    s = x.shape[0]
    steps, nk = _pieces_steps(pieces)
    npc = PW // D
    np_ = len(pieces)

    def body(*refs):
        d_refs = refs[:np_]
        w_ref, x_ref, dx2_ref, gn_ref, sc_ref, gx_ref, vec_ref, acc = refs[np_:]
        i, k = pl.program_id(0), pl.program_id(1)

        @pl.when(k == 0)
        def _():
            acc[...] = jnp.zeros_like(acc)

        @pl.when((i == 0) & (k == 0))
        def _():
            vec_ref[...] = jnp.zeros_like(vec_ref)

        for (s0, n), d_ref in zip(steps, d_refs):
            @pl.when((k >= s0) & (k < s0 + n))
            def _(d_ref=d_ref):
                acc[...] += _dot_nt(d_ref[...], w_ref[0])

        @pl.when(k == nk - 1)
        def _():
            dh = acc[...]
            xt = x_ref[...]
            rstd = lax.rsqrt(jnp.mean(xt * xt, axis=-1, keepdims=True) + EPS)
            xh = xt * rstd
            gn_v = gn_ref[...]
            sc1 = 1.0 + sc_ref[...]
            dhx = dh * xh
            vec_ref[0:1, :] += _colsum(dh)
            vec_ref[1:2, :] += _colsum(dhx) * gn_v
            vec_ref[2:3, :] += _colsum(dhx) * sc1
            dxh = dh * (gn_v * sc1)
            gx_ref[...] = rstd * (dxh - xh * jnp.mean(dxh * xh, axis=-1, keepdims=True)) + dx2_ref[...]

    def piece_spec(s0, n):
        return pl.BlockSpec((tm, D), lambda i, k: (i, jnp.clip(k - s0, 0, n - 1)))

    rowt = lambda i, k: (i, 0)
    row = pl.BlockSpec((1, D), lambda i, k: (0, 0))
    return pl.pallas_call(
        body, name="inproj_bwd_x", grid=(s // tm, nk),
        in_specs=[piece_spec(s0, n) for s0, n in steps] +
                 [pl.BlockSpec((1, D, D), lambda i, k: (k // npc, 0, k % npc)),
                  pl.BlockSpec((tm, D), rowt), pl.BlockSpec((tm, D), rowt), row, row],
        out_specs=(pl.BlockSpec((tm, D), rowt), pl.BlockSpec((8, D), lambda i, k: (0, 0))),
        out_shape=(jax.ShapeDtypeStruct((s, D), F32), jax.ShapeDtypeStruct((8, D), F32)),
        scratch_shapes=[pltpu.VMEM((tm, D), F32)],
        compiler_params=_params(("arbitrary", "arbitrary")),
    )(*pieces, wg, x, dx2, gn, scale)


def _inproj_bwd_w(pieces, hbf, tk=512):
    s = hbf.shape[0]
    steps, nk = _pieces_steps(pieces)
    npc = PW // D
    ns = s // tk
    np_ = len(pieces)

    def body(*refs):
        d_refs = refs[:np_]
        h_ref, g_ref = refs[np_:]
        cb, k = pl.program_id(0), pl.program_id(1)

        @pl.when(k == 0)
        def _():
            g_ref[...] = jnp.zeros_like(g_ref)

        for (s0, n), d_ref in zip(steps, d_refs):
            @pl.when((cb >= s0) & (cb < s0 + n))
            def _(d_ref=d_ref):
                g_ref[0] += _dot_tn(h_ref[...], d_ref[...])

    def piece_spec(s0, n):
        def imap(cb, k):
            active = (cb >= s0) & (cb < s0 + n)
            return (jnp.where(active, k, 0), jnp.clip(cb - s0, 0, n - 1))
        return pl.BlockSpec((tk, D), imap)

    return pl.pallas_call(
        body, name="inproj_bwd_w", grid=(nk, ns),
        in_specs=[piece_spec(s0, n) for s0, n in steps] + [pl.BlockSpec((tk, D), lambda cb, k: (k, 0))],
        out_specs=pl.BlockSpec((1, D, D), lambda cb, k: (cb // npc, 0, cb % npc)),
        out_shape=jax.ShapeDtypeStruct((4, D, PW), F32),
        compiler_params=_params(("parallel", "arbitrary")),
    )(*pieces, hbf)


D2D_CHUNK_BYTES = 512 * 1024


def _chunk_rows(a):
    return max(8, D2D_CHUNK_BYTES // (a.shape[-1] * a.dtype.itemsize))


def _pair_exchange(arrs):
    na = len(arrs)
    chunks = []
    for t, a in enumerate(arrs):
        hr = a.shape[1] // 2
        cr = _chunk_rows(a)
        chunks += [(t, j, r0, cr) for j in range(a.shape[0]) for r0 in range(0, hr, cr)]
    nch = len(chunks)

    def body(*refs):
        a_refs = refs[:na]
        rb_refs = refs[na:2 * na]
        ss, rs = refs[2 * na:]
        x, y, c = _coords()
        sib = (x, y, 1 - c)
        rcs = []
        for n, (t, j, r0, cr) in enumerate(chunks):
            hr = a_refs[t].shape[1] // 2
            rc = _rcopy(a_refs[t].at[j, pl.ds((1 - c) * hr + r0, cr), :], rb_refs[t].at[j, pl.ds(r0, cr), :],
                        ss.at[n], rs.at[n], sib)
            rc.start()
            rcs.append(rc)
        for rc in rcs:
            rc.wait_recv()
        for rc in rcs:
            rc.wait_send()

    any_ = pl.BlockSpec(memory_space=pl.ANY)
    halves = [jax.ShapeDtypeStruct((a.shape[0], a.shape[1] // 2, a.shape[2]), a.dtype) for a in arrs]
    return pl.pallas_call(
        body, name="pair_exchange",
        out_shape=tuple(halves),
        in_specs=[any_] * na, out_specs=tuple([any_] * na),
        scratch_shapes=[pltpu.SemaphoreType.DMA((nch,)), pltpu.SemaphoreType.DMA((nch,))],
        compiler_params=_params(),
    )(*arrs)


def _chip_scatter(arrs):
    na = len(arrs)

    def body(*refs):
        a_refs = refs[:na]
        o_refs = refs[na:2 * na]
        ss, rs = refs[2 * na:]
        x, y, c = _coords()
        j = 2 * x + y
        chips = ((1, 0), (0, 1), (1, 1))
        cps = []
        for t, (a, o) in enumerate(zip(a_refs, o_refs)):
            for q, (kx, ky) in enumerate(chips):
                jt = j ^ (2 * kx + ky)
                rc = _rcopy(a.at[jt], o.at[q], ss.at[3 * t + q], rs.at[3 * t + q], (_flip(x, kx), _flip(y, ky), c))
                rc.start()
                cps.append(rc)
        for cp in cps:
            cp.wait()

    any_ = pl.BlockSpec(memory_space=pl.ANY)
    return pl.pallas_call(
        body, name="chip_scatter",
        out_shape=tuple(jax.ShapeDtypeStruct((3,) + a.shape[1:], a.dtype) for a in arrs),
        in_specs=[any_] * na, out_specs=tuple([any_] * na),
        scratch_shapes=[pltpu.SemaphoreType.DMA((3 * na,)), pltpu.SemaphoreType.DMA((3 * na,))],
        compiler_params=_params(),
    )(*arrs)


def _pair_swap(arrs):
    na = len(arrs)
    chunks = []
    for t, a in enumerate(arrs):
        cr = _chunk_rows(a)
        chunks += [(t, r0, cr) for r0 in range(0, a.shape[0], cr)]
    nch = len(chunks)

    def body(*refs):
        a_refs = refs[:na]
        o_refs = refs[na:2 * na]
        ss, rs = refs[2 * na:]
        x, y, c = _coords()
        sib = (x, y, 1 - c)
        rcs = []
        for n, (t, r0, cr) in enumerate(chunks):
            rows = pl.ds(r0, cr)
            rc = _rcopy(a_refs[t].at[rows, :], o_refs[t].at[rows, :], ss.at[n], rs.at[n], sib)
            rc.start()
            rcs.append(rc)
        for rc in rcs:
            rc.wait_recv()
        for rc in rcs:
            rc.wait_send()

    any_ = pl.BlockSpec(memory_space=pl.ANY)
    return pl.pallas_call(
        body, name="pair_swap",
        out_shape=tuple(jax.ShapeDtypeStruct(a.shape, a.dtype) for a in arrs),
        in_specs=[any_] * na, out_specs=tuple([any_] * na),
        scratch_shapes=[pltpu.SemaphoreType.DMA((nch,)), pltpu.SemaphoreType.DMA((nch,))],
        compiler_params=_params(),
    )(*arrs)


def _add_half(full, rb, core, tr):
    n, r, cdim = full.shape
    nb = r // 2 // tr

    def body(c_ref, a_ref, b_ref, o_ref):
        o_ref[...] = a_ref[...] + b_ref[...]

    mine = pl.BlockSpec((1, tr, cdim), lambda i, j, c_ref: (i, c_ref[0] * nb + j, 0))
    spec = pl.BlockSpec((1, tr, cdim), lambda i, j, c_ref: (i, j, 0))
    return pl.pallas_call(
        body, name="add_half",
        grid_spec=pltpu.PrefetchScalarGridSpec(num_scalar_prefetch=1, grid=(n, nb), in_specs=[mine, spec],
                                               out_specs=spec),
        out_shape=jax.ShapeDtypeStruct(rb.shape, rb.dtype),
        compiler_params=_params(("parallel", "parallel")),
    )(core, full, rb)


def _sum_slots(q, r3, shard, tr):
    _, r, cdim = q.shape

    def body(j_ref, q_ref, r_ref, o_ref):
        o_ref[...] = ((q_ref[0] + r_ref[0]) + r_ref[1]) + r_ref[2]

    return pl.pallas_call(
        body, name="sum_slots",
        grid_spec=pltpu.PrefetchScalarGridSpec(
            num_scalar_prefetch=1, grid=(r // tr,),
            in_specs=[pl.BlockSpec((1, tr, cdim), lambda i, j_ref: (j_ref[0], i, 0)),
                      pl.BlockSpec((3, tr, cdim), lambda i, j_ref: (0, i, 0))],
            out_specs=pl.BlockSpec((tr, cdim), lambda i, j_ref: (i, 0))),
        out_shape=jax.ShapeDtypeStruct((r, cdim), q.dtype),
        compiler_params=_params(("parallel",)),
    )(shard, q, r3)


def _allreduce_small(pack):
    def body(p_ref, out_ref, rbuf, s1, r1, s2, r2):
        me = _my_index()
        chunk = lambda d: pl.ds(pl.multiple_of(d * AR_CHUNK, 8), AR_CHUNK)
        sends = []
        for k in range(1, NDEV):
            cp = _rcopy(p_ref.at[chunk(me ^ k)], rbuf.at[me], s1.at[k - 1], r1.at[k - 1], _peer(k))
            cp.start()
            sends.append(cp)
        rbuf[me] = p_ref[chunk(me), :]
        for k in range(1, NDEV):
            _rcopy(p_ref.at[chunk(me)], rbuf.at[me ^ k], s1.at[k - 1], r1.at[k - 1], _peer(k)).wait_recv()
        tot = rbuf[0]
        for d in range(1, NDEV):
            tot = tot + rbuf[d]
        out_ref[chunk(me), :] = tot
        for k in range(1, NDEV):
            cp = _rcopy(out_ref.at[chunk(me)], out_ref.at[chunk(me)], s2.at[k - 1], r2.at[k - 1], _peer(k))
            cp.start()
            sends.append(cp)
        for k in range(1, NDEV):
            _rcopy(out_ref.at[chunk(me)], out_ref.at[chunk(me ^ k)], s2.at[k - 1], r2.at[k - 1], _peer(k)).wait_recv()
        for cp in sends:
            cp.wait_send()

    vm = pl.BlockSpec(memory_space=pltpu.VMEM)
    return pl.pallas_call(
        body, name="allreduce_small",
        out_shape=jax.ShapeDtypeStruct((AR_ROWS, D), F32),
        in_specs=[vm], out_specs=vm,
        scratch_shapes=[pltpu.VMEM((NDEV, AR_CHUNK, D), F32),
                        pltpu.SemaphoreType.DMA((7,)), pltpu.SemaphoreType.DMA((7,)),
                        pltpu.SemaphoreType.DMA((7,)), pltpu.SemaphoreType.DMA((7,))],
        compiler_params=_params(),
    )(pack)


def _adamw(w, g, m, v, tr):
    r, cdim = w.shape

    def body(w_ref, g_ref, m_ref, v_ref, d_ref, nm_ref, nv_ref):
        gv = g_ref[...]
        nm = B1 * m_ref[...] + (1.0 - B1) * gv
        nv = B2 * v_ref[...] + (1.0 - B2) * (gv * gv)
        m_hat = nm / (1.0 - B1 ** STEP)
        v_hat = nv / (1.0 - B2 ** STEP)
        d_ref[...] = -LR * (m_hat / (jnp.sqrt(v_hat) + ADAM_EPS) + WD * w_ref[...])
        nm_ref[...] = nm
        nv_ref[...] = nv

    spec = pl.BlockSpec((tr, cdim), lambda i: (i, 0))
    sd = jax.ShapeDtypeStruct((r, cdim), F32)
    return pl.pallas_call(
        body, name="adamw", grid=(r // tr,), in_specs=[spec] * 4, out_specs=(spec,) * 3, out_shape=(sd,) * 3,
        compiler_params=_params(("parallel",)),
    )(w, g, m, v)


def _rope_tables(positions):
    inv_freq = ROPE_THETA ** (-jnp.arange(0, ROT, 2, dtype=F32) / ROT)
    ang = positions.astype(F32)[:, None] * inv_freq
    cos, sin = jnp.cos(ang), jnp.sin(ang)
    n = positions.shape[0]
    half = ROT // 2
    rc = jnp.concatenate([cos, cos, jnp.ones((n, DH - ROT), F32)], axis=1)
    rsa = jnp.concatenate([-sin, jnp.zeros((n, DH - half), F32)], axis=1)
    rsb = jnp.concatenate([jnp.zeros((n, half), F32), sin, jnp.zeros((n, DH - ROT), F32)], axis=1)
    return rc, rsa, rsb


def kernel(x, c, positions, g_norm, w_mod, b_mod, w_in, b_gate, conv_w, conv_b, w_a, b_a, w_x, b_x, lam, w_out_rnn, w_out_attn, w_o, g_final, loss_target, m_g_norm, m_w_mod, m_b_mod, m_w_in, m_b_gate, m_conv_w, m_conv_b, m_w_a, m_b_a, m_w_x, m_b_x, m_lam, m_w_out_rnn, m_w_out_attn, m_w_o, m_g_final, v_g_norm, v_w_mod, v_b_mod, v_w_in, v_b_gate, v_conv_w, v_conv_b, v_w_a, v_b_a, v_w_x, v_b_x, v_lam, v_w_out_rnn, v_w_out_attn, v_w_o, v_g_final):
    s = x.shape[1]
    xi = lax.axis_index("x")
    yi = lax.axis_index("y")
    ci = lax.axis_index("c")
    shard = 2 * xi + yi
    x2d = x[0]
    tgt = loss_target[0]
    pos = positions[0]

    c_all, mod4 = _mod_fwd(c, w_mod[0], b_mod.reshape(4, 1, 768))
    mod = mod4.reshape(1, 3 * D)
    shift, scale, gate = mod[:, :D], mod[:, D:2 * D], mod[:, 2 * D:]
    w3_sh = jnp.stack([w_out_rnn[0], w_out_attn[0], w_o[0]]).astype(BF16)
    wsh = jnp.concatenate([w_in[0].astype(BF16), w3_sh.reshape(384, PW)], axis=0)
    wg = lax.dynamic_update_slice(_gather_weights(wsh), wsh[None], (shard, 0, 0))
    w3 = wg[:, D:, :].reshape(4, 3, 256, D).transpose(1, 0, 2, 3).reshape(3, D, D)
    conv_all = _gather_small(conv_w[0])
    conv_full = conv_all[0::2].transpose(1, 0, 2).reshape(4, D)

    p, hbf = _norm_inproj(x2d, g_norm, shift, scale, wg)
    rc, rsa, rsb = _rope_tables(pos)
    pos_col = pos.reshape(s, 1)
    b_a3, b_x3 = b_a.reshape(H, 1, DH), b_x.reshape(H, 1, DH)
    hr, gr = _rnn_fwd(p, pos_col, conv_full, conv_b, w_a[0], b_a3, w_x[0], b_x3, lam)
    o, lse, ga = _attn_fwd(p, rc, rsa, rsb)

    dgr, dga, dc, dx2, vec_t, g_out = _tail(gr, ga, p, x2d, tgt, w3, b_gate, gate, g_final.reshape(1, D))

    dxr, dzr, g_wa, g_ba, g_wx, g_bx, g_lam, g_cw, g_cb = _rnn_bwd(
        p, hr, dgr, pos_col, conv_full, conv_b, w_a[0], b_a3, w_x[0], b_x3, lam)
    dq, dk, dv, dza = _attn_bwd(p, o, lse, dga, rc, rsa, rsb)

    pieces = [dxr, dzr, dq, dk, dv, dza, dc]
    grad_x, vec_n = _inproj_bwd_x(pieces, wg, x2d, dx2, g_norm, scale)
    g_win = _inproj_bwd_w(pieces, hbf)

    core = ci.reshape(1)
    shard1 = shard.reshape(1)
    rb_a, rb_b = _pair_exchange([g_win, g_out])
    q_a, q_b = _add_half(g_win, rb_a, core, tr=256), _add_half(g_out, rb_b, core, tr=128)
    r_a, r_b = _chip_scatter([q_a, q_b])
    f_a, f_b = _sum_slots(q_a, r_a, shard1, tr=256), _sum_slots(q_b, r_b, shard1, tr=128)
    s_a, s_b = _pair_swap([f_a, f_b])
    south = ci == 0
    grad_w_in = jnp.where(south, jnp.concatenate([f_a, s_a], axis=0), jnp.concatenate([s_a, f_a], axis=0))
    g3 = jnp.where(south, jnp.concatenate([f_b, s_b], axis=0), jnp.concatenate([s_b, f_b], axis=0)).reshape(3, 256, D)

    dmod_row = jnp.concatenate([vec_n[0:1], vec_n[1:2], vec_t[1:2]], axis=1)
    pack = jnp.concatenate([
        vec_n[2:3],
        dmod_row.reshape(3, D),
        vec_t[2:4],
        g_cb.reshape(1, D),
        g_wa.reshape(128, D),
        g_ba.reshape(1, D),
        g_wx.reshape(128, D),
        g_bx.reshape(1, D),
        g_lam.reshape(1, D),
        vec_t[0:1],
        g_cw.transpose(1, 0, 2).reshape(4, D),
        vec_t[4:5],
        jnp.zeros((AR_ROWS - 272, D), F32)], axis=0)
    red = _allreduce_small(pack)
    loss = red[271, 0]
    grad_w_mod = _mod_bwd(dmod_row.reshape(4, 1, 768), c_all)
    g_conv_sh = lax.dynamic_slice_in_dim(red[267:271], shard * 256, 256, axis=1)

    def small_pack(g_norm_, b_mod_, b_gate_, conv_b_, w_a_, b_a_, w_x_, b_x_, lam_, g_final_, conv_w_):
        return jnp.concatenate([
            g_norm_.reshape(1, D), b_mod_.reshape(3, D), b_gate_.reshape(2, D), conv_b_.reshape(1, D),
            w_a_.reshape(128, D), b_a_.reshape(1, D), w_x_.reshape(128, D), b_x_.reshape(1, D),
            lam_.reshape(1, D), g_final_.reshape(1, D), conv_w_.reshape(1, D),
            jnp.zeros((4, D), F32)], axis=0)

    wp = small_pack(g_norm, b_mod, b_gate, conv_b, w_a, b_a, w_x, b_x, lam, g_final, conv_w)
    mp = small_pack(m_g_norm, m_b_mod, m_b_gate, m_conv_b, m_w_a, m_b_a, m_w_x, m_b_x, m_lam, m_g_final, m_conv_w)
    vp = small_pack(v_g_norm, v_b_mod, v_b_gate, v_conv_b, v_w_a, v_b_a, v_w_x, v_b_x, v_lam, v_g_final, v_conv_w)
    gp = jnp.concatenate([red[0:267], g_conv_sh.reshape(1, D), jnp.zeros((4, D), F32)], axis=0)
    small = _adamw(wp, gp, mp, vp, tr=136)

    def unpack(a):
        return dict(
            g_norm=a[0:1], b_mod=a[1:4].reshape(1, 3 * D), b_gate=a[4:6].reshape(1, 2 * D), conv_b=a[6:7],
            w_a=a[7:135].reshape(1, H, DH, DH), b_a=a[135:136].reshape(1, H, DH),
            w_x=a[136:264].reshape(1, H, DH, DH), b_x=a[264:265].reshape(1, H, DH), lam=a[265:266],
            g_final=a[266].reshape(D), conv_w=a[267:268].reshape(1, 4, 256))

    big_in = _adamw(w_in[0], grad_w_in, m_w_in[0], v_w_in[0], tr=256)
    big_mod = _adamw(w_mod[0], grad_w_mod, m_w_mod[0], v_w_mod[0], tr=256)
    w3f = jnp.concatenate([w_out_rnn[0], w_out_attn[0], w_o[0]], axis=0)
    m3f = jnp.concatenate([m_w_out_rnn[0], m_w_out_attn[0], m_w_o[0]], axis=0)
    v3f = jnp.concatenate([v_w_out_rnn[0], v_w_out_attn[0], v_w_o[0]], axis=0)
    big_out = _adamw(w3f, g3.reshape(768, D), m3f, v3f, tr=256)

    names = ["g_norm", "w_mod", "b_mod", "w_in", "b_gate", "conv_w", "conv_b", "w_a", "b_a", "w_x", "b_x", "lam",
             "w_out_rnn", "w_out_attn", "w_o", "g_final"]
    grads = unpack(gp)
    grads.update(w_mod=grad_w_mod[None], w_in=grad_w_in[None],
                 w_out_rnn=g3[0][None], w_out_attn=g3[1][None], w_o=g3[2][None])
    outs = [grads]
    for idx in range(3):
        d = unpack(small[idx])
        d.update(w_mod=big_mod[idx][None], w_in=big_in[idx][None],
                 w_out_rnn=big_out[idx][0:256][None], w_out_attn=big_out[idx][256:512][None],
                 w_o=big_out[idx][512:768][None])
        outs.append(d)
    flat = [d[n] for d in outs for n in names]
    return (loss, grad_x[None], *flat)
```

```python
import jax
import jax.numpy as jnp
from jax import lax
from jax.experimental import pallas as pl
from jax.experimental.pallas import tpu as pltpu

F32, BF16 = jnp.float32, jnp.bfloat16
MESH = pl.DeviceIdType.MESH
HIGHEST = lax.Precision.HIGHEST

D = 1024
H = 8
DH = 128
PW = 2048
EPS = 1e-6
LRU_C = 8.0
SCALE = DH ** -0.5
NEG = -1e30
SPAN = 2048
UB = 128
DILATIONS = (1, 4, 16)
UNIT_UNROLL = 8
ROPE_THETA = 500000.0
ROT = 32

LR, B1, B2, ADAM_EPS, WD, STEP = 0.001, 0.9, 0.999, 1e-08, 0.01, 10

WROWS = 1024 + 384
NDEV = 8
AR_ROWS = 320
AR_CHUNK = AR_ROWS // NDEV


def _params(sem=None, vmem_mb=56):
    return pltpu.CompilerParams(dimension_semantics=sem, vmem_limit_bytes=vmem_mb * 2 ** 20)


def _coords():
    return lax.axis_index("x"), lax.axis_index("y"), lax.axis_index("c")


def _flip(v, bit):
    return 1 - v if bit else v


def _peer(k):
    x, y, c = _coords()
    return (_flip(x, (k >> 2) & 1), _flip(y, (k >> 1) & 1), _flip(c, k & 1))


def _my_index():
    x, y, c = _coords()
    return 4 * x + 2 * y + c


def _rcopy(src, dst, ssem, rsem, dev):
    return pltpu.make_async_remote_copy(src_ref=src, dst_ref=dst, send_sem=ssem, recv_sem=rsem,
                                        device_id=dev, device_id_type=MESH)


def _sigmoid(x):
    return jax.nn.sigmoid(x)


def _dot(a, b):
    return jnp.dot(a, b, preferred_element_type=F32)


def _dot_nt(a, b):
    return lax.dot_general(a, b, (((1,), (1,)), ((), ())), preferred_element_type=F32)


def _dot_tn(a, b):
    return lax.dot_general(a, b, (((0,), (0,)), ((), ())), preferred_element_type=F32)


def _colsum(a):
    return jnp.sum(a, axis=0, keepdims=True)


def _mod_fwd(c, w_mod_sh, b_mod4):
    def body(c_ref, w_ref, b_ref, call_ref, mod_ref, rows_ref, cmat_ref, s1, r1, s2, r2):
        x, y, _ = _coords()
        me = _my_index()
        j = 2 * x + y
        call_ref[me] = c_ref[...]
        sends = []
        for k in range(1, NDEV):
            cp = _rcopy(call_ref.at[me], call_ref.at[me], s1.at[k - 1], r1.at[k - 1], _peer(k))
            cp.start()
            sends.append(cp)
        for k in range(1, NDEV):
            pk = me ^ k
            _rcopy(call_ref.at[pk], call_ref.at[pk], s1.at[k - 1], r1.at[k - 1], _peer(k)).wait_recv()
        for b in range(NDEV):
            cmat_ref[pl.ds(b, 1), :] = call_ref[b]
        cm = cmat_ref[...]
        act = cm * _sigmoid(cm)
        mp = jnp.dot(act, w_ref[...], preferred_element_type=F32, precision=HIGHEST) + b_ref[j]
        for b in range(NDEV):
            rows_ref[b] = mp[b:b + 1]
        mod_ref[j] = rows_ref[me]
        for q, k in enumerate((2, 4, 6)):
            cp = _rcopy(rows_ref.at[me ^ k], mod_ref.at[j], s2.at[q], r2.at[q], _peer(k))
            cp.start()
            sends.append(cp)
        for q, k in enumerate((2, 4, 6)):
            jq = j ^ (k >> 1)
            _rcopy(rows_ref.at[me], mod_ref.at[jq], s2.at[q], r2.at[q], _peer(k)).wait_recv()
        for cp in sends:
            cp.wait_send()

    vm = pl.BlockSpec(memory_space=pltpu.VMEM)
    return pl.pallas_call(
        body, name="mod_fwd",
        out_shape=(jax.ShapeDtypeStruct((NDEV, 1, D), F32), jax.ShapeDtypeStruct((4, 1, 768), F32)),
        in_specs=[vm, vm, vm], out_specs=(vm, vm),
        scratch_shapes=[pltpu.VMEM((NDEV, 1, 768), F32), pltpu.VMEM((NDEV, D), F32),
                        pltpu.SemaphoreType.DMA((7,)), pltpu.SemaphoreType.DMA((7,)),
                        pltpu.SemaphoreType.DMA((3,)), pltpu.SemaphoreType.DMA((3,))],
        compiler_params=_params(),
    )(c, w_mod_sh, b_mod4)


def _mod_bwd(dmod4, c_all):
    def body(d_ref, call_ref, gw_ref, dall_ref, cmat_ref, dmat_ref, s1, r1):
        x, y, _ = _coords()
        me = _my_index()
        j = 2 * x + y
        dall_ref[me] = d_ref[...]
        sends = []
        for k in range(1, NDEV):
            cp = _rcopy(dall_ref.at[me], dall_ref.at[me], s1.at[k - 1], r1.at[k - 1], _peer(k))
            cp.start()
            sends.append(cp)
        for k in range(1, NDEV):
            pk = me ^ k
            _rcopy(dall_ref.at[pk], dall_ref.at[pk], s1.at[k - 1], r1.at[k - 1], _peer(k)).wait_recv()
        for cp in sends:
            cp.wait_send()
        for b in range(NDEV):
            cmat_ref[pl.ds(b, 1), :] = call_ref[b]
            dmat_ref[pl.ds(b, 1), :] = dall_ref[b, j]
        cm = cmat_ref[...]
        act = cm * _sigmoid(cm)
        gw_ref[...] = lax.dot_general(act, dmat_ref[...], (((0,), (0,)), ((), ())),
                                      preferred_element_type=F32, precision=HIGHEST)

    vm = pl.BlockSpec(memory_space=pltpu.VMEM)
    return pl.pallas_call(
        body, name="mod_bwd",
        out_shape=jax.ShapeDtypeStruct((D, 768), F32),
        in_specs=[vm, vm], out_specs=vm,
        scratch_shapes=[pltpu.VMEM((NDEV, 4, 1, 768), F32), pltpu.VMEM((NDEV, D), F32), pltpu.VMEM((NDEV, 768), F32),
                        pltpu.SemaphoreType.DMA((7,)), pltpu.SemaphoreType.DMA((7,))],
        compiler_params=_params(),
    )(dmod4, c_all)


def _gather_weights(wsh):
    rows, cols = wsh.shape
    half = rows // 2
    nch = 4
    cr = half // nch

    def body(w_ref, g_ref, ss, rs):
        x, y, c = _coords()
        j = 2 * x + y
        sib = (x, y, 1 - c)
        sends = []
        chips = ((1, 0), (0, 1), (1, 1))
        mine = lambda n: pl.ds(c * half + n * cr, cr)
        theirs = lambda n: pl.ds((1 - c) * half + n * cr, cr)
        for n in range(nch):
            for q, (kx, ky) in enumerate(chips):
                e = nch * q + n
                cp = _rcopy(w_ref.at[mine(n)], g_ref.at[j, mine(n)], ss.at[e], rs.at[e],
                            (_flip(x, kx), _flip(y, ky), c))
                cp.start()
                sends.append(cp)
        for n in range(nch):
            for q, (kx, ky) in enumerate(chips):
                jq = j ^ (2 * kx + ky)
                e = nch * q + n
                _rcopy(w_ref.at[mine(n)], g_ref.at[jq, mine(n)], ss.at[e], rs.at[e], sib).wait_recv()
                cp = _rcopy(g_ref.at[jq, mine(n)], g_ref.at[jq, mine(n)], ss.at[3 * nch + e], rs.at[3 * nch + e], sib)
                cp.start()
                sends.append(cp)
        for n in range(nch):
            for q, (kx, ky) in enumerate(chips):
                jq = j ^ (2 * kx + ky)
                e = nch * q + n
                _rcopy(w_ref.at[mine(n)], g_ref.at[jq, theirs(n)], ss.at[3 * nch + e], rs.at[3 * nch + e], sib).wait_recv()
        for cp in sends:
            cp.wait_send()

    any_ = pl.BlockSpec(memory_space=pl.ANY)
    return pl.pallas_call(
        body, name="gather_weights",
        out_shape=jax.ShapeDtypeStruct((4, rows, cols), wsh.dtype),
        in_specs=[any_], out_specs=any_,
        scratch_shapes=[pltpu.SemaphoreType.DMA((6 * nch,)), pltpu.SemaphoreType.DMA((6 * nch,))],
        compiler_params=_params(),
    )(wsh)


def _gather_small(v):
    r, cdim = v.shape

    def body(v_ref, out_ref, ss, rs):
        me = _my_index()
        out_ref[me] = v_ref[...]
        sends = []
        for k in range(1, NDEV):
            cp = _rcopy(out_ref.at[me], out_ref.at[me], ss.at[k - 1], rs.at[k - 1], _peer(k))
            cp.start()
            sends.append(cp)
        for k in range(1, NDEV):
            pk = me ^ k
            _rcopy(out_ref.at[pk], out_ref.at[pk], ss.at[k - 1], rs.at[k - 1], _peer(k)).wait_recv()
        for cp in sends:
            cp.wait_send()

    vm = pl.BlockSpec(memory_space=pltpu.VMEM)
    return pl.pallas_call(
        body, name="gather_small",
        out_shape=jax.ShapeDtypeStruct((NDEV, r, cdim), v.dtype),
        in_specs=[vm], out_specs=vm,
        scratch_shapes=[pltpu.SemaphoreType.DMA((7,)), pltpu.SemaphoreType.DMA((7,))],
        compiler_params=_params(),
    )(v)


def _norm_inproj(x, gn, shift, scale, wg, tm=1024, tn=512):
    s = x.shape[0]
    npc = PW // tn

    def body(x_ref, gn_ref, sh_ref, sc_ref, w_ref, p_ref, h_ref, hs):
        @pl.when(pl.program_id(1) == 0)
        def _():
            xt = x_ref[...]
            rstd = lax.rsqrt(jnp.mean(xt * xt, axis=-1, keepdims=True) + EPS)
            h = (xt * rstd * gn_ref[...]) * (1.0 + sc_ref[...]) + sh_ref[...]
            hs[...] = h.astype(BF16)
            h_ref[...] = hs[...]

        p_ref[0] = _dot(hs[...], w_ref[0])

    row = pl.BlockSpec((1, D), lambda i, j: (0, 0))
    return pl.pallas_call(
        body, name="norm_inproj", grid=(s // tm, 4 * npc),
        in_specs=[pl.BlockSpec((tm, D), lambda i, j: (i, 0)), row, row, row,
                  pl.BlockSpec((1, D, tn), lambda i, j: (j // npc, 0, j % npc))],
        out_specs=(pl.BlockSpec((1, tm, tn), lambda i, j: (j // npc, i, j % npc)),
                   pl.BlockSpec((tm, D), lambda i, j: (i, 0))),
        out_shape=(jax.ShapeDtypeStruct((4, s, PW), F32), jax.ShapeDtypeStruct((s, D), BF16)),
        scratch_shapes=[pltpu.VMEM((tm, D), BF16)],
        compiler_params=_params(("parallel", "arbitrary")),
    )(x, gn, shift, scale, wg)


def _shift_down(prev8, cur, d):
    t = cur.shape[0]
    ext = jnp.concatenate([prev8, cur], axis=0)
    return pltpu.roll(ext, d, 0)[8:]


def _shift_up(cur, next8, d):
    t = cur.shape[0]
    ext = jnp.concatenate([cur, next8], axis=0)
    return pltpu.roll(ext, t + 8 - d, 0)[:t]


def _rnn_gates(xr, prev8, cw, cb, wa, ba, wx, bx, lam, reset):
    xc = cw[3:4] * xr + cb
    for d in (1, 2, 3):
        xc = xc + cw[3 - d:4 - d] * _shift_down(prev8, xr, d)
    xcb = xc.astype(BF16)
    r = _sigmoid(_dot(xcb, wa.astype(BF16)) + ba)
    ig = _sigmoid(_dot(xcb, wx.astype(BF16)) + bx)
    nl = -lam
    sp = jnp.maximum(nl, 0.0) + jnp.log1p(jnp.exp(-jnp.abs(nl)))
    log_a = (-LRU_C * r) * sp
    a = jnp.where(reset, 0.0, jnp.exp(log_a))
    mult = jnp.where(reset, 1.0, jnp.sqrt(1.0 - jnp.exp(2.0 * log_a)))
    return xc, r, ig, sp, a, mult


def _scan_down(a, b, t):
    rows = lax.broadcasted_iota(jnp.int32, a.shape, 0)
    d = 1
    while d < t:
        m = rows >= d
        a_s = pltpu.roll(a, d, 0)
        b_s = pltpu.roll(b, d, 0)
        b = jnp.where(m, a * b_s + b, b)
        a = jnp.where(m, a * a_s, a)
        d *= 2
    return a, b


def _scan_up(a, b, t):
    rows = lax.broadcasted_iota(jnp.int32, a.shape, 0)
    d = 1
    while d < t:
        m = rows < t - d
        a_s = pltpu.roll(a, t - d, 0)
        b_s = pltpu.roll(b, t - d, 0)
        b = jnp.where(m, a * b_s + b, b)
        a = jnp.where(m, a * a_s, a)
        d *= 2
    return a, b


def _rnn_fwd(p, pos, conv_w, conv_b, w_a, b_a, w_x, b_x, lam, tt=256):
    s = p.shape[1]
    nt = s // tt

    def body(xr_ref, z_ref, pos_ref, cw_ref, cb_ref, wa_ref, ba_ref, wx_ref, bx_ref, lam_ref,
             hr_ref, gr_ref, xprev, hprev):
        @pl.when(pl.program_id(1) == 0)
        def _():
            xprev[...] = jnp.zeros_like(xprev)
            hprev[...] = jnp.zeros_like(hprev)

        xr = xr_ref[0]
        z = z_ref[0]
        reset = pos_ref[...] == 0
        xc, r, ig, sp, a, mult = _rnn_gates(xr, xprev[...], cw_ref[...], cb_ref[...], wa_ref[0], ba_ref[0],
                                            wx_ref[0], bx_ref[0], lam_ref[...], reset)
        bx = mult * ig * xc
        a_cum, h0 = _scan_down(a, bx, tt)
        h = a_cum * hprev[0:1] + h0
        xprev[...] = xr[tt - 8:]
        hprev[...] = jnp.broadcast_to(h[tt - 1:tt], (8, DH))
        hr_ref[...] = h
        gr_ref[...] = (h * (z * _sigmoid(z))).astype(BF16)

    head_row = lambda hh, t: (0, hh)
    return pl.pallas_call(
        body, name="rnn_fwd", grid=(H, nt),
        in_specs=[pl.BlockSpec((1, tt, DH), lambda hh, t: (0, t, hh)),
                  pl.BlockSpec((1, tt, DH), lambda hh, t: (0, t, H + hh)),
                  pl.BlockSpec((tt, 1), lambda hh, t: (t, 0)),
                  pl.BlockSpec((4, DH), head_row), pl.BlockSpec((1, DH), head_row),
                  pl.BlockSpec((1, DH, DH), lambda hh, t: (hh, 0, 0)), pl.BlockSpec((1, 1, DH), lambda hh, t: (hh, 0, 0)),
                  pl.BlockSpec((1, DH, DH), lambda hh, t: (hh, 0, 0)), pl.BlockSpec((1, 1, DH), lambda hh, t: (hh, 0, 0)),
                  pl.BlockSpec((1, DH), head_row)],
        out_specs=(pl.BlockSpec((tt, DH), lambda hh, t: (t, hh)), pl.BlockSpec((tt, DH), lambda hh, t: (t, hh))),
        out_shape=(jax.ShapeDtypeStruct((s, D), F32), jax.ShapeDtypeStruct((s, D), BF16)),
        scratch_shapes=[pltpu.VMEM((8, DH), F32), pltpu.VMEM((8, DH), F32)],
        compiler_params=_params(("parallel", "arbitrary")),
    )(p, p, pos, conv_w, conv_b, w_a, b_a, w_x, b_x, lam)


def _rnn_bwd(p, hr, dgr, pos, conv_w, conv_b, w_a, b_a, w_x, b_x, lam, tt=256):
    s = p.shape[1]
    nt = s // tt
    t8 = tt // 8

    def body(xr_ref, z_ref, xp_ref, hr_ref, hp_ref, dg_ref, pos_ref, cw_ref, cb_ref, wa_ref, ba_ref, wx_ref, bx_ref,
             lam_ref, dxr_ref, dz_ref, gwa_ref, gba_ref, gwx_ref, gbx_ref, glam_ref, gcw_ref, gcb_ref,
             a_next, g_next, dxc_next):
        t = pl.program_id(1)
        has_prev = t < nt - 1

        @pl.when(t == 0)
        def _():
            a_next[...] = jnp.zeros_like(a_next)
            g_next[...] = jnp.zeros_like(g_next)
            dxc_next[...] = jnp.zeros_like(dxc_next)
            gwa_ref[...] = jnp.zeros_like(gwa_ref)
            gba_ref[...] = jnp.zeros_like(gba_ref)
            gwx_ref[...] = jnp.zeros_like(gwx_ref)
            gbx_ref[...] = jnp.zeros_like(gbx_ref)
            glam_ref[...] = jnp.zeros_like(glam_ref)
            gcw_ref[...] = jnp.zeros_like(gcw_ref)
            gcb_ref[...] = jnp.zeros_like(gcb_ref)

        xr = xr_ref[0]
        z = z_ref[0]
        hr_blk = hr_ref[...]
        dg = dg_ref[...]
        xprev = jnp.where(has_prev, xp_ref[0], 0.0)
        hprev8 = jnp.where(has_prev, hp_ref[...], 0.0)
        reset = pos_ref[...] == 0
        cw = cw_ref[...]
        wa = wa_ref[0]
        wx = wx_ref[0]
        lam_v = lam_ref[...]
        xc, r, ig, sp, a, mult = _rnn_gates(xr, xprev, cw, cb_ref[...], wa, ba_ref[0], wx, bx_ref[0], lam_v, reset)

        sz = _sigmoid(z)
        dh = dg * (z * sz)
        dz_ref[...] = (dg * hr_blk * (sz * (1.0 + z * (1.0 - sz)))).astype(BF16)

        an = _shift_up(a, a_next[...], 1)
        a_cum, g0 = _scan_up(an, dh, tt)
        g = g0 + a_cum * g_next[0:1]
        a_next[...] = jnp.broadcast_to(a[0:1], (8, DH))
        g_next[...] = jnp.broadcast_to(g[0:1], (8, DH))

        hm1 = _shift_down(hprev8, hr_blk, 1)
        da = g * hm1
        dmult = g * (ig * xc)
        di = g * (mult * xc)
        dxc = g * (mult * ig)
        dla = jnp.where(reset, 0.0, da * a - dmult * (a * a) / mult)
        dr = dla * (-LRU_C * sp)
        dsp = _colsum(dla * (-LRU_C * r))
        glam_ref[0] += dsp * (-_sigmoid(-lam_v))
        dpa = dr * r * (1.0 - r)
        dpx = di * ig * (1.0 - ig)
        dpab = dpa.astype(BF16)
        dpxb = dpx.astype(BF16)
        dxc = dxc + _dot_nt(dpab, wa.astype(BF16)) + _dot_nt(dpxb, wx.astype(BF16))
        xcb = xc.astype(BF16)
        gwa_ref[0] += _dot_tn(xcb, dpab)
        gwx_ref[0] += _dot_tn(xcb, dpxb)
        gba_ref[0] += _colsum(dpa)
        gbx_ref[0] += _colsum(dpx)

        dxr = cw[3:4] * dxc
        for d in (1, 2, 3):
            dxr = dxr + cw[3 - d:4 - d] * _shift_up(dxc, dxc_next[...], d)
        dxr_ref[...] = dxr.astype(BF16)
        dxc_next[...] = dxc[0:8]
        gcb_ref[0] += _colsum(dxc)
        gcw_ref[0, 3:4, :] += _colsum(xr * dxc)
        for d in (1, 2, 3):
            gcw_ref[0, 3 - d:4 - d, :] += _colsum(_shift_down(xprev, xr, d) * dxc)

    rt = lambda t: nt - 1 - t
    prev8 = lambda t: jnp.maximum(rt(t) * t8 - 1, 0)
    head_row = lambda hh, t: (0, hh)
    hsm = lambda hh, t: (hh, 0, 0)
    return pl.pallas_call(
        body, name="rnn_bwd", grid=(H, nt),
        in_specs=[pl.BlockSpec((1, tt, DH), lambda hh, t: (0, rt(t), hh)),
                  pl.BlockSpec((1, tt, DH), lambda hh, t: (0, rt(t), H + hh)),
                  pl.BlockSpec((1, 8, DH), lambda hh, t: (0, prev8(t), hh)),
                  pl.BlockSpec((tt, DH), lambda hh, t: (rt(t), hh)),
                  pl.BlockSpec((8, DH), lambda hh, t: (prev8(t), hh)),
                  pl.BlockSpec((tt, DH), lambda hh, t: (rt(t), hh)),
                  pl.BlockSpec((tt, 1), lambda hh, t: (rt(t), 0)),
                  pl.BlockSpec((4, DH), head_row), pl.BlockSpec((1, DH), head_row),
                  pl.BlockSpec((1, DH, DH), hsm), pl.BlockSpec((1, 1, DH), hsm),
                  pl.BlockSpec((1, DH, DH), hsm), pl.BlockSpec((1, 1, DH), hsm),
                  pl.BlockSpec((1, DH), head_row)],
        out_specs=(pl.BlockSpec((tt, DH), lambda hh, t: (rt(t), hh)), pl.BlockSpec((tt, DH), lambda hh, t: (rt(t), hh)),
                   pl.BlockSpec((1, DH, DH), hsm), pl.BlockSpec((1, 1, DH), hsm),
                   pl.BlockSpec((1, DH, DH), hsm), pl.BlockSpec((1, 1, DH), hsm),
                   pl.BlockSpec((1, 1, DH), hsm), pl.BlockSpec((1, 4, DH), hsm), pl.BlockSpec((1, 1, DH), hsm)),
        out_shape=(jax.ShapeDtypeStruct((s, D), BF16), jax.ShapeDtypeStruct((s, D), BF16),
                   jax.ShapeDtypeStruct((H, DH, DH), F32), jax.ShapeDtypeStruct((H, 1, DH), F32),
                   jax.ShapeDtypeStruct((H, DH, DH), F32), jax.ShapeDtypeStruct((H, 1, DH), F32),
                   jax.ShapeDtypeStruct((H, 1, DH), F32), jax.ShapeDtypeStruct((H, 4, DH), F32),
                   jax.ShapeDtypeStruct((H, 1, DH), F32)),
        scratch_shapes=[pltpu.VMEM((8, DH), F32), pltpu.VMEM((8, DH), F32), pltpu.VMEM((8, DH), F32)],
        compiler_params=_params(("parallel", "arbitrary")),
    )(p, p, p, hr, hr, dgr, pos, conv_w, conv_b, w_a, b_a, w_x, b_x, lam)


def _rope(t, c, sa, sb):
    return t * c + pltpu.roll(t, DH - ROT // 2, 1) * sa + pltpu.roll(t, ROT // 2, 1) * sb


def _rope_bwd(g, c, sa, sb):
    return g * c + pltpu.roll(g * sa, ROT // 2, 1) + pltpu.roll(g * sb, DH - ROT // 2, 1)


def _unit_bases(gi, u):
    dil = DILATIONS[gi]
    if dil == 1:
        return u * UB, SPAN + (u - 1) * UB, u == 0
    if dil == 4:
        blk, r = u // 4, u % 4
        return blk * 4 * UB + r, SPAN + (blk - 1) * 4 * UB + r, blk == 0
    return u, u, True


def _band_mask(first_in_span, has_prev):
    qi = lax.broadcasted_iota(jnp.int32, (UB, 2 * UB), 0)
    ki = lax.broadcasted_iota(jnp.int32, (UB, 2 * UB), 1)
    dist = UB + qi - ki
    band = (dist >= 0) & (dist <= UB)
    return band & ((ki >= UB) | jnp.logical_not(first_in_span) | has_prev)


def _attn_fwd(p, rc, rsa, rsb):
    s = p.shape[1]
    ns = s // SPAN
    nunit = SPAN // UB

    def body(q_ref, k_ref, v_ref, z_ref, c_ref, sa_ref, sb_ref, o_ref, lse_ref, ga_ref,
             qr, kf, vf, acc, mm, ll):
        n = pl.program_id(1)

        @pl.when(n == 0)
        def _():
            kf[0:SPAN] = jnp.zeros((SPAN, DH), F32)
            vf[0:SPAN] = jnp.zeros((SPAN, DH), F32)

        c, sa, sb = c_ref[...], sa_ref[...], sb_ref[...]
        qr[...] = _rope(q_ref[0], c, sa, sb)
        kf[SPAN:] = _rope(k_ref[0], c, sa, sb)
        vf[SPAN:] = v_ref[0]
        has_prev = n > 0

        for gi, dil in enumerate(DILATIONS):
            def unit(u, carry, gi=gi, dil=dil):
                qb0, kb0, first = _unit_bases(gi, u)
                qsl = pl.ds(qb0, UB, stride=dil) if dil > 1 else pl.ds(pl.multiple_of(qb0, UB), UB)
                ksl = pl.ds(kb0, 2 * UB, stride=dil) if dil > 1 else pl.ds(pl.multiple_of(kb0, UB), 2 * UB)
                qb = qr[qsl, :].astype(BF16)
                kb = kf[ksl, :].astype(BF16)
                vb = vf[ksl, :].astype(BF16)
                sc = _dot_nt(qb, kb) * SCALE
                sc = jnp.where(_band_mask(first, has_prev), sc, NEG)
                m = jnp.max(sc, axis=-1, keepdims=True)
                pr = jnp.exp(sc - m)
                l = jnp.sum(pr, axis=-1, keepdims=True)
                acc[gi, qsl, :] = _dot(pr.astype(BF16), vb)
                mm[gi, qsl, :] = jnp.broadcast_to(m, (UB, DH))
                ll[gi, qsl, :] = jnp.broadcast_to(l, (UB, DH))
                return carry

            lax.fori_loop(0, nunit, unit, 0, unroll=UNIT_UNROLL)

        m_all =jnp.maximum(jnp.maximum(mm[0], mm[1]), mm[2])
        num = jnp.zeros((SPAN, DH), F32)
        den = jnp.zeros((SPAN, DH), F32)
        for gi in range(3):
            w = jnp.exp(mm[gi] - m_all)
            num = num + w * acc[gi]
            den = den + w * ll[gi]
        o = num / den
        o_ref[...] = o
        lse_ref[...] = m_all + jnp.log(den)
        z = z_ref[0]
        ga_ref[...] = (o * (z * _sigmoid(z))).astype(BF16)
        kf[0:SPAN] = kf[SPAN:]
        vf[0:SPAN] = vf[SPAN:]

    blk = lambda piece, off: pl.BlockSpec((1, SPAN, DH), lambda hh, n: (piece, n, off + hh))
    tab = pl.BlockSpec((SPAN, DH), lambda hh, n: (n, 0))
    outb = pl.BlockSpec((SPAN, DH), lambda hh, n: (n, hh))
    return pl.pallas_call(
        body, name="attn_fwd", grid=(H, ns),
        in_specs=[blk(1, 0), blk(1, H), blk(2, 0), blk(2, H), tab, tab, tab],
        out_specs=(outb, outb, outb),
        out_shape=(jax.ShapeDtypeStruct((s, D), F32), jax.ShapeDtypeStruct((s, D), F32),
                   jax.ShapeDtypeStruct((s, D), BF16)),
        scratch_shapes=[pltpu.VMEM((SPAN, DH), F32), pltpu.VMEM((2 * SPAN, DH), F32), pltpu.VMEM((2 * SPAN, DH), F32),
                        pltpu.VMEM((3, SPAN, DH), F32), pltpu.VMEM((3, SPAN, DH), F32), pltpu.VMEM((3, SPAN, DH), F32)],
        compiler_params=_params(("parallel", "arbitrary")),
    )(p, p, p, p, rc, rsa, rsb)


def _attn_bwd(p, o, lse, dga, rc, rsa, rsb):
    s = p.shape[1]
    ns = s // SPAN
    nunit = SPAN // UB

    def body(q_ref, k_ref, kp_ref, v_ref, vp_ref, z_ref, c_ref, sa_ref, sb_ref, cp_ref, sap_ref, sbp_ref,
             o_ref, lse_ref, dg_ref, dq_ref, dk_ref, dv_ref, dz_ref,
             qr, kf, vf, dof, dlt, dqa, dkf, dvf):
        step = pl.program_id(1)
        n = ns - 1 - step
        has_prev = n > 0

        @pl.when(step == 0)
        def _():
            dkf[...] = jnp.zeros_like(dkf)
            dvf[...] = jnp.zeros_like(dvf)

        @pl.when(step > 0)
        def _():
            dkf[SPAN:] = dkf[0:SPAN]
            dvf[SPAN:] = dvf[0:SPAN]
            dkf[0:SPAN] = jnp.zeros((SPAN, DH), F32)
            dvf[0:SPAN] = jnp.zeros((SPAN, DH), F32)

        c, sa, sb = c_ref[...], sa_ref[...], sb_ref[...]
        qr[...] = _rope(q_ref[0], c, sa, sb)
        kf[SPAN:] = _rope(k_ref[0], c, sa, sb)
        vf[SPAN:] = v_ref[0]
        kf[0:SPAN] = jnp.where(has_prev, _rope(kp_ref[0], cp_ref[...], sap_ref[...], sbp_ref[...]), 0.0)
        vf[0:SPAN] = jnp.where(has_prev, vp_ref[0], 0.0)
        z = z_ref[0]
        sz = _sigmoid(z)
        dg = dg_ref[...]
        ov = o_ref[...]
        do = dg * (z * sz)
        dz_ref[...] = (dg * ov * (sz * (1.0 + z * (1.0 - sz)))).astype(BF16)
        dof[...] = do
        dlt[...] = jnp.broadcast_to(jnp.sum(do * ov, axis=-1, keepdims=True), (SPAN, DH))
        dqa[...] = jnp.zeros_like(dqa)

        for gi, dil in enumerate(DILATIONS):
            def unit(u, carry, gi=gi, dil=dil):
                qb0, kb0, first = _unit_bases(gi, u)
                qsl = pl.ds(qb0, UB, stride=dil) if dil > 1 else pl.ds(pl.multiple_of(qb0, UB), UB)
                ksl = pl.ds(kb0, 2 * UB, stride=dil) if dil > 1 else pl.ds(pl.multiple_of(kb0, UB), 2 * UB)
                qb = qr[qsl, :].astype(BF16)
                kb = kf[ksl, :].astype(BF16)
                vb = vf[ksl, :].astype(BF16)
                dob = dof[qsl, :].astype(BF16)
                lse_b = lse_ref[qsl, :]
                dl_b = dlt[qsl, :]
                sc = _dot_nt(qb, kb) * SCALE
                pr = jnp.exp(sc - jnp.concatenate([lse_b, lse_b], axis=1))
                pr = jnp.where(_band_mask(first, has_prev), pr, 0.0)
                prb = pr.astype(BF16)
                dvf[ksl, :] += _dot_tn(prb, dob)
                dp = _dot_nt(dob, vb)
                ds = (pr * (dp - jnp.concatenate([dl_b, dl_b], axis=1)) * SCALE).astype(BF16)
                dqa[qsl, :] += _dot(ds, kb)
                dkf[ksl, :] += _dot_tn(ds, qb)
                return carry

            lax.fori_loop(0, nunit, unit, 0, unroll=UNIT_UNROLL)

        dq_ref[...] = _rope_bwd(dqa[...], c, sa, sb).astype(BF16)
        dk_ref[...] = _rope_bwd(dkf[SPAN:], c, sa, sb).astype(BF16)
        dv_ref[...] = dvf[SPAN:].astype(BF16)

    rn = lambda n: ns - 1 - n
    pn = lambda n: jnp.maximum(ns - 2 - n, 0)
    blk = lambda piece, off: pl.BlockSpec((1, SPAN, DH), lambda hh, n: (piece, rn(n), off + hh))
    blkp = lambda piece, off: pl.BlockSpec((1, SPAN, DH), lambda hh, n: (piece, pn(n), off + hh))
    tab = pl.BlockSpec((SPAN, DH), lambda hh, n: (rn(n), 0))
    tabp = pl.BlockSpec((SPAN, DH), lambda hh, n: (pn(n), 0))
    io = pl.BlockSpec((SPAN, DH), lambda hh, n: (rn(n), hh))
    return pl.pallas_call(
        body, name="attn_bwd", grid=(H, ns),
        in_specs=[blk(1, 0), blk(1, H), blkp(1, H), blk(2, 0), blkp(2, 0), blk(2, H),
                  tab, tab, tab, tabp, tabp, tabp, io, io, io],
        out_specs=(io, io, io, io),
        out_shape=tuple(jax.ShapeDtypeStruct((s, D), BF16) for _ in range(4)),
        scratch_shapes=[pltpu.VMEM((SPAN, DH), F32), pltpu.VMEM((2 * SPAN, DH), F32), pltpu.VMEM((2 * SPAN, DH), F32),
                        pltpu.VMEM((SPAN, DH), F32), pltpu.VMEM((SPAN, DH), F32), pltpu.VMEM((SPAN, DH), F32),
                        pltpu.VMEM((2 * SPAN, DH), F32), pltpu.VMEM((2 * SPAN, DH), F32)],
        compiler_params=_params(("parallel", "arbitrary")),
    )(p, p, p, p, p, p, rc, rsa, rsb, rc, rsa, rsb, o, lse, dga)


def _tail(gr, ga, p, x, tgt, w3, b_gate, gate, g_final, tm=256):
    s = x.shape[0]
    nt = s // tm

    def body(gr_ref, ga_ref, pr_ref, pa_ref, x_ref, t_ref, bg_ref, gate_ref, gf_ref, w_hbm,
             dgr_ref, dga_ref, dc_ref, dx2_ref, vec_ref, go_hbm, w_s, acc_s, sem):
        i = pl.program_id(0)

        @pl.when(i == 0)
        def _():
            cp = pltpu.make_async_copy(w_hbm, w_s, sem.at[12])
            cp.start()
            acc_s[...] = jnp.zeros_like(acc_s)
            vec_ref[...] = jnp.zeros_like(vec_ref)
            cp.wait()

        grb = gr_ref[...]
        gab = ga_ref[...]
        bg = bg_ref[...]
        gate_v = gate_ref[...]
        gf = gf_ref[...]
        y_r = _dot(grb, w_s[0])
        y_a = _dot(gab, w_s[1])
        sr = _sigmoid(pr_ref[0] + bg[:, :D])
        sa = _sigmoid(pa_ref[0] + bg[:, D:])
        mb = (sr * y_r + sa * y_a).astype(BF16)
        u = _dot(mb, w_s[2])
        x2 = x_ref[...] + gate_v * u
        rstd = lax.rsqrt(jnp.mean(x2 * x2, axis=-1, keepdims=True) + EPS)
        xh = x2 * rstd
        e = xh * gf - t_ref[...]
        dy = e * (1.0 / D)
        dyg = dy * gf
        dx2 = rstd * (dyg - xh * jnp.mean(dyg * xh, axis=-1, keepdims=True))
        dx2_ref[...] = dx2
        dub = (dx2 * gate_v).astype(BF16)
        dm = _dot_nt(dub, w_s[2])
        dyr = (dm * sr).astype(BF16)
        dya = (dm * sa).astype(BF16)
        dpr = dm * y_r * (sr * (1.0 - sr))
        dpa = dm * y_a * (sa * (1.0 - sa))
        dc_ref[:, :D] = dpr.astype(BF16)
        dc_ref[:, D:] = dpa.astype(BF16)
        dgr_ref[...] = _dot_nt(dyr, w_s[0])
        dga_ref[...] = _dot_nt(dya, w_s[1])
        acc_s[0] += _dot_tn(grb, dyr)
        acc_s[1] += _dot_tn(gab, dya)
        acc_s[2] += _dot_tn(mb, dub)
        vec_ref[0:1, :] += _colsum(dy * xh)
        vec_ref[1:2, :] += _colsum(dx2 * u)
        vec_ref[2:3, :] += _colsum(dpr)
        vec_ref[3:4, :] += _colsum(dpa)
        vec_ref[4:5, :] += _colsum(e * e)

        @pl.when(i == nt - 1)
        def _():
            vec_ref[4:5, :] = jnp.broadcast_to(jnp.sum(vec_ref[4:5, :]) * (0.5 / D), (1, D))
            cps = []
            for w in range(3):
                for j in range(4):
                    cps.append(pltpu.make_async_copy(acc_s.at[w, pl.ds(256 * j, 256)],
                                                     go_hbm.at[j, pl.ds(256 * w, 256)], sem.at[4 * w + j]))
            for cp in cps:
                cp.start()
            for cp in cps:
                cp.wait()

    rowt = lambda i: (i, 0)
    row = lambda w: pl.BlockSpec((1, w), lambda i: (0, 0))
    any_ = pl.BlockSpec(memory_space=pl.ANY)
    return pl.pallas_call(
        body, name="tail", grid=(nt,),
        in_specs=[pl.BlockSpec((tm, D), rowt), pl.BlockSpec((tm, D), rowt),
                  pl.BlockSpec((1, tm, D), lambda i: (3, i, 0)), pl.BlockSpec((1, tm, D), lambda i: (3, i, 1)),
                  pl.BlockSpec((tm, D), rowt), pl.BlockSpec((tm, D), rowt),
                  row(2 * D), row(D), row(D), any_],
        out_specs=(pl.BlockSpec((tm, D), rowt), pl.BlockSpec((tm, D), rowt), pl.BlockSpec((tm, 2 * D), rowt),
                   pl.BlockSpec((tm, D), rowt), pl.BlockSpec((8, D), lambda i: (0, 0)), any_),
        out_shape=(jax.ShapeDtypeStruct((s, D), F32), jax.ShapeDtypeStruct((s, D), F32),
                   jax.ShapeDtypeStruct((s, 2 * D), BF16), jax.ShapeDtypeStruct((s, D), F32),
                   jax.ShapeDtypeStruct((8, D), F32), jax.ShapeDtypeStruct((4, 768, D), F32)),
        scratch_shapes=[pltpu.VMEM((3, D, D), BF16), pltpu.VMEM((3, D, D), F32), pltpu.SemaphoreType.DMA((13,))],
        compiler_params=_params(("arbitrary",)),
    )(gr, ga, p, p, x, tgt, b_gate, gate, g_final, w3)


def _pieces_steps(pieces):
    out, s0 = [], 0
    for a in pieces:
        n = a.shape[1] // D
        out.append((s0, n))
        s0 += n
    return out, s0


def _inproj_bwd_x(pieces, wg, x, dx2, gn, scale, sums, tm=512):
    s = x.shape[0]
    steps, nk = _pieces_steps(pieces)
    npc = PW // D
    np_ = len(pieces)
    na = len(sums)
    ni = s // tm

    def body(*refs):
        d_refs = refs[:np_]
        w_ref, x_ref, dx2_ref, gn_ref, sc_ref = refs[np_:np_ + 5]
        q_refs = refs[np_ + 5:np_ + 5 + na]
        gx_ref, vec_ref = refs[np_ + 5 + na:np_ + 7 + na]
        r_refs = refs[np_ + 7 + na:np_ + 7 + 2 * na]
        acc, ss, rs = refs[np_ + 7 + 2 * na:]
        i, k = pl.program_id(0), pl.program_id(1)

        def scatter_copies():
            cx, cy, cc = _coords()
            j = 2 * cx + cy
            cps = []
            for t, (q, r) in enumerate(zip(q_refs, r_refs)):
                for e, (kx, ky) in enumerate(((1, 0), (0, 1), (1, 1))):
                    cps.append(_rcopy(q.at[j ^ (2 * kx + ky)], r.at[e], ss.at[3 * t + e], rs.at[3 * t + e],
                                      (_flip(cx, kx), _flip(cy, ky), cc)))
            return cps

        @pl.when(k == 0)
        def _():
            acc[...] = jnp.zeros_like(acc)

        @pl.when((i == 0) & (k == 0))
        def _():
            vec_ref[...] = jnp.zeros_like(vec_ref)
            for cp in scatter_copies():
                cp.start()

        @pl.when((i == ni - 1) & (k == nk - 1))
        def _():
            for cp in scatter_copies():
                cp.wait()

        for (s0, n), d_ref in zip(steps, d_refs):
            @pl.when((k >= s0) & (k < s0 + n))
            def _(d_ref=d_ref):
                acc[...] += _dot_nt(d_ref[...], w_ref[0])

        @pl.when(k == nk - 1)
        def _():
            dh = acc[...]
            xt = x_ref[...]
            rstd = lax.rsqrt(jnp.mean(xt * xt, axis=-1, keepdims=True) + EPS)
            xh = xt * rstd
            gn_v = gn_ref[...]
            sc1 = 1.0 + sc_ref[...]
            dhx = dh * xh
            vec_ref[0:1, :] += _colsum(dh)
            vec_ref[1:2, :] += _colsum(dhx) * gn_v
            vec_ref[2:3, :] += _colsum(dhx) * sc1
            dxh = dh * (gn_v * sc1)
            gx_ref[...] = rstd * (dxh - xh * jnp.mean(dxh * xh, axis=-1, keepdims=True)) + dx2_ref[...]

    def piece_spec(s0, n):
        return pl.BlockSpec((tm, D), lambda i, k: (i, jnp.clip(k - s0, 0, n - 1)))

    rowt = lambda i, k: (i, 0)
    row = pl.BlockSpec((1, D), lambda i, k: (0, 0))
    any_ = pl.BlockSpec(memory_space=pl.ANY)
    outs = pl.pallas_call(
        body, name="inproj_bwd_x", grid=(ni, nk),
        in_specs=[piece_spec(s0, n) for s0, n in steps] +
                 [pl.BlockSpec((1, D, D), lambda i, k: (k // npc, 0, k % npc)),
                  pl.BlockSpec((tm, D), rowt), pl.BlockSpec((tm, D), rowt), row, row] + [any_] * na,
        out_specs=(pl.BlockSpec((tm, D), rowt), pl.BlockSpec((8, D), lambda i, k: (0, 0))) + (any_,) * na,
        out_shape=(jax.ShapeDtypeStruct((s, D), F32), jax.ShapeDtypeStruct((8, D), F32)) +
                  tuple(jax.ShapeDtypeStruct((3,) + q.shape[1:], q.dtype) for q in sums),
        scratch_shapes=[pltpu.VMEM((tm, D), F32), pltpu.SemaphoreType.DMA((3 * na,)), pltpu.SemaphoreType.DMA((3 * na,))],
        compiler_params=_params(("arbitrary", "arbitrary")),
    )(*pieces, wg, x, dx2, gn, scale, *sums)
    return outs[0], outs[1], outs[2:]


def _inproj_bwd_w(pieces, hbf, tk=512):
    s = hbf.shape[0]
    steps, nk = _pieces_steps(pieces)
    npc = PW // D
    ns = s // tk
    np_ = len(pieces)

    def body(*refs):
        d_refs = refs[:np_]
        h_ref, g_ref = refs[np_:]
        cb, k = pl.program_id(0), pl.program_id(1)

        @pl.when(k == 0)
        def _():
            g_ref[...] = jnp.zeros_like(g_ref)

        for (s0, n), d_ref in zip(steps, d_refs):
            @pl.when((cb >= s0) & (cb < s0 + n))
            def _(d_ref=d_ref):
                g_ref[0] += _dot_tn(h_ref[...], d_ref[...])

    def piece_spec(s0, n):
        def imap(cb, k):
            active = (cb >= s0) & (cb < s0 + n)
            return (jnp.where(active, k, 0), jnp.clip(cb - s0, 0, n - 1))
        return pl.BlockSpec((tk, D), imap)

    return pl.pallas_call(
        body, name="inproj_bwd_w", grid=(nk, ns),
        in_specs=[piece_spec(s0, n) for s0, n in steps] + [pl.BlockSpec((tk, D), lambda cb, k: (k, 0))],
        out_specs=pl.BlockSpec((1, D, D), lambda cb, k: (cb // npc, 0, cb % npc)),
        out_shape=jax.ShapeDtypeStruct((4, D, PW), F32),
        compiler_params=_params(("parallel", "arbitrary")),
    )(*pieces, hbf)


D2D_CHUNK_BYTES = 512 * 1024


def _chunk_rows(a):
    return max(8, D2D_CHUNK_BYTES // (a.shape[-1] * a.dtype.itemsize))


def _pair_exchange(arrs):
    na = len(arrs)
    chunks = []
    for t, a in enumerate(arrs):
        hr = a.shape[1] // 2
        cr = _chunk_rows(a)
        chunks += [(t, j, r0, cr) for j in range(a.shape[0]) for r0 in range(0, hr, cr)]
    nch = len(chunks)

    def body(*refs):
        a_refs = refs[:na]
        rb_refs = refs[na:2 * na]
        ss, rs = refs[2 * na:]
        x, y, c = _coords()
        sib = (x, y, 1 - c)
        rcs = []
        for n, (t, j, r0, cr) in enumerate(chunks):
            hr = a_refs[t].shape[1] // 2
            rc = _rcopy(a_refs[t].at[j, pl.ds((1 - c) * hr + r0, cr), :], rb_refs[t].at[j, pl.ds(r0, cr), :],
                        ss.at[n], rs.at[n], sib)
            rc.start()
            rcs.append(rc)
        for rc in rcs:
            rc.wait_recv()
        for rc in rcs:
            rc.wait_send()

    any_ = pl.BlockSpec(memory_space=pl.ANY)
    halves = [jax.ShapeDtypeStruct((a.shape[0], a.shape[1] // 2, a.shape[2]), a.dtype) for a in arrs]
    return pl.pallas_call(
        body, name="pair_exchange",
        out_shape=tuple(halves),
        in_specs=[any_] * na, out_specs=tuple([any_] * na),
        scratch_shapes=[pltpu.SemaphoreType.DMA((nch,)), pltpu.SemaphoreType.DMA((nch,))],
        compiler_params=_params(),
    )(*arrs)


def _pair_swap(arrs):
    na = len(arrs)
    chunks = []
    for t, a in enumerate(arrs):
        cr = _chunk_rows(a)
        chunks += [(t, r0, cr) for r0 in range(0, a.shape[0], cr)]
    nch = len(chunks)

    def body(*refs):
        a_refs = refs[:na]
        o_refs = refs[na:2 * na]
        ss, rs = refs[2 * na:]
        x, y, c = _coords()
        sib = (x, y, 1 - c)
        rcs = []
        for n, (t, r0, cr) in enumerate(chunks):
            rows = pl.ds(r0, cr)
            rc = _rcopy(a_refs[t].at[rows, :], o_refs[t].at[rows, :], ss.at[n], rs.at[n], sib)
            rc.start()
            rcs.append(rc)
        for rc in rcs:
            rc.wait_recv()
        for rc in rcs:
            rc.wait_send()

    any_ = pl.BlockSpec(memory_space=pl.ANY)
    return pl.pallas_call(
        body, name="pair_swap",
        out_shape=tuple(jax.ShapeDtypeStruct(a.shape, a.dtype) for a in arrs),
        in_specs=[any_] * na, out_specs=tuple([any_] * na),
        scratch_shapes=[pltpu.SemaphoreType.DMA((nch,)), pltpu.SemaphoreType.DMA((nch,))],
        compiler_params=_params(),
    )(*arrs)


def _add_half(full, rb, core, tr):
    n, r, cdim = full.shape
    nb = r // 2 // tr

    def body(c_ref, a_ref, b_ref, o_ref):
        o_ref[...] = a_ref[...] + b_ref[...]

    mine = pl.BlockSpec((1, tr, cdim), lambda i, j, c_ref: (i, c_ref[0] * nb + j, 0))
    spec = pl.BlockSpec((1, tr, cdim), lambda i, j, c_ref: (i, j, 0))
    return pl.pallas_call(
        body, name="add_half",
        grid_spec=pltpu.PrefetchScalarGridSpec(num_scalar_prefetch=1, grid=(n, nb), in_specs=[mine, spec],
                                               out_specs=spec),
        out_shape=jax.ShapeDtypeStruct(rb.shape, rb.dtype),
        compiler_params=_params(("parallel", "parallel")),
    )(core, full, rb)


def _sum_slots(q, r3, shard, tr):
    _, r, cdim = q.shape

    def body(j_ref, q_ref, r_ref, o_ref):
        o_ref[...] = ((q_ref[0] + r_ref[0]) + r_ref[1]) + r_ref[2]

    return pl.pallas_call(
        body, name="sum_slots",
        grid_spec=pltpu.PrefetchScalarGridSpec(
            num_scalar_prefetch=1, grid=(r // tr,),
            in_specs=[pl.BlockSpec((1, tr, cdim), lambda i, j_ref: (j_ref[0], i, 0)),
                      pl.BlockSpec((3, tr, cdim), lambda i, j_ref: (0, i, 0))],
            out_specs=pl.BlockSpec((tr, cdim), lambda i, j_ref: (i, 0))),
        out_shape=jax.ShapeDtypeStruct((r, cdim), q.dtype),
        compiler_params=_params(("parallel",)),
    )(shard, q, r3)


def _allreduce_small(pack):
    def body(p_ref, out_ref, rbuf, s1, r1, s2, r2):
        me = _my_index()
        chunk = lambda d: pl.ds(pl.multiple_of(d * AR_CHUNK, 8), AR_CHUNK)
        sends = []
        for k in range(1, NDEV):
            cp = _rcopy(p_ref.at[chunk(me ^ k)], rbuf.at[me], s1.at[k - 1], r1.at[k - 1], _peer(k))
            cp.start()
            sends.append(cp)
        rbuf[me] = p_ref[chunk(me), :]
        for k in range(1, NDEV):
            _rcopy(p_ref.at[chunk(me)], rbuf.at[me ^ k], s1.at[k - 1], r1.at[k - 1], _peer(k)).wait_recv()
        tot = rbuf[0]
        for d in range(1, NDEV):
            tot = tot + rbuf[d]
        out_ref[chunk(me), :] = tot
        for k in range(1, NDEV):
            cp = _rcopy(out_ref.at[chunk(me)], out_ref.at[chunk(me)], s2.at[k - 1], r2.at[k - 1], _peer(k))
            cp.start()
            sends.append(cp)
        for k in range(1, NDEV):
            _rcopy(out_ref.at[chunk(me)], out_ref.at[chunk(me ^ k)], s2.at[k - 1], r2.at[k - 1], _peer(k)).wait_recv()
        for cp in sends:
            cp.wait_send()

    vm = pl.BlockSpec(memory_space=pltpu.VMEM)
    return pl.pallas_call(
        body, name="allreduce_small",
        out_shape=jax.ShapeDtypeStruct((AR_ROWS, D), F32),
        in_specs=[vm], out_specs=vm,
        scratch_shapes=[pltpu.VMEM((NDEV, AR_CHUNK, D), F32),
                        pltpu.SemaphoreType.DMA((7,)), pltpu.SemaphoreType.DMA((7,)),
                        pltpu.SemaphoreType.DMA((7,)), pltpu.SemaphoreType.DMA((7,))],
        compiler_params=_params(),
    )(pack)


def _adamw(w, g, m, v, tr):
    r, cdim = w.shape

    def body(w_ref, g_ref, m_ref, v_ref, d_ref, nm_ref, nv_ref):
        gv = g_ref[...]
        nm = B1 * m_ref[...] + (1.0 - B1) * gv
        nv = B2 * v_ref[...] + (1.0 - B2) * (gv * gv)
        m_hat = nm / (1.0 - B1 ** STEP)
        v_hat = nv / (1.0 - B2 ** STEP)
        d_ref[...] = -LR * (m_hat / (jnp.sqrt(v_hat) + ADAM_EPS) + WD * w_ref[...])
        nm_ref[...] = nm
        nv_ref[...] = nv

    spec = pl.BlockSpec((tr, cdim), lambda i: (i, 0))
    sd = jax.ShapeDtypeStruct((r, cdim), F32)
    return pl.pallas_call(
        body, name="adamw", grid=(r // tr,), in_specs=[spec] * 4, out_specs=(spec,) * 3, out_shape=(sd,) * 3,
        compiler_params=_params(("parallel",)),
    )(w, g, m, v)


def _rope_tables(positions):
    inv_freq = ROPE_THETA ** (-jnp.arange(0, ROT, 2, dtype=F32) / ROT)
    ang = positions.astype(F32)[:, None] * inv_freq
    cos, sin = jnp.cos(ang), jnp.sin(ang)
    n = positions.shape[0]
    half = ROT // 2
    rc = jnp.concatenate([cos, cos, jnp.ones((n, DH - ROT), F32)], axis=1)
    rsa = jnp.concatenate([-sin, jnp.zeros((n, DH - half), F32)], axis=1)
    rsb = jnp.concatenate([jnp.zeros((n, half), F32), sin, jnp.zeros((n, DH - ROT), F32)], axis=1)
    return rc, rsa, rsb


def kernel(x, c, positions, g_norm, w_mod, b_mod, w_in, b_gate, conv_w, conv_b, w_a, b_a, w_x, b_x, lam, w_out_rnn, w_out_attn, w_o, g_final, loss_target, m_g_norm, m_w_mod, m_b_mod, m_w_in, m_b_gate, m_conv_w, m_conv_b, m_w_a, m_b_a, m_w_x, m_b_x, m_lam, m_w_out_rnn, m_w_out_attn, m_w_o, m_g_final, v_g_norm, v_w_mod, v_b_mod, v_w_in, v_b_gate, v_conv_w, v_conv_b, v_w_a, v_b_a, v_w_x, v_b_x, v_lam, v_w_out_rnn, v_w_out_attn, v_w_o, v_g_final):
    s = x.shape[1]
    xi = lax.axis_index("x")
    yi = lax.axis_index("y")
    ci = lax.axis_index("c")
    shard = 2 * xi + yi
    x2d = x[0]
    tgt = loss_target[0]
    pos = positions[0]

    c_all, mod4 = _mod_fwd(c, w_mod[0], b_mod.reshape(4, 1, 768))
    mod = mod4.reshape(1, 3 * D)
    shift, scale, gate = mod[:, :D], mod[:, D:2 * D], mod[:, 2 * D:]
    w3_sh = jnp.stack([w_out_rnn[0], w_out_attn[0], w_o[0]]).astype(BF16)
    wsh = jnp.concatenate([w_in[0].astype(BF16), w3_sh.reshape(384, PW)], axis=0)
    wg = lax.dynamic_update_slice(_gather_weights(wsh), wsh[None], (shard, 0, 0))
    w3 = wg[:, D:, :].reshape(4, 3, 256, D).transpose(1, 0, 2, 3).reshape(3, D, D)
    conv_all = _gather_small(conv_w[0])
    conv_full = conv_all[0::2].transpose(1, 0, 2).reshape(4, D)

    p, hbf = _norm_inproj(x2d, g_norm, shift, scale, wg)
    rc, rsa, rsb = _rope_tables(pos)
    pos_col = pos.reshape(s, 1)
    b_a3, b_x3 = b_a.reshape(H, 1, DH), b_x.reshape(H, 1, DH)
    hr, gr = _rnn_fwd(p, pos_col, conv_full, conv_b, w_a[0], b_a3, w_x[0], b_x3, lam)
    o, lse, ga = _attn_fwd(p, rc, rsa, rsb)

    dgr, dga, dc, dx2, vec_t, g_out = _tail(gr, ga, p, x2d, tgt, w3, b_gate, gate, g_final.reshape(1, D))

    dxr, dzr, g_wa, g_ba, g_wx, g_bx, g_lam, g_cw, g_cb = _rnn_bwd(
        p, hr, dgr, pos_col, conv_full, conv_b, w_a[0], b_a3, w_x[0], b_x3, lam)
    dq, dk, dv, dza = _attn_bwd(p, o, lse, dga, rc, rsa, rsb)

    pieces = [dxr, dzr, dq, dk, dv, dza, dc]
    g_win = _inproj_bwd_w(pieces, hbf)

    core = ci.reshape(1)
    shard1 = shard.reshape(1)
    rb_a, rb_b = _pair_exchange([g_win, g_out])
    q_a, q_b = _add_half(g_win, rb_a, core, tr=256), _add_half(g_out, rb_b, core, tr=128)
    grad_x, vec_n, (r_a, r_b) = _inproj_bwd_x(pieces, wg, x2d, dx2, g_norm, scale, [q_a, q_b])
    f_a, f_b = _sum_slots(q_a, r_a, shard1, tr=256), _sum_slots(q_b, r_b, shard1, tr=128)
    s_a, s_b = _pair_swap([f_a, f_b])
    south = ci == 0
    grad_w_in = jnp.where(south, jnp.concatenate([f_a, s_a], axis=0), jnp.concatenate([s_a, f_a], axis=0))
    g3 = jnp.where(south, jnp.concatenate([f_b, s_b], axis=0), jnp.concatenate([s_b, f_b], axis=0)).reshape(3, 256, D)

    dmod_row = jnp.concatenate([vec_n[0:1], vec_n[1:2], vec_t[1:2]], axis=1)
    pack = jnp.concatenate([
        vec_n[2:3],
        dmod_row.reshape(3, D),
        vec_t[2:4],
        g_cb.reshape(1, D),
        g_wa.reshape(128, D),
        g_ba.reshape(1, D),
        g_wx.reshape(128, D),
        g_bx.reshape(1, D),
        g_lam.reshape(1, D),
        vec_t[0:1],
        g_cw.transpose(1, 0, 2).reshape(4, D),
        vec_t[4:5],
        jnp.zeros((AR_ROWS - 272, D), F32)], axis=0)
    red = _allreduce_small(pack)
    loss = red[271, 0]
    grad_w_mod = _mod_bwd(dmod_row.reshape(4, 1, 768), c_all)
    g_conv_sh = lax.dynamic_slice_in_dim(red[267:271], shard * 256, 256, axis=1)

    def small_pack(g_norm_, b_mod_, b_gate_, conv_b_, w_a_, b_a_, w_x_, b_x_, lam_, g_final_, conv_w_):
        return jnp.concatenate([
            g_norm_.reshape(1, D), b_mod_.reshape(3, D), b_gate_.reshape(2, D), conv_b_.reshape(1, D),
            w_a_.reshape(128, D), b_a_.reshape(1, D), w_x_.reshape(128, D), b_x_.reshape(1, D),
            lam_.reshape(1, D), g_final_.reshape(1, D), conv_w_.reshape(1, D),
            jnp.zeros((4, D), F32)], axis=0)

    wp = small_pack(g_norm, b_mod, b_gate, conv_b, w_a, b_a, w_x, b_x, lam, g_final, conv_w)
    mp = small_pack(m_g_norm, m_b_mod, m_b_gate, m_conv_b, m_w_a, m_b_a, m_w_x, m_b_x, m_lam, m_g_final, m_conv_w)
    vp = small_pack(v_g_norm, v_b_mod, v_b_gate, v_conv_b, v_w_a, v_b_a, v_w_x, v_b_x, v_lam, v_g_final, v_conv_w)
    gp = jnp.concatenate([red[0:267], g_conv_sh.reshape(1, D), jnp.zeros((4, D), F32)], axis=0)
    small = _adamw(wp, gp, mp, vp, tr=136)

    def unpack(a):
        return dict(
            g_norm=a[0:1], b_mod=a[1:4].reshape(1, 3 * D), b_gate=a[4:6].reshape(1, 2 * D), conv_b=a[6:7],
            w_a=a[7:135].reshape(1, H, DH, DH), b_a=a[135:136].reshape(1, H, DH),
            w_x=a[136:264].reshape(1, H, DH, DH), b_x=a[264:265].reshape(1, H, DH), lam=a[265:266],
            g_final=a[266].reshape(D), conv_w=a[267:268].reshape(1, 4, 256))

    big_in = _adamw(w_in[0], grad_w_in, m_w_in[0], v_w_in[0], tr=256)
    big_mod = _adamw(w_mod[0], grad_w_mod, m_w_mod[0], v_w_mod[0], tr=256)
    w3f = jnp.concatenate([w_out_rnn[0], w_out_attn[0], w_o[0]], axis=0)
    m3f = jnp.concatenate([m_w_out_rnn[0], m_w_out_attn[0], m_w_o[0]], axis=0)
    v3f = jnp.concatenate([v_w_out_rnn[0], v_w_out_attn[0], v_w_o[0]], axis=0)
    big_out = _adamw(w3f, g3.reshape(768, D), m3f, v3f, tr=256)

    names = ["g_norm", "w_mod", "b_mod", "w_in", "b_gate", "conv_w", "conv_b", "w_a", "b_a", "w_x", "b_x", "lam",
             "w_out_rnn", "w_out_attn", "w_o", "g_final"]
    grads = unpack(gp)
    grads.update(w_mod=grad_w_mod[None], w_in=grad_w_in[None],
                 w_out_rnn=g3[0][None], w_out_attn=g3[1][None], w_o=g3[2][None])
    outs = [grads]
    for idx in range(3):
        d = unpack(small[idx])
        d.update(w_mod=big_mod[idx][None], w_in=big_in[idx][None],
                 w_out_rnn=big_out[idx][0:256][None], w_out_attn=big_out[idx][256:512][None],
                 w_o=big_out[idx][512:768][None])
        outs.append(d)
    flat = [d[n] for d in outs for n in names]
    return (loss, grad_x[None], *flat)
```

```python
import jax
import jax.numpy as jnp
from jax import lax
from jax.experimental import pallas as pl
from jax.experimental.pallas import tpu as pltpu

F32, BF16 = jnp.float32, jnp.bfloat16
MESH = pl.DeviceIdType.MESH
HIGHEST = lax.Precision.HIGHEST

D = 1024
H = 8
DH = 128
PW = 2048
EPS = 1e-6
LRU_C = 8.0
SCALE = DH ** -0.5
NEG = -1e30
SPAN = 2048
UB = 128
DILATIONS = (1, 4, 16)
UNIT_BATCH = 8
UNIT_UNROLL = 8
ROPE_THETA = 500000.0
ROT = 32

LR, B1, B2, ADAM_EPS, WD, STEP = 0.001, 0.9, 0.999, 1e-08, 0.01, 10

WROWS = 1024 + 384
NDEV = 8
AR_ROWS = 320
AR_CHUNK = AR_ROWS // NDEV


def _params(sem=None, vmem_mb=56):
    return pltpu.CompilerParams(dimension_semantics=sem, vmem_limit_bytes=vmem_mb * 2 ** 20)


def _coords():
    return lax.axis_index("x"), lax.axis_index("y"), lax.axis_index("c")


def _flip(v, bit):
    return 1 - v if bit else v


def _peer(k):
    x, y, c = _coords()
    return (_flip(x, (k >> 2) & 1), _flip(y, (k >> 1) & 1), _flip(c, k & 1))


def _my_index():
    x, y, c = _coords()
    return 4 * x + 2 * y + c


def _rcopy(src, dst, ssem, rsem, dev):
    return pltpu.make_async_remote_copy(src_ref=src, dst_ref=dst, send_sem=ssem, recv_sem=rsem,
                                        device_id=dev, device_id_type=MESH)


def _sigmoid(x):
    return jax.nn.sigmoid(x)


def _dot(a, b):
    return jnp.dot(a, b, preferred_element_type=F32)


def _dot_nt(a, b):
    return lax.dot_general(a, b, (((1,), (1,)), ((), ())), preferred_element_type=F32)


def _dot_tn(a, b):
    return lax.dot_general(a, b, (((0,), (0,)), ((), ())), preferred_element_type=F32)


def _colsum(a):
    return jnp.sum(a, axis=0, keepdims=True)


def _mod_fwd(c, w_mod_sh, b_mod4):
    def body(c_ref, w_ref, b_ref, call_ref, mod_ref, rows_ref, cmat_ref, s1, r1, s2, r2):
        x, y, _ = _coords()
        me = _my_index()
        j = 2 * x + y
        call_ref[me] = c_ref[...]
        sends = []
        for k in range(1, NDEV):
            cp = _rcopy(call_ref.at[me], call_ref.at[me], s1.at[k - 1], r1.at[k - 1], _peer(k))
            cp.start()
            sends.append(cp)
        for k in range(1, NDEV):
            pk = me ^ k
            _rcopy(call_ref.at[pk], call_ref.at[pk], s1.at[k - 1], r1.at[k - 1], _peer(k)).wait_recv()
        for b in range(NDEV):
            cmat_ref[pl.ds(b, 1), :] = call_ref[b]
        cm = cmat_ref[...]
        act = cm * _sigmoid(cm)
        mp = jnp.dot(act, w_ref[...], preferred_element_type=F32, precision=HIGHEST) + b_ref[j]
        for b in range(NDEV):
            rows_ref[b] = mp[b:b + 1]
        mod_ref[j] = rows_ref[me]
        for q, k in enumerate((2, 4, 6)):
            cp = _rcopy(rows_ref.at[me ^ k], mod_ref.at[j], s2.at[q], r2.at[q], _peer(k))
            cp.start()
            sends.append(cp)
        for q, k in enumerate((2, 4, 6)):
            jq = j ^ (k >> 1)
            _rcopy(rows_ref.at[me], mod_ref.at[jq], s2.at[q], r2.at[q], _peer(k)).wait_recv()
        for cp in sends:
            cp.wait_send()

    vm = pl.BlockSpec(memory_space=pltpu.VMEM)
    return pl.pallas_call(
        body, name="mod_fwd",
        out_shape=(jax.ShapeDtypeStruct((NDEV, 1, D), F32), jax.ShapeDtypeStruct((4, 1, 768), F32)),
        in_specs=[vm, vm, vm], out_specs=(vm, vm),
        scratch_shapes=[pltpu.VMEM((NDEV, 1, 768), F32), pltpu.VMEM((NDEV, D), F32),
                        pltpu.SemaphoreType.DMA((7,)), pltpu.SemaphoreType.DMA((7,)),
                        pltpu.SemaphoreType.DMA((3,)), pltpu.SemaphoreType.DMA((3,))],
        compiler_params=_params(),
    )(c, w_mod_sh, b_mod4)


def _mod_bwd(dmod4, c_all):
    def body(d_ref, call_ref, gw_ref, dall_ref, cmat_ref, dmat_ref, s1, r1):
        x, y, _ = _coords()
        me = _my_index()
        j = 2 * x + y
        dall_ref[me] = d_ref[...]
        sends = []
        for k in range(1, NDEV):
            cp = _rcopy(dall_ref.at[me], dall_ref.at[me], s1.at[k - 1], r1.at[k - 1], _peer(k))
            cp.start()
            sends.append(cp)
        for k in range(1, NDEV):
            pk = me ^ k
            _rcopy(dall_ref.at[pk], dall_ref.at[pk], s1.at[k - 1], r1.at[k - 1], _peer(k)).wait_recv()
        for cp in sends:
            cp.wait_send()
        for b in range(NDEV):
            cmat_ref[pl.ds(b, 1), :] = call_ref[b]
            dmat_ref[pl.ds(b, 1), :] = dall_ref[b, j]
        cm = cmat_ref[...]
        act = cm * _sigmoid(cm)
        gw_ref[...] = lax.dot_general(act, dmat_ref[...], (((0,), (0,)), ((), ())),
                                      preferred_element_type=F32, precision=HIGHEST)

    vm = pl.BlockSpec(memory_space=pltpu.VMEM)
    return pl.pallas_call(
        body, name="mod_bwd",
        out_shape=jax.ShapeDtypeStruct((D, 768), F32),
        in_specs=[vm, vm], out_specs=vm,
        scratch_shapes=[pltpu.VMEM((NDEV, 4, 1, 768), F32), pltpu.VMEM((NDEV, D), F32), pltpu.VMEM((NDEV, 768), F32),
                        pltpu.SemaphoreType.DMA((7,)), pltpu.SemaphoreType.DMA((7,))],
        compiler_params=_params(),
    )(dmod4, c_all)


def _gather_weights(wsh):
    rows, cols = wsh.shape
    half = rows // 2
    nch = 4
    cr = half // nch

    def body(w_ref, g_ref, ss, rs):
        x, y, c = _coords()
        j = 2 * x + y
        sib = (x, y, 1 - c)
        sends = []
        chips = ((1, 0), (0, 1), (1, 1))
        mine = lambda n: pl.ds(c * half + n * cr, cr)
        theirs = lambda n: pl.ds((1 - c) * half + n * cr, cr)
        for n in range(nch):
            for q, (kx, ky) in enumerate(chips):
                e = nch * q + n
                cp = _rcopy(w_ref.at[mine(n)], g_ref.at[j, mine(n)], ss.at[e], rs.at[e],
                            (_flip(x, kx), _flip(y, ky), c))
                cp.start()
                sends.append(cp)
        for n in range(nch):
            for q, (kx, ky) in enumerate(chips):
                jq = j ^ (2 * kx + ky)
                e = nch * q + n
                _rcopy(w_ref.at[mine(n)], g_ref.at[jq, mine(n)], ss.at[e], rs.at[e], sib).wait_recv()
                cp = _rcopy(g_ref.at[jq, mine(n)], g_ref.at[jq, mine(n)], ss.at[3 * nch + e], rs.at[3 * nch + e], sib)
                cp.start()
                sends.append(cp)
        for n in range(nch):
            for q, (kx, ky) in enumerate(chips):
                jq = j ^ (2 * kx + ky)
                e = nch * q + n
                _rcopy(w_ref.at[mine(n)], g_ref.at[jq, theirs(n)], ss.at[3 * nch + e], rs.at[3 * nch + e], sib).wait_recv()
        for cp in sends:
            cp.wait_send()

    any_ = pl.BlockSpec(memory_space=pl.ANY)
    return pl.pallas_call(
        body, name="gather_weights",
        out_shape=jax.ShapeDtypeStruct((4, rows, cols), wsh.dtype),
        in_specs=[any_], out_specs=any_,
        scratch_shapes=[pltpu.SemaphoreType.DMA((6 * nch,)), pltpu.SemaphoreType.DMA((6 * nch,))],
        compiler_params=_params(),
    )(wsh)


def _gather_small(v):
    r, cdim = v.shape

    def body(v_ref, out_ref, ss, rs):
        me = _my_index()
        out_ref[me] = v_ref[...]
        sends = []
        for k in range(1, NDEV):
            cp = _rcopy(out_ref.at[me], out_ref.at[me], ss.at[k - 1], rs.at[k - 1], _peer(k))
            cp.start()
            sends.append(cp)
        for k in range(1, NDEV):
            pk = me ^ k
            _rcopy(out_ref.at[pk], out_ref.at[pk], ss.at[k - 1], rs.at[k - 1], _peer(k)).wait_recv()
        for cp in sends:
            cp.wait_send()

    vm = pl.BlockSpec(memory_space=pltpu.VMEM)
    return pl.pallas_call(
        body, name="gather_small",
        out_shape=jax.ShapeDtypeStruct((NDEV, r, cdim), v.dtype),
        in_specs=[vm], out_specs=vm,
        scratch_shapes=[pltpu.SemaphoreType.DMA((7,)), pltpu.SemaphoreType.DMA((7,))],
        compiler_params=_params(),
    )(v)


def _norm_inproj(x, gn, shift, scale, wg, tm=1024, tn=512):
    s = x.shape[0]
    npc = PW // tn

    def body(x_ref, gn_ref, sh_ref, sc_ref, w_ref, p_ref, h_ref, hs):
        @pl.when(pl.program_id(1) == 0)
        def _():
            xt = x_ref[...]
            rstd = lax.rsqrt(jnp.mean(xt * xt, axis=-1, keepdims=True) + EPS)
            h = (xt * rstd * gn_ref[...]) * (1.0 + sc_ref[...]) + sh_ref[...]
            hs[...] = h.astype(BF16)
            h_ref[...] = hs[...]

        p_ref[0] = _dot(hs[...], w_ref[0])

    row = pl.BlockSpec((1, D), lambda i, j: (0, 0))
    return pl.pallas_call(
        body, name="norm_inproj", grid=(s // tm, 4 * npc),
        in_specs=[pl.BlockSpec((tm, D), lambda i, j: (i, 0)), row, row, row,
                  pl.BlockSpec((1, D, tn), lambda i, j: (j // npc, 0, j % npc))],
        out_specs=(pl.BlockSpec((1, tm, tn), lambda i, j: (j // npc, i, j % npc)),
                   pl.BlockSpec((tm, D), lambda i, j: (i, 0))),
        out_shape=(jax.ShapeDtypeStruct((4, s, PW), F32), jax.ShapeDtypeStruct((s, D), BF16)),
        scratch_shapes=[pltpu.VMEM((tm, D), BF16)],
        compiler_params=_params(("parallel", "arbitrary")),
    )(x, gn, shift, scale, wg)


def _shift_down(prev8, cur, d):
    t = cur.shape[0]
    ext = jnp.concatenate([prev8, cur], axis=0)
    return pltpu.roll(ext, d, 0)[8:]


def _shift_up(cur, next8, d):
    t = cur.shape[0]
    ext = jnp.concatenate([cur, next8], axis=0)
    return pltpu.roll(ext, t + 8 - d, 0)[:t]


def _rnn_gates(xr, prev8, cw, cb, wa, ba, wx, bx, lam, reset):
    xc = cw[3:4] * xr + cb
    for d in (1, 2, 3):
        xc = xc + cw[3 - d:4 - d] * _shift_down(prev8, xr, d)
    xcb = xc.astype(BF16)
    r = _sigmoid(_dot(xcb, wa.astype(BF16)) + ba)
    ig = _sigmoid(_dot(xcb, wx.astype(BF16)) + bx)
    nl = -lam
    sp = jnp.maximum(nl, 0.0) + jnp.log1p(jnp.exp(-jnp.abs(nl)))
    log_a = (-LRU_C * r) * sp
    a = jnp.where(reset, 0.0, jnp.exp(log_a))
    mult = jnp.where(reset, 1.0, jnp.sqrt(1.0 - jnp.exp(2.0 * log_a)))
    return xc, r, ig, sp, a, mult


def _scan_down(a, b, t):
    rows = lax.broadcasted_iota(jnp.int32, a.shape, 0)
    d = 1
    while d < t:
        m = rows >= d
        a_s = pltpu.roll(a, d, 0)
        b_s = pltpu.roll(b, d, 0)
        b = jnp.where(m, a * b_s + b, b)
        a = jnp.where(m, a * a_s, a)
        d *= 2
    return a, b


def _scan_up(a, b, t):
    rows = lax.broadcasted_iota(jnp.int32, a.shape, 0)
    d = 1
    while d < t:
        m = rows < t - d
        a_s = pltpu.roll(a, t - d, 0)
        b_s = pltpu.roll(b, t - d, 0)
        b = jnp.where(m, a * b_s + b, b)
        a = jnp.where(m, a * a_s, a)
        d *= 2
    return a, b


def _rnn_fwd(p, pos, conv_w, conv_b, w_a, b_a, w_x, b_x, lam, tt=256):
    s = p.shape[1]
    nt = s // tt

    def body(xr_ref, z_ref, pos_ref, cw_ref, cb_ref, wa_ref, ba_ref, wx_ref, bx_ref, lam_ref,
             hr_ref, gr_ref, xprev, hprev):
        @pl.when(pl.program_id(1) == 0)
        def _():
            xprev[...] = jnp.zeros_like(xprev)
            hprev[...] = jnp.zeros_like(hprev)

        xr = xr_ref[0]
        z = z_ref[0]
        reset = pos_ref[...] == 0
        xc, r, ig, sp, a, mult = _rnn_gates(xr, xprev[...], cw_ref[...], cb_ref[...], wa_ref[0], ba_ref[0],
                                            wx_ref[0], bx_ref[0], lam_ref[...], reset)
        bx = mult * ig * xc
        a_cum, h0 = _scan_down(a, bx, tt)
        h = a_cum * hprev[0:1] + h0
        xprev[...] = xr[tt - 8:]
        hprev[...] = jnp.broadcast_to(h[tt - 1:tt], (8, DH))
        hr_ref[...] = h
        gr_ref[...] = (h * (z * _sigmoid(z))).astype(BF16)

    head_row = lambda hh, t: (0, hh)
    return pl.pallas_call(
        body, name="rnn_fwd", grid=(H, nt),
        in_specs=[pl.BlockSpec((1, tt, DH), lambda hh, t: (0, t, hh)),
                  pl.BlockSpec((1, tt, DH), lambda hh, t: (0, t, H + hh)),
                  pl.BlockSpec((tt, 1), lambda hh, t: (t, 0)),
                  pl.BlockSpec((4, DH), head_row), pl.BlockSpec((1, DH), head_row),
                  pl.BlockSpec((1, DH, DH), lambda hh, t: (hh, 0, 0)), pl.BlockSpec((1, 1, DH), lambda hh, t: (hh, 0, 0)),
                  pl.BlockSpec((1, DH, DH), lambda hh, t: (hh, 0, 0)), pl.BlockSpec((1, 1, DH), lambda hh, t: (hh, 0, 0)),
                  pl.BlockSpec((1, DH), head_row)],
        out_specs=(pl.BlockSpec((tt, DH), lambda hh, t: (t, hh)), pl.BlockSpec((tt, DH), lambda hh, t: (t, hh))),
        out_shape=(jax.ShapeDtypeStruct((s, D), F32), jax.ShapeDtypeStruct((s, D), BF16)),
        scratch_shapes=[pltpu.VMEM((8, DH), F32), pltpu.VMEM((8, DH), F32)],
        compiler_params=_params(("parallel", "arbitrary")),
    )(p, p, pos, conv_w, conv_b, w_a, b_a, w_x, b_x, lam)


def _rnn_bwd(p, hr, dgr, pos, conv_w, conv_b, w_a, b_a, w_x, b_x, lam, tt=256):
    s = p.shape[1]
    nt = s // tt
    t8 = tt // 8

    def body(xr_ref, z_ref, xp_ref, hr_ref, hp_ref, dg_ref, pos_ref, cw_ref, cb_ref, wa_ref, ba_ref, wx_ref, bx_ref,
             lam_ref, dxr_ref, dz_ref, gwa_ref, gba_ref, gwx_ref, gbx_ref, glam_ref, gcw_ref, gcb_ref,
             a_next, g_next, dxc_next):
        t = pl.program_id(1)
        has_prev = t < nt - 1

        @pl.when(t == 0)
        def _():
            a_next[...] = jnp.zeros_like(a_next)
            g_next[...] = jnp.zeros_like(g_next)
            dxc_next[...] = jnp.zeros_like(dxc_next)
            gwa_ref[...] = jnp.zeros_like(gwa_ref)
            gba_ref[...] = jnp.zeros_like(gba_ref)
            gwx_ref[...] = jnp.zeros_like(gwx_ref)
            gbx_ref[...] = jnp.zeros_like(gbx_ref)
            glam_ref[...] = jnp.zeros_like(glam_ref)
            gcw_ref[...] = jnp.zeros_like(gcw_ref)
            gcb_ref[...] = jnp.zeros_like(gcb_ref)

        xr = xr_ref[0]
        z = z_ref[0]
        hr_blk = hr_ref[...]
        dg = dg_ref[...]
        xprev = jnp.where(has_prev, xp_ref[0], 0.0)
        hprev8 = jnp.where(has_prev, hp_ref[...], 0.0)
        reset = pos_ref[...] == 0
        cw = cw_ref[...]
        wa = wa_ref[0]
        wx = wx_ref[0]
        lam_v = lam_ref[...]
        xc, r, ig, sp, a, mult = _rnn_gates(xr, xprev, cw, cb_ref[...], wa, ba_ref[0], wx, bx_ref[0], lam_v, reset)

        sz = _sigmoid(z)
        dh = dg * (z * sz)
        dz_ref[...] = (dg * hr_blk * (sz * (1.0 + z * (1.0 - sz)))).astype(BF16)

        an = _shift_up(a, a_next[...], 1)
        a_cum, g0 = _scan_up(an, dh, tt)
        g = g0 + a_cum * g_next[0:1]
        a_next[...] = jnp.broadcast_to(a[0:1], (8, DH))
        g_next[...] = jnp.broadcast_to(g[0:1], (8, DH))

        hm1 = _shift_down(hprev8, hr_blk, 1)
        da = g * hm1
        dmult = g * (ig * xc)
        di = g * (mult * xc)
        dxc = g * (mult * ig)
        dla = jnp.where(reset, 0.0, da * a - dmult * (a * a) / mult)
        dr = dla * (-LRU_C * sp)
        dsp = _colsum(dla * (-LRU_C * r))
        glam_ref[0] += dsp * (-_sigmoid(-lam_v))
        dpa = dr * r * (1.0 - r)
        dpx = di * ig * (1.0 - ig)
        dpab = dpa.astype(BF16)
        dpxb = dpx.astype(BF16)
        dxc = dxc + _dot_nt(dpab, wa.astype(BF16)) + _dot_nt(dpxb, wx.astype(BF16))
        xcb = xc.astype(BF16)
        gwa_ref[0] += _dot_tn(xcb, dpab)
        gwx_ref[0] += _dot_tn(xcb, dpxb)
        gba_ref[0] += _colsum(dpa)
        gbx_ref[0] += _colsum(dpx)

        dxr = cw[3:4] * dxc
        for d in (1, 2, 3):
            dxr = dxr + cw[3 - d:4 - d] * _shift_up(dxc, dxc_next[...], d)
        dxr_ref[...] = dxr.astype(BF16)
        dxc_next[...] = dxc[0:8]
        gcb_ref[0] += _colsum(dxc)
        gcw_ref[0, 3:4, :] += _colsum(xr * dxc)
        for d in (1, 2, 3):
            gcw_ref[0, 3 - d:4 - d, :] += _colsum(_shift_down(xprev, xr, d) * dxc)

    rt = lambda t: nt - 1 - t
    prev8 = lambda t: jnp.maximum(rt(t) * t8 - 1, 0)
    head_row = lambda hh, t: (0, hh)
    hsm = lambda hh, t: (hh, 0, 0)
    return pl.pallas_call(
        body, name="rnn_bwd", grid=(H, nt),
        in_specs=[pl.BlockSpec((1, tt, DH), lambda hh, t: (0, rt(t), hh)),
                  pl.BlockSpec((1, tt, DH), lambda hh, t: (0, rt(t), H + hh)),
                  pl.BlockSpec((1, 8, DH), lambda hh, t: (0, prev8(t), hh)),
                  pl.BlockSpec((tt, DH), lambda hh, t: (rt(t), hh)),
                  pl.BlockSpec((8, DH), lambda hh, t: (prev8(t), hh)),
                  pl.BlockSpec((tt, DH), lambda hh, t: (rt(t), hh)),
                  pl.BlockSpec((tt, 1), lambda hh, t: (rt(t), 0)),
                  pl.BlockSpec((4, DH), head_row), pl.BlockSpec((1, DH), head_row),
                  pl.BlockSpec((1, DH, DH), hsm), pl.BlockSpec((1, 1, DH), hsm),
                  pl.BlockSpec((1, DH, DH), hsm), pl.BlockSpec((1, 1, DH), hsm),
                  pl.BlockSpec((1, DH), head_row)],
        out_specs=(pl.BlockSpec((tt, DH), lambda hh, t: (rt(t), hh)), pl.BlockSpec((tt, DH), lambda hh, t: (rt(t), hh)),
                   pl.BlockSpec((1, DH, DH), hsm), pl.BlockSpec((1, 1, DH), hsm),
                   pl.BlockSpec((1, DH, DH), hsm), pl.BlockSpec((1, 1, DH), hsm),
                   pl.BlockSpec((1, 1, DH), hsm), pl.BlockSpec((1, 4, DH), hsm), pl.BlockSpec((1, 1, DH), hsm)),
        out_shape=(jax.ShapeDtypeStruct((s, D), BF16), jax.ShapeDtypeStruct((s, D), BF16),
                   jax.ShapeDtypeStruct((H, DH, DH), F32), jax.ShapeDtypeStruct((H, 1, DH), F32),
                   jax.ShapeDtypeStruct((H, DH, DH), F32), jax.ShapeDtypeStruct((H, 1, DH), F32),
                   jax.ShapeDtypeStruct((H, 1, DH), F32), jax.ShapeDtypeStruct((H, 4, DH), F32),
                   jax.ShapeDtypeStruct((H, 1, DH), F32)),
        scratch_shapes=[pltpu.VMEM((8, DH), F32), pltpu.VMEM((8, DH), F32), pltpu.VMEM((8, DH), F32)],
        compiler_params=_params(("parallel", "arbitrary")),
    )(p, p, p, hr, hr, dgr, pos, conv_w, conv_b, w_a, b_a, w_x, b_x, lam)


def _rope(t, c, sa, sb):
    return t * c + pltpu.roll(t, DH - ROT // 2, 1) * sa + pltpu.roll(t, ROT // 2, 1) * sb


def _rope_bwd(g, c, sa, sb):
    return g * c + pltpu.roll(g * sa, ROT // 2, 1) + pltpu.roll(g * sb, DH - ROT // 2, 1)


def _unit_bases(gi, u):
    dil = DILATIONS[gi]
    if dil == 1:
        return u * UB, SPAN + (u - 1) * UB, u == 0
    if dil == 4:
        blk, r = u // 4, u % 4
        return blk * 4 * UB + r, SPAN + (blk - 1) * 4 * UB + r, blk == 0
    return u, u, True


def _unit_slices(gi, u):
    dil = DILATIONS[gi]
    qb0, kb0, first = _unit_bases(gi, u)
    if dil == 1:
        return pl.ds(pl.multiple_of(qb0, UB), UB), pl.ds(pl.multiple_of(kb0, UB), 2 * UB), first
    return pl.ds(qb0, UB, stride=dil), pl.ds(kb0, 2 * UB, stride=dil), first


def _bdot(a, b):
    return lax.dot_general(a, b, (((2,), (1,)), ((0,), (0,))), preferred_element_type=F32)


def _bdot_nt(a, b):
    return lax.dot_general(a, b, (((2,), (2,)), ((0,), (0,))), preferred_element_type=F32)


def _bdot_tn(a, b):
    return lax.dot_general(a, b, (((1,), (1,)), ((0,), (0,))), preferred_element_type=F32)


def _band_mask(first_in_span, has_prev):
    qi = lax.broadcasted_iota(jnp.int32, (UB, 2 * UB), 0)
    ki = lax.broadcasted_iota(jnp.int32, (UB, 2 * UB), 1)
    dist = UB + qi - ki
    band = (dist >= 0) & (dist <= UB)
    return band & ((ki >= UB) | jnp.logical_not(first_in_span) | has_prev)


def _attn_fwd(p, rc, rsa, rsb):
    s = p.shape[1]
    ns = s // SPAN
    nunit = SPAN // UB

    def body(q_ref, k_ref, v_ref, z_ref, c_ref, sa_ref, sb_ref, o_ref, lse_ref, ga_ref,
             qr, kf, vf, acc, mm, ll):
        n = pl.program_id(1)

        @pl.when(n == 0)
        def _():
            kf[0:SPAN] = jnp.zeros((SPAN, DH), F32)
            vf[0:SPAN] = jnp.zeros((SPAN, DH), F32)

        c, sa, sb = c_ref[...], sa_ref[...], sb_ref[...]
        qr[...] = _rope(q_ref[0], c, sa, sb)
        kf[SPAN:] = _rope(k_ref[0], c, sa, sb)
        vf[SPAN:] = v_ref[0]
        has_prev = n > 0

        for gi, dil in enumerate(DILATIONS):
            def trip(t, carry, gi=gi, dil=dil):
                qsls, ksls, firsts = [], [], []
                for b in range(UNIT_BATCH):
                    qsl, ksl, first = _unit_slices(gi, t * UNIT_BATCH + b)
                    qsls.append(qsl)
                    ksls.append(ksl)
                    firsts.append(first)
                qb = jnp.stack([qr[qsl, :].astype(BF16) for qsl in qsls])
                kb = jnp.stack([kf[ksl, :].astype(BF16) for ksl in ksls])
                vb = jnp.stack([vf[ksl, :].astype(BF16) for ksl in ksls])
                s_all = _bdot_nt(qb, kb)
                prs = []
                for b in range(UNIT_BATCH):
                    sc = jnp.where(_band_mask(firsts[b], has_prev), s_all[b] * SCALE, NEG)
                    m = jnp.max(sc, axis=-1, keepdims=True)
                    pr = jnp.exp(sc - m)
                    l = jnp.sum(pr, axis=-1, keepdims=True)
                    mm[gi, qsls[b], :] = jnp.broadcast_to(m, (UB, DH))
                    ll[gi, qsls[b], :] = jnp.broadcast_to(l, (UB, DH))
                    prs.append(pr.astype(BF16))
                o_all = _bdot(jnp.stack(prs), vb)
                for b in range(UNIT_BATCH):
                    acc[gi, qsls[b], :] = o_all[b]
                return carry

            lax.fori_loop(0, nunit // UNIT_BATCH, trip, 0)

        m_all =jnp.maximum(jnp.maximum(mm[0], mm[1]), mm[2])
        num = jnp.zeros((SPAN, DH), F32)
        den = jnp.zeros((SPAN, DH), F32)
        for gi in range(3):
            w = jnp.exp(mm[gi] - m_all)
            num = num + w * acc[gi]
            den = den + w * ll[gi]
        o = num / den
        o_ref[...] = o
        lse_ref[...] = m_all + jnp.log(den)
        z = z_ref[0]
        ga_ref[...] = (o * (z * _sigmoid(z))).astype(BF16)
        kf[0:SPAN] = kf[SPAN:]
        vf[0:SPAN] = vf[SPAN:]

    blk = lambda piece, off: pl.BlockSpec((1, SPAN, DH), lambda hh, n: (piece, n, off + hh))
    tab = pl.BlockSpec((SPAN, DH), lambda hh, n: (n, 0))
    outb = pl.BlockSpec((SPAN, DH), lambda hh, n: (n, hh))
    return pl.pallas_call(
        body, name="attn_fwd", grid=(H, ns),
        in_specs=[blk(1, 0), blk(1, H), blk(2, 0), blk(2, H), tab, tab, tab],
        out_specs=(outb, outb, outb),
        out_shape=(jax.ShapeDtypeStruct((s, D), F32), jax.ShapeDtypeStruct((s, D), F32),
                   jax.ShapeDtypeStruct((s, D), BF16)),
        scratch_shapes=[pltpu.VMEM((SPAN, DH), F32), pltpu.VMEM((2 * SPAN, DH), F32), pltpu.VMEM((2 * SPAN, DH), F32),
                        pltpu.VMEM((3, SPAN, DH), F32), pltpu.VMEM((3, SPAN, DH), F32), pltpu.VMEM((3, SPAN, DH), F32)],
        compiler_params=_params(("parallel", "arbitrary")),
    )(p, p, p, p, rc, rsa, rsb)


def _attn_bwd(p, o, lse, dga, rc, rsa, rsb):
    s = p.shape[1]
    ns = s // SPAN
    nunit = SPAN // UB

    def body(q_ref, k_ref, kp_ref, v_ref, vp_ref, z_ref, c_ref, sa_ref, sb_ref, cp_ref, sap_ref, sbp_ref,
             o_ref, lse_ref, dg_ref, dq_ref, dk_ref, dv_ref, dz_ref,
             qr, kf, vf, dof, dlt, dqa, dkf, dvf):
        step = pl.program_id(1)
        n = ns - 1 - step
        has_prev = n > 0

        @pl.when(step == 0)
        def _():
            dkf[...] = jnp.zeros_like(dkf)
            dvf[...] = jnp.zeros_like(dvf)

        @pl.when(step > 0)
        def _():
            dkf[SPAN:] = dkf[0:SPAN]
            dvf[SPAN:] = dvf[0:SPAN]
            dkf[0:SPAN] = jnp.zeros((SPAN, DH), F32)
            dvf[0:SPAN] = jnp.zeros((SPAN, DH), F32)

        c, sa, sb = c_ref[...], sa_ref[...], sb_ref[...]
        qr[...] = _rope(q_ref[0], c, sa, sb)
        kf[SPAN:] = _rope(k_ref[0], c, sa, sb)
        vf[SPAN:] = v_ref[0]
        kf[0:SPAN] = jnp.where(has_prev, _rope(kp_ref[0], cp_ref[...], sap_ref[...], sbp_ref[...]), 0.0)
        vf[0:SPAN] = jnp.where(has_prev, vp_ref[0], 0.0)
        z = z_ref[0]
        sz = _sigmoid(z)
        dg = dg_ref[...]
        ov = o_ref[...]
        do = dg * (z * sz)
        dz_ref[...] = (dg * ov * (sz * (1.0 + z * (1.0 - sz)))).astype(BF16)
        dof[...] = do
        dlt[...] = jnp.broadcast_to(jnp.sum(do * ov, axis=-1, keepdims=True), (SPAN, DH))
        dqa[...] = jnp.zeros_like(dqa)

        for gi, dil in enumerate(DILATIONS):
            def trip(t, carry, gi=gi, dil=dil):
                qsls, ksls, firsts = [], [], []
                for b in range(UNIT_BATCH):
                    qsl, ksl, first = _unit_slices(gi, t * UNIT_BATCH + b)
                    qsls.append(qsl)
                    ksls.append(ksl)
                    firsts.append(first)
                qb = jnp.stack([qr[qsl, :].astype(BF16) for qsl in qsls])
                kb = jnp.stack([kf[ksl, :].astype(BF16) for ksl in ksls])
                vb = jnp.stack([vf[ksl, :].astype(BF16) for ksl in ksls])
                dob = jnp.stack([dof[qsl, :].astype(BF16) for qsl in qsls])
                s_all = _bdot_nt(qb, kb)
                dp_all = _bdot_nt(dob, vb)
                prs, dss = [], []
                for b in range(UNIT_BATCH):
                    lse_b = lse_ref[qsls[b], :]
                    dl_b = dlt[qsls[b], :]
                    pr = jnp.exp(s_all[b] * SCALE - jnp.concatenate([lse_b, lse_b], axis=1))
                    pr = jnp.where(_band_mask(firsts[b], has_prev), pr, 0.0)
                    prs.append(pr.astype(BF16))
                    dss.append((pr * (dp_all[b] - jnp.concatenate([dl_b, dl_b], axis=1)) * SCALE).astype(BF16))
                ds_all = jnp.stack(dss)
                dv_all = _bdot_tn(jnp.stack(prs), dob)
                dq_all = _bdot(ds_all, kb)
                dk_all = _bdot_tn(ds_all, qb)
                for b in range(UNIT_BATCH):
                    dvf[ksls[b], :] += dv_all[b]
                    dqa[qsls[b], :] += dq_all[b]
                    dkf[ksls[b], :] += dk_all[b]
                return carry

            lax.fori_loop(0, nunit // UNIT_BATCH, trip, 0)

        dq_ref[...] = _rope_bwd(dqa[...], c, sa, sb).astype(BF16)
        dk_ref[...] = _rope_bwd(dkf[SPAN:], c, sa, sb).astype(BF16)
        dv_ref[...] = dvf[SPAN:].astype(BF16)

    rn = lambda n: ns - 1 - n
    pn = lambda n: jnp.maximum(ns - 2 - n, 0)
    blk = lambda piece, off: pl.BlockSpec((1, SPAN, DH), lambda hh, n: (piece, rn(n), off + hh))
    blkp = lambda piece, off: pl.BlockSpec((1, SPAN, DH), lambda hh, n: (piece, pn(n), off + hh))
    tab = pl.BlockSpec((SPAN, DH), lambda hh, n: (rn(n), 0))
    tabp = pl.BlockSpec((SPAN, DH), lambda hh, n: (pn(n), 0))
    io = pl.BlockSpec((SPAN, DH), lambda hh, n: (rn(n), hh))
    return pl.pallas_call(
        body, name="attn_bwd", grid=(H, ns),
        in_specs=[blk(1, 0), blk(1, H), blkp(1, H), blk(2, 0), blkp(2, 0), blk(2, H),
                  tab, tab, tab, tabp, tabp, tabp, io, io, io],
        out_specs=(io, io, io, io),
        out_shape=tuple(jax.ShapeDtypeStruct((s, D), BF16) for _ in range(4)),
        scratch_shapes=[pltpu.VMEM((SPAN, DH), F32), pltpu.VMEM((2 * SPAN, DH), F32), pltpu.VMEM((2 * SPAN, DH), F32),
                        pltpu.VMEM((SPAN, DH), F32), pltpu.VMEM((SPAN, DH), F32), pltpu.VMEM((SPAN, DH), F32),
                        pltpu.VMEM((2 * SPAN, DH), F32), pltpu.VMEM((2 * SPAN, DH), F32)],
        compiler_params=_params(("parallel", "arbitrary")),
    )(p, p, p, p, p, p, rc, rsa, rsb, rc, rsa, rsb, o, lse, dga)


def _tail(gr, ga, p, x, tgt, w3, b_gate, gate, g_final, tm=256):
    s = x.shape[0]
    nt = s // tm

    def body(gr_ref, ga_ref, pr_ref, pa_ref, x_ref, t_ref, bg_ref, gate_ref, gf_ref, w_hbm,
             dgr_ref, dga_ref, dc_ref, dx2_ref, vec_ref, go_hbm, w_s, acc_s, sem):
        i = pl.program_id(0)

        @pl.when(i == 0)
        def _():
            cp = pltpu.make_async_copy(w_hbm, w_s, sem.at[12])
            cp.start()
            acc_s[...] = jnp.zeros_like(acc_s)
            vec_ref[...] = jnp.zeros_like(vec_ref)
            cp.wait()

        grb = gr_ref[...]
        gab = ga_ref[...]
        bg = bg_ref[...]
        gate_v = gate_ref[...]
        gf = gf_ref[...]
        y_r = _dot(grb, w_s[0])
        y_a = _dot(gab, w_s[1])
        sr = _sigmoid(pr_ref[0] + bg[:, :D])
        sa = _sigmoid(pa_ref[0] + bg[:, D:])
        mb = (sr * y_r + sa * y_a).astype(BF16)
        u = _dot(mb, w_s[2])
        x2 = x_ref[...] + gate_v * u
        rstd = lax.rsqrt(jnp.mean(x2 * x2, axis=-1, keepdims=True) + EPS)
        xh = x2 * rstd
        e = xh * gf - t_ref[...]
        dy = e * (1.0 / D)
        dyg = dy * gf
        dx2 = rstd * (dyg - xh * jnp.mean(dyg * xh, axis=-1, keepdims=True))
        dx2_ref[...] = dx2
        dub = (dx2 * gate_v).astype(BF16)
        dm = _dot_nt(dub, w_s[2])
        dyr = (dm * sr).astype(BF16)
        dya = (dm * sa).astype(BF16)
        dpr = dm * y_r * (sr * (1.0 - sr))
        dpa = dm * y_a * (sa * (1.0 - sa))
        dc_ref[:, :D] = dpr.astype(BF16)
        dc_ref[:, D:] = dpa.astype(BF16)
        dgr_ref[...] = _dot_nt(dyr, w_s[0])
        dga_ref[...] = _dot_nt(dya, w_s[1])
        acc_s[0] += _dot_tn(grb, dyr)
        acc_s[1] += _dot_tn(gab, dya)
        acc_s[2] += _dot_tn(mb, dub)
        vec_ref[0:1, :] += _colsum(dy * xh)
        vec_ref[1:2, :] += _colsum(dx2 * u)
        vec_ref[2:3, :] += _colsum(dpr)
        vec_ref[3:4, :] += _colsum(dpa)
        vec_ref[4:5, :] += _colsum(e * e)

        @pl.when(i == nt - 1)
        def _():
            vec_ref[4:5, :] = jnp.broadcast_to(jnp.sum(vec_ref[4:5, :]) * (0.5 / D), (1, D))
            cps = []
            for w in range(3):
                for j in range(4):
                    cps.append(pltpu.make_async_copy(acc_s.at[w, pl.ds(256 * j, 256)],
                                                     go_hbm.at[j, pl.ds(256 * w, 256)], sem.at[4 * w + j]))
            for cp in cps:
                cp.start()
            for cp in cps:
                cp.wait()

    rowt = lambda i: (i, 0)
    row = lambda w: pl.BlockSpec((1, w), lambda i: (0, 0))
    any_ = pl.BlockSpec(memory_space=pl.ANY)
    return pl.pallas_call(
        body, name="tail", grid=(nt,),
        in_specs=[pl.BlockSpec((tm, D), rowt), pl.BlockSpec((tm, D), rowt),
                  pl.BlockSpec((1, tm, D), lambda i: (3, i, 0)), pl.BlockSpec((1, tm, D), lambda i: (3, i, 1)),
                  pl.BlockSpec((tm, D), rowt), pl.BlockSpec((tm, D), rowt),
                  row(2 * D), row(D), row(D), any_],
        out_specs=(pl.BlockSpec((tm, D), rowt), pl.BlockSpec((tm, D), rowt), pl.BlockSpec((tm, 2 * D), rowt),
                   pl.BlockSpec((tm, D), rowt), pl.BlockSpec((8, D), lambda i: (0, 0)), any_),
        out_shape=(jax.ShapeDtypeStruct((s, D), F32), jax.ShapeDtypeStruct((s, D), F32),
                   jax.ShapeDtypeStruct((s, 2 * D), BF16), jax.ShapeDtypeStruct((s, D), F32),
                   jax.ShapeDtypeStruct((8, D), F32), jax.ShapeDtypeStruct((4, 768, D), F32)),
        scratch_shapes=[pltpu.VMEM((3, D, D), BF16), pltpu.VMEM((3, D, D), F32), pltpu.SemaphoreType.DMA((13,))],
        compiler_params=_params(("arbitrary",)),
    )(gr, ga, p, p, x, tgt, b_gate, gate, g_final, w3)


def _pieces_steps(pieces):
    out, s0 = [], 0
    for a in pieces:
        n = a.shape[1] // D
        out.append((s0, n))
        s0 += n
    return out, s0


def _inproj_bwd_x(pieces, wg, x, dx2, gn, scale, sums, tm=512):
    s = x.shape[0]
    steps, nk = _pieces_steps(pieces)
    npc = PW // D
    np_ = len(pieces)
    na = len(sums)
    ni = s // tm

    def body(*refs):
        d_refs = refs[:np_]
        w_ref, x_ref, dx2_ref, gn_ref, sc_ref = refs[np_:np_ + 5]
        q_refs = refs[np_ + 5:np_ + 5 + na]
        gx_ref, vec_ref = refs[np_ + 5 + na:np_ + 7 + na]
        r_refs = refs[np_ + 7 + na:np_ + 7 + 2 * na]
        acc, ss, rs = refs[np_ + 7 + 2 * na:]
        i, k = pl.program_id(0), pl.program_id(1)

        def scatter_copies():
            cx, cy, cc = _coords()
            j = 2 * cx + cy
            cps = []
            for t, (q, r) in enumerate(zip(q_refs, r_refs)):
                for e, (kx, ky) in enumerate(((1, 0), (0, 1), (1, 1))):
                    cps.append(_rcopy(q.at[j ^ (2 * kx + ky)], r.at[e], ss.at[3 * t + e], rs.at[3 * t + e],
                                      (_flip(cx, kx), _flip(cy, ky), cc)))
            return cps

        @pl.when(k == 0)
        def _():
            acc[...] = jnp.zeros_like(acc)

        @pl.when((i == 0) & (k == 0))
        def _():
            vec_ref[...] = jnp.zeros_like(vec_ref)
            for cp in scatter_copies():
                cp.start()

        @pl.when((i == ni - 1) & (k == nk - 1))
        def _():
            for cp in scatter_copies():
                cp.wait()

        for (s0, n), d_ref in zip(steps, d_refs):
            @pl.when((k >= s0) & (k < s0 + n))
            def _(d_ref=d_ref):
                acc[...] += _dot_nt(d_ref[...], w_ref[0])

        @pl.when(k == nk - 1)
        def _():
            dh = acc[...]
            xt = x_ref[...]
            rstd = lax.rsqrt(jnp.mean(xt * xt, axis=-1, keepdims=True) + EPS)
            xh = xt * rstd
            gn_v = gn_ref[...]
            sc1 = 1.0 + sc_ref[...]
            dhx = dh * xh
            vec_ref[0:1, :] += _colsum(dh)
            vec_ref[1:2, :] += _colsum(dhx) * gn_v
            vec_ref[2:3, :] += _colsum(dhx) * sc1
            dxh = dh * (gn_v * sc1)
            gx_ref[...] = rstd * (dxh - xh * jnp.mean(dxh * xh, axis=-1, keepdims=True)) + dx2_ref[...]

    def piece_spec(s0, n):
        return pl.BlockSpec((tm, D), lambda i, k: (i, jnp.clip(k - s0, 0, n - 1)))

    rowt = lambda i, k: (i, 0)
    row = pl.BlockSpec((1, D), lambda i, k: (0, 0))
    any_ = pl.BlockSpec(memory_space=pl.ANY)
    outs = pl.pallas_call(
        body, name="inproj_bwd_x", grid=(ni, nk),
        in_specs=[piece_spec(s0, n) for s0, n in steps] +
                 [pl.BlockSpec((1, D, D), lambda i, k: (k // npc, 0, k % npc)),
                  pl.BlockSpec((tm, D), rowt), pl.BlockSpec((tm, D), rowt), row, row] + [any_] * na,
        out_specs=(pl.BlockSpec((tm, D), rowt), pl.BlockSpec((8, D), lambda i, k: (0, 0))) + (any_,) * na,
        out_shape=(jax.ShapeDtypeStruct((s, D), F32), jax.ShapeDtypeStruct((8, D), F32)) +
                  tuple(jax.ShapeDtypeStruct((3,) + q.shape[1:], q.dtype) for q in sums),
        scratch_shapes=[pltpu.VMEM((tm, D), F32), pltpu.SemaphoreType.DMA((3 * na,)), pltpu.SemaphoreType.DMA((3 * na,))],
        compiler_params=_params(("arbitrary", "arbitrary")),
    )(*pieces, wg, x, dx2, gn, scale, *sums)
    return outs[0], outs[1], outs[2:]


def _inproj_bwd_w(pieces, hbf, tk=512):
    s = hbf.shape[0]
    steps, nk = _pieces_steps(pieces)
    npc = PW // D
    ns = s // tk
    np_ = len(pieces)

    def body(*refs):
        d_refs = refs[:np_]
        h_ref, g_ref = refs[np_:]
        cb, k = pl.program_id(0), pl.program_id(1)

        @pl.when(k == 0)
        def _():
            g_ref[...] = jnp.zeros_like(g_ref)

        for (s0, n), d_ref in zip(steps, d_refs):
            @pl.when((cb >= s0) & (cb < s0 + n))
            def _(d_ref=d_ref):
                g_ref[0] += _dot_tn(h_ref[...], d_ref[...])

    def piece_spec(s0, n):
        def imap(cb, k):
            active = (cb >= s0) & (cb < s0 + n)
            return (jnp.where(active, k, 0), jnp.clip(cb - s0, 0, n - 1))
        return pl.BlockSpec((tk, D), imap)

    return pl.pallas_call(
        body, name="inproj_bwd_w", grid=(nk, ns),
        in_specs=[piece_spec(s0, n) for s0, n in steps] + [pl.BlockSpec((tk, D), lambda cb, k: (k, 0))],
        out_specs=pl.BlockSpec((1, D, D), lambda cb, k: (cb // npc, 0, cb % npc)),
        out_shape=jax.ShapeDtypeStruct((4, D, PW), F32),
        compiler_params=_params(("parallel", "arbitrary")),
    )(*pieces, hbf)


D2D_CHUNK_BYTES = 512 * 1024


def _chunk_rows(a):
    return max(8, D2D_CHUNK_BYTES // (a.shape[-1] * a.dtype.itemsize))


def _pair_exchange(arrs):
    na = len(arrs)
    chunks = []
    for t, a in enumerate(arrs):
        hr = a.shape[1] // 2
        cr = _chunk_rows(a)
        chunks += [(t, j, r0, cr) for j in range(a.shape[0]) for r0 in range(0, hr, cr)]
    nch = len(chunks)

    def body(*refs):
        a_refs = refs[:na]
        rb_refs = refs[na:2 * na]
        ss, rs = refs[2 * na:]
        x, y, c = _coords()
        sib = (x, y, 1 - c)
        rcs = []
        for n, (t, j, r0, cr) in enumerate(chunks):
            hr = a_refs[t].shape[1] // 2
            rc = _rcopy(a_refs[t].at[j, pl.ds((1 - c) * hr + r0, cr), :], rb_refs[t].at[j, pl.ds(r0, cr), :],
                        ss.at[n], rs.at[n], sib)
            rc.start()
            rcs.append(rc)
        for rc in rcs:
            rc.wait_recv()
        for rc in rcs:
            rc.wait_send()

    any_ = pl.BlockSpec(memory_space=pl.ANY)
    halves = [jax.ShapeDtypeStruct((a.shape[0], a.shape[1] // 2, a.shape[2]), a.dtype) for a in arrs]
    return pl.pallas_call(
        body, name="pair_exchange",
        out_shape=tuple(halves),
        in_specs=[any_] * na, out_specs=tuple([any_] * na),
        scratch_shapes=[pltpu.SemaphoreType.DMA((nch,)), pltpu.SemaphoreType.DMA((nch,))],
        compiler_params=_params(),
    )(*arrs)


def _pair_swap(arrs):
    na = len(arrs)
    chunks = []
    for t, a in enumerate(arrs):
        cr = _chunk_rows(a)
        chunks += [(t, r0, cr) for r0 in range(0, a.shape[0], cr)]
    nch = len(chunks)

    def body(*refs):
        a_refs = refs[:na]
        o_refs = refs[na:2 * na]
        ss, rs = refs[2 * na:]
        x, y, c = _coords()
        sib = (x, y, 1 - c)
        rcs = []
        for n, (t, r0, cr) in enumerate(chunks):
            rows = pl.ds(r0, cr)
            rc = _rcopy(a_refs[t].at[rows, :], o_refs[t].at[rows, :], ss.at[n], rs.at[n], sib)
            rc.start()
            rcs.append(rc)
        for rc in rcs:
            rc.wait_recv()
        for rc in rcs:
            rc.wait_send()

    any_ = pl.BlockSpec(memory_space=pl.ANY)
    return pl.pallas_call(
        body, name="pair_swap",
        out_shape=tuple(jax.ShapeDtypeStruct(a.shape, a.dtype) for a in arrs),
        in_specs=[any_] * na, out_specs=tuple([any_] * na),
        scratch_shapes=[pltpu.SemaphoreType.DMA((nch,)), pltpu.SemaphoreType.DMA((nch,))],
        compiler_params=_params(),
    )(*arrs)


def _add_half(full, rb, core, tr):
    n, r, cdim = full.shape
    nb = r // 2 // tr

    def body(c_ref, a_ref, b_ref, o_ref):
        o_ref[...] = a_ref[...] + b_ref[...]

    mine = pl.BlockSpec((1, tr, cdim), lambda i, j, c_ref: (i, c_ref[0] * nb + j, 0))
    spec = pl.BlockSpec((1, tr, cdim), lambda i, j, c_ref: (i, j, 0))
    return pl.pallas_call(
        body, name="add_half",
        grid_spec=pltpu.PrefetchScalarGridSpec(num_scalar_prefetch=1, grid=(n, nb), in_specs=[mine, spec],
                                               out_specs=spec),
        out_shape=jax.ShapeDtypeStruct(rb.shape, rb.dtype),
        compiler_params=_params(("parallel", "parallel")),
    )(core, full, rb)


def _sum_slots(q, r3, shard, tr):
    _, r, cdim = q.shape

    def body(j_ref, q_ref, r_ref, o_ref):
        o_ref[...] = ((q_ref[0] + r_ref[0]) + r_ref[1]) + r_ref[2]

    return pl.pallas_call(
        body, name="sum_slots",
        grid_spec=pltpu.PrefetchScalarGridSpec(
            num_scalar_prefetch=1, grid=(r // tr,),
            in_specs=[pl.BlockSpec((1, tr, cdim), lambda i, j_ref: (j_ref[0], i, 0)),
                      pl.BlockSpec((3, tr, cdim), lambda i, j_ref: (0, i, 0))],
            out_specs=pl.BlockSpec((tr, cdim), lambda i, j_ref: (i, 0))),
        out_shape=jax.ShapeDtypeStruct((r, cdim), q.dtype),
        compiler_params=_params(("parallel",)),
    )(shard, q, r3)


def _allreduce_small(pack):
    def body(p_ref, out_ref, rbuf, s1, r1, s2, r2):
        me = _my_index()
        chunk = lambda d: pl.ds(pl.multiple_of(d * AR_CHUNK, 8), AR_CHUNK)
        sends = []
        for k in range(1, NDEV):
            cp = _rcopy(p_ref.at[chunk(me ^ k)], rbuf.at[me], s1.at[k - 1], r1.at[k - 1], _peer(k))
            cp.start()
            sends.append(cp)
        rbuf[me] = p_ref[chunk(me), :]
        for k in range(1, NDEV):
            _rcopy(p_ref.at[chunk(me)], rbuf.at[me ^ k], s1.at[k - 1], r1.at[k - 1], _peer(k)).wait_recv()
        tot = rbuf[0]
        for d in range(1, NDEV):
            tot = tot + rbuf[d]
        out_ref[chunk(me), :] = tot
        for k in range(1, NDEV):
            cp = _rcopy(out_ref.at[chunk(me)], out_ref.at[chunk(me)], s2.at[k - 1], r2.at[k - 1], _peer(k))
            cp.start()
            sends.append(cp)
        for k in range(1, NDEV):
            _rcopy(out_ref.at[chunk(me)], out_ref.at[chunk(me ^ k)], s2.at[k - 1], r2.at[k - 1], _peer(k)).wait_recv()
        for cp in sends:
            cp.wait_send()

    vm = pl.BlockSpec(memory_space=pltpu.VMEM)
    return pl.pallas_call(
        body, name="allreduce_small",
        out_shape=jax.ShapeDtypeStruct((AR_ROWS, D), F32),
        in_specs=[vm], out_specs=vm,
        scratch_shapes=[pltpu.VMEM((NDEV, AR_CHUNK, D), F32),
                        pltpu.SemaphoreType.DMA((7,)), pltpu.SemaphoreType.DMA((7,)),
                        pltpu.SemaphoreType.DMA((7,)), pltpu.SemaphoreType.DMA((7,))],
        compiler_params=_params(),
    )(pack)


def _adamw(w, g, m, v, tr):
    r, cdim = w.shape

    def body(w_ref, g_ref, m_ref, v_ref, d_ref, nm_ref, nv_ref):
        gv = g_ref[...]
        nm = B1 * m_ref[...] + (1.0 - B1) * gv
        nv = B2 * v_ref[...] + (1.0 - B2) * (gv * gv)
        m_hat = nm / (1.0 - B1 ** STEP)
        v_hat = nv / (1.0 - B2 ** STEP)
        d_ref[...] = -LR * (m_hat / (jnp.sqrt(v_hat) + ADAM_EPS) + WD * w_ref[...])
        nm_ref[...] = nm
        nv_ref[...] = nv

    spec = pl.BlockSpec((tr, cdim), lambda i: (i, 0))
    sd = jax.ShapeDtypeStruct((r, cdim), F32)
    return pl.pallas_call(
        body, name="adamw", grid=(r // tr,), in_specs=[spec] * 4, out_specs=(spec,) * 3, out_shape=(sd,) * 3,
        compiler_params=_params(("parallel",)),
    )(w, g, m, v)


def _rope_tables(positions):
    inv_freq = ROPE_THETA ** (-jnp.arange(0, ROT, 2, dtype=F32) / ROT)
    ang = positions.astype(F32)[:, None] * inv_freq
    cos, sin = jnp.cos(ang), jnp.sin(ang)
    n = positions.shape[0]
    half = ROT // 2
    rc = jnp.concatenate([cos, cos, jnp.ones((n, DH - ROT), F32)], axis=1)
    rsa = jnp.concatenate([-sin, jnp.zeros((n, DH - half), F32)], axis=1)
    rsb = jnp.concatenate([jnp.zeros((n, half), F32), sin, jnp.zeros((n, DH - ROT), F32)], axis=1)
    return rc, rsa, rsb


def kernel(x, c, positions, g_norm, w_mod, b_mod, w_in, b_gate, conv_w, conv_b, w_a, b_a, w_x, b_x, lam, w_out_rnn, w_out_attn, w_o, g_final, loss_target, m_g_norm, m_w_mod, m_b_mod, m_w_in, m_b_gate, m_conv_w, m_conv_b, m_w_a, m_b_a, m_w_x, m_b_x, m_lam, m_w_out_rnn, m_w_out_attn, m_w_o, m_g_final, v_g_norm, v_w_mod, v_b_mod, v_w_in, v_b_gate, v_conv_w, v_conv_b, v_w_a, v_b_a, v_w_x, v_b_x, v_lam, v_w_out_rnn, v_w_out_attn, v_w_o, v_g_final):
    s = x.shape[1]
    xi = lax.axis_index("x")
    yi = lax.axis_index("y")
    ci = lax.axis_index("c")
    shard = 2 * xi + yi
    x2d = x[0]
    tgt = loss_target[0]
    pos = positions[0]

    c_all, mod4 = _mod_fwd(c, w_mod[0], b_mod.reshape(4, 1, 768))
    mod = mod4.reshape(1, 3 * D)
    shift, scale, gate = mod[:, :D], mod[:, D:2 * D], mod[:, 2 * D:]
    w3_sh = jnp.stack([w_out_rnn[0], w_out_attn[0], w_o[0]]).astype(BF16)
    wsh = jnp.concatenate([w_in[0].astype(BF16), w3_sh.reshape(384, PW)], axis=0)
    wg = lax.dynamic_update_slice(_gather_weights(wsh), wsh[None], (shard, 0, 0))
    w3 = wg[:, D:, :].reshape(4, 3, 256, D).transpose(1, 0, 2, 3).reshape(3, D, D)
    conv_all = _gather_small(conv_w[0])
    conv_full = conv_all[0::2].transpose(1, 0, 2).reshape(4, D)

    p, hbf = _norm_inproj(x2d, g_norm, shift, scale, wg)
    rc, rsa, rsb = _rope_tables(pos)
    pos_col = pos.reshape(s, 1)
    b_a3, b_x3 = b_a.reshape(H, 1, DH), b_x.reshape(H, 1, DH)
    hr, gr = _rnn_fwd(p, pos_col, conv_full, conv_b, w_a[0], b_a3, w_x[0], b_x3, lam)
    o, lse, ga = _attn_fwd(p, rc, rsa, rsb)

    dgr, dga, dc, dx2, vec_t, g_out = _tail(gr, ga, p, x2d, tgt, w3, b_gate, gate, g_final.reshape(1, D))

    dxr, dzr, g_wa, g_ba, g_wx, g_bx, g_lam, g_cw, g_cb = _rnn_bwd(
        p, hr, dgr, pos_col, conv_full, conv_b, w_a[0], b_a3, w_x[0], b_x3, lam)
    dq, dk, dv, dza = _attn_bwd(p, o, lse, dga, rc, rsa, rsb)

    pieces = [dxr, dzr, dq, dk, dv, dza, dc]
    g_win = _inproj_bwd_w(pieces, hbf)

    core = ci.reshape(1)
    shard1 = shard.reshape(1)
    rb_a, rb_b = _pair_exchange([g_win, g_out])
    q_a, q_b = _add_half(g_win, rb_a, core, tr=256), _add_half(g_out, rb_b, core, tr=128)
    grad_x, vec_n, (r_a, r_b) = _inproj_bwd_x(pieces, wg, x2d, dx2, g_norm, scale, [q_a, q_b])
    f_a, f_b = _sum_slots(q_a, r_a, shard1, tr=256), _sum_slots(q_b, r_b, shard1, tr=128)
    s_a, s_b = _pair_swap([f_a, f_b])
    south = ci == 0
    grad_w_in = jnp.where(south, jnp.concatenate([f_a, s_a], axis=0), jnp.concatenate([s_a, f_a], axis=0))
    g3 = jnp.where(south, jnp.concatenate([f_b, s_b], axis=0), jnp.concatenate([s_b, f_b], axis=0)).reshape(3, 256, D)

    dmod_row = jnp.concatenate([vec_n[0:1], vec_n[1:2], vec_t[1:2]], axis=1)
    pack = jnp.concatenate([
        vec_n[2:3],
        dmod_row.reshape(3, D),
        vec_t[2:4],
        g_cb.reshape(1, D),
        g_wa.reshape(128, D),
        g_ba.reshape(1, D),
        g_wx.reshape(128, D),
        g_bx.reshape(1, D),
        g_lam.reshape(1, D),
        vec_t[0:1],
        g_cw.transpose(1, 0, 2).reshape(4, D),
        vec_t[4:5],
        jnp.zeros((AR_ROWS - 272, D), F32)], axis=0)
    red = _allreduce_small(pack)
    loss = red[271, 0]
    grad_w_mod = _mod_bwd(dmod_row.reshape(4, 1, 768), c_all)
    g_conv_sh = lax.dynamic_slice_in_dim(red[267:271], shard * 256, 256, axis=1)

    def small_pack(g_norm_, b_mod_, b_gate_, conv_b_, w_a_, b_a_, w_x_, b_x_, lam_, g_final_, conv_w_):
        return jnp.concatenate([
            g_norm_.reshape(1, D), b_mod_.reshape(3, D), b_gate_.reshape(2, D), conv_b_.reshape(1, D),
            w_a_.reshape(128, D), b_a_.reshape(1, D), w_x_.reshape(128, D), b_x_.reshape(1, D),
            lam_.reshape(1, D), g_final_.reshape(1, D), conv_w_.reshape(1, D),
            jnp.zeros((4, D), F32)], axis=0)

    wp = small_pack(g_norm, b_mod, b_gate, conv_b, w_a, b_a, w_x, b_x, lam, g_final, conv_w)
    mp = small_pack(m_g_norm, m_b_mod, m_b_gate, m_conv_b, m_w_a, m_b_a, m_w_x, m_b_x, m_lam, m_g_final, m_conv_w)
    vp = small_pack(v_g_norm, v_b_mod, v_b_gate, v_conv_b, v_w_a, v_b_a, v_w_x, v_b_x, v_lam, v_g_final, v_conv_w)
    gp = jnp.concatenate([red[0:267], g_conv_sh.reshape(1, D), jnp.zeros((4, D), F32)], axis=0)
    small = _adamw(wp, gp, mp, vp, tr=136)

    def unpack(a):
        return dict(
            g_norm=a[0:1], b_mod=a[1:4].reshape(1, 3 * D), b_gate=a[4:6].reshape(1, 2 * D), conv_b=a[6:7],
            w_a=a[7:135].reshape(1, H, DH, DH), b_a=a[135:136].reshape(1, H, DH),
            w_x=a[136:264].reshape(1, H, DH, DH), b_x=a[264:265].reshape(1, H, DH), lam=a[265:266],
            g_final=a[266].reshape(D), conv_w=a[267:268].reshape(1, 4, 256))

    big_in = _adamw(w_in[0], grad_w_in, m_w_in[0], v_w_in[0], tr=256)
    big_mod = _adamw(w_mod[0], grad_w_mod, m_w_mod[0], v_w_mod[0], tr=256)
    w3f = jnp.concatenate([w_out_rnn[0], w_out_attn[0], w_o[0]], axis=0)
    m3f = jnp.concatenate([m_w_out_rnn[0], m_w_out_attn[0], m_w_o[0]], axis=0)
    v3f = jnp.concatenate([v_w_out_rnn[0], v_w_out_attn[0], v_w_o[0]], axis=0)
    big_out = _adamw(w3f, g3.reshape(768, D), m3f, v3f, tr=256)

    names = ["g_norm", "w_mod", "b_mod", "w_in", "b_gate", "conv_w", "conv_b", "w_a", "b_a", "w_x", "b_x", "lam",
             "w_out_rnn", "w_out_attn", "w_o", "g_final"]
    grads = unpack(gp)
    grads.update(w_mod=grad_w_mod[None], w_in=grad_w_in[None],
                 w_out_rnn=g3[0][None], w_out_attn=g3[1][None], w_o=g3[2][None])
    outs = [grads]
    for idx in range(3):
        d = unpack(small[idx])
        d.update(w_mod=big_mod[idx][None], w_in=big_in[idx][None],
                 w_out_rnn=big_out[idx][0:256][None], w_out_attn=big_out[idx][256:512][None],
                 w_o=big_out[idx][512:768][None])
        outs.append(d)
    flat = [d[n] for d in outs for n in names]
    return (loss, grad_x[None], *flat)
```

```python
import jax
import jax.numpy as jnp
from jax import lax
from jax.experimental import pallas as pl
from jax.experimental.pallas import tpu as pltpu

F32, BF16 = jnp.float32, jnp.bfloat16
MESH = pl.DeviceIdType.MESH
HIGHEST = lax.Precision.HIGHEST

D = 1024
H = 8
DH = 128
PW = 2048
EPS = 1e-6
LRU_C = 8.0
SCALE = DH ** -0.5
NEG = -1e30
SPAN = 2048
UB = 128
DILATIONS = (1, 4, 16)
UNIT_BATCH = 8
UNIT_UNROLL = 8
ROPE_THETA = 500000.0
ROT = 32

LR, B1, B2, ADAM_EPS, WD, STEP = 0.001, 0.9, 0.999, 1e-08, 0.01, 10

WROWS = 1024 + 384
NDEV = 8
AR_ROWS = 320
AR_CHUNK = AR_ROWS // NDEV


def _params(sem=None, vmem_mb=56):
    return pltpu.CompilerParams(dimension_semantics=sem, vmem_limit_bytes=vmem_mb * 2 ** 20)


def _coords():
    return lax.axis_index("x"), lax.axis_index("y"), lax.axis_index("c")


def _flip(v, bit):
    return 1 - v if bit else v


def _peer(k):
    x, y, c = _coords()
    return (_flip(x, (k >> 2) & 1), _flip(y, (k >> 1) & 1), _flip(c, k & 1))


def _my_index():
    x, y, c = _coords()
    return 4 * x + 2 * y + c


def _rcopy(src, dst, ssem, rsem, dev):
    return pltpu.make_async_remote_copy(src_ref=src, dst_ref=dst, send_sem=ssem, recv_sem=rsem,
                                        device_id=dev, device_id_type=MESH)


def _sigmoid(x):
    return jax.nn.sigmoid(x)


def _dot(a, b):
    return jnp.dot(a, b, preferred_element_type=F32)


def _dot_nt(a, b):
    return lax.dot_general(a, b, (((1,), (1,)), ((), ())), preferred_element_type=F32)


def _dot_tn(a, b):
    return lax.dot_general(a, b, (((0,), (0,)), ((), ())), preferred_element_type=F32)


def _colsum(a):
    return jnp.sum(a, axis=0, keepdims=True)


def _mod_fwd(c, w_mod_sh, b_mod4):
    def body(c_ref, w_ref, b_ref, call_ref, mod_ref, rows_ref, cmat_ref, s1, r1, s2, r2):
        x, y, _ = _coords()
        me = _my_index()
        j = 2 * x + y
        call_ref[me] = c_ref[...]
        sends = []
        for k in range(1, NDEV):
            cp = _rcopy(call_ref.at[me], call_ref.at[me], s1.at[k - 1], r1.at[k - 1], _peer(k))
            cp.start()
            sends.append(cp)
        for k in range(1, NDEV):
            pk = me ^ k
            _rcopy(call_ref.at[pk], call_ref.at[pk], s1.at[k - 1], r1.at[k - 1], _peer(k)).wait_recv()
        for b in range(NDEV):
            cmat_ref[pl.ds(b, 1), :] = call_ref[b]
        cm = cmat_ref[...]
        act = cm * _sigmoid(cm)
        mp = jnp.dot(act, w_ref[...], preferred_element_type=F32, precision=HIGHEST) + b_ref[j]
        for b in range(NDEV):
            rows_ref[b] = mp[b:b + 1]
        mod_ref[j] = rows_ref[me]
        for q, k in enumerate((2, 4, 6)):
            cp = _rcopy(rows_ref.at[me ^ k], mod_ref.at[j], s2.at[q], r2.at[q], _peer(k))
            cp.start()
            sends.append(cp)
        for q, k in enumerate((2, 4, 6)):
            jq = j ^ (k >> 1)
            _rcopy(rows_ref.at[me], mod_ref.at[jq], s2.at[q], r2.at[q], _peer(k)).wait_recv()
        for cp in sends:
            cp.wait_send()

    vm = pl.BlockSpec(memory_space=pltpu.VMEM)
    return pl.pallas_call(
        body, name="mod_fwd",
        out_shape=(jax.ShapeDtypeStruct((NDEV, 1, D), F32), jax.ShapeDtypeStruct((4, 1, 768), F32)),
        in_specs=[vm, vm, vm], out_specs=(vm, vm),
        scratch_shapes=[pltpu.VMEM((NDEV, 1, 768), F32), pltpu.VMEM((NDEV, D), F32),
                        pltpu.SemaphoreType.DMA((7,)), pltpu.SemaphoreType.DMA((7,)),
                        pltpu.SemaphoreType.DMA((3,)), pltpu.SemaphoreType.DMA((3,))],
        compiler_params=_params(),
    )(c, w_mod_sh, b_mod4)


def _mod_bwd(dmod4, c_all):
    def body(d_ref, call_ref, gw_ref, dall_ref, cmat_ref, dmat_ref, s1, r1):
        x, y, _ = _coords()
        me = _my_index()
        j = 2 * x + y
        dall_ref[me] = d_ref[...]
        sends = []
        for k in range(1, NDEV):
            cp = _rcopy(dall_ref.at[me], dall_ref.at[me], s1.at[k - 1], r1.at[k - 1], _peer(k))
            cp.start()
            sends.append(cp)
        for k in range(1, NDEV):
            pk = me ^ k
            _rcopy(dall_ref.at[pk], dall_ref.at[pk], s1.at[k - 1], r1.at[k - 1], _peer(k)).wait_recv()
        for cp in sends:
            cp.wait_send()
        for b in range(NDEV):
            cmat_ref[pl.ds(b, 1), :] = call_ref[b]
            dmat_ref[pl.ds(b, 1), :] = dall_ref[b, j]
        cm = cmat_ref[...]
        act = cm * _sigmoid(cm)
        gw_ref[...] = lax.dot_general(act, dmat_ref[...], (((0,), (0,)), ((), ())),
                                      preferred_element_type=F32, precision=HIGHEST)

    vm = pl.BlockSpec(memory_space=pltpu.VMEM)
    return pl.pallas_call(
        body, name="mod_bwd",
        out_shape=jax.ShapeDtypeStruct((D, 768), F32),
        in_specs=[vm, vm], out_specs=vm,
        scratch_shapes=[pltpu.VMEM((NDEV, 4, 1, 768), F32), pltpu.VMEM((NDEV, D), F32), pltpu.VMEM((NDEV, 768), F32),
                        pltpu.SemaphoreType.DMA((7,)), pltpu.SemaphoreType.DMA((7,))],
        compiler_params=_params(),
    )(dmod4, c_all)


def _gather_weights(wsh):
    rows, cols = wsh.shape
    half = rows // 2
    nch = 4
    cr = half // nch

    def body(w_ref, g_ref, ss, rs):
        x, y, c = _coords()
        j = 2 * x + y
        sib = (x, y, 1 - c)
        sends = []
        chips = ((1, 0), (0, 1), (1, 1))
        mine = lambda n: pl.ds(c * half + n * cr, cr)
        theirs = lambda n: pl.ds((1 - c) * half + n * cr, cr)
        for n in range(nch):
            for q, (kx, ky) in enumerate(chips):
                e = nch * q + n
                cp = _rcopy(w_ref.at[mine(n)], g_ref.at[j, mine(n)], ss.at[e], rs.at[e],
                            (_flip(x, kx), _flip(y, ky), c))
                cp.start()
                sends.append(cp)
        for n in range(nch):
            for q, (kx, ky) in enumerate(chips):
                jq = j ^ (2 * kx + ky)
                e = nch * q + n
                _rcopy(w_ref.at[mine(n)], g_ref.at[jq, mine(n)], ss.at[e], rs.at[e], sib).wait_recv()
                cp = _rcopy(g_ref.at[jq, mine(n)], g_ref.at[jq, mine(n)], ss.at[3 * nch + e], rs.at[3 * nch + e], sib)
                cp.start()
                sends.append(cp)
        for n in range(nch):
            for q, (kx, ky) in enumerate(chips):
                jq = j ^ (2 * kx + ky)
                e = nch * q + n
                _rcopy(w_ref.at[mine(n)], g_ref.at[jq, theirs(n)], ss.at[3 * nch + e], rs.at[3 * nch + e], sib).wait_recv()
        for cp in sends:
            cp.wait_send()

    any_ = pl.BlockSpec(memory_space=pl.ANY)
    return pl.pallas_call(
        body, name="gather_weights",
        out_shape=jax.ShapeDtypeStruct((4, rows, cols), wsh.dtype),
        in_specs=[any_], out_specs=any_,
        scratch_shapes=[pltpu.SemaphoreType.DMA((6 * nch,)), pltpu.SemaphoreType.DMA((6 * nch,))],
        compiler_params=_params(),
    )(wsh)


def _gather_small(v):
    r, cdim = v.shape

    def body(v_ref, out_ref, ss, rs):
        me = _my_index()
        out_ref[me] = v_ref[...]
        sends = []
        for k in range(1, NDEV):
            cp = _rcopy(out_ref.at[me], out_ref.at[me], ss.at[k - 1], rs.at[k - 1], _peer(k))
            cp.start()
            sends.append(cp)
        for k in range(1, NDEV):
            pk = me ^ k
            _rcopy(out_ref.at[pk], out_ref.at[pk], ss.at[k - 1], rs.at[k - 1], _peer(k)).wait_recv()
        for cp in sends:
            cp.wait_send()

    vm = pl.BlockSpec(memory_space=pltpu.VMEM)
    return pl.pallas_call(
        body, name="gather_small",
        out_shape=jax.ShapeDtypeStruct((NDEV, r, cdim), v.dtype),
        in_specs=[vm], out_specs=vm,
        scratch_shapes=[pltpu.SemaphoreType.DMA((7,)), pltpu.SemaphoreType.DMA((7,))],
        compiler_params=_params(),
    )(v)


def _load_w_in(w_hbm, w_s, sem, first, piece, piece_start):
    def copy(pc):
        return pltpu.make_async_copy(w_hbm.at[pc, pl.ds(0, D), :], w_s.at[pc], sem.at[pc])

    @pl.when(first)
    def _():
        for pc in range(4):
            copy(pc).start()

    @pl.when(piece_start)
    def _():
        copy(piece).wait()


def _norm_inproj(x, gn, shift, scale, wg, tm=1024, tn=512):
    s = x.shape[0]
    npc = PW // tn

    def body(x_ref, gn_ref, sh_ref, sc_ref, w_hbm, p_ref, h_ref, hs, w_s, sem):
        i, pc, col = pl.program_id(0), pl.program_id(1), pl.program_id(2)
        _load_w_in(w_hbm, w_s, sem, (i == 0) & (pc == 0) & (col == 0), pc, (i == 0) & (col == 0))

        @pl.when((pc == 0) & (col == 0))
        def _():
            xt = x_ref[...]
            rstd = lax.rsqrt(jnp.mean(xt * xt, axis=-1, keepdims=True) + EPS)
            h = (xt * rstd * gn_ref[...]) * (1.0 + sc_ref[...]) + sh_ref[...]
            hs[...] = h.astype(BF16)
            h_ref[...] = hs[...]

        p_ref[0] = _dot(hs[...], w_s[pc, :, pl.ds(pl.multiple_of(col * tn, tn), tn)]).astype(BF16)

    row = pl.BlockSpec((1, D), lambda i, pc, col: (0, 0))
    return pl.pallas_call(
        body, name="norm_inproj", grid=(s // tm, 4, npc),
        in_specs=[pl.BlockSpec((tm, D), lambda i, pc, col: (i, 0)), row, row, row,
                  pl.BlockSpec(memory_space=pl.ANY)],
        out_specs=(pl.BlockSpec((1, tm, tn), lambda i, pc, col: (pc, i, col)),
                   pl.BlockSpec((tm, D), lambda i, pc, col: (i, 0))),
        out_shape=(jax.ShapeDtypeStruct((4, s, PW), BF16), jax.ShapeDtypeStruct((s, D), BF16)),
        scratch_shapes=[pltpu.VMEM((tm, D), BF16), pltpu.VMEM((4, D, PW), BF16), pltpu.SemaphoreType.DMA((4,))],
        compiler_params=_params(("arbitrary", "arbitrary", "arbitrary")),
    )(x, gn, shift, scale, wg)


def _shift_down(prev8, cur, d):
    t = cur.shape[0]
    c3 = cur.reshape(t // 8, 8, DH)
    rot = pltpu.roll(c3, d, 1)
    before = jnp.concatenate([pltpu.roll(prev8, d, 0).reshape(1, 8, DH), rot[:-1]], axis=0)
    rows = lax.broadcasted_iota(jnp.int32, c3.shape, 1)
    return jnp.where(rows >= d, rot, before).reshape(t, DH)


def _shift_up(cur, next8, d):
    t = cur.shape[0]
    c3 = cur.reshape(t // 8, 8, DH)
    rot = pltpu.roll(c3, 8 - d, 1)
    after = jnp.concatenate([rot[1:], pltpu.roll(next8, 8 - d, 0).reshape(1, 8, DH)], axis=0)
    rows = lax.broadcasted_iota(jnp.int32, c3.shape, 1)
    return jnp.where(rows < 8 - d, rot, after).reshape(t, DH)


def _rnn_gates(xr, prev8, cw, cb, wa, ba, wx, bx, lam, reset):
    xc = cw[3:4] * xr + cb
    for d in (1, 2, 3):
        xc = xc + cw[3 - d:4 - d] * _shift_down(prev8, xr, d)
    xcb = xc.astype(BF16)
    r = _sigmoid(_dot(xcb, wa.astype(BF16)) + ba)
    ig = _sigmoid(_dot(xcb, wx.astype(BF16)) + bx)
    nl = -lam
    sp = jnp.maximum(nl, 0.0) + jnp.log1p(jnp.exp(-jnp.abs(nl)))
    log_a = (-LRU_C * r) * sp
    a = jnp.where(reset, 0.0, jnp.exp(log_a))
    mult = jnp.where(reset, 1.0, jnp.sqrt(1.0 - jnp.exp(2.0 * log_a)))
    return xc, r, ig, sp, a, mult


def _log_scan(a, b, axis, up):
    n = a.shape[axis]
    rows = lax.broadcasted_iota(jnp.int32, a.shape, axis)
    d = 1
    while d < n:
        m = rows < n - d if up else rows >= d
        shift = n - d if up else d
        a_s = pltpu.roll(a, shift, axis)
        b_s = pltpu.roll(b, shift, axis)
        b = jnp.where(m, a * b_s + b, b)
        a = jnp.where(m, a * a_s, a)
        d *= 2
    return a, b


def _scan(a, b, t, edge, up=False):
    g = t // 8
    a3, b3 = _log_scan(a.reshape(g, 8, DH), b.reshape(g, 8, DH), 1, up)
    last = 0 if up else 7
    ag, bg = _log_scan(a3[:, last, :], b3[:, last, :], 0, up)
    hg = ag * edge + bg
    grp = lax.broadcasted_iota(jnp.int32, hg.shape, 0)
    if up:
        cin = jnp.where(grp == g - 1, edge, pltpu.roll(hg, g - 1, 0))
        tail = hg[0:1]
    else:
        cin = jnp.where(grp == 0, edge, pltpu.roll(hg, 1, 0))
        tail = hg[g - 1:g]
    return (a3 * cin[:, None, :] + b3).reshape(t, DH), tail


def _rnn_fwd(p, pos, conv_w, conv_b, w_a, b_a, w_x, b_x, lam, tt=512):
    s = p.shape[1]
    nt = s // tt

    def body(xr_ref, z_ref, pos_ref, cw_ref, cb_ref, wa_ref, ba_ref, wx_ref, bx_ref, lam_ref,
             hr_ref, gr_ref, xprev, hprev):
        @pl.when(pl.program_id(1) == 0)
        def _():
            xprev[...] = jnp.zeros_like(xprev)
            hprev[...] = jnp.zeros_like(hprev)

        xr = xr_ref[0].astype(F32)
        z = z_ref[0].astype(F32)
        reset = pos_ref[...] == 0
        xc, r, ig, sp, a, mult = _rnn_gates(xr, xprev[...], cw_ref[...], cb_ref[...], wa_ref[0], ba_ref[0],
                                            wx_ref[0], bx_ref[0], lam_ref[...], reset)
        bx = mult * ig * xc
        h, h_last = _scan(a, bx, tt, hprev[0:1])
        xprev[...] = xr[tt - 8:]
        hprev[...] = jnp.broadcast_to(h_last, (8, DH))
        hr_ref[...] = h
        gr_ref[...] = (h * (z * _sigmoid(z))).astype(BF16)

    head_row = lambda hh, t: (0, hh)
    return pl.pallas_call(
        body, name="rnn_fwd", grid=(H, nt),
        in_specs=[pl.BlockSpec((1, tt, DH), lambda hh, t: (0, t, hh)),
                  pl.BlockSpec((1, tt, DH), lambda hh, t: (0, t, H + hh)),
                  pl.BlockSpec((tt, 1), lambda hh, t: (t, 0)),
                  pl.BlockSpec((4, DH), head_row), pl.BlockSpec((1, DH), head_row),
                  pl.BlockSpec((1, DH, DH), lambda hh, t: (hh, 0, 0)), pl.BlockSpec((1, 1, DH), lambda hh, t: (hh, 0, 0)),
                  pl.BlockSpec((1, DH, DH), lambda hh, t: (hh, 0, 0)), pl.BlockSpec((1, 1, DH), lambda hh, t: (hh, 0, 0)),
                  pl.BlockSpec((1, DH), head_row)],
        out_specs=(pl.BlockSpec((tt, DH), lambda hh, t: (t, hh)), pl.BlockSpec((tt, DH), lambda hh, t: (t, hh))),
        out_shape=(jax.ShapeDtypeStruct((s, D), F32), jax.ShapeDtypeStruct((s, D), BF16)),
        scratch_shapes=[pltpu.VMEM((8, DH), F32), pltpu.VMEM((8, DH), F32)],
        compiler_params=_params(("parallel", "arbitrary")),
    )(p, p, pos, conv_w, conv_b, w_a, b_a, w_x, b_x, lam)


def _rnn_bwd(p, hr, dgr, pos, conv_w, conv_b, w_a, b_a, w_x, b_x, lam, tt=512):
    s = p.shape[1]
    nt = s // tt
    t8 = tt // 8

    def body(xr_ref, z_ref, xp_ref, hr_ref, hp_ref, dg_ref, pos_ref, cw_ref, cb_ref, wa_ref, ba_ref, wx_ref, bx_ref,
             lam_ref, dxr_ref, dz_ref, gwa_ref, gba_ref, gwx_ref, gbx_ref, glam_ref, gcw_ref, gcb_ref,
             a_next, g_next, dxc_next):
        t = pl.program_id(1)
        has_prev = t < nt - 1

        @pl.when(t == 0)
        def _():
            a_next[...] = jnp.zeros_like(a_next)
            g_next[...] = jnp.zeros_like(g_next)
            dxc_next[...] = jnp.zeros_like(dxc_next)
            gwa_ref[...] = jnp.zeros_like(gwa_ref)
            gba_ref[...] = jnp.zeros_like(gba_ref)
            gwx_ref[...] = jnp.zeros_like(gwx_ref)
            gbx_ref[...] = jnp.zeros_like(gbx_ref)
            glam_ref[...] = jnp.zeros_like(glam_ref)
            gcw_ref[...] = jnp.zeros_like(gcw_ref)
            gcb_ref[...] = jnp.zeros_like(gcb_ref)

        xr = xr_ref[0].astype(F32)
        z = z_ref[0].astype(F32)
        hr_blk = hr_ref[...]
        dg = dg_ref[...]
        xprev = jnp.where(has_prev, xp_ref[0].astype(F32)[8:], 0.0)
        hprev8 = jnp.where(has_prev, hp_ref[...], 0.0)
        reset = pos_ref[...] == 0
        cw = cw_ref[...]
        wa = wa_ref[0]
        wx = wx_ref[0]
        lam_v = lam_ref[...]
        xc, r, ig, sp, a, mult = _rnn_gates(xr, xprev, cw, cb_ref[...], wa, ba_ref[0], wx, bx_ref[0], lam_v, reset)

        sz = _sigmoid(z)
        dh = dg * (z * sz)
        dz_ref[...] = (dg * hr_blk * (sz * (1.0 + z * (1.0 - sz)))).astype(BF16)

        an = _shift_up(a, a_next[...], 1)
        g, g_first = _scan(an, dh, tt, g_next[0:1], up=True)
        a_next[...] = jnp.broadcast_to(a[0:1], (8, DH))
        g_next[...] = jnp.broadcast_to(g_first, (8, DH))

        hm1 = _shift_down(hprev8, hr_blk, 1)
        da = g * hm1
        dmult = g * (ig * xc)
        di = g * (mult * xc)
        dxc = g * (mult * ig)
        dla = jnp.where(reset, 0.0, da * a - dmult * (a * a) / mult)
        dr = dla * (-LRU_C * sp)
        dsp = _colsum(dla * (-LRU_C * r))
        glam_ref[0] += dsp * (-_sigmoid(-lam_v))
        dpa = dr * r * (1.0 - r)
        dpx = di * ig * (1.0 - ig)
        dpab = dpa.astype(BF16)
        dpxb = dpx.astype(BF16)
        dxc = dxc + _dot_nt(dpab, wa.astype(BF16)) + _dot_nt(dpxb, wx.astype(BF16))
        xcb = xc.astype(BF16)
        gwa_ref[0] += _dot_tn(xcb, dpab)
        gwx_ref[0] += _dot_tn(xcb, dpxb)
        gba_ref[0] += _colsum(dpa)
        gbx_ref[0] += _colsum(dpx)

        dxr = cw[3:4] * dxc
        for d in (1, 2, 3):
            dxr = dxr + cw[3 - d:4 - d] * _shift_up(dxc, dxc_next[...], d)
        dxr_ref[...] = dxr.astype(BF16)
        dxc_next[...] = dxc[0:8]
        gcb_ref[0] += _colsum(dxc)
        gcw_ref[0, 3:4, :] += _colsum(xr * dxc)
        for d in (1, 2, 3):
            gcw_ref[0, 3 - d:4 - d, :] += _colsum(_shift_down(xprev, xr, d) * dxc)

    rt = lambda t: nt - 1 - t
    prev8 = lambda t: jnp.maximum(rt(t) * t8 - 1, 0)
    head_row = lambda hh, t: (0, hh)
    hsm = lambda hh, t: (hh, 0, 0)
    return pl.pallas_call(
        body, name="rnn_bwd", grid=(H, nt),
        in_specs=[pl.BlockSpec((1, tt, DH), lambda hh, t: (0, rt(t), hh)),
                  pl.BlockSpec((1, tt, DH), lambda hh, t: (0, rt(t), H + hh)),
                  pl.BlockSpec((1, 16, DH), lambda hh, t: (0, jnp.maximum(rt(t) * (tt // 16) - 1, 0), hh)),
                  pl.BlockSpec((tt, DH), lambda hh, t: (rt(t), hh)),
                  pl.BlockSpec((8, DH), lambda hh, t: (prev8(t), hh)),
                  pl.BlockSpec((tt, DH), lambda hh, t: (rt(t), hh)),
                  pl.BlockSpec((tt, 1), lambda hh, t: (rt(t), 0)),
                  pl.BlockSpec((4, DH), head_row), pl.BlockSpec((1, DH), head_row),
                  pl.BlockSpec((1, DH, DH), hsm), pl.BlockSpec((1, 1, DH), hsm),
                  pl.BlockSpec((1, DH, DH), hsm), pl.BlockSpec((1, 1, DH), hsm),
                  pl.BlockSpec((1, DH), head_row)],
        out_specs=(pl.BlockSpec((tt, DH), lambda hh, t: (rt(t), hh)), pl.BlockSpec((tt, DH), lambda hh, t: (rt(t), hh)),
                   pl.BlockSpec((1, DH, DH), hsm), pl.BlockSpec((1, 1, DH), hsm),
                   pl.BlockSpec((1, DH, DH), hsm), pl.BlockSpec((1, 1, DH), hsm),
                   pl.BlockSpec((1, 1, DH), hsm), pl.BlockSpec((1, 4, DH), hsm), pl.BlockSpec((1, 1, DH), hsm)),
        out_shape=(jax.ShapeDtypeStruct((s, D), BF16), jax.ShapeDtypeStruct((s, D), BF16),
                   jax.ShapeDtypeStruct((H, DH, DH), F32), jax.ShapeDtypeStruct((H, 1, DH), F32),
                   jax.ShapeDtypeStruct((H, DH, DH), F32), jax.ShapeDtypeStruct((H, 1, DH), F32),
                   jax.ShapeDtypeStruct((H, 1, DH), F32), jax.ShapeDtypeStruct((H, 4, DH), F32),
                   jax.ShapeDtypeStruct((H, 1, DH), F32)),
        scratch_shapes=[pltpu.VMEM((8, DH), F32), pltpu.VMEM((8, DH), F32), pltpu.VMEM((8, DH), F32)],
        compiler_params=_params(("parallel", "arbitrary")),
    )(p, p, p, hr, hr, dgr, pos, conv_w, conv_b, w_a, b_a, w_x, b_x, lam)


def _rope(t, c, sa, sb):
    return t * c + pltpu.roll(t, DH - ROT // 2, 1) * sa + pltpu.roll(t, ROT // 2, 1) * sb


def _rope_bwd(g, c, sa, sb):
    return g * c + pltpu.roll(g * sa, ROT // 2, 1) + pltpu.roll(g * sb, DH - ROT // 2, 1)


def _unit_bases(gi, u):
    dil = DILATIONS[gi]
    if dil == 1:
        return u * UB, SPAN + (u - 1) * UB, u == 0
    if dil == 4:
        blk, r = u // 4, u % 4
        return blk * 4 * UB + r, SPAN + (blk - 1) * 4 * UB + r, blk == 0
    return u, u, True


def _unit_slices(gi, u):
    dil = DILATIONS[gi]
    qb0, kb0, first = _unit_bases(gi, u)
    if dil == 1:
        return pl.ds(pl.multiple_of(qb0, UB), UB), pl.ds(pl.multiple_of(kb0, UB), 2 * UB), first
    return pl.ds(qb0, UB, stride=dil), pl.ds(kb0, 2 * UB, stride=dil), first


def _bdot(a, b):
    return lax.dot_general(a, b, (((2,), (1,)), ((0,), (0,))), preferred_element_type=F32)


def _bdot_nt(a, b):
    return lax.dot_general(a, b, (((2,), (2,)), ((0,), (0,))), preferred_element_type=F32)


def _bdot_tn(a, b):
    return lax.dot_general(a, b, (((1,), (1,)), ((0,), (0,))), preferred_element_type=F32)


def _band_mask(first_in_span, has_prev):
    qi = lax.broadcasted_iota(jnp.int32, (UB, 2 * UB), 0)
    ki = lax.broadcasted_iota(jnp.int32, (UB, 2 * UB), 1)
    dist = UB + qi - ki
    band = (dist >= 0) & (dist <= UB)
    return band & ((ki >= UB) | jnp.logical_not(first_in_span) | has_prev)


def _attn_fwd(p, rc, rsa, rsb):
    s = p.shape[1]
    ns = s // SPAN
    nunit = SPAN // UB

    def body(q_ref, k_ref, v_ref, z_ref, c_ref, sa_ref, sb_ref, o_ref, lse_ref, ga_ref,
             qr, kf, vf, acc, mm, ll):
        n = pl.program_id(1)

        @pl.when(n == 0)
        def _():
            kf[0:SPAN] = jnp.zeros((SPAN, DH), F32)
            vf[0:SPAN] = jnp.zeros((SPAN, DH), F32)

        c, sa, sb = c_ref[...], sa_ref[...], sb_ref[...]
        qr[...] = _rope(q_ref[0].astype(F32), c, sa, sb)
        kf[SPAN:] = _rope(k_ref[0].astype(F32), c, sa, sb)
        vf[SPAN:] = v_ref[0].astype(F32)
        has_prev = n > 0

        for gi, dil in enumerate(DILATIONS):
            def trip(t, carry, gi=gi, dil=dil):
                qsls, ksls, firsts = [], [], []
                for b in range(UNIT_BATCH):
                    qsl, ksl, first = _unit_slices(gi, t * UNIT_BATCH + b)
                    qsls.append(qsl)
                    ksls.append(ksl)
                    firsts.append(first)
                qb = jnp.stack([qr[qsl, :].astype(BF16) for qsl in qsls])
                kb = jnp.stack([kf[ksl, :].astype(BF16) for ksl in ksls])
                vb = jnp.stack([vf[ksl, :].astype(BF16) for ksl in ksls])
                s_all = _bdot_nt(qb, kb)
                prs = []
                for b in range(UNIT_BATCH):
                    sc = jnp.where(_band_mask(firsts[b], has_prev), s_all[b] * SCALE, NEG)
                    m = jnp.max(sc, axis=-1, keepdims=True)
                    pr = jnp.exp(sc - m)
                    l = jnp.sum(pr, axis=-1, keepdims=True)
                    mm[gi, qsls[b], :] = jnp.broadcast_to(m, (UB, DH))
                    ll[gi, qsls[b], :] = jnp.broadcast_to(l, (UB, DH))
                    prs.append(pr.astype(BF16))
                o_all = _bdot(jnp.stack(prs), vb)
                for b in range(UNIT_BATCH):
                    acc[gi, qsls[b], :] = o_all[b]
                return carry

            lax.fori_loop(0, nunit // UNIT_BATCH, trip, 0)

        m_all =jnp.maximum(jnp.maximum(mm[0], mm[1]), mm[2])
        num = jnp.zeros((SPAN, DH), F32)
        den = jnp.zeros((SPAN, DH), F32)
        for gi in range(3):
            w = jnp.exp(mm[gi] - m_all)
            num = num + w * acc[gi]
            den = den + w * ll[gi]
        o = num / den
        o_ref[...] = o
        lse_ref[...] = m_all + jnp.log(den)
        z = z_ref[0].astype(F32)
        ga_ref[...] = (o * (z * _sigmoid(z))).astype(BF16)
        kf[0:SPAN] = kf[SPAN:]
        vf[0:SPAN] = vf[SPAN:]

    blk = lambda piece, off: pl.BlockSpec((1, SPAN, DH), lambda hh, n: (piece, n, off + hh))
    tab = pl.BlockSpec((SPAN, DH), lambda hh, n: (n, 0))
    outb = pl.BlockSpec((SPAN, DH), lambda hh, n: (n, hh))
    return pl.pallas_call(
        body, name="attn_fwd", grid=(H, ns),
        in_specs=[blk(1, 0), blk(1, H), blk(2, 0), blk(2, H), tab, tab, tab],
        out_specs=(outb, outb, outb),
        out_shape=(jax.ShapeDtypeStruct((s, D), F32), jax.ShapeDtypeStruct((s, D), F32),
                   jax.ShapeDtypeStruct((s, D), BF16)),
        scratch_shapes=[pltpu.VMEM((SPAN, DH), F32), pltpu.VMEM((2 * SPAN, DH), F32), pltpu.VMEM((2 * SPAN, DH), F32),
                        pltpu.VMEM((3, SPAN, DH), F32), pltpu.VMEM((3, SPAN, DH), F32), pltpu.VMEM((3, SPAN, DH), F32)],
        compiler_params=_params(("parallel", "arbitrary")),
    )(p, p, p, p, rc, rsa, rsb)


def _attn_bwd(p, o, lse, dga, rc, rsa, rsb):
    s = p.shape[1]
    ns = s // SPAN
    nunit = SPAN // UB

    def body(q_ref, k_ref, kp_ref, v_ref, vp_ref, z_ref, c_ref, sa_ref, sb_ref, cp_ref, sap_ref, sbp_ref,
             o_ref, lse_ref, dg_ref, dq_ref, dk_ref, dv_ref, dz_ref,
             qr, kf, vf, dof, dlt, dqa, dkf, dvf):
        step = pl.program_id(1)
        n = ns - 1 - step
        has_prev = n > 0

        @pl.when(step == 0)
        def _():
            dkf[...] = jnp.zeros_like(dkf)
            dvf[...] = jnp.zeros_like(dvf)

        @pl.when(step > 0)
        def _():
            dkf[SPAN:] = dkf[0:SPAN]
            dvf[SPAN:] = dvf[0:SPAN]
            dkf[0:SPAN] = jnp.zeros((SPAN, DH), F32)
            dvf[0:SPAN] = jnp.zeros((SPAN, DH), F32)

        c, sa, sb = c_ref[...], sa_ref[...], sb_ref[...]
        qr[...] = _rope(q_ref[0].astype(F32), c, sa, sb)
        kf[SPAN:] = _rope(k_ref[0].astype(F32), c, sa, sb)
        vf[SPAN:] = v_ref[0].astype(F32)
        kf[0:SPAN] = jnp.where(has_prev, _rope(kp_ref[0].astype(F32), cp_ref[...], sap_ref[...], sbp_ref[...]), 0.0)
        vf[0:SPAN] = jnp.where(has_prev, vp_ref[0].astype(F32), 0.0)
        z = z_ref[0].astype(F32)
        sz = _sigmoid(z)
        dg = dg_ref[...]
        ov = o_ref[...]
        do = dg * (z * sz)
        dz_ref[...] = (dg * ov * (sz * (1.0 + z * (1.0 - sz)))).astype(BF16)
        dof[...] = do
        dlt[...] = jnp.broadcast_to(jnp.sum(do * ov, axis=-1, keepdims=True), (SPAN, DH))
        dqa[...] = jnp.zeros_like(dqa)

        for gi, dil in enumerate(DILATIONS):
            def trip(t, carry, gi=gi, dil=dil):
                qsls, ksls, firsts = [], [], []
                for b in range(UNIT_BATCH):
                    qsl, ksl, first = _unit_slices(gi, t * UNIT_BATCH + b)
                    qsls.append(qsl)
                    ksls.append(ksl)
                    firsts.append(first)
                qb = jnp.stack([qr[qsl, :].astype(BF16) for qsl in qsls])
                kb = jnp.stack([kf[ksl, :].astype(BF16) for ksl in ksls])
                vb = jnp.stack([vf[ksl, :].astype(BF16) for ksl in ksls])
                dob = jnp.stack([dof[qsl, :].astype(BF16) for qsl in qsls])
                s_all = _bdot_nt(qb, kb)
                dp_all = _bdot_nt(dob, vb)
                prs, dss = [], []
                for b in range(UNIT_BATCH):
                    lse_b = lse_ref[qsls[b], :]
                    dl_b = dlt[qsls[b], :]
                    pr = jnp.exp(s_all[b] * SCALE - jnp.concatenate([lse_b, lse_b], axis=1))
                    pr = jnp.where(_band_mask(firsts[b], has_prev), pr, 0.0)
                    prs.append(pr.astype(BF16))
                    dss.append((pr * (dp_all[b] - jnp.concatenate([dl_b, dl_b], axis=1)) * SCALE).astype(BF16))
                ds_all = jnp.stack(dss)
                dv_all = _bdot_tn(jnp.stack(prs), dob)
                dq_all = _bdot(ds_all, kb)
                dk_all = _bdot_tn(ds_all, qb)
                for b in range(UNIT_BATCH):
                    dvf[ksls[b], :] += dv_all[b]
                    dqa[qsls[b], :] += dq_all[b]
                    dkf[ksls[b], :] += dk_all[b]
                return carry

            lax.fori_loop(0, nunit // UNIT_BATCH, trip, 0)

        dq_ref[...] = _rope_bwd(dqa[...], c, sa, sb).astype(BF16)
        dk_ref[...] = _rope_bwd(dkf[SPAN:], c, sa, sb).astype(BF16)
        dv_ref[...] = dvf[SPAN:].astype(BF16)

    rn = lambda n: ns - 1 - n
    pn = lambda n: jnp.maximum(ns - 2 - n, 0)
    blk = lambda piece, off: pl.BlockSpec((1, SPAN, DH), lambda hh, n: (piece, rn(n), off + hh))
    blkp = lambda piece, off: pl.BlockSpec((1, SPAN, DH), lambda hh, n: (piece, pn(n), off + hh))
    tab = pl.BlockSpec((SPAN, DH), lambda hh, n: (rn(n), 0))
    tabp = pl.BlockSpec((SPAN, DH), lambda hh, n: (pn(n), 0))
    io = pl.BlockSpec((SPAN, DH), lambda hh, n: (rn(n), hh))
    return pl.pallas_call(
        body, name="attn_bwd", grid=(H, ns),
        in_specs=[blk(1, 0), blk(1, H), blkp(1, H), blk(2, 0), blkp(2, 0), blk(2, H),
                  tab, tab, tab, tabp, tabp, tabp, io, io, io],
        out_specs=(io, io, io, io),
        out_shape=tuple(jax.ShapeDtypeStruct((s, D), BF16) for _ in range(4)),
        scratch_shapes=[pltpu.VMEM((SPAN, DH), F32), pltpu.VMEM((2 * SPAN, DH), F32), pltpu.VMEM((2 * SPAN, DH), F32),
                        pltpu.VMEM((SPAN, DH), F32), pltpu.VMEM((SPAN, DH), F32), pltpu.VMEM((SPAN, DH), F32),
                        pltpu.VMEM((2 * SPAN, DH), F32), pltpu.VMEM((2 * SPAN, DH), F32)],
        compiler_params=_params(("parallel", "arbitrary")),
    )(p, p, p, p, p, p, rc, rsa, rsb, rc, rsa, rsb, o, lse, dga)


def _tail(gr, ga, p, x, tgt, w3, b_gate, gate, g_final, tm=256):
    s = x.shape[0]
    nt = s // tm

    def body(gr_ref, ga_ref, pr_ref, pa_ref, x_ref, t_ref, bg_ref, gate_ref, gf_ref, w_hbm,
             dgr_ref, dga_ref, dc_ref, dx2_ref, vec_ref, go_hbm, w_s, acc_s, sem):
        i = pl.program_id(0)

        @pl.when(i == 0)
        def _():
            cp = pltpu.make_async_copy(w_hbm, w_s, sem.at[12])
            cp.start()
            acc_s[...] = jnp.zeros_like(acc_s)
            vec_ref[...] = jnp.zeros_like(vec_ref)
            cp.wait()

        grb = gr_ref[...]
        gab = ga_ref[...]
        bg = bg_ref[...]
        gate_v = gate_ref[...]
        gf = gf_ref[...]
        y_r = _dot(grb, w_s[0])
        y_a = _dot(gab, w_s[1])
        sr = _sigmoid(pr_ref[0].astype(F32) + bg[:, :D])
        sa = _sigmoid(pa_ref[0].astype(F32) + bg[:, D:])
        mb = (sr * y_r + sa * y_a).astype(BF16)
        u = _dot(mb, w_s[2])
        x2 = x_ref[...] + gate_v * u
        rstd = lax.rsqrt(jnp.mean(x2 * x2, axis=-1, keepdims=True) + EPS)
        xh = x2 * rstd
        e = xh * gf - t_ref[...]
        dy = e * (1.0 / D)
        dyg = dy * gf
        dx2 = rstd * (dyg - xh * jnp.mean(dyg * xh, axis=-1, keepdims=True))
        dx2_ref[...] = dx2
        dub = (dx2 * gate_v).astype(BF16)
        dm = _dot_nt(dub, w_s[2])
        dyr = (dm * sr).astype(BF16)
        dya = (dm * sa).astype(BF16)
        dpr = dm * y_r * (sr * (1.0 - sr))
        dpa = dm * y_a * (sa * (1.0 - sa))
        dc_ref[:, :D] = dpr.astype(BF16)
        dc_ref[:, D:] = dpa.astype(BF16)
        dgr_ref[...] = _dot_nt(dyr, w_s[0])
        dga_ref[...] = _dot_nt(dya, w_s[1])
        acc_s[0] += _dot_tn(grb, dyr)
        acc_s[1] += _dot_tn(gab, dya)
        acc_s[2] += _dot_tn(mb, dub)
        vec_ref[0:1, :] += _colsum(dy * xh)
        vec_ref[1:2, :] += _colsum(dx2 * u)
        vec_ref[2:3, :] += _colsum(dpr)
        vec_ref[3:4, :] += _colsum(dpa)
        vec_ref[4:5, :] += _colsum(e * e)

        @pl.when(i == nt - 1)
        def _():
            vec_ref[4:5, :] = jnp.broadcast_to(jnp.sum(vec_ref[4:5, :]) * (0.5 / D), (1, D))
            cps = []
            for w in range(3):
                for j in range(4):
                    cps.append(pltpu.make_async_copy(acc_s.at[w, pl.ds(256 * j, 256)],
                                                     go_hbm.at[j, pl.ds(256 * w, 256)], sem.at[4 * w + j]))
            for cp in cps:
                cp.start()
            for cp in cps:
                cp.wait()

    rowt = lambda i: (i, 0)
    row = lambda w: pl.BlockSpec((1, w), lambda i: (0, 0))
    any_ = pl.BlockSpec(memory_space=pl.ANY)
    return pl.pallas_call(
        body, name="tail", grid=(nt,),
        in_specs=[pl.BlockSpec((tm, D), rowt), pl.BlockSpec((tm, D), rowt),
                  pl.BlockSpec((1, tm, D), lambda i: (3, i, 0)), pl.BlockSpec((1, tm, D), lambda i: (3, i, 1)),
                  pl.BlockSpec((tm, D), rowt), pl.BlockSpec((tm, D), rowt),
                  row(2 * D), row(D), row(D), any_],
        out_specs=(pl.BlockSpec((tm, D), rowt), pl.BlockSpec((tm, D), rowt), pl.BlockSpec((tm, 2 * D), rowt),
                   pl.BlockSpec((tm, D), rowt), pl.BlockSpec((8, D), lambda i: (0, 0)), any_),
        out_shape=(jax.ShapeDtypeStruct((s, D), F32), jax.ShapeDtypeStruct((s, D), F32),
                   jax.ShapeDtypeStruct((s, 2 * D), BF16), jax.ShapeDtypeStruct((s, D), F32),
                   jax.ShapeDtypeStruct((8, D), F32), jax.ShapeDtypeStruct((4, 768, D), F32)),
        scratch_shapes=[pltpu.VMEM((3, D, D), BF16), pltpu.VMEM((3, D, D), F32), pltpu.SemaphoreType.DMA((13,))],
        compiler_params=_params(("arbitrary",)),
    )(gr, ga, p, p, x, tgt, b_gate, gate, g_final, w3)


def _pieces_steps(pieces):
    out, s0 = [], 0
    for a in pieces:
        n = a.shape[1] // D
        out.append((s0, n))
        s0 += n
    return out, s0


def _inproj_bwd_x(pieces, wg, x, dx2, gn, scale, sums, tm=512):
    s = x.shape[0]
    steps, nk = _pieces_steps(pieces)
    npc = PW // D
    np_ = len(pieces)
    na = len(sums)
    ni = s // tm

    def body(*refs):
        d_refs = refs[:np_]
        w_hbm, x_ref, dx2_ref, gn_ref, sc_ref = refs[np_:np_ + 5]
        q_refs = refs[np_ + 5:np_ + 5 + na]
        gx_ref, vec_ref = refs[np_ + 5 + na:np_ + 7 + na]
        r_refs = refs[np_ + 7 + na:np_ + 7 + 2 * na]
        acc, w_s, wsem, ss, rs = refs[np_ + 7 + 2 * na:]
        i, k = pl.program_id(0), pl.program_id(1)
        wpc = k // npc
        wcols = pl.ds(pl.multiple_of((k % npc) * D, D), D)
        _load_w_in(w_hbm, w_s, wsem, (i == 0) & (k == 0), wpc, (i == 0) & (k % npc == 0))

        def scatter_copies():
            cx, cy, cc = _coords()
            j = 2 * cx + cy
            cps = []
            for t, (q, r) in enumerate(zip(q_refs, r_refs)):
                for e, (kx, ky) in enumerate(((1, 0), (0, 1), (1, 1))):
                    cps.append(_rcopy(q.at[j ^ (2 * kx + ky)], r.at[e], ss.at[3 * t + e], rs.at[3 * t + e],
                                      (_flip(cx, kx), _flip(cy, ky), cc)))
            return cps

        @pl.when(k == 0)
        def _():
            acc[...] = jnp.zeros_like(acc)

        @pl.when((i == 0) & (k == 0))
        def _():
            vec_ref[...] = jnp.zeros_like(vec_ref)
            for cp in scatter_copies():
                cp.start()

        @pl.when((i == ni - 1) & (k == nk - 1))
        def _():
            for cp in scatter_copies():
                cp.wait()

        for (s0, n), d_ref in zip(steps, d_refs):
            @pl.when((k >= s0) & (k < s0 + n))
            def _(d_ref=d_ref):
                acc[...] += _dot_nt(d_ref[...], w_s[wpc, :, wcols])

        @pl.when(k == nk - 1)
        def _():
            dh = acc[...]
            xt = x_ref[...]
            rstd = lax.rsqrt(jnp.mean(xt * xt, axis=-1, keepdims=True) + EPS)
            xh = xt * rstd
            gn_v = gn_ref[...]
            sc1 = 1.0 + sc_ref[...]
            dhx = dh * xh
            vec_ref[0:1, :] += _colsum(dh)
            vec_ref[1:2, :] += _colsum(dhx) * gn_v
            vec_ref[2:3, :] += _colsum(dhx) * sc1
            dxh = dh * (gn_v * sc1)
            gx_ref[...] = rstd * (dxh - xh * jnp.mean(dxh * xh, axis=-1, keepdims=True)) + dx2_ref[...]

    def piece_spec(s0, n):
        return pl.BlockSpec((tm, D), lambda i, k: (i, jnp.clip(k - s0, 0, n - 1)))

    rowt = lambda i, k: (i, 0)
    row = pl.BlockSpec((1, D), lambda i, k: (0, 0))
    any_ = pl.BlockSpec(memory_space=pl.ANY)
    outs = pl.pallas_call(
        body, name="inproj_bwd_x", grid=(ni, nk),
        in_specs=[piece_spec(s0, n) for s0, n in steps] +
                 [any_, pl.BlockSpec((tm, D), rowt), pl.BlockSpec((tm, D), rowt), row, row] + [any_] * na,
        out_specs=(pl.BlockSpec((tm, D), rowt), pl.BlockSpec((8, D), lambda i, k: (0, 0))) + (any_,) * na,
        out_shape=(jax.ShapeDtypeStruct((s, D), F32), jax.ShapeDtypeStruct((8, D), F32)) +
                  tuple(jax.ShapeDtypeStruct((3,) + q.shape[1:], q.dtype) for q in sums),
        scratch_shapes=[pltpu.VMEM((tm, D), F32), pltpu.VMEM((4, D, PW), BF16), pltpu.SemaphoreType.DMA((4,)),
                        pltpu.SemaphoreType.DMA((3 * na,)), pltpu.SemaphoreType.DMA((3 * na,))],
        compiler_params=_params(("arbitrary", "arbitrary")),
    )(*pieces, wg, x, dx2, gn, scale, *sums)
    return outs[0], outs[1], outs[2:]


def _inproj_bwd_w(pieces, hbf, tk=512):
    s = hbf.shape[0]
    steps, nk = _pieces_steps(pieces)
    npc = PW // D
    ns = s // tk
    np_ = len(pieces)

    def body(*refs):
        d_refs = refs[:np_]
        h_hbm, g_ref, h_s, hsem = refs[np_:]
        cb, k = pl.program_id(0), pl.program_id(1)
        rows = pl.ds(pl.multiple_of(k * tk, tk), tk)

        @pl.when((cb == 0) & (k == 0))
        def _():
            for kk in range(ns):
                blk = pl.ds(kk * tk, tk)
                pltpu.make_async_copy(h_hbm.at[blk, :], h_s.at[blk, :], hsem.at[kk]).start()

        @pl.when(cb == 0)
        def _():
            pltpu.make_async_copy(h_hbm.at[rows, :], h_s.at[rows, :], hsem.at[k]).wait()

        @pl.when(k == 0)
        def _():
            g_ref[...] = jnp.zeros_like(g_ref)

        for (s0, n), d_ref in zip(steps, d_refs):
            @pl.when((cb >= s0) & (cb < s0 + n))
            def _(d_ref=d_ref):
                g_ref[0] += _dot_tn(h_s[rows, :], d_ref[...])

    def piece_spec(s0, n):
        def imap(cb, k):
            active = (cb >= s0) & (cb < s0 + n)
            return (jnp.where(active, k, 0), jnp.clip(cb - s0, 0, n - 1))
        return pl.BlockSpec((tk, D), imap)

    return pl.pallas_call(
        body, name="inproj_bwd_w", grid=(nk, ns),
        in_specs=[piece_spec(s0, n) for s0, n in steps] + [pl.BlockSpec(memory_space=pl.ANY)],
        out_specs=pl.BlockSpec((1, D, D), lambda cb, k: (cb // npc, 0, cb % npc)),
        out_shape=jax.ShapeDtypeStruct((4, D, PW), F32),
        scratch_shapes=[pltpu.VMEM((s, D), BF16), pltpu.SemaphoreType.DMA((ns,))],
        compiler_params=_params(("arbitrary", "arbitrary")),
    )(*pieces, hbf)


D2D_CHUNK_BYTES = 512 * 1024


def _chunk_rows(a):
    return max(8, D2D_CHUNK_BYTES // (a.shape[-1] * a.dtype.itemsize))


def _pair_exchange(arrs):
    na = len(arrs)
    chunks = []
    for t, a in enumerate(arrs):
        hr = a.shape[1] // 2
        cr = _chunk_rows(a)
        chunks += [(t, j, r0, cr) for j in range(a.shape[0]) for r0 in range(0, hr, cr)]
    nch = len(chunks)

    def body(*refs):
        a_refs = refs[:na]
        rb_refs = refs[na:2 * na]
        ss, rs = refs[2 * na:]
        x, y, c = _coords()
        sib = (x, y, 1 - c)
        rcs = []
        for n, (t, j, r0, cr) in enumerate(chunks):
            hr = a_refs[t].shape[1] // 2
            rc = _rcopy(a_refs[t].at[j, pl.ds((1 - c) * hr + r0, cr), :], rb_refs[t].at[j, pl.ds(r0, cr), :],
                        ss.at[n], rs.at[n], sib)
            rc.start()
            rcs.append(rc)
        for rc in rcs:
            rc.wait_recv()
        for rc in rcs:
            rc.wait_send()

    any_ = pl.BlockSpec(memory_space=pl.ANY)
    halves = [jax.ShapeDtypeStruct((a.shape[0], a.shape[1] // 2, a.shape[2]), a.dtype) for a in arrs]
    return pl.pallas_call(
        body, name="pair_exchange",
        out_shape=tuple(halves),
        in_specs=[any_] * na, out_specs=tuple([any_] * na),
        scratch_shapes=[pltpu.SemaphoreType.DMA((nch,)), pltpu.SemaphoreType.DMA((nch,))],
        compiler_params=_params(),
    )(*arrs)


def _pair_swap(arrs):
    na = len(arrs)
    chunks = []
    for t, a in enumerate(arrs):
        cr = _chunk_rows(a)
        chunks += [(t, r0, cr) for r0 in range(0, a.shape[0], cr)]
    nch = len(chunks)

    def body(*refs):
        a_refs = refs[:na]
        o_refs = refs[na:2 * na]
        ss, rs = refs[2 * na:]
        x, y, c = _coords()
        sib = (x, y, 1 - c)
        rcs = []
        for n, (t, r0, cr) in enumerate(chunks):
            rows = pl.ds(r0, cr)
            rc = _rcopy(a_refs[t].at[rows, :], o_refs[t].at[rows, :], ss.at[n], rs.at[n], sib)
            rc.start()
            rcs.append(rc)
        for rc in rcs:
            rc.wait_recv()
        for rc in rcs:
            rc.wait_send()

    any_ = pl.BlockSpec(memory_space=pl.ANY)
    return pl.pallas_call(
        body, name="pair_swap",
        out_shape=tuple(jax.ShapeDtypeStruct(a.shape, a.dtype) for a in arrs),
        in_specs=[any_] * na, out_specs=tuple([any_] * na),
        scratch_shapes=[pltpu.SemaphoreType.DMA((nch,)), pltpu.SemaphoreType.DMA((nch,))],
        compiler_params=_params(),
    )(*arrs)


def _add_half(full, rb, core, tr):
    n, r, cdim = full.shape
    nb = r // 2 // tr

    def body(c_ref, a_ref, b_ref, o_ref):
        o_ref[...] = a_ref[...] + b_ref[...]

    mine = pl.BlockSpec((1, tr, cdim), lambda i, j, c_ref: (i, c_ref[0] * nb + j, 0))
    spec = pl.BlockSpec((1, tr, cdim), lambda i, j, c_ref: (i, j, 0))
    return pl.pallas_call(
        body, name="add_half",
        grid_spec=pltpu.PrefetchScalarGridSpec(num_scalar_prefetch=1, grid=(n, nb), in_specs=[mine, spec],
                                               out_specs=spec),
        out_shape=jax.ShapeDtypeStruct(rb.shape, rb.dtype),
        compiler_params=_params(("parallel", "parallel")),
    )(core, full, rb)


def _sum_slots(q, r3, shard, tr):
    _, r, cdim = q.shape

    def body(j_ref, q_ref, r_ref, o_ref):
        o_ref[...] = ((q_ref[0] + r_ref[0]) + r_ref[1]) + r_ref[2]

    return pl.pallas_call(
        body, name="sum_slots",
        grid_spec=pltpu.PrefetchScalarGridSpec(
            num_scalar_prefetch=1, grid=(r // tr,),
            in_specs=[pl.BlockSpec((1, tr, cdim), lambda i, j_ref: (j_ref[0], i, 0)),
                      pl.BlockSpec((3, tr, cdim), lambda i, j_ref: (0, i, 0))],
            out_specs=pl.BlockSpec((tr, cdim), lambda i, j_ref: (i, 0))),
        out_shape=jax.ShapeDtypeStruct((r, cdim), q.dtype),
        compiler_params=_params(("parallel",)),
    )(shard, q, r3)


def _allreduce_small(pack):
    def body(p_ref, out_ref, rbuf, s1, r1, s2, r2):
        me = _my_index()
        chunk = lambda d: pl.ds(pl.multiple_of(d * AR_CHUNK, 8), AR_CHUNK)
        sends = []
        for k in range(1, NDEV):
            cp = _rcopy(p_ref.at[chunk(me ^ k)], rbuf.at[me], s1.at[k - 1], r1.at[k - 1], _peer(k))
            cp.start()
            sends.append(cp)
        rbuf[me] = p_ref[chunk(me), :]
        for k in range(1, NDEV):
            _rcopy(p_ref.at[chunk(me)], rbuf.at[me ^ k], s1.at[k - 1], r1.at[k - 1], _peer(k)).wait_recv()
        tot = rbuf[0]
        for d in range(1, NDEV):
            tot = tot + rbuf[d]
        out_ref[chunk(me), :] = tot
        for k in range(1, NDEV):
            cp = _rcopy(out_ref.at[chunk(me)], out_ref.at[chunk(me)], s2.at[k - 1], r2.at[k - 1], _peer(k))
            cp.start()
            sends.append(cp)
        for k in range(1, NDEV):
            _rcopy(out_ref.at[chunk(me)], out_ref.at[chunk(me ^ k)], s2.at[k - 1], r2.at[k - 1], _peer(k)).wait_recv()
        for cp in sends:
            cp.wait_send()

    vm = pl.BlockSpec(memory_space=pltpu.VMEM)
    return pl.pallas_call(
        body, name="allreduce_small",
        out_shape=jax.ShapeDtypeStruct((AR_ROWS, D), F32),
        in_specs=[vm], out_specs=vm,
        scratch_shapes=[pltpu.VMEM((NDEV, AR_CHUNK, D), F32),
                        pltpu.SemaphoreType.DMA((7,)), pltpu.SemaphoreType.DMA((7,)),
                        pltpu.SemaphoreType.DMA((7,)), pltpu.SemaphoreType.DMA((7,))],
        compiler_params=_params(),
    )(pack)


def _adamw(w, g, m, v, tr):
    r, cdim = w.shape

    def body(w_ref, g_ref, m_ref, v_ref, d_ref, nm_ref, nv_ref):
        gv = g_ref[...]
        nm = B1 * m_ref[...] + (1.0 - B1) * gv
        nv = B2 * v_ref[...] + (1.0 - B2) * (gv * gv)
        m_hat = nm / (1.0 - B1 ** STEP)
        v_hat = nv / (1.0 - B2 ** STEP)
        d_ref[...] = -LR * (m_hat / (jnp.sqrt(v_hat) + ADAM_EPS) + WD * w_ref[...])
        nm_ref[...] = nm
        nv_ref[...] = nv

    spec = pl.BlockSpec((tr, cdim), lambda i: (i, 0))
    sd = jax.ShapeDtypeStruct((r, cdim), F32)
    return pl.pallas_call(
        body, name="adamw", grid=(r // tr,), in_specs=[spec] * 4, out_specs=(spec,) * 3, out_shape=(sd,) * 3,
        compiler_params=_params(("parallel",)),
    )(w, g, m, v)


def _rope_tables(positions):
    inv_freq = ROPE_THETA ** (-jnp.arange(0, ROT, 2, dtype=F32) / ROT)
    ang = positions.astype(F32)[:, None] * inv_freq
    cos, sin = jnp.cos(ang), jnp.sin(ang)
    n = positions.shape[0]
    half = ROT // 2
    rc = jnp.concatenate([cos, cos, jnp.ones((n, DH - ROT), F32)], axis=1)
    rsa = jnp.concatenate([-sin, jnp.zeros((n, DH - half), F32)], axis=1)
    rsb = jnp.concatenate([jnp.zeros((n, half), F32), sin, jnp.zeros((n, DH - ROT), F32)], axis=1)
    return rc, rsa, rsb


def kernel(x, c, positions, g_norm, w_mod, b_mod, w_in, b_gate, conv_w, conv_b, w_a, b_a, w_x, b_x, lam, w_out_rnn, w_out_attn, w_o, g_final, loss_target, m_g_norm, m_w_mod, m_b_mod, m_w_in, m_b_gate, m_conv_w, m_conv_b, m_w_a, m_b_a, m_w_x, m_b_x, m_lam, m_w_out_rnn, m_w_out_attn, m_w_o, m_g_final, v_g_norm, v_w_mod, v_b_mod, v_w_in, v_b_gate, v_conv_w, v_conv_b, v_w_a, v_b_a, v_w_x, v_b_x, v_lam, v_w_out_rnn, v_w_out_attn, v_w_o, v_g_final):
    s = x.shape[1]
    xi = lax.axis_index("x")
    yi = lax.axis_index("y")
    ci = lax.axis_index("c")
    shard = 2 * xi + yi
    x2d = x[0]
    tgt = loss_target[0]
    pos = positions[0]

    c_all, mod4 = _mod_fwd(c, w_mod[0], b_mod.reshape(4, 1, 768))
    mod = mod4.reshape(1, 3 * D)
    shift, scale, gate = mod[:, :D], mod[:, D:2 * D], mod[:, 2 * D:]
    w3_sh = jnp.stack([w_out_rnn[0], w_out_attn[0], w_o[0]]).astype(BF16)
    wsh = jnp.concatenate([w_in[0].astype(BF16), w3_sh.reshape(384, PW)], axis=0)
    wg = lax.dynamic_update_slice(_gather_weights(wsh), wsh[None], (shard, 0, 0))
    w3 = wg[:, D:, :].reshape(4, 3, 256, D).transpose(1, 0, 2, 3).reshape(3, D, D)
    conv_all = _gather_small(conv_w[0])
    conv_full = conv_all[0::2].transpose(1, 0, 2).reshape(4, D)

    p, hbf = _norm_inproj(x2d, g_norm, shift, scale, wg)
    rc, rsa, rsb = _rope_tables(pos)
    pos_col = pos.reshape(s, 1)
    b_a3, b_x3 = b_a.reshape(H, 1, DH), b_x.reshape(H, 1, DH)
    hr, gr = _rnn_fwd(p, pos_col, conv_full, conv_b, w_a[0], b_a3, w_x[0], b_x3, lam)
    o, lse, ga = _attn_fwd(p, rc, rsa, rsb)

    dgr, dga, dc, dx2, vec_t, g_out = _tail(gr, ga, p, x2d, tgt, w3, b_gate, gate, g_final.reshape(1, D))

    dxr, dzr, g_wa, g_ba, g_wx, g_bx, g_lam, g_cw, g_cb = _rnn_bwd(
        p, hr, dgr, pos_col, conv_full, conv_b, w_a[0], b_a3, w_x[0], b_x3, lam)
    dq, dk, dv, dza = _attn_bwd(p, o, lse, dga, rc, rsa, rsb)

    pieces = [dxr, dzr, dq, dk, dv, dza, dc]
    g_win = _inproj_bwd_w(pieces, hbf)

    core = ci.reshape(1)
    shard1 = shard.reshape(1)
    rb_a, rb_b = _pair_exchange([g_win, g_out])
    q_a, q_b = _add_half(g_win, rb_a, core, tr=256), _add_half(g_out, rb_b, core, tr=128)
    grad_x, vec_n, (r_a, r_b) = _inproj_bwd_x(pieces, wg, x2d, dx2, g_norm, scale, [q_a, q_b])
    f_a, f_b = _sum_slots(q_a, r_a, shard1, tr=256), _sum_slots(q_b, r_b, shard1, tr=128)
    s_a, s_b = _pair_swap([f_a, f_b])
    south = ci == 0
    grad_w_in = jnp.where(south, jnp.concatenate([f_a, s_a], axis=0), jnp.concatenate([s_a, f_a], axis=0))
    g3 = jnp.where(south, jnp.concatenate([f_b, s_b], axis=0), jnp.concatenate([s_b, f_b], axis=0)).reshape(3, 256, D)

    dmod_row = jnp.concatenate([vec_n[0:1], vec_n[1:2], vec_t[1:2]], axis=1)
    pack = jnp.concatenate([
        vec_n[2:3],
        dmod_row.reshape(3, D),
        vec_t[2:4],
        g_cb.reshape(1, D),
        g_wa.reshape(128, D),
        g_ba.reshape(1, D),
        g_wx.reshape(128, D),
        g_bx.reshape(1, D),
        g_lam.reshape(1, D),
        vec_t[0:1],
        g_cw.transpose(1, 0, 2).reshape(4, D),
        vec_t[4:5],
        jnp.zeros((AR_ROWS - 272, D), F32)], axis=0)
    red = _allreduce_small(pack)
    loss = red[271, 0]
    grad_w_mod = _mod_bwd(dmod_row.reshape(4, 1, 768), c_all)
    g_conv_sh = lax.dynamic_slice_in_dim(red[267:271], shard * 256, 256, axis=1)

    def small_pack(g_norm_, b_mod_, b_gate_, conv_b_, w_a_, b_a_, w_x_, b_x_, lam_, g_final_, conv_w_):
        return jnp.concatenate([
            g_norm_.reshape(1, D), b_mod_.reshape(3, D), b_gate_.reshape(2, D), conv_b_.reshape(1, D),
            w_a_.reshape(128, D), b_a_.reshape(1, D), w_x_.reshape(128, D), b_x_.reshape(1, D),
            lam_.reshape(1, D), g_final_.reshape(1, D), conv_w_.reshape(1, D),
            jnp.zeros((4, D), F32)], axis=0)

    wp = small_pack(g_norm, b_mod, b_gate, conv_b, w_a, b_a, w_x, b_x, lam, g_final, conv_w)
    mp = small_pack(m_g_norm, m_b_mod, m_b_gate, m_conv_b, m_w_a, m_b_a, m_w_x, m_b_x, m_lam, m_g_final, m_conv_w)
    vp = small_pack(v_g_norm, v_b_mod, v_b_gate, v_conv_b, v_w_a, v_b_a, v_w_x, v_b_x, v_lam, v_g_final, v_conv_w)
    gp = jnp.concatenate([red[0:267], g_conv_sh.reshape(1, D), jnp.zeros((4, D), F32)], axis=0)
    small = _adamw(wp, gp, mp, vp, tr=136)

    def unpack(a):
        return dict(
            g_norm=a[0:1], b_mod=a[1:4].reshape(1, 3 * D), b_gate=a[4:6].reshape(1, 2 * D), conv_b=a[6:7],
            w_a=a[7:135].reshape(1, H, DH, DH), b_a=a[135:136].reshape(1, H, DH),
            w_x=a[136:264].reshape(1, H, DH, DH), b_x=a[264:265].reshape(1, H, DH), lam=a[265:266],
            g_final=a[266].reshape(D), conv_w=a[267:268].reshape(1, 4, 256))

    big_in = _adamw(w_in[0], grad_w_in, m_w_in[0], v_w_in[0], tr=256)
    big_mod = _adamw(w_mod[0], grad_w_mod, m_w_mod[0], v_w_mod[0], tr=256)
    w3f = jnp.concatenate([w_out_rnn[0], w_out_attn[0], w_o[0]], axis=0)
    m3f = jnp.concatenate([m_w_out_rnn[0], m_w_out_attn[0], m_w_o[0]], axis=0)
    v3f = jnp.concatenate([v_w_out_rnn[0], v_w_out_attn[0], v_w_o[0]], axis=0)
    big_out = _adamw(w3f, g3.reshape(768, D), m3f, v3f, tr=256)

    names = ["g_norm", "w_mod", "b_mod", "w_in", "b_gate", "conv_w", "conv_b", "w_a", "b_a", "w_x", "b_x", "lam",
             "w_out_rnn", "w_out_attn", "w_o", "g_final"]
    grads = unpack(gp)
    grads.update(w_mod=grad_w_mod[None], w_in=grad_w_in[None],
                 w_out_rnn=g3[0][None], w_out_attn=g3[1][None], w_o=g3[2][None])
    outs = [grads]
    for idx in range(3):
        d = unpack(small[idx])
        d.update(w_mod=big_mod[idx][None], w_in=big_in[idx][None],
                 w_out_rnn=big_out[idx][0:256][None], w_out_attn=big_out[idx][256:512][None],
                 w_o=big_out[idx][512:768][None])
        outs.append(d)
    flat = [d[n] for d in outs for n in names]
    return (loss, grad_x[None], *flat)
```

```python
import jax
import jax.numpy as jnp
from jax import lax
from jax.experimental import pallas as pl
from jax.experimental.pallas import tpu as pltpu

F32, BF16 = jnp.float32, jnp.bfloat16
MESH = pl.DeviceIdType.MESH
HIGHEST = lax.Precision.HIGHEST

D = 1024
H = 8
DH = 128
PW = 2048
EPS = 1e-6
LRU_C = 8.0
SCALE = DH ** -0.5
NEG = -1e30
SPAN = 2048
UB = 128
DILATIONS = (1, 4, 16)
UNIT_BATCH = 8
UNIT_UNROLL = 8
ROPE_THETA = 500000.0
ROT = 32

LR, B1, B2, ADAM_EPS, WD, STEP = 0.001, 0.9, 0.999, 1e-08, 0.01, 10

WROWS = 1024 + 384
NDEV = 8
AR_ROWS = 320
AR_CHUNK = AR_ROWS // NDEV


def _params(sem=None, vmem_mb=56):
    return pltpu.CompilerParams(dimension_semantics=sem, vmem_limit_bytes=vmem_mb * 2 ** 20)


def _coords():
    return lax.axis_index("x"), lax.axis_index("y"), lax.axis_index("c")


def _flip(v, bit):
    return 1 - v if bit else v


def _peer(k):
    x, y, c = _coords()
    return (_flip(x, (k >> 2) & 1), _flip(y, (k >> 1) & 1), _flip(c, k & 1))


def _my_index():
    x, y, c = _coords()
    return 4 * x + 2 * y + c


def _rcopy(src, dst, ssem, rsem, dev):
    return pltpu.make_async_remote_copy(src_ref=src, dst_ref=dst, send_sem=ssem, recv_sem=rsem,
                                        device_id=dev, device_id_type=MESH)


def _sigmoid(x):
    return jax.nn.sigmoid(x)


def _dot(a, b):
    return jnp.dot(a, b, preferred_element_type=F32)


def _dot_nt(a, b):
    return lax.dot_general(a, b, (((1,), (1,)), ((), ())), preferred_element_type=F32)


def _dot_tn(a, b):
    return lax.dot_general(a, b, (((0,), (0,)), ((), ())), preferred_element_type=F32)


def _colsum(a):
    return jnp.sum(a, axis=0, keepdims=True)


def _mod_fwd(c, w_mod_sh, b_mod4):
    def body(c_ref, w_ref, b_ref, call_ref, mod_ref, rows_ref, cmat_ref, s1, r1, s2, r2):
        x, y, _ = _coords()
        me = _my_index()
        j = 2 * x + y
        call_ref[me] = c_ref[...]
        sends = []
        for k in range(1, NDEV):
            cp = _rcopy(call_ref.at[me], call_ref.at[me], s1.at[k - 1], r1.at[k - 1], _peer(k))
            cp.start()
            sends.append(cp)
        for k in range(1, NDEV):
            pk = me ^ k
            _rcopy(call_ref.at[pk], call_ref.at[pk], s1.at[k - 1], r1.at[k - 1], _peer(k)).wait_recv()
        for b in range(NDEV):
            cmat_ref[pl.ds(b, 1), :] = call_ref[b]
        cm = cmat_ref[...]
        act = cm * _sigmoid(cm)
        mp = jnp.dot(act, w_ref[...], preferred_element_type=F32, precision=HIGHEST) + b_ref[j]
        for b in range(NDEV):
            rows_ref[b] = mp[b:b + 1]
        mod_ref[j] = rows_ref[me]
        for q, k in enumerate((2, 4, 6)):
            cp = _rcopy(rows_ref.at[me ^ k], mod_ref.at[j], s2.at[q], r2.at[q], _peer(k))
            cp.start()
            sends.append(cp)
        for q, k in enumerate((2, 4, 6)):
            jq = j ^ (k >> 1)
            _rcopy(rows_ref.at[me], mod_ref.at[jq], s2.at[q], r2.at[q], _peer(k)).wait_recv()
        for cp in sends:
            cp.wait_send()

    vm = pl.BlockSpec(memory_space=pltpu.VMEM)
    return pl.pallas_call(
        body, name="mod_fwd",
        out_shape=(jax.ShapeDtypeStruct((NDEV, 1, D), F32), jax.ShapeDtypeStruct((4, 1, 768), F32)),
        in_specs=[vm, vm, vm], out_specs=(vm, vm),
        scratch_shapes=[pltpu.VMEM((NDEV, 1, 768), F32), pltpu.VMEM((NDEV, D), F32),
                        pltpu.SemaphoreType.DMA((7,)), pltpu.SemaphoreType.DMA((7,)),
                        pltpu.SemaphoreType.DMA((3,)), pltpu.SemaphoreType.DMA((3,))],
        compiler_params=_params(),
    )(c, w_mod_sh, b_mod4)


def _mod_bwd(dmod4, c_all):
    def body(d_ref, call_ref, gw_ref, dall_ref, cmat_ref, dmat_ref, s1, r1):
        x, y, _ = _coords()
        me = _my_index()
        j = 2 * x + y
        dall_ref[me] = d_ref[...]
        sends = []
        for k in range(1, NDEV):
            cp = _rcopy(dall_ref.at[me], dall_ref.at[me], s1.at[k - 1], r1.at[k - 1], _peer(k))
            cp.start()
            sends.append(cp)
        for k in range(1, NDEV):
            pk = me ^ k
            _rcopy(dall_ref.at[pk], dall_ref.at[pk], s1.at[k - 1], r1.at[k - 1], _peer(k)).wait_recv()
        for cp in sends:
            cp.wait_send()
        for b in range(NDEV):
            cmat_ref[pl.ds(b, 1), :] = call_ref[b]
            dmat_ref[pl.ds(b, 1), :] = dall_ref[b, j]
        cm = cmat_ref[...]
        act = cm * _sigmoid(cm)
        gw_ref[...] = lax.dot_general(act, dmat_ref[...], (((0,), (0,)), ((), ())),
                                      preferred_element_type=F32, precision=HIGHEST)

    vm = pl.BlockSpec(memory_space=pltpu.VMEM)
    return pl.pallas_call(
        body, name="mod_bwd",
        out_shape=jax.ShapeDtypeStruct((D, 768), F32),
        in_specs=[vm, vm], out_specs=vm,
        scratch_shapes=[pltpu.VMEM((NDEV, 4, 1, 768), F32), pltpu.VMEM((NDEV, D), F32), pltpu.VMEM((NDEV, 768), F32),
                        pltpu.SemaphoreType.DMA((7,)), pltpu.SemaphoreType.DMA((7,))],
        compiler_params=_params(),
    )(dmod4, c_all)


def _gather_weights(wsh):
    rows, cols = wsh.shape
    half = rows // 2
    nch = 4
    cr = half // nch

    def body(w_ref, g_ref, ss, rs):
        x, y, c = _coords()
        j = 2 * x + y
        sib = (x, y, 1 - c)
        sends = []
        chips = ((1, 0), (0, 1), (1, 1))
        mine = lambda n: pl.ds(c * half + n * cr, cr)
        theirs = lambda n: pl.ds((1 - c) * half + n * cr, cr)
        for n in range(nch):
            for q, (kx, ky) in enumerate(chips):
                e = nch * q + n
                cp = _rcopy(w_ref.at[mine(n)], g_ref.at[j, mine(n)], ss.at[e], rs.at[e],
                            (_flip(x, kx), _flip(y, ky), c))
                cp.start()
                sends.append(cp)
        for n in range(nch):
            for q, (kx, ky) in enumerate(chips):
                jq = j ^ (2 * kx + ky)
                e = nch * q + n
                _rcopy(w_ref.at[mine(n)], g_ref.at[jq, mine(n)], ss.at[e], rs.at[e], sib).wait_recv()
                cp = _rcopy(g_ref.at[jq, mine(n)], g_ref.at[jq, mine(n)], ss.at[3 * nch + e], rs.at[3 * nch + e], sib)
                cp.start()
                sends.append(cp)
        for n in range(nch):
            for q, (kx, ky) in enumerate(chips):
                jq = j ^ (2 * kx + ky)
                e = nch * q + n
                _rcopy(w_ref.at[mine(n)], g_ref.at[jq, theirs(n)], ss.at[3 * nch + e], rs.at[3 * nch + e], sib).wait_recv()
        for cp in sends:
            cp.wait_send()

    any_ = pl.BlockSpec(memory_space=pl.ANY)
    return pl.pallas_call(
        body, name="gather_weights",
        out_shape=jax.ShapeDtypeStruct((4, rows, cols), wsh.dtype),
        in_specs=[any_], out_specs=any_,
        scratch_shapes=[pltpu.SemaphoreType.DMA((6 * nch,)), pltpu.SemaphoreType.DMA((6 * nch,))],
        compiler_params=_params(),
    )(wsh)


def _gather_small(v):
    r, cdim = v.shape

    def body(v_ref, out_ref, ss, rs):
        me = _my_index()
        out_ref[me] = v_ref[...]
        sends = []
        for k in range(1, NDEV):
            cp = _rcopy(out_ref.at[me], out_ref.at[me], ss.at[k - 1], rs.at[k - 1], _peer(k))
            cp.start()
            sends.append(cp)
        for k in range(1, NDEV):
            pk = me ^ k
            _rcopy(out_ref.at[pk], out_ref.at[pk], ss.at[k - 1], rs.at[k - 1], _peer(k)).wait_recv()
        for cp in sends:
            cp.wait_send()

    vm = pl.BlockSpec(memory_space=pltpu.VMEM)
    return pl.pallas_call(
        body, name="gather_small",
        out_shape=jax.ShapeDtypeStruct((NDEV, r, cdim), v.dtype),
        in_specs=[vm], out_specs=vm,
        scratch_shapes=[pltpu.SemaphoreType.DMA((7,)), pltpu.SemaphoreType.DMA((7,))],
        compiler_params=_params(),
    )(v)


def _load_w_in(w_hbm, w_s, sem, first, piece, piece_start):
    def copy(pc):
        return pltpu.make_async_copy(w_hbm.at[pc, pl.ds(0, D), :], w_s.at[pc], sem.at[pc])

    @pl.when(first)
    def _():
        for pc in range(4):
            copy(pc).start()

    @pl.when(piece_start)
    def _():
        copy(piece).wait()


def _norm_inproj(x, gn, shift, scale, wg, tm=1024, tn=512):
    s = x.shape[0]
    npc = PW // tn

    def body(x_ref, gn_ref, sh_ref, sc_ref, w_hbm, p_ref, h_ref, hs, w_s, sem):
        i, pc, col = pl.program_id(0), pl.program_id(1), pl.program_id(2)
        _load_w_in(w_hbm, w_s, sem, (i == 0) & (pc == 0) & (col == 0), pc, (i == 0) & (col == 0))

        @pl.when((pc == 0) & (col == 0))
        def _():
            xt = x_ref[...]
            rstd = lax.rsqrt(jnp.mean(xt * xt, axis=-1, keepdims=True) + EPS)
            h = (xt * rstd * gn_ref[...]) * (1.0 + sc_ref[...]) + sh_ref[...]
            hs[...] = h.astype(BF16)
            h_ref[...] = hs[...]

        p_ref[0] = _dot(hs[...], w_s[pc, :, pl.ds(pl.multiple_of(col * tn, tn), tn)]).astype(BF16)

    row = pl.BlockSpec((1, D), lambda i, pc, col: (0, 0))
    return pl.pallas_call(
        body, name="norm_inproj", grid=(s // tm, 4, npc),
        in_specs=[pl.BlockSpec((tm, D), lambda i, pc, col: (i, 0)), row, row, row,
                  pl.BlockSpec(memory_space=pl.ANY)],
        out_specs=(pl.BlockSpec((1, tm, tn), lambda i, pc, col: (pc, i, col)),
                   pl.BlockSpec((tm, D), lambda i, pc, col: (i, 0))),
        out_shape=(jax.ShapeDtypeStruct((4, s, PW), BF16), jax.ShapeDtypeStruct((s, D), BF16)),
        scratch_shapes=[pltpu.VMEM((tm, D), BF16), pltpu.VMEM((4, D, PW), BF16), pltpu.SemaphoreType.DMA((4,))],
        compiler_params=_params(("arbitrary", "arbitrary", "arbitrary")),
    )(x, gn, shift, scale, wg)


def _shift_down(prev8, cur, d):
    t = cur.shape[0]
    c3 = cur.reshape(t // 8, 8, DH)
    rot = pltpu.roll(c3, d, 1)
    before = jnp.concatenate([pltpu.roll(prev8, d, 0).reshape(1, 8, DH), rot[:-1]], axis=0)
    rows = lax.broadcasted_iota(jnp.int32, c3.shape, 1)
    return jnp.where(rows >= d, rot, before).reshape(t, DH)


def _shift_up(cur, next8, d):
    t = cur.shape[0]
    c3 = cur.reshape(t // 8, 8, DH)
    rot = pltpu.roll(c3, 8 - d, 1)
    after = jnp.concatenate([rot[1:], pltpu.roll(next8, 8 - d, 0).reshape(1, 8, DH)], axis=0)
    rows = lax.broadcasted_iota(jnp.int32, c3.shape, 1)
    return jnp.where(rows < 8 - d, rot, after).reshape(t, DH)


def _rnn_gates(xr, prev8, cw, cb, wa, ba, wx, bx, lam, reset):
    xc = cw[3:4] * xr + cb
    for d in (1, 2, 3):
        xc = xc + cw[3 - d:4 - d] * _shift_down(prev8, xr, d)
    xcb = xc.astype(BF16)
    r = _sigmoid(_dot(xcb, wa.astype(BF16)) + ba)
    ig = _sigmoid(_dot(xcb, wx.astype(BF16)) + bx)
    nl = -lam
    sp = jnp.maximum(nl, 0.0) + jnp.log1p(jnp.exp(-jnp.abs(nl)))
    log_a = (-LRU_C * r) * sp
    a = jnp.where(reset, 0.0, jnp.exp(log_a))
    mult = jnp.where(reset, 1.0, jnp.sqrt(1.0 - jnp.exp(2.0 * log_a)))
    return xc, r, ig, sp, a, mult


def _log_scan(a, b, axis, up):
    n = a.shape[axis]
    rows = lax.broadcasted_iota(jnp.int32, a.shape, axis)
    d = 1
    while d < n:
        m = rows < n - d if up else rows >= d
        shift = n - d if up else d
        a_s = pltpu.roll(a, shift, axis)
        b_s = pltpu.roll(b, shift, axis)
        b = jnp.where(m, a * b_s + b, b)
        a = jnp.where(m, a * a_s, a)
        d *= 2
    return a, b


def _scan(a, b, t, edge, up=False):
    g = t // 8
    a3, b3 = _log_scan(a.reshape(g, 8, DH), b.reshape(g, 8, DH), 1, up)
    last = 0 if up else 7
    ag, bg = _log_scan(a3[:, last, :], b3[:, last, :], 0, up)
    hg = ag * edge + bg
    grp = lax.broadcasted_iota(jnp.int32, hg.shape, 0)
    if up:
        cin = jnp.where(grp == g - 1, edge, pltpu.roll(hg, g - 1, 0))
        tail = hg[0:1]
    else:
        cin = jnp.where(grp == 0, edge, pltpu.roll(hg, 1, 0))
        tail = hg[g - 1:g]
    return (a3 * cin[:, None, :] + b3).reshape(t, DH), tail


def _rnn_fwd(p, pos, conv_w, conv_b, w_a, b_a, w_x, b_x, lam, tt=512):
    s = p.shape[1]
    nt = s // tt

    def body(xr_ref, z_ref, pos_ref, cw_ref, cb_ref, wa_ref, ba_ref, wx_ref, bx_ref, lam_ref,
             hr_ref, gr_ref, xprev, hprev):
        @pl.when(pl.program_id(1) == 0)
        def _():
            xprev[...] = jnp.zeros_like(xprev)
            hprev[...] = jnp.zeros_like(hprev)

        xr = xr_ref[0].astype(F32)
        z = z_ref[0].astype(F32)
        reset = pos_ref[...] == 0
        xc, r, ig, sp, a, mult = _rnn_gates(xr, xprev[...], cw_ref[...], cb_ref[...], wa_ref[0], ba_ref[0],
                                            wx_ref[0], bx_ref[0], lam_ref[...], reset)
        bx = mult * ig * xc
        h, h_last = _scan(a, bx, tt, hprev[0:1])
        xprev[...] = xr[tt - 8:]
        hprev[...] = jnp.broadcast_to(h_last, (8, DH))
        hr_ref[...] = h
        gr_ref[...] = (h * (z * _sigmoid(z))).astype(BF16)

    head_row = lambda hh, t: (0, hh)
    return pl.pallas_call(
        body, name="rnn_fwd", grid=(H, nt),
        in_specs=[pl.BlockSpec((1, tt, DH), lambda hh, t: (0, t, hh)),
                  pl.BlockSpec((1, tt, DH), lambda hh, t: (0, t, H + hh)),
                  pl.BlockSpec((tt, 1), lambda hh, t: (t, 0)),
                  pl.BlockSpec((4, DH), head_row), pl.BlockSpec((1, DH), head_row),
                  pl.BlockSpec((1, DH, DH), lambda hh, t: (hh, 0, 0)), pl.BlockSpec((1, 1, DH), lambda hh, t: (hh, 0, 0)),
                  pl.BlockSpec((1, DH, DH), lambda hh, t: (hh, 0, 0)), pl.BlockSpec((1, 1, DH), lambda hh, t: (hh, 0, 0)),
                  pl.BlockSpec((1, DH), head_row)],
        out_specs=(pl.BlockSpec((tt, DH), lambda hh, t: (t, hh)), pl.BlockSpec((tt, DH), lambda hh, t: (t, hh))),
        out_shape=(jax.ShapeDtypeStruct((s, D), F32), jax.ShapeDtypeStruct((s, D), BF16)),
        scratch_shapes=[pltpu.VMEM((8, DH), F32), pltpu.VMEM((8, DH), F32)],
        compiler_params=_params(("parallel", "arbitrary")),
    )(p, p, pos, conv_w, conv_b, w_a, b_a, w_x, b_x, lam)


def _rnn_bwd(p, hr, dgr, pos, conv_w, conv_b, w_a, b_a, w_x, b_x, lam, tt=512):
    s = p.shape[1]
    nt = s // tt
    t8 = tt // 8

    def body(xr_ref, z_ref, xp_ref, hr_ref, hp_ref, dg_ref, pos_ref, cw_ref, cb_ref, wa_ref, ba_ref, wx_ref, bx_ref,
             lam_ref, dxr_ref, dz_ref, gwa_ref, gba_ref, gwx_ref, gbx_ref, glam_ref, gcw_ref, gcb_ref,
             a_next, g_next, dxc_next):
        t = pl.program_id(1)
        has_prev = t < nt - 1

        @pl.when(t == 0)
        def _():
            a_next[...] = jnp.zeros_like(a_next)
            g_next[...] = jnp.zeros_like(g_next)
            dxc_next[...] = jnp.zeros_like(dxc_next)
            gwa_ref[...] = jnp.zeros_like(gwa_ref)
            gba_ref[...] = jnp.zeros_like(gba_ref)
            gwx_ref[...] = jnp.zeros_like(gwx_ref)
            gbx_ref[...] = jnp.zeros_like(gbx_ref)
            glam_ref[...] = jnp.zeros_like(glam_ref)
            gcw_ref[...] = jnp.zeros_like(gcw_ref)
            gcb_ref[...] = jnp.zeros_like(gcb_ref)

        xr = xr_ref[0].astype(F32)
        z = z_ref[0].astype(F32)
        hr_blk = hr_ref[...]
        dg = dg_ref[...]
        xprev = jnp.where(has_prev, xp_ref[0].astype(F32)[8:], 0.0)
        hprev8 = jnp.where(has_prev, hp_ref[...], 0.0)
        reset = pos_ref[...] == 0
        cw = cw_ref[...]
        wa = wa_ref[0]
        wx = wx_ref[0]
        lam_v = lam_ref[...]
        xc, r, ig, sp, a, mult = _rnn_gates(xr, xprev, cw, cb_ref[...], wa, ba_ref[0], wx, bx_ref[0], lam_v, reset)

        sz = _sigmoid(z)
        dh = dg * (z * sz)
        dz_ref[...] = (dg * hr_blk * (sz * (1.0 + z * (1.0 - sz)))).astype(BF16)

        an = _shift_up(a, a_next[...], 1)
        g, g_first = _scan(an, dh, tt, g_next[0:1], up=True)
        a_next[...] = jnp.broadcast_to(a[0:1], (8, DH))
        g_next[...] = jnp.broadcast_to(g_first, (8, DH))

        hm1 = _shift_down(hprev8, hr_blk, 1)
        da = g * hm1
        dmult = g * (ig * xc)
        di = g * (mult * xc)
        dxc = g * (mult * ig)
        dla = jnp.where(reset, 0.0, da * a - dmult * (a * a) / mult)
        dr = dla * (-LRU_C * sp)
        dsp = _colsum(dla * (-LRU_C * r))
        glam_ref[0] += dsp * (-_sigmoid(-lam_v))
        dpa = dr * r * (1.0 - r)
        dpx = di * ig * (1.0 - ig)
        dpab = dpa.astype(BF16)
        dpxb = dpx.astype(BF16)
        dxc = dxc + _dot_nt(dpab, wa.astype(BF16)) + _dot_nt(dpxb, wx.astype(BF16))
        xcb = xc.astype(BF16)
        gwa_ref[0] += _dot_tn(xcb, dpab)
        gwx_ref[0] += _dot_tn(xcb, dpxb)
        gba_ref[0] += _colsum(dpa)
        gbx_ref[0] += _colsum(dpx)

        dxr = cw[3:4] * dxc
        for d in (1, 2, 3):
            dxr = dxr + cw[3 - d:4 - d] * _shift_up(dxc, dxc_next[...], d)
        dxr_ref[...] = dxr.astype(BF16)
        dxc_next[...] = dxc[0:8]
        gcb_ref[0] += _colsum(dxc)
        gcw_ref[0, 3:4, :] += _colsum(xr * dxc)
        for d in (1, 2, 3):
            gcw_ref[0, 3 - d:4 - d, :] += _colsum(_shift_down(xprev, xr, d) * dxc)

    rt = lambda t: nt - 1 - t
    prev8 = lambda t: jnp.maximum(rt(t) * t8 - 1, 0)
    head_row = lambda hh, t: (0, hh)
    hsm = lambda hh, t: (hh, 0, 0)
    return pl.pallas_call(
        body, name="rnn_bwd", grid=(H, nt),
        in_specs=[pl.BlockSpec((1, tt, DH), lambda hh, t: (0, rt(t), hh)),
                  pl.BlockSpec((1, tt, DH), lambda hh, t: (0, rt(t), H + hh)),
                  pl.BlockSpec((1, 16, DH), lambda hh, t: (0, jnp.maximum(rt(t) * (tt // 16) - 1, 0), hh)),
                  pl.BlockSpec((tt, DH), lambda hh, t: (rt(t), hh)),
                  pl.BlockSpec((8, DH), lambda hh, t: (prev8(t), hh)),
                  pl.BlockSpec((tt, DH), lambda hh, t: (rt(t), hh)),
                  pl.BlockSpec((tt, 1), lambda hh, t: (rt(t), 0)),
                  pl.BlockSpec((4, DH), head_row), pl.BlockSpec((1, DH), head_row),
                  pl.BlockSpec((1, DH, DH), hsm), pl.BlockSpec((1, 1, DH), hsm),
                  pl.BlockSpec((1, DH, DH), hsm), pl.BlockSpec((1, 1, DH), hsm),
                  pl.BlockSpec((1, DH), head_row)],
        out_specs=(pl.BlockSpec((tt, DH), lambda hh, t: (rt(t), hh)), pl.BlockSpec((tt, DH), lambda hh, t: (rt(t), hh)),
                   pl.BlockSpec((1, DH, DH), hsm), pl.BlockSpec((1, 1, DH), hsm),
                   pl.BlockSpec((1, DH, DH), hsm), pl.BlockSpec((1, 1, DH), hsm),
                   pl.BlockSpec((1, 1, DH), hsm), pl.BlockSpec((1, 4, DH), hsm), pl.BlockSpec((1, 1, DH), hsm)),
        out_shape=(jax.ShapeDtypeStruct((s, D), BF16), jax.ShapeDtypeStruct((s, D), BF16),
                   jax.ShapeDtypeStruct((H, DH, DH), F32), jax.ShapeDtypeStruct((H, 1, DH), F32),
                   jax.ShapeDtypeStruct((H, DH, DH), F32), jax.ShapeDtypeStruct((H, 1, DH), F32),
                   jax.ShapeDtypeStruct((H, 1, DH), F32), jax.ShapeDtypeStruct((H, 4, DH), F32),
                   jax.ShapeDtypeStruct((H, 1, DH), F32)),
        scratch_shapes=[pltpu.VMEM((8, DH), F32), pltpu.VMEM((8, DH), F32), pltpu.VMEM((8, DH), F32)],
        compiler_params=_params(("parallel", "arbitrary")),
    )(p, p, p, hr, hr, dgr, pos, conv_w, conv_b, w_a, b_a, w_x, b_x, lam)


def _rope(t, c, sa, sb):
    return t * c + pltpu.roll(t, DH - ROT // 2, 1) * sa + pltpu.roll(t, ROT // 2, 1) * sb


def _rope_bwd(g, c, sa, sb):
    return g * c + pltpu.roll(g * sa, ROT // 2, 1) + pltpu.roll(g * sb, DH - ROT // 2, 1)


def _unit_bases(gi, u):
    dil = DILATIONS[gi]
    if dil == 1:
        return u * UB, SPAN + (u - 1) * UB, u == 0
    if dil == 4:
        blk, r = u // 4, u % 4
        return blk * 4 * UB + r, SPAN + (blk - 1) * 4 * UB + r, blk == 0
    return u, u, True


def _unit_slices(gi, u):
    dil = DILATIONS[gi]
    qb0, kb0, first = _unit_bases(gi, u)
    if dil == 1:
        return pl.ds(pl.multiple_of(qb0, UB), UB), pl.ds(pl.multiple_of(kb0, UB), 2 * UB), first
    return pl.ds(qb0, UB, stride=dil), pl.ds(kb0, 2 * UB, stride=dil), first


def _bdot(a, b):
    return lax.dot_general(a, b, (((2,), (1,)), ((0,), (0,))), preferred_element_type=F32)


def _bdot_nt(a, b):
    return lax.dot_general(a, b, (((2,), (2,)), ((0,), (0,))), preferred_element_type=F32)


def _bdot_tn(a, b):
    return lax.dot_general(a, b, (((1,), (1,)), ((0,), (0,))), preferred_element_type=F32)


def _band_mask(first_in_span, has_prev):
    qi = lax.broadcasted_iota(jnp.int32, (UB, 2 * UB), 0)
    ki = lax.broadcasted_iota(jnp.int32, (UB, 2 * UB), 1)
    dist = UB + qi - ki
    band = (dist >= 0) & (dist <= UB)
    return band & ((ki >= UB) | jnp.logical_not(first_in_span) | has_prev)


def _attn_fwd(p, rc, rsa, rsb):
    s = p.shape[1]
    ns = s // SPAN
    nunit = SPAN // UB

    def body(q_ref, k_ref, v_ref, z_ref, c_ref, sa_ref, sb_ref, o_ref, lse_ref, ga_ref,
             qr, kf, vf, acc, mm, ll):
        n = pl.program_id(1)

        @pl.when(n == 0)
        def _():
            kf[0:SPAN] = jnp.zeros((SPAN, DH), F32)
            vf[0:SPAN] = jnp.zeros((SPAN, DH), F32)

        c, sa, sb = c_ref[...], sa_ref[...], sb_ref[...]
        qr[...] = _rope(q_ref[0].astype(F32), c, sa, sb)
        kf[SPAN:] = _rope(k_ref[0].astype(F32), c, sa, sb)
        vf[SPAN:] = v_ref[0].astype(F32)
        has_prev = n > 0

        for gi, dil in enumerate(DILATIONS):
            def trip(t, carry, gi=gi, dil=dil):
                qsls, ksls, firsts = [], [], []
                for b in range(UNIT_BATCH):
                    qsl, ksl, first = _unit_slices(gi, t * UNIT_BATCH + b)
                    qsls.append(qsl)
                    ksls.append(ksl)
                    firsts.append(first)
                qb = jnp.stack([qr[qsl, :].astype(BF16) for qsl in qsls])
                kb = jnp.stack([kf[ksl, :].astype(BF16) for ksl in ksls])
                vb = jnp.stack([vf[ksl, :].astype(BF16) for ksl in ksls])
                s_all = _bdot_nt(qb, kb)
                prs = []
                for b in range(UNIT_BATCH):
                    sc = jnp.where(_band_mask(firsts[b], has_prev), s_all[b] * SCALE, NEG)
                    m = jnp.max(sc, axis=-1, keepdims=True)
                    pr = jnp.exp(sc - m)
                    l = jnp.sum(pr, axis=-1, keepdims=True)
                    mm[gi, qsls[b], :] = jnp.broadcast_to(m, (UB, DH))
                    ll[gi, qsls[b], :] = jnp.broadcast_to(l, (UB, DH))
                    prs.append(pr.astype(BF16))
                o_all = _bdot(jnp.stack(prs), vb)
                for b in range(UNIT_BATCH):
                    acc[gi, qsls[b], :] = o_all[b]
                return carry

            lax.fori_loop(0, nunit // UNIT_BATCH, trip, 0)

        m_all =jnp.maximum(jnp.maximum(mm[0], mm[1]), mm[2])
        num = jnp.zeros((SPAN, DH), F32)
        den = jnp.zeros((SPAN, DH), F32)
        for gi in range(3):
            w = jnp.exp(mm[gi] - m_all)
            num = num + w * acc[gi]
            den = den + w * ll[gi]
        o = num / den
        o_ref[...] = o
        lse_ref[...] = m_all + jnp.log(den)
        z = z_ref[0].astype(F32)
        ga_ref[...] = (o * (z * _sigmoid(z))).astype(BF16)
        kf[0:SPAN] = kf[SPAN:]
        vf[0:SPAN] = vf[SPAN:]

    blk = lambda piece, off: pl.BlockSpec((1, SPAN, DH), lambda hh, n: (piece, n, off + hh))
    tab = pl.BlockSpec((SPAN, DH), lambda hh, n: (n, 0))
    outb = pl.BlockSpec((SPAN, DH), lambda hh, n: (n, hh))
    return pl.pallas_call(
        body, name="attn_fwd", grid=(H, ns),
        in_specs=[blk(1, 0), blk(1, H), blk(2, 0), blk(2, H), tab, tab, tab],
        out_specs=(outb, outb, outb),
        out_shape=(jax.ShapeDtypeStruct((s, D), F32), jax.ShapeDtypeStruct((s, D), F32),
                   jax.ShapeDtypeStruct((s, D), BF16)),
        scratch_shapes=[pltpu.VMEM((SPAN, DH), F32), pltpu.VMEM((2 * SPAN, DH), F32), pltpu.VMEM((2 * SPAN, DH), F32),
                        pltpu.VMEM((3, SPAN, DH), F32), pltpu.VMEM((3, SPAN, DH), F32), pltpu.VMEM((3, SPAN, DH), F32)],
        compiler_params=_params(("parallel", "arbitrary")),
    )(p, p, p, p, rc, rsa, rsb)


def _attn_bwd(p, o, lse, dga, rc, rsa, rsb):
    s = p.shape[1]
    ns = s // SPAN
    nunit = SPAN // UB

    def body(q_ref, k_ref, kp_ref, v_ref, vp_ref, z_ref, c_ref, sa_ref, sb_ref, cp_ref, sap_ref, sbp_ref,
             o_ref, lse_ref, dg_ref, dq_ref, dk_ref, dv_ref, dz_ref,
             qr, kf, vf, dof, dlt, dqa, dkf, dvf):
        step = pl.program_id(1)
        n = ns - 1 - step
        has_prev = n > 0

        @pl.when(step == 0)
        def _():
            dkf[...] = jnp.zeros_like(dkf)
            dvf[...] = jnp.zeros_like(dvf)

        @pl.when(step > 0)
        def _():
            dkf[SPAN:] = dkf[0:SPAN]
            dvf[SPAN:] = dvf[0:SPAN]
            dkf[0:SPAN] = jnp.zeros((SPAN, DH), F32)
            dvf[0:SPAN] = jnp.zeros((SPAN, DH), F32)

        c, sa, sb = c_ref[...], sa_ref[...], sb_ref[...]
        qr[...] = _rope(q_ref[0].astype(F32), c, sa, sb)
        kf[SPAN:] = _rope(k_ref[0].astype(F32), c, sa, sb)
        vf[SPAN:] = v_ref[0].astype(F32)
        kf[0:SPAN] = jnp.where(has_prev, _rope(kp_ref[0].astype(F32), cp_ref[...], sap_ref[...], sbp_ref[...]), 0.0)
        vf[0:SPAN] = jnp.where(has_prev, vp_ref[0].astype(F32), 0.0)
        z = z_ref[0].astype(F32)
        sz = _sigmoid(z)
        dg = dg_ref[...]
        ov = o_ref[...]
        do = dg * (z * sz)
        dz_ref[...] = (dg * ov * (sz * (1.0 + z * (1.0 - sz)))).astype(BF16)
        dof[...] = do
        dlt[...] = jnp.broadcast_to(jnp.sum(do * ov, axis=-1, keepdims=True), (SPAN, DH))
        dqa[...] = jnp.zeros_like(dqa)

        for gi, dil in enumerate(DILATIONS):
            def trip(t, carry, gi=gi, dil=dil):
                qsls, ksls, firsts = [], [], []
                for b in range(UNIT_BATCH):
                    qsl, ksl, first = _unit_slices(gi, t * UNIT_BATCH + b)
                    qsls.append(qsl)
                    ksls.append(ksl)
                    firsts.append(first)
                qb = jnp.stack([qr[qsl, :].astype(BF16) for qsl in qsls])
                kb = jnp.stack([kf[ksl, :].astype(BF16) for ksl in ksls])
                vb = jnp.stack([vf[ksl, :].astype(BF16) for ksl in ksls])
                dob = jnp.stack([dof[qsl, :].astype(BF16) for qsl in qsls])
                s_all = _bdot_nt(qb, kb)
                dp_all = _bdot_nt(dob, vb)
                prs, dss = [], []
                for b in range(UNIT_BATCH):
                    lse_b = lse_ref[qsls[b], :]
                    dl_b = dlt[qsls[b], :]
                    pr = jnp.exp(s_all[b] * SCALE - jnp.concatenate([lse_b, lse_b], axis=1))
                    pr = jnp.where(_band_mask(firsts[b], has_prev), pr, 0.0)
                    prs.append(pr.astype(BF16))
                    dss.append((pr * (dp_all[b] - jnp.concatenate([dl_b, dl_b], axis=1)) * SCALE).astype(BF16))
                ds_all = jnp.stack(dss)
                dv_all = _bdot_tn(jnp.stack(prs), dob)
                dq_all = _bdot(ds_all, kb)
                dk_all = _bdot_tn(ds_all, qb)
                for b in range(UNIT_BATCH):
                    dvf[ksls[b], :] += dv_all[b]
                    dqa[qsls[b], :] += dq_all[b]
                    dkf[ksls[b], :] += dk_all[b]
                return carry

            lax.fori_loop(0, nunit // UNIT_BATCH, trip, 0)

        dq_ref[...] = _rope_bwd(dqa[...], c, sa, sb).astype(BF16)
        dk_ref[...] = _rope_bwd(dkf[SPAN:], c, sa, sb).astype(BF16)
        dv_ref[...] = dvf[SPAN:].astype(BF16)

    rn = lambda n: ns - 1 - n
    pn = lambda n: jnp.maximum(ns - 2 - n, 0)
    blk = lambda piece, off: pl.BlockSpec((1, SPAN, DH), lambda hh, n: (piece, rn(n), off + hh))
    blkp = lambda piece, off: pl.BlockSpec((1, SPAN, DH), lambda hh, n: (piece, pn(n), off + hh))
    tab = pl.BlockSpec((SPAN, DH), lambda hh, n: (rn(n), 0))
    tabp = pl.BlockSpec((SPAN, DH), lambda hh, n: (pn(n), 0))
    io = pl.BlockSpec((SPAN, DH), lambda hh, n: (rn(n), hh))
    return pl.pallas_call(
        body, name="attn_bwd", grid=(H, ns),
        in_specs=[blk(1, 0), blk(1, H), blkp(1, H), blk(2, 0), blkp(2, 0), blk(2, H),
                  tab, tab, tab, tabp, tabp, tabp, io, io, io],
        out_specs=(io, io, io, io),
        out_shape=tuple(jax.ShapeDtypeStruct((s, D), BF16) for _ in range(4)),
        scratch_shapes=[pltpu.VMEM((SPAN, DH), F32), pltpu.VMEM((2 * SPAN, DH), F32), pltpu.VMEM((2 * SPAN, DH), F32),
                        pltpu.VMEM((SPAN, DH), F32), pltpu.VMEM((SPAN, DH), F32), pltpu.VMEM((SPAN, DH), F32),
                        pltpu.VMEM((2 * SPAN, DH), F32), pltpu.VMEM((2 * SPAN, DH), F32)],
        compiler_params=_params(("parallel", "arbitrary")),
    )(p, p, p, p, p, p, rc, rsa, rsb, rc, rsa, rsb, o, lse, dga)


def _tail(gr, ga, p, x, tgt, w3, b_gate, gate, g_final, tm=256):
    s = x.shape[0]
    nt = s // tm

    def body(gr_ref, ga_ref, pr_ref, pa_ref, x_ref, t_ref, bg_ref, gate_ref, gf_ref, w_hbm,
             dgr_ref, dga_ref, dc_ref, dx2_ref, vec_ref, go_hbm, w_s, acc_s, sem):
        i = pl.program_id(0)

        @pl.when(i == 0)
        def _():
            cp = pltpu.make_async_copy(w_hbm, w_s, sem.at[12])
            cp.start()
            acc_s[...] = jnp.zeros_like(acc_s)
            vec_ref[...] = jnp.zeros_like(vec_ref)
            cp.wait()

        grb = gr_ref[...]
        gab = ga_ref[...]
        bg = bg_ref[...]
        gate_v = gate_ref[...]
        gf = gf_ref[...]
        y_r = _dot(grb, w_s[0])
        y_a = _dot(gab, w_s[1])
        sr = _sigmoid(pr_ref[0].astype(F32) + bg[:, :D])
        sa = _sigmoid(pa_ref[0].astype(F32) + bg[:, D:])
        mb = (sr * y_r + sa * y_a).astype(BF16)
        u = _dot(mb, w_s[2])
        x2 = x_ref[...] + gate_v * u
        rstd = lax.rsqrt(jnp.mean(x2 * x2, axis=-1, keepdims=True) + EPS)
        xh = x2 * rstd
        e = xh * gf - t_ref[...]
        dy = e * (1.0 / D)
        dyg = dy * gf
        dx2 = rstd * (dyg - xh * jnp.mean(dyg * xh, axis=-1, keepdims=True))
        dx2_ref[...] = dx2
        dub = (dx2 * gate_v).astype(BF16)
        dm = _dot_nt(dub, w_s[2])
        dyr = (dm * sr).astype(BF16)
        dya = (dm * sa).astype(BF16)
        dpr = dm * y_r * (sr * (1.0 - sr))
        dpa = dm * y_a * (sa * (1.0 - sa))
        dc_ref[:, :D] = dpr.astype(BF16)
        dc_ref[:, D:] = dpa.astype(BF16)
        dgr_ref[...] = _dot_nt(dyr, w_s[0])
        dga_ref[...] = _dot_nt(dya, w_s[1])
        acc_s[0] += _dot_tn(grb, dyr)
        acc_s[1] += _dot_tn(gab, dya)
        acc_s[2] += _dot_tn(mb, dub)
        vec_ref[0:1, :] += _colsum(dy * xh)
        vec_ref[1:2, :] += _colsum(dx2 * u)
        vec_ref[2:3, :] += _colsum(dpr)
        vec_ref[3:4, :] += _colsum(dpa)
        vec_ref[4:5, :] += _colsum(e * e)

        @pl.when(i == nt - 1)
        def _():
            vec_ref[4:5, :] = jnp.broadcast_to(jnp.sum(vec_ref[4:5, :]) * (0.5 / D), (1, D))
            cps = []
            for w in range(3):
                for j in range(4):
                    cps.append(pltpu.make_async_copy(acc_s.at[w, pl.ds(256 * j, 256)],
                                                     go_hbm.at[j, pl.ds(256 * w, 256)], sem.at[4 * w + j]))
            for cp in cps:
                cp.start()
            for cp in cps:
                cp.wait()

    rowt = lambda i: (i, 0)
    row = lambda w: pl.BlockSpec((1, w), lambda i: (0, 0))
    any_ = pl.BlockSpec(memory_space=pl.ANY)
    return pl.pallas_call(
        body, name="tail", grid=(nt,),
        in_specs=[pl.BlockSpec((tm, D), rowt), pl.BlockSpec((tm, D), rowt),
                  pl.BlockSpec((1, tm, D), lambda i: (3, i, 0)), pl.BlockSpec((1, tm, D), lambda i: (3, i, 1)),
                  pl.BlockSpec((tm, D), rowt), pl.BlockSpec((tm, D), rowt),
                  row(2 * D), row(D), row(D), any_],
        out_specs=(pl.BlockSpec((tm, D), rowt), pl.BlockSpec((tm, D), rowt), pl.BlockSpec((tm, 2 * D), rowt),
                   pl.BlockSpec((tm, D), rowt), pl.BlockSpec((8, D), lambda i: (0, 0)), any_),
        out_shape=(jax.ShapeDtypeStruct((s, D), F32), jax.ShapeDtypeStruct((s, D), F32),
                   jax.ShapeDtypeStruct((s, 2 * D), BF16), jax.ShapeDtypeStruct((s, D), F32),
                   jax.ShapeDtypeStruct((8, D), F32), jax.ShapeDtypeStruct((4, 768, D), F32)),
        scratch_shapes=[pltpu.VMEM((3, D, D), BF16), pltpu.VMEM((3, D, D), F32), pltpu.SemaphoreType.DMA((13,))],
        compiler_params=_params(("arbitrary",)),
    )(gr, ga, p, p, x, tgt, b_gate, gate, g_final, w3)


def _pieces_steps(pieces):
    out, s0 = [], 0
    for a in pieces:
        n = a.shape[1] // D
        out.append((s0, n))
        s0 += n
    return out, s0


def _inproj_bwd_x(pieces, wg, x, dx2, gn, scale, sums, tm=512):
    s = x.shape[0]
    np_ = len(pieces)
    na = len(sums)
    ni = s // tm
    groups, cur, width = [], [], 0
    for t, a in enumerate(pieces):
        cur.append(t)
        width += a.shape[1]
        if width == PW:
            groups.append(cur)
            cur, width = [], 0
    assert len(groups) == 4 and not cur

    def body(*refs):
        d_refs = refs[:np_]
        w_hbm, x_ref, dx2_ref, gn_ref, sc_ref = refs[np_:np_ + 5]
        q_refs = refs[np_ + 5:np_ + 5 + na]
        gx_ref, vec_ref = refs[np_ + 5 + na:np_ + 7 + na]
        r_refs = refs[np_ + 7 + na:np_ + 7 + 2 * na]
        w_s, wsem, ss, rs = refs[np_ + 7 + 2 * na:]
        i = pl.program_id(0)

        def scatter_copies():
            cx, cy, cc = _coords()
            j = 2 * cx + cy
            cps = []
            for t, (q, r) in enumerate(zip(q_refs, r_refs)):
                for e, (kx, ky) in enumerate(((1, 0), (0, 1), (1, 1))):
                    cps.append(_rcopy(q.at[j ^ (2 * kx + ky)], r.at[e], ss.at[3 * t + e], rs.at[3 * t + e],
                                      (_flip(cx, kx), _flip(cy, ky), cc)))
            return cps

        def w_copy(pc):
            return pltpu.make_async_copy(w_hbm.at[pc, pl.ds(0, D), :], w_s.at[pc], wsem.at[pc])

        @pl.when(i == 0)
        def _():
            for pc in range(4):
                w_copy(pc).start()
            vec_ref[...] = jnp.zeros_like(vec_ref)
            for cp in scatter_copies():
                cp.start()

        dh = None
        for pc, group in enumerate(groups):
            @pl.when(i == 0)
            def _(pc=pc):
                w_copy(pc).wait()

            tiles = [d_refs[t][...] for t in group]
            lhs = tiles[0] if len(tiles) == 1 else jnp.concatenate(tiles, axis=1)
            part = _dot_nt(lhs, w_s[pc])
            dh = part if dh is None else dh + part

        xt = x_ref[...]
        rstd = lax.rsqrt(jnp.mean(xt * xt, axis=-1, keepdims=True) + EPS)
        xh = xt * rstd
        gn_v = gn_ref[...]
        sc1 = 1.0 + sc_ref[...]
        dhx = dh * xh
        vec_ref[0:1, :] += _colsum(dh)
        vec_ref[1:2, :] += _colsum(dhx) * gn_v
        vec_ref[2:3, :] += _colsum(dhx) * sc1
        dxh = dh * (gn_v * sc1)
        gx_ref[...] = rstd * (dxh - xh * jnp.mean(dxh * xh, axis=-1, keepdims=True)) + dx2_ref[...]

        @pl.when(i == ni - 1)
        def _():
            for cp in scatter_copies():
                cp.wait()

    rowt = lambda i: (i, 0)
    row = pl.BlockSpec((1, D), lambda i: (0, 0))
    any_ = pl.BlockSpec(memory_space=pl.ANY)
    outs = pl.pallas_call(
        body, name="inproj_bwd_x", grid=(ni,),
        in_specs=[pl.BlockSpec((tm, a.shape[1]), rowt) for a in pieces] +
                 [any_, pl.BlockSpec((tm, D), rowt), pl.BlockSpec((tm, D), rowt), row, row] + [any_] * na,
        out_specs=(pl.BlockSpec((tm, D), rowt), pl.BlockSpec((8, D), lambda i: (0, 0))) + (any_,) * na,
        out_shape=(jax.ShapeDtypeStruct((s, D), F32), jax.ShapeDtypeStruct((8, D), F32)) +
                  tuple(jax.ShapeDtypeStruct((3,) + q.shape[1:], q.dtype) for q in sums),
        scratch_shapes=[pltpu.VMEM((4, D, PW), BF16), pltpu.SemaphoreType.DMA((4,)),
                        pltpu.SemaphoreType.DMA((3 * na,)), pltpu.SemaphoreType.DMA((3 * na,))],
        compiler_params=_params(("arbitrary",)),
    )(*pieces, wg, x, dx2, gn, scale, *sums)
    return outs[0], outs[1], outs[2:]


def _inproj_bwd_w(pieces, hbf, tk=1024):
    s = hbf.shape[0]
    steps, nk = _pieces_steps(pieces)
    npc = PW // D
    ns = s // tk
    np_ = len(pieces)

    def body(*refs):
        d_refs = refs[:np_]
        h_ref, g_ref = refs[np_:]
        cb, k = pl.program_id(0), pl.program_id(1)

        @pl.when(k == 0)
        def _():
            g_ref[...] = jnp.zeros_like(g_ref)

        for (s0, n), d_ref in zip(steps, d_refs):
            @pl.when((cb >= s0) & (cb < s0 + n))
            def _(d_ref=d_ref):
                g_ref[0] += _dot_tn(h_ref[...], d_ref[...])

    def piece_spec(s0, n):
        def imap(cb, k):
            active = (cb >= s0) & (cb < s0 + n)
            return (jnp.where(active, k, 0), jnp.clip(cb - s0, 0, n - 1))
        return pl.BlockSpec((tk, D), imap)

    return pl.pallas_call(
        body, name="inproj_bwd_w", grid=(nk, ns),
        in_specs=[piece_spec(s0, n) for s0, n in steps] + [pl.BlockSpec((tk, D), lambda cb, k: (k, 0))],
        out_specs=pl.BlockSpec((1, D, D), lambda cb, k: (cb // npc, 0, cb % npc)),
        out_shape=jax.ShapeDtypeStruct((4, D, PW), F32),
        compiler_params=_params(("parallel", "arbitrary")),
    )(*pieces, hbf)


D2D_CHUNK_BYTES = 512 * 1024


def _chunk_rows(a):
    return max(8, D2D_CHUNK_BYTES // (a.shape[-1] * a.dtype.itemsize))


def _pair_exchange(arrs):
    na = len(arrs)
    chunks = []
    for t, a in enumerate(arrs):
        hr = a.shape[1] // 2
        cr = _chunk_rows(a)
        chunks += [(t, j, r0, cr) for j in range(a.shape[0]) for r0 in range(0, hr, cr)]
    nch = len(chunks)

    def body(*refs):
        a_refs = refs[:na]
        rb_refs = refs[na:2 * na]
        ss, rs = refs[2 * na:]
        x, y, c = _coords()
        sib = (x, y, 1 - c)
        rcs = []
        for n, (t, j, r0, cr) in enumerate(chunks):
            hr = a_refs[t].shape[1] // 2
            rc = _rcopy(a_refs[t].at[j, pl.ds((1 - c) * hr + r0, cr), :], rb_refs[t].at[j, pl.ds(r0, cr), :],
                        ss.at[n], rs.at[n], sib)
            rc.start()
            rcs.append(rc)
        for rc in rcs:
            rc.wait_recv()
        for rc in rcs:
            rc.wait_send()

    any_ = pl.BlockSpec(memory_space=pl.ANY)
    halves = [jax.ShapeDtypeStruct((a.shape[0], a.shape[1] // 2, a.shape[2]), a.dtype) for a in arrs]
    return pl.pallas_call(
        body, name="pair_exchange",
        out_shape=tuple(halves),
        in_specs=[any_] * na, out_specs=tuple([any_] * na),
        scratch_shapes=[pltpu.SemaphoreType.DMA((nch,)), pltpu.SemaphoreType.DMA((nch,))],
        compiler_params=_params(),
    )(*arrs)


def _pair_swap(arrs):
    na = len(arrs)
    chunks = []
    for t, a in enumerate(arrs):
        cr = _chunk_rows(a)
        chunks += [(t, r0, cr) for r0 in range(0, a.shape[0], cr)]
    nch = len(chunks)

    def body(*refs):
        a_refs = refs[:na]
        o_refs = refs[na:2 * na]
        ss, rs = refs[2 * na:]
        x, y, c = _coords()
        sib = (x, y, 1 - c)
        rcs = []
        for n, (t, r0, cr) in enumerate(chunks):
            rows = pl.ds(r0, cr)
            rc = _rcopy(a_refs[t].at[rows, :], o_refs[t].at[rows, :], ss.at[n], rs.at[n], sib)
            rc.start()
            rcs.append(rc)
        for rc in rcs:
            rc.wait_recv()
        for rc in rcs:
            rc.wait_send()

    any_ = pl.BlockSpec(memory_space=pl.ANY)
    return pl.pallas_call(
        body, name="pair_swap",
        out_shape=tuple(jax.ShapeDtypeStruct(a.shape, a.dtype) for a in arrs),
        in_specs=[any_] * na, out_specs=tuple([any_] * na),
        scratch_shapes=[pltpu.SemaphoreType.DMA((nch,)), pltpu.SemaphoreType.DMA((nch,))],
        compiler_params=_params(),
    )(*arrs)


def _add_half(full, rb, core, tr):
    n, r, cdim = full.shape
    nb = r // 2 // tr

    def body(c_ref, a_ref, b_ref, o_ref):
        o_ref[...] = a_ref[...] + b_ref[...]

    mine = pl.BlockSpec((1, tr, cdim), lambda i, j, c_ref: (i, c_ref[0] * nb + j, 0))
    spec = pl.BlockSpec((1, tr, cdim), lambda i, j, c_ref: (i, j, 0))
    return pl.pallas_call(
        body, name="add_half",
        grid_spec=pltpu.PrefetchScalarGridSpec(num_scalar_prefetch=1, grid=(n, nb), in_specs=[mine, spec],
                                               out_specs=spec),
        out_shape=jax.ShapeDtypeStruct(rb.shape, rb.dtype),
        compiler_params=_params(("parallel", "parallel")),
    )(core, full, rb)


def _sum_slots(q, r3, shard, tr):
    _, r, cdim = q.shape

    def body(j_ref, q_ref, r_ref, o_ref):
        o_ref[...] = ((q_ref[0] + r_ref[0]) + r_ref[1]) + r_ref[2]

    return pl.pallas_call(
        body, name="sum_slots",
        grid_spec=pltpu.PrefetchScalarGridSpec(
            num_scalar_prefetch=1, grid=(r // tr,),
            in_specs=[pl.BlockSpec((1, tr, cdim), lambda i, j_ref: (j_ref[0], i, 0)),
                      pl.BlockSpec((3, tr, cdim), lambda i, j_ref: (0, i, 0))],
            out_specs=pl.BlockSpec((tr, cdim), lambda i, j_ref: (i, 0))),
        out_shape=jax.ShapeDtypeStruct((r, cdim), q.dtype),
        compiler_params=_params(("parallel",)),
    )(shard, q, r3)


def _allreduce_small(pack):
    def body(p_ref, out_ref, rbuf, s1, r1, s2, r2):
        me = _my_index()
        chunk = lambda d: pl.ds(pl.multiple_of(d * AR_CHUNK, 8), AR_CHUNK)
        sends = []
        for k in range(1, NDEV):
            cp = _rcopy(p_ref.at[chunk(me ^ k)], rbuf.at[me], s1.at[k - 1], r1.at[k - 1], _peer(k))
            cp.start()
            sends.append(cp)
        rbuf[me] = p_ref[chunk(me), :]
        for k in range(1, NDEV):
            _rcopy(p_ref.at[chunk(me)], rbuf.at[me ^ k], s1.at[k - 1], r1.at[k - 1], _peer(k)).wait_recv()
        tot = rbuf[0]
        for d in range(1, NDEV):
            tot = tot + rbuf[d]
        out_ref[chunk(me), :] = tot
        for k in range(1, NDEV):
            cp = _rcopy(out_ref.at[chunk(me)], out_ref.at[chunk(me)], s2.at[k - 1], r2.at[k - 1], _peer(k))
            cp.start()
            sends.append(cp)
        for k in range(1, NDEV):
            _rcopy(out_ref.at[chunk(me)], out_ref.at[chunk(me ^ k)], s2.at[k - 1], r2.at[k - 1], _peer(k)).wait_recv()
        for cp in sends:
            cp.wait_send()

    vm = pl.BlockSpec(memory_space=pltpu.VMEM)
    return pl.pallas_call(
        body, name="allreduce_small",
        out_shape=jax.ShapeDtypeStruct((AR_ROWS, D), F32),
        in_specs=[vm], out_specs=vm,
        scratch_shapes=[pltpu.VMEM((NDEV, AR_CHUNK, D), F32),
                        pltpu.SemaphoreType.DMA((7,)), pltpu.SemaphoreType.DMA((7,)),
                        pltpu.SemaphoreType.DMA((7,)), pltpu.SemaphoreType.DMA((7,))],
        compiler_params=_params(),
    )(pack)


def _adamw(w, g, m, v, tr):
    r, cdim = w.shape

    def body(w_ref, g_ref, m_ref, v_ref, d_ref, nm_ref, nv_ref):
        gv = g_ref[...]
        nm = B1 * m_ref[...] + (1.0 - B1) * gv
        nv = B2 * v_ref[...] + (1.0 - B2) * (gv * gv)
        m_hat = nm / (1.0 - B1 ** STEP)
        v_hat = nv / (1.0 - B2 ** STEP)
        d_ref[...] = -LR * (m_hat / (jnp.sqrt(v_hat) + ADAM_EPS) + WD * w_ref[...])
        nm_ref[...] = nm
        nv_ref[...] = nv

    spec = pl.BlockSpec((tr, cdim), lambda i: (i, 0))
    sd = jax.ShapeDtypeStruct((r, cdim), F32)
    return pl.pallas_call(
        body, name="adamw", grid=(r // tr,), in_specs=[spec] * 4, out_specs=(spec,) * 3, out_shape=(sd,) * 3,
        compiler_params=_params(("parallel",)),
    )(w, g, m, v)


def _rope_tables(positions):
    inv_freq = ROPE_THETA ** (-jnp.arange(0, ROT, 2, dtype=F32) / ROT)
    ang = positions.astype(F32)[:, None] * inv_freq
    cos, sin = jnp.cos(ang), jnp.sin(ang)
    n = positions.shape[0]
    half = ROT // 2
    rc = jnp.concatenate([cos, cos, jnp.ones((n, DH - ROT), F32)], axis=1)
    rsa = jnp.concatenate([-sin, jnp.zeros((n, DH - half), F32)], axis=1)
    rsb = jnp.concatenate([jnp.zeros((n, half), F32), sin, jnp.zeros((n, DH - ROT), F32)], axis=1)
    return rc, rsa, rsb


def kernel(x, c, positions, g_norm, w_mod, b_mod, w_in, b_gate, conv_w, conv_b, w_a, b_a, w_x, b_x, lam, w_out_rnn, w_out_attn, w_o, g_final, loss_target, m_g_norm, m_w_mod, m_b_mod, m_w_in, m_b_gate, m_conv_w, m_conv_b, m_w_a, m_b_a, m_w_x, m_b_x, m_lam, m_w_out_rnn, m_w_out_attn, m_w_o, m_g_final, v_g_norm, v_w_mod, v_b_mod, v_w_in, v_b_gate, v_conv_w, v_conv_b, v_w_a, v_b_a, v_w_x, v_b_x, v_lam, v_w_out_rnn, v_w_out_attn, v_w_o, v_g_final):
    s = x.shape[1]
    xi = lax.axis_index("x")
    yi = lax.axis_index("y")
    ci = lax.axis_index("c")
    shard = 2 * xi + yi
    x2d = x[0]
    tgt = loss_target[0]
    pos = positions[0]

    c_all, mod4 = _mod_fwd(c, w_mod[0], b_mod.reshape(4, 1, 768))
    mod = mod4.reshape(1, 3 * D)
    shift, scale, gate = mod[:, :D], mod[:, D:2 * D], mod[:, 2 * D:]
    w3_sh = jnp.stack([w_out_rnn[0], w_out_attn[0], w_o[0]]).astype(BF16)
    wsh = jnp.concatenate([w_in[0].astype(BF16), w3_sh.reshape(384, PW)], axis=0)
    wg = lax.dynamic_update_slice(_gather_weights(wsh), wsh[None], (shard, 0, 0))
    w3 = wg[:, D:, :].reshape(4, 3, 256, D).transpose(1, 0, 2, 3).reshape(3, D, D)
    conv_all = _gather_small(conv_w[0])
    conv_full = conv_all[0::2].transpose(1, 0, 2).reshape(4, D)

    p, hbf = _norm_inproj(x2d, g_norm, shift, scale, wg)
    rc, rsa, rsb = _rope_tables(pos)
    pos_col = pos.reshape(s, 1)
    b_a3, b_x3 = b_a.reshape(H, 1, DH), b_x.reshape(H, 1, DH)
    hr, gr = _rnn_fwd(p, pos_col, conv_full, conv_b, w_a[0], b_a3, w_x[0], b_x3, lam)
    o, lse, ga = _attn_fwd(p, rc, rsa, rsb)

    dgr, dga, dc, dx2, vec_t, g_out = _tail(gr, ga, p, x2d, tgt, w3, b_gate, gate, g_final.reshape(1, D))

    dxr, dzr, g_wa, g_ba, g_wx, g_bx, g_lam, g_cw, g_cb = _rnn_bwd(
        p, hr, dgr, pos_col, conv_full, conv_b, w_a[0], b_a3, w_x[0], b_x3, lam)
    dq, dk, dv, dza = _attn_bwd(p, o, lse, dga, rc, rsa, rsb)

    pieces = [dxr, dzr, dq, dk, dv, dza, dc]
    g_win = _inproj_bwd_w(pieces, hbf)

    core = ci.reshape(1)
    shard1 = shard.reshape(1)
    rb_a, rb_b = _pair_exchange([g_win, g_out])
    q_a, q_b = _add_half(g_win, rb_a, core, tr=256), _add_half(g_out, rb_b, core, tr=128)
    grad_x, vec_n, (r_a, r_b) = _inproj_bwd_x(pieces, wg, x2d, dx2, g_norm, scale, [q_a, q_b])
    f_a, f_b = _sum_slots(q_a, r_a, shard1, tr=256), _sum_slots(q_b, r_b, shard1, tr=128)
    s_a, s_b = _pair_swap([f_a, f_b])
    south = ci == 0
    grad_w_in = jnp.where(south, jnp.concatenate([f_a, s_a], axis=0), jnp.concatenate([s_a, f_a], axis=0))
    g3 = jnp.where(south, jnp.concatenate([f_b, s_b], axis=0), jnp.concatenate([s_b, f_b], axis=0)).reshape(3, 256, D)

    dmod_row = jnp.concatenate([vec_n[0:1], vec_n[1:2], vec_t[1:2]], axis=1)
    pack = jnp.concatenate([
        vec_n[2:3],
        dmod_row.reshape(3, D),
        vec_t[2:4],
        g_cb.reshape(1, D),
        g_wa.reshape(128, D),
        g_ba.reshape(1, D),
        g_wx.reshape(128, D),
        g_bx.reshape(1, D),
        g_lam.reshape(1, D),
        vec_t[0:1],
        g_cw.transpose(1, 0, 2).reshape(4, D),
        vec_t[4:5],
        jnp.zeros((AR_ROWS - 272, D), F32)], axis=0)
    red = _allreduce_small(pack)
    loss = red[271, 0]
    grad_w_mod = _mod_bwd(dmod_row.reshape(4, 1, 768), c_all)
    g_conv_sh = lax.dynamic_slice_in_dim(red[267:271], shard * 256, 256, axis=1)

    def small_pack(g_norm_, b_mod_, b_gate_, conv_b_, w_a_, b_a_, w_x_, b_x_, lam_, g_final_, conv_w_):
        return jnp.concatenate([
            g_norm_.reshape(1, D), b_mod_.reshape(3, D), b_gate_.reshape(2, D), conv_b_.reshape(1, D),
            w_a_.reshape(128, D), b_a_.reshape(1, D), w_x_.reshape(128, D), b_x_.reshape(1, D),
            lam_.reshape(1, D), g_final_.reshape(1, D), conv_w_.reshape(1, D),
            jnp.zeros((4, D), F32)], axis=0)

    wp = small_pack(g_norm, b_mod, b_gate, conv_b, w_a, b_a, w_x, b_x, lam, g_final, conv_w)
    mp = small_pack(m_g_norm, m_b_mod, m_b_gate, m_conv_b, m_w_a, m_b_a, m_w_x, m_b_x, m_lam, m_g_final, m_conv_w)
    vp = small_pack(v_g_norm, v_b_mod, v_b_gate, v_conv_b, v_w_a, v_b_a, v_w_x, v_b_x, v_lam, v_g_final, v_conv_w)
    gp = jnp.concatenate([red[0:267], g_conv_sh.reshape(1, D), jnp.zeros((4, D), F32)], axis=0)
    small = _adamw(wp, gp, mp, vp, tr=136)

    def unpack(a):
        return dict(
            g_norm=a[0:1], b_mod=a[1:4].reshape(1, 3 * D), b_gate=a[4:6].reshape(1, 2 * D), conv_b=a[6:7],
            w_a=a[7:135].reshape(1, H, DH, DH), b_a=a[135:136].reshape(1, H, DH),
            w_x=a[136:264].reshape(1, H, DH, DH), b_x=a[264:265].reshape(1, H, DH), lam=a[265:266],
            g_final=a[266].reshape(D), conv_w=a[267:268].reshape(1, 4, 256))

    big_in = _adamw(w_in[0], grad_w_in, m_w_in[0], v_w_in[0], tr=256)
    big_mod = _adamw(w_mod[0], grad_w_mod, m_w_mod[0], v_w_mod[0], tr=256)
    w3f = jnp.concatenate([w_out_rnn[0], w_out_attn[0], w_o[0]], axis=0)
    m3f = jnp.concatenate([m_w_out_rnn[0], m_w_out_attn[0], m_w_o[0]], axis=0)
    v3f = jnp.concatenate([v_w_out_rnn[0], v_w_out_attn[0], v_w_o[0]], axis=0)
    big_out = _adamw(w3f, g3.reshape(768, D), m3f, v3f, tr=256)

    names = ["g_norm", "w_mod", "b_mod", "w_in", "b_gate", "conv_w", "conv_b", "w_a", "b_a", "w_x", "b_x", "lam",
             "w_out_rnn", "w_out_attn", "w_o", "g_final"]
    grads = unpack(gp)
    grads.update(w_mod=grad_w_mod[None], w_in=grad_w_in[None],
                 w_out_rnn=g3[0][None], w_out_attn=g3[1][None], w_o=g3[2][None])
    outs = [grads]
    for idx in range(3):
        d = unpack(small[idx])
        d.update(w_mod=big_mod[idx][None], w_in=big_in[idx][None],
                 w_out_rnn=big_out[idx][0:256][None], w_out_attn=big_out[idx][256:512][None],
                 w_o=big_out[idx][512:768][None])
        outs.append(d)
    flat = [d[n] for d in outs for n in names]
    return (loss, grad_x[None], *flat)
```

```python
import jax
import jax.numpy as jnp
from jax import lax
from jax.experimental import pallas as pl
from jax.experimental.pallas import tpu as pltpu

F32, BF16 = jnp.float32, jnp.bfloat16
MESH = pl.DeviceIdType.MESH
HIGHEST = lax.Precision.HIGHEST

D = 1024
H = 8
DH = 128
PW = 2048
EPS = 1e-6
LRU_C = 8.0
SCALE = DH ** -0.5
NEG = -1e30
SPAN = 2048
UB = 128
DILATIONS = (1, 4, 16)
UNIT_BATCH = 8
UNIT_UNROLL = 8
ROPE_THETA = 500000.0
ROT = 32

LR, B1, B2, ADAM_EPS, WD, STEP = 0.001, 0.9, 0.999, 1e-08, 0.01, 10

WROWS = 1024 + 384
NDEV = 8
AR_ROWS = 320
AR_CHUNK = AR_ROWS // NDEV


def _params(sem=None, vmem_mb=56):
    return pltpu.CompilerParams(dimension_semantics=sem, vmem_limit_bytes=vmem_mb * 2 ** 20)


def _coords():
    return lax.axis_index("x"), lax.axis_index("y"), lax.axis_index("c")


def _flip(v, bit):
    return 1 - v if bit else v


def _peer(k):
    x, y, c = _coords()
    return (_flip(x, (k >> 2) & 1), _flip(y, (k >> 1) & 1), _flip(c, k & 1))


def _my_index():
    x, y, c = _coords()
    return 4 * x + 2 * y + c


def _rcopy(src, dst, ssem, rsem, dev):
    return pltpu.make_async_remote_copy(src_ref=src, dst_ref=dst, send_sem=ssem, recv_sem=rsem,
                                        device_id=dev, device_id_type=MESH)


def _sigmoid(x):
    return jax.nn.sigmoid(x)


def _dot(a, b):
    return jnp.dot(a, b, preferred_element_type=F32)


def _dot_nt(a, b):
    return lax.dot_general(a, b, (((1,), (1,)), ((), ())), preferred_element_type=F32)


def _dot_tn(a, b):
    return lax.dot_general(a, b, (((0,), (0,)), ((), ())), preferred_element_type=F32)


def _colsum(a):
    return jnp.sum(a, axis=0, keepdims=True)


def _mod_fwd(c, w_mod_sh, b_mod4):
    def body(c_ref, w_ref, b_ref, call_ref, mod_ref, rows_ref, cmat_ref, s1, r1, s2, r2):
        x, y, _ = _coords()
        me = _my_index()
        j = 2 * x + y
        call_ref[me] = c_ref[...]
        sends = []
        for k in range(1, NDEV):
            cp = _rcopy(call_ref.at[me], call_ref.at[me], s1.at[k - 1], r1.at[k - 1], _peer(k))
            cp.start()
            sends.append(cp)
        for k in range(1, NDEV):
            pk = me ^ k
            _rcopy(call_ref.at[pk], call_ref.at[pk], s1.at[k - 1], r1.at[k - 1], _peer(k)).wait_recv()
        for b in range(NDEV):
            cmat_ref[pl.ds(b, 1), :] = call_ref[b]
        cm = cmat_ref[...]
        act = cm * _sigmoid(cm)
        mp = jnp.dot(act, w_ref[...], preferred_element_type=F32, precision=HIGHEST) + b_ref[j]
        for b in range(NDEV):
            rows_ref[b] = mp[b:b + 1]
        mod_ref[j] = rows_ref[me]
        for q, k in enumerate((2, 4, 6)):
            cp = _rcopy(rows_ref.at[me ^ k], mod_ref.at[j], s2.at[q], r2.at[q], _peer(k))
            cp.start()
            sends.append(cp)
        for q, k in enumerate((2, 4, 6)):
            jq = j ^ (k >> 1)
            _rcopy(rows_ref.at[me], mod_ref.at[jq], s2.at[q], r2.at[q], _peer(k)).wait_recv()
        for cp in sends:
            cp.wait_send()

    vm = pl.BlockSpec(memory_space=pltpu.VMEM)
    return pl.pallas_call(
        body, name="mod_fwd",
        out_shape=(jax.ShapeDtypeStruct((NDEV, 1, D), F32), jax.ShapeDtypeStruct((4, 1, 768), F32)),
        in_specs=[vm, vm, vm], out_specs=(vm, vm),
        scratch_shapes=[pltpu.VMEM((NDEV, 1, 768), F32), pltpu.VMEM((NDEV, D), F32),
                        pltpu.SemaphoreType.DMA((7,)), pltpu.SemaphoreType.DMA((7,)),
                        pltpu.SemaphoreType.DMA((3,)), pltpu.SemaphoreType.DMA((3,))],
        compiler_params=_params(),
    )(c, w_mod_sh, b_mod4)


def _mod_bwd(dmod4, c_all):
    def body(d_ref, call_ref, gw_ref, dall_ref, cmat_ref, dmat_ref, s1, r1):
        x, y, _ = _coords()
        me = _my_index()
        j = 2 * x + y
        dall_ref[me] = d_ref[...]
        sends = []
        for k in range(1, NDEV):
            cp = _rcopy(dall_ref.at[me], dall_ref.at[me], s1.at[k - 1], r1.at[k - 1], _peer(k))
            cp.start()
            sends.append(cp)
        for k in range(1, NDEV):
            pk = me ^ k
            _rcopy(dall_ref.at[pk], dall_ref.at[pk], s1.at[k - 1], r1.at[k - 1], _peer(k)).wait_recv()
        for cp in sends:
            cp.wait_send()
        for b in range(NDEV):
            cmat_ref[pl.ds(b, 1), :] = call_ref[b]
            dmat_ref[pl.ds(b, 1), :] = dall_ref[b, j]
        cm = cmat_ref[...]
        act = cm * _sigmoid(cm)
        gw_ref[...] = lax.dot_general(act, dmat_ref[...], (((0,), (0,)), ((), ())),
                                      preferred_element_type=F32, precision=HIGHEST)

    vm = pl.BlockSpec(memory_space=pltpu.VMEM)
    return pl.pallas_call(
        body, name="mod_bwd",
        out_shape=jax.ShapeDtypeStruct((D, 768), F32),
        in_specs=[vm, vm], out_specs=vm,
        scratch_shapes=[pltpu.VMEM((NDEV, 4, 1, 768), F32), pltpu.VMEM((NDEV, D), F32), pltpu.VMEM((NDEV, 768), F32),
                        pltpu.SemaphoreType.DMA((7,)), pltpu.SemaphoreType.DMA((7,))],
        compiler_params=_params(),
    )(dmod4, c_all)


def _gather_norm_inproj(x, gn, shift, scale, wsh, order, tm=1024, tn=512):
    s = x.shape[0]
    ni = s // tm
    npc = PW // tn
    rows, cols = wsh.shape
    half = rows // 2
    nch = 4
    cr = half // nch
    chips = ((1, 0), (0, 1), (1, 1))

    def body(ord_ref, x_ref, gn_ref, sh_ref, sc_ref, wsh_hbm, p_ref, h_ref, wg_hbm, hs_all, w_s, wsem, ss, rs):
        slot, i, col = pl.program_id(0), pl.program_id(1), pl.program_id(2)
        cx, cy, cc = _coords()
        j = 2 * cx + cy
        sib = (cx, cy, 1 - cc)
        mine = lambda n: pl.ds(cc * half + n * cr, cr)
        theirs = lambda n: pl.ds((1 - cc) * half + n * cr, cr)
        shard_of = lambda q: j ^ (2 * chips[q][0] + chips[q][1])

        def to_chip(q, n):
            e = nch * q + n
            return _rcopy(wsh_hbm.at[mine(n)], wg_hbm.at[j, mine(n)], ss.at[e], rs.at[e],
                          (_flip(cx, chips[q][0]), _flip(cy, chips[q][1]), cc))

        def from_chip(q, n):
            e = nch * q + n
            return _rcopy(wsh_hbm.at[mine(n)], wg_hbm.at[shard_of(q), mine(n)], ss.at[e], rs.at[e], sib)

        def to_sibling(q, n):
            e = 3 * nch + nch * q + n
            return _rcopy(wg_hbm.at[shard_of(q), mine(n)], wg_hbm.at[shard_of(q), mine(n)], ss.at[e], rs.at[e], sib)

        def from_sibling(q, n):
            e = 3 * nch + nch * q + n
            return _rcopy(wsh_hbm.at[mine(n)], wg_hbm.at[shard_of(q), theirs(n)], ss.at[e], rs.at[e], sib)

        def load(sl, src):
            cp = pltpu.make_async_copy(src, w_s.at[sl], wsem.at[sl])
            cp.start()
            cp.wait()

        first = (i == 0) & (col == 0)

        @pl.when(first & (slot == 0))
        def _():
            for n in range(nch):
                for q in range(3):
                    to_chip(q, n).start()
            load(0, wsh_hbm.at[pl.ds(0, D), :])

        @pl.when(first & (slot == 1))
        def _():
            for q in (0, 1):
                for n in range(nch):
                    from_chip(q, n).wait_recv()
                    to_sibling(q, n).start()
            for n in range(nch):
                from_sibling(0, n).wait_recv()
            load(1, wg_hbm.at[shard_of(0), pl.ds(0, D), :])

        @pl.when(first & (slot == 2))
        def _():
            for n in range(nch):
                from_chip(2, n).wait_recv()
                to_sibling(2, n).start()
            for n in range(nch):
                from_sibling(1, n).wait_recv()
            load(2, wg_hbm.at[shard_of(1), pl.ds(0, D), :])

        @pl.when(first & (slot == 3))
        def _():
            for n in range(nch):
                from_sibling(2, n).wait_recv()
            load(3, wg_hbm.at[shard_of(2), pl.ds(0, D), :])
            for q in range(3):
                for n in range(nch):
                    to_chip(q, n).wait_send()
                    to_sibling(q, n).wait_send()

        @pl.when((slot == 0) & (col == 0))
        def _():
            xt = x_ref[...]
            rstd = lax.rsqrt(jnp.mean(xt * xt, axis=-1, keepdims=True) + EPS)
            h = ((xt * rstd * gn_ref[...]) * (1.0 + sc_ref[...]) + sh_ref[...]).astype(BF16)
            hs_all[i] = h
            h_ref[...] = h

        p_ref[0] = _dot(hs_all[i], w_s[slot, :, pl.ds(pl.multiple_of(col * tn, tn), tn)]).astype(BF16)

    row = pl.BlockSpec((1, D), lambda sl, i, col, o: (0, 0))
    x_rows = lambda sl, i, col, o: (jnp.where(sl == 0, i, ni - 1), 0)
    any_ = pl.BlockSpec(memory_space=pl.ANY)
    return pl.pallas_call(
        body, name="gather_norm_inproj",
        grid_spec=pltpu.PrefetchScalarGridSpec(
            num_scalar_prefetch=1, grid=(4, ni, npc),
            in_specs=[pl.BlockSpec((tm, D), x_rows), row, row, row, any_],
            out_specs=(pl.BlockSpec((1, tm, tn), lambda sl, i, col, o: (o[sl], i, col)),
                       pl.BlockSpec((tm, D), x_rows), any_),
            scratch_shapes=[pltpu.VMEM((ni, tm, D), BF16), pltpu.VMEM((4, D, PW), BF16),
                            pltpu.SemaphoreType.DMA((4,)),
                            pltpu.SemaphoreType.DMA((6 * nch,)), pltpu.SemaphoreType.DMA((6 * nch,))]),
        out_shape=(jax.ShapeDtypeStruct((4, s, PW), BF16), jax.ShapeDtypeStruct((s, D), BF16),
                   jax.ShapeDtypeStruct((4, rows, cols), wsh.dtype)),
        compiler_params=_params(("arbitrary", "arbitrary", "arbitrary")),
    )(order, x, gn, shift, scale, wsh)


def _gather_small(v):
    r, cdim = v.shape

    def body(v_ref, out_ref, ss, rs):
        me = _my_index()
        out_ref[me] = v_ref[...]
        sends = []
        for k in range(1, NDEV):
            cp = _rcopy(out_ref.at[me], out_ref.at[me], ss.at[k - 1], rs.at[k - 1], _peer(k))
            cp.start()
            sends.append(cp)
        for k in range(1, NDEV):
            pk = me ^ k
            _rcopy(out_ref.at[pk], out_ref.at[pk], ss.at[k - 1], rs.at[k - 1], _peer(k)).wait_recv()
        for cp in sends:
            cp.wait_send()

    vm = pl.BlockSpec(memory_space=pltpu.VMEM)
    return pl.pallas_call(
        body, name="gather_small",
        out_shape=jax.ShapeDtypeStruct((NDEV, r, cdim), v.dtype),
        in_specs=[vm], out_specs=vm,
        scratch_shapes=[pltpu.SemaphoreType.DMA((7,)), pltpu.SemaphoreType.DMA((7,))],
        compiler_params=_params(),
    )(v)


def _shift_down(prev8, cur, d):
    t = cur.shape[0]
    c3 = cur.reshape(t // 8, 8, DH)
    rot = pltpu.roll(c3, d, 1)
    before = jnp.concatenate([pltpu.roll(prev8, d, 0).reshape(1, 8, DH), rot[:-1]], axis=0)
    rows = lax.broadcasted_iota(jnp.int32, c3.shape, 1)
    return jnp.where(rows >= d, rot, before).reshape(t, DH)


def _shift_up(cur, next8, d):
    t = cur.shape[0]
    c3 = cur.reshape(t // 8, 8, DH)
    rot = pltpu.roll(c3, 8 - d, 1)
    after = jnp.concatenate([rot[1:], pltpu.roll(next8, 8 - d, 0).reshape(1, 8, DH)], axis=0)
    rows = lax.broadcasted_iota(jnp.int32, c3.shape, 1)
    return jnp.where(rows < 8 - d, rot, after).reshape(t, DH)


def _rnn_gates(xr, prev8, cw, cb, wa, ba, wx, bx, lam, reset):
    xc = cw[3:4] * xr + cb
    for d in (1, 2, 3):
        xc = xc + cw[3 - d:4 - d] * _shift_down(prev8, xr, d)
    xcb = xc.astype(BF16)
    r = _sigmoid(_dot(xcb, wa.astype(BF16)) + ba)
    ig = _sigmoid(_dot(xcb, wx.astype(BF16)) + bx)
    nl = -lam
    sp = jnp.maximum(nl, 0.0) + jnp.log1p(jnp.exp(-jnp.abs(nl)))
    log_a = (-LRU_C * r) * sp
    a = jnp.where(reset, 0.0, jnp.exp(log_a))
    mult = jnp.where(reset, 1.0, jnp.sqrt(1.0 - jnp.exp(2.0 * log_a)))
    return xc, r, ig, sp, a, mult


def _log_scan(a, b, axis, up):
    n = a.shape[axis]
    rows = lax.broadcasted_iota(jnp.int32, a.shape, axis)
    d = 1
    while d < n:
        m = rows < n - d if up else rows >= d
        shift = n - d if up else d
        a_s = pltpu.roll(a, shift, axis)
        b_s = pltpu.roll(b, shift, axis)
        b = jnp.where(m, a * b_s + b, b)
        a = jnp.where(m, a * a_s, a)
        d *= 2
    return a, b


def _scan(a, b, t, edge, up=False):
    g = t // 8
    a3, b3 = _log_scan(a.reshape(g, 8, DH), b.reshape(g, 8, DH), 1, up)
    last = 0 if up else 7
    ag, bg = _log_scan(a3[:, last, :], b3[:, last, :], 0, up)
    hg = ag * edge + bg
    grp = lax.broadcasted_iota(jnp.int32, hg.shape, 0)
    if up:
        cin = jnp.where(grp == g - 1, edge, pltpu.roll(hg, g - 1, 0))
        tail = hg[0:1]
    else:
        cin = jnp.where(grp == 0, edge, pltpu.roll(hg, 1, 0))
        tail = hg[g - 1:g]
    return (a3 * cin[:, None, :] + b3).reshape(t, DH), tail


def _rnn_fwd(p, pos, conv_w, conv_b, w_a, b_a, w_x, b_x, lam, tt=512):
    s = p.shape[1]
    nt = s // tt

    def body(xr_ref, z_ref, pos_ref, cw_ref, cb_ref, wa_ref, ba_ref, wx_ref, bx_ref, lam_ref,
             hr_ref, gr_ref, xprev, hprev):
        @pl.when(pl.program_id(1) == 0)
        def _():
            xprev[...] = jnp.zeros_like(xprev)
            hprev[...] = jnp.zeros_like(hprev)

        xr = xr_ref[0].astype(F32)
        z = z_ref[0].astype(F32)
        reset = pos_ref[...] == 0
        xc, r, ig, sp, a, mult = _rnn_gates(xr, xprev[...], cw_ref[...], cb_ref[...], wa_ref[0], ba_ref[0],
                                            wx_ref[0], bx_ref[0], lam_ref[...], reset)
        bx = mult * ig * xc
        h, h_last = _scan(a, bx, tt, hprev[0:1])
        xprev[...] = xr[tt - 8:]
        hprev[...] = jnp.broadcast_to(h_last, (8, DH))
        hr_ref[...] = h
        gr_ref[...] = (h * (z * _sigmoid(z))).astype(BF16)

    head_row = lambda hh, t: (0, hh)
    return pl.pallas_call(
        body, name="rnn_fwd", grid=(H, nt),
        in_specs=[pl.BlockSpec((1, tt, DH), lambda hh, t: (0, t, hh)),
                  pl.BlockSpec((1, tt, DH), lambda hh, t: (0, t, H + hh)),
                  pl.BlockSpec((tt, 1), lambda hh, t: (t, 0)),
                  pl.BlockSpec((4, DH), head_row), pl.BlockSpec((1, DH), head_row),
                  pl.BlockSpec((1, DH, DH), lambda hh, t: (hh, 0, 0)), pl.BlockSpec((1, 1, DH), lambda hh, t: (hh, 0, 0)),
                  pl.BlockSpec((1, DH, DH), lambda hh, t: (hh, 0, 0)), pl.BlockSpec((1, 1, DH), lambda hh, t: (hh, 0, 0)),
                  pl.BlockSpec((1, DH), head_row)],
        out_specs=(pl.BlockSpec((tt, DH), lambda hh, t: (t, hh)), pl.BlockSpec((tt, DH), lambda hh, t: (t, hh))),
        out_shape=(jax.ShapeDtypeStruct((s, D), F32), jax.ShapeDtypeStruct((s, D), BF16)),
        scratch_shapes=[pltpu.VMEM((8, DH), F32), pltpu.VMEM((8, DH), F32)],
        compiler_params=_params(("parallel", "arbitrary")),
    )(p, p, pos, conv_w, conv_b, w_a, b_a, w_x, b_x, lam)


def _rnn_bwd(p, hr, dgr, pos, conv_w, conv_b, w_a, b_a, w_x, b_x, lam, tt=512):
    s = p.shape[1]
    nt = s // tt
    t8 = tt // 8

    def body(xr_ref, z_ref, xp_ref, hr_ref, hp_ref, dg_ref, pos_ref, cw_ref, cb_ref, wa_ref, ba_ref, wx_ref, bx_ref,
             lam_ref, dxr_ref, dz_ref, gwa_ref, gba_ref, gwx_ref, gbx_ref, glam_ref, gcw_ref, gcb_ref,
             a_next, g_next, dxc_next):
        t = pl.program_id(1)
        has_prev = t < nt - 1

        @pl.when(t == 0)
        def _():
            a_next[...] = jnp.zeros_like(a_next)
            g_next[...] = jnp.zeros_like(g_next)
            dxc_next[...] = jnp.zeros_like(dxc_next)
            gwa_ref[...] = jnp.zeros_like(gwa_ref)
            gba_ref[...] = jnp.zeros_like(gba_ref)
            gwx_ref[...] = jnp.zeros_like(gwx_ref)
            gbx_ref[...] = jnp.zeros_like(gbx_ref)
            glam_ref[...] = jnp.zeros_like(glam_ref)
            gcw_ref[...] = jnp.zeros_like(gcw_ref)
            gcb_ref[...] = jnp.zeros_like(gcb_ref)

        xr = xr_ref[0].astype(F32)
        z = z_ref[0].astype(F32)
        hr_blk = hr_ref[...]
        dg = dg_ref[...]
        xprev = jnp.where(has_prev, xp_ref[0].astype(F32)[8:], 0.0)
        hprev8 = jnp.where(has_prev, hp_ref[...], 0.0)
        reset = pos_ref[...] == 0
        cw = cw_ref[...]
        wa = wa_ref[0]
        wx = wx_ref[0]
        lam_v = lam_ref[...]
        xc, r, ig, sp, a, mult = _rnn_gates(xr, xprev, cw, cb_ref[...], wa, ba_ref[0], wx, bx_ref[0], lam_v, reset)

        sz = _sigmoid(z)
        dh = dg * (z * sz)
        dz_ref[...] = (dg * hr_blk * (sz * (1.0 + z * (1.0 - sz)))).astype(BF16)

        an = _shift_up(a, a_next[...], 1)
        g, g_first = _scan(an, dh, tt, g_next[0:1], up=True)
        a_next[...] = jnp.broadcast_to(a[0:1], (8, DH))
        g_next[...] = jnp.broadcast_to(g_first, (8, DH))

        hm1 = _shift_down(hprev8, hr_blk, 1)
        da = g * hm1
        dmult = g * (ig * xc)
        di = g * (mult * xc)
        dxc = g * (mult * ig)
        dla = jnp.where(reset, 0.0, da * a - dmult * (a * a) / mult)
        dr = dla * (-LRU_C * sp)
        dsp = _colsum(dla * (-LRU_C * r))
        glam_ref[0] += dsp * (-_sigmoid(-lam_v))
        dpa = dr * r * (1.0 - r)
        dpx = di * ig * (1.0 - ig)
        dpab = dpa.astype(BF16)
        dpxb = dpx.astype(BF16)
        dxc = dxc + _dot_nt(dpab, wa.astype(BF16)) + _dot_nt(dpxb, wx.astype(BF16))
        xcb = xc.astype(BF16)
        gwa_ref[0] += _dot_tn(xcb, dpab)
        gwx_ref[0] += _dot_tn(xcb, dpxb)
        gba_ref[0] += _colsum(dpa)
        gbx_ref[0] += _colsum(dpx)

        dxr = cw[3:4] * dxc
        for d in (1, 2, 3):
            dxr = dxr + cw[3 - d:4 - d] * _shift_up(dxc, dxc_next[...], d)
        dxr_ref[...] = dxr.astype(BF16)
        dxc_next[...] = dxc[0:8]
        gcb_ref[0] += _colsum(dxc)
        gcw_ref[0, 3:4, :] += _colsum(xr * dxc)
        for d in (1, 2, 3):
            gcw_ref[0, 3 - d:4 - d, :] += _colsum(_shift_down(xprev, xr, d) * dxc)

    rt = lambda t: nt - 1 - t
    prev8 = lambda t: jnp.maximum(rt(t) * t8 - 1, 0)
    head_row = lambda hh, t: (0, hh)
    hsm = lambda hh, t: (hh, 0, 0)
    return pl.pallas_call(
        body, name="rnn_bwd", grid=(H, nt),
        in_specs=[pl.BlockSpec((1, tt, DH), lambda hh, t: (0, rt(t), hh)),
                  pl.BlockSpec((1, tt, DH), lambda hh, t: (0, rt(t), H + hh)),
                  pl.BlockSpec((1, 16, DH), lambda hh, t: (0, jnp.maximum(rt(t) * (tt // 16) - 1, 0), hh)),
                  pl.BlockSpec((tt, DH), lambda hh, t: (rt(t), hh)),
                  pl.BlockSpec((8, DH), lambda hh, t: (prev8(t), hh)),
                  pl.BlockSpec((tt, DH), lambda hh, t: (rt(t), hh)),
                  pl.BlockSpec((tt, 1), lambda hh, t: (rt(t), 0)),
                  pl.BlockSpec((4, DH), head_row), pl.BlockSpec((1, DH), head_row),
                  pl.BlockSpec((1, DH, DH), hsm), pl.BlockSpec((1, 1, DH), hsm),
                  pl.BlockSpec((1, DH, DH), hsm), pl.BlockSpec((1, 1, DH), hsm),
                  pl.BlockSpec((1, DH), head_row)],
        out_specs=(pl.BlockSpec((tt, DH), lambda hh, t: (rt(t), hh)), pl.BlockSpec((tt, DH), lambda hh, t: (rt(t), hh)),
                   pl.BlockSpec((1, DH, DH), hsm), pl.BlockSpec((1, 1, DH), hsm),
                   pl.BlockSpec((1, DH, DH), hsm), pl.BlockSpec((1, 1, DH), hsm),
                   pl.BlockSpec((1, 1, DH), hsm), pl.BlockSpec((1, 4, DH), hsm), pl.BlockSpec((1, 1, DH), hsm)),
        out_shape=(jax.ShapeDtypeStruct((s, D), BF16), jax.ShapeDtypeStruct((s, D), BF16),
                   jax.ShapeDtypeStruct((H, DH, DH), F32), jax.ShapeDtypeStruct((H, 1, DH), F32),
                   jax.ShapeDtypeStruct((H, DH, DH), F32), jax.ShapeDtypeStruct((H, 1, DH), F32),
                   jax.ShapeDtypeStruct((H, 1, DH), F32), jax.ShapeDtypeStruct((H, 4, DH), F32),
                   jax.ShapeDtypeStruct((H, 1, DH), F32)),
        scratch_shapes=[pltpu.VMEM((8, DH), F32), pltpu.VMEM((8, DH), F32), pltpu.VMEM((8, DH), F32)],
        compiler_params=_params(("parallel", "arbitrary")),
    )(p, p, p, hr, hr, dgr, pos, conv_w, conv_b, w_a, b_a, w_x, b_x, lam)


def _rope(t, c, sa, sb):
    return t * c + pltpu.roll(t, DH - ROT // 2, 1) * sa + pltpu.roll(t, ROT // 2, 1) * sb


def _rope_bwd(g, c, sa, sb):
    return g * c + pltpu.roll(g * sa, ROT // 2, 1) + pltpu.roll(g * sb, DH - ROT // 2, 1)


def _unit_bases(gi, u):
    dil = DILATIONS[gi]
    if dil == 1:
        return u * UB, SPAN + (u - 1) * UB, u == 0
    if dil == 4:
        blk, r = u // 4, u % 4
        return blk * 4 * UB + r, SPAN + (blk - 1) * 4 * UB + r, blk == 0
    return u, u, True


def _unit_slices(gi, u):
    dil = DILATIONS[gi]
    qb0, kb0, first = _unit_bases(gi, u)
    if dil == 1:
        return pl.ds(pl.multiple_of(qb0, UB), UB), pl.ds(pl.multiple_of(kb0, UB), 2 * UB), first
    return pl.ds(qb0, UB, stride=dil), pl.ds(kb0, 2 * UB, stride=dil), first


def _bdot(a, b):
    return lax.dot_general(a, b, (((2,), (1,)), ((0,), (0,))), preferred_element_type=F32)


def _bdot_nt(a, b):
    return lax.dot_general(a, b, (((2,), (2,)), ((0,), (0,))), preferred_element_type=F32)


def _bdot_tn(a, b):
    return lax.dot_general(a, b, (((1,), (1,)), ((0,), (0,))), preferred_element_type=F32)


def _band_mask(first_in_span, has_prev):
    qi = lax.broadcasted_iota(jnp.int32, (UB, 2 * UB), 0)
    ki = lax.broadcasted_iota(jnp.int32, (UB, 2 * UB), 1)
    dist = UB + qi - ki
    band = (dist >= 0) & (dist <= UB)
    return band & ((ki >= UB) | jnp.logical_not(first_in_span) | has_prev)


def _attn_fwd(p, rc, rsa, rsb):
    s = p.shape[1]
    ns = s // SPAN
    nunit = SPAN // UB

    def body(q_ref, k_ref, v_ref, z_ref, c_ref, sa_ref, sb_ref, o_ref, lse_ref, ga_ref,
             qr, kf, vf, acc, mm, ll):
        n = pl.program_id(1)

        @pl.when(n == 0)
        def _():
            kf[0:SPAN] = jnp.zeros((SPAN, DH), F32)
            vf[0:SPAN] = jnp.zeros((SPAN, DH), F32)

        c, sa, sb = c_ref[...], sa_ref[...], sb_ref[...]
        qr[...] = _rope(q_ref[0].astype(F32), c, sa, sb)
        kf[SPAN:] = _rope(k_ref[0].astype(F32), c, sa, sb)
        vf[SPAN:] = v_ref[0].astype(F32)
        has_prev = n > 0

        for gi, dil in enumerate(DILATIONS):
            def trip(t, carry, gi=gi, dil=dil):
                qsls, ksls, firsts = [], [], []
                for b in range(UNIT_BATCH):
                    qsl, ksl, first = _unit_slices(gi, t * UNIT_BATCH + b)
                    qsls.append(qsl)
                    ksls.append(ksl)
                    firsts.append(first)
                qb = jnp.stack([qr[qsl, :].astype(BF16) for qsl in qsls])
                kb = jnp.stack([kf[ksl, :].astype(BF16) for ksl in ksls])
                vb = jnp.stack([vf[ksl, :].astype(BF16) for ksl in ksls])
                s_all = _bdot_nt(qb, kb)
                prs = []
                for b in range(UNIT_BATCH):
                    sc = jnp.where(_band_mask(firsts[b], has_prev), s_all[b] * SCALE, NEG)
                    m = jnp.max(sc, axis=-1, keepdims=True)
                    pr = jnp.exp(sc - m)
                    l = jnp.sum(pr, axis=-1, keepdims=True)
                    mm[gi, qsls[b], :] = jnp.broadcast_to(m, (UB, DH))
                    ll[gi, qsls[b], :] = jnp.broadcast_to(l, (UB, DH))
                    prs.append(pr.astype(BF16))
                o_all = _bdot(jnp.stack(prs), vb)
                for b in range(UNIT_BATCH):
                    acc[gi, qsls[b], :] = o_all[b]
                return carry

            lax.fori_loop(0, nunit // UNIT_BATCH, trip, 0)

        m_all =jnp.maximum(jnp.maximum(mm[0], mm[1]), mm[2])
        num = jnp.zeros((SPAN, DH), F32)
        den = jnp.zeros((SPAN, DH), F32)
        for gi in range(3):
            w = jnp.exp(mm[gi] - m_all)
            num = num + w * acc[gi]
            den = den + w * ll[gi]
        o = num / den
        o_ref[...] = o
        lse_ref[...] = m_all + jnp.log(den)
        z = z_ref[0].astype(F32)
        ga_ref[...] = (o * (z * _sigmoid(z))).astype(BF16)
        kf[0:SPAN] = kf[SPAN:]
        vf[0:SPAN] = vf[SPAN:]

    blk = lambda piece, off: pl.BlockSpec((1, SPAN, DH), lambda hh, n: (piece, n, off + hh))
    tab = pl.BlockSpec((SPAN, DH), lambda hh, n: (n, 0))
    outb = pl.BlockSpec((SPAN, DH), lambda hh, n: (n, hh))
    return pl.pallas_call(
        body, name="attn_fwd", grid=(H, ns),
        in_specs=[blk(1, 0), blk(1, H), blk(2, 0), blk(2, H), tab, tab, tab],
        out_specs=(outb, outb, outb),
        out_shape=(jax.ShapeDtypeStruct((s, D), F32), jax.ShapeDtypeStruct((s, D), F32),
                   jax.ShapeDtypeStruct((s, D), BF16)),
        scratch_shapes=[pltpu.VMEM((SPAN, DH), F32), pltpu.VMEM((2 * SPAN, DH), F32), pltpu.VMEM((2 * SPAN, DH), F32),
                        pltpu.VMEM((3, SPAN, DH), F32), pltpu.VMEM((3, SPAN, DH), F32), pltpu.VMEM((3, SPAN, DH), F32)],
        compiler_params=_params(("parallel", "arbitrary")),
    )(p, p, p, p, rc, rsa, rsb)


def _attn_bwd(p, o, lse, dga, rc, rsa, rsb):
    s = p.shape[1]
    ns = s // SPAN
    nunit = SPAN // UB

    def body(q_ref, k_ref, kp_ref, v_ref, vp_ref, z_ref, c_ref, sa_ref, sb_ref, cp_ref, sap_ref, sbp_ref,
             o_ref, lse_ref, dg_ref, dq_ref, dk_ref, dv_ref, dz_ref,
             qr, kf, vf, dof, dlt, dqa, dkf, dvf):
        step = pl.program_id(1)
        n = ns - 1 - step
        has_prev = n > 0

        @pl.when(step == 0)
        def _():
            dkf[...] = jnp.zeros_like(dkf)
            dvf[...] = jnp.zeros_like(dvf)

        @pl.when(step > 0)
        def _():
            dkf[SPAN:] = dkf[0:SPAN]
            dvf[SPAN:] = dvf[0:SPAN]
            dkf[0:SPAN] = jnp.zeros((SPAN, DH), F32)
            dvf[0:SPAN] = jnp.zeros((SPAN, DH), F32)

        c, sa, sb = c_ref[...], sa_ref[...], sb_ref[...]
        qr[...] = _rope(q_ref[0].astype(F32), c, sa, sb)
        kf[SPAN:] = _rope(k_ref[0].astype(F32), c, sa, sb)
        vf[SPAN:] = v_ref[0].astype(F32)
        kf[0:SPAN] = jnp.where(has_prev, _rope(kp_ref[0].astype(F32), cp_ref[...], sap_ref[...], sbp_ref[...]), 0.0)
        vf[0:SPAN] = jnp.where(has_prev, vp_ref[0].astype(F32), 0.0)
        z = z_ref[0].astype(F32)
        sz = _sigmoid(z)
        dg = dg_ref[...]
        ov = o_ref[...]
        do = dg * (z * sz)
        dz_ref[...] = (dg * ov * (sz * (1.0 + z * (1.0 - sz)))).astype(BF16)
        dof[...] = do
        dlt[...] = jnp.broadcast_to(jnp.sum(do * ov, axis=-1, keepdims=True), (SPAN, DH))
        dqa[...] = jnp.zeros_like(dqa)

        for gi, dil in enumerate(DILATIONS):
            def trip(t, carry, gi=gi, dil=dil):
                qsls, ksls, firsts = [], [], []
                for b in range(UNIT_BATCH):
                    qsl, ksl, first = _unit_slices(gi, t * UNIT_BATCH + b)
                    qsls.append(qsl)
                    ksls.append(ksl)
                    firsts.append(first)
                qb = jnp.stack([qr[qsl, :].astype(BF16) for qsl in qsls])
                kb = jnp.stack([kf[ksl, :].astype(BF16) for ksl in ksls])
                vb = jnp.stack([vf[ksl, :].astype(BF16) for ksl in ksls])
                dob = jnp.stack([dof[qsl, :].astype(BF16) for qsl in qsls])
                s_all = _bdot_nt(qb, kb)
                dp_all = _bdot_nt(dob, vb)
                prs, dss = [], []
                for b in range(UNIT_BATCH):
                    lse_b = lse_ref[qsls[b], :]
                    dl_b = dlt[qsls[b], :]
                    pr = jnp.exp(s_all[b] * SCALE - jnp.concatenate([lse_b, lse_b], axis=1))
                    pr = jnp.where(_band_mask(firsts[b], has_prev), pr, 0.0)
                    prs.append(pr.astype(BF16))
                    dss.append((pr * (dp_all[b] - jnp.concatenate([dl_b, dl_b], axis=1)) * SCALE).astype(BF16))
                ds_all = jnp.stack(dss)
                dv_all = _bdot_tn(jnp.stack(prs), dob)
                dq_all = _bdot(ds_all, kb)
                dk_all = _bdot_tn(ds_all, qb)
                for b in range(UNIT_BATCH):
                    dvf[ksls[b], :] += dv_all[b]
                    dqa[qsls[b], :] += dq_all[b]
                    dkf[ksls[b], :] += dk_all[b]
                return carry

            lax.fori_loop(0, nunit // UNIT_BATCH, trip, 0)

        dq_ref[...] = _rope_bwd(dqa[...], c, sa, sb).astype(BF16)
        dk_ref[...] = _rope_bwd(dkf[SPAN:], c, sa, sb).astype(BF16)
        dv_ref[...] = dvf[SPAN:].astype(BF16)

    rn = lambda n: ns - 1 - n
    pn = lambda n: jnp.maximum(ns - 2 - n, 0)
    blk = lambda piece, off: pl.BlockSpec((1, SPAN, DH), lambda hh, n: (piece, rn(n), off + hh))
    blkp = lambda piece, off: pl.BlockSpec((1, SPAN, DH), lambda hh, n: (piece, pn(n), off + hh))
    tab = pl.BlockSpec((SPAN, DH), lambda hh, n: (rn(n), 0))
    tabp = pl.BlockSpec((SPAN, DH), lambda hh, n: (pn(n), 0))
    io = pl.BlockSpec((SPAN, DH), lambda hh, n: (rn(n), hh))
    return pl.pallas_call(
        body, name="attn_bwd", grid=(H, ns),
        in_specs=[blk(1, 0), blk(1, H), blkp(1, H), blk(2, 0), blkp(2, 0), blk(2, H),
                  tab, tab, tab, tabp, tabp, tabp, io, io, io],
        out_specs=(io, io, io, io),
        out_shape=tuple(jax.ShapeDtypeStruct((s, D), BF16) for _ in range(4)),
        scratch_shapes=[pltpu.VMEM((SPAN, DH), F32), pltpu.VMEM((2 * SPAN, DH), F32), pltpu.VMEM((2 * SPAN, DH), F32),
                        pltpu.VMEM((SPAN, DH), F32), pltpu.VMEM((SPAN, DH), F32), pltpu.VMEM((SPAN, DH), F32),
                        pltpu.VMEM((2 * SPAN, DH), F32), pltpu.VMEM((2 * SPAN, DH), F32)],
        compiler_params=_params(("parallel", "arbitrary")),
    )(p, p, p, p, p, p, rc, rsa, rsb, rc, rsa, rsb, o, lse, dga)


def _tail(gr, ga, p, x, tgt, w3, b_gate, gate, g_final, tm=256):
    s = x.shape[0]
    nt = s // tm

    def body(gr_ref, ga_ref, pr_ref, pa_ref, x_ref, t_ref, bg_ref, gate_ref, gf_ref, w_hbm,
             dgr_ref, dga_ref, dc_ref, dx2_ref, vec_ref, go_hbm, w_s, acc_s, sem):
        i = pl.program_id(0)

        @pl.when(i == 0)
        def _():
            cp = pltpu.make_async_copy(w_hbm, w_s, sem.at[12])
            cp.start()
            acc_s[...] = jnp.zeros_like(acc_s)
            vec_ref[...] = jnp.zeros_like(vec_ref)
            cp.wait()

        grb = gr_ref[...]
        gab = ga_ref[...]
        bg = bg_ref[...]
        gate_v = gate_ref[...]
        gf = gf_ref[...]
        y_r = _dot(grb, w_s[0])
        y_a = _dot(gab, w_s[1])
        sr = _sigmoid(pr_ref[0].astype(F32) + bg[:, :D])
        sa = _sigmoid(pa_ref[0].astype(F32) + bg[:, D:])
        mb = (sr * y_r + sa * y_a).astype(BF16)
        u = _dot(mb, w_s[2])
        x2 = x_ref[...] + gate_v * u
        rstd = lax.rsqrt(jnp.mean(x2 * x2, axis=-1, keepdims=True) + EPS)
        xh = x2 * rstd
        e = xh * gf - t_ref[...]
        dy = e * (1.0 / D)
        dyg = dy * gf
        dx2 = rstd * (dyg - xh * jnp.mean(dyg * xh, axis=-1, keepdims=True))
        dx2_ref[...] = dx2
        dub = (dx2 * gate_v).astype(BF16)
        dm = _dot_nt(dub, w_s[2])
        dyr = (dm * sr).astype(BF16)
        dya = (dm * sa).astype(BF16)
        dpr = dm * y_r * (sr * (1.0 - sr))
        dpa = dm * y_a * (sa * (1.0 - sa))
        dc_ref[:, :D] = dpr.astype(BF16)
        dc_ref[:, D:] = dpa.astype(BF16)
        dgr_ref[...] = _dot_nt(dyr, w_s[0])
        dga_ref[...] = _dot_nt(dya, w_s[1])
        acc_s[0] += _dot_tn(grb, dyr)
        acc_s[1] += _dot_tn(gab, dya)
        acc_s[2] += _dot_tn(mb, dub)
        vec_ref[0:1, :] += _colsum(dy * xh)
        vec_ref[1:2, :] += _colsum(dx2 * u)
        vec_ref[2:3, :] += _colsum(dpr)
        vec_ref[3:4, :] += _colsum(dpa)
        vec_ref[4:5, :] += _colsum(e * e)

        @pl.when(i == nt - 1)
        def _():
            vec_ref[4:5, :] = jnp.broadcast_to(jnp.sum(vec_ref[4:5, :]) * (0.5 / D), (1, D))
            cps = []
            for w in range(3):
                for j in range(4):
                    cps.append(pltpu.make_async_copy(acc_s.at[w, pl.ds(256 * j, 256)],
                                                     go_hbm.at[j, pl.ds(256 * w, 256)], sem.at[4 * w + j]))
            for cp in cps:
                cp.start()
            for cp in cps:
                cp.wait()

    rowt = lambda i: (i, 0)
    row = lambda w: pl.BlockSpec((1, w), lambda i: (0, 0))
    any_ = pl.BlockSpec(memory_space=pl.ANY)
    return pl.pallas_call(
        body, name="tail", grid=(nt,),
        in_specs=[pl.BlockSpec((tm, D), rowt), pl.BlockSpec((tm, D), rowt),
                  pl.BlockSpec((1, tm, D), lambda i: (3, i, 0)), pl.BlockSpec((1, tm, D), lambda i: (3, i, 1)),
                  pl.BlockSpec((tm, D), rowt), pl.BlockSpec((tm, D), rowt),
                  row(2 * D), row(D), row(D), any_],
        out_specs=(pl.BlockSpec((tm, D), rowt), pl.BlockSpec((tm, D), rowt), pl.BlockSpec((tm, 2 * D), rowt),
                   pl.BlockSpec((tm, D), rowt), pl.BlockSpec((8, D), lambda i: (0, 0)), any_),
        out_shape=(jax.ShapeDtypeStruct((s, D), F32), jax.ShapeDtypeStruct((s, D), F32),
                   jax.ShapeDtypeStruct((s, 2 * D), BF16), jax.ShapeDtypeStruct((s, D), F32),
                   jax.ShapeDtypeStruct((8, D), F32), jax.ShapeDtypeStruct((4, 768, D), F32)),
        scratch_shapes=[pltpu.VMEM((3, D, D), BF16), pltpu.VMEM((3, D, D), F32), pltpu.SemaphoreType.DMA((13,))],
        compiler_params=_params(("arbitrary",)),
    )(gr, ga, p, p, x, tgt, b_gate, gate, g_final, w3)


def _pieces_steps(pieces):
    out, s0 = [], 0
    for a in pieces:
        n = a.shape[1] // D
        out.append((s0, n))
        s0 += n
    return out, s0


def _inproj_bwd_x(pieces, wg, x, dx2, gn, scale, sums, tm=512):
    s = x.shape[0]
    np_ = len(pieces)
    na = len(sums)
    ni = s // tm
    groups, cur, width = [], [], 0
    for t, a in enumerate(pieces):
        cur.append(t)
        width += a.shape[1]
        if width == PW:
            groups.append(cur)
            cur, width = [], 0
    assert len(groups) == 4 and not cur

    def body(*refs):
        d_refs = refs[:np_]
        w_hbm, x_ref, dx2_ref, gn_ref, sc_ref = refs[np_:np_ + 5]
        q_refs = refs[np_ + 5:np_ + 5 + na]
        gx_ref, vec_ref = refs[np_ + 5 + na:np_ + 7 + na]
        r_refs = refs[np_ + 7 + na:np_ + 7 + 2 * na]
        w_s, wsem, ss, rs = refs[np_ + 7 + 2 * na:]
        i = pl.program_id(0)

        def scatter_copies():
            cx, cy, cc = _coords()
            j = 2 * cx + cy
            cps = []
            for t, (q, r) in enumerate(zip(q_refs, r_refs)):
                for e, (kx, ky) in enumerate(((1, 0), (0, 1), (1, 1))):
                    cps.append(_rcopy(q.at[j ^ (2 * kx + ky)], r.at[e], ss.at[3 * t + e], rs.at[3 * t + e],
                                      (_flip(cx, kx), _flip(cy, ky), cc)))
            return cps

        def w_copy(pc):
            return pltpu.make_async_copy(w_hbm.at[pc, pl.ds(0, D), :], w_s.at[pc], wsem.at[pc])

        @pl.when(i == 0)
        def _():
            for pc in range(4):
                w_copy(pc).start()
            vec_ref[...] = jnp.zeros_like(vec_ref)
            for cp in scatter_copies():
                cp.start()

        dh = None
        for pc, group in enumerate(groups):
            @pl.when(i == 0)
            def _(pc=pc):
                w_copy(pc).wait()

            tiles = [d_refs[t][...] for t in group]
            lhs = tiles[0] if len(tiles) == 1 else jnp.concatenate(tiles, axis=1)
            part = _dot_nt(lhs, w_s[pc])
            dh = part if dh is None else dh + part

        xt = x_ref[...]
        rstd = lax.rsqrt(jnp.mean(xt * xt, axis=-1, keepdims=True) + EPS)
        xh = xt * rstd
        gn_v = gn_ref[...]
        sc1 = 1.0 + sc_ref[...]
        dhx = dh * xh
        vec_ref[0:1, :] += _colsum(dh)
        vec_ref[1:2, :] += _colsum(dhx) * gn_v
        vec_ref[2:3, :] += _colsum(dhx) * sc1
        dxh = dh * (gn_v * sc1)
        gx_ref[...] = rstd * (dxh - xh * jnp.mean(dxh * xh, axis=-1, keepdims=True)) + dx2_ref[...]

        @pl.when(i == ni - 1)
        def _():
            for cp in scatter_copies():
                cp.wait()

    rowt = lambda i: (i, 0)
    row = pl.BlockSpec((1, D), lambda i: (0, 0))
    any_ = pl.BlockSpec(memory_space=pl.ANY)
    outs = pl.pallas_call(
        body, name="inproj_bwd_x", grid=(ni,),
        in_specs=[pl.BlockSpec((tm, a.shape[1]), rowt) for a in pieces] +
                 [any_, pl.BlockSpec((tm, D), rowt), pl.BlockSpec((tm, D), rowt), row, row] + [any_] * na,
        out_specs=(pl.BlockSpec((tm, D), rowt), pl.BlockSpec((8, D), lambda i: (0, 0))) + (any_,) * na,
        out_shape=(jax.ShapeDtypeStruct((s, D), F32), jax.ShapeDtypeStruct((8, D), F32)) +
                  tuple(jax.ShapeDtypeStruct((3,) + q.shape[1:], q.dtype) for q in sums),
        scratch_shapes=[pltpu.VMEM((4, D, PW), BF16), pltpu.SemaphoreType.DMA((4,)),
                        pltpu.SemaphoreType.DMA((3 * na,)), pltpu.SemaphoreType.DMA((3 * na,))],
        compiler_params=_params(("arbitrary",)),
    )(*pieces, wg, x, dx2, gn, scale, *sums)
    return outs[0], outs[1], outs[2:]


def _inproj_bwd_w(pieces, hbf, tk=1024):
    s = hbf.shape[0]
    steps, nk = _pieces_steps(pieces)
    npc = PW // D
    ns = s // tk
    np_ = len(pieces)

    def body(*refs):
        d_refs = refs[:np_]
        h_ref, g_ref = refs[np_:]
        cb, k = pl.program_id(0), pl.program_id(1)

        @pl.when(k == 0)
        def _():
            g_ref[...] = jnp.zeros_like(g_ref)

        for (s0, n), d_ref in zip(steps, d_refs):
            @pl.when((cb >= s0) & (cb < s0 + n))
            def _(d_ref=d_ref):
                g_ref[0] += _dot_tn(h_ref[...], d_ref[...])

    def piece_spec(s0, n):
        def imap(cb, k):
            active = (cb >= s0) & (cb < s0 + n)
            return (jnp.where(active, k, 0), jnp.clip(cb - s0, 0, n - 1))
        return pl.BlockSpec((tk, D), imap)

    return pl.pallas_call(
        body, name="inproj_bwd_w", grid=(nk, ns),
        in_specs=[piece_spec(s0, n) for s0, n in steps] + [pl.BlockSpec((tk, D), lambda cb, k: (k, 0))],
        out_specs=pl.BlockSpec((1, D, D), lambda cb, k: (cb // npc, 0, cb % npc)),
        out_shape=jax.ShapeDtypeStruct((4, D, PW), F32),
        compiler_params=_params(("parallel", "arbitrary")),
    )(*pieces, hbf)


D2D_CHUNK_BYTES = 512 * 1024


def _chunk_rows(a):
    return max(8, D2D_CHUNK_BYTES // (a.shape[-1] * a.dtype.itemsize))


def _pair_exchange(arrs):
    na = len(arrs)
    chunks = []
    for t, a in enumerate(arrs):
        hr = a.shape[1] // 2
        cr = _chunk_rows(a)
        chunks += [(t, j, r0, cr) for j in range(a.shape[0]) for r0 in range(0, hr, cr)]
    nch = len(chunks)

    def body(*refs):
        a_refs = refs[:na]
        rb_refs = refs[na:2 * na]
        ss, rs = refs[2 * na:]
        x, y, c = _coords()
        sib = (x, y, 1 - c)
        rcs = []
        for n, (t, j, r0, cr) in enumerate(chunks):
            hr = a_refs[t].shape[1] // 2
            rc = _rcopy(a_refs[t].at[j, pl.ds((1 - c) * hr + r0, cr), :], rb_refs[t].at[j, pl.ds(r0, cr), :],
                        ss.at[n], rs.at[n], sib)
            rc.start()
            rcs.append(rc)
        for rc in rcs:
            rc.wait_recv()
        for rc in rcs:
            rc.wait_send()

    any_ = pl.BlockSpec(memory_space=pl.ANY)
    halves = [jax.ShapeDtypeStruct((a.shape[0], a.shape[1] // 2, a.shape[2]), a.dtype) for a in arrs]
    return pl.pallas_call(
        body, name="pair_exchange",
        out_shape=tuple(halves),
        in_specs=[any_] * na, out_specs=tuple([any_] * na),
        scratch_shapes=[pltpu.SemaphoreType.DMA((nch,)), pltpu.SemaphoreType.DMA((nch,))],
        compiler_params=_params(),
    )(*arrs)


def _pair_swap(arrs):
    na = len(arrs)
    chunks = []
    for t, a in enumerate(arrs):
        cr = _chunk_rows(a)
        chunks += [(t, r0, cr) for r0 in range(0, a.shape[0], cr)]
    nch = len(chunks)

    def body(*refs):
        a_refs = refs[:na]
        o_refs = refs[na:2 * na]
        ss, rs = refs[2 * na:]
        x, y, c = _coords()
        sib = (x, y, 1 - c)
        rcs = []
        for n, (t, r0, cr) in enumerate(chunks):
            rows = pl.ds(r0, cr)
            rc = _rcopy(a_refs[t].at[rows, :], o_refs[t].at[rows, :], ss.at[n], rs.at[n], sib)
            rc.start()
            rcs.append(rc)
        for rc in rcs:
            rc.wait_recv()
        for rc in rcs:
            rc.wait_send()

    any_ = pl.BlockSpec(memory_space=pl.ANY)
    return pl.pallas_call(
        body, name="pair_swap",
        out_shape=tuple(jax.ShapeDtypeStruct(a.shape, a.dtype) for a in arrs),
        in_specs=[any_] * na, out_specs=tuple([any_] * na),
        scratch_shapes=[pltpu.SemaphoreType.DMA((nch,)), pltpu.SemaphoreType.DMA((nch,))],
        compiler_params=_params(),
    )(*arrs)


def _add_half(full, rb, core, tr):
    n, r, cdim = full.shape
    nb = r // 2 // tr

    def body(c_ref, a_ref, b_ref, o_ref):
        o_ref[...] = a_ref[...] + b_ref[...]

    mine = pl.BlockSpec((1, tr, cdim), lambda i, j, c_ref: (i, c_ref[0] * nb + j, 0))
    spec = pl.BlockSpec((1, tr, cdim), lambda i, j, c_ref: (i, j, 0))
    return pl.pallas_call(
        body, name="add_half",
        grid_spec=pltpu.PrefetchScalarGridSpec(num_scalar_prefetch=1, grid=(n, nb), in_specs=[mine, spec],
                                               out_specs=spec),
        out_shape=jax.ShapeDtypeStruct(rb.shape, rb.dtype),
        compiler_params=_params(("parallel", "parallel")),
    )(core, full, rb)


def _sum_slots(q, r3, shard, tr):
    _, r, cdim = q.shape

    def body(j_ref, q_ref, r_ref, o_ref):
        o_ref[...] = ((q_ref[0] + r_ref[0]) + r_ref[1]) + r_ref[2]

    return pl.pallas_call(
        body, name="sum_slots",
        grid_spec=pltpu.PrefetchScalarGridSpec(
            num_scalar_prefetch=1, grid=(r // tr,),
            in_specs=[pl.BlockSpec((1, tr, cdim), lambda i, j_ref: (j_ref[0], i, 0)),
                      pl.BlockSpec((3, tr, cdim), lambda i, j_ref: (0, i, 0))],
            out_specs=pl.BlockSpec((tr, cdim), lambda i, j_ref: (i, 0))),
        out_shape=jax.ShapeDtypeStruct((r, cdim), q.dtype),
        compiler_params=_params(("parallel",)),
    )(shard, q, r3)


def _allreduce_small(pack):
    def body(p_ref, out_ref, rbuf, s1, r1, s2, r2):
        me = _my_index()
        chunk = lambda d: pl.ds(pl.multiple_of(d * AR_CHUNK, 8), AR_CHUNK)
        sends = []
        for k in range(1, NDEV):
            cp = _rcopy(p_ref.at[chunk(me ^ k)], rbuf.at[me], s1.at[k - 1], r1.at[k - 1], _peer(k))
            cp.start()
            sends.append(cp)
        rbuf[me] = p_ref[chunk(me), :]
        for k in range(1, NDEV):
            _rcopy(p_ref.at[chunk(me)], rbuf.at[me ^ k], s1.at[k - 1], r1.at[k - 1], _peer(k)).wait_recv()
        tot = rbuf[0]
        for d in range(1, NDEV):
            tot = tot + rbuf[d]
        out_ref[chunk(me), :] = tot
        for k in range(1, NDEV):
            cp = _rcopy(out_ref.at[chunk(me)], out_ref.at[chunk(me)], s2.at[k - 1], r2.at[k - 1], _peer(k))
            cp.start()
            sends.append(cp)
        for k in range(1, NDEV):
            _rcopy(out_ref.at[chunk(me)], out_ref.at[chunk(me ^ k)], s2.at[k - 1], r2.at[k - 1], _peer(k)).wait_recv()
        for cp in sends:
            cp.wait_send()

    vm = pl.BlockSpec(memory_space=pltpu.VMEM)
    return pl.pallas_call(
        body, name="allreduce_small",
        out_shape=jax.ShapeDtypeStruct((AR_ROWS, D), F32),
        in_specs=[vm], out_specs=vm,
        scratch_shapes=[pltpu.VMEM((NDEV, AR_CHUNK, D), F32),
                        pltpu.SemaphoreType.DMA((7,)), pltpu.SemaphoreType.DMA((7,)),
                        pltpu.SemaphoreType.DMA((7,)), pltpu.SemaphoreType.DMA((7,))],
        compiler_params=_params(),
    )(pack)


def _adamw(w, g, m, v, tr):
    r, cdim = w.shape

    def body(w_ref, g_ref, m_ref, v_ref, d_ref, nm_ref, nv_ref):
        gv = g_ref[...]
        nm = B1 * m_ref[...] + (1.0 - B1) * gv
        nv = B2 * v_ref[...] + (1.0 - B2) * (gv * gv)
        m_hat = nm / (1.0 - B1 ** STEP)
        v_hat = nv / (1.0 - B2 ** STEP)
        d_ref[...] = -LR * (m_hat / (jnp.sqrt(v_hat) + ADAM_EPS) + WD * w_ref[...])
        nm_ref[...] = nm
        nv_ref[...] = nv

    spec = pl.BlockSpec((tr, cdim), lambda i: (i, 0))
    sd = jax.ShapeDtypeStruct((r, cdim), F32)
    return pl.pallas_call(
        body, name="adamw", grid=(r // tr,), in_specs=[spec] * 4, out_specs=(spec,) * 3, out_shape=(sd,) * 3,
        compiler_params=_params(("parallel",)),
    )(w, g, m, v)


def _rope_tables(positions):
    inv_freq = ROPE_THETA ** (-jnp.arange(0, ROT, 2, dtype=F32) / ROT)
    ang = positions.astype(F32)[:, None] * inv_freq
    cos, sin = jnp.cos(ang), jnp.sin(ang)
    n = positions.shape[0]
    half = ROT // 2
    rc = jnp.concatenate([cos, cos, jnp.ones((n, DH - ROT), F32)], axis=1)
    rsa = jnp.concatenate([-sin, jnp.zeros((n, DH - half), F32)], axis=1)
    rsb = jnp.concatenate([jnp.zeros((n, half), F32), sin, jnp.zeros((n, DH - ROT), F32)], axis=1)
    return rc, rsa, rsb


def kernel(x, c, positions, g_norm, w_mod, b_mod, w_in, b_gate, conv_w, conv_b, w_a, b_a, w_x, b_x, lam, w_out_rnn, w_out_attn, w_o, g_final, loss_target, m_g_norm, m_w_mod, m_b_mod, m_w_in, m_b_gate, m_conv_w, m_conv_b, m_w_a, m_b_a, m_w_x, m_b_x, m_lam, m_w_out_rnn, m_w_out_attn, m_w_o, m_g_final, v_g_norm, v_w_mod, v_b_mod, v_w_in, v_b_gate, v_conv_w, v_conv_b, v_w_a, v_b_a, v_w_x, v_b_x, v_lam, v_w_out_rnn, v_w_out_attn, v_w_o, v_g_final):
    s = x.shape[1]
    xi = lax.axis_index("x")
    yi = lax.axis_index("y")
    ci = lax.axis_index("c")
    shard = 2 * xi + yi
    x2d = x[0]
    tgt = loss_target[0]
    pos = positions[0]

    c_all, mod4 = _mod_fwd(c, w_mod[0], b_mod.reshape(4, 1, 768))
    mod = mod4.reshape(1, 3 * D)
    shift, scale, gate = mod[:, :D], mod[:, D:2 * D], mod[:, 2 * D:]
    w3_sh = jnp.stack([w_out_rnn[0], w_out_attn[0], w_o[0]]).astype(BF16)
    wsh = jnp.concatenate([w_in[0].astype(BF16), w3_sh.reshape(384, PW)], axis=0)
    conv_all = _gather_small(conv_w[0])
    conv_full = conv_all[0::2].transpose(1, 0, 2).reshape(4, D)

    order = jnp.stack([shard, shard ^ 2, shard ^ 1, shard ^ 3]).astype(jnp.int32)
    p, hbf, wg = _gather_norm_inproj(x2d, g_norm, shift, scale, wsh, order)
    wg = lax.dynamic_update_slice(wg, wsh[None], (shard, 0, 0))
    w3 = wg[:, D:, :].reshape(4, 3, 256, D).transpose(1, 0, 2, 3).reshape(3, D, D)
    rc, rsa, rsb = _rope_tables(pos)
    pos_col = pos.reshape(s, 1)
    b_a3, b_x3 = b_a.reshape(H, 1, DH), b_x.reshape(H, 1, DH)
    hr, gr = _rnn_fwd(p, pos_col, conv_full, conv_b, w_a[0], b_a3, w_x[0], b_x3, lam)
    o, lse, ga = _attn_fwd(p, rc, rsa, rsb)

    dgr, dga, dc, dx2, vec_t, g_out = _tail(gr, ga, p, x2d, tgt, w3, b_gate, gate, g_final.reshape(1, D))

    dxr, dzr, g_wa, g_ba, g_wx, g_bx, g_lam, g_cw, g_cb = _rnn_bwd(
        p, hr, dgr, pos_col, conv_full, conv_b, w_a[0], b_a3, w_x[0], b_x3, lam)
    dq, dk, dv, dza = _attn_bwd(p, o, lse, dga, rc, rsa, rsb)

    pieces = [dxr, dzr, dq, dk, dv, dza, dc]
    g_win = _inproj_bwd_w(pieces, hbf)

    core = ci.reshape(1)
    shard1 = shard.reshape(1)
    rb_a, rb_b = _pair_exchange([g_win, g_out])
    q_a, q_b = _add_half(g_win, rb_a, core, tr=256), _add_half(g_out, rb_b, core, tr=128)
    grad_x, vec_n, (r_a, r_b) = _inproj_bwd_x(pieces, wg, x2d, dx2, g_norm, scale, [q_a, q_b])
    f_a, f_b = _sum_slots(q_a, r_a, shard1, tr=256), _sum_slots(q_b, r_b, shard1, tr=128)
    s_a, s_b = _pair_swap([f_a, f_b])
    south = ci == 0
    grad_w_in = jnp.where(south, jnp.concatenate([f_a, s_a], axis=0), jnp.concatenate([s_a, f_a], axis=0))
    g3 = jnp.where(south, jnp.concatenate([f_b, s_b], axis=0), jnp.concatenate([s_b, f_b], axis=0)).reshape(3, 256, D)

    dmod_row = jnp.concatenate([vec_n[0:1], vec_n[1:2], vec_t[1:2]], axis=1)
    pack = jnp.concatenate([
        vec_n[2:3],
        dmod_row.reshape(3, D),
        vec_t[2:4],
        g_cb.reshape(1, D),
        g_wa.reshape(128, D),
        g_ba.reshape(1, D),
        g_wx.reshape(128, D),
        g_bx.reshape(1, D),
        g_lam.reshape(1, D),
        vec_t[0:1],
        g_cw.transpose(1, 0, 2).reshape(4, D),
        vec_t[4:5],
        jnp.zeros((AR_ROWS - 272, D), F32)], axis=0)
    red = _allreduce_small(pack)
    loss = red[271, 0]
    grad_w_mod = _mod_bwd(dmod_row.reshape(4, 1, 768), c_all)
    g_conv_sh = lax.dynamic_slice_in_dim(red[267:271], shard * 256, 256, axis=1)

    def small_pack(g_norm_, b_mod_, b_gate_, conv_b_, w_a_, b_a_, w_x_, b_x_, lam_, g_final_, conv_w_):
        return jnp.concatenate([
            g_norm_.reshape(1, D), b_mod_.reshape(3, D), b_gate_.reshape(2, D), conv_b_.reshape(1, D),
            w_a_.reshape(128, D), b_a_.reshape(1, D), w_x_.reshape(128, D), b_x_.reshape(1, D),
            lam_.reshape(1, D), g_final_.reshape(1, D), conv_w_.reshape(1, D),
            jnp.zeros((4, D), F32)], axis=0)

    wp = small_pack(g_norm, b_mod, b_gate, conv_b, w_a, b_a, w_x, b_x, lam, g_final, conv_w)
    mp = small_pack(m_g_norm, m_b_mod, m_b_gate, m_conv_b, m_w_a, m_b_a, m_w_x, m_b_x, m_lam, m_g_final, m_conv_w)
    vp = small_pack(v_g_norm, v_b_mod, v_b_gate, v_conv_b, v_w_a, v_b_a, v_w_x, v_b_x, v_lam, v_g_final, v_conv_w)
    gp = jnp.concatenate([red[0:267], g_conv_sh.reshape(1, D), jnp.zeros((4, D), F32)], axis=0)
    small = _adamw(wp, gp, mp, vp, tr=136)

    def unpack(a):
        return dict(
            g_norm=a[0:1], b_mod=a[1:4].reshape(1, 3 * D), b_gate=a[4:6].reshape(1, 2 * D), conv_b=a[6:7],
            w_a=a[7:135].reshape(1, H, DH, DH), b_a=a[135:136].reshape(1, H, DH),
            w_x=a[136:264].reshape(1, H, DH, DH), b_x=a[264:265].reshape(1, H, DH), lam=a[265:266],
            g_final=a[266].reshape(D), conv_w=a[267:268].reshape(1, 4, 256))

    big_in = _adamw(w_in[0], grad_w_in, m_w_in[0], v_w_in[0], tr=256)
    big_mod = _adamw(w_mod[0], grad_w_mod, m_w_mod[0], v_w_mod[0], tr=256)
    w3f = jnp.concatenate([w_out_rnn[0], w_out_attn[0], w_o[0]], axis=0)
    m3f = jnp.concatenate([m_w_out_rnn[0], m_w_out_attn[0], m_w_o[0]], axis=0)
    v3f = jnp.concatenate([v_w_out_rnn[0], v_w_out_attn[0], v_w_o[0]], axis=0)
    big_out = _adamw(w3f, g3.reshape(768, D), m3f, v3f, tr=256)

    names = ["g_norm", "w_mod", "b_mod", "w_in", "b_gate", "conv_w", "conv_b", "w_a", "b_a", "w_x", "b_x", "lam",
             "w_out_rnn", "w_out_attn", "w_o", "g_final"]
    grads = unpack(gp)
    grads.update(w_mod=grad_w_mod[None], w_in=grad_w_in[None],
                 w_out_rnn=g3[0][None], w_out_attn=g3[1][None], w_o=g3[2][None])
    outs = [grads]
    for idx in range(3):
        d = unpack(small[idx])
        d.update(w_mod=big_mod[idx][None], w_in=big_in[idx][None],
                 w_out_rnn=big_out[idx][0:256][None], w_out_attn=big_out[idx][256:512][None],
                 w_o=big_out[idx][512:768][None])
        outs.append(d)
    flat = [d[n] for d in outs for n in names]
    return (loss, grad_x[None], *flat)
```

```python
import jax
import jax.numpy as jnp
from jax import lax
from jax.experimental import pallas as pl
from jax.experimental.pallas import tpu as pltpu

F32, BF16 = jnp.float32, jnp.bfloat16
MESH = pl.DeviceIdType.MESH
HIGHEST = lax.Precision.HIGHEST

D = 1024
H = 8
DH = 128
PW = 2048
EPS = 1e-6
LRU_C = 8.0
SCALE = DH ** -0.5
NEG = -1e30
SPAN = 2048
UB = 128
DILATIONS = (1, 4, 16)
UNIT_BATCH = 8
UNIT_UNROLL = 8
ROPE_THETA = 500000.0
ROT = 32

LR, B1, B2, ADAM_EPS, WD, STEP = 0.001, 0.9, 0.999, 1e-08, 0.01, 10

WROWS = 1024 + 384
NDEV = 8
AR_ROWS = 320
AR_CHUNK = AR_ROWS // NDEV


def _params(sem=None, vmem_mb=56):
    return pltpu.CompilerParams(dimension_semantics=sem, vmem_limit_bytes=vmem_mb * 2 ** 20)


def _coords():
    return lax.axis_index("x"), lax.axis_index("y"), lax.axis_index("c")


def _flip(v, bit):
    return 1 - v if bit else v


def _peer(k):
    x, y, c = _coords()
    return (_flip(x, (k >> 2) & 1), _flip(y, (k >> 1) & 1), _flip(c, k & 1))


def _my_index():
    x, y, c = _coords()
    return 4 * x + 2 * y + c


def _rcopy(src, dst, ssem, rsem, dev):
    return pltpu.make_async_remote_copy(src_ref=src, dst_ref=dst, send_sem=ssem, recv_sem=rsem,
                                        device_id=dev, device_id_type=MESH)


def _sigmoid(x):
    return jax.nn.sigmoid(x)


def _dot(a, b):
    return jnp.dot(a, b, preferred_element_type=F32)


def _dot_nt(a, b):
    return lax.dot_general(a, b, (((1,), (1,)), ((), ())), preferred_element_type=F32)


def _dot_tn(a, b):
    return lax.dot_general(a, b, (((0,), (0,)), ((), ())), preferred_element_type=F32)


def _colsum(a):
    return jnp.sum(a, axis=0, keepdims=True)


def _mod_fwd(c, w_mod_sh, b_mod4, conv_sh):
    def body(c_ref, w_ref, b_ref, cv_ref, call_ref, mod_ref, cvall_ref, rows_ref, cmat_ref, s1, r1, s2, r2, s3, r3):
        x, y, _ = _coords()
        me = _my_index()
        j = 2 * x + y
        call_ref[me] = c_ref[...]
        cvall_ref[me] = cv_ref[...]
        sends = []
        for k in range(1, NDEV):
            cp = _rcopy(call_ref.at[me], call_ref.at[me], s1.at[k - 1], r1.at[k - 1], _peer(k))
            cp.start()
            sends.append(cp)
            cp = _rcopy(cvall_ref.at[me], cvall_ref.at[me], s3.at[k - 1], r3.at[k - 1], _peer(k))
            cp.start()
            sends.append(cp)
        for k in range(1, NDEV):
            pk = me ^ k
            _rcopy(call_ref.at[pk], call_ref.at[pk], s1.at[k - 1], r1.at[k - 1], _peer(k)).wait_recv()
        for b in range(NDEV):
            cmat_ref[pl.ds(b, 1), :] = call_ref[b]
        cm = cmat_ref[...]
        act = cm * _sigmoid(cm)
        mp = jnp.dot(act, w_ref[...], preferred_element_type=F32, precision=HIGHEST) + b_ref[j]
        for b in range(NDEV):
            rows_ref[b] = mp[b:b + 1]
        mod_ref[j] = rows_ref[me]
        for q, k in enumerate((2, 4, 6)):
            cp = _rcopy(rows_ref.at[me ^ k], mod_ref.at[j], s2.at[q], r2.at[q], _peer(k))
            cp.start()
            sends.append(cp)
        for q, k in enumerate((2, 4, 6)):
            jq = j ^ (k >> 1)
            _rcopy(rows_ref.at[me], mod_ref.at[jq], s2.at[q], r2.at[q], _peer(k)).wait_recv()
        for k in range(1, NDEV):
            pk = me ^ k
            _rcopy(cvall_ref.at[pk], cvall_ref.at[pk], s3.at[k - 1], r3.at[k - 1], _peer(k)).wait_recv()
        for cp in sends:
            cp.wait_send()

    vm = pl.BlockSpec(memory_space=pltpu.VMEM)
    return pl.pallas_call(
        body, name="mod_fwd",
        out_shape=(jax.ShapeDtypeStruct((NDEV, 1, D), F32), jax.ShapeDtypeStruct((4, 1, 768), F32),
                   jax.ShapeDtypeStruct((NDEV,) + conv_sh.shape, F32)),
        in_specs=[vm, vm, vm, vm], out_specs=(vm, vm, vm),
        scratch_shapes=[pltpu.VMEM((NDEV, 1, 768), F32), pltpu.VMEM((NDEV, D), F32),
                        pltpu.SemaphoreType.DMA((7,)), pltpu.SemaphoreType.DMA((7,)),
                        pltpu.SemaphoreType.DMA((3,)), pltpu.SemaphoreType.DMA((3,)),
                        pltpu.SemaphoreType.DMA((7,)), pltpu.SemaphoreType.DMA((7,))],
        compiler_params=_params(),
    )(c, w_mod_sh, b_mod4, conv_sh)


def _mod_bwd(dmod4, c_all):
    def body(d_ref, call_ref, gw_ref, dall_ref, cmat_ref, dmat_ref, s1, r1):
        x, y, _ = _coords()
        me = _my_index()
        j = 2 * x + y
        dall_ref[me] = d_ref[...]
        sends = []
        for k in range(1, NDEV):
            cp = _rcopy(dall_ref.at[me], dall_ref.at[me], s1.at[k - 1], r1.at[k - 1], _peer(k))
            cp.start()
            sends.append(cp)
        for k in range(1, NDEV):
            pk = me ^ k
            _rcopy(dall_ref.at[pk], dall_ref.at[pk], s1.at[k - 1], r1.at[k - 1], _peer(k)).wait_recv()
        for cp in sends:
            cp.wait_send()
        for b in range(NDEV):
            cmat_ref[pl.ds(b, 1), :] = call_ref[b]
            dmat_ref[pl.ds(b, 1), :] = dall_ref[b, j]
        cm = cmat_ref[...]
        act = cm * _sigmoid(cm)
        gw_ref[...] = lax.dot_general(act, dmat_ref[...], (((0,), (0,)), ((), ())),
                                      preferred_element_type=F32, precision=HIGHEST)

    vm = pl.BlockSpec(memory_space=pltpu.VMEM)
    return pl.pallas_call(
        body, name="mod_bwd",
        out_shape=jax.ShapeDtypeStruct((D, 768), F32),
        in_specs=[vm, vm], out_specs=vm,
        scratch_shapes=[pltpu.VMEM((NDEV, 4, 1, 768), F32), pltpu.VMEM((NDEV, D), F32), pltpu.VMEM((NDEV, 768), F32),
                        pltpu.SemaphoreType.DMA((7,)), pltpu.SemaphoreType.DMA((7,))],
        compiler_params=_params(),
    )(dmod4, c_all)


def _gather_norm_inproj(x, gn, shift, scale, wsh, order, tm=1024, tn=512):
    s = x.shape[0]
    ni = s // tm
    npc = PW // tn
    rows, cols = wsh.shape
    half = rows // 2
    nch = 4
    cr = half // nch
    chips = ((1, 0), (0, 1), (1, 1))

    def body(ord_ref, x_ref, gn_ref, sh_ref, sc_ref, wsh_hbm, p_ref, h_ref, wg_hbm, hs_all, w_s, wsem, ss, rs):
        slot, i, col = pl.program_id(0), pl.program_id(1), pl.program_id(2)
        cx, cy, cc = _coords()
        j = 2 * cx + cy
        sib = (cx, cy, 1 - cc)
        mine = lambda n: pl.ds(cc * half + n * cr, cr)
        theirs = lambda n: pl.ds((1 - cc) * half + n * cr, cr)
        shard_of = lambda q: j ^ (2 * chips[q][0] + chips[q][1])

        def to_chip(q, n):
            e = nch * q + n
            return _rcopy(wsh_hbm.at[mine(n)], wg_hbm.at[j, mine(n)], ss.at[e], rs.at[e],
                          (_flip(cx, chips[q][0]), _flip(cy, chips[q][1]), cc))

        def from_chip(q, n):
            e = nch * q + n
            return _rcopy(wsh_hbm.at[mine(n)], wg_hbm.at[shard_of(q), mine(n)], ss.at[e], rs.at[e], sib)

        def to_sibling(q, n):
            e = 3 * nch + nch * q + n
            return _rcopy(wg_hbm.at[shard_of(q), mine(n)], wg_hbm.at[shard_of(q), mine(n)], ss.at[e], rs.at[e], sib)

        def from_sibling(q, n):
            e = 3 * nch + nch * q + n
            return _rcopy(wsh_hbm.at[mine(n)], wg_hbm.at[shard_of(q), theirs(n)], ss.at[e], rs.at[e], sib)

        def load(sl, src):
            cp = pltpu.make_async_copy(src, w_s.at[sl], wsem.at[sl])
            cp.start()
            cp.wait()

        first = (i == 0) & (col == 0)

        @pl.when(first & (slot == 0))
        def _():
            for n in range(nch):
                for q in (0, 1):
                    to_chip(q, n).start()
            load(0, wsh_hbm.at[pl.ds(0, D), :])

        @pl.when(first & (slot == 1))
        def _():
            for q in (0, 1):
                for n in range(nch):
                    from_chip(q, n).wait_recv()
                    to_sibling(q, n).start()
            for n in range(nch):
                to_chip(2, n).start()
            for n in range(nch):
                from_sibling(0, n).wait_recv()
            load(1, wg_hbm.at[shard_of(0), pl.ds(0, D), :])

        @pl.when(first & (slot == 2))
        def _():
            for n in range(nch):
                from_sibling(1, n).wait_recv()
            load(2, wg_hbm.at[shard_of(1), pl.ds(0, D), :])

        @pl.when(first & (slot == 3))
        def _():
            for n in range(nch):
                from_chip(2, n).wait_recv()
                to_sibling(2, n).start()
            for n in range(nch):
                from_sibling(2, n).wait_recv()
            load(3, wg_hbm.at[shard_of(2), pl.ds(0, D), :])
            for q in range(3):
                for n in range(nch):
                    to_chip(q, n).wait_send()
                    to_sibling(q, n).wait_send()

        @pl.when((slot == 0) & (col == 0))
        def _():
            xt = x_ref[...]
            rstd = lax.rsqrt(jnp.mean(xt * xt, axis=-1, keepdims=True) + EPS)
            h = ((xt * rstd * gn_ref[...]) * (1.0 + sc_ref[...]) + sh_ref[...]).astype(BF16)
            hs_all[i] = h
            h_ref[...] = h

        p_ref[0] = _dot(hs_all[i], w_s[slot, :, pl.ds(pl.multiple_of(col * tn, tn), tn)]).astype(BF16)

    row = pl.BlockSpec((1, D), lambda sl, i, col, o: (0, 0))
    x_rows = lambda sl, i, col, o: (jnp.where(sl == 0, i, ni - 1), 0)
    any_ = pl.BlockSpec(memory_space=pl.ANY)
    return pl.pallas_call(
        body, name="gather_norm_inproj",
        grid_spec=pltpu.PrefetchScalarGridSpec(
            num_scalar_prefetch=1, grid=(4, ni, npc),
            in_specs=[pl.BlockSpec((tm, D), x_rows), row, row, row, any_],
            out_specs=(pl.BlockSpec((1, tm, tn), lambda sl, i, col, o: (o[sl], i, col)),
                       pl.BlockSpec((tm, D), x_rows), any_),
            scratch_shapes=[pltpu.VMEM((ni, tm, D), BF16), pltpu.VMEM((4, D, PW), BF16),
                            pltpu.SemaphoreType.DMA((4,)),
                            pltpu.SemaphoreType.DMA((6 * nch,)), pltpu.SemaphoreType.DMA((6 * nch,))]),
        out_shape=(jax.ShapeDtypeStruct((4, s, PW), BF16), jax.ShapeDtypeStruct((s, D), BF16),
                   jax.ShapeDtypeStruct((4, rows, cols), wsh.dtype)),
        compiler_params=_params(("arbitrary", "arbitrary", "arbitrary")),
    )(order, x, gn, shift, scale, wsh)


def _shift_down(prev8, cur, d):
    t = cur.shape[0]
    c3 = cur.reshape(t // 8, 8, DH)
    rot = pltpu.roll(c3, d, 1)
    before = jnp.concatenate([pltpu.roll(prev8, d, 0).reshape(1, 8, DH), rot[:-1]], axis=0)
    rows = lax.broadcasted_iota(jnp.int32, c3.shape, 1)
    return jnp.where(rows >= d, rot, before).reshape(t, DH)


def _shift_up(cur, next8, d):
    t = cur.shape[0]
    c3 = cur.reshape(t // 8, 8, DH)
    rot = pltpu.roll(c3, 8 - d, 1)
    after = jnp.concatenate([rot[1:], pltpu.roll(next8, 8 - d, 0).reshape(1, 8, DH)], axis=0)
    rows = lax.broadcasted_iota(jnp.int32, c3.shape, 1)
    return jnp.where(rows < 8 - d, rot, after).reshape(t, DH)


def _rnn_gates(xr, prev8, cw, cb, wa, ba, wx, bx, lam, reset):
    xc = cw[3:4] * xr + cb
    for d in (1, 2, 3):
        xc = xc + cw[3 - d:4 - d] * _shift_down(prev8, xr, d)
    xcb = xc.astype(BF16)
    r = _sigmoid(_dot(xcb, wa.astype(BF16)) + ba)
    ig = _sigmoid(_dot(xcb, wx.astype(BF16)) + bx)
    nl = -lam
    sp = jnp.maximum(nl, 0.0) + jnp.log1p(jnp.exp(-jnp.abs(nl)))
    log_a = (-LRU_C * r) * sp
    a = jnp.where(reset, 0.0, jnp.exp(log_a))
    mult = jnp.where(reset, 1.0, jnp.sqrt(1.0 - jnp.exp(2.0 * log_a)))
    return xc, r, ig, sp, a, mult


def _log_scan(a, b, axis, up):
    n = a.shape[axis]
    rows = lax.broadcasted_iota(jnp.int32, a.shape, axis)
    d = 1
    while d < n:
        m = rows < n - d if up else rows >= d
        shift = n - d if up else d
        a_s = pltpu.roll(a, shift, axis)
        b_s = pltpu.roll(b, shift, axis)
        b = jnp.where(m, a * b_s + b, b)
        a = jnp.where(m, a * a_s, a)
        d *= 2
    return a, b


def _scan(a, b, t, edge, up=False):
    g = t // 8
    a3, b3 = _log_scan(a.reshape(g, 8, DH), b.reshape(g, 8, DH), 1, up)
    last = 0 if up else 7
    ag, bg = _log_scan(a3[:, last, :], b3[:, last, :], 0, up)
    hg = ag * edge + bg
    grp = lax.broadcasted_iota(jnp.int32, hg.shape, 0)
    if up:
        cin = jnp.where(grp == g - 1, edge, pltpu.roll(hg, g - 1, 0))
        tail = hg[0:1]
    else:
        cin = jnp.where(grp == 0, edge, pltpu.roll(hg, 1, 0))
        tail = hg[g - 1:g]
    return (a3 * cin[:, None, :] + b3).reshape(t, DH), tail


def _rnn_fwd(p, pos, conv_w, conv_b, w_a, b_a, w_x, b_x, lam, tt=512):
    s = p.shape[1]
    nt = s // tt

    def body(xr_ref, z_ref, pos_ref, cw_ref, cb_ref, wa_ref, ba_ref, wx_ref, bx_ref, lam_ref,
             hr_ref, gr_ref, xprev, hprev):
        @pl.when(pl.program_id(1) == 0)
        def _():
            xprev[...] = jnp.zeros_like(xprev)
            hprev[...] = jnp.zeros_like(hprev)

        xr = xr_ref[0].astype(F32)
        z = z_ref[0].astype(F32)
        reset = pos_ref[...] == 0
        xc, r, ig, sp, a, mult = _rnn_gates(xr, xprev[...], cw_ref[...], cb_ref[...], wa_ref[0], ba_ref[0],
                                            wx_ref[0], bx_ref[0], lam_ref[...], reset)
        bx = mult * ig * xc
        h, h_last = _scan(a, bx, tt, hprev[0:1])
        xprev[...] = xr[tt - 8:]
        hprev[...] = jnp.broadcast_to(h_last, (8, DH))
        hr_ref[...] = h
        gr_ref[...] = (h * (z * _sigmoid(z))).astype(BF16)

    head_row = lambda hh, t: (0, hh)
    return pl.pallas_call(
        body, name="rnn_fwd", grid=(H, nt),
        in_specs=[pl.BlockSpec((1, tt, DH), lambda hh, t: (0, t, hh)),
                  pl.BlockSpec((1, tt, DH), lambda hh, t: (0, t, H + hh)),
                  pl.BlockSpec((tt, 1), lambda hh, t: (t, 0)),
                  pl.BlockSpec((4, DH), head_row), pl.BlockSpec((1, DH), head_row),
                  pl.BlockSpec((1, DH, DH), lambda hh, t: (hh, 0, 0)), pl.BlockSpec((1, 1, DH), lambda hh, t: (hh, 0, 0)),
                  pl.BlockSpec((1, DH, DH), lambda hh, t: (hh, 0, 0)), pl.BlockSpec((1, 1, DH), lambda hh, t: (hh, 0, 0)),
                  pl.BlockSpec((1, DH), head_row)],
        out_specs=(pl.BlockSpec((tt, DH), lambda hh, t: (t, hh)), pl.BlockSpec((tt, DH), lambda hh, t: (t, hh))),
        out_shape=(jax.ShapeDtypeStruct((s, D), F32), jax.ShapeDtypeStruct((s, D), BF16)),
        scratch_shapes=[pltpu.VMEM((8, DH), F32), pltpu.VMEM((8, DH), F32)],
        compiler_params=_params(("parallel", "arbitrary")),
    )(p, p, pos, conv_w, conv_b, w_a, b_a, w_x, b_x, lam)


def _rnn_bwd(p, hr, dgr, pos, conv_w, conv_b, w_a, b_a, w_x, b_x, lam, tt=512):
    s = p.shape[1]
    nt = s // tt
    t8 = tt // 8

    def body(xr_ref, z_ref, xp_ref, hr_ref, hp_ref, dg_ref, pos_ref, cw_ref, cb_ref, wa_ref, ba_ref, wx_ref, bx_ref,
             lam_ref, dxr_ref, dz_ref, gwa_ref, gba_ref, gwx_ref, gbx_ref, glam_ref, gcw_ref, gcb_ref,
             a_next, g_next, dxc_next):
        t = pl.program_id(1)
        has_prev = t < nt - 1

        @pl.when(t == 0)
        def _():
            a_next[...] = jnp.zeros_like(a_next)
            g_next[...] = jnp.zeros_like(g_next)
            dxc_next[...] = jnp.zeros_like(dxc_next)
            gwa_ref[...] = jnp.zeros_like(gwa_ref)
            gba_ref[...] = jnp.zeros_like(gba_ref)
            gwx_ref[...] = jnp.zeros_like(gwx_ref)
            gbx_ref[...] = jnp.zeros_like(gbx_ref)
            glam_ref[...] = jnp.zeros_like(glam_ref)
            gcw_ref[...] = jnp.zeros_like(gcw_ref)
            gcb_ref[...] = jnp.zeros_like(gcb_ref)

        xr = xr_ref[0].astype(F32)
        z = z_ref[0].astype(F32)
        hr_blk = hr_ref[...]
        dg = dg_ref[...]
        xprev = jnp.where(has_prev, xp_ref[0].astype(F32)[8:], 0.0)
        hprev8 = jnp.where(has_prev, hp_ref[...], 0.0)
        reset = pos_ref[...] == 0
        cw = cw_ref[...]
        wa = wa_ref[0]
        wx = wx_ref[0]
        lam_v = lam_ref[...]
        xc, r, ig, sp, a, mult = _rnn_gates(xr, xprev, cw, cb_ref[...], wa, ba_ref[0], wx, bx_ref[0], lam_v, reset)

        sz = _sigmoid(z)
        dh = dg * (z * sz)
        dz_ref[...] = (dg * hr_blk * (sz * (1.0 + z * (1.0 - sz)))).astype(BF16)

        an = _shift_up(a, a_next[...], 1)
        g, g_first = _scan(an, dh, tt, g_next[0:1], up=True)
        a_next[...] = jnp.broadcast_to(a[0:1], (8, DH))
        g_next[...] = jnp.broadcast_to(g_first, (8, DH))

        hm1 = _shift_down(hprev8, hr_blk, 1)
        da = g * hm1
        dmult = g * (ig * xc)
        di = g * (mult * xc)
        dxc = g * (mult * ig)
        dla = jnp.where(reset, 0.0, da * a - dmult * (a * a) / mult)
        dr = dla * (-LRU_C * sp)
        dsp = _colsum(dla * (-LRU_C * r))
        glam_ref[0] += dsp * (-_sigmoid(-lam_v))
        dpa = dr * r * (1.0 - r)
        dpx = di * ig * (1.0 - ig)
        dpab = dpa.astype(BF16)
        dpxb = dpx.astype(BF16)
        dxc = dxc + _dot_nt(dpab, wa.astype(BF16)) + _dot_nt(dpxb, wx.astype(BF16))
        xcb = xc.astype(BF16)
        gwa_ref[0] += _dot_tn(xcb, dpab)
        gwx_ref[0] += _dot_tn(xcb, dpxb)
        gba_ref[0] += _colsum(dpa)
        gbx_ref[0] += _colsum(dpx)

        dxr = cw[3:4] * dxc
        for d in (1, 2, 3):
            dxr = dxr + cw[3 - d:4 - d] * _shift_up(dxc, dxc_next[...], d)
        dxr_ref[...] = dxr.astype(BF16)
        dxc_next[...] = dxc[0:8]
        gcb_ref[0] += _colsum(dxc)
        gcw_ref[0, 3:4, :] += _colsum(xr * dxc)
        for d in (1, 2, 3):
            gcw_ref[0, 3 - d:4 - d, :] += _colsum(_shift_down(xprev, xr, d) * dxc)

    rt = lambda t: nt - 1 - t
    prev8 = lambda t: jnp.maximum(rt(t) * t8 - 1, 0)
    head_row = lambda hh, t: (0, hh)
    hsm = lambda hh, t: (hh, 0, 0)
    return pl.pallas_call(
        body, name="rnn_bwd", grid=(H, nt),
        in_specs=[pl.BlockSpec((1, tt, DH), lambda hh, t: (0, rt(t), hh)),
                  pl.BlockSpec((1, tt, DH), lambda hh, t: (0, rt(t), H + hh)),
                  pl.BlockSpec((1, 16, DH), lambda hh, t: (0, jnp.maximum(rt(t) * (tt // 16) - 1, 0), hh)),
                  pl.BlockSpec((tt, DH), lambda hh, t: (rt(t), hh)),
                  pl.BlockSpec((8, DH), lambda hh, t: (prev8(t), hh)),
                  pl.BlockSpec((tt, DH), lambda hh, t: (rt(t), hh)),
                  pl.BlockSpec((tt, 1), lambda hh, t: (rt(t), 0)),
                  pl.BlockSpec((4, DH), head_row), pl.BlockSpec((1, DH), head_row),
                  pl.BlockSpec((1, DH, DH), hsm), pl.BlockSpec((1, 1, DH), hsm),
                  pl.BlockSpec((1, DH, DH), hsm), pl.BlockSpec((1, 1, DH), hsm),
                  pl.BlockSpec((1, DH), head_row)],
        out_specs=(pl.BlockSpec((tt, DH), lambda hh, t: (rt(t), hh)), pl.BlockSpec((tt, DH), lambda hh, t: (rt(t), hh)),
                   pl.BlockSpec((1, DH, DH), hsm), pl.BlockSpec((1, 1, DH), hsm),
                   pl.BlockSpec((1, DH, DH), hsm), pl.BlockSpec((1, 1, DH), hsm),
                   pl.BlockSpec((1, 1, DH), hsm), pl.BlockSpec((1, 4, DH), hsm), pl.BlockSpec((1, 1, DH), hsm)),
        out_shape=(jax.ShapeDtypeStruct((s, D), BF16), jax.ShapeDtypeStruct((s, D), BF16),
                   jax.ShapeDtypeStruct((H, DH, DH), F32), jax.ShapeDtypeStruct((H, 1, DH), F32),
                   jax.ShapeDtypeStruct((H, DH, DH), F32), jax.ShapeDtypeStruct((H, 1, DH), F32),
                   jax.ShapeDtypeStruct((H, 1, DH), F32), jax.ShapeDtypeStruct((H, 4, DH), F32),
                   jax.ShapeDtypeStruct((H, 1, DH), F32)),
        scratch_shapes=[pltpu.VMEM((8, DH), F32), pltpu.VMEM((8, DH), F32), pltpu.VMEM((8, DH), F32)],
        compiler_params=_params(("parallel", "arbitrary")),
    )(p, p, p, hr, hr, dgr, pos, conv_w, conv_b, w_a, b_a, w_x, b_x, lam)


def _rope(t, c, sa, sb):
    return t * c + pltpu.roll(t, DH - ROT // 2, 1) * sa + pltpu.roll(t, ROT // 2, 1) * sb


def _rope_bwd(g, c, sa, sb):
    return g * c + pltpu.roll(g * sa, ROT // 2, 1) + pltpu.roll(g * sb, DH - ROT // 2, 1)


def _unit_bases(gi, u):
    dil = DILATIONS[gi]
    if dil == 1:
        return u * UB, SPAN + (u - 1) * UB, u == 0
    if dil == 4:
        blk, r = u // 4, u % 4
        return blk * 4 * UB + r, SPAN + (blk - 1) * 4 * UB + r, blk == 0
    return u, u, True


def _unit_slices(gi, u):
    dil = DILATIONS[gi]
    qb0, kb0, first = _unit_bases(gi, u)
    if dil == 1:
        return pl.ds(pl.multiple_of(qb0, UB), UB), pl.ds(pl.multiple_of(kb0, UB), 2 * UB), first
    return pl.ds(qb0, UB, stride=dil), pl.ds(kb0, 2 * UB, stride=dil), first


def _bdot(a, b):
    return lax.dot_general(a, b, (((2,), (1,)), ((0,), (0,))), preferred_element_type=F32)


def _bdot_nt(a, b):
    return lax.dot_general(a, b, (((2,), (2,)), ((0,), (0,))), preferred_element_type=F32)


def _bdot_tn(a, b):
    return lax.dot_general(a, b, (((1,), (1,)), ((0,), (0,))), preferred_element_type=F32)


def _band_mask(first_in_span, has_prev):
    qi = lax.broadcasted_iota(jnp.int32, (UB, 2 * UB), 0)
    ki = lax.broadcasted_iota(jnp.int32, (UB, 2 * UB), 1)
    dist = UB + qi - ki
    band = (dist >= 0) & (dist <= UB)
    return band & ((ki >= UB) | jnp.logical_not(first_in_span) | has_prev)


def _attn_fwd(p, rc, rsa, rsb):
    s = p.shape[1]
    ns = s // SPAN
    nunit = SPAN // UB

    def body(q_ref, k_ref, v_ref, z_ref, c_ref, sa_ref, sb_ref, o_ref, lse_ref, ga_ref,
             qr, kf, vf, acc, mm, ll):
        n = pl.program_id(1)

        @pl.when(n == 0)
        def _():
            kf[0:SPAN] = jnp.zeros((SPAN, DH), F32)
            vf[0:SPAN] = jnp.zeros((SPAN, DH), F32)

        c, sa, sb = c_ref[...], sa_ref[...], sb_ref[...]
        qr[...] = _rope(q_ref[0].astype(F32), c, sa, sb)
        kf[SPAN:] = _rope(k_ref[0].astype(F32), c, sa, sb)
        vf[SPAN:] = v_ref[0].astype(F32)
        has_prev = n > 0

        for gi, dil in enumerate(DILATIONS):
            def trip(t, carry, gi=gi, dil=dil):
                qsls, ksls, firsts = [], [], []
                for b in range(UNIT_BATCH):
                    qsl, ksl, first = _unit_slices(gi, t * UNIT_BATCH + b)
                    qsls.append(qsl)
                    ksls.append(ksl)
                    firsts.append(first)
                qb = jnp.stack([qr[qsl, :].astype(BF16) for qsl in qsls])
                kb = jnp.stack([kf[ksl, :].astype(BF16) for ksl in ksls])
                vb = jnp.stack([vf[ksl, :].astype(BF16) for ksl in ksls])
                s_all = _bdot_nt(qb, kb)
                prs = []
                for b in range(UNIT_BATCH):
                    sc = jnp.where(_band_mask(firsts[b], has_prev), s_all[b] * SCALE, NEG)
                    m = jnp.max(sc, axis=-1, keepdims=True)
                    pr = jnp.exp(sc - m)
                    l = jnp.sum(pr, axis=-1, keepdims=True)
                    mm[gi, qsls[b], :] = jnp.broadcast_to(m, (UB, DH))
                    ll[gi, qsls[b], :] = jnp.broadcast_to(l, (UB, DH))
                    prs.append(pr.astype(BF16))
                o_all = _bdot(jnp.stack(prs), vb)
                for b in range(UNIT_BATCH):
                    acc[gi, qsls[b], :] = o_all[b]
                return carry

            lax.fori_loop(0, nunit // UNIT_BATCH, trip, 0)

        m_all =jnp.maximum(jnp.maximum(mm[0], mm[1]), mm[2])
        num = jnp.zeros((SPAN, DH), F32)
        den = jnp.zeros((SPAN, DH), F32)
        for gi in range(3):
            w = jnp.exp(mm[gi] - m_all)
            num = num + w * acc[gi]
            den = den + w * ll[gi]
        o = num / den
        o_ref[...] = o
        lse_ref[...] = m_all + jnp.log(den)
        z = z_ref[0].astype(F32)
        ga_ref[...] = (o * (z * _sigmoid(z))).astype(BF16)
        kf[0:SPAN] = kf[SPAN:]
        vf[0:SPAN] = vf[SPAN:]

    blk = lambda piece, off: pl.BlockSpec((1, SPAN, DH), lambda hh, n: (piece, n, off + hh))
    tab = pl.BlockSpec((SPAN, DH), lambda hh, n: (n, 0))
    outb = pl.BlockSpec((SPAN, DH), lambda hh, n: (n, hh))
    return pl.pallas_call(
        body, name="attn_fwd", grid=(H, ns),
        in_specs=[blk(1, 0), blk(1, H), blk(2, 0), blk(2, H), tab, tab, tab],
        out_specs=(outb, outb, outb),
        out_shape=(jax.ShapeDtypeStruct((s, D), F32), jax.ShapeDtypeStruct((s, D), F32),
                   jax.ShapeDtypeStruct((s, D), BF16)),
        scratch_shapes=[pltpu.VMEM((SPAN, DH), F32), pltpu.VMEM((2 * SPAN, DH), F32), pltpu.VMEM((2 * SPAN, DH), F32),
                        pltpu.VMEM((3, SPAN, DH), F32), pltpu.VMEM((3, SPAN, DH), F32), pltpu.VMEM((3, SPAN, DH), F32)],
        compiler_params=_params(("parallel", "arbitrary")),
    )(p, p, p, p, rc, rsa, rsb)


def _attn_bwd(p, o, lse, dga, rc, rsa, rsb):
    s = p.shape[1]
    ns = s // SPAN
    nunit = SPAN // UB

    def body(q_ref, k_ref, kp_ref, v_ref, vp_ref, z_ref, c_ref, sa_ref, sb_ref, cp_ref, sap_ref, sbp_ref,
             o_ref, lse_ref, dg_ref, dq_ref, dk_ref, dv_ref, dz_ref,
             qr, kf, vf, dof, dlt, dqa, dkf, dvf):
        step = pl.program_id(1)
        n = ns - 1 - step
        has_prev = n > 0

        @pl.when(step == 0)
        def _():
            dkf[...] = jnp.zeros_like(dkf)
            dvf[...] = jnp.zeros_like(dvf)

        @pl.when(step > 0)
        def _():
            dkf[SPAN:] = dkf[0:SPAN]
            dvf[SPAN:] = dvf[0:SPAN]
            dkf[0:SPAN] = jnp.zeros((SPAN, DH), F32)
            dvf[0:SPAN] = jnp.zeros((SPAN, DH), F32)

        c, sa, sb = c_ref[...], sa_ref[...], sb_ref[...]
        qr[...] = _rope(q_ref[0].astype(F32), c, sa, sb)
        kf[SPAN:] = _rope(k_ref[0].astype(F32), c, sa, sb)
        vf[SPAN:] = v_ref[0].astype(F32)
        kf[0:SPAN] = jnp.where(has_prev, _rope(kp_ref[0].astype(F32), cp_ref[...], sap_ref[...], sbp_ref[...]), 0.0)
        vf[0:SPAN] = jnp.where(has_prev, vp_ref[0].astype(F32), 0.0)
        z = z_ref[0].astype(F32)
        sz = _sigmoid(z)
        dg = dg_ref[...]
        ov = o_ref[...]
        do = dg * (z * sz)
        dz_ref[...] = (dg * ov * (sz * (1.0 + z * (1.0 - sz)))).astype(BF16)
        dof[...] = do
        dlt[...] = jnp.broadcast_to(jnp.sum(do * ov, axis=-1, keepdims=True), (SPAN, DH))
        dqa[...] = jnp.zeros_like(dqa)

        for gi, dil in enumerate(DILATIONS):
            def trip(t, carry, gi=gi, dil=dil):
                qsls, ksls, firsts = [], [], []
                for b in range(UNIT_BATCH):
                    qsl, ksl, first = _unit_slices(gi, t * UNIT_BATCH + b)
                    qsls.append(qsl)
                    ksls.append(ksl)
                    firsts.append(first)
                qb = jnp.stack([qr[qsl, :].astype(BF16) for qsl in qsls])
                kb = jnp.stack([kf[ksl, :].astype(BF16) for ksl in ksls])
                vb = jnp.stack([vf[ksl, :].astype(BF16) for ksl in ksls])
                dob = jnp.stack([dof[qsl, :].astype(BF16) for qsl in qsls])
                s_all = _bdot_nt(qb, kb)
                dp_all = _bdot_nt(dob, vb)
                prs, dss = [], []
                for b in range(UNIT_BATCH):
                    lse_b = lse_ref[qsls[b], :]
                    dl_b = dlt[qsls[b], :]
                    pr = jnp.exp(s_all[b] * SCALE - jnp.concatenate([lse_b, lse_b], axis=1))
                    pr = jnp.where(_band_mask(firsts[b], has_prev), pr, 0.0)
                    prs.append(pr.astype(BF16))
                    dss.append((pr * (dp_all[b] - jnp.concatenate([dl_b, dl_b], axis=1)) * SCALE).astype(BF16))
                ds_all = jnp.stack(dss)
                dv_all = _bdot_tn(jnp.stack(prs), dob)
                dq_all = _bdot(ds_all, kb)
                dk_all = _bdot_tn(ds_all, qb)
                for b in range(UNIT_BATCH):
                    dvf[ksls[b], :] += dv_all[b]
                    dqa[qsls[b], :] += dq_all[b]
                    dkf[ksls[b], :] += dk_all[b]
                return carry

            lax.fori_loop(0, nunit // UNIT_BATCH, trip, 0)

        dq_ref[...] = _rope_bwd(dqa[...], c, sa, sb).astype(BF16)
        dk_ref[...] = _rope_bwd(dkf[SPAN:], c, sa, sb).astype(BF16)
        dv_ref[...] = dvf[SPAN:].astype(BF16)

    rn = lambda n: ns - 1 - n
    pn = lambda n: jnp.maximum(ns - 2 - n, 0)
    blk = lambda piece, off: pl.BlockSpec((1, SPAN, DH), lambda hh, n: (piece, rn(n), off + hh))
    blkp = lambda piece, off: pl.BlockSpec((1, SPAN, DH), lambda hh, n: (piece, pn(n), off + hh))
    tab = pl.BlockSpec((SPAN, DH), lambda hh, n: (rn(n), 0))
    tabp = pl.BlockSpec((SPAN, DH), lambda hh, n: (pn(n), 0))
    io = pl.BlockSpec((SPAN, DH), lambda hh, n: (rn(n), hh))
    return pl.pallas_call(
        body, name="attn_bwd", grid=(H, ns),
        in_specs=[blk(1, 0), blk(1, H), blkp(1, H), blk(2, 0), blkp(2, 0), blk(2, H),
                  tab, tab, tab, tabp, tabp, tabp, io, io, io],
        out_specs=(io, io, io, io),
        out_shape=tuple(jax.ShapeDtypeStruct((s, D), BF16) for _ in range(4)),
        scratch_shapes=[pltpu.VMEM((SPAN, DH), F32), pltpu.VMEM((2 * SPAN, DH), F32), pltpu.VMEM((2 * SPAN, DH), F32),
                        pltpu.VMEM((SPAN, DH), F32), pltpu.VMEM((SPAN, DH), F32), pltpu.VMEM((SPAN, DH), F32),
                        pltpu.VMEM((2 * SPAN, DH), F32), pltpu.VMEM((2 * SPAN, DH), F32)],
        compiler_params=_params(("parallel", "arbitrary")),
    )(p, p, p, p, p, p, rc, rsa, rsb, rc, rsa, rsb, o, lse, dga)


def _tail(gr, ga, p, x, tgt, w3, b_gate, gate, g_final, tm=256):
    s = x.shape[0]
    nt = s // tm

    def body(gr_ref, ga_ref, pr_ref, pa_ref, x_ref, t_ref, bg_ref, gate_ref, gf_ref, w_hbm,
             dgr_ref, dga_ref, dc_ref, dx2_ref, vec_ref, go_hbm, w_s, acc_s, sem):
        i = pl.program_id(0)

        @pl.when(i == 0)
        def _():
            cp = pltpu.make_async_copy(w_hbm, w_s, sem.at[12])
            cp.start()
            acc_s[...] = jnp.zeros_like(acc_s)
            vec_ref[...] = jnp.zeros_like(vec_ref)
            cp.wait()

        grb = gr_ref[...]
        gab = ga_ref[...]
        bg = bg_ref[...]
        gate_v = gate_ref[...]
        gf = gf_ref[...]
        y_r = _dot(grb, w_s[0])
        y_a = _dot(gab, w_s[1])
        sr = _sigmoid(pr_ref[0].astype(F32) + bg[:, :D])
        sa = _sigmoid(pa_ref[0].astype(F32) + bg[:, D:])
        mb = (sr * y_r + sa * y_a).astype(BF16)
        u = _dot(mb, w_s[2])
        x2 = x_ref[...] + gate_v * u
        rstd = lax.rsqrt(jnp.mean(x2 * x2, axis=-1, keepdims=True) + EPS)
        xh = x2 * rstd
        e = xh * gf - t_ref[...]
        dy = e * (1.0 / D)
        dyg = dy * gf
        dx2 = rstd * (dyg - xh * jnp.mean(dyg * xh, axis=-1, keepdims=True))
        dx2_ref[...] = dx2
        dub = (dx2 * gate_v).astype(BF16)
        dm = _dot_nt(dub, w_s[2])
        dyr = (dm * sr).astype(BF16)
        dya = (dm * sa).astype(BF16)
        dpr = dm * y_r * (sr * (1.0 - sr))
        dpa = dm * y_a * (sa * (1.0 - sa))
        dc_ref[:, :D] = dpr.astype(BF16)
        dc_ref[:, D:] = dpa.astype(BF16)
        dgr_ref[...] = _dot_nt(dyr, w_s[0])
        dga_ref[...] = _dot_nt(dya, w_s[1])
        acc_s[0] += _dot_tn(grb, dyr)
        acc_s[1] += _dot_tn(gab, dya)
        acc_s[2] += _dot_tn(mb, dub)
        vec_ref[0:1, :] += _colsum(dy * xh)
        vec_ref[1:2, :] += _colsum(dx2 * u)
        vec_ref[2:3, :] += _colsum(dpr)
        vec_ref[3:4, :] += _colsum(dpa)
        vec_ref[4:5, :] += _colsum(e * e)

        @pl.when(i == nt - 1)
        def _():
            vec_ref[4:5, :] = jnp.broadcast_to(jnp.sum(vec_ref[4:5, :]) * (0.5 / D), (1, D))
            cps = []
            for w in range(3):
                for j in range(4):
                    cps.append(pltpu.make_async_copy(acc_s.at[w, pl.ds(256 * j, 256)],
                                                     go_hbm.at[j, pl.ds(256 * w, 256)], sem.at[4 * w + j]))
            for cp in cps:
                cp.start()
            for cp in cps:
                cp.wait()

    rowt = lambda i: (i, 0)
    row = lambda w: pl.BlockSpec((1, w), lambda i: (0, 0))
    any_ = pl.BlockSpec(memory_space=pl.ANY)
    return pl.pallas_call(
        body, name="tail", grid=(nt,),
        in_specs=[pl.BlockSpec((tm, D), rowt), pl.BlockSpec((tm, D), rowt),
                  pl.BlockSpec((1, tm, D), lambda i: (3, i, 0)), pl.BlockSpec((1, tm, D), lambda i: (3, i, 1)),
                  pl.BlockSpec((tm, D), rowt), pl.BlockSpec((tm, D), rowt),
                  row(2 * D), row(D), row(D), any_],
        out_specs=(pl.BlockSpec((tm, D), rowt), pl.BlockSpec((tm, D), rowt), pl.BlockSpec((tm, 2 * D), rowt),
                   pl.BlockSpec((tm, D), rowt), pl.BlockSpec((8, D), lambda i: (0, 0)), any_),
        out_shape=(jax.ShapeDtypeStruct((s, D), F32), jax.ShapeDtypeStruct((s, D), F32),
                   jax.ShapeDtypeStruct((s, 2 * D), BF16), jax.ShapeDtypeStruct((s, D), F32),
                   jax.ShapeDtypeStruct((8, D), F32), jax.ShapeDtypeStruct((4, 768, D), F32)),
        scratch_shapes=[pltpu.VMEM((3, D, D), BF16), pltpu.VMEM((3, D, D), F32), pltpu.SemaphoreType.DMA((13,))],
        compiler_params=_params(("arbitrary",)),
    )(gr, ga, p, p, x, tgt, b_gate, gate, g_final, w3)


def _pieces_steps(pieces):
    out, s0 = [], 0
    for a in pieces:
        n = a.shape[1] // D
        out.append((s0, n))
        s0 += n
    return out, s0


def _inproj_bwd_x(pieces, wg, x, dx2, gn, scale, sums, tm=512):
    s = x.shape[0]
    np_ = len(pieces)
    na = len(sums)
    ni = s // tm
    groups, cur, width = [], [], 0
    for t, a in enumerate(pieces):
        cur.append(t)
        width += a.shape[1]
        if width == PW:
            groups.append(cur)
            cur, width = [], 0
    assert len(groups) == 4 and not cur

    def body(*refs):
        d_refs = refs[:np_]
        w_hbm, x_ref, dx2_ref, gn_ref, sc_ref = refs[np_:np_ + 5]
        q_refs = refs[np_ + 5:np_ + 5 + na]
        gx_ref, vec_ref = refs[np_ + 5 + na:np_ + 7 + na]
        r_refs = refs[np_ + 7 + na:np_ + 7 + 2 * na]
        w_s, wsem, ss, rs = refs[np_ + 7 + 2 * na:]
        i = pl.program_id(0)

        def scatter_copies():
            cx, cy, cc = _coords()
            j = 2 * cx + cy
            cps = []
            for t, (q, r) in enumerate(zip(q_refs, r_refs)):
                for e, (kx, ky) in enumerate(((1, 0), (0, 1), (1, 1))):
                    cps.append(_rcopy(q.at[j ^ (2 * kx + ky)], r.at[e], ss.at[3 * t + e], rs.at[3 * t + e],
                                      (_flip(cx, kx), _flip(cy, ky), cc)))
            return cps

        def w_copy(pc):
            return pltpu.make_async_copy(w_hbm.at[pc, pl.ds(0, D), :], w_s.at[pc], wsem.at[pc])

        @pl.when(i == 0)
        def _():
            for pc in range(4):
                w_copy(pc).start()
            vec_ref[...] = jnp.zeros_like(vec_ref)
            for cp in scatter_copies():
                cp.start()

        dh = None
        for pc, group in enumerate(groups):
            @pl.when(i == 0)
            def _(pc=pc):
                w_copy(pc).wait()

            tiles = [d_refs[t][...] for t in group]
            lhs = tiles[0] if len(tiles) == 1 else jnp.concatenate(tiles, axis=1)
            part = _dot_nt(lhs, w_s[pc])
            dh = part if dh is None else dh + part

        xt = x_ref[...]
        rstd = lax.rsqrt(jnp.mean(xt * xt, axis=-1, keepdims=True) + EPS)
        xh = xt * rstd
        gn_v = gn_ref[...]
        sc1 = 1.0 + sc_ref[...]
        dhx = dh * xh
        vec_ref[0:1, :] += _colsum(dh)
        vec_ref[1:2, :] += _colsum(dhx) * gn_v
        vec_ref[2:3, :] += _colsum(dhx) * sc1
        dxh = dh * (gn_v * sc1)
        gx_ref[...] = rstd * (dxh - xh * jnp.mean(dxh * xh, axis=-1, keepdims=True)) + dx2_ref[...]

        @pl.when(i == ni - 1)
        def _():
            for cp in scatter_copies():
                cp.wait()

    rowt = lambda i: (i, 0)
    row = pl.BlockSpec((1, D), lambda i: (0, 0))
    any_ = pl.BlockSpec(memory_space=pl.ANY)
    outs = pl.pallas_call(
        body, name="inproj_bwd_x", grid=(ni,),
        in_specs=[pl.BlockSpec((tm, a.shape[1]), rowt) for a in pieces] +
                 [any_, pl.BlockSpec((tm, D), rowt), pl.BlockSpec((tm, D), rowt), row, row] + [any_] * na,
        out_specs=(pl.BlockSpec((tm, D), rowt), pl.BlockSpec((8, D), lambda i: (0, 0))) + (any_,) * na,
        out_shape=(jax.ShapeDtypeStruct((s, D), F32), jax.ShapeDtypeStruct((8, D), F32)) +
                  tuple(jax.ShapeDtypeStruct((3,) + q.shape[1:], q.dtype) for q in sums),
        scratch_shapes=[pltpu.VMEM((4, D, PW), BF16), pltpu.SemaphoreType.DMA((4,)),
                        pltpu.SemaphoreType.DMA((3 * na,)), pltpu.SemaphoreType.DMA((3 * na,))],
        compiler_params=_params(("arbitrary",)),
    )(*pieces, wg, x, dx2, gn, scale, *sums)
    return outs[0], outs[1], outs[2:]


def _inproj_bwd_w(pieces, hbf, tk=1024):
    s = hbf.shape[0]
    steps, nk = _pieces_steps(pieces)
    npc = PW // D
    ns = s // tk
    np_ = len(pieces)

    def body(*refs):
        d_refs = refs[:np_]
        h_ref, g_ref = refs[np_:]
        cb, k = pl.program_id(0), pl.program_id(1)

        @pl.when(k == 0)
        def _():
            g_ref[...] = jnp.zeros_like(g_ref)

        for (s0, n), d_ref in zip(steps, d_refs):
            @pl.when((cb >= s0) & (cb < s0 + n))
            def _(d_ref=d_ref):
                g_ref[0] += _dot_tn(h_ref[...], d_ref[...])

    def piece_spec(s0, n):
        def imap(cb, k):
            active = (cb >= s0) & (cb < s0 + n)
            return (jnp.where(active, k, 0), jnp.clip(cb - s0, 0, n - 1))
        return pl.BlockSpec((tk, D), imap)

    return pl.pallas_call(
        body, name="inproj_bwd_w", grid=(nk, ns),
        in_specs=[piece_spec(s0, n) for s0, n in steps] + [pl.BlockSpec((tk, D), lambda cb, k: (k, 0))],
        out_specs=pl.BlockSpec((1, D, D), lambda cb, k: (cb // npc, 0, cb % npc)),
        out_shape=jax.ShapeDtypeStruct((4, D, PW), F32),
        compiler_params=_params(("parallel", "arbitrary")),
    )(*pieces, hbf)


D2D_CHUNK_BYTES = 512 * 1024


def _chunk_rows(a):
    return max(8, D2D_CHUNK_BYTES // (a.shape[-1] * a.dtype.itemsize))


def _pair_exchange(arrs):
    na = len(arrs)
    chunks = []
    for t, a in enumerate(arrs):
        hr = a.shape[1] // 2
        cr = _chunk_rows(a)
        chunks += [(t, j, r0, cr) for j in range(a.shape[0]) for r0 in range(0, hr, cr)]
    nch = len(chunks)

    def body(*refs):
        a_refs = refs[:na]
        rb_refs = refs[na:2 * na]
        ss, rs = refs[2 * na:]
        x, y, c = _coords()
        sib = (x, y, 1 - c)
        rcs = []
        for n, (t, j, r0, cr) in enumerate(chunks):
            hr = a_refs[t].shape[1] // 2
            rc = _rcopy(a_refs[t].at[j, pl.ds((1 - c) * hr + r0, cr), :], rb_refs[t].at[j, pl.ds(r0, cr), :],
                        ss.at[n], rs.at[n], sib)
            rc.start()
            rcs.append(rc)
        for rc in rcs:
            rc.wait_recv()
        for rc in rcs:
            rc.wait_send()

    any_ = pl.BlockSpec(memory_space=pl.ANY)
    halves = [jax.ShapeDtypeStruct((a.shape[0], a.shape[1] // 2, a.shape[2]), a.dtype) for a in arrs]
    return pl.pallas_call(
        body, name="pair_exchange",
        out_shape=tuple(halves),
        in_specs=[any_] * na, out_specs=tuple([any_] * na),
        scratch_shapes=[pltpu.SemaphoreType.DMA((nch,)), pltpu.SemaphoreType.DMA((nch,))],
        compiler_params=_params(),
    )(*arrs)


def _pair_swap(arrs):
    na = len(arrs)
    chunks = []
    for t, a in enumerate(arrs):
        cr = _chunk_rows(a)
        chunks += [(t, r0, cr) for r0 in range(0, a.shape[0], cr)]
    nch = len(chunks)

    def body(*refs):
        a_refs = refs[:na]
        o_refs = refs[na:2 * na]
        ss, rs = refs[2 * na:]
        x, y, c = _coords()
        sib = (x, y, 1 - c)
        rcs = []
        for n, (t, r0, cr) in enumerate(chunks):
            rows = pl.ds(r0, cr)
            rc = _rcopy(a_refs[t].at[rows, :], o_refs[t].at[rows, :], ss.at[n], rs.at[n], sib)
            rc.start()
            rcs.append(rc)
        for rc in rcs:
            rc.wait_recv()
        for rc in rcs:
            rc.wait_send()

    any_ = pl.BlockSpec(memory_space=pl.ANY)
    return pl.pallas_call(
        body, name="pair_swap",
        out_shape=tuple(jax.ShapeDtypeStruct(a.shape, a.dtype) for a in arrs),
        in_specs=[any_] * na, out_specs=tuple([any_] * na),
        scratch_shapes=[pltpu.SemaphoreType.DMA((nch,)), pltpu.SemaphoreType.DMA((nch,))],
        compiler_params=_params(),
    )(*arrs)


def _add_half(full, rb, core, tr):
    n, r, cdim = full.shape
    nb = r // 2 // tr

    def body(c_ref, a_ref, b_ref, o_ref):
        o_ref[...] = a_ref[...] + b_ref[...]

    mine = pl.BlockSpec((1, tr, cdim), lambda i, j, c_ref: (i, c_ref[0] * nb + j, 0))
    spec = pl.BlockSpec((1, tr, cdim), lambda i, j, c_ref: (i, j, 0))
    return pl.pallas_call(
        body, name="add_half",
        grid_spec=pltpu.PrefetchScalarGridSpec(num_scalar_prefetch=1, grid=(n, nb), in_specs=[mine, spec],
                                               out_specs=spec),
        out_shape=jax.ShapeDtypeStruct(rb.shape, rb.dtype),
        compiler_params=_params(("parallel", "parallel")),
    )(core, full, rb)


def _sum_slots(q, r3, shard, tr):
    _, r, cdim = q.shape

    def body(j_ref, q_ref, r_ref, o_ref):
        o_ref[...] = ((q_ref[0] + r_ref[0]) + r_ref[1]) + r_ref[2]

    return pl.pallas_call(
        body, name="sum_slots",
        grid_spec=pltpu.PrefetchScalarGridSpec(
            num_scalar_prefetch=1, grid=(r // tr,),
            in_specs=[pl.BlockSpec((1, tr, cdim), lambda i, j_ref: (j_ref[0], i, 0)),
                      pl.BlockSpec((3, tr, cdim), lambda i, j_ref: (0, i, 0))],
            out_specs=pl.BlockSpec((tr, cdim), lambda i, j_ref: (i, 0))),
        out_shape=jax.ShapeDtypeStruct((r, cdim), q.dtype),
        compiler_params=_params(("parallel",)),
    )(shard, q, r3)


def _allreduce_small(packs):
    na = len(packs)

    def body(*refs):
        p_refs, o_refs, rbufs = refs[:na], refs[na:2 * na], refs[2 * na:3 * na]
        s1, r1, s2, r2 = refs[3 * na:]
        me = _my_index()
        sends = []

        def chunk(t, d):
            ch = p_refs[t].shape[0] // NDEV
            return pl.ds(pl.multiple_of(d * ch, 8), ch)

        for t in range(na):
            for k in range(1, NDEV):
                e = 7 * t + k - 1
                cp = _rcopy(p_refs[t].at[chunk(t, me ^ k)], rbufs[t].at[me], s1.at[e], r1.at[e], _peer(k))
                cp.start()
                sends.append(cp)
            rbufs[t][me] = p_refs[t][chunk(t, me), :]
        for t in range(na):
            for k in range(1, NDEV):
                e = 7 * t + k - 1
                _rcopy(p_refs[t].at[chunk(t, me)], rbufs[t].at[me ^ k], s1.at[e], r1.at[e], _peer(k)).wait_recv()
            tot = rbufs[t][0]
            for d in range(1, NDEV):
                tot = tot + rbufs[t][d]
            o_refs[t][chunk(t, me), :] = tot
            for k in range(1, NDEV):
                e = 7 * t + k - 1
                cp = _rcopy(o_refs[t].at[chunk(t, me)], o_refs[t].at[chunk(t, me)], s2.at[e], r2.at[e], _peer(k))
                cp.start()
                sends.append(cp)
        for t in range(na):
            for k in range(1, NDEV):
                e = 7 * t + k - 1
                _rcopy(o_refs[t].at[chunk(t, me)], o_refs[t].at[chunk(t, me ^ k)], s2.at[e], r2.at[e],
                       _peer(k)).wait_recv()
        for cp in sends:
            cp.wait_send()

    vm = pl.BlockSpec(memory_space=pltpu.VMEM)
    return pl.pallas_call(
        body, name="allreduce_small",
        out_shape=tuple(jax.ShapeDtypeStruct(a.shape, F32) for a in packs),
        in_specs=[vm] * na, out_specs=tuple([vm] * na),
        scratch_shapes=[pltpu.VMEM((NDEV, a.shape[0] // NDEV, a.shape[1]), F32) for a in packs] +
                       [pltpu.SemaphoreType.DMA((7 * na,)) for _ in range(4)],
        compiler_params=_params(),
    )(*packs)


def _adamw_update(w, g, m, v):
    nm = B1 * m + (1.0 - B1) * g
    nv = B2 * v + (1.0 - B2) * (g * g)
    m_hat = nm / (1.0 - B1 ** STEP)
    v_hat = nv / (1.0 - B2 ** STEP)
    return -LR * (m_hat / (jnp.sqrt(v_hat) + ADAM_EPS) + WD * w), nm, nv


def _adamw(w, g, m, v, tr):
    r, cdim = w.shape

    def body(w_ref, g_ref, m_ref, v_ref, d_ref, nm_ref, nv_ref):
        d_ref[...], nm_ref[...], nv_ref[...] = _adamw_update(w_ref[...], g_ref[...], m_ref[...], v_ref[...])

    spec = pl.BlockSpec((tr, cdim), lambda i: (i, 0))
    sd = jax.ShapeDtypeStruct((r, cdim), F32)
    return pl.pallas_call(
        body, name="adamw", grid=(r // tr,), in_specs=[spec] * 4, out_specs=(spec,) * 3, out_shape=(sd,) * 3,
        compiler_params=_params(("parallel",)),
    )(w, g, m, v)


V_G_NORM, V_DMOD, V_B_GATE, V_CONV_B, V_LAM, V_G_FINAL, V_CONV_W, V_LOSS, V_ROWS = 0, 1, 4, 6, 7, 8, 9, 13, 64
M_W_A, M_W_X, M_B_A, M_B_X, M_ROWS = 0, H * DH, 2 * H * DH, 2 * H * DH + H, 2112
SMALL = ("g_norm", "b_mod", "b_gate", "conv_b", "lam", "g_final", "conv_w", "w_a", "w_x", "b_a", "b_x")


def _adamw_small(redv, redm, g_conv, wmv):
    def grad(name, rv, rm, gc):
        if name == "g_norm":
            return rv[V_G_NORM:V_G_NORM + 1, :]
        if name == "b_mod":
            return jnp.concatenate([rv[V_DMOD + t:V_DMOD + t + 1, :] for t in range(3)], axis=1)
        if name == "b_gate":
            return jnp.concatenate([rv[V_B_GATE + t:V_B_GATE + t + 1, :] for t in range(2)], axis=1)
        if name == "conv_b":
            return rv[V_CONV_B:V_CONV_B + 1, :]
        if name == "lam":
            return rv[V_LAM:V_LAM + 1, :]
        if name == "g_final":
            return rv[V_G_FINAL:V_G_FINAL + 1, :]
        if name == "conv_w":
            return gc[...]
        if name == "w_a":
            return rm[M_W_A:M_W_A + H * DH, :]
        if name == "w_x":
            return rm[M_W_X:M_W_X + H * DH, :]
        if name == "b_a":
            return rm[M_B_A:M_B_A + H, :]
        return rm[M_B_X:M_B_X + H, :]

    n = len(SMALL)

    def body(*refs):
        rv, rm, gc = refs[:3]
        ins, outs = refs[3:3 + 3 * n], refs[3 + 3 * n:]
        for t, name in enumerate(SMALL):
            w_ref, m_ref, v_ref = ins[3 * t:3 * t + 3]
            g_out, d_out, m_out, v_out = outs[4 * t:4 * t + 4]
            g = grad(name, rv, rm, gc)
            g_out[...] = g
            d_out[...], m_out[...], v_out[...] = _adamw_update(w_ref[...], g, m_ref[...], v_ref[...])

    vm = pl.BlockSpec(memory_space=pltpu.VMEM)
    flat = [a for name in SMALL for a in wmv[name]]
    shapes = [jax.ShapeDtypeStruct(wmv[name][0].shape, F32) for name in SMALL for _ in range(4)]
    outs = pl.pallas_call(
        body, name="adamw_small", out_shape=tuple(shapes),
        in_specs=[vm] * (3 + len(flat)), out_specs=tuple([vm] * len(shapes)),
        compiler_params=_params(),
    )(redv, redm, g_conv, *flat)
    return {name: outs[4 * t:4 * t + 4] for t, name in enumerate(SMALL)}


def _rope_tables(positions):
    inv_freq = ROPE_THETA ** (-jnp.arange(0, ROT, 2, dtype=F32) / ROT)
    ang = positions.astype(F32)[:, None] * inv_freq
    cos, sin = jnp.cos(ang), jnp.sin(ang)
    n = positions.shape[0]
    half = ROT // 2
    rc = jnp.concatenate([cos, cos, jnp.ones((n, DH - ROT), F32)], axis=1)
    rsa = jnp.concatenate([-sin, jnp.zeros((n, DH - half), F32)], axis=1)
    rsb = jnp.concatenate([jnp.zeros((n, half), F32), sin, jnp.zeros((n, DH - ROT), F32)], axis=1)
    return rc, rsa, rsb


def kernel(x, c, positions, g_norm, w_mod, b_mod, w_in, b_gate, conv_w, conv_b, w_a, b_a, w_x, b_x, lam, w_out_rnn, w_out_attn, w_o, g_final, loss_target, m_g_norm, m_w_mod, m_b_mod, m_w_in, m_b_gate, m_conv_w, m_conv_b, m_w_a, m_b_a, m_w_x, m_b_x, m_lam, m_w_out_rnn, m_w_out_attn, m_w_o, m_g_final, v_g_norm, v_w_mod, v_b_mod, v_w_in, v_b_gate, v_conv_w, v_conv_b, v_w_a, v_b_a, v_w_x, v_b_x, v_lam, v_w_out_rnn, v_w_out_attn, v_w_o, v_g_final):
    s = x.shape[1]
    xi = lax.axis_index("x")
    yi = lax.axis_index("y")
    ci = lax.axis_index("c")
    shard = 2 * xi + yi
    x2d = x[0]
    tgt = loss_target[0]
    pos = positions[0]

    c_all, mod4, conv_all = _mod_fwd(c, w_mod[0], b_mod.reshape(4, 1, 768), conv_w[0])
    mod = mod4.reshape(1, 3 * D)
    shift, scale, gate = mod[:, :D], mod[:, D:2 * D], mod[:, 2 * D:]
    w3_sh = jnp.stack([w_out_rnn[0], w_out_attn[0], w_o[0]]).astype(BF16)
    wsh = jnp.concatenate([w_in[0].astype(BF16), w3_sh.reshape(384, PW)], axis=0)
    conv_full = conv_all[0::2].transpose(1, 0, 2).reshape(4, D)

    order = jnp.stack([shard, shard ^ 2, shard ^ 1, shard ^ 3]).astype(jnp.int32)
    p, hbf, wg = _gather_norm_inproj(x2d, g_norm, shift, scale, wsh, order)
    wg = lax.dynamic_update_slice(wg, wsh[None], (shard, 0, 0))
    w3 = wg[:, D:, :].reshape(4, 3, 256, D).transpose(1, 0, 2, 3).reshape(3, D, D)
    rc, rsa, rsb = _rope_tables(pos)
    pos_col = pos.reshape(s, 1)
    b_a3, b_x3 = b_a.reshape(H, 1, DH), b_x.reshape(H, 1, DH)
    hr, gr = _rnn_fwd(p, pos_col, conv_full, conv_b, w_a[0], b_a3, w_x[0], b_x3, lam)
    o, lse, ga = _attn_fwd(p, rc, rsa, rsb)

    dgr, dga, dc, dx2, vec_t, g_out = _tail(gr, ga, p, x2d, tgt, w3, b_gate, gate, g_final.reshape(1, D))

    dxr, dzr, g_wa, g_ba, g_wx, g_bx, g_lam, g_cw, g_cb = _rnn_bwd(
        p, hr, dgr, pos_col, conv_full, conv_b, w_a[0], b_a3, w_x[0], b_x3, lam)
    dq, dk, dv, dza = _attn_bwd(p, o, lse, dga, rc, rsa, rsb)

    pieces = [dxr, dzr, dq, dk, dv, dza, dc]
    g_win = _inproj_bwd_w(pieces, hbf)

    core = ci.reshape(1)
    shard1 = shard.reshape(1)
    rb_a, rb_b = _pair_exchange([g_win, g_out])
    q_a, q_b = _add_half(g_win, rb_a, core, tr=256), _add_half(g_out, rb_b, core, tr=128)
    grad_x, vec_n, (r_a, r_b) = _inproj_bwd_x(pieces, wg, x2d, dx2, g_norm, scale, [q_a, q_b])
    f_a, f_b = _sum_slots(q_a, r_a, shard1, tr=256), _sum_slots(q_b, r_b, shard1, tr=128)
    s_a, s_b = _pair_swap([f_a, f_b])
    south = ci == 0
    grad_w_in = jnp.where(south, jnp.concatenate([f_a, s_a], axis=0), jnp.concatenate([s_a, f_a], axis=0))
    g3 = jnp.where(south, jnp.concatenate([f_b, s_b], axis=0), jnp.concatenate([s_b, f_b], axis=0)).reshape(3, 256, D)

    dmod_row = jnp.concatenate([vec_n[0:1], vec_n[1:2], vec_t[1:2]], axis=1)
    vpack = jnp.concatenate([
        vec_n[2:3],
        vec_n[0:2], vec_t[1:2],
        vec_t[2:4],
        g_cb.reshape(1, D),
        g_lam.reshape(1, D),
        vec_t[0:1],
        g_cw.transpose(1, 0, 2).reshape(4, D),
        vec_t[4:5],
        jnp.zeros((V_ROWS - 14, D), F32)], axis=0)
    mpack = jnp.concatenate([
        g_wa.reshape(H * DH, DH), g_wx.reshape(H * DH, DH), g_ba.reshape(H, DH), g_bx.reshape(H, DH),
        jnp.zeros((M_ROWS - 2 * H * DH - 2 * H, DH), F32)], axis=0)
    redv, redm = _allreduce_small([vpack, mpack])
    loss = redv[V_LOSS, 0]
    grad_w_mod = _mod_bwd(dmod_row.reshape(4, 1, 768), c_all)
    g_conv_sh = lax.dynamic_slice_in_dim(redv[V_CONV_W:V_CONV_W + 4], shard * 256, 256, axis=1)

    shape2d = dict(g_norm=(1, D), b_mod=(1, 3 * D), b_gate=(1, 2 * D), conv_b=(1, D), lam=(1, D), g_final=(1, D),
                   conv_w=(4, 256), w_a=(H * DH, DH), w_x=(H * DH, DH), b_a=(H, DH), b_x=(H, DH))
    given = dict(
        g_norm=(g_norm, m_g_norm, v_g_norm), b_mod=(b_mod, m_b_mod, v_b_mod), b_gate=(b_gate, m_b_gate, v_b_gate),
        conv_b=(conv_b, m_conv_b, v_conv_b), lam=(lam, m_lam, v_lam), g_final=(g_final, m_g_final, v_g_final),
        conv_w=(conv_w, m_conv_w, v_conv_w), w_a=(w_a, m_w_a, v_w_a), w_x=(w_x, m_w_x, v_w_x),
        b_a=(b_a, m_b_a, v_b_a), b_x=(b_x, m_b_x, v_b_x))
    small = _adamw_small(redv, redm, g_conv_sh,
                         {n: tuple(a.reshape(shape2d[n]) for a in given[n]) for n in SMALL})

    big_in = _adamw(w_in[0], grad_w_in, m_w_in[0], v_w_in[0], tr=256)
    big_mod = _adamw(w_mod[0], grad_w_mod, m_w_mod[0], v_w_mod[0], tr=256)
    w3f = jnp.concatenate([w_out_rnn[0], w_out_attn[0], w_o[0]], axis=0)
    m3f = jnp.concatenate([m_w_out_rnn[0], m_w_out_attn[0], m_w_o[0]], axis=0)
    v3f = jnp.concatenate([v_w_out_rnn[0], v_w_out_attn[0], v_w_o[0]], axis=0)
    big_out = _adamw(w3f, g3.reshape(768, D), m3f, v3f, tr=256)

    names = ["g_norm", "w_mod", "b_mod", "w_in", "b_gate", "conv_w", "conv_b", "w_a", "b_a", "w_x", "b_x", "lam",
             "w_out_rnn", "w_out_attn", "w_o", "g_final"]
    outs = []
    for idx in range(4):
        d = {n: small[n][idx].reshape(given[n][0].shape) for n in SMALL}
        if idx == 0:
            d.update(w_mod=grad_w_mod[None], w_in=grad_w_in[None],
                     w_out_rnn=g3[0][None], w_out_attn=g3[1][None], w_o=g3[2][None])
        else:
            d.update(w_mod=big_mod[idx - 1][None], w_in=big_in[idx - 1][None],
                     w_out_rnn=big_out[idx - 1][0:256][None], w_out_attn=big_out[idx - 1][256:512][None],
                     w_o=big_out[idx - 1][512:768][None])
        outs.append(d)
    flat = [d[n] for d in outs for n in names]
    return (loss, grad_x[None], *flat)
```

```python
import jax
import jax.numpy as jnp
from jax import lax
from jax.experimental import pallas as pl
from jax.experimental.pallas import tpu as pltpu

F32, BF16 = jnp.float32, jnp.bfloat16
MESH = pl.DeviceIdType.MESH
HIGHEST = lax.Precision.HIGHEST

D = 1024
H = 8
DH = 128
PW = 2048
EPS = 1e-6
LRU_C = 8.0
SCALE = DH ** -0.5
NEG = -1e30
SPAN = 2048
UB = 128
DILATIONS = (1, 4, 16)
UNIT_BATCH = 8
ROPE_THETA = 500000.0
ROT = 32

LR, B1, B2, ADAM_EPS, WD, STEP = 0.001, 0.9, 0.999, 1e-08, 0.01, 10

WROWS = 1024 + 384
NDEV = 8


def _params(sem=None, vmem_mb=56):
    return pltpu.CompilerParams(dimension_semantics=sem, vmem_limit_bytes=vmem_mb * 2 ** 20)


def _coords():
    return lax.axis_index("x"), lax.axis_index("y"), lax.axis_index("c")


def _flip(v, bit):
    return 1 - v if bit else v


def _peer(k):
    x, y, c = _coords()
    return (_flip(x, (k >> 2) & 1), _flip(y, (k >> 1) & 1), _flip(c, k & 1))


def _my_index():
    x, y, c = _coords()
    return 4 * x + 2 * y + c


def _rcopy(src, dst, ssem, rsem, dev):
    return pltpu.make_async_remote_copy(src_ref=src, dst_ref=dst, send_sem=ssem, recv_sem=rsem,
                                        device_id=dev, device_id_type=MESH)


def _sigmoid(x):
    return jax.nn.sigmoid(x)


def _dot(a, b):
    return jnp.dot(a, b, preferred_element_type=F32)


def _dot_nt(a, b):
    return lax.dot_general(a, b, (((1,), (1,)), ((), ())), preferred_element_type=F32)


def _dot_tn(a, b):
    return lax.dot_general(a, b, (((0,), (0,)), ((), ())), preferred_element_type=F32)


def _colsum(a):
    return jnp.sum(a, axis=0, keepdims=True)


def _mod_fwd(c, w_mod_sh, b_mod4, conv_sh):
    def body(c_ref, w_ref, b_ref, cv_ref, call_ref, mod_ref, cvall_ref, rows_ref, cmat_ref, s1, r1, s2, r2, s3, r3):
        x, y, _ = _coords()
        me = _my_index()
        j = 2 * x + y
        call_ref[me] = c_ref[...]
        cvall_ref[me] = cv_ref[...]
        sends = []
        for k in range(1, NDEV):
            cp = _rcopy(call_ref.at[me], call_ref.at[me], s1.at[k - 1], r1.at[k - 1], _peer(k))
            cp.start()
            sends.append(cp)
            cp = _rcopy(cvall_ref.at[me], cvall_ref.at[me], s3.at[k - 1], r3.at[k - 1], _peer(k))
            cp.start()
            sends.append(cp)
        for k in range(1, NDEV):
            pk = me ^ k
            _rcopy(call_ref.at[pk], call_ref.at[pk], s1.at[k - 1], r1.at[k - 1], _peer(k)).wait_recv()
        for b in range(NDEV):
            cmat_ref[pl.ds(b, 1), :] = call_ref[b]
        cm = cmat_ref[...]
        act = cm * _sigmoid(cm)
        mp = jnp.dot(act, w_ref[...], preferred_element_type=F32, precision=HIGHEST) + b_ref[j]
        for b in range(NDEV):
            rows_ref[b] = mp[b:b + 1]
        mod_ref[j] = rows_ref[me]
        for q, k in enumerate((2, 4, 6)):
            cp = _rcopy(rows_ref.at[me ^ k], mod_ref.at[j], s2.at[q], r2.at[q], _peer(k))
            cp.start()
            sends.append(cp)
        for q, k in enumerate((2, 4, 6)):
            jq = j ^ (k >> 1)
            _rcopy(rows_ref.at[me], mod_ref.at[jq], s2.at[q], r2.at[q], _peer(k)).wait_recv()
        for k in range(1, NDEV):
            pk = me ^ k
            _rcopy(cvall_ref.at[pk], cvall_ref.at[pk], s3.at[k - 1], r3.at[k - 1], _peer(k)).wait_recv()
        for cp in sends:
            cp.wait_send()

    vm = pl.BlockSpec(memory_space=pltpu.VMEM)
    return pl.pallas_call(
        body, name="mod_fwd",
        out_shape=(jax.ShapeDtypeStruct((NDEV, 1, D), F32), jax.ShapeDtypeStruct((4, 1, 768), F32),
                   jax.ShapeDtypeStruct((NDEV,) + conv_sh.shape, F32)),
        in_specs=[vm, vm, vm, vm], out_specs=(vm, vm, vm),
        scratch_shapes=[pltpu.VMEM((NDEV, 1, 768), F32), pltpu.VMEM((NDEV, D), F32),
                        pltpu.SemaphoreType.DMA((7,)), pltpu.SemaphoreType.DMA((7,)),
                        pltpu.SemaphoreType.DMA((3,)), pltpu.SemaphoreType.DMA((3,)),
                        pltpu.SemaphoreType.DMA((7,)), pltpu.SemaphoreType.DMA((7,))],
        compiler_params=_params(),
    )(c, w_mod_sh, b_mod4, conv_sh)


def _mod_bwd(dmod4, c_all):
    def body(d_ref, call_ref, gw_ref, dall_ref, cmat_ref, dmat_ref, s1, r1):
        x, y, _ = _coords()
        me = _my_index()
        j = 2 * x + y
        dall_ref[me] = d_ref[...]
        sends = []
        for k in range(1, NDEV):
            cp = _rcopy(dall_ref.at[me], dall_ref.at[me], s1.at[k - 1], r1.at[k - 1], _peer(k))
            cp.start()
            sends.append(cp)
        for k in range(1, NDEV):
            pk = me ^ k
            _rcopy(dall_ref.at[pk], dall_ref.at[pk], s1.at[k - 1], r1.at[k - 1], _peer(k)).wait_recv()
        for cp in sends:
            cp.wait_send()
        for b in range(NDEV):
            cmat_ref[pl.ds(b, 1), :] = call_ref[b]
            dmat_ref[pl.ds(b, 1), :] = dall_ref[b, j]
        cm = cmat_ref[...]
        act = cm * _sigmoid(cm)
        gw_ref[...] = lax.dot_general(act, dmat_ref[...], (((0,), (0,)), ((), ())),
                                      preferred_element_type=F32, precision=HIGHEST)

    vm = pl.BlockSpec(memory_space=pltpu.VMEM)
    return pl.pallas_call(
        body, name="mod_bwd",
        out_shape=jax.ShapeDtypeStruct((D, 768), F32),
        in_specs=[vm, vm], out_specs=vm,
        scratch_shapes=[pltpu.VMEM((NDEV, 4, 1, 768), F32), pltpu.VMEM((NDEV, D), F32), pltpu.VMEM((NDEV, 768), F32),
                        pltpu.SemaphoreType.DMA((7,)), pltpu.SemaphoreType.DMA((7,))],
        compiler_params=_params(),
    )(dmod4, c_all)


def _gather_norm_inproj(x, gn, shift, scale, wsh, order, tm=1024, tn=512):
    s = x.shape[0]
    ni = s // tm
    npc = PW // tn
    rows, cols = wsh.shape
    half = rows // 2
    nch = 4
    cr = half // nch
    chips = ((1, 0), (0, 1), (1, 1))

    def body(ord_ref, x_ref, gn_ref, sh_ref, sc_ref, wsh_hbm, p_ref, h_ref, wg_hbm, hs_all, w_s, wsem, ss, rs):
        slot, i, col = pl.program_id(0), pl.program_id(1), pl.program_id(2)
        cx, cy, cc = _coords()
        j = 2 * cx + cy
        sib = (cx, cy, 1 - cc)
        mine = lambda n: pl.ds(cc * half + n * cr, cr)
        theirs = lambda n: pl.ds((1 - cc) * half + n * cr, cr)
        shard_of = lambda q: j ^ (2 * chips[q][0] + chips[q][1])

        def to_chip(q, n):
            e = nch * q + n
            return _rcopy(wsh_hbm.at[mine(n)], wg_hbm.at[j, mine(n)], ss.at[e], rs.at[e],
                          (_flip(cx, chips[q][0]), _flip(cy, chips[q][1]), cc))

        def from_chip(q, n):
            e = nch * q + n
            return _rcopy(wsh_hbm.at[mine(n)], wg_hbm.at[shard_of(q), mine(n)], ss.at[e], rs.at[e], sib)

        def to_sibling(q, n):
            e = 3 * nch + nch * q + n
            return _rcopy(wg_hbm.at[shard_of(q), mine(n)], wg_hbm.at[shard_of(q), mine(n)], ss.at[e], rs.at[e], sib)

        def from_sibling(q, n):
            e = 3 * nch + nch * q + n
            return _rcopy(wsh_hbm.at[mine(n)], wg_hbm.at[shard_of(q), theirs(n)], ss.at[e], rs.at[e], sib)

        def load(sl, src):
            cp = pltpu.make_async_copy(src, w_s.at[sl], wsem.at[sl])
            cp.start()
            cp.wait()

        first = (i == 0) & (col == 0)

        @pl.when(first & (slot == 0))
        def _():
            for n in range(nch):
                for q in (0, 1):
                    to_chip(q, n).start()
            load(0, wsh_hbm.at[pl.ds(0, D), :])

        @pl.when(first & (slot == 1))
        def _():
            for q in (0, 1):
                for n in range(nch):
                    from_chip(q, n).wait_recv()
                    to_sibling(q, n).start()
            for n in range(nch):
                to_chip(2, n).start()
            for n in range(nch):
                from_sibling(0, n).wait_recv()
            load(1, wg_hbm.at[shard_of(0), pl.ds(0, D), :])

        @pl.when(first & (slot == 2))
        def _():
            for n in range(nch):
                from_sibling(1, n).wait_recv()
            load(2, wg_hbm.at[shard_of(1), pl.ds(0, D), :])

        @pl.when(first & (slot == 3))
        def _():
            for n in range(nch):
                from_chip(2, n).wait_recv()
                to_sibling(2, n).start()
            for n in range(nch):
                from_sibling(2, n).wait_recv()
            load(3, wg_hbm.at[shard_of(2), pl.ds(0, D), :])
            for q in range(3):
                for n in range(nch):
                    to_chip(q, n).wait_send()
                    to_sibling(q, n).wait_send()

        @pl.when((slot == 0) & (col == 0))
        def _():
            xt = x_ref[...]
            rstd = lax.rsqrt(jnp.mean(xt * xt, axis=-1, keepdims=True) + EPS)
            h = ((xt * rstd * gn_ref[...]) * (1.0 + sc_ref[...]) + sh_ref[...]).astype(BF16)
            hs_all[i] = h
            h_ref[...] = h

        p_ref[0] = _dot(hs_all[i], w_s[slot, :, pl.ds(pl.multiple_of(col * tn, tn), tn)]).astype(BF16)

    row = pl.BlockSpec((1, D), lambda sl, i, col, o: (0, 0))
    x_rows = lambda sl, i, col, o: (jnp.where(sl == 0, i, ni - 1), 0)
    any_ = pl.BlockSpec(memory_space=pl.ANY)
    return pl.pallas_call(
        body, name="gather_norm_inproj",
        grid_spec=pltpu.PrefetchScalarGridSpec(
            num_scalar_prefetch=1, grid=(4, ni, npc),
            in_specs=[pl.BlockSpec((tm, D), x_rows), row, row, row, any_],
            out_specs=(pl.BlockSpec((1, tm, tn), lambda sl, i, col, o: (o[sl], i, col)),
                       pl.BlockSpec((tm, D), x_rows), any_),
            scratch_shapes=[pltpu.VMEM((ni, tm, D), BF16), pltpu.VMEM((4, D, PW), BF16),
                            pltpu.SemaphoreType.DMA((4,)),
                            pltpu.SemaphoreType.DMA((6 * nch,)), pltpu.SemaphoreType.DMA((6 * nch,))]),
        out_shape=(jax.ShapeDtypeStruct((4, s, PW), BF16), jax.ShapeDtypeStruct((s, D), BF16),
                   jax.ShapeDtypeStruct((4, rows, cols), wsh.dtype)),
        compiler_params=_params(("arbitrary", "arbitrary", "arbitrary")),
    )(order, x, gn, shift, scale, wsh)


def _shift_down(prev8, cur, d):
    t = cur.shape[0]
    c3 = cur.reshape(t // 8, 8, DH)
    rot = pltpu.roll(c3, d, 1)
    before = jnp.concatenate([pltpu.roll(prev8, d, 0).reshape(1, 8, DH), rot[:-1]], axis=0)
    rows = lax.broadcasted_iota(jnp.int32, c3.shape, 1)
    return jnp.where(rows >= d, rot, before).reshape(t, DH)


def _shift_up(cur, next8, d):
    t = cur.shape[0]
    c3 = cur.reshape(t // 8, 8, DH)
    rot = pltpu.roll(c3, 8 - d, 1)
    after = jnp.concatenate([rot[1:], pltpu.roll(next8, 8 - d, 0).reshape(1, 8, DH)], axis=0)
    rows = lax.broadcasted_iota(jnp.int32, c3.shape, 1)
    return jnp.where(rows < 8 - d, rot, after).reshape(t, DH)


def _rnn_gates(xr, prev8, cw, cb, wa, ba, wx, bx, lam, reset):
    xc = cw[3:4] * xr + cb
    for d in (1, 2, 3):
        xc = xc + cw[3 - d:4 - d] * _shift_down(prev8, xr, d)
    xcb = xc.astype(BF16)
    r = _sigmoid(_dot(xcb, wa.astype(BF16)) + ba)
    ig = _sigmoid(_dot(xcb, wx.astype(BF16)) + bx)
    nl = -lam
    sp = jnp.maximum(nl, 0.0) + jnp.log1p(jnp.exp(-jnp.abs(nl)))
    log_a = (-LRU_C * r) * sp
    a = jnp.where(reset, 0.0, jnp.exp(log_a))
    mult = jnp.where(reset, 1.0, jnp.sqrt(1.0 - jnp.exp(2.0 * log_a)))
    return xc, r, ig, sp, a, mult


def _log_scan(a, b, axis, up):
    n = a.shape[axis]
    rows = lax.broadcasted_iota(jnp.int32, a.shape, axis)
    d = 1
    while d < n:
        m = rows < n - d if up else rows >= d
        shift = n - d if up else d
        a_s = pltpu.roll(a, shift, axis)
        b_s = pltpu.roll(b, shift, axis)
        b = jnp.where(m, a * b_s + b, b)
        a = jnp.where(m, a * a_s, a)
        d *= 2
    return a, b


def _scan(a, b, t, edge, up=False):
    g = t // 8
    a3, b3 = _log_scan(a.reshape(g, 8, DH), b.reshape(g, 8, DH), 1, up)
    last = 0 if up else 7
    ag, bg = _log_scan(a3[:, last, :], b3[:, last, :], 0, up)
    hg = ag * edge + bg
    grp = lax.broadcasted_iota(jnp.int32, hg.shape, 0)
    if up:
        cin = jnp.where(grp == g - 1, edge, pltpu.roll(hg, g - 1, 0))
        tail = hg[0:1]
    else:
        cin = jnp.where(grp == 0, edge, pltpu.roll(hg, 1, 0))
        tail = hg[g - 1:g]
    return (a3 * cin[:, None, :] + b3).reshape(t, DH), tail


def _rnn_fwd(p, pos, conv_w, conv_b, w_a, b_a, w_x, b_x, lam, tt=512):
    s = p.shape[1]
    nt = s // tt

    def body(xr_ref, z_ref, pos_ref, cw_ref, cb_ref, wa_ref, ba_ref, wx_ref, bx_ref, lam_ref,
             hr_ref, gr_ref, xprev, hprev):
        @pl.when(pl.program_id(1) == 0)
        def _():
            xprev[...] = jnp.zeros_like(xprev)
            hprev[...] = jnp.zeros_like(hprev)

        xr = xr_ref[0].astype(F32)
        z = z_ref[0].astype(F32)
        reset = pos_ref[...] == 0
        xc, r, ig, sp, a, mult = _rnn_gates(xr, xprev[...], cw_ref[...], cb_ref[...], wa_ref[0], ba_ref[0],
                                            wx_ref[0], bx_ref[0], lam_ref[...], reset)
        bx = mult * ig * xc
        h, h_last = _scan(a, bx, tt, hprev[0:1])
        xprev[...] = xr[tt - 8:]
        hprev[...] = jnp.broadcast_to(h_last, (8, DH))
        hr_ref[...] = h
        gr_ref[...] = (h * (z * _sigmoid(z))).astype(BF16)

    head_row = lambda hh, t: (0, hh)
    return pl.pallas_call(
        body, name="rnn_fwd", grid=(H, nt),
        in_specs=[pl.BlockSpec((1, tt, DH), lambda hh, t: (0, t, hh)),
                  pl.BlockSpec((1, tt, DH), lambda hh, t: (0, t, H + hh)),
                  pl.BlockSpec((tt, 1), lambda hh, t: (t, 0)),
                  pl.BlockSpec((4, DH), head_row), pl.BlockSpec((1, DH), head_row),
                  pl.BlockSpec((1, DH, DH), lambda hh, t: (hh, 0, 0)), pl.BlockSpec((1, 1, DH), lambda hh, t: (hh, 0, 0)),
                  pl.BlockSpec((1, DH, DH), lambda hh, t: (hh, 0, 0)), pl.BlockSpec((1, 1, DH), lambda hh, t: (hh, 0, 0)),
                  pl.BlockSpec((1, DH), head_row)],
        out_specs=(pl.BlockSpec((tt, DH), lambda hh, t: (t, hh)), pl.BlockSpec((tt, DH), lambda hh, t: (t, hh))),
        out_shape=(jax.ShapeDtypeStruct((s, D), F32), jax.ShapeDtypeStruct((s, D), BF16)),
        scratch_shapes=[pltpu.VMEM((8, DH), F32), pltpu.VMEM((8, DH), F32)],
        compiler_params=_params(("parallel", "arbitrary")),
    )(p, p, pos, conv_w, conv_b, w_a, b_a, w_x, b_x, lam)


def _rnn_bwd(p, hr, dgr, pos, conv_w, conv_b, w_a, b_a, w_x, b_x, lam, tt=512):
    s = p.shape[1]
    nt = s // tt
    t8 = tt // 8

    def body(xr_ref, z_ref, xp_ref, hr_ref, hp_ref, dg_ref, pos_ref, cw_ref, cb_ref, wa_ref, ba_ref, wx_ref, bx_ref,
             lam_ref, dxr_ref, dz_ref, gwa_ref, gba_ref, gwx_ref, gbx_ref, glam_ref, gcw_ref, gcb_ref,
             a_next, g_next, dxc_next):
        t = pl.program_id(1)
        has_prev = t < nt - 1

        @pl.when(t == 0)
        def _():
            a_next[...] = jnp.zeros_like(a_next)
            g_next[...] = jnp.zeros_like(g_next)
            dxc_next[...] = jnp.zeros_like(dxc_next)
            gwa_ref[...] = jnp.zeros_like(gwa_ref)
            gba_ref[...] = jnp.zeros_like(gba_ref)
            gwx_ref[...] = jnp.zeros_like(gwx_ref)
            gbx_ref[...] = jnp.zeros_like(gbx_ref)
            glam_ref[...] = jnp.zeros_like(glam_ref)
            gcw_ref[...] = jnp.zeros_like(gcw_ref)
            gcb_ref[...] = jnp.zeros_like(gcb_ref)

        xr = xr_ref[0].astype(F32)
        z = z_ref[0].astype(F32)
        hr_blk = hr_ref[...]
        dg = dg_ref[...]
        xprev = jnp.where(has_prev, xp_ref[0].astype(F32)[8:], 0.0)
        hprev8 = jnp.where(has_prev, hp_ref[...], 0.0)
        reset = pos_ref[...] == 0
        cw = cw_ref[...]
        wa = wa_ref[0]
        wx = wx_ref[0]
        lam_v = lam_ref[...]
        xc, r, ig, sp, a, mult = _rnn_gates(xr, xprev, cw, cb_ref[...], wa, ba_ref[0], wx, bx_ref[0], lam_v, reset)

        sz = _sigmoid(z)
        dh = dg * (z * sz)
        dz_ref[...] = (dg * hr_blk * (sz * (1.0 + z * (1.0 - sz)))).astype(BF16)

        an = _shift_up(a, a_next[...], 1)
        g, g_first = _scan(an, dh, tt, g_next[0:1], up=True)
        a_next[...] = jnp.broadcast_to(a[0:1], (8, DH))
        g_next[...] = jnp.broadcast_to(g_first, (8, DH))

        hm1 = _shift_down(hprev8, hr_blk, 1)
        da = g * hm1
        dmult = g * (ig * xc)
        di = g * (mult * xc)
        dxc = g * (mult * ig)
        dla = jnp.where(reset, 0.0, da * a - dmult * (a * a) / mult)
        dr = dla * (-LRU_C * sp)
        dsp = _colsum(dla * (-LRU_C * r))
        glam_ref[0] += dsp * (-_sigmoid(-lam_v))
        dpa = dr * r * (1.0 - r)
        dpx = di * ig * (1.0 - ig)
        dpab = dpa.astype(BF16)
        dpxb = dpx.astype(BF16)
        dxc = dxc + _dot_nt(dpab, wa.astype(BF16)) + _dot_nt(dpxb, wx.astype(BF16))
        xcb = xc.astype(BF16)
        gwa_ref[0] += _dot_tn(xcb, dpab)
        gwx_ref[0] += _dot_tn(xcb, dpxb)
        gba_ref[0] += _colsum(dpa)
        gbx_ref[0] += _colsum(dpx)

        dxr = cw[3:4] * dxc
        for d in (1, 2, 3):
            dxr = dxr + cw[3 - d:4 - d] * _shift_up(dxc, dxc_next[...], d)
        dxr_ref[...] = dxr.astype(BF16)
        dxc_next[...] = dxc[0:8]
        gcb_ref[0] += _colsum(dxc)
        gcw_ref[0, 3:4, :] += _colsum(xr * dxc)
        for d in (1, 2, 3):
            gcw_ref[0, 3 - d:4 - d, :] += _colsum(_shift_down(xprev, xr, d) * dxc)

    rt = lambda t: nt - 1 - t
    prev8 = lambda t: jnp.maximum(rt(t) * t8 - 1, 0)
    head_row = lambda hh, t: (0, hh)
    hsm = lambda hh, t: (hh, 0, 0)
    return pl.pallas_call(
        body, name="rnn_bwd", grid=(H, nt),
        in_specs=[pl.BlockSpec((1, tt, DH), lambda hh, t: (0, rt(t), hh)),
                  pl.BlockSpec((1, tt, DH), lambda hh, t: (0, rt(t), H + hh)),
                  pl.BlockSpec((1, 16, DH), lambda hh, t: (0, jnp.maximum(rt(t) * (tt // 16) - 1, 0), hh)),
                  pl.BlockSpec((tt, DH), lambda hh, t: (rt(t), hh)),
                  pl.BlockSpec((8, DH), lambda hh, t: (prev8(t), hh)),
                  pl.BlockSpec((tt, DH), lambda hh, t: (rt(t), hh)),
                  pl.BlockSpec((tt, 1), lambda hh, t: (rt(t), 0)),
                  pl.BlockSpec((4, DH), head_row), pl.BlockSpec((1, DH), head_row),
                  pl.BlockSpec((1, DH, DH), hsm), pl.BlockSpec((1, 1, DH), hsm),
                  pl.BlockSpec((1, DH, DH), hsm), pl.BlockSpec((1, 1, DH), hsm),
                  pl.BlockSpec((1, DH), head_row)],
        out_specs=(pl.BlockSpec((tt, DH), lambda hh, t: (rt(t), hh)), pl.BlockSpec((tt, DH), lambda hh, t: (rt(t), hh)),
                   pl.BlockSpec((1, DH, DH), hsm), pl.BlockSpec((1, 1, DH), hsm),
                   pl.BlockSpec((1, DH, DH), hsm), pl.BlockSpec((1, 1, DH), hsm),
                   pl.BlockSpec((1, 1, DH), hsm), pl.BlockSpec((1, 4, DH), hsm), pl.BlockSpec((1, 1, DH), hsm)),
        out_shape=(jax.ShapeDtypeStruct((s, D), BF16), jax.ShapeDtypeStruct((s, D), BF16),
                   jax.ShapeDtypeStruct((H, DH, DH), F32), jax.ShapeDtypeStruct((H, 1, DH), F32),
                   jax.ShapeDtypeStruct((H, DH, DH), F32), jax.ShapeDtypeStruct((H, 1, DH), F32),
                   jax.ShapeDtypeStruct((H, 1, DH), F32), jax.ShapeDtypeStruct((H, 4, DH), F32),
                   jax.ShapeDtypeStruct((H, 1, DH), F32)),
        scratch_shapes=[pltpu.VMEM((8, DH), F32), pltpu.VMEM((8, DH), F32), pltpu.VMEM((8, DH), F32)],
        compiler_params=_params(("parallel", "arbitrary")),
    )(p, p, p, hr, hr, dgr, pos, conv_w, conv_b, w_a, b_a, w_x, b_x, lam)


def _rope(t, c, sa, sb):
    return t * c + pltpu.roll(t, DH - ROT // 2, 1) * sa + pltpu.roll(t, ROT // 2, 1) * sb


def _rope_bwd(g, c, sa, sb):
    return g * c + pltpu.roll(g * sa, ROT // 2, 1) + pltpu.roll(g * sb, DH - ROT // 2, 1)


def _unit_bases(gi, u):
    dil = DILATIONS[gi]
    if dil == 1:
        return u * UB, SPAN + (u - 1) * UB, u == 0
    if dil == 4:
        blk, r = u // 4, u % 4
        return blk * 4 * UB + r, SPAN + (blk - 1) * 4 * UB + r, blk == 0
    return u, u, True


def _unit_slices(gi, u):
    dil = DILATIONS[gi]
    qb0, kb0, first = _unit_bases(gi, u)
    if dil == 1:
        return pl.ds(pl.multiple_of(qb0, UB), UB), pl.ds(pl.multiple_of(kb0, UB), 2 * UB), first
    return pl.ds(qb0, UB, stride=dil), pl.ds(kb0, 2 * UB, stride=dil), first


def _bdot(a, b):
    return lax.dot_general(a, b, (((2,), (1,)), ((0,), (0,))), preferred_element_type=F32)


def _bdot_nt(a, b):
    return lax.dot_general(a, b, (((2,), (2,)), ((0,), (0,))), preferred_element_type=F32)


def _bdot_tn(a, b):
    return lax.dot_general(a, b, (((1,), (1,)), ((0,), (0,))), preferred_element_type=F32)


def _band_mask(first_in_span, has_prev):
    qi = lax.broadcasted_iota(jnp.int32, (UB, 2 * UB), 0)
    ki = lax.broadcasted_iota(jnp.int32, (UB, 2 * UB), 1)
    dist = UB + qi - ki
    band = (dist >= 0) & (dist <= UB)
    return band & ((ki >= UB) | jnp.logical_not(first_in_span) | has_prev)


def _attn_fwd(p, rc, rsa, rsb):
    s = p.shape[1]
    ns = s // SPAN
    nunit = SPAN // UB

    def body(q_ref, k_ref, v_ref, z_ref, c_ref, sa_ref, sb_ref, o_ref, lse_ref, ga_ref,
             qr, kf, vf, acc, mm, ll):
        n = pl.program_id(1)

        @pl.when(n == 0)
        def _():
            kf[0:SPAN] = jnp.zeros((SPAN, DH), F32)
            vf[0:SPAN] = jnp.zeros((SPAN, DH), F32)

        c, sa, sb = c_ref[...], sa_ref[...], sb_ref[...]
        qr[...] = _rope(q_ref[0].astype(F32), c, sa, sb)
        kf[SPAN:] = _rope(k_ref[0].astype(F32), c, sa, sb)
        vf[SPAN:] = v_ref[0].astype(F32)
        has_prev = n > 0

        for gi, dil in enumerate(DILATIONS):
            def trip(t, carry, gi=gi, dil=dil):
                qsls, ksls, firsts = [], [], []
                for b in range(UNIT_BATCH):
                    qsl, ksl, first = _unit_slices(gi, t * UNIT_BATCH + b)
                    qsls.append(qsl)
                    ksls.append(ksl)
                    firsts.append(first)
                qb = jnp.stack([qr[qsl, :].astype(BF16) for qsl in qsls])
                kb = jnp.stack([kf[ksl, :].astype(BF16) for ksl in ksls])
                vb = jnp.stack([vf[ksl, :].astype(BF16) for ksl in ksls])
                s_all = _bdot_nt(qb, kb)
                prs = []
                for b in range(UNIT_BATCH):
                    sc = jnp.where(_band_mask(firsts[b], has_prev), s_all[b] * SCALE, NEG)
                    m = jnp.max(sc, axis=-1, keepdims=True)
                    pr = jnp.exp(sc - m)
                    l = jnp.sum(pr, axis=-1, keepdims=True)
                    mm[gi, qsls[b], :] = jnp.broadcast_to(m, (UB, DH))
                    ll[gi, qsls[b], :] = jnp.broadcast_to(l, (UB, DH))
                    prs.append(pr.astype(BF16))
                o_all = _bdot(jnp.stack(prs), vb)
                for b in range(UNIT_BATCH):
                    acc[gi, qsls[b], :] = o_all[b]
                return carry

            lax.fori_loop(0, nunit // UNIT_BATCH, trip, 0)

        m_all =jnp.maximum(jnp.maximum(mm[0], mm[1]), mm[2])
        num = jnp.zeros((SPAN, DH), F32)
        den = jnp.zeros((SPAN, DH), F32)
        for gi in range(3):
            w = jnp.exp(mm[gi] - m_all)
            num = num + w * acc[gi]
            den = den + w * ll[gi]
        o = num / den
        o_ref[...] = o
        lse_ref[...] = m_all + jnp.log(den)
        z = z_ref[0].astype(F32)
        ga_ref[...] = (o * (z * _sigmoid(z))).astype(BF16)
        kf[0:SPAN] = kf[SPAN:]
        vf[0:SPAN] = vf[SPAN:]

    blk = lambda piece, off: pl.BlockSpec((1, SPAN, DH), lambda hh, n: (piece, n, off + hh))
    tab = pl.BlockSpec((SPAN, DH), lambda hh, n: (n, 0))
    outb = pl.BlockSpec((SPAN, DH), lambda hh, n: (n, hh))
    return pl.pallas_call(
        body, name="attn_fwd", grid=(H, ns),
        in_specs=[blk(1, 0), blk(1, H), blk(2, 0), blk(2, H), tab, tab, tab],
        out_specs=(outb, outb, outb),
        out_shape=(jax.ShapeDtypeStruct((s, D), F32), jax.ShapeDtypeStruct((s, D), F32),
                   jax.ShapeDtypeStruct((s, D), BF16)),
        scratch_shapes=[pltpu.VMEM((SPAN, DH), F32), pltpu.VMEM((2 * SPAN, DH), F32), pltpu.VMEM((2 * SPAN, DH), F32),
                        pltpu.VMEM((3, SPAN, DH), F32), pltpu.VMEM((3, SPAN, DH), F32), pltpu.VMEM((3, SPAN, DH), F32)],
        compiler_params=_params(("parallel", "arbitrary")),
    )(p, p, p, p, rc, rsa, rsb)


def _attn_bwd(p, o, lse, dga, rc, rsa, rsb):
    s = p.shape[1]
    ns = s // SPAN
    nunit = SPAN // UB

    def body(q_ref, k_ref, kp_ref, v_ref, vp_ref, z_ref, c_ref, sa_ref, sb_ref, cp_ref, sap_ref, sbp_ref,
             o_ref, lse_ref, dg_ref, dq_ref, dk_ref, dv_ref, dz_ref,
             qr, kf, vf, dof, dlt, dqa, dkf, dvf):
        step = pl.program_id(1)
        n = ns - 1 - step
        has_prev = n > 0

        @pl.when(step == 0)
        def _():
            dkf[...] = jnp.zeros_like(dkf)
            dvf[...] = jnp.zeros_like(dvf)

        @pl.when(step > 0)
        def _():
            dkf[SPAN:] = dkf[0:SPAN]
            dvf[SPAN:] = dvf[0:SPAN]
            dkf[0:SPAN] = jnp.zeros((SPAN, DH), F32)
            dvf[0:SPAN] = jnp.zeros((SPAN, DH), F32)

        c, sa, sb = c_ref[...], sa_ref[...], sb_ref[...]
        qr[...] = _rope(q_ref[0].astype(F32), c, sa, sb)
        kf[SPAN:] = _rope(k_ref[0].astype(F32), c, sa, sb)
        vf[SPAN:] = v_ref[0].astype(F32)
        kf[0:SPAN] = jnp.where(has_prev, _rope(kp_ref[0].astype(F32), cp_ref[...], sap_ref[...], sbp_ref[...]), 0.0)
        vf[0:SPAN] = jnp.where(has_prev, vp_ref[0].astype(F32), 0.0)
        z = z_ref[0].astype(F32)
        sz = _sigmoid(z)
        dg = dg_ref[...]
        ov = o_ref[...]
        do = dg * (z * sz)
        dz_ref[...] = (dg * ov * (sz * (1.0 + z * (1.0 - sz)))).astype(BF16)
        dof[...] = do
        dlt[...] = jnp.broadcast_to(jnp.sum(do * ov, axis=-1, keepdims=True), (SPAN, DH))
        dqa[...] = jnp.zeros_like(dqa)

        for gi, dil in enumerate(DILATIONS):
            def trip(t, carry, gi=gi, dil=dil):
                qsls, ksls, firsts = [], [], []
                for b in range(UNIT_BATCH):
                    qsl, ksl, first = _unit_slices(gi, t * UNIT_BATCH + b)
                    qsls.append(qsl)
                    ksls.append(ksl)
                    firsts.append(first)
                qb = jnp.stack([qr[qsl, :].astype(BF16) for qsl in qsls])
                kb = jnp.stack([kf[ksl, :].astype(BF16) for ksl in ksls])
                vb = jnp.stack([vf[ksl, :].astype(BF16) for ksl in ksls])
                dob = jnp.stack([dof[qsl, :].astype(BF16) for qsl in qsls])
                s_all = _bdot_nt(qb, kb)
                dp_all = _bdot_nt(dob, vb)
                prs, dss = [], []
                for b in range(UNIT_BATCH):
                    lse_b = lse_ref[qsls[b], :]
                    dl_b = dlt[qsls[b], :]
                    pr = jnp.exp(s_all[b] * SCALE - jnp.concatenate([lse_b, lse_b], axis=1))
                    pr = jnp.where(_band_mask(firsts[b], has_prev), pr, 0.0)
                    prs.append(pr.astype(BF16))
                    dss.append((pr * (dp_all[b] - jnp.concatenate([dl_b, dl_b], axis=1)) * SCALE).astype(BF16))
                ds_all = jnp.stack(dss)
                dv_all = _bdot_tn(jnp.stack(prs), dob)
                dq_all = _bdot(ds_all, kb)
                dk_all = _bdot_tn(ds_all, qb)
                for b in range(UNIT_BATCH):
                    dvf[ksls[b], :] += dv_all[b]
                    dqa[qsls[b], :] += dq_all[b]
                    dkf[ksls[b], :] += dk_all[b]
                return carry

            lax.fori_loop(0, nunit // UNIT_BATCH, trip, 0)

        dq_ref[...] = _rope_bwd(dqa[...], c, sa, sb).astype(BF16)
        dk_ref[...] = _rope_bwd(dkf[SPAN:], c, sa, sb).astype(BF16)
        dv_ref[...] = dvf[SPAN:].astype(BF16)

    rn = lambda n: ns - 1 - n
    pn = lambda n: jnp.maximum(ns - 2 - n, 0)
    blk = lambda piece, off: pl.BlockSpec((1, SPAN, DH), lambda hh, n: (piece, rn(n), off + hh))
    blkp = lambda piece, off: pl.BlockSpec((1, SPAN, DH), lambda hh, n: (piece, pn(n), off + hh))
    tab = pl.BlockSpec((SPAN, DH), lambda hh, n: (rn(n), 0))
    tabp = pl.BlockSpec((SPAN, DH), lambda hh, n: (pn(n), 0))
    io = pl.BlockSpec((SPAN, DH), lambda hh, n: (rn(n), hh))
    return pl.pallas_call(
        body, name="attn_bwd", grid=(H, ns),
        in_specs=[blk(1, 0), blk(1, H), blkp(1, H), blk(2, 0), blkp(2, 0), blk(2, H),
                  tab, tab, tab, tabp, tabp, tabp, io, io, io],
        out_specs=(io, io, io, io),
        out_shape=tuple(jax.ShapeDtypeStruct((s, D), BF16) for _ in range(4)),
        scratch_shapes=[pltpu.VMEM((SPAN, DH), F32), pltpu.VMEM((2 * SPAN, DH), F32), pltpu.VMEM((2 * SPAN, DH), F32),
                        pltpu.VMEM((SPAN, DH), F32), pltpu.VMEM((SPAN, DH), F32), pltpu.VMEM((SPAN, DH), F32),
                        pltpu.VMEM((2 * SPAN, DH), F32), pltpu.VMEM((2 * SPAN, DH), F32)],
        compiler_params=_params(("parallel", "arbitrary")),
    )(p, p, p, p, p, p, rc, rsa, rsb, rc, rsa, rsb, o, lse, dga)


def _tail(gr, ga, p, x, tgt, w3, b_gate, gate, g_final, tm=256):
    s = x.shape[0]
    nt = s // tm

    def body(gr_ref, ga_ref, pr_ref, pa_ref, x_ref, t_ref, bg_ref, gate_ref, gf_ref, w_hbm,
             dgr_ref, dga_ref, dc_ref, dx2_ref, vec_ref, go_hbm, w_s, acc_s, sem):
        i = pl.program_id(0)

        @pl.when(i == 0)
        def _():
            cp = pltpu.make_async_copy(w_hbm, w_s, sem.at[12])
            cp.start()
            acc_s[...] = jnp.zeros_like(acc_s)
            vec_ref[...] = jnp.zeros_like(vec_ref)
            cp.wait()

        grb = gr_ref[...]
        gab = ga_ref[...]
        bg = bg_ref[...]
        gate_v = gate_ref[...]
        gf = gf_ref[...]
        y_r = _dot(grb, w_s[0])
        y_a = _dot(gab, w_s[1])
        sr = _sigmoid(pr_ref[0].astype(F32) + bg[:, :D])
        sa = _sigmoid(pa_ref[0].astype(F32) + bg[:, D:])
        mb = (sr * y_r + sa * y_a).astype(BF16)
        u = _dot(mb, w_s[2])
        x2 = x_ref[...] + gate_v * u
        rstd = lax.rsqrt(jnp.mean(x2 * x2, axis=-1, keepdims=True) + EPS)
        xh = x2 * rstd
        e = xh * gf - t_ref[...]
        dy = e * (1.0 / D)
        dyg = dy * gf
        dx2 = rstd * (dyg - xh * jnp.mean(dyg * xh, axis=-1, keepdims=True))
        dx2_ref[...] = dx2
        dub = (dx2 * gate_v).astype(BF16)
        dm = _dot_nt(dub, w_s[2])
        dyr = (dm * sr).astype(BF16)
        dya = (dm * sa).astype(BF16)
        dpr = dm * y_r * (sr * (1.0 - sr))
        dpa = dm * y_a * (sa * (1.0 - sa))
        dc_ref[:, :D] = dpr.astype(BF16)
        dc_ref[:, D:] = dpa.astype(BF16)
        dgr_ref[...] = _dot_nt(dyr, w_s[0])
        dga_ref[...] = _dot_nt(dya, w_s[1])
        acc_s[0] += _dot_tn(grb, dyr)
        acc_s[1] += _dot_tn(gab, dya)
        acc_s[2] += _dot_tn(mb, dub)
        vec_ref[0:1, :] += _colsum(dy * xh)
        vec_ref[1:2, :] += _colsum(dx2 * u)
        vec_ref[2:3, :] += _colsum(dpr)
        vec_ref[3:4, :] += _colsum(dpa)
        vec_ref[4:5, :] += _colsum(e * e)

        @pl.when(i == nt - 1)
        def _():
            vec_ref[4:5, :] = jnp.broadcast_to(jnp.sum(vec_ref[4:5, :]) * (0.5 / D), (1, D))
            cps = []
            for w in range(3):
                for j in range(4):
                    cps.append(pltpu.make_async_copy(acc_s.at[w, pl.ds(256 * j, 256)],
                                                     go_hbm.at[j, pl.ds(256 * w, 256)], sem.at[4 * w + j]))
            for cp in cps:
                cp.start()
            for cp in cps:
                cp.wait()

    rowt = lambda i: (i, 0)
    row = lambda w: pl.BlockSpec((1, w), lambda i: (0, 0))
    any_ = pl.BlockSpec(memory_space=pl.ANY)
    return pl.pallas_call(
        body, name="tail", grid=(nt,),
        in_specs=[pl.BlockSpec((tm, D), rowt), pl.BlockSpec((tm, D), rowt),
                  pl.BlockSpec((1, tm, D), lambda i: (3, i, 0)), pl.BlockSpec((1, tm, D), lambda i: (3, i, 1)),
                  pl.BlockSpec((tm, D), rowt), pl.BlockSpec((tm, D), rowt),
                  row(2 * D), row(D), row(D), any_],
        out_specs=(pl.BlockSpec((tm, D), rowt), pl.BlockSpec((tm, D), rowt), pl.BlockSpec((tm, 2 * D), rowt),
                   pl.BlockSpec((tm, D), rowt), pl.BlockSpec((8, D), lambda i: (0, 0)), any_),
        out_shape=(jax.ShapeDtypeStruct((s, D), F32), jax.ShapeDtypeStruct((s, D), F32),
                   jax.ShapeDtypeStruct((s, 2 * D), BF16), jax.ShapeDtypeStruct((s, D), F32),
                   jax.ShapeDtypeStruct((8, D), F32), jax.ShapeDtypeStruct((4, 768, D), F32)),
        scratch_shapes=[pltpu.VMEM((3, D, D), BF16), pltpu.VMEM((3, D, D), F32), pltpu.SemaphoreType.DMA((13,))],
        compiler_params=_params(("arbitrary",)),
    )(gr, ga, p, p, x, tgt, b_gate, gate, g_final, w3)


def _pieces_steps(pieces):
    out, s0 = [], 0
    for a in pieces:
        n = a.shape[1] // D
        out.append((s0, n))
        s0 += n
    return out, s0


def _inproj_bwd_x(pieces, wg, x, dx2, gn, scale, sums, tm=512):
    s = x.shape[0]
    np_ = len(pieces)
    na = len(sums)
    ni = s // tm
    groups, cur, width = [], [], 0
    for t, a in enumerate(pieces):
        cur.append(t)
        width += a.shape[1]
        if width == PW:
            groups.append(cur)
            cur, width = [], 0
    assert len(groups) == 4 and not cur

    def body(*refs):
        d_refs = refs[:np_]
        w_hbm, x_ref, dx2_ref, gn_ref, sc_ref = refs[np_:np_ + 5]
        q_refs = refs[np_ + 5:np_ + 5 + na]
        gx_ref, vec_ref = refs[np_ + 5 + na:np_ + 7 + na]
        r_refs = refs[np_ + 7 + na:np_ + 7 + 2 * na]
        w_s, wsem, ss, rs = refs[np_ + 7 + 2 * na:]
        i = pl.program_id(0)

        def scatter_copies():
            cx, cy, cc = _coords()
            j = 2 * cx + cy
            cps = []
            for t, (q, r) in enumerate(zip(q_refs, r_refs)):
                for e, (kx, ky) in enumerate(((1, 0), (0, 1), (1, 1))):
                    cps.append(_rcopy(q.at[j ^ (2 * kx + ky)], r.at[e], ss.at[3 * t + e], rs.at[3 * t + e],
                                      (_flip(cx, kx), _flip(cy, ky), cc)))
            return cps

        def w_copy(pc):
            return pltpu.make_async_copy(w_hbm.at[pc, pl.ds(0, D), :], w_s.at[pc], wsem.at[pc])

        @pl.when(i == 0)
        def _():
            for pc in range(4):
                w_copy(pc).start()
            vec_ref[...] = jnp.zeros_like(vec_ref)
            for cp in scatter_copies():
                cp.start()

        dh = None
        for pc, group in enumerate(groups):
            @pl.when(i == 0)
            def _(pc=pc):
                w_copy(pc).wait()

            tiles = [d_refs[t][...] for t in group]
            lhs = tiles[0] if len(tiles) == 1 else jnp.concatenate(tiles, axis=1)
            part = _dot_nt(lhs, w_s[pc])
            dh = part if dh is None else dh + part

        xt = x_ref[...]
        rstd = lax.rsqrt(jnp.mean(xt * xt, axis=-1, keepdims=True) + EPS)
        xh = xt * rstd
        gn_v = gn_ref[...]
        sc1 = 1.0 + sc_ref[...]
        dhx = dh * xh
        vec_ref[0:1, :] += _colsum(dh)
        vec_ref[1:2, :] += _colsum(dhx) * gn_v
        vec_ref[2:3, :] += _colsum(dhx) * sc1
        dxh = dh * (gn_v * sc1)
        gx_ref[...] = rstd * (dxh - xh * jnp.mean(dxh * xh, axis=-1, keepdims=True)) + dx2_ref[...]

        @pl.when(i == ni - 1)
        def _():
            for cp in scatter_copies():
                cp.wait()

    rowt = lambda i: (i, 0)
    row = pl.BlockSpec((1, D), lambda i: (0, 0))
    any_ = pl.BlockSpec(memory_space=pl.ANY)
    outs = pl.pallas_call(
        body, name="inproj_bwd_x", grid=(ni,),
        in_specs=[pl.BlockSpec((tm, a.shape[1]), rowt) for a in pieces] +
                 [any_, pl.BlockSpec((tm, D), rowt), pl.BlockSpec((tm, D), rowt), row, row] + [any_] * na,
        out_specs=(pl.BlockSpec((tm, D), rowt), pl.BlockSpec((8, D), lambda i: (0, 0))) + (any_,) * na,
        out_shape=(jax.ShapeDtypeStruct((s, D), F32), jax.ShapeDtypeStruct((8, D), F32)) +
                  tuple(jax.ShapeDtypeStruct((3,) + q.shape[1:], q.dtype) for q in sums),
        scratch_shapes=[pltpu.VMEM((4, D, PW), BF16), pltpu.SemaphoreType.DMA((4,)),
                        pltpu.SemaphoreType.DMA((3 * na,)), pltpu.SemaphoreType.DMA((3 * na,))],
        compiler_params=_params(("arbitrary",)),
    )(*pieces, wg, x, dx2, gn, scale, *sums)
    return outs[0], outs[1], outs[2:]


def _inproj_bwd_w(pieces, hbf, tk=1024):
    s = hbf.shape[0]
    steps, nk = _pieces_steps(pieces)
    npc = PW // D
    ns = s // tk
    np_ = len(pieces)

    def body(*refs):
        d_refs = refs[:np_]
        h_ref, g_ref = refs[np_:]
        cb, k = pl.program_id(0), pl.program_id(1)

        @pl.when(k == 0)
        def _():
            g_ref[...] = jnp.zeros_like(g_ref)

        for (s0, n), d_ref in zip(steps, d_refs):
            @pl.when((cb >= s0) & (cb < s0 + n))
            def _(d_ref=d_ref):
                g_ref[0] += _dot_tn(h_ref[...], d_ref[...])

    def piece_spec(s0, n):
        def imap(cb, k):
            active = (cb >= s0) & (cb < s0 + n)
            return (jnp.where(active, k, 0), jnp.clip(cb - s0, 0, n - 1))
        return pl.BlockSpec((tk, D), imap)

    return pl.pallas_call(
        body, name="inproj_bwd_w", grid=(nk, ns),
        in_specs=[piece_spec(s0, n) for s0, n in steps] + [pl.BlockSpec((tk, D), lambda cb, k: (k, 0))],
        out_specs=pl.BlockSpec((1, D, D), lambda cb, k: (cb // npc, 0, cb % npc)),
        out_shape=jax.ShapeDtypeStruct((4, D, PW), F32),
        compiler_params=_params(("parallel", "arbitrary")),
    )(*pieces, hbf)


D2D_CHUNK_BYTES = 512 * 1024


def _chunk_rows(a):
    return max(8, D2D_CHUNK_BYTES // (a.shape[-1] * a.dtype.itemsize))


def _pair_exchange(arrs):
    na = len(arrs)
    chunks = []
    for t, a in enumerate(arrs):
        hr = a.shape[1] // 2
        cr = _chunk_rows(a)
        chunks += [(t, j, r0, cr) for j in range(a.shape[0]) for r0 in range(0, hr, cr)]
    nch = len(chunks)

    def body(*refs):
        a_refs = refs[:na]
        rb_refs = refs[na:2 * na]
        ss, rs = refs[2 * na:]
        x, y, c = _coords()
        sib = (x, y, 1 - c)
        rcs = []
        for n, (t, j, r0, cr) in enumerate(chunks):
            hr = a_refs[t].shape[1] // 2
            rc = _rcopy(a_refs[t].at[j, pl.ds((1 - c) * hr + r0, cr), :], rb_refs[t].at[j, pl.ds(r0, cr), :],
                        ss.at[n], rs.at[n], sib)
            rc.start()
            rcs.append(rc)
        for rc in rcs:
            rc.wait_recv()
        for rc in rcs:
            rc.wait_send()

    any_ = pl.BlockSpec(memory_space=pl.ANY)
    halves = [jax.ShapeDtypeStruct((a.shape[0], a.shape[1] // 2, a.shape[2]), a.dtype) for a in arrs]
    return pl.pallas_call(
        body, name="pair_exchange",
        out_shape=tuple(halves),
        in_specs=[any_] * na, out_specs=tuple([any_] * na),
        scratch_shapes=[pltpu.SemaphoreType.DMA((nch,)), pltpu.SemaphoreType.DMA((nch,))],
        compiler_params=_params(),
    )(*arrs)


def _pair_swap(arrs):
    na = len(arrs)
    chunks = []
    for t, a in enumerate(arrs):
        cr = _chunk_rows(a)
        chunks += [(t, r0, cr) for r0 in range(0, a.shape[0], cr)]
    nch = len(chunks)

    def body(*refs):
        a_refs = refs[:na]
        o_refs = refs[na:2 * na]
        ss, rs = refs[2 * na:]
        x, y, c = _coords()
        sib = (x, y, 1 - c)
        rcs = []
        for n, (t, r0, cr) in enumerate(chunks):
            rows = pl.ds(r0, cr)
            rc = _rcopy(a_refs[t].at[rows, :], o_refs[t].at[rows, :], ss.at[n], rs.at[n], sib)
            rc.start()
            rcs.append(rc)
        for rc in rcs:
            rc.wait_recv()
        for rc in rcs:
            rc.wait_send()

    any_ = pl.BlockSpec(memory_space=pl.ANY)
    return pl.pallas_call(
        body, name="pair_swap",
        out_shape=tuple(jax.ShapeDtypeStruct(a.shape, a.dtype) for a in arrs),
        in_specs=[any_] * na, out_specs=tuple([any_] * na),
        scratch_shapes=[pltpu.SemaphoreType.DMA((nch,)), pltpu.SemaphoreType.DMA((nch,))],
        compiler_params=_params(),
    )(*arrs)


def _add_half(full, rb, core, tr):
    n, r, cdim = full.shape
    nb = r // 2 // tr

    def body(c_ref, a_ref, b_ref, o_ref, ob_ref):
        tot = a_ref[...] + b_ref[...]
        o_ref[...] = tot
        ob_ref[...] = tot.astype(BF16)

    mine = pl.BlockSpec((1, tr, cdim), lambda i, j, c_ref: (i, c_ref[0] * nb + j, 0))
    spec = pl.BlockSpec((1, tr, cdim), lambda i, j, c_ref: (i, j, 0))
    return pl.pallas_call(
        body, name="add_half",
        grid_spec=pltpu.PrefetchScalarGridSpec(num_scalar_prefetch=1, grid=(n, nb), in_specs=[mine, spec],
                                               out_specs=(spec, spec)),
        out_shape=(jax.ShapeDtypeStruct(rb.shape, rb.dtype), jax.ShapeDtypeStruct(rb.shape, BF16)),
        compiler_params=_params(("parallel", "parallel")),
    )(core, full, rb)


def _sum_slots(q, r3, shard, tr):
    _, r, cdim = q.shape

    def body(j_ref, q_ref, r_ref, o_ref):
        o_ref[...] = ((q_ref[0] + r_ref[0].astype(F32)) + r_ref[1].astype(F32)) + r_ref[2].astype(F32)

    return pl.pallas_call(
        body, name="sum_slots",
        grid_spec=pltpu.PrefetchScalarGridSpec(
            num_scalar_prefetch=1, grid=(r // tr,),
            in_specs=[pl.BlockSpec((1, tr, cdim), lambda i, j_ref: (j_ref[0], i, 0)),
                      pl.BlockSpec((3, tr, cdim), lambda i, j_ref: (0, i, 0))],
            out_specs=pl.BlockSpec((tr, cdim), lambda i, j_ref: (i, 0))),
        out_shape=jax.ShapeDtypeStruct((r, cdim), q.dtype),
        compiler_params=_params(("parallel",)),
    )(shard, q, r3)


def _allreduce_small(packs):
    na = len(packs)

    def body(*refs):
        p_refs, o_refs, rbufs = refs[:na], refs[na:2 * na], refs[2 * na:3 * na]
        s1, r1, s2, r2 = refs[3 * na:]
        me = _my_index()
        sends = []

        def chunk(t, d):
            ch = p_refs[t].shape[0] // NDEV
            return pl.ds(pl.multiple_of(d * ch, 8), ch)

        for t in range(na):
            for k in range(1, NDEV):
                e = 7 * t + k - 1
                cp = _rcopy(p_refs[t].at[chunk(t, me ^ k)], rbufs[t].at[me], s1.at[e], r1.at[e], _peer(k))
                cp.start()
                sends.append(cp)
            rbufs[t][me] = p_refs[t][chunk(t, me), :]
        for t in range(na):
            for k in range(1, NDEV):
                e = 7 * t + k - 1
                _rcopy(p_refs[t].at[chunk(t, me)], rbufs[t].at[me ^ k], s1.at[e], r1.at[e], _peer(k)).wait_recv()
            tot = rbufs[t][0]
            for d in range(1, NDEV):
                tot = tot + rbufs[t][d]
            o_refs[t][chunk(t, me), :] = tot
            for k in range(1, NDEV):
                e = 7 * t + k - 1
                cp = _rcopy(o_refs[t].at[chunk(t, me)], o_refs[t].at[chunk(t, me)], s2.at[e], r2.at[e], _peer(k))
                cp.start()
                sends.append(cp)
        for t in range(na):
            for k in range(1, NDEV):
                e = 7 * t + k - 1
                _rcopy(o_refs[t].at[chunk(t, me)], o_refs[t].at[chunk(t, me ^ k)], s2.at[e], r2.at[e],
                       _peer(k)).wait_recv()
        for cp in sends:
            cp.wait_send()

    vm = pl.BlockSpec(memory_space=pltpu.VMEM)
    return pl.pallas_call(
        body, name="allreduce_small",
        out_shape=tuple(jax.ShapeDtypeStruct(a.shape, F32) for a in packs),
        in_specs=[vm] * na, out_specs=tuple([vm] * na),
        scratch_shapes=[pltpu.VMEM((NDEV, a.shape[0] // NDEV, a.shape[1]), F32) for a in packs] +
                       [pltpu.SemaphoreType.DMA((7 * na,)) for _ in range(4)],
        compiler_params=_params(),
    )(*packs)


def _adamw_update(w, g, m, v):
    nm = B1 * m + (1.0 - B1) * g
    nv = B2 * v + (1.0 - B2) * (g * g)
    m_hat = nm / (1.0 - B1 ** STEP)
    v_hat = nv / (1.0 - B2 ** STEP)
    return -LR * (m_hat / (jnp.sqrt(v_hat) + ADAM_EPS) + WD * w), nm, nv


def _adamw(w, g, m, v, tr):
    r, cdim = w.shape

    def body(w_ref, g_ref, m_ref, v_ref, d_ref, nm_ref, nv_ref):
        d_ref[...], nm_ref[...], nv_ref[...] = _adamw_update(w_ref[...], g_ref[...], m_ref[...], v_ref[...])

    spec = pl.BlockSpec((tr, cdim), lambda i: (i, 0))
    sd = jax.ShapeDtypeStruct((r, cdim), F32)
    return pl.pallas_call(
        body, name="adamw", grid=(r // tr,), in_specs=[spec] * 4, out_specs=(spec,) * 3, out_shape=(sd,) * 3,
        compiler_params=_params(("parallel",)),
    )(w, g, m, v)


V_G_NORM, V_DMOD, V_B_GATE, V_CONV_B, V_LAM, V_G_FINAL, V_CONV_W, V_LOSS, V_ROWS = 0, 1, 4, 6, 7, 8, 9, 13, 64
M_W_A, M_W_X, M_B_A, M_B_X, M_ROWS = 0, H * DH, 2 * H * DH, 2 * H * DH + H, 2112
SMALL = ("g_norm", "b_mod", "b_gate", "conv_b", "lam", "g_final", "conv_w", "w_a", "w_x", "b_a", "b_x")


def _adamw_small(redv, redm, g_conv, wmv):
    def grad(name, rv, rm, gc):
        if name == "g_norm":
            return rv[V_G_NORM:V_G_NORM + 1, :]
        if name == "b_mod":
            return jnp.concatenate([rv[V_DMOD + t:V_DMOD + t + 1, :] for t in range(3)], axis=1)
        if name == "b_gate":
            return jnp.concatenate([rv[V_B_GATE + t:V_B_GATE + t + 1, :] for t in range(2)], axis=1)
        if name == "conv_b":
            return rv[V_CONV_B:V_CONV_B + 1, :]
        if name == "lam":
            return rv[V_LAM:V_LAM + 1, :]
        if name == "g_final":
            return rv[V_G_FINAL:V_G_FINAL + 1, :]
        if name == "conv_w":
            return gc[...]
        if name == "w_a":
            return rm[M_W_A:M_W_A + H * DH, :]
        if name == "w_x":
            return rm[M_W_X:M_W_X + H * DH, :]
        if name == "b_a":
            return rm[M_B_A:M_B_A + H, :]
        return rm[M_B_X:M_B_X + H, :]

    n = len(SMALL)

    def body(*refs):
        rv, rm, gc = refs[:3]
        ins, outs = refs[3:3 + 3 * n], refs[3 + 3 * n:]
        for t, name in enumerate(SMALL):
            w_ref, m_ref, v_ref = ins[3 * t:3 * t + 3]
            g_out, d_out, m_out, v_out = outs[4 * t:4 * t + 4]
            g = grad(name, rv, rm, gc)
            g_out[...] = g
            d_out[...], m_out[...], v_out[...] = _adamw_update(w_ref[...], g, m_ref[...], v_ref[...])

    vm = pl.BlockSpec(memory_space=pltpu.VMEM)
    flat = [a for name in SMALL for a in wmv[name]]
    shapes = [jax.ShapeDtypeStruct(wmv[name][0].shape, F32) for name in SMALL for _ in range(4)]
    outs = pl.pallas_call(
        body, name="adamw_small", out_shape=tuple(shapes),
        in_specs=[vm] * (3 + len(flat)), out_specs=tuple([vm] * len(shapes)),
        compiler_params=_params(),
    )(redv, redm, g_conv, *flat)
    return {name: outs[4 * t:4 * t + 4] for t, name in enumerate(SMALL)}


def _rope_tables(positions):
    inv_freq = ROPE_THETA ** (-jnp.arange(0, ROT, 2, dtype=F32) / ROT)
    ang = positions.astype(F32)[:, None] * inv_freq
    cos, sin = jnp.cos(ang), jnp.sin(ang)
    n = positions.shape[0]
    half = ROT // 2
    rc = jnp.concatenate([cos, cos, jnp.ones((n, DH - ROT), F32)], axis=1)
    rsa = jnp.concatenate([-sin, jnp.zeros((n, DH - half), F32)], axis=1)
    rsb = jnp.concatenate([jnp.zeros((n, half), F32), sin, jnp.zeros((n, DH - ROT), F32)], axis=1)
    return rc, rsa, rsb


def kernel(x, c, positions, g_norm, w_mod, b_mod, w_in, b_gate, conv_w, conv_b, w_a, b_a, w_x, b_x, lam, w_out_rnn, w_out_attn, w_o, g_final, loss_target, m_g_norm, m_w_mod, m_b_mod, m_w_in, m_b_gate, m_conv_w, m_conv_b, m_w_a, m_b_a, m_w_x, m_b_x, m_lam, m_w_out_rnn, m_w_out_attn, m_w_o, m_g_final, v_g_norm, v_w_mod, v_b_mod, v_w_in, v_b_gate, v_conv_w, v_conv_b, v_w_a, v_b_a, v_w_x, v_b_x, v_lam, v_w_out_rnn, v_w_out_attn, v_w_o, v_g_final):
    s = x.shape[1]
    xi = lax.axis_index("x")
    yi = lax.axis_index("y")
    ci = lax.axis_index("c")
    shard = 2 * xi + yi
    x2d = x[0]
    tgt = loss_target[0]
    pos = positions[0]

    c_all, mod4, conv_all = _mod_fwd(c, w_mod[0], b_mod.reshape(4, 1, 768), conv_w[0])
    mod = mod4.reshape(1, 3 * D)
    shift, scale, gate = mod[:, :D], mod[:, D:2 * D], mod[:, 2 * D:]
    w3_sh = jnp.stack([w_out_rnn[0], w_out_attn[0], w_o[0]]).astype(BF16)
    wsh = jnp.concatenate([w_in[0].astype(BF16), w3_sh.reshape(384, PW)], axis=0)
    conv_full = conv_all[0::2].transpose(1, 0, 2).reshape(4, D)

    order = jnp.stack([shard, shard ^ 2, shard ^ 1, shard ^ 3]).astype(jnp.int32)
    p, hbf, wg = _gather_norm_inproj(x2d, g_norm, shift, scale, wsh, order)
    wg = lax.dynamic_update_slice(wg, wsh[None], (shard, 0, 0))
    w3 = wg[:, D:, :].reshape(4, 3, 256, D).transpose(1, 0, 2, 3).reshape(3, D, D)
    rc, rsa, rsb = _rope_tables(pos)
    pos_col = pos.reshape(s, 1)
    b_a3, b_x3 = b_a.reshape(H, 1, DH), b_x.reshape(H, 1, DH)
    hr, gr = _rnn_fwd(p, pos_col, conv_full, conv_b, w_a[0], b_a3, w_x[0], b_x3, lam)
    o, lse, ga = _attn_fwd(p, rc, rsa, rsb)

    dgr, dga, dc, dx2, vec_t, g_out = _tail(gr, ga, p, x2d, tgt, w3, b_gate, gate, g_final.reshape(1, D))

    dxr, dzr, g_wa, g_ba, g_wx, g_bx, g_lam, g_cw, g_cb = _rnn_bwd(
        p, hr, dgr, pos_col, conv_full, conv_b, w_a[0], b_a3, w_x[0], b_x3, lam)
    dq, dk, dv, dza = _attn_bwd(p, o, lse, dga, rc, rsa, rsb)

    pieces = [dxr, dzr, dq, dk, dv, dza, dc]
    g_win = _inproj_bwd_w(pieces, hbf)

    core = ci.reshape(1)
    shard1 = shard.reshape(1)
    rb_a, rb_b = _pair_exchange([g_win, g_out])
    (q_a, qh_a), (q_b, qh_b) = _add_half(g_win, rb_a, core, tr=256), _add_half(g_out, rb_b, core, tr=128)
    grad_x, vec_n, (r_a, r_b) = _inproj_bwd_x(pieces, wg, x2d, dx2, g_norm, scale, [qh_a, qh_b])
    f_a, f_b = _sum_slots(q_a, r_a, shard1, tr=256), _sum_slots(q_b, r_b, shard1, tr=128)
    s_a, s_b = _pair_swap([f_a, f_b])
    south = ci == 0
    grad_w_in = jnp.where(south, jnp.concatenate([f_a, s_a], axis=0), jnp.concatenate([s_a, f_a], axis=0))
    g3 = jnp.where(south, jnp.concatenate([f_b, s_b], axis=0), jnp.concatenate([s_b, f_b], axis=0)).reshape(3, 256, D)

    dmod_row = jnp.concatenate([vec_n[0:1], vec_n[1:2], vec_t[1:2]], axis=1)
    vpack = jnp.concatenate([
        vec_n[2:3],
        vec_n[0:2], vec_t[1:2],
        vec_t[2:4],
        g_cb.reshape(1, D),
        g_lam.reshape(1, D),
        vec_t[0:1],
        g_cw.transpose(1, 0, 2).reshape(4, D),
        vec_t[4:5],
        jnp.zeros((V_ROWS - 14, D), F32)], axis=0)
    mpack = jnp.concatenate([
        g_wa.reshape(H * DH, DH), g_wx.reshape(H * DH, DH), g_ba.reshape(H, DH), g_bx.reshape(H, DH),
        jnp.zeros((M_ROWS - 2 * H * DH - 2 * H, DH), F32)], axis=0)
    redv, redm = _allreduce_small([vpack, mpack])
    loss = redv[V_LOSS, 0]
    grad_w_mod = _mod_bwd(dmod_row.reshape(4, 1, 768), c_all)
    g_conv_sh = lax.dynamic_slice_in_dim(redv[V_CONV_W:V_CONV_W + 4], shard * 256, 256, axis=1)

    shape2d = dict(g_norm=(1, D), b_mod=(1, 3 * D), b_gate=(1, 2 * D), conv_b=(1, D), lam=(1, D), g_final=(1, D),
                   conv_w=(4, 256), w_a=(H * DH, DH), w_x=(H * DH, DH), b_a=(H, DH), b_x=(H, DH))
    given = dict(
        g_norm=(g_norm, m_g_norm, v_g_norm), b_mod=(b_mod, m_b_mod, v_b_mod), b_gate=(b_gate, m_b_gate, v_b_gate),
        conv_b=(conv_b, m_conv_b, v_conv_b), lam=(lam, m_lam, v_lam), g_final=(g_final, m_g_final, v_g_final),
        conv_w=(conv_w, m_conv_w, v_conv_w), w_a=(w_a, m_w_a, v_w_a), w_x=(w_x, m_w_x, v_w_x),
        b_a=(b_a, m_b_a, v_b_a), b_x=(b_x, m_b_x, v_b_x))
    small = _adamw_small(redv, redm, g_conv_sh,
                         {n: tuple(a.reshape(shape2d[n]) for a in given[n]) for n in SMALL})

    big_in = _adamw(w_in[0], grad_w_in, m_w_in[0], v_w_in[0], tr=256)
    big_mod = _adamw(w_mod[0], grad_w_mod, m_w_mod[0], v_w_mod[0], tr=256)
    w3f = jnp.concatenate([w_out_rnn[0], w_out_attn[0], w_o[0]], axis=0)
    m3f = jnp.concatenate([m_w_out_rnn[0], m_w_out_attn[0], m_w_o[0]], axis=0)
    v3f = jnp.concatenate([v_w_out_rnn[0], v_w_out_attn[0], v_w_o[0]], axis=0)
    big_out = _adamw(w3f, g3.reshape(768, D), m3f, v3f, tr=256)

    names = ["g_norm", "w_mod", "b_mod", "w_in", "b_gate", "conv_w", "conv_b", "w_a", "b_a", "w_x", "b_x", "lam",
             "w_out_rnn", "w_out_attn", "w_o", "g_final"]
    outs = []
    for idx in range(4):
        d = {n: small[n][idx].reshape(given[n][0].shape) for n in SMALL}
        if idx == 0:
            d.update(w_mod=grad_w_mod[None], w_in=grad_w_in[None],
                     w_out_rnn=g3[0][None], w_out_attn=g3[1][None], w_o=g3[2][None])
        else:
            d.update(w_mod=big_mod[idx - 1][None], w_in=big_in[idx - 1][None],
                     w_out_rnn=big_out[idx - 1][0:256][None], w_out_attn=big_out[idx - 1][256:512][None],
                     w_o=big_out[idx - 1][512:768][None])
        outs.append(d)
    flat = [d[n] for d in outs for n in names]
    return (loss, grad_x[None], *flat)
```

```python
import jax
import jax.numpy as jnp
from jax import lax
from jax.experimental import pallas as pl
from jax.experimental.pallas import tpu as pltpu

F32, BF16 = jnp.float32, jnp.bfloat16
MESH = pl.DeviceIdType.MESH
HIGHEST = lax.Precision.HIGHEST

D = 1024
H = 8
DH = 128
PW = 2048
EPS = 1e-6
LRU_C = 8.0
SCALE = DH ** -0.5
NEG = -1e30
SPAN = 2048
UB = 128
DILATIONS = (1, 4, 16)
UNIT_BATCH = 8
ROPE_THETA = 500000.0
ROT = 32

LR, B1, B2, ADAM_EPS, WD, STEP = 0.001, 0.9, 0.999, 1e-08, 0.01, 10

NDEV = 8


def _params(sem=None, vmem_mb=56):
    return pltpu.CompilerParams(dimension_semantics=sem, vmem_limit_bytes=vmem_mb * 2 ** 20)


def _coords():
    return lax.axis_index("x"), lax.axis_index("y"), lax.axis_index("c")


def _flip(v, bit):
    return 1 - v if bit else v


def _peer(k):
    x, y, c = _coords()
    return (_flip(x, (k >> 2) & 1), _flip(y, (k >> 1) & 1), _flip(c, k & 1))


def _my_index():
    x, y, c = _coords()
    return 4 * x + 2 * y + c


def _rcopy(src, dst, ssem, rsem, dev):
    return pltpu.make_async_remote_copy(src_ref=src, dst_ref=dst, send_sem=ssem, recv_sem=rsem,
                                        device_id=dev, device_id_type=MESH)


def _sigmoid(x):
    return jax.nn.sigmoid(x)


def _dot(a, b):
    return jnp.dot(a, b, preferred_element_type=F32)


def _dot_nt(a, b):
    return lax.dot_general(a, b, (((1,), (1,)), ((), ())), preferred_element_type=F32)


def _dot_tn(a, b):
    return lax.dot_general(a, b, (((0,), (0,)), ((), ())), preferred_element_type=F32)


def _colsum(a):
    return jnp.sum(a, axis=0, keepdims=True)


def _mod_fwd(c, w_mod_sh, b_mod4, conv_sh):
    def body(c_ref, w_ref, b_ref, cv_ref, call_ref, mod_ref, cvall_ref, rows_ref, cmat_ref, s1, r1, s2, r2, s3, r3):
        x, y, _ = _coords()
        me = _my_index()
        j = 2 * x + y
        call_ref[me] = c_ref[...]
        cvall_ref[me] = cv_ref[...]
        sends = []
        for k in range(1, NDEV):
            cp = _rcopy(call_ref.at[me], call_ref.at[me], s1.at[k - 1], r1.at[k - 1], _peer(k))
            cp.start()
            sends.append(cp)
            cp = _rcopy(cvall_ref.at[me], cvall_ref.at[me], s3.at[k - 1], r3.at[k - 1], _peer(k))
            cp.start()
            sends.append(cp)
        for k in range(1, NDEV):
            pk = me ^ k
            _rcopy(call_ref.at[pk], call_ref.at[pk], s1.at[k - 1], r1.at[k - 1], _peer(k)).wait_recv()
        for b in range(NDEV):
            cmat_ref[pl.ds(b, 1), :] = call_ref[b]
        cm = cmat_ref[...]
        act = cm * _sigmoid(cm)
        mp = jnp.dot(act, w_ref[...], preferred_element_type=F32, precision=HIGHEST) + b_ref[j]
        for b in range(NDEV):
            rows_ref[b] = mp[b:b + 1]
        mod_ref[j] = rows_ref[me]
        for q, k in enumerate((2, 4, 6)):
            cp = _rcopy(rows_ref.at[me ^ k], mod_ref.at[j], s2.at[q], r2.at[q], _peer(k))
            cp.start()
            sends.append(cp)
        for q, k in enumerate((2, 4, 6)):
            jq = j ^ (k >> 1)
            _rcopy(rows_ref.at[me], mod_ref.at[jq], s2.at[q], r2.at[q], _peer(k)).wait_recv()
        for k in range(1, NDEV):
            pk = me ^ k
            _rcopy(cvall_ref.at[pk], cvall_ref.at[pk], s3.at[k - 1], r3.at[k - 1], _peer(k)).wait_recv()
        for cp in sends:
            cp.wait_send()

    vm = pl.BlockSpec(memory_space=pltpu.VMEM)
    return pl.pallas_call(
        body, name="mod_fwd",
        out_shape=(jax.ShapeDtypeStruct((NDEV, 1, D), F32), jax.ShapeDtypeStruct((4, 1, 768), F32),
                   jax.ShapeDtypeStruct((NDEV,) + conv_sh.shape, F32)),
        in_specs=[vm, vm, vm, vm], out_specs=(vm, vm, vm),
        scratch_shapes=[pltpu.VMEM((NDEV, 1, 768), F32), pltpu.VMEM((NDEV, D), F32),
                        pltpu.SemaphoreType.DMA((7,)), pltpu.SemaphoreType.DMA((7,)),
                        pltpu.SemaphoreType.DMA((3,)), pltpu.SemaphoreType.DMA((3,)),
                        pltpu.SemaphoreType.DMA((7,)), pltpu.SemaphoreType.DMA((7,))],
        compiler_params=_params(),
    )(c, w_mod_sh, b_mod4, conv_sh)


def _mod_bwd(dmod4, c_all):
    def body(d_ref, call_ref, gw_ref, dall_ref, cmat_ref, dmat_ref, s1, r1):
        x, y, _ = _coords()
        me = _my_index()
        j = 2 * x + y
        dall_ref[me] = d_ref[...]
        sends = []
        for k in range(1, NDEV):
            cp = _rcopy(dall_ref.at[me], dall_ref.at[me], s1.at[k - 1], r1.at[k - 1], _peer(k))
            cp.start()
            sends.append(cp)
        for k in range(1, NDEV):
            pk = me ^ k
            _rcopy(dall_ref.at[pk], dall_ref.at[pk], s1.at[k - 1], r1.at[k - 1], _peer(k)).wait_recv()
        for cp in sends:
            cp.wait_send()
        for b in range(NDEV):
            cmat_ref[pl.ds(b, 1), :] = call_ref[b]
            dmat_ref[pl.ds(b, 1), :] = dall_ref[b, j]
        cm = cmat_ref[...]
        act = cm * _sigmoid(cm)
        gw_ref[...] = lax.dot_general(act, dmat_ref[...], (((0,), (0,)), ((), ())),
                                      preferred_element_type=F32, precision=HIGHEST)

    vm = pl.BlockSpec(memory_space=pltpu.VMEM)
    return pl.pallas_call(
        body, name="mod_bwd",
        out_shape=jax.ShapeDtypeStruct((D, 768), F32),
        in_specs=[vm, vm], out_specs=vm,
        scratch_shapes=[pltpu.VMEM((NDEV, 4, 1, 768), F32), pltpu.VMEM((NDEV, D), F32), pltpu.VMEM((NDEV, 768), F32),
                        pltpu.SemaphoreType.DMA((7,)), pltpu.SemaphoreType.DMA((7,))],
        compiler_params=_params(),
    )(dmod4, c_all)


def _gather_norm_inproj(x, gn, shift, scale, wsh, order, tm=1024, tn=512):
    s = x.shape[0]
    ni = s // tm
    npc = PW // tn
    rows, cols = wsh.shape
    half = rows // 2
    nch = 4
    cr = half // nch
    chips = ((1, 0), (0, 1), (1, 1))

    def body(ord_ref, x_ref, gn_ref, sh_ref, sc_ref, wsh_hbm, p_ref, h_ref, wg_hbm, hs_all, w_s, wsem, ss, rs):
        slot, i, col = pl.program_id(0), pl.program_id(1), pl.program_id(2)
        cx, cy, cc = _coords()
        j = 2 * cx + cy
        sib = (cx, cy, 1 - cc)
        mine = lambda n: pl.ds(cc * half + n * cr, cr)
        theirs = lambda n: pl.ds((1 - cc) * half + n * cr, cr)
        shard_of = lambda q: j ^ (2 * chips[q][0] + chips[q][1])

        def to_chip(q, n):
            e = nch * q + n
            return _rcopy(wsh_hbm.at[mine(n)], wg_hbm.at[j, mine(n)], ss.at[e], rs.at[e],
                          (_flip(cx, chips[q][0]), _flip(cy, chips[q][1]), cc))

        def from_chip(q, n):
            e = nch * q + n
            return _rcopy(wsh_hbm.at[mine(n)], wg_hbm.at[shard_of(q), mine(n)], ss.at[e], rs.at[e], sib)

        def to_sibling(q, n):
            e = 3 * nch + nch * q + n
            return _rcopy(wg_hbm.at[shard_of(q), mine(n)], wg_hbm.at[shard_of(q), mine(n)], ss.at[e], rs.at[e], sib)

        def from_sibling(q, n):
            e = 3 * nch + nch * q + n
            return _rcopy(wsh_hbm.at[mine(n)], wg_hbm.at[shard_of(q), theirs(n)], ss.at[e], rs.at[e], sib)

        def load(sl, src):
            cp = pltpu.make_async_copy(src, w_s.at[sl], wsem.at[sl])
            cp.start()
            cp.wait()

        first = (i == 0) & (col == 0)

        @pl.when(first & (slot == 0))
        def _():
            for n in range(nch):
                for q in (0, 1):
                    to_chip(q, n).start()
            load(0, wsh_hbm.at[pl.ds(0, D), :])

        @pl.when(first & (slot == 1))
        def _():
            for q in (0, 1):
                for n in range(nch):
                    from_chip(q, n).wait_recv()
                    to_sibling(q, n).start()
            for n in range(nch):
                to_chip(2, n).start()
            for n in range(nch):
                from_sibling(0, n).wait_recv()
            load(1, wg_hbm.at[shard_of(0), pl.ds(0, D), :])

        @pl.when(first & (slot == 2))
        def _():
            for n in range(nch):
                from_sibling(1, n).wait_recv()
            load(2, wg_hbm.at[shard_of(1), pl.ds(0, D), :])

        @pl.when(first & (slot == 3))
        def _():
            for n in range(nch):
                from_chip(2, n).wait_recv()
                to_sibling(2, n).start()
            for n in range(nch):
                from_sibling(2, n).wait_recv()
            load(3, wg_hbm.at[shard_of(2), pl.ds(0, D), :])
            for q in range(3):
                for n in range(nch):
                    to_chip(q, n).wait_send()
                    to_sibling(q, n).wait_send()

        @pl.when((slot == 0) & (col == 0))
        def _():
            xt = x_ref[...]
            rstd = lax.rsqrt(jnp.mean(xt * xt, axis=-1, keepdims=True) + EPS)
            h = ((xt * rstd * gn_ref[...]) * (1.0 + sc_ref[...]) + sh_ref[...]).astype(BF16)
            hs_all[i] = h
            h_ref[...] = h

        p_ref[0] = _dot(hs_all[i], w_s[slot, :, pl.ds(pl.multiple_of(col * tn, tn), tn)]).astype(BF16)

    row = pl.BlockSpec((1, D), lambda sl, i, col, o: (0, 0))
    x_rows = lambda sl, i, col, o: (jnp.where(sl == 0, i, ni - 1), 0)
    any_ = pl.BlockSpec(memory_space=pl.ANY)
    return pl.pallas_call(
        body, name="gather_norm_inproj",
        grid_spec=pltpu.PrefetchScalarGridSpec(
            num_scalar_prefetch=1, grid=(4, ni, npc),
            in_specs=[pl.BlockSpec((tm, D), x_rows), row, row, row, any_],
            out_specs=(pl.BlockSpec((1, tm, tn), lambda sl, i, col, o: (o[sl], i, col)),
                       pl.BlockSpec((tm, D), x_rows), any_),
            scratch_shapes=[pltpu.VMEM((ni, tm, D), BF16), pltpu.VMEM((4, D, PW), BF16),
                            pltpu.SemaphoreType.DMA((4,)),
                            pltpu.SemaphoreType.DMA((6 * nch,)), pltpu.SemaphoreType.DMA((6 * nch,))]),
        out_shape=(jax.ShapeDtypeStruct((4, s, PW), BF16), jax.ShapeDtypeStruct((s, D), BF16),
                   jax.ShapeDtypeStruct((4, rows, cols), wsh.dtype)),
        compiler_params=_params(("arbitrary", "arbitrary", "arbitrary")),
    )(order, x, gn, shift, scale, wsh)


def _shift_down(prev8, cur, d):
    t = cur.shape[0]
    c3 = cur.reshape(t // 8, 8, DH)
    rot = pltpu.roll(c3, d, 1)
    before = jnp.concatenate([pltpu.roll(prev8, d, 0).reshape(1, 8, DH), rot[:-1]], axis=0)
    rows = lax.broadcasted_iota(jnp.int32, c3.shape, 1)
    return jnp.where(rows >= d, rot, before).reshape(t, DH)


def _shift_up(cur, next8, d):
    t = cur.shape[0]
    c3 = cur.reshape(t // 8, 8, DH)
    rot = pltpu.roll(c3, 8 - d, 1)
    after = jnp.concatenate([rot[1:], pltpu.roll(next8, 8 - d, 0).reshape(1, 8, DH)], axis=0)
    rows = lax.broadcasted_iota(jnp.int32, c3.shape, 1)
    return jnp.where(rows < 8 - d, rot, after).reshape(t, DH)


def _rnn_gates(xr, prev8, cw, cb, wa, ba, wx, bx, lam, reset):
    xc = cw[3:4] * xr + cb
    for d in (1, 2, 3):
        xc = xc + cw[3 - d:4 - d] * _shift_down(prev8, xr, d)
    xcb = xc.astype(BF16)
    r = _sigmoid(_dot(xcb, wa.astype(BF16)) + ba)
    ig = _sigmoid(_dot(xcb, wx.astype(BF16)) + bx)
    nl = -lam
    sp = jnp.maximum(nl, 0.0) + jnp.log1p(jnp.exp(-jnp.abs(nl)))
    log_a = (-LRU_C * r) * sp
    a = jnp.where(reset, 0.0, jnp.exp(log_a))
    mult = jnp.where(reset, 1.0, jnp.sqrt(1.0 - jnp.exp(2.0 * log_a)))
    return xc, r, ig, sp, a, mult


def _log_scan(a, b, axis, up):
    n = a.shape[axis]
    rows = lax.broadcasted_iota(jnp.int32, a.shape, axis)
    d = 1
    while d < n:
        m = rows < n - d if up else rows >= d
        shift = n - d if up else d
        a_s = pltpu.roll(a, shift, axis)
        b_s = pltpu.roll(b, shift, axis)
        b = jnp.where(m, a * b_s + b, b)
        a = jnp.where(m, a * a_s, a)
        d *= 2
    return a, b


def _scan(a, b, t, edge, up=False):
    g = t // 8
    a3, b3 = _log_scan(a.reshape(g, 8, DH), b.reshape(g, 8, DH), 1, up)
    last = 0 if up else 7
    ag, bg = _log_scan(a3[:, last, :], b3[:, last, :], 0, up)
    hg = ag * edge + bg
    grp = lax.broadcasted_iota(jnp.int32, hg.shape, 0)
    if up:
        cin = jnp.where(grp == g - 1, edge, pltpu.roll(hg, g - 1, 0))
        tail = hg[0:1]
    else:
        cin = jnp.where(grp == 0, edge, pltpu.roll(hg, 1, 0))
        tail = hg[g - 1:g]
    return (a3 * cin[:, None, :] + b3).reshape(t, DH), tail


def _rnn_fwd(p, pos, conv_w, conv_b, w_a, b_a, w_x, b_x, lam, tt=512):
    s = p.shape[1]
    nt = s // tt

    def body(xr_ref, z_ref, pos_ref, cw_ref, cb_ref, wa_ref, ba_ref, wx_ref, bx_ref, lam_ref,
             hr_ref, gr_ref, xprev, hprev):
        @pl.when(pl.program_id(1) == 0)
        def _():
            xprev[...] = jnp.zeros_like(xprev)
            hprev[...] = jnp.zeros_like(hprev)

        xr = xr_ref[0].astype(F32)
        z = z_ref[0].astype(F32)
        reset = pos_ref[...] == 0
        xc, r, ig, sp, a, mult = _rnn_gates(xr, xprev[...], cw_ref[...], cb_ref[...], wa_ref[0], ba_ref[0],
                                            wx_ref[0], bx_ref[0], lam_ref[...], reset)
        bx = mult * ig * xc
        h, h_last = _scan(a, bx, tt, hprev[0:1])
        xprev[...] = xr[tt - 8:]
        hprev[...] = jnp.broadcast_to(h_last, (8, DH))
        hr_ref[...] = h
        gr_ref[...] = (h * (z * _sigmoid(z))).astype(BF16)

    head_row = lambda hh, t: (0, hh)
    return pl.pallas_call(
        body, name="rnn_fwd", grid=(H, nt),
        in_specs=[pl.BlockSpec((1, tt, DH), lambda hh, t: (0, t, hh)),
                  pl.BlockSpec((1, tt, DH), lambda hh, t: (0, t, H + hh)),
                  pl.BlockSpec((tt, 1), lambda hh, t: (t, 0)),
                  pl.BlockSpec((4, DH), head_row), pl.BlockSpec((1, DH), head_row),
                  pl.BlockSpec((1, DH, DH), lambda hh, t: (hh, 0, 0)), pl.BlockSpec((1, 1, DH), lambda hh, t: (hh, 0, 0)),
                  pl.BlockSpec((1, DH, DH), lambda hh, t: (hh, 0, 0)), pl.BlockSpec((1, 1, DH), lambda hh, t: (hh, 0, 0)),
                  pl.BlockSpec((1, DH), head_row)],
        out_specs=(pl.BlockSpec((tt, DH), lambda hh, t: (t, hh)), pl.BlockSpec((tt, DH), lambda hh, t: (t, hh))),
        out_shape=(jax.ShapeDtypeStruct((s, D), F32), jax.ShapeDtypeStruct((s, D), BF16)),
        scratch_shapes=[pltpu.VMEM((8, DH), F32), pltpu.VMEM((8, DH), F32)],
        compiler_params=_params(("parallel", "arbitrary")),
    )(p, p, pos, conv_w, conv_b, w_a, b_a, w_x, b_x, lam)


def _rnn_bwd(p, hr, dgr, pos, conv_w, conv_b, w_a, b_a, w_x, b_x, lam, tt=512):
    s = p.shape[1]
    nt = s // tt
    t8 = tt // 8

    def body(xr_ref, z_ref, xp_ref, hr_ref, hp_ref, dg_ref, pos_ref, cw_ref, cb_ref, wa_ref, ba_ref, wx_ref, bx_ref,
             lam_ref, dxr_ref, dz_ref, gwa_ref, gba_ref, gwx_ref, gbx_ref, glam_ref, gcw_ref, gcb_ref,
             a_next, g_next, dxc_next):
        t = pl.program_id(1)
        has_prev = t < nt - 1

        @pl.when(t == 0)
        def _():
            a_next[...] = jnp.zeros_like(a_next)
            g_next[...] = jnp.zeros_like(g_next)
            dxc_next[...] = jnp.zeros_like(dxc_next)
            gwa_ref[...] = jnp.zeros_like(gwa_ref)
            gba_ref[...] = jnp.zeros_like(gba_ref)
            gwx_ref[...] = jnp.zeros_like(gwx_ref)
            gbx_ref[...] = jnp.zeros_like(gbx_ref)
            glam_ref[...] = jnp.zeros_like(glam_ref)
            gcw_ref[...] = jnp.zeros_like(gcw_ref)
            gcb_ref[...] = jnp.zeros_like(gcb_ref)

        xr = xr_ref[0].astype(F32)
        z = z_ref[0].astype(F32)
        hr_blk = hr_ref[...]
        dg = dg_ref[...]
        xprev = jnp.where(has_prev, xp_ref[0].astype(F32)[8:], 0.0)
        hprev8 = jnp.where(has_prev, hp_ref[...], 0.0)
        reset = pos_ref[...] == 0
        cw = cw_ref[...]
        wa = wa_ref[0]
        wx = wx_ref[0]
        lam_v = lam_ref[...]
        xc, r, ig, sp, a, mult = _rnn_gates(xr, xprev, cw, cb_ref[...], wa, ba_ref[0], wx, bx_ref[0], lam_v, reset)

        sz = _sigmoid(z)
        dh = dg * (z * sz)
        dz_ref[...] = (dg * hr_blk * (sz * (1.0 + z * (1.0 - sz)))).astype(BF16)

        an = _shift_up(a, a_next[...], 1)
        g, g_first = _scan(an, dh, tt, g_next[0:1], up=True)
        a_next[...] = jnp.broadcast_to(a[0:1], (8, DH))
        g_next[...] = jnp.broadcast_to(g_first, (8, DH))

        hm1 = _shift_down(hprev8, hr_blk, 1)
        da = g * hm1
        dmult = g * (ig * xc)
        di = g * (mult * xc)
        dxc = g * (mult * ig)
        dla = jnp.where(reset, 0.0, da * a - dmult * (a * a) / mult)
        dr = dla * (-LRU_C * sp)
        dsp = _colsum(dla * (-LRU_C * r))
        glam_ref[0] += dsp * (-_sigmoid(-lam_v))
        dpa = dr * r * (1.0 - r)
        dpx = di * ig * (1.0 - ig)
        dpab = dpa.astype(BF16)
        dpxb = dpx.astype(BF16)
        dxc = dxc + _dot_nt(dpab, wa.astype(BF16)) + _dot_nt(dpxb, wx.astype(BF16))
        xcb = xc.astype(BF16)
        gwa_ref[0] += _dot_tn(xcb, dpab)
        gwx_ref[0] += _dot_tn(xcb, dpxb)
        gba_ref[0] += _colsum(dpa)
        gbx_ref[0] += _colsum(dpx)

        dxr = cw[3:4] * dxc
        for d in (1, 2, 3):
            dxr = dxr + cw[3 - d:4 - d] * _shift_up(dxc, dxc_next[...], d)
        dxr_ref[...] = dxr.astype(BF16)
        dxc_next[...] = dxc[0:8]
        gcb_ref[0] += _colsum(dxc)
        gcw_ref[0, 3:4, :] += _colsum(xr * dxc)
        for d in (1, 2, 3):
            gcw_ref[0, 3 - d:4 - d, :] += _colsum(_shift_down(xprev, xr, d) * dxc)

    rt = lambda t: nt - 1 - t
    prev8 = lambda t: jnp.maximum(rt(t) * t8 - 1, 0)
    head_row = lambda hh, t: (0, hh)
    hsm = lambda hh, t: (hh, 0, 0)
    return pl.pallas_call(
        body, name="rnn_bwd", grid=(H, nt),
        in_specs=[pl.BlockSpec((1, tt, DH), lambda hh, t: (0, rt(t), hh)),
                  pl.BlockSpec((1, tt, DH), lambda hh, t: (0, rt(t), H + hh)),
                  pl.BlockSpec((1, 16, DH), lambda hh, t: (0, jnp.maximum(rt(t) * (tt // 16) - 1, 0), hh)),
                  pl.BlockSpec((tt, DH), lambda hh, t: (rt(t), hh)),
                  pl.BlockSpec((8, DH), lambda hh, t: (prev8(t), hh)),
                  pl.BlockSpec((tt, DH), lambda hh, t: (rt(t), hh)),
                  pl.BlockSpec((tt, 1), lambda hh, t: (rt(t), 0)),
                  pl.BlockSpec((4, DH), head_row), pl.BlockSpec((1, DH), head_row),
                  pl.BlockSpec((1, DH, DH), hsm), pl.BlockSpec((1, 1, DH), hsm),
                  pl.BlockSpec((1, DH, DH), hsm), pl.BlockSpec((1, 1, DH), hsm),
                  pl.BlockSpec((1, DH), head_row)],
        out_specs=(pl.BlockSpec((tt, DH), lambda hh, t: (rt(t), hh)), pl.BlockSpec((tt, DH), lambda hh, t: (rt(t), hh)),
                   pl.BlockSpec((1, DH, DH), hsm), pl.BlockSpec((1, 1, DH), hsm),
                   pl.BlockSpec((1, DH, DH), hsm), pl.BlockSpec((1, 1, DH), hsm),
                   pl.BlockSpec((1, 1, DH), hsm), pl.BlockSpec((1, 4, DH), hsm), pl.BlockSpec((1, 1, DH), hsm)),
        out_shape=(jax.ShapeDtypeStruct((s, D), BF16), jax.ShapeDtypeStruct((s, D), BF16),
                   jax.ShapeDtypeStruct((H, DH, DH), F32), jax.ShapeDtypeStruct((H, 1, DH), F32),
                   jax.ShapeDtypeStruct((H, DH, DH), F32), jax.ShapeDtypeStruct((H, 1, DH), F32),
                   jax.ShapeDtypeStruct((H, 1, DH), F32), jax.ShapeDtypeStruct((H, 4, DH), F32),
                   jax.ShapeDtypeStruct((H, 1, DH), F32)),
        scratch_shapes=[pltpu.VMEM((8, DH), F32), pltpu.VMEM((8, DH), F32), pltpu.VMEM((8, DH), F32)],
        compiler_params=_params(("parallel", "arbitrary")),
    )(p, p, p, hr, hr, dgr, pos, conv_w, conv_b, w_a, b_a, w_x, b_x, lam)


def _rope(t, c, sa, sb):
    return t * c + pltpu.roll(t, DH - ROT // 2, 1) * sa + pltpu.roll(t, ROT // 2, 1) * sb


def _rope_bwd(g, c, sa, sb):
    return g * c + pltpu.roll(g * sa, ROT // 2, 1) + pltpu.roll(g * sb, DH - ROT // 2, 1)


def _unit_bases(gi, u):
    dil = DILATIONS[gi]
    if dil == 1:
        return u * UB, SPAN + (u - 1) * UB, u == 0
    if dil == 4:
        blk, r = u // 4, u % 4
        return blk * 4 * UB + r, SPAN + (blk - 1) * 4 * UB + r, blk == 0
    return u, u, True


def _unit_slices(gi, u):
    dil = DILATIONS[gi]
    qb0, kb0, first = _unit_bases(gi, u)
    if dil == 1:
        return pl.ds(pl.multiple_of(qb0, UB), UB), pl.ds(pl.multiple_of(kb0, UB), 2 * UB), first
    return pl.ds(qb0, UB, stride=dil), pl.ds(kb0, 2 * UB, stride=dil), first


def _bdot(a, b):
    return lax.dot_general(a, b, (((2,), (1,)), ((0,), (0,))), preferred_element_type=F32)


def _bdot_nt(a, b):
    return lax.dot_general(a, b, (((2,), (2,)), ((0,), (0,))), preferred_element_type=F32)


def _bdot_tn(a, b):
    return lax.dot_general(a, b, (((1,), (1,)), ((0,), (0,))), preferred_element_type=F32)


def _band_mask(first_in_span, has_prev):
    qi = lax.broadcasted_iota(jnp.int32, (UB, 2 * UB), 0)
    ki = lax.broadcasted_iota(jnp.int32, (UB, 2 * UB), 1)
    dist = UB + qi - ki
    band = (dist >= 0) & (dist <= UB)
    return band & ((ki >= UB) | jnp.logical_not(first_in_span) | has_prev)


def _gather_halves(phase, src_hbm, dst_hbm, ss, rs):
    half = src_hbm.shape[0] // 2
    cx, cy, cc = _coords()
    j = 2 * cx + cy
    sib = (cx, cy, 1 - cc)
    mine = pl.ds(cc * half, half)
    theirs = pl.ds((1 - cc) * half, half)
    chips = ((1, 0), (0, 1), (1, 1))
    for q, (kx, ky) in enumerate(chips):
        jq = j ^ (2 * kx + ky)
        out = _rcopy(src_hbm.at[mine], dst_hbm.at[j, mine], ss.at[q], rs.at[q], (_flip(cx, kx), _flip(cy, ky), cc))
        landed = _rcopy(src_hbm.at[mine], dst_hbm.at[jq, mine], ss.at[q], rs.at[q], sib)
        onward = _rcopy(dst_hbm.at[jq, mine], dst_hbm.at[jq, mine], ss.at[3 + q], rs.at[3 + q], sib)
        from_sib = _rcopy(src_hbm.at[mine], dst_hbm.at[jq, theirs], ss.at[3 + q], rs.at[3 + q], sib)
        if phase == 0:
            out.start()
        elif phase == 1:
            landed.wait_recv()
            onward.start()
        else:
            from_sib.wait_recv()
            out.wait_send()
            onward.wait_send()


def _attn_fwd(p, rc, rsa, rsb, w3sh):
    s = p.shape[1]
    ns = s // SPAN
    nunit = SPAN // UB

    def body(q_ref, k_ref, v_ref, z_ref, c_ref, sa_ref, sb_ref, w3_hbm, o_ref, lse_ref, ga_ref, w3g_hbm,
             qr, kf, vf, acc, mm, ll, ss, rs):
        hh, n = pl.program_id(0), pl.program_id(1)
        for phase, at_head, at_span in ((0, 0, 0), (1, H // 2, 0), (2, H - 1, ns - 1)):
            @pl.when((hh == at_head) & (n == at_span))
            def _(phase=phase):
                _gather_halves(phase, w3_hbm, w3g_hbm, ss, rs)

        @pl.when(n == 0)
        def _():
            kf[0:SPAN] = jnp.zeros((SPAN, DH), F32)
            vf[0:SPAN] = jnp.zeros((SPAN, DH), F32)

        c, sa, sb = c_ref[...], sa_ref[...], sb_ref[...]
        qr[...] = _rope(q_ref[0].astype(F32), c, sa, sb)
        kf[SPAN:] = _rope(k_ref[0].astype(F32), c, sa, sb)
        vf[SPAN:] = v_ref[0].astype(F32)
        has_prev = n > 0

        for gi, dil in enumerate(DILATIONS):
            def trip(t, carry, gi=gi, dil=dil):
                qsls, ksls, firsts = [], [], []
                for b in range(UNIT_BATCH):
                    qsl, ksl, first = _unit_slices(gi, t * UNIT_BATCH + b)
                    qsls.append(qsl)
                    ksls.append(ksl)
                    firsts.append(first)
                qb = jnp.stack([qr[qsl, :].astype(BF16) for qsl in qsls])
                kb = jnp.stack([kf[ksl, :].astype(BF16) for ksl in ksls])
                vb = jnp.stack([vf[ksl, :].astype(BF16) for ksl in ksls])
                s_all = _bdot_nt(qb, kb)
                prs = []
                for b in range(UNIT_BATCH):
                    sc = jnp.where(_band_mask(firsts[b], has_prev), s_all[b] * SCALE, NEG)
                    m = jnp.max(sc, axis=-1, keepdims=True)
                    pr = jnp.exp(sc - m)
                    l = jnp.sum(pr, axis=-1, keepdims=True)
                    mm[gi, qsls[b], :] = jnp.broadcast_to(m, (UB, DH))
                    ll[gi, qsls[b], :] = jnp.broadcast_to(l, (UB, DH))
                    prs.append(pr.astype(BF16))
                o_all = _bdot(jnp.stack(prs), vb)
                for b in range(UNIT_BATCH):
                    acc[gi, qsls[b], :] = o_all[b]
                return carry

            lax.fori_loop(0, nunit // UNIT_BATCH, trip, 0)

        m_all =jnp.maximum(jnp.maximum(mm[0], mm[1]), mm[2])
        num = jnp.zeros((SPAN, DH), F32)
        den = jnp.zeros((SPAN, DH), F32)
        for gi in range(3):
            w = jnp.exp(mm[gi] - m_all)
            num = num + w * acc[gi]
            den = den + w * ll[gi]
        o = num / den
        o_ref[...] = o
        lse_ref[...] = m_all + jnp.log(den)
        z = z_ref[0].astype(F32)
        ga_ref[...] = (o * (z * _sigmoid(z))).astype(BF16)
        kf[0:SPAN] = kf[SPAN:]
        vf[0:SPAN] = vf[SPAN:]

    blk = lambda piece, off: pl.BlockSpec((1, SPAN, DH), lambda hh, n: (piece, n, off + hh))
    tab = pl.BlockSpec((SPAN, DH), lambda hh, n: (n, 0))
    outb = pl.BlockSpec((SPAN, DH), lambda hh, n: (n, hh))
    any_ = pl.BlockSpec(memory_space=pl.ANY)
    return pl.pallas_call(
        body, name="attn_fwd", grid=(H, ns),
        in_specs=[blk(1, 0), blk(1, H), blk(2, 0), blk(2, H), tab, tab, tab, any_],
        out_specs=(outb, outb, outb, any_),
        out_shape=(jax.ShapeDtypeStruct((s, D), F32), jax.ShapeDtypeStruct((s, D), F32),
                   jax.ShapeDtypeStruct((s, D), BF16), jax.ShapeDtypeStruct((4,) + w3sh.shape, w3sh.dtype)),
        scratch_shapes=[pltpu.VMEM((SPAN, DH), F32), pltpu.VMEM((2 * SPAN, DH), F32), pltpu.VMEM((2 * SPAN, DH), F32),
                        pltpu.VMEM((3, SPAN, DH), F32), pltpu.VMEM((3, SPAN, DH), F32), pltpu.VMEM((3, SPAN, DH), F32),
                        pltpu.SemaphoreType.DMA((6,)), pltpu.SemaphoreType.DMA((6,))],
        compiler_params=_params(("arbitrary", "arbitrary")),
    )(p, p, p, p, rc, rsa, rsb, w3sh)


def _attn_bwd(p, o, lse, dga, rc, rsa, rsb):
    s = p.shape[1]
    ns = s // SPAN
    nunit = SPAN // UB

    def body(q_ref, k_ref, kp_ref, v_ref, vp_ref, z_ref, c_ref, sa_ref, sb_ref, cp_ref, sap_ref, sbp_ref,
             o_ref, lse_ref, dg_ref, dq_ref, dk_ref, dv_ref, dz_ref,
             qr, kf, vf, dof, dlt, dqa, dkf, dvf):
        step = pl.program_id(1)
        n = ns - 1 - step
        has_prev = n > 0

        @pl.when(step == 0)
        def _():
            dkf[...] = jnp.zeros_like(dkf)
            dvf[...] = jnp.zeros_like(dvf)

        @pl.when(step > 0)
        def _():
            dkf[SPAN:] = dkf[0:SPAN]
            dvf[SPAN:] = dvf[0:SPAN]
            dkf[0:SPAN] = jnp.zeros((SPAN, DH), F32)
            dvf[0:SPAN] = jnp.zeros((SPAN, DH), F32)

        c, sa, sb = c_ref[...], sa_ref[...], sb_ref[...]
        qr[...] = _rope(q_ref[0].astype(F32), c, sa, sb)
        kf[SPAN:] = _rope(k_ref[0].astype(F32), c, sa, sb)
        vf[SPAN:] = v_ref[0].astype(F32)
        kf[0:SPAN] = jnp.where(has_prev, _rope(kp_ref[0].astype(F32), cp_ref[...], sap_ref[...], sbp_ref[...]), 0.0)
        vf[0:SPAN] = jnp.where(has_prev, vp_ref[0].astype(F32), 0.0)
        z = z_ref[0].astype(F32)
        sz = _sigmoid(z)
        dg = dg_ref[...]
        ov = o_ref[...]
        do = dg * (z * sz)
        dz_ref[...] = (dg * ov * (sz * (1.0 + z * (1.0 - sz)))).astype(BF16)
        dof[...] = do
        dlt[...] = jnp.broadcast_to(jnp.sum(do * ov, axis=-1, keepdims=True), (SPAN, DH))
        dqa[...] = jnp.zeros_like(dqa)

        for gi, dil in enumerate(DILATIONS):
            def trip(t, carry, gi=gi, dil=dil):
                qsls, ksls, firsts = [], [], []
                for b in range(UNIT_BATCH):
                    qsl, ksl, first = _unit_slices(gi, t * UNIT_BATCH + b)
                    qsls.append(qsl)
                    ksls.append(ksl)
                    firsts.append(first)
                qb = jnp.stack([qr[qsl, :].astype(BF16) for qsl in qsls])
                kb = jnp.stack([kf[ksl, :].astype(BF16) for ksl in ksls])
                vb = jnp.stack([vf[ksl, :].astype(BF16) for ksl in ksls])
                dob = jnp.stack([dof[qsl, :].astype(BF16) for qsl in qsls])
                s_all = _bdot_nt(qb, kb)
                dp_all = _bdot_nt(dob, vb)
                prs, dss = [], []
                for b in range(UNIT_BATCH):
                    lse_b = lse_ref[qsls[b], :]
                    dl_b = dlt[qsls[b], :]
                    pr = jnp.exp(s_all[b] * SCALE - jnp.concatenate([lse_b, lse_b], axis=1))
                    pr = jnp.where(_band_mask(firsts[b], has_prev), pr, 0.0)
                    prs.append(pr.astype(BF16))
                    dss.append((pr * (dp_all[b] - jnp.concatenate([dl_b, dl_b], axis=1)) * SCALE).astype(BF16))
                ds_all = jnp.stack(dss)
                dv_all = _bdot_tn(jnp.stack(prs), dob)
                dq_all = _bdot(ds_all, kb)
                dk_all = _bdot_tn(ds_all, qb)
                for b in range(UNIT_BATCH):
                    dvf[ksls[b], :] += dv_all[b]
                    dqa[qsls[b], :] += dq_all[b]
                    dkf[ksls[b], :] += dk_all[b]
                return carry

            lax.fori_loop(0, nunit // UNIT_BATCH, trip, 0)

        dq_ref[...] = _rope_bwd(dqa[...], c, sa, sb).astype(BF16)
        dk_ref[...] = _rope_bwd(dkf[SPAN:], c, sa, sb).astype(BF16)
        dv_ref[...] = dvf[SPAN:].astype(BF16)

    rn = lambda n: ns - 1 - n
    pn = lambda n: jnp.maximum(ns - 2 - n, 0)
    blk = lambda piece, off: pl.BlockSpec((1, SPAN, DH), lambda hh, n: (piece, rn(n), off + hh))
    blkp = lambda piece, off: pl.BlockSpec((1, SPAN, DH), lambda hh, n: (piece, pn(n), off + hh))
    tab = pl.BlockSpec((SPAN, DH), lambda hh, n: (rn(n), 0))
    tabp = pl.BlockSpec((SPAN, DH), lambda hh, n: (pn(n), 0))
    io = pl.BlockSpec((SPAN, DH), lambda hh, n: (rn(n), hh))
    return pl.pallas_call(
        body, name="attn_bwd", grid=(H, ns),
        in_specs=[blk(1, 0), blk(1, H), blkp(1, H), blk(2, 0), blkp(2, 0), blk(2, H),
                  tab, tab, tab, tabp, tabp, tabp, io, io, io],
        out_specs=(io, io, io, io),
        out_shape=tuple(jax.ShapeDtypeStruct((s, D), BF16) for _ in range(4)),
        scratch_shapes=[pltpu.VMEM((SPAN, DH), F32), pltpu.VMEM((2 * SPAN, DH), F32), pltpu.VMEM((2 * SPAN, DH), F32),
                        pltpu.VMEM((SPAN, DH), F32), pltpu.VMEM((SPAN, DH), F32), pltpu.VMEM((SPAN, DH), F32),
                        pltpu.VMEM((2 * SPAN, DH), F32), pltpu.VMEM((2 * SPAN, DH), F32)],
        compiler_params=_params(("parallel", "arbitrary")),
    )(p, p, p, p, p, p, rc, rsa, rsb, rc, rsa, rsb, o, lse, dga)


def _tail(gr, ga, p, x, tgt, w3, b_gate, gate, g_final, tm=256):
    s = x.shape[0]
    nt = s // tm

    def body(gr_ref, ga_ref, pr_ref, pa_ref, x_ref, t_ref, bg_ref, gate_ref, gf_ref, w_hbm,
             dgr_ref, dga_ref, dc_ref, dx2_ref, vec_ref, go_hbm, w_s, acc_s, sem):
        i = pl.program_id(0)

        @pl.when(i == 0)
        def _():
            cp = pltpu.make_async_copy(w_hbm, w_s, sem.at[12])
            cp.start()
            acc_s[...] = jnp.zeros_like(acc_s)
            vec_ref[...] = jnp.zeros_like(vec_ref)
            cp.wait()

        grb = gr_ref[...]
        gab = ga_ref[...]
        bg = bg_ref[...]
        gate_v = gate_ref[...]
        gf = gf_ref[...]
        y_r = _dot(grb, w_s[0])
        y_a = _dot(gab, w_s[1])
        sr = _sigmoid(pr_ref[0].astype(F32) + bg[:, :D])
        sa = _sigmoid(pa_ref[0].astype(F32) + bg[:, D:])
        mb = (sr * y_r + sa * y_a).astype(BF16)
        u = _dot(mb, w_s[2])
        x2 = x_ref[...] + gate_v * u
        rstd = lax.rsqrt(jnp.mean(x2 * x2, axis=-1, keepdims=True) + EPS)
        xh = x2 * rstd
        e = xh * gf - t_ref[...]
        dy = e * (1.0 / D)
        dyg = dy * gf
        dx2 = rstd * (dyg - xh * jnp.mean(dyg * xh, axis=-1, keepdims=True))
        dx2_ref[...] = dx2
        dub = (dx2 * gate_v).astype(BF16)
        dm = _dot_nt(dub, w_s[2])
        dyr = (dm * sr).astype(BF16)
        dya = (dm * sa).astype(BF16)
        dpr = dm * y_r * (sr * (1.0 - sr))
        dpa = dm * y_a * (sa * (1.0 - sa))
        dc_ref[:, :D] = dpr.astype(BF16)
        dc_ref[:, D:] = dpa.astype(BF16)
        dgr_ref[...] = _dot_nt(dyr, w_s[0])
        dga_ref[...] = _dot_nt(dya, w_s[1])
        acc_s[0] += _dot_tn(grb, dyr)
        acc_s[1] += _dot_tn(gab, dya)
        acc_s[2] += _dot_tn(mb, dub)
        vec_ref[0:1, :] += _colsum(dy * xh)
        vec_ref[1:2, :] += _colsum(dx2 * u)
        vec_ref[2:3, :] += _colsum(dpr)
        vec_ref[3:4, :] += _colsum(dpa)
        vec_ref[4:5, :] += _colsum(e * e)

        @pl.when(i == nt - 1)
        def _():
            vec_ref[4:5, :] = jnp.broadcast_to(jnp.sum(vec_ref[4:5, :]) * (0.5 / D), (1, D))
            cps = []
            for w in range(3):
                for j in range(4):
                    cps.append(pltpu.make_async_copy(acc_s.at[w, pl.ds(256 * j, 256)],
                                                     go_hbm.at[j, pl.ds(256 * w, 256)], sem.at[4 * w + j]))
            for cp in cps:
                cp.start()
            for cp in cps:
                cp.wait()

    rowt = lambda i: (i, 0)
    row = lambda w: pl.BlockSpec((1, w), lambda i: (0, 0))
    any_ = pl.BlockSpec(memory_space=pl.ANY)
    return pl.pallas_call(
        body, name="tail", grid=(nt,),
        in_specs=[pl.BlockSpec((tm, D), rowt), pl.BlockSpec((tm, D), rowt),
                  pl.BlockSpec((1, tm, D), lambda i: (3, i, 0)), pl.BlockSpec((1, tm, D), lambda i: (3, i, 1)),
                  pl.BlockSpec((tm, D), rowt), pl.BlockSpec((tm, D), rowt),
                  row(2 * D), row(D), row(D), any_],
        out_specs=(pl.BlockSpec((tm, D), rowt), pl.BlockSpec((tm, D), rowt), pl.BlockSpec((tm, 2 * D), rowt),
                   pl.BlockSpec((tm, D), rowt), pl.BlockSpec((8, D), lambda i: (0, 0)), any_),
        out_shape=(jax.ShapeDtypeStruct((s, D), F32), jax.ShapeDtypeStruct((s, D), F32),
                   jax.ShapeDtypeStruct((s, 2 * D), BF16), jax.ShapeDtypeStruct((s, D), F32),
                   jax.ShapeDtypeStruct((8, D), F32), jax.ShapeDtypeStruct((4, 768, D), F32)),
        scratch_shapes=[pltpu.VMEM((3, D, D), BF16), pltpu.VMEM((3, D, D), F32), pltpu.SemaphoreType.DMA((13,))],
        compiler_params=_params(("arbitrary",)),
    )(gr, ga, p, p, x, tgt, b_gate, gate, g_final, w3)


def _pieces_steps(pieces):
    out, s0 = [], 0
    for a in pieces:
        n = a.shape[1] // D
        out.append((s0, n))
        s0 += n
    return out, s0


def _inproj_bwd_x(pieces, wg, x, dx2, gn, scale, sums, tm=512):
    s = x.shape[0]
    np_ = len(pieces)
    na = len(sums)
    ni = s // tm
    groups, cur, width = [], [], 0
    for t, a in enumerate(pieces):
        cur.append(t)
        width += a.shape[1]
        if width == PW:
            groups.append(cur)
            cur, width = [], 0
    assert len(groups) == 4 and not cur

    def body(*refs):
        d_refs = refs[:np_]
        w_hbm, x_ref, dx2_ref, gn_ref, sc_ref = refs[np_:np_ + 5]
        q_refs = refs[np_ + 5:np_ + 5 + na]
        gx_ref, vec_ref = refs[np_ + 5 + na:np_ + 7 + na]
        r_refs = refs[np_ + 7 + na:np_ + 7 + 2 * na]
        w_s, wsem, ss, rs = refs[np_ + 7 + 2 * na:]
        i = pl.program_id(0)

        def scatter_copies():
            cx, cy, cc = _coords()
            j = 2 * cx + cy
            cps = []
            for t, (q, r) in enumerate(zip(q_refs, r_refs)):
                for e, (kx, ky) in enumerate(((1, 0), (0, 1), (1, 1))):
                    cps.append(_rcopy(q.at[j ^ (2 * kx + ky)], r.at[e], ss.at[3 * t + e], rs.at[3 * t + e],
                                      (_flip(cx, kx), _flip(cy, ky), cc)))
            return cps

        def w_copy(pc):
            return pltpu.make_async_copy(w_hbm.at[pc, pl.ds(0, D), :], w_s.at[pc], wsem.at[pc])

        @pl.when(i == 0)
        def _():
            for pc in range(4):
                w_copy(pc).start()
            vec_ref[...] = jnp.zeros_like(vec_ref)
            for cp in scatter_copies():
                cp.start()

        dh = None
        for pc, group in enumerate(groups):
            @pl.when(i == 0)
            def _(pc=pc):
                w_copy(pc).wait()

            tiles = [d_refs[t][...] for t in group]
            lhs = tiles[0] if len(tiles) == 1 else jnp.concatenate(tiles, axis=1)
            part = _dot_nt(lhs, w_s[pc])
            dh = part if dh is None else dh + part

        xt = x_ref[...]
        rstd = lax.rsqrt(jnp.mean(xt * xt, axis=-1, keepdims=True) + EPS)
        xh = xt * rstd
        gn_v = gn_ref[...]
        sc1 = 1.0 + sc_ref[...]
        dhx = dh * xh
        vec_ref[0:1, :] += _colsum(dh)
        vec_ref[1:2, :] += _colsum(dhx) * gn_v
        vec_ref[2:3, :] += _colsum(dhx) * sc1
        dxh = dh * (gn_v * sc1)
        gx_ref[...] = rstd * (dxh - xh * jnp.mean(dxh * xh, axis=-1, keepdims=True)) + dx2_ref[...]

        @pl.when(i == ni - 1)
        def _():
            for cp in scatter_copies():
                cp.wait()

    rowt = lambda i: (i, 0)
    row = pl.BlockSpec((1, D), lambda i: (0, 0))
    any_ = pl.BlockSpec(memory_space=pl.ANY)
    outs = pl.pallas_call(
        body, name="inproj_bwd_x", grid=(ni,),
        in_specs=[pl.BlockSpec((tm, a.shape[1]), rowt) for a in pieces] +
                 [any_, pl.BlockSpec((tm, D), rowt), pl.BlockSpec((tm, D), rowt), row, row] + [any_] * na,
        out_specs=(pl.BlockSpec((tm, D), rowt), pl.BlockSpec((8, D), lambda i: (0, 0))) + (any_,) * na,
        out_shape=(jax.ShapeDtypeStruct((s, D), F32), jax.ShapeDtypeStruct((8, D), F32)) +
                  tuple(jax.ShapeDtypeStruct((3,) + q.shape[1:], q.dtype) for q in sums),
        scratch_shapes=[pltpu.VMEM((4, D, PW), BF16), pltpu.SemaphoreType.DMA((4,)),
                        pltpu.SemaphoreType.DMA((3 * na,)), pltpu.SemaphoreType.DMA((3 * na,))],
        compiler_params=_params(("arbitrary",)),
    )(*pieces, wg, x, dx2, gn, scale, *sums)
    return outs[0], outs[1], outs[2:]


def _inproj_bwd_w(pieces, hbf, tk=1024):
    s = hbf.shape[0]
    steps, nk = _pieces_steps(pieces)
    npc = PW // D
    ns = s // tk
    np_ = len(pieces)

    def body(*refs):
        d_refs = refs[:np_]
        h_ref, g_ref = refs[np_:]
        cb, k = pl.program_id(0), pl.program_id(1)

        @pl.when(k == 0)
        def _():
            g_ref[...] = jnp.zeros_like(g_ref)

        for (s0, n), d_ref in zip(steps, d_refs):
            @pl.when((cb >= s0) & (cb < s0 + n))
            def _(d_ref=d_ref):
                g_ref[0] += _dot_tn(h_ref[...], d_ref[...])

    def piece_spec(s0, n):
        def imap(cb, k):
            active = (cb >= s0) & (cb < s0 + n)
            return (jnp.where(active, k, 0), jnp.clip(cb - s0, 0, n - 1))
        return pl.BlockSpec((tk, D), imap)

    return pl.pallas_call(
        body, name="inproj_bwd_w", grid=(nk, ns),
        in_specs=[piece_spec(s0, n) for s0, n in steps] + [pl.BlockSpec((tk, D), lambda cb, k: (k, 0))],
        out_specs=pl.BlockSpec((1, D, D), lambda cb, k: (cb // npc, 0, cb % npc)),
        out_shape=jax.ShapeDtypeStruct((4, D, PW), F32),
        compiler_params=_params(("parallel", "arbitrary")),
    )(*pieces, hbf)


D2D_CHUNK_BYTES = 512 * 1024


def _chunk_rows(a):
    return max(8, D2D_CHUNK_BYTES // (a.shape[-1] * a.dtype.itemsize))


def _pair_exchange(arrs):
    na = len(arrs)
    chunks = []
    for t, a in enumerate(arrs):
        hr = a.shape[1] // 2
        cr = _chunk_rows(a)
        chunks += [(t, j, r0, cr) for j in range(a.shape[0]) for r0 in range(0, hr, cr)]
    nch = len(chunks)

    def body(*refs):
        a_refs = refs[:na]
        rb_refs = refs[na:2 * na]
        ss, rs = refs[2 * na:]
        x, y, c = _coords()
        sib = (x, y, 1 - c)
        rcs = []
        for n, (t, j, r0, cr) in enumerate(chunks):
            hr = a_refs[t].shape[1] // 2
            rc = _rcopy(a_refs[t].at[j, pl.ds((1 - c) * hr + r0, cr), :], rb_refs[t].at[j, pl.ds(r0, cr), :],
                        ss.at[n], rs.at[n], sib)
            rc.start()
            rcs.append(rc)
        for rc in rcs:
            rc.wait_recv()
        for rc in rcs:
            rc.wait_send()

    any_ = pl.BlockSpec(memory_space=pl.ANY)
    halves = [jax.ShapeDtypeStruct((a.shape[0], a.shape[1] // 2, a.shape[2]), a.dtype) for a in arrs]
    return pl.pallas_call(
        body, name="pair_exchange",
        out_shape=tuple(halves),
        in_specs=[any_] * na, out_specs=tuple([any_] * na),
        scratch_shapes=[pltpu.SemaphoreType.DMA((nch,)), pltpu.SemaphoreType.DMA((nch,))],
        compiler_params=_params(),
    )(*arrs)


def _pair_swap(arrs):
    na = len(arrs)
    chunks = []
    for t, a in enumerate(arrs):
        cr = _chunk_rows(a)
        chunks += [(t, r0, cr) for r0 in range(0, a.shape[0], cr)]
    nch = len(chunks)

    def body(*refs):
        a_refs = refs[:na]
        o_refs = refs[na:2 * na]
        ss, rs = refs[2 * na:]
        x, y, c = _coords()
        sib = (x, y, 1 - c)
        rcs = []
        for n, (t, r0, cr) in enumerate(chunks):
            rows = pl.ds(r0, cr)
            rc = _rcopy(a_refs[t].at[rows, :], o_refs[t].at[rows, :], ss.at[n], rs.at[n], sib)
            rc.start()
            rcs.append(rc)
        for rc in rcs:
            rc.wait_recv()
        for rc in rcs:
            rc.wait_send()

    any_ = pl.BlockSpec(memory_space=pl.ANY)
    return pl.pallas_call(
        body, name="pair_swap",
        out_shape=tuple(jax.ShapeDtypeStruct(a.shape, a.dtype) for a in arrs),
        in_specs=[any_] * na, out_specs=tuple([any_] * na),
        scratch_shapes=[pltpu.SemaphoreType.DMA((nch,)), pltpu.SemaphoreType.DMA((nch,))],
        compiler_params=_params(),
    )(*arrs)


def _add_half(full, rb, core, tr):
    n, r, cdim = full.shape
    nb = r // 2 // tr

    def body(c_ref, a_ref, b_ref, o_ref, ob_ref):
        tot = a_ref[...] + b_ref[...]
        o_ref[...] = tot
        ob_ref[...] = tot.astype(BF16)

    mine = pl.BlockSpec((1, tr, cdim), lambda i, j, c_ref: (i, c_ref[0] * nb + j, 0))
    spec = pl.BlockSpec((1, tr, cdim), lambda i, j, c_ref: (i, j, 0))
    return pl.pallas_call(
        body, name="add_half",
        grid_spec=pltpu.PrefetchScalarGridSpec(num_scalar_prefetch=1, grid=(n, nb), in_specs=[mine, spec],
                                               out_specs=(spec, spec)),
        out_shape=(jax.ShapeDtypeStruct(rb.shape, rb.dtype), jax.ShapeDtypeStruct(rb.shape, BF16)),
        compiler_params=_params(("parallel", "parallel")),
    )(core, full, rb)


def _sum_slots(q, r3, shard, tr):
    _, r, cdim = q.shape

    def body(j_ref, q_ref, r_ref, o_ref):
        o_ref[...] = ((q_ref[0] + r_ref[0].astype(F32)) + r_ref[1].astype(F32)) + r_ref[2].astype(F32)

    return pl.pallas_call(
        body, name="sum_slots",
        grid_spec=pltpu.PrefetchScalarGridSpec(
            num_scalar_prefetch=1, grid=(r // tr,),
            in_specs=[pl.BlockSpec((1, tr, cdim), lambda i, j_ref: (j_ref[0], i, 0)),
                      pl.BlockSpec((3, tr, cdim), lambda i, j_ref: (0, i, 0))],
            out_specs=pl.BlockSpec((tr, cdim), lambda i, j_ref: (i, 0))),
        out_shape=jax.ShapeDtypeStruct((r, cdim), q.dtype),
        compiler_params=_params(("parallel",)),
    )(shard, q, r3)


def _allreduce_small(packs):
    na = len(packs)

    def body(*refs):
        p_refs, o_refs, rbufs = refs[:na], refs[na:2 * na], refs[2 * na:3 * na]
        s1, r1, s2, r2 = refs[3 * na:]
        me = _my_index()
        sends = []

        def chunk(t, d):
            ch = p_refs[t].shape[0] // NDEV
            return pl.ds(pl.multiple_of(d * ch, 8), ch)

        for t in range(na):
            for k in range(1, NDEV):
                e = 7 * t + k - 1
                cp = _rcopy(p_refs[t].at[chunk(t, me ^ k)], rbufs[t].at[me], s1.at[e], r1.at[e], _peer(k))
                cp.start()
                sends.append(cp)
            rbufs[t][me] = p_refs[t][chunk(t, me), :]
        for t in range(na):
            for k in range(1, NDEV):
                e = 7 * t + k - 1
                _rcopy(p_refs[t].at[chunk(t, me)], rbufs[t].at[me ^ k], s1.at[e], r1.at[e], _peer(k)).wait_recv()
            tot = rbufs[t][0]
            for d in range(1, NDEV):
                tot = tot + rbufs[t][d]
            o_refs[t][chunk(t, me), :] = tot
            for k in range(1, NDEV):
                e = 7 * t + k - 1
                cp = _rcopy(o_refs[t].at[chunk(t, me)], o_refs[t].at[chunk(t, me)], s2.at[e], r2.at[e], _peer(k))
                cp.start()
                sends.append(cp)
        for t in range(na):
            for k in range(1, NDEV):
                e = 7 * t + k - 1
                _rcopy(o_refs[t].at[chunk(t, me)], o_refs[t].at[chunk(t, me ^ k)], s2.at[e], r2.at[e],
                       _peer(k)).wait_recv()
        for cp in sends:
            cp.wait_send()

    vm = pl.BlockSpec(memory_space=pltpu.VMEM)
    return pl.pallas_call(
        body, name="allreduce_small",
        out_shape=tuple(jax.ShapeDtypeStruct(a.shape, F32) for a in packs),
        in_specs=[vm] * na, out_specs=tuple([vm] * na),
        scratch_shapes=[pltpu.VMEM((NDEV, a.shape[0] // NDEV, a.shape[1]), F32) for a in packs] +
                       [pltpu.SemaphoreType.DMA((7 * na,)) for _ in range(4)],
        compiler_params=_params(),
    )(*packs)


def _adamw_update(w, g, m, v):
    nm = B1 * m + (1.0 - B1) * g
    nv = B2 * v + (1.0 - B2) * (g * g)
    m_hat = nm / (1.0 - B1 ** STEP)
    v_hat = nv / (1.0 - B2 ** STEP)
    return -LR * (m_hat / (jnp.sqrt(v_hat) + ADAM_EPS) + WD * w), nm, nv


def _adamw(w, g, m, v, tr):
    r, cdim = w.shape

    def body(w_ref, g_ref, m_ref, v_ref, d_ref, nm_ref, nv_ref):
        d_ref[...], nm_ref[...], nv_ref[...] = _adamw_update(w_ref[...], g_ref[...], m_ref[...], v_ref[...])

    spec = pl.BlockSpec((tr, cdim), lambda i: (i, 0))
    sd = jax.ShapeDtypeStruct((r, cdim), F32)
    return pl.pallas_call(
        body, name="adamw", grid=(r // tr,), in_specs=[spec] * 4, out_specs=(spec,) * 3, out_shape=(sd,) * 3,
        compiler_params=_params(("parallel",)),
    )(w, g, m, v)


V_G_NORM, V_DMOD, V_B_GATE, V_CONV_B, V_LAM, V_G_FINAL, V_CONV_W, V_LOSS, V_ROWS = 0, 1, 4, 6, 7, 8, 9, 13, 64
M_W_A, M_W_X, M_B_A, M_B_X, M_ROWS = 0, H * DH, 2 * H * DH, 2 * H * DH + H, 2112
SMALL = ("g_norm", "b_mod", "b_gate", "conv_b", "lam", "g_final", "conv_w", "w_a", "w_x", "b_a", "b_x")


def _adamw_small(redv, redm, g_conv, wmv):
    def grad(name, rv, rm, gc):
        if name == "g_norm":
            return rv[V_G_NORM:V_G_NORM + 1, :]
        if name == "b_mod":
            return jnp.concatenate([rv[V_DMOD + t:V_DMOD + t + 1, :] for t in range(3)], axis=1)
        if name == "b_gate":
            return jnp.concatenate([rv[V_B_GATE + t:V_B_GATE + t + 1, :] for t in range(2)], axis=1)
        if name == "conv_b":
            return rv[V_CONV_B:V_CONV_B + 1, :]
        if name == "lam":
            return rv[V_LAM:V_LAM + 1, :]
        if name == "g_final":
            return rv[V_G_FINAL:V_G_FINAL + 1, :]
        if name == "conv_w":
            return gc[...]
        if name == "w_a":
            return rm[M_W_A:M_W_A + H * DH, :]
        if name == "w_x":
            return rm[M_W_X:M_W_X + H * DH, :]
        if name == "b_a":
            return rm[M_B_A:M_B_A + H, :]
        return rm[M_B_X:M_B_X + H, :]

    n = len(SMALL)

    def body(*refs):
        rv, rm, gc = refs[:3]
        ins, outs = refs[3:3 + 3 * n], refs[3 + 3 * n:]
        for t, name in enumerate(SMALL):
            w_ref, m_ref, v_ref = ins[3 * t:3 * t + 3]
            g_out, d_out, m_out, v_out = outs[4 * t:4 * t + 4]
            g = grad(name, rv, rm, gc)
            g_out[...] = g
            d_out[...], m_out[...], v_out[...] = _adamw_update(w_ref[...], g, m_ref[...], v_ref[...])

    vm = pl.BlockSpec(memory_space=pltpu.VMEM)
    flat = [a for name in SMALL for a in wmv[name]]
    shapes = [jax.ShapeDtypeStruct(wmv[name][0].shape, F32) for name in SMALL for _ in range(4)]
    outs = pl.pallas_call(
        body, name="adamw_small", out_shape=tuple(shapes),
        in_specs=[vm] * (3 + len(flat)), out_specs=tuple([vm] * len(shapes)),
        compiler_params=_params(),
    )(redv, redm, g_conv, *flat)
    return {name: outs[4 * t:4 * t + 4] for t, name in enumerate(SMALL)}


def _rope_tables(positions):
    inv_freq = ROPE_THETA ** (-jnp.arange(0, ROT, 2, dtype=F32) / ROT)
    ang = positions.astype(F32)[:, None] * inv_freq
    cos, sin = jnp.cos(ang), jnp.sin(ang)
    n = positions.shape[0]
    half = ROT // 2
    rc = jnp.concatenate([cos, cos, jnp.ones((n, DH - ROT), F32)], axis=1)
    rsa = jnp.concatenate([-sin, jnp.zeros((n, DH - half), F32)], axis=1)
    rsb = jnp.concatenate([jnp.zeros((n, half), F32), sin, jnp.zeros((n, DH - ROT), F32)], axis=1)
    return rc, rsa, rsb


def kernel(x, c, positions, g_norm, w_mod, b_mod, w_in, b_gate, conv_w, conv_b, w_a, b_a, w_x, b_x, lam, w_out_rnn, w_out_attn, w_o, g_final, loss_target, m_g_norm, m_w_mod, m_b_mod, m_w_in, m_b_gate, m_conv_w, m_conv_b, m_w_a, m_b_a, m_w_x, m_b_x, m_lam, m_w_out_rnn, m_w_out_attn, m_w_o, m_g_final, v_g_norm, v_w_mod, v_b_mod, v_w_in, v_b_gate, v_conv_w, v_conv_b, v_w_a, v_b_a, v_w_x, v_b_x, v_lam, v_w_out_rnn, v_w_out_attn, v_w_o, v_g_final):
    s = x.shape[1]
    xi = lax.axis_index("x")
    yi = lax.axis_index("y")
    ci = lax.axis_index("c")
    shard = 2 * xi + yi
    x2d = x[0]
    tgt = loss_target[0]
    pos = positions[0]

    c_all, mod4, conv_all = _mod_fwd(c, w_mod[0], b_mod.reshape(4, 1, 768), conv_w[0])
    mod = mod4.reshape(1, 3 * D)
    shift, scale, gate = mod[:, :D], mod[:, D:2 * D], mod[:, 2 * D:]
    w3sh = jnp.concatenate([w_out_rnn[0], w_out_attn[0], w_o[0]], axis=0).astype(BF16)
    wsh = w_in[0].astype(BF16)
    conv_full = conv_all[0::2].transpose(1, 0, 2).reshape(4, D)

    order = jnp.stack([shard, shard ^ 2, shard ^ 1, shard ^ 3]).astype(jnp.int32)
    p, hbf, wg = _gather_norm_inproj(x2d, g_norm, shift, scale, wsh, order)
    wg = lax.dynamic_update_slice(wg, wsh[None], (shard, 0, 0))
    rc, rsa, rsb = _rope_tables(pos)
    pos_col = pos.reshape(s, 1)
    b_a3, b_x3 = b_a.reshape(H, 1, DH), b_x.reshape(H, 1, DH)
    hr, gr = _rnn_fwd(p, pos_col, conv_full, conv_b, w_a[0], b_a3, w_x[0], b_x3, lam)
    o, lse, ga, w3g = _attn_fwd(p, rc, rsa, rsb, w3sh)
    w3g = lax.dynamic_update_slice(w3g, w3sh[None], (shard, 0, 0))
    w3 = w3g.reshape(4, 3, 256, D).transpose(1, 0, 2, 3).reshape(3, D, D)

    dgr, dga, dc, dx2, vec_t, g_out = _tail(gr, ga, p, x2d, tgt, w3, b_gate, gate, g_final.reshape(1, D))

    dxr, dzr, g_wa, g_ba, g_wx, g_bx, g_lam, g_cw, g_cb = _rnn_bwd(
        p, hr, dgr, pos_col, conv_full, conv_b, w_a[0], b_a3, w_x[0], b_x3, lam)
    dq, dk, dv, dza = _attn_bwd(p, o, lse, dga, rc, rsa, rsb)

    pieces = [dxr, dzr, dq, dk, dv, dza, dc]
    g_win = _inproj_bwd_w(pieces, hbf)

    core = ci.reshape(1)
    shard1 = shard.reshape(1)
    rb_a, rb_b = _pair_exchange([g_win, g_out])
    (q_a, qh_a), (q_b, qh_b) = _add_half(g_win, rb_a, core, tr=256), _add_half(g_out, rb_b, core, tr=128)
    grad_x, vec_n, (r_a, r_b) = _inproj_bwd_x(pieces, wg, x2d, dx2, g_norm, scale, [qh_a, qh_b])
    f_a, f_b = _sum_slots(q_a, r_a, shard1, tr=256), _sum_slots(q_b, r_b, shard1, tr=128)
    s_a, s_b = _pair_swap([f_a, f_b])
    south = ci == 0
    grad_w_in = jnp.where(south, jnp.concatenate([f_a, s_a], axis=0), jnp.concatenate([s_a, f_a], axis=0))
    g3 = jnp.where(south, jnp.concatenate([f_b, s_b], axis=0), jnp.concatenate([s_b, f_b], axis=0)).reshape(3, 256, D)

    dmod_row = jnp.concatenate([vec_n[0:1], vec_n[1:2], vec_t[1:2]], axis=1)
    vpack = jnp.concatenate([
        vec_n[2:3],
        vec_n[0:2], vec_t[1:2],
        vec_t[2:4],
        g_cb.reshape(1, D),
        g_lam.reshape(1, D),
        vec_t[0:1],
        g_cw.transpose(1, 0, 2).reshape(4, D),
        vec_t[4:5],
        jnp.zeros((V_ROWS - 14, D), F32)], axis=0)
    mpack = jnp.concatenate([
        g_wa.reshape(H * DH, DH), g_wx.reshape(H * DH, DH), g_ba.reshape(H, DH), g_bx.reshape(H, DH),
        jnp.zeros((M_ROWS - 2 * H * DH - 2 * H, DH), F32)], axis=0)
    redv, redm = _allreduce_small([vpack, mpack])
    loss = redv[V_LOSS, 0]
    grad_w_mod = _mod_bwd(dmod_row.reshape(4, 1, 768), c_all)
    g_conv_sh = lax.dynamic_slice_in_dim(redv[V_CONV_W:V_CONV_W + 4], shard * 256, 256, axis=1)

    shape2d = dict(g_norm=(1, D), b_mod=(1, 3 * D), b_gate=(1, 2 * D), conv_b=(1, D), lam=(1, D), g_final=(1, D),
                   conv_w=(4, 256), w_a=(H * DH, DH), w_x=(H * DH, DH), b_a=(H, DH), b_x=(H, DH))
    given = dict(
        g_norm=(g_norm, m_g_norm, v_g_norm), b_mod=(b_mod, m_b_mod, v_b_mod), b_gate=(b_gate, m_b_gate, v_b_gate),
        conv_b=(conv_b, m_conv_b, v_conv_b), lam=(lam, m_lam, v_lam), g_final=(g_final, m_g_final, v_g_final),
        conv_w=(conv_w, m_conv_w, v_conv_w), w_a=(w_a, m_w_a, v_w_a), w_x=(w_x, m_w_x, v_w_x),
        b_a=(b_a, m_b_a, v_b_a), b_x=(b_x, m_b_x, v_b_x))
    small = _adamw_small(redv, redm, g_conv_sh,
                         {n: tuple(a.reshape(shape2d[n]) for a in given[n]) for n in SMALL})

    big_in = _adamw(w_in[0], grad_w_in, m_w_in[0], v_w_in[0], tr=256)
    big_mod = _adamw(w_mod[0], grad_w_mod, m_w_mod[0], v_w_mod[0], tr=256)
    w3f = jnp.concatenate([w_out_rnn[0], w_out_attn[0], w_o[0]], axis=0)
    m3f = jnp.concatenate([m_w_out_rnn[0], m_w_out_attn[0], m_w_o[0]], axis=0)
    v3f = jnp.concatenate([v_w_out_rnn[0], v_w_out_attn[0], v_w_o[0]], axis=0)
    big_out = _adamw(w3f, g3.reshape(768, D), m3f, v3f, tr=256)

    names = ["g_norm", "w_mod", "b_mod", "w_in", "b_gate", "conv_w", "conv_b", "w_a", "b_a", "w_x", "b_x", "lam",
             "w_out_rnn", "w_out_attn", "w_o", "g_final"]
    outs = []
    for idx in range(4):
        d = {n: small[n][idx].reshape(given[n][0].shape) for n in SMALL}
        if idx == 0:
            d.update(w_mod=grad_w_mod[None], w_in=grad_w_in[None],
                     w_out_rnn=g3[0][None], w_out_attn=g3[1][None], w_o=g3[2][None])
        else:
            d.update(w_mod=big_mod[idx - 1][None], w_in=big_in[idx - 1][None],
                     w_out_rnn=big_out[idx - 1][0:256][None], w_out_attn=big_out[idx - 1][256:512][None],
                     w_o=big_out[idx - 1][512:768][None])
        outs.append(d)
    flat = [d[n] for d in outs for n in names]
    return (loss, grad_x[None], *flat)
```

```python
import jax
import jax.numpy as jnp
from jax import lax
from jax.experimental import pallas as pl
from jax.experimental.pallas import tpu as pltpu

F32, BF16 = jnp.float32, jnp.bfloat16
MESH = pl.DeviceIdType.MESH
HIGHEST = lax.Precision.HIGHEST

D = 1024
H = 8
DH = 128
PW = 2048
EPS = 1e-6
LRU_C = 8.0
SCALE = DH ** -0.5
NEG = -1e30
SPAN = 2048
UB = 128
DILATIONS = (1, 4, 16)
UNIT_BATCH = 8
ROPE_THETA = 500000.0
ROT = 32

LR, B1, B2, ADAM_EPS, WD, STEP = 0.001, 0.9, 0.999, 1e-08, 0.01, 10

NDEV = 8


def _params(sem=None, vmem_mb=56):
    return pltpu.CompilerParams(dimension_semantics=sem, vmem_limit_bytes=vmem_mb * 2 ** 20)


def _coords():
    return lax.axis_index("x"), lax.axis_index("y"), lax.axis_index("c")


def _flip(v, bit):
    return 1 - v if bit else v


def _peer(k):
    x, y, c = _coords()
    return (_flip(x, (k >> 2) & 1), _flip(y, (k >> 1) & 1), _flip(c, k & 1))


def _my_index():
    x, y, c = _coords()
    return 4 * x + 2 * y + c


def _rcopy(src, dst, ssem, rsem, dev):
    return pltpu.make_async_remote_copy(src_ref=src, dst_ref=dst, send_sem=ssem, recv_sem=rsem,
                                        device_id=dev, device_id_type=MESH)


def _sigmoid(x):
    return jax.nn.sigmoid(x)


def _dot(a, b):
    return jnp.dot(a, b, preferred_element_type=F32)


def _dot_nt(a, b):
    return lax.dot_general(a, b, (((1,), (1,)), ((), ())), preferred_element_type=F32)


def _dot_tn(a, b):
    return lax.dot_general(a, b, (((0,), (0,)), ((), ())), preferred_element_type=F32)


def _colsum(a):
    return jnp.sum(a, axis=0, keepdims=True)


def _mod_fwd(c, w_mod_sh, b_mod4, conv_sh):
    def body(c_ref, w_ref, b_ref, cv_ref, call_ref, mod_ref, cvall_ref, rows_ref, cmat_ref, s1, r1, s2, r2, s3, r3):
        x, y, _ = _coords()
        me = _my_index()
        j = 2 * x + y
        call_ref[me] = c_ref[...]
        cvall_ref[me] = cv_ref[...]
        sends = []
        for k in range(1, NDEV):
            cp = _rcopy(call_ref.at[me], call_ref.at[me], s1.at[k - 1], r1.at[k - 1], _peer(k))
            cp.start()
            sends.append(cp)
            cp = _rcopy(cvall_ref.at[me], cvall_ref.at[me], s3.at[k - 1], r3.at[k - 1], _peer(k))
            cp.start()
            sends.append(cp)
        for k in range(1, NDEV):
            pk = me ^ k
            _rcopy(call_ref.at[pk], call_ref.at[pk], s1.at[k - 1], r1.at[k - 1], _peer(k)).wait_recv()
        for b in range(NDEV):
            cmat_ref[pl.ds(b, 1), :] = call_ref[b]
        cm = cmat_ref[...]
        act = cm * _sigmoid(cm)
        mp = jnp.dot(act, w_ref[...], preferred_element_type=F32, precision=HIGHEST) + b_ref[j]
        for b in range(NDEV):
            rows_ref[b] = mp[b:b + 1]
        mod_ref[j] = rows_ref[me]
        for q, k in enumerate((2, 4, 6)):
            cp = _rcopy(rows_ref.at[me ^ k], mod_ref.at[j], s2.at[q], r2.at[q], _peer(k))
            cp.start()
            sends.append(cp)
        for q, k in enumerate((2, 4, 6)):
            jq = j ^ (k >> 1)
            _rcopy(rows_ref.at[me], mod_ref.at[jq], s2.at[q], r2.at[q], _peer(k)).wait_recv()
        for k in range(1, NDEV):
            pk = me ^ k
            _rcopy(cvall_ref.at[pk], cvall_ref.at[pk], s3.at[k - 1], r3.at[k - 1], _peer(k)).wait_recv()
        for cp in sends:
            cp.wait_send()

    vm = pl.BlockSpec(memory_space=pltpu.VMEM)
    return pl.pallas_call(
        body, name="mod_fwd",
        out_shape=(jax.ShapeDtypeStruct((NDEV, 1, D), F32), jax.ShapeDtypeStruct((4, 1, 768), F32),
                   jax.ShapeDtypeStruct((NDEV,) + conv_sh.shape, F32)),
        in_specs=[vm, vm, vm, vm], out_specs=(vm, vm, vm),
        scratch_shapes=[pltpu.VMEM((NDEV, 1, 768), F32), pltpu.VMEM((NDEV, D), F32),
                        pltpu.SemaphoreType.DMA((7,)), pltpu.SemaphoreType.DMA((7,)),
                        pltpu.SemaphoreType.DMA((3,)), pltpu.SemaphoreType.DMA((3,)),
                        pltpu.SemaphoreType.DMA((7,)), pltpu.SemaphoreType.DMA((7,))],
        compiler_params=_params(),
    )(c, w_mod_sh, b_mod4, conv_sh)


def _mod_bwd(dmod4, c_all):
    def body(d_ref, call_ref, gw_ref, dall_ref, cmat_ref, dmat_ref, s1, r1):
        x, y, _ = _coords()
        me = _my_index()
        j = 2 * x + y
        dall_ref[me] = d_ref[...]
        sends = []
        for k in range(1, NDEV):
            cp = _rcopy(dall_ref.at[me], dall_ref.at[me], s1.at[k - 1], r1.at[k - 1], _peer(k))
            cp.start()
            sends.append(cp)
        for k in range(1, NDEV):
            pk = me ^ k
            _rcopy(dall_ref.at[pk], dall_ref.at[pk], s1.at[k - 1], r1.at[k - 1], _peer(k)).wait_recv()
        for cp in sends:
            cp.wait_send()
        for b in range(NDEV):
            cmat_ref[pl.ds(b, 1), :] = call_ref[b]
            dmat_ref[pl.ds(b, 1), :] = dall_ref[b, j]
        cm = cmat_ref[...]
        act = cm * _sigmoid(cm)
        gw_ref[...] = lax.dot_general(act, dmat_ref[...], (((0,), (0,)), ((), ())),
                                      preferred_element_type=F32, precision=HIGHEST)

    vm = pl.BlockSpec(memory_space=pltpu.VMEM)
    return pl.pallas_call(
        body, name="mod_bwd",
        out_shape=jax.ShapeDtypeStruct((D, 768), F32),
        in_specs=[vm, vm], out_specs=vm,
        scratch_shapes=[pltpu.VMEM((NDEV, 4, 1, 768), F32), pltpu.VMEM((NDEV, D), F32), pltpu.VMEM((NDEV, 768), F32),
                        pltpu.SemaphoreType.DMA((7,)), pltpu.SemaphoreType.DMA((7,))],
        compiler_params=_params(),
    )(dmod4, c_all)


def _gather_norm_inproj(x, gn, shift, scale, wsh, order, tm=1024, tn=1024):
    s = x.shape[0]
    ni = s // tm
    npc = PW // tn
    rows, cols = wsh.shape
    half = rows // 2
    nch = 4
    cr = half // nch
    chips = ((1, 0), (0, 1), (1, 1))

    def body(ord_ref, x_ref, gn_ref, sh_ref, sc_ref, wsh_hbm, p_ref, h_ref, wg_hbm, hs_all, w_s, wsem, ss, rs):
        slot, i, col = pl.program_id(0), pl.program_id(1), pl.program_id(2)
        cx, cy, cc = _coords()
        j = 2 * cx + cy
        sib = (cx, cy, 1 - cc)
        mine = lambda n: pl.ds(cc * half + n * cr, cr)
        theirs = lambda n: pl.ds((1 - cc) * half + n * cr, cr)
        shard_of = lambda q: j ^ (2 * chips[q][0] + chips[q][1])

        def to_chip(q, n):
            e = nch * q + n
            return _rcopy(wsh_hbm.at[mine(n)], wg_hbm.at[j, mine(n)], ss.at[e], rs.at[e],
                          (_flip(cx, chips[q][0]), _flip(cy, chips[q][1]), cc))

        def from_chip(q, n):
            e = nch * q + n
            return _rcopy(wsh_hbm.at[mine(n)], wg_hbm.at[shard_of(q), mine(n)], ss.at[e], rs.at[e], sib)

        def to_sibling(q, n):
            e = 3 * nch + nch * q + n
            return _rcopy(wg_hbm.at[shard_of(q), mine(n)], wg_hbm.at[shard_of(q), mine(n)], ss.at[e], rs.at[e], sib)

        def from_sibling(q, n):
            e = 3 * nch + nch * q + n
            return _rcopy(wsh_hbm.at[mine(n)], wg_hbm.at[shard_of(q), theirs(n)], ss.at[e], rs.at[e], sib)

        def load(sl, src):
            cp = pltpu.make_async_copy(src, w_s.at[sl], wsem.at[sl])
            cp.start()
            cp.wait()

        first = (i == 0) & (col == 0)

        @pl.when(first & (slot == 0))
        def _():
            for n in range(nch):
                for q in (0, 1):
                    to_chip(q, n).start()
            load(0, wsh_hbm.at[pl.ds(0, D), :])

        @pl.when(first & (slot == 1))
        def _():
            for q in (0, 1):
                for n in range(nch):
                    from_chip(q, n).wait_recv()
                    to_sibling(q, n).start()
            for n in range(nch):
                to_chip(2, n).start()
            for n in range(nch):
                from_sibling(0, n).wait_recv()
            load(1, wg_hbm.at[shard_of(0), pl.ds(0, D), :])

        @pl.when(first & (slot == 2))
        def _():
            for n in range(nch):
                from_sibling(1, n).wait_recv()
            load(2, wg_hbm.at[shard_of(1), pl.ds(0, D), :])

        @pl.when(first & (slot == 3))
        def _():
            for n in range(nch):
                from_chip(2, n).wait_recv()
                to_sibling(2, n).start()
            for n in range(nch):
                from_sibling(2, n).wait_recv()
            load(3, wg_hbm.at[shard_of(2), pl.ds(0, D), :])
            for q in range(3):
                for n in range(nch):
                    to_chip(q, n).wait_send()
                    to_sibling(q, n).wait_send()

        @pl.when((slot == 0) & (col == 0))
        def _():
            xt = x_ref[...]
            rstd = lax.rsqrt(jnp.mean(xt * xt, axis=-1, keepdims=True) + EPS)
            h = ((xt * rstd * gn_ref[...]) * (1.0 + sc_ref[...]) + sh_ref[...]).astype(BF16)
            hs_all[i] = h
            h_ref[...] = h

        p_ref[0] = _dot(hs_all[i], w_s[slot, :, pl.ds(pl.multiple_of(col * tn, tn), tn)]).astype(BF16)

    row = pl.BlockSpec((1, D), lambda sl, i, col, o: (0, 0))
    x_rows = lambda sl, i, col, o: (jnp.where(sl == 0, i, ni - 1), 0)
    any_ = pl.BlockSpec(memory_space=pl.ANY)
    return pl.pallas_call(
        body, name="gather_norm_inproj",
        grid_spec=pltpu.PrefetchScalarGridSpec(
            num_scalar_prefetch=1, grid=(4, ni, npc),
            in_specs=[pl.BlockSpec((tm, D), x_rows), row, row, row, any_],
            out_specs=(pl.BlockSpec((1, tm, tn), lambda sl, i, col, o: (o[sl], i, col)),
                       pl.BlockSpec((tm, D), x_rows), any_),
            scratch_shapes=[pltpu.VMEM((ni, tm, D), BF16), pltpu.VMEM((4, D, PW), BF16),
                            pltpu.SemaphoreType.DMA((4,)),
                            pltpu.SemaphoreType.DMA((6 * nch,)), pltpu.SemaphoreType.DMA((6 * nch,))]),
        out_shape=(jax.ShapeDtypeStruct((4, s, PW), BF16), jax.ShapeDtypeStruct((s, D), BF16),
                   jax.ShapeDtypeStruct((4, rows, cols), wsh.dtype)),
        compiler_params=_params(("arbitrary", "arbitrary", "arbitrary")),
    )(order, x, gn, shift, scale, wsh)


def _shift_down(prev8, cur, d):
    t = cur.shape[0]
    c3 = cur.reshape(t // 8, 8, DH)
    rot = pltpu.roll(c3, d, 1)
    before = jnp.concatenate([pltpu.roll(prev8, d, 0).reshape(1, 8, DH), rot[:-1]], axis=0)
    rows = lax.broadcasted_iota(jnp.int32, c3.shape, 1)
    return jnp.where(rows >= d, rot, before).reshape(t, DH)


def _shift_up(cur, next8, d):
    t = cur.shape[0]
    c3 = cur.reshape(t // 8, 8, DH)
    rot = pltpu.roll(c3, 8 - d, 1)
    after = jnp.concatenate([rot[1:], pltpu.roll(next8, 8 - d, 0).reshape(1, 8, DH)], axis=0)
    rows = lax.broadcasted_iota(jnp.int32, c3.shape, 1)
    return jnp.where(rows < 8 - d, rot, after).reshape(t, DH)


def _rnn_gates(xr, prev8, cw, cb, wa, ba, wx, bx, lam, reset):
    xc = cw[3:4] * xr + cb
    for d in (1, 2, 3):
        xc = xc + cw[3 - d:4 - d] * _shift_down(prev8, xr, d)
    xcb = xc.astype(BF16)
    r = _sigmoid(_dot(xcb, wa.astype(BF16)) + ba)
    ig = _sigmoid(_dot(xcb, wx.astype(BF16)) + bx)
    nl = -lam
    sp = jnp.maximum(nl, 0.0) + jnp.log1p(jnp.exp(-jnp.abs(nl)))
    log_a = (-LRU_C * r) * sp
    a = jnp.where(reset, 0.0, jnp.exp(log_a))
    mult = jnp.where(reset, 1.0, jnp.sqrt(1.0 - jnp.exp(2.0 * log_a)))
    return xc, r, ig, sp, a, mult


def _log_scan(a, b, axis, up):
    n = a.shape[axis]
    rows = lax.broadcasted_iota(jnp.int32, a.shape, axis)
    d = 1
    while d < n:
        m = rows < n - d if up else rows >= d
        shift = n - d if up else d
        a_s = pltpu.roll(a, shift, axis)
        b_s = pltpu.roll(b, shift, axis)
        b = jnp.where(m, a * b_s + b, b)
        a = jnp.where(m, a * a_s, a)
        d *= 2
    return a, b


def _scan(a, b, t, edge, up=False):
    g = t // 8
    a3, b3 = _log_scan(a.reshape(g, 8, DH), b.reshape(g, 8, DH), 1, up)
    last = 0 if up else 7
    ag, bg = _log_scan(a3[:, last, :], b3[:, last, :], 0, up)
    hg = ag * edge + bg
    grp = lax.broadcasted_iota(jnp.int32, hg.shape, 0)
    if up:
        cin = jnp.where(grp == g - 1, edge, pltpu.roll(hg, g - 1, 0))
        tail = hg[0:1]
    else:
        cin = jnp.where(grp == 0, edge, pltpu.roll(hg, 1, 0))
        tail = hg[g - 1:g]
    return (a3 * cin[:, None, :] + b3).reshape(t, DH), tail


def _rnn_fwd(p, pos, conv_w, conv_b, w_a, b_a, w_x, b_x, lam, tt=512):
    s = p.shape[1]
    nt = s // tt

    def body(xr_ref, z_ref, pos_ref, cw_ref, cb_ref, wa_ref, ba_ref, wx_ref, bx_ref, lam_ref,
             hr_ref, gr_ref, xprev, hprev):
        @pl.when(pl.program_id(1) == 0)
        def _():
            xprev[...] = jnp.zeros_like(xprev)
            hprev[...] = jnp.zeros_like(hprev)

        xr = xr_ref[0].astype(F32)
        z = z_ref[0].astype(F32)
        reset = pos_ref[...] == 0
        xc, r, ig, sp, a, mult = _rnn_gates(xr, xprev[...], cw_ref[...], cb_ref[...], wa_ref[0], ba_ref[0],
                                            wx_ref[0], bx_ref[0], lam_ref[...], reset)
        bx = mult * ig * xc
        h, h_last = _scan(a, bx, tt, hprev[0:1])
        xprev[...] = xr[tt - 8:]
        hprev[...] = jnp.broadcast_to(h_last, (8, DH))
        hr_ref[...] = h
        gr_ref[...] = (h * (z * _sigmoid(z))).astype(BF16)

    head_row = lambda hh, t: (0, hh)
    return pl.pallas_call(
        body, name="rnn_fwd", grid=(H, nt),
        in_specs=[pl.BlockSpec((1, tt, DH), lambda hh, t: (0, t, hh)),
                  pl.BlockSpec((1, tt, DH), lambda hh, t: (0, t, H + hh)),
                  pl.BlockSpec((tt, 1), lambda hh, t: (t, 0)),
                  pl.BlockSpec((4, DH), head_row), pl.BlockSpec((1, DH), head_row),
                  pl.BlockSpec((1, DH, DH), lambda hh, t: (hh, 0, 0)), pl.BlockSpec((1, 1, DH), lambda hh, t: (hh, 0, 0)),
                  pl.BlockSpec((1, DH, DH), lambda hh, t: (hh, 0, 0)), pl.BlockSpec((1, 1, DH), lambda hh, t: (hh, 0, 0)),
                  pl.BlockSpec((1, DH), head_row)],
        out_specs=(pl.BlockSpec((tt, DH), lambda hh, t: (t, hh)), pl.BlockSpec((tt, DH), lambda hh, t: (t, hh))),
        out_shape=(jax.ShapeDtypeStruct((s, D), F32), jax.ShapeDtypeStruct((s, D), BF16)),
        scratch_shapes=[pltpu.VMEM((8, DH), F32), pltpu.VMEM((8, DH), F32)],
        compiler_params=_params(("parallel", "arbitrary")),
    )(p, p, pos, conv_w, conv_b, w_a, b_a, w_x, b_x, lam)


def _rnn_bwd(p, hr, dgr, pos, conv_w, conv_b, w_a, b_a, w_x, b_x, lam, tt=512):
    s = p.shape[1]
    nt = s // tt
    t8 = tt // 8

    def body(xr_ref, z_ref, xp_ref, hr_ref, hp_ref, dg_ref, pos_ref, cw_ref, cb_ref, wa_ref, ba_ref, wx_ref, bx_ref,
             lam_ref, dxr_ref, dz_ref, gwa_ref, gba_ref, gwx_ref, gbx_ref, glam_ref, gcw_ref, gcb_ref,
             a_next, g_next, dxc_next):
        t = pl.program_id(1)
        has_prev = t < nt - 1

        @pl.when(t == 0)
        def _():
            a_next[...] = jnp.zeros_like(a_next)
            g_next[...] = jnp.zeros_like(g_next)
            dxc_next[...] = jnp.zeros_like(dxc_next)
            gwa_ref[...] = jnp.zeros_like(gwa_ref)
            gba_ref[...] = jnp.zeros_like(gba_ref)
            gwx_ref[...] = jnp.zeros_like(gwx_ref)
            gbx_ref[...] = jnp.zeros_like(gbx_ref)
            glam_ref[...] = jnp.zeros_like(glam_ref)
            gcw_ref[...] = jnp.zeros_like(gcw_ref)
            gcb_ref[...] = jnp.zeros_like(gcb_ref)

        xr = xr_ref[0].astype(F32)
        z = z_ref[0].astype(F32)
        hr_blk = hr_ref[...]
        dg = dg_ref[...]
        xprev = jnp.where(has_prev, xp_ref[0].astype(F32)[8:], 0.0)
        hprev8 = jnp.where(has_prev, hp_ref[...], 0.0)
        reset = pos_ref[...] == 0
        cw = cw_ref[...]
        wa = wa_ref[0]
        wx = wx_ref[0]
        lam_v = lam_ref[...]
        xc, r, ig, sp, a, mult = _rnn_gates(xr, xprev, cw, cb_ref[...], wa, ba_ref[0], wx, bx_ref[0], lam_v, reset)

        sz = _sigmoid(z)
        dh = dg * (z * sz)
        dz_ref[...] = (dg * hr_blk * (sz * (1.0 + z * (1.0 - sz)))).astype(BF16)

        an = _shift_up(a, a_next[...], 1)
        g, g_first = _scan(an, dh, tt, g_next[0:1], up=True)
        a_next[...] = jnp.broadcast_to(a[0:1], (8, DH))
        g_next[...] = jnp.broadcast_to(g_first, (8, DH))

        hm1 = _shift_down(hprev8, hr_blk, 1)
        da = g * hm1
        dmult = g * (ig * xc)
        di = g * (mult * xc)
        dxc = g * (mult * ig)
        dla = jnp.where(reset, 0.0, da * a - dmult * (a * a) / mult)
        dr = dla * (-LRU_C * sp)
        dsp = _colsum(dla * (-LRU_C * r))
        glam_ref[0] += dsp * (-_sigmoid(-lam_v))
        dpa = dr * r * (1.0 - r)
        dpx = di * ig * (1.0 - ig)
        dpab = dpa.astype(BF16)
        dpxb = dpx.astype(BF16)
        dxc = dxc + _dot_nt(dpab, wa.astype(BF16)) + _dot_nt(dpxb, wx.astype(BF16))
        xcb = xc.astype(BF16)
        gwa_ref[0] += _dot_tn(xcb, dpab)
        gwx_ref[0] += _dot_tn(xcb, dpxb)
        gba_ref[0] += _colsum(dpa)
        gbx_ref[0] += _colsum(dpx)

        dxr = cw[3:4] * dxc
        for d in (1, 2, 3):
            dxr = dxr + cw[3 - d:4 - d] * _shift_up(dxc, dxc_next[...], d)
        dxr_ref[...] = dxr.astype(BF16)
        dxc_next[...] = dxc[0:8]
        gcb_ref[0] += _colsum(dxc)
        gcw_ref[0, 3:4, :] += _colsum(xr * dxc)
        for d in (1, 2, 3):
            gcw_ref[0, 3 - d:4 - d, :] += _colsum(_shift_down(xprev, xr, d) * dxc)

    rt = lambda t: nt - 1 - t
    prev8 = lambda t: jnp.maximum(rt(t) * t8 - 1, 0)
    head_row = lambda hh, t: (0, hh)
    hsm = lambda hh, t: (hh, 0, 0)
    return pl.pallas_call(
        body, name="rnn_bwd", grid=(H, nt),
        in_specs=[pl.BlockSpec((1, tt, DH), lambda hh, t: (0, rt(t), hh)),
                  pl.BlockSpec((1, tt, DH), lambda hh, t: (0, rt(t), H + hh)),
                  pl.BlockSpec((1, 16, DH), lambda hh, t: (0, jnp.maximum(rt(t) * (tt // 16) - 1, 0), hh)),
                  pl.BlockSpec((tt, DH), lambda hh, t: (rt(t), hh)),
                  pl.BlockSpec((8, DH), lambda hh, t: (prev8(t), hh)),
                  pl.BlockSpec((tt, DH), lambda hh, t: (rt(t), hh)),
                  pl.BlockSpec((tt, 1), lambda hh, t: (rt(t), 0)),
                  pl.BlockSpec((4, DH), head_row), pl.BlockSpec((1, DH), head_row),
                  pl.BlockSpec((1, DH, DH), hsm), pl.BlockSpec((1, 1, DH), hsm),
                  pl.BlockSpec((1, DH, DH), hsm), pl.BlockSpec((1, 1, DH), hsm),
                  pl.BlockSpec((1, DH), head_row)],
        out_specs=(pl.BlockSpec((tt, DH), lambda hh, t: (rt(t), hh)), pl.BlockSpec((tt, DH), lambda hh, t: (rt(t), hh)),
                   pl.BlockSpec((1, DH, DH), hsm), pl.BlockSpec((1, 1, DH), hsm),
                   pl.BlockSpec((1, DH, DH), hsm), pl.BlockSpec((1, 1, DH), hsm),
                   pl.BlockSpec((1, 1, DH), hsm), pl.BlockSpec((1, 4, DH), hsm), pl.BlockSpec((1, 1, DH), hsm)),
        out_shape=(jax.ShapeDtypeStruct((s, D), BF16), jax.ShapeDtypeStruct((s, D), BF16),
                   jax.ShapeDtypeStruct((H, DH, DH), F32), jax.ShapeDtypeStruct((H, 1, DH), F32),
                   jax.ShapeDtypeStruct((H, DH, DH), F32), jax.ShapeDtypeStruct((H, 1, DH), F32),
                   jax.ShapeDtypeStruct((H, 1, DH), F32), jax.ShapeDtypeStruct((H, 4, DH), F32),
                   jax.ShapeDtypeStruct((H, 1, DH), F32)),
        scratch_shapes=[pltpu.VMEM((8, DH), F32), pltpu.VMEM((8, DH), F32), pltpu.VMEM((8, DH), F32)],
        compiler_params=_params(("parallel", "arbitrary")),
    )(p, p, p, hr, hr, dgr, pos, conv_w, conv_b, w_a, b_a, w_x, b_x, lam)


def _rope(t, c, sa, sb):
    return t * c + pltpu.roll(t, DH - ROT // 2, 1) * sa + pltpu.roll(t, ROT // 2, 1) * sb


def _rope_bwd(g, c, sa, sb):
    return g * c + pltpu.roll(g * sa, ROT // 2, 1) + pltpu.roll(g * sb, DH - ROT // 2, 1)


def _unit_bases(gi, u):
    dil = DILATIONS[gi]
    if dil == 1:
        return u * UB, SPAN + (u - 1) * UB, u == 0
    if dil == 4:
        blk, r = u // 4, u % 4
        return blk * 4 * UB + r, SPAN + (blk - 1) * 4 * UB + r, blk == 0
    return u, u, True


def _unit_slices(gi, u):
    dil = DILATIONS[gi]
    qb0, kb0, first = _unit_bases(gi, u)
    if dil == 1:
        return pl.ds(pl.multiple_of(qb0, UB), UB), pl.ds(pl.multiple_of(kb0, UB), 2 * UB), first
    return pl.ds(qb0, UB, stride=dil), pl.ds(kb0, 2 * UB, stride=dil), first


def _bdot(a, b):
    return lax.dot_general(a, b, (((2,), (1,)), ((0,), (0,))), preferred_element_type=F32)


def _bdot_nt(a, b):
    return lax.dot_general(a, b, (((2,), (2,)), ((0,), (0,))), preferred_element_type=F32)


def _bdot_tn(a, b):
    return lax.dot_general(a, b, (((1,), (1,)), ((0,), (0,))), preferred_element_type=F32)


def _band_mask(first_in_span, has_prev):
    qi = lax.broadcasted_iota(jnp.int32, (UB, 2 * UB), 0)
    ki = lax.broadcasted_iota(jnp.int32, (UB, 2 * UB), 1)
    dist = UB + qi - ki
    band = (dist >= 0) & (dist <= UB)
    return band & ((ki >= UB) | jnp.logical_not(first_in_span) | has_prev)


def _gather_halves(phase, src_hbm, dst_hbm, ss, rs):
    half = src_hbm.shape[0] // 2
    cx, cy, cc = _coords()
    j = 2 * cx + cy
    sib = (cx, cy, 1 - cc)
    mine = pl.ds(cc * half, half)
    theirs = pl.ds((1 - cc) * half, half)
    chips = ((1, 0), (0, 1), (1, 1))
    for q, (kx, ky) in enumerate(chips):
        jq = j ^ (2 * kx + ky)
        out = _rcopy(src_hbm.at[mine], dst_hbm.at[j, mine], ss.at[q], rs.at[q], (_flip(cx, kx), _flip(cy, ky), cc))
        landed = _rcopy(src_hbm.at[mine], dst_hbm.at[jq, mine], ss.at[q], rs.at[q], sib)
        onward = _rcopy(dst_hbm.at[jq, mine], dst_hbm.at[jq, mine], ss.at[3 + q], rs.at[3 + q], sib)
        from_sib = _rcopy(src_hbm.at[mine], dst_hbm.at[jq, theirs], ss.at[3 + q], rs.at[3 + q], sib)
        if phase == 0:
            out.start()
        elif phase == 1:
            landed.wait_recv()
            onward.start()
        else:
            from_sib.wait_recv()
            out.wait_send()
            onward.wait_send()


def _attn_fwd(p, rc, rsa, rsb, w3sh):
    s = p.shape[1]
    ns = s // SPAN
    nunit = SPAN // UB

    def body(q_ref, k_ref, v_ref, z_ref, c_ref, sa_ref, sb_ref, w3_hbm, o_ref, lse_ref, ga_ref, w3g_hbm,
             qr, kf, vf, acc, mm, ll, ss, rs):
        hh, n = pl.program_id(0), pl.program_id(1)
        for phase, at_head, at_span in ((0, 0, 0), (1, H // 2, 0), (2, H - 1, ns - 1)):
            @pl.when((hh == at_head) & (n == at_span))
            def _(phase=phase):
                _gather_halves(phase, w3_hbm, w3g_hbm, ss, rs)

        @pl.when(n == 0)
        def _():
            kf[0:SPAN] = jnp.zeros((SPAN, DH), F32)
            vf[0:SPAN] = jnp.zeros((SPAN, DH), F32)

        c, sa, sb = c_ref[...], sa_ref[...], sb_ref[...]
        qr[...] = _rope(q_ref[0].astype(F32), c, sa, sb)
        kf[SPAN:] = _rope(k_ref[0].astype(F32), c, sa, sb)
        vf[SPAN:] = v_ref[0].astype(F32)
        has_prev = n > 0

        for gi, dil in enumerate(DILATIONS):
            def trip(t, carry, gi=gi, dil=dil):
                qsls, ksls, firsts = [], [], []
                for b in range(UNIT_BATCH):
                    qsl, ksl, first = _unit_slices(gi, t * UNIT_BATCH + b)
                    qsls.append(qsl)
                    ksls.append(ksl)
                    firsts.append(first)
                qb = jnp.stack([qr[qsl, :].astype(BF16) for qsl in qsls])
                kb = jnp.stack([kf[ksl, :].astype(BF16) for ksl in ksls])
                vb = jnp.stack([vf[ksl, :].astype(BF16) for ksl in ksls])
                s_all = _bdot_nt(qb, kb)
                prs = []
                for b in range(UNIT_BATCH):
                    sc = jnp.where(_band_mask(firsts[b], has_prev), s_all[b] * SCALE, NEG)
                    m = jnp.max(sc, axis=-1, keepdims=True)
                    pr = jnp.exp(sc - m)
                    l = jnp.sum(pr, axis=-1, keepdims=True)
                    mm[gi, qsls[b], :] = jnp.broadcast_to(m, (UB, DH))
                    ll[gi, qsls[b], :] = jnp.broadcast_to(l, (UB, DH))
                    prs.append(pr.astype(BF16))
                o_all = _bdot(jnp.stack(prs), vb)
                for b in range(UNIT_BATCH):
                    acc[gi, qsls[b], :] = o_all[b]
                return carry

            lax.fori_loop(0, nunit // UNIT_BATCH, trip, 0)

        m_all =jnp.maximum(jnp.maximum(mm[0], mm[1]), mm[2])
        num = jnp.zeros((SPAN, DH), F32)
        den = jnp.zeros((SPAN, DH), F32)
        for gi in range(3):
            w = jnp.exp(mm[gi] - m_all)
            num = num + w * acc[gi]
            den = den + w * ll[gi]
        o = num / den
        o_ref[...] = o
        lse_ref[...] = m_all + jnp.log(den)
        z = z_ref[0].astype(F32)
        ga_ref[...] = (o * (z * _sigmoid(z))).astype(BF16)
        kf[0:SPAN] = kf[SPAN:]
        vf[0:SPAN] = vf[SPAN:]

    blk = lambda piece, off: pl.BlockSpec((1, SPAN, DH), lambda hh, n: (piece, n, off + hh))
    tab = pl.BlockSpec((SPAN, DH), lambda hh, n: (n, 0))
    outb = pl.BlockSpec((SPAN, DH), lambda hh, n: (n, hh))
    any_ = pl.BlockSpec(memory_space=pl.ANY)
    return pl.pallas_call(
        body, name="attn_fwd", grid=(H, ns),
        in_specs=[blk(1, 0), blk(1, H), blk(2, 0), blk(2, H), tab, tab, tab, any_],
        out_specs=(outb, outb, outb, any_),
        out_shape=(jax.ShapeDtypeStruct((s, D), F32), jax.ShapeDtypeStruct((s, D), F32),
                   jax.ShapeDtypeStruct((s, D), BF16), jax.ShapeDtypeStruct((4,) + w3sh.shape, w3sh.dtype)),
        scratch_shapes=[pltpu.VMEM((SPAN, DH), F32), pltpu.VMEM((2 * SPAN, DH), F32), pltpu.VMEM((2 * SPAN, DH), F32),
                        pltpu.VMEM((3, SPAN, DH), F32), pltpu.VMEM((3, SPAN, DH), F32), pltpu.VMEM((3, SPAN, DH), F32),
                        pltpu.SemaphoreType.DMA((6,)), pltpu.SemaphoreType.DMA((6,))],
        compiler_params=_params(("arbitrary", "arbitrary")),
    )(p, p, p, p, rc, rsa, rsb, w3sh)


def _attn_bwd(p, o, lse, dga, rc, rsa, rsb):
    s = p.shape[1]
    ns = s // SPAN
    nunit = SPAN // UB

    def body(q_ref, k_ref, kp_ref, v_ref, vp_ref, z_ref, c_ref, sa_ref, sb_ref, cp_ref, sap_ref, sbp_ref,
             o_ref, lse_ref, dg_ref, dq_ref, dk_ref, dv_ref, dz_ref,
             qr, kf, vf, dof, dlt, dqa, dkf, dvf):
        step = pl.program_id(1)
        n = ns - 1 - step
        has_prev = n > 0

        @pl.when(step == 0)
        def _():
            dkf[...] = jnp.zeros_like(dkf)
            dvf[...] = jnp.zeros_like(dvf)

        @pl.when(step > 0)
        def _():
            dkf[SPAN:] = dkf[0:SPAN]
            dvf[SPAN:] = dvf[0:SPAN]
            dkf[0:SPAN] = jnp.zeros((SPAN, DH), F32)
            dvf[0:SPAN] = jnp.zeros((SPAN, DH), F32)

        c, sa, sb = c_ref[...], sa_ref[...], sb_ref[...]
        qr[...] = _rope(q_ref[0].astype(F32), c, sa, sb)
        kf[SPAN:] = _rope(k_ref[0].astype(F32), c, sa, sb)
        vf[SPAN:] = v_ref[0].astype(F32)
        kf[0:SPAN] = jnp.where(has_prev, _rope(kp_ref[0].astype(F32), cp_ref[...], sap_ref[...], sbp_ref[...]), 0.0)
        vf[0:SPAN] = jnp.where(has_prev, vp_ref[0].astype(F32), 0.0)
        z = z_ref[0].astype(F32)
        sz = _sigmoid(z)
        dg = dg_ref[...]
        ov = o_ref[...]
        do = dg * (z * sz)
        dz_ref[...] = (dg * ov * (sz * (1.0 + z * (1.0 - sz)))).astype(BF16)
        dof[...] = do
        dlt[...] = jnp.broadcast_to(jnp.sum(do * ov, axis=-1, keepdims=True), (SPAN, DH))
        dqa[...] = jnp.zeros_like(dqa)

        for gi, dil in enumerate(DILATIONS):
            def trip(t, carry, gi=gi, dil=dil):
                qsls, ksls, firsts = [], [], []
                for b in range(UNIT_BATCH):
                    qsl, ksl, first = _unit_slices(gi, t * UNIT_BATCH + b)
                    qsls.append(qsl)
                    ksls.append(ksl)
                    firsts.append(first)
                qb = jnp.stack([qr[qsl, :].astype(BF16) for qsl in qsls])
                kb = jnp.stack([kf[ksl, :].astype(BF16) for ksl in ksls])
                vb = jnp.stack([vf[ksl, :].astype(BF16) for ksl in ksls])
                dob = jnp.stack([dof[qsl, :].astype(BF16) for qsl in qsls])
                s_all = _bdot_nt(qb, kb)
                dp_all = _bdot_nt(dob, vb)
                prs, dss = [], []
                for b in range(UNIT_BATCH):
                    lse_b = lse_ref[qsls[b], :]
                    dl_b = dlt[qsls[b], :]
                    pr = jnp.exp(s_all[b] * SCALE - jnp.concatenate([lse_b, lse_b], axis=1))
                    pr = jnp.where(_band_mask(firsts[b], has_prev), pr, 0.0)
                    prs.append(pr.astype(BF16))
                    dss.append((pr * (dp_all[b] - jnp.concatenate([dl_b, dl_b], axis=1)) * SCALE).astype(BF16))
                ds_all = jnp.stack(dss)
                dv_all = _bdot_tn(jnp.stack(prs), dob)
                dq_all = _bdot(ds_all, kb)
                dk_all = _bdot_tn(ds_all, qb)
                for b in range(UNIT_BATCH):
                    dvf[ksls[b], :] += dv_all[b]
                    dqa[qsls[b], :] += dq_all[b]
                    dkf[ksls[b], :] += dk_all[b]
                return carry

            lax.fori_loop(0, nunit // UNIT_BATCH, trip, 0)

        dq_ref[...] = _rope_bwd(dqa[...], c, sa, sb).astype(BF16)
        dk_ref[...] = _rope_bwd(dkf[SPAN:], c, sa, sb).astype(BF16)
        dv_ref[...] = dvf[SPAN:].astype(BF16)

    rn = lambda n: ns - 1 - n
    pn = lambda n: jnp.maximum(ns - 2 - n, 0)
    blk = lambda piece, off: pl.BlockSpec((1, SPAN, DH), lambda hh, n: (piece, rn(n), off + hh))
    blkp = lambda piece, off: pl.BlockSpec((1, SPAN, DH), lambda hh, n: (piece, pn(n), off + hh))
    tab = pl.BlockSpec((SPAN, DH), lambda hh, n: (rn(n), 0))
    tabp = pl.BlockSpec((SPAN, DH), lambda hh, n: (pn(n), 0))
    io = pl.BlockSpec((SPAN, DH), lambda hh, n: (rn(n), hh))
    return pl.pallas_call(
        body, name="attn_bwd", grid=(H, ns),
        in_specs=[blk(1, 0), blk(1, H), blkp(1, H), blk(2, 0), blkp(2, 0), blk(2, H),
                  tab, tab, tab, tabp, tabp, tabp, io, io, io],
        out_specs=(io, io, io, io),
        out_shape=tuple(jax.ShapeDtypeStruct((s, D), BF16) for _ in range(4)),
        scratch_shapes=[pltpu.VMEM((SPAN, DH), F32), pltpu.VMEM((2 * SPAN, DH), F32), pltpu.VMEM((2 * SPAN, DH), F32),
                        pltpu.VMEM((SPAN, DH), F32), pltpu.VMEM((SPAN, DH), F32), pltpu.VMEM((SPAN, DH), F32),
                        pltpu.VMEM((2 * SPAN, DH), F32), pltpu.VMEM((2 * SPAN, DH), F32)],
        compiler_params=_params(("parallel", "arbitrary")),
    )(p, p, p, p, p, p, rc, rsa, rsb, rc, rsa, rsb, o, lse, dga)


def _tail(gr, ga, p, x, tgt, w3, b_gate, gate, g_final, tm=256):
    s = x.shape[0]
    nt = s // tm

    def body(gr_ref, ga_ref, pr_ref, pa_ref, x_ref, t_ref, bg_ref, gate_ref, gf_ref, w_hbm,
             dgr_ref, dga_ref, dc_ref, dx2_ref, vec_ref, go_hbm, w_s, acc_s, sem):
        i = pl.program_id(0)

        @pl.when(i == 0)
        def _():
            cp = pltpu.make_async_copy(w_hbm, w_s, sem.at[12])
            cp.start()
            acc_s[...] = jnp.zeros_like(acc_s)
            vec_ref[...] = jnp.zeros_like(vec_ref)
            cp.wait()

        grb = gr_ref[...]
        gab = ga_ref[...]
        bg = bg_ref[...]
        gate_v = gate_ref[...]
        gf = gf_ref[...]
        y_r = _dot(grb, w_s[0])
        y_a = _dot(gab, w_s[1])
        sr = _sigmoid(pr_ref[0].astype(F32) + bg[:, :D])
        sa = _sigmoid(pa_ref[0].astype(F32) + bg[:, D:])
        mb = (sr * y_r + sa * y_a).astype(BF16)
        u = _dot(mb, w_s[2])
        x2 = x_ref[...] + gate_v * u
        rstd = lax.rsqrt(jnp.mean(x2 * x2, axis=-1, keepdims=True) + EPS)
        xh = x2 * rstd
        e = xh * gf - t_ref[...]
        dy = e * (1.0 / D)
        dyg = dy * gf
        dx2 = rstd * (dyg - xh * jnp.mean(dyg * xh, axis=-1, keepdims=True))
        dx2_ref[...] = dx2
        dub = (dx2 * gate_v).astype(BF16)
        dm = _dot_nt(dub, w_s[2])
        dyr = (dm * sr).astype(BF16)
        dya = (dm * sa).astype(BF16)
        dpr = dm * y_r * (sr * (1.0 - sr))
        dpa = dm * y_a * (sa * (1.0 - sa))
        dc_ref[:, :D] = dpr.astype(BF16)
        dc_ref[:, D:] = dpa.astype(BF16)
        dgr_ref[...] = _dot_nt(dyr, w_s[0])
        dga_ref[...] = _dot_nt(dya, w_s[1])
        acc_s[0] += _dot_tn(grb, dyr)
        acc_s[1] += _dot_tn(gab, dya)
        acc_s[2] += _dot_tn(mb, dub)
        vec_ref[0:1, :] += _colsum(dy * xh)
        vec_ref[1:2, :] += _colsum(dx2 * u)
        vec_ref[2:3, :] += _colsum(dpr)
        vec_ref[3:4, :] += _colsum(dpa)
        vec_ref[4:5, :] += _colsum(e * e)

        @pl.when(i == nt - 1)
        def _():
            vec_ref[4:5, :] = jnp.broadcast_to(jnp.sum(vec_ref[4:5, :]) * (0.5 / D), (1, D))
            cps = []
            for w in range(3):
                for j in range(4):
                    cps.append(pltpu.make_async_copy(acc_s.at[w, pl.ds(256 * j, 256)],
                                                     go_hbm.at[j, pl.ds(256 * w, 256)], sem.at[4 * w + j]))
            for cp in cps:
                cp.start()
            for cp in cps:
                cp.wait()

    rowt = lambda i: (i, 0)
    row = lambda w: pl.BlockSpec((1, w), lambda i: (0, 0))
    any_ = pl.BlockSpec(memory_space=pl.ANY)
    return pl.pallas_call(
        body, name="tail", grid=(nt,),
        in_specs=[pl.BlockSpec((tm, D), rowt), pl.BlockSpec((tm, D), rowt),
                  pl.BlockSpec((1, tm, D), lambda i: (3, i, 0)), pl.BlockSpec((1, tm, D), lambda i: (3, i, 1)),
                  pl.BlockSpec((tm, D), rowt), pl.BlockSpec((tm, D), rowt),
                  row(2 * D), row(D), row(D), any_],
        out_specs=(pl.BlockSpec((tm, D), rowt), pl.BlockSpec((tm, D), rowt), pl.BlockSpec((tm, 2 * D), rowt),
                   pl.BlockSpec((tm, D), rowt), pl.BlockSpec((8, D), lambda i: (0, 0)), any_),
        out_shape=(jax.ShapeDtypeStruct((s, D), F32), jax.ShapeDtypeStruct((s, D), F32),
                   jax.ShapeDtypeStruct((s, 2 * D), BF16), jax.ShapeDtypeStruct((s, D), F32),
                   jax.ShapeDtypeStruct((8, D), F32), jax.ShapeDtypeStruct((4, 768, D), F32)),
        scratch_shapes=[pltpu.VMEM((3, D, D), BF16), pltpu.VMEM((3, D, D), F32), pltpu.SemaphoreType.DMA((13,))],
        compiler_params=_params(("arbitrary",)),
    )(gr, ga, p, p, x, tgt, b_gate, gate, g_final, w3)


def _pieces_steps(pieces):
    out, s0 = [], 0
    for a in pieces:
        n = a.shape[1] // D
        out.append((s0, n))
        s0 += n
    return out, s0


def _inproj_bwd_x(pieces, wg, x, dx2, gn, scale, sums, tm=512):
    s = x.shape[0]
    np_ = len(pieces)
    na = len(sums)
    ni = s // tm
    groups, cur, width = [], [], 0
    for t, a in enumerate(pieces):
        cur.append(t)
        width += a.shape[1]
        if width == PW:
            groups.append(cur)
            cur, width = [], 0
    assert len(groups) == 4 and not cur

    def body(*refs):
        d_refs = refs[:np_]
        w_hbm, x_ref, dx2_ref, gn_ref, sc_ref = refs[np_:np_ + 5]
        q_refs = refs[np_ + 5:np_ + 5 + na]
        gx_ref, vec_ref = refs[np_ + 5 + na:np_ + 7 + na]
        r_refs = refs[np_ + 7 + na:np_ + 7 + 2 * na]
        w_s, wsem, ss, rs = refs[np_ + 7 + 2 * na:]
        i = pl.program_id(0)

        def scatter_copies():
            cx, cy, cc = _coords()
            j = 2 * cx + cy
            cps = []
            for t, (q, r) in enumerate(zip(q_refs, r_refs)):
                for e, (kx, ky) in enumerate(((1, 0), (0, 1), (1, 1))):
                    cps.append(_rcopy(q.at[j ^ (2 * kx + ky)], r.at[e], ss.at[3 * t + e], rs.at[3 * t + e],
                                      (_flip(cx, kx), _flip(cy, ky), cc)))
            return cps

        def w_copy(pc):
            return pltpu.make_async_copy(w_hbm.at[pc, pl.ds(0, D), :], w_s.at[pc], wsem.at[pc])

        @pl.when(i == 0)
        def _():
            for pc in range(4):
                w_copy(pc).start()
            vec_ref[...] = jnp.zeros_like(vec_ref)
            for cp in scatter_copies():
                cp.start()

        dh = None
        for pc, group in enumerate(groups):
            @pl.when(i == 0)
            def _(pc=pc):
                w_copy(pc).wait()

            tiles = [d_refs[t][...] for t in group]
            lhs = tiles[0] if len(tiles) == 1 else jnp.concatenate(tiles, axis=1)
            part = _dot_nt(lhs, w_s[pc])
            dh = part if dh is None else dh + part

        xt = x_ref[...]
        rstd = lax.rsqrt(jnp.mean(xt * xt, axis=-1, keepdims=True) + EPS)
        xh = xt * rstd
        gn_v = gn_ref[...]
        sc1 = 1.0 + sc_ref[...]
        dhx = dh * xh
        vec_ref[0:1, :] += _colsum(dh)
        vec_ref[1:2, :] += _colsum(dhx) * gn_v
        vec_ref[2:3, :] += _colsum(dhx) * sc1
        dxh = dh * (gn_v * sc1)
        gx_ref[...] = rstd * (dxh - xh * jnp.mean(dxh * xh, axis=-1, keepdims=True)) + dx2_ref[...]

        @pl.when(i == ni - 1)
        def _():
            for cp in scatter_copies():
                cp.wait()

    rowt = lambda i: (i, 0)
    row = pl.BlockSpec((1, D), lambda i: (0, 0))
    any_ = pl.BlockSpec(memory_space=pl.ANY)
    outs = pl.pallas_call(
        body, name="inproj_bwd_x", grid=(ni,),
        in_specs=[pl.BlockSpec((tm, a.shape[1]), rowt) for a in pieces] +
                 [any_, pl.BlockSpec((tm, D), rowt), pl.BlockSpec((tm, D), rowt), row, row] + [any_] * na,
        out_specs=(pl.BlockSpec((tm, D), rowt), pl.BlockSpec((8, D), lambda i: (0, 0))) + (any_,) * na,
        out_shape=(jax.ShapeDtypeStruct((s, D), F32), jax.ShapeDtypeStruct((8, D), F32)) +
                  tuple(jax.ShapeDtypeStruct((3,) + q.shape[1:], q.dtype) for q in sums),
        scratch_shapes=[pltpu.VMEM((4, D, PW), BF16), pltpu.SemaphoreType.DMA((4,)),
                        pltpu.SemaphoreType.DMA((3 * na,)), pltpu.SemaphoreType.DMA((3 * na,))],
        compiler_params=_params(("arbitrary",)),
    )(*pieces, wg, x, dx2, gn, scale, *sums)
    return outs[0], outs[1], outs[2:]


def _inproj_bwd_w(pieces, hbf, tk=1024):
    s = hbf.shape[0]
    steps, nk = _pieces_steps(pieces)
    npc = PW // D
    ns = s // tk
    np_ = len(pieces)

    def body(*refs):
        d_refs = refs[:np_]
        h_ref, g_ref = refs[np_:]
        cb, k = pl.program_id(0), pl.program_id(1)

        @pl.when(k == 0)
        def _():
            g_ref[...] = jnp.zeros_like(g_ref)

        for (s0, n), d_ref in zip(steps, d_refs):
            @pl.when((cb >= s0) & (cb < s0 + n))
            def _(d_ref=d_ref):
                g_ref[0] += _dot_tn(h_ref[...], d_ref[...])

    def piece_spec(s0, n):
        def imap(cb, k):
            active = (cb >= s0) & (cb < s0 + n)
            return (jnp.where(active, k, 0), jnp.clip(cb - s0, 0, n - 1))
        return pl.BlockSpec((tk, D), imap)

    return pl.pallas_call(
        body, name="inproj_bwd_w", grid=(nk, ns),
        in_specs=[piece_spec(s0, n) for s0, n in steps] + [pl.BlockSpec((tk, D), lambda cb, k: (k, 0))],
        out_specs=pl.BlockSpec((1, D, D), lambda cb, k: (cb // npc, 0, cb % npc)),
        out_shape=jax.ShapeDtypeStruct((4, D, PW), F32),
        compiler_params=_params(("parallel", "arbitrary")),
    )(*pieces, hbf)


D2D_CHUNK_BYTES = 512 * 1024


def _chunk_rows(a):
    return max(8, D2D_CHUNK_BYTES // (a.shape[-1] * a.dtype.itemsize))


def _pair_exchange(arrs):
    na = len(arrs)
    chunks = []
    for t, a in enumerate(arrs):
        hr = a.shape[1] // 2
        cr = _chunk_rows(a)
        chunks += [(t, j, r0, cr) for j in range(a.shape[0]) for r0 in range(0, hr, cr)]
    nch = len(chunks)

    def body(*refs):
        a_refs = refs[:na]
        rb_refs = refs[na:2 * na]
        ss, rs = refs[2 * na:]
        x, y, c = _coords()
        sib = (x, y, 1 - c)
        rcs = []
        for n, (t, j, r0, cr) in enumerate(chunks):
            hr = a_refs[t].shape[1] // 2
            rc = _rcopy(a_refs[t].at[j, pl.ds((1 - c) * hr + r0, cr), :], rb_refs[t].at[j, pl.ds(r0, cr), :],
                        ss.at[n], rs.at[n], sib)
            rc.start()
            rcs.append(rc)
        for rc in rcs:
            rc.wait_recv()
        for rc in rcs:
            rc.wait_send()

    any_ = pl.BlockSpec(memory_space=pl.ANY)
    halves = [jax.ShapeDtypeStruct((a.shape[0], a.shape[1] // 2, a.shape[2]), a.dtype) for a in arrs]
    return pl.pallas_call(
        body, name="pair_exchange",
        out_shape=tuple(halves),
        in_specs=[any_] * na, out_specs=tuple([any_] * na),
        scratch_shapes=[pltpu.SemaphoreType.DMA((nch,)), pltpu.SemaphoreType.DMA((nch,))],
        compiler_params=_params(),
    )(*arrs)


def _pair_swap(arrs):
    na = len(arrs)
    chunks = []
    for t, a in enumerate(arrs):
        cr = _chunk_rows(a)
        chunks += [(t, r0, cr) for r0 in range(0, a.shape[0], cr)]
    nch = len(chunks)

    def body(*refs):
        a_refs = refs[:na]
        o_refs = refs[na:2 * na]
        ss, rs = refs[2 * na:]
        x, y, c = _coords()
        sib = (x, y, 1 - c)
        rcs = []
        for n, (t, r0, cr) in enumerate(chunks):
            rows = pl.ds(r0, cr)
            rc = _rcopy(a_refs[t].at[rows, :], o_refs[t].at[rows, :], ss.at[n], rs.at[n], sib)
            rc.start()
            rcs.append(rc)
        for rc in rcs:
            rc.wait_recv()
        for rc in rcs:
            rc.wait_send()

    any_ = pl.BlockSpec(memory_space=pl.ANY)
    return pl.pallas_call(
        body, name="pair_swap",
        out_shape=tuple(jax.ShapeDtypeStruct(a.shape, a.dtype) for a in arrs),
        in_specs=[any_] * na, out_specs=tuple([any_] * na),
        scratch_shapes=[pltpu.SemaphoreType.DMA((nch,)), pltpu.SemaphoreType.DMA((nch,))],
        compiler_params=_params(),
    )(*arrs)


def _add_half(full, rb, core, tr):
    n, r, cdim = full.shape
    nb = r // 2 // tr

    def body(c_ref, a_ref, b_ref, o_ref, ob_ref):
        tot = a_ref[...] + b_ref[...]
        o_ref[...] = tot
        ob_ref[...] = tot.astype(BF16)

    mine = pl.BlockSpec((1, tr, cdim), lambda i, j, c_ref: (i, c_ref[0] * nb + j, 0))
    spec = pl.BlockSpec((1, tr, cdim), lambda i, j, c_ref: (i, j, 0))
    return pl.pallas_call(
        body, name="add_half",
        grid_spec=pltpu.PrefetchScalarGridSpec(num_scalar_prefetch=1, grid=(n, nb), in_specs=[mine, spec],
                                               out_specs=(spec, spec)),
        out_shape=(jax.ShapeDtypeStruct(rb.shape, rb.dtype), jax.ShapeDtypeStruct(rb.shape, BF16)),
        compiler_params=_params(("parallel", "parallel")),
    )(core, full, rb)


def _sum_slots(q, r3, shard, tr):
    _, r, cdim = q.shape

    def body(j_ref, q_ref, r_ref, o_ref):
        o_ref[...] = ((q_ref[0] + r_ref[0].astype(F32)) + r_ref[1].astype(F32)) + r_ref[2].astype(F32)

    return pl.pallas_call(
        body, name="sum_slots",
        grid_spec=pltpu.PrefetchScalarGridSpec(
            num_scalar_prefetch=1, grid=(r // tr,),
            in_specs=[pl.BlockSpec((1, tr, cdim), lambda i, j_ref: (j_ref[0], i, 0)),
                      pl.BlockSpec((3, tr, cdim), lambda i, j_ref: (0, i, 0))],
            out_specs=pl.BlockSpec((tr, cdim), lambda i, j_ref: (i, 0))),
        out_shape=jax.ShapeDtypeStruct((r, cdim), q.dtype),
        compiler_params=_params(("parallel",)),
    )(shard, q, r3)


def _allreduce_small(packs):
    na = len(packs)

    def body(*refs):
        p_refs, o_refs, rbufs = refs[:na], refs[na:2 * na], refs[2 * na:3 * na]
        s1, r1, s2, r2 = refs[3 * na:]
        me = _my_index()
        sends = []

        def chunk(t, d):
            ch = p_refs[t].shape[0] // NDEV
            return pl.ds(pl.multiple_of(d * ch, 8), ch)

        for t in range(na):
            for k in range(1, NDEV):
                e = 7 * t + k - 1
                cp = _rcopy(p_refs[t].at[chunk(t, me ^ k)], rbufs[t].at[me], s1.at[e], r1.at[e], _peer(k))
                cp.start()
                sends.append(cp)
            rbufs[t][me] = p_refs[t][chunk(t, me), :]
        for t in range(na):
            for k in range(1, NDEV):
                e = 7 * t + k - 1
                _rcopy(p_refs[t].at[chunk(t, me)], rbufs[t].at[me ^ k], s1.at[e], r1.at[e], _peer(k)).wait_recv()
            tot = rbufs[t][0]
            for d in range(1, NDEV):
                tot = tot + rbufs[t][d]
            o_refs[t][chunk(t, me), :] = tot
            for k in range(1, NDEV):
                e = 7 * t + k - 1
                cp = _rcopy(o_refs[t].at[chunk(t, me)], o_refs[t].at[chunk(t, me)], s2.at[e], r2.at[e], _peer(k))
                cp.start()
                sends.append(cp)
        for t in range(na):
            for k in range(1, NDEV):
                e = 7 * t + k - 1
                _rcopy(o_refs[t].at[chunk(t, me)], o_refs[t].at[chunk(t, me ^ k)], s2.at[e], r2.at[e],
                       _peer(k)).wait_recv()
        for cp in sends:
            cp.wait_send()

    vm = pl.BlockSpec(memory_space=pltpu.VMEM)
    return pl.pallas_call(
        body, name="allreduce_small",
        out_shape=tuple(jax.ShapeDtypeStruct(a.shape, F32) for a in packs),
        in_specs=[vm] * na, out_specs=tuple([vm] * na),
        scratch_shapes=[pltpu.VMEM((NDEV, a.shape[0] // NDEV, a.shape[1]), F32) for a in packs] +
                       [pltpu.SemaphoreType.DMA((7 * na,)) for _ in range(4)],
        compiler_params=_params(),
    )(*packs)


def _adamw_update(w, g, m, v):
    nm = B1 * m + (1.0 - B1) * g
    nv = B2 * v + (1.0 - B2) * (g * g)
    m_hat = nm / (1.0 - B1 ** STEP)
    v_hat = nv / (1.0 - B2 ** STEP)
    return -LR * (m_hat / (jnp.sqrt(v_hat) + ADAM_EPS) + WD * w), nm, nv


def _adamw(w, g, m, v, tr):
    r, cdim = w.shape

    def body(w_ref, g_ref, m_ref, v_ref, d_ref, nm_ref, nv_ref):
        d_ref[...], nm_ref[...], nv_ref[...] = _adamw_update(w_ref[...], g_ref[...], m_ref[...], v_ref[...])

    spec = pl.BlockSpec((tr, cdim), lambda i: (i, 0))
    sd = jax.ShapeDtypeStruct((r, cdim), F32)
    return pl.pallas_call(
        body, name="adamw", grid=(r // tr,), in_specs=[spec] * 4, out_specs=(spec,) * 3, out_shape=(sd,) * 3,
        compiler_params=_params(("parallel",)),
    )(w, g, m, v)


V_G_NORM, V_DMOD, V_B_GATE, V_CONV_B, V_LAM, V_G_FINAL, V_CONV_W, V_LOSS, V_ROWS = 0, 1, 4, 6, 7, 8, 9, 13, 64
M_W_A, M_W_X, M_B_A, M_B_X, M_ROWS = 0, H * DH, 2 * H * DH, 2 * H * DH + H, 2112
SMALL = ("g_norm", "b_mod", "b_gate", "conv_b", "lam", "g_final", "conv_w", "w_a", "w_x", "b_a", "b_x")


def _adamw_small(redv, redm, g_conv, wmv):
    def grad(name, rv, rm, gc):
        if name == "g_norm":
            return rv[V_G_NORM:V_G_NORM + 1, :]
        if name == "b_mod":
            return jnp.concatenate([rv[V_DMOD + t:V_DMOD + t + 1, :] for t in range(3)], axis=1)
        if name == "b_gate":
            return jnp.concatenate([rv[V_B_GATE + t:V_B_GATE + t + 1, :] for t in range(2)], axis=1)
        if name == "conv_b":
            return rv[V_CONV_B:V_CONV_B + 1, :]
        if name == "lam":
            return rv[V_LAM:V_LAM + 1, :]
        if name == "g_final":
            return rv[V_G_FINAL:V_G_FINAL + 1, :]
        if name == "conv_w":
            return gc[...]
        if name == "w_a":
            return rm[M_W_A:M_W_A + H * DH, :]
        if name == "w_x":
            return rm[M_W_X:M_W_X + H * DH, :]
        if name == "b_a":
            return rm[M_B_A:M_B_A + H, :]
        return rm[M_B_X:M_B_X + H, :]

    n = len(SMALL)

    def body(*refs):
        rv, rm, gc = refs[:3]
        ins, outs = refs[3:3 + 3 * n], refs[3 + 3 * n:]
        for t, name in enumerate(SMALL):
            w_ref, m_ref, v_ref = ins[3 * t:3 * t + 3]
            g_out, d_out, m_out, v_out = outs[4 * t:4 * t + 4]
            g = grad(name, rv, rm, gc)
            g_out[...] = g
            d_out[...], m_out[...], v_out[...] = _adamw_update(w_ref[...], g, m_ref[...], v_ref[...])

    vm = pl.BlockSpec(memory_space=pltpu.VMEM)
    flat = [a for name in SMALL for a in wmv[name]]
    shapes = [jax.ShapeDtypeStruct(wmv[name][0].shape, F32) for name in SMALL for _ in range(4)]
    outs = pl.pallas_call(
        body, name="adamw_small", out_shape=tuple(shapes),
        in_specs=[vm] * (3 + len(flat)), out_specs=tuple([vm] * len(shapes)),
        compiler_params=_params(),
    )(redv, redm, g_conv, *flat)
    return {name: outs[4 * t:4 * t + 4] for t, name in enumerate(SMALL)}


def _rope_tables(positions):
    inv_freq = ROPE_THETA ** (-jnp.arange(0, ROT, 2, dtype=F32) / ROT)
    ang = positions.astype(F32)[:, None] * inv_freq
    cos, sin = jnp.cos(ang), jnp.sin(ang)
    n = positions.shape[0]
    half = ROT // 2
    rc = jnp.concatenate([cos, cos, jnp.ones((n, DH - ROT), F32)], axis=1)
    rsa = jnp.concatenate([-sin, jnp.zeros((n, DH - half), F32)], axis=1)
    rsb = jnp.concatenate([jnp.zeros((n, half), F32), sin, jnp.zeros((n, DH - ROT), F32)], axis=1)
    return rc, rsa, rsb


def kernel(x, c, positions, g_norm, w_mod, b_mod, w_in, b_gate, conv_w, conv_b, w_a, b_a, w_x, b_x, lam, w_out_rnn, w_out_attn, w_o, g_final, loss_target, m_g_norm, m_w_mod, m_b_mod, m_w_in, m_b_gate, m_conv_w, m_conv_b, m_w_a, m_b_a, m_w_x, m_b_x, m_lam, m_w_out_rnn, m_w_out_attn, m_w_o, m_g_final, v_g_norm, v_w_mod, v_b_mod, v_w_in, v_b_gate, v_conv_w, v_conv_b, v_w_a, v_b_a, v_w_x, v_b_x, v_lam, v_w_out_rnn, v_w_out_attn, v_w_o, v_g_final):
    s = x.shape[1]
    xi = lax.axis_index("x")
    yi = lax.axis_index("y")
    ci = lax.axis_index("c")
    shard = 2 * xi + yi
    x2d = x[0]
    tgt = loss_target[0]
    pos = positions[0]

    c_all, mod4, conv_all = _mod_fwd(c, w_mod[0], b_mod.reshape(4, 1, 768), conv_w[0])
    mod = mod4.reshape(1, 3 * D)
    shift, scale, gate = mod[:, :D], mod[:, D:2 * D], mod[:, 2 * D:]
    w3sh = jnp.concatenate([w_out_rnn[0], w_out_attn[0], w_o[0]], axis=0).astype(BF16)
    wsh = w_in[0].astype(BF16)
    conv_full = conv_all[0::2].transpose(1, 0, 2).reshape(4, D)

    order = jnp.stack([shard, shard ^ 2, shard ^ 1, shard ^ 3]).astype(jnp.int32)
    p, hbf, wg = _gather_norm_inproj(x2d, g_norm, shift, scale, wsh, order)
    wg = lax.dynamic_update_slice(wg, wsh[None], (shard, 0, 0))
    rc, rsa, rsb = _rope_tables(pos)
    pos_col = pos.reshape(s, 1)
    b_a3, b_x3 = b_a.reshape(H, 1, DH), b_x.reshape(H, 1, DH)
    hr, gr = _rnn_fwd(p, pos_col, conv_full, conv_b, w_a[0], b_a3, w_x[0], b_x3, lam)
    o, lse, ga, w3g = _attn_fwd(p, rc, rsa, rsb, w3sh)
    w3g = lax.dynamic_update_slice(w3g, w3sh[None], (shard, 0, 0))
    w3 = w3g.reshape(4, 3, 256, D).transpose(1, 0, 2, 3).reshape(3, D, D)

    dgr, dga, dc, dx2, vec_t, g_out = _tail(gr, ga, p, x2d, tgt, w3, b_gate, gate, g_final.reshape(1, D))

    dxr, dzr, g_wa, g_ba, g_wx, g_bx, g_lam, g_cw, g_cb = _rnn_bwd(
        p, hr, dgr, pos_col, conv_full, conv_b, w_a[0], b_a3, w_x[0], b_x3, lam)
    dq, dk, dv, dza = _attn_bwd(p, o, lse, dga, rc, rsa, rsb)

    pieces = [dxr, dzr, dq, dk, dv, dza, dc]
    g_win = _inproj_bwd_w(pieces, hbf)

    core = ci.reshape(1)
    shard1 = shard.reshape(1)
    rb_a, rb_b = _pair_exchange([g_win, g_out])
    (q_a, qh_a), (q_b, qh_b) = _add_half(g_win, rb_a, core, tr=256), _add_half(g_out, rb_b, core, tr=128)
    grad_x, vec_n, (r_a, r_b) = _inproj_bwd_x(pieces, wg, x2d, dx2, g_norm, scale, [qh_a, qh_b])
    f_a, f_b = _sum_slots(q_a, r_a, shard1, tr=256), _sum_slots(q_b, r_b, shard1, tr=128)
    s_a, s_b = _pair_swap([f_a, f_b])
    south = ci == 0
    grad_w_in = jnp.where(south, jnp.concatenate([f_a, s_a], axis=0), jnp.concatenate([s_a, f_a], axis=0))
    g3 = jnp.where(south, jnp.concatenate([f_b, s_b], axis=0), jnp.concatenate([s_b, f_b], axis=0)).reshape(3, 256, D)

    dmod_row = jnp.concatenate([vec_n[0:1], vec_n[1:2], vec_t[1:2]], axis=1)
    vpack = jnp.concatenate([
        vec_n[2:3],
        vec_n[0:2], vec_t[1:2],
        vec_t[2:4],
        g_cb.reshape(1, D),
        g_lam.reshape(1, D),
        vec_t[0:1],
        g_cw.transpose(1, 0, 2).reshape(4, D),
        vec_t[4:5],
        jnp.zeros((V_ROWS - 14, D), F32)], axis=0)
    mpack = jnp.concatenate([
        g_wa.reshape(H * DH, DH), g_wx.reshape(H * DH, DH), g_ba.reshape(H, DH), g_bx.reshape(H, DH),
        jnp.zeros((M_ROWS - 2 * H * DH - 2 * H, DH), F32)], axis=0)
    redv, redm = _allreduce_small([vpack, mpack])
    loss = redv[V_LOSS, 0]
    grad_w_mod = _mod_bwd(dmod_row.reshape(4, 1, 768), c_all)
    g_conv_sh = lax.dynamic_slice_in_dim(redv[V_CONV_W:V_CONV_W + 4], shard * 256, 256, axis=1)

    shape2d = dict(g_norm=(1, D), b_mod=(1, 3 * D), b_gate=(1, 2 * D), conv_b=(1, D), lam=(1, D), g_final=(1, D),
                   conv_w=(4, 256), w_a=(H * DH, DH), w_x=(H * DH, DH), b_a=(H, DH), b_x=(H, DH))
    given = dict(
        g_norm=(g_norm, m_g_norm, v_g_norm), b_mod=(b_mod, m_b_mod, v_b_mod), b_gate=(b_gate, m_b_gate, v_b_gate),
        conv_b=(conv_b, m_conv_b, v_conv_b), lam=(lam, m_lam, v_lam), g_final=(g_final, m_g_final, v_g_final),
        conv_w=(conv_w, m_conv_w, v_conv_w), w_a=(w_a, m_w_a, v_w_a), w_x=(w_x, m_w_x, v_w_x),
        b_a=(b_a, m_b_a, v_b_a), b_x=(b_x, m_b_x, v_b_x))
    small = _adamw_small(redv, redm, g_conv_sh,
                         {n: tuple(a.reshape(shape2d[n]) for a in given[n]) for n in SMALL})

    big_in = _adamw(w_in[0], grad_w_in, m_w_in[0], v_w_in[0], tr=256)
    big_mod = _adamw(w_mod[0], grad_w_mod, m_w_mod[0], v_w_mod[0], tr=256)
    w3f = jnp.concatenate([w_out_rnn[0], w_out_attn[0], w_o[0]], axis=0)
    m3f = jnp.concatenate([m_w_out_rnn[0], m_w_out_attn[0], m_w_o[0]], axis=0)
    v3f = jnp.concatenate([v_w_out_rnn[0], v_w_out_attn[0], v_w_o[0]], axis=0)
    big_out = _adamw(w3f, g3.reshape(768, D), m3f, v3f, tr=256)

    names = ["g_norm", "w_mod", "b_mod", "w_in", "b_gate", "conv_w", "conv_b", "w_a", "b_a", "w_x", "b_x", "lam",
             "w_out_rnn", "w_out_attn", "w_o", "g_final"]
    outs = []
    for idx in range(4):
        d = {n: small[n][idx].reshape(given[n][0].shape) for n in SMALL}
        if idx == 0:
            d.update(w_mod=grad_w_mod[None], w_in=grad_w_in[None],
                     w_out_rnn=g3[0][None], w_out_attn=g3[1][None], w_o=g3[2][None])
        else:
            d.update(w_mod=big_mod[idx - 1][None], w_in=big_in[idx - 1][None],
                     w_out_rnn=big_out[idx - 1][0:256][None], w_out_attn=big_out[idx - 1][256:512][None],
                     w_o=big_out[idx - 1][512:768][None])
        outs.append(d)
    flat = [d[n] for d in outs for n in names]
    return (loss, grad_x[None], *flat)
```

```python
import jax
import jax.numpy as jnp
from jax import lax
from jax.experimental import pallas as pl
from jax.experimental.pallas import tpu as pltpu

F32, BF16 = jnp.float32, jnp.bfloat16
MESH = pl.DeviceIdType.MESH
HIGHEST = lax.Precision.HIGHEST

D = 1024
H = 8
DH = 128
PW = 2048
EPS = 1e-6
LRU_C = 8.0
SCALE = DH ** -0.5
NEG = -1e30
SPAN = 2048
UB = 128
DILATIONS = (1, 4, 16)
UNIT_BATCH = 8
ROPE_THETA = 500000.0
ROT = 32

LR, B1, B2, ADAM_EPS, WD, STEP = 0.001, 0.9, 0.999, 1e-08, 0.01, 10

NDEV = 8


def _params(sem=None, vmem_mb=56):
    return pltpu.CompilerParams(dimension_semantics=sem, vmem_limit_bytes=vmem_mb * 2 ** 20)


def _coords():
    return lax.axis_index("x"), lax.axis_index("y"), lax.axis_index("c")


def _flip(v, bit):
    return 1 - v if bit else v


def _peer(k):
    x, y, c = _coords()
    return (_flip(x, (k >> 2) & 1), _flip(y, (k >> 1) & 1), _flip(c, k & 1))


def _my_index():
    x, y, c = _coords()
    return 4 * x + 2 * y + c


def _rcopy(src, dst, ssem, rsem, dev):
    return pltpu.make_async_remote_copy(src_ref=src, dst_ref=dst, send_sem=ssem, recv_sem=rsem,
                                        device_id=dev, device_id_type=MESH)


def _sigmoid(x):
    return jax.nn.sigmoid(x)


def _dot(a, b):
    return jnp.dot(a, b, preferred_element_type=F32)


def _dot_nt(a, b):
    return lax.dot_general(a, b, (((1,), (1,)), ((), ())), preferred_element_type=F32)


def _dot_tn(a, b):
    return lax.dot_general(a, b, (((0,), (0,)), ((), ())), preferred_element_type=F32)


def _colsum(a):
    return jnp.sum(a, axis=0, keepdims=True)


def _mod_fwd(c, w_mod_sh, b_mod4, conv_sh):
    def body(c_ref, w_ref, b_ref, cv_ref, call_ref, mod_ref, cvall_ref, rows_ref, cmat_ref, s1, r1, s2, r2, s3, r3):
        x, y, _ = _coords()
        me = _my_index()
        j = 2 * x + y
        call_ref[me] = c_ref[...]
        cvall_ref[me] = cv_ref[...]
        sends = []
        for k in range(1, NDEV):
            cp = _rcopy(call_ref.at[me], call_ref.at[me], s1.at[k - 1], r1.at[k - 1], _peer(k))
            cp.start()
            sends.append(cp)
            cp = _rcopy(cvall_ref.at[me], cvall_ref.at[me], s3.at[k - 1], r3.at[k - 1], _peer(k))
            cp.start()
            sends.append(cp)
        for k in range(1, NDEV):
            pk = me ^ k
            _rcopy(call_ref.at[pk], call_ref.at[pk], s1.at[k - 1], r1.at[k - 1], _peer(k)).wait_recv()
        for b in range(NDEV):
            cmat_ref[pl.ds(b, 1), :] = call_ref[b]
        cm = cmat_ref[...]
        act = cm * _sigmoid(cm)
        mp = jnp.dot(act, w_ref[...], preferred_element_type=F32, precision=HIGHEST) + b_ref[j]
        for b in range(NDEV):
            rows_ref[b] = mp[b:b + 1]
        mod_ref[j] = rows_ref[me]
        for q, k in enumerate((2, 4, 6)):
            cp = _rcopy(rows_ref.at[me ^ k], mod_ref.at[j], s2.at[q], r2.at[q], _peer(k))
            cp.start()
            sends.append(cp)
        for q, k in enumerate((2, 4, 6)):
            jq = j ^ (k >> 1)
            _rcopy(rows_ref.at[me], mod_ref.at[jq], s2.at[q], r2.at[q], _peer(k)).wait_recv()
        for k in range(1, NDEV):
            pk = me ^ k
            _rcopy(cvall_ref.at[pk], cvall_ref.at[pk], s3.at[k - 1], r3.at[k - 1], _peer(k)).wait_recv()
        for cp in sends:
            cp.wait_send()

    vm = pl.BlockSpec(memory_space=pltpu.VMEM)
    return pl.pallas_call(
        body, name="mod_fwd",
        out_shape=(jax.ShapeDtypeStruct((NDEV, 1, D), F32), jax.ShapeDtypeStruct((4, 1, 768), F32),
                   jax.ShapeDtypeStruct((NDEV,) + conv_sh.shape, F32)),
        in_specs=[vm, vm, vm, vm], out_specs=(vm, vm, vm),
        scratch_shapes=[pltpu.VMEM((NDEV, 1, 768), F32), pltpu.VMEM((NDEV, D), F32),
                        pltpu.SemaphoreType.DMA((7,)), pltpu.SemaphoreType.DMA((7,)),
                        pltpu.SemaphoreType.DMA((3,)), pltpu.SemaphoreType.DMA((3,)),
                        pltpu.SemaphoreType.DMA((7,)), pltpu.SemaphoreType.DMA((7,))],
        compiler_params=_params(),
    )(c, w_mod_sh, b_mod4, conv_sh)


def _mod_bwd(dmod4, c_all):
    def body(d_ref, call_ref, gw_ref, dall_ref, cmat_ref, dmat_ref, s1, r1):
        x, y, _ = _coords()
        me = _my_index()
        j = 2 * x + y
        dall_ref[me] = d_ref[...]
        sends = []
        for k in range(1, NDEV):
            cp = _rcopy(dall_ref.at[me], dall_ref.at[me], s1.at[k - 1], r1.at[k - 1], _peer(k))
            cp.start()
            sends.append(cp)
        for k in range(1, NDEV):
            pk = me ^ k
            _rcopy(dall_ref.at[pk], dall_ref.at[pk], s1.at[k - 1], r1.at[k - 1], _peer(k)).wait_recv()
        for cp in sends:
            cp.wait_send()
        for b in range(NDEV):
            cmat_ref[pl.ds(b, 1), :] = call_ref[b]
            dmat_ref[pl.ds(b, 1), :] = dall_ref[b, j]
        cm = cmat_ref[...]
        act = cm * _sigmoid(cm)
        gw_ref[...] = lax.dot_general(act, dmat_ref[...], (((0,), (0,)), ((), ())),
                                      preferred_element_type=F32, precision=HIGHEST)

    vm = pl.BlockSpec(memory_space=pltpu.VMEM)
    return pl.pallas_call(
        body, name="mod_bwd",
        out_shape=jax.ShapeDtypeStruct((D, 768), F32),
        in_specs=[vm, vm], out_specs=vm,
        scratch_shapes=[pltpu.VMEM((NDEV, 4, 1, 768), F32), pltpu.VMEM((NDEV, D), F32), pltpu.VMEM((NDEV, 768), F32),
                        pltpu.SemaphoreType.DMA((7,)), pltpu.SemaphoreType.DMA((7,))],
        compiler_params=_params(),
    )(dmod4, c_all)


def _gather_norm_inproj(x, gn, shift, scale, wsh, order, tm=1024, tn=1024):
    s = x.shape[0]
    ni = s // tm
    npc = PW // tn
    rows, cols = wsh.shape
    half = rows // 2
    nch = 4
    cr = half // nch
    chips = ((1, 0), (0, 1), (1, 1))

    def body(ord_ref, x_ref, gn_ref, sh_ref, sc_ref, wsh_hbm, p_ref, h_ref, wg_hbm, hs_all, w_s, wsem, ss, rs):
        slot, i, col = pl.program_id(0), pl.program_id(1), pl.program_id(2)
        cx, cy, cc = _coords()
        j = 2 * cx + cy
        sib = (cx, cy, 1 - cc)
        mine = lambda n: pl.ds(cc * half + n * cr, cr)
        theirs = lambda n: pl.ds((1 - cc) * half + n * cr, cr)
        shard_of = lambda q: j ^ (2 * chips[q][0] + chips[q][1])

        def to_chip(q, n):
            e = nch * q + n
            return _rcopy(wsh_hbm.at[mine(n)], wg_hbm.at[j, mine(n)], ss.at[e], rs.at[e],
                          (_flip(cx, chips[q][0]), _flip(cy, chips[q][1]), cc))

        def from_chip(q, n):
            e = nch * q + n
            return _rcopy(wsh_hbm.at[mine(n)], wg_hbm.at[shard_of(q), mine(n)], ss.at[e], rs.at[e], sib)

        def to_sibling(q, n):
            e = 3 * nch + nch * q + n
            return _rcopy(wg_hbm.at[shard_of(q), mine(n)], wg_hbm.at[shard_of(q), mine(n)], ss.at[e], rs.at[e], sib)

        def from_sibling(q, n):
            e = 3 * nch + nch * q + n
            return _rcopy(wsh_hbm.at[mine(n)], wg_hbm.at[shard_of(q), theirs(n)], ss.at[e], rs.at[e], sib)

        def load(sl, src):
            cp = pltpu.make_async_copy(src, w_s.at[sl], wsem.at[sl])
            cp.start()
            cp.wait()

        first = (i == 0) & (col == 0)

        @pl.when(first & (slot == 0))
        def _():
            for n in range(nch):
                for q in (0, 1):
                    to_chip(q, n).start()
            load(0, wsh_hbm.at[pl.ds(0, D), :])

        @pl.when(first & (slot == 1))
        def _():
            for q in (0, 1):
                for n in range(nch):
                    from_chip(q, n).wait_recv()
                    to_sibling(q, n).start()
            for n in range(nch):
                to_chip(2, n).start()
            for n in range(nch):
                from_sibling(0, n).wait_recv()
            load(1, wg_hbm.at[shard_of(0), pl.ds(0, D), :])

        @pl.when(first & (slot == 2))
        def _():
            for n in range(nch):
                from_sibling(1, n).wait_recv()
            load(2, wg_hbm.at[shard_of(1), pl.ds(0, D), :])

        @pl.when(first & (slot == 3))
        def _():
            for n in range(nch):
                from_chip(2, n).wait_recv()
                to_sibling(2, n).start()
            for n in range(nch):
                from_sibling(2, n).wait_recv()
            load(3, wg_hbm.at[shard_of(2), pl.ds(0, D), :])
            for q in range(3):
                for n in range(nch):
                    to_chip(q, n).wait_send()
                    to_sibling(q, n).wait_send()

        @pl.when((slot == 0) & (col == 0))
        def _():
            xt = x_ref[...]
            rstd = lax.rsqrt(jnp.mean(xt * xt, axis=-1, keepdims=True) + EPS)
            h = ((xt * rstd * gn_ref[...]) * (1.0 + sc_ref[...]) + sh_ref[...]).astype(BF16)
            hs_all[i] = h
            h_ref[...] = h

        p_ref[0] = _dot(hs_all[i], w_s[slot, :, pl.ds(pl.multiple_of(col * tn, tn), tn)]).astype(BF16)

    row = pl.BlockSpec((1, D), lambda sl, i, col, o: (0, 0))
    x_rows = lambda sl, i, col, o: (jnp.where(sl == 0, i, ni - 1), 0)
    any_ = pl.BlockSpec(memory_space=pl.ANY)
    return pl.pallas_call(
        body, name="gather_norm_inproj",
        grid_spec=pltpu.PrefetchScalarGridSpec(
            num_scalar_prefetch=1, grid=(4, ni, npc),
            in_specs=[pl.BlockSpec((tm, D), x_rows), row, row, row, any_],
            out_specs=(pl.BlockSpec((1, tm, tn), lambda sl, i, col, o: (o[sl], i, col)),
                       pl.BlockSpec((tm, D), x_rows), any_),
            scratch_shapes=[pltpu.VMEM((ni, tm, D), BF16), pltpu.VMEM((4, D, PW), BF16),
                            pltpu.SemaphoreType.DMA((4,)),
                            pltpu.SemaphoreType.DMA((6 * nch,)), pltpu.SemaphoreType.DMA((6 * nch,))]),
        out_shape=(jax.ShapeDtypeStruct((4, s, PW), BF16), jax.ShapeDtypeStruct((s, D), BF16),
                   jax.ShapeDtypeStruct((4, rows, cols), wsh.dtype)),
        compiler_params=_params(("arbitrary", "arbitrary", "arbitrary")),
    )(order, x, gn, shift, scale, wsh)


def _shift_down(prev8, cur, d):
    t = cur.shape[0]
    c3 = cur.reshape(t // 8, 8, DH)
    rot = pltpu.roll(c3, d, 1)
    before = jnp.concatenate([pltpu.roll(prev8, d, 0).reshape(1, 8, DH), rot[:-1]], axis=0)
    rows = lax.broadcasted_iota(jnp.int32, c3.shape, 1)
    return jnp.where(rows >= d, rot, before).reshape(t, DH)


def _shift_up(cur, next8, d):
    t = cur.shape[0]
    c3 = cur.reshape(t // 8, 8, DH)
    rot = pltpu.roll(c3, 8 - d, 1)
    after = jnp.concatenate([rot[1:], pltpu.roll(next8, 8 - d, 0).reshape(1, 8, DH)], axis=0)
    rows = lax.broadcasted_iota(jnp.int32, c3.shape, 1)
    return jnp.where(rows < 8 - d, rot, after).reshape(t, DH)


def _rnn_gates(xr, prev8, cw, cb, wa, ba, wx, bx, lam, reset):
    xc = cw[3:4] * xr + cb
    for d in (1, 2, 3):
        xc = xc + cw[3 - d:4 - d] * _shift_down(prev8, xr, d)
    xcb = xc.astype(BF16)
    r = _sigmoid(_dot(xcb, wa.astype(BF16)) + ba)
    ig = _sigmoid(_dot(xcb, wx.astype(BF16)) + bx)
    nl = -lam
    sp = jnp.maximum(nl, 0.0) + jnp.log1p(jnp.exp(-jnp.abs(nl)))
    log_a = (-LRU_C * r) * sp
    a = jnp.where(reset, 0.0, jnp.exp(log_a))
    mult = jnp.where(reset, 1.0, jnp.sqrt(1.0 - jnp.exp(2.0 * log_a)))
    return xc, r, ig, sp, a, mult


def _log_scan(a, b, axis, up):
    n = a.shape[axis]
    rows = lax.broadcasted_iota(jnp.int32, a.shape, axis)
    d = 1
    while d < n:
        m = rows < n - d if up else rows >= d
        shift = n - d if up else d
        a_s = pltpu.roll(a, shift, axis)
        b_s = pltpu.roll(b, shift, axis)
        b = jnp.where(m, a * b_s + b, b)
        a = jnp.where(m, a * a_s, a)
        d *= 2
    return a, b


def _scan(a, b, t, edge, up=False):
    g = t // 8
    a3, b3 = _log_scan(a.reshape(g, 8, DH), b.reshape(g, 8, DH), 1, up)
    last = 0 if up else 7
    ag, bg = _log_scan(a3[:, last, :], b3[:, last, :], 0, up)
    hg = ag * edge + bg
    grp = lax.broadcasted_iota(jnp.int32, hg.shape, 0)
    if up:
        cin = jnp.where(grp == g - 1, edge, pltpu.roll(hg, g - 1, 0))
        tail = hg[0:1]
    else:
        cin = jnp.where(grp == 0, edge, pltpu.roll(hg, 1, 0))
        tail = hg[g - 1:g]
    return (a3 * cin[:, None, :] + b3).reshape(t, DH), tail


def _rnn_fwd(p, pos, conv_w, conv_b, w_a, b_a, w_x, b_x, lam, tt=512):
    s = p.shape[1]
    nt = s // tt

    def body(xr_ref, z_ref, pos_ref, cw_ref, cb_ref, wa_ref, ba_ref, wx_ref, bx_ref, lam_ref,
             hr_ref, gr_ref, xprev, hprev):
        @pl.when(pl.program_id(1) == 0)
        def _():
            xprev[...] = jnp.zeros_like(xprev)
            hprev[...] = jnp.zeros_like(hprev)

        xr = xr_ref[0].astype(F32)
        z = z_ref[0].astype(F32)
        reset = pos_ref[...] == 0
        xc, r, ig, sp, a, mult = _rnn_gates(xr, xprev[...], cw_ref[...], cb_ref[...], wa_ref[0], ba_ref[0],
                                            wx_ref[0], bx_ref[0], lam_ref[...], reset)
        bx = mult * ig * xc
        h, h_last = _scan(a, bx, tt, hprev[0:1])
        xprev[...] = xr[tt - 8:]
        hprev[...] = jnp.broadcast_to(h_last, (8, DH))
        hr_ref[...] = h
        gr_ref[...] = (h * (z * _sigmoid(z))).astype(BF16)

    head_row = lambda hh, t: (0, hh)
    return pl.pallas_call(
        body, name="rnn_fwd", grid=(H, nt),
        in_specs=[pl.BlockSpec((1, tt, DH), lambda hh, t: (0, t, hh)),
                  pl.BlockSpec((1, tt, DH), lambda hh, t: (0, t, H + hh)),
                  pl.BlockSpec((tt, 1), lambda hh, t: (t, 0)),
                  pl.BlockSpec((4, DH), head_row), pl.BlockSpec((1, DH), head_row),
                  pl.BlockSpec((1, DH, DH), lambda hh, t: (hh, 0, 0)), pl.BlockSpec((1, 1, DH), lambda hh, t: (hh, 0, 0)),
                  pl.BlockSpec((1, DH, DH), lambda hh, t: (hh, 0, 0)), pl.BlockSpec((1, 1, DH), lambda hh, t: (hh, 0, 0)),
                  pl.BlockSpec((1, DH), head_row)],
        out_specs=(pl.BlockSpec((tt, DH), lambda hh, t: (t, hh)), pl.BlockSpec((tt, DH), lambda hh, t: (t, hh))),
        out_shape=(jax.ShapeDtypeStruct((s, D), F32), jax.ShapeDtypeStruct((s, D), BF16)),
        scratch_shapes=[pltpu.VMEM((8, DH), F32), pltpu.VMEM((8, DH), F32)],
        compiler_params=_params(("parallel", "arbitrary")),
    )(p, p, pos, conv_w, conv_b, w_a, b_a, w_x, b_x, lam)


def _rnn_bwd(p, hr, dgr, pos, conv_w, conv_b, w_a, b_a, w_x, b_x, lam, tt=512):
    s = p.shape[1]
    nt = s // tt
    t8 = tt // 8

    def body(xr_ref, z_ref, xp_ref, hr_ref, hp_ref, dg_ref, pos_ref, cw_ref, cb_ref, wa_ref, ba_ref, wx_ref, bx_ref,
             lam_ref, dxr_ref, dz_ref, gwa_ref, gba_ref, gwx_ref, gbx_ref, glam_ref, gcw_ref, gcb_ref,
             a_next, g_next, dxc_next):
        t = pl.program_id(1)
        has_prev = t < nt - 1

        @pl.when(t == 0)
        def _():
            a_next[...] = jnp.zeros_like(a_next)
            g_next[...] = jnp.zeros_like(g_next)
            dxc_next[...] = jnp.zeros_like(dxc_next)
            gwa_ref[...] = jnp.zeros_like(gwa_ref)
            gba_ref[...] = jnp.zeros_like(gba_ref)
            gwx_ref[...] = jnp.zeros_like(gwx_ref)
            gbx_ref[...] = jnp.zeros_like(gbx_ref)
            glam_ref[...] = jnp.zeros_like(glam_ref)
            gcw_ref[...] = jnp.zeros_like(gcw_ref)
            gcb_ref[...] = jnp.zeros_like(gcb_ref)

        xr = xr_ref[0].astype(F32)
        z = z_ref[0].astype(F32)
        hr_blk = hr_ref[...]
        dg = dg_ref[...]
        xprev = jnp.where(has_prev, xp_ref[0].astype(F32)[8:], 0.0)
        hprev8 = jnp.where(has_prev, hp_ref[...], 0.0)
        reset = pos_ref[...] == 0
        cw = cw_ref[...]
        wa = wa_ref[0]
        wx = wx_ref[0]
        lam_v = lam_ref[...]
        xc, r, ig, sp, a, mult = _rnn_gates(xr, xprev, cw, cb_ref[...], wa, ba_ref[0], wx, bx_ref[0], lam_v, reset)

        sz = _sigmoid(z)
        dh = dg * (z * sz)
        dz_ref[...] = (dg * hr_blk * (sz * (1.0 + z * (1.0 - sz)))).astype(BF16)

        an = _shift_up(a, a_next[...], 1)
        g, g_first = _scan(an, dh, tt, g_next[0:1], up=True)
        a_next[...] = jnp.broadcast_to(a[0:1], (8, DH))
        g_next[...] = jnp.broadcast_to(g_first, (8, DH))

        hm1 = _shift_down(hprev8, hr_blk, 1)
        da = g * hm1
        dmult = g * (ig * xc)
        di = g * (mult * xc)
        dxc = g * (mult * ig)
        dla = jnp.where(reset, 0.0, da * a - dmult * (a * a) / mult)
        dr = dla * (-LRU_C * sp)
        dsp = _colsum(dla * (-LRU_C * r))
        glam_ref[0] += dsp * (-_sigmoid(-lam_v))
        dpa = dr * r * (1.0 - r)
        dpx = di * ig * (1.0 - ig)
        dpab = dpa.astype(BF16)
        dpxb = dpx.astype(BF16)
        dxc = dxc + _dot_nt(dpab, wa.astype(BF16)) + _dot_nt(dpxb, wx.astype(BF16))
        xcb = xc.astype(BF16)
        gwa_ref[0] += _dot_tn(xcb, dpab)
        gwx_ref[0] += _dot_tn(xcb, dpxb)
        gba_ref[0] += _colsum(dpa)
        gbx_ref[0] += _colsum(dpx)

        dxr = cw[3:4] * dxc
        for d in (1, 2, 3):
            dxr = dxr + cw[3 - d:4 - d] * _shift_up(dxc, dxc_next[...], d)
        dxr_ref[...] = dxr.astype(BF16)
        dxc_next[...] = dxc[0:8]
        gcb_ref[0] += _colsum(dxc)
        gcw_ref[0, 3:4, :] += _colsum(xr * dxc)
        for d in (1, 2, 3):
            gcw_ref[0, 3 - d:4 - d, :] += _colsum(_shift_down(xprev, xr, d) * dxc)

    rt = lambda t: nt - 1 - t
    prev8 = lambda t: jnp.maximum(rt(t) * t8 - 1, 0)
    head_row = lambda hh, t: (0, hh)
    hsm = lambda hh, t: (hh, 0, 0)
    return pl.pallas_call(
        body, name="rnn_bwd", grid=(H, nt),
        in_specs=[pl.BlockSpec((1, tt, DH), lambda hh, t: (0, rt(t), hh)),
                  pl.BlockSpec((1, tt, DH), lambda hh, t: (0, rt(t), H + hh)),
                  pl.BlockSpec((1, 16, DH), lambda hh, t: (0, jnp.maximum(rt(t) * (tt // 16) - 1, 0), hh)),
                  pl.BlockSpec((tt, DH), lambda hh, t: (rt(t), hh)),
                  pl.BlockSpec((8, DH), lambda hh, t: (prev8(t), hh)),
                  pl.BlockSpec((tt, DH), lambda hh, t: (rt(t), hh)),
                  pl.BlockSpec((tt, 1), lambda hh, t: (rt(t), 0)),
                  pl.BlockSpec((4, DH), head_row), pl.BlockSpec((1, DH), head_row),
                  pl.BlockSpec((1, DH, DH), hsm), pl.BlockSpec((1, 1, DH), hsm),
                  pl.BlockSpec((1, DH, DH), hsm), pl.BlockSpec((1, 1, DH), hsm),
                  pl.BlockSpec((1, DH), head_row)],
        out_specs=(pl.BlockSpec((tt, DH), lambda hh, t: (rt(t), hh)), pl.BlockSpec((tt, DH), lambda hh, t: (rt(t), hh)),
                   pl.BlockSpec((1, DH, DH), hsm), pl.BlockSpec((1, 1, DH), hsm),
                   pl.BlockSpec((1, DH, DH), hsm), pl.BlockSpec((1, 1, DH), hsm),
                   pl.BlockSpec((1, 1, DH), hsm), pl.BlockSpec((1, 4, DH), hsm), pl.BlockSpec((1, 1, DH), hsm)),
        out_shape=(jax.ShapeDtypeStruct((s, D), BF16), jax.ShapeDtypeStruct((s, D), BF16),
                   jax.ShapeDtypeStruct((H, DH, DH), F32), jax.ShapeDtypeStruct((H, 1, DH), F32),
                   jax.ShapeDtypeStruct((H, DH, DH), F32), jax.ShapeDtypeStruct((H, 1, DH), F32),
                   jax.ShapeDtypeStruct((H, 1, DH), F32), jax.ShapeDtypeStruct((H, 4, DH), F32),
                   jax.ShapeDtypeStruct((H, 1, DH), F32)),
        scratch_shapes=[pltpu.VMEM((8, DH), F32), pltpu.VMEM((8, DH), F32), pltpu.VMEM((8, DH), F32)],
        compiler_params=_params(("parallel", "arbitrary")),
    )(p, p, p, hr, hr, dgr, pos, conv_w, conv_b, w_a, b_a, w_x, b_x, lam)


def _rope(t, c, sa, sb):
    return t * c + pltpu.roll(t, DH - ROT // 2, 1) * sa + pltpu.roll(t, ROT // 2, 1) * sb


def _rope_bwd(g, c, sa, sb):
    return g * c + pltpu.roll(g * sa, ROT // 2, 1) + pltpu.roll(g * sb, DH - ROT // 2, 1)


def _unit_bases(gi, u):
    dil = DILATIONS[gi]
    if dil == 1:
        return u * UB, SPAN + (u - 1) * UB, u == 0
    if dil == 4:
        blk, r = u // 4, u % 4
        return blk * 4 * UB + r, SPAN + (blk - 1) * 4 * UB + r, blk == 0
    return u, u, True


def _unit_slices(gi, u):
    dil = DILATIONS[gi]
    qb0, kb0, first = _unit_bases(gi, u)
    if dil == 1:
        return pl.ds(pl.multiple_of(qb0, UB), UB), pl.ds(pl.multiple_of(kb0, UB), 2 * UB), first
    return pl.ds(qb0, UB, stride=dil), pl.ds(kb0, 2 * UB, stride=dil), first


def _bdot(a, b):
    return lax.dot_general(a, b, (((2,), (1,)), ((0,), (0,))), preferred_element_type=F32)


def _bdot_nt(a, b):
    return lax.dot_general(a, b, (((2,), (2,)), ((0,), (0,))), preferred_element_type=F32)


def _bdot_tn(a, b):
    return lax.dot_general(a, b, (((1,), (1,)), ((0,), (0,))), preferred_element_type=F32)


def _band_mask(first_in_span, has_prev):
    qi = lax.broadcasted_iota(jnp.int32, (UB, 2 * UB), 0)
    ki = lax.broadcasted_iota(jnp.int32, (UB, 2 * UB), 1)
    dist = UB + qi - ki
    band = (dist >= 0) & (dist <= UB)
    return band & ((ki >= UB) | jnp.logical_not(first_in_span) | has_prev)


def _gather_halves(phase, src_hbm, dst_hbm, ss, rs):
    half = src_hbm.shape[0] // 2
    cx, cy, cc = _coords()
    j = 2 * cx + cy
    sib = (cx, cy, 1 - cc)
    mine = pl.ds(cc * half, half)
    theirs = pl.ds((1 - cc) * half, half)
    chips = ((1, 0), (0, 1), (1, 1))
    for q, (kx, ky) in enumerate(chips):
        jq = j ^ (2 * kx + ky)
        out = _rcopy(src_hbm.at[mine], dst_hbm.at[j, mine], ss.at[q], rs.at[q], (_flip(cx, kx), _flip(cy, ky), cc))
        landed = _rcopy(src_hbm.at[mine], dst_hbm.at[jq, mine], ss.at[q], rs.at[q], sib)
        onward = _rcopy(dst_hbm.at[jq, mine], dst_hbm.at[jq, mine], ss.at[3 + q], rs.at[3 + q], sib)
        from_sib = _rcopy(src_hbm.at[mine], dst_hbm.at[jq, theirs], ss.at[3 + q], rs.at[3 + q], sib)
        if phase == 0:
            out.start()
        elif phase == 1:
            landed.wait_recv()
            onward.start()
        else:
            from_sib.wait_recv()
            out.wait_send()
            onward.wait_send()


def _attn_fwd(p, rc, rsa, rsb, w3sh):
    s = p.shape[1]
    ns = s // SPAN
    nunit = SPAN // UB

    def body(q_ref, k_ref, v_ref, z_ref, c_ref, sa_ref, sb_ref, w3_hbm, o_ref, lse_ref, ga_ref, w3g_hbm,
             qr, kf, vf, acc, mm, ll, ss, rs):
        hh, n = pl.program_id(0), pl.program_id(1)
        for phase, at_head, at_span in ((0, 0, 0), (1, H // 2, 0), (2, H - 1, ns - 1)):
            @pl.when((hh == at_head) & (n == at_span))
            def _(phase=phase):
                _gather_halves(phase, w3_hbm, w3g_hbm, ss, rs)

        @pl.when(n == 0)
        def _():
            kf[0:SPAN] = jnp.zeros((SPAN, DH), F32)
            vf[0:SPAN] = jnp.zeros((SPAN, DH), F32)

        c, sa, sb = c_ref[...], sa_ref[...], sb_ref[...]
        qr[...] = _rope(q_ref[0].astype(F32), c, sa, sb)
        kf[SPAN:] = _rope(k_ref[0].astype(F32), c, sa, sb)
        vf[SPAN:] = v_ref[0].astype(F32)
        has_prev = n > 0

        for gi, dil in enumerate(DILATIONS):
            def trip(t, carry, gi=gi, dil=dil):
                qsls, ksls, firsts = [], [], []
                for b in range(UNIT_BATCH):
                    qsl, ksl, first = _unit_slices(gi, t * UNIT_BATCH + b)
                    qsls.append(qsl)
                    ksls.append(ksl)
                    firsts.append(first)
                qb = jnp.stack([qr[qsl, :].astype(BF16) for qsl in qsls])
                kb = jnp.stack([kf[ksl, :].astype(BF16) for ksl in ksls])
                vb = jnp.stack([vf[ksl, :].astype(BF16) for ksl in ksls])
                s_all = _bdot_nt(qb, kb)
                prs = []
                for b in range(UNIT_BATCH):
                    sc = jnp.where(_band_mask(firsts[b], has_prev), s_all[b] * SCALE, NEG)
                    m = jnp.max(sc, axis=-1, keepdims=True)
                    pr = jnp.exp(sc - m)
                    l = jnp.sum(pr, axis=-1, keepdims=True)
                    mm[gi, qsls[b], :] = jnp.broadcast_to(m, (UB, DH))
                    ll[gi, qsls[b], :] = jnp.broadcast_to(l, (UB, DH))
                    prs.append(pr.astype(BF16))
                o_all = _bdot(jnp.stack(prs), vb)
                for b in range(UNIT_BATCH):
                    acc[gi, qsls[b], :] = o_all[b]
                return carry

            lax.fori_loop(0, nunit // UNIT_BATCH, trip, 0)

        m_all =jnp.maximum(jnp.maximum(mm[0], mm[1]), mm[2])
        num = jnp.zeros((SPAN, DH), F32)
        den = jnp.zeros((SPAN, DH), F32)
        for gi in range(3):
            w = jnp.exp(mm[gi] - m_all)
            num = num + w * acc[gi]
            den = den + w * ll[gi]
        o = num / den
        o_ref[...] = o
        lse_ref[...] = m_all + jnp.log(den)
        z = z_ref[0].astype(F32)
        ga_ref[...] = (o * (z * _sigmoid(z))).astype(BF16)
        kf[0:SPAN] = kf[SPAN:]
        vf[0:SPAN] = vf[SPAN:]

    blk = lambda piece, off: pl.BlockSpec((1, SPAN, DH), lambda hh, n: (piece, n, off + hh))
    tab = pl.BlockSpec((SPAN, DH), lambda hh, n: (n, 0))
    outb = pl.BlockSpec((SPAN, DH), lambda hh, n: (n, hh))
    any_ = pl.BlockSpec(memory_space=pl.ANY)
    return pl.pallas_call(
        body, name="attn_fwd", grid=(H, ns),
        in_specs=[blk(1, 0), blk(1, H), blk(2, 0), blk(2, H), tab, tab, tab, any_],
        out_specs=(outb, outb, outb, any_),
        out_shape=(jax.ShapeDtypeStruct((s, D), F32), jax.ShapeDtypeStruct((s, D), F32),
                   jax.ShapeDtypeStruct((s, D), BF16), jax.ShapeDtypeStruct((4,) + w3sh.shape, w3sh.dtype)),
        scratch_shapes=[pltpu.VMEM((SPAN, DH), F32), pltpu.VMEM((2 * SPAN, DH), F32), pltpu.VMEM((2 * SPAN, DH), F32),
                        pltpu.VMEM((3, SPAN, DH), F32), pltpu.VMEM((3, SPAN, DH), F32), pltpu.VMEM((3, SPAN, DH), F32),
                        pltpu.SemaphoreType.DMA((6,)), pltpu.SemaphoreType.DMA((6,))],
        compiler_params=_params(("arbitrary", "arbitrary")),
    )(p, p, p, p, rc, rsa, rsb, w3sh)


def _attn_bwd(p, o, lse, dga, rc, rsa, rsb):
    s = p.shape[1]
    ns = s // SPAN
    nunit = SPAN // UB

    def body(q_ref, k_ref, kp_ref, v_ref, vp_ref, z_ref, c_ref, sa_ref, sb_ref, cp_ref, sap_ref, sbp_ref,
             o_ref, lse_ref, dg_ref, dq_ref, dk_ref, dv_ref, dz_ref,
             qr, kf, vf, dof, dlt, dqa, dkf, dvf):
        step = pl.program_id(1)
        n = ns - 1 - step
        has_prev = n > 0

        @pl.when(step == 0)
        def _():
            dkf[...] = jnp.zeros_like(dkf)
            dvf[...] = jnp.zeros_like(dvf)

        @pl.when(step > 0)
        def _():
            dkf[SPAN:] = dkf[0:SPAN]
            dvf[SPAN:] = dvf[0:SPAN]
            dkf[0:SPAN] = jnp.zeros((SPAN, DH), F32)
            dvf[0:SPAN] = jnp.zeros((SPAN, DH), F32)

        c, sa, sb = c_ref[...], sa_ref[...], sb_ref[...]
        qr[...] = _rope(q_ref[0].astype(F32), c, sa, sb)
        kf[SPAN:] = _rope(k_ref[0].astype(F32), c, sa, sb)
        vf[SPAN:] = v_ref[0].astype(F32)
        kf[0:SPAN] = jnp.where(has_prev, _rope(kp_ref[0].astype(F32), cp_ref[...], sap_ref[...], sbp_ref[...]), 0.0)
        vf[0:SPAN] = jnp.where(has_prev, vp_ref[0].astype(F32), 0.0)
        z = z_ref[0].astype(F32)
        sz = _sigmoid(z)
        dg = dg_ref[...]
        ov = o_ref[...]
        do = dg * (z * sz)
        dz_ref[...] = (dg * ov * (sz * (1.0 + z * (1.0 - sz)))).astype(BF16)
        dof[...] = do
        dlt[...] = jnp.broadcast_to(jnp.sum(do * ov, axis=-1, keepdims=True), (SPAN, DH))
        dqa[...] = jnp.zeros_like(dqa)

        for gi, dil in enumerate(DILATIONS):
            def trip(t, carry, gi=gi, dil=dil):
                qsls, ksls, firsts = [], [], []
                for b in range(UNIT_BATCH):
                    qsl, ksl, first = _unit_slices(gi, t * UNIT_BATCH + b)
                    qsls.append(qsl)
                    ksls.append(ksl)
                    firsts.append(first)
                qb = jnp.stack([qr[qsl, :].astype(BF16) for qsl in qsls])
                kb = jnp.stack([kf[ksl, :].astype(BF16) for ksl in ksls])
                vb = jnp.stack([vf[ksl, :].astype(BF16) for ksl in ksls])
                dob = jnp.stack([dof[qsl, :].astype(BF16) for qsl in qsls])
                s_all = _bdot_nt(qb, kb)
                dp_all = _bdot_nt(dob, vb)
                prs, dss = [], []
                for b in range(UNIT_BATCH):
                    lse_b = lse_ref[qsls[b], :]
                    dl_b = dlt[qsls[b], :]
                    pr = jnp.exp(s_all[b] * SCALE - jnp.concatenate([lse_b, lse_b], axis=1))
                    pr = jnp.where(_band_mask(firsts[b], has_prev), pr, 0.0)
                    prs.append(pr.astype(BF16))
                    dss.append((pr * (dp_all[b] - jnp.concatenate([dl_b, dl_b], axis=1)) * SCALE).astype(BF16))
                ds_all = jnp.stack(dss)
                dv_all = _bdot_tn(jnp.stack(prs), dob)
                dq_all = _bdot(ds_all, kb)
                dk_all = _bdot_tn(ds_all, qb)
                for b in range(UNIT_BATCH):
                    dvf[ksls[b], :] += dv_all[b]
                    dqa[qsls[b], :] += dq_all[b]
                    dkf[ksls[b], :] += dk_all[b]
                return carry

            lax.fori_loop(0, nunit // UNIT_BATCH, trip, 0)

        dq_ref[...] = _rope_bwd(dqa[...], c, sa, sb).astype(BF16)
        dk_ref[...] = _rope_bwd(dkf[SPAN:], c, sa, sb).astype(BF16)
        dv_ref[...] = dvf[SPAN:].astype(BF16)

    rn = lambda n: ns - 1 - n
    pn = lambda n: jnp.maximum(ns - 2 - n, 0)
    blk = lambda piece, off: pl.BlockSpec((1, SPAN, DH), lambda hh, n: (piece, rn(n), off + hh))
    blkp = lambda piece, off: pl.BlockSpec((1, SPAN, DH), lambda hh, n: (piece, pn(n), off + hh))
    tab = pl.BlockSpec((SPAN, DH), lambda hh, n: (rn(n), 0))
    tabp = pl.BlockSpec((SPAN, DH), lambda hh, n: (pn(n), 0))
    io = pl.BlockSpec((SPAN, DH), lambda hh, n: (rn(n), hh))
    return pl.pallas_call(
        body, name="attn_bwd", grid=(H, ns),
        in_specs=[blk(1, 0), blk(1, H), blkp(1, H), blk(2, 0), blkp(2, 0), blk(2, H),
                  tab, tab, tab, tabp, tabp, tabp, io, io, io],
        out_specs=(io, io, io, io),
        out_shape=tuple(jax.ShapeDtypeStruct((s, D), BF16) for _ in range(4)),
        scratch_shapes=[pltpu.VMEM((SPAN, DH), F32), pltpu.VMEM((2 * SPAN, DH), F32), pltpu.VMEM((2 * SPAN, DH), F32),
                        pltpu.VMEM((SPAN, DH), F32), pltpu.VMEM((SPAN, DH), F32), pltpu.VMEM((SPAN, DH), F32),
                        pltpu.VMEM((2 * SPAN, DH), F32), pltpu.VMEM((2 * SPAN, DH), F32)],
        compiler_params=_params(("parallel", "arbitrary")),
    )(p, p, p, p, p, p, rc, rsa, rsb, rc, rsa, rsb, o, lse, dga)


def _tail(gr, ga, p, x, tgt, w3, b_gate, gate, g_final, tm=256):
    s = x.shape[0]
    nt = s // tm

    def body(gr_ref, ga_ref, pr_ref, pa_ref, x_ref, t_ref, bg_ref, gate_ref, gf_ref, w_hbm,
             dgr_ref, dga_ref, dc_ref, dx2_ref, vec_ref, go_hbm, w_s, acc_s, sem):
        i = pl.program_id(0)

        @pl.when(i == 0)
        def _():
            cp = pltpu.make_async_copy(w_hbm, w_s, sem.at[12])
            cp.start()
            acc_s[...] = jnp.zeros_like(acc_s)
            vec_ref[...] = jnp.zeros_like(vec_ref)
            cp.wait()

        grb = gr_ref[...]
        gab = ga_ref[...]
        bg = bg_ref[...]
        gate_v = gate_ref[...]
        gf = gf_ref[...]
        y_r = _dot(grb, w_s[0])
        y_a = _dot(gab, w_s[1])
        sr = _sigmoid(pr_ref[0].astype(F32) + bg[:, :D])
        sa = _sigmoid(pa_ref[0].astype(F32) + bg[:, D:])
        mb = (sr * y_r + sa * y_a).astype(BF16)
        u = _dot(mb, w_s[2])
        x2 = x_ref[...] + gate_v * u
        rstd = lax.rsqrt(jnp.mean(x2 * x2, axis=-1, keepdims=True) + EPS)
        xh = x2 * rstd
        e = xh * gf - t_ref[...]
        dy = e * (1.0 / D)
        dyg = dy * gf
        dx2 = rstd * (dyg - xh * jnp.mean(dyg * xh, axis=-1, keepdims=True))
        dx2_ref[...] = dx2
        dub = (dx2 * gate_v).astype(BF16)
        dm = _dot_nt(dub, w_s[2])
        dyr = (dm * sr).astype(BF16)
        dya = (dm * sa).astype(BF16)
        dpr = dm * y_r * (sr * (1.0 - sr))
        dpa = dm * y_a * (sa * (1.0 - sa))
        dc_ref[:, :D] = dpr.astype(BF16)
        dc_ref[:, D:] = dpa.astype(BF16)
        dgr_ref[...] = _dot_nt(dyr, w_s[0])
        dga_ref[...] = _dot_nt(dya, w_s[1])
        acc_s[0] += _dot_tn(grb, dyr)
        acc_s[1] += _dot_tn(gab, dya)
        acc_s[2] += _dot_tn(mb, dub)
        vec_ref[0:1, :] += _colsum(dy * xh)
        vec_ref[1:2, :] += _colsum(dx2 * u)
        vec_ref[2:3, :] += _colsum(dpr)
        vec_ref[3:4, :] += _colsum(dpa)
        vec_ref[4:5, :] += _colsum(e * e)

        @pl.when(i == nt - 1)
        def _():
            vec_ref[4:5, :] = jnp.broadcast_to(jnp.sum(vec_ref[4:5, :]) * (0.5 / D), (1, D))
            cps = []
            for w in range(3):
                for j in range(4):
                    cps.append(pltpu.make_async_copy(acc_s.at[w, pl.ds(256 * j, 256)],
                                                     go_hbm.at[j, pl.ds(256 * w, 256)], sem.at[4 * w + j]))
            for cp in cps:
                cp.start()
            for cp in cps:
                cp.wait()

    rowt = lambda i: (i, 0)
    row = lambda w: pl.BlockSpec((1, w), lambda i: (0, 0))
    any_ = pl.BlockSpec(memory_space=pl.ANY)
    return pl.pallas_call(
        body, name="tail", grid=(nt,),
        in_specs=[pl.BlockSpec((tm, D), rowt), pl.BlockSpec((tm, D), rowt),
                  pl.BlockSpec((1, tm, D), lambda i: (3, i, 0)), pl.BlockSpec((1, tm, D), lambda i: (3, i, 1)),
                  pl.BlockSpec((tm, D), rowt), pl.BlockSpec((tm, D), rowt),
                  row(2 * D), row(D), row(D), any_],
        out_specs=(pl.BlockSpec((tm, D), rowt), pl.BlockSpec((tm, D), rowt), pl.BlockSpec((tm, 2 * D), rowt),
                   pl.BlockSpec((tm, D), rowt), pl.BlockSpec((8, D), lambda i: (0, 0)), any_),
        out_shape=(jax.ShapeDtypeStruct((s, D), F32), jax.ShapeDtypeStruct((s, D), F32),
                   jax.ShapeDtypeStruct((s, 2 * D), BF16), jax.ShapeDtypeStruct((s, D), F32),
                   jax.ShapeDtypeStruct((8, D), F32), jax.ShapeDtypeStruct((4, 768, D), F32)),
        scratch_shapes=[pltpu.VMEM((3, D, D), BF16), pltpu.VMEM((3, D, D), F32), pltpu.SemaphoreType.DMA((13,))],
        compiler_params=_params(("arbitrary",)),
    )(gr, ga, p, p, x, tgt, b_gate, gate, g_final, w3)


def _pieces_steps(pieces):
    out, s0 = [], 0
    for a in pieces:
        n = a.shape[1] // D
        out.append((s0, n))
        s0 += n
    return out, s0


def _inproj_bwd_x(pieces, wg, x, dx2, gn, scale, sums, tm=512):
    s = x.shape[0]
    np_ = len(pieces)
    na = len(sums)
    ni = s // tm
    groups, cur, width = [], [], 0
    for t, a in enumerate(pieces):
        cur.append(t)
        width += a.shape[1]
        if width == PW:
            groups.append(cur)
            cur, width = [], 0
    assert len(groups) == 4 and not cur

    def body(*refs):
        d_refs = refs[:np_]
        w_hbm, x_ref, dx2_ref, gn_ref, sc_ref = refs[np_:np_ + 5]
        q_refs = refs[np_ + 5:np_ + 5 + na]
        gx_ref, vec_ref = refs[np_ + 5 + na:np_ + 7 + na]
        r_refs = refs[np_ + 7 + na:np_ + 7 + 2 * na]
        w_s, wsem, ss, rs = refs[np_ + 7 + 2 * na:]
        i = pl.program_id(0)

        def scatter_copies():
            cx, cy, cc = _coords()
            j = 2 * cx + cy
            cps = []
            for t, (q, r) in enumerate(zip(q_refs, r_refs)):
                for e, (kx, ky) in enumerate(((1, 0), (0, 1), (1, 1))):
                    cps.append(_rcopy(q.at[j ^ (2 * kx + ky)], r.at[e], ss.at[3 * t + e], rs.at[3 * t + e],
                                      (_flip(cx, kx), _flip(cy, ky), cc)))
            return cps

        def w_copy(pc):
            return pltpu.make_async_copy(w_hbm.at[pc, pl.ds(0, D), :], w_s.at[pc], wsem.at[pc])

        @pl.when(i == 0)
        def _():
            for pc in range(4):
                w_copy(pc).start()
            vec_ref[...] = jnp.zeros_like(vec_ref)
            for cp in scatter_copies():
                cp.start()

        dh = None
        for pc, group in enumerate(groups):
            @pl.when(i == 0)
            def _(pc=pc):
                w_copy(pc).wait()

            tiles = [d_refs[t][...] for t in group]
            lhs = tiles[0] if len(tiles) == 1 else jnp.concatenate(tiles, axis=1)
            part = _dot_nt(lhs, w_s[pc])
            dh = part if dh is None else dh + part

        xt = x_ref[...]
        rstd = lax.rsqrt(jnp.mean(xt * xt, axis=-1, keepdims=True) + EPS)
        xh = xt * rstd
        gn_v = gn_ref[...]
        sc1 = 1.0 + sc_ref[...]
        dhx = dh * xh
        vec_ref[0:1, :] += _colsum(dh)
        vec_ref[1:2, :] += _colsum(dhx) * gn_v
        vec_ref[2:3, :] += _colsum(dhx) * sc1
        dxh = dh * (gn_v * sc1)
        gx_ref[...] = rstd * (dxh - xh * jnp.mean(dxh * xh, axis=-1, keepdims=True)) + dx2_ref[...]

        @pl.when(i == ni - 1)
        def _():
            for cp in scatter_copies():
                cp.wait()

    rowt = lambda i: (i, 0)
    row = pl.BlockSpec((1, D), lambda i: (0, 0))
    any_ = pl.BlockSpec(memory_space=pl.ANY)
    outs = pl.pallas_call(
        body, name="inproj_bwd_x", grid=(ni,),
        in_specs=[pl.BlockSpec((tm, a.shape[1]), rowt) for a in pieces] +
                 [any_, pl.BlockSpec((tm, D), rowt), pl.BlockSpec((tm, D), rowt), row, row] + [any_] * na,
        out_specs=(pl.BlockSpec((tm, D), rowt), pl.BlockSpec((8, D), lambda i: (0, 0))) + (any_,) * na,
        out_shape=(jax.ShapeDtypeStruct((s, D), F32), jax.ShapeDtypeStruct((8, D), F32)) +
                  tuple(jax.ShapeDtypeStruct((3,) + q.shape[1:], q.dtype) for q in sums),
        scratch_shapes=[pltpu.VMEM((4, D, PW), BF16), pltpu.SemaphoreType.DMA((4,)),
                        pltpu.SemaphoreType.DMA((3 * na,)), pltpu.SemaphoreType.DMA((3 * na,))],
        compiler_params=_params(("arbitrary",)),
    )(*pieces, wg, x, dx2, gn, scale, *sums)
    return outs[0], outs[1], outs[2:]


def _inproj_bwd_w(pieces, hbf, g_out, tk=1024):
    s = hbf.shape[0]
    steps, nk = _pieces_steps(pieces)
    npc = PW // D
    ns = s // tk
    np_ = len(pieces)
    hr = D // 2
    ohr = g_out.shape[1] // 2
    ocr = _chunk_rows(g_out)
    ochunks = [(j, r0) for j in range(g_out.shape[0]) for r0 in range(0, ohr, ocr)]
    noc = len(ochunks)

    def body(*refs):
        d_refs = refs[:np_]
        h_ref, go_hbm, g_ref, rb_hbm, rbo_hbm, stage, ss, rs, oss, ors = refs[np_:]
        cb, k = pl.program_id(0), pl.program_id(1)
        cx, cy, cc = _coords()
        sib = (cx, cy, 1 - cc)

        def block_copy(b):
            return _rcopy(stage.at[b % 2],
                          rb_hbm.at[b // npc, :, pl.ds(pl.multiple_of((b % npc) * D, D), D)], ss.at[b], rs.at[b], sib)

        def out_copy(e):
            j, r0 = ochunks[e]
            return _rcopy(go_hbm.at[j, pl.ds((1 - cc) * ohr + r0, ocr), :], rbo_hbm.at[j, pl.ds(r0, ocr), :],
                          oss.at[e], ors.at[e], sib)

        @pl.when((cb == 0) & (k == 0))
        def _():
            for e in range(noc):
                out_copy(e).start()

        @pl.when(k == 0)
        def _():
            g_ref[...] = jnp.zeros_like(g_ref)

        for (s0, n), d_ref in zip(steps, d_refs):
            @pl.when((cb >= s0) & (cb < s0 + n))
            def _(d_ref=d_ref):
                g_ref[0] += _dot_tn(h_ref[...], d_ref[...])

        @pl.when((k == ns - 1) & (cb > 1))
        def _():
            block_copy(cb - 2).wait_send()

        @pl.when(k == ns - 1)
        def _():
            stage[cb % 2] = g_ref[0, pl.ds(pl.multiple_of((1 - cc) * hr, hr), hr), :]
            block_copy(cb).start()

        @pl.when((k == ns - 1) & (cb == nk - 1))
        def _():
            block_copy(nk - 2).wait_send()
            block_copy(nk - 1).wait_send()
            for b in range(nk):
                block_copy(b).wait_recv()
            for e in range(noc):
                out_copy(e).wait_recv()
                out_copy(e).wait_send()

    def piece_spec(s0, n):
        def imap(cb, k):
            active = (cb >= s0) & (cb < s0 + n)
            return (jnp.where(active, k, 0), jnp.clip(cb - s0, 0, n - 1))
        return pl.BlockSpec((tk, D), imap)

    any_ = pl.BlockSpec(memory_space=pl.ANY)
    return pl.pallas_call(
        body, name="inproj_bwd_w", grid=(nk, ns),
        in_specs=[piece_spec(s0, n) for s0, n in steps] + [pl.BlockSpec((tk, D), lambda cb, k: (k, 0)), any_],
        out_specs=(pl.BlockSpec((1, D, D), lambda cb, k: (cb // npc, 0, cb % npc)), any_, any_),
        out_shape=(jax.ShapeDtypeStruct((4, D, PW), F32), jax.ShapeDtypeStruct((4, hr, PW), F32),
                   jax.ShapeDtypeStruct((g_out.shape[0], ohr, g_out.shape[2]), F32)),
        scratch_shapes=[pltpu.VMEM((2, hr, D), F32), pltpu.SemaphoreType.DMA((nk,)), pltpu.SemaphoreType.DMA((nk,)),
                        pltpu.SemaphoreType.DMA((noc,)), pltpu.SemaphoreType.DMA((noc,))],
        compiler_params=_params(("arbitrary", "arbitrary")),
    )(*pieces, hbf, g_out)


D2D_CHUNK_BYTES = 512 * 1024


def _chunk_rows(a):
    return max(8, D2D_CHUNK_BYTES // (a.shape[-1] * a.dtype.itemsize))


def _pair_swap(arrs):
    na = len(arrs)
    chunks = []
    for t, a in enumerate(arrs):
        cr = _chunk_rows(a)
        chunks += [(t, r0, cr) for r0 in range(0, a.shape[0], cr)]
    nch = len(chunks)

    def body(*refs):
        a_refs = refs[:na]
        o_refs = refs[na:2 * na]
        ss, rs = refs[2 * na:]
        x, y, c = _coords()
        sib = (x, y, 1 - c)
        rcs = []
        for n, (t, r0, cr) in enumerate(chunks):
            rows = pl.ds(r0, cr)
            rc = _rcopy(a_refs[t].at[rows, :], o_refs[t].at[rows, :], ss.at[n], rs.at[n], sib)
            rc.start()
            rcs.append(rc)
        for rc in rcs:
            rc.wait_recv()
        for rc in rcs:
            rc.wait_send()

    any_ = pl.BlockSpec(memory_space=pl.ANY)
    return pl.pallas_call(
        body, name="pair_swap",
        out_shape=tuple(jax.ShapeDtypeStruct(a.shape, a.dtype) for a in arrs),
        in_specs=[any_] * na, out_specs=tuple([any_] * na),
        scratch_shapes=[pltpu.SemaphoreType.DMA((nch,)), pltpu.SemaphoreType.DMA((nch,))],
        compiler_params=_params(),
    )(*arrs)


def _add_half(full, rb, core, tr):
    n, r, cdim = full.shape
    nb = r // 2 // tr

    def body(c_ref, a_ref, b_ref, o_ref, ob_ref):
        tot = a_ref[...] + b_ref[...]
        o_ref[...] = tot
        ob_ref[...] = tot.astype(BF16)

    mine = pl.BlockSpec((1, tr, cdim), lambda i, j, c_ref: (i, c_ref[0] * nb + j, 0))
    spec = pl.BlockSpec((1, tr, cdim), lambda i, j, c_ref: (i, j, 0))
    return pl.pallas_call(
        body, name="add_half",
        grid_spec=pltpu.PrefetchScalarGridSpec(num_scalar_prefetch=1, grid=(n, nb), in_specs=[mine, spec],
                                               out_specs=(spec, spec)),
        out_shape=(jax.ShapeDtypeStruct(rb.shape, rb.dtype), jax.ShapeDtypeStruct(rb.shape, BF16)),
        compiler_params=_params(("parallel", "parallel")),
    )(core, full, rb)


def _sum_slots(q, r3, shard, tr):
    _, r, cdim = q.shape

    def body(j_ref, q_ref, r_ref, o_ref):
        o_ref[...] = ((q_ref[0] + r_ref[0].astype(F32)) + r_ref[1].astype(F32)) + r_ref[2].astype(F32)

    return pl.pallas_call(
        body, name="sum_slots",
        grid_spec=pltpu.PrefetchScalarGridSpec(
            num_scalar_prefetch=1, grid=(r // tr,),
            in_specs=[pl.BlockSpec((1, tr, cdim), lambda i, j_ref: (j_ref[0], i, 0)),
                      pl.BlockSpec((3, tr, cdim), lambda i, j_ref: (0, i, 0))],
            out_specs=pl.BlockSpec((tr, cdim), lambda i, j_ref: (i, 0))),
        out_shape=jax.ShapeDtypeStruct((r, cdim), q.dtype),
        compiler_params=_params(("parallel",)),
    )(shard, q, r3)


def _allreduce_small(packs):
    na = len(packs)

    def body(*refs):
        p_refs, o_refs, rbufs = refs[:na], refs[na:2 * na], refs[2 * na:3 * na]
        s1, r1, s2, r2 = refs[3 * na:]
        me = _my_index()
        sends = []

        def chunk(t, d):
            ch = p_refs[t].shape[0] // NDEV
            return pl.ds(pl.multiple_of(d * ch, 8), ch)

        for t in range(na):
            for k in range(1, NDEV):
                e = 7 * t + k - 1
                cp = _rcopy(p_refs[t].at[chunk(t, me ^ k)], rbufs[t].at[me], s1.at[e], r1.at[e], _peer(k))
                cp.start()
                sends.append(cp)
            rbufs[t][me] = p_refs[t][chunk(t, me), :]
        for t in range(na):
            for k in range(1, NDEV):
                e = 7 * t + k - 1
                _rcopy(p_refs[t].at[chunk(t, me)], rbufs[t].at[me ^ k], s1.at[e], r1.at[e], _peer(k)).wait_recv()
            tot = rbufs[t][0]
            for d in range(1, NDEV):
                tot = tot + rbufs[t][d]
            o_refs[t][chunk(t, me), :] = tot
            for k in range(1, NDEV):
                e = 7 * t + k - 1
                cp = _rcopy(o_refs[t].at[chunk(t, me)], o_refs[t].at[chunk(t, me)], s2.at[e], r2.at[e], _peer(k))
                cp.start()
                sends.append(cp)
        for t in range(na):
            for k in range(1, NDEV):
                e = 7 * t + k - 1
                _rcopy(o_refs[t].at[chunk(t, me)], o_refs[t].at[chunk(t, me ^ k)], s2.at[e], r2.at[e],
                       _peer(k)).wait_recv()
        for cp in sends:
            cp.wait_send()

    vm = pl.BlockSpec(memory_space=pltpu.VMEM)
    return pl.pallas_call(
        body, name="allreduce_small",
        out_shape=tuple(jax.ShapeDtypeStruct(a.shape, F32) for a in packs),
        in_specs=[vm] * na, out_specs=tuple([vm] * na),
        scratch_shapes=[pltpu.VMEM((NDEV, a.shape[0] // NDEV, a.shape[1]), F32) for a in packs] +
                       [pltpu.SemaphoreType.DMA((7 * na,)) for _ in range(4)],
        compiler_params=_params(),
    )(*packs)


def _adamw_update(w, g, m, v):
    nm = B1 * m + (1.0 - B1) * g
    nv = B2 * v + (1.0 - B2) * (g * g)
    m_hat = nm / (1.0 - B1 ** STEP)
    v_hat = nv / (1.0 - B2 ** STEP)
    return -LR * (m_hat / (jnp.sqrt(v_hat) + ADAM_EPS) + WD * w), nm, nv


def _adamw(w, g, m, v, tr):
    r, cdim = w.shape

    def body(w_ref, g_ref, m_ref, v_ref, d_ref, nm_ref, nv_ref):
        d_ref[...], nm_ref[...], nv_ref[...] = _adamw_update(w_ref[...], g_ref[...], m_ref[...], v_ref[...])

    spec = pl.BlockSpec((tr, cdim), lambda i: (i, 0))
    sd = jax.ShapeDtypeStruct((r, cdim), F32)
    return pl.pallas_call(
        body, name="adamw", grid=(r // tr,), in_specs=[spec] * 4, out_specs=(spec,) * 3, out_shape=(sd,) * 3,
        compiler_params=_params(("parallel",)),
    )(w, g, m, v)


V_G_NORM, V_DMOD, V_B_GATE, V_CONV_B, V_LAM, V_G_FINAL, V_CONV_W, V_LOSS, V_ROWS = 0, 1, 4, 6, 7, 8, 9, 13, 64
M_W_A, M_W_X, M_B_A, M_B_X, M_ROWS = 0, H * DH, 2 * H * DH, 2 * H * DH + H, 2112
SMALL = ("g_norm", "b_mod", "b_gate", "conv_b", "lam", "g_final", "conv_w", "w_a", "w_x", "b_a", "b_x")


def _adamw_small(redv, redm, g_conv, wmv):
    def grad(name, rv, rm, gc):
        if name == "g_norm":
            return rv[V_G_NORM:V_G_NORM + 1, :]
        if name == "b_mod":
            return jnp.concatenate([rv[V_DMOD + t:V_DMOD + t + 1, :] for t in range(3)], axis=1)
        if name == "b_gate":
            return jnp.concatenate([rv[V_B_GATE + t:V_B_GATE + t + 1, :] for t in range(2)], axis=1)
        if name == "conv_b":
            return rv[V_CONV_B:V_CONV_B + 1, :]
        if name == "lam":
            return rv[V_LAM:V_LAM + 1, :]
        if name == "g_final":
            return rv[V_G_FINAL:V_G_FINAL + 1, :]
        if name == "conv_w":
            return gc[...]
        if name == "w_a":
            return rm[M_W_A:M_W_A + H * DH, :]
        if name == "w_x":
            return rm[M_W_X:M_W_X + H * DH, :]
        if name == "b_a":
            return rm[M_B_A:M_B_A + H, :]
        return rm[M_B_X:M_B_X + H, :]

    n = len(SMALL)

    def body(*refs):
        rv, rm, gc = refs[:3]
        ins, outs = refs[3:3 + 3 * n], refs[3 + 3 * n:]
        for t, name in enumerate(SMALL):
            w_ref, m_ref, v_ref = ins[3 * t:3 * t + 3]
            g_out, d_out, m_out, v_out = outs[4 * t:4 * t + 4]
            g = grad(name, rv, rm, gc)
            g_out[...] = g
            d_out[...], m_out[...], v_out[...] = _adamw_update(w_ref[...], g, m_ref[...], v_ref[...])

    vm = pl.BlockSpec(memory_space=pltpu.VMEM)
    flat = [a for name in SMALL for a in wmv[name]]
    shapes = [jax.ShapeDtypeStruct(wmv[name][0].shape, F32) for name in SMALL for _ in range(4)]
    outs = pl.pallas_call(
        body, name="adamw_small", out_shape=tuple(shapes),
        in_specs=[vm] * (3 + len(flat)), out_specs=tuple([vm] * len(shapes)),
        compiler_params=_params(),
    )(redv, redm, g_conv, *flat)
    return {name: outs[4 * t:4 * t + 4] for t, name in enumerate(SMALL)}


def _rope_tables(positions):
    inv_freq = ROPE_THETA ** (-jnp.arange(0, ROT, 2, dtype=F32) / ROT)
    ang = positions.astype(F32)[:, None] * inv_freq
    cos, sin = jnp.cos(ang), jnp.sin(ang)
    n = positions.shape[0]
    half = ROT // 2
    rc = jnp.concatenate([cos, cos, jnp.ones((n, DH - ROT), F32)], axis=1)
    rsa = jnp.concatenate([-sin, jnp.zeros((n, DH - half), F32)], axis=1)
    rsb = jnp.concatenate([jnp.zeros((n, half), F32), sin, jnp.zeros((n, DH - ROT), F32)], axis=1)
    return rc, rsa, rsb


def kernel(x, c, positions, g_norm, w_mod, b_mod, w_in, b_gate, conv_w, conv_b, w_a, b_a, w_x, b_x, lam, w_out_rnn, w_out_attn, w_o, g_final, loss_target, m_g_norm, m_w_mod, m_b_mod, m_w_in, m_b_gate, m_conv_w, m_conv_b, m_w_a, m_b_a, m_w_x, m_b_x, m_lam, m_w_out_rnn, m_w_out_attn, m_w_o, m_g_final, v_g_norm, v_w_mod, v_b_mod, v_w_in, v_b_gate, v_conv_w, v_conv_b, v_w_a, v_b_a, v_w_x, v_b_x, v_lam, v_w_out_rnn, v_w_out_attn, v_w_o, v_g_final):
    s = x.shape[1]
    xi = lax.axis_index("x")
    yi = lax.axis_index("y")
    ci = lax.axis_index("c")
    shard = 2 * xi + yi
    x2d = x[0]
    tgt = loss_target[0]
    pos = positions[0]

    c_all, mod4, conv_all = _mod_fwd(c, w_mod[0], b_mod.reshape(4, 1, 768), conv_w[0])
    mod = mod4.reshape(1, 3 * D)
    shift, scale, gate = mod[:, :D], mod[:, D:2 * D], mod[:, 2 * D:]
    w3sh = jnp.concatenate([w_out_rnn[0], w_out_attn[0], w_o[0]], axis=0).astype(BF16)
    wsh = w_in[0].astype(BF16)
    conv_full = conv_all[0::2].transpose(1, 0, 2).reshape(4, D)

    order = jnp.stack([shard, shard ^ 2, shard ^ 1, shard ^ 3]).astype(jnp.int32)
    p, hbf, wg = _gather_norm_inproj(x2d, g_norm, shift, scale, wsh, order)
    wg = lax.dynamic_update_slice(wg, wsh[None], (shard, 0, 0))
    rc, rsa, rsb = _rope_tables(pos)
    pos_col = pos.reshape(s, 1)
    b_a3, b_x3 = b_a.reshape(H, 1, DH), b_x.reshape(H, 1, DH)
    hr, gr = _rnn_fwd(p, pos_col, conv_full, conv_b, w_a[0], b_a3, w_x[0], b_x3, lam)
    o, lse, ga, w3g = _attn_fwd(p, rc, rsa, rsb, w3sh)
    w3g = lax.dynamic_update_slice(w3g, w3sh[None], (shard, 0, 0))
    w3 = w3g.reshape(4, 3, 256, D).transpose(1, 0, 2, 3).reshape(3, D, D)

    dgr, dga, dc, dx2, vec_t, g_out = _tail(gr, ga, p, x2d, tgt, w3, b_gate, gate, g_final.reshape(1, D))

    dxr, dzr, g_wa, g_ba, g_wx, g_bx, g_lam, g_cw, g_cb = _rnn_bwd(
        p, hr, dgr, pos_col, conv_full, conv_b, w_a[0], b_a3, w_x[0], b_x3, lam)
    dq, dk, dv, dza = _attn_bwd(p, o, lse, dga, rc, rsa, rsb)

    pieces = [dxr, dzr, dq, dk, dv, dza, dc]
    g_win, rb_a, rb_b = _inproj_bwd_w(pieces, hbf, g_out)

    core = ci.reshape(1)
    shard1 = shard.reshape(1)
    (q_a, qh_a), (q_b, qh_b) = _add_half(g_win, rb_a, core, tr=256), _add_half(g_out, rb_b, core, tr=128)
    grad_x, vec_n, (r_a, r_b) = _inproj_bwd_x(pieces, wg, x2d, dx2, g_norm, scale, [qh_a, qh_b])
    f_a, f_b = _sum_slots(q_a, r_a, shard1, tr=256), _sum_slots(q_b, r_b, shard1, tr=128)
    s_a, s_b = _pair_swap([f_a, f_b])
    south = ci == 0
    grad_w_in = jnp.where(south, jnp.concatenate([f_a, s_a], axis=0), jnp.concatenate([s_a, f_a], axis=0))
    g3 = jnp.where(south, jnp.concatenate([f_b, s_b], axis=0), jnp.concatenate([s_b, f_b], axis=0)).reshape(3, 256, D)

    dmod_row = jnp.concatenate([vec_n[0:1], vec_n[1:2], vec_t[1:2]], axis=1)
    vpack = jnp.concatenate([
        vec_n[2:3],
        vec_n[0:2], vec_t[1:2],
        vec_t[2:4],
        g_cb.reshape(1, D),
        g_lam.reshape(1, D),
        vec_t[0:1],
        g_cw.transpose(1, 0, 2).reshape(4, D),
        vec_t[4:5],
        jnp.zeros((V_ROWS - 14, D), F32)], axis=0)
    mpack = jnp.concatenate([
        g_wa.reshape(H * DH, DH), g_wx.reshape(H * DH, DH), g_ba.reshape(H, DH), g_bx.reshape(H, DH),
        jnp.zeros((M_ROWS - 2 * H * DH - 2 * H, DH), F32)], axis=0)
    redv, redm = _allreduce_small([vpack, mpack])
    loss = redv[V_LOSS, 0]
    grad_w_mod = _mod_bwd(dmod_row.reshape(4, 1, 768), c_all)
    g_conv_sh = lax.dynamic_slice_in_dim(redv[V_CONV_W:V_CONV_W + 4], shard * 256, 256, axis=1)

    shape2d = dict(g_norm=(1, D), b_mod=(1, 3 * D), b_gate=(1, 2 * D), conv_b=(1, D), lam=(1, D), g_final=(1, D),
                   conv_w=(4, 256), w_a=(H * DH, DH), w_x=(H * DH, DH), b_a=(H, DH), b_x=(H, DH))
    given = dict(
        g_norm=(g_norm, m_g_norm, v_g_norm), b_mod=(b_mod, m_b_mod, v_b_mod), b_gate=(b_gate, m_b_gate, v_b_gate),
        conv_b=(conv_b, m_conv_b, v_conv_b), lam=(lam, m_lam, v_lam), g_final=(g_final, m_g_final, v_g_final),
        conv_w=(conv_w, m_conv_w, v_conv_w), w_a=(w_a, m_w_a, v_w_a), w_x=(w_x, m_w_x, v_w_x),
        b_a=(b_a, m_b_a, v_b_a), b_x=(b_x, m_b_x, v_b_x))
    small = _adamw_small(redv, redm, g_conv_sh,
                         {n: tuple(a.reshape(shape2d[n]) for a in given[n]) for n in SMALL})

    big_in = _adamw(w_in[0], grad_w_in, m_w_in[0], v_w_in[0], tr=256)
    big_mod = _adamw(w_mod[0], grad_w_mod, m_w_mod[0], v_w_mod[0], tr=256)
    w3f = jnp.concatenate([w_out_rnn[0], w_out_attn[0], w_o[0]], axis=0)
    m3f = jnp.concatenate([m_w_out_rnn[0], m_w_out_attn[0], m_w_o[0]], axis=0)
    v3f = jnp.concatenate([v_w_out_rnn[0], v_w_out_attn[0], v_w_o[0]], axis=0)
    big_out = _adamw(w3f, g3.reshape(768, D), m3f, v3f, tr=256)

    names = ["g_norm", "w_mod", "b_mod", "w_in", "b_gate", "conv_w", "conv_b", "w_a", "b_a", "w_x", "b_x", "lam",
             "w_out_rnn", "w_out_attn", "w_o", "g_final"]
    outs = []
    for idx in range(4):
        d = {n: small[n][idx].reshape(given[n][0].shape) for n in SMALL}
        if idx == 0:
            d.update(w_mod=grad_w_mod[None], w_in=grad_w_in[None],
                     w_out_rnn=g3[0][None], w_out_attn=g3[1][None], w_o=g3[2][None])
        else:
            d.update(w_mod=big_mod[idx - 1][None], w_in=big_in[idx - 1][None],
                     w_out_rnn=big_out[idx - 1][0:256][None], w_out_attn=big_out[idx - 1][256:512][None],
                     w_o=big_out[idx - 1][512:768][None])
        outs.append(d)
    flat = [d[n] for d in outs for n in names]
    return (loss, grad_x[None], *flat)
```

```python
import jax
import jax.numpy as jnp
from jax import lax
from jax.experimental import pallas as pl
from jax.experimental.pallas import tpu as pltpu

F32, BF16 = jnp.float32, jnp.bfloat16
MESH = pl.DeviceIdType.MESH
HIGHEST = lax.Precision.HIGHEST

D = 1024
H = 8
DH = 128
PW = 2048
EPS = 1e-6
LRU_C = 8.0
SCALE = DH ** -0.5
NEG = -1e30
SPAN = 2048
UB = 128
DILATIONS = (1, 4, 16)
UNIT_BATCH = 16
ROPE_THETA = 500000.0
ROT = 32

LR, B1, B2, ADAM_EPS, WD, STEP = 0.001, 0.9, 0.999, 1e-08, 0.01, 10

NDEV = 8


def _params(sem=None, vmem_mb=56):
    return pltpu.CompilerParams(dimension_semantics=sem, vmem_limit_bytes=vmem_mb * 2 ** 20)


def _coords():
    return lax.axis_index("x"), lax.axis_index("y"), lax.axis_index("c")


def _flip(v, bit):
    return 1 - v if bit else v


def _peer(k):
    x, y, c = _coords()
    return (_flip(x, (k >> 2) & 1), _flip(y, (k >> 1) & 1), _flip(c, k & 1))


def _my_index():
    x, y, c = _coords()
    return 4 * x + 2 * y + c


def _rcopy(src, dst, ssem, rsem, dev):
    return pltpu.make_async_remote_copy(src_ref=src, dst_ref=dst, send_sem=ssem, recv_sem=rsem,
                                        device_id=dev, device_id_type=MESH)


def _sigmoid(x):
    return jax.nn.sigmoid(x)


def _dot(a, b):
    return jnp.dot(a, b, preferred_element_type=F32)


def _dot_nt(a, b):
    return lax.dot_general(a, b, (((1,), (1,)), ((), ())), preferred_element_type=F32)


def _dot_tn(a, b):
    return lax.dot_general(a, b, (((0,), (0,)), ((), ())), preferred_element_type=F32)


def _colsum(a):
    return jnp.sum(a, axis=0, keepdims=True)


def _mod_fwd(c, w_mod_sh, b_mod4, conv_sh):
    def body(c_ref, w_ref, b_ref, cv_ref, call_ref, mod_ref, cvall_ref, rows_ref, cmat_ref, s1, r1, s2, r2, s3, r3):
        x, y, _ = _coords()
        me = _my_index()
        j = 2 * x + y
        call_ref[me] = c_ref[...]
        cvall_ref[me] = cv_ref[...]
        sends = []
        for k in range(1, NDEV):
            cp = _rcopy(call_ref.at[me], call_ref.at[me], s1.at[k - 1], r1.at[k - 1], _peer(k))
            cp.start()
            sends.append(cp)
            cp = _rcopy(cvall_ref.at[me], cvall_ref.at[me], s3.at[k - 1], r3.at[k - 1], _peer(k))
            cp.start()
            sends.append(cp)
        for k in range(1, NDEV):
            pk = me ^ k
            _rcopy(call_ref.at[pk], call_ref.at[pk], s1.at[k - 1], r1.at[k - 1], _peer(k)).wait_recv()
        for b in range(NDEV):
            cmat_ref[pl.ds(b, 1), :] = call_ref[b]
        cm = cmat_ref[...]
        act = cm * _sigmoid(cm)
        mp = jnp.dot(act, w_ref[...], preferred_element_type=F32, precision=HIGHEST) + b_ref[j]
        for b in range(NDEV):
            rows_ref[b] = mp[b:b + 1]
        mod_ref[j] = rows_ref[me]
        for q, k in enumerate((2, 4, 6)):
            cp = _rcopy(rows_ref.at[me ^ k], mod_ref.at[j], s2.at[q], r2.at[q], _peer(k))
            cp.start()
            sends.append(cp)
        for q, k in enumerate((2, 4, 6)):
            jq = j ^ (k >> 1)
            _rcopy(rows_ref.at[me], mod_ref.at[jq], s2.at[q], r2.at[q], _peer(k)).wait_recv()
        for k in range(1, NDEV):
            pk = me ^ k
            _rcopy(cvall_ref.at[pk], cvall_ref.at[pk], s3.at[k - 1], r3.at[k - 1], _peer(k)).wait_recv()
        for cp in sends:
            cp.wait_send()

    vm = pl.BlockSpec(memory_space=pltpu.VMEM)
    return pl.pallas_call(
        body, name="mod_fwd",
        out_shape=(jax.ShapeDtypeStruct((NDEV, 1, D), F32), jax.ShapeDtypeStruct((4, 1, 768), F32),
                   jax.ShapeDtypeStruct((NDEV,) + conv_sh.shape, F32)),
        in_specs=[vm, vm, vm, vm], out_specs=(vm, vm, vm),
        scratch_shapes=[pltpu.VMEM((NDEV, 1, 768), F32), pltpu.VMEM((NDEV, D), F32),
                        pltpu.SemaphoreType.DMA((7,)), pltpu.SemaphoreType.DMA((7,)),
                        pltpu.SemaphoreType.DMA((3,)), pltpu.SemaphoreType.DMA((3,)),
                        pltpu.SemaphoreType.DMA((7,)), pltpu.SemaphoreType.DMA((7,))],
        compiler_params=_params(),
    )(c, w_mod_sh, b_mod4, conv_sh)


def _mod_bwd(dmod4, c_all):
    def body(d_ref, call_ref, gw_ref, dall_ref, cmat_ref, dmat_ref, s1, r1):
        x, y, _ = _coords()
        me = _my_index()
        j = 2 * x + y
        dall_ref[me] = d_ref[...]
        sends = []
        for k in range(1, NDEV):
            cp = _rcopy(dall_ref.at[me], dall_ref.at[me], s1.at[k - 1], r1.at[k - 1], _peer(k))
            cp.start()
            sends.append(cp)
        for k in range(1, NDEV):
            pk = me ^ k
            _rcopy(dall_ref.at[pk], dall_ref.at[pk], s1.at[k - 1], r1.at[k - 1], _peer(k)).wait_recv()
        for cp in sends:
            cp.wait_send()
        for b in range(NDEV):
            cmat_ref[pl.ds(b, 1), :] = call_ref[b]
            dmat_ref[pl.ds(b, 1), :] = dall_ref[b, j]
        cm = cmat_ref[...]
        act = cm * _sigmoid(cm)
        gw_ref[...] = lax.dot_general(act, dmat_ref[...], (((0,), (0,)), ((), ())),
                                      preferred_element_type=F32, precision=HIGHEST)

    vm = pl.BlockSpec(memory_space=pltpu.VMEM)
    return pl.pallas_call(
        body, name="mod_bwd",
        out_shape=jax.ShapeDtypeStruct((D, 768), F32),
        in_specs=[vm, vm], out_specs=vm,
        scratch_shapes=[pltpu.VMEM((NDEV, 4, 1, 768), F32), pltpu.VMEM((NDEV, D), F32), pltpu.VMEM((NDEV, 768), F32),
                        pltpu.SemaphoreType.DMA((7,)), pltpu.SemaphoreType.DMA((7,))],
        compiler_params=_params(),
    )(dmod4, c_all)


def _gather_norm_inproj(x, gn, shift, scale, wsh, order, tm=1024, tn=1024):
    s = x.shape[0]
    ni = s // tm
    npc = PW // tn
    rows, cols = wsh.shape
    half = rows // 2
    nch = 4
    cr = half // nch
    chips = ((1, 0), (0, 1), (1, 1))

    def body(ord_ref, x_ref, gn_ref, sh_ref, sc_ref, wsh_hbm, p_ref, h_ref, wg_hbm, hs_all, w_s, wsem, ss, rs):
        slot, i, col = pl.program_id(0), pl.program_id(1), pl.program_id(2)
        cx, cy, cc = _coords()
        j = 2 * cx + cy
        sib = (cx, cy, 1 - cc)
        mine = lambda n: pl.ds(cc * half + n * cr, cr)
        theirs = lambda n: pl.ds((1 - cc) * half + n * cr, cr)
        shard_of = lambda q: j ^ (2 * chips[q][0] + chips[q][1])

        def to_chip(q, n):
            e = nch * q + n
            return _rcopy(wsh_hbm.at[mine(n)], wg_hbm.at[j, mine(n)], ss.at[e], rs.at[e],
                          (_flip(cx, chips[q][0]), _flip(cy, chips[q][1]), cc))

        def from_chip(q, n):
            e = nch * q + n
            return _rcopy(wsh_hbm.at[mine(n)], wg_hbm.at[shard_of(q), mine(n)], ss.at[e], rs.at[e], sib)

        def to_sibling(q, n):
            e = 3 * nch + nch * q + n
            return _rcopy(wg_hbm.at[shard_of(q), mine(n)], wg_hbm.at[shard_of(q), mine(n)], ss.at[e], rs.at[e], sib)

        def from_sibling(q, n):
            e = 3 * nch + nch * q + n
            return _rcopy(wsh_hbm.at[mine(n)], wg_hbm.at[shard_of(q), theirs(n)], ss.at[e], rs.at[e], sib)

        def load(sl, src):
            cp = pltpu.make_async_copy(src, w_s.at[sl], wsem.at[sl])
            cp.start()
            cp.wait()

        first = (i == 0) & (col == 0)

        @pl.when(first & (slot == 0))
        def _():
            for n in range(nch):
                for q in (0, 1):
                    to_chip(q, n).start()
            load(0, wsh_hbm.at[pl.ds(0, D), :])

        @pl.when(first & (slot == 1))
        def _():
            for q in (0, 1):
                for n in range(nch):
                    from_chip(q, n).wait_recv()
                    to_sibling(q, n).start()
            for n in range(nch):
                to_chip(2, n).start()
            for n in range(nch):
                from_sibling(0, n).wait_recv()
            load(1, wg_hbm.at[shard_of(0), pl.ds(0, D), :])

        @pl.when(first & (slot == 2))
        def _():
            for n in range(nch):
                from_sibling(1, n).wait_recv()
            load(2, wg_hbm.at[shard_of(1), pl.ds(0, D), :])

        @pl.when(first & (slot == 3))
        def _():
            for n in range(nch):
                from_chip(2, n).wait_recv()
                to_sibling(2, n).start()
            for n in range(nch):
                from_sibling(2, n).wait_recv()
            load(3, wg_hbm.at[shard_of(2), pl.ds(0, D), :])
            for q in range(3):
                for n in range(nch):
                    to_chip(q, n).wait_send()
                    to_sibling(q, n).wait_send()

        @pl.when((slot == 0) & (col == 0))
        def _():
            xt = x_ref[...]
            rstd = lax.rsqrt(jnp.mean(xt * xt, axis=-1, keepdims=True) + EPS)
            h = ((xt * rstd * gn_ref[...]) * (1.0 + sc_ref[...]) + sh_ref[...]).astype(BF16)
            hs_all[i] = h
            h_ref[...] = h

        p_ref[0] = _dot(hs_all[i], w_s[slot, :, pl.ds(pl.multiple_of(col * tn, tn), tn)]).astype(BF16)

    row = pl.BlockSpec((1, D), lambda sl, i, col, o: (0, 0))
    x_rows = lambda sl, i, col, o: (jnp.where(sl == 0, i, ni - 1), 0)
    any_ = pl.BlockSpec(memory_space=pl.ANY)
    return pl.pallas_call(
        body, name="gather_norm_inproj",
        grid_spec=pltpu.PrefetchScalarGridSpec(
            num_scalar_prefetch=1, grid=(4, ni, npc),
            in_specs=[pl.BlockSpec((tm, D), x_rows), row, row, row, any_],
            out_specs=(pl.BlockSpec((1, tm, tn), lambda sl, i, col, o: (o[sl], i, col)),
                       pl.BlockSpec((tm, D), x_rows), any_),
            scratch_shapes=[pltpu.VMEM((ni, tm, D), BF16), pltpu.VMEM((4, D, PW), BF16),
                            pltpu.SemaphoreType.DMA((4,)),
                            pltpu.SemaphoreType.DMA((6 * nch,)), pltpu.SemaphoreType.DMA((6 * nch,))]),
        out_shape=(jax.ShapeDtypeStruct((4, s, PW), BF16), jax.ShapeDtypeStruct((s, D), BF16),
                   jax.ShapeDtypeStruct((4, rows, cols), wsh.dtype)),
        compiler_params=_params(("arbitrary", "arbitrary", "arbitrary")),
    )(order, x, gn, shift, scale, wsh)


def _shift_down(prev8, cur, d):
    t = cur.shape[0]
    c3 = cur.reshape(t // 8, 8, DH)
    rot = pltpu.roll(c3, d, 1)
    before = jnp.concatenate([pltpu.roll(prev8, d, 0).reshape(1, 8, DH), rot[:-1]], axis=0)
    rows = lax.broadcasted_iota(jnp.int32, c3.shape, 1)
    return jnp.where(rows >= d, rot, before).reshape(t, DH)


def _shift_up(cur, next8, d):
    t = cur.shape[0]
    c3 = cur.reshape(t // 8, 8, DH)
    rot = pltpu.roll(c3, 8 - d, 1)
    after = jnp.concatenate([rot[1:], pltpu.roll(next8, 8 - d, 0).reshape(1, 8, DH)], axis=0)
    rows = lax.broadcasted_iota(jnp.int32, c3.shape, 1)
    return jnp.where(rows < 8 - d, rot, after).reshape(t, DH)


def _rnn_gates(xr, prev8, cw, cb, wa, ba, wx, bx, lam, reset):
    xc = cw[3:4] * xr + cb
    for d in (1, 2, 3):
        xc = xc + cw[3 - d:4 - d] * _shift_down(prev8, xr, d)
    xcb = xc.astype(BF16)
    r = _sigmoid(_dot(xcb, wa.astype(BF16)) + ba)
    ig = _sigmoid(_dot(xcb, wx.astype(BF16)) + bx)
    nl = -lam
    sp = jnp.maximum(nl, 0.0) + jnp.log1p(jnp.exp(-jnp.abs(nl)))
    log_a = (-LRU_C * r) * sp
    a = jnp.where(reset, 0.0, jnp.exp(log_a))
    mult = jnp.where(reset, 1.0, jnp.sqrt(1.0 - jnp.exp(2.0 * log_a)))
    return xc, r, ig, sp, a, mult


def _log_scan(a, b, axis, up):
    n = a.shape[axis]
    rows = lax.broadcasted_iota(jnp.int32, a.shape, axis)
    d = 1
    while d < n:
        m = rows < n - d if up else rows >= d
        shift = n - d if up else d
        a_s = pltpu.roll(a, shift, axis)
        b_s = pltpu.roll(b, shift, axis)
        b = jnp.where(m, a * b_s + b, b)
        a = jnp.where(m, a * a_s, a)
        d *= 2
    return a, b


def _scan(a, b, t, edge, up=False):
    g = t // 8
    a3, b3 = _log_scan(a.reshape(g, 8, DH), b.reshape(g, 8, DH), 1, up)
    last = 0 if up else 7
    ag, bg = _log_scan(a3[:, last, :], b3[:, last, :], 0, up)
    hg = ag * edge + bg
    grp = lax.broadcasted_iota(jnp.int32, hg.shape, 0)
    if up:
        cin = jnp.where(grp == g - 1, edge, pltpu.roll(hg, g - 1, 0))
        tail = hg[0:1]
    else:
        cin = jnp.where(grp == 0, edge, pltpu.roll(hg, 1, 0))
        tail = hg[g - 1:g]
    return (a3 * cin[:, None, :] + b3).reshape(t, DH), tail


def _rnn_fwd(p, pos, conv_w, conv_b, w_a, b_a, w_x, b_x, lam, tt=512):
    s = p.shape[1]
    nt = s // tt

    def body(xr_ref, z_ref, pos_ref, cw_ref, cb_ref, wa_ref, ba_ref, wx_ref, bx_ref, lam_ref,
             hr_ref, gr_ref, xprev, hprev):
        @pl.when(pl.program_id(1) == 0)
        def _():
            xprev[...] = jnp.zeros_like(xprev)
            hprev[...] = jnp.zeros_like(hprev)

        xr = xr_ref[0].astype(F32)
        z = z_ref[0].astype(F32)
        reset = pos_ref[...] == 0
        xc, r, ig, sp, a, mult = _rnn_gates(xr, xprev[...], cw_ref[...], cb_ref[...], wa_ref[0], ba_ref[0],
                                            wx_ref[0], bx_ref[0], lam_ref[...], reset)
        bx = mult * ig * xc
        h, h_last = _scan(a, bx, tt, hprev[0:1])
        xprev[...] = xr[tt - 8:]
        hprev[...] = jnp.broadcast_to(h_last, (8, DH))
        hr_ref[...] = h
        gr_ref[...] = (h * (z * _sigmoid(z))).astype(BF16)

    head_row = lambda hh, t: (0, hh)
    return pl.pallas_call(
        body, name="rnn_fwd", grid=(H, nt),
        in_specs=[pl.BlockSpec((1, tt, DH), lambda hh, t: (0, t, hh)),
                  pl.BlockSpec((1, tt, DH), lambda hh, t: (0, t, H + hh)),
                  pl.BlockSpec((tt, 1), lambda hh, t: (t, 0)),
                  pl.BlockSpec((4, DH), head_row), pl.BlockSpec((1, DH), head_row),
                  pl.BlockSpec((1, DH, DH), lambda hh, t: (hh, 0, 0)), pl.BlockSpec((1, 1, DH), lambda hh, t: (hh, 0, 0)),
                  pl.BlockSpec((1, DH, DH), lambda hh, t: (hh, 0, 0)), pl.BlockSpec((1, 1, DH), lambda hh, t: (hh, 0, 0)),
                  pl.BlockSpec((1, DH), head_row)],
        out_specs=(pl.BlockSpec((tt, DH), lambda hh, t: (t, hh)), pl.BlockSpec((tt, DH), lambda hh, t: (t, hh))),
        out_shape=(jax.ShapeDtypeStruct((s, D), F32), jax.ShapeDtypeStruct((s, D), BF16)),
        scratch_shapes=[pltpu.VMEM((8, DH), F32), pltpu.VMEM((8, DH), F32)],
        compiler_params=_params(("parallel", "arbitrary")),
    )(p, p, pos, conv_w, conv_b, w_a, b_a, w_x, b_x, lam)


def _rnn_bwd(p, hr, dgr, pos, conv_w, conv_b, w_a, b_a, w_x, b_x, lam, tt=512):
    s = p.shape[1]
    nt = s // tt
    t8 = tt // 8

    def body(xr_ref, z_ref, xp_ref, hr_ref, hp_ref, dg_ref, pos_ref, cw_ref, cb_ref, wa_ref, ba_ref, wx_ref, bx_ref,
             lam_ref, dxr_ref, dz_ref, gwa_ref, gba_ref, gwx_ref, gbx_ref, glam_ref, gcw_ref, gcb_ref,
             a_next, g_next, dxc_next):
        t = pl.program_id(1)
        has_prev = t < nt - 1

        @pl.when(t == 0)
        def _():
            a_next[...] = jnp.zeros_like(a_next)
            g_next[...] = jnp.zeros_like(g_next)
            dxc_next[...] = jnp.zeros_like(dxc_next)
            gwa_ref[...] = jnp.zeros_like(gwa_ref)
            gba_ref[...] = jnp.zeros_like(gba_ref)
            gwx_ref[...] = jnp.zeros_like(gwx_ref)
            gbx_ref[...] = jnp.zeros_like(gbx_ref)
            glam_ref[...] = jnp.zeros_like(glam_ref)
            gcw_ref[...] = jnp.zeros_like(gcw_ref)
            gcb_ref[...] = jnp.zeros_like(gcb_ref)

        xr = xr_ref[0].astype(F32)
        z = z_ref[0].astype(F32)
        hr_blk = hr_ref[...]
        dg = dg_ref[...]
        xprev = jnp.where(has_prev, xp_ref[0].astype(F32)[8:], 0.0)
        hprev8 = jnp.where(has_prev, hp_ref[...], 0.0)
        reset = pos_ref[...] == 0
        cw = cw_ref[...]
        wa = wa_ref[0]
        wx = wx_ref[0]
        lam_v = lam_ref[...]
        xc, r, ig, sp, a, mult = _rnn_gates(xr, xprev, cw, cb_ref[...], wa, ba_ref[0], wx, bx_ref[0], lam_v, reset)

        sz = _sigmoid(z)
        dh = dg * (z * sz)
        dz_ref[...] = (dg * hr_blk * (sz * (1.0 + z * (1.0 - sz)))).astype(BF16)

        an = _shift_up(a, a_next[...], 1)
        g, g_first = _scan(an, dh, tt, g_next[0:1], up=True)
        a_next[...] = jnp.broadcast_to(a[0:1], (8, DH))
        g_next[...] = jnp.broadcast_to(g_first, (8, DH))

        hm1 = _shift_down(hprev8, hr_blk, 1)
        da = g * hm1
        dmult = g * (ig * xc)
        di = g * (mult * xc)
        dxc = g * (mult * ig)
        dla = jnp.where(reset, 0.0, da * a - dmult * (a * a) / mult)
        dr = dla * (-LRU_C * sp)
        dsp = _colsum(dla * (-LRU_C * r))
        glam_ref[0] += dsp * (-_sigmoid(-lam_v))
        dpa = dr * r * (1.0 - r)
        dpx = di * ig * (1.0 - ig)
        dpab = dpa.astype(BF16)
        dpxb = dpx.astype(BF16)
        dxc = dxc + _dot_nt(dpab, wa.astype(BF16)) + _dot_nt(dpxb, wx.astype(BF16))
        xcb = xc.astype(BF16)
        gwa_ref[0] += _dot_tn(xcb, dpab)
        gwx_ref[0] += _dot_tn(xcb, dpxb)
        gba_ref[0] += _colsum(dpa)
        gbx_ref[0] += _colsum(dpx)

        dxr = cw[3:4] * dxc
        for d in (1, 2, 3):
            dxr = dxr + cw[3 - d:4 - d] * _shift_up(dxc, dxc_next[...], d)
        dxr_ref[...] = dxr.astype(BF16)
        dxc_next[...] = dxc[0:8]
        gcb_ref[0] += _colsum(dxc)
        gcw_ref[0, 3:4, :] += _colsum(xr * dxc)
        for d in (1, 2, 3):
            gcw_ref[0, 3 - d:4 - d, :] += _colsum(_shift_down(xprev, xr, d) * dxc)

    rt = lambda t: nt - 1 - t
    prev8 = lambda t: jnp.maximum(rt(t) * t8 - 1, 0)
    head_row = lambda hh, t: (0, hh)
    hsm = lambda hh, t: (hh, 0, 0)
    return pl.pallas_call(
        body, name="rnn_bwd", grid=(H, nt),
        in_specs=[pl.BlockSpec((1, tt, DH), lambda hh, t: (0, rt(t), hh)),
                  pl.BlockSpec((1, tt, DH), lambda hh, t: (0, rt(t), H + hh)),
                  pl.BlockSpec((1, 16, DH), lambda hh, t: (0, jnp.maximum(rt(t) * (tt // 16) - 1, 0), hh)),
                  pl.BlockSpec((tt, DH), lambda hh, t: (rt(t), hh)),
                  pl.BlockSpec((8, DH), lambda hh, t: (prev8(t), hh)),
                  pl.BlockSpec((tt, DH), lambda hh, t: (rt(t), hh)),
                  pl.BlockSpec((tt, 1), lambda hh, t: (rt(t), 0)),
                  pl.BlockSpec((4, DH), head_row), pl.BlockSpec((1, DH), head_row),
                  pl.BlockSpec((1, DH, DH), hsm), pl.BlockSpec((1, 1, DH), hsm),
                  pl.BlockSpec((1, DH, DH), hsm), pl.BlockSpec((1, 1, DH), hsm),
                  pl.BlockSpec((1, DH), head_row)],
        out_specs=(pl.BlockSpec((tt, DH), lambda hh, t: (rt(t), hh)), pl.BlockSpec((tt, DH), lambda hh, t: (rt(t), hh)),
                   pl.BlockSpec((1, DH, DH), hsm), pl.BlockSpec((1, 1, DH), hsm),
                   pl.BlockSpec((1, DH, DH), hsm), pl.BlockSpec((1, 1, DH), hsm),
                   pl.BlockSpec((1, 1, DH), hsm), pl.BlockSpec((1, 4, DH), hsm), pl.BlockSpec((1, 1, DH), hsm)),
        out_shape=(jax.ShapeDtypeStruct((s, D), BF16), jax.ShapeDtypeStruct((s, D), BF16),
                   jax.ShapeDtypeStruct((H, DH, DH), F32), jax.ShapeDtypeStruct((H, 1, DH), F32),
                   jax.ShapeDtypeStruct((H, DH, DH), F32), jax.ShapeDtypeStruct((H, 1, DH), F32),
                   jax.ShapeDtypeStruct((H, 1, DH), F32), jax.ShapeDtypeStruct((H, 4, DH), F32),
                   jax.ShapeDtypeStruct((H, 1, DH), F32)),
        scratch_shapes=[pltpu.VMEM((8, DH), F32), pltpu.VMEM((8, DH), F32), pltpu.VMEM((8, DH), F32)],
        compiler_params=_params(("parallel", "arbitrary")),
    )(p, p, p, hr, hr, dgr, pos, conv_w, conv_b, w_a, b_a, w_x, b_x, lam)


def _rope(t, c, sa, sb):
    return t * c + pltpu.roll(t, DH - ROT // 2, 1) * sa + pltpu.roll(t, ROT // 2, 1) * sb


def _rope_bwd(g, c, sa, sb):
    return g * c + pltpu.roll(g * sa, ROT // 2, 1) + pltpu.roll(g * sb, DH - ROT // 2, 1)


def _unit_bases(gi, u):
    dil = DILATIONS[gi]
    if dil == 1:
        return u * UB, SPAN + (u - 1) * UB, u == 0
    if dil == 4:
        blk, r = u // 4, u % 4
        return blk * 4 * UB + r, SPAN + (blk - 1) * 4 * UB + r, blk == 0
    return u, u, True


def _unit_slices(gi, u):
    dil = DILATIONS[gi]
    qb0, kb0, first = _unit_bases(gi, u)
    if dil == 1:
        return pl.ds(pl.multiple_of(qb0, UB), UB), pl.ds(pl.multiple_of(kb0, UB), 2 * UB), first
    return pl.ds(qb0, UB, stride=dil), pl.ds(kb0, 2 * UB, stride=dil), first


def _bdot(a, b):
    return lax.dot_general(a, b, (((2,), (1,)), ((0,), (0,))), preferred_element_type=F32)


def _bdot_nt(a, b):
    return lax.dot_general(a, b, (((2,), (2,)), ((0,), (0,))), preferred_element_type=F32)


def _bdot_tn(a, b):
    return lax.dot_general(a, b, (((1,), (1,)), ((0,), (0,))), preferred_element_type=F32)


def _band_mask(first_in_span, has_prev):
    qi = lax.broadcasted_iota(jnp.int32, (UB, 2 * UB), 0)
    ki = lax.broadcasted_iota(jnp.int32, (UB, 2 * UB), 1)
    dist = UB + qi - ki
    band = (dist >= 0) & (dist <= UB)
    return band & ((ki >= UB) | jnp.logical_not(first_in_span) | has_prev)


def _gather_halves(phase, src_hbm, dst_hbm, ss, rs):
    half = src_hbm.shape[0] // 2
    cx, cy, cc = _coords()
    j = 2 * cx + cy
    sib = (cx, cy, 1 - cc)
    mine = pl.ds(cc * half, half)
    theirs = pl.ds((1 - cc) * half, half)
    chips = ((1, 0), (0, 1), (1, 1))
    for q, (kx, ky) in enumerate(chips):
        jq = j ^ (2 * kx + ky)
        out = _rcopy(src_hbm.at[mine], dst_hbm.at[j, mine], ss.at[q], rs.at[q], (_flip(cx, kx), _flip(cy, ky), cc))
        landed = _rcopy(src_hbm.at[mine], dst_hbm.at[jq, mine], ss.at[q], rs.at[q], sib)
        onward = _rcopy(dst_hbm.at[jq, mine], dst_hbm.at[jq, mine], ss.at[3 + q], rs.at[3 + q], sib)
        from_sib = _rcopy(src_hbm.at[mine], dst_hbm.at[jq, theirs], ss.at[3 + q], rs.at[3 + q], sib)
        if phase == 0:
            out.start()
        elif phase == 1:
            landed.wait_recv()
            onward.start()
        else:
            from_sib.wait_recv()
            out.wait_send()
            onward.wait_send()


def _attn_fwd(p, rc, rsa, rsb, w3sh):
    s = p.shape[1]
    ns = s // SPAN
    nunit = SPAN // UB

    def body(q_ref, k_ref, v_ref, z_ref, c_ref, sa_ref, sb_ref, w3_hbm, o_ref, lse_ref, ga_ref, w3g_hbm,
             qr, kf, vf, acc, mm, ll, ss, rs):
        hh, n = pl.program_id(0), pl.program_id(1)
        for phase, at_head, at_span in ((0, 0, 0), (1, H // 2, 0), (2, H - 1, ns - 1)):
            @pl.when((hh == at_head) & (n == at_span))
            def _(phase=phase):
                _gather_halves(phase, w3_hbm, w3g_hbm, ss, rs)

        @pl.when(n == 0)
        def _():
            kf[0:SPAN] = jnp.zeros((SPAN, DH), F32)
            vf[0:SPAN] = jnp.zeros((SPAN, DH), F32)

        c, sa, sb = c_ref[...], sa_ref[...], sb_ref[...]
        qr[...] = _rope(q_ref[0].astype(F32), c, sa, sb)
        kf[SPAN:] = _rope(k_ref[0].astype(F32), c, sa, sb)
        vf[SPAN:] = v_ref[0].astype(F32)
        has_prev = n > 0

        for gi, dil in enumerate(DILATIONS):
            def trip(t, carry, gi=gi, dil=dil):
                qsls, ksls, firsts = [], [], []
                for b in range(UNIT_BATCH):
                    qsl, ksl, first = _unit_slices(gi, t * UNIT_BATCH + b)
                    qsls.append(qsl)
                    ksls.append(ksl)
                    firsts.append(first)
                qb = jnp.stack([qr[qsl, :].astype(BF16) for qsl in qsls])
                kb = jnp.stack([kf[ksl, :].astype(BF16) for ksl in ksls])
                vb = jnp.stack([vf[ksl, :].astype(BF16) for ksl in ksls])
                s_all = _bdot_nt(qb, kb)
                prs = []
                for b in range(UNIT_BATCH):
                    sc = jnp.where(_band_mask(firsts[b], has_prev), s_all[b] * SCALE, NEG)
                    m = jnp.max(sc, axis=-1, keepdims=True)
                    pr = jnp.exp(sc - m)
                    l = jnp.sum(pr, axis=-1, keepdims=True)
                    mm[gi, qsls[b], :] = jnp.broadcast_to(m, (UB, DH))
                    ll[gi, qsls[b], :] = jnp.broadcast_to(l, (UB, DH))
                    prs.append(pr.astype(BF16))
                o_all = _bdot(jnp.stack(prs), vb)
                for b in range(UNIT_BATCH):
                    acc[gi, qsls[b], :] = o_all[b]
                return carry

            lax.fori_loop(0, nunit // UNIT_BATCH, trip, 0)

        m_all =jnp.maximum(jnp.maximum(mm[0], mm[1]), mm[2])
        num = jnp.zeros((SPAN, DH), F32)
        den = jnp.zeros((SPAN, DH), F32)
        for gi in range(3):
            w = jnp.exp(mm[gi] - m_all)
            num = num + w * acc[gi]
            den = den + w * ll[gi]
        o = num / den
        o_ref[...] = o
        lse_ref[...] = m_all + jnp.log(den)
        z = z_ref[0].astype(F32)
        ga_ref[...] = (o * (z * _sigmoid(z))).astype(BF16)
        kf[0:SPAN] = kf[SPAN:]
        vf[0:SPAN] = vf[SPAN:]

    blk = lambda piece, off: pl.BlockSpec((1, SPAN, DH), lambda hh, n: (piece, n, off + hh))
    tab = pl.BlockSpec((SPAN, DH), lambda hh, n: (n, 0))
    outb = pl.BlockSpec((SPAN, DH), lambda hh, n: (n, hh))
    any_ = pl.BlockSpec(memory_space=pl.ANY)
    return pl.pallas_call(
        body, name="attn_fwd", grid=(H, ns),
        in_specs=[blk(1, 0), blk(1, H), blk(2, 0), blk(2, H), tab, tab, tab, any_],
        out_specs=(outb, outb, outb, any_),
        out_shape=(jax.ShapeDtypeStruct((s, D), F32), jax.ShapeDtypeStruct((s, D), F32),
                   jax.ShapeDtypeStruct((s, D), BF16), jax.ShapeDtypeStruct((4,) + w3sh.shape, w3sh.dtype)),
        scratch_shapes=[pltpu.VMEM((SPAN, DH), F32), pltpu.VMEM((2 * SPAN, DH), F32), pltpu.VMEM((2 * SPAN, DH), F32),
                        pltpu.VMEM((3, SPAN, DH), F32), pltpu.VMEM((3, SPAN, DH), F32), pltpu.VMEM((3, SPAN, DH), F32),
                        pltpu.SemaphoreType.DMA((6,)), pltpu.SemaphoreType.DMA((6,))],
        compiler_params=_params(("arbitrary", "arbitrary")),
    )(p, p, p, p, rc, rsa, rsb, w3sh)


def _attn_bwd(p, o, lse, dga, rc, rsa, rsb):
    s = p.shape[1]
    ns = s // SPAN
    nunit = SPAN // UB

    def body(q_ref, k_ref, kp_ref, v_ref, vp_ref, z_ref, c_ref, sa_ref, sb_ref, cp_ref, sap_ref, sbp_ref,
             o_ref, lse_ref, dg_ref, dq_ref, dk_ref, dv_ref, dz_ref,
             qr, kf, vf, dof, dlt, dqa, dkf, dvf):
        step = pl.program_id(1)
        n = ns - 1 - step
        has_prev = n > 0

        @pl.when(step == 0)
        def _():
            dkf[...] = jnp.zeros_like(dkf)
            dvf[...] = jnp.zeros_like(dvf)

        @pl.when(step > 0)
        def _():
            dkf[SPAN:] = dkf[0:SPAN]
            dvf[SPAN:] = dvf[0:SPAN]
            dkf[0:SPAN] = jnp.zeros((SPAN, DH), F32)
            dvf[0:SPAN] = jnp.zeros((SPAN, DH), F32)

        c, sa, sb = c_ref[...], sa_ref[...], sb_ref[...]
        qr[...] = _rope(q_ref[0].astype(F32), c, sa, sb)
        kf[SPAN:] = _rope(k_ref[0].astype(F32), c, sa, sb)
        vf[SPAN:] = v_ref[0].astype(F32)
        kf[0:SPAN] = jnp.where(has_prev, _rope(kp_ref[0].astype(F32), cp_ref[...], sap_ref[...], sbp_ref[...]), 0.0)
        vf[0:SPAN] = jnp.where(has_prev, vp_ref[0].astype(F32), 0.0)
        z = z_ref[0].astype(F32)
        sz = _sigmoid(z)
        dg = dg_ref[...]
        ov = o_ref[...]
        do = dg * (z * sz)
        dz_ref[...] = (dg * ov * (sz * (1.0 + z * (1.0 - sz)))).astype(BF16)
        dof[...] = do
        dlt[...] = jnp.broadcast_to(jnp.sum(do * ov, axis=-1, keepdims=True), (SPAN, DH))
        dqa[...] = jnp.zeros_like(dqa)

        for gi, dil in enumerate(DILATIONS):
            def trip(t, carry, gi=gi, dil=dil):
                qsls, ksls, firsts = [], [], []
                for b in range(UNIT_BATCH):
                    qsl, ksl, first = _unit_slices(gi, t * UNIT_BATCH + b)
                    qsls.append(qsl)
                    ksls.append(ksl)
                    firsts.append(first)
                qb = jnp.stack([qr[qsl, :].astype(BF16) for qsl in qsls])
                kb = jnp.stack([kf[ksl, :].astype(BF16) for ksl in ksls])
                vb = jnp.stack([vf[ksl, :].astype(BF16) for ksl in ksls])
                dob = jnp.stack([dof[qsl, :].astype(BF16) for qsl in qsls])
                s_all = _bdot_nt(qb, kb)
                dp_all = _bdot_nt(dob, vb)
                prs, dss = [], []
                for b in range(UNIT_BATCH):
                    lse_b = lse_ref[qsls[b], :]
                    dl_b = dlt[qsls[b], :]
                    pr = jnp.exp(s_all[b] * SCALE - jnp.concatenate([lse_b, lse_b], axis=1))
                    pr = jnp.where(_band_mask(firsts[b], has_prev), pr, 0.0)
                    prs.append(pr.astype(BF16))
                    dss.append((pr * (dp_all[b] - jnp.concatenate([dl_b, dl_b], axis=1)) * SCALE).astype(BF16))
                ds_all = jnp.stack(dss)
                dv_all = _bdot_tn(jnp.stack(prs), dob)
                dq_all = _bdot(ds_all, kb)
                dk_all = _bdot_tn(ds_all, qb)
                for b in range(UNIT_BATCH):
                    dvf[ksls[b], :] += dv_all[b]
                    dqa[qsls[b], :] += dq_all[b]
                    dkf[ksls[b], :] += dk_all[b]
                return carry

            lax.fori_loop(0, nunit // UNIT_BATCH, trip, 0)

        dq_ref[...] = _rope_bwd(dqa[...], c, sa, sb).astype(BF16)
        dk_ref[...] = _rope_bwd(dkf[SPAN:], c, sa, sb).astype(BF16)
        dv_ref[...] = dvf[SPAN:].astype(BF16)

    rn = lambda n: ns - 1 - n
    pn = lambda n: jnp.maximum(ns - 2 - n, 0)
    blk = lambda piece, off: pl.BlockSpec((1, SPAN, DH), lambda hh, n: (piece, rn(n), off + hh))
    blkp = lambda piece, off: pl.BlockSpec((1, SPAN, DH), lambda hh, n: (piece, pn(n), off + hh))
    tab = pl.BlockSpec((SPAN, DH), lambda hh, n: (rn(n), 0))
    tabp = pl.BlockSpec((SPAN, DH), lambda hh, n: (pn(n), 0))
    io = pl.BlockSpec((SPAN, DH), lambda hh, n: (rn(n), hh))
    return pl.pallas_call(
        body, name="attn_bwd", grid=(H, ns),
        in_specs=[blk(1, 0), blk(1, H), blkp(1, H), blk(2, 0), blkp(2, 0), blk(2, H),
                  tab, tab, tab, tabp, tabp, tabp, io, io, io],
        out_specs=(io, io, io, io),
        out_shape=tuple(jax.ShapeDtypeStruct((s, D), BF16) for _ in range(4)),
        scratch_shapes=[pltpu.VMEM((SPAN, DH), F32), pltpu.VMEM((2 * SPAN, DH), F32), pltpu.VMEM((2 * SPAN, DH), F32),
                        pltpu.VMEM((SPAN, DH), F32), pltpu.VMEM((SPAN, DH), F32), pltpu.VMEM((SPAN, DH), F32),
                        pltpu.VMEM((2 * SPAN, DH), F32), pltpu.VMEM((2 * SPAN, DH), F32)],
        compiler_params=_params(("parallel", "arbitrary")),
    )(p, p, p, p, p, p, rc, rsa, rsb, rc, rsa, rsb, o, lse, dga)


def _tail(gr, ga, p, x, tgt, w3, b_gate, gate, g_final, tm=256):
    s = x.shape[0]
    nt = s // tm

    def body(gr_ref, ga_ref, pr_ref, pa_ref, x_ref, t_ref, bg_ref, gate_ref, gf_ref, w_hbm,
             dgr_ref, dga_ref, dc_ref, dx2_ref, vec_ref, go_hbm, w_s, acc_s, sem):
        i = pl.program_id(0)

        @pl.when(i == 0)
        def _():
            cp = pltpu.make_async_copy(w_hbm, w_s, sem.at[12])
            cp.start()
            acc_s[...] = jnp.zeros_like(acc_s)
            vec_ref[...] = jnp.zeros_like(vec_ref)
            cp.wait()

        grb = gr_ref[...]
        gab = ga_ref[...]
        bg = bg_ref[...]
        gate_v = gate_ref[...]
        gf = gf_ref[...]
        y_r = _dot(grb, w_s[0])
        y_a = _dot(gab, w_s[1])
        sr = _sigmoid(pr_ref[0].astype(F32) + bg[:, :D])
        sa = _sigmoid(pa_ref[0].astype(F32) + bg[:, D:])
        mb = (sr * y_r + sa * y_a).astype(BF16)
        u = _dot(mb, w_s[2])
        x2 = x_ref[...] + gate_v * u
        rstd = lax.rsqrt(jnp.mean(x2 * x2, axis=-1, keepdims=True) + EPS)
        xh = x2 * rstd
        e = xh * gf - t_ref[...]
        dy = e * (1.0 / D)
        dyg = dy * gf
        dx2 = rstd * (dyg - xh * jnp.mean(dyg * xh, axis=-1, keepdims=True))
        dx2_ref[...] = dx2
        dub = (dx2 * gate_v).astype(BF16)
        dm = _dot_nt(dub, w_s[2])
        dyr = (dm * sr).astype(BF16)
        dya = (dm * sa).astype(BF16)
        dpr = dm * y_r * (sr * (1.0 - sr))
        dpa = dm * y_a * (sa * (1.0 - sa))
        dc_ref[:, :D] = dpr.astype(BF16)
        dc_ref[:, D:] = dpa.astype(BF16)
        dgr_ref[...] = _dot_nt(dyr, w_s[0])
        dga_ref[...] = _dot_nt(dya, w_s[1])
        acc_s[0] += _dot_tn(grb, dyr)
        acc_s[1] += _dot_tn(gab, dya)
        acc_s[2] += _dot_tn(mb, dub)
        vec_ref[0:1, :] += _colsum(dy * xh)
        vec_ref[1:2, :] += _colsum(dx2 * u)
        vec_ref[2:3, :] += _colsum(dpr)
        vec_ref[3:4, :] += _colsum(dpa)
        vec_ref[4:5, :] += _colsum(e * e)

        @pl.when(i == nt - 1)
        def _():
            vec_ref[4:5, :] = jnp.broadcast_to(jnp.sum(vec_ref[4:5, :]) * (0.5 / D), (1, D))
            cps = []
            for w in range(3):
                for j in range(4):
                    cps.append(pltpu.make_async_copy(acc_s.at[w, pl.ds(256 * j, 256)],
                                                     go_hbm.at[j, pl.ds(256 * w, 256)], sem.at[4 * w + j]))
            for cp in cps:
                cp.start()
            for cp in cps:
                cp.wait()

    rowt = lambda i: (i, 0)
    row = lambda w: pl.BlockSpec((1, w), lambda i: (0, 0))
    any_ = pl.BlockSpec(memory_space=pl.ANY)
    return pl.pallas_call(
        body, name="tail", grid=(nt,),
        in_specs=[pl.BlockSpec((tm, D), rowt), pl.BlockSpec((tm, D), rowt),
                  pl.BlockSpec((1, tm, D), lambda i: (3, i, 0)), pl.BlockSpec((1, tm, D), lambda i: (3, i, 1)),
                  pl.BlockSpec((tm, D), rowt), pl.BlockSpec((tm, D), rowt),
                  row(2 * D), row(D), row(D), any_],
        out_specs=(pl.BlockSpec((tm, D), rowt), pl.BlockSpec((tm, D), rowt), pl.BlockSpec((tm, 2 * D), rowt),
                   pl.BlockSpec((tm, D), rowt), pl.BlockSpec((8, D), lambda i: (0, 0)), any_),
        out_shape=(jax.ShapeDtypeStruct((s, D), F32), jax.ShapeDtypeStruct((s, D), F32),
                   jax.ShapeDtypeStruct((s, 2 * D), BF16), jax.ShapeDtypeStruct((s, D), F32),
                   jax.ShapeDtypeStruct((8, D), F32), jax.ShapeDtypeStruct((4, 768, D), F32)),
        scratch_shapes=[pltpu.VMEM((3, D, D), BF16), pltpu.VMEM((3, D, D), F32), pltpu.SemaphoreType.DMA((13,))],
        compiler_params=_params(("arbitrary",)),
    )(gr, ga, p, p, x, tgt, b_gate, gate, g_final, w3)


def _pieces_steps(pieces):
    out, s0 = [], 0
    for a in pieces:
        n = a.shape[1] // D
        out.append((s0, n))
        s0 += n
    return out, s0


def _inproj_bwd_x(pieces, wg, x, dx2, gn, scale, sums, tm=512):
    s = x.shape[0]
    np_ = len(pieces)
    na = len(sums)
    ni = s // tm
    groups, cur, width = [], [], 0
    for t, a in enumerate(pieces):
        cur.append(t)
        width += a.shape[1]
        if width == PW:
            groups.append(cur)
            cur, width = [], 0
    assert len(groups) == 4 and not cur

    def body(*refs):
        d_refs = refs[:np_]
        w_hbm, x_ref, dx2_ref, gn_ref, sc_ref = refs[np_:np_ + 5]
        q_refs = refs[np_ + 5:np_ + 5 + na]
        gx_ref, vec_ref = refs[np_ + 5 + na:np_ + 7 + na]
        r_refs = refs[np_ + 7 + na:np_ + 7 + 2 * na]
        w_s, wsem, ss, rs = refs[np_ + 7 + 2 * na:]
        i = pl.program_id(0)

        def scatter_copies():
            cx, cy, cc = _coords()
            j = 2 * cx + cy
            cps = []
            for t, (q, r) in enumerate(zip(q_refs, r_refs)):
                for e, (kx, ky) in enumerate(((1, 0), (0, 1), (1, 1))):
                    cps.append(_rcopy(q.at[j ^ (2 * kx + ky)], r.at[e], ss.at[3 * t + e], rs.at[3 * t + e],
                                      (_flip(cx, kx), _flip(cy, ky), cc)))
            return cps

        def w_copy(pc):
            return pltpu.make_async_copy(w_hbm.at[pc, pl.ds(0, D), :], w_s.at[pc], wsem.at[pc])

        @pl.when(i == 0)
        def _():
            for pc in range(4):
                w_copy(pc).start()
            vec_ref[...] = jnp.zeros_like(vec_ref)
            for cp in scatter_copies():
                cp.start()

        dh = None
        for pc, group in enumerate(groups):
            @pl.when(i == 0)
            def _(pc=pc):
                w_copy(pc).wait()

            tiles = [d_refs[t][...] for t in group]
            lhs = tiles[0] if len(tiles) == 1 else jnp.concatenate(tiles, axis=1)
            part = _dot_nt(lhs, w_s[pc])
            dh = part if dh is None else dh + part

        xt = x_ref[...]
        rstd = lax.rsqrt(jnp.mean(xt * xt, axis=-1, keepdims=True) + EPS)
        xh = xt * rstd
        gn_v = gn_ref[...]
        sc1 = 1.0 + sc_ref[...]
        dhx = dh * xh
        vec_ref[0:1, :] += _colsum(dh)
        vec_ref[1:2, :] += _colsum(dhx) * gn_v
        vec_ref[2:3, :] += _colsum(dhx) * sc1
        dxh = dh * (gn_v * sc1)
        gx_ref[...] = rstd * (dxh - xh * jnp.mean(dxh * xh, axis=-1, keepdims=True)) + dx2_ref[...]

        @pl.when(i == ni - 1)
        def _():
            for cp in scatter_copies():
                cp.wait()

    rowt = lambda i: (i, 0)
    row = pl.BlockSpec((1, D), lambda i: (0, 0))
    any_ = pl.BlockSpec(memory_space=pl.ANY)
    outs = pl.pallas_call(
        body, name="inproj_bwd_x", grid=(ni,),
        in_specs=[pl.BlockSpec((tm, a.shape[1]), rowt) for a in pieces] +
                 [any_, pl.BlockSpec((tm, D), rowt), pl.BlockSpec((tm, D), rowt), row, row] + [any_] * na,
        out_specs=(pl.BlockSpec((tm, D), rowt), pl.BlockSpec((8, D), lambda i: (0, 0))) + (any_,) * na,
        out_shape=(jax.ShapeDtypeStruct((s, D), F32), jax.ShapeDtypeStruct((8, D), F32)) +
                  tuple(jax.ShapeDtypeStruct((3,) + q.shape[1:], q.dtype) for q in sums),
        scratch_shapes=[pltpu.VMEM((4, D, PW), BF16), pltpu.SemaphoreType.DMA((4,)),
                        pltpu.SemaphoreType.DMA((3 * na,)), pltpu.SemaphoreType.DMA((3 * na,))],
        compiler_params=_params(("arbitrary",)),
    )(*pieces, wg, x, dx2, gn, scale, *sums)
    return outs[0], outs[1], outs[2:]


def _inproj_bwd_w(pieces, hbf, g_out, tk=1024):
    s = hbf.shape[0]
    steps, nk = _pieces_steps(pieces)
    npc = PW // D
    ns = s // tk
    np_ = len(pieces)
    hr = D // 2
    ohr = g_out.shape[1] // 2
    ocr = _chunk_rows(g_out)
    ochunks = [(j, r0) for j in range(g_out.shape[0]) for r0 in range(0, ohr, ocr)]
    noc = len(ochunks)

    def body(*refs):
        d_refs = refs[:np_]
        h_ref, go_hbm, g_ref, rb_hbm, rbo_hbm, stage, ss, rs, oss, ors = refs[np_:]
        cb, k = pl.program_id(0), pl.program_id(1)
        cx, cy, cc = _coords()
        sib = (cx, cy, 1 - cc)

        def block_copy(b):
            return _rcopy(stage.at[b % 2],
                          rb_hbm.at[b // npc, :, pl.ds(pl.multiple_of((b % npc) * D, D), D)], ss.at[b], rs.at[b], sib)

        def out_copy(e):
            j, r0 = ochunks[e]
            return _rcopy(go_hbm.at[j, pl.ds((1 - cc) * ohr + r0, ocr), :], rbo_hbm.at[j, pl.ds(r0, ocr), :],
                          oss.at[e], ors.at[e], sib)

        @pl.when((cb == 0) & (k == 0))
        def _():
            for e in range(noc):
                out_copy(e).start()

        @pl.when(k == 0)
        def _():
            g_ref[...] = jnp.zeros_like(g_ref)

        for (s0, n), d_ref in zip(steps, d_refs):
            @pl.when((cb >= s0) & (cb < s0 + n))
            def _(d_ref=d_ref):
                g_ref[0] += _dot_tn(h_ref[...], d_ref[...])

        @pl.when((k == ns - 1) & (cb > 1))
        def _():
            block_copy(cb - 2).wait_send()

        @pl.when(k == ns - 1)
        def _():
            stage[cb % 2] = g_ref[0, pl.ds(pl.multiple_of((1 - cc) * hr, hr), hr), :]
            block_copy(cb).start()

        @pl.when((k == ns - 1) & (cb == nk - 1))
        def _():
            block_copy(nk - 2).wait_send()
            block_copy(nk - 1).wait_send()
            for b in range(nk):
                block_copy(b).wait_recv()
            for e in range(noc):
                out_copy(e).wait_recv()
                out_copy(e).wait_send()

    def piece_spec(s0, n):
        def imap(cb, k):
            active = (cb >= s0) & (cb < s0 + n)
            return (jnp.where(active, k, 0), jnp.clip(cb - s0, 0, n - 1))
        return pl.BlockSpec((tk, D), imap)

    any_ = pl.BlockSpec(memory_space=pl.ANY)
    return pl.pallas_call(
        body, name="inproj_bwd_w", grid=(nk, ns),
        in_specs=[piece_spec(s0, n) for s0, n in steps] + [pl.BlockSpec((tk, D), lambda cb, k: (k, 0)), any_],
        out_specs=(pl.BlockSpec((1, D, D), lambda cb, k: (cb // npc, 0, cb % npc)), any_, any_),
        out_shape=(jax.ShapeDtypeStruct((4, D, PW), F32), jax.ShapeDtypeStruct((4, hr, PW), F32),
                   jax.ShapeDtypeStruct((g_out.shape[0], ohr, g_out.shape[2]), F32)),
        scratch_shapes=[pltpu.VMEM((2, hr, D), F32), pltpu.SemaphoreType.DMA((nk,)), pltpu.SemaphoreType.DMA((nk,)),
                        pltpu.SemaphoreType.DMA((noc,)), pltpu.SemaphoreType.DMA((noc,))],
        compiler_params=_params(("arbitrary", "arbitrary")),
    )(*pieces, hbf, g_out)


D2D_CHUNK_BYTES = 512 * 1024


def _chunk_rows(a):
    return max(8, D2D_CHUNK_BYTES // (a.shape[-1] * a.dtype.itemsize))


def _pair_swap(arrs):
    na = len(arrs)
    chunks = []
    for t, a in enumerate(arrs):
        cr = _chunk_rows(a)
        chunks += [(t, r0, cr) for r0 in range(0, a.shape[0], cr)]
    nch = len(chunks)

    def body(*refs):
        a_refs = refs[:na]
        o_refs = refs[na:2 * na]
        ss, rs = refs[2 * na:]
        x, y, c = _coords()
        sib = (x, y, 1 - c)
        rcs = []
        for n, (t, r0, cr) in enumerate(chunks):
            rows = pl.ds(r0, cr)
            rc = _rcopy(a_refs[t].at[rows, :], o_refs[t].at[rows, :], ss.at[n], rs.at[n], sib)
            rc.start()
            rcs.append(rc)
        for rc in rcs:
            rc.wait_recv()
        for rc in rcs:
            rc.wait_send()

    any_ = pl.BlockSpec(memory_space=pl.ANY)
    return pl.pallas_call(
        body, name="pair_swap",
        out_shape=tuple(jax.ShapeDtypeStruct(a.shape, a.dtype) for a in arrs),
        in_specs=[any_] * na, out_specs=tuple([any_] * na),
        scratch_shapes=[pltpu.SemaphoreType.DMA((nch,)), pltpu.SemaphoreType.DMA((nch,))],
        compiler_params=_params(),
    )(*arrs)


def _add_half(full, rb, core, tr):
    n, r, cdim = full.shape
    nb = r // 2 // tr

    def body(c_ref, a_ref, b_ref, o_ref, ob_ref):
        tot = a_ref[...] + b_ref[...]
        o_ref[...] = tot
        ob_ref[...] = tot.astype(BF16)

    mine = pl.BlockSpec((1, tr, cdim), lambda i, j, c_ref: (i, c_ref[0] * nb + j, 0))
    spec = pl.BlockSpec((1, tr, cdim), lambda i, j, c_ref: (i, j, 0))
    return pl.pallas_call(
        body, name="add_half",
        grid_spec=pltpu.PrefetchScalarGridSpec(num_scalar_prefetch=1, grid=(n, nb), in_specs=[mine, spec],
                                               out_specs=(spec, spec)),
        out_shape=(jax.ShapeDtypeStruct(rb.shape, rb.dtype), jax.ShapeDtypeStruct(rb.shape, BF16)),
        compiler_params=_params(("parallel", "parallel")),
    )(core, full, rb)


def _sum_slots(q, r3, shard, tr):
    _, r, cdim = q.shape

    def body(j_ref, q_ref, r_ref, o_ref):
        o_ref[...] = ((q_ref[0] + r_ref[0].astype(F32)) + r_ref[1].astype(F32)) + r_ref[2].astype(F32)

    return pl.pallas_call(
        body, name="sum_slots",
        grid_spec=pltpu.PrefetchScalarGridSpec(
            num_scalar_prefetch=1, grid=(r // tr,),
            in_specs=[pl.BlockSpec((1, tr, cdim), lambda i, j_ref: (j_ref[0], i, 0)),
                      pl.BlockSpec((3, tr, cdim), lambda i, j_ref: (0, i, 0))],
            out_specs=pl.BlockSpec((tr, cdim), lambda i, j_ref: (i, 0))),
        out_shape=jax.ShapeDtypeStruct((r, cdim), q.dtype),
        compiler_params=_params(("parallel",)),
    )(shard, q, r3)


def _allreduce_small(packs):
    na = len(packs)

    def body(*refs):
        p_refs, o_refs, rbufs = refs[:na], refs[na:2 * na], refs[2 * na:3 * na]
        s1, r1, s2, r2 = refs[3 * na:]
        me = _my_index()
        sends = []

        def chunk(t, d):
            ch = p_refs[t].shape[0] // NDEV
            return pl.ds(pl.multiple_of(d * ch, 8), ch)

        for t in range(na):
            for k in range(1, NDEV):
                e = 7 * t + k - 1
                cp = _rcopy(p_refs[t].at[chunk(t, me ^ k)], rbufs[t].at[me], s1.at[e], r1.at[e], _peer(k))
                cp.start()
                sends.append(cp)
            rbufs[t][me] = p_refs[t][chunk(t, me), :]
        for t in range(na):
            for k in range(1, NDEV):
                e = 7 * t + k - 1
                _rcopy(p_refs[t].at[chunk(t, me)], rbufs[t].at[me ^ k], s1.at[e], r1.at[e], _peer(k)).wait_recv()
            tot = rbufs[t][0]
            for d in range(1, NDEV):
                tot = tot + rbufs[t][d]
            o_refs[t][chunk(t, me), :] = tot
            for k in range(1, NDEV):
                e = 7 * t + k - 1
                cp = _rcopy(o_refs[t].at[chunk(t, me)], o_refs[t].at[chunk(t, me)], s2.at[e], r2.at[e], _peer(k))
                cp.start()
                sends.append(cp)
        for t in range(na):
            for k in range(1, NDEV):
                e = 7 * t + k - 1
                _rcopy(o_refs[t].at[chunk(t, me)], o_refs[t].at[chunk(t, me ^ k)], s2.at[e], r2.at[e],
                       _peer(k)).wait_recv()
        for cp in sends:
            cp.wait_send()

    vm = pl.BlockSpec(memory_space=pltpu.VMEM)
    return pl.pallas_call(
        body, name="allreduce_small",
        out_shape=tuple(jax.ShapeDtypeStruct(a.shape, F32) for a in packs),
        in_specs=[vm] * na, out_specs=tuple([vm] * na),
        scratch_shapes=[pltpu.VMEM((NDEV, a.shape[0] // NDEV, a.shape[1]), F32) for a in packs] +
                       [pltpu.SemaphoreType.DMA((7 * na,)) for _ in range(4)],
        compiler_params=_params(),
    )(*packs)


def _adamw_update(w, g, m, v):
    nm = B1 * m + (1.0 - B1) * g
    nv = B2 * v + (1.0 - B2) * (g * g)
    m_hat = nm / (1.0 - B1 ** STEP)
    v_hat = nv / (1.0 - B2 ** STEP)
    return -LR * (m_hat / (jnp.sqrt(v_hat) + ADAM_EPS) + WD * w), nm, nv


def _adamw(w, g, m, v, tr):
    r, cdim = w.shape

    def body(w_ref, g_ref, m_ref, v_ref, d_ref, nm_ref, nv_ref):
        d_ref[...], nm_ref[...], nv_ref[...] = _adamw_update(w_ref[...], g_ref[...], m_ref[...], v_ref[...])

    spec = pl.BlockSpec((tr, cdim), lambda i: (i, 0))
    sd = jax.ShapeDtypeStruct((r, cdim), F32)
    return pl.pallas_call(
        body, name="adamw", grid=(r // tr,), in_specs=[spec] * 4, out_specs=(spec,) * 3, out_shape=(sd,) * 3,
        compiler_params=_params(("parallel",)),
    )(w, g, m, v)


V_G_NORM, V_DMOD, V_B_GATE, V_CONV_B, V_LAM, V_G_FINAL, V_CONV_W, V_LOSS, V_ROWS = 0, 1, 4, 6, 7, 8, 9, 13, 64
M_W_A, M_W_X, M_B_A, M_B_X, M_ROWS = 0, H * DH, 2 * H * DH, 2 * H * DH + H, 2112
SMALL = ("g_norm", "b_mod", "b_gate", "conv_b", "lam", "g_final", "conv_w", "w_a", "w_x", "b_a", "b_x")


def _adamw_small(redv, redm, g_conv, wmv):
    def grad(name, rv, rm, gc):
        if name == "g_norm":
            return rv[V_G_NORM:V_G_NORM + 1, :]
        if name == "b_mod":
            return jnp.concatenate([rv[V_DMOD + t:V_DMOD + t + 1, :] for t in range(3)], axis=1)
        if name == "b_gate":
            return jnp.concatenate([rv[V_B_GATE + t:V_B_GATE + t + 1, :] for t in range(2)], axis=1)
        if name == "conv_b":
            return rv[V_CONV_B:V_CONV_B + 1, :]
        if name == "lam":
            return rv[V_LAM:V_LAM + 1, :]
        if name == "g_final":
            return rv[V_G_FINAL:V_G_FINAL + 1, :]
        if name == "conv_w":
            return gc[...]
        if name == "w_a":
            return rm[M_W_A:M_W_A + H * DH, :]
        if name == "w_x":
            return rm[M_W_X:M_W_X + H * DH, :]
        if name == "b_a":
            return rm[M_B_A:M_B_A + H, :]
        return rm[M_B_X:M_B_X + H, :]

    n = len(SMALL)

    def body(*refs):
        rv, rm, gc = refs[:3]
        ins, outs = refs[3:3 + 3 * n], refs[3 + 3 * n:]
        for t, name in enumerate(SMALL):
            w_ref, m_ref, v_ref = ins[3 * t:3 * t + 3]
            g_out, d_out, m_out, v_out = outs[4 * t:4 * t + 4]
            g = grad(name, rv, rm, gc)
            g_out[...] = g
            d_out[...], m_out[...], v_out[...] = _adamw_update(w_ref[...], g, m_ref[...], v_ref[...])

    vm = pl.BlockSpec(memory_space=pltpu.VMEM)
    flat = [a for name in SMALL for a in wmv[name]]
    shapes = [jax.ShapeDtypeStruct(wmv[name][0].shape, F32) for name in SMALL for _ in range(4)]
    outs = pl.pallas_call(
        body, name="adamw_small", out_shape=tuple(shapes),
        in_specs=[vm] * (3 + len(flat)), out_specs=tuple([vm] * len(shapes)),
        compiler_params=_params(),
    )(redv, redm, g_conv, *flat)
    return {name: outs[4 * t:4 * t + 4] for t, name in enumerate(SMALL)}


def _rope_tables(positions):
    inv_freq = ROPE_THETA ** (-jnp.arange(0, ROT, 2, dtype=F32) / ROT)
    ang = positions.astype(F32)[:, None] * inv_freq
    cos, sin = jnp.cos(ang), jnp.sin(ang)
    n = positions.shape[0]
    half = ROT // 2
    rc = jnp.concatenate([cos, cos, jnp.ones((n, DH - ROT), F32)], axis=1)
    rsa = jnp.concatenate([-sin, jnp.zeros((n, DH - half), F32)], axis=1)
    rsb = jnp.concatenate([jnp.zeros((n, half), F32), sin, jnp.zeros((n, DH - ROT), F32)], axis=1)
    return rc, rsa, rsb


def kernel(x, c, positions, g_norm, w_mod, b_mod, w_in, b_gate, conv_w, conv_b, w_a, b_a, w_x, b_x, lam, w_out_rnn, w_out_attn, w_o, g_final, loss_target, m_g_norm, m_w_mod, m_b_mod, m_w_in, m_b_gate, m_conv_w, m_conv_b, m_w_a, m_b_a, m_w_x, m_b_x, m_lam, m_w_out_rnn, m_w_out_attn, m_w_o, m_g_final, v_g_norm, v_w_mod, v_b_mod, v_w_in, v_b_gate, v_conv_w, v_conv_b, v_w_a, v_b_a, v_w_x, v_b_x, v_lam, v_w_out_rnn, v_w_out_attn, v_w_o, v_g_final):
    s = x.shape[1]
    xi = lax.axis_index("x")
    yi = lax.axis_index("y")
    ci = lax.axis_index("c")
    shard = 2 * xi + yi
    x2d = x[0]
    tgt = loss_target[0]
    pos = positions[0]

    c_all, mod4, conv_all = _mod_fwd(c, w_mod[0], b_mod.reshape(4, 1, 768), conv_w[0])
    mod = mod4.reshape(1, 3 * D)
    shift, scale, gate = mod[:, :D], mod[:, D:2 * D], mod[:, 2 * D:]
    w3sh = jnp.concatenate([w_out_rnn[0], w_out_attn[0], w_o[0]], axis=0).astype(BF16)
    wsh = w_in[0].astype(BF16)
    conv_full = conv_all[0::2].transpose(1, 0, 2).reshape(4, D)

    order = jnp.stack([shard, shard ^ 2, shard ^ 1, shard ^ 3]).astype(jnp.int32)
    p, hbf, wg = _gather_norm_inproj(x2d, g_norm, shift, scale, wsh, order)
    wg = lax.dynamic_update_slice(wg, wsh[None], (shard, 0, 0))
    rc, rsa, rsb = _rope_tables(pos)
    pos_col = pos.reshape(s, 1)
    b_a3, b_x3 = b_a.reshape(H, 1, DH), b_x.reshape(H, 1, DH)
    hr, gr = _rnn_fwd(p, pos_col, conv_full, conv_b, w_a[0], b_a3, w_x[0], b_x3, lam)
    o, lse, ga, w3g = _attn_fwd(p, rc, rsa, rsb, w3sh)
    w3g = lax.dynamic_update_slice(w3g, w3sh[None], (shard, 0, 0))
    w3 = w3g.reshape(4, 3, 256, D).transpose(1, 0, 2, 3).reshape(3, D, D)

    dgr, dga, dc, dx2, vec_t, g_out = _tail(gr, ga, p, x2d, tgt, w3, b_gate, gate, g_final.reshape(1, D))

    dxr, dzr, g_wa, g_ba, g_wx, g_bx, g_lam, g_cw, g_cb = _rnn_bwd(
        p, hr, dgr, pos_col, conv_full, conv_b, w_a[0], b_a3, w_x[0], b_x3, lam)
    dq, dk, dv, dza = _attn_bwd(p, o, lse, dga, rc, rsa, rsb)

    pieces = [dxr, dzr, dq, dk, dv, dza, dc]
    g_win, rb_a, rb_b = _inproj_bwd_w(pieces, hbf, g_out)

    core = ci.reshape(1)
    shard1 = shard.reshape(1)
    (q_a, qh_a), (q_b, qh_b) = _add_half(g_win, rb_a, core, tr=256), _add_half(g_out, rb_b, core, tr=128)
    grad_x, vec_n, (r_a, r_b) = _inproj_bwd_x(pieces, wg, x2d, dx2, g_norm, scale, [qh_a, qh_b])
    f_a, f_b = _sum_slots(q_a, r_a, shard1, tr=256), _sum_slots(q_b, r_b, shard1, tr=128)
    s_a, s_b = _pair_swap([f_a, f_b])
    south = ci == 0
    grad_w_in = jnp.where(south, jnp.concatenate([f_a, s_a], axis=0), jnp.concatenate([s_a, f_a], axis=0))
    g3 = jnp.where(south, jnp.concatenate([f_b, s_b], axis=0), jnp.concatenate([s_b, f_b], axis=0)).reshape(3, 256, D)

    dmod_row = jnp.concatenate([vec_n[0:1], vec_n[1:2], vec_t[1:2]], axis=1)
    vpack = jnp.concatenate([
        vec_n[2:3],
        vec_n[0:2], vec_t[1:2],
        vec_t[2:4],
        g_cb.reshape(1, D),
        g_lam.reshape(1, D),
        vec_t[0:1],
        g_cw.transpose(1, 0, 2).reshape(4, D),
        vec_t[4:5],
        jnp.zeros((V_ROWS - 14, D), F32)], axis=0)
    mpack = jnp.concatenate([
        g_wa.reshape(H * DH, DH), g_wx.reshape(H * DH, DH), g_ba.reshape(H, DH), g_bx.reshape(H, DH),
        jnp.zeros((M_ROWS - 2 * H * DH - 2 * H, DH), F32)], axis=0)
    redv, redm = _allreduce_small([vpack, mpack])
    loss = redv[V_LOSS, 0]
    grad_w_mod = _mod_bwd(dmod_row.reshape(4, 1, 768), c_all)
    g_conv_sh = lax.dynamic_slice_in_dim(redv[V_CONV_W:V_CONV_W + 4], shard * 256, 256, axis=1)

    shape2d = dict(g_norm=(1, D), b_mod=(1, 3 * D), b_gate=(1, 2 * D), conv_b=(1, D), lam=(1, D), g_final=(1, D),
                   conv_w=(4, 256), w_a=(H * DH, DH), w_x=(H * DH, DH), b_a=(H, DH), b_x=(H, DH))
    given = dict(
        g_norm=(g_norm, m_g_norm, v_g_norm), b_mod=(b_mod, m_b_mod, v_b_mod), b_gate=(b_gate, m_b_gate, v_b_gate),
        conv_b=(conv_b, m_conv_b, v_conv_b), lam=(lam, m_lam, v_lam), g_final=(g_final, m_g_final, v_g_final),
        conv_w=(conv_w, m_conv_w, v_conv_w), w_a=(w_a, m_w_a, v_w_a), w_x=(w_x, m_w_x, v_w_x),
        b_a=(b_a, m_b_a, v_b_a), b_x=(b_x, m_b_x, v_b_x))
    small = _adamw_small(redv, redm, g_conv_sh,
                         {n: tuple(a.reshape(shape2d[n]) for a in given[n]) for n in SMALL})

    big_in = _adamw(w_in[0], grad_w_in, m_w_in[0], v_w_in[0], tr=256)
    big_mod = _adamw(w_mod[0], grad_w_mod, m_w_mod[0], v_w_mod[0], tr=256)
    w3f = jnp.concatenate([w_out_rnn[0], w_out_attn[0], w_o[0]], axis=0)
    m3f = jnp.concatenate([m_w_out_rnn[0], m_w_out_attn[0], m_w_o[0]], axis=0)
    v3f = jnp.concatenate([v_w_out_rnn[0], v_w_out_attn[0], v_w_o[0]], axis=0)
    big_out = _adamw(w3f, g3.reshape(768, D), m3f, v3f, tr=256)

    names = ["g_norm", "w_mod", "b_mod", "w_in", "b_gate", "conv_w", "conv_b", "w_a", "b_a", "w_x", "b_x", "lam",
             "w_out_rnn", "w_out_attn", "w_o", "g_final"]
    outs = []
    for idx in range(4):
        d = {n: small[n][idx].reshape(given[n][0].shape) for n in SMALL}
        if idx == 0:
            d.update(w_mod=grad_w_mod[None], w_in=grad_w_in[None],
                     w_out_rnn=g3[0][None], w_out_attn=g3[1][None], w_o=g3[2][None])
        else:
            d.update(w_mod=big_mod[idx - 1][None], w_in=big_in[idx - 1][None],
                     w_out_rnn=big_out[idx - 1][0:256][None], w_out_attn=big_out[idx - 1][256:512][None],
                     w_o=big_out[idx - 1][512:768][None])
        outs.append(d)
    flat = [d[n] for d in outs for n in names]
    return (loss, grad_x[None], *flat)
```

```python
import jax
import jax.numpy as jnp
from jax import lax
from jax.experimental import pallas as pl
from jax.experimental.pallas import tpu as pltpu

F32, BF16 = jnp.float32, jnp.bfloat16
MESH = pl.DeviceIdType.MESH
HIGHEST = lax.Precision.HIGHEST

D = 1024
H = 8
DH = 128
PW = 2048
EPS = 1e-6
LRU_C = 8.0
SCALE = DH ** -0.5
NEG = -1e30
SPAN = 2048
UB = 128
DILATIONS = (1, 4, 16)
UNIT_BATCH = 16
ROPE_THETA = 500000.0
ROT = 32

LR, B1, B2, ADAM_EPS, WD, STEP = 0.001, 0.9, 0.999, 1e-08, 0.01, 10

NDEV = 8


def _params(sem=None, vmem_mb=56):
    return pltpu.CompilerParams(dimension_semantics=sem, vmem_limit_bytes=vmem_mb * 2 ** 20)


def _coords():
    return lax.axis_index("x"), lax.axis_index("y"), lax.axis_index("c")


def _flip(v, bit):
    return 1 - v if bit else v


def _peer(k):
    x, y, c = _coords()
    return (_flip(x, (k >> 2) & 1), _flip(y, (k >> 1) & 1), _flip(c, k & 1))


def _my_index():
    x, y, c = _coords()
    return 4 * x + 2 * y + c


def _rcopy(src, dst, ssem, rsem, dev):
    return pltpu.make_async_remote_copy(src_ref=src, dst_ref=dst, send_sem=ssem, recv_sem=rsem,
                                        device_id=dev, device_id_type=MESH)


def _sigmoid(x):
    return jax.nn.sigmoid(x)


def _dot(a, b):
    return jnp.dot(a, b, preferred_element_type=F32)


def _dot_nt(a, b):
    return lax.dot_general(a, b, (((1,), (1,)), ((), ())), preferred_element_type=F32)


def _dot_tn(a, b):
    return lax.dot_general(a, b, (((0,), (0,)), ((), ())), preferred_element_type=F32)


def _colsum(a):
    return jnp.sum(a, axis=0, keepdims=True)


def _mod_fwd(c, w_mod_sh, b_mod4, conv_sh):
    def body(c_ref, w_ref, b_ref, cv_ref, call_ref, mod_ref, cvall_ref, rows_ref, cmat_ref, s1, r1, s2, r2, s3, r3):
        x, y, _ = _coords()
        me = _my_index()
        j = 2 * x + y
        call_ref[me] = c_ref[...]
        cvall_ref[me] = cv_ref[...]
        sends = []
        for k in range(1, NDEV):
            cp = _rcopy(call_ref.at[me], call_ref.at[me], s1.at[k - 1], r1.at[k - 1], _peer(k))
            cp.start()
            sends.append(cp)
            cp = _rcopy(cvall_ref.at[me], cvall_ref.at[me], s3.at[k - 1], r3.at[k - 1], _peer(k))
            cp.start()
            sends.append(cp)
        for k in range(1, NDEV):
            pk = me ^ k
            _rcopy(call_ref.at[pk], call_ref.at[pk], s1.at[k - 1], r1.at[k - 1], _peer(k)).wait_recv()
        for b in range(NDEV):
            cmat_ref[pl.ds(b, 1), :] = call_ref[b]
        cm = cmat_ref[...]
        act = cm * _sigmoid(cm)
        mp = jnp.dot(act, w_ref[...], preferred_element_type=F32, precision=HIGHEST) + b_ref[j]
        for b in range(NDEV):
            rows_ref[b] = mp[b:b + 1]
        mod_ref[j] = rows_ref[me]
        for q, k in enumerate((2, 4, 6)):
            cp = _rcopy(rows_ref.at[me ^ k], mod_ref.at[j], s2.at[q], r2.at[q], _peer(k))
            cp.start()
            sends.append(cp)
        for q, k in enumerate((2, 4, 6)):
            jq = j ^ (k >> 1)
            _rcopy(rows_ref.at[me], mod_ref.at[jq], s2.at[q], r2.at[q], _peer(k)).wait_recv()
        for k in range(1, NDEV):
            pk = me ^ k
            _rcopy(cvall_ref.at[pk], cvall_ref.at[pk], s3.at[k - 1], r3.at[k - 1], _peer(k)).wait_recv()
        for cp in sends:
            cp.wait_send()

    vm = pl.BlockSpec(memory_space=pltpu.VMEM)
    return pl.pallas_call(
        body, name="mod_fwd",
        out_shape=(jax.ShapeDtypeStruct((NDEV, 1, D), F32), jax.ShapeDtypeStruct((4, 1, 768), F32),
                   jax.ShapeDtypeStruct((NDEV,) + conv_sh.shape, F32)),
        in_specs=[vm, vm, vm, vm], out_specs=(vm, vm, vm),
        scratch_shapes=[pltpu.VMEM((NDEV, 1, 768), F32), pltpu.VMEM((NDEV, D), F32),
                        pltpu.SemaphoreType.DMA((7,)), pltpu.SemaphoreType.DMA((7,)),
                        pltpu.SemaphoreType.DMA((3,)), pltpu.SemaphoreType.DMA((3,)),
                        pltpu.SemaphoreType.DMA((7,)), pltpu.SemaphoreType.DMA((7,))],
        compiler_params=_params(),
    )(c, w_mod_sh, b_mod4, conv_sh)


def _mod_bwd(dmod4, c_all):
    def body(d_ref, call_ref, gw_ref, dall_ref, cmat_ref, dmat_ref, s1, r1):
        x, y, _ = _coords()
        me = _my_index()
        j = 2 * x + y
        dall_ref[me] = d_ref[...]
        sends = []
        for k in range(1, NDEV):
            cp = _rcopy(dall_ref.at[me], dall_ref.at[me], s1.at[k - 1], r1.at[k - 1], _peer(k))
            cp.start()
            sends.append(cp)
        for k in range(1, NDEV):
            pk = me ^ k
            _rcopy(dall_ref.at[pk], dall_ref.at[pk], s1.at[k - 1], r1.at[k - 1], _peer(k)).wait_recv()
        for cp in sends:
            cp.wait_send()
        for b in range(NDEV):
            cmat_ref[pl.ds(b, 1), :] = call_ref[b]
            dmat_ref[pl.ds(b, 1), :] = dall_ref[b, j]
        cm = cmat_ref[...]
        act = cm * _sigmoid(cm)
        gw_ref[...] = lax.dot_general(act, dmat_ref[...], (((0,), (0,)), ((), ())),
                                      preferred_element_type=F32, precision=HIGHEST)

    vm = pl.BlockSpec(memory_space=pltpu.VMEM)
    return pl.pallas_call(
        body, name="mod_bwd",
        out_shape=jax.ShapeDtypeStruct((D, 768), F32),
        in_specs=[vm, vm], out_specs=vm,
        scratch_shapes=[pltpu.VMEM((NDEV, 4, 1, 768), F32), pltpu.VMEM((NDEV, D), F32), pltpu.VMEM((NDEV, 768), F32),
                        pltpu.SemaphoreType.DMA((7,)), pltpu.SemaphoreType.DMA((7,))],
        compiler_params=_params(),
    )(dmod4, c_all)


def _gather_norm_inproj(x, gn, shift, scale, wsh, order, tm=1024, tn=1024):
    s = x.shape[0]
    ni = s // tm
    npc = PW // tn
    rows, cols = wsh.shape
    half = rows // 2
    nch = 4
    cr = half // nch
    chips = ((1, 0), (0, 1), (1, 1))

    def body(ord_ref, x_ref, gn_ref, sh_ref, sc_ref, wsh_hbm, p_ref, h_ref, wg_hbm, hs_all, w_s, wsem, ss, rs):
        slot, i, col = pl.program_id(0), pl.program_id(1), pl.program_id(2)
        cx, cy, cc = _coords()
        j = 2 * cx + cy
        sib = (cx, cy, 1 - cc)
        mine = lambda n: pl.ds(cc * half + n * cr, cr)
        theirs = lambda n: pl.ds((1 - cc) * half + n * cr, cr)
        shard_of = lambda q: j ^ (2 * chips[q][0] + chips[q][1])

        def to_chip(q, n):
            e = nch * q + n
            return _rcopy(wsh_hbm.at[mine(n)], wg_hbm.at[j, mine(n)], ss.at[e], rs.at[e],
                          (_flip(cx, chips[q][0]), _flip(cy, chips[q][1]), cc))

        def from_chip(q, n):
            e = nch * q + n
            return _rcopy(wsh_hbm.at[mine(n)], wg_hbm.at[shard_of(q), mine(n)], ss.at[e], rs.at[e], sib)

        def to_sibling(q, n):
            e = 3 * nch + nch * q + n
            return _rcopy(wg_hbm.at[shard_of(q), mine(n)], wg_hbm.at[shard_of(q), mine(n)], ss.at[e], rs.at[e], sib)

        def from_sibling(q, n):
            e = 3 * nch + nch * q + n
            return _rcopy(wsh_hbm.at[mine(n)], wg_hbm.at[shard_of(q), theirs(n)], ss.at[e], rs.at[e], sib)

        def load(sl, src):
            cp = pltpu.make_async_copy(src, w_s.at[sl], wsem.at[sl])
            cp.start()
            cp.wait()

        first = (i == 0) & (col == 0)

        @pl.when(first & (slot == 0))
        def _():
            for n in range(nch):
                for q in (0, 1):
                    to_chip(q, n).start()
            load(0, wsh_hbm.at[pl.ds(0, D), :])

        @pl.when(first & (slot == 1))
        def _():
            for q in (0, 1):
                for n in range(nch):
                    from_chip(q, n).wait_recv()
                    to_sibling(q, n).start()
            for n in range(nch):
                to_chip(2, n).start()
            for n in range(nch):
                from_sibling(0, n).wait_recv()
            load(1, wg_hbm.at[shard_of(0), pl.ds(0, D), :])

        @pl.when(first & (slot == 2))
        def _():
            for n in range(nch):
                from_sibling(1, n).wait_recv()
            load(2, wg_hbm.at[shard_of(1), pl.ds(0, D), :])

        @pl.when(first & (slot == 3))
        def _():
            for n in range(nch):
                from_chip(2, n).wait_recv()
                to_sibling(2, n).start()
            for n in range(nch):
                from_sibling(2, n).wait_recv()
            load(3, wg_hbm.at[shard_of(2), pl.ds(0, D), :])
            for q in range(3):
                for n in range(nch):
                    to_chip(q, n).wait_send()
                    to_sibling(q, n).wait_send()

        @pl.when((slot == 0) & (col == 0))
        def _():
            xt = x_ref[...]
            rstd = lax.rsqrt(jnp.mean(xt * xt, axis=-1, keepdims=True) + EPS)
            h = ((xt * rstd * gn_ref[...]) * (1.0 + sc_ref[...]) + sh_ref[...]).astype(BF16)
            hs_all[i] = h
            h_ref[...] = h

        p_ref[0] = _dot(hs_all[i], w_s[slot, :, pl.ds(pl.multiple_of(col * tn, tn), tn)]).astype(BF16)

    row = pl.BlockSpec((1, D), lambda sl, i, col, o: (0, 0))
    x_rows = lambda sl, i, col, o: (jnp.where(sl == 0, i, ni - 1), 0)
    any_ = pl.BlockSpec(memory_space=pl.ANY)
    return pl.pallas_call(
        body, name="gather_norm_inproj",
        grid_spec=pltpu.PrefetchScalarGridSpec(
            num_scalar_prefetch=1, grid=(4, ni, npc),
            in_specs=[pl.BlockSpec((tm, D), x_rows), row, row, row, any_],
            out_specs=(pl.BlockSpec((1, tm, tn), lambda sl, i, col, o: (o[sl], i, col)),
                       pl.BlockSpec((tm, D), x_rows), any_),
            scratch_shapes=[pltpu.VMEM((ni, tm, D), BF16), pltpu.VMEM((4, D, PW), BF16),
                            pltpu.SemaphoreType.DMA((4,)),
                            pltpu.SemaphoreType.DMA((6 * nch,)), pltpu.SemaphoreType.DMA((6 * nch,))]),
        out_shape=(jax.ShapeDtypeStruct((4, s, PW), BF16), jax.ShapeDtypeStruct((s, D), BF16),
                   jax.ShapeDtypeStruct((4, rows, cols), wsh.dtype)),
        compiler_params=_params(("arbitrary", "arbitrary", "arbitrary")),
    )(order, x, gn, shift, scale, wsh)


def _shift_down(prev8, cur, d):
    t = cur.shape[0]
    c3 = cur.reshape(t // 8, 8, DH)
    rot = pltpu.roll(c3, d, 1)
    before = jnp.concatenate([pltpu.roll(prev8, d, 0).reshape(1, 8, DH), rot[:-1]], axis=0)
    rows = lax.broadcasted_iota(jnp.int32, c3.shape, 1)
    return jnp.where(rows >= d, rot, before).reshape(t, DH)


def _shift_up(cur, next8, d):
    t = cur.shape[0]
    c3 = cur.reshape(t // 8, 8, DH)
    rot = pltpu.roll(c3, 8 - d, 1)
    after = jnp.concatenate([rot[1:], pltpu.roll(next8, 8 - d, 0).reshape(1, 8, DH)], axis=0)
    rows = lax.broadcasted_iota(jnp.int32, c3.shape, 1)
    return jnp.where(rows < 8 - d, rot, after).reshape(t, DH)


def _rnn_gates(xr, prev8, cw, cb, wa, ba, wx, bx, lam, reset):
    xc = cw[3:4] * xr + cb
    for d in (1, 2, 3):
        xc = xc + cw[3 - d:4 - d] * _shift_down(prev8, xr, d)
    xcb = xc.astype(BF16)
    r = _sigmoid(_dot(xcb, wa.astype(BF16)) + ba)
    ig = _sigmoid(_dot(xcb, wx.astype(BF16)) + bx)
    nl = -lam
    sp = jnp.maximum(nl, 0.0) + jnp.log1p(jnp.exp(-jnp.abs(nl)))
    log_a = (-LRU_C * r) * sp
    a = jnp.where(reset, 0.0, jnp.exp(log_a))
    mult = jnp.where(reset, 1.0, jnp.sqrt(1.0 - jnp.exp(2.0 * log_a)))
    return xc, r, ig, sp, a, mult


def _log_scan(a, b, axis, up):
    n = a.shape[axis]
    rows = lax.broadcasted_iota(jnp.int32, a.shape, axis)
    d = 1
    while d < n:
        m = rows < n - d if up else rows >= d
        shift = n - d if up else d
        a_s = pltpu.roll(a, shift, axis)
        b_s = pltpu.roll(b, shift, axis)
        b = jnp.where(m, a * b_s + b, b)
        a = jnp.where(m, a * a_s, a)
        d *= 2
    return a, b


def _scan(a, b, t, edge, up=False):
    g = t // 8
    a3, b3 = _log_scan(a.reshape(g, 8, DH), b.reshape(g, 8, DH), 1, up)
    last = 0 if up else 7
    ag, bg = _log_scan(a3[:, last, :], b3[:, last, :], 0, up)
    hg = ag * edge + bg
    grp = lax.broadcasted_iota(jnp.int32, hg.shape, 0)
    if up:
        cin = jnp.where(grp == g - 1, edge, pltpu.roll(hg, g - 1, 0))
        tail = hg[0:1]
    else:
        cin = jnp.where(grp == 0, edge, pltpu.roll(hg, 1, 0))
        tail = hg[g - 1:g]
    return (a3 * cin[:, None, :] + b3).reshape(t, DH), tail


def _rnn_fwd(p, pos, conv_w, conv_b, w_a, b_a, w_x, b_x, lam, tt=512):
    s = p.shape[1]
    nt = s // tt

    def body(xr_ref, z_ref, pos_ref, cw_ref, cb_ref, wa_ref, ba_ref, wx_ref, bx_ref, lam_ref,
             hr_ref, gr_ref, xprev, hprev):
        @pl.when(pl.program_id(1) == 0)
        def _():
            xprev[...] = jnp.zeros_like(xprev)
            hprev[...] = jnp.zeros_like(hprev)

        xr = xr_ref[0].astype(F32)
        z = z_ref[0].astype(F32)
        reset = pos_ref[...] == 0
        xc, r, ig, sp, a, mult = _rnn_gates(xr, xprev[...], cw_ref[...], cb_ref[...], wa_ref[0], ba_ref[0],
                                            wx_ref[0], bx_ref[0], lam_ref[...], reset)
        bx = mult * ig * xc
        h, h_last = _scan(a, bx, tt, hprev[0:1])
        xprev[...] = xr[tt - 8:]
        hprev[...] = jnp.broadcast_to(h_last, (8, DH))
        hr_ref[...] = h
        gr_ref[...] = (h * (z * _sigmoid(z))).astype(BF16)

    head_row = lambda hh, t: (0, hh)
    return pl.pallas_call(
        body, name="rnn_fwd", grid=(H, nt),
        in_specs=[pl.BlockSpec((1, tt, DH), lambda hh, t: (0, t, hh)),
                  pl.BlockSpec((1, tt, DH), lambda hh, t: (0, t, H + hh)),
                  pl.BlockSpec((tt, 1), lambda hh, t: (t, 0)),
                  pl.BlockSpec((4, DH), head_row), pl.BlockSpec((1, DH), head_row),
                  pl.BlockSpec((1, DH, DH), lambda hh, t: (hh, 0, 0)), pl.BlockSpec((1, 1, DH), lambda hh, t: (hh, 0, 0)),
                  pl.BlockSpec((1, DH, DH), lambda hh, t: (hh, 0, 0)), pl.BlockSpec((1, 1, DH), lambda hh, t: (hh, 0, 0)),
                  pl.BlockSpec((1, DH), head_row)],
        out_specs=(pl.BlockSpec((tt, DH), lambda hh, t: (t, hh)), pl.BlockSpec((tt, DH), lambda hh, t: (t, hh))),
        out_shape=(jax.ShapeDtypeStruct((s, D), F32), jax.ShapeDtypeStruct((s, D), BF16)),
        scratch_shapes=[pltpu.VMEM((8, DH), F32), pltpu.VMEM((8, DH), F32)],
        compiler_params=_params(("parallel", "arbitrary")),
    )(p, p, pos, conv_w, conv_b, w_a, b_a, w_x, b_x, lam)


def _rnn_bwd(p, hr, dgr, pos, conv_w, conv_b, w_a, b_a, w_x, b_x, lam, tt=512):
    s = p.shape[1]
    nt = s // tt
    t8 = tt // 8

    def body(xr_ref, z_ref, xp_ref, hr_ref, hp_ref, dg_ref, pos_ref, cw_ref, cb_ref, wa_ref, ba_ref, wx_ref, bx_ref,
             lam_ref, dxr_ref, dz_ref, gwa_ref, gba_ref, gwx_ref, gbx_ref, glam_ref, gcw_ref, gcb_ref,
             a_next, g_next, dxc_next):
        t = pl.program_id(1)
        has_prev = t < nt - 1

        @pl.when(t == 0)
        def _():
            a_next[...] = jnp.zeros_like(a_next)
            g_next[...] = jnp.zeros_like(g_next)
            dxc_next[...] = jnp.zeros_like(dxc_next)
            gwa_ref[...] = jnp.zeros_like(gwa_ref)
            gba_ref[...] = jnp.zeros_like(gba_ref)
            gwx_ref[...] = jnp.zeros_like(gwx_ref)
            gbx_ref[...] = jnp.zeros_like(gbx_ref)
            glam_ref[...] = jnp.zeros_like(glam_ref)
            gcw_ref[...] = jnp.zeros_like(gcw_ref)
            gcb_ref[...] = jnp.zeros_like(gcb_ref)

        xr = xr_ref[0].astype(F32)
        z = z_ref[0].astype(F32)
        hr_blk = hr_ref[...]
        dg = dg_ref[...]
        xprev = jnp.where(has_prev, xp_ref[0].astype(F32)[8:], 0.0)
        hprev8 = jnp.where(has_prev, hp_ref[...], 0.0)
        reset = pos_ref[...] == 0
        cw = cw_ref[...]
        wa = wa_ref[0]
        wx = wx_ref[0]
        lam_v = lam_ref[...]
        xc, r, ig, sp, a, mult = _rnn_gates(xr, xprev, cw, cb_ref[...], wa, ba_ref[0], wx, bx_ref[0], lam_v, reset)

        sz = _sigmoid(z)
        dh = dg * (z * sz)
        dz_ref[...] = (dg * hr_blk * (sz * (1.0 + z * (1.0 - sz)))).astype(BF16)

        an = _shift_up(a, a_next[...], 1)
        g, g_first = _scan(an, dh, tt, g_next[0:1], up=True)
        a_next[...] = jnp.broadcast_to(a[0:1], (8, DH))
        g_next[...] = jnp.broadcast_to(g_first, (8, DH))

        hm1 = _shift_down(hprev8, hr_blk, 1)
        da = g * hm1
        dmult = g * (ig * xc)
        di = g * (mult * xc)
        dxc = g * (mult * ig)
        dla = jnp.where(reset, 0.0, da * a - dmult * (a * a) / mult)
        dr = dla * (-LRU_C * sp)
        dsp = _colsum(dla * (-LRU_C * r))
        glam_ref[0] += dsp * (-_sigmoid(-lam_v))
        dpa = dr * r * (1.0 - r)
        dpx = di * ig * (1.0 - ig)
        dpab = dpa.astype(BF16)
        dpxb = dpx.astype(BF16)
        dxc = dxc + _dot_nt(dpab, wa.astype(BF16)) + _dot_nt(dpxb, wx.astype(BF16))
        xcb = xc.astype(BF16)
        gwa_ref[0] += _dot_tn(xcb, dpab)
        gwx_ref[0] += _dot_tn(xcb, dpxb)
        gba_ref[0] += _colsum(dpa)
        gbx_ref[0] += _colsum(dpx)

        dxr = cw[3:4] * dxc
        for d in (1, 2, 3):
            dxr = dxr + cw[3 - d:4 - d] * _shift_up(dxc, dxc_next[...], d)
        dxr_ref[...] = dxr.astype(BF16)
        dxc_next[...] = dxc[0:8]
        gcb_ref[0] += _colsum(dxc)
        gcw_ref[0, 3:4, :] += _colsum(xr * dxc)
        for d in (1, 2, 3):
            gcw_ref[0, 3 - d:4 - d, :] += _colsum(_shift_down(xprev, xr, d) * dxc)

    rt = lambda t: nt - 1 - t
    prev8 = lambda t: jnp.maximum(rt(t) * t8 - 1, 0)
    head_row = lambda hh, t: (0, hh)
    hsm = lambda hh, t: (hh, 0, 0)
    return pl.pallas_call(
        body, name="rnn_bwd", grid=(H, nt),
        in_specs=[pl.BlockSpec((1, tt, DH), lambda hh, t: (0, rt(t), hh)),
                  pl.BlockSpec((1, tt, DH), lambda hh, t: (0, rt(t), H + hh)),
                  pl.BlockSpec((1, 16, DH), lambda hh, t: (0, jnp.maximum(rt(t) * (tt // 16) - 1, 0), hh)),
                  pl.BlockSpec((tt, DH), lambda hh, t: (rt(t), hh)),
                  pl.BlockSpec((8, DH), lambda hh, t: (prev8(t), hh)),
                  pl.BlockSpec((tt, DH), lambda hh, t: (rt(t), hh)),
                  pl.BlockSpec((tt, 1), lambda hh, t: (rt(t), 0)),
                  pl.BlockSpec((4, DH), head_row), pl.BlockSpec((1, DH), head_row),
                  pl.BlockSpec((1, DH, DH), hsm), pl.BlockSpec((1, 1, DH), hsm),
                  pl.BlockSpec((1, DH, DH), hsm), pl.BlockSpec((1, 1, DH), hsm),
                  pl.BlockSpec((1, DH), head_row)],
        out_specs=(pl.BlockSpec((tt, DH), lambda hh, t: (rt(t), hh)), pl.BlockSpec((tt, DH), lambda hh, t: (rt(t), hh)),
                   pl.BlockSpec((1, DH, DH), hsm), pl.BlockSpec((1, 1, DH), hsm),
                   pl.BlockSpec((1, DH, DH), hsm), pl.BlockSpec((1, 1, DH), hsm),
                   pl.BlockSpec((1, 1, DH), hsm), pl.BlockSpec((1, 4, DH), hsm), pl.BlockSpec((1, 1, DH), hsm)),
        out_shape=(jax.ShapeDtypeStruct((s, D), BF16), jax.ShapeDtypeStruct((s, D), BF16),
                   jax.ShapeDtypeStruct((H, DH, DH), F32), jax.ShapeDtypeStruct((H, 1, DH), F32),
                   jax.ShapeDtypeStruct((H, DH, DH), F32), jax.ShapeDtypeStruct((H, 1, DH), F32),
                   jax.ShapeDtypeStruct((H, 1, DH), F32), jax.ShapeDtypeStruct((H, 4, DH), F32),
                   jax.ShapeDtypeStruct((H, 1, DH), F32)),
        scratch_shapes=[pltpu.VMEM((8, DH), F32), pltpu.VMEM((8, DH), F32), pltpu.VMEM((8, DH), F32)],
        compiler_params=_params(("parallel", "arbitrary")),
    )(p, p, p, hr, hr, dgr, pos, conv_w, conv_b, w_a, b_a, w_x, b_x, lam)


def _rope(t, c, sa, sb):
    return t * c + pltpu.roll(t, DH - ROT // 2, 1) * sa + pltpu.roll(t, ROT // 2, 1) * sb


def _rope_bwd(g, c, sa, sb):
    return g * c + pltpu.roll(g * sa, ROT // 2, 1) + pltpu.roll(g * sb, DH - ROT // 2, 1)


def _unit_bases(gi, u):
    dil = DILATIONS[gi]
    if dil == 1:
        return u * UB, SPAN + (u - 1) * UB, u == 0
    if dil == 4:
        blk, r = u // 4, u % 4
        return blk * 4 * UB + r, SPAN + (blk - 1) * 4 * UB + r, blk == 0
    return u, u, True


def _unit_slices(gi, u):
    dil = DILATIONS[gi]
    qb0, kb0, first = _unit_bases(gi, u)
    if dil == 1:
        return pl.ds(pl.multiple_of(qb0, UB), UB), pl.ds(pl.multiple_of(kb0, UB), 2 * UB), first
    return pl.ds(qb0, UB, stride=dil), pl.ds(kb0, 2 * UB, stride=dil), first


def _bdot(a, b):
    return lax.dot_general(a, b, (((2,), (1,)), ((0,), (0,))), preferred_element_type=F32)


def _bdot_nt(a, b):
    return lax.dot_general(a, b, (((2,), (2,)), ((0,), (0,))), preferred_element_type=F32)


def _bdot_tn(a, b):
    return lax.dot_general(a, b, (((1,), (1,)), ((0,), (0,))), preferred_element_type=F32)


def _band_mask(first_in_span, has_prev):
    qi = lax.broadcasted_iota(jnp.int32, (UB, 2 * UB), 0)
    ki = lax.broadcasted_iota(jnp.int32, (UB, 2 * UB), 1)
    dist = UB + qi - ki
    band = (dist >= 0) & (dist <= UB)
    return band & ((ki >= UB) | jnp.logical_not(first_in_span) | has_prev)


def _gather_halves(phase, src_hbm, dst_hbm, ss, rs):
    half = src_hbm.shape[0] // 2
    cx, cy, cc = _coords()
    j = 2 * cx + cy
    sib = (cx, cy, 1 - cc)
    mine = pl.ds(cc * half, half)
    theirs = pl.ds((1 - cc) * half, half)
    chips = ((1, 0), (0, 1), (1, 1))
    for q, (kx, ky) in enumerate(chips):
        jq = j ^ (2 * kx + ky)
        out = _rcopy(src_hbm.at[mine], dst_hbm.at[j, mine], ss.at[q], rs.at[q], (_flip(cx, kx), _flip(cy, ky), cc))
        landed = _rcopy(src_hbm.at[mine], dst_hbm.at[jq, mine], ss.at[q], rs.at[q], sib)
        onward = _rcopy(dst_hbm.at[jq, mine], dst_hbm.at[jq, mine], ss.at[3 + q], rs.at[3 + q], sib)
        from_sib = _rcopy(src_hbm.at[mine], dst_hbm.at[jq, theirs], ss.at[3 + q], rs.at[3 + q], sib)
        if phase == 0:
            out.start()
        elif phase == 1:
            landed.wait_recv()
            onward.start()
        else:
            from_sib.wait_recv()
            out.wait_send()
            onward.wait_send()


def _attn_fwd(p, rc, rsa, rsb, w3sh):
    s = p.shape[1]
    ns = s // SPAN
    nunit = SPAN // UB

    def body(q_ref, k_ref, v_ref, z_ref, c_ref, sa_ref, sb_ref, w3_hbm, o_ref, lse_ref, ga_ref, qro_ref, kro_ref,
             w3g_hbm, qr, kf, vf, acc, mm, ll, ss, rs):
        hh, n = pl.program_id(0), pl.program_id(1)
        for phase, at_head, at_span in ((0, 0, 0), (1, H // 2, 0), (2, H - 1, ns - 1)):
            @pl.when((hh == at_head) & (n == at_span))
            def _(phase=phase):
                _gather_halves(phase, w3_hbm, w3g_hbm, ss, rs)

        @pl.when(n == 0)
        def _():
            kf[0:SPAN] = jnp.zeros((SPAN, DH), F32)
            vf[0:SPAN] = jnp.zeros((SPAN, DH), F32)

        c, sa, sb = c_ref[...], sa_ref[...], sb_ref[...]
        q_rot = _rope(q_ref[0].astype(F32), c, sa, sb).astype(BF16)
        k_rot = _rope(k_ref[0].astype(F32), c, sa, sb).astype(BF16)
        qro_ref[...] = q_rot
        kro_ref[...] = k_rot
        qr[...] = q_rot.astype(F32)
        kf[SPAN:] = k_rot.astype(F32)
        vf[SPAN:] = v_ref[0].astype(F32)
        has_prev = n > 0

        for gi, dil in enumerate(DILATIONS):
            def trip(t, carry, gi=gi, dil=dil):
                qsls, ksls, firsts = [], [], []
                for b in range(UNIT_BATCH):
                    qsl, ksl, first = _unit_slices(gi, t * UNIT_BATCH + b)
                    qsls.append(qsl)
                    ksls.append(ksl)
                    firsts.append(first)
                qb = jnp.stack([qr[qsl, :].astype(BF16) for qsl in qsls])
                kb = jnp.stack([kf[ksl, :].astype(BF16) for ksl in ksls])
                vb = jnp.stack([vf[ksl, :].astype(BF16) for ksl in ksls])
                s_all = _bdot_nt(qb, kb)
                prs = []
                for b in range(UNIT_BATCH):
                    sc = jnp.where(_band_mask(firsts[b], has_prev), s_all[b] * SCALE, NEG)
                    m = jnp.max(sc, axis=-1, keepdims=True)
                    pr = jnp.exp(sc - m)
                    l = jnp.sum(pr, axis=-1, keepdims=True)
                    mm[gi, qsls[b], :] = jnp.broadcast_to(m, (UB, DH))
                    ll[gi, qsls[b], :] = jnp.broadcast_to(l, (UB, DH))
                    prs.append(pr.astype(BF16))
                o_all = _bdot(jnp.stack(prs), vb)
                for b in range(UNIT_BATCH):
                    acc[gi, qsls[b], :] = o_all[b]
                return carry

            lax.fori_loop(0, nunit // UNIT_BATCH, trip, 0)

        m_all =jnp.maximum(jnp.maximum(mm[0], mm[1]), mm[2])
        num = jnp.zeros((SPAN, DH), F32)
        den = jnp.zeros((SPAN, DH), F32)
        for gi in range(3):
            w = jnp.exp(mm[gi] - m_all)
            num = num + w * acc[gi]
            den = den + w * ll[gi]
        o = num / den
        o_ref[...] = o
        lse_ref[...] = m_all + jnp.log(den)
        z = z_ref[0].astype(F32)
        ga_ref[...] = (o * (z * _sigmoid(z))).astype(BF16)
        kf[0:SPAN] = kf[SPAN:]
        vf[0:SPAN] = vf[SPAN:]

    blk = lambda piece, off: pl.BlockSpec((1, SPAN, DH), lambda hh, n: (piece, n, off + hh))
    tab = pl.BlockSpec((SPAN, DH), lambda hh, n: (n, 0))
    outb = pl.BlockSpec((SPAN, DH), lambda hh, n: (n, hh))
    any_ = pl.BlockSpec(memory_space=pl.ANY)
    return pl.pallas_call(
        body, name="attn_fwd", grid=(H, ns),
        in_specs=[blk(1, 0), blk(1, H), blk(2, 0), blk(2, H), tab, tab, tab, any_],
        out_specs=(outb, outb, outb, outb, outb, any_),
        out_shape=(jax.ShapeDtypeStruct((s, D), F32), jax.ShapeDtypeStruct((s, D), F32),
                   jax.ShapeDtypeStruct((s, D), BF16), jax.ShapeDtypeStruct((s, D), BF16),
                   jax.ShapeDtypeStruct((s, D), BF16), jax.ShapeDtypeStruct((4,) + w3sh.shape, w3sh.dtype)),
        scratch_shapes=[pltpu.VMEM((SPAN, DH), F32), pltpu.VMEM((2 * SPAN, DH), F32), pltpu.VMEM((2 * SPAN, DH), F32),
                        pltpu.VMEM((3, SPAN, DH), F32), pltpu.VMEM((3, SPAN, DH), F32), pltpu.VMEM((3, SPAN, DH), F32),
                        pltpu.SemaphoreType.DMA((6,)), pltpu.SemaphoreType.DMA((6,))],
        compiler_params=_params(("arbitrary", "arbitrary")),
    )(p, p, p, p, rc, rsa, rsb, w3sh)


def _attn_bwd(p, q_rot, k_rot, o, lse, dga, rc, rsa, rsb):
    s = p.shape[1]
    ns = s // SPAN
    nunit = SPAN // UB

    def body(q_ref, k_ref, kp_ref, v_ref, vp_ref, z_ref, c_ref, sa_ref, sb_ref,
             o_ref, lse_ref, dg_ref, dq_ref, dk_ref, dv_ref, dz_ref,
             qr, kf, vf, dof, dlt, dqa, dkf, dvf):
        step = pl.program_id(1)
        n = ns - 1 - step
        has_prev = n > 0

        @pl.when(step == 0)
        def _():
            dkf[...] = jnp.zeros_like(dkf)
            dvf[...] = jnp.zeros_like(dvf)

        @pl.when(step > 0)
        def _():
            dkf[SPAN:] = dkf[0:SPAN]
            dvf[SPAN:] = dvf[0:SPAN]
            dkf[0:SPAN] = jnp.zeros((SPAN, DH), F32)
            dvf[0:SPAN] = jnp.zeros((SPAN, DH), F32)

        c, sa, sb = c_ref[...], sa_ref[...], sb_ref[...]
        qr[...] = q_ref[...].astype(F32)
        kf[SPAN:] = k_ref[...].astype(F32)
        vf[SPAN:] = v_ref[0].astype(F32)
        kf[0:SPAN] = jnp.where(has_prev, kp_ref[...].astype(F32), 0.0)
        vf[0:SPAN] = jnp.where(has_prev, vp_ref[0].astype(F32), 0.0)
        z = z_ref[0].astype(F32)
        sz = _sigmoid(z)
        dg = dg_ref[...]
        ov = o_ref[...]
        do = dg * (z * sz)
        dz_ref[...] = (dg * ov * (sz * (1.0 + z * (1.0 - sz)))).astype(BF16)
        dof[...] = do
        dlt[...] = jnp.dot(do * ov, jnp.ones((DH, DH), F32), preferred_element_type=F32, precision=HIGHEST)
        dqa[...] = jnp.zeros_like(dqa)

        for gi, dil in enumerate(DILATIONS):
            def trip(t, carry, gi=gi, dil=dil):
                qsls, ksls, firsts = [], [], []
                for b in range(UNIT_BATCH):
                    qsl, ksl, first = _unit_slices(gi, t * UNIT_BATCH + b)
                    qsls.append(qsl)
                    ksls.append(ksl)
                    firsts.append(first)
                qb = jnp.stack([qr[qsl, :].astype(BF16) for qsl in qsls])
                kb = jnp.stack([kf[ksl, :].astype(BF16) for ksl in ksls])
                vb = jnp.stack([vf[ksl, :].astype(BF16) for ksl in ksls])
                dob = jnp.stack([dof[qsl, :].astype(BF16) for qsl in qsls])
                s_all = _bdot_nt(qb, kb)
                dp_all = _bdot_nt(dob, vb)
                prs, dss = [], []
                for b in range(UNIT_BATCH):
                    lse_b = lse_ref[qsls[b], :]
                    dl_b = dlt[qsls[b], :]
                    pr = jnp.exp(s_all[b] * SCALE - jnp.concatenate([lse_b, lse_b], axis=1))
                    pr = jnp.where(_band_mask(firsts[b], has_prev), pr, 0.0)
                    prs.append(pr.astype(BF16))
                    dss.append((pr * (dp_all[b] - jnp.concatenate([dl_b, dl_b], axis=1)) * SCALE).astype(BF16))
                ds_all = jnp.stack(dss)
                dv_all = _bdot_tn(jnp.stack(prs), dob)
                dq_all = _bdot(ds_all, kb)
                dk_all = _bdot_tn(ds_all, qb)
                for b in range(UNIT_BATCH):
                    dvf[ksls[b], :] += dv_all[b]
                    dqa[qsls[b], :] += dq_all[b]
                    dkf[ksls[b], :] += dk_all[b]
                return carry

            lax.fori_loop(0, nunit // UNIT_BATCH, trip, 0)

        dq_ref[...] = _rope_bwd(dqa[...], c, sa, sb).astype(BF16)
        dk_ref[...] = _rope_bwd(dkf[SPAN:], c, sa, sb).astype(BF16)
        dv_ref[...] = dvf[SPAN:].astype(BF16)

    rn = lambda n: ns - 1 - n
    pn = lambda n: jnp.maximum(ns - 2 - n, 0)
    blk = lambda piece, off: pl.BlockSpec((1, SPAN, DH), lambda hh, n: (piece, rn(n), off + hh))
    blkp = lambda piece, off: pl.BlockSpec((1, SPAN, DH), lambda hh, n: (piece, pn(n), off + hh))
    tab = pl.BlockSpec((SPAN, DH), lambda hh, n: (rn(n), 0))
    io = pl.BlockSpec((SPAN, DH), lambda hh, n: (rn(n), hh))
    iop = pl.BlockSpec((SPAN, DH), lambda hh, n: (pn(n), hh))
    return pl.pallas_call(
        body, name="attn_bwd", grid=(H, ns),
        in_specs=[io, io, iop, blk(2, 0), blkp(2, 0), blk(2, H), tab, tab, tab, io, io, io],
        out_specs=(io, io, io, io),
        out_shape=tuple(jax.ShapeDtypeStruct((s, D), BF16) for _ in range(4)),
        scratch_shapes=[pltpu.VMEM((SPAN, DH), F32), pltpu.VMEM((2 * SPAN, DH), F32), pltpu.VMEM((2 * SPAN, DH), F32),
                        pltpu.VMEM((SPAN, DH), F32), pltpu.VMEM((SPAN, DH), F32), pltpu.VMEM((SPAN, DH), F32),
                        pltpu.VMEM((2 * SPAN, DH), F32), pltpu.VMEM((2 * SPAN, DH), F32)],
        compiler_params=_params(("parallel", "arbitrary")),
    )(q_rot, k_rot, k_rot, p, p, p, rc, rsa, rsb, o, lse, dga)


def _tail(gr, ga, p, x, tgt, w3, b_gate, gate, g_final, tm=256):
    s = x.shape[0]
    nt = s // tm

    def body(gr_ref, ga_ref, pr_ref, pa_ref, x_ref, t_ref, bg_ref, gate_ref, gf_ref, w_hbm,
             dgr_ref, dga_ref, dc_ref, dx2_ref, vec_ref, go_hbm, w_s, acc_s, sem):
        i = pl.program_id(0)

        @pl.when(i == 0)
        def _():
            cp = pltpu.make_async_copy(w_hbm, w_s, sem.at[12])
            cp.start()
            acc_s[...] = jnp.zeros_like(acc_s)
            vec_ref[...] = jnp.zeros_like(vec_ref)
            cp.wait()

        grb = gr_ref[...]
        gab = ga_ref[...]
        bg = bg_ref[...]
        gate_v = gate_ref[...]
        gf = gf_ref[...]
        y_r = _dot(grb, w_s[0])
        y_a = _dot(gab, w_s[1])
        sr = _sigmoid(pr_ref[0].astype(F32) + bg[:, :D])
        sa = _sigmoid(pa_ref[0].astype(F32) + bg[:, D:])
        mb = (sr * y_r + sa * y_a).astype(BF16)
        u = _dot(mb, w_s[2])
        x2 = x_ref[...] + gate_v * u
        rstd = lax.rsqrt(jnp.mean(x2 * x2, axis=-1, keepdims=True) + EPS)
        xh = x2 * rstd
        e = xh * gf - t_ref[...]
        dy = e * (1.0 / D)
        dyg = dy * gf
        dx2 = rstd * (dyg - xh * jnp.mean(dyg * xh, axis=-1, keepdims=True))
        dx2_ref[...] = dx2
        dub = (dx2 * gate_v).astype(BF16)
        dm = _dot_nt(dub, w_s[2])
        dyr = (dm * sr).astype(BF16)
        dya = (dm * sa).astype(BF16)
        dpr = dm * y_r * (sr * (1.0 - sr))
        dpa = dm * y_a * (sa * (1.0 - sa))
        dc_ref[:, :D] = dpr.astype(BF16)
        dc_ref[:, D:] = dpa.astype(BF16)
        dgr_ref[...] = _dot_nt(dyr, w_s[0])
        dga_ref[...] = _dot_nt(dya, w_s[1])
        acc_s[0] += _dot_tn(grb, dyr)
        acc_s[1] += _dot_tn(gab, dya)
        acc_s[2] += _dot_tn(mb, dub)
        vec_ref[0:1, :] += _colsum(dy * xh)
        vec_ref[1:2, :] += _colsum(dx2 * u)
        vec_ref[2:3, :] += _colsum(dpr)
        vec_ref[3:4, :] += _colsum(dpa)
        vec_ref[4:5, :] += _colsum(e * e)

        @pl.when(i == nt - 1)
        def _():
            vec_ref[4:5, :] = jnp.broadcast_to(jnp.sum(vec_ref[4:5, :]) * (0.5 / D), (1, D))
            cps = []
            for w in range(3):
                for j in range(4):
                    cps.append(pltpu.make_async_copy(acc_s.at[w, pl.ds(256 * j, 256)],
                                                     go_hbm.at[j, pl.ds(256 * w, 256)], sem.at[4 * w + j]))
            for cp in cps:
                cp.start()
            for cp in cps:
                cp.wait()

    rowt = lambda i: (i, 0)
    row = lambda w: pl.BlockSpec((1, w), lambda i: (0, 0))
    any_ = pl.BlockSpec(memory_space=pl.ANY)
    return pl.pallas_call(
        body, name="tail", grid=(nt,),
        in_specs=[pl.BlockSpec((tm, D), rowt), pl.BlockSpec((tm, D), rowt),
                  pl.BlockSpec((1, tm, D), lambda i: (3, i, 0)), pl.BlockSpec((1, tm, D), lambda i: (3, i, 1)),
                  pl.BlockSpec((tm, D), rowt), pl.BlockSpec((tm, D), rowt),
                  row(2 * D), row(D), row(D), any_],
        out_specs=(pl.BlockSpec((tm, D), rowt), pl.BlockSpec((tm, D), rowt), pl.BlockSpec((tm, 2 * D), rowt),
                   pl.BlockSpec((tm, D), rowt), pl.BlockSpec((8, D), lambda i: (0, 0)), any_),
        out_shape=(jax.ShapeDtypeStruct((s, D), F32), jax.ShapeDtypeStruct((s, D), F32),
                   jax.ShapeDtypeStruct((s, 2 * D), BF16), jax.ShapeDtypeStruct((s, D), F32),
                   jax.ShapeDtypeStruct((8, D), F32), jax.ShapeDtypeStruct((4, 768, D), F32)),
        scratch_shapes=[pltpu.VMEM((3, D, D), BF16), pltpu.VMEM((3, D, D), F32), pltpu.SemaphoreType.DMA((13,))],
        compiler_params=_params(("arbitrary",)),
    )(gr, ga, p, p, x, tgt, b_gate, gate, g_final, w3)


def _pieces_steps(pieces):
    out, s0 = [], 0
    for a in pieces:
        n = a.shape[1] // D
        out.append((s0, n))
        s0 += n
    return out, s0


def _inproj_bwd_x(pieces, wg, x, dx2, gn, scale, sums, tm=512):
    s = x.shape[0]
    np_ = len(pieces)
    na = len(sums)
    ni = s // tm
    groups, cur, width = [], [], 0
    for t, a in enumerate(pieces):
        cur.append(t)
        width += a.shape[1]
        if width == PW:
            groups.append(cur)
            cur, width = [], 0
    assert len(groups) == 4 and not cur

    def body(*refs):
        d_refs = refs[:np_]
        w_hbm, x_ref, dx2_ref, gn_ref, sc_ref = refs[np_:np_ + 5]
        q_refs = refs[np_ + 5:np_ + 5 + na]
        gx_ref, vec_ref = refs[np_ + 5 + na:np_ + 7 + na]
        r_refs = refs[np_ + 7 + na:np_ + 7 + 2 * na]
        w_s, wsem, ss, rs = refs[np_ + 7 + 2 * na:]
        i = pl.program_id(0)

        def scatter_copies():
            cx, cy, cc = _coords()
            j = 2 * cx + cy
            cps = []
            for t, (q, r) in enumerate(zip(q_refs, r_refs)):
                for e, (kx, ky) in enumerate(((1, 0), (0, 1), (1, 1))):
                    cps.append(_rcopy(q.at[j ^ (2 * kx + ky)], r.at[e], ss.at[3 * t + e], rs.at[3 * t + e],
                                      (_flip(cx, kx), _flip(cy, ky), cc)))
            return cps

        def w_copy(pc):
            return pltpu.make_async_copy(w_hbm.at[pc, pl.ds(0, D), :], w_s.at[pc], wsem.at[pc])

        @pl.when(i == 0)
        def _():
            for pc in range(4):
                w_copy(pc).start()
            vec_ref[...] = jnp.zeros_like(vec_ref)
            for cp in scatter_copies():
                cp.start()

        dh = None
        for pc, group in enumerate(groups):
            @pl.when(i == 0)
            def _(pc=pc):
                w_copy(pc).wait()

            tiles = [d_refs[t][...] for t in group]
            lhs = tiles[0] if len(tiles) == 1 else jnp.concatenate(tiles, axis=1)
            part = _dot_nt(lhs, w_s[pc])
            dh = part if dh is None else dh + part

        xt = x_ref[...]
        rstd = lax.rsqrt(jnp.mean(xt * xt, axis=-1, keepdims=True) + EPS)
        xh = xt * rstd
        gn_v = gn_ref[...]
        sc1 = 1.0 + sc_ref[...]
        dhx = dh * xh
        vec_ref[0:1, :] += _colsum(dh)
        vec_ref[1:2, :] += _colsum(dhx) * gn_v
        vec_ref[2:3, :] += _colsum(dhx) * sc1
        dxh = dh * (gn_v * sc1)
        gx_ref[...] = rstd * (dxh - xh * jnp.mean(dxh * xh, axis=-1, keepdims=True)) + dx2_ref[...]

        @pl.when(i == ni - 1)
        def _():
            for cp in scatter_copies():
                cp.wait()

    rowt = lambda i: (i, 0)
    row = pl.BlockSpec((1, D), lambda i: (0, 0))
    any_ = pl.BlockSpec(memory_space=pl.ANY)
    outs = pl.pallas_call(
        body, name="inproj_bwd_x", grid=(ni,),
        in_specs=[pl.BlockSpec((tm, a.shape[1]), rowt) for a in pieces] +
                 [any_, pl.BlockSpec((tm, D), rowt), pl.BlockSpec((tm, D), rowt), row, row] + [any_] * na,
        out_specs=(pl.BlockSpec((tm, D), rowt), pl.BlockSpec((8, D), lambda i: (0, 0))) + (any_,) * na,
        out_shape=(jax.ShapeDtypeStruct((s, D), F32), jax.ShapeDtypeStruct((8, D), F32)) +
                  tuple(jax.ShapeDtypeStruct((3,) + q.shape[1:], q.dtype) for q in sums),
        scratch_shapes=[pltpu.VMEM((4, D, PW), BF16), pltpu.SemaphoreType.DMA((4,)),
                        pltpu.SemaphoreType.DMA((3 * na,)), pltpu.SemaphoreType.DMA((3 * na,))],
        compiler_params=_params(("arbitrary",)),
    )(*pieces, wg, x, dx2, gn, scale, *sums)
    return outs[0], outs[1], outs[2:]


def _inproj_bwd_w(pieces, hbf, g_out, tk=1024):
    s = hbf.shape[0]
    steps, nk = _pieces_steps(pieces)
    npc = PW // D
    ns = s // tk
    np_ = len(pieces)
    hr = D // 2
    ohr = g_out.shape[1] // 2
    ocr = _chunk_rows(g_out)
    ochunks = [(j, r0) for j in range(g_out.shape[0]) for r0 in range(0, ohr, ocr)]
    noc = len(ochunks)

    def body(*refs):
        d_refs = refs[:np_]
        h_ref, go_hbm, g_ref, rb_hbm, rbo_hbm, stage, ss, rs, oss, ors = refs[np_:]
        cb, k = pl.program_id(0), pl.program_id(1)
        cx, cy, cc = _coords()
        sib = (cx, cy, 1 - cc)

        def block_copy(b):
            return _rcopy(stage.at[b % 2],
                          rb_hbm.at[b // npc, :, pl.ds(pl.multiple_of((b % npc) * D, D), D)], ss.at[b], rs.at[b], sib)

        def out_copy(e):
            j, r0 = ochunks[e]
            return _rcopy(go_hbm.at[j, pl.ds((1 - cc) * ohr + r0, ocr), :], rbo_hbm.at[j, pl.ds(r0, ocr), :],
                          oss.at[e], ors.at[e], sib)

        @pl.when((cb == 0) & (k == 0))
        def _():
            for e in range(noc):
                out_copy(e).start()

        @pl.when(k == 0)
        def _():
            g_ref[...] = jnp.zeros_like(g_ref)

        for (s0, n), d_ref in zip(steps, d_refs):
            @pl.when((cb >= s0) & (cb < s0 + n))
            def _(d_ref=d_ref):
                g_ref[0] += _dot_tn(h_ref[...], d_ref[...])

        @pl.when((k == ns - 1) & (cb > 1))
        def _():
            block_copy(cb - 2).wait_send()

        @pl.when(k == ns - 1)
        def _():
            stage[cb % 2] = g_ref[0, pl.ds(pl.multiple_of((1 - cc) * hr, hr), hr), :]
            block_copy(cb).start()

        @pl.when((k == ns - 1) & (cb == nk - 1))
        def _():
            block_copy(nk - 2).wait_send()
            block_copy(nk - 1).wait_send()
            for b in range(nk):
                block_copy(b).wait_recv()
            for e in range(noc):
                out_copy(e).wait_recv()
                out_copy(e).wait_send()

    def piece_spec(s0, n):
        def imap(cb, k):
            active = (cb >= s0) & (cb < s0 + n)
            return (jnp.where(active, k, 0), jnp.clip(cb - s0, 0, n - 1))
        return pl.BlockSpec((tk, D), imap)

    any_ = pl.BlockSpec(memory_space=pl.ANY)
    return pl.pallas_call(
        body, name="inproj_bwd_w", grid=(nk, ns),
        in_specs=[piece_spec(s0, n) for s0, n in steps] + [pl.BlockSpec((tk, D), lambda cb, k: (k, 0)), any_],
        out_specs=(pl.BlockSpec((1, D, D), lambda cb, k: (cb // npc, 0, cb % npc)), any_, any_),
        out_shape=(jax.ShapeDtypeStruct((4, D, PW), F32), jax.ShapeDtypeStruct((4, hr, PW), F32),
                   jax.ShapeDtypeStruct((g_out.shape[0], ohr, g_out.shape[2]), F32)),
        scratch_shapes=[pltpu.VMEM((2, hr, D), F32), pltpu.SemaphoreType.DMA((nk,)), pltpu.SemaphoreType.DMA((nk,)),
                        pltpu.SemaphoreType.DMA((noc,)), pltpu.SemaphoreType.DMA((noc,))],
        compiler_params=_params(("arbitrary", "arbitrary")),
    )(*pieces, hbf, g_out)


D2D_CHUNK_BYTES = 512 * 1024


def _chunk_rows(a):
    return max(8, D2D_CHUNK_BYTES // (a.shape[-1] * a.dtype.itemsize))


def _pair_swap(arrs):
    na = len(arrs)
    chunks = []
    for t, a in enumerate(arrs):
        cr = _chunk_rows(a)
        chunks += [(t, r0, cr) for r0 in range(0, a.shape[0], cr)]
    nch = len(chunks)

    def body(*refs):
        a_refs = refs[:na]
        o_refs = refs[na:2 * na]
        ss, rs = refs[2 * na:]
        x, y, c = _coords()
        sib = (x, y, 1 - c)
        rcs = []
        for n, (t, r0, cr) in enumerate(chunks):
            rows = pl.ds(r0, cr)
            rc = _rcopy(a_refs[t].at[rows, :], o_refs[t].at[rows, :], ss.at[n], rs.at[n], sib)
            rc.start()
            rcs.append(rc)
        for rc in rcs:
            rc.wait_recv()
        for rc in rcs:
            rc.wait_send()

    any_ = pl.BlockSpec(memory_space=pl.ANY)
    return pl.pallas_call(
        body, name="pair_swap",
        out_shape=tuple(jax.ShapeDtypeStruct(a.shape, a.dtype) for a in arrs),
        in_specs=[any_] * na, out_specs=tuple([any_] * na),
        scratch_shapes=[pltpu.SemaphoreType.DMA((nch,)), pltpu.SemaphoreType.DMA((nch,))],
        compiler_params=_params(),
    )(*arrs)


def _add_half(full, rb, core, tr):
    n, r, cdim = full.shape
    nb = r // 2 // tr

    def body(c_ref, a_ref, b_ref, o_ref, ob_ref):
        tot = a_ref[...] + b_ref[...]
        o_ref[...] = tot
        ob_ref[...] = tot.astype(BF16)

    mine = pl.BlockSpec((1, tr, cdim), lambda i, j, c_ref: (i, c_ref[0] * nb + j, 0))
    spec = pl.BlockSpec((1, tr, cdim), lambda i, j, c_ref: (i, j, 0))
    return pl.pallas_call(
        body, name="add_half",
        grid_spec=pltpu.PrefetchScalarGridSpec(num_scalar_prefetch=1, grid=(n, nb), in_specs=[mine, spec],
                                               out_specs=(spec, spec)),
        out_shape=(jax.ShapeDtypeStruct(rb.shape, rb.dtype), jax.ShapeDtypeStruct(rb.shape, BF16)),
        compiler_params=_params(("parallel", "parallel")),
    )(core, full, rb)


def _sum_slots(q, r3, shard, tr):
    _, r, cdim = q.shape

    def body(j_ref, q_ref, r_ref, o_ref):
        o_ref[...] = ((q_ref[0] + r_ref[0].astype(F32)) + r_ref[1].astype(F32)) + r_ref[2].astype(F32)

    return pl.pallas_call(
        body, name="sum_slots",
        grid_spec=pltpu.PrefetchScalarGridSpec(
            num_scalar_prefetch=1, grid=(r // tr,),
            in_specs=[pl.BlockSpec((1, tr, cdim), lambda i, j_ref: (j_ref[0], i, 0)),
                      pl.BlockSpec((3, tr, cdim), lambda i, j_ref: (0, i, 0))],
            out_specs=pl.BlockSpec((tr, cdim), lambda i, j_ref: (i, 0))),
        out_shape=jax.ShapeDtypeStruct((r, cdim), q.dtype),
        compiler_params=_params(("parallel",)),
    )(shard, q, r3)


def _allreduce_small(packs):
    na = len(packs)

    def body(*refs):
        p_refs, o_refs, rbufs = refs[:na], refs[na:2 * na], refs[2 * na:3 * na]
        s1, r1, s2, r2 = refs[3 * na:]
        me = _my_index()
        sends = []

        def chunk(t, d):
            ch = p_refs[t].shape[0] // NDEV
            return pl.ds(pl.multiple_of(d * ch, 8), ch)

        for t in range(na):
            for k in range(1, NDEV):
                e = 7 * t + k - 1
                cp = _rcopy(p_refs[t].at[chunk(t, me ^ k)], rbufs[t].at[me], s1.at[e], r1.at[e], _peer(k))
                cp.start()
                sends.append(cp)
            rbufs[t][me] = p_refs[t][chunk(t, me), :]
        for t in range(na):
            for k in range(1, NDEV):
                e = 7 * t + k - 1
                _rcopy(p_refs[t].at[chunk(t, me)], rbufs[t].at[me ^ k], s1.at[e], r1.at[e], _peer(k)).wait_recv()
            tot = rbufs[t][0]
            for d in range(1, NDEV):
                tot = tot + rbufs[t][d]
            o_refs[t][chunk(t, me), :] = tot
            for k in range(1, NDEV):
                e = 7 * t + k - 1
                cp = _rcopy(o_refs[t].at[chunk(t, me)], o_refs[t].at[chunk(t, me)], s2.at[e], r2.at[e], _peer(k))
                cp.start()
                sends.append(cp)
        for t in range(na):
            for k in range(1, NDEV):
                e = 7 * t + k - 1
                _rcopy(o_refs[t].at[chunk(t, me)], o_refs[t].at[chunk(t, me ^ k)], s2.at[e], r2.at[e],
                       _peer(k)).wait_recv()
        for cp in sends:
            cp.wait_send()

    vm = pl.BlockSpec(memory_space=pltpu.VMEM)
    return pl.pallas_call(
        body, name="allreduce_small",
        out_shape=tuple(jax.ShapeDtypeStruct(a.shape, F32) for a in packs),
        in_specs=[vm] * na, out_specs=tuple([vm] * na),
        scratch_shapes=[pltpu.VMEM((NDEV, a.shape[0] // NDEV, a.shape[1]), F32) for a in packs] +
                       [pltpu.SemaphoreType.DMA((7 * na,)) for _ in range(4)],
        compiler_params=_params(),
    )(*packs)


def _adamw_update(w, g, m, v):
    nm = B1 * m + (1.0 - B1) * g
    nv = B2 * v + (1.0 - B2) * (g * g)
    m_hat = nm / (1.0 - B1 ** STEP)
    v_hat = nv / (1.0 - B2 ** STEP)
    return -LR * (m_hat / (jnp.sqrt(v_hat) + ADAM_EPS) + WD * w), nm, nv


def _adamw(w, g, m, v, tr):
    r, cdim = w.shape

    def body(w_ref, g_ref, m_ref, v_ref, d_ref, nm_ref, nv_ref):
        d_ref[...], nm_ref[...], nv_ref[...] = _adamw_update(w_ref[...], g_ref[...], m_ref[...], v_ref[...])

    spec = pl.BlockSpec((tr, cdim), lambda i: (i, 0))
    sd = jax.ShapeDtypeStruct((r, cdim), F32)
    return pl.pallas_call(
        body, name="adamw", grid=(r // tr,), in_specs=[spec] * 4, out_specs=(spec,) * 3, out_shape=(sd,) * 3,
        compiler_params=_params(("parallel",)),
    )(w, g, m, v)


V_G_NORM, V_DMOD, V_B_GATE, V_CONV_B, V_LAM, V_G_FINAL, V_CONV_W, V_LOSS, V_ROWS = 0, 1, 4, 6, 7, 8, 9, 13, 64
M_W_A, M_W_X, M_B_A, M_B_X, M_ROWS = 0, H * DH, 2 * H * DH, 2 * H * DH + H, 2112
SMALL = ("g_norm", "b_mod", "b_gate", "conv_b", "lam", "g_final", "conv_w", "w_a", "w_x", "b_a", "b_x")


def _adamw_small(redv, redm, g_conv, wmv):
    def grad(name, rv, rm, gc):
        if name == "g_norm":
            return rv[V_G_NORM:V_G_NORM + 1, :]
        if name == "b_mod":
            return jnp.concatenate([rv[V_DMOD + t:V_DMOD + t + 1, :] for t in range(3)], axis=1)
        if name == "b_gate":
            return jnp.concatenate([rv[V_B_GATE + t:V_B_GATE + t + 1, :] for t in range(2)], axis=1)
        if name == "conv_b":
            return rv[V_CONV_B:V_CONV_B + 1, :]
        if name == "lam":
            return rv[V_LAM:V_LAM + 1, :]
        if name == "g_final":
            return rv[V_G_FINAL:V_G_FINAL + 1, :]
        if name == "conv_w":
            return gc[...]
        if name == "w_a":
            return rm[M_W_A:M_W_A + H * DH, :]
        if name == "w_x":
            return rm[M_W_X:M_W_X + H * DH, :]
        if name == "b_a":
            return rm[M_B_A:M_B_A + H, :]
        return rm[M_B_X:M_B_X + H, :]

    n = len(SMALL)

    def body(*refs):
        rv, rm, gc = refs[:3]
        ins, outs = refs[3:3 + 3 * n], refs[3 + 3 * n:]
        for t, name in enumerate(SMALL):
            w_ref, m_ref, v_ref = ins[3 * t:3 * t + 3]
            g_out, d_out, m_out, v_out = outs[4 * t:4 * t + 4]
            g = grad(name, rv, rm, gc)
            g_out[...] = g
            d_out[...], m_out[...], v_out[...] = _adamw_update(w_ref[...], g, m_ref[...], v_ref[...])

    vm = pl.BlockSpec(memory_space=pltpu.VMEM)
    flat = [a for name in SMALL for a in wmv[name]]
    shapes = [jax.ShapeDtypeStruct(wmv[name][0].shape, F32) for name in SMALL for _ in range(4)]
    outs = pl.pallas_call(
        body, name="adamw_small", out_shape=tuple(shapes),
        in_specs=[vm] * (3 + len(flat)), out_specs=tuple([vm] * len(shapes)),
        compiler_params=_params(),
    )(redv, redm, g_conv, *flat)
    return {name: outs[4 * t:4 * t + 4] for t, name in enumerate(SMALL)}


def _rope_tables(positions):
    inv_freq = ROPE_THETA ** (-jnp.arange(0, ROT, 2, dtype=F32) / ROT)
    ang = positions.astype(F32)[:, None] * inv_freq
    cos, sin = jnp.cos(ang), jnp.sin(ang)
    n = positions.shape[0]
    half = ROT // 2
    rc = jnp.concatenate([cos, cos, jnp.ones((n, DH - ROT), F32)], axis=1)
    rsa = jnp.concatenate([-sin, jnp.zeros((n, DH - half), F32)], axis=1)
    rsb = jnp.concatenate([jnp.zeros((n, half), F32), sin, jnp.zeros((n, DH - ROT), F32)], axis=1)
    return rc, rsa, rsb


def kernel(x, c, positions, g_norm, w_mod, b_mod, w_in, b_gate, conv_w, conv_b, w_a, b_a, w_x, b_x, lam, w_out_rnn, w_out_attn, w_o, g_final, loss_target, m_g_norm, m_w_mod, m_b_mod, m_w_in, m_b_gate, m_conv_w, m_conv_b, m_w_a, m_b_a, m_w_x, m_b_x, m_lam, m_w_out_rnn, m_w_out_attn, m_w_o, m_g_final, v_g_norm, v_w_mod, v_b_mod, v_w_in, v_b_gate, v_conv_w, v_conv_b, v_w_a, v_b_a, v_w_x, v_b_x, v_lam, v_w_out_rnn, v_w_out_attn, v_w_o, v_g_final):
    s = x.shape[1]
    xi = lax.axis_index("x")
    yi = lax.axis_index("y")
    ci = lax.axis_index("c")
    shard = 2 * xi + yi
    x2d = x[0]
    tgt = loss_target[0]
    pos = positions[0]

    c_all, mod4, conv_all = _mod_fwd(c, w_mod[0], b_mod.reshape(4, 1, 768), conv_w[0])
    mod = mod4.reshape(1, 3 * D)
    shift, scale, gate = mod[:, :D], mod[:, D:2 * D], mod[:, 2 * D:]
    w3sh = jnp.concatenate([w_out_rnn[0], w_out_attn[0], w_o[0]], axis=0).astype(BF16)
    wsh = w_in[0].astype(BF16)
    conv_full = conv_all[0::2].transpose(1, 0, 2).reshape(4, D)

    order = jnp.stack([shard, shard ^ 2, shard ^ 1, shard ^ 3]).astype(jnp.int32)
    p, hbf, wg = _gather_norm_inproj(x2d, g_norm, shift, scale, wsh, order)
    wg = lax.dynamic_update_slice(wg, wsh[None], (shard, 0, 0))
    rc, rsa, rsb = _rope_tables(pos)
    pos_col = pos.reshape(s, 1)
    b_a3, b_x3 = b_a.reshape(H, 1, DH), b_x.reshape(H, 1, DH)
    hr, gr = _rnn_fwd(p, pos_col, conv_full, conv_b, w_a[0], b_a3, w_x[0], b_x3, lam)
    o, lse, ga, q_rot, k_rot, w3g = _attn_fwd(p, rc, rsa, rsb, w3sh)
    w3g = lax.dynamic_update_slice(w3g, w3sh[None], (shard, 0, 0))
    w3 = w3g.reshape(4, 3, 256, D).transpose(1, 0, 2, 3).reshape(3, D, D)

    dgr, dga, dc, dx2, vec_t, g_out = _tail(gr, ga, p, x2d, tgt, w3, b_gate, gate, g_final.reshape(1, D))

    dxr, dzr, g_wa, g_ba, g_wx, g_bx, g_lam, g_cw, g_cb = _rnn_bwd(
        p, hr, dgr, pos_col, conv_full, conv_b, w_a[0], b_a3, w_x[0], b_x3, lam)
    dq, dk, dv, dza = _attn_bwd(p, q_rot, k_rot, o, lse, dga, rc, rsa, rsb)

    pieces = [dxr, dzr, dq, dk, dv, dza, dc]
    g_win, rb_a, rb_b = _inproj_bwd_w(pieces, hbf, g_out)

    core = ci.reshape(1)
    shard1 = shard.reshape(1)
    (q_a, qh_a), (q_b, qh_b) = _add_half(g_win, rb_a, core, tr=256), _add_half(g_out, rb_b, core, tr=128)
    grad_x, vec_n, (r_a, r_b) = _inproj_bwd_x(pieces, wg, x2d, dx2, g_norm, scale, [qh_a, qh_b])
    f_a, f_b = _sum_slots(q_a, r_a, shard1, tr=256), _sum_slots(q_b, r_b, shard1, tr=128)
    s_a, s_b = _pair_swap([f_a, f_b])
    south = ci == 0
    grad_w_in = jnp.where(south, jnp.concatenate([f_a, s_a], axis=0), jnp.concatenate([s_a, f_a], axis=0))
    g3 = jnp.where(south, jnp.concatenate([f_b, s_b], axis=0), jnp.concatenate([s_b, f_b], axis=0)).reshape(3, 256, D)

    dmod_row = jnp.concatenate([vec_n[0:1], vec_n[1:2], vec_t[1:2]], axis=1)
    vpack = jnp.concatenate([
        vec_n[2:3],
        vec_n[0:2], vec_t[1:2],
        vec_t[2:4],
        g_cb.reshape(1, D),
        g_lam.reshape(1, D),
        vec_t[0:1],
        g_cw.transpose(1, 0, 2).reshape(4, D),
        vec_t[4:5],
        jnp.zeros((V_ROWS - 14, D), F32)], axis=0)
    mpack = jnp.concatenate([
        g_wa.reshape(H * DH, DH), g_wx.reshape(H * DH, DH), g_ba.reshape(H, DH), g_bx.reshape(H, DH),
        jnp.zeros((M_ROWS - 2 * H * DH - 2 * H, DH), F32)], axis=0)
    redv, redm = _allreduce_small([vpack, mpack])
    loss = redv[V_LOSS, 0]
    grad_w_mod = _mod_bwd(dmod_row.reshape(4, 1, 768), c_all)
    g_conv_sh = lax.dynamic_slice_in_dim(redv[V_CONV_W:V_CONV_W + 4], shard * 256, 256, axis=1)

    shape2d = dict(g_norm=(1, D), b_mod=(1, 3 * D), b_gate=(1, 2 * D), conv_b=(1, D), lam=(1, D), g_final=(1, D),
                   conv_w=(4, 256), w_a=(H * DH, DH), w_x=(H * DH, DH), b_a=(H, DH), b_x=(H, DH))
    given = dict(
        g_norm=(g_norm, m_g_norm, v_g_norm), b_mod=(b_mod, m_b_mod, v_b_mod), b_gate=(b_gate, m_b_gate, v_b_gate),
        conv_b=(conv_b, m_conv_b, v_conv_b), lam=(lam, m_lam, v_lam), g_final=(g_final, m_g_final, v_g_final),
        conv_w=(conv_w, m_conv_w, v_conv_w), w_a=(w_a, m_w_a, v_w_a), w_x=(w_x, m_w_x, v_w_x),
        b_a=(b_a, m_b_a, v_b_a), b_x=(b_x, m_b_x, v_b_x))
    small = _adamw_small(redv, redm, g_conv_sh,
                         {n: tuple(a.reshape(shape2d[n]) for a in given[n]) for n in SMALL})

    big_in = _adamw(w_in[0], grad_w_in, m_w_in[0], v_w_in[0], tr=256)
    big_mod = _adamw(w_mod[0], grad_w_mod, m_w_mod[0], v_w_mod[0], tr=256)
    w3f = jnp.concatenate([w_out_rnn[0], w_out_attn[0], w_o[0]], axis=0)
    m3f = jnp.concatenate([m_w_out_rnn[0], m_w_out_attn[0], m_w_o[0]], axis=0)
    v3f = jnp.concatenate([v_w_out_rnn[0], v_w_out_attn[0], v_w_o[0]], axis=0)
    big_out = _adamw(w3f, g3.reshape(768, D), m3f, v3f, tr=256)

    names = ["g_norm", "w_mod", "b_mod", "w_in", "b_gate", "conv_w", "conv_b", "w_a", "b_a", "w_x", "b_x", "lam",
             "w_out_rnn", "w_out_attn", "w_o", "g_final"]
    outs = []
    for idx in range(4):
        d = {n: small[n][idx].reshape(given[n][0].shape) for n in SMALL}
        if idx == 0:
            d.update(w_mod=grad_w_mod[None], w_in=grad_w_in[None],
                     w_out_rnn=g3[0][None], w_out_attn=g3[1][None], w_o=g3[2][None])
        else:
            d.update(w_mod=big_mod[idx - 1][None], w_in=big_in[idx - 1][None],
                     w_out_rnn=big_out[idx - 1][0:256][None], w_out_attn=big_out[idx - 1][256:512][None],
                     w_o=big_out[idx - 1][512:768][None])
        outs.append(d)
    flat = [d[n] for d in outs for n in names]
    return (loss, grad_x[None], *flat)
```

```python
import jax
import jax.numpy as jnp
from jax import lax
from jax.experimental import pallas as pl
from jax.experimental.pallas import tpu as pltpu

F32, BF16 = jnp.float32, jnp.bfloat16
MESH = pl.DeviceIdType.MESH
HIGHEST = lax.Precision.HIGHEST

D = 1024
H = 8
DH = 128
PW = 2048
EPS = 1e-6
LRU_C = 8.0
SCALE = DH ** -0.5
NEG = -1e30
SPAN = 2048
UB = 128
DILATIONS = (1, 4, 16)
UNIT_BATCH = 16
ROPE_THETA = 500000.0
ROT = 32

LR, B1, B2, ADAM_EPS, WD, STEP = 0.001, 0.9, 0.999, 1e-08, 0.01, 10

NDEV = 8


def _params(sem=None, vmem_mb=56):
    return pltpu.CompilerParams(dimension_semantics=sem, vmem_limit_bytes=vmem_mb * 2 ** 20)


def _coords():
    return lax.axis_index("x"), lax.axis_index("y"), lax.axis_index("c")


def _flip(v, bit):
    return 1 - v if bit else v


def _peer(k):
    x, y, c = _coords()
    return (_flip(x, (k >> 2) & 1), _flip(y, (k >> 1) & 1), _flip(c, k & 1))


def _my_index():
    x, y, c = _coords()
    return 4 * x + 2 * y + c


def _rcopy(src, dst, ssem, rsem, dev):
    return pltpu.make_async_remote_copy(src_ref=src, dst_ref=dst, send_sem=ssem, recv_sem=rsem,
                                        device_id=dev, device_id_type=MESH)


def _sigmoid(x):
    return jax.nn.sigmoid(x)


def _dot(a, b):
    return jnp.dot(a, b, preferred_element_type=F32)


def _dot_nt(a, b):
    return lax.dot_general(a, b, (((1,), (1,)), ((), ())), preferred_element_type=F32)


def _dot_tn(a, b):
    return lax.dot_general(a, b, (((0,), (0,)), ((), ())), preferred_element_type=F32)


def _colsum(a):
    return jnp.sum(a, axis=0, keepdims=True)


def _mod_fwd(c, w_mod_sh, b_mod4, conv_sh):
    def body(c_ref, w_ref, b_ref, cv_ref, call_ref, mod_ref, cvall_ref, rows_ref, cmat_ref, s1, r1, s2, r2, s3, r3):
        x, y, _ = _coords()
        me = _my_index()
        j = 2 * x + y
        call_ref[me] = c_ref[...]
        cvall_ref[me] = cv_ref[...]
        sends = []
        for k in range(1, NDEV):
            cp = _rcopy(call_ref.at[me], call_ref.at[me], s1.at[k - 1], r1.at[k - 1], _peer(k))
            cp.start()
            sends.append(cp)
            cp = _rcopy(cvall_ref.at[me], cvall_ref.at[me], s3.at[k - 1], r3.at[k - 1], _peer(k))
            cp.start()
            sends.append(cp)
        for k in range(1, NDEV):
            pk = me ^ k
            _rcopy(call_ref.at[pk], call_ref.at[pk], s1.at[k - 1], r1.at[k - 1], _peer(k)).wait_recv()
        for b in range(NDEV):
            cmat_ref[pl.ds(b, 1), :] = call_ref[b]
        cm = cmat_ref[...]
        act = cm * _sigmoid(cm)
        mp = jnp.dot(act, w_ref[...], preferred_element_type=F32, precision=HIGHEST) + b_ref[j]
        for b in range(NDEV):
            rows_ref[b] = mp[b:b + 1]
        mod_ref[j] = rows_ref[me]
        for q, k in enumerate((2, 4, 6)):
            cp = _rcopy(rows_ref.at[me ^ k], mod_ref.at[j], s2.at[q], r2.at[q], _peer(k))
            cp.start()
            sends.append(cp)
        for q, k in enumerate((2, 4, 6)):
            jq = j ^ (k >> 1)
            _rcopy(rows_ref.at[me], mod_ref.at[jq], s2.at[q], r2.at[q], _peer(k)).wait_recv()
        for k in range(1, NDEV):
            pk = me ^ k
            _rcopy(cvall_ref.at[pk], cvall_ref.at[pk], s3.at[k - 1], r3.at[k - 1], _peer(k)).wait_recv()
        for cp in sends:
            cp.wait_send()

    vm = pl.BlockSpec(memory_space=pltpu.VMEM)
    return pl.pallas_call(
        body, name="mod_fwd",
        out_shape=(jax.ShapeDtypeStruct((NDEV, 1, D), F32), jax.ShapeDtypeStruct((4, 1, 768), F32),
                   jax.ShapeDtypeStruct((NDEV,) + conv_sh.shape, F32)),
        in_specs=[vm, vm, vm, vm], out_specs=(vm, vm, vm),
        scratch_shapes=[pltpu.VMEM((NDEV, 1, 768), F32), pltpu.VMEM((NDEV, D), F32),
                        pltpu.SemaphoreType.DMA((7,)), pltpu.SemaphoreType.DMA((7,)),
                        pltpu.SemaphoreType.DMA((3,)), pltpu.SemaphoreType.DMA((3,)),
                        pltpu.SemaphoreType.DMA((7,)), pltpu.SemaphoreType.DMA((7,))],
        compiler_params=_params(),
    )(c, w_mod_sh, b_mod4, conv_sh)


def _mod_bwd(dmod4, gn_row, c_all):
    def body(d_ref, g_ref, call_ref, gw_ref, gb_ref, gg_ref, dall_ref, gall_ref, cmat_ref, dmat_ref, s1, r1, s2, r2):
        x, y, _ = _coords()
        me = _my_index()
        j = 2 * x + y
        dall_ref[me] = d_ref[...]
        gall_ref[me] = g_ref[...]
        sends = []
        for k in range(1, NDEV):
            for buf, ss, rs in ((dall_ref, s1, r1), (gall_ref, s2, r2)):
                cp = _rcopy(buf.at[me], buf.at[me], ss.at[k - 1], rs.at[k - 1], _peer(k))
                cp.start()
                sends.append(cp)
        for k in range(1, NDEV):
            pk = me ^ k
            for buf, ss, rs in ((dall_ref, s1, r1), (gall_ref, s2, r2)):
                _rcopy(buf.at[pk], buf.at[pk], ss.at[k - 1], rs.at[k - 1], _peer(k)).wait_recv()
        for cp in sends:
            cp.wait_send()
        gb, gg = dall_ref[0], gall_ref[0]
        for b in range(1, NDEV):
            gb = gb + dall_ref[b]
            gg = gg + gall_ref[b]
        gb_ref[...] = gb
        gg_ref[...] = gg
        for b in range(NDEV):
            cmat_ref[pl.ds(b, 1), :] = call_ref[b]
            dmat_ref[pl.ds(b, 1), :] = dall_ref[b, j]
        cm = cmat_ref[...]
        act = cm * _sigmoid(cm)
        gw_ref[...] = lax.dot_general(act, dmat_ref[...], (((0,), (0,)), ((), ())),
                                      preferred_element_type=F32, precision=HIGHEST)

    vm = pl.BlockSpec(memory_space=pltpu.VMEM)
    return pl.pallas_call(
        body, name="mod_bwd",
        out_shape=(jax.ShapeDtypeStruct((D, 768), F32), jax.ShapeDtypeStruct((4, 1, 768), F32),
                   jax.ShapeDtypeStruct((1, D), F32)),
        in_specs=[vm, vm, vm], out_specs=(vm, vm, vm),
        scratch_shapes=[pltpu.VMEM((NDEV, 4, 1, 768), F32), pltpu.VMEM((NDEV, 1, D), F32),
                        pltpu.VMEM((NDEV, D), F32), pltpu.VMEM((NDEV, 768), F32),
                        pltpu.SemaphoreType.DMA((7,)), pltpu.SemaphoreType.DMA((7,)),
                        pltpu.SemaphoreType.DMA((7,)), pltpu.SemaphoreType.DMA((7,))],
        compiler_params=_params(),
    )(dmod4, gn_row, c_all)


def _gather_norm_inproj(x, gn, shift, scale, wsh, order, tm=1024, tn=1024):
    s = x.shape[0]
    ni = s // tm
    npc = PW // tn
    rows, cols = wsh.shape
    half = rows // 2
    nch = 4
    cr = half // nch
    chips = ((1, 0), (0, 1), (1, 1))

    def body(ord_ref, x_ref, gn_ref, sh_ref, sc_ref, wsh_hbm, p_ref, h_ref, wg_hbm, hs_all, w_s, wsem, ss, rs):
        slot, i, col = pl.program_id(0), pl.program_id(1), pl.program_id(2)
        cx, cy, cc = _coords()
        j = 2 * cx + cy
        sib = (cx, cy, 1 - cc)
        mine = lambda n: pl.ds(cc * half + n * cr, cr)
        theirs = lambda n: pl.ds((1 - cc) * half + n * cr, cr)
        shard_of = lambda q: j ^ (2 * chips[q][0] + chips[q][1])

        def to_chip(q, n):
            e = nch * q + n
            return _rcopy(wsh_hbm.at[mine(n)], wg_hbm.at[j, mine(n)], ss.at[e], rs.at[e],
                          (_flip(cx, chips[q][0]), _flip(cy, chips[q][1]), cc))

        def from_chip(q, n):
            e = nch * q + n
            return _rcopy(wsh_hbm.at[mine(n)], wg_hbm.at[shard_of(q), mine(n)], ss.at[e], rs.at[e], sib)

        def to_sibling(q, n):
            e = 3 * nch + nch * q + n
            return _rcopy(wg_hbm.at[shard_of(q), mine(n)], wg_hbm.at[shard_of(q), mine(n)], ss.at[e], rs.at[e], sib)

        def from_sibling(q, n):
            e = 3 * nch + nch * q + n
            return _rcopy(wsh_hbm.at[mine(n)], wg_hbm.at[shard_of(q), theirs(n)], ss.at[e], rs.at[e], sib)

        def load(sl, src):
            cp = pltpu.make_async_copy(src, w_s.at[sl], wsem.at[sl])
            cp.start()
            cp.wait()

        first = (i == 0) & (col == 0)

        @pl.when(first & (slot == 0))
        def _():
            for n in range(nch):
                for q in (0, 1):
                    to_chip(q, n).start()
            load(0, wsh_hbm.at[pl.ds(0, D), :])

        @pl.when(first & (slot == 1))
        def _():
            for q in (0, 1):
                for n in range(nch):
                    from_chip(q, n).wait_recv()
                    to_sibling(q, n).start()
            for n in range(nch):
                to_chip(2, n).start()
            for n in range(nch):
                from_sibling(0, n).wait_recv()
            load(1, wg_hbm.at[shard_of(0), pl.ds(0, D), :])

        @pl.when(first & (slot == 2))
        def _():
            for n in range(nch):
                from_sibling(1, n).wait_recv()
            load(2, wg_hbm.at[shard_of(1), pl.ds(0, D), :])

        @pl.when(first & (slot == 3))
        def _():
            for n in range(nch):
                from_chip(2, n).wait_recv()
                to_sibling(2, n).start()
            for n in range(nch):
                from_sibling(2, n).wait_recv()
            load(3, wg_hbm.at[shard_of(2), pl.ds(0, D), :])
            for q in range(3):
                for n in range(nch):
                    to_chip(q, n).wait_send()
                    to_sibling(q, n).wait_send()

        @pl.when((slot == 0) & (col == 0))
        def _():
            xt = x_ref[...]
            rstd = lax.rsqrt(jnp.mean(xt * xt, axis=-1, keepdims=True) + EPS)
            h = ((xt * rstd * gn_ref[...]) * (1.0 + sc_ref[...]) + sh_ref[...]).astype(BF16)
            hs_all[i] = h
            h_ref[...] = h

        p_ref[0] = _dot(hs_all[i], w_s[slot, :, pl.ds(pl.multiple_of(col * tn, tn), tn)]).astype(BF16)

    row = pl.BlockSpec((1, D), lambda sl, i, col, o: (0, 0))
    x_rows = lambda sl, i, col, o: (jnp.where(sl == 0, i, ni - 1), 0)
    any_ = pl.BlockSpec(memory_space=pl.ANY)
    return pl.pallas_call(
        body, name="gather_norm_inproj",
        grid_spec=pltpu.PrefetchScalarGridSpec(
            num_scalar_prefetch=1, grid=(4, ni, npc),
            in_specs=[pl.BlockSpec((tm, D), x_rows), row, row, row, any_],
            out_specs=(pl.BlockSpec((1, tm, tn), lambda sl, i, col, o: (o[sl], i, col)),
                       pl.BlockSpec((tm, D), x_rows), any_),
            scratch_shapes=[pltpu.VMEM((ni, tm, D), BF16), pltpu.VMEM((4, D, PW), BF16),
                            pltpu.SemaphoreType.DMA((4,)),
                            pltpu.SemaphoreType.DMA((6 * nch,)), pltpu.SemaphoreType.DMA((6 * nch,))]),
        out_shape=(jax.ShapeDtypeStruct((4, s, PW), BF16), jax.ShapeDtypeStruct((s, D), BF16),
                   jax.ShapeDtypeStruct((4, rows, cols), wsh.dtype)),
        compiler_params=_params(("arbitrary", "arbitrary", "arbitrary")),
    )(order, x, gn, shift, scale, wsh)


def _shift_down(prev8, cur, d):
    t = cur.shape[0]
    c3 = cur.reshape(t // 8, 8, DH)
    rot = pltpu.roll(c3, d, 1)
    before = jnp.concatenate([pltpu.roll(prev8, d, 0).reshape(1, 8, DH), rot[:-1]], axis=0)
    rows = lax.broadcasted_iota(jnp.int32, c3.shape, 1)
    return jnp.where(rows >= d, rot, before).reshape(t, DH)


def _shift_up(cur, next8, d):
    t = cur.shape[0]
    c3 = cur.reshape(t // 8, 8, DH)
    rot = pltpu.roll(c3, 8 - d, 1)
    after = jnp.concatenate([rot[1:], pltpu.roll(next8, 8 - d, 0).reshape(1, 8, DH)], axis=0)
    rows = lax.broadcasted_iota(jnp.int32, c3.shape, 1)
    return jnp.where(rows < 8 - d, rot, after).reshape(t, DH)


def _rnn_gates(xr, prev8, cw, cb, wa, ba, wx, bx, lam, reset):
    xc = cw[3:4] * xr + cb
    for d in (1, 2, 3):
        xc = xc + cw[3 - d:4 - d] * _shift_down(prev8, xr, d)
    xcb = xc.astype(BF16)
    r = _sigmoid(_dot(xcb, wa.astype(BF16)) + ba)
    ig = _sigmoid(_dot(xcb, wx.astype(BF16)) + bx)
    nl = -lam
    sp = jnp.maximum(nl, 0.0) + jnp.log1p(jnp.exp(-jnp.abs(nl)))
    log_a = (-LRU_C * r) * sp
    a_raw = jnp.exp(log_a)
    a = jnp.where(reset, 0.0, a_raw)
    mult = jnp.where(reset, 1.0, jnp.sqrt(1.0 - a_raw * a_raw))
    return xc, r, ig, sp, a, mult


def _log_scan(a, b, axis, up):
    n = a.shape[axis]
    rows = lax.broadcasted_iota(jnp.int32, a.shape, axis)
    d = 1
    while d < n:
        m = rows < n - d if up else rows >= d
        shift = n - d if up else d
        a_s = pltpu.roll(a, shift, axis)
        b_s = pltpu.roll(b, shift, axis)
        b = jnp.where(m, a * b_s + b, b)
        a = jnp.where(m, a * a_s, a)
        d *= 2
    return a, b


def _scan(a, b, t, edge, up=False):
    g = t // 8
    a3, b3 = _log_scan(a.reshape(g, 8, DH), b.reshape(g, 8, DH), 1, up)
    last = 0 if up else 7
    ag, bg = _log_scan(a3[:, last, :], b3[:, last, :], 0, up)
    hg = ag * edge + bg
    grp = lax.broadcasted_iota(jnp.int32, hg.shape, 0)
    if up:
        cin = jnp.where(grp == g - 1, edge, pltpu.roll(hg, g - 1, 0))
        tail = hg[0:1]
    else:
        cin = jnp.where(grp == 0, edge, pltpu.roll(hg, 1, 0))
        tail = hg[g - 1:g]
    return (a3 * cin[:, None, :] + b3).reshape(t, DH), tail


def _rnn_fwd(p, pos, conv_w, conv_b, w_a, b_a, w_x, b_x, lam, tt=512):
    s = p.shape[1]
    nt = s // tt

    def body(xr_ref, z_ref, pos_ref, cw_ref, cb_ref, wa_ref, ba_ref, wx_ref, bx_ref, lam_ref,
             hr_ref, gr_ref, xprev, hprev):
        @pl.when(pl.program_id(1) == 0)
        def _():
            xprev[...] = jnp.zeros_like(xprev)
            hprev[...] = jnp.zeros_like(hprev)

        xr = xr_ref[0].astype(F32)
        z = z_ref[0].astype(F32)
        reset = pos_ref[...] > 0.5
        xc, r, ig, sp, a, mult = _rnn_gates(xr, xprev[...], cw_ref[...], cb_ref[...], wa_ref[0], ba_ref[0],
                                            wx_ref[0], bx_ref[0], lam_ref[...], reset)
        bx = mult * ig * xc
        h, h_last = _scan(a, bx, tt, hprev[0:1])
        xprev[...] = xr[tt - 8:]
        hprev[...] = jnp.broadcast_to(h_last, (8, DH))
        hr_ref[...] = h
        gr_ref[...] = (h * (z * _sigmoid(z))).astype(BF16)

    head_row = lambda hh, t: (0, hh)
    return pl.pallas_call(
        body, name="rnn_fwd", grid=(H, nt),
        in_specs=[pl.BlockSpec((1, tt, DH), lambda hh, t: (0, t, hh)),
                  pl.BlockSpec((1, tt, DH), lambda hh, t: (0, t, H + hh)),
                  pl.BlockSpec((tt, DH), lambda hh, t: (t, 0)),
                  pl.BlockSpec((4, DH), head_row), pl.BlockSpec((1, DH), head_row),
                  pl.BlockSpec((1, DH, DH), lambda hh, t: (hh, 0, 0)), pl.BlockSpec((1, 1, DH), lambda hh, t: (hh, 0, 0)),
                  pl.BlockSpec((1, DH, DH), lambda hh, t: (hh, 0, 0)), pl.BlockSpec((1, 1, DH), lambda hh, t: (hh, 0, 0)),
                  pl.BlockSpec((1, DH), head_row)],
        out_specs=(pl.BlockSpec((tt, DH), lambda hh, t: (t, hh)), pl.BlockSpec((tt, DH), lambda hh, t: (t, hh))),
        out_shape=(jax.ShapeDtypeStruct((s, D), F32), jax.ShapeDtypeStruct((s, D), BF16)),
        scratch_shapes=[pltpu.VMEM((8, DH), F32), pltpu.VMEM((8, DH), F32)],
        compiler_params=_params(("parallel", "arbitrary")),
    )(p, p, pos, conv_w, conv_b, w_a, b_a, w_x, b_x, lam)


def _rnn_bwd(p, hr, dgr, pos, conv_w, conv_b, w_a, b_a, w_x, b_x, lam, tt=512):
    s = p.shape[1]
    nt = s // tt
    t8 = tt // 8

    def body(xr_ref, z_ref, xp_ref, hr_ref, hp_ref, dg_ref, pos_ref, cw_ref, cb_ref, wa_ref, ba_ref, wx_ref, bx_ref,
             lam_ref, dxr_ref, dz_ref, gwa_ref, gba_ref, gwx_ref, gbx_ref, glam_ref, gcw_ref, gcb_ref,
             a_next, g_next, dxc_next):
        t = pl.program_id(1)
        has_prev = t < nt - 1

        @pl.when(t == 0)
        def _():
            a_next[...] = jnp.zeros_like(a_next)
            g_next[...] = jnp.zeros_like(g_next)
            dxc_next[...] = jnp.zeros_like(dxc_next)
            gwa_ref[...] = jnp.zeros_like(gwa_ref)
            gba_ref[...] = jnp.zeros_like(gba_ref)
            gwx_ref[...] = jnp.zeros_like(gwx_ref)
            gbx_ref[...] = jnp.zeros_like(gbx_ref)
            glam_ref[...] = jnp.zeros_like(glam_ref)
            gcw_ref[...] = jnp.zeros_like(gcw_ref)
            gcb_ref[...] = jnp.zeros_like(gcb_ref)

        xr = xr_ref[0].astype(F32)
        z = z_ref[0].astype(F32)
        hr_blk = hr_ref[...]
        dg = dg_ref[...]
        xprev = jnp.where(has_prev, xp_ref[0].astype(F32)[8:], 0.0)
        hprev8 = jnp.where(has_prev, hp_ref[...], 0.0)
        reset = pos_ref[...] > 0.5
        cw = cw_ref[...]
        wa = wa_ref[0]
        wx = wx_ref[0]
        lam_v = lam_ref[...]
        xc, r, ig, sp, a, mult = _rnn_gates(xr, xprev, cw, cb_ref[...], wa, ba_ref[0], wx, bx_ref[0], lam_v, reset)

        sz = _sigmoid(z)
        dh = dg * (z * sz)
        dz_ref[...] = (dg * hr_blk * (sz * (1.0 + z * (1.0 - sz)))).astype(BF16)

        an = _shift_up(a, a_next[...], 1)
        g, g_first = _scan(an, dh, tt, g_next[0:1], up=True)
        a_next[...] = jnp.broadcast_to(a[0:1], (8, DH))
        g_next[...] = jnp.broadcast_to(g_first, (8, DH))

        hm1 = _shift_down(hprev8, hr_blk, 1)
        da = g * hm1
        dmult = g * (ig * xc)
        di = g * (mult * xc)
        dxc = g * (mult * ig)
        dla = jnp.where(reset, 0.0, da * a - dmult * (a * a) / mult)
        dr = dla * (-LRU_C * sp)
        dsp = _colsum(dla * (-LRU_C * r))
        glam_ref[0] += dsp * (-_sigmoid(-lam_v))
        dpa = dr * r * (1.0 - r)
        dpx = di * ig * (1.0 - ig)
        dpab = dpa.astype(BF16)
        dpxb = dpx.astype(BF16)
        dxc = dxc + _dot_nt(dpab, wa.astype(BF16)) + _dot_nt(dpxb, wx.astype(BF16))
        xcb = xc.astype(BF16)
        gwa_ref[0] += _dot_tn(xcb, dpab)
        gwx_ref[0] += _dot_tn(xcb, dpxb)
        gba_ref[0] += _colsum(dpa)
        gbx_ref[0] += _colsum(dpx)

        dxr = cw[3:4] * dxc
        for d in (1, 2, 3):
            dxr = dxr + cw[3 - d:4 - d] * _shift_up(dxc, dxc_next[...], d)
        dxr_ref[...] = dxr.astype(BF16)
        dxc_next[...] = dxc[0:8]
        gcb_ref[0] += _colsum(dxc)
        gcw_ref[0, 3:4, :] += _colsum(xr * dxc)
        for d in (1, 2, 3):
            gcw_ref[0, 3 - d:4 - d, :] += _colsum(_shift_down(xprev, xr, d) * dxc)

    rt = lambda t: nt - 1 - t
    prev8 = lambda t: jnp.maximum(rt(t) * t8 - 1, 0)
    head_row = lambda hh, t: (0, hh)
    hsm = lambda hh, t: (hh, 0, 0)
    return pl.pallas_call(
        body, name="rnn_bwd", grid=(H, nt),
        in_specs=[pl.BlockSpec((1, tt, DH), lambda hh, t: (0, rt(t), hh)),
                  pl.BlockSpec((1, tt, DH), lambda hh, t: (0, rt(t), H + hh)),
                  pl.BlockSpec((1, 16, DH), lambda hh, t: (0, jnp.maximum(rt(t) * (tt // 16) - 1, 0), hh)),
                  pl.BlockSpec((tt, DH), lambda hh, t: (rt(t), hh)),
                  pl.BlockSpec((8, DH), lambda hh, t: (prev8(t), hh)),
                  pl.BlockSpec((tt, DH), lambda hh, t: (rt(t), hh)),
                  pl.BlockSpec((tt, DH), lambda hh, t: (rt(t), 0)),
                  pl.BlockSpec((4, DH), head_row), pl.BlockSpec((1, DH), head_row),
                  pl.BlockSpec((1, DH, DH), hsm), pl.BlockSpec((1, 1, DH), hsm),
                  pl.BlockSpec((1, DH, DH), hsm), pl.BlockSpec((1, 1, DH), hsm),
                  pl.BlockSpec((1, DH), head_row)],
        out_specs=(pl.BlockSpec((tt, DH), lambda hh, t: (rt(t), hh)), pl.BlockSpec((tt, DH), lambda hh, t: (rt(t), hh)),
                   pl.BlockSpec((1, DH, DH), hsm), pl.BlockSpec((1, 1, DH), hsm),
                   pl.BlockSpec((1, DH, DH), hsm), pl.BlockSpec((1, 1, DH), hsm),
                   pl.BlockSpec((1, 1, DH), hsm), pl.BlockSpec((1, 4, DH), hsm), pl.BlockSpec((1, 1, DH), hsm)),
        out_shape=(jax.ShapeDtypeStruct((s, D), BF16), jax.ShapeDtypeStruct((s, D), BF16),
                   jax.ShapeDtypeStruct((H, DH, DH), F32), jax.ShapeDtypeStruct((H, 1, DH), F32),
                   jax.ShapeDtypeStruct((H, DH, DH), F32), jax.ShapeDtypeStruct((H, 1, DH), F32),
                   jax.ShapeDtypeStruct((H, 1, DH), F32), jax.ShapeDtypeStruct((H, 4, DH), F32),
                   jax.ShapeDtypeStruct((H, 1, DH), F32)),
        scratch_shapes=[pltpu.VMEM((8, DH), F32), pltpu.VMEM((8, DH), F32), pltpu.VMEM((8, DH), F32)],
        compiler_params=_params(("parallel", "arbitrary")),
    )(p, p, p, hr, hr, dgr, pos, conv_w, conv_b, w_a, b_a, w_x, b_x, lam)


def _rope(t, c, sa, sb):
    return t * c + pltpu.roll(t, DH - ROT // 2, 1) * sa + pltpu.roll(t, ROT // 2, 1) * sb


def _rope_bwd(g, c, sa, sb):
    return g * c + pltpu.roll(g * sa, ROT // 2, 1) + pltpu.roll(g * sb, DH - ROT // 2, 1)


def _unit_bases(gi, u):
    dil = DILATIONS[gi]
    if dil == 1:
        return u * UB, SPAN + (u - 1) * UB, u == 0
    if dil == 4:
        blk, r = u // 4, u % 4
        return blk * 4 * UB + r, SPAN + (blk - 1) * 4 * UB + r, blk == 0
    return u, u, True


def _unit_slices(gi, u):
    dil = DILATIONS[gi]
    qb0, kb0, first = _unit_bases(gi, u)
    if dil == 1:
        if not isinstance(qb0, int):
            qb0, kb0 = pl.multiple_of(qb0, UB), pl.multiple_of(kb0, UB)
        return pl.ds(qb0, UB), pl.ds(kb0, 2 * UB), first
    return pl.ds(qb0, UB, stride=dil), pl.ds(kb0, 2 * UB, stride=dil), first


def _bdot(a, b):
    return lax.dot_general(a, b, (((2,), (1,)), ((0,), (0,))), preferred_element_type=F32)


def _bdot_nt(a, b):
    return lax.dot_general(a, b, (((2,), (2,)), ((0,), (0,))), preferred_element_type=F32)


def _bdot_tn(a, b):
    return lax.dot_general(a, b, (((1,), (1,)), ((0,), (0,))), preferred_element_type=F32)


def _band_mask(first_in_span, has_prev):
    qi = lax.broadcasted_iota(jnp.int32, (UB, 2 * UB), 0)
    ki = lax.broadcasted_iota(jnp.int32, (UB, 2 * UB), 1)
    dist = UB + qi - ki
    band = (dist >= 0) & (dist <= UB)
    return band & ((ki >= UB) | jnp.logical_not(first_in_span) | has_prev)


def _gather_halves(phase, src_hbm, dst_hbm, ss, rs):
    half = src_hbm.shape[0] // 2
    cx, cy, cc = _coords()
    j = 2 * cx + cy
    sib = (cx, cy, 1 - cc)
    mine = pl.ds(cc * half, half)
    theirs = pl.ds((1 - cc) * half, half)
    chips = ((1, 0), (0, 1), (1, 1))
    for q, (kx, ky) in enumerate(chips):
        jq = j ^ (2 * kx + ky)
        out = _rcopy(src_hbm.at[mine], dst_hbm.at[j, mine], ss.at[q], rs.at[q], (_flip(cx, kx), _flip(cy, ky), cc))
        landed = _rcopy(src_hbm.at[mine], dst_hbm.at[jq, mine], ss.at[q], rs.at[q], sib)
        onward = _rcopy(dst_hbm.at[jq, mine], dst_hbm.at[jq, mine], ss.at[3 + q], rs.at[3 + q], sib)
        from_sib = _rcopy(src_hbm.at[mine], dst_hbm.at[jq, theirs], ss.at[3 + q], rs.at[3 + q], sib)
        if phase == 0:
            out.start()
        elif phase == 1:
            landed.wait_recv()
            onward.start()
        else:
            from_sib.wait_recv()
            out.wait_send()
            onward.wait_send()


def _attn_fwd(p, rc, rsa, rsb, w3sh):
    s = p.shape[1]
    ns = s // SPAN
    nunit = SPAN // UB

    def body(q_ref, k_ref, v_ref, z_ref, c_ref, sa_ref, sb_ref, w3_hbm, o_ref, lse_ref, ga_ref, qro_ref, kro_ref,
             w3g_hbm, qr, kf, vf, acc, mm, ll, ss, rs):
        hh, n = pl.program_id(0), pl.program_id(1)
        for phase, at_head, at_span in ((0, 0, 0), (1, H // 2, 0), (2, H - 1, ns - 1)):
            @pl.when((hh == at_head) & (n == at_span))
            def _(phase=phase):
                _gather_halves(phase, w3_hbm, w3g_hbm, ss, rs)

        @pl.when(n == 0)
        def _():
            kf[0:SPAN] = jnp.zeros((SPAN, DH), F32)
            vf[0:SPAN] = jnp.zeros((SPAN, DH), F32)

        c, sa, sb = c_ref[...], sa_ref[...], sb_ref[...]
        q_rot = _rope(q_ref[0].astype(F32), c, sa, sb).astype(BF16)
        k_rot = _rope(k_ref[0].astype(F32), c, sa, sb).astype(BF16)
        qro_ref[...] = q_rot
        kro_ref[...] = k_rot
        qr[...] = q_rot.astype(F32)
        kf[SPAN:] = k_rot.astype(F32)
        vf[SPAN:] = v_ref[0].astype(F32)
        has_prev = n > 0

        for gi, dil in enumerate(DILATIONS):
            def trip(t, carry, gi=gi, dil=dil):
                qsls, ksls, firsts = [], [], []
                for b in range(UNIT_BATCH):
                    qsl, ksl, first = _unit_slices(gi, t * UNIT_BATCH + b)
                    qsls.append(qsl)
                    ksls.append(ksl)
                    firsts.append(first)
                qb = jnp.stack([qr[qsl, :].astype(BF16) for qsl in qsls])
                kb = jnp.stack([kf[ksl, :].astype(BF16) for ksl in ksls])
                vb = jnp.stack([vf[ksl, :].astype(BF16) for ksl in ksls])
                s_all = _bdot_nt(qb, kb)
                prs = []
                for b in range(UNIT_BATCH):
                    sc = jnp.where(_band_mask(firsts[b], has_prev), s_all[b] * SCALE, NEG)
                    m = jnp.max(sc, axis=-1, keepdims=True)
                    pr = jnp.exp(sc - m)
                    l = jnp.sum(pr, axis=-1, keepdims=True)
                    mm[gi, qsls[b], :] = jnp.broadcast_to(m, (UB, DH))
                    ll[gi, qsls[b], :] = jnp.broadcast_to(l, (UB, DH))
                    prs.append(pr.astype(BF16))
                o_all = _bdot(jnp.stack(prs), vb)
                for b in range(UNIT_BATCH):
                    acc[gi, qsls[b], :] = o_all[b]
                return carry

            lax.fori_loop(0, nunit // UNIT_BATCH, trip, 0)

        m_all =jnp.maximum(jnp.maximum(mm[0], mm[1]), mm[2])
        num = jnp.zeros((SPAN, DH), F32)
        den = jnp.zeros((SPAN, DH), F32)
        for gi in range(3):
            w = jnp.exp(mm[gi] - m_all)
            num = num + w * acc[gi]
            den = den + w * ll[gi]
        o = num / den
        o_ref[...] = o
        lse_ref[...] = m_all + jnp.log(den)
        z = z_ref[0].astype(F32)
        ga_ref[...] = (o * (z * _sigmoid(z))).astype(BF16)
        kf[0:SPAN] = kf[SPAN:]
        vf[0:SPAN] = vf[SPAN:]

    blk = lambda piece, off: pl.BlockSpec((1, SPAN, DH), lambda hh, n: (piece, n, off + hh))
    tab = pl.BlockSpec((SPAN, DH), lambda hh, n: (n, 0))
    outb = pl.BlockSpec((SPAN, DH), lambda hh, n: (n, hh))
    any_ = pl.BlockSpec(memory_space=pl.ANY)
    return pl.pallas_call(
        body, name="attn_fwd", grid=(H, ns),
        in_specs=[blk(1, 0), blk(1, H), blk(2, 0), blk(2, H), tab, tab, tab, any_],
        out_specs=(outb, outb, outb, outb, outb, any_),
        out_shape=(jax.ShapeDtypeStruct((s, D), F32), jax.ShapeDtypeStruct((s, D), F32),
                   jax.ShapeDtypeStruct((s, D), BF16), jax.ShapeDtypeStruct((s, D), BF16),
                   jax.ShapeDtypeStruct((s, D), BF16), jax.ShapeDtypeStruct((4,) + w3sh.shape, w3sh.dtype)),
        scratch_shapes=[pltpu.VMEM((SPAN, DH), F32), pltpu.VMEM((2 * SPAN, DH), F32), pltpu.VMEM((2 * SPAN, DH), F32),
                        pltpu.VMEM((3, SPAN, DH), F32), pltpu.VMEM((3, SPAN, DH), F32), pltpu.VMEM((3, SPAN, DH), F32),
                        pltpu.SemaphoreType.DMA((6,)), pltpu.SemaphoreType.DMA((6,))],
        compiler_params=_params(("arbitrary", "arbitrary")),
    )(p, p, p, p, rc, rsa, rsb, w3sh)


def _attn_bwd(p, q_rot, k_rot, o, lse, dga, rc, rsa, rsb, packs):
    s = p.shape[1]
    ns = s // SPAN
    nunit = SPAN // UB
    npk = len(packs)

    def body(*refs):
        (q_ref, k_ref, kp_ref, v_ref, vp_ref, z_ref, c_ref, sa_ref, sb_ref, o_ref, lse_ref, dg_ref) = refs[:12]
        pk_refs = refs[12:12 + npk]
        dq_ref, dk_ref, dv_ref, dz_ref = refs[12 + npk:16 + npk]
        red_refs = refs[16 + npk:16 + 2 * npk]
        qr, kf, vf, dof, dlt, dqa, dkf, dvf = refs[16 + 2 * npk:24 + 2 * npk]
        rbufs = refs[24 + 2 * npk:24 + 3 * npk]
        sums = refs[24 + 3 * npk:24 + 4 * npk]
        ar_sems = refs[24 + 4 * npk:]
        hh, step = pl.program_id(0), pl.program_id(1)
        n = ns - 1 - step
        has_prev = n > 0
        for phase, at_head, at_step in ((0, 0, 0), (1, H // 2, 0), (2, H - 1, ns - 1)):
            @pl.when((hh == at_head) & (step == at_step))
            def _(phase=phase):
                _allreduce_phase(phase, pk_refs, sums, rbufs, *ar_sems, out_refs=red_refs)

        @pl.when(step == 0)
        def _():
            dkf[...] = jnp.zeros_like(dkf)
            dvf[...] = jnp.zeros_like(dvf)

        @pl.when(step > 0)
        def _():
            dkf[SPAN:] = dkf[0:SPAN]
            dvf[SPAN:] = dvf[0:SPAN]
            dkf[0:SPAN] = jnp.zeros((SPAN, DH), F32)
            dvf[0:SPAN] = jnp.zeros((SPAN, DH), F32)

        c, sa, sb = c_ref[...], sa_ref[...], sb_ref[...]
        qr[...] = q_ref[...].astype(F32)
        kf[SPAN:] = k_ref[...].astype(F32)
        vf[SPAN:] = v_ref[0].astype(F32)
        kf[0:SPAN] = jnp.where(has_prev, kp_ref[...].astype(F32), 0.0)
        vf[0:SPAN] = jnp.where(has_prev, vp_ref[0].astype(F32), 0.0)
        z = z_ref[0].astype(F32)
        sz = _sigmoid(z)
        dg = dg_ref[...]
        ov = o_ref[...]
        do = dg * (z * sz)
        dz_ref[...] = (dg * ov * (sz * (1.0 + z * (1.0 - sz)))).astype(BF16)
        dof[...] = do
        dlt[...] = jnp.dot(do * ov, jnp.ones((DH, DH), F32), preferred_element_type=F32, precision=HIGHEST)
        dqa[...] = jnp.zeros_like(dqa)

        for gi, dil in enumerate(DILATIONS):
            def trip(t, carry, gi=gi, dil=dil):
                qsls, ksls, firsts = [], [], []
                for b in range(UNIT_BATCH):
                    qsl, ksl, first = _unit_slices(gi, t * UNIT_BATCH + b)
                    qsls.append(qsl)
                    ksls.append(ksl)
                    firsts.append(first)
                qb = jnp.stack([qr[qsl, :].astype(BF16) for qsl in qsls])
                kb = jnp.stack([kf[ksl, :].astype(BF16) for ksl in ksls])
                vb = jnp.stack([vf[ksl, :].astype(BF16) for ksl in ksls])
                dob = jnp.stack([dof[qsl, :].astype(BF16) for qsl in qsls])
                s_all = _bdot_nt(qb, kb)
                dp_all = _bdot_nt(dob, vb)
                prs, dss = [], []
                for b in range(UNIT_BATCH):
                    lse_b = lse_ref[qsls[b], :]
                    dl_b = dlt[qsls[b], :]
                    pr = jnp.exp(s_all[b] * SCALE - jnp.concatenate([lse_b, lse_b], axis=1))
                    pr = jnp.where(_band_mask(firsts[b], has_prev), pr, 0.0)
                    prs.append(pr.astype(BF16))
                    dss.append((pr * (dp_all[b] - jnp.concatenate([dl_b, dl_b], axis=1)) * SCALE).astype(BF16))
                ds_all = jnp.stack(dss)
                dv_all = _bdot_tn(jnp.stack(prs), dob)
                dq_all = _bdot(ds_all, kb)
                dk_all = _bdot_tn(ds_all, qb)
                for b in range(UNIT_BATCH):
                    dvf[ksls[b], :] += dv_all[b]
                    dqa[qsls[b], :] += dq_all[b]
                    dkf[ksls[b], :] += dk_all[b]
                return carry

            lax.fori_loop(0, nunit // UNIT_BATCH, trip, 0)

        dq_ref[...] = _rope_bwd(dqa[...], c, sa, sb).astype(BF16)
        dk_ref[...] = _rope_bwd(dkf[SPAN:], c, sa, sb).astype(BF16)
        dv_ref[...] = dvf[SPAN:].astype(BF16)

    rn = lambda n: ns - 1 - n
    pn = lambda n: jnp.maximum(ns - 2 - n, 0)
    blk = lambda piece, off: pl.BlockSpec((1, SPAN, DH), lambda hh, n: (piece, rn(n), off + hh))
    blkp = lambda piece, off: pl.BlockSpec((1, SPAN, DH), lambda hh, n: (piece, pn(n), off + hh))
    tab = pl.BlockSpec((SPAN, DH), lambda hh, n: (rn(n), 0))
    io = pl.BlockSpec((SPAN, DH), lambda hh, n: (rn(n), hh))
    iop = pl.BlockSpec((SPAN, DH), lambda hh, n: (pn(n), hh))
    vm = pl.BlockSpec(memory_space=pltpu.VMEM)
    outs = pl.pallas_call(
        body, name="attn_bwd", grid=(H, ns),
        in_specs=[io, io, iop, blk(2, 0), blkp(2, 0), blk(2, H), tab, tab, tab, io, io, io] + [vm] * npk,
        out_specs=(io, io, io, io) + (vm,) * npk,
        out_shape=tuple(jax.ShapeDtypeStruct((s, D), BF16) for _ in range(4)) +
                  tuple(jax.ShapeDtypeStruct(a.shape, F32) for a in packs),
        scratch_shapes=[pltpu.VMEM((SPAN, DH), F32), pltpu.VMEM((2 * SPAN, DH), F32), pltpu.VMEM((2 * SPAN, DH), F32),
                        pltpu.VMEM((SPAN, DH), F32), pltpu.VMEM((SPAN, DH), F32), pltpu.VMEM((SPAN, DH), F32),
                        pltpu.VMEM((2 * SPAN, DH), F32), pltpu.VMEM((2 * SPAN, DH), F32)] +
                       [pltpu.VMEM((NDEV, a.shape[0] // NDEV, a.shape[1]), F32) for a in packs] +
                       [pltpu.VMEM(a.shape, F32) for a in packs] +
                       [pltpu.SemaphoreType.DMA((7 * npk,)) for _ in range(4)],
        compiler_params=_params(("arbitrary", "arbitrary")),
    )(q_rot, k_rot, k_rot, p, p, p, rc, rsa, rsb, o, lse, dga, *packs)
    return outs[:4], outs[4:]


def _tail(gr, ga, p, x, tgt, w3, b_gate, gate, g_final, tm=256):
    s = x.shape[0]
    nt = s // tm

    def body(gr_ref, ga_ref, pr_ref, pa_ref, x_ref, t_ref, bg_ref, gate_ref, gf_ref, w_hbm,
             dgr_ref, dga_ref, dc_ref, dx2_ref, vec_ref, go_hbm, w_s, acc_s, sem):
        i = pl.program_id(0)

        @pl.when(i == 0)
        def _():
            cp = pltpu.make_async_copy(w_hbm, w_s, sem.at[12])
            cp.start()
            acc_s[...] = jnp.zeros_like(acc_s)
            vec_ref[...] = jnp.zeros_like(vec_ref)
            cp.wait()

        grb = gr_ref[...]
        gab = ga_ref[...]
        bg = bg_ref[...]
        gate_v = gate_ref[...]
        gf = gf_ref[...]
        y_r = _dot(grb, w_s[0])
        y_a = _dot(gab, w_s[1])
        sr = _sigmoid(pr_ref[0].astype(F32) + bg[:, :D])
        sa = _sigmoid(pa_ref[0].astype(F32) + bg[:, D:])
        mb = (sr * y_r + sa * y_a).astype(BF16)
        u = _dot(mb, w_s[2])
        x2 = x_ref[...] + gate_v * u
        rstd = lax.rsqrt(jnp.mean(x2 * x2, axis=-1, keepdims=True) + EPS)
        xh = x2 * rstd
        e = xh * gf - t_ref[...]
        dy = e * (1.0 / D)
        dyg = dy * gf
        dx2 = rstd * (dyg - xh * jnp.mean(dyg * xh, axis=-1, keepdims=True))
        dx2_ref[...] = dx2
        dub = (dx2 * gate_v).astype(BF16)
        dm = _dot_nt(dub, w_s[2])
        dyr = (dm * sr).astype(BF16)
        dya = (dm * sa).astype(BF16)
        dpr = dm * y_r * (sr * (1.0 - sr))
        dpa = dm * y_a * (sa * (1.0 - sa))
        dc_ref[:, :D] = dpr.astype(BF16)
        dc_ref[:, D:] = dpa.astype(BF16)
        dgr_ref[...] = _dot_nt(dyr, w_s[0])
        dga_ref[...] = _dot_nt(dya, w_s[1])
        acc_s[0] += _dot_tn(grb, dyr)
        acc_s[1] += _dot_tn(gab, dya)
        acc_s[2] += _dot_tn(mb, dub)
        vec_ref[0:1, :] += _colsum(dy * xh)
        vec_ref[1:2, :] += _colsum(dx2 * u)
        vec_ref[2:3, :] += _colsum(dpr)
        vec_ref[3:4, :] += _colsum(dpa)
        vec_ref[4:5, :] += _colsum(e * e)

        @pl.when(i == nt - 1)
        def _():
            vec_ref[4:5, :] = jnp.broadcast_to(jnp.sum(vec_ref[4:5, :]) * (0.5 / D), (1, D))
            cps = []
            for w in range(3):
                for j in range(4):
                    cps.append(pltpu.make_async_copy(acc_s.at[w, pl.ds(256 * j, 256)],
                                                     go_hbm.at[j, pl.ds(256 * w, 256)], sem.at[4 * w + j]))
            for cp in cps:
                cp.start()
            for cp in cps:
                cp.wait()

    rowt = lambda i: (i, 0)
    row = lambda w: pl.BlockSpec((1, w), lambda i: (0, 0))
    any_ = pl.BlockSpec(memory_space=pl.ANY)
    return pl.pallas_call(
        body, name="tail", grid=(nt,),
        in_specs=[pl.BlockSpec((tm, D), rowt), pl.BlockSpec((tm, D), rowt),
                  pl.BlockSpec((1, tm, D), lambda i: (3, i, 0)), pl.BlockSpec((1, tm, D), lambda i: (3, i, 1)),
                  pl.BlockSpec((tm, D), rowt), pl.BlockSpec((tm, D), rowt),
                  row(2 * D), row(D), row(D), any_],
        out_specs=(pl.BlockSpec((tm, D), rowt), pl.BlockSpec((tm, D), rowt), pl.BlockSpec((tm, 2 * D), rowt),
                   pl.BlockSpec((tm, D), rowt), pl.BlockSpec((8, D), lambda i: (0, 0)), any_),
        out_shape=(jax.ShapeDtypeStruct((s, D), F32), jax.ShapeDtypeStruct((s, D), F32),
                   jax.ShapeDtypeStruct((s, 2 * D), BF16), jax.ShapeDtypeStruct((s, D), F32),
                   jax.ShapeDtypeStruct((8, D), F32), jax.ShapeDtypeStruct((4, 768, D), F32)),
        scratch_shapes=[pltpu.VMEM((3, D, D), BF16), pltpu.VMEM((3, D, D), F32), pltpu.SemaphoreType.DMA((13,))],
        compiler_params=_params(("arbitrary",)),
    )(gr, ga, p, p, x, tgt, b_gate, gate, g_final, w3)


def _pieces_steps(pieces):
    out, s0 = [], 0
    for a in pieces:
        n = a.shape[1] // D
        out.append((s0, n))
        s0 += n
    return out, s0


def _inproj_bwd_x(pieces, wg, x, dx2, gn, scale, sums, tm=512):
    s = x.shape[0]
    np_ = len(pieces)
    na = len(sums)
    ni = s // tm
    groups, cur, width = [], [], 0
    for t, a in enumerate(pieces):
        cur.append(t)
        width += a.shape[1]
        if width == PW:
            groups.append(cur)
            cur, width = [], 0
    assert len(groups) == 4 and not cur

    def body(*refs):
        d_refs = refs[:np_]
        w_hbm, x_ref, dx2_ref, gn_ref, sc_ref = refs[np_:np_ + 5]
        q_refs = refs[np_ + 5:np_ + 5 + na]
        gx_ref, vec_ref = refs[np_ + 5 + na:np_ + 7 + na]
        r_refs = refs[np_ + 7 + na:np_ + 7 + 2 * na]
        w_s, wsem, ss, rs = refs[np_ + 7 + 2 * na:]
        i = pl.program_id(0)

        def scatter_copies():
            cx, cy, cc = _coords()
            j = 2 * cx + cy
            cps = []
            for t, (q, r) in enumerate(zip(q_refs, r_refs)):
                for e, (kx, ky) in enumerate(((1, 0), (0, 1), (1, 1))):
                    cps.append(_rcopy(q.at[j ^ (2 * kx + ky)], r.at[e], ss.at[3 * t + e], rs.at[3 * t + e],
                                      (_flip(cx, kx), _flip(cy, ky), cc)))
            return cps

        def w_copy(pc):
            return pltpu.make_async_copy(w_hbm.at[pc, pl.ds(0, D), :], w_s.at[pc], wsem.at[pc])

        @pl.when(i == 0)
        def _():
            for pc in range(4):
                w_copy(pc).start()
            vec_ref[...] = jnp.zeros_like(vec_ref)
            for cp in scatter_copies():
                cp.start()

        dh = None
        for pc, group in enumerate(groups):
            @pl.when(i == 0)
            def _(pc=pc):
                w_copy(pc).wait()

            tiles = [d_refs[t][...] for t in group]
            lhs = tiles[0] if len(tiles) == 1 else jnp.concatenate(tiles, axis=1)
            part = _dot_nt(lhs, w_s[pc])
            dh = part if dh is None else dh + part

        xt = x_ref[...]
        rstd = lax.rsqrt(jnp.mean(xt * xt, axis=-1, keepdims=True) + EPS)
        xh = xt * rstd
        gn_v = gn_ref[...]
        sc1 = 1.0 + sc_ref[...]
        dhx = dh * xh
        vec_ref[0:1, :] += _colsum(dh)
        vec_ref[1:2, :] += _colsum(dhx) * gn_v
        vec_ref[2:3, :] += _colsum(dhx) * sc1
        dxh = dh * (gn_v * sc1)
        gx_ref[...] = rstd * (dxh - xh * jnp.mean(dxh * xh, axis=-1, keepdims=True)) + dx2_ref[...]

        @pl.when(i == ni - 1)
        def _():
            for cp in scatter_copies():
                cp.wait()

    rowt = lambda i: (i, 0)
    row = pl.BlockSpec((1, D), lambda i: (0, 0))
    any_ = pl.BlockSpec(memory_space=pl.ANY)
    outs = pl.pallas_call(
        body, name="inproj_bwd_x", grid=(ni,),
        in_specs=[pl.BlockSpec((tm, a.shape[1]), rowt) for a in pieces] +
                 [any_, pl.BlockSpec((tm, D), rowt), pl.BlockSpec((tm, D), rowt), row, row] + [any_] * na,
        out_specs=(pl.BlockSpec((tm, D), rowt), pl.BlockSpec((8, D), lambda i: (0, 0))) + (any_,) * na,
        out_shape=(jax.ShapeDtypeStruct((s, D), F32), jax.ShapeDtypeStruct((8, D), F32)) +
                  tuple(jax.ShapeDtypeStruct((3,) + q.shape[1:], q.dtype) for q in sums),
        scratch_shapes=[pltpu.VMEM((4, D, PW), BF16), pltpu.SemaphoreType.DMA((4,)),
                        pltpu.SemaphoreType.DMA((3 * na,)), pltpu.SemaphoreType.DMA((3 * na,))],
        compiler_params=_params(("arbitrary",)),
    )(*pieces, wg, x, dx2, gn, scale, *sums)
    return outs[0], outs[1], outs[2:]


def _inproj_bwd_w(pieces, hbf, g_out, tk=1024):
    s = hbf.shape[0]
    steps, nk = _pieces_steps(pieces)
    npc = PW // D
    ns = s // tk
    np_ = len(pieces)
    hr = D // 2
    ohr = g_out.shape[1] // 2
    ocr = _chunk_rows(g_out)
    ochunks = [(j, r0) for j in range(g_out.shape[0]) for r0 in range(0, ohr, ocr)]
    noc = len(ochunks)

    def body(*refs):
        d_refs = refs[:np_]
        h_ref, go_hbm, g_ref, rb_hbm, rbo_hbm, stage, ss, rs, oss, ors = refs[np_:]
        cb, k = pl.program_id(0), pl.program_id(1)
        cx, cy, cc = _coords()
        sib = (cx, cy, 1 - cc)

        def block_copy(b):
            return _rcopy(stage.at[b % 2],
                          rb_hbm.at[b // npc, :, pl.ds(pl.multiple_of((b % npc) * D, D), D)], ss.at[b], rs.at[b], sib)

        def out_copy(e):
            j, r0 = ochunks[e]
            return _rcopy(go_hbm.at[j, pl.ds((1 - cc) * ohr + r0, ocr), :], rbo_hbm.at[j, pl.ds(r0, ocr), :],
                          oss.at[e], ors.at[e], sib)

        @pl.when((cb == 0) & (k == 0))
        def _():
            for e in range(noc):
                out_copy(e).start()

        @pl.when(k == 0)
        def _():
            g_ref[...] = jnp.zeros_like(g_ref)

        for (s0, n), d_ref in zip(steps, d_refs):
            @pl.when((cb >= s0) & (cb < s0 + n))
            def _(d_ref=d_ref):
                g_ref[0] += _dot_tn(h_ref[...], d_ref[...])

        @pl.when((k == ns - 1) & (cb > 1))
        def _():
            block_copy(cb - 2).wait_send()

        @pl.when(k == ns - 1)
        def _():
            stage[cb % 2] = g_ref[0, pl.ds(pl.multiple_of((1 - cc) * hr, hr), hr), :]
            block_copy(cb).start()

        @pl.when((k == ns - 1) & (cb == nk - 1))
        def _():
            block_copy(nk - 2).wait_send()
            block_copy(nk - 1).wait_send()
            for b in range(nk):
                block_copy(b).wait_recv()
            for e in range(noc):
                out_copy(e).wait_recv()
                out_copy(e).wait_send()

    def piece_spec(s0, n):
        def imap(cb, k):
            active = (cb >= s0) & (cb < s0 + n)
            return (jnp.where(active, k, 0), jnp.clip(cb - s0, 0, n - 1))
        return pl.BlockSpec((tk, D), imap)

    any_ = pl.BlockSpec(memory_space=pl.ANY)
    return pl.pallas_call(
        body, name="inproj_bwd_w", grid=(nk, ns),
        in_specs=[piece_spec(s0, n) for s0, n in steps] + [pl.BlockSpec((tk, D), lambda cb, k: (k, 0)), any_],
        out_specs=(pl.BlockSpec((1, D, D), lambda cb, k: (cb // npc, 0, cb % npc)), any_, any_),
        out_shape=(jax.ShapeDtypeStruct((4, D, PW), F32), jax.ShapeDtypeStruct((4, hr, PW), F32),
                   jax.ShapeDtypeStruct((g_out.shape[0], ohr, g_out.shape[2]), F32)),
        scratch_shapes=[pltpu.VMEM((2, hr, D), F32), pltpu.SemaphoreType.DMA((nk,)), pltpu.SemaphoreType.DMA((nk,)),
                        pltpu.SemaphoreType.DMA((noc,)), pltpu.SemaphoreType.DMA((noc,))],
        compiler_params=_params(("arbitrary", "arbitrary")),
    )(*pieces, hbf, g_out)


D2D_CHUNK_BYTES = 512 * 1024


def _chunk_rows(a):
    return max(8, D2D_CHUNK_BYTES // (a.shape[-1] * a.dtype.itemsize))


def _pair_swap(arrs):
    na = len(arrs)
    chunks = []
    for t, a in enumerate(arrs):
        cr = _chunk_rows(a)
        chunks += [(t, r0, cr) for r0 in range(0, a.shape[0], cr)]
    nch = len(chunks)

    def body(*refs):
        a_refs = refs[:na]
        o_refs = refs[na:2 * na]
        ss, rs = refs[2 * na:]
        x, y, c = _coords()
        sib = (x, y, 1 - c)
        rcs = []
        for n, (t, r0, cr) in enumerate(chunks):
            rows = pl.ds(r0, cr)
            rc = _rcopy(a_refs[t].at[rows, :], o_refs[t].at[rows, :], ss.at[n], rs.at[n], sib)
            rc.start()
            rcs.append(rc)
        for rc in rcs:
            rc.wait_recv()
        for rc in rcs:
            rc.wait_send()

    any_ = pl.BlockSpec(memory_space=pl.ANY)
    return pl.pallas_call(
        body, name="pair_swap",
        out_shape=tuple(jax.ShapeDtypeStruct(a.shape, a.dtype) for a in arrs),
        in_specs=[any_] * na, out_specs=tuple([any_] * na),
        scratch_shapes=[pltpu.SemaphoreType.DMA((nch,)), pltpu.SemaphoreType.DMA((nch,))],
        compiler_params=_params(),
    )(*arrs)


def _add_half(full, rb, core, tr):
    n, r, cdim = full.shape
    nb = r // 2 // tr

    def body(c_ref, a_ref, b_ref, o_ref, ob_ref):
        tot = a_ref[...] + b_ref[...]
        o_ref[...] = tot
        ob_ref[...] = tot.astype(BF16)

    mine = pl.BlockSpec((1, tr, cdim), lambda i, j, c_ref: (i, c_ref[0] * nb + j, 0))
    spec = pl.BlockSpec((1, tr, cdim), lambda i, j, c_ref: (i, j, 0))
    return pl.pallas_call(
        body, name="add_half",
        grid_spec=pltpu.PrefetchScalarGridSpec(num_scalar_prefetch=1, grid=(n, nb), in_specs=[mine, spec],
                                               out_specs=(spec, spec)),
        out_shape=(jax.ShapeDtypeStruct(rb.shape, rb.dtype), jax.ShapeDtypeStruct(rb.shape, BF16)),
        compiler_params=_params(("parallel", "parallel")),
    )(core, full, rb)


def _sum_slots(q, r3, shard, tr):
    _, r, cdim = q.shape

    def body(j_ref, q_ref, r_ref, o_ref):
        o_ref[...] = ((q_ref[0] + r_ref[0].astype(F32)) + r_ref[1].astype(F32)) + r_ref[2].astype(F32)

    return pl.pallas_call(
        body, name="sum_slots",
        grid_spec=pltpu.PrefetchScalarGridSpec(
            num_scalar_prefetch=1, grid=(r // tr,),
            in_specs=[pl.BlockSpec((1, tr, cdim), lambda i, j_ref: (j_ref[0], i, 0)),
                      pl.BlockSpec((3, tr, cdim), lambda i, j_ref: (0, i, 0))],
            out_specs=pl.BlockSpec((tr, cdim), lambda i, j_ref: (i, 0))),
        out_shape=jax.ShapeDtypeStruct((r, cdim), q.dtype),
        compiler_params=_params(("parallel",)),
    )(shard, q, r3)


def _allreduce_phase(phase, p_refs, o_refs, rbufs, s1, r1, s2, r2, out_refs=None):
    me = _my_index()

    def chunk(t, d):
        ch = p_refs[t].shape[0] // NDEV
        return pl.ds(pl.multiple_of(d * ch, 8), ch)

    def scatter(t, k):
        e = 7 * t + k - 1
        return _rcopy(p_refs[t].at[chunk(t, me ^ k)], rbufs[t].at[me], s1.at[e], r1.at[e], _peer(k))

    def gather(t, k):
        e = 7 * t + k - 1
        return _rcopy(o_refs[t].at[chunk(t, me)], o_refs[t].at[chunk(t, me)], s2.at[e], r2.at[e], _peer(k))

    for t in range(len(p_refs)):
        if phase == 0:
            for k in range(1, NDEV):
                scatter(t, k).start()
            rbufs[t][me] = p_refs[t][chunk(t, me), :]
        elif phase == 1:
            for k in range(1, NDEV):
                e = 7 * t + k - 1
                _rcopy(p_refs[t].at[chunk(t, me)], rbufs[t].at[me ^ k], s1.at[e], r1.at[e], _peer(k)).wait_recv()
            tot = rbufs[t][0]
            for d in range(1, NDEV):
                tot = tot + rbufs[t][d]
            o_refs[t][chunk(t, me), :] = tot
            for k in range(1, NDEV):
                gather(t, k).start()
        else:
            for k in range(1, NDEV):
                e = 7 * t + k - 1
                _rcopy(o_refs[t].at[chunk(t, me)], o_refs[t].at[chunk(t, me ^ k)], s2.at[e], r2.at[e],
                       _peer(k)).wait_recv()
            for k in range(1, NDEV):
                scatter(t, k).wait_send()
                gather(t, k).wait_send()
            if out_refs is not None:
                out_refs[t][...] = o_refs[t][...]


def _adamw_update(w, g, m, v):
    nm = B1 * m + (1.0 - B1) * g
    nv = B2 * v + (1.0 - B2) * (g * g)
    m_hat = nm / (1.0 - B1 ** STEP)
    v_hat = nv / (1.0 - B2 ** STEP)
    return -LR * (m_hat / (jnp.sqrt(v_hat) + ADAM_EPS) + WD * w), nm, nv


def _adamw(w, g, m, v, tr):
    r, cdim = w.shape

    def body(w_ref, g_ref, m_ref, v_ref, d_ref, nm_ref, nv_ref):
        d_ref[...], nm_ref[...], nv_ref[...] = _adamw_update(w_ref[...], g_ref[...], m_ref[...], v_ref[...])

    spec = pl.BlockSpec((tr, cdim), lambda i: (i, 0))
    sd = jax.ShapeDtypeStruct((r, cdim), F32)
    return pl.pallas_call(
        body, name="adamw", grid=(r // tr,), in_specs=[spec] * 4, out_specs=(spec,) * 3, out_shape=(sd,) * 3,
        compiler_params=_params(("parallel",)),
    )(w, g, m, v)


V_B_GATE, V_CONV_B, V_LAM, V_G_FINAL, V_CONV_W, V_LOSS, V_ROWS = 0, 2, 3, 4, 5, 9, 64
M_W_A, M_W_X, M_B_A, M_B_X, M_ROWS = 0, H * DH, 2 * H * DH, 2 * H * DH + H, 2112
SMALL = ("g_norm", "b_mod", "b_gate", "conv_b", "lam", "g_final", "conv_w", "w_a", "w_x", "b_a", "b_x")


def _adamw_small(redv, redm, g_conv, g_gnorm, g_bmod, wmv):
    def grad(name, rv, rm, gc, gg, gb):
        if name == "g_norm":
            return gg[...]
        if name == "b_mod":
            return gb[...]
        if name == "b_gate":
            return jnp.concatenate([rv[V_B_GATE + t:V_B_GATE + t + 1, :] for t in range(2)], axis=1)
        if name == "conv_b":
            return rv[V_CONV_B:V_CONV_B + 1, :]
        if name == "lam":
            return rv[V_LAM:V_LAM + 1, :]
        if name == "g_final":
            return rv[V_G_FINAL:V_G_FINAL + 1, :]
        if name == "conv_w":
            return gc[...]
        if name == "w_a":
            return rm[M_W_A:M_W_A + H * DH, :]
        if name == "w_x":
            return rm[M_W_X:M_W_X + H * DH, :]
        if name == "b_a":
            return rm[M_B_A:M_B_A + H, :]
        return rm[M_B_X:M_B_X + H, :]

    n = len(SMALL)

    def body(*refs):
        rv, rm, gc, gg, gb = refs[:5]
        ins, outs = refs[5:5 + 3 * n], refs[5 + 3 * n:]
        for t, name in enumerate(SMALL):
            w_ref, m_ref, v_ref = ins[3 * t:3 * t + 3]
            g_out, d_out, m_out, v_out = outs[4 * t:4 * t + 4]
            g = grad(name, rv, rm, gc, gg, gb)
            g_out[...] = g
            d_out[...], m_out[...], v_out[...] = _adamw_update(w_ref[...], g, m_ref[...], v_ref[...])

    vm = pl.BlockSpec(memory_space=pltpu.VMEM)
    flat = [a for name in SMALL for a in wmv[name]]
    shapes = [jax.ShapeDtypeStruct(wmv[name][0].shape, F32) for name in SMALL for _ in range(4)]
    outs = pl.pallas_call(
        body, name="adamw_small", out_shape=tuple(shapes),
        in_specs=[vm] * (5 + len(flat)), out_specs=tuple([vm] * len(shapes)),
        compiler_params=_params(),
    )(redv, redm, g_conv, g_gnorm, g_bmod, *flat)
    return {name: outs[4 * t:4 * t + 4] for t, name in enumerate(SMALL)}


def _rope_tables(positions):
    inv_freq = ROPE_THETA ** (-jnp.arange(0, ROT, 2, dtype=F32) / ROT)
    ang = positions.astype(F32)[:, None] * inv_freq
    cos, sin = jnp.cos(ang), jnp.sin(ang)
    n = positions.shape[0]
    half = ROT // 2
    rc = jnp.concatenate([cos, cos, jnp.ones((n, DH - ROT), F32)], axis=1)
    rsa = jnp.concatenate([-sin, jnp.zeros((n, DH - half), F32)], axis=1)
    rsb = jnp.concatenate([jnp.zeros((n, half), F32), sin, jnp.zeros((n, DH - ROT), F32)], axis=1)
    return rc, rsa, rsb


def kernel(x, c, positions, g_norm, w_mod, b_mod, w_in, b_gate, conv_w, conv_b, w_a, b_a, w_x, b_x, lam, w_out_rnn, w_out_attn, w_o, g_final, loss_target, m_g_norm, m_w_mod, m_b_mod, m_w_in, m_b_gate, m_conv_w, m_conv_b, m_w_a, m_b_a, m_w_x, m_b_x, m_lam, m_w_out_rnn, m_w_out_attn, m_w_o, m_g_final, v_g_norm, v_w_mod, v_b_mod, v_w_in, v_b_gate, v_conv_w, v_conv_b, v_w_a, v_b_a, v_w_x, v_b_x, v_lam, v_w_out_rnn, v_w_out_attn, v_w_o, v_g_final):
    s = x.shape[1]
    xi = lax.axis_index("x")
    yi = lax.axis_index("y")
    ci = lax.axis_index("c")
    shard = 2 * xi + yi
    x2d = x[0]
    tgt = loss_target[0]
    pos = positions[0]

    c_all, mod4, conv_all = _mod_fwd(c, w_mod[0], b_mod.reshape(4, 1, 768), conv_w[0])
    mod = mod4.reshape(1, 3 * D)
    shift, scale, gate = mod[:, :D], mod[:, D:2 * D], mod[:, 2 * D:]
    w3sh = jnp.concatenate([w_out_rnn[0], w_out_attn[0], w_o[0]], axis=0).astype(BF16)
    wsh = w_in[0].astype(BF16)
    conv_full = conv_all[0::2].transpose(1, 0, 2).reshape(4, D)

    order = jnp.stack([shard, shard ^ 2, shard ^ 1, shard ^ 3]).astype(jnp.int32)
    p, hbf, wg = _gather_norm_inproj(x2d, g_norm, shift, scale, wsh, order)
    wg = lax.dynamic_update_slice(wg, wsh[None], (shard, 0, 0))
    rc, rsa, rsb = _rope_tables(pos)
    pos_col = jnp.broadcast_to((pos == 0).astype(F32)[:, None], (s, DH))
    b_a3, b_x3 = b_a.reshape(H, 1, DH), b_x.reshape(H, 1, DH)
    hr, gr = _rnn_fwd(p, pos_col, conv_full, conv_b, w_a[0], b_a3, w_x[0], b_x3, lam)
    o, lse, ga, q_rot, k_rot, w3g = _attn_fwd(p, rc, rsa, rsb, w3sh)
    w3g = lax.dynamic_update_slice(w3g, w3sh[None], (shard, 0, 0))
    w3 = w3g.reshape(4, 3, 256, D).transpose(1, 0, 2, 3).reshape(3, D, D)

    dgr, dga, dc, dx2, vec_t, g_out = _tail(gr, ga, p, x2d, tgt, w3, b_gate, gate, g_final.reshape(1, D))

    dxr, dzr, g_wa, g_ba, g_wx, g_bx, g_lam, g_cw, g_cb = _rnn_bwd(
        p, hr, dgr, pos_col, conv_full, conv_b, w_a[0], b_a3, w_x[0], b_x3, lam)
    vpack = jnp.concatenate([
        vec_t[2:4],
        g_cb.reshape(1, D),
        g_lam.reshape(1, D),
        vec_t[0:1],
        g_cw.transpose(1, 0, 2).reshape(4, D),
        vec_t[4:5],
        jnp.zeros((V_ROWS - 10, D), F32)], axis=0)
    mpack = jnp.concatenate([
        g_wa.reshape(H * DH, DH), g_wx.reshape(H * DH, DH), g_ba.reshape(H, DH), g_bx.reshape(H, DH),
        jnp.zeros((M_ROWS - 2 * H * DH - 2 * H, DH), F32)], axis=0)
    (dq, dk, dv, dza), (redv, redm) = _attn_bwd(p, q_rot, k_rot, o, lse, dga, rc, rsa, rsb, [vpack, mpack])

    pieces = [dxr, dzr, dq, dk, dv, dza, dc]
    g_win, rb_a, rb_b = _inproj_bwd_w(pieces, hbf, g_out)

    core = ci.reshape(1)
    shard1 = shard.reshape(1)
    (q_a, qh_a), (q_b, qh_b) = _add_half(g_win, rb_a, core, tr=256), _add_half(g_out, rb_b, core, tr=128)
    grad_x, vec_n, (r_a, r_b) = _inproj_bwd_x(pieces, wg, x2d, dx2, g_norm, scale, [qh_a, qh_b])
    f_a, f_b = _sum_slots(q_a, r_a, shard1, tr=256), _sum_slots(q_b, r_b, shard1, tr=128)
    s_a, s_b = _pair_swap([f_a, f_b])
    south = ci == 0
    grad_w_in = jnp.where(south, jnp.concatenate([f_a, s_a], axis=0), jnp.concatenate([s_a, f_a], axis=0))
    g3 = jnp.where(south, jnp.concatenate([f_b, s_b], axis=0), jnp.concatenate([s_b, f_b], axis=0)).reshape(3, 256, D)

    dmod_row = jnp.concatenate([vec_n[0:1], vec_n[1:2], vec_t[1:2]], axis=1)
    loss = redv[V_LOSS, 0]
    grad_w_mod, g_bmod4, g_gnorm = _mod_bwd(dmod_row.reshape(4, 1, 768), vec_n[2:3], c_all)
    g_conv_sh = lax.dynamic_slice_in_dim(redv[V_CONV_W:V_CONV_W + 4], shard * 256, 256, axis=1)

    shape2d = dict(g_norm=(1, D), b_mod=(1, 3 * D), b_gate=(1, 2 * D), conv_b=(1, D), lam=(1, D), g_final=(1, D),
                   conv_w=(4, 256), w_a=(H * DH, DH), w_x=(H * DH, DH), b_a=(H, DH), b_x=(H, DH))
    given = dict(
        g_norm=(g_norm, m_g_norm, v_g_norm), b_mod=(b_mod, m_b_mod, v_b_mod), b_gate=(b_gate, m_b_gate, v_b_gate),
        conv_b=(conv_b, m_conv_b, v_conv_b), lam=(lam, m_lam, v_lam), g_final=(g_final, m_g_final, v_g_final),
        conv_w=(conv_w, m_conv_w, v_conv_w), w_a=(w_a, m_w_a, v_w_a), w_x=(w_x, m_w_x, v_w_x),
        b_a=(b_a, m_b_a, v_b_a), b_x=(b_x, m_b_x, v_b_x))
    small = _adamw_small(redv, redm, g_conv_sh, g_gnorm, g_bmod4.reshape(1, 3 * D),
                         {n: tuple(a.reshape(shape2d[n]) for a in given[n]) for n in SMALL})

    big_in = _adamw(w_in[0], grad_w_in, m_w_in[0], v_w_in[0], tr=256)
    big_mod = _adamw(w_mod[0], grad_w_mod, m_w_mod[0], v_w_mod[0], tr=256)
    w3f = jnp.concatenate([w_out_rnn[0], w_out_attn[0], w_o[0]], axis=0)
    m3f = jnp.concatenate([m_w_out_rnn[0], m_w_out_attn[0], m_w_o[0]], axis=0)
    v3f = jnp.concatenate([v_w_out_rnn[0], v_w_out_attn[0], v_w_o[0]], axis=0)
    big_out = _adamw(w3f, g3.reshape(768, D), m3f, v3f, tr=256)

    names = ["g_norm", "w_mod", "b_mod", "w_in", "b_gate", "conv_w", "conv_b", "w_a", "b_a", "w_x", "b_x", "lam",
             "w_out_rnn", "w_out_attn", "w_o", "g_final"]
    outs = []
    for idx in range(4):
        d = {n: small[n][idx].reshape(given[n][0].shape) for n in SMALL}
        if idx == 0:
            d.update(w_mod=grad_w_mod[None], w_in=grad_w_in[None],
                     w_out_rnn=g3[0][None], w_out_attn=g3[1][None], w_o=g3[2][None])
        else:
            d.update(w_mod=big_mod[idx - 1][None], w_in=big_in[idx - 1][None],
                     w_out_rnn=big_out[idx - 1][0:256][None], w_out_attn=big_out[idx - 1][256:512][None],
                     w_o=big_out[idx - 1][512:768][None])
        outs.append(d)
    flat = [d[n] for d in outs for n in names]
    return (loss, grad_x[None], *flat)
```

```python
import jax
import jax.numpy as jnp
from jax import lax
from jax.experimental import pallas as pl
from jax.experimental.pallas import tpu as pltpu

F32, BF16 = jnp.float32, jnp.bfloat16
MESH = pl.DeviceIdType.MESH
HIGHEST = lax.Precision.HIGHEST

D = 1024
H = 8
DH = 128
PW = 2048
EPS = 1e-6
LRU_C = 8.0
SCALE = DH ** -0.5
NEG = -1e30
SPAN = 2048
UB = 128
DILATIONS = (1, 4, 16)
UNIT_BATCH = 16
UNIT_BATCH_FWD = 8
ROPE_THETA = 500000.0
ROT = 32

LR, B1, B2, ADAM_EPS, WD, STEP = 0.001, 0.9, 0.999, 1e-08, 0.01, 10

NDEV = 8


def _params(sem=None, vmem_mb=56):
    return pltpu.CompilerParams(dimension_semantics=sem, vmem_limit_bytes=vmem_mb * 2 ** 20)


def _coords():
    return lax.axis_index("x"), lax.axis_index("y"), lax.axis_index("c")


def _flip(v, bit):
    return 1 - v if bit else v


def _peer(k):
    x, y, c = _coords()
    return (_flip(x, (k >> 2) & 1), _flip(y, (k >> 1) & 1), _flip(c, k & 1))


def _my_index():
    x, y, c = _coords()
    return 4 * x + 2 * y + c


def _rcopy(src, dst, ssem, rsem, dev):
    return pltpu.make_async_remote_copy(src_ref=src, dst_ref=dst, send_sem=ssem, recv_sem=rsem,
                                        device_id=dev, device_id_type=MESH)


def _sigmoid(x):
    return jax.nn.sigmoid(x)


def _dot(a, b):
    return jnp.dot(a, b, preferred_element_type=F32)


def _dot_nt(a, b):
    return lax.dot_general(a, b, (((1,), (1,)), ((), ())), preferred_element_type=F32)


def _dot_tn(a, b):
    return lax.dot_general(a, b, (((0,), (0,)), ((), ())), preferred_element_type=F32)


def _colsum(a):
    return jnp.sum(a, axis=0, keepdims=True)


def _mod_fwd(c, w_mod_sh, b_mod4, conv_sh):
    def body(c_ref, w_ref, b_ref, cv_ref, call_ref, mod_ref, cvall_ref, rows_ref, cmat_ref, s1, r1, s2, r2, s3, r3):
        x, y, _ = _coords()
        me = _my_index()
        j = 2 * x + y
        call_ref[me] = c_ref[...]
        cvall_ref[me] = cv_ref[...]
        sends = []
        for k in range(1, NDEV):
            cp = _rcopy(call_ref.at[me], call_ref.at[me], s1.at[k - 1], r1.at[k - 1], _peer(k))
            cp.start()
            sends.append(cp)
            cp = _rcopy(cvall_ref.at[me], cvall_ref.at[me], s3.at[k - 1], r3.at[k - 1], _peer(k))
            cp.start()
            sends.append(cp)
        for k in range(1, NDEV):
            pk = me ^ k
            _rcopy(call_ref.at[pk], call_ref.at[pk], s1.at[k - 1], r1.at[k - 1], _peer(k)).wait_recv()
        for b in range(NDEV):
            cmat_ref[pl.ds(b, 1), :] = call_ref[b]
        cm = cmat_ref[...]
        act = cm * _sigmoid(cm)
        mp = jnp.dot(act, w_ref[...], preferred_element_type=F32, precision=HIGHEST) + b_ref[j]
        for b in range(NDEV):
            rows_ref[b] = mp[b:b + 1]
        mod_ref[j] = rows_ref[me]
        for q, k in enumerate((2, 4, 6)):
            cp = _rcopy(rows_ref.at[me ^ k], mod_ref.at[j], s2.at[q], r2.at[q], _peer(k))
            cp.start()
            sends.append(cp)
        for q, k in enumerate((2, 4, 6)):
            jq = j ^ (k >> 1)
            _rcopy(rows_ref.at[me], mod_ref.at[jq], s2.at[q], r2.at[q], _peer(k)).wait_recv()
        for k in range(1, NDEV):
            pk = me ^ k
            _rcopy(cvall_ref.at[pk], cvall_ref.at[pk], s3.at[k - 1], r3.at[k - 1], _peer(k)).wait_recv()
        for cp in sends:
            cp.wait_send()

    vm = pl.BlockSpec(memory_space=pltpu.VMEM)
    return pl.pallas_call(
        body, name="mod_fwd",
        out_shape=(jax.ShapeDtypeStruct((NDEV, 1, D), F32), jax.ShapeDtypeStruct((4, 1, 768), F32),
                   jax.ShapeDtypeStruct((NDEV,) + conv_sh.shape, F32)),
        in_specs=[vm, vm, vm, vm], out_specs=(vm, vm, vm),
        scratch_shapes=[pltpu.VMEM((NDEV, 1, 768), F32), pltpu.VMEM((NDEV, D), F32),
                        pltpu.SemaphoreType.DMA((7,)), pltpu.SemaphoreType.DMA((7,)),
                        pltpu.SemaphoreType.DMA((3,)), pltpu.SemaphoreType.DMA((3,)),
                        pltpu.SemaphoreType.DMA((7,)), pltpu.SemaphoreType.DMA((7,))],
        compiler_params=_params(),
    )(c, w_mod_sh, b_mod4, conv_sh)


def _mod_bwd(dmod4, gn_row, c_all):
    def body(d_ref, g_ref, call_ref, gw_ref, gb_ref, gg_ref, dall_ref, gall_ref, cmat_ref, dmat_ref, s1, r1, s2, r2):
        x, y, _ = _coords()
        me = _my_index()
        j = 2 * x + y
        dall_ref[me] = d_ref[...]
        gall_ref[me] = g_ref[...]
        sends = []
        for k in range(1, NDEV):
            for buf, ss, rs in ((dall_ref, s1, r1), (gall_ref, s2, r2)):
                cp = _rcopy(buf.at[me], buf.at[me], ss.at[k - 1], rs.at[k - 1], _peer(k))
                cp.start()
                sends.append(cp)
        for k in range(1, NDEV):
            pk = me ^ k
            for buf, ss, rs in ((dall_ref, s1, r1), (gall_ref, s2, r2)):
                _rcopy(buf.at[pk], buf.at[pk], ss.at[k - 1], rs.at[k - 1], _peer(k)).wait_recv()
        for cp in sends:
            cp.wait_send()
        gb, gg = dall_ref[0], gall_ref[0]
        for b in range(1, NDEV):
            gb = gb + dall_ref[b]
            gg = gg + gall_ref[b]
        gb_ref[...] = gb
        gg_ref[...] = gg
        for b in range(NDEV):
            cmat_ref[pl.ds(b, 1), :] = call_ref[b]
            dmat_ref[pl.ds(b, 1), :] = dall_ref[b, j]
        cm = cmat_ref[...]
        act = cm * _sigmoid(cm)
        gw_ref[...] = lax.dot_general(act, dmat_ref[...], (((0,), (0,)), ((), ())),
                                      preferred_element_type=F32, precision=HIGHEST)

    vm = pl.BlockSpec(memory_space=pltpu.VMEM)
    return pl.pallas_call(
        body, name="mod_bwd",
        out_shape=(jax.ShapeDtypeStruct((D, 768), F32), jax.ShapeDtypeStruct((4, 1, 768), F32),
                   jax.ShapeDtypeStruct((1, D), F32)),
        in_specs=[vm, vm, vm], out_specs=(vm, vm, vm),
        scratch_shapes=[pltpu.VMEM((NDEV, 4, 1, 768), F32), pltpu.VMEM((NDEV, 1, D), F32),
                        pltpu.VMEM((NDEV, D), F32), pltpu.VMEM((NDEV, 768), F32),
                        pltpu.SemaphoreType.DMA((7,)), pltpu.SemaphoreType.DMA((7,)),
                        pltpu.SemaphoreType.DMA((7,)), pltpu.SemaphoreType.DMA((7,))],
        compiler_params=_params(),
    )(dmod4, gn_row, c_all)


def _gather_norm_inproj(x, gn, shift, scale, wsh, order, tm=1024, tn=1024):
    s = x.shape[0]
    ni = s // tm
    npc = PW // tn
    rows, cols = wsh.shape
    half = rows // 2
    nch = 4
    cr = half // nch
    chips = ((1, 0), (0, 1), (1, 1))

    def body(ord_ref, x_ref, gn_ref, sh_ref, sc_ref, wsh_hbm, p_ref, h_ref, wg_hbm, hs_all, w_s, wsem, ss, rs):
        slot, i, col = pl.program_id(0), pl.program_id(1), pl.program_id(2)
        cx, cy, cc = _coords()
        j = 2 * cx + cy
        sib = (cx, cy, 1 - cc)
        mine = lambda n: pl.ds(cc * half + n * cr, cr)
        theirs = lambda n: pl.ds((1 - cc) * half + n * cr, cr)
        shard_of = lambda q: j ^ (2 * chips[q][0] + chips[q][1])

        def to_chip(q, n):
            e = nch * q + n
            return _rcopy(wsh_hbm.at[mine(n)], wg_hbm.at[j, mine(n)], ss.at[e], rs.at[e],
                          (_flip(cx, chips[q][0]), _flip(cy, chips[q][1]), cc))

        def from_chip(q, n):
            e = nch * q + n
            return _rcopy(wsh_hbm.at[mine(n)], wg_hbm.at[shard_of(q), mine(n)], ss.at[e], rs.at[e], sib)

        def to_sibling(q, n):
            e = 3 * nch + nch * q + n
            return _rcopy(wg_hbm.at[shard_of(q), mine(n)], wg_hbm.at[shard_of(q), mine(n)], ss.at[e], rs.at[e], sib)

        def from_sibling(q, n):
            e = 3 * nch + nch * q + n
            return _rcopy(wsh_hbm.at[mine(n)], wg_hbm.at[shard_of(q), theirs(n)], ss.at[e], rs.at[e], sib)

        def load(sl, src):
            cp = pltpu.make_async_copy(src, w_s.at[sl], wsem.at[sl])
            cp.start()
            cp.wait()

        first = (i == 0) & (col == 0)

        @pl.when(first & (slot == 0))
        def _():
            for n in range(nch):
                for q in (0, 1):
                    to_chip(q, n).start()
            load(0, wsh_hbm.at[pl.ds(0, D), :])

        @pl.when(first & (slot == 1))
        def _():
            for q in (0, 1):
                for n in range(nch):
                    from_chip(q, n).wait_recv()
                    to_sibling(q, n).start()
            for n in range(nch):
                to_chip(2, n).start()
            for n in range(nch):
                from_sibling(0, n).wait_recv()
            load(1, wg_hbm.at[shard_of(0), pl.ds(0, D), :])

        @pl.when(first & (slot == 2))
        def _():
            for n in range(nch):
                from_sibling(1, n).wait_recv()
            load(2, wg_hbm.at[shard_of(1), pl.ds(0, D), :])

        @pl.when(first & (slot == 3))
        def _():
            for n in range(nch):
                from_chip(2, n).wait_recv()
                to_sibling(2, n).start()
            for n in range(nch):
                from_sibling(2, n).wait_recv()
            load(3, wg_hbm.at[shard_of(2), pl.ds(0, D), :])
            for q in range(3):
                for n in range(nch):
                    to_chip(q, n).wait_send()
                    to_sibling(q, n).wait_send()

        @pl.when((slot == 0) & (col == 0))
        def _():
            xt = x_ref[...]
            rstd = lax.rsqrt(jnp.mean(xt * xt, axis=-1, keepdims=True) + EPS)
            h = ((xt * rstd * gn_ref[...]) * (1.0 + sc_ref[...]) + sh_ref[...]).astype(BF16)
            hs_all[i] = h
            h_ref[...] = h

        p_ref[0] = _dot(hs_all[i], w_s[slot, :, pl.ds(pl.multiple_of(col * tn, tn), tn)]).astype(BF16)

    row = pl.BlockSpec((1, D), lambda sl, i, col, o: (0, 0))
    x_rows = lambda sl, i, col, o: (jnp.where(sl == 0, i, ni - 1), 0)
    any_ = pl.BlockSpec(memory_space=pl.ANY)
    return pl.pallas_call(
        body, name="gather_norm_inproj",
        grid_spec=pltpu.PrefetchScalarGridSpec(
            num_scalar_prefetch=1, grid=(4, ni, npc),
            in_specs=[pl.BlockSpec((tm, D), x_rows), row, row, row, any_],
            out_specs=(pl.BlockSpec((1, tm, tn), lambda sl, i, col, o: (o[sl], i, col)),
                       pl.BlockSpec((tm, D), x_rows), any_),
            scratch_shapes=[pltpu.VMEM((ni, tm, D), BF16), pltpu.VMEM((4, D, PW), BF16),
                            pltpu.SemaphoreType.DMA((4,)),
                            pltpu.SemaphoreType.DMA((6 * nch,)), pltpu.SemaphoreType.DMA((6 * nch,))]),
        out_shape=(jax.ShapeDtypeStruct((4, s, PW), BF16), jax.ShapeDtypeStruct((s, D), BF16),
                   jax.ShapeDtypeStruct((4, rows, cols), wsh.dtype)),
        compiler_params=_params(("arbitrary", "arbitrary", "arbitrary")),
    )(order, x, gn, shift, scale, wsh)


def _shift_down(prev8, cur, d):
    t = cur.shape[0]
    c3 = cur.reshape(t // 8, 8, DH)
    rot = pltpu.roll(c3, d, 1)
    before = jnp.concatenate([pltpu.roll(prev8, d, 0).reshape(1, 8, DH), rot[:-1]], axis=0)
    rows = lax.broadcasted_iota(jnp.int32, c3.shape, 1)
    return jnp.where(rows >= d, rot, before).reshape(t, DH)


def _shift_up(cur, next8, d):
    t = cur.shape[0]
    c3 = cur.reshape(t // 8, 8, DH)
    rot = pltpu.roll(c3, 8 - d, 1)
    after = jnp.concatenate([rot[1:], pltpu.roll(next8, 8 - d, 0).reshape(1, 8, DH)], axis=0)
    rows = lax.broadcasted_iota(jnp.int32, c3.shape, 1)
    return jnp.where(rows < 8 - d, rot, after).reshape(t, DH)


def _rnn_gates(xr, prev8, cw, cb, wa, ba, wx, bx, lam, reset):
    xc = cw[3:4] * xr + cb
    for d in (1, 2, 3):
        xc = xc + cw[3 - d:4 - d] * _shift_down(prev8, xr, d)
    xcb = xc.astype(BF16)
    r = _sigmoid(_dot(xcb, wa.astype(BF16)) + ba)
    ig = _sigmoid(_dot(xcb, wx.astype(BF16)) + bx)
    nl = -lam
    sp = jnp.maximum(nl, 0.0) + jnp.log1p(jnp.exp(-jnp.abs(nl)))
    log_a = (-LRU_C * r) * sp
    a_raw = jnp.exp(log_a)
    a = jnp.where(reset, 0.0, a_raw)
    mult = jnp.where(reset, 1.0, jnp.sqrt(1.0 - a_raw * a_raw))
    return xc, r, ig, sp, a, mult


def _log_scan(a, b, axis, up):
    n = a.shape[axis]
    rows = lax.broadcasted_iota(jnp.int32, a.shape, axis)
    d = 1
    while d < n:
        m = rows < n - d if up else rows >= d
        shift = n - d if up else d
        a_s = pltpu.roll(a, shift, axis)
        b_s = pltpu.roll(b, shift, axis)
        b = jnp.where(m, a * b_s + b, b)
        a = jnp.where(m, a * a_s, a)
        d *= 2
    return a, b


def _scan(a, b, t, edge, up=False):
    g = t // 8
    a3, b3 = _log_scan(a.reshape(g, 8, DH), b.reshape(g, 8, DH), 1, up)
    last = 0 if up else 7
    ag, bg = _log_scan(a3[:, last, :], b3[:, last, :], 0, up)
    hg = ag * edge + bg
    grp = lax.broadcasted_iota(jnp.int32, hg.shape, 0)
    if up:
        cin = jnp.where(grp == g - 1, edge, pltpu.roll(hg, g - 1, 0))
        tail = hg[0:1]
    else:
        cin = jnp.where(grp == 0, edge, pltpu.roll(hg, 1, 0))
        tail = hg[g - 1:g]
    return (a3 * cin[:, None, :] + b3).reshape(t, DH), tail


def _rnn_fwd(p, pos, conv_w, conv_b, w_a, b_a, w_x, b_x, lam, tt=512):
    s = p.shape[1]
    nt = s // tt

    def body(xr_ref, z_ref, pos_ref, cw_ref, cb_ref, wa_ref, ba_ref, wx_ref, bx_ref, lam_ref,
             hr_ref, gr_ref, xprev, hprev):
        @pl.when(pl.program_id(1) == 0)
        def _():
            xprev[...] = jnp.zeros_like(xprev)
            hprev[...] = jnp.zeros_like(hprev)

        xr = xr_ref[0].astype(F32)
        z = z_ref[0].astype(F32)
        reset = pos_ref[...] > 0.5
        xc, r, ig, sp, a, mult = _rnn_gates(xr, xprev[...], cw_ref[...], cb_ref[...], wa_ref[0], ba_ref[0],
                                            wx_ref[0], bx_ref[0], lam_ref[...], reset)
        bx = mult * ig * xc
        h, h_last = _scan(a, bx, tt, hprev[0:1])
        xprev[...] = xr[tt - 8:]
        hprev[...] = jnp.broadcast_to(h_last, (8, DH))
        hr_ref[...] = h
        gr_ref[...] = (h * (z * _sigmoid(z))).astype(BF16)

    head_row = lambda hh, t: (0, hh)
    return pl.pallas_call(
        body, name="rnn_fwd", grid=(H, nt),
        in_specs=[pl.BlockSpec((1, tt, DH), lambda hh, t: (0, t, hh)),
                  pl.BlockSpec((1, tt, DH), lambda hh, t: (0, t, H + hh)),
                  pl.BlockSpec((tt, DH), lambda hh, t: (t, 0)),
                  pl.BlockSpec((4, DH), head_row), pl.BlockSpec((1, DH), head_row),
                  pl.BlockSpec((1, DH, DH), lambda hh, t: (hh, 0, 0)), pl.BlockSpec((1, 1, DH), lambda hh, t: (hh, 0, 0)),
                  pl.BlockSpec((1, DH, DH), lambda hh, t: (hh, 0, 0)), pl.BlockSpec((1, 1, DH), lambda hh, t: (hh, 0, 0)),
                  pl.BlockSpec((1, DH), head_row)],
        out_specs=(pl.BlockSpec((tt, DH), lambda hh, t: (t, hh)), pl.BlockSpec((tt, DH), lambda hh, t: (t, hh))),
        out_shape=(jax.ShapeDtypeStruct((s, D), F32), jax.ShapeDtypeStruct((s, D), BF16)),
        scratch_shapes=[pltpu.VMEM((8, DH), F32), pltpu.VMEM((8, DH), F32)],
        compiler_params=_params(("parallel", "arbitrary")),
    )(p, p, pos, conv_w, conv_b, w_a, b_a, w_x, b_x, lam)


def _rnn_bwd(p, hr, dgr, pos, conv_w, conv_b, w_a, b_a, w_x, b_x, lam, tt=512):
    s = p.shape[1]
    nt = s // tt
    t8 = tt // 8

    def body(xr_ref, z_ref, xp_ref, hr_ref, hp_ref, dg_ref, pos_ref, cw_ref, cb_ref, wa_ref, ba_ref, wx_ref, bx_ref,
             lam_ref, dxr_ref, dz_ref, gwa_ref, gba_ref, gwx_ref, gbx_ref, glam_ref, gcw_ref, gcb_ref,
             a_next, g_next, dxc_next):
        t = pl.program_id(1)
        has_prev = t < nt - 1

        @pl.when(t == 0)
        def _():
            a_next[...] = jnp.zeros_like(a_next)
            g_next[...] = jnp.zeros_like(g_next)
            dxc_next[...] = jnp.zeros_like(dxc_next)
            gwa_ref[...] = jnp.zeros_like(gwa_ref)
            gba_ref[...] = jnp.zeros_like(gba_ref)
            gwx_ref[...] = jnp.zeros_like(gwx_ref)
            gbx_ref[...] = jnp.zeros_like(gbx_ref)
            glam_ref[...] = jnp.zeros_like(glam_ref)
            gcw_ref[...] = jnp.zeros_like(gcw_ref)
            gcb_ref[...] = jnp.zeros_like(gcb_ref)

        xr = xr_ref[0].astype(F32)
        z = z_ref[0].astype(F32)
        hr_blk = hr_ref[...]
        dg = dg_ref[...]
        xprev = jnp.where(has_prev, xp_ref[0].astype(F32)[8:], 0.0)
        hprev8 = jnp.where(has_prev, hp_ref[...], 0.0)
        reset = pos_ref[...] > 0.5
        cw = cw_ref[...]
        wa = wa_ref[0]
        wx = wx_ref[0]
        lam_v = lam_ref[...]
        xc, r, ig, sp, a, mult = _rnn_gates(xr, xprev, cw, cb_ref[...], wa, ba_ref[0], wx, bx_ref[0], lam_v, reset)

        sz = _sigmoid(z)
        dh = dg * (z * sz)
        dz_ref[...] = (dg * hr_blk * (sz * (1.0 + z * (1.0 - sz)))).astype(BF16)

        an = _shift_up(a, a_next[...], 1)
        g, g_first = _scan(an, dh, tt, g_next[0:1], up=True)
        a_next[...] = jnp.broadcast_to(a[0:1], (8, DH))
        g_next[...] = jnp.broadcast_to(g_first, (8, DH))

        hm1 = _shift_down(hprev8, hr_blk, 1)
        da = g * hm1
        dmult = g * (ig * xc)
        di = g * (mult * xc)
        dxc = g * (mult * ig)
        dla = jnp.where(reset, 0.0, da * a - dmult * (a * a) / mult)
        dr = dla * (-LRU_C * sp)
        dsp = _colsum(dla * (-LRU_C * r))
        glam_ref[0] += dsp * (-_sigmoid(-lam_v))
        dpa = dr * r * (1.0 - r)
        dpx = di * ig * (1.0 - ig)
        dpab = dpa.astype(BF16)
        dpxb = dpx.astype(BF16)
        dxc = dxc + _dot_nt(dpab, wa.astype(BF16)) + _dot_nt(dpxb, wx.astype(BF16))
        xcb = xc.astype(BF16)
        gwa_ref[0] += _dot_tn(xcb, dpab)
        gwx_ref[0] += _dot_tn(xcb, dpxb)
        gba_ref[0] += _colsum(dpa)
        gbx_ref[0] += _colsum(dpx)

        dxr = cw[3:4] * dxc
        for d in (1, 2, 3):
            dxr = dxr + cw[3 - d:4 - d] * _shift_up(dxc, dxc_next[...], d)
        dxr_ref[...] = dxr.astype(BF16)
        dxc_next[...] = dxc[0:8]
        gcb_ref[0] += _colsum(dxc)
        gcw_ref[0, 3:4, :] += _colsum(xr * dxc)
        for d in (1, 2, 3):
            gcw_ref[0, 3 - d:4 - d, :] += _colsum(_shift_down(xprev, xr, d) * dxc)

    rt = lambda t: nt - 1 - t
    prev8 = lambda t: jnp.maximum(rt(t) * t8 - 1, 0)
    head_row = lambda hh, t: (0, hh)
    hsm = lambda hh, t: (hh, 0, 0)
    return pl.pallas_call(
        body, name="rnn_bwd", grid=(H, nt),
        in_specs=[pl.BlockSpec((1, tt, DH), lambda hh, t: (0, rt(t), hh)),
                  pl.BlockSpec((1, tt, DH), lambda hh, t: (0, rt(t), H + hh)),
                  pl.BlockSpec((1, 16, DH), lambda hh, t: (0, jnp.maximum(rt(t) * (tt // 16) - 1, 0), hh)),
                  pl.BlockSpec((tt, DH), lambda hh, t: (rt(t), hh)),
                  pl.BlockSpec((8, DH), lambda hh, t: (prev8(t), hh)),
                  pl.BlockSpec((tt, DH), lambda hh, t: (rt(t), hh)),
                  pl.BlockSpec((tt, DH), lambda hh, t: (rt(t), 0)),
                  pl.BlockSpec((4, DH), head_row), pl.BlockSpec((1, DH), head_row),
                  pl.BlockSpec((1, DH, DH), hsm), pl.BlockSpec((1, 1, DH), hsm),
                  pl.BlockSpec((1, DH, DH), hsm), pl.BlockSpec((1, 1, DH), hsm),
                  pl.BlockSpec((1, DH), head_row)],
        out_specs=(pl.BlockSpec((tt, DH), lambda hh, t: (rt(t), hh)), pl.BlockSpec((tt, DH), lambda hh, t: (rt(t), hh)),
                   pl.BlockSpec((1, DH, DH), hsm), pl.BlockSpec((1, 1, DH), hsm),
                   pl.BlockSpec((1, DH, DH), hsm), pl.BlockSpec((1, 1, DH), hsm),
                   pl.BlockSpec((1, 1, DH), hsm), pl.BlockSpec((1, 4, DH), hsm), pl.BlockSpec((1, 1, DH), hsm)),
        out_shape=(jax.ShapeDtypeStruct((s, D), BF16), jax.ShapeDtypeStruct((s, D), BF16),
                   jax.ShapeDtypeStruct((H, DH, DH), F32), jax.ShapeDtypeStruct((H, 1, DH), F32),
                   jax.ShapeDtypeStruct((H, DH, DH), F32), jax.ShapeDtypeStruct((H, 1, DH), F32),
                   jax.ShapeDtypeStruct((H, 1, DH), F32), jax.ShapeDtypeStruct((H, 4, DH), F32),
                   jax.ShapeDtypeStruct((H, 1, DH), F32)),
        scratch_shapes=[pltpu.VMEM((8, DH), F32), pltpu.VMEM((8, DH), F32), pltpu.VMEM((8, DH), F32)],
        compiler_params=_params(("parallel", "arbitrary")),
    )(p, p, p, hr, hr, dgr, pos, conv_w, conv_b, w_a, b_a, w_x, b_x, lam)


def _rope(t, c, sa, sb):
    return t * c + pltpu.roll(t, DH - ROT // 2, 1) * sa + pltpu.roll(t, ROT // 2, 1) * sb


def _rope_bwd(g, c, sa, sb):
    return g * c + pltpu.roll(g * sa, ROT // 2, 1) + pltpu.roll(g * sb, DH - ROT // 2, 1)


def _unit_bases(gi, u):
    dil = DILATIONS[gi]
    if dil == 1:
        return u * UB, SPAN + (u - 1) * UB, u == 0
    if dil == 4:
        blk, r = u // 4, u % 4
        return blk * 4 * UB + r, SPAN + (blk - 1) * 4 * UB + r, blk == 0
    return u, u, True


def _unit_slices(gi, u):
    dil = DILATIONS[gi]
    qb0, kb0, first = _unit_bases(gi, u)
    if dil == 1:
        if not isinstance(qb0, int):
            qb0, kb0 = pl.multiple_of(qb0, UB), pl.multiple_of(kb0, UB)
        return pl.ds(qb0, UB), pl.ds(kb0, 2 * UB), first
    return pl.ds(qb0, UB, stride=dil), pl.ds(kb0, 2 * UB, stride=dil), first


def _bdot(a, b):
    return lax.dot_general(a, b, (((2,), (1,)), ((0,), (0,))), preferred_element_type=F32)


def _bdot_nt(a, b):
    return lax.dot_general(a, b, (((2,), (2,)), ((0,), (0,))), preferred_element_type=F32)


def _bdot_tn(a, b):
    return lax.dot_general(a, b, (((1,), (1,)), ((0,), (0,))), preferred_element_type=F32)


def _band_mask(first_in_span, has_prev):
    qi = lax.broadcasted_iota(jnp.int32, (UB, 2 * UB), 0)
    ki = lax.broadcasted_iota(jnp.int32, (UB, 2 * UB), 1)
    dist = UB + qi - ki
    band = (dist >= 0) & (dist <= UB)
    return band & ((ki >= UB) | jnp.logical_not(first_in_span) | has_prev)


def _gather_halves(phase, src_hbm, dst_hbm, ss, rs):
    half = src_hbm.shape[0] // 2
    cx, cy, cc = _coords()
    j = 2 * cx + cy
    sib = (cx, cy, 1 - cc)
    mine = pl.ds(cc * half, half)
    theirs = pl.ds((1 - cc) * half, half)
    chips = ((1, 0), (0, 1), (1, 1))
    for q, (kx, ky) in enumerate(chips):
        jq = j ^ (2 * kx + ky)
        out = _rcopy(src_hbm.at[mine], dst_hbm.at[j, mine], ss.at[q], rs.at[q], (_flip(cx, kx), _flip(cy, ky), cc))
        landed = _rcopy(src_hbm.at[mine], dst_hbm.at[jq, mine], ss.at[q], rs.at[q], sib)
        onward = _rcopy(dst_hbm.at[jq, mine], dst_hbm.at[jq, mine], ss.at[3 + q], rs.at[3 + q], sib)
        from_sib = _rcopy(src_hbm.at[mine], dst_hbm.at[jq, theirs], ss.at[3 + q], rs.at[3 + q], sib)
        if phase == 0:
            out.start()
        elif phase == 1:
            landed.wait_recv()
            onward.start()
        else:
            from_sib.wait_recv()
            out.wait_send()
            onward.wait_send()


def _attn_fwd(p, rc, rsa, rsb, w3sh):
    s = p.shape[1]
    ns = s // SPAN
    nunit = SPAN // UB

    def body(q_ref, k_ref, v_ref, z_ref, c_ref, sa_ref, sb_ref, w3_hbm, o_ref, lse_ref, ga_ref, qro_ref, kro_ref,
             w3g_hbm, qr, kf, vf, acc, mm, ll, ss, rs):
        hh, n = pl.program_id(0), pl.program_id(1)
        for phase, at_head, at_span in ((0, 0, 0), (1, H // 2, 0), (2, H - 1, ns - 1)):
            @pl.when((hh == at_head) & (n == at_span))
            def _(phase=phase):
                _gather_halves(phase, w3_hbm, w3g_hbm, ss, rs)

        @pl.when(n == 0)
        def _():
            kf[0:SPAN] = jnp.zeros((SPAN, DH), F32)
            vf[0:SPAN] = jnp.zeros((SPAN, DH), F32)

        c, sa, sb = c_ref[...], sa_ref[...], sb_ref[...]
        q_rot = _rope(q_ref[0].astype(F32), c, sa, sb).astype(BF16)
        k_rot = _rope(k_ref[0].astype(F32), c, sa, sb).astype(BF16)
        qro_ref[...] = q_rot
        kro_ref[...] = k_rot
        qr[...] = q_rot.astype(F32)
        kf[SPAN:] = k_rot.astype(F32)
        vf[SPAN:] = v_ref[0].astype(F32)
        has_prev = n > 0

        for gi, dil in enumerate(DILATIONS):
            def trip(t, carry, gi=gi, dil=dil):
                qsls, ksls, firsts = [], [], []
                for b in range(UNIT_BATCH_FWD):
                    qsl, ksl, first = _unit_slices(gi, t * UNIT_BATCH_FWD + b)
                    qsls.append(qsl)
                    ksls.append(ksl)
                    firsts.append(first)
                qb = jnp.stack([qr[qsl, :].astype(BF16) for qsl in qsls])
                kb = jnp.stack([kf[ksl, :].astype(BF16) for ksl in ksls])
                vb = jnp.stack([vf[ksl, :].astype(BF16) for ksl in ksls])
                s_all = _bdot_nt(qb, kb)
                prs = []
                for b in range(UNIT_BATCH_FWD):
                    sc = jnp.where(_band_mask(firsts[b], has_prev), s_all[b] * SCALE, NEG)
                    m = jnp.max(sc, axis=-1, keepdims=True)
                    pr = jnp.exp(sc - m)
                    l = jnp.sum(pr, axis=-1, keepdims=True)
                    mm[gi, qsls[b], :] = jnp.broadcast_to(m, (UB, DH))
                    ll[gi, qsls[b], :] = jnp.broadcast_to(l, (UB, DH))
                    prs.append(pr.astype(BF16))
                o_all = _bdot(jnp.stack(prs), vb)
                for b in range(UNIT_BATCH_FWD):
                    acc[gi, qsls[b], :] = o_all[b]
                return carry

            lax.fori_loop(0, nunit // UNIT_BATCH_FWD, trip, 0)

        m_all =jnp.maximum(jnp.maximum(mm[0], mm[1]), mm[2])
        num = jnp.zeros((SPAN, DH), F32)
        den = jnp.zeros((SPAN, DH), F32)
        for gi in range(3):
            w = jnp.exp(mm[gi] - m_all)
            num = num + w * acc[gi]
            den = den + w * ll[gi]
        o = num / den
        o_ref[...] = o
        lse_ref[...] = m_all + jnp.log(den)
        z = z_ref[0].astype(F32)
        ga_ref[...] = (o * (z * _sigmoid(z))).astype(BF16)
        kf[0:SPAN] = kf[SPAN:]
        vf[0:SPAN] = vf[SPAN:]

    blk = lambda piece, off: pl.BlockSpec((1, SPAN, DH), lambda hh, n: (piece, n, off + hh))
    tab = pl.BlockSpec((SPAN, DH), lambda hh, n: (n, 0))
    outb = pl.BlockSpec((SPAN, DH), lambda hh, n: (n, hh))
    any_ = pl.BlockSpec(memory_space=pl.ANY)
    return pl.pallas_call(
        body, name="attn_fwd", grid=(H, ns),
        in_specs=[blk(1, 0), blk(1, H), blk(2, 0), blk(2, H), tab, tab, tab, any_],
        out_specs=(outb, outb, outb, outb, outb, any_),
        out_shape=(jax.ShapeDtypeStruct((s, D), F32), jax.ShapeDtypeStruct((s, D), F32),
                   jax.ShapeDtypeStruct((s, D), BF16), jax.ShapeDtypeStruct((s, D), BF16),
                   jax.ShapeDtypeStruct((s, D), BF16), jax.ShapeDtypeStruct((4,) + w3sh.shape, w3sh.dtype)),
        scratch_shapes=[pltpu.VMEM((SPAN, DH), F32), pltpu.VMEM((2 * SPAN, DH), F32), pltpu.VMEM((2 * SPAN, DH), F32),
                        pltpu.VMEM((3, SPAN, DH), F32), pltpu.VMEM((3, SPAN, DH), F32), pltpu.VMEM((3, SPAN, DH), F32),
                        pltpu.SemaphoreType.DMA((6,)), pltpu.SemaphoreType.DMA((6,))],
        compiler_params=_params(("arbitrary", "arbitrary")),
    )(p, p, p, p, rc, rsa, rsb, w3sh)


def _attn_bwd(p, q_rot, k_rot, o, lse, dga, rc, rsa, rsb, packs):
    s = p.shape[1]
    ns = s // SPAN
    nunit = SPAN // UB
    npk = len(packs)

    def body(*refs):
        (q_ref, k_ref, kp_ref, v_ref, vp_ref, z_ref, c_ref, sa_ref, sb_ref, o_ref, lse_ref, dg_ref) = refs[:12]
        pk_refs = refs[12:12 + npk]
        dq_ref, dk_ref, dv_ref, dz_ref = refs[12 + npk:16 + npk]
        red_refs = refs[16 + npk:16 + 2 * npk]
        qr, kf, vf, dof, dlt, dqa, dkf, dvf = refs[16 + 2 * npk:24 + 2 * npk]
        rbufs = refs[24 + 2 * npk:24 + 3 * npk]
        sums = refs[24 + 3 * npk:24 + 4 * npk]
        ar_sems = refs[24 + 4 * npk:]
        hh, step = pl.program_id(0), pl.program_id(1)
        n = ns - 1 - step
        has_prev = n > 0
        for phase, at_head, at_step in ((0, 0, 0), (1, H // 2, 0), (2, H - 1, ns - 1)):
            @pl.when((hh == at_head) & (step == at_step))
            def _(phase=phase):
                _allreduce_phase(phase, pk_refs, sums, rbufs, *ar_sems, out_refs=red_refs)

        @pl.when(step == 0)
        def _():
            dkf[...] = jnp.zeros_like(dkf)
            dvf[...] = jnp.zeros_like(dvf)

        @pl.when(step > 0)
        def _():
            dkf[SPAN:] = dkf[0:SPAN]
            dvf[SPAN:] = dvf[0:SPAN]
            dkf[0:SPAN] = jnp.zeros((SPAN, DH), F32)
            dvf[0:SPAN] = jnp.zeros((SPAN, DH), F32)

        c, sa, sb = c_ref[...], sa_ref[...], sb_ref[...]
        qr[...] = q_ref[...].astype(F32)
        kf[SPAN:] = k_ref[...].astype(F32)
        vf[SPAN:] = v_ref[0].astype(F32)
        kf[0:SPAN] = jnp.where(has_prev, kp_ref[...].astype(F32), 0.0)
        vf[0:SPAN] = jnp.where(has_prev, vp_ref[0].astype(F32), 0.0)
        z = z_ref[0].astype(F32)
        sz = _sigmoid(z)
        dg = dg_ref[...]
        ov = o_ref[...]
        do = dg * (z * sz)
        dz_ref[...] = (dg * ov * (sz * (1.0 + z * (1.0 - sz)))).astype(BF16)
        dof[...] = do
        dlt[...] = jnp.dot(do * ov, jnp.ones((DH, DH), F32), preferred_element_type=F32, precision=HIGHEST)
        dqa[...] = jnp.zeros_like(dqa)

        for gi, dil in enumerate(DILATIONS):
            def trip(t, carry, gi=gi, dil=dil):
                qsls, ksls, firsts = [], [], []
                for b in range(UNIT_BATCH):
                    qsl, ksl, first = _unit_slices(gi, t * UNIT_BATCH + b)
                    qsls.append(qsl)
                    ksls.append(ksl)
                    firsts.append(first)
                qb = jnp.stack([qr[qsl, :].astype(BF16) for qsl in qsls])
                kb = jnp.stack([kf[ksl, :].astype(BF16) for ksl in ksls])
                vb = jnp.stack([vf[ksl, :].astype(BF16) for ksl in ksls])
                dob = jnp.stack([dof[qsl, :].astype(BF16) for qsl in qsls])
                s_all = _bdot_nt(qb, kb)
                dp_all = _bdot_nt(dob, vb)
                prs, dss = [], []
                for b in range(UNIT_BATCH):
                    lse_b = lse_ref[qsls[b], :]
                    dl_b = dlt[qsls[b], :]
                    pr = jnp.exp(s_all[b] * SCALE - jnp.concatenate([lse_b, lse_b], axis=1))
                    pr = jnp.where(_band_mask(firsts[b], has_prev), pr, 0.0)
                    prs.append(pr.astype(BF16))
                    dss.append((pr * (dp_all[b] - jnp.concatenate([dl_b, dl_b], axis=1)) * SCALE).astype(BF16))
                ds_all = jnp.stack(dss)
                dv_all = _bdot_tn(jnp.stack(prs), dob)
                dq_all = _bdot(ds_all, kb)
                dk_all = _bdot_tn(ds_all, qb)
                for b in range(UNIT_BATCH):
                    dvf[ksls[b], :] += dv_all[b]
                    dqa[qsls[b], :] += dq_all[b]
                    dkf[ksls[b], :] += dk_all[b]
                return carry

            lax.fori_loop(0, nunit // UNIT_BATCH, trip, 0)

        dq_ref[...] = _rope_bwd(dqa[...], c, sa, sb).astype(BF16)
        dk_ref[...] = _rope_bwd(dkf[SPAN:], c, sa, sb).astype(BF16)
        dv_ref[...] = dvf[SPAN:].astype(BF16)

    rn = lambda n: ns - 1 - n
    pn = lambda n: jnp.maximum(ns - 2 - n, 0)
    blk = lambda piece, off: pl.BlockSpec((1, SPAN, DH), lambda hh, n: (piece, rn(n), off + hh))
    blkp = lambda piece, off: pl.BlockSpec((1, SPAN, DH), lambda hh, n: (piece, pn(n), off + hh))
    tab = pl.BlockSpec((SPAN, DH), lambda hh, n: (rn(n), 0))
    io = pl.BlockSpec((SPAN, DH), lambda hh, n: (rn(n), hh))
    iop = pl.BlockSpec((SPAN, DH), lambda hh, n: (pn(n), hh))
    vm = pl.BlockSpec(memory_space=pltpu.VMEM)
    outs = pl.pallas_call(
        body, name="attn_bwd", grid=(H, ns),
        in_specs=[io, io, iop, blk(2, 0), blkp(2, 0), blk(2, H), tab, tab, tab, io, io, io] + [vm] * npk,
        out_specs=(io, io, io, io) + (vm,) * npk,
        out_shape=tuple(jax.ShapeDtypeStruct((s, D), BF16) for _ in range(4)) +
                  tuple(jax.ShapeDtypeStruct(a.shape, F32) for a in packs),
        scratch_shapes=[pltpu.VMEM((SPAN, DH), F32), pltpu.VMEM((2 * SPAN, DH), F32), pltpu.VMEM((2 * SPAN, DH), F32),
                        pltpu.VMEM((SPAN, DH), F32), pltpu.VMEM((SPAN, DH), F32), pltpu.VMEM((SPAN, DH), F32),
                        pltpu.VMEM((2 * SPAN, DH), F32), pltpu.VMEM((2 * SPAN, DH), F32)] +
                       [pltpu.VMEM((NDEV, a.shape[0] // NDEV, a.shape[1]), F32) for a in packs] +
                       [pltpu.VMEM(a.shape, F32) for a in packs] +
                       [pltpu.SemaphoreType.DMA((7 * npk,)) for _ in range(4)],
        compiler_params=_params(("arbitrary", "arbitrary")),
    )(q_rot, k_rot, k_rot, p, p, p, rc, rsa, rsb, o, lse, dga, *packs)
    return outs[:4], outs[4:]


def _tail(gr, ga, p, x, tgt, w3, b_gate, gate, g_final, tm=256):
    s = x.shape[0]
    nt = s // tm

    def body(gr_ref, ga_ref, pr_ref, pa_ref, x_ref, t_ref, bg_ref, gate_ref, gf_ref, w_hbm,
             dgr_ref, dga_ref, dc_ref, dx2_ref, vec_ref, go_hbm, w_s, acc_s, sem):
        i = pl.program_id(0)

        @pl.when(i == 0)
        def _():
            cp = pltpu.make_async_copy(w_hbm, w_s, sem.at[12])
            cp.start()
            acc_s[...] = jnp.zeros_like(acc_s)
            vec_ref[...] = jnp.zeros_like(vec_ref)
            cp.wait()

        grb = gr_ref[...]
        gab = ga_ref[...]
        bg = bg_ref[...]
        gate_v = gate_ref[...]
        gf = gf_ref[...]
        y_r = _dot(grb, w_s[0])
        y_a = _dot(gab, w_s[1])
        sr = _sigmoid(pr_ref[0].astype(F32) + bg[:, :D])
        sa = _sigmoid(pa_ref[0].astype(F32) + bg[:, D:])
        mb = (sr * y_r + sa * y_a).astype(BF16)
        u = _dot(mb, w_s[2])
        x2 = x_ref[...] + gate_v * u
        rstd = lax.rsqrt(jnp.mean(x2 * x2, axis=-1, keepdims=True) + EPS)
        xh = x2 * rstd
        e = xh * gf - t_ref[...]
        dy = e * (1.0 / D)
        dyg = dy * gf
        dx2 = rstd * (dyg - xh * jnp.mean(dyg * xh, axis=-1, keepdims=True))
        dx2_ref[...] = dx2
        dub = (dx2 * gate_v).astype(BF16)
        dm = _dot_nt(dub, w_s[2])
        dyr = (dm * sr).astype(BF16)
        dya = (dm * sa).astype(BF16)
        dpr = dm * y_r * (sr * (1.0 - sr))
        dpa = dm * y_a * (sa * (1.0 - sa))
        dc_ref[:, :D] = dpr.astype(BF16)
        dc_ref[:, D:] = dpa.astype(BF16)
        dgr_ref[...] = _dot_nt(dyr, w_s[0])
        dga_ref[...] = _dot_nt(dya, w_s[1])
        acc_s[0] += _dot_tn(grb, dyr)
        acc_s[1] += _dot_tn(gab, dya)
        acc_s[2] += _dot_tn(mb, dub)
        vec_ref[0:1, :] += _colsum(dy * xh)
        vec_ref[1:2, :] += _colsum(dx2 * u)
        vec_ref[2:3, :] += _colsum(dpr)
        vec_ref[3:4, :] += _colsum(dpa)
        vec_ref[4:5, :] += _colsum(e * e)

        @pl.when(i == nt - 1)
        def _():
            vec_ref[4:5, :] = jnp.broadcast_to(jnp.sum(vec_ref[4:5, :]) * (0.5 / D), (1, D))
            cps = []
            for w in range(3):
                for j in range(4):
                    cps.append(pltpu.make_async_copy(acc_s.at[w, pl.ds(256 * j, 256)],
                                                     go_hbm.at[j, pl.ds(256 * w, 256)], sem.at[4 * w + j]))
            for cp in cps:
                cp.start()
            for cp in cps:
                cp.wait()

    rowt = lambda i: (i, 0)
    row = lambda w: pl.BlockSpec((1, w), lambda i: (0, 0))
    any_ = pl.BlockSpec(memory_space=pl.ANY)
    return pl.pallas_call(
        body, name="tail", grid=(nt,),
        in_specs=[pl.BlockSpec((tm, D), rowt), pl.BlockSpec((tm, D), rowt),
                  pl.BlockSpec((1, tm, D), lambda i: (3, i, 0)), pl.BlockSpec((1, tm, D), lambda i: (3, i, 1)),
                  pl.BlockSpec((tm, D), rowt), pl.BlockSpec((tm, D), rowt),
                  row(2 * D), row(D), row(D), any_],
        out_specs=(pl.BlockSpec((tm, D), rowt), pl.BlockSpec((tm, D), rowt), pl.BlockSpec((tm, 2 * D), rowt),
                   pl.BlockSpec((tm, D), rowt), pl.BlockSpec((8, D), lambda i: (0, 0)), any_),
        out_shape=(jax.ShapeDtypeStruct((s, D), F32), jax.ShapeDtypeStruct((s, D), F32),
                   jax.ShapeDtypeStruct((s, 2 * D), BF16), jax.ShapeDtypeStruct((s, D), F32),
                   jax.ShapeDtypeStruct((8, D), F32), jax.ShapeDtypeStruct((4, 768, D), F32)),
        scratch_shapes=[pltpu.VMEM((3, D, D), BF16), pltpu.VMEM((3, D, D), F32), pltpu.SemaphoreType.DMA((13,))],
        compiler_params=_params(("arbitrary",)),
    )(gr, ga, p, p, x, tgt, b_gate, gate, g_final, w3)


def _pieces_steps(pieces):
    out, s0 = [], 0
    for a in pieces:
        n = a.shape[1] // D
        out.append((s0, n))
        s0 += n
    return out, s0


def _inproj_bwd_x(pieces, wg, x, dx2, gn, scale, sums, tm=512):
    s = x.shape[0]
    np_ = len(pieces)
    na = len(sums)
    ni = s // tm
    groups, cur, width = [], [], 0
    for t, a in enumerate(pieces):
        cur.append(t)
        width += a.shape[1]
        if width == PW:
            groups.append(cur)
            cur, width = [], 0
    assert len(groups) == 4 and not cur

    def body(*refs):
        d_refs = refs[:np_]
        w_hbm, x_ref, dx2_ref, gn_ref, sc_ref = refs[np_:np_ + 5]
        q_refs = refs[np_ + 5:np_ + 5 + na]
        gx_ref, vec_ref = refs[np_ + 5 + na:np_ + 7 + na]
        r_refs = refs[np_ + 7 + na:np_ + 7 + 2 * na]
        w_s, wsem, ss, rs = refs[np_ + 7 + 2 * na:]
        i = pl.program_id(0)

        def scatter_copies():
            cx, cy, cc = _coords()
            j = 2 * cx + cy
            cps = []
            for t, (q, r) in enumerate(zip(q_refs, r_refs)):
                for e, (kx, ky) in enumerate(((1, 0), (0, 1), (1, 1))):
                    cps.append(_rcopy(q.at[j ^ (2 * kx + ky)], r.at[e], ss.at[3 * t + e], rs.at[3 * t + e],
                                      (_flip(cx, kx), _flip(cy, ky), cc)))
            return cps

        def w_copy(pc):
            return pltpu.make_async_copy(w_hbm.at[pc, pl.ds(0, D), :], w_s.at[pc], wsem.at[pc])

        @pl.when(i == 0)
        def _():
            for pc in range(4):
                w_copy(pc).start()
            vec_ref[...] = jnp.zeros_like(vec_ref)
            for cp in scatter_copies():
                cp.start()

        dh = None
        for pc, group in enumerate(groups):
            @pl.when(i == 0)
            def _(pc=pc):
                w_copy(pc).wait()

            tiles = [d_refs[t][...] for t in group]
            lhs = tiles[0] if len(tiles) == 1 else jnp.concatenate(tiles, axis=1)
            part = _dot_nt(lhs, w_s[pc])
            dh = part if dh is None else dh + part

        xt = x_ref[...]
        rstd = lax.rsqrt(jnp.mean(xt * xt, axis=-1, keepdims=True) + EPS)
        xh = xt * rstd
        gn_v = gn_ref[...]
        sc1 = 1.0 + sc_ref[...]
        dhx = dh * xh
        vec_ref[0:1, :] += _colsum(dh)
        vec_ref[1:2, :] += _colsum(dhx) * gn_v
        vec_ref[2:3, :] += _colsum(dhx) * sc1
        dxh = dh * (gn_v * sc1)
        gx_ref[...] = rstd * (dxh - xh * jnp.mean(dxh * xh, axis=-1, keepdims=True)) + dx2_ref[...]

        @pl.when(i == ni - 1)
        def _():
            for cp in scatter_copies():
                cp.wait()

    rowt = lambda i: (i, 0)
    row = pl.BlockSpec((1, D), lambda i: (0, 0))
    any_ = pl.BlockSpec(memory_space=pl.ANY)
    outs = pl.pallas_call(
        body, name="inproj_bwd_x", grid=(ni,),
        in_specs=[pl.BlockSpec((tm, a.shape[1]), rowt) for a in pieces] +
                 [any_, pl.BlockSpec((tm, D), rowt), pl.BlockSpec((tm, D), rowt), row, row] + [any_] * na,
        out_specs=(pl.BlockSpec((tm, D), rowt), pl.BlockSpec((8, D), lambda i: (0, 0))) + (any_,) * na,
        out_shape=(jax.ShapeDtypeStruct((s, D), F32), jax.ShapeDtypeStruct((8, D), F32)) +
                  tuple(jax.ShapeDtypeStruct((3,) + q.shape[1:], q.dtype) for q in sums),
        scratch_shapes=[pltpu.VMEM((4, D, PW), BF16), pltpu.SemaphoreType.DMA((4,)),
                        pltpu.SemaphoreType.DMA((3 * na,)), pltpu.SemaphoreType.DMA((3 * na,))],
        compiler_params=_params(("arbitrary",)),
    )(*pieces, wg, x, dx2, gn, scale, *sums)
    return outs[0], outs[1], outs[2:]


def _inproj_bwd_w(pieces, hbf, g_out, tk=1024):
    s = hbf.shape[0]
    steps, nk = _pieces_steps(pieces)
    npc = PW // D
    ns = s // tk
    np_ = len(pieces)
    hr = D // 2
    ohr = g_out.shape[1] // 2
    ocr = _chunk_rows(g_out)
    ochunks = [(j, r0) for j in range(g_out.shape[0]) for r0 in range(0, ohr, ocr)]
    noc = len(ochunks)

    def body(*refs):
        d_refs = refs[:np_]
        h_ref, go_hbm, g_ref, rb_hbm, rbo_hbm, stage, ss, rs, oss, ors = refs[np_:]
        cb, k = pl.program_id(0), pl.program_id(1)
        cx, cy, cc = _coords()
        sib = (cx, cy, 1 - cc)

        def block_copy(b):
            return _rcopy(stage.at[b % 2],
                          rb_hbm.at[b // npc, :, pl.ds(pl.multiple_of((b % npc) * D, D), D)], ss.at[b], rs.at[b], sib)

        def out_copy(e):
            j, r0 = ochunks[e]
            return _rcopy(go_hbm.at[j, pl.ds((1 - cc) * ohr + r0, ocr), :], rbo_hbm.at[j, pl.ds(r0, ocr), :],
                          oss.at[e], ors.at[e], sib)

        @pl.when((cb == 0) & (k == 0))
        def _():
            for e in range(noc):
                out_copy(e).start()

        @pl.when(k == 0)
        def _():
            g_ref[...] = jnp.zeros_like(g_ref)

        for (s0, n), d_ref in zip(steps, d_refs):
            @pl.when((cb >= s0) & (cb < s0 + n))
            def _(d_ref=d_ref):
                g_ref[0] += _dot_tn(h_ref[...], d_ref[...])

        @pl.when((k == ns - 1) & (cb > 1))
        def _():
            block_copy(cb - 2).wait_send()

        @pl.when(k == ns - 1)
        def _():
            stage[cb % 2] = g_ref[0, pl.ds(pl.multiple_of((1 - cc) * hr, hr), hr), :]
            block_copy(cb).start()

        @pl.when((k == ns - 1) & (cb == nk - 1))
        def _():
            block_copy(nk - 2).wait_send()
            block_copy(nk - 1).wait_send()
            for b in range(nk):
                block_copy(b).wait_recv()
            for e in range(noc):
                out_copy(e).wait_recv()
                out_copy(e).wait_send()

    def piece_spec(s0, n):
        def imap(cb, k):
            active = (cb >= s0) & (cb < s0 + n)
            return (jnp.where(active, k, 0), jnp.clip(cb - s0, 0, n - 1))
        return pl.BlockSpec((tk, D), imap)

    any_ = pl.BlockSpec(memory_space=pl.ANY)
    return pl.pallas_call(
        body, name="inproj_bwd_w", grid=(nk, ns),
        in_specs=[piece_spec(s0, n) for s0, n in steps] + [pl.BlockSpec((tk, D), lambda cb, k: (k, 0)), any_],
        out_specs=(pl.BlockSpec((1, D, D), lambda cb, k: (cb // npc, 0, cb % npc)), any_, any_),
        out_shape=(jax.ShapeDtypeStruct((4, D, PW), F32), jax.ShapeDtypeStruct((4, hr, PW), F32),
                   jax.ShapeDtypeStruct((g_out.shape[0], ohr, g_out.shape[2]), F32)),
        scratch_shapes=[pltpu.VMEM((2, hr, D), F32), pltpu.SemaphoreType.DMA((nk,)), pltpu.SemaphoreType.DMA((nk,)),
                        pltpu.SemaphoreType.DMA((noc,)), pltpu.SemaphoreType.DMA((noc,))],
        compiler_params=_params(("arbitrary", "arbitrary")),
    )(*pieces, hbf, g_out)


D2D_CHUNK_BYTES = 512 * 1024


def _chunk_rows(a):
    return max(8, D2D_CHUNK_BYTES // (a.shape[-1] * a.dtype.itemsize))


def _pair_swap(arrs):
    na = len(arrs)
    chunks = []
    for t, a in enumerate(arrs):
        cr = _chunk_rows(a)
        chunks += [(t, r0, cr) for r0 in range(0, a.shape[0], cr)]
    nch = len(chunks)

    def body(*refs):
        a_refs = refs[:na]
        o_refs = refs[na:2 * na]
        ss, rs = refs[2 * na:]
        x, y, c = _coords()
        sib = (x, y, 1 - c)
        rcs = []
        for n, (t, r0, cr) in enumerate(chunks):
            rows = pl.ds(r0, cr)
            rc = _rcopy(a_refs[t].at[rows, :], o_refs[t].at[rows, :], ss.at[n], rs.at[n], sib)
            rc.start()
            rcs.append(rc)
        for rc in rcs:
            rc.wait_recv()
        for rc in rcs:
            rc.wait_send()

    any_ = pl.BlockSpec(memory_space=pl.ANY)
    return pl.pallas_call(
        body, name="pair_swap",
        out_shape=tuple(jax.ShapeDtypeStruct(a.shape, a.dtype) for a in arrs),
        in_specs=[any_] * na, out_specs=tuple([any_] * na),
        scratch_shapes=[pltpu.SemaphoreType.DMA((nch,)), pltpu.SemaphoreType.DMA((nch,))],
        compiler_params=_params(),
    )(*arrs)


def _add_half(full, rb, core, tr):
    n, r, cdim = full.shape
    nb = r // 2 // tr

    def body(c_ref, a_ref, b_ref, o_ref, ob_ref):
        tot = a_ref[...] + b_ref[...]
        o_ref[...] = tot
        ob_ref[...] = tot.astype(BF16)

    mine = pl.BlockSpec((1, tr, cdim), lambda i, j, c_ref: (i, c_ref[0] * nb + j, 0))
    spec = pl.BlockSpec((1, tr, cdim), lambda i, j, c_ref: (i, j, 0))
    return pl.pallas_call(
        body, name="add_half",
        grid_spec=pltpu.PrefetchScalarGridSpec(num_scalar_prefetch=1, grid=(n, nb), in_specs=[mine, spec],
                                               out_specs=(spec, spec)),
        out_shape=(jax.ShapeDtypeStruct(rb.shape, rb.dtype), jax.ShapeDtypeStruct(rb.shape, BF16)),
        compiler_params=_params(("parallel", "parallel")),
    )(core, full, rb)


def _sum_slots(q, r3, shard, tr):
    _, r, cdim = q.shape

    def body(j_ref, q_ref, r_ref, o_ref):
        o_ref[...] = ((q_ref[0] + r_ref[0].astype(F32)) + r_ref[1].astype(F32)) + r_ref[2].astype(F32)

    return pl.pallas_call(
        body, name="sum_slots",
        grid_spec=pltpu.PrefetchScalarGridSpec(
            num_scalar_prefetch=1, grid=(r // tr,),
            in_specs=[pl.BlockSpec((1, tr, cdim), lambda i, j_ref: (j_ref[0], i, 0)),
                      pl.BlockSpec((3, tr, cdim), lambda i, j_ref: (0, i, 0))],
            out_specs=pl.BlockSpec((tr, cdim), lambda i, j_ref: (i, 0))),
        out_shape=jax.ShapeDtypeStruct((r, cdim), q.dtype),
        compiler_params=_params(("parallel",)),
    )(shard, q, r3)


def _allreduce_phase(phase, p_refs, o_refs, rbufs, s1, r1, s2, r2, out_refs=None):
    me = _my_index()

    def chunk(t, d):
        ch = p_refs[t].shape[0] // NDEV
        return pl.ds(pl.multiple_of(d * ch, 8), ch)

    def scatter(t, k):
        e = 7 * t + k - 1
        return _rcopy(p_refs[t].at[chunk(t, me ^ k)], rbufs[t].at[me], s1.at[e], r1.at[e], _peer(k))

    def gather(t, k):
        e = 7 * t + k - 1
        return _rcopy(o_refs[t].at[chunk(t, me)], o_refs[t].at[chunk(t, me)], s2.at[e], r2.at[e], _peer(k))

    for t in range(len(p_refs)):
        if phase == 0:
            for k in range(1, NDEV):
                scatter(t, k).start()
            rbufs[t][me] = p_refs[t][chunk(t, me), :]
        elif phase == 1:
            for k in range(1, NDEV):
                e = 7 * t + k - 1
                _rcopy(p_refs[t].at[chunk(t, me)], rbufs[t].at[me ^ k], s1.at[e], r1.at[e], _peer(k)).wait_recv()
            tot = rbufs[t][0]
            for d in range(1, NDEV):
                tot = tot + rbufs[t][d]
            o_refs[t][chunk(t, me), :] = tot
            for k in range(1, NDEV):
                gather(t, k).start()
        else:
            for k in range(1, NDEV):
                e = 7 * t + k - 1
                _rcopy(o_refs[t].at[chunk(t, me)], o_refs[t].at[chunk(t, me ^ k)], s2.at[e], r2.at[e],
                       _peer(k)).wait_recv()
            for k in range(1, NDEV):
                scatter(t, k).wait_send()
                gather(t, k).wait_send()
            if out_refs is not None:
                out_refs[t][...] = o_refs[t][...]


def _adamw_update(w, g, m, v):
    nm = B1 * m + (1.0 - B1) * g
    nv = B2 * v + (1.0 - B2) * (g * g)
    m_hat = nm / (1.0 - B1 ** STEP)
    v_hat = nv / (1.0 - B2 ** STEP)
    return -LR * (m_hat / (jnp.sqrt(v_hat) + ADAM_EPS) + WD * w), nm, nv


def _adamw(w, g, m, v, tr):
    r, cdim = w.shape

    def body(w_ref, g_ref, m_ref, v_ref, d_ref, nm_ref, nv_ref):
        d_ref[...], nm_ref[...], nv_ref[...] = _adamw_update(w_ref[...], g_ref[...], m_ref[...], v_ref[...])

    spec = pl.BlockSpec((tr, cdim), lambda i: (i, 0))
    sd = jax.ShapeDtypeStruct((r, cdim), F32)
    return pl.pallas_call(
        body, name="adamw", grid=(r // tr,), in_specs=[spec] * 4, out_specs=(spec,) * 3, out_shape=(sd,) * 3,
        compiler_params=_params(("parallel",)),
    )(w, g, m, v)


V_B_GATE, V_CONV_B, V_LAM, V_G_FINAL, V_CONV_W, V_LOSS, V_ROWS = 0, 2, 3, 4, 5, 9, 64
M_W_A, M_W_X, M_B_A, M_B_X, M_ROWS = 0, H * DH, 2 * H * DH, 2 * H * DH + H, 2112
SMALL = ("g_norm", "b_mod", "b_gate", "conv_b", "lam", "g_final", "conv_w", "w_a", "w_x", "b_a", "b_x")


def _adamw_small(redv, redm, g_conv, g_gnorm, g_bmod, wmv):
    def grad(name, rv, rm, gc, gg, gb):
        if name == "g_norm":
            return gg[...]
        if name == "b_mod":
            return gb[...]
        if name == "b_gate":
            return jnp.concatenate([rv[V_B_GATE + t:V_B_GATE + t + 1, :] for t in range(2)], axis=1)
        if name == "conv_b":
            return rv[V_CONV_B:V_CONV_B + 1, :]
        if name == "lam":
            return rv[V_LAM:V_LAM + 1, :]
        if name == "g_final":
            return rv[V_G_FINAL:V_G_FINAL + 1, :]
        if name == "conv_w":
            return gc[...]
        if name == "w_a":
            return rm[M_W_A:M_W_A + H * DH, :]
        if name == "w_x":
            return rm[M_W_X:M_W_X + H * DH, :]
        if name == "b_a":
            return rm[M_B_A:M_B_A + H, :]
        return rm[M_B_X:M_B_X + H, :]

    n = len(SMALL)

    def body(*refs):
        rv, rm, gc, gg, gb = refs[:5]
        ins, outs = refs[5:5 + 3 * n], refs[5 + 3 * n:]
        for t, name in enumerate(SMALL):
            w_ref, m_ref, v_ref = ins[3 * t:3 * t + 3]
            g_out, d_out, m_out, v_out = outs[4 * t:4 * t + 4]
            g = grad(name, rv, rm, gc, gg, gb)
            g_out[...] = g
            d_out[...], m_out[...], v_out[...] = _adamw_update(w_ref[...], g, m_ref[...], v_ref[...])

    vm = pl.BlockSpec(memory_space=pltpu.VMEM)
    flat = [a for name in SMALL for a in wmv[name]]
    shapes = [jax.ShapeDtypeStruct(wmv[name][0].shape, F32) for name in SMALL for _ in range(4)]
    outs = pl.pallas_call(
        body, name="adamw_small", out_shape=tuple(shapes),
        in_specs=[vm] * (5 + len(flat)), out_specs=tuple([vm] * len(shapes)),
        compiler_params=_params(),
    )(redv, redm, g_conv, g_gnorm, g_bmod, *flat)
    return {name: outs[4 * t:4 * t + 4] for t, name in enumerate(SMALL)}


def _rope_tables(positions):
    inv_freq = ROPE_THETA ** (-jnp.arange(0, ROT, 2, dtype=F32) / ROT)
    ang = positions.astype(F32)[:, None] * inv_freq
    cos, sin = jnp.cos(ang), jnp.sin(ang)
    n = positions.shape[0]
    half = ROT // 2
    rc = jnp.concatenate([cos, cos, jnp.ones((n, DH - ROT), F32)], axis=1)
    rsa = jnp.concatenate([-sin, jnp.zeros((n, DH - half), F32)], axis=1)
    rsb = jnp.concatenate([jnp.zeros((n, half), F32), sin, jnp.zeros((n, DH - ROT), F32)], axis=1)
    return rc, rsa, rsb


def kernel(x, c, positions, g_norm, w_mod, b_mod, w_in, b_gate, conv_w, conv_b, w_a, b_a, w_x, b_x, lam, w_out_rnn, w_out_attn, w_o, g_final, loss_target, m_g_norm, m_w_mod, m_b_mod, m_w_in, m_b_gate, m_conv_w, m_conv_b, m_w_a, m_b_a, m_w_x, m_b_x, m_lam, m_w_out_rnn, m_w_out_attn, m_w_o, m_g_final, v_g_norm, v_w_mod, v_b_mod, v_w_in, v_b_gate, v_conv_w, v_conv_b, v_w_a, v_b_a, v_w_x, v_b_x, v_lam, v_w_out_rnn, v_w_out_attn, v_w_o, v_g_final):
    s = x.shape[1]
    xi = lax.axis_index("x")
    yi = lax.axis_index("y")
    ci = lax.axis_index("c")
    shard = 2 * xi + yi
    x2d = x[0]
    tgt = loss_target[0]
    pos = positions[0]

    c_all, mod4, conv_all = _mod_fwd(c, w_mod[0], b_mod.reshape(4, 1, 768), conv_w[0])
    mod = mod4.reshape(1, 3 * D)
    shift, scale, gate = mod[:, :D], mod[:, D:2 * D], mod[:, 2 * D:]
    w3sh = jnp.concatenate([w_out_rnn[0], w_out_attn[0], w_o[0]], axis=0).astype(BF16)
    wsh = w_in[0].astype(BF16)
    conv_full = conv_all[0::2].transpose(1, 0, 2).reshape(4, D)

    order = jnp.stack([shard, shard ^ 2, shard ^ 1, shard ^ 3]).astype(jnp.int32)
    p, hbf, wg = _gather_norm_inproj(x2d, g_norm, shift, scale, wsh, order)
    wg = lax.dynamic_update_slice(wg, wsh[None], (shard, 0, 0))
    rc, rsa, rsb = _rope_tables(pos)
    pos_col = jnp.broadcast_to((pos == 0).astype(F32)[:, None], (s, DH))
    b_a3, b_x3 = b_a.reshape(H, 1, DH), b_x.reshape(H, 1, DH)
    hr, gr = _rnn_fwd(p, pos_col, conv_full, conv_b, w_a[0], b_a3, w_x[0], b_x3, lam)
    o, lse, ga, q_rot, k_rot, w3g = _attn_fwd(p, rc, rsa, rsb, w3sh)
    w3g = lax.dynamic_update_slice(w3g, w3sh[None], (shard, 0, 0))
    w3 = w3g.reshape(4, 3, 256, D).transpose(1, 0, 2, 3).reshape(3, D, D)

    dgr, dga, dc, dx2, vec_t, g_out = _tail(gr, ga, p, x2d, tgt, w3, b_gate, gate, g_final.reshape(1, D))

    dxr, dzr, g_wa, g_ba, g_wx, g_bx, g_lam, g_cw, g_cb = _rnn_bwd(
        p, hr, dgr, pos_col, conv_full, conv_b, w_a[0], b_a3, w_x[0], b_x3, lam)
    vpack = jnp.concatenate([
        vec_t[2:4],
        g_cb.reshape(1, D),
        g_lam.reshape(1, D),
        vec_t[0:1],
        g_cw.transpose(1, 0, 2).reshape(4, D),
        vec_t[4:5],
        jnp.zeros((V_ROWS - 10, D), F32)], axis=0)
    mpack = jnp.concatenate([
        g_wa.reshape(H * DH, DH), g_wx.reshape(H * DH, DH), g_ba.reshape(H, DH), g_bx.reshape(H, DH),
        jnp.zeros((M_ROWS - 2 * H * DH - 2 * H, DH), F32)], axis=0)
    (dq, dk, dv, dza), (redv, redm) = _attn_bwd(p, q_rot, k_rot, o, lse, dga, rc, rsa, rsb, [vpack, mpack])

    pieces = [dxr, dzr, dq, dk, dv, dza, dc]
    g_win, rb_a, rb_b = _inproj_bwd_w(pieces, hbf, g_out)

    core = ci.reshape(1)
    shard1 = shard.reshape(1)
    (q_a, qh_a), (q_b, qh_b) = _add_half(g_win, rb_a, core, tr=256), _add_half(g_out, rb_b, core, tr=128)
    grad_x, vec_n, (r_a, r_b) = _inproj_bwd_x(pieces, wg, x2d, dx2, g_norm, scale, [qh_a, qh_b])
    f_a, f_b = _sum_slots(q_a, r_a, shard1, tr=256), _sum_slots(q_b, r_b, shard1, tr=128)
    s_a, s_b = _pair_swap([f_a, f_b])
    south = ci == 0
    grad_w_in = jnp.where(south, jnp.concatenate([f_a, s_a], axis=0), jnp.concatenate([s_a, f_a], axis=0))
    g3 = jnp.where(south, jnp.concatenate([f_b, s_b], axis=0), jnp.concatenate([s_b, f_b], axis=0)).reshape(3, 256, D)

    dmod_row = jnp.concatenate([vec_n[0:1], vec_n[1:2], vec_t[1:2]], axis=1)
    loss = redv[V_LOSS, 0]
    grad_w_mod, g_bmod4, g_gnorm = _mod_bwd(dmod_row.reshape(4, 1, 768), vec_n[2:3], c_all)
    g_conv_sh = lax.dynamic_slice_in_dim(redv[V_CONV_W:V_CONV_W + 4], shard * 256, 256, axis=1)

    shape2d = dict(g_norm=(1, D), b_mod=(1, 3 * D), b_gate=(1, 2 * D), conv_b=(1, D), lam=(1, D), g_final=(1, D),
                   conv_w=(4, 256), w_a=(H * DH, DH), w_x=(H * DH, DH), b_a=(H, DH), b_x=(H, DH))
    given = dict(
        g_norm=(g_norm, m_g_norm, v_g_norm), b_mod=(b_mod, m_b_mod, v_b_mod), b_gate=(b_gate, m_b_gate, v_b_gate),
        conv_b=(conv_b, m_conv_b, v_conv_b), lam=(lam, m_lam, v_lam), g_final=(g_final, m_g_final, v_g_final),
        conv_w=(conv_w, m_conv_w, v_conv_w), w_a=(w_a, m_w_a, v_w_a), w_x=(w_x, m_w_x, v_w_x),
        b_a=(b_a, m_b_a, v_b_a), b_x=(b_x, m_b_x, v_b_x))
    small = _adamw_small(redv, redm, g_conv_sh, g_gnorm, g_bmod4.reshape(1, 3 * D),
                         {n: tuple(a.reshape(shape2d[n]) for a in given[n]) for n in SMALL})

    big_in = _adamw(w_in[0], grad_w_in, m_w_in[0], v_w_in[0], tr=256)
    big_mod = _adamw(w_mod[0], grad_w_mod, m_w_mod[0], v_w_mod[0], tr=256)
    w3f = jnp.concatenate([w_out_rnn[0], w_out_attn[0], w_o[0]], axis=0)
    m3f = jnp.concatenate([m_w_out_rnn[0], m_w_out_attn[0], m_w_o[0]], axis=0)
    v3f = jnp.concatenate([v_w_out_rnn[0], v_w_out_attn[0], v_w_o[0]], axis=0)
    big_out = _adamw(w3f, g3.reshape(768, D), m3f, v3f, tr=256)

    names = ["g_norm", "w_mod", "b_mod", "w_in", "b_gate", "conv_w", "conv_b", "w_a", "b_a", "w_x", "b_x", "lam",
             "w_out_rnn", "w_out_attn", "w_o", "g_final"]
    outs = []
    for idx in range(4):
        d = {n: small[n][idx].reshape(given[n][0].shape) for n in SMALL}
        if idx == 0:
            d.update(w_mod=grad_w_mod[None], w_in=grad_w_in[None],
                     w_out_rnn=g3[0][None], w_out_attn=g3[1][None], w_o=g3[2][None])
        else:
            d.update(w_mod=big_mod[idx - 1][None], w_in=big_in[idx - 1][None],
                     w_out_rnn=big_out[idx - 1][0:256][None], w_out_attn=big_out[idx - 1][256:512][None],
                     w_o=big_out[idx - 1][512:768][None])
        outs.append(d)
    flat = [d[n] for d in outs for n in names]
    return (loss, grad_x[None], *flat)
```

```python
import jax
import jax.numpy as jnp
from jax import lax
from jax.experimental import pallas as pl
from jax.experimental.pallas import tpu as pltpu

F32, BF16 = jnp.float32, jnp.bfloat16
MESH = pl.DeviceIdType.MESH
HIGHEST = lax.Precision.HIGHEST

D = 1024
H = 8
DH = 128
PW = 2048
EPS = 1e-6
LRU_C = 8.0
SCALE = DH ** -0.5
NEG = -1e30
SPAN = 2048
UB = 128
DILATIONS = (1, 4, 16)
UNIT_BATCH = 16
UNIT_BATCH_FWD = 8
ROPE_THETA = 500000.0
ROT = 32

LR, B1, B2, ADAM_EPS, WD, STEP = 0.001, 0.9, 0.999, 1e-08, 0.01, 10

NDEV = 8


def _params(sem=None, vmem_mb=56):
    return pltpu.CompilerParams(dimension_semantics=sem, vmem_limit_bytes=vmem_mb * 2 ** 20)


def _coords():
    return lax.axis_index("x"), lax.axis_index("y"), lax.axis_index("c")


def _flip(v, bit):
    return 1 - v if bit else v


def _peer(k):
    x, y, c = _coords()
    return (_flip(x, (k >> 2) & 1), _flip(y, (k >> 1) & 1), _flip(c, k & 1))


def _my_index():
    x, y, c = _coords()
    return 4 * x + 2 * y + c


def _rcopy(src, dst, ssem, rsem, dev):
    return pltpu.make_async_remote_copy(src_ref=src, dst_ref=dst, send_sem=ssem, recv_sem=rsem,
                                        device_id=dev, device_id_type=MESH)


def _sigmoid(x):
    return jax.nn.sigmoid(x)


def _dot(a, b):
    return jnp.dot(a, b, preferred_element_type=F32)


def _dot_nt(a, b):
    return lax.dot_general(a, b, (((1,), (1,)), ((), ())), preferred_element_type=F32)


def _dot_tn(a, b):
    return lax.dot_general(a, b, (((0,), (0,)), ((), ())), preferred_element_type=F32)


def _colsum(a):
    return jnp.sum(a, axis=0, keepdims=True)


def _mod_fwd(c, w_mod_sh, b_mod4, conv_sh):
    def body(c_ref, w_ref, b_ref, cv_ref, call_ref, mod_ref, cvall_ref, rows_ref, cmat_ref, s1, r1, s2, r2, s3, r3):
        x, y, _ = _coords()
        me = _my_index()
        j = 2 * x + y
        call_ref[me] = c_ref[...]
        cvall_ref[me] = cv_ref[...]
        sends = []
        for k in range(1, NDEV):
            cp = _rcopy(call_ref.at[me], call_ref.at[me], s1.at[k - 1], r1.at[k - 1], _peer(k))
            cp.start()
            sends.append(cp)
            cp = _rcopy(cvall_ref.at[me], cvall_ref.at[me], s3.at[k - 1], r3.at[k - 1], _peer(k))
            cp.start()
            sends.append(cp)
        for k in range(1, NDEV):
            pk = me ^ k
            _rcopy(call_ref.at[pk], call_ref.at[pk], s1.at[k - 1], r1.at[k - 1], _peer(k)).wait_recv()
        for b in range(NDEV):
            cmat_ref[pl.ds(b, 1), :] = call_ref[b]
        cm = cmat_ref[...]
        act = cm * _sigmoid(cm)
        mp = jnp.dot(act, w_ref[...], preferred_element_type=F32, precision=HIGHEST) + b_ref[j]
        for b in range(NDEV):
            rows_ref[b] = mp[b:b + 1]
        mod_ref[j] = rows_ref[me]
        for q, k in enumerate((2, 4, 6)):
            cp = _rcopy(rows_ref.at[me ^ k], mod_ref.at[j], s2.at[q], r2.at[q], _peer(k))
            cp.start()
            sends.append(cp)
        for q, k in enumerate((2, 4, 6)):
            jq = j ^ (k >> 1)
            _rcopy(rows_ref.at[me], mod_ref.at[jq], s2.at[q], r2.at[q], _peer(k)).wait_recv()
        for k in range(1, NDEV):
            pk = me ^ k
            _rcopy(cvall_ref.at[pk], cvall_ref.at[pk], s3.at[k - 1], r3.at[k - 1], _peer(k)).wait_recv()
        for cp in sends:
            cp.wait_send()

    vm = pl.BlockSpec(memory_space=pltpu.VMEM)
    return pl.pallas_call(
        body, name="mod_fwd",
        out_shape=(jax.ShapeDtypeStruct((NDEV, 1, D), F32), jax.ShapeDtypeStruct((4, 1, 768), F32),
                   jax.ShapeDtypeStruct((NDEV,) + conv_sh.shape, F32)),
        in_specs=[vm, vm, vm, vm], out_specs=(vm, vm, vm),
        scratch_shapes=[pltpu.VMEM((NDEV, 1, 768), F32), pltpu.VMEM((NDEV, D), F32),
                        pltpu.SemaphoreType.DMA((7,)), pltpu.SemaphoreType.DMA((7,)),
                        pltpu.SemaphoreType.DMA((3,)), pltpu.SemaphoreType.DMA((3,)),
                        pltpu.SemaphoreType.DMA((7,)), pltpu.SemaphoreType.DMA((7,))],
        compiler_params=_params(),
    )(c, w_mod_sh, b_mod4, conv_sh)


def _mod_bwd(dmod4, gn_row, c_all):
    def body(d_ref, g_ref, call_ref, gw_ref, gb_ref, gg_ref, dall_ref, gall_ref, cmat_ref, dmat_ref, s1, r1, s2, r2):
        x, y, _ = _coords()
        me = _my_index()
        j = 2 * x + y
        dall_ref[me] = d_ref[...]
        gall_ref[me] = g_ref[...]
        sends = []
        for k in range(1, NDEV):
            for buf, ss, rs in ((dall_ref, s1, r1), (gall_ref, s2, r2)):
                cp = _rcopy(buf.at[me], buf.at[me], ss.at[k - 1], rs.at[k - 1], _peer(k))
                cp.start()
                sends.append(cp)
        for k in range(1, NDEV):
            pk = me ^ k
            for buf, ss, rs in ((dall_ref, s1, r1), (gall_ref, s2, r2)):
                _rcopy(buf.at[pk], buf.at[pk], ss.at[k - 1], rs.at[k - 1], _peer(k)).wait_recv()
        for cp in sends:
            cp.wait_send()
        gb, gg = dall_ref[0], gall_ref[0]
        for b in range(1, NDEV):
            gb = gb + dall_ref[b]
            gg = gg + gall_ref[b]
        gb_ref[...] = gb
        gg_ref[...] = gg
        for b in range(NDEV):
            cmat_ref[pl.ds(b, 1), :] = call_ref[b]
            dmat_ref[pl.ds(b, 1), :] = dall_ref[b, j]
        cm = cmat_ref[...]
        act = cm * _sigmoid(cm)
        gw_ref[...] = lax.dot_general(act, dmat_ref[...], (((0,), (0,)), ((), ())),
                                      preferred_element_type=F32, precision=HIGHEST)

    vm = pl.BlockSpec(memory_space=pltpu.VMEM)
    return pl.pallas_call(
        body, name="mod_bwd",
        out_shape=(jax.ShapeDtypeStruct((D, 768), F32), jax.ShapeDtypeStruct((4, 1, 768), F32),
                   jax.ShapeDtypeStruct((1, D), F32)),
        in_specs=[vm, vm, vm], out_specs=(vm, vm, vm),
        scratch_shapes=[pltpu.VMEM((NDEV, 4, 1, 768), F32), pltpu.VMEM((NDEV, 1, D), F32),
                        pltpu.VMEM((NDEV, D), F32), pltpu.VMEM((NDEV, 768), F32),
                        pltpu.SemaphoreType.DMA((7,)), pltpu.SemaphoreType.DMA((7,)),
                        pltpu.SemaphoreType.DMA((7,)), pltpu.SemaphoreType.DMA((7,))],
        compiler_params=_params(),
    )(dmod4, gn_row, c_all)


def _gather_norm_inproj(x, gn, shift, scale, wsh, order, tm=1024, tn=1024):
    s = x.shape[0]
    ni = s // tm
    npc = PW // tn
    rows, cols = wsh.shape
    half = rows // 2
    nch = 4
    cr = half // nch
    chips = ((1, 0), (0, 1), (1, 1))

    def body(ord_ref, x_ref, gn_ref, sh_ref, sc_ref, wsh_hbm, p_ref, h_ref, wg_hbm, hs_all, w_s, wsem, ss, rs):
        slot, i, col = pl.program_id(0), pl.program_id(1), pl.program_id(2)
        cx, cy, cc = _coords()
        j = 2 * cx + cy
        sib = (cx, cy, 1 - cc)
        mine = lambda n: pl.ds(cc * half + n * cr, cr)
        theirs = lambda n: pl.ds((1 - cc) * half + n * cr, cr)
        shard_of = lambda q: j ^ (2 * chips[q][0] + chips[q][1])

        def to_chip(q, n):
            e = nch * q + n
            return _rcopy(wsh_hbm.at[mine(n)], wg_hbm.at[j, mine(n)], ss.at[e], rs.at[e],
                          (_flip(cx, chips[q][0]), _flip(cy, chips[q][1]), cc))

        def from_chip(q, n):
            e = nch * q + n
            return _rcopy(wsh_hbm.at[mine(n)], wg_hbm.at[shard_of(q), mine(n)], ss.at[e], rs.at[e], sib)

        def to_sibling(q, n):
            e = 3 * nch + nch * q + n
            return _rcopy(wg_hbm.at[shard_of(q), mine(n)], wg_hbm.at[shard_of(q), mine(n)], ss.at[e], rs.at[e], sib)

        def from_sibling(q, n):
            e = 3 * nch + nch * q + n
            return _rcopy(wsh_hbm.at[mine(n)], wg_hbm.at[shard_of(q), theirs(n)], ss.at[e], rs.at[e], sib)

        def load(sl, src):
            cp = pltpu.make_async_copy(src, w_s.at[sl], wsem.at[sl])
            cp.start()
            cp.wait()

        first = (i == 0) & (col == 0)

        @pl.when(first & (slot == 0))
        def _():
            for n in range(nch):
                for q in (0, 1):
                    to_chip(q, n).start()
            load(0, wsh_hbm.at[pl.ds(0, D), :])

        @pl.when(first & (slot == 1))
        def _():
            for q in (0, 1):
                for n in range(nch):
                    from_chip(q, n).wait_recv()
                    to_sibling(q, n).start()
            for n in range(nch):
                to_chip(2, n).start()
            for n in range(nch):
                from_sibling(0, n).wait_recv()
            load(1, wg_hbm.at[shard_of(0), pl.ds(0, D), :])

        @pl.when(first & (slot == 2))
        def _():
            for n in range(nch):
                from_sibling(1, n).wait_recv()
            load(2, wg_hbm.at[shard_of(1), pl.ds(0, D), :])

        @pl.when(first & (slot == 3))
        def _():
            for n in range(nch):
                from_chip(2, n).wait_recv()
                to_sibling(2, n).start()
            for n in range(nch):
                from_sibling(2, n).wait_recv()
            load(3, wg_hbm.at[shard_of(2), pl.ds(0, D), :])
            for q in range(3):
                for n in range(nch):
                    to_chip(q, n).wait_send()
                    to_sibling(q, n).wait_send()

        @pl.when((slot == 0) & (col == 0))
        def _():
            xt = x_ref[...]
            rstd = lax.rsqrt(jnp.mean(xt * xt, axis=-1, keepdims=True) + EPS)
            h = ((xt * rstd * gn_ref[...]) * (1.0 + sc_ref[...]) + sh_ref[...]).astype(BF16)
            hs_all[i] = h
            h_ref[...] = h

        p_ref[0] = _dot(hs_all[i], w_s[slot, :, pl.ds(pl.multiple_of(col * tn, tn), tn)]).astype(BF16)

    row = pl.BlockSpec((1, D), lambda sl, i, col, o: (0, 0))
    x_rows = lambda sl, i, col, o: (jnp.where(sl == 0, i, ni - 1), 0)
    any_ = pl.BlockSpec(memory_space=pl.ANY)
    return pl.pallas_call(
        body, name="gather_norm_inproj",
        grid_spec=pltpu.PrefetchScalarGridSpec(
            num_scalar_prefetch=1, grid=(4, ni, npc),
            in_specs=[pl.BlockSpec((tm, D), x_rows), row, row, row, any_],
            out_specs=(pl.BlockSpec((1, tm, tn), lambda sl, i, col, o: (o[sl], i, col)),
                       pl.BlockSpec((tm, D), x_rows), any_),
            scratch_shapes=[pltpu.VMEM((ni, tm, D), BF16), pltpu.VMEM((4, D, PW), BF16),
                            pltpu.SemaphoreType.DMA((4,)),
                            pltpu.SemaphoreType.DMA((6 * nch,)), pltpu.SemaphoreType.DMA((6 * nch,))]),
        out_shape=(jax.ShapeDtypeStruct((4, s, PW), BF16), jax.ShapeDtypeStruct((s, D), BF16),
                   jax.ShapeDtypeStruct((4, rows, cols), wsh.dtype)),
        compiler_params=_params(("arbitrary", "arbitrary", "arbitrary")),
    )(order, x, gn, shift, scale, wsh)


def _shift_down(prev8, cur, d):
    t = cur.shape[0]
    c3 = cur.reshape(t // 8, 8, DH)
    rot = pltpu.roll(c3, d, 1)
    before = jnp.concatenate([pltpu.roll(prev8, d, 0).reshape(1, 8, DH), rot[:-1]], axis=0)
    rows = lax.broadcasted_iota(jnp.int32, c3.shape, 1)
    return jnp.where(rows >= d, rot, before).reshape(t, DH)


def _shift_up(cur, next8, d):
    t = cur.shape[0]
    c3 = cur.reshape(t // 8, 8, DH)
    rot = pltpu.roll(c3, 8 - d, 1)
    after = jnp.concatenate([rot[1:], pltpu.roll(next8, 8 - d, 0).reshape(1, 8, DH)], axis=0)
    rows = lax.broadcasted_iota(jnp.int32, c3.shape, 1)
    return jnp.where(rows < 8 - d, rot, after).reshape(t, DH)


def _rnn_gates(xr, prev8, cw, cb, wa, ba, wx, bx, lam, reset):
    xc = cw[3:4] * xr + cb
    for d in (1, 2, 3):
        xc = xc + cw[3 - d:4 - d] * _shift_down(prev8, xr, d)
    xcb = xc.astype(BF16)
    r = _sigmoid(_dot(xcb, wa.astype(BF16)) + ba)
    ig = _sigmoid(_dot(xcb, wx.astype(BF16)) + bx)
    nl = -lam
    sp = jnp.maximum(nl, 0.0) + jnp.log1p(jnp.exp(-jnp.abs(nl)))
    log_a = (-LRU_C * r) * sp
    a_raw = jnp.exp(log_a)
    a = jnp.where(reset, 0.0, a_raw)
    mult = jnp.where(reset, 1.0, jnp.sqrt(1.0 - a_raw * a_raw))
    return xc, r, ig, sp, a, mult


def _log_scan(a, b, axis, up):
    n = a.shape[axis]
    rows = lax.broadcasted_iota(jnp.int32, a.shape, axis)
    d = 1
    while d < n:
        m = rows < n - d if up else rows >= d
        shift = n - d if up else d
        a_s = pltpu.roll(a, shift, axis)
        b_s = pltpu.roll(b, shift, axis)
        b = jnp.where(m, a * b_s + b, b)
        a = jnp.where(m, a * a_s, a)
        d *= 2
    return a, b


def _scan(a, b, t, edge, up=False):
    g = t // 8
    a3, b3 = _log_scan(a.reshape(g, 8, DH), b.reshape(g, 8, DH), 1, up)
    last = 0 if up else 7
    ag, bg = _log_scan(a3[:, last, :], b3[:, last, :], 0, up)
    hg = ag * edge + bg
    grp = lax.broadcasted_iota(jnp.int32, hg.shape, 0)
    if up:
        cin = jnp.where(grp == g - 1, edge, pltpu.roll(hg, g - 1, 0))
        tail = hg[0:1]
    else:
        cin = jnp.where(grp == 0, edge, pltpu.roll(hg, 1, 0))
        tail = hg[g - 1:g]
    return (a3 * cin[:, None, :] + b3).reshape(t, DH), tail


def _rnn_fwd(p, pos, conv_w, conv_b, w_a, b_a, w_x, b_x, lam, tt=1024):
    s = p.shape[1]
    nt = s // tt

    def body(xr_ref, z_ref, pos_ref, cw_ref, cb_ref, wa_ref, ba_ref, wx_ref, bx_ref, lam_ref,
             hr_ref, gr_ref, xprev, hprev):
        @pl.when(pl.program_id(1) == 0)
        def _():
            xprev[...] = jnp.zeros_like(xprev)
            hprev[...] = jnp.zeros_like(hprev)

        xr = xr_ref[0].astype(F32)
        z = z_ref[0].astype(F32)
        reset = pos_ref[...] > 0.5
        xc, r, ig, sp, a, mult = _rnn_gates(xr, xprev[...], cw_ref[...], cb_ref[...], wa_ref[0], ba_ref[0],
                                            wx_ref[0], bx_ref[0], lam_ref[...], reset)
        bx = mult * ig * xc
        h, h_last = _scan(a, bx, tt, hprev[0:1])
        xprev[...] = xr[tt - 8:]
        hprev[...] = jnp.broadcast_to(h_last, (8, DH))
        hr_ref[...] = h
        gr_ref[...] = (h * (z * _sigmoid(z))).astype(BF16)

    head_row = lambda hh, t: (0, hh)
    return pl.pallas_call(
        body, name="rnn_fwd", grid=(H, nt),
        in_specs=[pl.BlockSpec((1, tt, DH), lambda hh, t: (0, t, hh)),
                  pl.BlockSpec((1, tt, DH), lambda hh, t: (0, t, H + hh)),
                  pl.BlockSpec((tt, DH), lambda hh, t: (t, 0)),
                  pl.BlockSpec((4, DH), head_row), pl.BlockSpec((1, DH), head_row),
                  pl.BlockSpec((1, DH, DH), lambda hh, t: (hh, 0, 0)), pl.BlockSpec((1, 1, DH), lambda hh, t: (hh, 0, 0)),
                  pl.BlockSpec((1, DH, DH), lambda hh, t: (hh, 0, 0)), pl.BlockSpec((1, 1, DH), lambda hh, t: (hh, 0, 0)),
                  pl.BlockSpec((1, DH), head_row)],
        out_specs=(pl.BlockSpec((tt, DH), lambda hh, t: (t, hh)), pl.BlockSpec((tt, DH), lambda hh, t: (t, hh))),
        out_shape=(jax.ShapeDtypeStruct((s, D), F32), jax.ShapeDtypeStruct((s, D), BF16)),
        scratch_shapes=[pltpu.VMEM((8, DH), F32), pltpu.VMEM((8, DH), F32)],
        compiler_params=_params(("parallel", "arbitrary")),
    )(p, p, pos, conv_w, conv_b, w_a, b_a, w_x, b_x, lam)


def _rnn_bwd(p, hr, dgr, pos, conv_w, conv_b, w_a, b_a, w_x, b_x, lam, tt=1024):
    s = p.shape[1]
    nt = s // tt
    t8 = tt // 8

    def body(xr_ref, z_ref, xp_ref, hr_ref, hp_ref, dg_ref, pos_ref, cw_ref, cb_ref, wa_ref, ba_ref, wx_ref, bx_ref,
             lam_ref, dxr_ref, dz_ref, gwa_ref, gba_ref, gwx_ref, gbx_ref, glam_ref, gcw_ref, gcb_ref,
             a_next, g_next, dxc_next):
        t = pl.program_id(1)
        has_prev = t < nt - 1

        @pl.when(t == 0)
        def _():
            a_next[...] = jnp.zeros_like(a_next)
            g_next[...] = jnp.zeros_like(g_next)
            dxc_next[...] = jnp.zeros_like(dxc_next)
            gwa_ref[...] = jnp.zeros_like(gwa_ref)
            gba_ref[...] = jnp.zeros_like(gba_ref)
            gwx_ref[...] = jnp.zeros_like(gwx_ref)
            gbx_ref[...] = jnp.zeros_like(gbx_ref)
            glam_ref[...] = jnp.zeros_like(glam_ref)
            gcw_ref[...] = jnp.zeros_like(gcw_ref)
            gcb_ref[...] = jnp.zeros_like(gcb_ref)

        xr = xr_ref[0].astype(F32)
        z = z_ref[0].astype(F32)
        hr_blk = hr_ref[...]
        dg = dg_ref[...]
        xprev = jnp.where(has_prev, xp_ref[0].astype(F32)[8:], 0.0)
        hprev8 = jnp.where(has_prev, hp_ref[...], 0.0)
        reset = pos_ref[...] > 0.5
        cw = cw_ref[...]
        wa = wa_ref[0]
        wx = wx_ref[0]
        lam_v = lam_ref[...]
        xc, r, ig, sp, a, mult = _rnn_gates(xr, xprev, cw, cb_ref[...], wa, ba_ref[0], wx, bx_ref[0], lam_v, reset)

        sz = _sigmoid(z)
        dh = dg * (z * sz)
        dz_ref[...] = (dg * hr_blk * (sz * (1.0 + z * (1.0 - sz)))).astype(BF16)

        an = _shift_up(a, a_next[...], 1)
        g, g_first = _scan(an, dh, tt, g_next[0:1], up=True)
        a_next[...] = jnp.broadcast_to(a[0:1], (8, DH))
        g_next[...] = jnp.broadcast_to(g_first, (8, DH))

        hm1 = _shift_down(hprev8, hr_blk, 1)
        da = g * hm1
        dmult = g * (ig * xc)
        di = g * (mult * xc)
        dxc = g * (mult * ig)
        dla = jnp.where(reset, 0.0, da * a - dmult * (a * a) / mult)
        dr = dla * (-LRU_C * sp)
        dsp = _colsum(dla * (-LRU_C * r))
        glam_ref[0] += dsp * (-_sigmoid(-lam_v))
        dpa = dr * r * (1.0 - r)
        dpx = di * ig * (1.0 - ig)
        dpab = dpa.astype(BF16)
        dpxb = dpx.astype(BF16)
        dxc = dxc + _dot_nt(dpab, wa.astype(BF16)) + _dot_nt(dpxb, wx.astype(BF16))
        xcb = xc.astype(BF16)
        gwa_ref[0] += _dot_tn(xcb, dpab)
        gwx_ref[0] += _dot_tn(xcb, dpxb)
        gba_ref[0] += _colsum(dpa)
        gbx_ref[0] += _colsum(dpx)

        dxr = cw[3:4] * dxc
        for d in (1, 2, 3):
            dxr = dxr + cw[3 - d:4 - d] * _shift_up(dxc, dxc_next[...], d)
        dxr_ref[...] = dxr.astype(BF16)
        dxc_next[...] = dxc[0:8]
        gcb_ref[0] += _colsum(dxc)
        gcw_ref[0, 3:4, :] += _colsum(xr * dxc)
        for d in (1, 2, 3):
            gcw_ref[0, 3 - d:4 - d, :] += _colsum(_shift_down(xprev, xr, d) * dxc)

    rt = lambda t: nt - 1 - t
    prev8 = lambda t: jnp.maximum(rt(t) * t8 - 1, 0)
    head_row = lambda hh, t: (0, hh)
    hsm = lambda hh, t: (hh, 0, 0)
    return pl.pallas_call(
        body, name="rnn_bwd", grid=(H, nt),
        in_specs=[pl.BlockSpec((1, tt, DH), lambda hh, t: (0, rt(t), hh)),
                  pl.BlockSpec((1, tt, DH), lambda hh, t: (0, rt(t), H + hh)),
                  pl.BlockSpec((1, 16, DH), lambda hh, t: (0, jnp.maximum(rt(t) * (tt // 16) - 1, 0), hh)),
                  pl.BlockSpec((tt, DH), lambda hh, t: (rt(t), hh)),
                  pl.BlockSpec((8, DH), lambda hh, t: (prev8(t), hh)),
                  pl.BlockSpec((tt, DH), lambda hh, t: (rt(t), hh)),
                  pl.BlockSpec((tt, DH), lambda hh, t: (rt(t), 0)),
                  pl.BlockSpec((4, DH), head_row), pl.BlockSpec((1, DH), head_row),
                  pl.BlockSpec((1, DH, DH), hsm), pl.BlockSpec((1, 1, DH), hsm),
                  pl.BlockSpec((1, DH, DH), hsm), pl.BlockSpec((1, 1, DH), hsm),
                  pl.BlockSpec((1, DH), head_row)],
        out_specs=(pl.BlockSpec((tt, DH), lambda hh, t: (rt(t), hh)), pl.BlockSpec((tt, DH), lambda hh, t: (rt(t), hh)),
                   pl.BlockSpec((1, DH, DH), hsm), pl.BlockSpec((1, 1, DH), hsm),
                   pl.BlockSpec((1, DH, DH), hsm), pl.BlockSpec((1, 1, DH), hsm),
                   pl.BlockSpec((1, 1, DH), hsm), pl.BlockSpec((1, 4, DH), hsm), pl.BlockSpec((1, 1, DH), hsm)),
        out_shape=(jax.ShapeDtypeStruct((s, D), BF16), jax.ShapeDtypeStruct((s, D), BF16),
                   jax.ShapeDtypeStruct((H, DH, DH), F32), jax.ShapeDtypeStruct((H, 1, DH), F32),
                   jax.ShapeDtypeStruct((H, DH, DH), F32), jax.ShapeDtypeStruct((H, 1, DH), F32),
                   jax.ShapeDtypeStruct((H, 1, DH), F32), jax.ShapeDtypeStruct((H, 4, DH), F32),
                   jax.ShapeDtypeStruct((H, 1, DH), F32)),
        scratch_shapes=[pltpu.VMEM((8, DH), F32), pltpu.VMEM((8, DH), F32), pltpu.VMEM((8, DH), F32)],
        compiler_params=_params(("parallel", "arbitrary")),
    )(p, p, p, hr, hr, dgr, pos, conv_w, conv_b, w_a, b_a, w_x, b_x, lam)


def _rope(t, c, sa, sb):
    return t * c + pltpu.roll(t, DH - ROT // 2, 1) * sa + pltpu.roll(t, ROT // 2, 1) * sb


def _rope_bwd(g, c, sa, sb):
    return g * c + pltpu.roll(g * sa, ROT // 2, 1) + pltpu.roll(g * sb, DH - ROT // 2, 1)


def _unit_bases(gi, u):
    dil = DILATIONS[gi]
    if dil == 1:
        return u * UB, SPAN + (u - 1) * UB, u == 0
    if dil == 4:
        blk, r = u // 4, u % 4
        return blk * 4 * UB + r, SPAN + (blk - 1) * 4 * UB + r, blk == 0
    return u, u, True


def _unit_slices(gi, u):
    dil = DILATIONS[gi]
    qb0, kb0, first = _unit_bases(gi, u)
    if dil == 1:
        if not isinstance(qb0, int):
            qb0, kb0 = pl.multiple_of(qb0, UB), pl.multiple_of(kb0, UB)
        return pl.ds(qb0, UB), pl.ds(kb0, 2 * UB), first
    return pl.ds(qb0, UB, stride=dil), pl.ds(kb0, 2 * UB, stride=dil), first


def _bdot(a, b):
    return lax.dot_general(a, b, (((2,), (1,)), ((0,), (0,))), preferred_element_type=F32)


def _bdot_nt(a, b):
    return lax.dot_general(a, b, (((2,), (2,)), ((0,), (0,))), preferred_element_type=F32)


def _bdot_tn(a, b):
    return lax.dot_general(a, b, (((1,), (1,)), ((0,), (0,))), preferred_element_type=F32)


def _band_mask(first_in_span, has_prev):
    qi = lax.broadcasted_iota(jnp.int32, (UB, 2 * UB), 0)
    ki = lax.broadcasted_iota(jnp.int32, (UB, 2 * UB), 1)
    dist = UB + qi - ki
    band = (dist >= 0) & (dist <= UB)
    return band & ((ki >= UB) | jnp.logical_not(first_in_span) | has_prev)


def _gather_halves(phase, src_hbm, dst_hbm, ss, rs):
    half = src_hbm.shape[0] // 2
    cx, cy, cc = _coords()
    j = 2 * cx + cy
    sib = (cx, cy, 1 - cc)
    mine = pl.ds(cc * half, half)
    theirs = pl.ds((1 - cc) * half, half)
    chips = ((1, 0), (0, 1), (1, 1))
    for q, (kx, ky) in enumerate(chips):
        jq = j ^ (2 * kx + ky)
        out = _rcopy(src_hbm.at[mine], dst_hbm.at[j, mine], ss.at[q], rs.at[q], (_flip(cx, kx), _flip(cy, ky), cc))
        landed = _rcopy(src_hbm.at[mine], dst_hbm.at[jq, mine], ss.at[q], rs.at[q], sib)
        onward = _rcopy(dst_hbm.at[jq, mine], dst_hbm.at[jq, mine], ss.at[3 + q], rs.at[3 + q], sib)
        from_sib = _rcopy(src_hbm.at[mine], dst_hbm.at[jq, theirs], ss.at[3 + q], rs.at[3 + q], sib)
        if phase == 0:
            out.start()
        elif phase == 1:
            landed.wait_recv()
            onward.start()
        else:
            from_sib.wait_recv()
            out.wait_send()
            onward.wait_send()


def _attn_fwd(p, rc, rsa, rsb, w3sh):
    s = p.shape[1]
    ns = s // SPAN
    nunit = SPAN // UB

    def body(q_ref, k_ref, v_ref, z_ref, c_ref, sa_ref, sb_ref, w3_hbm, o_ref, lse_ref, ga_ref, qro_ref, kro_ref,
             w3g_hbm, qr, kf, vf, acc, mm, ll, ss, rs):
        hh, n = pl.program_id(0), pl.program_id(1)
        for phase, at_head, at_span in ((0, 0, 0), (1, H // 2, 0), (2, H - 1, ns - 1)):
            @pl.when((hh == at_head) & (n == at_span))
            def _(phase=phase):
                _gather_halves(phase, w3_hbm, w3g_hbm, ss, rs)

        @pl.when(n == 0)
        def _():
            kf[0:SPAN] = jnp.zeros((SPAN, DH), F32)
            vf[0:SPAN] = jnp.zeros((SPAN, DH), F32)

        c, sa, sb = c_ref[...], sa_ref[...], sb_ref[...]
        q_rot = _rope(q_ref[0].astype(F32), c, sa, sb).astype(BF16)
        k_rot = _rope(k_ref[0].astype(F32), c, sa, sb).astype(BF16)
        qro_ref[...] = q_rot
        kro_ref[...] = k_rot
        qr[...] = q_rot.astype(F32)
        kf[SPAN:] = k_rot.astype(F32)
        vf[SPAN:] = v_ref[0].astype(F32)
        has_prev = n > 0

        for gi, dil in enumerate(DILATIONS):
            def trip(t, carry, gi=gi, dil=dil):
                qsls, ksls, firsts = [], [], []
                for b in range(UNIT_BATCH_FWD):
                    qsl, ksl, first = _unit_slices(gi, t * UNIT_BATCH_FWD + b)
                    qsls.append(qsl)
                    ksls.append(ksl)
                    firsts.append(first)
                qb = jnp.stack([qr[qsl, :].astype(BF16) for qsl in qsls])
                kb = jnp.stack([kf[ksl, :].astype(BF16) for ksl in ksls])
                vb = jnp.stack([vf[ksl, :].astype(BF16) for ksl in ksls])
                s_all = _bdot_nt(qb, kb)
                prs = []
                for b in range(UNIT_BATCH_FWD):
                    sc = jnp.where(_band_mask(firsts[b], has_prev), s_all[b] * SCALE, NEG)
                    m = jnp.max(sc, axis=-1, keepdims=True)
                    pr = jnp.exp(sc - m)
                    l = jnp.sum(pr, axis=-1, keepdims=True)
                    mm[gi, qsls[b], :] = jnp.broadcast_to(m, (UB, DH))
                    ll[gi, qsls[b], :] = jnp.broadcast_to(l, (UB, DH))
                    prs.append(pr.astype(BF16))
                o_all = _bdot(jnp.stack(prs), vb)
                for b in range(UNIT_BATCH_FWD):
                    acc[gi, qsls[b], :] = o_all[b]
                return carry

            lax.fori_loop(0, nunit // UNIT_BATCH_FWD, trip, 0)

        m_all =jnp.maximum(jnp.maximum(mm[0], mm[1]), mm[2])
        num = jnp.zeros((SPAN, DH), F32)
        den = jnp.zeros((SPAN, DH), F32)
        for gi in range(3):
            w = jnp.exp(mm[gi] - m_all)
            num = num + w * acc[gi]
            den = den + w * ll[gi]
        o = num / den
        o_ref[...] = o
        lse_ref[...] = m_all + jnp.log(den)
        z = z_ref[0].astype(F32)
        ga_ref[...] = (o * (z * _sigmoid(z))).astype(BF16)
        kf[0:SPAN] = kf[SPAN:]
        vf[0:SPAN] = vf[SPAN:]

    blk = lambda piece, off: pl.BlockSpec((1, SPAN, DH), lambda hh, n: (piece, n, off + hh))
    tab = pl.BlockSpec((SPAN, DH), lambda hh, n: (n, 0))
    outb = pl.BlockSpec((SPAN, DH), lambda hh, n: (n, hh))
    any_ = pl.BlockSpec(memory_space=pl.ANY)
    return pl.pallas_call(
        body, name="attn_fwd", grid=(H, ns),
        in_specs=[blk(1, 0), blk(1, H), blk(2, 0), blk(2, H), tab, tab, tab, any_],
        out_specs=(outb, outb, outb, outb, outb, any_),
        out_shape=(jax.ShapeDtypeStruct((s, D), F32), jax.ShapeDtypeStruct((s, D), F32),
                   jax.ShapeDtypeStruct((s, D), BF16), jax.ShapeDtypeStruct((s, D), BF16),
                   jax.ShapeDtypeStruct((s, D), BF16), jax.ShapeDtypeStruct((4,) + w3sh.shape, w3sh.dtype)),
        scratch_shapes=[pltpu.VMEM((SPAN, DH), F32), pltpu.VMEM((2 * SPAN, DH), F32), pltpu.VMEM((2 * SPAN, DH), F32),
                        pltpu.VMEM((3, SPAN, DH), F32), pltpu.VMEM((3, SPAN, DH), F32), pltpu.VMEM((3, SPAN, DH), F32),
                        pltpu.SemaphoreType.DMA((6,)), pltpu.SemaphoreType.DMA((6,))],
        compiler_params=_params(("arbitrary", "arbitrary")),
    )(p, p, p, p, rc, rsa, rsb, w3sh)


def _attn_bwd(p, q_rot, k_rot, o, lse, dga, rc, rsa, rsb, packs):
    s = p.shape[1]
    ns = s // SPAN
    nunit = SPAN // UB
    npk = len(packs)

    def body(*refs):
        (q_ref, k_ref, kp_ref, v_ref, vp_ref, z_ref, c_ref, sa_ref, sb_ref, o_ref, lse_ref, dg_ref) = refs[:12]
        pk_refs = refs[12:12 + npk]
        dq_ref, dk_ref, dv_ref, dz_ref = refs[12 + npk:16 + npk]
        red_refs = refs[16 + npk:16 + 2 * npk]
        qr, kf, vf, dof, dlt, dqa, dkf, dvf = refs[16 + 2 * npk:24 + 2 * npk]
        rbufs = refs[24 + 2 * npk:24 + 3 * npk]
        sums = refs[24 + 3 * npk:24 + 4 * npk]
        ar_sems = refs[24 + 4 * npk:]
        hh, step = pl.program_id(0), pl.program_id(1)
        n = ns - 1 - step
        has_prev = n > 0
        for phase, at_head, at_step in ((0, 0, 0), (1, H // 2, 0), (2, H - 1, ns - 1)):
            @pl.when((hh == at_head) & (step == at_step))
            def _(phase=phase):
                _allreduce_phase(phase, pk_refs, sums, rbufs, *ar_sems, out_refs=red_refs)

        @pl.when(step == 0)
        def _():
            dkf[...] = jnp.zeros_like(dkf)
            dvf[...] = jnp.zeros_like(dvf)

        @pl.when(step > 0)
        def _():
            dkf[SPAN:] = dkf[0:SPAN]
            dvf[SPAN:] = dvf[0:SPAN]
            dkf[0:SPAN] = jnp.zeros((SPAN, DH), F32)
            dvf[0:SPAN] = jnp.zeros((SPAN, DH), F32)

        c, sa, sb = c_ref[...], sa_ref[...], sb_ref[...]
        qr[...] = q_ref[...].astype(F32)
        kf[SPAN:] = k_ref[...].astype(F32)
        vf[SPAN:] = v_ref[0].astype(F32)
        kf[0:SPAN] = jnp.where(has_prev, kp_ref[...].astype(F32), 0.0)
        vf[0:SPAN] = jnp.where(has_prev, vp_ref[0].astype(F32), 0.0)
        z = z_ref[0].astype(F32)
        sz = _sigmoid(z)
        dg = dg_ref[...]
        ov = o_ref[...]
        do = dg * (z * sz)
        dz_ref[...] = (dg * ov * (sz * (1.0 + z * (1.0 - sz)))).astype(BF16)
        dof[...] = do
        dlt[...] = jnp.dot(do * ov, jnp.ones((DH, DH), F32), preferred_element_type=F32, precision=HIGHEST)
        dqa[...] = jnp.zeros_like(dqa)

        for gi, dil in enumerate(DILATIONS):
            def trip(t, carry, gi=gi, dil=dil):
                qsls, ksls, firsts = [], [], []
                for b in range(UNIT_BATCH):
                    qsl, ksl, first = _unit_slices(gi, t * UNIT_BATCH + b)
                    qsls.append(qsl)
                    ksls.append(ksl)
                    firsts.append(first)
                qb = jnp.stack([qr[qsl, :].astype(BF16) for qsl in qsls])
                kb = jnp.stack([kf[ksl, :].astype(BF16) for ksl in ksls])
                vb = jnp.stack([vf[ksl, :].astype(BF16) for ksl in ksls])
                dob = jnp.stack([dof[qsl, :].astype(BF16) for qsl in qsls])
                s_all = _bdot_nt(qb, kb)
                dp_all = _bdot_nt(dob, vb)
                prs, dss = [], []
                for b in range(UNIT_BATCH):
                    lse_b = lse_ref[qsls[b], :]
                    dl_b = dlt[qsls[b], :]
                    pr = jnp.exp(s_all[b] * SCALE - jnp.concatenate([lse_b, lse_b], axis=1))
                    pr = jnp.where(_band_mask(firsts[b], has_prev), pr, 0.0)
                    prs.append(pr.astype(BF16))
                    dss.append((pr * (dp_all[b] - jnp.concatenate([dl_b, dl_b], axis=1)) * SCALE).astype(BF16))
                ds_all = jnp.stack(dss)
                dv_all = _bdot_tn(jnp.stack(prs), dob)
                dq_all = _bdot(ds_all, kb)
                dk_all = _bdot_tn(ds_all, qb)
                for b in range(UNIT_BATCH):
                    dvf[ksls[b], :] += dv_all[b]
                    dqa[qsls[b], :] += dq_all[b]
                    dkf[ksls[b], :] += dk_all[b]
                return carry

            lax.fori_loop(0, nunit // UNIT_BATCH, trip, 0)

        dq_ref[...] = _rope_bwd(dqa[...], c, sa, sb).astype(BF16)
        dk_ref[...] = _rope_bwd(dkf[SPAN:], c, sa, sb).astype(BF16)
        dv_ref[...] = dvf[SPAN:].astype(BF16)

    rn = lambda n: ns - 1 - n
    pn = lambda n: jnp.maximum(ns - 2 - n, 0)
    blk = lambda piece, off: pl.BlockSpec((1, SPAN, DH), lambda hh, n: (piece, rn(n), off + hh))
    blkp = lambda piece, off: pl.BlockSpec((1, SPAN, DH), lambda hh, n: (piece, pn(n), off + hh))
    tab = pl.BlockSpec((SPAN, DH), lambda hh, n: (rn(n), 0))
    io = pl.BlockSpec((SPAN, DH), lambda hh, n: (rn(n), hh))
    iop = pl.BlockSpec((SPAN, DH), lambda hh, n: (pn(n), hh))
    vm = pl.BlockSpec(memory_space=pltpu.VMEM)
    outs = pl.pallas_call(
        body, name="attn_bwd", grid=(H, ns),
        in_specs=[io, io, iop, blk(2, 0), blkp(2, 0), blk(2, H), tab, tab, tab, io, io, io] + [vm] * npk,
        out_specs=(io, io, io, io) + (vm,) * npk,
        out_shape=tuple(jax.ShapeDtypeStruct((s, D), BF16) for _ in range(4)) +
                  tuple(jax.ShapeDtypeStruct(a.shape, F32) for a in packs),
        scratch_shapes=[pltpu.VMEM((SPAN, DH), F32), pltpu.VMEM((2 * SPAN, DH), F32), pltpu.VMEM((2 * SPAN, DH), F32),
                        pltpu.VMEM((SPAN, DH), F32), pltpu.VMEM((SPAN, DH), F32), pltpu.VMEM((SPAN, DH), F32),
                        pltpu.VMEM((2 * SPAN, DH), F32), pltpu.VMEM((2 * SPAN, DH), F32)] +
                       [pltpu.VMEM((NDEV, a.shape[0] // NDEV, a.shape[1]), F32) for a in packs] +
                       [pltpu.VMEM(a.shape, F32) for a in packs] +
                       [pltpu.SemaphoreType.DMA((7 * npk,)) for _ in range(4)],
        compiler_params=_params(("arbitrary", "arbitrary")),
    )(q_rot, k_rot, k_rot, p, p, p, rc, rsa, rsb, o, lse, dga, *packs)
    return outs[:4], outs[4:]


def _tail(gr, ga, p, x, tgt, w3, b_gate, gate, g_final, tm=256):
    s = x.shape[0]
    nt = s // tm

    def body(gr_ref, ga_ref, pr_ref, pa_ref, x_ref, t_ref, bg_ref, gate_ref, gf_ref, w_hbm,
             dgr_ref, dga_ref, dc_ref, dx2_ref, vec_ref, go_hbm, w_s, acc_s, sem):
        i = pl.program_id(0)

        @pl.when(i == 0)
        def _():
            cp = pltpu.make_async_copy(w_hbm, w_s, sem.at[12])
            cp.start()
            acc_s[...] = jnp.zeros_like(acc_s)
            vec_ref[...] = jnp.zeros_like(vec_ref)
            cp.wait()

        grb = gr_ref[...]
        gab = ga_ref[...]
        bg = bg_ref[...]
        gate_v = gate_ref[...]
        gf = gf_ref[...]
        y_r = _dot(grb, w_s[0])
        y_a = _dot(gab, w_s[1])
        sr = _sigmoid(pr_ref[0].astype(F32) + bg[:, :D])
        sa = _sigmoid(pa_ref[0].astype(F32) + bg[:, D:])
        mb = (sr * y_r + sa * y_a).astype(BF16)
        u = _dot(mb, w_s[2])
        x2 = x_ref[...] + gate_v * u
        rstd = lax.rsqrt(jnp.mean(x2 * x2, axis=-1, keepdims=True) + EPS)
        xh = x2 * rstd
        e = xh * gf - t_ref[...]
        dy = e * (1.0 / D)
        dyg = dy * gf
        dx2 = rstd * (dyg - xh * jnp.mean(dyg * xh, axis=-1, keepdims=True))
        dx2_ref[...] = dx2
        dub = (dx2 * gate_v).astype(BF16)
        dm = _dot_nt(dub, w_s[2])
        dyr = (dm * sr).astype(BF16)
        dya = (dm * sa).astype(BF16)
        dpr = dm * y_r * (sr * (1.0 - sr))
        dpa = dm * y_a * (sa * (1.0 - sa))
        dc_ref[:, :D] = dpr.astype(BF16)
        dc_ref[:, D:] = dpa.astype(BF16)
        dgr_ref[...] = _dot_nt(dyr, w_s[0])
        dga_ref[...] = _dot_nt(dya, w_s[1])
        acc_s[0] += _dot_tn(grb, dyr)
        acc_s[1] += _dot_tn(gab, dya)
        acc_s[2] += _dot_tn(mb, dub)
        vec_ref[0:1, :] += _colsum(dy * xh)
        vec_ref[1:2, :] += _colsum(dx2 * u)
        vec_ref[2:3, :] += _colsum(dpr)
        vec_ref[3:4, :] += _colsum(dpa)
        vec_ref[4:5, :] += _colsum(e * e)

        @pl.when(i == nt - 1)
        def _():
            vec_ref[4:5, :] = jnp.broadcast_to(jnp.sum(vec_ref[4:5, :]) * (0.5 / D), (1, D))
            cps = []
            for w in range(3):
                for j in range(4):
                    cps.append(pltpu.make_async_copy(acc_s.at[w, pl.ds(256 * j, 256)],
                                                     go_hbm.at[j, pl.ds(256 * w, 256)], sem.at[4 * w + j]))
            for cp in cps:
                cp.start()
            for cp in cps:
                cp.wait()

    rowt = lambda i: (i, 0)
    row = lambda w: pl.BlockSpec((1, w), lambda i: (0, 0))
    any_ = pl.BlockSpec(memory_space=pl.ANY)
    return pl.pallas_call(
        body, name="tail", grid=(nt,),
        in_specs=[pl.BlockSpec((tm, D), rowt), pl.BlockSpec((tm, D), rowt),
                  pl.BlockSpec((1, tm, D), lambda i: (3, i, 0)), pl.BlockSpec((1, tm, D), lambda i: (3, i, 1)),
                  pl.BlockSpec((tm, D), rowt), pl.BlockSpec((tm, D), rowt),
                  row(2 * D), row(D), row(D), any_],
        out_specs=(pl.BlockSpec((tm, D), rowt), pl.BlockSpec((tm, D), rowt), pl.BlockSpec((tm, 2 * D), rowt),
                   pl.BlockSpec((tm, D), rowt), pl.BlockSpec((8, D), lambda i: (0, 0)), any_),
        out_shape=(jax.ShapeDtypeStruct((s, D), F32), jax.ShapeDtypeStruct((s, D), F32),
                   jax.ShapeDtypeStruct((s, 2 * D), BF16), jax.ShapeDtypeStruct((s, D), F32),
                   jax.ShapeDtypeStruct((8, D), F32), jax.ShapeDtypeStruct((4, 768, D), F32)),
        scratch_shapes=[pltpu.VMEM((3, D, D), BF16), pltpu.VMEM((3, D, D), F32), pltpu.SemaphoreType.DMA((13,))],
        compiler_params=_params(("arbitrary",)),
    )(gr, ga, p, p, x, tgt, b_gate, gate, g_final, w3)


def _pieces_steps(pieces):
    out, s0 = [], 0
    for a in pieces:
        n = a.shape[1] // D
        out.append((s0, n))
        s0 += n
    return out, s0


def _inproj_bwd_x(pieces, wg, wsh, x, dx2, gn, scale, sums, tm=512):
    s = x.shape[0]
    np_ = len(pieces)
    na = len(sums)
    ni = s // tm
    groups, cur, width = [], [], 0
    for t, a in enumerate(pieces):
        cur.append(t)
        width += a.shape[1]
        if width == PW:
            groups.append(cur)
            cur, width = [], 0
    assert len(groups) == 4 and not cur

    def body(*refs):
        d_refs = refs[:np_]
        w_hbm, wsh_hbm, x_ref, dx2_ref, gn_ref, sc_ref = refs[np_:np_ + 6]
        q_refs = refs[np_ + 6:np_ + 6 + na]
        gx_ref, vec_ref = refs[np_ + 6 + na:np_ + 8 + na]
        r_refs = refs[np_ + 8 + na:np_ + 8 + 2 * na]
        w_s, wsem, ss, rs = refs[np_ + 8 + 2 * na:]
        i = pl.program_id(0)
        my_shard = 2 * lax.axis_index("x") + lax.axis_index("y")

        def scatter_copies():
            cx, cy, cc = _coords()
            j = 2 * cx + cy
            cps = []
            for t, (q, r) in enumerate(zip(q_refs, r_refs)):
                for e, (kx, ky) in enumerate(((1, 0), (0, 1), (1, 1))):
                    cps.append(_rcopy(q.at[j ^ (2 * kx + ky)], r.at[e], ss.at[3 * t + e], rs.at[3 * t + e],
                                      (_flip(cx, kx), _flip(cy, ky), cc)))
            return cps

        def w_copy(pc):
            return pltpu.make_async_copy(w_hbm.at[pc, pl.ds(0, D), :], w_s.at[pc], wsem.at[pc])

        for pc in range(4):
            @pl.when((i == 0) & (my_shard != pc))
            def _(pc=pc):
                w_copy(pc).start()

            @pl.when((i == 0) & (my_shard == pc))
            def _(pc=pc):
                pltpu.make_async_copy(wsh_hbm, w_s.at[pc], wsem.at[pc]).start()

        @pl.when(i == 0)
        def _():
            vec_ref[...] = jnp.zeros_like(vec_ref)
            for cp in scatter_copies():
                cp.start()

        dh = None
        for pc, group in enumerate(groups):
            @pl.when(i == 0)
            def _(pc=pc):
                w_copy(pc).wait()

            tiles = [d_refs[t][...] for t in group]
            lhs = tiles[0] if len(tiles) == 1 else jnp.concatenate(tiles, axis=1)
            part = _dot_nt(lhs, w_s[pc])
            dh = part if dh is None else dh + part

        xt = x_ref[...]
        rstd = lax.rsqrt(jnp.mean(xt * xt, axis=-1, keepdims=True) + EPS)
        xh = xt * rstd
        gn_v = gn_ref[...]
        sc1 = 1.0 + sc_ref[...]
        dhx = dh * xh
        vec_ref[0:1, :] += _colsum(dh)
        vec_ref[1:2, :] += _colsum(dhx) * gn_v
        vec_ref[2:3, :] += _colsum(dhx) * sc1
        dxh = dh * (gn_v * sc1)
        gx_ref[...] = rstd * (dxh - xh * jnp.mean(dxh * xh, axis=-1, keepdims=True)) + dx2_ref[...]

        @pl.when(i == ni - 1)
        def _():
            for cp in scatter_copies():
                cp.wait()

    rowt = lambda i: (i, 0)
    row = pl.BlockSpec((1, D), lambda i: (0, 0))
    any_ = pl.BlockSpec(memory_space=pl.ANY)
    outs = pl.pallas_call(
        body, name="inproj_bwd_x", grid=(ni,),
        in_specs=[pl.BlockSpec((tm, a.shape[1]), rowt) for a in pieces] +
                 [any_, any_, pl.BlockSpec((tm, D), rowt), pl.BlockSpec((tm, D), rowt), row, row] + [any_] * na,
        out_specs=(pl.BlockSpec((tm, D), rowt), pl.BlockSpec((8, D), lambda i: (0, 0))) + (any_,) * na,
        out_shape=(jax.ShapeDtypeStruct((s, D), F32), jax.ShapeDtypeStruct((8, D), F32)) +
                  tuple(jax.ShapeDtypeStruct((3,) + q.shape[1:], q.dtype) for q in sums),
        scratch_shapes=[pltpu.VMEM((4, D, PW), BF16), pltpu.SemaphoreType.DMA((4,)),
                        pltpu.SemaphoreType.DMA((3 * na,)), pltpu.SemaphoreType.DMA((3 * na,))],
        compiler_params=_params(("arbitrary",)),
    )(*pieces, wg, wsh, x, dx2, gn, scale, *sums)
    return outs[0], outs[1], outs[2:]


def _inproj_bwd_w(pieces, hbf, g_out, tk=1024):
    s = hbf.shape[0]
    steps, nk = _pieces_steps(pieces)
    npc = PW // D
    ns = s // tk
    np_ = len(pieces)
    hr = D // 2
    ohr = g_out.shape[1] // 2
    ocr = _chunk_rows(g_out)
    ochunks = [(j, r0) for j in range(g_out.shape[0]) for r0 in range(0, ohr, ocr)]
    noc = len(ochunks)

    def body(*refs):
        d_refs = refs[:np_]
        h_ref, go_hbm, g_ref, rb_hbm, rbo_hbm, stage, ss, rs, oss, ors = refs[np_:]
        cb, k = pl.program_id(0), pl.program_id(1)
        cx, cy, cc = _coords()
        sib = (cx, cy, 1 - cc)

        def block_copy(b):
            return _rcopy(stage.at[b % 2],
                          rb_hbm.at[b // npc, :, pl.ds(pl.multiple_of((b % npc) * D, D), D)], ss.at[b], rs.at[b], sib)

        def out_copy(e):
            j, r0 = ochunks[e]
            return _rcopy(go_hbm.at[j, pl.ds((1 - cc) * ohr + r0, ocr), :], rbo_hbm.at[j, pl.ds(r0, ocr), :],
                          oss.at[e], ors.at[e], sib)

        @pl.when((cb == 0) & (k == 0))
        def _():
            for e in range(noc):
                out_copy(e).start()

        @pl.when(k == 0)
        def _():
            g_ref[...] = jnp.zeros_like(g_ref)

        for (s0, n), d_ref in zip(steps, d_refs):
            @pl.when((cb >= s0) & (cb < s0 + n))
            def _(d_ref=d_ref):
                g_ref[0] += _dot_tn(h_ref[...], d_ref[...])

        @pl.when((k == ns - 1) & (cb > 1))
        def _():
            block_copy(cb - 2).wait_send()

        @pl.when(k == ns - 1)
        def _():
            stage[cb % 2] = g_ref[0, pl.ds(pl.multiple_of((1 - cc) * hr, hr), hr), :]
            block_copy(cb).start()

        @pl.when((k == ns - 1) & (cb == nk - 1))
        def _():
            block_copy(nk - 2).wait_send()
            block_copy(nk - 1).wait_send()
            for b in range(nk):
                block_copy(b).wait_recv()
            for e in range(noc):
                out_copy(e).wait_recv()
                out_copy(e).wait_send()

    def piece_spec(s0, n):
        def imap(cb, k):
            active = (cb >= s0) & (cb < s0 + n)
            return (jnp.where(active, k, 0), jnp.clip(cb - s0, 0, n - 1))
        return pl.BlockSpec((tk, D), imap)

    any_ = pl.BlockSpec(memory_space=pl.ANY)
    return pl.pallas_call(
        body, name="inproj_bwd_w", grid=(nk, ns),
        in_specs=[piece_spec(s0, n) for s0, n in steps] + [pl.BlockSpec((tk, D), lambda cb, k: (k, 0)), any_],
        out_specs=(pl.BlockSpec((1, D, D), lambda cb, k: (cb // npc, 0, cb % npc)), any_, any_),
        out_shape=(jax.ShapeDtypeStruct((4, D, PW), F32), jax.ShapeDtypeStruct((4, hr, PW), F32),
                   jax.ShapeDtypeStruct((g_out.shape[0], ohr, g_out.shape[2]), F32)),
        scratch_shapes=[pltpu.VMEM((2, hr, D), F32), pltpu.SemaphoreType.DMA((nk,)), pltpu.SemaphoreType.DMA((nk,)),
                        pltpu.SemaphoreType.DMA((noc,)), pltpu.SemaphoreType.DMA((noc,))],
        compiler_params=_params(("arbitrary", "arbitrary")),
    )(*pieces, hbf, g_out)


D2D_CHUNK_BYTES = 512 * 1024


def _chunk_rows(a):
    return max(8, D2D_CHUNK_BYTES // (a.shape[-1] * a.dtype.itemsize))


def _pair_swap(arrs):
    na = len(arrs)
    chunks = []
    for t, a in enumerate(arrs):
        cr = _chunk_rows(a)
        chunks += [(t, r0, cr) for r0 in range(0, a.shape[0], cr)]
    nch = len(chunks)

    def body(*refs):
        a_refs = refs[:na]
        o_refs = refs[na:2 * na]
        ss, rs = refs[2 * na:]
        x, y, c = _coords()
        sib = (x, y, 1 - c)
        rcs = []
        for n, (t, r0, cr) in enumerate(chunks):
            rows = pl.ds(r0, cr)
            rc = _rcopy(a_refs[t].at[rows, :], o_refs[t].at[rows, :], ss.at[n], rs.at[n], sib)
            rc.start()
            rcs.append(rc)
        for rc in rcs:
            rc.wait_recv()
        for rc in rcs:
            rc.wait_send()

    any_ = pl.BlockSpec(memory_space=pl.ANY)
    return pl.pallas_call(
        body, name="pair_swap",
        out_shape=tuple(jax.ShapeDtypeStruct(a.shape, a.dtype) for a in arrs),
        in_specs=[any_] * na, out_specs=tuple([any_] * na),
        scratch_shapes=[pltpu.SemaphoreType.DMA((nch,)), pltpu.SemaphoreType.DMA((nch,))],
        compiler_params=_params(),
    )(*arrs)


def _add_half(full, rb, core, tr):
    n, r, cdim = full.shape
    nb = r // 2 // tr

    def body(c_ref, a_ref, b_ref, o_ref, ob_ref):
        tot = a_ref[...] + b_ref[...]
        o_ref[...] = tot
        ob_ref[...] = tot.astype(BF16)

    mine = pl.BlockSpec((1, tr, cdim), lambda i, j, c_ref: (i, c_ref[0] * nb + j, 0))
    spec = pl.BlockSpec((1, tr, cdim), lambda i, j, c_ref: (i, j, 0))
    return pl.pallas_call(
        body, name="add_half",
        grid_spec=pltpu.PrefetchScalarGridSpec(num_scalar_prefetch=1, grid=(n, nb), in_specs=[mine, spec],
                                               out_specs=(spec, spec)),
        out_shape=(jax.ShapeDtypeStruct(rb.shape, rb.dtype), jax.ShapeDtypeStruct(rb.shape, BF16)),
        compiler_params=_params(("parallel", "parallel")),
    )(core, full, rb)


def _sum_slots(q, r3, shard, tr):
    _, r, cdim = q.shape

    def body(j_ref, q_ref, r_ref, o_ref):
        o_ref[...] = ((q_ref[0] + r_ref[0].astype(F32)) + r_ref[1].astype(F32)) + r_ref[2].astype(F32)

    return pl.pallas_call(
        body, name="sum_slots",
        grid_spec=pltpu.PrefetchScalarGridSpec(
            num_scalar_prefetch=1, grid=(r // tr,),
            in_specs=[pl.BlockSpec((1, tr, cdim), lambda i, j_ref: (j_ref[0], i, 0)),
                      pl.BlockSpec((3, tr, cdim), lambda i, j_ref: (0, i, 0))],
            out_specs=pl.BlockSpec((tr, cdim), lambda i, j_ref: (i, 0))),
        out_shape=jax.ShapeDtypeStruct((r, cdim), q.dtype),
        compiler_params=_params(("parallel",)),
    )(shard, q, r3)


def _allreduce_phase(phase, p_refs, o_refs, rbufs, s1, r1, s2, r2, out_refs=None):
    me = _my_index()

    def chunk(t, d):
        ch = p_refs[t].shape[0] // NDEV
        return pl.ds(pl.multiple_of(d * ch, 8), ch)

    def scatter(t, k):
        e = 7 * t + k - 1
        return _rcopy(p_refs[t].at[chunk(t, me ^ k)], rbufs[t].at[me], s1.at[e], r1.at[e], _peer(k))

    def gather(t, k):
        e = 7 * t + k - 1
        return _rcopy(o_refs[t].at[chunk(t, me)], o_refs[t].at[chunk(t, me)], s2.at[e], r2.at[e], _peer(k))

    for t in range(len(p_refs)):
        if phase == 0:
            for k in range(1, NDEV):
                scatter(t, k).start()
            rbufs[t][me] = p_refs[t][chunk(t, me), :]
        elif phase == 1:
            for k in range(1, NDEV):
                e = 7 * t + k - 1
                _rcopy(p_refs[t].at[chunk(t, me)], rbufs[t].at[me ^ k], s1.at[e], r1.at[e], _peer(k)).wait_recv()
            tot = rbufs[t][0]
            for d in range(1, NDEV):
                tot = tot + rbufs[t][d]
            o_refs[t][chunk(t, me), :] = tot
            for k in range(1, NDEV):
                gather(t, k).start()
        else:
            for k in range(1, NDEV):
                e = 7 * t + k - 1
                _rcopy(o_refs[t].at[chunk(t, me)], o_refs[t].at[chunk(t, me ^ k)], s2.at[e], r2.at[e],
                       _peer(k)).wait_recv()
            for k in range(1, NDEV):
                scatter(t, k).wait_send()
                gather(t, k).wait_send()
            if out_refs is not None:
                out_refs[t][...] = o_refs[t][...]


def _adamw_update(w, g, m, v):
    nm = B1 * m + (1.0 - B1) * g
    nv = B2 * v + (1.0 - B2) * (g * g)
    m_hat = nm / (1.0 - B1 ** STEP)
    v_hat = nv / (1.0 - B2 ** STEP)
    return -LR * (m_hat / (jnp.sqrt(v_hat) + ADAM_EPS) + WD * w), nm, nv


def _adamw(w, g, m, v, tr):
    r, cdim = w.shape

    def body(w_ref, g_ref, m_ref, v_ref, d_ref, nm_ref, nv_ref):
        d_ref[...], nm_ref[...], nv_ref[...] = _adamw_update(w_ref[...], g_ref[...], m_ref[...], v_ref[...])

    spec = pl.BlockSpec((tr, cdim), lambda i: (i, 0))
    sd = jax.ShapeDtypeStruct((r, cdim), F32)
    return pl.pallas_call(
        body, name="adamw", grid=(r // tr,), in_specs=[spec] * 4, out_specs=(spec,) * 3, out_shape=(sd,) * 3,
        compiler_params=_params(("parallel",)),
    )(w, g, m, v)


V_B_GATE, V_CONV_B, V_LAM, V_G_FINAL, V_CONV_W, V_LOSS, V_ROWS = 0, 2, 3, 4, 5, 9, 64
M_W_A, M_W_X, M_B_A, M_B_X, M_ROWS = 0, H * DH, 2 * H * DH, 2 * H * DH + H, 2112
SMALL = ("g_norm", "b_mod", "b_gate", "conv_b", "lam", "g_final", "conv_w", "w_a", "w_x", "b_a", "b_x")


def _adamw_small(redv, redm, g_conv, g_gnorm, g_bmod, wmv):
    def grad(name, rv, rm, gc, gg, gb):
        if name == "g_norm":
            return gg[...]
        if name == "b_mod":
            return gb[...]
        if name == "b_gate":
            return jnp.concatenate([rv[V_B_GATE + t:V_B_GATE + t + 1, :] for t in range(2)], axis=1)
        if name == "conv_b":
            return rv[V_CONV_B:V_CONV_B + 1, :]
        if name == "lam":
            return rv[V_LAM:V_LAM + 1, :]
        if name == "g_final":
            return rv[V_G_FINAL:V_G_FINAL + 1, :]
        if name == "conv_w":
            return gc[...]
        if name == "w_a":
            return rm[M_W_A:M_W_A + H * DH, :]
        if name == "w_x":
            return rm[M_W_X:M_W_X + H * DH, :]
        if name == "b_a":
            return rm[M_B_A:M_B_A + H, :]
        return rm[M_B_X:M_B_X + H, :]

    n = len(SMALL)

    def body(*refs):
        rv, rm, gc, gg, gb = refs[:5]
        ins, outs = refs[5:5 + 3 * n], refs[5 + 3 * n:]
        for t, name in enumerate(SMALL):
            w_ref, m_ref, v_ref = ins[3 * t:3 * t + 3]
            g_out, d_out, m_out, v_out = outs[4 * t:4 * t + 4]
            g = grad(name, rv, rm, gc, gg, gb)
            g_out[...] = g
            d_out[...], m_out[...], v_out[...] = _adamw_update(w_ref[...], g, m_ref[...], v_ref[...])

    vm = pl.BlockSpec(memory_space=pltpu.VMEM)
    flat = [a for name in SMALL for a in wmv[name]]
    shapes = [jax.ShapeDtypeStruct(wmv[name][0].shape, F32) for name in SMALL for _ in range(4)]
    outs = pl.pallas_call(
        body, name="adamw_small", out_shape=tuple(shapes),
        in_specs=[vm] * (5 + len(flat)), out_specs=tuple([vm] * len(shapes)),
        compiler_params=_params(),
    )(redv, redm, g_conv, g_gnorm, g_bmod, *flat)
    return {name: outs[4 * t:4 * t + 4] for t, name in enumerate(SMALL)}


def _rope_tables(positions):
    inv_freq = ROPE_THETA ** (-jnp.arange(0, ROT, 2, dtype=F32) / ROT)
    ang = positions.astype(F32)[:, None] * inv_freq
    cos, sin = jnp.cos(ang), jnp.sin(ang)
    n = positions.shape[0]
    half = ROT // 2
    rc = jnp.concatenate([cos, cos, jnp.ones((n, DH - ROT), F32)], axis=1)
    rsa = jnp.concatenate([-sin, jnp.zeros((n, DH - half), F32)], axis=1)
    rsb = jnp.concatenate([jnp.zeros((n, half), F32), sin, jnp.zeros((n, DH - ROT), F32)], axis=1)
    return rc, rsa, rsb


def kernel(x, c, positions, g_norm, w_mod, b_mod, w_in, b_gate, conv_w, conv_b, w_a, b_a, w_x, b_x, lam, w_out_rnn, w_out_attn, w_o, g_final, loss_target, m_g_norm, m_w_mod, m_b_mod, m_w_in, m_b_gate, m_conv_w, m_conv_b, m_w_a, m_b_a, m_w_x, m_b_x, m_lam, m_w_out_rnn, m_w_out_attn, m_w_o, m_g_final, v_g_norm, v_w_mod, v_b_mod, v_w_in, v_b_gate, v_conv_w, v_conv_b, v_w_a, v_b_a, v_w_x, v_b_x, v_lam, v_w_out_rnn, v_w_out_attn, v_w_o, v_g_final):
    s = x.shape[1]
    xi = lax.axis_index("x")
    yi = lax.axis_index("y")
    ci = lax.axis_index("c")
    shard = 2 * xi + yi
    x2d = x[0]
    tgt = loss_target[0]
    pos = positions[0]

    c_all, mod4, conv_all = _mod_fwd(c, w_mod[0], b_mod.reshape(4, 1, 768), conv_w[0])
    mod = mod4.reshape(1, 3 * D)
    shift, scale, gate = mod[:, :D], mod[:, D:2 * D], mod[:, 2 * D:]
    w3sh = jnp.concatenate([w_out_rnn[0], w_out_attn[0], w_o[0]], axis=0).astype(BF16)
    wsh = w_in[0].astype(BF16)
    conv_full = conv_all[0::2].transpose(1, 0, 2).reshape(4, D)

    order = jnp.stack([shard, shard ^ 2, shard ^ 1, shard ^ 3]).astype(jnp.int32)
    p, hbf, wg = _gather_norm_inproj(x2d, g_norm, shift, scale, wsh, order)
    rc, rsa, rsb = _rope_tables(pos)
    pos_col = jnp.broadcast_to((pos == 0).astype(F32)[:, None], (s, DH))
    b_a3, b_x3 = b_a.reshape(H, 1, DH), b_x.reshape(H, 1, DH)
    hr, gr = _rnn_fwd(p, pos_col, conv_full, conv_b, w_a[0], b_a3, w_x[0], b_x3, lam)
    o, lse, ga, q_rot, k_rot, w3g = _attn_fwd(p, rc, rsa, rsb, w3sh)
    w3g = lax.dynamic_update_slice(w3g, w3sh[None], (shard, 0, 0))
    w3 = w3g.reshape(4, 3, 256, D).transpose(1, 0, 2, 3).reshape(3, D, D)

    dgr, dga, dc, dx2, vec_t, g_out = _tail(gr, ga, p, x2d, tgt, w3, b_gate, gate, g_final.reshape(1, D))

    dxr, dzr, g_wa, g_ba, g_wx, g_bx, g_lam, g_cw, g_cb = _rnn_bwd(
        p, hr, dgr, pos_col, conv_full, conv_b, w_a[0], b_a3, w_x[0], b_x3, lam)
    vpack = jnp.concatenate([
        vec_t[2:4],
        g_cb.reshape(1, D),
        g_lam.reshape(1, D),
        vec_t[0:1],
        g_cw.transpose(1, 0, 2).reshape(4, D),
        vec_t[4:5],
        jnp.zeros((V_ROWS - 10, D), F32)], axis=0)
    mpack = jnp.concatenate([
        g_wa.reshape(H * DH, DH), g_wx.reshape(H * DH, DH), g_ba.reshape(H, DH), g_bx.reshape(H, DH),
        jnp.zeros((M_ROWS - 2 * H * DH - 2 * H, DH), F32)], axis=0)
    (dq, dk, dv, dza), (redv, redm) = _attn_bwd(p, q_rot, k_rot, o, lse, dga, rc, rsa, rsb, [vpack, mpack])

    pieces = [dxr, dzr, dq, dk, dv, dza, dc]
    g_win, rb_a, rb_b = _inproj_bwd_w(pieces, hbf, g_out)

    core = ci.reshape(1)
    shard1 = shard.reshape(1)
    (q_a, qh_a), (q_b, qh_b) = _add_half(g_win, rb_a, core, tr=256), _add_half(g_out, rb_b, core, tr=128)
    grad_x, vec_n, (r_a, r_b) = _inproj_bwd_x(pieces, wg, wsh, x2d, dx2, g_norm, scale, [qh_a, qh_b])
    f_a, f_b = _sum_slots(q_a, r_a, shard1, tr=256), _sum_slots(q_b, r_b, shard1, tr=128)
    s_a, s_b = _pair_swap([f_a, f_b])
    south = ci == 0
    grad_w_in = jnp.where(south, jnp.concatenate([f_a, s_a], axis=0), jnp.concatenate([s_a, f_a], axis=0))
    g3 = jnp.where(south, jnp.concatenate([f_b, s_b], axis=0), jnp.concatenate([s_b, f_b], axis=0)).reshape(3, 256, D)

    dmod_row = jnp.concatenate([vec_n[0:1], vec_n[1:2], vec_t[1:2]], axis=1)
    loss = redv[V_LOSS, 0]
    grad_w_mod, g_bmod4, g_gnorm = _mod_bwd(dmod_row.reshape(4, 1, 768), vec_n[2:3], c_all)
    g_conv_sh = lax.dynamic_slice_in_dim(redv[V_CONV_W:V_CONV_W + 4], shard * 256, 256, axis=1)

    shape2d = dict(g_norm=(1, D), b_mod=(1, 3 * D), b_gate=(1, 2 * D), conv_b=(1, D), lam=(1, D), g_final=(1, D),
                   conv_w=(4, 256), w_a=(H * DH, DH), w_x=(H * DH, DH), b_a=(H, DH), b_x=(H, DH))
    given = dict(
        g_norm=(g_norm, m_g_norm, v_g_norm), b_mod=(b_mod, m_b_mod, v_b_mod), b_gate=(b_gate, m_b_gate, v_b_gate),
        conv_b=(conv_b, m_conv_b, v_conv_b), lam=(lam, m_lam, v_lam), g_final=(g_final, m_g_final, v_g_final),
        conv_w=(conv_w, m_conv_w, v_conv_w), w_a=(w_a, m_w_a, v_w_a), w_x=(w_x, m_w_x, v_w_x),
        b_a=(b_a, m_b_a, v_b_a), b_x=(b_x, m_b_x, v_b_x))
    small = _adamw_small(redv, redm, g_conv_sh, g_gnorm, g_bmod4.reshape(1, 3 * D),
                         {n: tuple(a.reshape(shape2d[n]) for a in given[n]) for n in SMALL})

    big_in = _adamw(w_in[0], grad_w_in, m_w_in[0], v_w_in[0], tr=256)
    big_mod = _adamw(w_mod[0], grad_w_mod, m_w_mod[0], v_w_mod[0], tr=256)
    w3f = jnp.concatenate([w_out_rnn[0], w_out_attn[0], w_o[0]], axis=0)
    m3f = jnp.concatenate([m_w_out_rnn[0], m_w_out_attn[0], m_w_o[0]], axis=0)
    v3f = jnp.concatenate([v_w_out_rnn[0], v_w_out_attn[0], v_w_o[0]], axis=0)
    big_out = _adamw(w3f, g3.reshape(768, D), m3f, v3f, tr=256)

    names = ["g_norm", "w_mod", "b_mod", "w_in", "b_gate", "conv_w", "conv_b", "w_a", "b_a", "w_x", "b_x", "lam",
             "w_out_rnn", "w_out_attn", "w_o", "g_final"]
    outs = []
    for idx in range(4):
        d = {n: small[n][idx].reshape(given[n][0].shape) for n in SMALL}
        if idx == 0:
            d.update(w_mod=grad_w_mod[None], w_in=grad_w_in[None],
                     w_out_rnn=g3[0][None], w_out_attn=g3[1][None], w_o=g3[2][None])
        else:
            d.update(w_mod=big_mod[idx - 1][None], w_in=big_in[idx - 1][None],
                     w_out_rnn=big_out[idx - 1][0:256][None], w_out_attn=big_out[idx - 1][256:512][None],
                     w_o=big_out[idx - 1][512:768][None])
        outs.append(d)
    flat = [d[n] for d in outs for n in names]
    return (loss, grad_x[None], *flat)
```

```python
import jax
import jax.numpy as jnp
from jax import lax
from jax.experimental import pallas as pl
from jax.experimental.pallas import tpu as pltpu

F32, BF16 = jnp.float32, jnp.bfloat16
MESH = pl.DeviceIdType.MESH
HIGHEST = lax.Precision.HIGHEST

D = 1024
H = 8
DH = 128
PW = 2048
EPS = 1e-6
LRU_C = 8.0
SCALE = DH ** -0.5
NEG = -1e30
SPAN = 2048
UB = 128
DILATIONS = (1, 4, 16)
UNIT_BATCH = 16
UNIT_BATCH_FWD = 8
ROPE_THETA = 500000.0
ROT = 32

LR, B1, B2, ADAM_EPS, WD, STEP = 0.001, 0.9, 0.999, 1e-08, 0.01, 10

NDEV = 8


def _params(sem=None, vmem_mb=56):
    return pltpu.CompilerParams(dimension_semantics=sem, vmem_limit_bytes=vmem_mb * 2 ** 20)


def _coords():
    return lax.axis_index("x"), lax.axis_index("y"), lax.axis_index("c")


def _flip(v, bit):
    return 1 - v if bit else v


def _peer(k):
    x, y, c = _coords()
    return (_flip(x, (k >> 2) & 1), _flip(y, (k >> 1) & 1), _flip(c, k & 1))


def _my_index():
    x, y, c = _coords()
    return 4 * x + 2 * y + c


def _rcopy(src, dst, ssem, rsem, dev):
    return pltpu.make_async_remote_copy(src_ref=src, dst_ref=dst, send_sem=ssem, recv_sem=rsem,
                                        device_id=dev, device_id_type=MESH)


def _sigmoid(x):
    return jax.nn.sigmoid(x)


def _dot(a, b):
    return jnp.dot(a, b, preferred_element_type=F32)


def _dot_nt(a, b):
    return lax.dot_general(a, b, (((1,), (1,)), ((), ())), preferred_element_type=F32)


def _dot_tn(a, b):
    return lax.dot_general(a, b, (((0,), (0,)), ((), ())), preferred_element_type=F32)


def _colsum(a):
    return jnp.sum(a, axis=0, keepdims=True)


def _mod_fwd(c, w_mod_sh, b_mod4, conv_sh):
    def body(c_ref, w_ref, b_ref, cv_ref, call_ref, mod_ref, cvall_ref, rows_ref, cmat_ref, s1, r1, s2, r2, s3, r3):
        x, y, _ = _coords()
        me = _my_index()
        j = 2 * x + y
        call_ref[me] = c_ref[...]
        cvall_ref[me] = cv_ref[...]
        sends = []
        for k in range(1, NDEV):
            cp = _rcopy(call_ref.at[me], call_ref.at[me], s1.at[k - 1], r1.at[k - 1], _peer(k))
            cp.start()
            sends.append(cp)
            cp = _rcopy(cvall_ref.at[me], cvall_ref.at[me], s3.at[k - 1], r3.at[k - 1], _peer(k))
            cp.start()
            sends.append(cp)
        for k in range(1, NDEV):
            pk = me ^ k
            _rcopy(call_ref.at[pk], call_ref.at[pk], s1.at[k - 1], r1.at[k - 1], _peer(k)).wait_recv()
        for b in range(NDEV):
            cmat_ref[pl.ds(b, 1), :] = call_ref[b]
        cm = cmat_ref[...]
        act = cm * _sigmoid(cm)
        mp = jnp.dot(act, w_ref[...], preferred_element_type=F32, precision=HIGHEST) + b_ref[j]
        for b in range(NDEV):
            rows_ref[b] = mp[b:b + 1]
        mod_ref[j] = rows_ref[me]
        for q, k in enumerate((2, 4, 6)):
            cp = _rcopy(rows_ref.at[me ^ k], mod_ref.at[j], s2.at[q], r2.at[q], _peer(k))
            cp.start()
            sends.append(cp)
        for q, k in enumerate((2, 4, 6)):
            jq = j ^ (k >> 1)
            _rcopy(rows_ref.at[me], mod_ref.at[jq], s2.at[q], r2.at[q], _peer(k)).wait_recv()
        for k in range(1, NDEV):
            pk = me ^ k
            _rcopy(cvall_ref.at[pk], cvall_ref.at[pk], s3.at[k - 1], r3.at[k - 1], _peer(k)).wait_recv()
        for cp in sends:
            cp.wait_send()

    vm = pl.BlockSpec(memory_space=pltpu.VMEM)
    return pl.pallas_call(
        body, name="mod_fwd",
        out_shape=(jax.ShapeDtypeStruct((NDEV, 1, D), F32), jax.ShapeDtypeStruct((4, 1, 768), F32),
                   jax.ShapeDtypeStruct((NDEV,) + conv_sh.shape, F32)),
        in_specs=[vm, vm, vm, vm], out_specs=(vm, vm, vm),
        scratch_shapes=[pltpu.VMEM((NDEV, 1, 768), F32), pltpu.VMEM((NDEV, D), F32),
                        pltpu.SemaphoreType.DMA((7,)), pltpu.SemaphoreType.DMA((7,)),
                        pltpu.SemaphoreType.DMA((3,)), pltpu.SemaphoreType.DMA((3,)),
                        pltpu.SemaphoreType.DMA((7,)), pltpu.SemaphoreType.DMA((7,))],
        compiler_params=_params(),
    )(c, w_mod_sh, b_mod4, conv_sh)


def _mod_bwd(dmod4, gn_row, c_all):
    def body(d_ref, g_ref, call_ref, gw_ref, gb_ref, gg_ref, dall_ref, gall_ref, cmat_ref, dmat_ref, s1, r1, s2, r2):
        x, y, _ = _coords()
        me = _my_index()
        j = 2 * x + y
        dall_ref[me] = d_ref[...]
        gall_ref[me] = g_ref[...]
        sends = []
        for k in range(1, NDEV):
            for buf, ss, rs in ((dall_ref, s1, r1), (gall_ref, s2, r2)):
                cp = _rcopy(buf.at[me], buf.at[me], ss.at[k - 1], rs.at[k - 1], _peer(k))
                cp.start()
                sends.append(cp)
        for k in range(1, NDEV):
            pk = me ^ k
            for buf, ss, rs in ((dall_ref, s1, r1), (gall_ref, s2, r2)):
                _rcopy(buf.at[pk], buf.at[pk], ss.at[k - 1], rs.at[k - 1], _peer(k)).wait_recv()
        for cp in sends:
            cp.wait_send()
        gb, gg = dall_ref[0], gall_ref[0]
        for b in range(1, NDEV):
            gb = gb + dall_ref[b]
            gg = gg + gall_ref[b]
        gb_ref[...] = gb
        gg_ref[...] = gg
        for b in range(NDEV):
            cmat_ref[pl.ds(b, 1), :] = call_ref[b]
            dmat_ref[pl.ds(b, 1), :] = dall_ref[b, j]
        cm = cmat_ref[...]
        act = cm * _sigmoid(cm)
        gw_ref[...] = lax.dot_general(act, dmat_ref[...], (((0,), (0,)), ((), ())),
                                      preferred_element_type=F32, precision=HIGHEST)

    vm = pl.BlockSpec(memory_space=pltpu.VMEM)
    return pl.pallas_call(
        body, name="mod_bwd",
        out_shape=(jax.ShapeDtypeStruct((D, 768), F32), jax.ShapeDtypeStruct((4, 1, 768), F32),
                   jax.ShapeDtypeStruct((1, D), F32)),
        in_specs=[vm, vm, vm], out_specs=(vm, vm, vm),
        scratch_shapes=[pltpu.VMEM((NDEV, 4, 1, 768), F32), pltpu.VMEM((NDEV, 1, D), F32),
                        pltpu.VMEM((NDEV, D), F32), pltpu.VMEM((NDEV, 768), F32),
                        pltpu.SemaphoreType.DMA((7,)), pltpu.SemaphoreType.DMA((7,)),
                        pltpu.SemaphoreType.DMA((7,)), pltpu.SemaphoreType.DMA((7,))],
        compiler_params=_params(),
    )(dmod4, gn_row, c_all)


def _gather_norm_inproj(x, gn, shift, scale, wsh, order, tm=1024, tn=1024):
    s = x.shape[0]
    ni = s // tm
    npc = PW // tn
    rows, cols = wsh.shape
    half = rows // 2
    nch = 4
    cr = half // nch
    chips = ((1, 0), (0, 1), (1, 1))

    def body(ord_ref, x_ref, gn_ref, sh_ref, sc_ref, wsh_hbm, p_ref, h_ref, wg_hbm, hs_all, w_s, wsem, ss, rs):
        slot, i, col = pl.program_id(0), pl.program_id(1), pl.program_id(2)
        cx, cy, cc = _coords()
        j = 2 * cx + cy
        sib = (cx, cy, 1 - cc)
        mine = lambda n: pl.ds(cc * half + n * cr, cr)
        theirs = lambda n: pl.ds((1 - cc) * half + n * cr, cr)
        shard_of = lambda q: j ^ (2 * chips[q][0] + chips[q][1])

        def to_chip(q, n):
            e = nch * q + n
            return _rcopy(wsh_hbm.at[mine(n)], wg_hbm.at[j, mine(n)], ss.at[e], rs.at[e],
                          (_flip(cx, chips[q][0]), _flip(cy, chips[q][1]), cc))

        def from_chip(q, n):
            e = nch * q + n
            return _rcopy(wsh_hbm.at[mine(n)], wg_hbm.at[shard_of(q), mine(n)], ss.at[e], rs.at[e], sib)

        def to_sibling(q, n):
            e = 3 * nch + nch * q + n
            return _rcopy(wg_hbm.at[shard_of(q), mine(n)], wg_hbm.at[shard_of(q), mine(n)], ss.at[e], rs.at[e], sib)

        def from_sibling(q, n):
            e = 3 * nch + nch * q + n
            return _rcopy(wsh_hbm.at[mine(n)], wg_hbm.at[shard_of(q), theirs(n)], ss.at[e], rs.at[e], sib)

        def load(sl, src):
            cp = pltpu.make_async_copy(src, w_s.at[sl], wsem.at[sl])
            cp.start()
            cp.wait()

        first = (i == 0) & (col == 0)

        @pl.when(first & (slot == 0))
        def _():
            for n in range(nch):
                for q in (0, 1):
                    to_chip(q, n).start()
            load(0, wsh_hbm.at[pl.ds(0, D), :])

        @pl.when(first & (slot == 1))
        def _():
            for q in (0, 1):
                for n in range(nch):
                    from_chip(q, n).wait_recv()
                    to_sibling(q, n).start()
            for n in range(nch):
                to_chip(2, n).start()
            for n in range(nch):
                from_sibling(0, n).wait_recv()
            load(1, wg_hbm.at[shard_of(0), pl.ds(0, D), :])

        @pl.when(first & (slot == 2))
        def _():
            for n in range(nch):
                from_sibling(1, n).wait_recv()
            load(2, wg_hbm.at[shard_of(1), pl.ds(0, D), :])

        @pl.when(first & (slot == 3))
        def _():
            for n in range(nch):
                from_chip(2, n).wait_recv()
                to_sibling(2, n).start()
            for n in range(nch):
                from_sibling(2, n).wait_recv()
            load(3, wg_hbm.at[shard_of(2), pl.ds(0, D), :])
            for q in range(3):
                for n in range(nch):
                    to_chip(q, n).wait_send()
                    to_sibling(q, n).wait_send()

        @pl.when((slot == 0) & (col == 0))
        def _():
            xt = x_ref[...]
            rstd = lax.rsqrt(jnp.mean(xt * xt, axis=-1, keepdims=True) + EPS)
            h = ((xt * rstd * gn_ref[...]) * (1.0 + sc_ref[...]) + sh_ref[...]).astype(BF16)
            hs_all[i] = h
            h_ref[...] = h

        p_ref[0] = _dot(hs_all[i], w_s[slot, :, pl.ds(pl.multiple_of(col * tn, tn), tn)]).astype(BF16)

    row = pl.BlockSpec((1, D), lambda sl, i, col, o: (0, 0))
    x_rows = lambda sl, i, col, o: (jnp.where(sl == 0, i, ni - 1), 0)
    any_ = pl.BlockSpec(memory_space=pl.ANY)
    return pl.pallas_call(
        body, name="gather_norm_inproj",
        grid_spec=pltpu.PrefetchScalarGridSpec(
            num_scalar_prefetch=1, grid=(4, ni, npc),
            in_specs=[pl.BlockSpec((tm, D), x_rows), row, row, row, any_],
            out_specs=(pl.BlockSpec((1, tm, tn), lambda sl, i, col, o: (o[sl], i, col)),
                       pl.BlockSpec((tm, D), x_rows), any_),
            scratch_shapes=[pltpu.VMEM((ni, tm, D), BF16), pltpu.VMEM((4, D, PW), BF16),
                            pltpu.SemaphoreType.DMA((4,)),
                            pltpu.SemaphoreType.DMA((6 * nch,)), pltpu.SemaphoreType.DMA((6 * nch,))]),
        out_shape=(jax.ShapeDtypeStruct((4, s, PW), BF16), jax.ShapeDtypeStruct((s, D), BF16),
                   jax.ShapeDtypeStruct((4, rows, cols), wsh.dtype)),
        compiler_params=_params(("arbitrary", "arbitrary", "arbitrary")),
    )(order, x, gn, shift, scale, wsh)


def _shift_down(prev8, cur, d):
    t = cur.shape[0]
    c3 = cur.reshape(t // 8, 8, DH)
    rot = pltpu.roll(c3, d, 1)
    before = jnp.concatenate([pltpu.roll(prev8, d, 0).reshape(1, 8, DH), rot[:-1]], axis=0)
    rows = lax.broadcasted_iota(jnp.int32, c3.shape, 1)
    return jnp.where(rows >= d, rot, before).reshape(t, DH)


def _shift_up(cur, next8, d):
    t = cur.shape[0]
    c3 = cur.reshape(t // 8, 8, DH)
    rot = pltpu.roll(c3, 8 - d, 1)
    after = jnp.concatenate([rot[1:], pltpu.roll(next8, 8 - d, 0).reshape(1, 8, DH)], axis=0)
    rows = lax.broadcasted_iota(jnp.int32, c3.shape, 1)
    return jnp.where(rows < 8 - d, rot, after).reshape(t, DH)


def _rnn_gates(xr, prev8, cw, cb, wa, ba, wx, bx, lam, reset):
    xc = cw[3:4] * xr + cb
    for d in (1, 2, 3):
        xc = xc + cw[3 - d:4 - d] * _shift_down(prev8, xr, d)
    xcb = xc.astype(BF16)
    r = _sigmoid(_dot(xcb, wa.astype(BF16)) + ba)
    ig = _sigmoid(_dot(xcb, wx.astype(BF16)) + bx)
    nl = -lam
    sp = jnp.maximum(nl, 0.0) + jnp.log1p(jnp.exp(-jnp.abs(nl)))
    log_a = (-LRU_C * r) * sp
    a_raw = jnp.exp(log_a)
    a = jnp.where(reset, 0.0, a_raw)
    mult = jnp.where(reset, 1.0, jnp.sqrt(1.0 - a_raw * a_raw))
    return xc, r, ig, sp, a, mult


def _log_scan(a, b, axis, up):
    n = a.shape[axis]
    rows = lax.broadcasted_iota(jnp.int32, a.shape, axis)
    d = 1
    while d < n:
        m = rows < n - d if up else rows >= d
        shift = n - d if up else d
        a_s = pltpu.roll(a, shift, axis)
        b_s = pltpu.roll(b, shift, axis)
        b = jnp.where(m, a * b_s + b, b)
        a = jnp.where(m, a * a_s, a)
        d *= 2
    return a, b


def _scan(a, b, t, edge, up=False):
    g = t // 8
    a3, b3 = _log_scan(a.reshape(g, 8, DH), b.reshape(g, 8, DH), 1, up)
    last = 0 if up else 7
    ag, bg = _log_scan(a3[:, last, :], b3[:, last, :], 0, up)
    hg = ag * edge + bg
    grp = lax.broadcasted_iota(jnp.int32, hg.shape, 0)
    if up:
        cin = jnp.where(grp == g - 1, edge, pltpu.roll(hg, g - 1, 0))
        tail = hg[0:1]
    else:
        cin = jnp.where(grp == 0, edge, pltpu.roll(hg, 1, 0))
        tail = hg[g - 1:g]
    return (a3 * cin[:, None, :] + b3).reshape(t, DH), tail


def _rnn_fwd(p, pos, conv_w, conv_b, w_a, b_a, w_x, b_x, lam, tt=2048):
    s = p.shape[1]
    nt = s // tt

    def body(xr_ref, z_ref, pos_ref, cw_ref, cb_ref, wa_ref, ba_ref, wx_ref, bx_ref, lam_ref,
             hr_ref, gr_ref, xprev, hprev):
        @pl.when(pl.program_id(1) == 0)
        def _():
            xprev[...] = jnp.zeros_like(xprev)
            hprev[...] = jnp.zeros_like(hprev)

        xr = xr_ref[0].astype(F32)
        z = z_ref[0].astype(F32)
        reset = pos_ref[...] > 0.5
        xc, r, ig, sp, a, mult = _rnn_gates(xr, xprev[...], cw_ref[...], cb_ref[...], wa_ref[0], ba_ref[0],
                                            wx_ref[0], bx_ref[0], lam_ref[...], reset)
        bx = mult * ig * xc
        h, h_last = _scan(a, bx, tt, hprev[0:1])
        xprev[...] = xr[tt - 8:]
        hprev[...] = jnp.broadcast_to(h_last, (8, DH))
        hr_ref[...] = h
        gr_ref[...] = (h * (z * _sigmoid(z))).astype(BF16)

    head_row = lambda hh, t: (0, hh)
    return pl.pallas_call(
        body, name="rnn_fwd", grid=(H, nt),
        in_specs=[pl.BlockSpec((1, tt, DH), lambda hh, t: (0, t, hh)),
                  pl.BlockSpec((1, tt, DH), lambda hh, t: (0, t, H + hh)),
                  pl.BlockSpec((tt, DH), lambda hh, t: (t, 0)),
                  pl.BlockSpec((4, DH), head_row), pl.BlockSpec((1, DH), head_row),
                  pl.BlockSpec((1, DH, DH), lambda hh, t: (hh, 0, 0)), pl.BlockSpec((1, 1, DH), lambda hh, t: (hh, 0, 0)),
                  pl.BlockSpec((1, DH, DH), lambda hh, t: (hh, 0, 0)), pl.BlockSpec((1, 1, DH), lambda hh, t: (hh, 0, 0)),
                  pl.BlockSpec((1, DH), head_row)],
        out_specs=(pl.BlockSpec((tt, DH), lambda hh, t: (t, hh)), pl.BlockSpec((tt, DH), lambda hh, t: (t, hh))),
        out_shape=(jax.ShapeDtypeStruct((s, D), F32), jax.ShapeDtypeStruct((s, D), BF16)),
        scratch_shapes=[pltpu.VMEM((8, DH), F32), pltpu.VMEM((8, DH), F32)],
        compiler_params=_params(("parallel", "arbitrary")),
    )(p, p, pos, conv_w, conv_b, w_a, b_a, w_x, b_x, lam)


def _rnn_bwd(p, hr, dgr, pos, conv_w, conv_b, w_a, b_a, w_x, b_x, lam, tt=2048):
    s = p.shape[1]
    nt = s // tt
    t8 = tt // 8

    def body(xr_ref, z_ref, xp_ref, hr_ref, hp_ref, dg_ref, pos_ref, cw_ref, cb_ref, wa_ref, ba_ref, wx_ref, bx_ref,
             lam_ref, dxr_ref, dz_ref, gwa_ref, gba_ref, gwx_ref, gbx_ref, glam_ref, gcw_ref, gcb_ref,
             a_next, g_next, dxc_next):
        t = pl.program_id(1)
        has_prev = t < nt - 1

        @pl.when(t == 0)
        def _():
            a_next[...] = jnp.zeros_like(a_next)
            g_next[...] = jnp.zeros_like(g_next)
            dxc_next[...] = jnp.zeros_like(dxc_next)
            gwa_ref[...] = jnp.zeros_like(gwa_ref)
            gba_ref[...] = jnp.zeros_like(gba_ref)
            gwx_ref[...] = jnp.zeros_like(gwx_ref)
            gbx_ref[...] = jnp.zeros_like(gbx_ref)
            glam_ref[...] = jnp.zeros_like(glam_ref)
            gcw_ref[...] = jnp.zeros_like(gcw_ref)
            gcb_ref[...] = jnp.zeros_like(gcb_ref)

        xr = xr_ref[0].astype(F32)
        z = z_ref[0].astype(F32)
        hr_blk = hr_ref[...]
        dg = dg_ref[...]
        xprev = jnp.where(has_prev, xp_ref[0].astype(F32)[8:], 0.0)
        hprev8 = jnp.where(has_prev, hp_ref[...], 0.0)
        reset = pos_ref[...] > 0.5
        cw = cw_ref[...]
        wa = wa_ref[0]
        wx = wx_ref[0]
        lam_v = lam_ref[...]
        xc, r, ig, sp, a, mult = _rnn_gates(xr, xprev, cw, cb_ref[...], wa, ba_ref[0], wx, bx_ref[0], lam_v, reset)

        sz = _sigmoid(z)
        dh = dg * (z * sz)
        dz_ref[...] = (dg * hr_blk * (sz * (1.0 + z * (1.0 - sz)))).astype(BF16)

        an = _shift_up(a, a_next[...], 1)
        g, g_first = _scan(an, dh, tt, g_next[0:1], up=True)
        a_next[...] = jnp.broadcast_to(a[0:1], (8, DH))
        g_next[...] = jnp.broadcast_to(g_first, (8, DH))

        hm1 = _shift_down(hprev8, hr_blk, 1)
        da = g * hm1
        dmult = g * (ig * xc)
        di = g * (mult * xc)
        dxc = g * (mult * ig)
        dla = jnp.where(reset, 0.0, da * a - dmult * (a * a) / mult)
        dr = dla * (-LRU_C * sp)
        dsp = _colsum(dla * (-LRU_C * r))
        glam_ref[0] += dsp * (-_sigmoid(-lam_v))
        dpa = dr * r * (1.0 - r)
        dpx = di * ig * (1.0 - ig)
        dpab = dpa.astype(BF16)
        dpxb = dpx.astype(BF16)
        dxc = dxc + _dot_nt(dpab, wa.astype(BF16)) + _dot_nt(dpxb, wx.astype(BF16))
        xcb = xc.astype(BF16)
        gwa_ref[0] += _dot_tn(xcb, dpab)
        gwx_ref[0] += _dot_tn(xcb, dpxb)
        gba_ref[0] += _colsum(dpa)
        gbx_ref[0] += _colsum(dpx)

        dxr = cw[3:4] * dxc
        for d in (1, 2, 3):
            dxr = dxr + cw[3 - d:4 - d] * _shift_up(dxc, dxc_next[...], d)
        dxr_ref[...] = dxr.astype(BF16)
        dxc_next[...] = dxc[0:8]
        gcb_ref[0] += _colsum(dxc)
        gcw_ref[0, 3:4, :] += _colsum(xr * dxc)
        for d in (1, 2, 3):
            gcw_ref[0, 3 - d:4 - d, :] += _colsum(_shift_down(xprev, xr, d) * dxc)

    rt = lambda t: nt - 1 - t
    prev8 = lambda t: jnp.maximum(rt(t) * t8 - 1, 0)
    head_row = lambda hh, t: (0, hh)
    hsm = lambda hh, t: (hh, 0, 0)
    return pl.pallas_call(
        body, name="rnn_bwd", grid=(H, nt),
        in_specs=[pl.BlockSpec((1, tt, DH), lambda hh, t: (0, rt(t), hh)),
                  pl.BlockSpec((1, tt, DH), lambda hh, t: (0, rt(t), H + hh)),
                  pl.BlockSpec((1, 16, DH), lambda hh, t: (0, jnp.maximum(rt(t) * (tt // 16) - 1, 0), hh)),
                  pl.BlockSpec((tt, DH), lambda hh, t: (rt(t), hh)),
                  pl.BlockSpec((8, DH), lambda hh, t: (prev8(t), hh)),
                  pl.BlockSpec((tt, DH), lambda hh, t: (rt(t), hh)),
                  pl.BlockSpec((tt, DH), lambda hh, t: (rt(t), 0)),
                  pl.BlockSpec((4, DH), head_row), pl.BlockSpec((1, DH), head_row),
                  pl.BlockSpec((1, DH, DH), hsm), pl.BlockSpec((1, 1, DH), hsm),
                  pl.BlockSpec((1, DH, DH), hsm), pl.BlockSpec((1, 1, DH), hsm),
                  pl.BlockSpec((1, DH), head_row)],
        out_specs=(pl.BlockSpec((tt, DH), lambda hh, t: (rt(t), hh)), pl.BlockSpec((tt, DH), lambda hh, t: (rt(t), hh)),
                   pl.BlockSpec((1, DH, DH), hsm), pl.BlockSpec((1, 1, DH), hsm),
                   pl.BlockSpec((1, DH, DH), hsm), pl.BlockSpec((1, 1, DH), hsm),
                   pl.BlockSpec((1, 1, DH), hsm), pl.BlockSpec((1, 4, DH), hsm), pl.BlockSpec((1, 1, DH), hsm)),
        out_shape=(jax.ShapeDtypeStruct((s, D), BF16), jax.ShapeDtypeStruct((s, D), BF16),
                   jax.ShapeDtypeStruct((H, DH, DH), F32), jax.ShapeDtypeStruct((H, 1, DH), F32),
                   jax.ShapeDtypeStruct((H, DH, DH), F32), jax.ShapeDtypeStruct((H, 1, DH), F32),
                   jax.ShapeDtypeStruct((H, 1, DH), F32), jax.ShapeDtypeStruct((H, 4, DH), F32),
                   jax.ShapeDtypeStruct((H, 1, DH), F32)),
        scratch_shapes=[pltpu.VMEM((8, DH), F32), pltpu.VMEM((8, DH), F32), pltpu.VMEM((8, DH), F32)],
        compiler_params=_params(("parallel", "arbitrary")),
    )(p, p, p, hr, hr, dgr, pos, conv_w, conv_b, w_a, b_a, w_x, b_x, lam)


def _rope(t, c, sa, sb):
    return t * c + pltpu.roll(t, DH - ROT // 2, 1) * sa + pltpu.roll(t, ROT // 2, 1) * sb


def _rope_bwd(g, c, sa, sb):
    return g * c + pltpu.roll(g * sa, ROT // 2, 1) + pltpu.roll(g * sb, DH - ROT // 2, 1)


def _unit_bases(gi, u):
    dil = DILATIONS[gi]
    if dil == 1:
        return u * UB, SPAN + (u - 1) * UB, u == 0
    if dil == 4:
        blk, r = u // 4, u % 4
        return blk * 4 * UB + r, SPAN + (blk - 1) * 4 * UB + r, blk == 0
    return u, u, True


def _unit_slices(gi, u):
    dil = DILATIONS[gi]
    qb0, kb0, first = _unit_bases(gi, u)
    if dil == 1:
        if not isinstance(qb0, int):
            qb0, kb0 = pl.multiple_of(qb0, UB), pl.multiple_of(kb0, UB)
        return pl.ds(qb0, UB), pl.ds(kb0, 2 * UB), first
    return pl.ds(qb0, UB, stride=dil), pl.ds(kb0, 2 * UB, stride=dil), first


def _bdot(a, b):
    return lax.dot_general(a, b, (((2,), (1,)), ((0,), (0,))), preferred_element_type=F32)


def _bdot_nt(a, b):
    return lax.dot_general(a, b, (((2,), (2,)), ((0,), (0,))), preferred_element_type=F32)


def _bdot_tn(a, b):
    return lax.dot_general(a, b, (((1,), (1,)), ((0,), (0,))), preferred_element_type=F32)


def _band_mask(first_in_span, has_prev):
    qi = lax.broadcasted_iota(jnp.int32, (UB, 2 * UB), 0)
    ki = lax.broadcasted_iota(jnp.int32, (UB, 2 * UB), 1)
    dist = UB + qi - ki
    band = (dist >= 0) & (dist <= UB)
    return band & ((ki >= UB) | jnp.logical_not(first_in_span) | has_prev)


def _gather_halves(phase, src_hbm, dst_hbm, ss, rs):
    half = src_hbm.shape[0] // 2
    cx, cy, cc = _coords()
    j = 2 * cx + cy
    sib = (cx, cy, 1 - cc)
    mine = pl.ds(cc * half, half)
    theirs = pl.ds((1 - cc) * half, half)
    chips = ((1, 0), (0, 1), (1, 1))
    for q, (kx, ky) in enumerate(chips):
        jq = j ^ (2 * kx + ky)
        out = _rcopy(src_hbm.at[mine], dst_hbm.at[j, mine], ss.at[q], rs.at[q], (_flip(cx, kx), _flip(cy, ky), cc))
        landed = _rcopy(src_hbm.at[mine], dst_hbm.at[jq, mine], ss.at[q], rs.at[q], sib)
        onward = _rcopy(dst_hbm.at[jq, mine], dst_hbm.at[jq, mine], ss.at[3 + q], rs.at[3 + q], sib)
        from_sib = _rcopy(src_hbm.at[mine], dst_hbm.at[jq, theirs], ss.at[3 + q], rs.at[3 + q], sib)
        if phase == 0:
            out.start()
        elif phase == 1:
            landed.wait_recv()
            onward.start()
        else:
            from_sib.wait_recv()
            out.wait_send()
            onward.wait_send()


def _attn_fwd(p, rc, rsa, rsb, w3sh):
    s = p.shape[1]
    ns = s // SPAN
    nunit = SPAN // UB

    def body(q_ref, k_ref, v_ref, z_ref, c_ref, sa_ref, sb_ref, w3_hbm, o_ref, lse_ref, ga_ref, qro_ref, kro_ref,
             w3g_hbm, qr, kf, vf, acc, mm, ll, ss, rs):
        hh, n = pl.program_id(0), pl.program_id(1)
        for phase, at_head, at_span in ((0, 0, 0), (1, H // 2, 0), (2, H - 1, ns - 1)):
            @pl.when((hh == at_head) & (n == at_span))
            def _(phase=phase):
                _gather_halves(phase, w3_hbm, w3g_hbm, ss, rs)

        @pl.when(n == 0)
        def _():
            kf[0:SPAN] = jnp.zeros((SPAN, DH), F32)
            vf[0:SPAN] = jnp.zeros((SPAN, DH), F32)

        c, sa, sb = c_ref[...], sa_ref[...], sb_ref[...]
        q_rot = _rope(q_ref[0].astype(F32), c, sa, sb).astype(BF16)
        k_rot = _rope(k_ref[0].astype(F32), c, sa, sb).astype(BF16)
        qro_ref[...] = q_rot
        kro_ref[...] = k_rot
        qr[...] = q_rot.astype(F32)
        kf[SPAN:] = k_rot.astype(F32)
        vf[SPAN:] = v_ref[0].astype(F32)
        has_prev = n > 0

        for gi, dil in enumerate(DILATIONS):
            def trip(t, carry, gi=gi, dil=dil):
                qsls, ksls, firsts = [], [], []
                for b in range(UNIT_BATCH_FWD):
                    qsl, ksl, first = _unit_slices(gi, t * UNIT_BATCH_FWD + b)
                    qsls.append(qsl)
                    ksls.append(ksl)
                    firsts.append(first)
                qb = jnp.stack([qr[qsl, :].astype(BF16) for qsl in qsls])
                kb = jnp.stack([kf[ksl, :].astype(BF16) for ksl in ksls])
                vb = jnp.stack([vf[ksl, :].astype(BF16) for ksl in ksls])
                s_all = _bdot_nt(qb, kb)
                prs = []
                for b in range(UNIT_BATCH_FWD):
                    sc = jnp.where(_band_mask(firsts[b], has_prev), s_all[b] * SCALE, NEG)
                    m = jnp.max(sc, axis=-1, keepdims=True)
                    pr = jnp.exp(sc - m)
                    l = jnp.sum(pr, axis=-1, keepdims=True)
                    mm[gi, qsls[b], :] = jnp.broadcast_to(m, (UB, DH))
                    ll[gi, qsls[b], :] = jnp.broadcast_to(l, (UB, DH))
                    prs.append(pr.astype(BF16))
                o_all = _bdot(jnp.stack(prs), vb)
                for b in range(UNIT_BATCH_FWD):
                    acc[gi, qsls[b], :] = o_all[b]
                return carry

            lax.fori_loop(0, nunit // UNIT_BATCH_FWD, trip, 0)

        m_all =jnp.maximum(jnp.maximum(mm[0], mm[1]), mm[2])
        num = jnp.zeros((SPAN, DH), F32)
        den = jnp.zeros((SPAN, DH), F32)
        for gi in range(3):
            w = jnp.exp(mm[gi] - m_all)
            num = num + w * acc[gi]
            den = den + w * ll[gi]
        o = num / den
        o_ref[...] = o
        lse_ref[...] = m_all + jnp.log(den)
        z = z_ref[0].astype(F32)
        ga_ref[...] = (o * (z * _sigmoid(z))).astype(BF16)
        kf[0:SPAN] = kf[SPAN:]
        vf[0:SPAN] = vf[SPAN:]

    blk = lambda piece, off: pl.BlockSpec((1, SPAN, DH), lambda hh, n: (piece, n, off + hh))
    tab = pl.BlockSpec((SPAN, DH), lambda hh, n: (n, 0))
    outb = pl.BlockSpec((SPAN, DH), lambda hh, n: (n, hh))
    any_ = pl.BlockSpec(memory_space=pl.ANY)
    return pl.pallas_call(
        body, name="attn_fwd", grid=(H, ns),
        in_specs=[blk(1, 0), blk(1, H), blk(2, 0), blk(2, H), tab, tab, tab, any_],
        out_specs=(outb, outb, outb, outb, outb, any_),
        out_shape=(jax.ShapeDtypeStruct((s, D), F32), jax.ShapeDtypeStruct((s, D), F32),
                   jax.ShapeDtypeStruct((s, D), BF16), jax.ShapeDtypeStruct((s, D), BF16),
                   jax.ShapeDtypeStruct((s, D), BF16), jax.ShapeDtypeStruct((4,) + w3sh.shape, w3sh.dtype)),
        scratch_shapes=[pltpu.VMEM((SPAN, DH), F32), pltpu.VMEM((2 * SPAN, DH), F32), pltpu.VMEM((2 * SPAN, DH), F32),
                        pltpu.VMEM((3, SPAN, DH), F32), pltpu.VMEM((3, SPAN, DH), F32), pltpu.VMEM((3, SPAN, DH), F32),
                        pltpu.SemaphoreType.DMA((6,)), pltpu.SemaphoreType.DMA((6,))],
        compiler_params=_params(("arbitrary", "arbitrary")),
    )(p, p, p, p, rc, rsa, rsb, w3sh)


def _attn_bwd(p, q_rot, k_rot, o, lse, dga, rc, rsa, rsb, packs):
    s = p.shape[1]
    ns = s // SPAN
    nunit = SPAN // UB
    npk = len(packs)

    def body(*refs):
        (q_ref, k_ref, kp_ref, v_ref, vp_ref, z_ref, c_ref, sa_ref, sb_ref, o_ref, lse_ref, dg_ref) = refs[:12]
        pk_refs = refs[12:12 + npk]
        dq_ref, dk_ref, dv_ref, dz_ref = refs[12 + npk:16 + npk]
        red_refs = refs[16 + npk:16 + 2 * npk]
        qr, kf, vf, dof, dlt, dqa, dkf, dvf = refs[16 + 2 * npk:24 + 2 * npk]
        rbufs = refs[24 + 2 * npk:24 + 3 * npk]
        sums = refs[24 + 3 * npk:24 + 4 * npk]
        ar_sems = refs[24 + 4 * npk:]
        hh, step = pl.program_id(0), pl.program_id(1)
        n = ns - 1 - step
        has_prev = n > 0
        for phase, at_head, at_step in ((0, 0, 0), (1, H // 2, 0), (2, H - 1, ns - 1)):
            @pl.when((hh == at_head) & (step == at_step))
            def _(phase=phase):
                _allreduce_phase(phase, pk_refs, sums, rbufs, *ar_sems, out_refs=red_refs)

        @pl.when(step == 0)
        def _():
            dkf[...] = jnp.zeros_like(dkf)
            dvf[...] = jnp.zeros_like(dvf)

        @pl.when(step > 0)
        def _():
            dkf[SPAN:] = dkf[0:SPAN]
            dvf[SPAN:] = dvf[0:SPAN]
            dkf[0:SPAN] = jnp.zeros((SPAN, DH), F32)
            dvf[0:SPAN] = jnp.zeros((SPAN, DH), F32)

        c, sa, sb = c_ref[...], sa_ref[...], sb_ref[...]
        qr[...] = q_ref[...].astype(F32)
        kf[SPAN:] = k_ref[...].astype(F32)
        vf[SPAN:] = v_ref[0].astype(F32)
        kf[0:SPAN] = jnp.where(has_prev, kp_ref[...].astype(F32), 0.0)
        vf[0:SPAN] = jnp.where(has_prev, vp_ref[0].astype(F32), 0.0)
        z = z_ref[0].astype(F32)
        sz = _sigmoid(z)
        dg = dg_ref[...]
        ov = o_ref[...]
        do = dg * (z * sz)
        dz_ref[...] = (dg * ov * (sz * (1.0 + z * (1.0 - sz)))).astype(BF16)
        dof[...] = do
        dlt[...] = jnp.dot(do * ov, jnp.ones((DH, DH), F32), preferred_element_type=F32, precision=HIGHEST)
        dqa[...] = jnp.zeros_like(dqa)

        for gi, dil in enumerate(DILATIONS):
            def trip(t, carry, gi=gi, dil=dil):
                qsls, ksls, firsts = [], [], []
                for b in range(UNIT_BATCH):
                    qsl, ksl, first = _unit_slices(gi, t * UNIT_BATCH + b)
                    qsls.append(qsl)
                    ksls.append(ksl)
                    firsts.append(first)
                qb = jnp.stack([qr[qsl, :].astype(BF16) for qsl in qsls])
                kb = jnp.stack([kf[ksl, :].astype(BF16) for ksl in ksls])
                vb = jnp.stack([vf[ksl, :].astype(BF16) for ksl in ksls])
                dob = jnp.stack([dof[qsl, :].astype(BF16) for qsl in qsls])
                s_all = _bdot_nt(qb, kb)
                dp_all = _bdot_nt(dob, vb)
                prs, dss = [], []
                for b in range(UNIT_BATCH):
                    lse_b = lse_ref[qsls[b], :]
                    dl_b = dlt[qsls[b], :]
                    pr = jnp.exp(s_all[b] * SCALE - jnp.concatenate([lse_b, lse_b], axis=1))
                    pr = jnp.where(_band_mask(firsts[b], has_prev), pr, 0.0)
                    prs.append(pr.astype(BF16))
                    dss.append((pr * (dp_all[b] - jnp.concatenate([dl_b, dl_b], axis=1)) * SCALE).astype(BF16))
                ds_all = jnp.stack(dss)
                dv_all = _bdot_tn(jnp.stack(prs), dob)
                dq_all = _bdot(ds_all, kb)
                dk_all = _bdot_tn(ds_all, qb)
                for b in range(UNIT_BATCH):
                    dvf[ksls[b], :] += dv_all[b]
                    dqa[qsls[b], :] += dq_all[b]
                    dkf[ksls[b], :] += dk_all[b]
                return carry

            lax.fori_loop(0, nunit // UNIT_BATCH, trip, 0)

        dq_ref[...] = _rope_bwd(dqa[...], c, sa, sb).astype(BF16)
        dk_ref[...] = _rope_bwd(dkf[SPAN:], c, sa, sb).astype(BF16)
        dv_ref[...] = dvf[SPAN:].astype(BF16)

    rn = lambda n: ns - 1 - n
    pn = lambda n: jnp.maximum(ns - 2 - n, 0)
    blk = lambda piece, off: pl.BlockSpec((1, SPAN, DH), lambda hh, n: (piece, rn(n), off + hh))
    blkp = lambda piece, off: pl.BlockSpec((1, SPAN, DH), lambda hh, n: (piece, pn(n), off + hh))
    tab = pl.BlockSpec((SPAN, DH), lambda hh, n: (rn(n), 0))
    io = pl.BlockSpec((SPAN, DH), lambda hh, n: (rn(n), hh))
    iop = pl.BlockSpec((SPAN, DH), lambda hh, n: (pn(n), hh))
    vm = pl.BlockSpec(memory_space=pltpu.VMEM)
    outs = pl.pallas_call(
        body, name="attn_bwd", grid=(H, ns),
        in_specs=[io, io, iop, blk(2, 0), blkp(2, 0), blk(2, H), tab, tab, tab, io, io, io] + [vm] * npk,
        out_specs=(io, io, io, io) + (vm,) * npk,
        out_shape=tuple(jax.ShapeDtypeStruct((s, D), BF16) for _ in range(4)) +
                  tuple(jax.ShapeDtypeStruct(a.shape, F32) for a in packs),
        scratch_shapes=[pltpu.VMEM((SPAN, DH), F32), pltpu.VMEM((2 * SPAN, DH), F32), pltpu.VMEM((2 * SPAN, DH), F32),
                        pltpu.VMEM((SPAN, DH), F32), pltpu.VMEM((SPAN, DH), F32), pltpu.VMEM((SPAN, DH), F32),
                        pltpu.VMEM((2 * SPAN, DH), F32), pltpu.VMEM((2 * SPAN, DH), F32)] +
                       [pltpu.VMEM((NDEV, a.shape[0] // NDEV, a.shape[1]), F32) for a in packs] +
                       [pltpu.VMEM(a.shape, F32) for a in packs] +
                       [pltpu.SemaphoreType.DMA((7 * npk,)) for _ in range(4)],
        compiler_params=_params(("arbitrary", "arbitrary")),
    )(q_rot, k_rot, k_rot, p, p, p, rc, rsa, rsb, o, lse, dga, *packs)
    return outs[:4], outs[4:]


def _tail(gr, ga, p, x, tgt, w3, b_gate, gate, g_final, tm=256):
    s = x.shape[0]
    nt = s // tm

    def body(gr_ref, ga_ref, pr_ref, pa_ref, x_ref, t_ref, bg_ref, gate_ref, gf_ref, w_hbm,
             dgr_ref, dga_ref, dc_ref, dx2_ref, vec_ref, go_hbm, w_s, acc_s, sem):
        i = pl.program_id(0)

        @pl.when(i == 0)
        def _():
            cp = pltpu.make_async_copy(w_hbm, w_s, sem.at[12])
            cp.start()
            acc_s[...] = jnp.zeros_like(acc_s)
            vec_ref[...] = jnp.zeros_like(vec_ref)
            cp.wait()

        grb = gr_ref[...]
        gab = ga_ref[...]
        bg = bg_ref[...]
        gate_v = gate_ref[...]
        gf = gf_ref[...]
        y_r = _dot(grb, w_s[0])
        y_a = _dot(gab, w_s[1])
        sr = _sigmoid(pr_ref[0].astype(F32) + bg[:, :D])
        sa = _sigmoid(pa_ref[0].astype(F32) + bg[:, D:])
        mb = (sr * y_r + sa * y_a).astype(BF16)
        u = _dot(mb, w_s[2])
        x2 = x_ref[...] + gate_v * u
        rstd = lax.rsqrt(jnp.mean(x2 * x2, axis=-1, keepdims=True) + EPS)
        xh = x2 * rstd
        e = xh * gf - t_ref[...]
        dy = e * (1.0 / D)
        dyg = dy * gf
        dx2 = rstd * (dyg - xh * jnp.mean(dyg * xh, axis=-1, keepdims=True))
        dx2_ref[...] = dx2
        dub = (dx2 * gate_v).astype(BF16)
        dm = _dot_nt(dub, w_s[2])
        dyr = (dm * sr).astype(BF16)
        dya = (dm * sa).astype(BF16)
        dpr = dm * y_r * (sr * (1.0 - sr))
        dpa = dm * y_a * (sa * (1.0 - sa))
        dc_ref[:, :D] = dpr.astype(BF16)
        dc_ref[:, D:] = dpa.astype(BF16)
        dgr_ref[...] = _dot_nt(dyr, w_s[0])
        dga_ref[...] = _dot_nt(dya, w_s[1])
        acc_s[0] += _dot_tn(grb, dyr)
        acc_s[1] += _dot_tn(gab, dya)
        acc_s[2] += _dot_tn(mb, dub)
        vec_ref[0:1, :] += _colsum(dy * xh)
        vec_ref[1:2, :] += _colsum(dx2 * u)
        vec_ref[2:3, :] += _colsum(dpr)
        vec_ref[3:4, :] += _colsum(dpa)
        vec_ref[4:5, :] += _colsum(e * e)

        @pl.when(i == nt - 1)
        def _():
            vec_ref[4:5, :] = jnp.broadcast_to(jnp.sum(vec_ref[4:5, :]) * (0.5 / D), (1, D))
            cps = []
            for w in range(3):
                for j in range(4):
                    cps.append(pltpu.make_async_copy(acc_s.at[w, pl.ds(256 * j, 256)],
                                                     go_hbm.at[j, pl.ds(256 * w, 256)], sem.at[4 * w + j]))
            for cp in cps:
                cp.start()
            for cp in cps:
                cp.wait()

    rowt = lambda i: (i, 0)
    row = lambda w: pl.BlockSpec((1, w), lambda i: (0, 0))
    any_ = pl.BlockSpec(memory_space=pl.ANY)
    return pl.pallas_call(
        body, name="tail", grid=(nt,),
        in_specs=[pl.BlockSpec((tm, D), rowt), pl.BlockSpec((tm, D), rowt),
                  pl.BlockSpec((1, tm, D), lambda i: (3, i, 0)), pl.BlockSpec((1, tm, D), lambda i: (3, i, 1)),
                  pl.BlockSpec((tm, D), rowt), pl.BlockSpec((tm, D), rowt),
                  row(2 * D), row(D), row(D), any_],
        out_specs=(pl.BlockSpec((tm, D), rowt), pl.BlockSpec((tm, D), rowt), pl.BlockSpec((tm, 2 * D), rowt),
                   pl.BlockSpec((tm, D), rowt), pl.BlockSpec((8, D), lambda i: (0, 0)), any_),
        out_shape=(jax.ShapeDtypeStruct((s, D), F32), jax.ShapeDtypeStruct((s, D), F32),
                   jax.ShapeDtypeStruct((s, 2 * D), BF16), jax.ShapeDtypeStruct((s, D), F32),
                   jax.ShapeDtypeStruct((8, D), F32), jax.ShapeDtypeStruct((4, 768, D), F32)),
        scratch_shapes=[pltpu.VMEM((3, D, D), BF16), pltpu.VMEM((3, D, D), F32), pltpu.SemaphoreType.DMA((13,))],
        compiler_params=_params(("arbitrary",)),
    )(gr, ga, p, p, x, tgt, b_gate, gate, g_final, w3)


def _pieces_steps(pieces):
    out, s0 = [], 0
    for a in pieces:
        n = a.shape[1] // D
        out.append((s0, n))
        s0 += n
    return out, s0


def _inproj_bwd_x(pieces, wg, wsh, x, dx2, gn, scale, sums, tm=512):
    s = x.shape[0]
    np_ = len(pieces)
    na = len(sums)
    ni = s // tm
    groups, cur, width = [], [], 0
    for t, a in enumerate(pieces):
        cur.append(t)
        width += a.shape[1]
        if width == PW:
            groups.append(cur)
            cur, width = [], 0
    assert len(groups) == 4 and not cur

    def body(*refs):
        d_refs = refs[:np_]
        w_hbm, wsh_hbm, x_ref, dx2_ref, gn_ref, sc_ref = refs[np_:np_ + 6]
        q_refs = refs[np_ + 6:np_ + 6 + na]
        gx_ref, vec_ref = refs[np_ + 6 + na:np_ + 8 + na]
        r_refs = refs[np_ + 8 + na:np_ + 8 + 2 * na]
        w_s, wsem, ss, rs = refs[np_ + 8 + 2 * na:]
        i = pl.program_id(0)
        my_shard = 2 * lax.axis_index("x") + lax.axis_index("y")

        def scatter_copies():
            cx, cy, cc = _coords()
            j = 2 * cx + cy
            cps = []
            for t, (q, r) in enumerate(zip(q_refs, r_refs)):
                for e, (kx, ky) in enumerate(((1, 0), (0, 1), (1, 1))):
                    cps.append(_rcopy(q.at[j ^ (2 * kx + ky)], r.at[e], ss.at[3 * t + e], rs.at[3 * t + e],
                                      (_flip(cx, kx), _flip(cy, ky), cc)))
            return cps

        def w_copy(pc):
            return pltpu.make_async_copy(w_hbm.at[pc, pl.ds(0, D), :], w_s.at[pc], wsem.at[pc])

        for pc in range(4):
            @pl.when((i == 0) & (my_shard != pc))
            def _(pc=pc):
                w_copy(pc).start()

            @pl.when((i == 0) & (my_shard == pc))
            def _(pc=pc):
                pltpu.make_async_copy(wsh_hbm, w_s.at[pc], wsem.at[pc]).start()

        @pl.when(i == 0)
        def _():
            vec_ref[...] = jnp.zeros_like(vec_ref)
            for cp in scatter_copies():
                cp.start()

        dh = None
        for pc, group in enumerate(groups):
            @pl.when(i == 0)
            def _(pc=pc):
                w_copy(pc).wait()

            tiles = [d_refs[t][...] for t in group]
            lhs = tiles[0] if len(tiles) == 1 else jnp.concatenate(tiles, axis=1)
            part = _dot_nt(lhs, w_s[pc])
            dh = part if dh is None else dh + part

        xt = x_ref[...]
        rstd = lax.rsqrt(jnp.mean(xt * xt, axis=-1, keepdims=True) + EPS)
        xh = xt * rstd
        gn_v = gn_ref[...]
        sc1 = 1.0 + sc_ref[...]
        dhx = dh * xh
        vec_ref[0:1, :] += _colsum(dh)
        vec_ref[1:2, :] += _colsum(dhx) * gn_v
        vec_ref[2:3, :] += _colsum(dhx) * sc1
        dxh = dh * (gn_v * sc1)
        gx_ref[...] = rstd * (dxh - xh * jnp.mean(dxh * xh, axis=-1, keepdims=True)) + dx2_ref[...]

        @pl.when(i == ni - 1)
        def _():
            for cp in scatter_copies():
                cp.wait()

    rowt = lambda i: (i, 0)
    row = pl.BlockSpec((1, D), lambda i: (0, 0))
    any_ = pl.BlockSpec(memory_space=pl.ANY)
    outs = pl.pallas_call(
        body, name="inproj_bwd_x", grid=(ni,),
        in_specs=[pl.BlockSpec((tm, a.shape[1]), rowt) for a in pieces] +
                 [any_, any_, pl.BlockSpec((tm, D), rowt), pl.BlockSpec((tm, D), rowt), row, row] + [any_] * na,
        out_specs=(pl.BlockSpec((tm, D), rowt), pl.BlockSpec((8, D), lambda i: (0, 0))) + (any_,) * na,
        out_shape=(jax.ShapeDtypeStruct((s, D), F32), jax.ShapeDtypeStruct((8, D), F32)) +
                  tuple(jax.ShapeDtypeStruct((3,) + q.shape[1:], q.dtype) for q in sums),
        scratch_shapes=[pltpu.VMEM((4, D, PW), BF16), pltpu.SemaphoreType.DMA((4,)),
                        pltpu.SemaphoreType.DMA((3 * na,)), pltpu.SemaphoreType.DMA((3 * na,))],
        compiler_params=_params(("arbitrary",)),
    )(*pieces, wg, wsh, x, dx2, gn, scale, *sums)
    return outs[0], outs[1], outs[2:]


def _inproj_bwd_w(pieces, hbf, g_out, tk=1024):
    s = hbf.shape[0]
    steps, nk = _pieces_steps(pieces)
    npc = PW // D
    ns = s // tk
    np_ = len(pieces)
    hr = D // 2
    ohr = g_out.shape[1] // 2
    ocr = _chunk_rows(g_out)
    ochunks = [(j, r0) for j in range(g_out.shape[0]) for r0 in range(0, ohr, ocr)]
    noc = len(ochunks)

    def body(*refs):
        d_refs = refs[:np_]
        h_ref, go_hbm, g_ref, rb_hbm, rbo_hbm, stage, ss, rs, oss, ors = refs[np_:]
        cb, k = pl.program_id(0), pl.program_id(1)
        cx, cy, cc = _coords()
        sib = (cx, cy, 1 - cc)

        def block_copy(b):
            return _rcopy(stage.at[b % 2],
                          rb_hbm.at[b // npc, :, pl.ds(pl.multiple_of((b % npc) * D, D), D)], ss.at[b], rs.at[b], sib)

        def out_copy(e):
            j, r0 = ochunks[e]
            return _rcopy(go_hbm.at[j, pl.ds((1 - cc) * ohr + r0, ocr), :], rbo_hbm.at[j, pl.ds(r0, ocr), :],
                          oss.at[e], ors.at[e], sib)

        @pl.when((cb == 0) & (k == 0))
        def _():
            for e in range(noc):
                out_copy(e).start()

        @pl.when(k == 0)
        def _():
            g_ref[...] = jnp.zeros_like(g_ref)

        for (s0, n), d_ref in zip(steps, d_refs):
            @pl.when((cb >= s0) & (cb < s0 + n))
            def _(d_ref=d_ref):
                g_ref[0] += _dot_tn(h_ref[...], d_ref[...])

        @pl.when((k == ns - 1) & (cb > 1))
        def _():
            block_copy(cb - 2).wait_send()

        @pl.when(k == ns - 1)
        def _():
            stage[cb % 2] = g_ref[0, pl.ds(pl.multiple_of((1 - cc) * hr, hr), hr), :]
            block_copy(cb).start()

        @pl.when((k == ns - 1) & (cb == nk - 1))
        def _():
            block_copy(nk - 2).wait_send()
            block_copy(nk - 1).wait_send()
            for b in range(nk):
                block_copy(b).wait_recv()
            for e in range(noc):
                out_copy(e).wait_recv()
                out_copy(e).wait_send()

    def piece_spec(s0, n):
        def imap(cb, k):
            active = (cb >= s0) & (cb < s0 + n)
            return (jnp.where(active, k, 0), jnp.clip(cb - s0, 0, n - 1))
        return pl.BlockSpec((tk, D), imap)

    any_ = pl.BlockSpec(memory_space=pl.ANY)
    return pl.pallas_call(
        body, name="inproj_bwd_w", grid=(nk, ns),
        in_specs=[piece_spec(s0, n) for s0, n in steps] + [pl.BlockSpec((tk, D), lambda cb, k: (k, 0)), any_],
        out_specs=(pl.BlockSpec((1, D, D), lambda cb, k: (cb // npc, 0, cb % npc)), any_, any_),
        out_shape=(jax.ShapeDtypeStruct((4, D, PW), F32), jax.ShapeDtypeStruct((4, hr, PW), F32),
                   jax.ShapeDtypeStruct((g_out.shape[0], ohr, g_out.shape[2]), F32)),
        scratch_shapes=[pltpu.VMEM((2, hr, D), F32), pltpu.SemaphoreType.DMA((nk,)), pltpu.SemaphoreType.DMA((nk,)),
                        pltpu.SemaphoreType.DMA((noc,)), pltpu.SemaphoreType.DMA((noc,))],
        compiler_params=_params(("arbitrary", "arbitrary")),
    )(*pieces, hbf, g_out)


D2D_CHUNK_BYTES = 512 * 1024


def _chunk_rows(a):
    return max(8, D2D_CHUNK_BYTES // (a.shape[-1] * a.dtype.itemsize))


def _pair_swap(arrs):
    na = len(arrs)
    chunks = []
    for t, a in enumerate(arrs):
        cr = _chunk_rows(a)
        chunks += [(t, r0, cr) for r0 in range(0, a.shape[0], cr)]
    nch = len(chunks)

    def body(*refs):
        a_refs = refs[:na]
        o_refs = refs[na:2 * na]
        ss, rs = refs[2 * na:]
        x, y, c = _coords()
        sib = (x, y, 1 - c)
        rcs = []
        for n, (t, r0, cr) in enumerate(chunks):
            rows = pl.ds(r0, cr)
            rc = _rcopy(a_refs[t].at[rows, :], o_refs[t].at[rows, :], ss.at[n], rs.at[n], sib)
            rc.start()
            rcs.append(rc)
        for rc in rcs:
            rc.wait_recv()
        for rc in rcs:
            rc.wait_send()

    any_ = pl.BlockSpec(memory_space=pl.ANY)
    return pl.pallas_call(
        body, name="pair_swap",
        out_shape=tuple(jax.ShapeDtypeStruct(a.shape, a.dtype) for a in arrs),
        in_specs=[any_] * na, out_specs=tuple([any_] * na),
        scratch_shapes=[pltpu.SemaphoreType.DMA((nch,)), pltpu.SemaphoreType.DMA((nch,))],
        compiler_params=_params(),
    )(*arrs)


def _add_half(full, rb, core, tr):
    n, r, cdim = full.shape
    nb = r // 2 // tr

    def body(c_ref, a_ref, b_ref, o_ref, ob_ref):
        tot = a_ref[...] + b_ref[...]
        o_ref[...] = tot
        ob_ref[...] = tot.astype(BF16)

    mine = pl.BlockSpec((1, tr, cdim), lambda i, j, c_ref: (i, c_ref[0] * nb + j, 0))
    spec = pl.BlockSpec((1, tr, cdim), lambda i, j, c_ref: (i, j, 0))
    return pl.pallas_call(
        body, name="add_half",
        grid_spec=pltpu.PrefetchScalarGridSpec(num_scalar_prefetch=1, grid=(n, nb), in_specs=[mine, spec],
                                               out_specs=(spec, spec)),
        out_shape=(jax.ShapeDtypeStruct(rb.shape, rb.dtype), jax.ShapeDtypeStruct(rb.shape, BF16)),
        compiler_params=_params(("parallel", "parallel")),
    )(core, full, rb)


def _sum_slots(q, r3, shard, tr):
    _, r, cdim = q.shape

    def body(j_ref, q_ref, r_ref, o_ref):
        o_ref[...] = ((q_ref[0] + r_ref[0].astype(F32)) + r_ref[1].astype(F32)) + r_ref[2].astype(F32)

    return pl.pallas_call(
        body, name="sum_slots",
        grid_spec=pltpu.PrefetchScalarGridSpec(
            num_scalar_prefetch=1, grid=(r // tr,),
            in_specs=[pl.BlockSpec((1, tr, cdim), lambda i, j_ref: (j_ref[0], i, 0)),
                      pl.BlockSpec((3, tr, cdim), lambda i, j_ref: (0, i, 0))],
            out_specs=pl.BlockSpec((tr, cdim), lambda i, j_ref: (i, 0))),
        out_shape=jax.ShapeDtypeStruct((r, cdim), q.dtype),
        compiler_params=_params(("parallel",)),
    )(shard, q, r3)


def _allreduce_phase(phase, p_refs, o_refs, rbufs, s1, r1, s2, r2, out_refs=None):
    me = _my_index()

    def chunk(t, d):
        ch = p_refs[t].shape[0] // NDEV
        return pl.ds(pl.multiple_of(d * ch, 8), ch)

    def scatter(t, k):
        e = 7 * t + k - 1
        return _rcopy(p_refs[t].at[chunk(t, me ^ k)], rbufs[t].at[me], s1.at[e], r1.at[e], _peer(k))

    def gather(t, k):
        e = 7 * t + k - 1
        return _rcopy(o_refs[t].at[chunk(t, me)], o_refs[t].at[chunk(t, me)], s2.at[e], r2.at[e], _peer(k))

    for t in range(len(p_refs)):
        if phase == 0:
            for k in range(1, NDEV):
                scatter(t, k).start()
            rbufs[t][me] = p_refs[t][chunk(t, me), :]
        elif phase == 1:
            for k in range(1, NDEV):
                e = 7 * t + k - 1
                _rcopy(p_refs[t].at[chunk(t, me)], rbufs[t].at[me ^ k], s1.at[e], r1.at[e], _peer(k)).wait_recv()
            tot = rbufs[t][0]
            for d in range(1, NDEV):
                tot = tot + rbufs[t][d]
            o_refs[t][chunk(t, me), :] = tot
            for k in range(1, NDEV):
                gather(t, k).start()
        else:
            for k in range(1, NDEV):
                e = 7 * t + k - 1
                _rcopy(o_refs[t].at[chunk(t, me)], o_refs[t].at[chunk(t, me ^ k)], s2.at[e], r2.at[e],
                       _peer(k)).wait_recv()
            for k in range(1, NDEV):
                scatter(t, k).wait_send()
                gather(t, k).wait_send()
            if out_refs is not None:
                out_refs[t][...] = o_refs[t][...]


def _adamw_update(w, g, m, v):
    nm = B1 * m + (1.0 - B1) * g
    nv = B2 * v + (1.0 - B2) * (g * g)
    m_hat = nm / (1.0 - B1 ** STEP)
    v_hat = nv / (1.0 - B2 ** STEP)
    return -LR * (m_hat / (jnp.sqrt(v_hat) + ADAM_EPS) + WD * w), nm, nv


def _adamw(w, g, m, v, tr):
    r, cdim = w.shape

    def body(w_ref, g_ref, m_ref, v_ref, d_ref, nm_ref, nv_ref):
        d_ref[...], nm_ref[...], nv_ref[...] = _adamw_update(w_ref[...], g_ref[...], m_ref[...], v_ref[...])

    spec = pl.BlockSpec((tr, cdim), lambda i: (i, 0))
    sd = jax.ShapeDtypeStruct((r, cdim), F32)
    return pl.pallas_call(
        body, name="adamw", grid=(r // tr,), in_specs=[spec] * 4, out_specs=(spec,) * 3, out_shape=(sd,) * 3,
        compiler_params=_params(("parallel",)),
    )(w, g, m, v)


V_B_GATE, V_CONV_B, V_LAM, V_G_FINAL, V_CONV_W, V_LOSS, V_ROWS = 0, 2, 3, 4, 5, 9, 64
M_W_A, M_W_X, M_B_A, M_B_X, M_ROWS = 0, H * DH, 2 * H * DH, 2 * H * DH + H, 2112
SMALL = ("g_norm", "b_mod", "b_gate", "conv_b", "lam", "g_final", "conv_w", "w_a", "w_x", "b_a", "b_x")


def _adamw_small(redv, redm, g_conv, g_gnorm, g_bmod, wmv):
    def grad(name, rv, rm, gc, gg, gb):
        if name == "g_norm":
            return gg[...]
        if name == "b_mod":
            return gb[...]
        if name == "b_gate":
            return jnp.concatenate([rv[V_B_GATE + t:V_B_GATE + t + 1, :] for t in range(2)], axis=1)
        if name == "conv_b":
            return rv[V_CONV_B:V_CONV_B + 1, :]
        if name == "lam":
            return rv[V_LAM:V_LAM + 1, :]
        if name == "g_final":
            return rv[V_G_FINAL:V_G_FINAL + 1, :]
        if name == "conv_w":
            return gc[...]
        if name == "w_a":
            return rm[M_W_A:M_W_A + H * DH, :]
        if name == "w_x":
            return rm[M_W_X:M_W_X + H * DH, :]
        if name == "b_a":
            return rm[M_B_A:M_B_A + H, :]
        return rm[M_B_X:M_B_X + H, :]

    n = len(SMALL)

    def body(*refs):
        rv, rm, gc, gg, gb = refs[:5]
        ins, outs = refs[5:5 + 3 * n], refs[5 + 3 * n:]
        for t, name in enumerate(SMALL):
            w_ref, m_ref, v_ref = ins[3 * t:3 * t + 3]
            g_out, d_out, m_out, v_out = outs[4 * t:4 * t + 4]
            g = grad(name, rv, rm, gc, gg, gb)
            g_out[...] = g
            d_out[...], m_out[...], v_out[...] = _adamw_update(w_ref[...], g, m_ref[...], v_ref[...])

    vm = pl.BlockSpec(memory_space=pltpu.VMEM)
    flat = [a for name in SMALL for a in wmv[name]]
    shapes = [jax.ShapeDtypeStruct(wmv[name][0].shape, F32) for name in SMALL for _ in range(4)]
    outs = pl.pallas_call(
        body, name="adamw_small", out_shape=tuple(shapes),
        in_specs=[vm] * (5 + len(flat)), out_specs=tuple([vm] * len(shapes)),
        compiler_params=_params(),
    )(redv, redm, g_conv, g_gnorm, g_bmod, *flat)
    return {name: outs[4 * t:4 * t + 4] for t, name in enumerate(SMALL)}


def _rope_tables(positions):
    inv_freq = ROPE_THETA ** (-jnp.arange(0, ROT, 2, dtype=F32) / ROT)
    ang = positions.astype(F32)[:, None] * inv_freq
    cos, sin = jnp.cos(ang), jnp.sin(ang)
    n = positions.shape[0]
    half = ROT // 2
    rc = jnp.concatenate([cos, cos, jnp.ones((n, DH - ROT), F32)], axis=1)
    rsa = jnp.concatenate([-sin, jnp.zeros((n, DH - half), F32)], axis=1)
    rsb = jnp.concatenate([jnp.zeros((n, half), F32), sin, jnp.zeros((n, DH - ROT), F32)], axis=1)
    return rc, rsa, rsb


def kernel(x, c, positions, g_norm, w_mod, b_mod, w_in, b_gate, conv_w, conv_b, w_a, b_a, w_x, b_x, lam, w_out_rnn, w_out_attn, w_o, g_final, loss_target, m_g_norm, m_w_mod, m_b_mod, m_w_in, m_b_gate, m_conv_w, m_conv_b, m_w_a, m_b_a, m_w_x, m_b_x, m_lam, m_w_out_rnn, m_w_out_attn, m_w_o, m_g_final, v_g_norm, v_w_mod, v_b_mod, v_w_in, v_b_gate, v_conv_w, v_conv_b, v_w_a, v_b_a, v_w_x, v_b_x, v_lam, v_w_out_rnn, v_w_out_attn, v_w_o, v_g_final):
    s = x.shape[1]
    xi = lax.axis_index("x")
    yi = lax.axis_index("y")
    ci = lax.axis_index("c")
    shard = 2 * xi + yi
    x2d = x[0]
    tgt = loss_target[0]
    pos = positions[0]

    c_all, mod4, conv_all = _mod_fwd(c, w_mod[0], b_mod.reshape(4, 1, 768), conv_w[0])
    mod = mod4.reshape(1, 3 * D)
    shift, scale, gate = mod[:, :D], mod[:, D:2 * D], mod[:, 2 * D:]
    w3sh = jnp.concatenate([w_out_rnn[0], w_out_attn[0], w_o[0]], axis=0).astype(BF16)
    wsh = w_in[0].astype(BF16)
    conv_full = conv_all[0::2].transpose(1, 0, 2).reshape(4, D)

    order = jnp.stack([shard, shard ^ 2, shard ^ 1, shard ^ 3]).astype(jnp.int32)
    p, hbf, wg = _gather_norm_inproj(x2d, g_norm, shift, scale, wsh, order)
    rc, rsa, rsb = _rope_tables(pos)
    pos_col = jnp.broadcast_to((pos == 0).astype(F32)[:, None], (s, DH))
    b_a3, b_x3 = b_a.reshape(H, 1, DH), b_x.reshape(H, 1, DH)
    hr, gr = _rnn_fwd(p, pos_col, conv_full, conv_b, w_a[0], b_a3, w_x[0], b_x3, lam)
    o, lse, ga, q_rot, k_rot, w3g = _attn_fwd(p, rc, rsa, rsb, w3sh)
    w3g = lax.dynamic_update_slice(w3g, w3sh[None], (shard, 0, 0))
    w3 = w3g.reshape(4, 3, 256, D).transpose(1, 0, 2, 3).reshape(3, D, D)

    dgr, dga, dc, dx2, vec_t, g_out = _tail(gr, ga, p, x2d, tgt, w3, b_gate, gate, g_final.reshape(1, D))

    dxr, dzr, g_wa, g_ba, g_wx, g_bx, g_lam, g_cw, g_cb = _rnn_bwd(
        p, hr, dgr, pos_col, conv_full, conv_b, w_a[0], b_a3, w_x[0], b_x3, lam)
    vpack = jnp.concatenate([
        vec_t[2:4],
        g_cb.reshape(1, D),
        g_lam.reshape(1, D),
        vec_t[0:1],
        g_cw.transpose(1, 0, 2).reshape(4, D),
        vec_t[4:5],
        jnp.zeros((V_ROWS - 10, D), F32)], axis=0)
    mpack = jnp.concatenate([
        g_wa.reshape(H * DH, DH), g_wx.reshape(H * DH, DH), g_ba.reshape(H, DH), g_bx.reshape(H, DH),
        jnp.zeros((M_ROWS - 2 * H * DH - 2 * H, DH), F32)], axis=0)
    (dq, dk, dv, dza), (redv, redm) = _attn_bwd(p, q_rot, k_rot, o, lse, dga, rc, rsa, rsb, [vpack, mpack])

    pieces = [dxr, dzr, dq, dk, dv, dza, dc]
    g_win, rb_a, rb_b = _inproj_bwd_w(pieces, hbf, g_out)

    core = ci.reshape(1)
    shard1 = shard.reshape(1)
    (q_a, qh_a), (q_b, qh_b) = _add_half(g_win, rb_a, core, tr=256), _add_half(g_out, rb_b, core, tr=128)
    grad_x, vec_n, (r_a, r_b) = _inproj_bwd_x(pieces, wg, wsh, x2d, dx2, g_norm, scale, [qh_a, qh_b])
    f_a, f_b = _sum_slots(q_a, r_a, shard1, tr=256), _sum_slots(q_b, r_b, shard1, tr=128)
    s_a, s_b = _pair_swap([f_a, f_b])
    south = ci == 0
    grad_w_in = jnp.where(south, jnp.concatenate([f_a, s_a], axis=0), jnp.concatenate([s_a, f_a], axis=0))
    g3 = jnp.where(south, jnp.concatenate([f_b, s_b], axis=0), jnp.concatenate([s_b, f_b], axis=0)).reshape(3, 256, D)

    dmod_row = jnp.concatenate([vec_n[0:1], vec_n[1:2], vec_t[1:2]], axis=1)
    loss = redv[V_LOSS, 0]
    grad_w_mod, g_bmod4, g_gnorm = _mod_bwd(dmod_row.reshape(4, 1, 768), vec_n[2:3], c_all)
    g_conv_sh = lax.dynamic_slice_in_dim(redv[V_CONV_W:V_CONV_W + 4], shard * 256, 256, axis=1)

    shape2d = dict(g_norm=(1, D), b_mod=(1, 3 * D), b_gate=(1, 2 * D), conv_b=(1, D), lam=(1, D), g_final=(1, D),
                   conv_w=(4, 256), w_a=(H * DH, DH), w_x=(H * DH, DH), b_a=(H, DH), b_x=(H, DH))
    given = dict(
        g_norm=(g_norm, m_g_norm, v_g_norm), b_mod=(b_mod, m_b_mod, v_b_mod), b_gate=(b_gate, m_b_gate, v_b_gate),
        conv_b=(conv_b, m_conv_b, v_conv_b), lam=(lam, m_lam, v_lam), g_final=(g_final, m_g_final, v_g_final),
        conv_w=(conv_w, m_conv_w, v_conv_w), w_a=(w_a, m_w_a, v_w_a), w_x=(w_x, m_w_x, v_w_x),
        b_a=(b_a, m_b_a, v_b_a), b_x=(b_x, m_b_x, v_b_x))
    small = _adamw_small(redv, redm, g_conv_sh, g_gnorm, g_bmod4.reshape(1, 3 * D),
                         {n: tuple(a.reshape(shape2d[n]) for a in given[n]) for n in SMALL})

    big_in = _adamw(w_in[0], grad_w_in, m_w_in[0], v_w_in[0], tr=256)
    big_mod = _adamw(w_mod[0], grad_w_mod, m_w_mod[0], v_w_mod[0], tr=256)
    w3f = jnp.concatenate([w_out_rnn[0], w_out_attn[0], w_o[0]], axis=0)
    m3f = jnp.concatenate([m_w_out_rnn[0], m_w_out_attn[0], m_w_o[0]], axis=0)
    v3f = jnp.concatenate([v_w_out_rnn[0], v_w_out_attn[0], v_w_o[0]], axis=0)
    big_out = _adamw(w3f, g3.reshape(768, D), m3f, v3f, tr=256)

    names = ["g_norm", "w_mod", "b_mod", "w_in", "b_gate", "conv_w", "conv_b", "w_a", "b_a", "w_x", "b_x", "lam",
             "w_out_rnn", "w_out_attn", "w_o", "g_final"]
    outs = []
    for idx in range(4):
        d = {n: small[n][idx].reshape(given[n][0].shape) for n in SMALL}
        if idx == 0:
            d.update(w_mod=grad_w_mod[None], w_in=grad_w_in[None],
                     w_out_rnn=g3[0][None], w_out_attn=g3[1][None], w_o=g3[2][None])
        else:
            d.update(w_mod=big_mod[idx - 1][None], w_in=big_in[idx - 1][None],
                     w_out_rnn=big_out[idx - 1][0:256][None], w_out_attn=big_out[idx - 1][256:512][None],
                     w_o=big_out[idx - 1][512:768][None])
        outs.append(d)
    flat = [d[n] for d in outs for n in names]
    return (loss, grad_x[None], *flat)
```

```python
import jax
import jax.numpy as jnp
from jax import lax
from jax.experimental import pallas as pl
from jax.experimental.pallas import tpu as pltpu

F32, BF16 = jnp.float32, jnp.bfloat16
MESH = pl.DeviceIdType.MESH
HIGHEST = lax.Precision.HIGHEST

D = 1024
H = 8
DH = 128
PW = 2048
EPS = 1e-6
LRU_C = 8.0
SCALE = DH ** -0.5
NEG = -1e30
SPAN = 2048
UB = 128
DILATIONS = (1, 4, 16)
UNIT_BATCH = 16
UNIT_BATCH_FWD = 8
ROPE_THETA = 500000.0
ROT = 32

LR, B1, B2, ADAM_EPS, WD, STEP = 0.001, 0.9, 0.999, 1e-08, 0.01, 10

NDEV = 8


def _params(sem=None, vmem_mb=56):
    return pltpu.CompilerParams(dimension_semantics=sem, vmem_limit_bytes=vmem_mb * 2 ** 20)


def _coords():
    return lax.axis_index("x"), lax.axis_index("y"), lax.axis_index("c")


def _flip(v, bit):
    return 1 - v if bit else v


def _peer(k):
    x, y, c = _coords()
    return (_flip(x, (k >> 2) & 1), _flip(y, (k >> 1) & 1), _flip(c, k & 1))


def _my_index():
    x, y, c = _coords()
    return 4 * x + 2 * y + c


def _rcopy(src, dst, ssem, rsem, dev):
    return pltpu.make_async_remote_copy(src_ref=src, dst_ref=dst, send_sem=ssem, recv_sem=rsem,
                                        device_id=dev, device_id_type=MESH)


def _sigmoid(x):
    return jax.nn.sigmoid(x)


def _dot(a, b):
    return jnp.dot(a, b, preferred_element_type=F32)


def _dot_nt(a, b):
    return lax.dot_general(a, b, (((1,), (1,)), ((), ())), preferred_element_type=F32)


def _dot_tn(a, b):
    return lax.dot_general(a, b, (((0,), (0,)), ((), ())), preferred_element_type=F32)


def _colsum(a):
    return jnp.sum(a, axis=0, keepdims=True)


def _mod_fwd(c, w_mod_sh, b_mod4, conv_sh):
    def body(c_ref, w_ref, b_ref, cv_ref, call_ref, mod_ref, cvall_ref, rows_ref, cmat_ref, s1, r1, s2, r2, s3, r3):
        x, y, _ = _coords()
        me = _my_index()
        j = 2 * x + y
        call_ref[me] = c_ref[...]
        cvall_ref[me] = cv_ref[...]
        sends = []
        for k in range(1, NDEV):
            cp = _rcopy(call_ref.at[me], call_ref.at[me], s1.at[k - 1], r1.at[k - 1], _peer(k))
            cp.start()
            sends.append(cp)
            cp = _rcopy(cvall_ref.at[me], cvall_ref.at[me], s3.at[k - 1], r3.at[k - 1], _peer(k))
            cp.start()
            sends.append(cp)
        for k in range(1, NDEV):
            pk = me ^ k
            _rcopy(call_ref.at[pk], call_ref.at[pk], s1.at[k - 1], r1.at[k - 1], _peer(k)).wait_recv()
        for b in range(NDEV):
            cmat_ref[pl.ds(b, 1), :] = call_ref[b]
        cm = cmat_ref[...]
        act = cm * _sigmoid(cm)
        mp = jnp.dot(act, w_ref[...], preferred_element_type=F32, precision=HIGHEST) + b_ref[j]
        for b in range(NDEV):
            rows_ref[b] = mp[b:b + 1]
        mod_ref[j] = rows_ref[me]
        for q, k in enumerate((2, 4, 6)):
            cp = _rcopy(rows_ref.at[me ^ k], mod_ref.at[j], s2.at[q], r2.at[q], _peer(k))
            cp.start()
            sends.append(cp)
        for q, k in enumerate((2, 4, 6)):
            jq = j ^ (k >> 1)
            _rcopy(rows_ref.at[me], mod_ref.at[jq], s2.at[q], r2.at[q], _peer(k)).wait_recv()
        for k in range(1, NDEV):
            pk = me ^ k
            _rcopy(cvall_ref.at[pk], cvall_ref.at[pk], s3.at[k - 1], r3.at[k - 1], _peer(k)).wait_recv()
        for cp in sends:
            cp.wait_send()

    vm = pl.BlockSpec(memory_space=pltpu.VMEM)
    return pl.pallas_call(
        body, name="mod_fwd",
        out_shape=(jax.ShapeDtypeStruct((NDEV, 1, D), F32), jax.ShapeDtypeStruct((4, 1, 768), F32),
                   jax.ShapeDtypeStruct((NDEV,) + conv_sh.shape, F32)),
        in_specs=[vm, vm, vm, vm], out_specs=(vm, vm, vm),
        scratch_shapes=[pltpu.VMEM((NDEV, 1, 768), F32), pltpu.VMEM((NDEV, D), F32),
                        pltpu.SemaphoreType.DMA((7,)), pltpu.SemaphoreType.DMA((7,)),
                        pltpu.SemaphoreType.DMA((3,)), pltpu.SemaphoreType.DMA((3,)),
                        pltpu.SemaphoreType.DMA((7,)), pltpu.SemaphoreType.DMA((7,))],
        compiler_params=_params(),
    )(c, w_mod_sh, b_mod4, conv_sh)


def _mod_bwd(dmod4, gn_row, c_all):
    def body(d_ref, g_ref, call_ref, gw_ref, gb_ref, gg_ref, dall_ref, gall_ref, cmat_ref, dmat_ref, s1, r1, s2, r2):
        x, y, _ = _coords()
        me = _my_index()
        j = 2 * x + y
        dall_ref[me] = d_ref[...]
        gall_ref[me] = g_ref[...]
        sends = []
        for k in range(1, NDEV):
            for buf, ss, rs in ((dall_ref, s1, r1), (gall_ref, s2, r2)):
                cp = _rcopy(buf.at[me], buf.at[me], ss.at[k - 1], rs.at[k - 1], _peer(k))
                cp.start()
                sends.append(cp)
        for k in range(1, NDEV):
            pk = me ^ k
            for buf, ss, rs in ((dall_ref, s1, r1), (gall_ref, s2, r2)):
                _rcopy(buf.at[pk], buf.at[pk], ss.at[k - 1], rs.at[k - 1], _peer(k)).wait_recv()
        for cp in sends:
            cp.wait_send()
        gb, gg = dall_ref[0], gall_ref[0]
        for b in range(1, NDEV):
            gb = gb + dall_ref[b]
            gg = gg + gall_ref[b]
        gb_ref[...] = gb
        gg_ref[...] = gg
        for b in range(NDEV):
            cmat_ref[pl.ds(b, 1), :] = call_ref[b]
            dmat_ref[pl.ds(b, 1), :] = dall_ref[b, j]
        cm = cmat_ref[...]
        act = cm * _sigmoid(cm)
        gw_ref[...] = lax.dot_general(act, dmat_ref[...], (((0,), (0,)), ((), ())),
                                      preferred_element_type=F32, precision=HIGHEST)

    vm = pl.BlockSpec(memory_space=pltpu.VMEM)
    return pl.pallas_call(
        body, name="mod_bwd",
        out_shape=(jax.ShapeDtypeStruct((D, 768), F32), jax.ShapeDtypeStruct((4, 1, 768), F32),
                   jax.ShapeDtypeStruct((1, D), F32)),
        in_specs=[vm, vm, vm], out_specs=(vm, vm, vm),
        scratch_shapes=[pltpu.VMEM((NDEV, 4, 1, 768), F32), pltpu.VMEM((NDEV, 1, D), F32),
                        pltpu.VMEM((NDEV, D), F32), pltpu.VMEM((NDEV, 768), F32),
                        pltpu.SemaphoreType.DMA((7,)), pltpu.SemaphoreType.DMA((7,)),
                        pltpu.SemaphoreType.DMA((7,)), pltpu.SemaphoreType.DMA((7,))],
        compiler_params=_params(),
    )(dmod4, gn_row, c_all)


def _gather_norm_inproj(x, gn, shift, scale, wsh, order, tm=1024, tn=1024):
    s = x.shape[0]
    ni = s // tm
    npc = PW // tn
    rows, cols = wsh.shape
    half = rows // 2
    nch = 4
    cr = half // nch
    chips = ((1, 0), (0, 1), (1, 1))

    def body(ord_ref, x_ref, gn_ref, sh_ref, sc_ref, wsh_hbm, p_ref, h_ref, wg_hbm, hs_all, w_s, wsem, ss, rs):
        slot, i, col = pl.program_id(0), pl.program_id(1), pl.program_id(2)
        cx, cy, cc = _coords()
        j = 2 * cx + cy
        sib = (cx, cy, 1 - cc)
        mine = lambda n: pl.ds(cc * half + n * cr, cr)
        theirs = lambda n: pl.ds((1 - cc) * half + n * cr, cr)
        shard_of = lambda q: j ^ (2 * chips[q][0] + chips[q][1])

        def to_chip(q, n):
            e = nch * q + n
            return _rcopy(wsh_hbm.at[mine(n)], wg_hbm.at[j, mine(n)], ss.at[e], rs.at[e],
                          (_flip(cx, chips[q][0]), _flip(cy, chips[q][1]), cc))

        def from_chip(q, n):
            e = nch * q + n
            return _rcopy(wsh_hbm.at[mine(n)], wg_hbm.at[shard_of(q), mine(n)], ss.at[e], rs.at[e], sib)

        def to_sibling(q, n):
            e = 3 * nch + nch * q + n
            return _rcopy(wg_hbm.at[shard_of(q), mine(n)], wg_hbm.at[shard_of(q), mine(n)], ss.at[e], rs.at[e], sib)

        def from_sibling(q, n):
            e = 3 * nch + nch * q + n
            return _rcopy(wsh_hbm.at[mine(n)], wg_hbm.at[shard_of(q), theirs(n)], ss.at[e], rs.at[e], sib)

        def load(sl, src):
            cp = pltpu.make_async_copy(src, w_s.at[sl], wsem.at[sl])
            cp.start()
            cp.wait()

        first = (i == 0) & (col == 0)

        @pl.when(first & (slot == 0))
        def _():
            for n in range(nch):
                for q in (0, 1):
                    to_chip(q, n).start()
            load(0, wsh_hbm.at[pl.ds(0, D), :])

        @pl.when(first & (slot == 1))
        def _():
            for q in (0, 1):
                for n in range(nch):
                    from_chip(q, n).wait_recv()
                    to_sibling(q, n).start()
            for n in range(nch):
                to_chip(2, n).start()
            for n in range(nch):
                from_sibling(0, n).wait_recv()
            load(1, wg_hbm.at[shard_of(0), pl.ds(0, D), :])

        @pl.when(first & (slot == 2))
        def _():
            for n in range(nch):
                from_sibling(1, n).wait_recv()
            load(2, wg_hbm.at[shard_of(1), pl.ds(0, D), :])

        @pl.when(first & (slot == 3))
        def _():
            for n in range(nch):
                from_chip(2, n).wait_recv()
                to_sibling(2, n).start()
            for n in range(nch):
                from_sibling(2, n).wait_recv()
            load(3, wg_hbm.at[shard_of(2), pl.ds(0, D), :])
            for q in range(3):
                for n in range(nch):
                    to_chip(q, n).wait_send()
                    to_sibling(q, n).wait_send()

        @pl.when((slot == 0) & (col == 0))
        def _():
            xt = x_ref[...]
            rstd = lax.rsqrt(jnp.mean(xt * xt, axis=-1, keepdims=True) + EPS)
            h = ((xt * rstd * gn_ref[...]) * (1.0 + sc_ref[...]) + sh_ref[...]).astype(BF16)
            hs_all[i] = h
            h_ref[...] = h

        p_ref[0] = _dot(hs_all[i], w_s[slot, :, pl.ds(pl.multiple_of(col * tn, tn), tn)]).astype(BF16)

    row = pl.BlockSpec((1, D), lambda sl, i, col, o: (0, 0))
    x_rows = lambda sl, i, col, o: (jnp.where(sl == 0, i, ni - 1), 0)
    any_ = pl.BlockSpec(memory_space=pl.ANY)
    return pl.pallas_call(
        body, name="gather_norm_inproj",
        grid_spec=pltpu.PrefetchScalarGridSpec(
            num_scalar_prefetch=1, grid=(4, ni, npc),
            in_specs=[pl.BlockSpec((tm, D), x_rows), row, row, row, any_],
            out_specs=(pl.BlockSpec((1, tm, tn), lambda sl, i, col, o: (o[sl], i, col)),
                       pl.BlockSpec((tm, D), x_rows), any_),
            scratch_shapes=[pltpu.VMEM((ni, tm, D), BF16), pltpu.VMEM((4, D, PW), BF16),
                            pltpu.SemaphoreType.DMA((4,)),
                            pltpu.SemaphoreType.DMA((6 * nch,)), pltpu.SemaphoreType.DMA((6 * nch,))]),
        out_shape=(jax.ShapeDtypeStruct((4, s, PW), BF16), jax.ShapeDtypeStruct((s, D), BF16),
                   jax.ShapeDtypeStruct((4, rows, cols), wsh.dtype)),
        compiler_params=_params(("arbitrary", "arbitrary", "arbitrary")),
    )(order, x, gn, shift, scale, wsh)


def _shift_down(prev8, cur, d):
    t = cur.shape[0]
    c3 = cur.reshape(t // 8, 8, DH)
    rot = pltpu.roll(c3, d, 1)
    before = jnp.concatenate([pltpu.roll(prev8, d, 0).reshape(1, 8, DH), rot[:-1]], axis=0)
    rows = lax.broadcasted_iota(jnp.int32, c3.shape, 1)
    return jnp.where(rows >= d, rot, before).reshape(t, DH)


def _shift_up(cur, next8, d):
    t = cur.shape[0]
    c3 = cur.reshape(t // 8, 8, DH)
    rot = pltpu.roll(c3, 8 - d, 1)
    after = jnp.concatenate([rot[1:], pltpu.roll(next8, 8 - d, 0).reshape(1, 8, DH)], axis=0)
    rows = lax.broadcasted_iota(jnp.int32, c3.shape, 1)
    return jnp.where(rows < 8 - d, rot, after).reshape(t, DH)


def _rnn_gates(xr, prev8, cw, cb, wa, ba, wx, bx, lam, reset):
    xc = cw[3:4] * xr + cb
    for d in (1, 2, 3):
        xc = xc + cw[3 - d:4 - d] * _shift_down(prev8, xr, d)
    xcb = xc.astype(BF16)
    r = _sigmoid(_dot(xcb, wa.astype(BF16)) + ba)
    ig = _sigmoid(_dot(xcb, wx.astype(BF16)) + bx)
    nl = -lam
    sp = jnp.maximum(nl, 0.0) + jnp.log1p(jnp.exp(-jnp.abs(nl)))
    log_a = (-LRU_C * r) * sp
    a_raw = jnp.exp(log_a)
    a = jnp.where(reset, 0.0, a_raw)
    mult = jnp.where(reset, 1.0, jnp.sqrt(1.0 - a_raw * a_raw))
    return xc, r, ig, sp, a, mult


def _log_scan(a, b, axis, up):
    n = a.shape[axis]
    rows = lax.broadcasted_iota(jnp.int32, a.shape, axis)
    d = 1
    while d < n:
        m = rows < n - d if up else rows >= d
        shift = n - d if up else d
        a_s = pltpu.roll(a, shift, axis)
        b_s = pltpu.roll(b, shift, axis)
        b = jnp.where(m, a * b_s + b, b)
        a = jnp.where(m, a * a_s, a)
        d *= 2
    return a, b


def _scan(a, b, t, edge, up=False):
    g = t // 8
    a3, b3 = _log_scan(a.reshape(g, 8, DH), b.reshape(g, 8, DH), 1, up)
    last = 0 if up else 7
    ag, bg = _log_scan(a3[:, last, :], b3[:, last, :], 0, up)
    hg = ag * edge + bg
    grp = lax.broadcasted_iota(jnp.int32, hg.shape, 0)
    if up:
        cin = jnp.where(grp == g - 1, edge, pltpu.roll(hg, g - 1, 0))
        tail = hg[0:1]
    else:
        cin = jnp.where(grp == 0, edge, pltpu.roll(hg, 1, 0))
        tail = hg[g - 1:g]
    return (a3 * cin[:, None, :] + b3).reshape(t, DH), tail


def _rnn_fwd(p, pos, conv_w, conv_b, w_a, b_a, w_x, b_x, lam, tt=1024):
    s = p.shape[1]
    nt = s // tt

    def body(xr_ref, z_ref, pos_ref, cw_ref, cb_ref, wa_ref, ba_ref, wx_ref, bx_ref, lam_ref,
             hr_ref, gr_ref, xprev, hprev):
        @pl.when(pl.program_id(1) == 0)
        def _():
            xprev[...] = jnp.zeros_like(xprev)
            hprev[...] = jnp.zeros_like(hprev)

        xr = xr_ref[0].astype(F32)
        z = z_ref[0].astype(F32)
        reset = pos_ref[...] > 0.5
        xc, r, ig, sp, a, mult = _rnn_gates(xr, xprev[...], cw_ref[...], cb_ref[...], wa_ref[0], ba_ref[0],
                                            wx_ref[0], bx_ref[0], lam_ref[...], reset)
        bx = mult * ig * xc
        h, h_last = _scan(a, bx, tt, hprev[0:1])
        xprev[...] = xr[tt - 8:]
        hprev[...] = jnp.broadcast_to(h_last, (8, DH))
        hr_ref[...] = h
        gr_ref[...] = (h * (z * _sigmoid(z))).astype(BF16)

    head_row = lambda hh, t: (0, hh)
    return pl.pallas_call(
        body, name="rnn_fwd", grid=(H, nt),
        in_specs=[pl.BlockSpec((1, tt, DH), lambda hh, t: (0, t, hh)),
                  pl.BlockSpec((1, tt, DH), lambda hh, t: (0, t, H + hh)),
                  pl.BlockSpec((tt, DH), lambda hh, t: (t, 0)),
                  pl.BlockSpec((4, DH), head_row), pl.BlockSpec((1, DH), head_row),
                  pl.BlockSpec((1, DH, DH), lambda hh, t: (hh, 0, 0)), pl.BlockSpec((1, 1, DH), lambda hh, t: (hh, 0, 0)),
                  pl.BlockSpec((1, DH, DH), lambda hh, t: (hh, 0, 0)), pl.BlockSpec((1, 1, DH), lambda hh, t: (hh, 0, 0)),
                  pl.BlockSpec((1, DH), head_row)],
        out_specs=(pl.BlockSpec((tt, DH), lambda hh, t: (t, hh)), pl.BlockSpec((tt, DH), lambda hh, t: (t, hh))),
        out_shape=(jax.ShapeDtypeStruct((s, D), F32), jax.ShapeDtypeStruct((s, D), BF16)),
        scratch_shapes=[pltpu.VMEM((8, DH), F32), pltpu.VMEM((8, DH), F32)],
        compiler_params=_params(("parallel", "arbitrary")),
    )(p, p, pos, conv_w, conv_b, w_a, b_a, w_x, b_x, lam)


def _rnn_bwd(p, hr, dgr, pos, conv_w, conv_b, w_a, b_a, w_x, b_x, lam, tt=1024):
    s = p.shape[1]
    nt = s // tt
    t8 = tt // 8

    def body(xr_ref, z_ref, xp_ref, hr_ref, hp_ref, dg_ref, pos_ref, cw_ref, cb_ref, wa_ref, ba_ref, wx_ref, bx_ref,
             lam_ref, dxr_ref, dz_ref, gwa_ref, gba_ref, gwx_ref, gbx_ref, glam_ref, gcw_ref, gcb_ref,
             a_next, g_next, dxc_next):
        t = pl.program_id(1)
        has_prev = t < nt - 1

        @pl.when(t == 0)
        def _():
            a_next[...] = jnp.zeros_like(a_next)
            g_next[...] = jnp.zeros_like(g_next)
            dxc_next[...] = jnp.zeros_like(dxc_next)
            gwa_ref[...] = jnp.zeros_like(gwa_ref)
            gba_ref[...] = jnp.zeros_like(gba_ref)
            gwx_ref[...] = jnp.zeros_like(gwx_ref)
            gbx_ref[...] = jnp.zeros_like(gbx_ref)
            glam_ref[...] = jnp.zeros_like(glam_ref)
            gcw_ref[...] = jnp.zeros_like(gcw_ref)
            gcb_ref[...] = jnp.zeros_like(gcb_ref)

        xr = xr_ref[0].astype(F32)
        z = z_ref[0].astype(F32)
        hr_blk = hr_ref[...]
        dg = dg_ref[...]
        xprev = jnp.where(has_prev, xp_ref[0].astype(F32)[8:], 0.0)
        hprev8 = jnp.where(has_prev, hp_ref[...], 0.0)
        reset = pos_ref[...] > 0.5
        cw = cw_ref[...]
        wa = wa_ref[0]
        wx = wx_ref[0]
        lam_v = lam_ref[...]
        xc, r, ig, sp, a, mult = _rnn_gates(xr, xprev, cw, cb_ref[...], wa, ba_ref[0], wx, bx_ref[0], lam_v, reset)

        sz = _sigmoid(z)
        dh = dg * (z * sz)
        dz_ref[...] = (dg * hr_blk * (sz * (1.0 + z * (1.0 - sz)))).astype(BF16)

        an = _shift_up(a, a_next[...], 1)
        g, g_first = _scan(an, dh, tt, g_next[0:1], up=True)
        a_next[...] = jnp.broadcast_to(a[0:1], (8, DH))
        g_next[...] = jnp.broadcast_to(g_first, (8, DH))

        hm1 = _shift_down(hprev8, hr_blk, 1)
        da = g * hm1
        dmult = g * (ig * xc)
        di = g * (mult * xc)
        dxc = g * (mult * ig)
        dla = jnp.where(reset, 0.0, da * a - dmult * (a * a) / mult)
        dr = dla * (-LRU_C * sp)
        dsp = _colsum(dla * (-LRU_C * r))
        glam_ref[0] += dsp * (-_sigmoid(-lam_v))
        dpa = dr * r * (1.0 - r)
        dpx = di * ig * (1.0 - ig)
        dpab = dpa.astype(BF16)
        dpxb = dpx.astype(BF16)
        dxc = dxc + _dot_nt(dpab, wa.astype(BF16)) + _dot_nt(dpxb, wx.astype(BF16))
        xcb = xc.astype(BF16)
        gwa_ref[0] += _dot_tn(xcb, dpab)
        gwx_ref[0] += _dot_tn(xcb, dpxb)
        gba_ref[0] += _colsum(dpa)
        gbx_ref[0] += _colsum(dpx)

        dxr = cw[3:4] * dxc
        for d in (1, 2, 3):
            dxr = dxr + cw[3 - d:4 - d] * _shift_up(dxc, dxc_next[...], d)
        dxr_ref[...] = dxr.astype(BF16)
        dxc_next[...] = dxc[0:8]
        gcb_ref[0] += _colsum(dxc)
        gcw_ref[0, 3:4, :] += _colsum(xr * dxc)
        for d in (1, 2, 3):
            gcw_ref[0, 3 - d:4 - d, :] += _colsum(_shift_down(xprev, xr, d) * dxc)

    rt = lambda t: nt - 1 - t
    prev8 = lambda t: jnp.maximum(rt(t) * t8 - 1, 0)
    head_row = lambda hh, t: (0, hh)
    hsm = lambda hh, t: (hh, 0, 0)
    return pl.pallas_call(
        body, name="rnn_bwd", grid=(H, nt),
        in_specs=[pl.BlockSpec((1, tt, DH), lambda hh, t: (0, rt(t), hh)),
                  pl.BlockSpec((1, tt, DH), lambda hh, t: (0, rt(t), H + hh)),
                  pl.BlockSpec((1, 16, DH), lambda hh, t: (0, jnp.maximum(rt(t) * (tt // 16) - 1, 0), hh)),
                  pl.BlockSpec((tt, DH), lambda hh, t: (rt(t), hh)),
                  pl.BlockSpec((8, DH), lambda hh, t: (prev8(t), hh)),
                  pl.BlockSpec((tt, DH), lambda hh, t: (rt(t), hh)),
                  pl.BlockSpec((tt, DH), lambda hh, t: (rt(t), 0)),
                  pl.BlockSpec((4, DH), head_row), pl.BlockSpec((1, DH), head_row),
                  pl.BlockSpec((1, DH, DH), hsm), pl.BlockSpec((1, 1, DH), hsm),
                  pl.BlockSpec((1, DH, DH), hsm), pl.BlockSpec((1, 1, DH), hsm),
                  pl.BlockSpec((1, DH), head_row)],
        out_specs=(pl.BlockSpec((tt, DH), lambda hh, t: (rt(t), hh)), pl.BlockSpec((tt, DH), lambda hh, t: (rt(t), hh)),
                   pl.BlockSpec((1, DH, DH), hsm), pl.BlockSpec((1, 1, DH), hsm),
                   pl.BlockSpec((1, DH, DH), hsm), pl.BlockSpec((1, 1, DH), hsm),
                   pl.BlockSpec((1, 1, DH), hsm), pl.BlockSpec((1, 4, DH), hsm), pl.BlockSpec((1, 1, DH), hsm)),
        out_shape=(jax.ShapeDtypeStruct((s, D), BF16), jax.ShapeDtypeStruct((s, D), BF16),
                   jax.ShapeDtypeStruct((H, DH, DH), F32), jax.ShapeDtypeStruct((H, 1, DH), F32),
                   jax.ShapeDtypeStruct((H, DH, DH), F32), jax.ShapeDtypeStruct((H, 1, DH), F32),
                   jax.ShapeDtypeStruct((H, 1, DH), F32), jax.ShapeDtypeStruct((H, 4, DH), F32),
                   jax.ShapeDtypeStruct((H, 1, DH), F32)),
        scratch_shapes=[pltpu.VMEM((8, DH), F32), pltpu.VMEM((8, DH), F32), pltpu.VMEM((8, DH), F32)],
        compiler_params=_params(("parallel", "arbitrary")),
    )(p, p, p, hr, hr, dgr, pos, conv_w, conv_b, w_a, b_a, w_x, b_x, lam)


def _rope(t, c, sa, sb):
    return t * c + pltpu.roll(t, DH - ROT // 2, 1) * sa + pltpu.roll(t, ROT // 2, 1) * sb


def _rope_bwd(g, c, sa, sb):
    return g * c + pltpu.roll(g * sa, ROT // 2, 1) + pltpu.roll(g * sb, DH - ROT // 2, 1)


def _unit_bases(gi, u):
    dil = DILATIONS[gi]
    if dil == 1:
        return u * UB, SPAN + (u - 1) * UB, u == 0
    if dil == 4:
        blk, r = u // 4, u % 4
        return blk * 4 * UB + r, SPAN + (blk - 1) * 4 * UB + r, blk == 0
    return u, u, True


def _unit_slices(gi, u):
    dil = DILATIONS[gi]
    qb0, kb0, first = _unit_bases(gi, u)
    if dil == 1:
        if not isinstance(qb0, int):
            qb0, kb0 = pl.multiple_of(qb0, UB), pl.multiple_of(kb0, UB)
        return pl.ds(qb0, UB), pl.ds(kb0, 2 * UB), first
    return pl.ds(qb0, UB, stride=dil), pl.ds(kb0, 2 * UB, stride=dil), first


def _bdot(a, b):
    return lax.dot_general(a, b, (((2,), (1,)), ((0,), (0,))), preferred_element_type=F32)


def _bdot_nt(a, b):
    return lax.dot_general(a, b, (((2,), (2,)), ((0,), (0,))), preferred_element_type=F32)


def _bdot_tn(a, b):
    return lax.dot_general(a, b, (((1,), (1,)), ((0,), (0,))), preferred_element_type=F32)


def _band_mask(first_in_span, has_prev):
    qi = lax.broadcasted_iota(jnp.int32, (UB, 2 * UB), 0)
    ki = lax.broadcasted_iota(jnp.int32, (UB, 2 * UB), 1)
    dist = UB + qi - ki
    band = (dist >= 0) & (dist <= UB)
    return band & ((ki >= UB) | jnp.logical_not(first_in_span) | has_prev)


def _gather_halves(phase, src_hbm, dst_hbm, ss, rs):
    half = src_hbm.shape[0] // 2
    cx, cy, cc = _coords()
    j = 2 * cx + cy
    sib = (cx, cy, 1 - cc)
    mine = pl.ds(cc * half, half)
    theirs = pl.ds((1 - cc) * half, half)
    chips = ((1, 0), (0, 1), (1, 1))
    for q, (kx, ky) in enumerate(chips):
        jq = j ^ (2 * kx + ky)
        out = _rcopy(src_hbm.at[mine], dst_hbm.at[j, mine], ss.at[q], rs.at[q], (_flip(cx, kx), _flip(cy, ky), cc))
        landed = _rcopy(src_hbm.at[mine], dst_hbm.at[jq, mine], ss.at[q], rs.at[q], sib)
        onward = _rcopy(dst_hbm.at[jq, mine], dst_hbm.at[jq, mine], ss.at[3 + q], rs.at[3 + q], sib)
        from_sib = _rcopy(src_hbm.at[mine], dst_hbm.at[jq, theirs], ss.at[3 + q], rs.at[3 + q], sib)
        if phase == 0:
            out.start()
        elif phase == 1:
            landed.wait_recv()
            onward.start()
        else:
            from_sib.wait_recv()
            out.wait_send()
            onward.wait_send()


def _attn_fwd(p, rc, rsa, rsb, w3sh):
    s = p.shape[1]
    ns = s // SPAN
    nunit = SPAN // UB

    def body(q_ref, k_ref, v_ref, z_ref, c_ref, sa_ref, sb_ref, w3_hbm, o_ref, lse_ref, ga_ref, qro_ref, kro_ref,
             w3g_hbm, qr, kf, vf, acc, mm, ll, ss, rs):
        hh, n = pl.program_id(0), pl.program_id(1)
        for phase, at_head, at_span in ((0, 0, 0), (1, H // 2, 0), (2, H - 1, ns - 1)):
            @pl.when((hh == at_head) & (n == at_span))
            def _(phase=phase):
                _gather_halves(phase, w3_hbm, w3g_hbm, ss, rs)

        @pl.when(n == 0)
        def _():
            kf[0:SPAN] = jnp.zeros((SPAN, DH), F32)
            vf[0:SPAN] = jnp.zeros((SPAN, DH), F32)

        c, sa, sb = c_ref[...], sa_ref[...], sb_ref[...]
        q_rot = _rope(q_ref[0].astype(F32), c, sa, sb).astype(BF16)
        k_rot = _rope(k_ref[0].astype(F32), c, sa, sb).astype(BF16)
        qro_ref[...] = q_rot
        kro_ref[...] = k_rot
        qr[...] = q_rot.astype(F32)
        kf[SPAN:] = k_rot.astype(F32)
        vf[SPAN:] = v_ref[0].astype(F32)
        has_prev = n > 0

        for gi, dil in enumerate(DILATIONS):
            def trip(t, carry, gi=gi, dil=dil):
                qsls, ksls, firsts = [], [], []
                for b in range(UNIT_BATCH_FWD):
                    qsl, ksl, first = _unit_slices(gi, t * UNIT_BATCH_FWD + b)
                    qsls.append(qsl)
                    ksls.append(ksl)
                    firsts.append(first)
                qb = jnp.stack([qr[qsl, :].astype(BF16) for qsl in qsls])
                kb = jnp.stack([kf[ksl, :].astype(BF16) for ksl in ksls])
                vb = jnp.stack([vf[ksl, :].astype(BF16) for ksl in ksls])
                s_all = _bdot_nt(qb, kb)
                prs = []
                for b in range(UNIT_BATCH_FWD):
                    sc = jnp.where(_band_mask(firsts[b], has_prev), s_all[b] * SCALE, NEG)
                    m = jnp.max(sc, axis=-1, keepdims=True)
                    pr = jnp.exp(sc - m)
                    l = jnp.sum(pr, axis=-1, keepdims=True)
                    mm[gi, qsls[b], :] = jnp.broadcast_to(m, (UB, DH))
                    ll[gi, qsls[b], :] = jnp.broadcast_to(l, (UB, DH))
                    prs.append(pr.astype(BF16))
                o_all = _bdot(jnp.stack(prs), vb)
                for b in range(UNIT_BATCH_FWD):
                    acc[gi, qsls[b], :] = o_all[b]
                return carry

            lax.fori_loop(0, nunit // UNIT_BATCH_FWD, trip, 0)

        m_all =jnp.maximum(jnp.maximum(mm[0], mm[1]), mm[2])
        num = jnp.zeros((SPAN, DH), F32)
        den = jnp.zeros((SPAN, DH), F32)
        for gi in range(3):
            w = jnp.exp(mm[gi] - m_all)
            num = num + w * acc[gi]
            den = den + w * ll[gi]
        o = num / den
        o_ref[...] = o
        lse_ref[...] = m_all + jnp.log(den)
        z = z_ref[0].astype(F32)
        ga_ref[...] = (o * (z * _sigmoid(z))).astype(BF16)
        kf[0:SPAN] = kf[SPAN:]
        vf[0:SPAN] = vf[SPAN:]

    blk = lambda piece, off: pl.BlockSpec((1, SPAN, DH), lambda hh, n: (piece, n, off + hh))
    tab = pl.BlockSpec((SPAN, DH), lambda hh, n: (n, 0))
    outb = pl.BlockSpec((SPAN, DH), lambda hh, n: (n, hh))
    any_ = pl.BlockSpec(memory_space=pl.ANY)
    return pl.pallas_call(
        body, name="attn_fwd", grid=(H, ns),
        in_specs=[blk(1, 0), blk(1, H), blk(2, 0), blk(2, H), tab, tab, tab, any_],
        out_specs=(outb, outb, outb, outb, outb, any_),
        out_shape=(jax.ShapeDtypeStruct((s, D), F32), jax.ShapeDtypeStruct((s, D), F32),
                   jax.ShapeDtypeStruct((s, D), BF16), jax.ShapeDtypeStruct((s, D), BF16),
                   jax.ShapeDtypeStruct((s, D), BF16), jax.ShapeDtypeStruct((4,) + w3sh.shape, w3sh.dtype)),
        scratch_shapes=[pltpu.VMEM((SPAN, DH), F32), pltpu.VMEM((2 * SPAN, DH), F32), pltpu.VMEM((2 * SPAN, DH), F32),
                        pltpu.VMEM((3, SPAN, DH), F32), pltpu.VMEM((3, SPAN, DH), F32), pltpu.VMEM((3, SPAN, DH), F32),
                        pltpu.SemaphoreType.DMA((6,)), pltpu.SemaphoreType.DMA((6,))],
        compiler_params=_params(("arbitrary", "arbitrary")),
    )(p, p, p, p, rc, rsa, rsb, w3sh)


def _attn_bwd(p, q_rot, k_rot, o, lse, dga, rc, rsa, rsb, packs):
    s = p.shape[1]
    ns = s // SPAN
    nunit = SPAN // UB
    npk = len(packs)

    def body(*refs):
        (q_ref, k_ref, kp_ref, v_ref, vp_ref, z_ref, c_ref, sa_ref, sb_ref, o_ref, lse_ref, dg_ref) = refs[:12]
        pk_refs = refs[12:12 + npk]
        dq_ref, dk_ref, dv_ref, dz_ref = refs[12 + npk:16 + npk]
        red_refs = refs[16 + npk:16 + 2 * npk]
        qr, kf, vf, dof, dlt, dqa, dkf, dvf = refs[16 + 2 * npk:24 + 2 * npk]
        rbufs = refs[24 + 2 * npk:24 + 3 * npk]
        sums = refs[24 + 3 * npk:24 + 4 * npk]
        ar_sems = refs[24 + 4 * npk:]
        hh, step = pl.program_id(0), pl.program_id(1)
        n = ns - 1 - step
        has_prev = n > 0
        for phase, at_head, at_step in ((0, 0, 0), (1, H // 2, 0), (2, H - 1, ns - 1)):
            @pl.when((hh == at_head) & (step == at_step))
            def _(phase=phase):
                _allreduce_phase(phase, pk_refs, sums, rbufs, *ar_sems, out_refs=red_refs)

        @pl.when(step == 0)
        def _():
            dkf[...] = jnp.zeros_like(dkf)
            dvf[...] = jnp.zeros_like(dvf)

        @pl.when(step > 0)
        def _():
            dkf[SPAN:] = dkf[0:SPAN]
            dvf[SPAN:] = dvf[0:SPAN]
            dkf[0:SPAN] = jnp.zeros((SPAN, DH), F32)
            dvf[0:SPAN] = jnp.zeros((SPAN, DH), F32)

        c, sa, sb = c_ref[...], sa_ref[...], sb_ref[...]
        qr[...] = q_ref[...].astype(F32)
        kf[SPAN:] = k_ref[...].astype(F32)
        vf[SPAN:] = v_ref[0].astype(F32)
        kf[0:SPAN] = jnp.where(has_prev, kp_ref[...].astype(F32), 0.0)
        vf[0:SPAN] = jnp.where(has_prev, vp_ref[0].astype(F32), 0.0)
        z = z_ref[0].astype(F32)
        sz = _sigmoid(z)
        dg = dg_ref[...]
        ov = o_ref[...]
        do = dg * (z * sz)
        dz_ref[...] = (dg * ov * (sz * (1.0 + z * (1.0 - sz)))).astype(BF16)
        dof[...] = do
        dlt[...] = jnp.dot(do * ov, jnp.ones((DH, DH), F32), preferred_element_type=F32, precision=HIGHEST)
        dqa[...] = jnp.zeros_like(dqa)

        for gi, dil in enumerate(DILATIONS):
            def trip(t, carry, gi=gi, dil=dil):
                qsls, ksls, firsts = [], [], []
                for b in range(UNIT_BATCH):
                    qsl, ksl, first = _unit_slices(gi, t * UNIT_BATCH + b)
                    qsls.append(qsl)
                    ksls.append(ksl)
                    firsts.append(first)
                qb = jnp.stack([qr[qsl, :].astype(BF16) for qsl in qsls])
                kb = jnp.stack([kf[ksl, :].astype(BF16) for ksl in ksls])
                vb = jnp.stack([vf[ksl, :].astype(BF16) for ksl in ksls])
                dob = jnp.stack([dof[qsl, :].astype(BF16) for qsl in qsls])
                s_all = _bdot_nt(qb, kb)
                dp_all = _bdot_nt(dob, vb)
                prs, dss = [], []
                for b in range(UNIT_BATCH):
                    lse_b = lse_ref[qsls[b], :]
                    dl_b = dlt[qsls[b], :]
                    pr = jnp.exp(s_all[b] * SCALE - jnp.concatenate([lse_b, lse_b], axis=1))
                    pr = jnp.where(_band_mask(firsts[b], has_prev), pr, 0.0)
                    prs.append(pr.astype(BF16))
                    dss.append((pr * (dp_all[b] - jnp.concatenate([dl_b, dl_b], axis=1)) * SCALE).astype(BF16))
                ds_all = jnp.stack(dss)
                dv_all = _bdot_tn(jnp.stack(prs), dob)
                dq_all = _bdot(ds_all, kb)
                dk_all = _bdot_tn(ds_all, qb)
                for b in range(UNIT_BATCH):
                    dvf[ksls[b], :] += dv_all[b]
                    dqa[qsls[b], :] += dq_all[b]
                    dkf[ksls[b], :] += dk_all[b]
                return carry

            lax.fori_loop(0, nunit // UNIT_BATCH, trip, 0)

        dq_ref[...] = _rope_bwd(dqa[...], c, sa, sb).astype(BF16)
        dk_ref[...] = _rope_bwd(dkf[SPAN:], c, sa, sb).astype(BF16)
        dv_ref[...] = dvf[SPAN:].astype(BF16)

    rn = lambda n: ns - 1 - n
    pn = lambda n: jnp.maximum(ns - 2 - n, 0)
    blk = lambda piece, off: pl.BlockSpec((1, SPAN, DH), lambda hh, n: (piece, rn(n), off + hh))
    blkp = lambda piece, off: pl.BlockSpec((1, SPAN, DH), lambda hh, n: (piece, pn(n), off + hh))
    tab = pl.BlockSpec((SPAN, DH), lambda hh, n: (rn(n), 0))
    io = pl.BlockSpec((SPAN, DH), lambda hh, n: (rn(n), hh))
    iop = pl.BlockSpec((SPAN, DH), lambda hh, n: (pn(n), hh))
    vm = pl.BlockSpec(memory_space=pltpu.VMEM)
    outs = pl.pallas_call(
        body, name="attn_bwd", grid=(H, ns),
        in_specs=[io, io, iop, blk(2, 0), blkp(2, 0), blk(2, H), tab, tab, tab, io, io, io] + [vm] * npk,
        out_specs=(io, io, io, io) + (vm,) * npk,
        out_shape=tuple(jax.ShapeDtypeStruct((s, D), BF16) for _ in range(4)) +
                  tuple(jax.ShapeDtypeStruct(a.shape, F32) for a in packs),
        scratch_shapes=[pltpu.VMEM((SPAN, DH), F32), pltpu.VMEM((2 * SPAN, DH), F32), pltpu.VMEM((2 * SPAN, DH), F32),
                        pltpu.VMEM((SPAN, DH), F32), pltpu.VMEM((SPAN, DH), F32), pltpu.VMEM((SPAN, DH), F32),
                        pltpu.VMEM((2 * SPAN, DH), F32), pltpu.VMEM((2 * SPAN, DH), F32)] +
                       [pltpu.VMEM((NDEV, a.shape[0] // NDEV, a.shape[1]), F32) for a in packs] +
                       [pltpu.VMEM(a.shape, F32) for a in packs] +
                       [pltpu.SemaphoreType.DMA((7 * npk,)) for _ in range(4)],
        compiler_params=_params(("arbitrary", "arbitrary")),
    )(q_rot, k_rot, k_rot, p, p, p, rc, rsa, rsb, o, lse, dga, *packs)
    return outs[:4], outs[4:]


def _tail(gr, ga, p, x, tgt, w3, b_gate, gate, g_final, tm=256):
    s = x.shape[0]
    nt = s // tm

    def body(gr_ref, ga_ref, pr_ref, pa_ref, x_ref, t_ref, bg_ref, gate_ref, gf_ref, w_hbm,
             dgr_ref, dga_ref, dc_ref, dx2_ref, vec_ref, go_hbm, w_s, acc_s, sem):
        i = pl.program_id(0)

        @pl.when(i == 0)
        def _():
            cp = pltpu.make_async_copy(w_hbm, w_s, sem.at[12])
            cp.start()
            acc_s[...] = jnp.zeros_like(acc_s)
            vec_ref[...] = jnp.zeros_like(vec_ref)
            cp.wait()

        grb = gr_ref[...]
        gab = ga_ref[...]
        bg = bg_ref[...]
        gate_v = gate_ref[...]
        gf = gf_ref[...]
        y_r = _dot(grb, w_s[0])
        y_a = _dot(gab, w_s[1])
        sr = _sigmoid(pr_ref[0].astype(F32) + bg[:, :D])
        sa = _sigmoid(pa_ref[0].astype(F32) + bg[:, D:])
        mb = (sr * y_r + sa * y_a).astype(BF16)
        u = _dot(mb, w_s[2])
        x2 = x_ref[...] + gate_v * u
        rstd = lax.rsqrt(jnp.mean(x2 * x2, axis=-1, keepdims=True) + EPS)
        xh = x2 * rstd
        e = xh * gf - t_ref[...]
        dy = e * (1.0 / D)
        dyg = dy * gf
        dx2 = rstd * (dyg - xh * jnp.mean(dyg * xh, axis=-1, keepdims=True))
        dx2_ref[...] = dx2
        dub = (dx2 * gate_v).astype(BF16)
        dm = _dot_nt(dub, w_s[2])
        dyr = (dm * sr).astype(BF16)
        dya = (dm * sa).astype(BF16)
        dpr = dm * y_r * (sr * (1.0 - sr))
        dpa = dm * y_a * (sa * (1.0 - sa))
        dc_ref[:, :D] = dpr.astype(BF16)
        dc_ref[:, D:] = dpa.astype(BF16)
        dgr_ref[...] = _dot_nt(dyr, w_s[0])
        dga_ref[...] = _dot_nt(dya, w_s[1])
        acc_s[0] += _dot_tn(grb, dyr)
        acc_s[1] += _dot_tn(gab, dya)
        acc_s[2] += _dot_tn(mb, dub)
        vec_ref[0:1, :] += _colsum(dy * xh)
        vec_ref[1:2, :] += _colsum(dx2 * u)
        vec_ref[2:3, :] += _colsum(dpr)
        vec_ref[3:4, :] += _colsum(dpa)
        vec_ref[4:5, :] += _colsum(e * e)

        @pl.when(i == nt - 1)
        def _():
            vec_ref[4:5, :] = jnp.broadcast_to(jnp.sum(vec_ref[4:5, :]) * (0.5 / D), (1, D))
            cps = []
            for w in range(3):
                for j in range(4):
                    cps.append(pltpu.make_async_copy(acc_s.at[w, pl.ds(256 * j, 256)],
                                                     go_hbm.at[j, pl.ds(256 * w, 256)], sem.at[4 * w + j]))
            for cp in cps:
                cp.start()
            for cp in cps:
                cp.wait()

    rowt = lambda i: (i, 0)
    row = lambda w: pl.BlockSpec((1, w), lambda i: (0, 0))
    any_ = pl.BlockSpec(memory_space=pl.ANY)
    return pl.pallas_call(
        body, name="tail", grid=(nt,),
        in_specs=[pl.BlockSpec((tm, D), rowt), pl.BlockSpec((tm, D), rowt),
                  pl.BlockSpec((1, tm, D), lambda i: (3, i, 0)), pl.BlockSpec((1, tm, D), lambda i: (3, i, 1)),
                  pl.BlockSpec((tm, D), rowt), pl.BlockSpec((tm, D), rowt),
                  row(2 * D), row(D), row(D), any_],
        out_specs=(pl.BlockSpec((tm, D), rowt), pl.BlockSpec((tm, D), rowt), pl.BlockSpec((tm, 2 * D), rowt),
                   pl.BlockSpec((tm, D), rowt), pl.BlockSpec((8, D), lambda i: (0, 0)), any_),
        out_shape=(jax.ShapeDtypeStruct((s, D), F32), jax.ShapeDtypeStruct((s, D), F32),
                   jax.ShapeDtypeStruct((s, 2 * D), BF16), jax.ShapeDtypeStruct((s, D), F32),
                   jax.ShapeDtypeStruct((8, D), F32), jax.ShapeDtypeStruct((4, 768, D), F32)),
        scratch_shapes=[pltpu.VMEM((3, D, D), BF16), pltpu.VMEM((3, D, D), F32), pltpu.SemaphoreType.DMA((13,))],
        compiler_params=_params(("arbitrary",)),
    )(gr, ga, p, p, x, tgt, b_gate, gate, g_final, w3)


def _pieces_steps(pieces):
    out, s0 = [], 0
    for a in pieces:
        n = a.shape[1] // D
        out.append((s0, n))
        s0 += n
    return out, s0


def _inproj_bwd_x(pieces, wg, wsh, x, dx2, gn, scale, sums, tm=512):
    s = x.shape[0]
    np_ = len(pieces)
    na = len(sums)
    ni = s // tm
    groups, cur, width = [], [], 0
    for t, a in enumerate(pieces):
        cur.append(t)
        width += a.shape[1]
        if width == PW:
            groups.append(cur)
            cur, width = [], 0
    assert len(groups) == 4 and not cur

    def body(*refs):
        d_refs = refs[:np_]
        w_hbm, wsh_hbm, x_ref, dx2_ref, gn_ref, sc_ref = refs[np_:np_ + 6]
        q_refs = refs[np_ + 6:np_ + 6 + na]
        gx_ref, vec_ref = refs[np_ + 6 + na:np_ + 8 + na]
        r_refs = refs[np_ + 8 + na:np_ + 8 + 2 * na]
        w_s, wsem, ss, rs = refs[np_ + 8 + 2 * na:]
        i = pl.program_id(0)
        my_shard = 2 * lax.axis_index("x") + lax.axis_index("y")

        def scatter_copies():
            cx, cy, cc = _coords()
            j = 2 * cx + cy
            cps = []
            for t, (q, r) in enumerate(zip(q_refs, r_refs)):
                for e, (kx, ky) in enumerate(((1, 0), (0, 1), (1, 1))):
                    cps.append(_rcopy(q.at[j ^ (2 * kx + ky)], r.at[e], ss.at[3 * t + e], rs.at[3 * t + e],
                                      (_flip(cx, kx), _flip(cy, ky), cc)))
            return cps

        def w_copy(pc):
            return pltpu.make_async_copy(w_hbm.at[pc, pl.ds(0, D), :], w_s.at[pc], wsem.at[pc])

        for pc in range(4):
            @pl.when((i == 0) & (my_shard != pc))
            def _(pc=pc):
                w_copy(pc).start()

            @pl.when((i == 0) & (my_shard == pc))
            def _(pc=pc):
                pltpu.make_async_copy(wsh_hbm, w_s.at[pc], wsem.at[pc]).start()

        @pl.when(i == 0)
        def _():
            vec_ref[...] = jnp.zeros_like(vec_ref)
            for cp in scatter_copies():
                cp.start()

        dh = None
        for pc, group in enumerate(groups):
            @pl.when(i == 0)
            def _(pc=pc):
                w_copy(pc).wait()

            tiles = [d_refs[t][...] for t in group]
            lhs = tiles[0] if len(tiles) == 1 else jnp.concatenate(tiles, axis=1)
            part = _dot_nt(lhs, w_s[pc])
            dh = part if dh is None else dh + part

        xt = x_ref[...]
        rstd = lax.rsqrt(jnp.mean(xt * xt, axis=-1, keepdims=True) + EPS)
        xh = xt * rstd
        gn_v = gn_ref[...]
        sc1 = 1.0 + sc_ref[...]
        dhx = dh * xh
        vec_ref[0:1, :] += _colsum(dh)
        vec_ref[1:2, :] += _colsum(dhx) * gn_v
        vec_ref[2:3, :] += _colsum(dhx) * sc1
        dxh = dh * (gn_v * sc1)
        gx_ref[...] = rstd * (dxh - xh * jnp.mean(dxh * xh, axis=-1, keepdims=True)) + dx2_ref[...]

        @pl.when(i == ni - 1)
        def _():
            for cp in scatter_copies():
                cp.wait()

    rowt = lambda i: (i, 0)
    row = pl.BlockSpec((1, D), lambda i: (0, 0))
    any_ = pl.BlockSpec(memory_space=pl.ANY)
    outs = pl.pallas_call(
        body, name="inproj_bwd_x", grid=(ni,),
        in_specs=[pl.BlockSpec((tm, a.shape[1]), rowt) for a in pieces] +
                 [any_, any_, pl.BlockSpec((tm, D), rowt), pl.BlockSpec((tm, D), rowt), row, row] + [any_] * na,
        out_specs=(pl.BlockSpec((tm, D), rowt), pl.BlockSpec((8, D), lambda i: (0, 0))) + (any_,) * na,
        out_shape=(jax.ShapeDtypeStruct((s, D), F32), jax.ShapeDtypeStruct((8, D), F32)) +
                  tuple(jax.ShapeDtypeStruct((3,) + q.shape[1:], q.dtype) for q in sums),
        scratch_shapes=[pltpu.VMEM((4, D, PW), BF16), pltpu.SemaphoreType.DMA((4,)),
                        pltpu.SemaphoreType.DMA((3 * na,)), pltpu.SemaphoreType.DMA((3 * na,))],
        compiler_params=_params(("arbitrary",)),
    )(*pieces, wg, wsh, x, dx2, gn, scale, *sums)
    return outs[0], outs[1], outs[2:]


def _inproj_bwd_w(pieces, hbf, g_out, tk=1024):
    s = hbf.shape[0]
    steps, nk = _pieces_steps(pieces)
    npc = PW // D
    ns = s // tk
    np_ = len(pieces)
    hr = D // 2
    ohr = g_out.shape[1] // 2
    ocr = _chunk_rows(g_out)
    ochunks = [(j, r0) for j in range(g_out.shape[0]) for r0 in range(0, ohr, ocr)]
    noc = len(ochunks)

    def body(*refs):
        d_refs = refs[:np_]
        h_ref, go_hbm, g_ref, rb_hbm, rbo_hbm, stage, ss, rs, oss, ors = refs[np_:]
        cb, k = pl.program_id(0), pl.program_id(1)
        cx, cy, cc = _coords()
        sib = (cx, cy, 1 - cc)

        def block_copy(b):
            return _rcopy(stage.at[b % 2],
                          rb_hbm.at[b // npc, :, pl.ds(pl.multiple_of((b % npc) * D, D), D)], ss.at[b], rs.at[b], sib)

        def out_copy(e):
            j, r0 = ochunks[e]
            return _rcopy(go_hbm.at[j, pl.ds((1 - cc) * ohr + r0, ocr), :], rbo_hbm.at[j, pl.ds(r0, ocr), :],
                          oss.at[e], ors.at[e], sib)

        @pl.when((cb == 0) & (k == 0))
        def _():
            for e in range(noc):
                out_copy(e).start()

        @pl.when(k == 0)
        def _():
            g_ref[...] = jnp.zeros_like(g_ref)

        for (s0, n), d_ref in zip(steps, d_refs):
            @pl.when((cb >= s0) & (cb < s0 + n))
            def _(d_ref=d_ref):
                g_ref[0] += _dot_tn(h_ref[...], d_ref[...])

        @pl.when((k == ns - 1) & (cb > 1))
        def _():
            block_copy(cb - 2).wait_send()

        @pl.when(k == ns - 1)
        def _():
            stage[cb % 2] = g_ref[0, pl.ds(pl.multiple_of((1 - cc) * hr, hr), hr), :]
            block_copy(cb).start()

        @pl.when((k == ns - 1) & (cb == nk - 1))
        def _():
            block_copy(nk - 2).wait_send()
            block_copy(nk - 1).wait_send()
            for b in range(nk):
                block_copy(b).wait_recv()
            for e in range(noc):
                out_copy(e).wait_recv()
                out_copy(e).wait_send()

    def piece_spec(s0, n):
        def imap(cb, k):
            active = (cb >= s0) & (cb < s0 + n)
            return (jnp.where(active, k, 0), jnp.clip(cb - s0, 0, n - 1))
        return pl.BlockSpec((tk, D), imap)

    any_ = pl.BlockSpec(memory_space=pl.ANY)
    return pl.pallas_call(
        body, name="inproj_bwd_w", grid=(nk, ns),
        in_specs=[piece_spec(s0, n) for s0, n in steps] + [pl.BlockSpec((tk, D), lambda cb, k: (k, 0)), any_],
        out_specs=(pl.BlockSpec((1, D, D), lambda cb, k: (cb // npc, 0, cb % npc)), any_, any_),
        out_shape=(jax.ShapeDtypeStruct((4, D, PW), F32), jax.ShapeDtypeStruct((4, hr, PW), F32),
                   jax.ShapeDtypeStruct((g_out.shape[0], ohr, g_out.shape[2]), F32)),
        scratch_shapes=[pltpu.VMEM((2, hr, D), F32), pltpu.SemaphoreType.DMA((nk,)), pltpu.SemaphoreType.DMA((nk,)),
                        pltpu.SemaphoreType.DMA((noc,)), pltpu.SemaphoreType.DMA((noc,))],
        compiler_params=_params(("arbitrary", "arbitrary")),
    )(*pieces, hbf, g_out)


D2D_CHUNK_BYTES = 512 * 1024


def _chunk_rows(a):
    return max(8, D2D_CHUNK_BYTES // (a.shape[-1] * a.dtype.itemsize))


def _pair_swap(arrs):
    na = len(arrs)
    chunks = []
    for t, a in enumerate(arrs):
        cr = _chunk_rows(a)
        chunks += [(t, r0, cr) for r0 in range(0, a.shape[0], cr)]
    nch = len(chunks)

    def body(*refs):
        a_refs = refs[:na]
        o_refs = refs[na:2 * na]
        ss, rs = refs[2 * na:]
        x, y, c = _coords()
        sib = (x, y, 1 - c)
        rcs = []
        for n, (t, r0, cr) in enumerate(chunks):
            rows = pl.ds(r0, cr)
            rc = _rcopy(a_refs[t].at[rows, :], o_refs[t].at[rows, :], ss.at[n], rs.at[n], sib)
            rc.start()
            rcs.append(rc)
        for rc in rcs:
            rc.wait_recv()
        for rc in rcs:
            rc.wait_send()

    any_ = pl.BlockSpec(memory_space=pl.ANY)
    return pl.pallas_call(
        body, name="pair_swap",
        out_shape=tuple(jax.ShapeDtypeStruct(a.shape, a.dtype) for a in arrs),
        in_specs=[any_] * na, out_specs=tuple([any_] * na),
        scratch_shapes=[pltpu.SemaphoreType.DMA((nch,)), pltpu.SemaphoreType.DMA((nch,))],
        compiler_params=_params(),
    )(*arrs)


def _add_half(full, rb, core, tr):
    n, r, cdim = full.shape
    nb = r // 2 // tr

    def body(c_ref, a_ref, b_ref, ob_ref):
        ob_ref[...] = (a_ref[...] + b_ref[...]).astype(BF16)

    mine = pl.BlockSpec((1, tr, cdim), lambda i, j, c_ref: (i, c_ref[0] * nb + j, 0))
    spec = pl.BlockSpec((1, tr, cdim), lambda i, j, c_ref: (i, j, 0))
    return pl.pallas_call(
        body, name="add_half",
        grid_spec=pltpu.PrefetchScalarGridSpec(num_scalar_prefetch=1, grid=(n, nb), in_specs=[mine, spec],
                                               out_specs=spec),
        out_shape=jax.ShapeDtypeStruct(rb.shape, BF16),
        compiler_params=_params(("parallel", "parallel")),
    )(core, full, rb)


def _sum_slots(full, rb, r3, shard_core, tr):
    _, hr, cdim = rb.shape
    nb = hr // tr

    def body(jc_ref, a_ref, b_ref, r_ref, o_ref):
        own = a_ref[0] + b_ref[0]
        o_ref[...] = ((own + r_ref[0].astype(F32)) + r_ref[1].astype(F32)) + r_ref[2].astype(F32)

    return pl.pallas_call(
        body, name="sum_slots",
        grid_spec=pltpu.PrefetchScalarGridSpec(
            num_scalar_prefetch=1, grid=(nb,),
            in_specs=[pl.BlockSpec((1, tr, cdim), lambda i, jc: (jc[0], jc[1] * nb + i, 0)),
                      pl.BlockSpec((1, tr, cdim), lambda i, jc: (jc[0], i, 0)),
                      pl.BlockSpec((3, tr, cdim), lambda i, jc: (0, i, 0))],
            out_specs=pl.BlockSpec((tr, cdim), lambda i, jc: (i, 0))),
        out_shape=jax.ShapeDtypeStruct((hr, cdim), F32),
        compiler_params=_params(("parallel",)),
    )(shard_core, full, rb, r3)


def _allreduce_phase(phase, p_refs, o_refs, rbufs, s1, r1, s2, r2, out_refs=None):
    me = _my_index()

    def chunk(t, d):
        ch = p_refs[t].shape[0] // NDEV
        return pl.ds(pl.multiple_of(d * ch, 8), ch)

    def scatter(t, k):
        e = 7 * t + k - 1
        return _rcopy(p_refs[t].at[chunk(t, me ^ k)], rbufs[t].at[me], s1.at[e], r1.at[e], _peer(k))

    def gather(t, k):
        e = 7 * t + k - 1
        return _rcopy(o_refs[t].at[chunk(t, me)], o_refs[t].at[chunk(t, me)], s2.at[e], r2.at[e], _peer(k))

    for t in range(len(p_refs)):
        if phase == 0:
            for k in range(1, NDEV):
                scatter(t, k).start()
            rbufs[t][me] = p_refs[t][chunk(t, me), :]
        elif phase == 1:
            for k in range(1, NDEV):
                e = 7 * t + k - 1
                _rcopy(p_refs[t].at[chunk(t, me)], rbufs[t].at[me ^ k], s1.at[e], r1.at[e], _peer(k)).wait_recv()
            tot = rbufs[t][0]
            for d in range(1, NDEV):
                tot = tot + rbufs[t][d]
            o_refs[t][chunk(t, me), :] = tot
            for k in range(1, NDEV):
                gather(t, k).start()
        else:
            for k in range(1, NDEV):
                e = 7 * t + k - 1
                _rcopy(o_refs[t].at[chunk(t, me)], o_refs[t].at[chunk(t, me ^ k)], s2.at[e], r2.at[e],
                       _peer(k)).wait_recv()
            for k in range(1, NDEV):
                scatter(t, k).wait_send()
                gather(t, k).wait_send()
            if out_refs is not None:
                out_refs[t][...] = o_refs[t][...]


def _adamw_update(w, g, m, v):
    nm = B1 * m + (1.0 - B1) * g
    nv = B2 * v + (1.0 - B2) * (g * g)
    m_hat = nm / (1.0 - B1 ** STEP)
    v_hat = nv / (1.0 - B2 ** STEP)
    return -LR * (m_hat / (jnp.sqrt(v_hat) + ADAM_EPS) + WD * w), nm, nv


def _adamw(w, g, m, v, tr):
    r, cdim = w.shape

    def body(w_ref, g_ref, m_ref, v_ref, d_ref, nm_ref, nv_ref):
        d_ref[...], nm_ref[...], nv_ref[...] = _adamw_update(w_ref[...], g_ref[...], m_ref[...], v_ref[...])

    spec = pl.BlockSpec((tr, cdim), lambda i: (i, 0))
    sd = jax.ShapeDtypeStruct((r, cdim), F32)
    return pl.pallas_call(
        body, name="adamw", grid=(r // tr,), in_specs=[spec] * 4, out_specs=(spec,) * 3, out_shape=(sd,) * 3,
        compiler_params=_params(("parallel",)),
    )(w, g, m, v)


def _adamw_halves(w, mine, theirs, m, v, core, tr):
    r, cdim = w.shape
    nbh = r // 2 // tr

    def body(c_ref, w_ref, a_ref, b_ref, m_ref, v_ref, g_ref, d_ref, nm_ref, nv_ref):
        is_mine = pl.program_id(0) // nbh == c_ref[0]
        g = jnp.where(is_mine, a_ref[...], b_ref[...])
        g_ref[...] = g
        d_ref[...], nm_ref[...], nv_ref[...] = _adamw_update(w_ref[...], g, m_ref[...], v_ref[...])

    spec = pl.BlockSpec((tr, cdim), lambda i, c: (i, 0))
    half = lambda own: pl.BlockSpec(
        (tr, cdim), lambda i, c: (jnp.clip(i - (c[0] if own else 1 - c[0]) * nbh, 0, nbh - 1), 0))
    sd = jax.ShapeDtypeStruct((r, cdim), F32)
    return pl.pallas_call(
        body, name="adamw_halves",
        grid_spec=pltpu.PrefetchScalarGridSpec(num_scalar_prefetch=1, grid=(r // tr,),
                                               in_specs=[spec, half(True), half(False), spec, spec],
                                               out_specs=(spec,) * 4),
        out_shape=(sd,) * 4,
        compiler_params=_params(("parallel",)),
    )(core, w, mine, theirs, m, v)


V_B_GATE, V_CONV_B, V_LAM, V_G_FINAL, V_CONV_W, V_LOSS, V_ROWS = 0, 2, 3, 4, 5, 9, 64
M_W_A, M_W_X, M_B_A, M_B_X, M_ROWS = 0, H * DH, 2 * H * DH, 2 * H * DH + H, 2112
SMALL = ("g_norm", "b_mod", "b_gate", "conv_b", "lam", "g_final", "conv_w", "w_a", "w_x", "b_a", "b_x")


def _adamw_small(redv, redm, g_conv, g_gnorm, g_bmod, wmv):
    def grad(name, rv, rm, gc, gg, gb):
        if name == "g_norm":
            return gg[...]
        if name == "b_mod":
            return gb[...]
        if name == "b_gate":
            return jnp.concatenate([rv[V_B_GATE + t:V_B_GATE + t + 1, :] for t in range(2)], axis=1)
        if name == "conv_b":
            return rv[V_CONV_B:V_CONV_B + 1, :]
        if name == "lam":
            return rv[V_LAM:V_LAM + 1, :]
        if name == "g_final":
            return rv[V_G_FINAL:V_G_FINAL + 1, :]
        if name == "conv_w":
            return gc[...]
        if name == "w_a":
            return rm[M_W_A:M_W_A + H * DH, :]
        if name == "w_x":
            return rm[M_W_X:M_W_X + H * DH, :]
        if name == "b_a":
            return rm[M_B_A:M_B_A + H, :]
        return rm[M_B_X:M_B_X + H, :]

    n = len(SMALL)

    def body(*refs):
        rv, rm, gc, gg, gb = refs[:5]
        ins, outs = refs[5:5 + 3 * n], refs[5 + 3 * n:]
        for t, name in enumerate(SMALL):
            w_ref, m_ref, v_ref = ins[3 * t:3 * t + 3]
            g_out, d_out, m_out, v_out = outs[4 * t:4 * t + 4]
            g = grad(name, rv, rm, gc, gg, gb)
            g_out[...] = g
            d_out[...], m_out[...], v_out[...] = _adamw_update(w_ref[...], g, m_ref[...], v_ref[...])

    vm = pl.BlockSpec(memory_space=pltpu.VMEM)
    flat = [a for name in SMALL for a in wmv[name]]
    shapes = [jax.ShapeDtypeStruct(wmv[name][0].shape, F32) for name in SMALL for _ in range(4)]
    outs = pl.pallas_call(
        body, name="adamw_small", out_shape=tuple(shapes),
        in_specs=[vm] * (5 + len(flat)), out_specs=tuple([vm] * len(shapes)),
        compiler_params=_params(),
    )(redv, redm, g_conv, g_gnorm, g_bmod, *flat)
    return {name: outs[4 * t:4 * t + 4] for t, name in enumerate(SMALL)}


def _rope_tables(positions):
    inv_freq = ROPE_THETA ** (-jnp.arange(0, ROT, 2, dtype=F32) / ROT)
    ang = positions.astype(F32)[:, None] * inv_freq
    cos, sin = jnp.cos(ang), jnp.sin(ang)
    n = positions.shape[0]
    half = ROT // 2
    rc = jnp.concatenate([cos, cos, jnp.ones((n, DH - ROT), F32)], axis=1)
    rsa = jnp.concatenate([-sin, jnp.zeros((n, DH - half), F32)], axis=1)
    rsb = jnp.concatenate([jnp.zeros((n, half), F32), sin, jnp.zeros((n, DH - ROT), F32)], axis=1)
    return rc, rsa, rsb


def kernel(x, c, positions, g_norm, w_mod, b_mod, w_in, b_gate, conv_w, conv_b, w_a, b_a, w_x, b_x, lam, w_out_rnn, w_out_attn, w_o, g_final, loss_target, m_g_norm, m_w_mod, m_b_mod, m_w_in, m_b_gate, m_conv_w, m_conv_b, m_w_a, m_b_a, m_w_x, m_b_x, m_lam, m_w_out_rnn, m_w_out_attn, m_w_o, m_g_final, v_g_norm, v_w_mod, v_b_mod, v_w_in, v_b_gate, v_conv_w, v_conv_b, v_w_a, v_b_a, v_w_x, v_b_x, v_lam, v_w_out_rnn, v_w_out_attn, v_w_o, v_g_final):
    s = x.shape[1]
    xi = lax.axis_index("x")
    yi = lax.axis_index("y")
    ci = lax.axis_index("c")
    shard = 2 * xi + yi
    x2d = x[0]
    tgt = loss_target[0]
    pos = positions[0]

    c_all, mod4, conv_all = _mod_fwd(c, w_mod[0], b_mod.reshape(4, 1, 768), conv_w[0])
    mod = mod4.reshape(1, 3 * D)
    shift, scale, gate = mod[:, :D], mod[:, D:2 * D], mod[:, 2 * D:]
    w3sh = jnp.concatenate([w_out_rnn[0], w_out_attn[0], w_o[0]], axis=0).astype(BF16)
    wsh = w_in[0].astype(BF16)
    conv_full = conv_all[0::2].transpose(1, 0, 2).reshape(4, D)

    order = jnp.stack([shard, shard ^ 2, shard ^ 1, shard ^ 3]).astype(jnp.int32)
    p, hbf, wg = _gather_norm_inproj(x2d, g_norm, shift, scale, wsh, order)
    rc, rsa, rsb = _rope_tables(pos)
    pos_col = jnp.broadcast_to((pos == 0).astype(F32)[:, None], (s, DH))
    b_a3, b_x3 = b_a.reshape(H, 1, DH), b_x.reshape(H, 1, DH)
    hr, gr = _rnn_fwd(p, pos_col, conv_full, conv_b, w_a[0], b_a3, w_x[0], b_x3, lam)
    o, lse, ga, q_rot, k_rot, w3g = _attn_fwd(p, rc, rsa, rsb, w3sh)
    w3g = lax.dynamic_update_slice(w3g, w3sh[None], (shard, 0, 0))
    w3 = w3g.reshape(4, 3, 256, D).transpose(1, 0, 2, 3).reshape(3, D, D)

    dgr, dga, dc, dx2, vec_t, g_out = _tail(gr, ga, p, x2d, tgt, w3, b_gate, gate, g_final.reshape(1, D))

    dxr, dzr, g_wa, g_ba, g_wx, g_bx, g_lam, g_cw, g_cb = _rnn_bwd(
        p, hr, dgr, pos_col, conv_full, conv_b, w_a[0], b_a3, w_x[0], b_x3, lam)
    vpack = jnp.concatenate([
        vec_t[2:4],
        g_cb.reshape(1, D),
        g_lam.reshape(1, D),
        vec_t[0:1],
        g_cw.transpose(1, 0, 2).reshape(4, D),
        vec_t[4:5],
        jnp.zeros((V_ROWS - 10, D), F32)], axis=0)
    mpack = jnp.concatenate([
        g_wa.reshape(H * DH, DH), g_wx.reshape(H * DH, DH), g_ba.reshape(H, DH), g_bx.reshape(H, DH),
        jnp.zeros((M_ROWS - 2 * H * DH - 2 * H, DH), F32)], axis=0)
    (dq, dk, dv, dza), (redv, redm) = _attn_bwd(p, q_rot, k_rot, o, lse, dga, rc, rsa, rsb, [vpack, mpack])

    pieces = [dxr, dzr, dq, dk, dv, dza, dc]
    g_win, rb_a, rb_b = _inproj_bwd_w(pieces, hbf, g_out)

    core = ci.reshape(1)
    shard_core = jnp.stack([shard, ci]).astype(jnp.int32)
    qh_a, qh_b = _add_half(g_win, rb_a, core, tr=256), _add_half(g_out, rb_b, core, tr=128)
    grad_x, vec_n, (r_a, r_b) = _inproj_bwd_x(pieces, wg, wsh, x2d, dx2, g_norm, scale, [qh_a, qh_b])
    f_a = _sum_slots(g_win, rb_a, r_a, shard_core, tr=256)
    f_b = _sum_slots(g_out, rb_b, r_b, shard_core, tr=128)
    s_a, s_b = _pair_swap([f_a, f_b])

    dmod_row = jnp.concatenate([vec_n[0:1], vec_n[1:2], vec_t[1:2]], axis=1)
    loss = redv[V_LOSS, 0]
    grad_w_mod, g_bmod4, g_gnorm = _mod_bwd(dmod_row.reshape(4, 1, 768), vec_n[2:3], c_all)
    g_conv_sh = lax.dynamic_slice_in_dim(redv[V_CONV_W:V_CONV_W + 4], shard * 256, 256, axis=1)

    shape2d = dict(g_norm=(1, D), b_mod=(1, 3 * D), b_gate=(1, 2 * D), conv_b=(1, D), lam=(1, D), g_final=(1, D),
                   conv_w=(4, 256), w_a=(H * DH, DH), w_x=(H * DH, DH), b_a=(H, DH), b_x=(H, DH))
    given = dict(
        g_norm=(g_norm, m_g_norm, v_g_norm), b_mod=(b_mod, m_b_mod, v_b_mod), b_gate=(b_gate, m_b_gate, v_b_gate),
        conv_b=(conv_b, m_conv_b, v_conv_b), lam=(lam, m_lam, v_lam), g_final=(g_final, m_g_final, v_g_final),
        conv_w=(conv_w, m_conv_w, v_conv_w), w_a=(w_a, m_w_a, v_w_a), w_x=(w_x, m_w_x, v_w_x),
        b_a=(b_a, m_b_a, v_b_a), b_x=(b_x, m_b_x, v_b_x))
    small = _adamw_small(redv, redm, g_conv_sh, g_gnorm, g_bmod4.reshape(1, 3 * D),
                         {n: tuple(a.reshape(shape2d[n]) for a in given[n]) for n in SMALL})

    big_in = _adamw_halves(w_in[0], f_a, s_a, m_w_in[0], v_w_in[0], core, tr=256)
    big_mod = (grad_w_mod,) + tuple(_adamw(w_mod[0], grad_w_mod, m_w_mod[0], v_w_mod[0], tr=256))
    w3f = jnp.concatenate([w_out_rnn[0], w_out_attn[0], w_o[0]], axis=0)
    m3f = jnp.concatenate([m_w_out_rnn[0], m_w_out_attn[0], m_w_o[0]], axis=0)
    v3f = jnp.concatenate([v_w_out_rnn[0], v_w_out_attn[0], v_w_o[0]], axis=0)
    big_out = _adamw_halves(w3f, f_b, s_b, m3f, v3f, core, tr=128)

    names = ["g_norm", "w_mod", "b_mod", "w_in", "b_gate", "conv_w", "conv_b", "w_a", "b_a", "w_x", "b_x", "lam",
             "w_out_rnn", "w_out_attn", "w_o", "g_final"]
    outs = []
    for idx in range(4):
        d = {n: small[n][idx].reshape(given[n][0].shape) for n in SMALL}
        d.update(w_mod=big_mod[idx][None], w_in=big_in[idx][None],
                 w_out_rnn=big_out[idx][0:256][None], w_out_attn=big_out[idx][256:512][None],
                 w_o=big_out[idx][512:768][None])
        outs.append(d)
    flat = [d[n] for d in outs for n in names]
    return (loss, grad_x[None], *flat)
```

```python
import jax
import jax.numpy as jnp
from jax import lax
from jax.experimental import pallas as pl
from jax.experimental.pallas import tpu as pltpu

F32, BF16 = jnp.float32, jnp.bfloat16
MESH = pl.DeviceIdType.MESH
HIGHEST = lax.Precision.HIGHEST

D = 1024
H = 8
DH = 128
PW = 2048
EPS = 1e-6
LRU_C = 8.0
SCALE = DH ** -0.5
NEG = -1e30
SPAN = 2048
UB = 128
DILATIONS = (1, 4, 16)
UNIT_BATCH = 16
UNIT_BATCH_FWD = 8
ROPE_THETA = 500000.0
ROT = 32

LR, B1, B2, ADAM_EPS, WD, STEP = 0.001, 0.9, 0.999, 1e-08, 0.01, 10

NDEV = 8


def _params(sem=None, vmem_mb=56):
    return pltpu.CompilerParams(dimension_semantics=sem, vmem_limit_bytes=vmem_mb * 2 ** 20)


def _coords():
    return lax.axis_index("x"), lax.axis_index("y"), lax.axis_index("c")


def _flip(v, bit):
    return 1 - v if bit else v


def _peer(k):
    x, y, c = _coords()
    return (_flip(x, (k >> 2) & 1), _flip(y, (k >> 1) & 1), _flip(c, k & 1))


def _my_index():
    x, y, c = _coords()
    return 4 * x + 2 * y + c


def _rcopy(src, dst, ssem, rsem, dev):
    return pltpu.make_async_remote_copy(src_ref=src, dst_ref=dst, send_sem=ssem, recv_sem=rsem,
                                        device_id=dev, device_id_type=MESH)


def _sigmoid(x):
    return jax.nn.sigmoid(x)


def _dot(a, b):
    return jnp.dot(a, b, preferred_element_type=F32)


def _dot_nt(a, b):
    return lax.dot_general(a, b, (((1,), (1,)), ((), ())), preferred_element_type=F32)


def _dot_tn(a, b):
    return lax.dot_general(a, b, (((0,), (0,)), ((), ())), preferred_element_type=F32)


def _colsum(a):
    return jnp.sum(a, axis=0, keepdims=True)


def _mod_fwd(c, w_mod_sh, b_mod4, conv_sh):
    def body(c_ref, w_ref, b_ref, cv_ref, call_ref, mod_ref, cvall_ref, rows_ref, cmat_ref, s1, r1, s2, r2, s3, r3):
        x, y, _ = _coords()
        me = _my_index()
        j = 2 * x + y
        call_ref[me] = c_ref[...]
        cvall_ref[me] = cv_ref[...]
        sends = []
        for k in range(1, NDEV):
            cp = _rcopy(call_ref.at[me], call_ref.at[me], s1.at[k - 1], r1.at[k - 1], _peer(k))
            cp.start()
            sends.append(cp)
            cp = _rcopy(cvall_ref.at[me], cvall_ref.at[me], s3.at[k - 1], r3.at[k - 1], _peer(k))
            cp.start()
            sends.append(cp)
        for k in range(1, NDEV):
            pk = me ^ k
            _rcopy(call_ref.at[pk], call_ref.at[pk], s1.at[k - 1], r1.at[k - 1], _peer(k)).wait_recv()
        for b in range(NDEV):
            cmat_ref[pl.ds(b, 1), :] = call_ref[b]
        cm = cmat_ref[...]
        act = cm * _sigmoid(cm)
        mp = jnp.dot(act, w_ref[...], preferred_element_type=F32, precision=HIGHEST) + b_ref[j]
        for b in range(NDEV):
            rows_ref[b] = mp[b:b + 1]
        mod_ref[j] = rows_ref[me]
        for q, k in enumerate((2, 4, 6)):
            cp = _rcopy(rows_ref.at[me ^ k], mod_ref.at[j], s2.at[q], r2.at[q], _peer(k))
            cp.start()
            sends.append(cp)
        for q, k in enumerate((2, 4, 6)):
            jq = j ^ (k >> 1)
            _rcopy(rows_ref.at[me], mod_ref.at[jq], s2.at[q], r2.at[q], _peer(k)).wait_recv()
        for k in range(1, NDEV):
            pk = me ^ k
            _rcopy(cvall_ref.at[pk], cvall_ref.at[pk], s3.at[k - 1], r3.at[k - 1], _peer(k)).wait_recv()
        for cp in sends:
            cp.wait_send()

    vm = pl.BlockSpec(memory_space=pltpu.VMEM)
    return pl.pallas_call(
        body, name="mod_fwd",
        out_shape=(jax.ShapeDtypeStruct((NDEV, 1, D), F32), jax.ShapeDtypeStruct((4, 1, 768), F32),
                   jax.ShapeDtypeStruct((NDEV,) + conv_sh.shape, F32)),
        in_specs=[vm, vm, vm, vm], out_specs=(vm, vm, vm),
        scratch_shapes=[pltpu.VMEM((NDEV, 1, 768), F32), pltpu.VMEM((NDEV, D), F32),
                        pltpu.SemaphoreType.DMA((7,)), pltpu.SemaphoreType.DMA((7,)),
                        pltpu.SemaphoreType.DMA((3,)), pltpu.SemaphoreType.DMA((3,)),
                        pltpu.SemaphoreType.DMA((7,)), pltpu.SemaphoreType.DMA((7,))],
        compiler_params=_params(),
    )(c, w_mod_sh, b_mod4, conv_sh)


def _mod_bwd(dmod4, gn_row, c_all):
    def body(d_ref, g_ref, call_ref, gw_ref, gb_ref, gg_ref, dall_ref, gall_ref, cmat_ref, dmat_ref, s1, r1, s2, r2):
        x, y, _ = _coords()
        me = _my_index()
        j = 2 * x + y
        dall_ref[me] = d_ref[...]
        gall_ref[me] = g_ref[...]
        sends = []
        for k in range(1, NDEV):
            for buf, ss, rs in ((dall_ref, s1, r1), (gall_ref, s2, r2)):
                cp = _rcopy(buf.at[me], buf.at[me], ss.at[k - 1], rs.at[k - 1], _peer(k))
                cp.start()
                sends.append(cp)
        for k in range(1, NDEV):
            pk = me ^ k
            for buf, ss, rs in ((dall_ref, s1, r1), (gall_ref, s2, r2)):
                _rcopy(buf.at[pk], buf.at[pk], ss.at[k - 1], rs.at[k - 1], _peer(k)).wait_recv()
        for cp in sends:
            cp.wait_send()
        gb, gg = dall_ref[0], gall_ref[0]
        for b in range(1, NDEV):
            gb = gb + dall_ref[b]
            gg = gg + gall_ref[b]
        gb_ref[...] = gb
        gg_ref[...] = gg
        for b in range(NDEV):
            cmat_ref[pl.ds(b, 1), :] = call_ref[b]
            dmat_ref[pl.ds(b, 1), :] = dall_ref[b, j]
        cm = cmat_ref[...]
        act = cm * _sigmoid(cm)
        gw_ref[...] = lax.dot_general(act, dmat_ref[...], (((0,), (0,)), ((), ())),
                                      preferred_element_type=F32, precision=HIGHEST)

    vm = pl.BlockSpec(memory_space=pltpu.VMEM)
    return pl.pallas_call(
        body, name="mod_bwd",
        out_shape=(jax.ShapeDtypeStruct((D, 768), F32), jax.ShapeDtypeStruct((4, 1, 768), F32),
                   jax.ShapeDtypeStruct((1, D), F32)),
        in_specs=[vm, vm, vm], out_specs=(vm, vm, vm),
        scratch_shapes=[pltpu.VMEM((NDEV, 4, 1, 768), F32), pltpu.VMEM((NDEV, 1, D), F32),
                        pltpu.VMEM((NDEV, D), F32), pltpu.VMEM((NDEV, 768), F32),
                        pltpu.SemaphoreType.DMA((7,)), pltpu.SemaphoreType.DMA((7,)),
                        pltpu.SemaphoreType.DMA((7,)), pltpu.SemaphoreType.DMA((7,))],
        compiler_params=_params(),
    )(dmod4, gn_row, c_all)


def _gather_norm_inproj(x, gn, shift, scale, wsh, order, tm=1024, tn=1024):
    s = x.shape[0]
    ni = s // tm
    npc = PW // tn
    rows, cols = wsh.shape
    half = rows // 2
    nch = 4
    cr = half // nch
    chips = ((1, 0), (0, 1), (1, 1))

    def body(ord_ref, x_ref, gn_ref, sh_ref, sc_ref, wsh_hbm, p_ref, h_ref, wg_hbm, hs_all, w_s, wsem, ss, rs):
        slot, i, col = pl.program_id(0), pl.program_id(1), pl.program_id(2)
        cx, cy, cc = _coords()
        j = 2 * cx + cy
        sib = (cx, cy, 1 - cc)
        mine = lambda n: pl.ds(cc * half + n * cr, cr)
        theirs = lambda n: pl.ds((1 - cc) * half + n * cr, cr)
        shard_of = lambda q: j ^ (2 * chips[q][0] + chips[q][1])

        def to_chip(q, n):
            e = nch * q + n
            return _rcopy(wsh_hbm.at[mine(n)], wg_hbm.at[j, mine(n)], ss.at[e], rs.at[e],
                          (_flip(cx, chips[q][0]), _flip(cy, chips[q][1]), cc))

        def from_chip(q, n):
            e = nch * q + n
            return _rcopy(wsh_hbm.at[mine(n)], wg_hbm.at[shard_of(q), mine(n)], ss.at[e], rs.at[e], sib)

        def to_sibling(q, n):
            e = 3 * nch + nch * q + n
            return _rcopy(wg_hbm.at[shard_of(q), mine(n)], wg_hbm.at[shard_of(q), mine(n)], ss.at[e], rs.at[e], sib)

        def from_sibling(q, n):
            e = 3 * nch + nch * q + n
            return _rcopy(wsh_hbm.at[mine(n)], wg_hbm.at[shard_of(q), theirs(n)], ss.at[e], rs.at[e], sib)

        def load(sl, src):
            cp = pltpu.make_async_copy(src, w_s.at[sl], wsem.at[sl])
            cp.start()
            cp.wait()

        first = (i == 0) & (col == 0)

        @pl.when(first & (slot == 0))
        def _():
            for n in range(nch):
                for q in (0, 1):
                    to_chip(q, n).start()
            load(0, wsh_hbm.at[pl.ds(0, D), :])

        @pl.when(first & (slot == 1))
        def _():
            for q in (0, 1):
                for n in range(nch):
                    from_chip(q, n).wait_recv()
                    to_sibling(q, n).start()
            for n in range(nch):
                to_chip(2, n).start()
            for n in range(nch):
                from_sibling(0, n).wait_recv()
            load(1, wg_hbm.at[shard_of(0), pl.ds(0, D), :])

        @pl.when(first & (slot == 2))
        def _():
            for n in range(nch):
                from_sibling(1, n).wait_recv()
            load(2, wg_hbm.at[shard_of(1), pl.ds(0, D), :])

        @pl.when(first & (slot == 3))
        def _():
            for n in range(nch):
                from_chip(2, n).wait_recv()
                to_sibling(2, n).start()
            for n in range(nch):
                from_sibling(2, n).wait_recv()
            load(3, wg_hbm.at[shard_of(2), pl.ds(0, D), :])
            for q in range(3):
                for n in range(nch):
                    to_chip(q, n).wait_send()
                    to_sibling(q, n).wait_send()

        @pl.when((slot == 0) & (col == 0))
        def _():
            xt = x_ref[...]
            rstd = lax.rsqrt(jnp.mean(xt * xt, axis=-1, keepdims=True) + EPS)
            h = ((xt * rstd * gn_ref[...]) * (1.0 + sc_ref[...]) + sh_ref[...]).astype(BF16)
            hs_all[i] = h
            h_ref[...] = h

        p_ref[0] = _dot(hs_all[i], w_s[slot, :, pl.ds(pl.multiple_of(col * tn, tn), tn)]).astype(BF16)

    row = pl.BlockSpec((1, D), lambda sl, i, col, o: (0, 0))
    x_rows = lambda sl, i, col, o: (jnp.where(sl == 0, i, ni - 1), 0)
    any_ = pl.BlockSpec(memory_space=pl.ANY)
    return pl.pallas_call(
        body, name="gather_norm_inproj",
        grid_spec=pltpu.PrefetchScalarGridSpec(
            num_scalar_prefetch=1, grid=(4, ni, npc),
            in_specs=[pl.BlockSpec((tm, D), x_rows), row, row, row, any_],
            out_specs=(pl.BlockSpec((1, tm, tn), lambda sl, i, col, o: (o[sl], i, col)),
                       pl.BlockSpec((tm, D), x_rows), any_),
            scratch_shapes=[pltpu.VMEM((ni, tm, D), BF16), pltpu.VMEM((4, D, PW), BF16),
                            pltpu.SemaphoreType.DMA((4,)),
                            pltpu.SemaphoreType.DMA((6 * nch,)), pltpu.SemaphoreType.DMA((6 * nch,))]),
        out_shape=(jax.ShapeDtypeStruct((4, s, PW), BF16), jax.ShapeDtypeStruct((s, D), BF16),
                   jax.ShapeDtypeStruct((4, rows, cols), wsh.dtype)),
        compiler_params=_params(("arbitrary", "arbitrary", "arbitrary")),
    )(order, x, gn, shift, scale, wsh)


def _shift_down(prev8, cur, d):
    t = cur.shape[0]
    c3 = cur.reshape(t // 8, 8, DH)
    rot = pltpu.roll(c3, d, 1)
    before = jnp.concatenate([pltpu.roll(prev8, d, 0).reshape(1, 8, DH), rot[:-1]], axis=0)
    rows = lax.broadcasted_iota(jnp.int32, c3.shape, 1)
    return jnp.where(rows >= d, rot, before).reshape(t, DH)


def _shift_up(cur, next8, d):
    t = cur.shape[0]
    c3 = cur.reshape(t // 8, 8, DH)
    rot = pltpu.roll(c3, 8 - d, 1)
    after = jnp.concatenate([rot[1:], pltpu.roll(next8, 8 - d, 0).reshape(1, 8, DH)], axis=0)
    rows = lax.broadcasted_iota(jnp.int32, c3.shape, 1)
    return jnp.where(rows < 8 - d, rot, after).reshape(t, DH)


def _rnn_gates(xr, prev8, cw, cb, wa, ba, wx, bx, lam, reset):
    xc = cw[3:4] * xr + cb
    for d in (1, 2, 3):
        xc = xc + cw[3 - d:4 - d] * _shift_down(prev8, xr, d)
    xcb = xc.astype(BF16)
    r = _sigmoid(_dot(xcb, wa.astype(BF16)) + ba)
    ig = _sigmoid(_dot(xcb, wx.astype(BF16)) + bx)
    nl = -lam
    sp = jnp.maximum(nl, 0.0) + jnp.log1p(jnp.exp(-jnp.abs(nl)))
    log_a = (-LRU_C * r) * sp
    a_raw = jnp.exp(log_a)
    a = jnp.where(reset, 0.0, a_raw)
    mult = jnp.where(reset, 1.0, jnp.sqrt(1.0 - a_raw * a_raw))
    return xc, r, ig, sp, a, mult


def _log_scan(a, b, axis, up):
    n = a.shape[axis]
    rows = lax.broadcasted_iota(jnp.int32, a.shape, axis)
    d = 1
    while d < n:
        m = rows < n - d if up else rows >= d
        shift = n - d if up else d
        a_s = pltpu.roll(a, shift, axis)
        b_s = pltpu.roll(b, shift, axis)
        b = jnp.where(m, a * b_s + b, b)
        a = jnp.where(m, a * a_s, a)
        d *= 2
    return a, b


def _scan(a, b, t, edge, up=False):
    g = t // 8
    a3, b3 = _log_scan(a.reshape(g, 8, DH), b.reshape(g, 8, DH), 1, up)
    last = 0 if up else 7
    ag, bg = _log_scan(a3[:, last, :], b3[:, last, :], 0, up)
    hg = ag * edge + bg
    grp = lax.broadcasted_iota(jnp.int32, hg.shape, 0)
    if up:
        cin = jnp.where(grp == g - 1, edge, pltpu.roll(hg, g - 1, 0))
        tail = hg[0:1]
    else:
        cin = jnp.where(grp == 0, edge, pltpu.roll(hg, 1, 0))
        tail = hg[g - 1:g]
    return (a3 * cin[:, None, :] + b3).reshape(t, DH), tail


def _rnn_fwd(p, pos, conv_w, conv_b, w_a, b_a, w_x, b_x, lam, tt=1024):
    s = p.shape[1]
    nt = s // tt

    def body(xr_ref, z_ref, pos_ref, cw_ref, cb_ref, wa_ref, ba_ref, wx_ref, bx_ref, lam_ref,
             hr_ref, gr_ref, xprev, hprev):
        @pl.when(pl.program_id(1) == 0)
        def _():
            xprev[...] = jnp.zeros_like(xprev)
            hprev[...] = jnp.zeros_like(hprev)

        xr = xr_ref[0].astype(F32)
        z = z_ref[0].astype(F32)
        reset = pos_ref[...] > 0.5
        xc, r, ig, sp, a, mult = _rnn_gates(xr, xprev[...], cw_ref[...], cb_ref[...], wa_ref[0], ba_ref[0],
                                            wx_ref[0], bx_ref[0], lam_ref[...], reset)
        bx = mult * ig * xc
        h, h_last = _scan(a, bx, tt, hprev[0:1])
        xprev[...] = xr[tt - 8:]
        hprev[...] = jnp.broadcast_to(h_last, (8, DH))
        hr_ref[...] = h
        gr_ref[...] = (h * (z * _sigmoid(z))).astype(BF16)

    head_row = lambda hh, t: (0, hh)
    return pl.pallas_call(
        body, name="rnn_fwd", grid=(H, nt),
        in_specs=[pl.BlockSpec((1, tt, DH), lambda hh, t: (0, t, hh)),
                  pl.BlockSpec((1, tt, DH), lambda hh, t: (0, t, H + hh)),
                  pl.BlockSpec((tt, DH), lambda hh, t: (t, 0)),
                  pl.BlockSpec((4, DH), head_row), pl.BlockSpec((1, DH), head_row),
                  pl.BlockSpec((1, DH, DH), lambda hh, t: (hh, 0, 0)), pl.BlockSpec((1, 1, DH), lambda hh, t: (hh, 0, 0)),
                  pl.BlockSpec((1, DH, DH), lambda hh, t: (hh, 0, 0)), pl.BlockSpec((1, 1, DH), lambda hh, t: (hh, 0, 0)),
                  pl.BlockSpec((1, DH), head_row)],
        out_specs=(pl.BlockSpec((tt, DH), lambda hh, t: (t, hh)), pl.BlockSpec((tt, DH), lambda hh, t: (t, hh))),
        out_shape=(jax.ShapeDtypeStruct((s, D), F32), jax.ShapeDtypeStruct((s, D), BF16)),
        scratch_shapes=[pltpu.VMEM((8, DH), F32), pltpu.VMEM((8, DH), F32)],
        compiler_params=_params(("parallel", "arbitrary")),
    )(p, p, pos, conv_w, conv_b, w_a, b_a, w_x, b_x, lam)


def _rnn_bwd(p, hr, dgr, pos, conv_w, conv_b, w_a, b_a, w_x, b_x, lam, tt=1024):
    s = p.shape[1]
    nt = s // tt
    t8 = tt // 8

    def body(xr_ref, z_ref, xp_ref, hr_ref, hp_ref, dg_ref, pos_ref, cw_ref, cb_ref, wa_ref, ba_ref, wx_ref, bx_ref,
             lam_ref, dxr_ref, dz_ref, gwa_ref, gba_ref, gwx_ref, gbx_ref, glam_ref, gcw_ref, gcb_ref,
             a_next, g_next, dxc_next):
        t = pl.program_id(1)
        has_prev = t < nt - 1

        @pl.when(t == 0)
        def _():
            a_next[...] = jnp.zeros_like(a_next)
            g_next[...] = jnp.zeros_like(g_next)
            dxc_next[...] = jnp.zeros_like(dxc_next)
            gwa_ref[...] = jnp.zeros_like(gwa_ref)
            gba_ref[...] = jnp.zeros_like(gba_ref)
            gwx_ref[...] = jnp.zeros_like(gwx_ref)
            gbx_ref[...] = jnp.zeros_like(gbx_ref)
            glam_ref[...] = jnp.zeros_like(glam_ref)
            gcw_ref[...] = jnp.zeros_like(gcw_ref)
            gcb_ref[...] = jnp.zeros_like(gcb_ref)

        xr = xr_ref[0].astype(F32)
        z = z_ref[0].astype(F32)
        hr_blk = hr_ref[...]
        dg = dg_ref[...]
        xprev = jnp.where(has_prev, xp_ref[0].astype(F32)[8:], 0.0)
        hprev8 = jnp.where(has_prev, hp_ref[...], 0.0)
        reset = pos_ref[...] > 0.5
        cw = cw_ref[...]
        wa = wa_ref[0]
        wx = wx_ref[0]
        lam_v = lam_ref[...]
        xc, r, ig, sp, a, mult = _rnn_gates(xr, xprev, cw, cb_ref[...], wa, ba_ref[0], wx, bx_ref[0], lam_v, reset)

        sz = _sigmoid(z)
        dh = dg * (z * sz)
        dz_ref[...] = (dg * hr_blk * (sz * (1.0 + z * (1.0 - sz)))).astype(BF16)

        an = _shift_up(a, a_next[...], 1)
        g, g_first = _scan(an, dh, tt, g_next[0:1], up=True)
        a_next[...] = jnp.broadcast_to(a[0:1], (8, DH))
        g_next[...] = jnp.broadcast_to(g_first, (8, DH))

        hm1 = _shift_down(hprev8, hr_blk, 1)
        da = g * hm1
        dmult = g * (ig * xc)
        di = g * (mult * xc)
        dxc = g * (mult * ig)
        dla = jnp.where(reset, 0.0, da * a - dmult * (a * a) / mult)
        dr = dla * (-LRU_C * sp)
        dsp = _colsum(dla * (-LRU_C * r))
        glam_ref[0] += dsp * (-_sigmoid(-lam_v))
        dpa = dr * r * (1.0 - r)
        dpx = di * ig * (1.0 - ig)
        dpab = dpa.astype(BF16)
        dpxb = dpx.astype(BF16)
        dxc = dxc + _dot_nt(dpab, wa.astype(BF16)) + _dot_nt(dpxb, wx.astype(BF16))
        xcb = xc.astype(BF16)
        gwa_ref[0] += _dot_tn(xcb, dpab)
        gwx_ref[0] += _dot_tn(xcb, dpxb)
        gba_ref[0] += _colsum(dpa)
        gbx_ref[0] += _colsum(dpx)

        dxr = cw[3:4] * dxc
        for d in (1, 2, 3):
            dxr = dxr + cw[3 - d:4 - d] * _shift_up(dxc, dxc_next[...], d)
        dxr_ref[...] = dxr.astype(BF16)
        dxc_next[...] = dxc[0:8]
        gcb_ref[0] += _colsum(dxc)
        gcw_ref[0, 3:4, :] += _colsum(xr * dxc)
        for d in (1, 2, 3):
            gcw_ref[0, 3 - d:4 - d, :] += _colsum(_shift_down(xprev, xr, d) * dxc)

    rt = lambda t: nt - 1 - t
    prev8 = lambda t: jnp.maximum(rt(t) * t8 - 1, 0)
    head_row = lambda hh, t: (0, hh)
    hsm = lambda hh, t: (hh, 0, 0)
    return pl.pallas_call(
        body, name="rnn_bwd", grid=(H, nt),
        in_specs=[pl.BlockSpec((1, tt, DH), lambda hh, t: (0, rt(t), hh)),
                  pl.BlockSpec((1, tt, DH), lambda hh, t: (0, rt(t), H + hh)),
                  pl.BlockSpec((1, 16, DH), lambda hh, t: (0, jnp.maximum(rt(t) * (tt // 16) - 1, 0), hh)),
                  pl.BlockSpec((tt, DH), lambda hh, t: (rt(t), hh)),
                  pl.BlockSpec((8, DH), lambda hh, t: (prev8(t), hh)),
                  pl.BlockSpec((tt, DH), lambda hh, t: (rt(t), hh)),
                  pl.BlockSpec((tt, DH), lambda hh, t: (rt(t), 0)),
                  pl.BlockSpec((4, DH), head_row), pl.BlockSpec((1, DH), head_row),
                  pl.BlockSpec((1, DH, DH), hsm), pl.BlockSpec((1, 1, DH), hsm),
                  pl.BlockSpec((1, DH, DH), hsm), pl.BlockSpec((1, 1, DH), hsm),
                  pl.BlockSpec((1, DH), head_row)],
        out_specs=(pl.BlockSpec((tt, DH), lambda hh, t: (rt(t), hh)), pl.BlockSpec((tt, DH), lambda hh, t: (rt(t), hh)),
                   pl.BlockSpec((1, DH, DH), hsm), pl.BlockSpec((1, 1, DH), hsm),
                   pl.BlockSpec((1, DH, DH), hsm), pl.BlockSpec((1, 1, DH), hsm),
                   pl.BlockSpec((1, 1, DH), hsm), pl.BlockSpec((1, 4, DH), hsm), pl.BlockSpec((1, 1, DH), hsm)),
        out_shape=(jax.ShapeDtypeStruct((s, D), BF16), jax.ShapeDtypeStruct((s, D), BF16),
                   jax.ShapeDtypeStruct((H, DH, DH), F32), jax.ShapeDtypeStruct((H, 1, DH), F32),
                   jax.ShapeDtypeStruct((H, DH, DH), F32), jax.ShapeDtypeStruct((H, 1, DH), F32),
                   jax.ShapeDtypeStruct((H, 1, DH), F32), jax.ShapeDtypeStruct((H, 4, DH), F32),
                   jax.ShapeDtypeStruct((H, 1, DH), F32)),
        scratch_shapes=[pltpu.VMEM((8, DH), F32), pltpu.VMEM((8, DH), F32), pltpu.VMEM((8, DH), F32)],
        compiler_params=_params(("parallel", "arbitrary")),
    )(p, p, p, hr, hr, dgr, pos, conv_w, conv_b, w_a, b_a, w_x, b_x, lam)


def _rope(t, c, sa, sb):
    return t * c + pltpu.roll(t, DH - ROT // 2, 1) * sa + pltpu.roll(t, ROT // 2, 1) * sb


def _rope_bwd(g, c, sa, sb):
    return g * c + pltpu.roll(g * sa, ROT // 2, 1) + pltpu.roll(g * sb, DH - ROT // 2, 1)


def _unit_bases(gi, u):
    dil = DILATIONS[gi]
    if dil == 1:
        return u * UB, SPAN + (u - 1) * UB, u == 0
    if dil == 4:
        blk, r = u // 4, u % 4
        return blk * 4 * UB + r, SPAN + (blk - 1) * 4 * UB + r, blk == 0
    return u, u, True


def _unit_slices(gi, u):
    dil = DILATIONS[gi]
    qb0, kb0, first = _unit_bases(gi, u)
    if dil == 1:
        if not isinstance(qb0, int):
            qb0, kb0 = pl.multiple_of(qb0, UB), pl.multiple_of(kb0, UB)
        return pl.ds(qb0, UB), pl.ds(kb0, 2 * UB), first
    return pl.ds(qb0, UB, stride=dil), pl.ds(kb0, 2 * UB, stride=dil), first


def _bdot(a, b):
    return lax.dot_general(a, b, (((2,), (1,)), ((0,), (0,))), preferred_element_type=F32)


def _bdot_nt(a, b):
    return lax.dot_general(a, b, (((2,), (2,)), ((0,), (0,))), preferred_element_type=F32)


def _bdot_tn(a, b):
    return lax.dot_general(a, b, (((1,), (1,)), ((0,), (0,))), preferred_element_type=F32)


def _band_mask(first_in_span, has_prev):
    qi = lax.broadcasted_iota(jnp.int32, (UB, 2 * UB), 0)
    ki = lax.broadcasted_iota(jnp.int32, (UB, 2 * UB), 1)
    dist = UB + qi - ki
    band = (dist >= 0) & (dist <= UB)
    return band & ((ki >= UB) | jnp.logical_not(first_in_span) | has_prev)


def _gather_halves(phase, src_hbm, dst_hbm, ss, rs):
    half = src_hbm.shape[0] // 2
    cx, cy, cc = _coords()
    j = 2 * cx + cy
    sib = (cx, cy, 1 - cc)
    mine = pl.ds(cc * half, half)
    theirs = pl.ds((1 - cc) * half, half)
    chips = ((1, 0), (0, 1), (1, 1))
    for q, (kx, ky) in enumerate(chips):
        jq = j ^ (2 * kx + ky)
        out = _rcopy(src_hbm.at[mine], dst_hbm.at[j, mine], ss.at[q], rs.at[q], (_flip(cx, kx), _flip(cy, ky), cc))
        landed = _rcopy(src_hbm.at[mine], dst_hbm.at[jq, mine], ss.at[q], rs.at[q], sib)
        onward = _rcopy(dst_hbm.at[jq, mine], dst_hbm.at[jq, mine], ss.at[3 + q], rs.at[3 + q], sib)
        from_sib = _rcopy(src_hbm.at[mine], dst_hbm.at[jq, theirs], ss.at[3 + q], rs.at[3 + q], sib)
        if phase == 0:
            out.start()
        elif phase == 1:
            landed.wait_recv()
            onward.start()
        else:
            from_sib.wait_recv()
            out.wait_send()
            onward.wait_send()


def _attn_fwd(p, rc, rsa, rsb, w3sh):
    s = p.shape[1]
    ns = s // SPAN
    nunit = SPAN // UB

    def body(q_ref, k_ref, v_ref, z_ref, c_ref, sa_ref, sb_ref, w3_hbm, o_ref, lse_ref, ga_ref, qro_ref, kro_ref,
             w3g_hbm, qr, kf, vf, acc, mm, ll, ss, rs):
        hh, n = pl.program_id(0), pl.program_id(1)
        for phase, at_head, at_span in ((0, 0, 0), (1, H // 2, 0), (2, H - 1, ns - 1)):
            @pl.when((hh == at_head) & (n == at_span))
            def _(phase=phase):
                _gather_halves(phase, w3_hbm, w3g_hbm, ss, rs)

        @pl.when(n == 0)
        def _():
            kf[0:SPAN] = jnp.zeros((SPAN, DH), F32)
            vf[0:SPAN] = jnp.zeros((SPAN, DH), F32)

        c, sa, sb = c_ref[...], sa_ref[...], sb_ref[...]
        q_rot = _rope(q_ref[0].astype(F32), c, sa, sb).astype(BF16)
        k_rot = _rope(k_ref[0].astype(F32), c, sa, sb).astype(BF16)
        qro_ref[...] = q_rot
        kro_ref[...] = k_rot
        qr[...] = q_rot.astype(F32)
        kf[SPAN:] = k_rot.astype(F32)
        vf[SPAN:] = v_ref[0].astype(F32)
        has_prev = n > 0

        for gi, dil in enumerate(DILATIONS):
            def trip(t, carry, gi=gi, dil=dil):
                qsls, ksls, firsts = [], [], []
                for b in range(UNIT_BATCH_FWD):
                    qsl, ksl, first = _unit_slices(gi, t * UNIT_BATCH_FWD + b)
                    qsls.append(qsl)
                    ksls.append(ksl)
                    firsts.append(first)
                qb = jnp.stack([qr[qsl, :].astype(BF16) for qsl in qsls])
                kb = jnp.stack([kf[ksl, :].astype(BF16) for ksl in ksls])
                vb = jnp.stack([vf[ksl, :].astype(BF16) for ksl in ksls])
                s_all = _bdot_nt(qb, kb)
                prs = []
                for b in range(UNIT_BATCH_FWD):
                    sc = jnp.where(_band_mask(firsts[b], has_prev), s_all[b] * SCALE, NEG)
                    m = jnp.max(sc, axis=-1, keepdims=True)
                    pr = jnp.exp(sc - m)
                    l = jnp.sum(pr, axis=-1, keepdims=True)
                    mm[gi, qsls[b], :] = jnp.broadcast_to(m, (UB, DH))
                    ll[gi, qsls[b], :] = jnp.broadcast_to(l, (UB, DH))
                    prs.append(pr.astype(BF16))
                o_all = _bdot(jnp.stack(prs), vb)
                for b in range(UNIT_BATCH_FWD):
                    acc[gi, qsls[b], :] = o_all[b]
                return carry

            lax.fori_loop(0, nunit // UNIT_BATCH_FWD, trip, 0)

        m_all =jnp.maximum(jnp.maximum(mm[0], mm[1]), mm[2])
        num = jnp.zeros((SPAN, DH), F32)
        den = jnp.zeros((SPAN, DH), F32)
        for gi in range(3):
            w = jnp.exp(mm[gi] - m_all)
            num = num + w * acc[gi]
            den = den + w * ll[gi]
        o = num / den
        o_ref[...] = o
        lse_ref[...] = m_all + jnp.log(den)
        z = z_ref[0].astype(F32)
        ga_ref[...] = (o * (z * _sigmoid(z))).astype(BF16)
        kf[0:SPAN] = kf[SPAN:]
        vf[0:SPAN] = vf[SPAN:]

    blk = lambda piece, off: pl.BlockSpec((1, SPAN, DH), lambda hh, n: (piece, n, off + hh))
    tab = pl.BlockSpec((SPAN, DH), lambda hh, n: (n, 0))
    outb = pl.BlockSpec((SPAN, DH), lambda hh, n: (n, hh))
    any_ = pl.BlockSpec(memory_space=pl.ANY)
    return pl.pallas_call(
        body, name="attn_fwd", grid=(H, ns),
        in_specs=[blk(1, 0), blk(1, H), blk(2, 0), blk(2, H), tab, tab, tab, any_],
        out_specs=(outb, outb, outb, outb, outb, any_),
        out_shape=(jax.ShapeDtypeStruct((s, D), F32), jax.ShapeDtypeStruct((s, D), F32),
                   jax.ShapeDtypeStruct((s, D), BF16), jax.ShapeDtypeStruct((s, D), BF16),
                   jax.ShapeDtypeStruct((s, D), BF16), jax.ShapeDtypeStruct((4,) + w3sh.shape, w3sh.dtype)),
        scratch_shapes=[pltpu.VMEM((SPAN, DH), F32), pltpu.VMEM((2 * SPAN, DH), F32), pltpu.VMEM((2 * SPAN, DH), F32),
                        pltpu.VMEM((3, SPAN, DH), F32), pltpu.VMEM((3, SPAN, DH), F32), pltpu.VMEM((3, SPAN, DH), F32),
                        pltpu.SemaphoreType.DMA((6,)), pltpu.SemaphoreType.DMA((6,))],
        compiler_params=_params(("arbitrary", "arbitrary")),
    )(p, p, p, p, rc, rsa, rsb, w3sh)


def _attn_bwd(p, q_rot, k_rot, o, lse, dga, rc, rsa, rsb, packs):
    s = p.shape[1]
    ns = s // SPAN
    nunit = SPAN // UB
    npk = len(packs)

    def body(*refs):
        (q_ref, k_ref, kp_ref, v_ref, vp_ref, z_ref, c_ref, sa_ref, sb_ref, o_ref, lse_ref, dg_ref) = refs[:12]
        pk_refs = refs[12:12 + npk]
        dq_ref, dk_ref, dv_ref, dz_ref = refs[12 + npk:16 + npk]
        red_refs = refs[16 + npk:16 + 2 * npk]
        qr, kf, vf, dof, dlt, dqa, dkf, dvf = refs[16 + 2 * npk:24 + 2 * npk]
        rbufs = refs[24 + 2 * npk:24 + 3 * npk]
        sums = refs[24 + 3 * npk:24 + 4 * npk]
        ar_sems = refs[24 + 4 * npk:]
        hh, step = pl.program_id(0), pl.program_id(1)
        n = ns - 1 - step
        has_prev = n > 0
        for phase, at_head, at_step in ((0, 0, 0), (1, H // 2, 0), (2, H - 1, ns - 1)):
            @pl.when((hh == at_head) & (step == at_step))
            def _(phase=phase):
                _allreduce_phase(phase, pk_refs, sums, rbufs, *ar_sems, out_refs=red_refs)

        @pl.when(step == 0)
        def _():
            dkf[...] = jnp.zeros_like(dkf)
            dvf[...] = jnp.zeros_like(dvf)

        @pl.when(step > 0)
        def _():
            dkf[SPAN:] = dkf[0:SPAN]
            dvf[SPAN:] = dvf[0:SPAN]
            dkf[0:SPAN] = jnp.zeros((SPAN, DH), F32)
            dvf[0:SPAN] = jnp.zeros((SPAN, DH), F32)

        c, sa, sb = c_ref[...], sa_ref[...], sb_ref[...]
        qr[...] = q_ref[...].astype(F32)
        kf[SPAN:] = k_ref[...].astype(F32)
        vf[SPAN:] = v_ref[0].astype(F32)
        kf[0:SPAN] = jnp.where(has_prev, kp_ref[...].astype(F32), 0.0)
        vf[0:SPAN] = jnp.where(has_prev, vp_ref[0].astype(F32), 0.0)
        z = z_ref[0].astype(F32)
        sz = _sigmoid(z)
        dg = dg_ref[...]
        ov = o_ref[...]
        do = dg * (z * sz)
        dz_ref[...] = (dg * ov * (sz * (1.0 + z * (1.0 - sz)))).astype(BF16)
        dof[...] = do
        dlt[...] = jnp.dot(do * ov, jnp.ones((DH, DH), F32), preferred_element_type=F32, precision=HIGHEST)
        dqa[...] = jnp.zeros_like(dqa)

        for gi, dil in enumerate(DILATIONS):
            def trip(t, carry, gi=gi, dil=dil):
                qsls, ksls, firsts = [], [], []
                for b in range(UNIT_BATCH):
                    qsl, ksl, first = _unit_slices(gi, t * UNIT_BATCH + b)
                    qsls.append(qsl)
                    ksls.append(ksl)
                    firsts.append(first)
                qb = jnp.stack([qr[qsl, :].astype(BF16) for qsl in qsls])
                kb = jnp.stack([kf[ksl, :].astype(BF16) for ksl in ksls])
                vb = jnp.stack([vf[ksl, :].astype(BF16) for ksl in ksls])
                dob = jnp.stack([dof[qsl, :].astype(BF16) for qsl in qsls])
                s_all = _bdot_nt(qb, kb)
                dp_all = _bdot_nt(dob, vb)
                prs, dss = [], []
                for b in range(UNIT_BATCH):
                    lse_b = lse_ref[qsls[b], :]
                    dl_b = dlt[qsls[b], :]
                    pr = jnp.exp(s_all[b] * SCALE - jnp.concatenate([lse_b, lse_b], axis=1))
                    pr = jnp.where(_band_mask(firsts[b], has_prev), pr, 0.0)
                    prs.append(pr.astype(BF16))
                    dss.append((pr * (dp_all[b] - jnp.concatenate([dl_b, dl_b], axis=1)) * SCALE).astype(BF16))
                ds_all = jnp.stack(dss)
                dv_all = _bdot_tn(jnp.stack(prs), dob)
                dq_all = _bdot(ds_all, kb)
                dk_all = _bdot_tn(ds_all, qb)
                for b in range(UNIT_BATCH):
                    dvf[ksls[b], :] += dv_all[b]
                    dqa[qsls[b], :] += dq_all[b]
                    dkf[ksls[b], :] += dk_all[b]
                return carry

            lax.fori_loop(0, nunit // UNIT_BATCH, trip, 0)

        dq_ref[...] = _rope_bwd(dqa[...], c, sa, sb).astype(BF16)
        dk_ref[...] = _rope_bwd(dkf[SPAN:], c, sa, sb).astype(BF16)
        dv_ref[...] = dvf[SPAN:].astype(BF16)

    rn = lambda n: ns - 1 - n
    pn = lambda n: jnp.maximum(ns - 2 - n, 0)
    blk = lambda piece, off: pl.BlockSpec((1, SPAN, DH), lambda hh, n: (piece, rn(n), off + hh))
    blkp = lambda piece, off: pl.BlockSpec((1, SPAN, DH), lambda hh, n: (piece, pn(n), off + hh))
    tab = pl.BlockSpec((SPAN, DH), lambda hh, n: (rn(n), 0))
    io = pl.BlockSpec((SPAN, DH), lambda hh, n: (rn(n), hh))
    iop = pl.BlockSpec((SPAN, DH), lambda hh, n: (pn(n), hh))
    vm = pl.BlockSpec(memory_space=pltpu.VMEM)
    outs = pl.pallas_call(
        body, name="attn_bwd", grid=(H, ns),
        in_specs=[io, io, iop, blk(2, 0), blkp(2, 0), blk(2, H), tab, tab, tab, io, io, io] + [vm] * npk,
        out_specs=(io, io, io, io) + (vm,) * npk,
        out_shape=tuple(jax.ShapeDtypeStruct((s, D), BF16) for _ in range(4)) +
                  tuple(jax.ShapeDtypeStruct(a.shape, F32) for a in packs),
        scratch_shapes=[pltpu.VMEM((SPAN, DH), F32), pltpu.VMEM((2 * SPAN, DH), F32), pltpu.VMEM((2 * SPAN, DH), F32),
                        pltpu.VMEM((SPAN, DH), F32), pltpu.VMEM((SPAN, DH), F32), pltpu.VMEM((SPAN, DH), F32),
                        pltpu.VMEM((2 * SPAN, DH), F32), pltpu.VMEM((2 * SPAN, DH), F32)] +
                       [pltpu.VMEM((NDEV, a.shape[0] // NDEV, a.shape[1]), F32) for a in packs] +
                       [pltpu.VMEM(a.shape, F32) for a in packs] +
                       [pltpu.SemaphoreType.DMA((7 * npk,)) for _ in range(4)],
        compiler_params=_params(("arbitrary", "arbitrary")),
    )(q_rot, k_rot, k_rot, p, p, p, rc, rsa, rsb, o, lse, dga, *packs)
    return outs[:4], outs[4:]


def _tail(gr, ga, p, x, tgt, w3, b_gate, gate, g_final, tm=256):
    s = x.shape[0]
    nt = s // tm

    def body(gr_ref, ga_ref, pr_ref, pa_ref, x_ref, t_ref, bg_ref, gate_ref, gf_ref, w_hbm,
             dgr_ref, dga_ref, dc_ref, dx2_ref, vec_ref, go_hbm, w_s, acc_s, sem):
        i = pl.program_id(0)

        @pl.when(i == 0)
        def _():
            cp = pltpu.make_async_copy(w_hbm, w_s, sem.at[12])
            cp.start()
            acc_s[...] = jnp.zeros_like(acc_s)
            vec_ref[...] = jnp.zeros_like(vec_ref)
            cp.wait()

        grb = gr_ref[...]
        gab = ga_ref[...]
        bg = bg_ref[...]
        gate_v = gate_ref[...]
        gf = gf_ref[...]
        y_r = _dot(grb, w_s[0])
        y_a = _dot(gab, w_s[1])
        sr = _sigmoid(pr_ref[0].astype(F32) + bg[:, :D])
        sa = _sigmoid(pa_ref[0].astype(F32) + bg[:, D:])
        mb = (sr * y_r + sa * y_a).astype(BF16)
        u = _dot(mb, w_s[2])
        x2 = x_ref[...] + gate_v * u
        rstd = lax.rsqrt(jnp.mean(x2 * x2, axis=-1, keepdims=True) + EPS)
        xh = x2 * rstd
        e = xh * gf - t_ref[...]
        dy = e * (1.0 / D)
        dyg = dy * gf
        dx2 = rstd * (dyg - xh * jnp.mean(dyg * xh, axis=-1, keepdims=True))
        dx2_ref[...] = dx2
        dub = (dx2 * gate_v).astype(BF16)
        dm = _dot_nt(dub, w_s[2])
        dyr = (dm * sr).astype(BF16)
        dya = (dm * sa).astype(BF16)
        dpr = dm * y_r * (sr * (1.0 - sr))
        dpa = dm * y_a * (sa * (1.0 - sa))
        dc_ref[:, :D] = dpr.astype(BF16)
        dc_ref[:, D:] = dpa.astype(BF16)
        dgr_ref[...] = _dot_nt(dyr, w_s[0])
        dga_ref[...] = _dot_nt(dya, w_s[1])
        acc_s[0] += _dot_tn(grb, dyr)
        acc_s[1] += _dot_tn(gab, dya)
        acc_s[2] += _dot_tn(mb, dub)
        vec_ref[0:1, :] += _colsum(dy * xh)
        vec_ref[1:2, :] += _colsum(dx2 * u)
        vec_ref[2:3, :] += _colsum(dpr)
        vec_ref[3:4, :] += _colsum(dpa)
        vec_ref[4:5, :] += _colsum(e * e)

        @pl.when(i == nt - 1)
        def _():
            vec_ref[4:5, :] = jnp.broadcast_to(jnp.sum(vec_ref[4:5, :]) * (0.5 / D), (1, D))
            cps = []
            for w in range(3):
                for j in range(4):
                    cps.append(pltpu.make_async_copy(acc_s.at[w, pl.ds(256 * j, 256)],
                                                     go_hbm.at[j, pl.ds(256 * w, 256)], sem.at[4 * w + j]))
            for cp in cps:
                cp.start()
            for cp in cps:
                cp.wait()

    rowt = lambda i: (i, 0)
    row = lambda w: pl.BlockSpec((1, w), lambda i: (0, 0))
    any_ = pl.BlockSpec(memory_space=pl.ANY)
    return pl.pallas_call(
        body, name="tail", grid=(nt,),
        in_specs=[pl.BlockSpec((tm, D), rowt), pl.BlockSpec((tm, D), rowt),
                  pl.BlockSpec((1, tm, D), lambda i: (3, i, 0)), pl.BlockSpec((1, tm, D), lambda i: (3, i, 1)),
                  pl.BlockSpec((tm, D), rowt), pl.BlockSpec((tm, D), rowt),
                  row(2 * D), row(D), row(D), any_],
        out_specs=(pl.BlockSpec((tm, D), rowt), pl.BlockSpec((tm, D), rowt), pl.BlockSpec((tm, 2 * D), rowt),
                   pl.BlockSpec((tm, D), rowt), pl.BlockSpec((8, D), lambda i: (0, 0)), any_),
        out_shape=(jax.ShapeDtypeStruct((s, D), F32), jax.ShapeDtypeStruct((s, D), F32),
                   jax.ShapeDtypeStruct((s, 2 * D), BF16), jax.ShapeDtypeStruct((s, D), F32),
                   jax.ShapeDtypeStruct((8, D), F32), jax.ShapeDtypeStruct((4, 768, D), F32)),
        scratch_shapes=[pltpu.VMEM((3, D, D), BF16), pltpu.VMEM((3, D, D), F32), pltpu.SemaphoreType.DMA((13,))],
        compiler_params=_params(("arbitrary",)),
    )(gr, ga, p, p, x, tgt, b_gate, gate, g_final, w3)


def _pieces_steps(pieces):
    out, s0 = [], 0
    for a in pieces:
        n = a.shape[1] // D
        out.append((s0, n))
        s0 += n
    return out, s0


def _inproj_bwd_x(pieces, wg, wsh, x, dx2, gn, scale, sums, tm=512):
    s = x.shape[0]
    np_ = len(pieces)
    na = len(sums)
    ni = s // tm
    groups, cur, width = [], [], 0
    for t, a in enumerate(pieces):
        cur.append(t)
        width += a.shape[1]
        if width == PW:
            groups.append(cur)
            cur, width = [], 0
    assert len(groups) == 4 and not cur

    def body(*refs):
        d_refs = refs[:np_]
        w_hbm, wsh_hbm, x_ref, dx2_ref, gn_ref, sc_ref = refs[np_:np_ + 6]
        q_refs = refs[np_ + 6:np_ + 6 + na]
        gx_ref, vec_ref = refs[np_ + 6 + na:np_ + 8 + na]
        r_refs = refs[np_ + 8 + na:np_ + 8 + 2 * na]
        w_s, wsem, ss, rs = refs[np_ + 8 + 2 * na:]
        i = pl.program_id(0)
        my_shard = 2 * lax.axis_index("x") + lax.axis_index("y")

        def scatter_copies():
            cx, cy, cc = _coords()
            j = 2 * cx + cy
            cps = []
            for t, (q, r) in enumerate(zip(q_refs, r_refs)):
                for e, (kx, ky) in enumerate(((1, 0), (0, 1), (1, 1))):
                    cps.append(_rcopy(q.at[j ^ (2 * kx + ky)], r.at[e], ss.at[3 * t + e], rs.at[3 * t + e],
                                      (_flip(cx, kx), _flip(cy, ky), cc)))
            return cps

        def w_copy(pc):
            return pltpu.make_async_copy(w_hbm.at[pc, pl.ds(0, D), :], w_s.at[pc], wsem.at[pc])

        for pc in range(4):
            @pl.when((i == 0) & (my_shard != pc))
            def _(pc=pc):
                w_copy(pc).start()

            @pl.when((i == 0) & (my_shard == pc))
            def _(pc=pc):
                pltpu.make_async_copy(wsh_hbm, w_s.at[pc], wsem.at[pc]).start()

        @pl.when(i == 0)
        def _():
            vec_ref[...] = jnp.zeros_like(vec_ref)
            for cp in scatter_copies():
                cp.start()

        dh = None
        for pc, group in enumerate(groups):
            @pl.when(i == 0)
            def _(pc=pc):
                w_copy(pc).wait()

            tiles = [d_refs[t][...] for t in group]
            lhs = tiles[0] if len(tiles) == 1 else jnp.concatenate(tiles, axis=1)
            part = _dot_nt(lhs, w_s[pc])
            dh = part if dh is None else dh + part

        xt = x_ref[...]
        rstd = lax.rsqrt(jnp.mean(xt * xt, axis=-1, keepdims=True) + EPS)
        xh = xt * rstd
        gn_v = gn_ref[...]
        sc1 = 1.0 + sc_ref[...]
        dhx = dh * xh
        vec_ref[0:1, :] += _colsum(dh)
        vec_ref[1:2, :] += _colsum(dhx) * gn_v
        vec_ref[2:3, :] += _colsum(dhx) * sc1
        dxh = dh * (gn_v * sc1)
        gx_ref[...] = rstd * (dxh - xh * jnp.mean(dxh * xh, axis=-1, keepdims=True)) + dx2_ref[...]

        @pl.when(i == ni - 1)
        def _():
            for cp in scatter_copies():
                cp.wait()

    rowt = lambda i: (i, 0)
    row = pl.BlockSpec((1, D), lambda i: (0, 0))
    any_ = pl.BlockSpec(memory_space=pl.ANY)
    outs = pl.pallas_call(
        body, name="inproj_bwd_x", grid=(ni,),
        in_specs=[pl.BlockSpec((tm, a.shape[1]), rowt) for a in pieces] +
                 [any_, any_, pl.BlockSpec((tm, D), rowt), pl.BlockSpec((tm, D), rowt), row, row] + [any_] * na,
        out_specs=(pl.BlockSpec((tm, D), rowt), pl.BlockSpec((8, D), lambda i: (0, 0))) + (any_,) * na,
        out_shape=(jax.ShapeDtypeStruct((s, D), F32), jax.ShapeDtypeStruct((8, D), F32)) +
                  tuple(jax.ShapeDtypeStruct((3,) + q.shape[1:], q.dtype) for q in sums),
        scratch_shapes=[pltpu.VMEM((4, D, PW), BF16), pltpu.SemaphoreType.DMA((4,)),
                        pltpu.SemaphoreType.DMA((3 * na,)), pltpu.SemaphoreType.DMA((3 * na,))],
        compiler_params=_params(("arbitrary",)),
    )(*pieces, wg, wsh, x, dx2, gn, scale, *sums)
    return outs[0], outs[1], outs[2:]


def _inproj_bwd_w(pieces, hbf, g_out, tk=1024):
    s = hbf.shape[0]
    steps, nk = _pieces_steps(pieces)
    npc = PW // D
    ns = s // tk
    np_ = len(pieces)
    hr = D // 2
    ohr = g_out.shape[1] // 2
    ocr = _chunk_rows(g_out)
    ochunks = [(j, r0) for j in range(g_out.shape[0]) for r0 in range(0, ohr, ocr)]
    noc = len(ochunks)

    def body(*refs):
        d_refs = refs[:np_]
        h_ref, go_hbm, g_ref, rb_hbm, rbo_hbm, stage, ss, rs, oss, ors = refs[np_:]
        cb, k = pl.program_id(0), pl.program_id(1)
        cx, cy, cc = _coords()
        sib = (cx, cy, 1 - cc)

        def block_copy(b):
            return _rcopy(stage.at[b % 2],
                          rb_hbm.at[b // npc, :, pl.ds(pl.multiple_of((b % npc) * D, D), D)], ss.at[b], rs.at[b], sib)

        def out_copy(e):
            j, r0 = ochunks[e]
            return _rcopy(go_hbm.at[j, pl.ds((1 - cc) * ohr + r0, ocr), :], rbo_hbm.at[j, pl.ds(r0, ocr), :],
                          oss.at[e], ors.at[e], sib)

        @pl.when((cb == 0) & (k == 0))
        def _():
            for e in range(noc):
                out_copy(e).start()

        @pl.when(k == 0)
        def _():
            g_ref[...] = jnp.zeros_like(g_ref)

        for (s0, n), d_ref in zip(steps, d_refs):
            @pl.when((cb >= s0) & (cb < s0 + n))
            def _(d_ref=d_ref):
                g_ref[0] += _dot_tn(h_ref[...], d_ref[...])

        @pl.when((k == ns - 1) & (cb > 1))
        def _():
            block_copy(cb - 2).wait_send()

        @pl.when(k == ns - 1)
        def _():
            stage[cb % 2] = g_ref[0, pl.ds(pl.multiple_of((1 - cc) * hr, hr), hr), :]
            block_copy(cb).start()

        @pl.when((k == ns - 1) & (cb == nk - 1))
        def _():
            block_copy(nk - 2).wait_send()
            block_copy(nk - 1).wait_send()
            for b in range(nk):
                block_copy(b).wait_recv()
            for e in range(noc):
                out_copy(e).wait_recv()
                out_copy(e).wait_send()

    def piece_spec(s0, n):
        def imap(cb, k):
            active = (cb >= s0) & (cb < s0 + n)
            return (jnp.where(active, k, 0), jnp.clip(cb - s0, 0, n - 1))
        return pl.BlockSpec((tk, D), imap)

    any_ = pl.BlockSpec(memory_space=pl.ANY)
    return pl.pallas_call(
        body, name="inproj_bwd_w", grid=(nk, ns),
        in_specs=[piece_spec(s0, n) for s0, n in steps] + [pl.BlockSpec((tk, D), lambda cb, k: (k, 0)), any_],
        out_specs=(pl.BlockSpec((1, D, D), lambda cb, k: (cb // npc, 0, cb % npc)), any_, any_),
        out_shape=(jax.ShapeDtypeStruct((4, D, PW), F32), jax.ShapeDtypeStruct((4, hr, PW), F32),
                   jax.ShapeDtypeStruct((g_out.shape[0], ohr, g_out.shape[2]), F32)),
        scratch_shapes=[pltpu.VMEM((2, hr, D), F32), pltpu.SemaphoreType.DMA((nk,)), pltpu.SemaphoreType.DMA((nk,)),
                        pltpu.SemaphoreType.DMA((noc,)), pltpu.SemaphoreType.DMA((noc,))],
        compiler_params=_params(("arbitrary", "arbitrary")),
    )(*pieces, hbf, g_out)


D2D_CHUNK_BYTES = 512 * 1024


def _chunk_rows(a):
    return max(8, D2D_CHUNK_BYTES // (a.shape[-1] * a.dtype.itemsize))


def _pair_swap(arrs):
    na = len(arrs)
    chunks = []
    for t, a in enumerate(arrs):
        cr = _chunk_rows(a)
        chunks += [(t, r0, cr) for r0 in range(0, a.shape[0], cr)]
    nch = len(chunks)

    def body(*refs):
        a_refs = refs[:na]
        o_refs = refs[na:2 * na]
        ss, rs = refs[2 * na:]
        x, y, c = _coords()
        sib = (x, y, 1 - c)
        rcs = []
        for n, (t, r0, cr) in enumerate(chunks):
            rows = pl.ds(r0, cr)
            rc = _rcopy(a_refs[t].at[rows, :], o_refs[t].at[rows, :], ss.at[n], rs.at[n], sib)
            rc.start()
            rcs.append(rc)
        for rc in rcs:
            rc.wait_recv()
        for rc in rcs:
            rc.wait_send()

    any_ = pl.BlockSpec(memory_space=pl.ANY)
    return pl.pallas_call(
        body, name="pair_swap",
        out_shape=tuple(jax.ShapeDtypeStruct(a.shape, a.dtype) for a in arrs),
        in_specs=[any_] * na, out_specs=tuple([any_] * na),
        scratch_shapes=[pltpu.SemaphoreType.DMA((nch,)), pltpu.SemaphoreType.DMA((nch,))],
        compiler_params=_params(),
    )(*arrs)


def _add_half(full, rb, core, tr):
    n, r, cdim = full.shape
    nb = r // 2 // tr

    def body(c_ref, a_ref, b_ref, ob_ref):
        ob_ref[...] = (a_ref[...] + b_ref[...]).astype(BF16)

    mine = pl.BlockSpec((1, tr, cdim), lambda i, j, c_ref: (i, c_ref[0] * nb + j, 0))
    spec = pl.BlockSpec((1, tr, cdim), lambda i, j, c_ref: (i, j, 0))
    return pl.pallas_call(
        body, name="add_half",
        grid_spec=pltpu.PrefetchScalarGridSpec(num_scalar_prefetch=1, grid=(n, nb), in_specs=[mine, spec],
                                               out_specs=spec),
        out_shape=jax.ShapeDtypeStruct(rb.shape, BF16),
        compiler_params=_params(("parallel", "parallel")),
    )(core, full, rb)


def _sum_slots(full, rb, r3, shard_core, tr):
    _, hr, cdim = rb.shape
    nb = hr // tr

    def body(jc_ref, a_ref, b_ref, r_ref, o_ref):
        own = a_ref[0] + b_ref[0]
        o_ref[...] = ((own + r_ref[0].astype(F32)) + r_ref[1].astype(F32)) + r_ref[2].astype(F32)

    return pl.pallas_call(
        body, name="sum_slots",
        grid_spec=pltpu.PrefetchScalarGridSpec(
            num_scalar_prefetch=1, grid=(nb,),
            in_specs=[pl.BlockSpec((1, tr, cdim), lambda i, jc: (jc[0], jc[1] * nb + i, 0)),
                      pl.BlockSpec((1, tr, cdim), lambda i, jc: (jc[0], i, 0)),
                      pl.BlockSpec((3, tr, cdim), lambda i, jc: (0, i, 0))],
            out_specs=pl.BlockSpec((tr, cdim), lambda i, jc: (i, 0))),
        out_shape=jax.ShapeDtypeStruct((hr, cdim), F32),
        compiler_params=_params(("parallel",)),
    )(shard_core, full, rb, r3)


def _allreduce_phase(phase, p_refs, o_refs, rbufs, s1, r1, s2, r2, out_refs=None):
    me = _my_index()

    def chunk(t, d):
        ch = p_refs[t].shape[0] // NDEV
        return pl.ds(pl.multiple_of(d * ch, 8), ch)

    def scatter(t, k):
        e = 7 * t + k - 1
        return _rcopy(p_refs[t].at[chunk(t, me ^ k)], rbufs[t].at[me], s1.at[e], r1.at[e], _peer(k))

    def gather(t, k):
        e = 7 * t + k - 1
        return _rcopy(o_refs[t].at[chunk(t, me)], o_refs[t].at[chunk(t, me)], s2.at[e], r2.at[e], _peer(k))

    for t in range(len(p_refs)):
        if phase == 0:
            for k in range(1, NDEV):
                scatter(t, k).start()
            rbufs[t][me] = p_refs[t][chunk(t, me), :]
        elif phase == 1:
            for k in range(1, NDEV):
                e = 7 * t + k - 1
                _rcopy(p_refs[t].at[chunk(t, me)], rbufs[t].at[me ^ k], s1.at[e], r1.at[e], _peer(k)).wait_recv()
            tot = rbufs[t][0]
            for d in range(1, NDEV):
                tot = tot + rbufs[t][d]
            o_refs[t][chunk(t, me), :] = tot
            for k in range(1, NDEV):
                gather(t, k).start()
        else:
            for k in range(1, NDEV):
                e = 7 * t + k - 1
                _rcopy(o_refs[t].at[chunk(t, me)], o_refs[t].at[chunk(t, me ^ k)], s2.at[e], r2.at[e],
                       _peer(k)).wait_recv()
            for k in range(1, NDEV):
                scatter(t, k).wait_send()
                gather(t, k).wait_send()
            if out_refs is not None:
                out_refs[t][...] = o_refs[t][...]


def _adamw_update(w, g, m, v):
    nm = B1 * m + (1.0 - B1) * g
    nv = B2 * v + (1.0 - B2) * (g * g)
    m_hat = nm / (1.0 - B1 ** STEP)
    v_hat = nv / (1.0 - B2 ** STEP)
    return -LR * (m_hat / (jnp.sqrt(v_hat) + ADAM_EPS) + WD * w), nm, nv


def _adamw(w, g, m, v, tr):
    r, cdim = w.shape

    def body(w_ref, g_ref, m_ref, v_ref, d_ref, nm_ref, nv_ref):
        d_ref[...], nm_ref[...], nv_ref[...] = _adamw_update(w_ref[...], g_ref[...], m_ref[...], v_ref[...])

    spec = pl.BlockSpec((tr, cdim), lambda i: (i, 0))
    sd = jax.ShapeDtypeStruct((r, cdim), F32)
    return pl.pallas_call(
        body, name="adamw", grid=(r // tr,), in_specs=[spec] * 4, out_specs=(spec,) * 3, out_shape=(sd,) * 3,
        compiler_params=_params(("parallel",)),
    )(w, g, m, v)


def _adamw_halves(w, mine, theirs, m, v, core, tr):
    r, cdim = w.shape
    nbh = r // 2 // tr

    def body(c_ref, w_ref, a_ref, b_ref, m_ref, v_ref, g_ref, d_ref, nm_ref, nv_ref):
        is_mine = pl.program_id(0) // nbh == c_ref[0]
        g = jnp.where(is_mine, a_ref[...], b_ref[...])
        g_ref[...] = g
        d_ref[...], nm_ref[...], nv_ref[...] = _adamw_update(w_ref[...], g, m_ref[...], v_ref[...])

    spec = pl.BlockSpec((tr, cdim), lambda i, c: (i, 0))
    half = lambda own: pl.BlockSpec(
        (tr, cdim), lambda i, c: (jnp.clip(i - (c[0] if own else 1 - c[0]) * nbh, 0, nbh - 1), 0))
    sd = jax.ShapeDtypeStruct((r, cdim), F32)
    return pl.pallas_call(
        body, name="adamw_halves",
        grid_spec=pltpu.PrefetchScalarGridSpec(num_scalar_prefetch=1, grid=(r // tr,),
                                               in_specs=[spec, half(True), half(False), spec, spec],
                                               out_specs=(spec,) * 4),
        out_shape=(sd,) * 4,
        compiler_params=_params(("parallel",)),
    )(core, w, mine, theirs, m, v)


V_B_GATE, V_CONV_B, V_LAM, V_G_FINAL, V_CONV_W, V_LOSS, V_ROWS = 0, 2, 3, 4, 5, 9, 64
M_W_A, M_W_X, M_B_A, M_B_X, M_ROWS = 0, H * DH, 2 * H * DH, 2 * H * DH + H, 2112
SMALL = ("g_norm", "b_mod", "b_gate", "conv_b", "lam", "g_final", "conv_w", "w_a", "w_x", "b_a", "b_x")


def _adamw_small(redv, redm, g_conv, g_gnorm, g_bmod, wmv):
    def grad(name, rv, rm, gc, gg, gb):
        if name == "g_norm":
            return gg[...]
        if name == "b_mod":
            return gb[...]
        if name == "b_gate":
            return jnp.concatenate([rv[V_B_GATE + t:V_B_GATE + t + 1, :] for t in range(2)], axis=1)
        if name == "conv_b":
            return rv[V_CONV_B:V_CONV_B + 1, :]
        if name == "lam":
            return rv[V_LAM:V_LAM + 1, :]
        if name == "g_final":
            return rv[V_G_FINAL:V_G_FINAL + 1, :]
        if name == "conv_w":
            return gc[...]
        if name == "w_a":
            return rm[M_W_A:M_W_A + H * DH, :]
        if name == "w_x":
            return rm[M_W_X:M_W_X + H * DH, :]
        if name == "b_a":
            return rm[M_B_A:M_B_A + H, :]
        return rm[M_B_X:M_B_X + H, :]

    n = len(SMALL)

    def body(*refs):
        rv, rm, gc, gg, gb = refs[:5]
        ins, outs = refs[5:5 + 3 * n], refs[5 + 3 * n:]
        for t, name in enumerate(SMALL):
            w_ref, m_ref, v_ref = ins[3 * t:3 * t + 3]
            g_out, d_out, m_out, v_out = outs[4 * t:4 * t + 4]
            g = grad(name, rv, rm, gc, gg, gb)
            g_out[...] = g
            d_out[...], m_out[...], v_out[...] = _adamw_update(w_ref[...], g, m_ref[...], v_ref[...])

    vm = pl.BlockSpec(memory_space=pltpu.VMEM)
    flat = [a for name in SMALL for a in wmv[name]]
    shapes = [jax.ShapeDtypeStruct(wmv[name][0].shape, F32) for name in SMALL for _ in range(4)]
    outs = pl.pallas_call(
        body, name="adamw_small", out_shape=tuple(shapes),
        in_specs=[vm] * (5 + len(flat)), out_specs=tuple([vm] * len(shapes)),
        compiler_params=_params(),
    )(redv, redm, g_conv, g_gnorm, g_bmod, *flat)
    return {name: outs[4 * t:4 * t + 4] for t, name in enumerate(SMALL)}


def _rope_tables(positions):
    inv_freq = ROPE_THETA ** (-jnp.arange(0, ROT, 2, dtype=F32) / ROT)
    ang = positions.astype(F32)[:, None] * inv_freq
    cos, sin = lax.optimization_barrier((jnp.cos(ang), jnp.sin(ang)))
    n = positions.shape[0]
    half = ROT // 2
    rc = jnp.concatenate([cos, cos, jnp.ones((n, DH - ROT), F32)], axis=1)
    rsa = jnp.concatenate([-sin, jnp.zeros((n, DH - half), F32)], axis=1)
    rsb = jnp.concatenate([jnp.zeros((n, half), F32), sin, jnp.zeros((n, DH - ROT), F32)], axis=1)
    return rc, rsa, rsb


def kernel(x, c, positions, g_norm, w_mod, b_mod, w_in, b_gate, conv_w, conv_b, w_a, b_a, w_x, b_x, lam, w_out_rnn, w_out_attn, w_o, g_final, loss_target, m_g_norm, m_w_mod, m_b_mod, m_w_in, m_b_gate, m_conv_w, m_conv_b, m_w_a, m_b_a, m_w_x, m_b_x, m_lam, m_w_out_rnn, m_w_out_attn, m_w_o, m_g_final, v_g_norm, v_w_mod, v_b_mod, v_w_in, v_b_gate, v_conv_w, v_conv_b, v_w_a, v_b_a, v_w_x, v_b_x, v_lam, v_w_out_rnn, v_w_out_attn, v_w_o, v_g_final):
    s = x.shape[1]
    xi = lax.axis_index("x")
    yi = lax.axis_index("y")
    ci = lax.axis_index("c")
    shard = 2 * xi + yi
    x2d = x[0]
    tgt = loss_target[0]
    pos = positions[0]

    c_all, mod4, conv_all = _mod_fwd(c, w_mod[0], b_mod.reshape(4, 1, 768), conv_w[0])
    mod = mod4.reshape(1, 3 * D)
    shift, scale, gate = mod[:, :D], mod[:, D:2 * D], mod[:, 2 * D:]
    w3sh = jnp.concatenate([w_out_rnn[0], w_out_attn[0], w_o[0]], axis=0).astype(BF16)
    wsh = w_in[0].astype(BF16)
    conv_full = conv_all[0::2].transpose(1, 0, 2).reshape(4, D)

    order = jnp.stack([shard, shard ^ 2, shard ^ 1, shard ^ 3]).astype(jnp.int32)
    p, hbf, wg = _gather_norm_inproj(x2d, g_norm, shift, scale, wsh, order)
    rc, rsa, rsb = _rope_tables(pos)
    pos_col = jnp.broadcast_to((pos == 0).astype(F32)[:, None], (s, DH))
    b_a3, b_x3 = b_a.reshape(H, 1, DH), b_x.reshape(H, 1, DH)
    hr, gr = _rnn_fwd(p, pos_col, conv_full, conv_b, w_a[0], b_a3, w_x[0], b_x3, lam)
    o, lse, ga, q_rot, k_rot, w3g = _attn_fwd(p, rc, rsa, rsb, w3sh)
    w3g = lax.dynamic_update_slice(w3g, w3sh[None], (shard, 0, 0))
    w3 = w3g.reshape(4, 3, 256, D).transpose(1, 0, 2, 3).reshape(3, D, D)

    dgr, dga, dc, dx2, vec_t, g_out = _tail(gr, ga, p, x2d, tgt, w3, b_gate, gate, g_final.reshape(1, D))

    dxr, dzr, g_wa, g_ba, g_wx, g_bx, g_lam, g_cw, g_cb = _rnn_bwd(
        p, hr, dgr, pos_col, conv_full, conv_b, w_a[0], b_a3, w_x[0], b_x3, lam)
    vpack = jnp.concatenate([
        vec_t[2:4],
        g_cb.reshape(1, D),
        g_lam.reshape(1, D),
        vec_t[0:1],
        g_cw.transpose(1, 0, 2).reshape(4, D),
        vec_t[4:5],
        jnp.zeros((V_ROWS - 10, D), F32)], axis=0)
    mpack = jnp.concatenate([
        g_wa.reshape(H * DH, DH), g_wx.reshape(H * DH, DH), g_ba.reshape(H, DH), g_bx.reshape(H, DH),
        jnp.zeros((M_ROWS - 2 * H * DH - 2 * H, DH), F32)], axis=0)
    (dq, dk, dv, dza), (redv, redm) = _attn_bwd(p, q_rot, k_rot, o, lse, dga, rc, rsa, rsb, [vpack, mpack])

    pieces = [dxr, dzr, dq, dk, dv, dza, dc]
    g_win, rb_a, rb_b = _inproj_bwd_w(pieces, hbf, g_out)

    core = ci.reshape(1)
    shard_core = jnp.stack([shard, ci]).astype(jnp.int32)
    qh_a, qh_b = _add_half(g_win, rb_a, core, tr=256), _add_half(g_out, rb_b, core, tr=384)
    grad_x, vec_n, (r_a, r_b) = _inproj_bwd_x(pieces, wg, wsh, x2d, dx2, g_norm, scale, [qh_a, qh_b])
    f_a = _sum_slots(g_win, rb_a, r_a, shard_core, tr=256)
    f_b = _sum_slots(g_out, rb_b, r_b, shard_core, tr=384)
    s_a, s_b = _pair_swap([f_a, f_b])

    dmod_row = jnp.concatenate([vec_n[0:1], vec_n[1:2], vec_t[1:2]], axis=1)
    loss = redv[V_LOSS, 0]
    grad_w_mod, g_bmod4, g_gnorm = _mod_bwd(dmod_row.reshape(4, 1, 768), vec_n[2:3], c_all)
    g_conv_sh = lax.dynamic_slice_in_dim(redv[V_CONV_W:V_CONV_W + 4], shard * 256, 256, axis=1)

    shape2d = dict(g_norm=(1, D), b_mod=(1, 3 * D), b_gate=(1, 2 * D), conv_b=(1, D), lam=(1, D), g_final=(1, D),
                   conv_w=(4, 256), w_a=(H * DH, DH), w_x=(H * DH, DH), b_a=(H, DH), b_x=(H, DH))
    given = dict(
        g_norm=(g_norm, m_g_norm, v_g_norm), b_mod=(b_mod, m_b_mod, v_b_mod), b_gate=(b_gate, m_b_gate, v_b_gate),
        conv_b=(conv_b, m_conv_b, v_conv_b), lam=(lam, m_lam, v_lam), g_final=(g_final, m_g_final, v_g_final),
        conv_w=(conv_w, m_conv_w, v_conv_w), w_a=(w_a, m_w_a, v_w_a), w_x=(w_x, m_w_x, v_w_x),
        b_a=(b_a, m_b_a, v_b_a), b_x=(b_x, m_b_x, v_b_x))
    small = _adamw_small(redv, redm, g_conv_sh, g_gnorm, g_bmod4.reshape(1, 3 * D),
                         {n: tuple(a.reshape(shape2d[n]) for a in given[n]) for n in SMALL})

    big_in = _adamw_halves(w_in[0], f_a, s_a, m_w_in[0], v_w_in[0], core, tr=256)
    big_mod = (grad_w_mod,) + tuple(_adamw(w_mod[0], grad_w_mod, m_w_mod[0], v_w_mod[0], tr=256))
    w3f = jnp.concatenate([w_out_rnn[0], w_out_attn[0], w_o[0]], axis=0)
    m3f = jnp.concatenate([m_w_out_rnn[0], m_w_out_attn[0], m_w_o[0]], axis=0)
    v3f = jnp.concatenate([v_w_out_rnn[0], v_w_out_attn[0], v_w_o[0]], axis=0)
    big_out = _adamw_halves(w3f, f_b, s_b, m3f, v3f, core, tr=384)

    names = ["g_norm", "w_mod", "b_mod", "w_in", "b_gate", "conv_w", "conv_b", "w_a", "b_a", "w_x", "b_x", "lam",
             "w_out_rnn", "w_out_attn", "w_o", "g_final"]
    outs = []
    for idx in range(4):
        d = {n: small[n][idx].reshape(given[n][0].shape) for n in SMALL}
        d.update(w_mod=big_mod[idx][None], w_in=big_in[idx][None],
                 w_out_rnn=big_out[idx][0:256][None], w_out_attn=big_out[idx][256:512][None],
                 w_o=big_out[idx][512:768][None])
        outs.append(d)
    flat = [d[n] for d in outs for n in names]
    return (loss, grad_x[None], *flat)
```

```python
import jax
import jax.numpy as jnp
from jax import lax
from jax.experimental import pallas as pl
from jax.experimental.pallas import tpu as pltpu

F32, BF16 = jnp.float32, jnp.bfloat16
MESH = pl.DeviceIdType.MESH
HIGHEST = lax.Precision.HIGHEST

D = 1024
H = 8
DH = 128
PW = 2048
EPS = 1e-6
LRU_C = 8.0
SCALE = DH ** -0.5
NEG = -1e30
SPAN = 2048
UB = 128
DILATIONS = (1, 4, 16)
UNIT_BATCH = 16
UNIT_BATCH_FWD = 8
ROPE_THETA = 500000.0
ROT = 32

LR, B1, B2, ADAM_EPS, WD, STEP = 0.001, 0.9, 0.999, 1e-08, 0.01, 10

NDEV = 8


def _params(sem=None, vmem_mb=56):
    return pltpu.CompilerParams(dimension_semantics=sem, vmem_limit_bytes=vmem_mb * 2 ** 20)


def _coords():
    return lax.axis_index("x"), lax.axis_index("y"), lax.axis_index("c")


def _flip(v, bit):
    return 1 - v if bit else v


def _peer(k):
    x, y, c = _coords()
    return (_flip(x, (k >> 2) & 1), _flip(y, (k >> 1) & 1), _flip(c, k & 1))


def _my_index():
    x, y, c = _coords()
    return 4 * x + 2 * y + c


def _rcopy(src, dst, ssem, rsem, dev):
    return pltpu.make_async_remote_copy(src_ref=src, dst_ref=dst, send_sem=ssem, recv_sem=rsem,
                                        device_id=dev, device_id_type=MESH)


def _sigmoid(x):
    return jax.nn.sigmoid(x)


def _dot(a, b):
    return jnp.dot(a, b, preferred_element_type=F32)


def _dot_nt(a, b):
    return lax.dot_general(a, b, (((1,), (1,)), ((), ())), preferred_element_type=F32)


def _dot_tn(a, b):
    return lax.dot_general(a, b, (((0,), (0,)), ((), ())), preferred_element_type=F32)


def _colsum(a):
    return jnp.sum(a, axis=0, keepdims=True)


def _mod_fwd(c, w_mod_sh, b_mod4, conv_sh):
    def body(c_ref, w_ref, b_ref, cv_ref, call_ref, mod_ref, cvall_ref, rows_ref, cmat_ref, s1, r1, s2, r2, s3, r3):
        x, y, _ = _coords()
        me = _my_index()
        j = 2 * x + y
        call_ref[me] = c_ref[...]
        cvall_ref[me] = cv_ref[...]
        sends = []
        for k in range(1, NDEV):
            cp = _rcopy(call_ref.at[me], call_ref.at[me], s1.at[k - 1], r1.at[k - 1], _peer(k))
            cp.start()
            sends.append(cp)
            cp = _rcopy(cvall_ref.at[me], cvall_ref.at[me], s3.at[k - 1], r3.at[k - 1], _peer(k))
            cp.start()
            sends.append(cp)
        for k in range(1, NDEV):
            pk = me ^ k
            _rcopy(call_ref.at[pk], call_ref.at[pk], s1.at[k - 1], r1.at[k - 1], _peer(k)).wait_recv()
        for b in range(NDEV):
            cmat_ref[pl.ds(b, 1), :] = call_ref[b]
        cm = cmat_ref[...]
        act = cm * _sigmoid(cm)
        mp = jnp.dot(act, w_ref[...], preferred_element_type=F32, precision=HIGHEST) + b_ref[j]
        for b in range(NDEV):
            rows_ref[b] = mp[b:b + 1]
        mod_ref[j] = rows_ref[me]
        for q, k in enumerate((2, 4, 6)):
            cp = _rcopy(rows_ref.at[me ^ k], mod_ref.at[j], s2.at[q], r2.at[q], _peer(k))
            cp.start()
            sends.append(cp)
        for q, k in enumerate((2, 4, 6)):
            jq = j ^ (k >> 1)
            _rcopy(rows_ref.at[me], mod_ref.at[jq], s2.at[q], r2.at[q], _peer(k)).wait_recv()
        for k in range(1, NDEV):
            pk = me ^ k
            _rcopy(cvall_ref.at[pk], cvall_ref.at[pk], s3.at[k - 1], r3.at[k - 1], _peer(k)).wait_recv()
        for cp in sends:
            cp.wait_send()

    vm = pl.BlockSpec(memory_space=pltpu.VMEM)
    return pl.pallas_call(
        body, name="mod_fwd",
        out_shape=(jax.ShapeDtypeStruct((NDEV, 1, D), F32), jax.ShapeDtypeStruct((4, 1, 768), F32),
                   jax.ShapeDtypeStruct((NDEV,) + conv_sh.shape, F32)),
        in_specs=[vm, vm, vm, vm], out_specs=(vm, vm, vm),
        scratch_shapes=[pltpu.VMEM((NDEV, 1, 768), F32), pltpu.VMEM((NDEV, D), F32),
                        pltpu.SemaphoreType.DMA((7,)), pltpu.SemaphoreType.DMA((7,)),
                        pltpu.SemaphoreType.DMA((3,)), pltpu.SemaphoreType.DMA((3,)),
                        pltpu.SemaphoreType.DMA((7,)), pltpu.SemaphoreType.DMA((7,))],
        compiler_params=_params(),
    )(c, w_mod_sh, b_mod4, conv_sh)


def _mod_bwd(dmod4, gn_row, c_all):
    def body(d_ref, g_ref, call_ref, gw_ref, gb_ref, gg_ref, dall_ref, gall_ref, cmat_ref, dmat_ref, s1, r1, s2, r2):
        x, y, _ = _coords()
        me = _my_index()
        j = 2 * x + y
        dall_ref[me] = d_ref[...]
        gall_ref[me] = g_ref[...]
        sends = []
        for k in range(1, NDEV):
            for buf, ss, rs in ((dall_ref, s1, r1), (gall_ref, s2, r2)):
                cp = _rcopy(buf.at[me], buf.at[me], ss.at[k - 1], rs.at[k - 1], _peer(k))
                cp.start()
                sends.append(cp)
        for k in range(1, NDEV):
            pk = me ^ k
            for buf, ss, rs in ((dall_ref, s1, r1), (gall_ref, s2, r2)):
                _rcopy(buf.at[pk], buf.at[pk], ss.at[k - 1], rs.at[k - 1], _peer(k)).wait_recv()
        for cp in sends:
            cp.wait_send()
        gb, gg = dall_ref[0], gall_ref[0]
        for b in range(1, NDEV):
            gb = gb + dall_ref[b]
            gg = gg + gall_ref[b]
        gb_ref[...] = gb
        gg_ref[...] = gg
        for b in range(NDEV):
            cmat_ref[pl.ds(b, 1), :] = call_ref[b]
            dmat_ref[pl.ds(b, 1), :] = dall_ref[b, j]
        cm = cmat_ref[...]
        act = cm * _sigmoid(cm)
        gw_ref[...] = lax.dot_general(act, dmat_ref[...], (((0,), (0,)), ((), ())),
                                      preferred_element_type=F32, precision=HIGHEST)

    vm = pl.BlockSpec(memory_space=pltpu.VMEM)
    return pl.pallas_call(
        body, name="mod_bwd",
        out_shape=(jax.ShapeDtypeStruct((D, 768), F32), jax.ShapeDtypeStruct((4, 1, 768), F32),
                   jax.ShapeDtypeStruct((1, D), F32)),
        in_specs=[vm, vm, vm], out_specs=(vm, vm, vm),
        scratch_shapes=[pltpu.VMEM((NDEV, 4, 1, 768), F32), pltpu.VMEM((NDEV, 1, D), F32),
                        pltpu.VMEM((NDEV, D), F32), pltpu.VMEM((NDEV, 768), F32),
                        pltpu.SemaphoreType.DMA((7,)), pltpu.SemaphoreType.DMA((7,)),
                        pltpu.SemaphoreType.DMA((7,)), pltpu.SemaphoreType.DMA((7,))],
        compiler_params=_params(),
    )(dmod4, gn_row, c_all)


def _gather_norm_inproj(x, gn, shift, scale, wsh, order, tm=1024, tn=1024):
    s = x.shape[0]
    ni = s // tm
    npc = PW // tn
    rows, cols = wsh.shape
    half = rows // 2
    nch = 4
    cr = half // nch
    chips = ((1, 0), (0, 1), (1, 1))

    def body(ord_ref, x_ref, gn_ref, sh_ref, sc_ref, wsh_hbm, p_ref, h_ref, wg_hbm, hs_all, w_s, wsem, ss, rs):
        slot, i, col = pl.program_id(0), pl.program_id(1), pl.program_id(2)
        cx, cy, cc = _coords()
        j = 2 * cx + cy
        sib = (cx, cy, 1 - cc)
        mine = lambda n: pl.ds(cc * half + n * cr, cr)
        theirs = lambda n: pl.ds((1 - cc) * half + n * cr, cr)
        shard_of = lambda q: j ^ (2 * chips[q][0] + chips[q][1])

        def to_chip(q, n):
            e = nch * q + n
            return _rcopy(wsh_hbm.at[mine(n)], wg_hbm.at[j, mine(n)], ss.at[e], rs.at[e],
                          (_flip(cx, chips[q][0]), _flip(cy, chips[q][1]), cc))

        def from_chip(q, n):
            e = nch * q + n
            return _rcopy(wsh_hbm.at[mine(n)], wg_hbm.at[shard_of(q), mine(n)], ss.at[e], rs.at[e], sib)

        def to_sibling(q, n):
            e = 3 * nch + nch * q + n
            return _rcopy(wg_hbm.at[shard_of(q), mine(n)], wg_hbm.at[shard_of(q), mine(n)], ss.at[e], rs.at[e], sib)

        def from_sibling(q, n):
            e = 3 * nch + nch * q + n
            return _rcopy(wsh_hbm.at[mine(n)], wg_hbm.at[shard_of(q), theirs(n)], ss.at[e], rs.at[e], sib)

        def load(sl, src):
            cp = pltpu.make_async_copy(src, w_s.at[sl], wsem.at[sl])
            cp.start()
            cp.wait()

        first = (i == 0) & (col == 0)

        @pl.when(first & (slot == 0))
        def _():
            for n in range(nch):
                for q in (0, 1):
                    to_chip(q, n).start()
            load(0, wsh_hbm.at[pl.ds(0, D), :])

        @pl.when(first & (slot == 1))
        def _():
            for q in (0, 1):
                for n in range(nch):
                    from_chip(q, n).wait_recv()
                    to_sibling(q, n).start()
            for n in range(nch):
                to_chip(2, n).start()
            for n in range(nch):
                from_sibling(0, n).wait_recv()
            load(1, wg_hbm.at[shard_of(0), pl.ds(0, D), :])

        @pl.when(first & (slot == 2))
        def _():
            for n in range(nch):
                from_sibling(1, n).wait_recv()
            load(2, wg_hbm.at[shard_of(1), pl.ds(0, D), :])

        @pl.when(first & (slot == 3))
        def _():
            for n in range(nch):
                from_chip(2, n).wait_recv()
                to_sibling(2, n).start()
            for n in range(nch):
                from_sibling(2, n).wait_recv()
            load(3, wg_hbm.at[shard_of(2), pl.ds(0, D), :])
            for q in range(3):
                for n in range(nch):
                    to_chip(q, n).wait_send()
                    to_sibling(q, n).wait_send()

        @pl.when((slot == 0) & (col == 0))
        def _():
            xt = x_ref[...]
            rstd = lax.rsqrt(jnp.mean(xt * xt, axis=-1, keepdims=True) + EPS)
            h = ((xt * rstd * gn_ref[...]) * (1.0 + sc_ref[...]) + sh_ref[...]).astype(BF16)
            hs_all[i] = h
            h_ref[...] = h

        p_ref[0] = _dot(hs_all[i], w_s[slot, :, pl.ds(pl.multiple_of(col * tn, tn), tn)]).astype(BF16)

    row = pl.BlockSpec((1, D), lambda sl, i, col, o: (0, 0))
    x_rows = lambda sl, i, col, o: (jnp.where(sl == 0, i, ni - 1), 0)
    any_ = pl.BlockSpec(memory_space=pl.ANY)
    return pl.pallas_call(
        body, name="gather_norm_inproj",
        grid_spec=pltpu.PrefetchScalarGridSpec(
            num_scalar_prefetch=1, grid=(4, ni, npc),
            in_specs=[pl.BlockSpec((tm, D), x_rows), row, row, row, any_],
            out_specs=(pl.BlockSpec((1, tm, tn), lambda sl, i, col, o: (o[sl], i, col)),
                       pl.BlockSpec((tm, D), x_rows), any_),
            scratch_shapes=[pltpu.VMEM((ni, tm, D), BF16), pltpu.VMEM((4, D, PW), BF16),
                            pltpu.SemaphoreType.DMA((4,)),
                            pltpu.SemaphoreType.DMA((6 * nch,)), pltpu.SemaphoreType.DMA((6 * nch,))]),
        out_shape=(jax.ShapeDtypeStruct((4, s, PW), BF16), jax.ShapeDtypeStruct((s, D), BF16),
                   jax.ShapeDtypeStruct((4, rows, cols), wsh.dtype)),
        compiler_params=_params(("arbitrary", "arbitrary", "arbitrary")),
    )(order, x, gn, shift, scale, wsh)


def _shift_down(prev8, cur, d):
    t = cur.shape[0]
    c3 = cur.reshape(t // 8, 8, DH)
    rot = pltpu.roll(c3, d, 1)
    before = jnp.concatenate([pltpu.roll(prev8, d, 0).reshape(1, 8, DH), rot[:-1]], axis=0)
    rows = lax.broadcasted_iota(jnp.int32, c3.shape, 1)
    return jnp.where(rows >= d, rot, before).reshape(t, DH)


def _shift_up(cur, next8, d):
    t = cur.shape[0]
    c3 = cur.reshape(t // 8, 8, DH)
    rot = pltpu.roll(c3, 8 - d, 1)
    after = jnp.concatenate([rot[1:], pltpu.roll(next8, 8 - d, 0).reshape(1, 8, DH)], axis=0)
    rows = lax.broadcasted_iota(jnp.int32, c3.shape, 1)
    return jnp.where(rows < 8 - d, rot, after).reshape(t, DH)


def _rnn_gates(xr, prev8, cw, cb, wa, ba, wx, bx, lam, reset):
    xc = cw[3:4] * xr + cb
    for d in (1, 2, 3):
        xc = xc + cw[3 - d:4 - d] * _shift_down(prev8, xr, d)
    xcb = xc.astype(BF16)
    r = _sigmoid(_dot(xcb, wa.astype(BF16)) + ba)
    ig = _sigmoid(_dot(xcb, wx.astype(BF16)) + bx)
    nl = -lam
    sp = jnp.maximum(nl, 0.0) + jnp.log1p(jnp.exp(-jnp.abs(nl)))
    log_a = (-LRU_C * r) * sp
    a_raw = jnp.exp(log_a)
    a = jnp.where(reset, 0.0, a_raw)
    mult = jnp.where(reset, 1.0, jnp.sqrt(1.0 - a_raw * a_raw))
    return xc, r, ig, sp, a, mult


def _log_scan(a, b, axis, up):
    n = a.shape[axis]
    rows = lax.broadcasted_iota(jnp.int32, a.shape, axis)
    d = 1
    while d < n:
        m = rows < n - d if up else rows >= d
        shift = n - d if up else d
        a_s = pltpu.roll(a, shift, axis)
        b_s = pltpu.roll(b, shift, axis)
        b = jnp.where(m, a * b_s + b, b)
        a = jnp.where(m, a * a_s, a)
        d *= 2
    return a, b


def _scan(a, b, t, edge, up=False):
    g = t // 8
    a3, b3 = _log_scan(a.reshape(g, 8, DH), b.reshape(g, 8, DH), 1, up)
    last = 0 if up else 7
    ag, bg = _log_scan(a3[:, last, :], b3[:, last, :], 0, up)
    hg = ag * edge + bg
    grp = lax.broadcasted_iota(jnp.int32, hg.shape, 0)
    if up:
        cin = jnp.where(grp == g - 1, edge, pltpu.roll(hg, g - 1, 0))
        tail = hg[0:1]
    else:
        cin = jnp.where(grp == 0, edge, pltpu.roll(hg, 1, 0))
        tail = hg[g - 1:g]
    return (a3 * cin[:, None, :] + b3).reshape(t, DH), tail


def _rnn_fwd(p, pos, conv_w, conv_b, w_a, b_a, w_x, b_x, lam, tt=1024):
    s = p.shape[1]
    nt = s // tt

    def body(xr_ref, z_ref, pos_ref, cw_ref, cb_ref, wa_ref, ba_ref, wx_ref, bx_ref, lam_ref,
             hr_ref, gr_ref, xprev, hprev):
        @pl.when(pl.program_id(1) == 0)
        def _():
            xprev[...] = jnp.zeros_like(xprev)
            hprev[...] = jnp.zeros_like(hprev)

        xr = xr_ref[0].astype(F32)
        z = z_ref[0].astype(F32)
        reset = pos_ref[...] > 0.5
        xc, r, ig, sp, a, mult = _rnn_gates(xr, xprev[...], cw_ref[...], cb_ref[...], wa_ref[0], ba_ref[0],
                                            wx_ref[0], bx_ref[0], lam_ref[...], reset)
        bx = mult * ig * xc
        h, h_last = _scan(a, bx, tt, hprev[0:1])
        xprev[...] = xr[tt - 8:]
        hprev[...] = jnp.broadcast_to(h_last, (8, DH))
        hr_ref[...] = h
        gr_ref[...] = (h * (z * _sigmoid(z))).astype(BF16)

    head_row = lambda hh, t: (0, hh)
    return pl.pallas_call(
        body, name="rnn_fwd", grid=(H, nt),
        in_specs=[pl.BlockSpec((1, tt, DH), lambda hh, t: (0, t, hh)),
                  pl.BlockSpec((1, tt, DH), lambda hh, t: (0, t, H + hh)),
                  pl.BlockSpec((tt, DH), lambda hh, t: (t, 0)),
                  pl.BlockSpec((4, DH), head_row), pl.BlockSpec((1, DH), head_row),
                  pl.BlockSpec((1, DH, DH), lambda hh, t: (hh, 0, 0)), pl.BlockSpec((1, 1, DH), lambda hh, t: (hh, 0, 0)),
                  pl.BlockSpec((1, DH, DH), lambda hh, t: (hh, 0, 0)), pl.BlockSpec((1, 1, DH), lambda hh, t: (hh, 0, 0)),
                  pl.BlockSpec((1, DH), head_row)],
        out_specs=(pl.BlockSpec((tt, DH), lambda hh, t: (t, hh)), pl.BlockSpec((tt, DH), lambda hh, t: (t, hh))),
        out_shape=(jax.ShapeDtypeStruct((s, D), F32), jax.ShapeDtypeStruct((s, D), BF16)),
        scratch_shapes=[pltpu.VMEM((8, DH), F32), pltpu.VMEM((8, DH), F32)],
        compiler_params=_params(("parallel", "arbitrary")),
    )(p, p, pos, conv_w, conv_b, w_a, b_a, w_x, b_x, lam)


def _rnn_bwd(p, hr, dgr, pos, conv_w, conv_b, w_a, b_a, w_x, b_x, lam, tt=1024):
    s = p.shape[1]
    nt = s // tt
    t8 = tt // 8

    def body(xr_ref, z_ref, xp_ref, hr_ref, hp_ref, dg_ref, pos_ref, cw_ref, cb_ref, wa_ref, ba_ref, wx_ref, bx_ref,
             lam_ref, dxr_ref, dz_ref, gwa_ref, gba_ref, gwx_ref, gbx_ref, glam_ref, gcw_ref, gcb_ref,
             a_next, g_next, dxc_next):
        t = pl.program_id(1)
        has_prev = t < nt - 1

        @pl.when(t == 0)
        def _():
            a_next[...] = jnp.zeros_like(a_next)
            g_next[...] = jnp.zeros_like(g_next)
            dxc_next[...] = jnp.zeros_like(dxc_next)
            gwa_ref[...] = jnp.zeros_like(gwa_ref)
            gba_ref[...] = jnp.zeros_like(gba_ref)
            gwx_ref[...] = jnp.zeros_like(gwx_ref)
            gbx_ref[...] = jnp.zeros_like(gbx_ref)
            glam_ref[...] = jnp.zeros_like(glam_ref)
            gcw_ref[...] = jnp.zeros_like(gcw_ref)
            gcb_ref[...] = jnp.zeros_like(gcb_ref)

        xr = xr_ref[0].astype(F32)
        z = z_ref[0].astype(F32)
        hr_blk = hr_ref[...]
        dg = dg_ref[...]
        xprev = jnp.where(has_prev, xp_ref[0].astype(F32)[8:], 0.0)
        hprev8 = jnp.where(has_prev, hp_ref[...], 0.0)
        reset = pos_ref[...] > 0.5
        cw = cw_ref[...]
        wa = wa_ref[0]
        wx = wx_ref[0]
        lam_v = lam_ref[...]
        xc, r, ig, sp, a, mult = _rnn_gates(xr, xprev, cw, cb_ref[...], wa, ba_ref[0], wx, bx_ref[0], lam_v, reset)

        sz = _sigmoid(z)
        dh = dg * (z * sz)
        dz_ref[...] = (dg * hr_blk * (sz * (1.0 + z * (1.0 - sz)))).astype(BF16)

        an = _shift_up(a, a_next[...], 1)
        g, g_first = _scan(an, dh, tt, g_next[0:1], up=True)
        a_next[...] = jnp.broadcast_to(a[0:1], (8, DH))
        g_next[...] = jnp.broadcast_to(g_first, (8, DH))

        hm1 = _shift_down(hprev8, hr_blk, 1)
        da = g * hm1
        dmult = g * (ig * xc)
        di = g * (mult * xc)
        dxc = g * (mult * ig)
        dla = jnp.where(reset, 0.0, da * a - dmult * (a * a) / mult)
        dr = dla * (-LRU_C * sp)
        dsp = _colsum(dla * (-LRU_C * r))
        glam_ref[0] += dsp * (-_sigmoid(-lam_v))
        dpa = dr * r * (1.0 - r)
        dpx = di * ig * (1.0 - ig)
        dpab = dpa.astype(BF16)
        dpxb = dpx.astype(BF16)
        dxc = dxc + _dot_nt(dpab, wa.astype(BF16)) + _dot_nt(dpxb, wx.astype(BF16))
        xcb = xc.astype(BF16)
        gwa_ref[0] += _dot_tn(xcb, dpab)
        gwx_ref[0] += _dot_tn(xcb, dpxb)
        gba_ref[0] += _colsum(dpa)
        gbx_ref[0] += _colsum(dpx)

        dxr = cw[3:4] * dxc
        for d in (1, 2, 3):
            dxr = dxr + cw[3 - d:4 - d] * _shift_up(dxc, dxc_next[...], d)
        dxr_ref[...] = dxr.astype(BF16)
        dxc_next[...] = dxc[0:8]
        gcb_ref[0] += _colsum(dxc)
        gcw_ref[0, 3:4, :] += _colsum(xr * dxc)
        for d in (1, 2, 3):
            gcw_ref[0, 3 - d:4 - d, :] += _colsum(_shift_down(xprev, xr, d) * dxc)

    rt = lambda t: nt - 1 - t
    prev8 = lambda t: jnp.maximum(rt(t) * t8 - 1, 0)
    head_row = lambda hh, t: (0, hh)
    hsm = lambda hh, t: (hh, 0, 0)
    return pl.pallas_call(
        body, name="rnn_bwd", grid=(H, nt),
        in_specs=[pl.BlockSpec((1, tt, DH), lambda hh, t: (0, rt(t), hh)),
                  pl.BlockSpec((1, tt, DH), lambda hh, t: (0, rt(t), H + hh)),
                  pl.BlockSpec((1, 16, DH), lambda hh, t: (0, jnp.maximum(rt(t) * (tt // 16) - 1, 0), hh)),
                  pl.BlockSpec((tt, DH), lambda hh, t: (rt(t), hh)),
                  pl.BlockSpec((8, DH), lambda hh, t: (prev8(t), hh)),
                  pl.BlockSpec((tt, DH), lambda hh, t: (rt(t), hh)),
                  pl.BlockSpec((tt, DH), lambda hh, t: (rt(t), 0)),
                  pl.BlockSpec((4, DH), head_row), pl.BlockSpec((1, DH), head_row),
                  pl.BlockSpec((1, DH, DH), hsm), pl.BlockSpec((1, 1, DH), hsm),
                  pl.BlockSpec((1, DH, DH), hsm), pl.BlockSpec((1, 1, DH), hsm),
                  pl.BlockSpec((1, DH), head_row)],
        out_specs=(pl.BlockSpec((tt, DH), lambda hh, t: (rt(t), hh)), pl.BlockSpec((tt, DH), lambda hh, t: (rt(t), hh)),
                   pl.BlockSpec((1, DH, DH), hsm), pl.BlockSpec((1, 1, DH), hsm),
                   pl.BlockSpec((1, DH, DH), hsm), pl.BlockSpec((1, 1, DH), hsm),
                   pl.BlockSpec((1, 1, DH), hsm), pl.BlockSpec((1, 4, DH), hsm), pl.BlockSpec((1, 1, DH), hsm)),
        out_shape=(jax.ShapeDtypeStruct((s, D), BF16), jax.ShapeDtypeStruct((s, D), BF16),
                   jax.ShapeDtypeStruct((H, DH, DH), F32), jax.ShapeDtypeStruct((H, 1, DH), F32),
                   jax.ShapeDtypeStruct((H, DH, DH), F32), jax.ShapeDtypeStruct((H, 1, DH), F32),
                   jax.ShapeDtypeStruct((H, 1, DH), F32), jax.ShapeDtypeStruct((H, 4, DH), F32),
                   jax.ShapeDtypeStruct((H, 1, DH), F32)),
        scratch_shapes=[pltpu.VMEM((8, DH), F32), pltpu.VMEM((8, DH), F32), pltpu.VMEM((8, DH), F32)],
        compiler_params=_params(("parallel", "arbitrary")),
    )(p, p, p, hr, hr, dgr, pos, conv_w, conv_b, w_a, b_a, w_x, b_x, lam)


def _rope(t, c, sa, sb):
    return t * c + pltpu.roll(t, DH - ROT // 2, 1) * sa + pltpu.roll(t, ROT // 2, 1) * sb


def _rope_bwd(g, c, sa, sb):
    return g * c + pltpu.roll(g * sa, ROT // 2, 1) + pltpu.roll(g * sb, DH - ROT // 2, 1)


def _unit_bases(gi, u):
    dil = DILATIONS[gi]
    if dil == 1:
        return u * UB, SPAN + (u - 1) * UB, u == 0
    if dil == 4:
        blk, r = u // 4, u % 4
        return blk * 4 * UB + r, SPAN + (blk - 1) * 4 * UB + r, blk == 0
    return u, u, True


def _unit_slices(gi, u):
    dil = DILATIONS[gi]
    qb0, kb0, first = _unit_bases(gi, u)
    if dil == 1:
        if not isinstance(qb0, int):
            qb0, kb0 = pl.multiple_of(qb0, UB), pl.multiple_of(kb0, UB)
        return pl.ds(qb0, UB), pl.ds(kb0, 2 * UB), first
    return pl.ds(qb0, UB, stride=dil), pl.ds(kb0, 2 * UB, stride=dil), first


def _bdot(a, b):
    return lax.dot_general(a, b, (((2,), (1,)), ((0,), (0,))), preferred_element_type=F32)


def _bdot_nt(a, b):
    return lax.dot_general(a, b, (((2,), (2,)), ((0,), (0,))), preferred_element_type=F32)


def _bdot_tn(a, b):
    return lax.dot_general(a, b, (((1,), (1,)), ((0,), (0,))), preferred_element_type=F32)


def _band_mask(first_in_span, has_prev):
    qi = lax.broadcasted_iota(jnp.int32, (UB, 2 * UB), 0)
    ki = lax.broadcasted_iota(jnp.int32, (UB, 2 * UB), 1)
    dist = UB + qi - ki
    band = (dist >= 0) & (dist <= UB)
    return band & ((ki >= UB) | jnp.logical_not(first_in_span) | has_prev)


def _gather_halves(phase, src_hbm, dst_hbm, ss, rs):
    half = src_hbm.shape[0] // 2
    cx, cy, cc = _coords()
    j = 2 * cx + cy
    sib = (cx, cy, 1 - cc)
    mine = pl.ds(cc * half, half)
    theirs = pl.ds((1 - cc) * half, half)
    chips = ((1, 0), (0, 1), (1, 1))
    for q, (kx, ky) in enumerate(chips):
        jq = j ^ (2 * kx + ky)
        out = _rcopy(src_hbm.at[mine], dst_hbm.at[j, mine], ss.at[q], rs.at[q], (_flip(cx, kx), _flip(cy, ky), cc))
        landed = _rcopy(src_hbm.at[mine], dst_hbm.at[jq, mine], ss.at[q], rs.at[q], sib)
        onward = _rcopy(dst_hbm.at[jq, mine], dst_hbm.at[jq, mine], ss.at[3 + q], rs.at[3 + q], sib)
        from_sib = _rcopy(src_hbm.at[mine], dst_hbm.at[jq, theirs], ss.at[3 + q], rs.at[3 + q], sib)
        if phase == 0:
            out.start()
        elif phase == 1:
            landed.wait_recv()
            onward.start()
        else:
            from_sib.wait_recv()
            out.wait_send()
            onward.wait_send()


def _attn_fwd(p, rc, rsa, rsb, w3sh):
    s = p.shape[1]
    ns = s // SPAN
    nunit = SPAN // UB

    def body(q_ref, k_ref, v_ref, z_ref, c_ref, sa_ref, sb_ref, w3_hbm, o_ref, lse_ref, ga_ref, qro_ref, kro_ref,
             w3g_hbm, qr, kf, vf, acc, mm, ll, ss, rs):
        hh, n = pl.program_id(0), pl.program_id(1)
        for phase, at_head, at_span in ((0, 0, 0), (1, H // 2, 0), (2, H - 1, ns - 1)):
            @pl.when((hh == at_head) & (n == at_span))
            def _(phase=phase):
                _gather_halves(phase, w3_hbm, w3g_hbm, ss, rs)

        @pl.when(n == 0)
        def _():
            kf[0:SPAN] = jnp.zeros((SPAN, DH), F32)
            vf[0:SPAN] = jnp.zeros((SPAN, DH), F32)

        c, sa, sb = c_ref[...], sa_ref[...], sb_ref[...]
        q_rot = _rope(q_ref[0].astype(F32), c, sa, sb).astype(BF16)
        k_rot = _rope(k_ref[0].astype(F32), c, sa, sb).astype(BF16)
        qro_ref[...] = q_rot
        kro_ref[...] = k_rot
        qr[...] = q_rot.astype(F32)
        kf[SPAN:] = k_rot.astype(F32)
        vf[SPAN:] = v_ref[0].astype(F32)
        has_prev = n > 0

        for gi, dil in enumerate(DILATIONS):
            def trip(t, carry, gi=gi, dil=dil):
                qsls, ksls, firsts = [], [], []
                for b in range(UNIT_BATCH_FWD):
                    qsl, ksl, first = _unit_slices(gi, t * UNIT_BATCH_FWD + b)
                    qsls.append(qsl)
                    ksls.append(ksl)
                    firsts.append(first)
                qb = jnp.stack([qr[qsl, :].astype(BF16) for qsl in qsls])
                kb = jnp.stack([kf[ksl, :].astype(BF16) for ksl in ksls])
                vb = jnp.stack([vf[ksl, :].astype(BF16) for ksl in ksls])
                s_all = _bdot_nt(qb, kb)
                prs = []
                for b in range(UNIT_BATCH_FWD):
                    sc = jnp.where(_band_mask(firsts[b], has_prev), s_all[b] * SCALE, NEG)
                    m = jnp.max(sc, axis=-1, keepdims=True)
                    pr = jnp.exp(sc - m)
                    l = jnp.sum(pr, axis=-1, keepdims=True)
                    mm[gi, qsls[b], :] = jnp.broadcast_to(m, (UB, DH))
                    ll[gi, qsls[b], :] = jnp.broadcast_to(l, (UB, DH))
                    prs.append(pr.astype(BF16))
                o_all = _bdot(jnp.stack(prs), vb)
                for b in range(UNIT_BATCH_FWD):
                    acc[gi, qsls[b], :] = o_all[b]
                return carry

            lax.fori_loop(0, nunit // UNIT_BATCH_FWD, trip, 0)

        m_all =jnp.maximum(jnp.maximum(mm[0], mm[1]), mm[2])
        num = jnp.zeros((SPAN, DH), F32)
        den = jnp.zeros((SPAN, DH), F32)
        for gi in range(3):
            w = jnp.exp(mm[gi] - m_all)
            num = num + w * acc[gi]
            den = den + w * ll[gi]
        o = num / den
        o_ref[...] = o
        lse_ref[...] = m_all + jnp.log(den)
        z = z_ref[0].astype(F32)
        ga_ref[...] = (o * (z * _sigmoid(z))).astype(BF16)
        kf[0:SPAN] = kf[SPAN:]
        vf[0:SPAN] = vf[SPAN:]

    blk = lambda piece, off: pl.BlockSpec((1, SPAN, DH), lambda hh, n: (piece, n, off + hh))
    tab = pl.BlockSpec((SPAN, DH), lambda hh, n: (n, 0))
    outb = pl.BlockSpec((SPAN, DH), lambda hh, n: (n, hh))
    any_ = pl.BlockSpec(memory_space=pl.ANY)
    return pl.pallas_call(
        body, name="attn_fwd", grid=(H, ns),
        in_specs=[blk(1, 0), blk(1, H), blk(2, 0), blk(2, H), tab, tab, tab, any_],
        out_specs=(outb, outb, outb, outb, outb, any_),
        out_shape=(jax.ShapeDtypeStruct((s, D), F32), jax.ShapeDtypeStruct((s, D), F32),
                   jax.ShapeDtypeStruct((s, D), BF16), jax.ShapeDtypeStruct((s, D), BF16),
                   jax.ShapeDtypeStruct((s, D), BF16), jax.ShapeDtypeStruct((4,) + w3sh.shape, w3sh.dtype)),
        scratch_shapes=[pltpu.VMEM((SPAN, DH), F32), pltpu.VMEM((2 * SPAN, DH), F32), pltpu.VMEM((2 * SPAN, DH), F32),
                        pltpu.VMEM((3, SPAN, DH), F32), pltpu.VMEM((3, SPAN, DH), F32), pltpu.VMEM((3, SPAN, DH), F32),
                        pltpu.SemaphoreType.DMA((6,)), pltpu.SemaphoreType.DMA((6,))],
        compiler_params=_params(("arbitrary", "arbitrary")),
    )(p, p, p, p, rc, rsa, rsb, w3sh)


def _attn_bwd(p, q_rot, k_rot, o, lse, dga, rc, rsa, rsb, packs):
    s = p.shape[1]
    ns = s // SPAN
    nunit = SPAN // UB
    npk = len(packs)

    def body(*refs):
        (q_ref, k_ref, kp_ref, v_ref, vp_ref, z_ref, c_ref, sa_ref, sb_ref, o_ref, lse_ref, dg_ref) = refs[:12]
        pk_refs = refs[12:12 + npk]
        dq_ref, dk_ref, dv_ref, dz_ref = refs[12 + npk:16 + npk]
        red_refs = refs[16 + npk:16 + 2 * npk]
        qr, kf, vf, dof, dlt, dqa, dkf, dvf = refs[16 + 2 * npk:24 + 2 * npk]
        rbufs = refs[24 + 2 * npk:24 + 3 * npk]
        sums = refs[24 + 3 * npk:24 + 4 * npk]
        ar_sems = refs[24 + 4 * npk:]
        hh, step = pl.program_id(0), pl.program_id(1)
        n = ns - 1 - step
        has_prev = n > 0
        for phase, at_head, at_step in ((0, 0, 0), (1, H // 2, 0), (2, H - 1, ns - 1)):
            @pl.when((hh == at_head) & (step == at_step))
            def _(phase=phase):
                _allreduce_phase(phase, pk_refs, sums, rbufs, *ar_sems, out_refs=red_refs)

        @pl.when(step == 0)
        def _():
            dkf[...] = jnp.zeros_like(dkf)
            dvf[...] = jnp.zeros_like(dvf)

        @pl.when(step > 0)
        def _():
            dkf[SPAN:] = dkf[0:SPAN]
            dvf[SPAN:] = dvf[0:SPAN]
            dkf[0:SPAN] = jnp.zeros((SPAN, DH), F32)
            dvf[0:SPAN] = jnp.zeros((SPAN, DH), F32)

        c, sa, sb = c_ref[...], sa_ref[...], sb_ref[...]
        qr[...] = q_ref[...].astype(F32)
        kf[SPAN:] = k_ref[...].astype(F32)
        vf[SPAN:] = v_ref[0].astype(F32)
        kf[0:SPAN] = jnp.where(has_prev, kp_ref[...].astype(F32), 0.0)
        vf[0:SPAN] = jnp.where(has_prev, vp_ref[0].astype(F32), 0.0)
        z = z_ref[0].astype(F32)
        sz = _sigmoid(z)
        dg = dg_ref[...]
        ov = o_ref[...]
        do = dg * (z * sz)
        dz_ref[...] = (dg * ov * (sz * (1.0 + z * (1.0 - sz)))).astype(BF16)
        dof[...] = do
        dlt[...] = jnp.dot(do * ov, jnp.ones((DH, DH), F32), preferred_element_type=F32, precision=HIGHEST)
        dqa[...] = jnp.zeros_like(dqa)

        for gi, dil in enumerate(DILATIONS):
            def trip(t, carry, gi=gi, dil=dil):
                qsls, ksls, firsts = [], [], []
                for b in range(UNIT_BATCH):
                    qsl, ksl, first = _unit_slices(gi, t * UNIT_BATCH + b)
                    qsls.append(qsl)
                    ksls.append(ksl)
                    firsts.append(first)
                qb = jnp.stack([qr[qsl, :].astype(BF16) for qsl in qsls])
                kb = jnp.stack([kf[ksl, :].astype(BF16) for ksl in ksls])
                vb = jnp.stack([vf[ksl, :].astype(BF16) for ksl in ksls])
                dob = jnp.stack([dof[qsl, :].astype(BF16) for qsl in qsls])
                s_all = _bdot_nt(qb, kb)
                dp_all = _bdot_nt(dob, vb)
                prs, dss = [], []
                for b in range(UNIT_BATCH):
                    lse_b = lse_ref[qsls[b], :]
                    dl_b = dlt[qsls[b], :]
                    pr = jnp.exp(s_all[b] * SCALE - jnp.concatenate([lse_b, lse_b], axis=1))
                    pr = jnp.where(_band_mask(firsts[b], has_prev), pr, 0.0)
                    prs.append(pr.astype(BF16))
                    dss.append((pr * (dp_all[b] - jnp.concatenate([dl_b, dl_b], axis=1)) * SCALE).astype(BF16))
                ds_all = jnp.stack(dss)
                dv_all = _bdot_tn(jnp.stack(prs), dob)
                dq_all = _bdot(ds_all, kb)
                dk_all = _bdot_tn(ds_all, qb)
                for b in range(UNIT_BATCH):
                    dvf[ksls[b], :] += dv_all[b]
                    dqa[qsls[b], :] += dq_all[b]
                    dkf[ksls[b], :] += dk_all[b]
                return carry

            lax.fori_loop(0, nunit // UNIT_BATCH, trip, 0)

        dq_ref[...] = _rope_bwd(dqa[...], c, sa, sb).astype(BF16)
        dk_ref[...] = _rope_bwd(dkf[SPAN:], c, sa, sb).astype(BF16)
        dv_ref[...] = dvf[SPAN:].astype(BF16)

    rn = lambda n: ns - 1 - n
    pn = lambda n: jnp.maximum(ns - 2 - n, 0)
    blk = lambda piece, off: pl.BlockSpec((1, SPAN, DH), lambda hh, n: (piece, rn(n), off + hh))
    blkp = lambda piece, off: pl.BlockSpec((1, SPAN, DH), lambda hh, n: (piece, pn(n), off + hh))
    tab = pl.BlockSpec((SPAN, DH), lambda hh, n: (rn(n), 0))
    io = pl.BlockSpec((SPAN, DH), lambda hh, n: (rn(n), hh))
    iop = pl.BlockSpec((SPAN, DH), lambda hh, n: (pn(n), hh))
    vm = pl.BlockSpec(memory_space=pltpu.VMEM)
    outs = pl.pallas_call(
        body, name="attn_bwd", grid=(H, ns),
        in_specs=[io, io, iop, blk(2, 0), blkp(2, 0), blk(2, H), tab, tab, tab, io, io, io] + [vm] * npk,
        out_specs=(io, io, io, io) + (vm,) * npk,
        out_shape=tuple(jax.ShapeDtypeStruct((s, D), BF16) for _ in range(4)) +
                  tuple(jax.ShapeDtypeStruct(a.shape, F32) for a in packs),
        scratch_shapes=[pltpu.VMEM((SPAN, DH), F32), pltpu.VMEM((2 * SPAN, DH), F32), pltpu.VMEM((2 * SPAN, DH), F32),
                        pltpu.VMEM((SPAN, DH), F32), pltpu.VMEM((SPAN, DH), F32), pltpu.VMEM((SPAN, DH), F32),
                        pltpu.VMEM((2 * SPAN, DH), F32), pltpu.VMEM((2 * SPAN, DH), F32)] +
                       [pltpu.VMEM((NDEV, a.shape[0] // NDEV, a.shape[1]), F32) for a in packs] +
                       [pltpu.VMEM(a.shape, F32) for a in packs] +
                       [pltpu.SemaphoreType.DMA((7 * npk,)) for _ in range(4)],
        compiler_params=_params(("arbitrary", "arbitrary")),
    )(q_rot, k_rot, k_rot, p, p, p, rc, rsa, rsb, o, lse, dga, *packs)
    return outs[:4], outs[4:]


def _tail(gr, ga, p, x, tgt, w3, b_gate, gate, g_final, tm=256):
    s = x.shape[0]
    nt = s // tm

    def body(gr_ref, ga_ref, pr_ref, pa_ref, x_ref, t_ref, bg_ref, gate_ref, gf_ref, w_hbm,
             dgr_ref, dga_ref, dc_ref, dx2_ref, vec_ref, go_hbm, w_s, acc_s, sem):
        i = pl.program_id(0)

        @pl.when(i == 0)
        def _():
            cp = pltpu.make_async_copy(w_hbm, w_s, sem.at[12])
            cp.start()
            acc_s[...] = jnp.zeros_like(acc_s)
            vec_ref[...] = jnp.zeros_like(vec_ref)
            cp.wait()

        grb = gr_ref[...]
        gab = ga_ref[...]
        bg = bg_ref[...]
        gate_v = gate_ref[...]
        gf = gf_ref[...]
        y_r = _dot(grb, w_s[0])
        y_a = _dot(gab, w_s[1])
        sr = _sigmoid(pr_ref[0].astype(F32) + bg[:, :D])
        sa = _sigmoid(pa_ref[0].astype(F32) + bg[:, D:])
        mb = (sr * y_r + sa * y_a).astype(BF16)
        u = _dot(mb, w_s[2])
        x2 = x_ref[...] + gate_v * u
        rstd = lax.rsqrt(jnp.mean(x2 * x2, axis=-1, keepdims=True) + EPS)
        xh = x2 * rstd
        e = xh * gf - t_ref[...]
        dy = e * (1.0 / D)
        dyg = dy * gf
        dx2 = rstd * (dyg - xh * jnp.mean(dyg * xh, axis=-1, keepdims=True))
        dx2_ref[...] = dx2
        dub = (dx2 * gate_v).astype(BF16)
        dm = _dot_nt(dub, w_s[2])
        dyr = (dm * sr).astype(BF16)
        dya = (dm * sa).astype(BF16)
        dpr = dm * y_r * (sr * (1.0 - sr))
        dpa = dm * y_a * (sa * (1.0 - sa))
        dc_ref[:, :D] = dpr.astype(BF16)
        dc_ref[:, D:] = dpa.astype(BF16)
        dgr_ref[...] = _dot_nt(dyr, w_s[0])
        dga_ref[...] = _dot_nt(dya, w_s[1])
        acc_s[0] += _dot_tn(grb, dyr)
        acc_s[1] += _dot_tn(gab, dya)
        acc_s[2] += _dot_tn(mb, dub)
        vec_ref[0:1, :] += _colsum(dy * xh)
        vec_ref[1:2, :] += _colsum(dx2 * u)
        vec_ref[2:3, :] += _colsum(dpr)
        vec_ref[3:4, :] += _colsum(dpa)
        vec_ref[4:5, :] += _colsum(e * e)

        @pl.when(i == nt - 1)
        def _():
            vec_ref[4:5, :] = jnp.broadcast_to(jnp.sum(vec_ref[4:5, :]) * (0.5 / D), (1, D))
            cps = []
            for w in range(3):
                for j in range(4):
                    cps.append(pltpu.make_async_copy(acc_s.at[w, pl.ds(256 * j, 256)],
                                                     go_hbm.at[j, pl.ds(256 * w, 256)], sem.at[4 * w + j]))
            for cp in cps:
                cp.start()
            for cp in cps:
                cp.wait()

    rowt = lambda i: (i, 0)
    row = lambda w: pl.BlockSpec((1, w), lambda i: (0, 0))
    any_ = pl.BlockSpec(memory_space=pl.ANY)
    return pl.pallas_call(
        body, name="tail", grid=(nt,),
        in_specs=[pl.BlockSpec((tm, D), rowt), pl.BlockSpec((tm, D), rowt),
                  pl.BlockSpec((1, tm, D), lambda i: (3, i, 0)), pl.BlockSpec((1, tm, D), lambda i: (3, i, 1)),
                  pl.BlockSpec((tm, D), rowt), pl.BlockSpec((tm, D), rowt),
                  row(2 * D), row(D), row(D), any_],
        out_specs=(pl.BlockSpec((tm, D), rowt), pl.BlockSpec((tm, D), rowt), pl.BlockSpec((tm, 2 * D), rowt),
                   pl.BlockSpec((tm, D), rowt), pl.BlockSpec((8, D), lambda i: (0, 0)), any_),
        out_shape=(jax.ShapeDtypeStruct((s, D), F32), jax.ShapeDtypeStruct((s, D), F32),
                   jax.ShapeDtypeStruct((s, 2 * D), BF16), jax.ShapeDtypeStruct((s, D), F32),
                   jax.ShapeDtypeStruct((8, D), F32), jax.ShapeDtypeStruct((4, 768, D), F32)),
        scratch_shapes=[pltpu.VMEM((3, D, D), BF16), pltpu.VMEM((3, D, D), F32), pltpu.SemaphoreType.DMA((13,))],
        compiler_params=_params(("arbitrary",)),
    )(gr, ga, p, p, x, tgt, b_gate, gate, g_final, w3)


def _pieces_steps(pieces):
    out, s0 = [], 0
    for a in pieces:
        n = a.shape[1] // D
        out.append((s0, n))
        s0 += n
    return out, s0


def _inproj_bwd_x(pieces, wg, wsh, x, dx2, gn, scale, sums, tm=512):
    s = x.shape[0]
    np_ = len(pieces)
    na = len(sums)
    ni = s // tm
    groups, cur, width = [], [], 0
    for t, a in enumerate(pieces):
        cur.append(t)
        width += a.shape[1]
        if width == PW:
            groups.append(cur)
            cur, width = [], 0
    assert len(groups) == 4 and not cur

    def body(*refs):
        d_refs = refs[:np_]
        w_hbm, wsh_hbm, x_ref, dx2_ref, gn_ref, sc_ref = refs[np_:np_ + 6]
        q_refs = refs[np_ + 6:np_ + 6 + na]
        gx_ref, vec_ref = refs[np_ + 6 + na:np_ + 8 + na]
        r_refs = refs[np_ + 8 + na:np_ + 8 + 2 * na]
        w_s, wsem, ss, rs = refs[np_ + 8 + 2 * na:]
        i = pl.program_id(0)
        my_shard = 2 * lax.axis_index("x") + lax.axis_index("y")

        def scatter_copies():
            cx, cy, cc = _coords()
            j = 2 * cx + cy
            cps = []
            for t, (q, r) in enumerate(zip(q_refs, r_refs)):
                for e, (kx, ky) in enumerate(((1, 0), (0, 1), (1, 1))):
                    cps.append(_rcopy(q.at[j ^ (2 * kx + ky)], r.at[e], ss.at[3 * t + e], rs.at[3 * t + e],
                                      (_flip(cx, kx), _flip(cy, ky), cc)))
            return cps

        def w_copy(pc):
            return pltpu.make_async_copy(w_hbm.at[pc, pl.ds(0, D), :], w_s.at[pc], wsem.at[pc])

        for pc in range(4):
            @pl.when((i == 0) & (my_shard != pc))
            def _(pc=pc):
                w_copy(pc).start()

            @pl.when((i == 0) & (my_shard == pc))
            def _(pc=pc):
                pltpu.make_async_copy(wsh_hbm, w_s.at[pc], wsem.at[pc]).start()

        @pl.when(i == 0)
        def _():
            vec_ref[...] = jnp.zeros_like(vec_ref)
            for cp in scatter_copies():
                cp.start()

        dh = None
        for pc, group in enumerate(groups):
            @pl.when(i == 0)
            def _(pc=pc):
                w_copy(pc).wait()

            tiles = [d_refs[t][...] for t in group]
            lhs = tiles[0] if len(tiles) == 1 else jnp.concatenate(tiles, axis=1)
            part = _dot_nt(lhs, w_s[pc])
            dh = part if dh is None else dh + part

        xt = x_ref[...]
        rstd = lax.rsqrt(jnp.mean(xt * xt, axis=-1, keepdims=True) + EPS)
        xh = xt * rstd
        gn_v = gn_ref[...]
        sc1 = 1.0 + sc_ref[...]
        dhx = dh * xh
        vec_ref[0:1, :] += _colsum(dh)
        vec_ref[1:2, :] += _colsum(dhx) * gn_v
        vec_ref[2:3, :] += _colsum(dhx) * sc1
        dxh = dh * (gn_v * sc1)
        gx_ref[...] = rstd * (dxh - xh * jnp.mean(dxh * xh, axis=-1, keepdims=True)) + dx2_ref[...]

        @pl.when(i == ni - 1)
        def _():
            for cp in scatter_copies():
                cp.wait()

    rowt = lambda i: (i, 0)
    row = pl.BlockSpec((1, D), lambda i: (0, 0))
    any_ = pl.BlockSpec(memory_space=pl.ANY)
    outs = pl.pallas_call(
        body, name="inproj_bwd_x", grid=(ni,),
        in_specs=[pl.BlockSpec((tm, a.shape[1]), rowt) for a in pieces] +
                 [any_, any_, pl.BlockSpec((tm, D), rowt), pl.BlockSpec((tm, D), rowt), row, row] + [any_] * na,
        out_specs=(pl.BlockSpec((tm, D), rowt), pl.BlockSpec((8, D), lambda i: (0, 0))) + (any_,) * na,
        out_shape=(jax.ShapeDtypeStruct((s, D), F32), jax.ShapeDtypeStruct((8, D), F32)) +
                  tuple(jax.ShapeDtypeStruct((3,) + q.shape[1:], q.dtype) for q in sums),
        scratch_shapes=[pltpu.VMEM((4, D, PW), BF16), pltpu.SemaphoreType.DMA((4,)),
                        pltpu.SemaphoreType.DMA((3 * na,)), pltpu.SemaphoreType.DMA((3 * na,))],
        compiler_params=_params(("arbitrary",)),
    )(*pieces, wg, wsh, x, dx2, gn, scale, *sums)
    return outs[0], outs[1], outs[2:]


def _inproj_bwd_w(pieces, hbf, g_out, tk=1024):
    s = hbf.shape[0]
    steps, nk = _pieces_steps(pieces)
    npc = PW // D
    ns = s // tk
    np_ = len(pieces)
    hr = D // 2
    ohr = g_out.shape[1] // 2
    ocr = _chunk_rows(g_out)
    ochunks = [(j, r0) for j in range(g_out.shape[0]) for r0 in range(0, ohr, ocr)]
    noc = len(ochunks)

    def body(*refs):
        d_refs = refs[:np_]
        h_ref, go_hbm, g_ref, rb_hbm, rbo_hbm, stage, ss, rs, oss, ors = refs[np_:]
        cb, k = pl.program_id(0), pl.program_id(1)
        cx, cy, cc = _coords()
        sib = (cx, cy, 1 - cc)

        def block_copy(b):
            return _rcopy(stage.at[b % 2],
                          rb_hbm.at[b // npc, :, pl.ds(pl.multiple_of((b % npc) * D, D), D)], ss.at[b], rs.at[b], sib)

        def out_copy(e):
            j, r0 = ochunks[e]
            return _rcopy(go_hbm.at[j, pl.ds((1 - cc) * ohr + r0, ocr), :], rbo_hbm.at[j, pl.ds(r0, ocr), :],
                          oss.at[e], ors.at[e], sib)

        @pl.when((cb == 0) & (k == 0))
        def _():
            for e in range(noc):
                out_copy(e).start()

        @pl.when(k == 0)
        def _():
            g_ref[...] = jnp.zeros_like(g_ref)

        for (s0, n), d_ref in zip(steps, d_refs):
            @pl.when((cb >= s0) & (cb < s0 + n))
            def _(d_ref=d_ref):
                g_ref[0] += _dot_tn(h_ref[...], d_ref[...])

        @pl.when((k == ns - 1) & (cb > 1))
        def _():
            block_copy(cb - 2).wait_send()

        @pl.when(k == ns - 1)
        def _():
            stage[cb % 2] = g_ref[0, pl.ds(pl.multiple_of((1 - cc) * hr, hr), hr), :]
            block_copy(cb).start()

        @pl.when((k == ns - 1) & (cb == nk - 1))
        def _():
            block_copy(nk - 2).wait_send()
            block_copy(nk - 1).wait_send()
            for b in range(nk):
                block_copy(b).wait_recv()
            for e in range(noc):
                out_copy(e).wait_recv()
                out_copy(e).wait_send()

    def piece_spec(s0, n):
        def imap(cb, k):
            active = (cb >= s0) & (cb < s0 + n)
            return (jnp.where(active, k, 0), jnp.clip(cb - s0, 0, n - 1))
        return pl.BlockSpec((tk, D), imap)

    any_ = pl.BlockSpec(memory_space=pl.ANY)
    return pl.pallas_call(
        body, name="inproj_bwd_w", grid=(nk, ns),
        in_specs=[piece_spec(s0, n) for s0, n in steps] + [pl.BlockSpec((tk, D), lambda cb, k: (k, 0)), any_],
        out_specs=(pl.BlockSpec((1, D, D), lambda cb, k: (cb // npc, 0, cb % npc)), any_, any_),
        out_shape=(jax.ShapeDtypeStruct((4, D, PW), F32), jax.ShapeDtypeStruct((4, hr, PW), F32),
                   jax.ShapeDtypeStruct((g_out.shape[0], ohr, g_out.shape[2]), F32)),
        scratch_shapes=[pltpu.VMEM((2, hr, D), F32), pltpu.SemaphoreType.DMA((nk,)), pltpu.SemaphoreType.DMA((nk,)),
                        pltpu.SemaphoreType.DMA((noc,)), pltpu.SemaphoreType.DMA((noc,))],
        compiler_params=_params(("arbitrary", "arbitrary")),
    )(*pieces, hbf, g_out)


D2D_CHUNK_BYTES = 512 * 1024


def _chunk_rows(a):
    return max(8, D2D_CHUNK_BYTES // (a.shape[-1] * a.dtype.itemsize))


def _pair_swap(arrs):
    na = len(arrs)
    chunks = []
    for t, a in enumerate(arrs):
        cr = _chunk_rows(a)
        chunks += [(t, r0, cr) for r0 in range(0, a.shape[0], cr)]
    nch = len(chunks)

    def body(*refs):
        a_refs = refs[:na]
        o_refs = refs[na:2 * na]
        ss, rs = refs[2 * na:]
        x, y, c = _coords()
        sib = (x, y, 1 - c)
        rcs = []
        for n, (t, r0, cr) in enumerate(chunks):
            rows = pl.ds(r0, cr)
            rc = _rcopy(a_refs[t].at[rows, :], o_refs[t].at[rows, :], ss.at[n], rs.at[n], sib)
            rc.start()
            rcs.append(rc)
        for rc in rcs:
            rc.wait_recv()
        for rc in rcs:
            rc.wait_send()

    any_ = pl.BlockSpec(memory_space=pl.ANY)
    return pl.pallas_call(
        body, name="pair_swap",
        out_shape=tuple(jax.ShapeDtypeStruct(a.shape, a.dtype) for a in arrs),
        in_specs=[any_] * na, out_specs=tuple([any_] * na),
        scratch_shapes=[pltpu.SemaphoreType.DMA((nch,)), pltpu.SemaphoreType.DMA((nch,))],
        compiler_params=_params(),
    )(*arrs)


def _add_half(full, rb, core, tr):
    n, r, cdim = full.shape
    nb = r // 2 // tr

    def body(c_ref, a_ref, b_ref, ob_ref):
        ob_ref[...] = (a_ref[...] + b_ref[...]).astype(BF16)

    mine = pl.BlockSpec((1, tr, cdim), lambda i, j, c_ref: (i, c_ref[0] * nb + j, 0))
    spec = pl.BlockSpec((1, tr, cdim), lambda i, j, c_ref: (i, j, 0))
    return pl.pallas_call(
        body, name="add_half",
        grid_spec=pltpu.PrefetchScalarGridSpec(num_scalar_prefetch=1, grid=(n, nb), in_specs=[mine, spec],
                                               out_specs=spec),
        out_shape=jax.ShapeDtypeStruct(rb.shape, BF16),
        compiler_params=_params(("parallel", "parallel")),
    )(core, full, rb)


def _sum_slots(full, rb, r3, shard_core, tr):
    _, hr, cdim = rb.shape
    nb = hr // tr

    def body(jc_ref, a_ref, b_ref, r_ref, o_ref):
        own = a_ref[0] + b_ref[0]
        o_ref[...] = ((own + r_ref[0].astype(F32)) + r_ref[1].astype(F32)) + r_ref[2].astype(F32)

    return pl.pallas_call(
        body, name="sum_slots",
        grid_spec=pltpu.PrefetchScalarGridSpec(
            num_scalar_prefetch=1, grid=(nb,),
            in_specs=[pl.BlockSpec((1, tr, cdim), lambda i, jc: (jc[0], jc[1] * nb + i, 0)),
                      pl.BlockSpec((1, tr, cdim), lambda i, jc: (jc[0], i, 0)),
                      pl.BlockSpec((3, tr, cdim), lambda i, jc: (0, i, 0))],
            out_specs=pl.BlockSpec((tr, cdim), lambda i, jc: (i, 0))),
        out_shape=jax.ShapeDtypeStruct((hr, cdim), F32),
        compiler_params=_params(("parallel",)),
    )(shard_core, full, rb, r3)


def _allreduce_phase(phase, p_refs, o_refs, rbufs, s1, r1, s2, r2, out_refs=None):
    me = _my_index()

    def chunk(t, d):
        ch = p_refs[t].shape[0] // NDEV
        return pl.ds(pl.multiple_of(d * ch, 8), ch)

    def scatter(t, k):
        e = 7 * t + k - 1
        return _rcopy(p_refs[t].at[chunk(t, me ^ k)], rbufs[t].at[me], s1.at[e], r1.at[e], _peer(k))

    def gather(t, k):
        e = 7 * t + k - 1
        return _rcopy(o_refs[t].at[chunk(t, me)], o_refs[t].at[chunk(t, me)], s2.at[e], r2.at[e], _peer(k))

    for t in range(len(p_refs)):
        if phase == 0:
            for k in range(1, NDEV):
                scatter(t, k).start()
            rbufs[t][me] = p_refs[t][chunk(t, me), :]
        elif phase == 1:
            for k in range(1, NDEV):
                e = 7 * t + k - 1
                _rcopy(p_refs[t].at[chunk(t, me)], rbufs[t].at[me ^ k], s1.at[e], r1.at[e], _peer(k)).wait_recv()
            tot = rbufs[t][0]
            for d in range(1, NDEV):
                tot = tot + rbufs[t][d]
            o_refs[t][chunk(t, me), :] = tot
            for k in range(1, NDEV):
                gather(t, k).start()
        else:
            for k in range(1, NDEV):
                e = 7 * t + k - 1
                _rcopy(o_refs[t].at[chunk(t, me)], o_refs[t].at[chunk(t, me ^ k)], s2.at[e], r2.at[e],
                       _peer(k)).wait_recv()
            for k in range(1, NDEV):
                scatter(t, k).wait_send()
                gather(t, k).wait_send()
            if out_refs is not None:
                out_refs[t][...] = o_refs[t][...]


def _adamw_update(w, g, m, v):
    nm = B1 * m + (1.0 - B1) * g
    nv = B2 * v + (1.0 - B2) * (g * g)
    m_hat = nm / (1.0 - B1 ** STEP)
    v_hat = nv / (1.0 - B2 ** STEP)
    return -LR * (m_hat / (jnp.sqrt(v_hat) + ADAM_EPS) + WD * w), nm, nv


def _adamw(w, g, m, v, tr):
    r, cdim = w.shape

    def body(w_ref, g_ref, m_ref, v_ref, d_ref, nm_ref, nv_ref):
        d_ref[...], nm_ref[...], nv_ref[...] = _adamw_update(w_ref[...], g_ref[...], m_ref[...], v_ref[...])

    spec = pl.BlockSpec((tr, cdim), lambda i: (i, 0))
    sd = jax.ShapeDtypeStruct((r, cdim), F32)
    return pl.pallas_call(
        body, name="adamw", grid=(r // tr,), in_specs=[spec] * 4, out_specs=(spec,) * 3, out_shape=(sd,) * 3,
        compiler_params=_params(("parallel",)),
    )(w, g, m, v)


def _adamw_halves(w, mine, theirs, m, v, core, tr):
    r, cdim = w.shape
    nbh = r // 2 // tr

    def body(c_ref, w_ref, a_ref, b_ref, m_ref, v_ref, g_ref, d_ref, nm_ref, nv_ref):
        is_mine = pl.program_id(0) // nbh == c_ref[0]
        g = jnp.where(is_mine, a_ref[...], b_ref[...])
        g_ref[...] = g
        d_ref[...], nm_ref[...], nv_ref[...] = _adamw_update(w_ref[...], g, m_ref[...], v_ref[...])

    spec = pl.BlockSpec((tr, cdim), lambda i, c: (i, 0))
    half = lambda own: pl.BlockSpec(
        (tr, cdim), lambda i, c: (jnp.clip(i - (c[0] if own else 1 - c[0]) * nbh, 0, nbh - 1), 0))
    sd = jax.ShapeDtypeStruct((r, cdim), F32)
    return pl.pallas_call(
        body, name="adamw_halves",
        grid_spec=pltpu.PrefetchScalarGridSpec(num_scalar_prefetch=1, grid=(r // tr,),
                                               in_specs=[spec, half(True), half(False), spec, spec],
                                               out_specs=(spec,) * 4),
        out_shape=(sd,) * 4,
        compiler_params=_params(("parallel",)),
    )(core, w, mine, theirs, m, v)


V_B_GATE, V_CONV_B, V_LAM, V_G_FINAL, V_CONV_W, V_LOSS, V_ROWS = 0, 2, 3, 4, 5, 9, 64
M_W_A, M_W_X, M_B_A, M_B_X, M_ROWS = 0, H * DH, 2 * H * DH, 2 * H * DH + H, 2112
SMALL = ("g_norm", "b_mod", "b_gate", "conv_b", "lam", "g_final", "conv_w", "w_a", "w_x", "b_a", "b_x")


def _adamw_small(redv, redm, g_conv, g_gnorm, g_bmod, wmv):
    def grad(name, rv, rm, gc, gg, gb):
        if name == "g_norm":
            return gg[...]
        if name == "b_mod":
            return gb[...]
        if name == "b_gate":
            return jnp.concatenate([rv[V_B_GATE + t:V_B_GATE + t + 1, :] for t in range(2)], axis=1)
        if name == "conv_b":
            return rv[V_CONV_B:V_CONV_B + 1, :]
        if name == "lam":
            return rv[V_LAM:V_LAM + 1, :]
        if name == "g_final":
            return rv[V_G_FINAL:V_G_FINAL + 1, :]
        if name == "conv_w":
            return gc[...]
        if name == "w_a":
            return rm[M_W_A:M_W_A + H * DH, :]
        if name == "w_x":
            return rm[M_W_X:M_W_X + H * DH, :]
        if name == "b_a":
            return rm[M_B_A:M_B_A + H, :]
        return rm[M_B_X:M_B_X + H, :]

    n = len(SMALL)

    def body(*refs):
        rv, rm, gc, gg, gb = refs[:5]
        ins, outs = refs[5:5 + 3 * n], refs[5 + 3 * n:]
        for t, name in enumerate(SMALL):
            w_ref, m_ref, v_ref = ins[3 * t:3 * t + 3]
            g_out, d_out, m_out, v_out = outs[4 * t:4 * t + 4]
            g = grad(name, rv, rm, gc, gg, gb)
            g_out[...] = g
            d_out[...], m_out[...], v_out[...] = _adamw_update(w_ref[...], g, m_ref[...], v_ref[...])

    vm = pl.BlockSpec(memory_space=pltpu.VMEM)
    flat = [a for name in SMALL for a in wmv[name]]
    shapes = [jax.ShapeDtypeStruct(wmv[name][0].shape, F32) for name in SMALL for _ in range(4)]
    outs = pl.pallas_call(
        body, name="adamw_small", out_shape=tuple(shapes),
        in_specs=[vm] * (5 + len(flat)), out_specs=tuple([vm] * len(shapes)),
        compiler_params=_params(),
    )(redv, redm, g_conv, g_gnorm, g_bmod, *flat)
    return {name: outs[4 * t:4 * t + 4] for t, name in enumerate(SMALL)}


def _rope_tables(positions):
    inv_freq = ROPE_THETA ** (-jnp.arange(0, ROT, 2, dtype=F32) / ROT)
    ang = positions.astype(F32)[:, None] * inv_freq
    cos, sin = lax.optimization_barrier((jnp.cos(ang), jnp.sin(ang)))
    n = positions.shape[0]
    half = ROT // 2
    rc = jnp.concatenate([cos, cos, jnp.ones((n, DH - ROT), F32)], axis=1)
    rsa = jnp.concatenate([-sin, jnp.zeros((n, DH - half), F32)], axis=1)
    rsb = jnp.concatenate([jnp.zeros((n, half), F32), sin, jnp.zeros((n, DH - ROT), F32)], axis=1)
    return rc, rsa, rsb


def kernel(x, c, positions, g_norm, w_mod, b_mod, w_in, b_gate, conv_w, conv_b, w_a, b_a, w_x, b_x, lam, w_out_rnn, w_out_attn, w_o, g_final, loss_target, m_g_norm, m_w_mod, m_b_mod, m_w_in, m_b_gate, m_conv_w, m_conv_b, m_w_a, m_b_a, m_w_x, m_b_x, m_lam, m_w_out_rnn, m_w_out_attn, m_w_o, m_g_final, v_g_norm, v_w_mod, v_b_mod, v_w_in, v_b_gate, v_conv_w, v_conv_b, v_w_a, v_b_a, v_w_x, v_b_x, v_lam, v_w_out_rnn, v_w_out_attn, v_w_o, v_g_final):
    s = x.shape[1]
    xi = lax.axis_index("x")
    yi = lax.axis_index("y")
    ci = lax.axis_index("c")
    shard = 2 * xi + yi
    x2d = x[0]
    tgt = loss_target[0]
    pos = positions[0]

    c_all, mod4, conv_all = _mod_fwd(c, w_mod[0], b_mod.reshape(4, 1, 768), conv_w[0])
    mod = mod4.reshape(1, 3 * D)
    shift, scale, gate = mod[:, :D], mod[:, D:2 * D], mod[:, 2 * D:]
    w3sh = jnp.concatenate([w_out_rnn[0], w_out_attn[0], w_o[0]], axis=0).astype(BF16)
    wsh = w_in[0].astype(BF16)
    conv_full = conv_all[0::2].transpose(1, 0, 2).reshape(4, D)

    order = jnp.stack([shard, shard ^ 2, shard ^ 1, shard ^ 3]).astype(jnp.int32)
    p, hbf, wg = _gather_norm_inproj(x2d, g_norm, shift, scale, wsh, order)
    rc, rsa, rsb = _rope_tables(pos)
    pos_col = jnp.broadcast_to((pos == 0).astype(F32)[:, None], (s, DH))
    b_a3, b_x3 = b_a.reshape(H, 1, DH), b_x.reshape(H, 1, DH)
    hr, gr = _rnn_fwd(p, pos_col, conv_full, conv_b, w_a[0], b_a3, w_x[0], b_x3, lam)
    o, lse, ga, q_rot, k_rot, w3g = _attn_fwd(p, rc, rsa, rsb, w3sh)
    w3g = lax.dynamic_update_slice(w3g, w3sh[None], (shard, 0, 0))
    w3 = w3g.reshape(4, 3, 256, D).transpose(1, 0, 2, 3).reshape(3, D, D)

    dgr, dga, dc, dx2, vec_t, g_out = _tail(gr, ga, p, x2d, tgt, w3, b_gate, gate, g_final.reshape(1, D))

    dxr, dzr, g_wa, g_ba, g_wx, g_bx, g_lam, g_cw, g_cb = _rnn_bwd(
        p, hr, dgr, pos_col, conv_full, conv_b, w_a[0], b_a3, w_x[0], b_x3, lam)
    vpack = jnp.concatenate([
        vec_t[2:4],
        g_cb.reshape(1, D),
        g_lam.reshape(1, D),
        vec_t[0:1],
        g_cw.transpose(1, 0, 2).reshape(4, D),
        vec_t[4:5],
        jnp.zeros((V_ROWS - 10, D), F32)], axis=0)
    mpack = jnp.concatenate([
        g_wa.reshape(H * DH, DH), g_wx.reshape(H * DH, DH), g_ba.reshape(H, DH), g_bx.reshape(H, DH),
        jnp.zeros((M_ROWS - 2 * H * DH - 2 * H, DH), F32)], axis=0)
    (dq, dk, dv, dza), (redv, redm) = _attn_bwd(p, q_rot, k_rot, o, lse, dga, rc, rsa, rsb, [vpack, mpack])

    pieces = [dxr, dzr, dq, dk, dv, dza, dc]
    g_win, rb_a, rb_b = _inproj_bwd_w(pieces, hbf, g_out)

    core = ci.reshape(1)
    shard_core = jnp.stack([shard, ci]).astype(jnp.int32)
    qh_a, qh_b = _add_half(g_win, rb_a, core, tr=512), _add_half(g_out, rb_b, core, tr=384)
    grad_x, vec_n, (r_a, r_b) = _inproj_bwd_x(pieces, wg, wsh, x2d, dx2, g_norm, scale, [qh_a, qh_b])
    f_a = _sum_slots(g_win, rb_a, r_a, shard_core, tr=512)
    f_b = _sum_slots(g_out, rb_b, r_b, shard_core, tr=384)
    s_a, s_b = _pair_swap([f_a, f_b])

    dmod_row = jnp.concatenate([vec_n[0:1], vec_n[1:2], vec_t[1:2]], axis=1)
    loss = redv[V_LOSS, 0]
    grad_w_mod, g_bmod4, g_gnorm = _mod_bwd(dmod_row.reshape(4, 1, 768), vec_n[2:3], c_all)
    g_conv_sh = lax.dynamic_slice_in_dim(redv[V_CONV_W:V_CONV_W + 4], shard * 256, 256, axis=1)

    shape2d = dict(g_norm=(1, D), b_mod=(1, 3 * D), b_gate=(1, 2 * D), conv_b=(1, D), lam=(1, D), g_final=(1, D),
                   conv_w=(4, 256), w_a=(H * DH, DH), w_x=(H * DH, DH), b_a=(H, DH), b_x=(H, DH))
    given = dict(
        g_norm=(g_norm, m_g_norm, v_g_norm), b_mod=(b_mod, m_b_mod, v_b_mod), b_gate=(b_gate, m_b_gate, v_b_gate),
        conv_b=(conv_b, m_conv_b, v_conv_b), lam=(lam, m_lam, v_lam), g_final=(g_final, m_g_final, v_g_final),
        conv_w=(conv_w, m_conv_w, v_conv_w), w_a=(w_a, m_w_a, v_w_a), w_x=(w_x, m_w_x, v_w_x),
        b_a=(b_a, m_b_a, v_b_a), b_x=(b_x, m_b_x, v_b_x))
    small = _adamw_small(redv, redm, g_conv_sh, g_gnorm, g_bmod4.reshape(1, 3 * D),
                         {n: tuple(a.reshape(shape2d[n]) for a in given[n]) for n in SMALL})

    big_in = _adamw_halves(w_in[0], f_a, s_a, m_w_in[0], v_w_in[0], core, tr=256)
    big_mod = (grad_w_mod,) + tuple(_adamw(w_mod[0], grad_w_mod, m_w_mod[0], v_w_mod[0], tr=256))
    w3f = jnp.concatenate([w_out_rnn[0], w_out_attn[0], w_o[0]], axis=0)
    m3f = jnp.concatenate([m_w_out_rnn[0], m_w_out_attn[0], m_w_o[0]], axis=0)
    v3f = jnp.concatenate([v_w_out_rnn[0], v_w_out_attn[0], v_w_o[0]], axis=0)
    big_out = _adamw_halves(w3f, f_b, s_b, m3f, v3f, core, tr=384)

    names = ["g_norm", "w_mod", "b_mod", "w_in", "b_gate", "conv_w", "conv_b", "w_a", "b_a", "w_x", "b_x", "lam",
             "w_out_rnn", "w_out_attn", "w_o", "g_final"]
    outs = []
    for idx in range(4):
        d = {n: small[n][idx].reshape(given[n][0].shape) for n in SMALL}
        d.update(w_mod=big_mod[idx][None], w_in=big_in[idx][None],
                 w_out_rnn=big_out[idx][0:256][None], w_out_attn=big_out[idx][256:512][None],
                 w_o=big_out[idx][512:768][None])
        outs.append(d)
    flat = [d[n] for d in outs for n in names]
    return (loss, grad_x[None], *flat)
```

```python
import jax
import jax.numpy as jnp
from jax import lax
from jax.experimental import pallas as pl
from jax.experimental.pallas import tpu as pltpu

F32, BF16 = jnp.float32, jnp.bfloat16
MESH = pl.DeviceIdType.MESH
HIGHEST = lax.Precision.HIGHEST

D = 1024
H = 8
DH = 128
PW = 2048
EPS = 1e-6
LRU_C = 8.0
SCALE = DH ** -0.5
NEG = -1e30
SPAN = 2048
UB = 128
DILATIONS = (1, 4, 16)
UNIT_BATCH = 16
UNIT_BATCH_FWD = 8
ROPE_THETA = 500000.0
ROT = 32

LR, B1, B2, ADAM_EPS, WD, STEP = 0.001, 0.9, 0.999, 1e-08, 0.01, 10

NDEV = 8


def _params(sem=None, vmem_mb=56):
    return pltpu.CompilerParams(dimension_semantics=sem, vmem_limit_bytes=vmem_mb * 2 ** 20)


def _coords():
    return lax.axis_index("x"), lax.axis_index("y"), lax.axis_index("c")


def _flip(v, bit):
    return 1 - v if bit else v


def _peer(k):
    x, y, c = _coords()
    return (_flip(x, (k >> 2) & 1), _flip(y, (k >> 1) & 1), _flip(c, k & 1))


def _my_index():
    x, y, c = _coords()
    return 4 * x + 2 * y + c


def _rcopy(src, dst, ssem, rsem, dev):
    return pltpu.make_async_remote_copy(src_ref=src, dst_ref=dst, send_sem=ssem, recv_sem=rsem,
                                        device_id=dev, device_id_type=MESH)


def _sigmoid(x):
    return jax.nn.sigmoid(x)


def _dot(a, b):
    return jnp.dot(a, b, preferred_element_type=F32)


def _dot_nt(a, b):
    return lax.dot_general(a, b, (((1,), (1,)), ((), ())), preferred_element_type=F32)


def _dot_tn(a, b):
    return lax.dot_general(a, b, (((0,), (0,)), ((), ())), preferred_element_type=F32)


def _colsum(a):
    return jnp.sum(a, axis=0, keepdims=True)


def _mod_fwd(c, w_mod_sh, b_mod4, conv_sh):
    def body(c_ref, w_ref, b_ref, cv_ref, call_ref, mod_ref, cvall_ref, rows_ref, cmat_ref, s1, r1, s2, r2, s3, r3):
        x, y, _ = _coords()
        me = _my_index()
        j = 2 * x + y
        call_ref[me] = c_ref[...]
        cvall_ref[me] = cv_ref[...]
        sends = []
        for k in range(1, NDEV):
            cp = _rcopy(call_ref.at[me], call_ref.at[me], s1.at[k - 1], r1.at[k - 1], _peer(k))
            cp.start()
            sends.append(cp)
            cp = _rcopy(cvall_ref.at[me], cvall_ref.at[me], s3.at[k - 1], r3.at[k - 1], _peer(k))
            cp.start()
            sends.append(cp)
        for k in range(1, NDEV):
            pk = me ^ k
            _rcopy(call_ref.at[pk], call_ref.at[pk], s1.at[k - 1], r1.at[k - 1], _peer(k)).wait_recv()
        for b in range(NDEV):
            cmat_ref[pl.ds(b, 1), :] = call_ref[b]
        cm = cmat_ref[...]
        act = cm * _sigmoid(cm)
        mp = jnp.dot(act, w_ref[...], preferred_element_type=F32, precision=HIGHEST) + b_ref[j]
        for b in range(NDEV):
            rows_ref[b] = mp[b:b + 1]
        mod_ref[j] = rows_ref[me]
        for q, k in enumerate((2, 4, 6)):
            cp = _rcopy(rows_ref.at[me ^ k], mod_ref.at[j], s2.at[q], r2.at[q], _peer(k))
            cp.start()
            sends.append(cp)
        for q, k in enumerate((2, 4, 6)):
            jq = j ^ (k >> 1)
            _rcopy(rows_ref.at[me], mod_ref.at[jq], s2.at[q], r2.at[q], _peer(k)).wait_recv()
        for k in range(1, NDEV):
            pk = me ^ k
            _rcopy(cvall_ref.at[pk], cvall_ref.at[pk], s3.at[k - 1], r3.at[k - 1], _peer(k)).wait_recv()
        for cp in sends:
            cp.wait_send()

    vm = pl.BlockSpec(memory_space=pltpu.VMEM)
    return pl.pallas_call(
        body, name="mod_fwd",
        out_shape=(jax.ShapeDtypeStruct((NDEV, 1, D), F32), jax.ShapeDtypeStruct((4, 1, 768), F32),
                   jax.ShapeDtypeStruct((NDEV,) + conv_sh.shape, F32)),
        in_specs=[vm, vm, vm, vm], out_specs=(vm, vm, vm),
        scratch_shapes=[pltpu.VMEM((NDEV, 1, 768), F32), pltpu.VMEM((NDEV, D), F32),
                        pltpu.SemaphoreType.DMA((7,)), pltpu.SemaphoreType.DMA((7,)),
                        pltpu.SemaphoreType.DMA((3,)), pltpu.SemaphoreType.DMA((3,)),
                        pltpu.SemaphoreType.DMA((7,)), pltpu.SemaphoreType.DMA((7,))],
        compiler_params=_params(),
    )(c, w_mod_sh, b_mod4, conv_sh)


def _mod_bwd(dmod4, gn_row, c_all):
    def body(d_ref, g_ref, call_ref, gw_ref, gb_ref, gg_ref, dall_ref, gall_ref, cmat_ref, dmat_ref, s1, r1, s2, r2):
        x, y, _ = _coords()
        me = _my_index()
        j = 2 * x + y
        dall_ref[me] = d_ref[...]
        gall_ref[me] = g_ref[...]
        sends = []
        for k in range(1, NDEV):
            for buf, ss, rs in ((dall_ref, s1, r1), (gall_ref, s2, r2)):
                cp = _rcopy(buf.at[me], buf.at[me], ss.at[k - 1], rs.at[k - 1], _peer(k))
                cp.start()
                sends.append(cp)
        for k in range(1, NDEV):
            pk = me ^ k
            for buf, ss, rs in ((dall_ref, s1, r1), (gall_ref, s2, r2)):
                _rcopy(buf.at[pk], buf.at[pk], ss.at[k - 1], rs.at[k - 1], _peer(k)).wait_recv()
        for cp in sends:
            cp.wait_send()
        gb, gg = dall_ref[0], gall_ref[0]
        for b in range(1, NDEV):
            gb = gb + dall_ref[b]
            gg = gg + gall_ref[b]
        gb_ref[...] = gb
        gg_ref[...] = gg
        for b in range(NDEV):
            cmat_ref[pl.ds(b, 1), :] = call_ref[b]
            dmat_ref[pl.ds(b, 1), :] = dall_ref[b, j]
        cm = cmat_ref[...]
        act = cm * _sigmoid(cm)
        gw_ref[...] = lax.dot_general(act, dmat_ref[...], (((0,), (0,)), ((), ())),
                                      preferred_element_type=F32, precision=HIGHEST)

    vm = pl.BlockSpec(memory_space=pltpu.VMEM)
    return pl.pallas_call(
        body, name="mod_bwd",
        out_shape=(jax.ShapeDtypeStruct((D, 768), F32), jax.ShapeDtypeStruct((4, 1, 768), F32),
                   jax.ShapeDtypeStruct((1, D), F32)),
        in_specs=[vm, vm, vm], out_specs=(vm, vm, vm),
        scratch_shapes=[pltpu.VMEM((NDEV, 4, 1, 768), F32), pltpu.VMEM((NDEV, 1, D), F32),
                        pltpu.VMEM((NDEV, D), F32), pltpu.VMEM((NDEV, 768), F32),
                        pltpu.SemaphoreType.DMA((7,)), pltpu.SemaphoreType.DMA((7,)),
                        pltpu.SemaphoreType.DMA((7,)), pltpu.SemaphoreType.DMA((7,))],
        compiler_params=_params(),
    )(dmod4, gn_row, c_all)


def _gather_norm_inproj(x, gn, shift, scale, wsh, order, tm=1024, tn=1024):
    s = x.shape[0]
    ni = s // tm
    npc = PW // tn
    rows, cols = wsh.shape
    half = rows // 2
    nch = 4
    cr = half // nch
    chips = ((1, 0), (0, 1), (1, 1))

    def body(ord_ref, x_ref, gn_ref, sh_ref, sc_ref, wsh_hbm, p_ref, h_ref, wg_hbm, hs_all, w_s, wsem, ss, rs):
        slot, i, col = pl.program_id(0), pl.program_id(1), pl.program_id(2)
        cx, cy, cc = _coords()
        j = 2 * cx + cy
        sib = (cx, cy, 1 - cc)
        mine = lambda n: pl.ds(cc * half + n * cr, cr)
        theirs = lambda n: pl.ds((1 - cc) * half + n * cr, cr)
        shard_of = lambda q: j ^ (2 * chips[q][0] + chips[q][1])

        def to_chip(q, n):
            e = nch * q + n
            return _rcopy(wsh_hbm.at[mine(n)], wg_hbm.at[j, mine(n)], ss.at[e], rs.at[e],
                          (_flip(cx, chips[q][0]), _flip(cy, chips[q][1]), cc))

        def from_chip(q, n):
            e = nch * q + n
            return _rcopy(wsh_hbm.at[mine(n)], wg_hbm.at[shard_of(q), mine(n)], ss.at[e], rs.at[e], sib)

        def to_sibling(q, n):
            e = 3 * nch + nch * q + n
            return _rcopy(wg_hbm.at[shard_of(q), mine(n)], wg_hbm.at[shard_of(q), mine(n)], ss.at[e], rs.at[e], sib)

        def from_sibling(q, n):
            e = 3 * nch + nch * q + n
            return _rcopy(wsh_hbm.at[mine(n)], wg_hbm.at[shard_of(q), theirs(n)], ss.at[e], rs.at[e], sib)

        def load(sl, src):
            cp = pltpu.make_async_copy(src, w_s.at[sl], wsem.at[sl])
            cp.start()
            cp.wait()

        first = (i == 0) & (col == 0)

        @pl.when(first & (slot == 0))
        def _():
            for n in range(nch):
                for q in (0, 1):
                    to_chip(q, n).start()
            load(0, wsh_hbm.at[pl.ds(0, D), :])

        @pl.when(first & (slot == 1))
        def _():
            for q in (0, 1):
                for n in range(nch):
                    from_chip(q, n).wait_recv()
                    to_sibling(q, n).start()
            for n in range(nch):
                to_chip(2, n).start()
            for n in range(nch):
                from_sibling(0, n).wait_recv()
            load(1, wg_hbm.at[shard_of(0), pl.ds(0, D), :])

        @pl.when(first & (slot == 2))
        def _():
            for n in range(nch):
                from_sibling(1, n).wait_recv()
            load(2, wg_hbm.at[shard_of(1), pl.ds(0, D), :])

        @pl.when(first & (slot == 3))
        def _():
            for n in range(nch):
                from_chip(2, n).wait_recv()
                to_sibling(2, n).start()
            for n in range(nch):
                from_sibling(2, n).wait_recv()
            load(3, wg_hbm.at[shard_of(2), pl.ds(0, D), :])
            for q in range(3):
                for n in range(nch):
                    to_chip(q, n).wait_send()
                    to_sibling(q, n).wait_send()

        @pl.when((slot == 0) & (col == 0))
        def _():
            xt = x_ref[...]
            rstd = lax.rsqrt(jnp.mean(xt * xt, axis=-1, keepdims=True) + EPS)
            h = ((xt * rstd * gn_ref[...]) * (1.0 + sc_ref[...]) + sh_ref[...]).astype(BF16)
            hs_all[i] = h
            h_ref[...] = h

        p_ref[0] = _dot(hs_all[i], w_s[slot, :, pl.ds(pl.multiple_of(col * tn, tn), tn)]).astype(BF16)

    row = pl.BlockSpec((1, D), lambda sl, i, col, o: (0, 0))
    x_rows = lambda sl, i, col, o: (jnp.where(sl == 0, i, ni - 1), 0)
    any_ = pl.BlockSpec(memory_space=pl.ANY)
    return pl.pallas_call(
        body, name="gather_norm_inproj",
        grid_spec=pltpu.PrefetchScalarGridSpec(
            num_scalar_prefetch=1, grid=(4, ni, npc),
            in_specs=[pl.BlockSpec((tm, D), x_rows), row, row, row, any_],
            out_specs=(pl.BlockSpec((1, tm, tn), lambda sl, i, col, o: (o[sl], i, col)),
                       pl.BlockSpec((tm, D), x_rows), any_),
            scratch_shapes=[pltpu.VMEM((ni, tm, D), BF16), pltpu.VMEM((4, D, PW), BF16),
                            pltpu.SemaphoreType.DMA((4,)),
                            pltpu.SemaphoreType.DMA((6 * nch,)), pltpu.SemaphoreType.DMA((6 * nch,))]),
        out_shape=(jax.ShapeDtypeStruct((4, s, PW), BF16), jax.ShapeDtypeStruct((s, D), BF16),
                   jax.ShapeDtypeStruct((4, rows, cols), wsh.dtype)),
        compiler_params=_params(("arbitrary", "arbitrary", "arbitrary")),
    )(order, x, gn, shift, scale, wsh)


def _shift_down(prev8, cur, d):
    t = cur.shape[0]
    c3 = cur.reshape(t // 8, 8, DH)
    rot = pltpu.roll(c3, d, 1)
    before = jnp.concatenate([pltpu.roll(prev8, d, 0).reshape(1, 8, DH), rot[:-1]], axis=0)
    rows = lax.broadcasted_iota(jnp.int32, c3.shape, 1)
    return jnp.where(rows >= d, rot, before).reshape(t, DH)


def _shift_up(cur, next8, d):
    t = cur.shape[0]
    c3 = cur.reshape(t // 8, 8, DH)
    rot = pltpu.roll(c3, 8 - d, 1)
    after = jnp.concatenate([rot[1:], pltpu.roll(next8, 8 - d, 0).reshape(1, 8, DH)], axis=0)
    rows = lax.broadcasted_iota(jnp.int32, c3.shape, 1)
    return jnp.where(rows < 8 - d, rot, after).reshape(t, DH)


def _rnn_gates(xr, prev8, cw, cb, wa, ba, wx, bx, lam, reset):
    xc = cw[3:4] * xr + cb
    for d in (1, 2, 3):
        xc = xc + cw[3 - d:4 - d] * _shift_down(prev8, xr, d)
    xcb = xc.astype(BF16)
    r = _sigmoid(_dot(xcb, wa.astype(BF16)) + ba)
    ig = _sigmoid(_dot(xcb, wx.astype(BF16)) + bx)
    nl = -lam
    sp = jnp.maximum(nl, 0.0) + jnp.log1p(jnp.exp(-jnp.abs(nl)))
    log_a = (-LRU_C * r) * sp
    a_raw = jnp.exp(log_a)
    a = jnp.where(reset, 0.0, a_raw)
    mult = jnp.where(reset, 1.0, jnp.sqrt(1.0 - a_raw * a_raw))
    return xc, r, ig, sp, a, mult


def _log_scan(a, b, axis, up):
    n = a.shape[axis]
    rows = lax.broadcasted_iota(jnp.int32, a.shape, axis)
    d = 1
    while d < n:
        m = rows < n - d if up else rows >= d
        shift = n - d if up else d
        a_s = pltpu.roll(a, shift, axis)
        b_s = pltpu.roll(b, shift, axis)
        b = jnp.where(m, a * b_s + b, b)
        a = jnp.where(m, a * a_s, a)
        d *= 2
    return a, b


def _scan(a, b, t, edge, up=False):
    g = t // 8
    a3, b3 = _log_scan(a.reshape(g, 8, DH), b.reshape(g, 8, DH), 1, up)
    last = 0 if up else 7
    ag, bg = _log_scan(a3[:, last, :], b3[:, last, :], 0, up)
    hg = ag * edge + bg
    grp = lax.broadcasted_iota(jnp.int32, hg.shape, 0)
    if up:
        cin = jnp.where(grp == g - 1, edge, pltpu.roll(hg, g - 1, 0))
        tail = hg[0:1]
    else:
        cin = jnp.where(grp == 0, edge, pltpu.roll(hg, 1, 0))
        tail = hg[g - 1:g]
    return (a3 * cin[:, None, :] + b3).reshape(t, DH), tail


def _rnn_fwd(p, pos, conv_w, conv_b, w_a, b_a, w_x, b_x, lam, tt=1024):
    s = p.shape[1]
    nt = s // tt

    def body(xr_ref, z_ref, pos_ref, cw_ref, cb_ref, wa_ref, ba_ref, wx_ref, bx_ref, lam_ref,
             hr_ref, gr_ref, xprev, hprev):
        @pl.when(pl.program_id(1) == 0)
        def _():
            xprev[...] = jnp.zeros_like(xprev)
            hprev[...] = jnp.zeros_like(hprev)

        xr = xr_ref[0].astype(F32)
        z = z_ref[0].astype(F32)
        reset = pos_ref[...] > 0.5
        xc, r, ig, sp, a, mult = _rnn_gates(xr, xprev[...], cw_ref[...], cb_ref[...], wa_ref[0], ba_ref[0],
                                            wx_ref[0], bx_ref[0], lam_ref[...], reset)
        bx = mult * ig * xc
        h, h_last = _scan(a, bx, tt, hprev[0:1])
        xprev[...] = xr[tt - 8:]
        hprev[...] = jnp.broadcast_to(h_last, (8, DH))
        hr_ref[...] = h
        gr_ref[...] = (h * (z * _sigmoid(z))).astype(BF16)

    head_row = lambda hh, t: (0, hh)
    return pl.pallas_call(
        body, name="rnn_fwd", grid=(H, nt),
        in_specs=[pl.BlockSpec((1, tt, DH), lambda hh, t: (0, t, hh)),
                  pl.BlockSpec((1, tt, DH), lambda hh, t: (0, t, H + hh)),
                  pl.BlockSpec((tt, DH), lambda hh, t: (t, 0)),
                  pl.BlockSpec((4, DH), head_row), pl.BlockSpec((1, DH), head_row),
                  pl.BlockSpec((1, DH, DH), lambda hh, t: (hh, 0, 0)), pl.BlockSpec((1, 1, DH), lambda hh, t: (hh, 0, 0)),
                  pl.BlockSpec((1, DH, DH), lambda hh, t: (hh, 0, 0)), pl.BlockSpec((1, 1, DH), lambda hh, t: (hh, 0, 0)),
                  pl.BlockSpec((1, DH), head_row)],
        out_specs=(pl.BlockSpec((tt, DH), lambda hh, t: (t, hh)), pl.BlockSpec((tt, DH), lambda hh, t: (t, hh))),
        out_shape=(jax.ShapeDtypeStruct((s, D), F32), jax.ShapeDtypeStruct((s, D), BF16)),
        scratch_shapes=[pltpu.VMEM((8, DH), F32), pltpu.VMEM((8, DH), F32)],
        compiler_params=_params(("parallel", "arbitrary")),
    )(p, p, pos, conv_w, conv_b, w_a, b_a, w_x, b_x, lam)


def _rnn_bwd(p, hr, dgr, pos, conv_w, conv_b, w_a, b_a, w_x, b_x, lam, tt=1024):
    s = p.shape[1]
    nt = s // tt
    t8 = tt // 8

    def body(xr_ref, z_ref, xp_ref, hr_ref, hp_ref, dg_ref, pos_ref, cw_ref, cb_ref, wa_ref, ba_ref, wx_ref, bx_ref,
             lam_ref, dxr_ref, dz_ref, gwa_ref, gba_ref, gwx_ref, gbx_ref, glam_ref, gcw_ref, gcb_ref,
             a_next, g_next, dxc_next):
        t = pl.program_id(1)
        has_prev = t < nt - 1

        @pl.when(t == 0)
        def _():
            a_next[...] = jnp.zeros_like(a_next)
            g_next[...] = jnp.zeros_like(g_next)
            dxc_next[...] = jnp.zeros_like(dxc_next)
            gwa_ref[...] = jnp.zeros_like(gwa_ref)
            gba_ref[...] = jnp.zeros_like(gba_ref)
            gwx_ref[...] = jnp.zeros_like(gwx_ref)
            gbx_ref[...] = jnp.zeros_like(gbx_ref)
            glam_ref[...] = jnp.zeros_like(glam_ref)
            gcw_ref[...] = jnp.zeros_like(gcw_ref)
            gcb_ref[...] = jnp.zeros_like(gcb_ref)

        xr = xr_ref[0].astype(F32)
        z = z_ref[0].astype(F32)
        hr_blk = hr_ref[...]
        dg = dg_ref[...]
        xprev = jnp.where(has_prev, xp_ref[0].astype(F32)[8:], 0.0)
        hprev8 = jnp.where(has_prev, hp_ref[...], 0.0)
        reset = pos_ref[...] > 0.5
        cw = cw_ref[...]
        wa = wa_ref[0]
        wx = wx_ref[0]
        lam_v = lam_ref[...]
        xc, r, ig, sp, a, mult = _rnn_gates(xr, xprev, cw, cb_ref[...], wa, ba_ref[0], wx, bx_ref[0], lam_v, reset)

        sz = _sigmoid(z)
        dh = dg * (z * sz)
        dz_ref[...] = (dg * hr_blk * (sz * (1.0 + z * (1.0 - sz)))).astype(BF16)

        an = _shift_up(a, a_next[...], 1)
        g, g_first = _scan(an, dh, tt, g_next[0:1], up=True)
        a_next[...] = jnp.broadcast_to(a[0:1], (8, DH))
        g_next[...] = jnp.broadcast_to(g_first, (8, DH))

        hm1 = _shift_down(hprev8, hr_blk, 1)
        da = g * hm1
        dmult = g * (ig * xc)
        di = g * (mult * xc)
        dxc = g * (mult * ig)
        dla = jnp.where(reset, 0.0, da * a - dmult * (a * a) / mult)
        dr = dla * (-LRU_C * sp)
        dsp = _colsum(dla * (-LRU_C * r))
        glam_ref[0] += dsp * (-_sigmoid(-lam_v))
        dpa = dr * r * (1.0 - r)
        dpx = di * ig * (1.0 - ig)
        dpab = dpa.astype(BF16)
        dpxb = dpx.astype(BF16)
        dxc = dxc + _dot_nt(dpab, wa.astype(BF16)) + _dot_nt(dpxb, wx.astype(BF16))
        xcb = xc.astype(BF16)
        gwa_ref[0] += _dot_tn(xcb, dpab)
        gwx_ref[0] += _dot_tn(xcb, dpxb)
        gba_ref[0] += _colsum(dpa)
        gbx_ref[0] += _colsum(dpx)

        dxr = cw[3:4] * dxc
        for d in (1, 2, 3):
            dxr = dxr + cw[3 - d:4 - d] * _shift_up(dxc, dxc_next[...], d)
        dxr_ref[...] = dxr.astype(BF16)
        dxc_next[...] = dxc[0:8]
        gcb_ref[0] += _colsum(dxc)
        gcw_ref[0, 3:4, :] += _colsum(xr * dxc)
        for d in (1, 2, 3):
            gcw_ref[0, 3 - d:4 - d, :] += _colsum(_shift_down(xprev, xr, d) * dxc)

    rt = lambda t: nt - 1 - t
    prev8 = lambda t: jnp.maximum(rt(t) * t8 - 1, 0)
    head_row = lambda hh, t: (0, hh)
    hsm = lambda hh, t: (hh, 0, 0)
    return pl.pallas_call(
        body, name="rnn_bwd", grid=(H, nt),
        in_specs=[pl.BlockSpec((1, tt, DH), lambda hh, t: (0, rt(t), hh)),
                  pl.BlockSpec((1, tt, DH), lambda hh, t: (0, rt(t), H + hh)),
                  pl.BlockSpec((1, 16, DH), lambda hh, t: (0, jnp.maximum(rt(t) * (tt // 16) - 1, 0), hh)),
                  pl.BlockSpec((tt, DH), lambda hh, t: (rt(t), hh)),
                  pl.BlockSpec((8, DH), lambda hh, t: (prev8(t), hh)),
                  pl.BlockSpec((tt, DH), lambda hh, t: (rt(t), hh)),
                  pl.BlockSpec((tt, DH), lambda hh, t: (rt(t), 0)),
                  pl.BlockSpec((4, DH), head_row), pl.BlockSpec((1, DH), head_row),
                  pl.BlockSpec((1, DH, DH), hsm), pl.BlockSpec((1, 1, DH), hsm),
                  pl.BlockSpec((1, DH, DH), hsm), pl.BlockSpec((1, 1, DH), hsm),
                  pl.BlockSpec((1, DH), head_row)],
        out_specs=(pl.BlockSpec((tt, DH), lambda hh, t: (rt(t), hh)), pl.BlockSpec((tt, DH), lambda hh, t: (rt(t), hh)),
                   pl.BlockSpec((1, DH, DH), hsm), pl.BlockSpec((1, 1, DH), hsm),
                   pl.BlockSpec((1, DH, DH), hsm), pl.BlockSpec((1, 1, DH), hsm),
                   pl.BlockSpec((1, 1, DH), hsm), pl.BlockSpec((1, 4, DH), hsm), pl.BlockSpec((1, 1, DH), hsm)),
        out_shape=(jax.ShapeDtypeStruct((s, D), BF16), jax.ShapeDtypeStruct((s, D), BF16),
                   jax.ShapeDtypeStruct((H, DH, DH), F32), jax.ShapeDtypeStruct((H, 1, DH), F32),
                   jax.ShapeDtypeStruct((H, DH, DH), F32), jax.ShapeDtypeStruct((H, 1, DH), F32),
                   jax.ShapeDtypeStruct((H, 1, DH), F32), jax.ShapeDtypeStruct((H, 4, DH), F32),
                   jax.ShapeDtypeStruct((H, 1, DH), F32)),
        scratch_shapes=[pltpu.VMEM((8, DH), F32), pltpu.VMEM((8, DH), F32), pltpu.VMEM((8, DH), F32)],
        compiler_params=_params(("parallel", "arbitrary")),
    )(p, p, p, hr, hr, dgr, pos, conv_w, conv_b, w_a, b_a, w_x, b_x, lam)


def _rope(t, c, sa, sb):
    return t * c + pltpu.roll(t, DH - ROT // 2, 1) * sa + pltpu.roll(t, ROT // 2, 1) * sb


def _rope_bwd(g, c, sa, sb):
    return g * c + pltpu.roll(g * sa, ROT // 2, 1) + pltpu.roll(g * sb, DH - ROT // 2, 1)


def _unit_bases(gi, u):
    dil = DILATIONS[gi]
    if dil == 1:
        return u * UB, SPAN + (u - 1) * UB, u == 0
    if dil == 4:
        blk, r = u // 4, u % 4
        return blk * 4 * UB + r, SPAN + (blk - 1) * 4 * UB + r, blk == 0
    return u, u, True


def _unit_slices(gi, u):
    dil = DILATIONS[gi]
    qb0, kb0, first = _unit_bases(gi, u)
    if dil == 1:
        if not isinstance(qb0, int):
            qb0, kb0 = pl.multiple_of(qb0, UB), pl.multiple_of(kb0, UB)
        return pl.ds(qb0, UB), pl.ds(kb0, 2 * UB), first
    return pl.ds(qb0, UB, stride=dil), pl.ds(kb0, 2 * UB, stride=dil), first


def _bdot(a, b):
    return lax.dot_general(a, b, (((2,), (1,)), ((0,), (0,))), preferred_element_type=F32)


def _bdot_nt(a, b):
    return lax.dot_general(a, b, (((2,), (2,)), ((0,), (0,))), preferred_element_type=F32)


def _bdot_tn(a, b):
    return lax.dot_general(a, b, (((1,), (1,)), ((0,), (0,))), preferred_element_type=F32)


def _band_mask(first_in_span, has_prev):
    qi = lax.broadcasted_iota(jnp.int32, (UB, 2 * UB), 0)
    ki = lax.broadcasted_iota(jnp.int32, (UB, 2 * UB), 1)
    dist = UB + qi - ki
    band = (dist >= 0) & (dist <= UB)
    return band & ((ki >= UB) | jnp.logical_not(first_in_span) | has_prev)


def _gather_halves(phase, src_hbm, dst_hbm, ss, rs):
    half = src_hbm.shape[0] // 2
    cx, cy, cc = _coords()
    j = 2 * cx + cy
    sib = (cx, cy, 1 - cc)
    mine = pl.ds(cc * half, half)
    theirs = pl.ds((1 - cc) * half, half)
    chips = ((1, 0), (0, 1), (1, 1))
    for q, (kx, ky) in enumerate(chips):
        jq = j ^ (2 * kx + ky)
        out = _rcopy(src_hbm.at[mine], dst_hbm.at[j, mine], ss.at[q], rs.at[q], (_flip(cx, kx), _flip(cy, ky), cc))
        landed = _rcopy(src_hbm.at[mine], dst_hbm.at[jq, mine], ss.at[q], rs.at[q], sib)
        onward = _rcopy(dst_hbm.at[jq, mine], dst_hbm.at[jq, mine], ss.at[3 + q], rs.at[3 + q], sib)
        from_sib = _rcopy(src_hbm.at[mine], dst_hbm.at[jq, theirs], ss.at[3 + q], rs.at[3 + q], sib)
        if phase == 0:
            out.start()
        elif phase == 1:
            landed.wait_recv()
            onward.start()
        else:
            from_sib.wait_recv()
            out.wait_send()
            onward.wait_send()


def _attn_fwd(p, rc, rsa, rsb, w3sh):
    s = p.shape[1]
    ns = s // SPAN
    nunit = SPAN // UB

    def body(q_ref, k_ref, v_ref, z_ref, c_ref, sa_ref, sb_ref, w3_hbm, o_ref, lse_ref, ga_ref, qro_ref, kro_ref,
             w3g_hbm, qr, kf, vf, acc, mm, ll, ss, rs):
        hh, n = pl.program_id(0), pl.program_id(1)
        for phase, at_head, at_span in ((0, 0, 0), (1, H // 2, 0), (2, H - 1, ns - 1)):
            @pl.when((hh == at_head) & (n == at_span))
            def _(phase=phase):
                _gather_halves(phase, w3_hbm, w3g_hbm, ss, rs)

        @pl.when(n == 0)
        def _():
            kf[0:SPAN] = jnp.zeros((SPAN, DH), F32)
            vf[0:SPAN] = jnp.zeros((SPAN, DH), F32)

        c, sa, sb = c_ref[...], sa_ref[...], sb_ref[...]
        q_rot = _rope(q_ref[0].astype(F32), c, sa, sb).astype(BF16)
        k_rot = _rope(k_ref[0].astype(F32), c, sa, sb).astype(BF16)
        qro_ref[...] = q_rot
        kro_ref[...] = k_rot
        qr[...] = q_rot.astype(F32)
        kf[SPAN:] = k_rot.astype(F32)
        vf[SPAN:] = v_ref[0].astype(F32)
        has_prev = n > 0

        for gi, dil in enumerate(DILATIONS):
            def trip(t, carry, gi=gi, dil=dil):
                qsls, ksls, firsts = [], [], []
                for b in range(UNIT_BATCH_FWD):
                    qsl, ksl, first = _unit_slices(gi, t * UNIT_BATCH_FWD + b)
                    qsls.append(qsl)
                    ksls.append(ksl)
                    firsts.append(first)
                qb = jnp.stack([qr[qsl, :].astype(BF16) for qsl in qsls])
                kb = jnp.stack([kf[ksl, :].astype(BF16) for ksl in ksls])
                vb = jnp.stack([vf[ksl, :].astype(BF16) for ksl in ksls])
                s_all = _bdot_nt(qb, kb)
                prs = []
                for b in range(UNIT_BATCH_FWD):
                    sc = jnp.where(_band_mask(firsts[b], has_prev), s_all[b] * SCALE, NEG)
                    m = jnp.max(sc, axis=-1, keepdims=True)
                    pr = jnp.exp(sc - m)
                    l = jnp.sum(pr, axis=-1, keepdims=True)
                    mm[gi, qsls[b], :] = jnp.broadcast_to(m, (UB, DH))
                    ll[gi, qsls[b], :] = jnp.broadcast_to(l, (UB, DH))
                    prs.append(pr.astype(BF16))
                o_all = _bdot(jnp.stack(prs), vb)
                for b in range(UNIT_BATCH_FWD):
                    acc[gi, qsls[b], :] = o_all[b]
                return carry

            lax.fori_loop(0, nunit // UNIT_BATCH_FWD, trip, 0)

        m_all =jnp.maximum(jnp.maximum(mm[0], mm[1]), mm[2])
        num = jnp.zeros((SPAN, DH), F32)
        den = jnp.zeros((SPAN, DH), F32)
        for gi in range(3):
            w = jnp.exp(mm[gi] - m_all)
            num = num + w * acc[gi]
            den = den + w * ll[gi]
        o = num / den
        o_ref[...] = o
        lse_ref[...] = m_all + jnp.log(den)
        z = z_ref[0].astype(F32)
        ga_ref[...] = (o * (z * _sigmoid(z))).astype(BF16)
        kf[0:SPAN] = kf[SPAN:]
        vf[0:SPAN] = vf[SPAN:]

    blk = lambda piece, off: pl.BlockSpec((1, SPAN, DH), lambda hh, n: (piece, n, off + hh))
    tab = pl.BlockSpec((SPAN, DH), lambda hh, n: (n, 0))
    outb = pl.BlockSpec((SPAN, DH), lambda hh, n: (n, hh))
    any_ = pl.BlockSpec(memory_space=pl.ANY)
    return pl.pallas_call(
        body, name="attn_fwd", grid=(H, ns),
        in_specs=[blk(1, 0), blk(1, H), blk(2, 0), blk(2, H), tab, tab, tab, any_],
        out_specs=(outb, outb, outb, outb, outb, any_),
        out_shape=(jax.ShapeDtypeStruct((s, D), F32), jax.ShapeDtypeStruct((s, D), F32),
                   jax.ShapeDtypeStruct((s, D), BF16), jax.ShapeDtypeStruct((s, D), BF16),
                   jax.ShapeDtypeStruct((s, D), BF16), jax.ShapeDtypeStruct((4,) + w3sh.shape, w3sh.dtype)),
        scratch_shapes=[pltpu.VMEM((SPAN, DH), F32), pltpu.VMEM((2 * SPAN, DH), F32), pltpu.VMEM((2 * SPAN, DH), F32),
                        pltpu.VMEM((3, SPAN, DH), F32), pltpu.VMEM((3, SPAN, DH), F32), pltpu.VMEM((3, SPAN, DH), F32),
                        pltpu.SemaphoreType.DMA((6,)), pltpu.SemaphoreType.DMA((6,))],
        compiler_params=_params(("arbitrary", "arbitrary")),
    )(p, p, p, p, rc, rsa, rsb, w3sh)


def _attn_bwd(p, q_rot, k_rot, o, lse, dga, rc, rsa, rsb, packs):
    s = p.shape[1]
    ns = s // SPAN
    nunit = SPAN // UB
    npk = len(packs)

    def body(*refs):
        (q_ref, k_ref, kp_ref, v_ref, vp_ref, z_ref, c_ref, sa_ref, sb_ref, o_ref, lse_ref, dg_ref) = refs[:12]
        pk_refs = refs[12:12 + npk]
        dq_ref, dk_ref, dv_ref, dz_ref = refs[12 + npk:16 + npk]
        red_refs = refs[16 + npk:16 + 2 * npk]
        qr, kf, vf, dof, dlt, dqa, dkf, dvf = refs[16 + 2 * npk:24 + 2 * npk]
        rbufs = refs[24 + 2 * npk:24 + 3 * npk]
        sums = refs[24 + 3 * npk:24 + 4 * npk]
        ar_sems = refs[24 + 4 * npk:]
        hh, step = pl.program_id(0), pl.program_id(1)
        n = ns - 1 - step
        has_prev = n > 0
        for phase, at_head, at_step in ((0, 0, 0), (1, H // 2, 0), (2, H - 1, ns - 1)):
            @pl.when((hh == at_head) & (step == at_step))
            def _(phase=phase):
                _allreduce_phase(phase, pk_refs, sums, rbufs, *ar_sems, out_refs=red_refs)

        @pl.when(step == 0)
        def _():
            dkf[...] = jnp.zeros_like(dkf)
            dvf[...] = jnp.zeros_like(dvf)

        @pl.when(step > 0)
        def _():
            dkf[SPAN:] = dkf[0:SPAN]
            dvf[SPAN:] = dvf[0:SPAN]
            dkf[0:SPAN] = jnp.zeros((SPAN, DH), F32)
            dvf[0:SPAN] = jnp.zeros((SPAN, DH), F32)

        c, sa, sb = c_ref[...], sa_ref[...], sb_ref[...]
        qr[...] = q_ref[...].astype(F32)
        kf[SPAN:] = k_ref[...].astype(F32)
        vf[SPAN:] = v_ref[0].astype(F32)
        kf[0:SPAN] = jnp.where(has_prev, kp_ref[...].astype(F32), 0.0)
        vf[0:SPAN] = jnp.where(has_prev, vp_ref[0].astype(F32), 0.0)
        z = z_ref[0].astype(F32)
        sz = _sigmoid(z)
        dg = dg_ref[...]
        ov = o_ref[...]
        do = dg * (z * sz)
        dz_ref[...] = (dg * ov * (sz * (1.0 + z * (1.0 - sz)))).astype(BF16)
        dof[...] = do
        dlt[...] = jnp.dot(do * ov, jnp.ones((DH, DH), F32), preferred_element_type=F32, precision=HIGHEST)
        dqa[...] = jnp.zeros_like(dqa)

        for gi, dil in enumerate(DILATIONS):
            def trip(t, carry, gi=gi, dil=dil):
                qsls, ksls, firsts = [], [], []
                for b in range(UNIT_BATCH):
                    qsl, ksl, first = _unit_slices(gi, t * UNIT_BATCH + b)
                    qsls.append(qsl)
                    ksls.append(ksl)
                    firsts.append(first)
                qb = jnp.stack([qr[qsl, :].astype(BF16) for qsl in qsls])
                kb = jnp.stack([kf[ksl, :].astype(BF16) for ksl in ksls])
                vb = jnp.stack([vf[ksl, :].astype(BF16) for ksl in ksls])
                dob = jnp.stack([dof[qsl, :].astype(BF16) for qsl in qsls])
                s_all = _bdot_nt(qb, kb)
                dp_all = _bdot_nt(dob, vb)
                prs, dss = [], []
                for b in range(UNIT_BATCH):
                    lse_b = lse_ref[qsls[b], :]
                    dl_b = dlt[qsls[b], :]
                    pr = jnp.exp(s_all[b] * SCALE - jnp.concatenate([lse_b, lse_b], axis=1))
                    pr = jnp.where(_band_mask(firsts[b], has_prev), pr, 0.0)
                    prs.append(pr.astype(BF16))
                    dss.append((pr * (dp_all[b] - jnp.concatenate([dl_b, dl_b], axis=1)) * SCALE).astype(BF16))
                ds_all = jnp.stack(dss)
                dv_all = _bdot_tn(jnp.stack(prs), dob)
                dq_all = _bdot(ds_all, kb)
                dk_all = _bdot_tn(ds_all, qb)
                for b in range(UNIT_BATCH):
                    dvf[ksls[b], :] += dv_all[b]
                    dqa[qsls[b], :] += dq_all[b]
                    dkf[ksls[b], :] += dk_all[b]
                return carry

            lax.fori_loop(0, nunit // UNIT_BATCH, trip, 0)

        dq_ref[...] = _rope_bwd(dqa[...], c, sa, sb).astype(BF16)
        dk_ref[...] = _rope_bwd(dkf[SPAN:], c, sa, sb).astype(BF16)
        dv_ref[...] = dvf[SPAN:].astype(BF16)

    rn = lambda n: ns - 1 - n
    pn = lambda n: jnp.maximum(ns - 2 - n, 0)
    blk = lambda piece, off: pl.BlockSpec((1, SPAN, DH), lambda hh, n: (piece, rn(n), off + hh))
    blkp = lambda piece, off: pl.BlockSpec((1, SPAN, DH), lambda hh, n: (piece, pn(n), off + hh))
    tab = pl.BlockSpec((SPAN, DH), lambda hh, n: (rn(n), 0))
    io = pl.BlockSpec((SPAN, DH), lambda hh, n: (rn(n), hh))
    iop = pl.BlockSpec((SPAN, DH), lambda hh, n: (pn(n), hh))
    vm = pl.BlockSpec(memory_space=pltpu.VMEM)
    outs = pl.pallas_call(
        body, name="attn_bwd", grid=(H, ns),
        in_specs=[io, io, iop, blk(2, 0), blkp(2, 0), blk(2, H), tab, tab, tab, io, io, io] + [vm] * npk,
        out_specs=(io, io, io, io) + (vm,) * npk,
        out_shape=tuple(jax.ShapeDtypeStruct((s, D), BF16) for _ in range(4)) +
                  tuple(jax.ShapeDtypeStruct(a.shape, F32) for a in packs),
        scratch_shapes=[pltpu.VMEM((SPAN, DH), F32), pltpu.VMEM((2 * SPAN, DH), F32), pltpu.VMEM((2 * SPAN, DH), F32),
                        pltpu.VMEM((SPAN, DH), F32), pltpu.VMEM((SPAN, DH), F32), pltpu.VMEM((SPAN, DH), F32),
                        pltpu.VMEM((2 * SPAN, DH), F32), pltpu.VMEM((2 * SPAN, DH), F32)] +
                       [pltpu.VMEM((NDEV, a.shape[0] // NDEV, a.shape[1]), F32) for a in packs] +
                       [pltpu.VMEM(a.shape, F32) for a in packs] +
                       [pltpu.SemaphoreType.DMA((7 * npk,)) for _ in range(4)],
        compiler_params=_params(("arbitrary", "arbitrary")),
    )(q_rot, k_rot, k_rot, p, p, p, rc, rsa, rsb, o, lse, dga, *packs)
    return outs[:4], outs[4:]


def _tail(gr, ga, p, x, tgt, w3, b_gate, gate, g_final, tm=256):
    s = x.shape[0]
    nt = s // tm

    def body(gr_ref, ga_ref, pr_ref, pa_ref, x_ref, t_ref, bg_ref, gate_ref, gf_ref, w_hbm,
             dgr_ref, dga_ref, dc_ref, dx2_ref, vec_ref, go_hbm, w_s, acc_s, sem):
        i = pl.program_id(0)

        @pl.when(i == 0)
        def _():
            cp = pltpu.make_async_copy(w_hbm, w_s, sem.at[12])
            cp.start()
            acc_s[...] = jnp.zeros_like(acc_s)
            vec_ref[...] = jnp.zeros_like(vec_ref)
            cp.wait()

        grb = gr_ref[...]
        gab = ga_ref[...]
        bg = bg_ref[...]
        gate_v = gate_ref[...]
        gf = gf_ref[...]
        y_r = _dot(grb, w_s[0])
        y_a = _dot(gab, w_s[1])
        sr = _sigmoid(pr_ref[0].astype(F32) + bg[:, :D])
        sa = _sigmoid(pa_ref[0].astype(F32) + bg[:, D:])
        mb = (sr * y_r + sa * y_a).astype(BF16)
        u = _dot(mb, w_s[2])
        x2 = x_ref[...] + gate_v * u
        rstd = lax.rsqrt(jnp.mean(x2 * x2, axis=-1, keepdims=True) + EPS)
        xh = x2 * rstd
        e = xh * gf - t_ref[...]
        dy = e * (1.0 / D)
        dyg = dy * gf
        dx2 = rstd * (dyg - xh * jnp.mean(dyg * xh, axis=-1, keepdims=True))
        dx2_ref[...] = dx2
        dub = (dx2 * gate_v).astype(BF16)
        dm = _dot_nt(dub, w_s[2])
        dyr = (dm * sr).astype(BF16)
        dya = (dm * sa).astype(BF16)
        dpr = dm * y_r * (sr * (1.0 - sr))
        dpa = dm * y_a * (sa * (1.0 - sa))
        dc_ref[:, :D] = dpr.astype(BF16)
        dc_ref[:, D:] = dpa.astype(BF16)
        dgr_ref[...] = _dot_nt(dyr, w_s[0])
        dga_ref[...] = _dot_nt(dya, w_s[1])
        acc_s[0] += _dot_tn(grb, dyr)
        acc_s[1] += _dot_tn(gab, dya)
        acc_s[2] += _dot_tn(mb, dub)
        vec_ref[0:1, :] += _colsum(dy * xh)
        vec_ref[1:2, :] += _colsum(dx2 * u)
        vec_ref[2:3, :] += _colsum(dpr)
        vec_ref[3:4, :] += _colsum(dpa)
        vec_ref[4:5, :] += _colsum(e * e)

        @pl.when(i == nt - 1)
        def _():
            vec_ref[4:5, :] = jnp.broadcast_to(jnp.sum(vec_ref[4:5, :]) * (0.5 / D), (1, D))
            cps = []
            for w in range(3):
                for j in range(4):
                    cps.append(pltpu.make_async_copy(acc_s.at[w, pl.ds(256 * j, 256)],
                                                     go_hbm.at[j, pl.ds(256 * w, 256)], sem.at[4 * w + j]))
            for cp in cps:
                cp.start()
            for cp in cps:
                cp.wait()

    rowt = lambda i: (i, 0)
    row = lambda w: pl.BlockSpec((1, w), lambda i: (0, 0))
    any_ = pl.BlockSpec(memory_space=pl.ANY)
    return pl.pallas_call(
        body, name="tail", grid=(nt,),
        in_specs=[pl.BlockSpec((tm, D), rowt), pl.BlockSpec((tm, D), rowt),
                  pl.BlockSpec((1, tm, D), lambda i: (3, i, 0)), pl.BlockSpec((1, tm, D), lambda i: (3, i, 1)),
                  pl.BlockSpec((tm, D), rowt), pl.BlockSpec((tm, D), rowt),
                  row(2 * D), row(D), row(D), any_],
        out_specs=(pl.BlockSpec((tm, D), rowt), pl.BlockSpec((tm, D), rowt), pl.BlockSpec((tm, 2 * D), rowt),
                   pl.BlockSpec((tm, D), rowt), pl.BlockSpec((8, D), lambda i: (0, 0)), any_),
        out_shape=(jax.ShapeDtypeStruct((s, D), F32), jax.ShapeDtypeStruct((s, D), F32),
                   jax.ShapeDtypeStruct((s, 2 * D), BF16), jax.ShapeDtypeStruct((s, D), F32),
                   jax.ShapeDtypeStruct((8, D), F32), jax.ShapeDtypeStruct((4, 768, D), F32)),
        scratch_shapes=[pltpu.VMEM((3, D, D), BF16), pltpu.VMEM((3, D, D), F32), pltpu.SemaphoreType.DMA((13,))],
        compiler_params=_params(("arbitrary",)),
    )(gr, ga, p, p, x, tgt, b_gate, gate, g_final, w3)


def _pieces_steps(pieces):
    out, s0 = [], 0
    for a in pieces:
        n = a.shape[1] // D
        out.append((s0, n))
        s0 += n
    return out, s0


def _inproj_bwd_x(pieces, wg, wsh, x, dx2, gn, scale, sums, tm=512):
    s = x.shape[0]
    np_ = len(pieces)
    na = len(sums)
    ni = s // tm
    groups, cur, width = [], [], 0
    for t, a in enumerate(pieces):
        cur.append(t)
        width += a.shape[1]
        if width == PW:
            groups.append(cur)
            cur, width = [], 0
    assert len(groups) == 4 and not cur

    def body(*refs):
        d_refs = refs[:np_]
        w_hbm, wsh_hbm, x_ref, dx2_ref, gn_ref, sc_ref = refs[np_:np_ + 6]
        q_refs = refs[np_ + 6:np_ + 6 + na]
        gx_ref, vec_ref = refs[np_ + 6 + na:np_ + 8 + na]
        r_refs = refs[np_ + 8 + na:np_ + 8 + 2 * na]
        w_s, wsem, ss, rs = refs[np_ + 8 + 2 * na:]
        i = pl.program_id(0)
        my_shard = 2 * lax.axis_index("x") + lax.axis_index("y")

        def scatter_copies():
            cx, cy, cc = _coords()
            j = 2 * cx + cy
            cps = []
            for t, (q, r) in enumerate(zip(q_refs, r_refs)):
                for e, (kx, ky) in enumerate(((1, 0), (0, 1), (1, 1))):
                    cps.append(_rcopy(q.at[j ^ (2 * kx + ky)], r.at[e], ss.at[3 * t + e], rs.at[3 * t + e],
                                      (_flip(cx, kx), _flip(cy, ky), cc)))
            return cps

        def w_copy(pc):
            return pltpu.make_async_copy(w_hbm.at[pc, pl.ds(0, D), :], w_s.at[pc], wsem.at[pc])

        for pc in range(4):
            @pl.when((i == 0) & (my_shard != pc))
            def _(pc=pc):
                w_copy(pc).start()

            @pl.when((i == 0) & (my_shard == pc))
            def _(pc=pc):
                pltpu.make_async_copy(wsh_hbm, w_s.at[pc], wsem.at[pc]).start()

        @pl.when(i == 0)
        def _():
            vec_ref[...] = jnp.zeros_like(vec_ref)
            for cp in scatter_copies():
                cp.start()

        dh = None
        for pc, group in enumerate(groups):
            @pl.when(i == 0)
            def _(pc=pc):
                w_copy(pc).wait()

            tiles = [d_refs[t][...] for t in group]
            lhs = tiles[0] if len(tiles) == 1 else jnp.concatenate(tiles, axis=1)
            part = _dot_nt(lhs, w_s[pc])
            dh = part if dh is None else dh + part

        xt = x_ref[...]
        rstd = lax.rsqrt(jnp.mean(xt * xt, axis=-1, keepdims=True) + EPS)
        xh = xt * rstd
        gn_v = gn_ref[...]
        sc1 = 1.0 + sc_ref[...]
        dhx = dh * xh
        vec_ref[0:1, :] += _colsum(dh)
        vec_ref[1:2, :] += _colsum(dhx) * gn_v
        vec_ref[2:3, :] += _colsum(dhx) * sc1
        dxh = dh * (gn_v * sc1)
        gx_ref[...] = rstd * (dxh - xh * jnp.mean(dxh * xh, axis=-1, keepdims=True)) + dx2_ref[...]

        @pl.when(i == ni - 1)
        def _():
            for cp in scatter_copies():
                cp.wait()

    rowt = lambda i: (i, 0)
    row = pl.BlockSpec((1, D), lambda i: (0, 0))
    any_ = pl.BlockSpec(memory_space=pl.ANY)
    outs = pl.pallas_call(
        body, name="inproj_bwd_x", grid=(ni,),
        in_specs=[pl.BlockSpec((tm, a.shape[1]), rowt) for a in pieces] +
                 [any_, any_, pl.BlockSpec((tm, D), rowt), pl.BlockSpec((tm, D), rowt), row, row] + [any_] * na,
        out_specs=(pl.BlockSpec((tm, D), rowt), pl.BlockSpec((8, D), lambda i: (0, 0))) + (any_,) * na,
        out_shape=(jax.ShapeDtypeStruct((s, D), F32), jax.ShapeDtypeStruct((8, D), F32)) +
                  tuple(jax.ShapeDtypeStruct((3,) + q.shape[1:], q.dtype) for q in sums),
        scratch_shapes=[pltpu.VMEM((4, D, PW), BF16), pltpu.SemaphoreType.DMA((4,)),
                        pltpu.SemaphoreType.DMA((3 * na,)), pltpu.SemaphoreType.DMA((3 * na,))],
        compiler_params=_params(("arbitrary",)),
    )(*pieces, wg, wsh, x, dx2, gn, scale, *sums)
    return outs[0], outs[1], outs[2:]


def _inproj_bwd_w(pieces, hbf, g_out, tk=2048):
    s = hbf.shape[0]
    steps, nk = _pieces_steps(pieces)
    npc = PW // D
    ns = s // tk
    np_ = len(pieces)
    hr = D // 2
    ohr = g_out.shape[1] // 2
    ocr = _chunk_rows(g_out)
    ochunks = [(j, r0) for j in range(g_out.shape[0]) for r0 in range(0, ohr, ocr)]
    noc = len(ochunks)

    def body(*refs):
        d_refs = refs[:np_]
        h_ref, go_hbm, g_ref, rb_hbm, rbo_hbm, stage, dbuf, dsem, ss, rs, oss, ors = refs[np_:]
        cb, k = pl.program_id(0), pl.program_id(1)
        cx, cy, cc = _coords()
        sib = (cx, cy, 1 - cc)

        def block_copy(b):
            return _rcopy(stage.at[b % 2],
                          rb_hbm.at[b // npc, :, pl.ds(pl.multiple_of((b % npc) * D, D), D)], ss.at[b], rs.at[b], sib)

        def out_copy(e):
            j, r0 = ochunks[e]
            return _rcopy(go_hbm.at[j, pl.ds((1 - cc) * ohr + r0, ocr), :], rbo_hbm.at[j, pl.ds(r0, ocr), :],
                          oss.at[e], ors.at[e], sib)

        @pl.when((cb == 0) & (k == 0))
        def _():
            for e in range(noc):
                out_copy(e).start()

        @pl.when(k == 0)
        def _():
            g_ref[...] = jnp.zeros_like(g_ref)

        t = cb * ns + k

        def fetch(cb_, k_, slot):
            for (s0, n), d_hbm in zip(steps, d_refs):
                @pl.when((cb_ >= s0) & (cb_ < s0 + n))
                def _(d_hbm=d_hbm, s0=s0):
                    src = d_hbm.at[pl.ds(pl.multiple_of(k_ * tk, tk), tk), pl.ds(pl.multiple_of((cb_ - s0) * D, D), D)]
                    pltpu.make_async_copy(src, dbuf.at[slot], dsem.at[slot]).start()

        @pl.when(t == 0)
        def _():
            fetch(cb, k, 0)

        @pl.when(t + 1 < nk * ns)
        def _():
            fetch((t + 1) // ns, (t + 1) % ns, (t + 1) % 2)

        pltpu.make_async_copy(d_refs[0].at[pl.ds(0, tk), pl.ds(0, D)], dbuf.at[t % 2], dsem.at[t % 2]).wait()
        g_ref[0] += _dot_tn(h_ref[...], dbuf[t % 2])

        @pl.when((k == ns - 1) & (cb > 1))
        def _():
            block_copy(cb - 2).wait_send()

        @pl.when(k == ns - 1)
        def _():
            stage[cb % 2] = g_ref[0, pl.ds(pl.multiple_of((1 - cc) * hr, hr), hr), :]
            block_copy(cb).start()

        @pl.when((k == ns - 1) & (cb == nk - 1))
        def _():
            block_copy(nk - 2).wait_send()
            block_copy(nk - 1).wait_send()
            for b in range(nk):
                block_copy(b).wait_recv()
            for e in range(noc):
                out_copy(e).wait_recv()
                out_copy(e).wait_send()

    any_ = pl.BlockSpec(memory_space=pl.ANY)
    return pl.pallas_call(
        body, name="inproj_bwd_w", grid=(nk, ns),
        in_specs=[any_] * np_ + [pl.BlockSpec((tk, D), lambda cb, k: (k, 0)), any_],
        out_specs=(pl.BlockSpec((1, D, D), lambda cb, k: (cb // npc, 0, cb % npc)), any_, any_),
        out_shape=(jax.ShapeDtypeStruct((4, D, PW), F32), jax.ShapeDtypeStruct((4, hr, PW), F32),
                   jax.ShapeDtypeStruct((g_out.shape[0], ohr, g_out.shape[2]), F32)),
        scratch_shapes=[pltpu.VMEM((2, hr, D), F32), pltpu.VMEM((2, tk, D), BF16), pltpu.SemaphoreType.DMA((2,)),
                        pltpu.SemaphoreType.DMA((nk,)), pltpu.SemaphoreType.DMA((nk,)),
                        pltpu.SemaphoreType.DMA((noc,)), pltpu.SemaphoreType.DMA((noc,))],
        compiler_params=_params(("arbitrary", "arbitrary")),
    )(*pieces, hbf, g_out)


D2D_CHUNK_BYTES = 512 * 1024


def _chunk_rows(a):
    return max(8, D2D_CHUNK_BYTES // (a.shape[-1] * a.dtype.itemsize))


def _pair_swap(arrs):
    na = len(arrs)
    chunks = []
    for t, a in enumerate(arrs):
        cr = _chunk_rows(a)
        chunks += [(t, r0, cr) for r0 in range(0, a.shape[0], cr)]
    nch = len(chunks)

    def body(*refs):
        a_refs = refs[:na]
        o_refs = refs[na:2 * na]
        ss, rs = refs[2 * na:]
        x, y, c = _coords()
        sib = (x, y, 1 - c)
        rcs = []
        for n, (t, r0, cr) in enumerate(chunks):
            rows = pl.ds(r0, cr)
            rc = _rcopy(a_refs[t].at[rows, :], o_refs[t].at[rows, :], ss.at[n], rs.at[n], sib)
            rc.start()
            rcs.append(rc)
        for rc in rcs:
            rc.wait_recv()
        for rc in rcs:
            rc.wait_send()

    any_ = pl.BlockSpec(memory_space=pl.ANY)
    return pl.pallas_call(
        body, name="pair_swap",
        out_shape=tuple(jax.ShapeDtypeStruct(a.shape, a.dtype) for a in arrs),
        in_specs=[any_] * na, out_specs=tuple([any_] * na),
        scratch_shapes=[pltpu.SemaphoreType.DMA((nch,)), pltpu.SemaphoreType.DMA((nch,))],
        compiler_params=_params(),
    )(*arrs)


def _add_half(full, rb, core, tr):
    n, r, cdim = full.shape
    nb = r // 2 // tr

    def body(c_ref, a_ref, b_ref, ob_ref):
        ob_ref[...] = (a_ref[...] + b_ref[...]).astype(BF16)

    mine = pl.BlockSpec((1, tr, cdim), lambda i, j, c_ref: (i, c_ref[0] * nb + j, 0))
    spec = pl.BlockSpec((1, tr, cdim), lambda i, j, c_ref: (i, j, 0))
    return pl.pallas_call(
        body, name="add_half",
        grid_spec=pltpu.PrefetchScalarGridSpec(num_scalar_prefetch=1, grid=(n, nb), in_specs=[mine, spec],
                                               out_specs=spec),
        out_shape=jax.ShapeDtypeStruct(rb.shape, BF16),
        compiler_params=_params(("parallel", "parallel")),
    )(core, full, rb)


def _sum_slots(full, rb, r3, shard_core, tr):
    _, hr, cdim = rb.shape
    nb = hr // tr

    def body(jc_ref, a_ref, b_ref, r_ref, o_ref):
        own = a_ref[0] + b_ref[0]
        o_ref[...] = ((own + r_ref[0].astype(F32)) + r_ref[1].astype(F32)) + r_ref[2].astype(F32)

    return pl.pallas_call(
        body, name="sum_slots",
        grid_spec=pltpu.PrefetchScalarGridSpec(
            num_scalar_prefetch=1, grid=(nb,),
            in_specs=[pl.BlockSpec((1, tr, cdim), lambda i, jc: (jc[0], jc[1] * nb + i, 0)),
                      pl.BlockSpec((1, tr, cdim), lambda i, jc: (jc[0], i, 0)),
                      pl.BlockSpec((3, tr, cdim), lambda i, jc: (0, i, 0))],
            out_specs=pl.BlockSpec((tr, cdim), lambda i, jc: (i, 0))),
        out_shape=jax.ShapeDtypeStruct((hr, cdim), F32),
        compiler_params=_params(("parallel",)),
    )(shard_core, full, rb, r3)


def _allreduce_phase(phase, p_refs, o_refs, rbufs, s1, r1, s2, r2, out_refs=None):
    me = _my_index()

    def chunk(t, d):
        ch = p_refs[t].shape[0] // NDEV
        return pl.ds(pl.multiple_of(d * ch, 8), ch)

    def scatter(t, k):
        e = 7 * t + k - 1
        return _rcopy(p_refs[t].at[chunk(t, me ^ k)], rbufs[t].at[me], s1.at[e], r1.at[e], _peer(k))

    def gather(t, k):
        e = 7 * t + k - 1
        return _rcopy(o_refs[t].at[chunk(t, me)], o_refs[t].at[chunk(t, me)], s2.at[e], r2.at[e], _peer(k))

    for t in range(len(p_refs)):
        if phase == 0:
            for k in range(1, NDEV):
                scatter(t, k).start()
            rbufs[t][me] = p_refs[t][chunk(t, me), :]
        elif phase == 1:
            for k in range(1, NDEV):
                e = 7 * t + k - 1
                _rcopy(p_refs[t].at[chunk(t, me)], rbufs[t].at[me ^ k], s1.at[e], r1.at[e], _peer(k)).wait_recv()
            tot = rbufs[t][0]
            for d in range(1, NDEV):
                tot = tot + rbufs[t][d]
            o_refs[t][chunk(t, me), :] = tot
            for k in range(1, NDEV):
                gather(t, k).start()
        else:
            for k in range(1, NDEV):
                e = 7 * t + k - 1
                _rcopy(o_refs[t].at[chunk(t, me)], o_refs[t].at[chunk(t, me ^ k)], s2.at[e], r2.at[e],
                       _peer(k)).wait_recv()
            for k in range(1, NDEV):
                scatter(t, k).wait_send()
                gather(t, k).wait_send()
            if out_refs is not None:
                out_refs[t][...] = o_refs[t][...]


def _adamw_update(w, g, m, v):
    nm = B1 * m + (1.0 - B1) * g
    nv = B2 * v + (1.0 - B2) * (g * g)
    m_hat = nm / (1.0 - B1 ** STEP)
    v_hat = nv / (1.0 - B2 ** STEP)
    return -LR * (m_hat / (jnp.sqrt(v_hat) + ADAM_EPS) + WD * w), nm, nv


def _adamw(w, g, m, v, tr):
    r, cdim = w.shape

    def body(w_ref, g_ref, m_ref, v_ref, d_ref, nm_ref, nv_ref):
        d_ref[...], nm_ref[...], nv_ref[...] = _adamw_update(w_ref[...], g_ref[...], m_ref[...], v_ref[...])

    spec = pl.BlockSpec((tr, cdim), lambda i: (i, 0))
    sd = jax.ShapeDtypeStruct((r, cdim), F32)
    return pl.pallas_call(
        body, name="adamw", grid=(r // tr,), in_specs=[spec] * 4, out_specs=(spec,) * 3, out_shape=(sd,) * 3,
        compiler_params=_params(("parallel",)),
    )(w, g, m, v)


def _adamw_halves(w, mine, theirs, m, v, core, tr):
    r, cdim = w.shape
    nbh = r // 2 // tr

    def body(c_ref, w_ref, a_ref, b_ref, m_ref, v_ref, g_ref, d_ref, nm_ref, nv_ref):
        is_mine = pl.program_id(0) // nbh == c_ref[0]
        g = jnp.where(is_mine, a_ref[...], b_ref[...])
        g_ref[...] = g
        d_ref[...], nm_ref[...], nv_ref[...] = _adamw_update(w_ref[...], g, m_ref[...], v_ref[...])

    spec = pl.BlockSpec((tr, cdim), lambda i, c: (i, 0))
    half = lambda own: pl.BlockSpec(
        (tr, cdim), lambda i, c: (jnp.clip(i - (c[0] if own else 1 - c[0]) * nbh, 0, nbh - 1), 0))
    sd = jax.ShapeDtypeStruct((r, cdim), F32)
    return pl.pallas_call(
        body, name="adamw_halves",
        grid_spec=pltpu.PrefetchScalarGridSpec(num_scalar_prefetch=1, grid=(r // tr,),
                                               in_specs=[spec, half(True), half(False), spec, spec],
                                               out_specs=(spec,) * 4),
        out_shape=(sd,) * 4,
        compiler_params=_params(("parallel",)),
    )(core, w, mine, theirs, m, v)


V_B_GATE, V_CONV_B, V_LAM, V_G_FINAL, V_CONV_W, V_LOSS, V_ROWS = 0, 2, 3, 4, 5, 9, 64
M_W_A, M_W_X, M_B_A, M_B_X, M_ROWS = 0, H * DH, 2 * H * DH, 2 * H * DH + H, 2112
SMALL = ("g_norm", "b_mod", "b_gate", "conv_b", "lam", "g_final", "conv_w", "w_a", "w_x", "b_a", "b_x")


def _adamw_small(redv, redm, g_conv, g_gnorm, g_bmod, wmv):
    def grad(name, rv, rm, gc, gg, gb):
        if name == "g_norm":
            return gg[...]
        if name == "b_mod":
            return gb[...]
        if name == "b_gate":
            return jnp.concatenate([rv[V_B_GATE + t:V_B_GATE + t + 1, :] for t in range(2)], axis=1)
        if name == "conv_b":
            return rv[V_CONV_B:V_CONV_B + 1, :]
        if name == "lam":
            return rv[V_LAM:V_LAM + 1, :]
        if name == "g_final":
            return rv[V_G_FINAL:V_G_FINAL + 1, :]
        if name == "conv_w":
            return gc[...]
        if name == "w_a":
            return rm[M_W_A:M_W_A + H * DH, :]
        if name == "w_x":
            return rm[M_W_X:M_W_X + H * DH, :]
        if name == "b_a":
            return rm[M_B_A:M_B_A + H, :]
        return rm[M_B_X:M_B_X + H, :]

    n = len(SMALL)

    def body(*refs):
        rv, rm, gc, gg, gb = refs[:5]
        ins, outs = refs[5:5 + 3 * n], refs[5 + 3 * n:]
        for t, name in enumerate(SMALL):
            w_ref, m_ref, v_ref = ins[3 * t:3 * t + 3]
            g_out, d_out, m_out, v_out = outs[4 * t:4 * t + 4]
            g = grad(name, rv, rm, gc, gg, gb)
            g_out[...] = g
            d_out[...], m_out[...], v_out[...] = _adamw_update(w_ref[...], g, m_ref[...], v_ref[...])

    vm = pl.BlockSpec(memory_space=pltpu.VMEM)
    flat = [a for name in SMALL for a in wmv[name]]
    shapes = [jax.ShapeDtypeStruct(wmv[name][0].shape, F32) for name in SMALL for _ in range(4)]
    outs = pl.pallas_call(
        body, name="adamw_small", out_shape=tuple(shapes),
        in_specs=[vm] * (5 + len(flat)), out_specs=tuple([vm] * len(shapes)),
        compiler_params=_params(),
    )(redv, redm, g_conv, g_gnorm, g_bmod, *flat)
    return {name: outs[4 * t:4 * t + 4] for t, name in enumerate(SMALL)}


def _rope_tables(positions):
    inv_freq = ROPE_THETA ** (-jnp.arange(0, ROT, 2, dtype=F32) / ROT)
    ang = positions.astype(F32)[:, None] * inv_freq
    cos, sin = lax.optimization_barrier((jnp.cos(ang), jnp.sin(ang)))
    n = positions.shape[0]
    half = ROT // 2
    rc = jnp.concatenate([cos, cos, jnp.ones((n, DH - ROT), F32)], axis=1)
    rsa = jnp.concatenate([-sin, jnp.zeros((n, DH - half), F32)], axis=1)
    rsb = jnp.concatenate([jnp.zeros((n, half), F32), sin, jnp.zeros((n, DH - ROT), F32)], axis=1)
    return rc, rsa, rsb


def kernel(x, c, positions, g_norm, w_mod, b_mod, w_in, b_gate, conv_w, conv_b, w_a, b_a, w_x, b_x, lam, w_out_rnn, w_out_attn, w_o, g_final, loss_target, m_g_norm, m_w_mod, m_b_mod, m_w_in, m_b_gate, m_conv_w, m_conv_b, m_w_a, m_b_a, m_w_x, m_b_x, m_lam, m_w_out_rnn, m_w_out_attn, m_w_o, m_g_final, v_g_norm, v_w_mod, v_b_mod, v_w_in, v_b_gate, v_conv_w, v_conv_b, v_w_a, v_b_a, v_w_x, v_b_x, v_lam, v_w_out_rnn, v_w_out_attn, v_w_o, v_g_final):
    s = x.shape[1]
    xi = lax.axis_index("x")
    yi = lax.axis_index("y")
    ci = lax.axis_index("c")
    shard = 2 * xi + yi
    x2d = x[0]
    tgt = loss_target[0]
    pos = positions[0]

    c_all, mod4, conv_all = _mod_fwd(c, w_mod[0], b_mod.reshape(4, 1, 768), conv_w[0])
    mod = mod4.reshape(1, 3 * D)
    shift, scale, gate = mod[:, :D], mod[:, D:2 * D], mod[:, 2 * D:]
    w3sh = jnp.concatenate([w_out_rnn[0], w_out_attn[0], w_o[0]], axis=0).astype(BF16)
    wsh = w_in[0].astype(BF16)
    conv_full = conv_all[0::2].transpose(1, 0, 2).reshape(4, D)

    order = jnp.stack([shard, shard ^ 2, shard ^ 1, shard ^ 3]).astype(jnp.int32)
    p, hbf, wg = _gather_norm_inproj(x2d, g_norm, shift, scale, wsh, order)
    rc, rsa, rsb = _rope_tables(pos)
    pos_col = jnp.broadcast_to((pos == 0).astype(F32)[:, None], (s, DH))
    b_a3, b_x3 = b_a.reshape(H, 1, DH), b_x.reshape(H, 1, DH)
    hr, gr = _rnn_fwd(p, pos_col, conv_full, conv_b, w_a[0], b_a3, w_x[0], b_x3, lam)
    o, lse, ga, q_rot, k_rot, w3g = _attn_fwd(p, rc, rsa, rsb, w3sh)
    w3g = lax.dynamic_update_slice(w3g, w3sh[None], (shard, 0, 0))
    w3 = w3g.reshape(4, 3, 256, D).transpose(1, 0, 2, 3).reshape(3, D, D)

    dgr, dga, dc, dx2, vec_t, g_out = _tail(gr, ga, p, x2d, tgt, w3, b_gate, gate, g_final.reshape(1, D))

    dxr, dzr, g_wa, g_ba, g_wx, g_bx, g_lam, g_cw, g_cb = _rnn_bwd(
        p, hr, dgr, pos_col, conv_full, conv_b, w_a[0], b_a3, w_x[0], b_x3, lam)
    vpack = jnp.concatenate([
        vec_t[2:4],
        g_cb.reshape(1, D),
        g_lam.reshape(1, D),
        vec_t[0:1],
        g_cw.transpose(1, 0, 2).reshape(4, D),
        vec_t[4:5],
        jnp.zeros((V_ROWS - 10, D), F32)], axis=0)
    mpack = jnp.concatenate([
        g_wa.reshape(H * DH, DH), g_wx.reshape(H * DH, DH), g_ba.reshape(H, DH), g_bx.reshape(H, DH),
        jnp.zeros((M_ROWS - 2 * H * DH - 2 * H, DH), F32)], axis=0)
    (dq, dk, dv, dza), (redv, redm) = _attn_bwd(p, q_rot, k_rot, o, lse, dga, rc, rsa, rsb, [vpack, mpack])

    pieces = [dxr, dzr, dq, dk, dv, dza, dc]
    g_win, rb_a, rb_b = _inproj_bwd_w(pieces, hbf, g_out)

    core = ci.reshape(1)
    shard_core = jnp.stack([shard, ci]).astype(jnp.int32)
    qh_a, qh_b = _add_half(g_win, rb_a, core, tr=256), _add_half(g_out, rb_b, core, tr=384)
    grad_x, vec_n, (r_a, r_b) = _inproj_bwd_x(pieces, wg, wsh, x2d, dx2, g_norm, scale, [qh_a, qh_b])
    f_a = _sum_slots(g_win, rb_a, r_a, shard_core, tr=256)
    f_b = _sum_slots(g_out, rb_b, r_b, shard_core, tr=384)
    s_a, s_b = _pair_swap([f_a, f_b])

    dmod_row = jnp.concatenate([vec_n[0:1], vec_n[1:2], vec_t[1:2]], axis=1)
    loss = redv[V_LOSS, 0]
    grad_w_mod, g_bmod4, g_gnorm = _mod_bwd(dmod_row.reshape(4, 1, 768), vec_n[2:3], c_all)
    g_conv_sh = lax.dynamic_slice_in_dim(redv[V_CONV_W:V_CONV_W + 4], shard * 256, 256, axis=1)

    shape2d = dict(g_norm=(1, D), b_mod=(1, 3 * D), b_gate=(1, 2 * D), conv_b=(1, D), lam=(1, D), g_final=(1, D),
                   conv_w=(4, 256), w_a=(H * DH, DH), w_x=(H * DH, DH), b_a=(H, DH), b_x=(H, DH))
    given = dict(
        g_norm=(g_norm, m_g_norm, v_g_norm), b_mod=(b_mod, m_b_mod, v_b_mod), b_gate=(b_gate, m_b_gate, v_b_gate),
        conv_b=(conv_b, m_conv_b, v_conv_b), lam=(lam, m_lam, v_lam), g_final=(g_final, m_g_final, v_g_final),
        conv_w=(conv_w, m_conv_w, v_conv_w), w_a=(w_a, m_w_a, v_w_a), w_x=(w_x, m_w_x, v_w_x),
        b_a=(b_a, m_b_a, v_b_a), b_x=(b_x, m_b_x, v_b_x))
    small = _adamw_small(redv, redm, g_conv_sh, g_gnorm, g_bmod4.reshape(1, 3 * D),
                         {n: tuple(a.reshape(shape2d[n]) for a in given[n]) for n in SMALL})

    big_in = _adamw_halves(w_in[0], f_a, s_a, m_w_in[0], v_w_in[0], core, tr=256)
    big_mod = (grad_w_mod,) + tuple(_adamw(w_mod[0], grad_w_mod, m_w_mod[0], v_w_mod[0], tr=256))
    w3f = jnp.concatenate([w_out_rnn[0], w_out_attn[0], w_o[0]], axis=0)
    m3f = jnp.concatenate([m_w_out_rnn[0], m_w_out_attn[0], m_w_o[0]], axis=0)
    v3f = jnp.concatenate([v_w_out_rnn[0], v_w_out_attn[0], v_w_o[0]], axis=0)
    big_out = _adamw_halves(w3f, f_b, s_b, m3f, v3f, core, tr=384)

    names = ["g_norm", "w_mod", "b_mod", "w_in", "b_gate", "conv_w", "conv_b", "w_a", "b_a", "w_x", "b_x", "lam",
             "w_out_rnn", "w_out_attn", "w_o", "g_final"]
    outs = []
    for idx in range(4):
        d = {n: small[n][idx].reshape(given[n][0].shape) for n in SMALL}
        d.update(w_mod=big_mod[idx][None], w_in=big_in[idx][None],
                 w_out_rnn=big_out[idx][0:256][None], w_out_attn=big_out[idx][256:512][None],
                 w_o=big_out[idx][512:768][None])
        outs.append(d)
    flat = [d[n] for d in outs for n in names]
    return (loss, grad_x[None], *flat)
```

```python
import jax
import jax.numpy as jnp
from jax import lax
from jax.experimental import pallas as pl
from jax.experimental.pallas import tpu as pltpu

F32, BF16 = jnp.float32, jnp.bfloat16
MESH = pl.DeviceIdType.MESH
HIGHEST = lax.Precision.HIGHEST

D = 1024
H = 8
DH = 128
PW = 2048
EPS = 1e-6
LRU_C = 8.0
SCALE = DH ** -0.5
NEG = -1e30
SPAN = 2048
UB = 128
DILATIONS = (1, 4, 16)
UNIT_BATCH = 16
UNIT_BATCH_FWD = 8
ROPE_THETA = 500000.0
ROT = 32

LR, B1, B2, ADAM_EPS, WD, STEP = 0.001, 0.9, 0.999, 1e-08, 0.01, 10

NDEV = 8


def _params(sem=None, vmem_mb=56):
    return pltpu.CompilerParams(dimension_semantics=sem, vmem_limit_bytes=vmem_mb * 2 ** 20)


def _coords():
    return lax.axis_index("x"), lax.axis_index("y"), lax.axis_index("c")


def _flip(v, bit):
    return 1 - v if bit else v


def _peer(k):
    x, y, c = _coords()
    return (_flip(x, (k >> 2) & 1), _flip(y, (k >> 1) & 1), _flip(c, k & 1))


def _my_index():
    x, y, c = _coords()
    return 4 * x + 2 * y + c


def _rcopy(src, dst, ssem, rsem, dev):
    return pltpu.make_async_remote_copy(src_ref=src, dst_ref=dst, send_sem=ssem, recv_sem=rsem,
                                        device_id=dev, device_id_type=MESH)


def _sigmoid(x):
    return jax.nn.sigmoid(x)


def _dot(a, b):
    return jnp.dot(a, b, preferred_element_type=F32)


def _dot_nt(a, b):
    return lax.dot_general(a, b, (((1,), (1,)), ((), ())), preferred_element_type=F32)


def _dot_tn(a, b):
    return lax.dot_general(a, b, (((0,), (0,)), ((), ())), preferred_element_type=F32)


def _colsum(a):
    return jnp.sum(a, axis=0, keepdims=True)


def _mod_fwd(c, w_mod_sh, b_mod4, conv_sh):
    def body(c_ref, w_ref, b_ref, cv_ref, call_ref, mod_ref, cvall_ref, rows_ref, cmat_ref, s1, r1, s2, r2, s3, r3):
        x, y, _ = _coords()
        me = _my_index()
        j = 2 * x + y
        call_ref[me] = c_ref[...]
        cvall_ref[me] = cv_ref[...]
        sends = []
        for k in range(1, NDEV):
            cp = _rcopy(call_ref.at[me], call_ref.at[me], s1.at[k - 1], r1.at[k - 1], _peer(k))
            cp.start()
            sends.append(cp)
            cp = _rcopy(cvall_ref.at[me], cvall_ref.at[me], s3.at[k - 1], r3.at[k - 1], _peer(k))
            cp.start()
            sends.append(cp)
        for k in range(1, NDEV):
            pk = me ^ k
            _rcopy(call_ref.at[pk], call_ref.at[pk], s1.at[k - 1], r1.at[k - 1], _peer(k)).wait_recv()
        for b in range(NDEV):
            cmat_ref[pl.ds(b, 1), :] = call_ref[b]
        cm = cmat_ref[...]
        act = cm * _sigmoid(cm)
        mp = jnp.dot(act, w_ref[...], preferred_element_type=F32, precision=HIGHEST) + b_ref[j]
        for b in range(NDEV):
            rows_ref[b] = mp[b:b + 1]
        mod_ref[j] = rows_ref[me]
        for q, k in enumerate((2, 4, 6)):
            cp = _rcopy(rows_ref.at[me ^ k], mod_ref.at[j], s2.at[q], r2.at[q], _peer(k))
            cp.start()
            sends.append(cp)
        for q, k in enumerate((2, 4, 6)):
            jq = j ^ (k >> 1)
            _rcopy(rows_ref.at[me], mod_ref.at[jq], s2.at[q], r2.at[q], _peer(k)).wait_recv()
        for k in range(1, NDEV):
            pk = me ^ k
            _rcopy(cvall_ref.at[pk], cvall_ref.at[pk], s3.at[k - 1], r3.at[k - 1], _peer(k)).wait_recv()
        for cp in sends:
            cp.wait_send()

    vm = pl.BlockSpec(memory_space=pltpu.VMEM)
    return pl.pallas_call(
        body, name="mod_fwd",
        out_shape=(jax.ShapeDtypeStruct((NDEV, 1, D), F32), jax.ShapeDtypeStruct((4, 1, 768), F32),
                   jax.ShapeDtypeStruct((NDEV,) + conv_sh.shape, F32)),
        in_specs=[vm, vm, vm, vm], out_specs=(vm, vm, vm),
        scratch_shapes=[pltpu.VMEM((NDEV, 1, 768), F32), pltpu.VMEM((NDEV, D), F32),
                        pltpu.SemaphoreType.DMA((7,)), pltpu.SemaphoreType.DMA((7,)),
                        pltpu.SemaphoreType.DMA((3,)), pltpu.SemaphoreType.DMA((3,)),
                        pltpu.SemaphoreType.DMA((7,)), pltpu.SemaphoreType.DMA((7,))],
        compiler_params=_params(),
    )(c, w_mod_sh, b_mod4, conv_sh)


def _mod_bwd(dmod4, gn_row, c_all):
    def body(d_ref, g_ref, call_ref, gw_ref, gb_ref, gg_ref, dall_ref, gall_ref, cmat_ref, dmat_ref, s1, r1, s2, r2):
        x, y, _ = _coords()
        me = _my_index()
        j = 2 * x + y
        dall_ref[me] = d_ref[...]
        gall_ref[me] = g_ref[...]
        sends = []
        for k in range(1, NDEV):
            for buf, ss, rs in ((dall_ref, s1, r1), (gall_ref, s2, r2)):
                cp = _rcopy(buf.at[me], buf.at[me], ss.at[k - 1], rs.at[k - 1], _peer(k))
                cp.start()
                sends.append(cp)
        for k in range(1, NDEV):
            pk = me ^ k
            for buf, ss, rs in ((dall_ref, s1, r1), (gall_ref, s2, r2)):
                _rcopy(buf.at[pk], buf.at[pk], ss.at[k - 1], rs.at[k - 1], _peer(k)).wait_recv()
        for cp in sends:
            cp.wait_send()
        gb, gg = dall_ref[0], gall_ref[0]
        for b in range(1, NDEV):
            gb = gb + dall_ref[b]
            gg = gg + gall_ref[b]
        gb_ref[...] = gb
        gg_ref[...] = gg
        for b in range(NDEV):
            cmat_ref[pl.ds(b, 1), :] = call_ref[b]
            dmat_ref[pl.ds(b, 1), :] = dall_ref[b, j]
        cm = cmat_ref[...]
        act = cm * _sigmoid(cm)
        gw_ref[...] = lax.dot_general(act, dmat_ref[...], (((0,), (0,)), ((), ())),
                                      preferred_element_type=F32, precision=HIGHEST)

    vm = pl.BlockSpec(memory_space=pltpu.VMEM)
    return pl.pallas_call(
        body, name="mod_bwd",
        out_shape=(jax.ShapeDtypeStruct((D, 768), F32), jax.ShapeDtypeStruct((4, 1, 768), F32),
                   jax.ShapeDtypeStruct((1, D), F32)),
        in_specs=[vm, vm, vm], out_specs=(vm, vm, vm),
        scratch_shapes=[pltpu.VMEM((NDEV, 4, 1, 768), F32), pltpu.VMEM((NDEV, 1, D), F32),
                        pltpu.VMEM((NDEV, D), F32), pltpu.VMEM((NDEV, 768), F32),
                        pltpu.SemaphoreType.DMA((7,)), pltpu.SemaphoreType.DMA((7,)),
                        pltpu.SemaphoreType.DMA((7,)), pltpu.SemaphoreType.DMA((7,))],
        compiler_params=_params(),
    )(dmod4, gn_row, c_all)


def _gather_norm_inproj(x, gn, shift, scale, wsh, order, tm=1024, tn=1024):
    s = x.shape[0]
    ni = s // tm
    npc = PW // tn
    rows, cols = wsh.shape
    half = rows // 2
    nch = 4
    cr = half // nch
    chips = ((1, 0), (0, 1), (1, 1))

    def body(ord_ref, x_ref, gn_ref, sh_ref, sc_ref, wsh_hbm, p_ref, h_ref, wg_hbm, hs_all, w_s, wsem, ss, rs):
        slot, i, col = pl.program_id(0), pl.program_id(1), pl.program_id(2)
        cx, cy, cc = _coords()
        j = 2 * cx + cy
        sib = (cx, cy, 1 - cc)
        mine = lambda n: pl.ds(cc * half + n * cr, cr)
        theirs = lambda n: pl.ds((1 - cc) * half + n * cr, cr)
        shard_of = lambda q: j ^ (2 * chips[q][0] + chips[q][1])

        def to_chip(q, n):
            e = nch * q + n
            return _rcopy(wsh_hbm.at[mine(n)], wg_hbm.at[j, mine(n)], ss.at[e], rs.at[e],
                          (_flip(cx, chips[q][0]), _flip(cy, chips[q][1]), cc))

        def from_chip(q, n):
            e = nch * q + n
            return _rcopy(wsh_hbm.at[mine(n)], wg_hbm.at[shard_of(q), mine(n)], ss.at[e], rs.at[e], sib)

        def to_sibling(q, n):
            e = 3 * nch + nch * q + n
            return _rcopy(wg_hbm.at[shard_of(q), mine(n)], wg_hbm.at[shard_of(q), mine(n)], ss.at[e], rs.at[e], sib)

        def from_sibling(q, n):
            e = 3 * nch + nch * q + n
            return _rcopy(wsh_hbm.at[mine(n)], wg_hbm.at[shard_of(q), theirs(n)], ss.at[e], rs.at[e], sib)

        def load(sl, src):
            cp = pltpu.make_async_copy(src, w_s.at[sl], wsem.at[sl])
            cp.start()
            cp.wait()

        first = (i == 0) & (col == 0)

        @pl.when(first & (slot == 0))
        def _():
            for n in range(nch):
                for q in (0, 1):
                    to_chip(q, n).start()
            load(0, wsh_hbm.at[pl.ds(0, D), :])

        @pl.when(first & (slot == 1))
        def _():
            for q in (0, 1):
                for n in range(nch):
                    from_chip(q, n).wait_recv()
                    to_sibling(q, n).start()
            for n in range(nch):
                to_chip(2, n).start()
            for n in range(nch):
                from_sibling(0, n).wait_recv()
            load(1, wg_hbm.at[shard_of(0), pl.ds(0, D), :])

        @pl.when(first & (slot == 2))
        def _():
            for n in range(nch):
                from_sibling(1, n).wait_recv()
            load(2, wg_hbm.at[shard_of(1), pl.ds(0, D), :])

        @pl.when(first & (slot == 3))
        def _():
            for n in range(nch):
                from_chip(2, n).wait_recv()
                to_sibling(2, n).start()
            for n in range(nch):
                from_sibling(2, n).wait_recv()
            load(3, wg_hbm.at[shard_of(2), pl.ds(0, D), :])
            for q in range(3):
                for n in range(nch):
                    to_chip(q, n).wait_send()
                    to_sibling(q, n).wait_send()

        @pl.when((slot == 0) & (col == 0))
        def _():
            xt = x_ref[...]
            rstd = lax.rsqrt(jnp.mean(xt * xt, axis=-1, keepdims=True) + EPS)
            h = ((xt * rstd * gn_ref[...]) * (1.0 + sc_ref[...]) + sh_ref[...]).astype(BF16)
            hs_all[i] = h
            h_ref[...] = h

        p_ref[0] = _dot(hs_all[i], w_s[slot, :, pl.ds(pl.multiple_of(col * tn, tn), tn)]).astype(BF16)

    row = pl.BlockSpec((1, D), lambda sl, i, col, o: (0, 0))
    x_rows = lambda sl, i, col, o: (jnp.where(sl == 0, i, ni - 1), 0)
    any_ = pl.BlockSpec(memory_space=pl.ANY)
    return pl.pallas_call(
        body, name="gather_norm_inproj",
        grid_spec=pltpu.PrefetchScalarGridSpec(
            num_scalar_prefetch=1, grid=(4, ni, npc),
            in_specs=[pl.BlockSpec((tm, D), x_rows), row, row, row, any_],
            out_specs=(pl.BlockSpec((1, tm, tn), lambda sl, i, col, o: (o[sl], i, col)),
                       pl.BlockSpec((tm, D), x_rows), any_),
            scratch_shapes=[pltpu.VMEM((ni, tm, D), BF16), pltpu.VMEM((4, D, PW), BF16),
                            pltpu.SemaphoreType.DMA((4,)),
                            pltpu.SemaphoreType.DMA((6 * nch,)), pltpu.SemaphoreType.DMA((6 * nch,))]),
        out_shape=(jax.ShapeDtypeStruct((4, s, PW), BF16), jax.ShapeDtypeStruct((s, D), BF16),
                   jax.ShapeDtypeStruct((4, rows, cols), wsh.dtype)),
        compiler_params=_params(("arbitrary", "arbitrary", "arbitrary")),
    )(order, x, gn, shift, scale, wsh)


def _shift_down(prev8, cur, d):
    t = cur.shape[0]
    c3 = cur.reshape(t // 8, 8, DH)
    rot = pltpu.roll(c3, d, 1)
    before = jnp.concatenate([pltpu.roll(prev8, d, 0).reshape(1, 8, DH), rot[:-1]], axis=0)
    rows = lax.broadcasted_iota(jnp.int32, c3.shape, 1)
    return jnp.where(rows >= d, rot, before).reshape(t, DH)


def _shift_up(cur, next8, d):
    t = cur.shape[0]
    c3 = cur.reshape(t // 8, 8, DH)
    rot = pltpu.roll(c3, 8 - d, 1)
    after = jnp.concatenate([rot[1:], pltpu.roll(next8, 8 - d, 0).reshape(1, 8, DH)], axis=0)
    rows = lax.broadcasted_iota(jnp.int32, c3.shape, 1)
    return jnp.where(rows < 8 - d, rot, after).reshape(t, DH)


def _rnn_gates(xr, prev8, cw, cb, wa, ba, wx, bx, lam, reset):
    xc = cw[3:4] * xr + cb
    for d in (1, 2, 3):
        xc = xc + cw[3 - d:4 - d] * _shift_down(prev8, xr, d)
    xcb = xc.astype(BF16)
    r = _sigmoid(_dot(xcb, wa.astype(BF16)) + ba)
    ig = _sigmoid(_dot(xcb, wx.astype(BF16)) + bx)
    nl = -lam
    sp = jnp.maximum(nl, 0.0) + jnp.log1p(jnp.exp(-jnp.abs(nl)))
    log_a = (-LRU_C * r) * sp
    a_raw = jnp.exp(log_a)
    a = jnp.where(reset, 0.0, a_raw)
    mult = jnp.where(reset, 1.0, jnp.sqrt(1.0 - a_raw * a_raw))
    return xc, r, ig, sp, a, mult


def _log_scan(a, b, axis, up):
    n = a.shape[axis]
    rows = lax.broadcasted_iota(jnp.int32, a.shape, axis)
    d = 1
    while d < n:
        m = rows < n - d if up else rows >= d
        shift = n - d if up else d
        a_s = pltpu.roll(a, shift, axis)
        b_s = pltpu.roll(b, shift, axis)
        b = jnp.where(m, a * b_s + b, b)
        a = jnp.where(m, a * a_s, a)
        d *= 2
    return a, b


def _scan(a, b, t, edge, up=False):
    g = t // 8
    a3, b3 = _log_scan(a.reshape(g, 8, DH), b.reshape(g, 8, DH), 1, up)
    last = 0 if up else 7
    ag, bg = _log_scan(a3[:, last, :], b3[:, last, :], 0, up)
    hg = ag * edge + bg
    grp = lax.broadcasted_iota(jnp.int32, hg.shape, 0)
    if up:
        cin = jnp.where(grp == g - 1, edge, pltpu.roll(hg, g - 1, 0))
        tail = hg[0:1]
    else:
        cin = jnp.where(grp == 0, edge, pltpu.roll(hg, 1, 0))
        tail = hg[g - 1:g]
    return (a3 * cin[:, None, :] + b3).reshape(t, DH), tail


def _rnn_fwd(p, pos, conv_w, conv_b, w_a, b_a, w_x, b_x, lam, tt=1024):
    s = p.shape[1]
    nt = s // tt

    def body(xr_ref, z_ref, pos_ref, cw_ref, cb_ref, wa_ref, ba_ref, wx_ref, bx_ref, lam_ref,
             hr_ref, gr_ref, xprev, hprev):
        @pl.when(pl.program_id(1) == 0)
        def _():
            xprev[...] = jnp.zeros_like(xprev)
            hprev[...] = jnp.zeros_like(hprev)

        xr = xr_ref[0].astype(F32)
        z = z_ref[0].astype(F32)
        reset = pos_ref[...] > 0.5
        xc, r, ig, sp, a, mult = _rnn_gates(xr, xprev[...], cw_ref[...], cb_ref[...], wa_ref[0], ba_ref[0],
                                            wx_ref[0], bx_ref[0], lam_ref[...], reset)
        bx = mult * ig * xc
        h, h_last = _scan(a, bx, tt, hprev[0:1])
        xprev[...] = xr[tt - 8:]
        hprev[...] = jnp.broadcast_to(h_last, (8, DH))
        hr_ref[...] = h
        gr_ref[...] = (h * (z * _sigmoid(z))).astype(BF16)

    head_row = lambda hh, t: (0, hh)
    return pl.pallas_call(
        body, name="rnn_fwd", grid=(H, nt),
        in_specs=[pl.BlockSpec((1, tt, DH), lambda hh, t: (0, t, hh)),
                  pl.BlockSpec((1, tt, DH), lambda hh, t: (0, t, H + hh)),
                  pl.BlockSpec((tt, DH), lambda hh, t: (t, 0)),
                  pl.BlockSpec((4, DH), head_row), pl.BlockSpec((1, DH), head_row),
                  pl.BlockSpec((1, DH, DH), lambda hh, t: (hh, 0, 0)), pl.BlockSpec((1, 1, DH), lambda hh, t: (hh, 0, 0)),
                  pl.BlockSpec((1, DH, DH), lambda hh, t: (hh, 0, 0)), pl.BlockSpec((1, 1, DH), lambda hh, t: (hh, 0, 0)),
                  pl.BlockSpec((1, DH), head_row)],
        out_specs=(pl.BlockSpec((tt, DH), lambda hh, t: (t, hh)), pl.BlockSpec((tt, DH), lambda hh, t: (t, hh))),
        out_shape=(jax.ShapeDtypeStruct((s, D), F32), jax.ShapeDtypeStruct((s, D), BF16)),
        scratch_shapes=[pltpu.VMEM((8, DH), F32), pltpu.VMEM((8, DH), F32)],
        compiler_params=_params(("parallel", "arbitrary")),
    )(p, p, pos, conv_w, conv_b, w_a, b_a, w_x, b_x, lam)


def _rnn_bwd(p, hr, dgr, pos, conv_w, conv_b, w_a, b_a, w_x, b_x, lam, tt=1024):
    s = p.shape[1]
    nt = s // tt
    t8 = tt // 8

    def body(xr_ref, z_ref, xp_ref, hr_ref, hp_ref, dg_ref, pos_ref, cw_ref, cb_ref, wa_ref, ba_ref, wx_ref, bx_ref,
             lam_ref, dxr_ref, dz_ref, gwa_ref, gba_ref, gwx_ref, gbx_ref, glam_ref, gcw_ref, gcb_ref,
             a_next, g_next, dxc_next):
        t = pl.program_id(1)
        has_prev = t < nt - 1

        @pl.when(t == 0)
        def _():
            a_next[...] = jnp.zeros_like(a_next)
            g_next[...] = jnp.zeros_like(g_next)
            dxc_next[...] = jnp.zeros_like(dxc_next)
            gwa_ref[...] = jnp.zeros_like(gwa_ref)
            gba_ref[...] = jnp.zeros_like(gba_ref)
            gwx_ref[...] = jnp.zeros_like(gwx_ref)
            gbx_ref[...] = jnp.zeros_like(gbx_ref)
            glam_ref[...] = jnp.zeros_like(glam_ref)
            gcw_ref[...] = jnp.zeros_like(gcw_ref)
            gcb_ref[...] = jnp.zeros_like(gcb_ref)

        xr = xr_ref[0].astype(F32)
        z = z_ref[0].astype(F32)
        hr_blk = hr_ref[...]
        dg = dg_ref[...]
        xprev = jnp.where(has_prev, xp_ref[0].astype(F32)[8:], 0.0)
        hprev8 = jnp.where(has_prev, hp_ref[...], 0.0)
        reset = pos_ref[...] > 0.5
        cw = cw_ref[...]
        wa = wa_ref[0]
        wx = wx_ref[0]
        lam_v = lam_ref[...]
        xc, r, ig, sp, a, mult = _rnn_gates(xr, xprev, cw, cb_ref[...], wa, ba_ref[0], wx, bx_ref[0], lam_v, reset)

        sz = _sigmoid(z)
        dh = dg * (z * sz)
        dz_ref[...] = (dg * hr_blk * (sz * (1.0 + z * (1.0 - sz)))).astype(BF16)

        an = _shift_up(a, a_next[...], 1)
        g, g_first = _scan(an, dh, tt, g_next[0:1], up=True)
        a_next[...] = jnp.broadcast_to(a[0:1], (8, DH))
        g_next[...] = jnp.broadcast_to(g_first, (8, DH))

        hm1 = _shift_down(hprev8, hr_blk, 1)
        da = g * hm1
        dmult = g * (ig * xc)
        di = g * (mult * xc)
        dxc = g * (mult * ig)
        dla = jnp.where(reset, 0.0, da * a - dmult * (a * a) / mult)
        dr = dla * (-LRU_C * sp)
        dsp = _colsum(dla * (-LRU_C * r))
        glam_ref[0] += dsp * (-_sigmoid(-lam_v))
        dpa = dr * r * (1.0 - r)
        dpx = di * ig * (1.0 - ig)
        dpab = dpa.astype(BF16)
        dpxb = dpx.astype(BF16)
        dxc = dxc + _dot_nt(dpab, wa.astype(BF16)) + _dot_nt(dpxb, wx.astype(BF16))
        xcb = xc.astype(BF16)
        gwa_ref[0] += _dot_tn(xcb, dpab)
        gwx_ref[0] += _dot_tn(xcb, dpxb)
        gba_ref[0] += _colsum(dpa)
        gbx_ref[0] += _colsum(dpx)

        dxr = cw[3:4] * dxc
        for d in (1, 2, 3):
            dxr = dxr + cw[3 - d:4 - d] * _shift_up(dxc, dxc_next[...], d)
        dxr_ref[...] = dxr.astype(BF16)
        dxc_next[...] = dxc[0:8]
        gcb_ref[0] += _colsum(dxc)
        gcw_ref[0, 3:4, :] += _colsum(xr * dxc)
        for d in (1, 2, 3):
            gcw_ref[0, 3 - d:4 - d, :] += _colsum(_shift_down(xprev, xr, d) * dxc)

    rt = lambda t: nt - 1 - t
    prev8 = lambda t: jnp.maximum(rt(t) * t8 - 1, 0)
    head_row = lambda hh, t: (0, hh)
    hsm = lambda hh, t: (hh, 0, 0)
    return pl.pallas_call(
        body, name="rnn_bwd", grid=(H, nt),
        in_specs=[pl.BlockSpec((1, tt, DH), lambda hh, t: (0, rt(t), hh)),
                  pl.BlockSpec((1, tt, DH), lambda hh, t: (0, rt(t), H + hh)),
                  pl.BlockSpec((1, 16, DH), lambda hh, t: (0, jnp.maximum(rt(t) * (tt // 16) - 1, 0), hh)),
                  pl.BlockSpec((tt, DH), lambda hh, t: (rt(t), hh)),
                  pl.BlockSpec((8, DH), lambda hh, t: (prev8(t), hh)),
                  pl.BlockSpec((tt, DH), lambda hh, t: (rt(t), hh)),
                  pl.BlockSpec((tt, DH), lambda hh, t: (rt(t), 0)),
                  pl.BlockSpec((4, DH), head_row), pl.BlockSpec((1, DH), head_row),
                  pl.BlockSpec((1, DH, DH), hsm), pl.BlockSpec((1, 1, DH), hsm),
                  pl.BlockSpec((1, DH, DH), hsm), pl.BlockSpec((1, 1, DH), hsm),
                  pl.BlockSpec((1, DH), head_row)],
        out_specs=(pl.BlockSpec((tt, DH), lambda hh, t: (rt(t), hh)), pl.BlockSpec((tt, DH), lambda hh, t: (rt(t), hh)),
                   pl.BlockSpec((1, DH, DH), hsm), pl.BlockSpec((1, 1, DH), hsm),
                   pl.BlockSpec((1, DH, DH), hsm), pl.BlockSpec((1, 1, DH), hsm),
                   pl.BlockSpec((1, 1, DH), hsm), pl.BlockSpec((1, 4, DH), hsm), pl.BlockSpec((1, 1, DH), hsm)),
        out_shape=(jax.ShapeDtypeStruct((s, D), BF16), jax.ShapeDtypeStruct((s, D), BF16),
                   jax.ShapeDtypeStruct((H, DH, DH), F32), jax.ShapeDtypeStruct((H, 1, DH), F32),
                   jax.ShapeDtypeStruct((H, DH, DH), F32), jax.ShapeDtypeStruct((H, 1, DH), F32),
                   jax.ShapeDtypeStruct((H, 1, DH), F32), jax.ShapeDtypeStruct((H, 4, DH), F32),
                   jax.ShapeDtypeStruct((H, 1, DH), F32)),
        scratch_shapes=[pltpu.VMEM((8, DH), F32), pltpu.VMEM((8, DH), F32), pltpu.VMEM((8, DH), F32)],
        compiler_params=_params(("parallel", "arbitrary")),
    )(p, p, p, hr, hr, dgr, pos, conv_w, conv_b, w_a, b_a, w_x, b_x, lam)


def _rope(t, c, sa, sb):
    return t * c + pltpu.roll(t, DH - ROT // 2, 1) * sa + pltpu.roll(t, ROT // 2, 1) * sb


def _rope_bwd(g, c, sa, sb):
    return g * c + pltpu.roll(g * sa, ROT // 2, 1) + pltpu.roll(g * sb, DH - ROT // 2, 1)


def _unit_bases(gi, u):
    dil = DILATIONS[gi]
    if dil == 1:
        return u * UB, SPAN + (u - 1) * UB, u == 0
    if dil == 4:
        blk, r = u // 4, u % 4
        return blk * 4 * UB + r, SPAN + (blk - 1) * 4 * UB + r, blk == 0
    return u, u, True


def _unit_slices(gi, u):
    dil = DILATIONS[gi]
    qb0, kb0, first = _unit_bases(gi, u)
    if dil == 1:
        if not isinstance(qb0, int):
            qb0, kb0 = pl.multiple_of(qb0, UB), pl.multiple_of(kb0, UB)
        return pl.ds(qb0, UB), pl.ds(kb0, 2 * UB), first
    return pl.ds(qb0, UB, stride=dil), pl.ds(kb0, 2 * UB, stride=dil), first


def _bdot(a, b):
    return lax.dot_general(a, b, (((2,), (1,)), ((0,), (0,))), preferred_element_type=F32)


def _bdot_nt(a, b):
    return lax.dot_general(a, b, (((2,), (2,)), ((0,), (0,))), preferred_element_type=F32)


def _bdot_tn(a, b):
    return lax.dot_general(a, b, (((1,), (1,)), ((0,), (0,))), preferred_element_type=F32)


def _band_mask(first_in_span, has_prev):
    qi = lax.broadcasted_iota(jnp.int32, (UB, 2 * UB), 0)
    ki = lax.broadcasted_iota(jnp.int32, (UB, 2 * UB), 1)
    dist = UB + qi - ki
    band = (dist >= 0) & (dist <= UB)
    return band & ((ki >= UB) | jnp.logical_not(first_in_span) | has_prev)


def _gather_halves(phase, src_hbm, dst_hbm, ss, rs):
    half = src_hbm.shape[0] // 2
    cx, cy, cc = _coords()
    j = 2 * cx + cy
    sib = (cx, cy, 1 - cc)
    mine = pl.ds(cc * half, half)
    theirs = pl.ds((1 - cc) * half, half)
    chips = ((1, 0), (0, 1), (1, 1))
    for q, (kx, ky) in enumerate(chips):
        jq = j ^ (2 * kx + ky)
        out = _rcopy(src_hbm.at[mine], dst_hbm.at[j, mine], ss.at[q], rs.at[q], (_flip(cx, kx), _flip(cy, ky), cc))
        landed = _rcopy(src_hbm.at[mine], dst_hbm.at[jq, mine], ss.at[q], rs.at[q], sib)
        onward = _rcopy(dst_hbm.at[jq, mine], dst_hbm.at[jq, mine], ss.at[3 + q], rs.at[3 + q], sib)
        from_sib = _rcopy(src_hbm.at[mine], dst_hbm.at[jq, theirs], ss.at[3 + q], rs.at[3 + q], sib)
        if phase == 0:
            out.start()
        elif phase == 1:
            landed.wait_recv()
            onward.start()
        else:
            from_sib.wait_recv()
            out.wait_send()
            onward.wait_send()


def _attn_fwd(p, rc, rsa, rsb, w3sh):
    s = p.shape[1]
    ns = s // SPAN
    nunit = SPAN // UB

    def body(q_ref, k_ref, v_ref, z_ref, c_ref, sa_ref, sb_ref, w3_hbm, o_ref, lse_ref, ga_ref, qro_ref, kro_ref,
             w3g_hbm, qr, kf, vf, acc, mm, ll, ss, rs):
        hh, n = pl.program_id(0), pl.program_id(1)
        for phase, at_head, at_span in ((0, 0, 0), (1, H // 2, 0), (2, H - 1, ns - 1)):
            @pl.when((hh == at_head) & (n == at_span))
            def _(phase=phase):
                _gather_halves(phase, w3_hbm, w3g_hbm, ss, rs)

        @pl.when(n == 0)
        def _():
            kf[0:SPAN] = jnp.zeros((SPAN, DH), F32)
            vf[0:SPAN] = jnp.zeros((SPAN, DH), F32)

        c, sa, sb = c_ref[...], sa_ref[...], sb_ref[...]
        q_rot = _rope(q_ref[0].astype(F32), c, sa, sb).astype(BF16)
        k_rot = _rope(k_ref[0].astype(F32), c, sa, sb).astype(BF16)
        qro_ref[...] = q_rot
        kro_ref[...] = k_rot
        qr[...] = q_rot.astype(F32)
        kf[SPAN:] = k_rot.astype(F32)
        vf[SPAN:] = v_ref[0].astype(F32)
        has_prev = n > 0

        for gi, dil in enumerate(DILATIONS):
            def trip(t, carry, gi=gi, dil=dil):
                qsls, ksls, firsts = [], [], []
                for b in range(UNIT_BATCH_FWD):
                    qsl, ksl, first = _unit_slices(gi, t * UNIT_BATCH_FWD + b)
                    qsls.append(qsl)
                    ksls.append(ksl)
                    firsts.append(first)
                qb = jnp.stack([qr[qsl, :].astype(BF16) for qsl in qsls])
                kb = jnp.stack([kf[ksl, :].astype(BF16) for ksl in ksls])
                vb = jnp.stack([vf[ksl, :].astype(BF16) for ksl in ksls])
                s_all = _bdot_nt(qb, kb)
                prs = []
                for b in range(UNIT_BATCH_FWD):
                    sc = jnp.where(_band_mask(firsts[b], has_prev), s_all[b] * SCALE, NEG)
                    m = jnp.max(sc, axis=-1, keepdims=True)
                    pr = jnp.exp(sc - m)
                    l = jnp.sum(pr, axis=-1, keepdims=True)
                    mm[gi, qsls[b], :] = jnp.broadcast_to(m, (UB, DH))
                    ll[gi, qsls[b], :] = jnp.broadcast_to(l, (UB, DH))
                    prs.append(pr.astype(BF16))
                o_all = _bdot(jnp.stack(prs), vb)
                for b in range(UNIT_BATCH_FWD):
                    acc[gi, qsls[b], :] = o_all[b]
                return carry

            lax.fori_loop(0, nunit // UNIT_BATCH_FWD, trip, 0)

        m_all =jnp.maximum(jnp.maximum(mm[0], mm[1]), mm[2])
        num = jnp.zeros((SPAN, DH), F32)
        den = jnp.zeros((SPAN, DH), F32)
        for gi in range(3):
            w = jnp.exp(mm[gi] - m_all)
            num = num + w * acc[gi]
            den = den + w * ll[gi]
        o = num / den
        o_ref[...] = o
        lse_ref[...] = m_all + jnp.log(den)
        z = z_ref[0].astype(F32)
        ga_ref[...] = (o * (z * _sigmoid(z))).astype(BF16)
        kf[0:SPAN] = kf[SPAN:]
        vf[0:SPAN] = vf[SPAN:]

    blk = lambda piece, off: pl.BlockSpec((1, SPAN, DH), lambda hh, n: (piece, n, off + hh))
    tab = pl.BlockSpec((SPAN, DH), lambda hh, n: (n, 0))
    outb = pl.BlockSpec((SPAN, DH), lambda hh, n: (n, hh))
    any_ = pl.BlockSpec(memory_space=pl.ANY)
    return pl.pallas_call(
        body, name="attn_fwd", grid=(H, ns),
        in_specs=[blk(1, 0), blk(1, H), blk(2, 0), blk(2, H), tab, tab, tab, any_],
        out_specs=(outb, outb, outb, outb, outb, any_),
        out_shape=(jax.ShapeDtypeStruct((s, D), F32), jax.ShapeDtypeStruct((s, D), F32),
                   jax.ShapeDtypeStruct((s, D), BF16), jax.ShapeDtypeStruct((s, D), BF16),
                   jax.ShapeDtypeStruct((s, D), BF16), jax.ShapeDtypeStruct((4,) + w3sh.shape, w3sh.dtype)),
        scratch_shapes=[pltpu.VMEM((SPAN, DH), F32), pltpu.VMEM((2 * SPAN, DH), F32), pltpu.VMEM((2 * SPAN, DH), F32),
                        pltpu.VMEM((3, SPAN, DH), F32), pltpu.VMEM((3, SPAN, DH), F32), pltpu.VMEM((3, SPAN, DH), F32),
                        pltpu.SemaphoreType.DMA((6,)), pltpu.SemaphoreType.DMA((6,))],
        compiler_params=_params(("arbitrary", "arbitrary")),
    )(p, p, p, p, rc, rsa, rsb, w3sh)


def _attn_bwd(p, q_rot, k_rot, o, lse, dga, rc, rsa, rsb, packs):
    s = p.shape[1]
    ns = s // SPAN
    nunit = SPAN // UB
    npk = len(packs)

    def body(*refs):
        (q_ref, k_ref, kp_ref, v_ref, vp_ref, z_ref, c_ref, sa_ref, sb_ref, o_ref, lse_ref, dg_ref) = refs[:12]
        pk_refs = refs[12:12 + npk]
        dq_ref, dk_ref, dv_ref, dz_ref = refs[12 + npk:16 + npk]
        red_refs = refs[16 + npk:16 + 2 * npk]
        qr, kf, vf, dof, dlt, dqa, dkf, dvf = refs[16 + 2 * npk:24 + 2 * npk]
        rbufs = refs[24 + 2 * npk:24 + 3 * npk]
        sums = refs[24 + 3 * npk:24 + 4 * npk]
        ar_sems = refs[24 + 4 * npk:]
        hh, step = pl.program_id(0), pl.program_id(1)
        n = ns - 1 - step
        has_prev = n > 0
        for phase, at_head, at_step in ((0, 0, 0), (1, H // 2, 0), (2, H - 1, ns - 1)):
            @pl.when((hh == at_head) & (step == at_step))
            def _(phase=phase):
                _allreduce_phase(phase, pk_refs, sums, rbufs, *ar_sems, out_refs=red_refs)

        @pl.when(step == 0)
        def _():
            dkf[...] = jnp.zeros_like(dkf)
            dvf[...] = jnp.zeros_like(dvf)

        @pl.when(step > 0)
        def _():
            dkf[SPAN:] = dkf[0:SPAN]
            dvf[SPAN:] = dvf[0:SPAN]
            dkf[0:SPAN] = jnp.zeros((SPAN, DH), F32)
            dvf[0:SPAN] = jnp.zeros((SPAN, DH), F32)

        c, sa, sb = c_ref[...], sa_ref[...], sb_ref[...]
        qr[...] = q_ref[...].astype(F32)
        kf[SPAN:] = k_ref[...].astype(F32)
        vf[SPAN:] = v_ref[0].astype(F32)
        kf[0:SPAN] = jnp.where(has_prev, kp_ref[...].astype(F32), 0.0)
        vf[0:SPAN] = jnp.where(has_prev, vp_ref[0].astype(F32), 0.0)
        z = z_ref[0].astype(F32)
        sz = _sigmoid(z)
        dg = dg_ref[...]
        ov = o_ref[...]
        do = dg * (z * sz)
        dz_ref[...] = (dg * ov * (sz * (1.0 + z * (1.0 - sz)))).astype(BF16)
        dof[...] = do
        dlt[...] = jnp.dot(do * ov, jnp.ones((DH, DH), F32), preferred_element_type=F32, precision=HIGHEST)
        dqa[...] = jnp.zeros_like(dqa)

        for gi, dil in enumerate(DILATIONS):
            def trip(t, carry, gi=gi, dil=dil):
                qsls, ksls, firsts = [], [], []
                for b in range(UNIT_BATCH):
                    qsl, ksl, first = _unit_slices(gi, t * UNIT_BATCH + b)
                    qsls.append(qsl)
                    ksls.append(ksl)
                    firsts.append(first)
                qb = jnp.stack([qr[qsl, :].astype(BF16) for qsl in qsls])
                kb = jnp.stack([kf[ksl, :].astype(BF16) for ksl in ksls])
                vb = jnp.stack([vf[ksl, :].astype(BF16) for ksl in ksls])
                dob = jnp.stack([dof[qsl, :].astype(BF16) for qsl in qsls])
                s_all = _bdot_nt(qb, kb)
                dp_all = _bdot_nt(dob, vb)
                prs, dss = [], []
                for b in range(UNIT_BATCH):
                    lse_b = lse_ref[qsls[b], :]
                    dl_b = dlt[qsls[b], :]
                    pr = jnp.exp(s_all[b] * SCALE - jnp.concatenate([lse_b, lse_b], axis=1))
                    pr = jnp.where(_band_mask(firsts[b], has_prev), pr, 0.0)
                    prs.append(pr.astype(BF16))
                    dss.append((pr * (dp_all[b] - jnp.concatenate([dl_b, dl_b], axis=1)) * SCALE).astype(BF16))
                ds_all = jnp.stack(dss)
                dv_all = _bdot_tn(jnp.stack(prs), dob)
                dq_all = _bdot(ds_all, kb)
                dk_all = _bdot_tn(ds_all, qb)
                for b in range(UNIT_BATCH):
                    dvf[ksls[b], :] += dv_all[b]
                    dqa[qsls[b], :] += dq_all[b]
                    dkf[ksls[b], :] += dk_all[b]
                return carry

            lax.fori_loop(0, nunit // UNIT_BATCH, trip, 0)

        dq_ref[...] = _rope_bwd(dqa[...], c, sa, sb).astype(BF16)
        dk_ref[...] = _rope_bwd(dkf[SPAN:], c, sa, sb).astype(BF16)
        dv_ref[...] = dvf[SPAN:].astype(BF16)

    rn = lambda n: ns - 1 - n
    pn = lambda n: jnp.maximum(ns - 2 - n, 0)
    blk = lambda piece, off: pl.BlockSpec((1, SPAN, DH), lambda hh, n: (piece, rn(n), off + hh))
    blkp = lambda piece, off: pl.BlockSpec((1, SPAN, DH), lambda hh, n: (piece, pn(n), off + hh))
    tab = pl.BlockSpec((SPAN, DH), lambda hh, n: (rn(n), 0))
    io = pl.BlockSpec((SPAN, DH), lambda hh, n: (rn(n), hh))
    iop = pl.BlockSpec((SPAN, DH), lambda hh, n: (pn(n), hh))
    vm = pl.BlockSpec(memory_space=pltpu.VMEM)
    outs = pl.pallas_call(
        body, name="attn_bwd", grid=(H, ns),
        in_specs=[io, io, iop, blk(2, 0), blkp(2, 0), blk(2, H), tab, tab, tab, io, io, io] + [vm] * npk,
        out_specs=(io, io, io, io) + (vm,) * npk,
        out_shape=tuple(jax.ShapeDtypeStruct((s, D), BF16) for _ in range(4)) +
                  tuple(jax.ShapeDtypeStruct(a.shape, F32) for a in packs),
        scratch_shapes=[pltpu.VMEM((SPAN, DH), F32), pltpu.VMEM((2 * SPAN, DH), F32), pltpu.VMEM((2 * SPAN, DH), F32),
                        pltpu.VMEM((SPAN, DH), F32), pltpu.VMEM((SPAN, DH), F32), pltpu.VMEM((SPAN, DH), F32),
                        pltpu.VMEM((2 * SPAN, DH), F32), pltpu.VMEM((2 * SPAN, DH), F32)] +
                       [pltpu.VMEM((NDEV, a.shape[0] // NDEV, a.shape[1]), F32) for a in packs] +
                       [pltpu.VMEM(a.shape, F32) for a in packs] +
                       [pltpu.SemaphoreType.DMA((7 * npk,)) for _ in range(4)],
        compiler_params=_params(("arbitrary", "arbitrary")),
    )(q_rot, k_rot, k_rot, p, p, p, rc, rsa, rsb, o, lse, dga, *packs)
    return outs[:4], outs[4:]


def _tail(gr, ga, p, x, tgt, w3, b_gate, gate, g_final, tm=256):
    s = x.shape[0]
    nt = s // tm

    def body(gr_ref, ga_ref, pr_ref, pa_ref, x_ref, t_ref, bg_ref, gate_ref, gf_ref, w_hbm,
             dgr_ref, dga_ref, dc_ref, dx2_ref, vec_ref, go_hbm, w_s, acc_s, sem):
        i = pl.program_id(0)

        @pl.when(i == 0)
        def _():
            cp = pltpu.make_async_copy(w_hbm, w_s, sem.at[12])
            cp.start()
            acc_s[...] = jnp.zeros_like(acc_s)
            vec_ref[...] = jnp.zeros_like(vec_ref)
            cp.wait()

        grb = gr_ref[...]
        gab = ga_ref[...]
        bg = bg_ref[...]
        gate_v = gate_ref[...]
        gf = gf_ref[...]
        y_r = _dot(grb, w_s[0])
        y_a = _dot(gab, w_s[1])
        sr = _sigmoid(pr_ref[0].astype(F32) + bg[:, :D])
        sa = _sigmoid(pa_ref[0].astype(F32) + bg[:, D:])
        mb = (sr * y_r + sa * y_a).astype(BF16)
        u = _dot(mb, w_s[2])
        x2 = x_ref[...] + gate_v * u
        rstd = lax.rsqrt(jnp.mean(x2 * x2, axis=-1, keepdims=True) + EPS)
        xh = x2 * rstd
        e = xh * gf - t_ref[...]
        dy = e * (1.0 / D)
        dyg = dy * gf
        dx2 = rstd * (dyg - xh * jnp.mean(dyg * xh, axis=-1, keepdims=True))
        dx2_ref[...] = dx2
        dub = (dx2 * gate_v).astype(BF16)
        dm = _dot_nt(dub, w_s[2])
        dyr = (dm * sr).astype(BF16)
        dya = (dm * sa).astype(BF16)
        dpr = dm * y_r * (sr * (1.0 - sr))
        dpa = dm * y_a * (sa * (1.0 - sa))
        dc_ref[:, :D] = dpr.astype(BF16)
        dc_ref[:, D:] = dpa.astype(BF16)
        dgr_ref[...] = _dot_nt(dyr, w_s[0])
        dga_ref[...] = _dot_nt(dya, w_s[1])
        acc_s[0] += _dot_tn(grb, dyr)
        acc_s[1] += _dot_tn(gab, dya)
        acc_s[2] += _dot_tn(mb, dub)
        vec_ref[0:1, :] += _colsum(dy * xh)
        vec_ref[1:2, :] += _colsum(dx2 * u)
        vec_ref[2:3, :] += _colsum(dpr)
        vec_ref[3:4, :] += _colsum(dpa)
        vec_ref[4:5, :] += _colsum(e * e)

        @pl.when(i == nt - 1)
        def _():
            vec_ref[4:5, :] = jnp.broadcast_to(jnp.sum(vec_ref[4:5, :]) * (0.5 / D), (1, D))
            cps = []
            for w in range(3):
                for j in range(4):
                    cps.append(pltpu.make_async_copy(acc_s.at[w, pl.ds(256 * j, 256)],
                                                     go_hbm.at[j, pl.ds(256 * w, 256)], sem.at[4 * w + j]))
            for cp in cps:
                cp.start()
            for cp in cps:
                cp.wait()

    rowt = lambda i: (i, 0)
    row = lambda w: pl.BlockSpec((1, w), lambda i: (0, 0))
    any_ = pl.BlockSpec(memory_space=pl.ANY)
    return pl.pallas_call(
        body, name="tail", grid=(nt,),
        in_specs=[pl.BlockSpec((tm, D), rowt), pl.BlockSpec((tm, D), rowt),
                  pl.BlockSpec((1, tm, D), lambda i: (3, i, 0)), pl.BlockSpec((1, tm, D), lambda i: (3, i, 1)),
                  pl.BlockSpec((tm, D), rowt), pl.BlockSpec((tm, D), rowt),
                  row(2 * D), row(D), row(D), any_],
        out_specs=(pl.BlockSpec((tm, D), rowt), pl.BlockSpec((tm, D), rowt), pl.BlockSpec((tm, 2 * D), rowt),
                   pl.BlockSpec((tm, D), rowt), pl.BlockSpec((8, D), lambda i: (0, 0)), any_),
        out_shape=(jax.ShapeDtypeStruct((s, D), F32), jax.ShapeDtypeStruct((s, D), F32),
                   jax.ShapeDtypeStruct((s, 2 * D), BF16), jax.ShapeDtypeStruct((s, D), F32),
                   jax.ShapeDtypeStruct((8, D), F32), jax.ShapeDtypeStruct((4, 768, D), F32)),
        scratch_shapes=[pltpu.VMEM((3, D, D), BF16), pltpu.VMEM((3, D, D), F32), pltpu.SemaphoreType.DMA((13,))],
        compiler_params=_params(("arbitrary",)),
    )(gr, ga, p, p, x, tgt, b_gate, gate, g_final, w3)


def _pieces_steps(pieces):
    out, s0 = [], 0
    for a in pieces:
        n = a.shape[1] // D
        out.append((s0, n))
        s0 += n
    return out, s0


def _inproj_bwd_x(pieces, wg, wsh, x, dx2, gn, scale, sums, tm=512):
    s = x.shape[0]
    np_ = len(pieces)
    na = len(sums)
    ni = s // tm
    groups, cur, width = [], [], 0
    for t, a in enumerate(pieces):
        cur.append(t)
        width += a.shape[1]
        if width == PW:
            groups.append(cur)
            cur, width = [], 0
    assert len(groups) == 4 and not cur

    def body(*refs):
        d_refs = refs[:np_]
        w_hbm, wsh_hbm, x_ref, dx2_ref, gn_ref, sc_ref = refs[np_:np_ + 6]
        q_refs = refs[np_ + 6:np_ + 6 + na]
        gx_ref, vec_ref = refs[np_ + 6 + na:np_ + 8 + na]
        r_refs = refs[np_ + 8 + na:np_ + 8 + 2 * na]
        w_s, wsem, ss, rs = refs[np_ + 8 + 2 * na:]
        i = pl.program_id(0)
        my_shard = 2 * lax.axis_index("x") + lax.axis_index("y")

        def scatter_copies():
            cx, cy, cc = _coords()
            j = 2 * cx + cy
            cps = []
            for t, (q, r) in enumerate(zip(q_refs, r_refs)):
                for e, (kx, ky) in enumerate(((1, 0), (0, 1), (1, 1))):
                    cps.append(_rcopy(q.at[j ^ (2 * kx + ky)], r.at[e], ss.at[3 * t + e], rs.at[3 * t + e],
                                      (_flip(cx, kx), _flip(cy, ky), cc)))
            return cps

        def w_copy(pc):
            return pltpu.make_async_copy(w_hbm.at[pc, pl.ds(0, D), :], w_s.at[pc], wsem.at[pc])

        for pc in range(4):
            @pl.when((i == 0) & (my_shard != pc))
            def _(pc=pc):
                w_copy(pc).start()

            @pl.when((i == 0) & (my_shard == pc))
            def _(pc=pc):
                pltpu.make_async_copy(wsh_hbm, w_s.at[pc], wsem.at[pc]).start()

        @pl.when(i == 0)
        def _():
            vec_ref[...] = jnp.zeros_like(vec_ref)
            for cp in scatter_copies():
                cp.start()

        dh = None
        for pc, group in enumerate(groups):
            @pl.when(i == 0)
            def _(pc=pc):
                w_copy(pc).wait()

            tiles = [d_refs[t][...] for t in group]
            lhs = tiles[0] if len(tiles) == 1 else jnp.concatenate(tiles, axis=1)
            part = _dot_nt(lhs, w_s[pc])
            dh = part if dh is None else dh + part

        xt = x_ref[...]
        rstd = lax.rsqrt(jnp.mean(xt * xt, axis=-1, keepdims=True) + EPS)
        xh = xt * rstd
        gn_v = gn_ref[...]
        sc1 = 1.0 + sc_ref[...]
        dhx = dh * xh
        vec_ref[0:1, :] += _colsum(dh)
        vec_ref[1:2, :] += _colsum(dhx) * gn_v
        vec_ref[2:3, :] += _colsum(dhx) * sc1
        dxh = dh * (gn_v * sc1)
        gx_ref[...] = rstd * (dxh - xh * jnp.mean(dxh * xh, axis=-1, keepdims=True)) + dx2_ref[...]

        @pl.when(i == ni - 1)
        def _():
            for cp in scatter_copies():
                cp.wait()

    rowt = lambda i: (i, 0)
    row = pl.BlockSpec((1, D), lambda i: (0, 0))
    any_ = pl.BlockSpec(memory_space=pl.ANY)
    outs = pl.pallas_call(
        body, name="inproj_bwd_x", grid=(ni,),
        in_specs=[pl.BlockSpec((tm, a.shape[1]), rowt) for a in pieces] +
                 [any_, any_, pl.BlockSpec((tm, D), rowt), pl.BlockSpec((tm, D), rowt), row, row] + [any_] * na,
        out_specs=(pl.BlockSpec((tm, D), rowt), pl.BlockSpec((8, D), lambda i: (0, 0))) + (any_,) * na,
        out_shape=(jax.ShapeDtypeStruct((s, D), F32), jax.ShapeDtypeStruct((8, D), F32)) +
                  tuple(jax.ShapeDtypeStruct((3,) + q.shape[1:], q.dtype) for q in sums),
        scratch_shapes=[pltpu.VMEM((4, D, PW), BF16), pltpu.SemaphoreType.DMA((4,)),
                        pltpu.SemaphoreType.DMA((3 * na,)), pltpu.SemaphoreType.DMA((3 * na,))],
        compiler_params=_params(("arbitrary",)),
    )(*pieces, wg, wsh, x, dx2, gn, scale, *sums)
    return outs[0], outs[1], outs[2:]


def _inproj_bwd_w(pieces, hbf, g_out, tk=4096):
    s = hbf.shape[0]
    steps, nk = _pieces_steps(pieces)
    npc = PW // D
    ns = s // tk
    np_ = len(pieces)
    hr = D // 2
    ohr = g_out.shape[1] // 2
    ocr = _chunk_rows(g_out)
    ochunks = [(j, r0) for j in range(g_out.shape[0]) for r0 in range(0, ohr, ocr)]
    noc = len(ochunks)

    def body(*refs):
        d_refs = refs[:np_]
        h_ref, go_hbm, g_ref, rb_hbm, rbo_hbm, stage, dbuf, dsem, ss, rs, oss, ors = refs[np_:]
        cb, k = pl.program_id(0), pl.program_id(1)
        cx, cy, cc = _coords()
        sib = (cx, cy, 1 - cc)

        def block_copy(b):
            return _rcopy(stage.at[b % 2],
                          rb_hbm.at[b // npc, :, pl.ds(pl.multiple_of((b % npc) * D, D), D)], ss.at[b], rs.at[b], sib)

        def out_copy(e):
            j, r0 = ochunks[e]
            return _rcopy(go_hbm.at[j, pl.ds((1 - cc) * ohr + r0, ocr), :], rbo_hbm.at[j, pl.ds(r0, ocr), :],
                          oss.at[e], ors.at[e], sib)

        @pl.when((cb == 0) & (k == 0))
        def _():
            for e in range(noc):
                out_copy(e).start()

        @pl.when(k == 0)
        def _():
            g_ref[...] = jnp.zeros_like(g_ref)

        t = cb * ns + k

        def fetch(cb_, k_, slot):
            for (s0, n), d_hbm in zip(steps, d_refs):
                @pl.when((cb_ >= s0) & (cb_ < s0 + n))
                def _(d_hbm=d_hbm, s0=s0):
                    src = d_hbm.at[pl.ds(pl.multiple_of(k_ * tk, tk), tk), pl.ds(pl.multiple_of((cb_ - s0) * D, D), D)]
                    pltpu.make_async_copy(src, dbuf.at[slot], dsem.at[slot]).start()

        @pl.when(t == 0)
        def _():
            fetch(cb, k, 0)

        @pl.when(t + 1 < nk * ns)
        def _():
            fetch((t + 1) // ns, (t + 1) % ns, (t + 1) % 2)

        pltpu.make_async_copy(d_refs[0].at[pl.ds(0, tk), pl.ds(0, D)], dbuf.at[t % 2], dsem.at[t % 2]).wait()
        g_ref[0] += _dot_tn(h_ref[...], dbuf[t % 2])

        @pl.when((k == ns - 1) & (cb > 1))
        def _():
            block_copy(cb - 2).wait_send()

        @pl.when(k == ns - 1)
        def _():
            stage[cb % 2] = g_ref[0, pl.ds(pl.multiple_of((1 - cc) * hr, hr), hr), :]
            block_copy(cb).start()

        @pl.when((k == ns - 1) & (cb == nk - 1))
        def _():
            block_copy(nk - 2).wait_send()
            block_copy(nk - 1).wait_send()
            for b in range(nk):
                block_copy(b).wait_recv()
            for e in range(noc):
                out_copy(e).wait_recv()
                out_copy(e).wait_send()

    any_ = pl.BlockSpec(memory_space=pl.ANY)
    return pl.pallas_call(
        body, name="inproj_bwd_w", grid=(nk, ns),
        in_specs=[any_] * np_ + [pl.BlockSpec((tk, D), lambda cb, k: (k, 0)), any_],
        out_specs=(pl.BlockSpec((1, D, D), lambda cb, k: (cb // npc, 0, cb % npc)), any_, any_),
        out_shape=(jax.ShapeDtypeStruct((4, D, PW), F32), jax.ShapeDtypeStruct((4, hr, PW), F32),
                   jax.ShapeDtypeStruct((g_out.shape[0], ohr, g_out.shape[2]), F32)),
        scratch_shapes=[pltpu.VMEM((2, hr, D), F32), pltpu.VMEM((2, tk, D), BF16), pltpu.SemaphoreType.DMA((2,)),
                        pltpu.SemaphoreType.DMA((nk,)), pltpu.SemaphoreType.DMA((nk,)),
                        pltpu.SemaphoreType.DMA((noc,)), pltpu.SemaphoreType.DMA((noc,))],
        compiler_params=_params(("arbitrary", "arbitrary")),
    )(*pieces, hbf, g_out)


D2D_CHUNK_BYTES = 512 * 1024


def _chunk_rows(a):
    return max(8, D2D_CHUNK_BYTES // (a.shape[-1] * a.dtype.itemsize))


def _pair_swap(arrs):
    na = len(arrs)
    chunks = []
    for t, a in enumerate(arrs):
        cr = _chunk_rows(a)
        chunks += [(t, r0, cr) for r0 in range(0, a.shape[0], cr)]
    nch = len(chunks)

    def body(*refs):
        a_refs = refs[:na]
        o_refs = refs[na:2 * na]
        ss, rs = refs[2 * na:]
        x, y, c = _coords()
        sib = (x, y, 1 - c)
        rcs = []
        for n, (t, r0, cr) in enumerate(chunks):
            rows = pl.ds(r0, cr)
            rc = _rcopy(a_refs[t].at[rows, :], o_refs[t].at[rows, :], ss.at[n], rs.at[n], sib)
            rc.start()
            rcs.append(rc)
        for rc in rcs:
            rc.wait_recv()
        for rc in rcs:
            rc.wait_send()

    any_ = pl.BlockSpec(memory_space=pl.ANY)
    return pl.pallas_call(
        body, name="pair_swap",
        out_shape=tuple(jax.ShapeDtypeStruct(a.shape, a.dtype) for a in arrs),
        in_specs=[any_] * na, out_specs=tuple([any_] * na),
        scratch_shapes=[pltpu.SemaphoreType.DMA((nch,)), pltpu.SemaphoreType.DMA((nch,))],
        compiler_params=_params(),
    )(*arrs)


def _add_half(full, rb, core, tr):
    n, r, cdim = full.shape
    nb = r // 2 // tr

    def body(c_ref, a_ref, b_ref, ob_ref):
        ob_ref[...] = (a_ref[...] + b_ref[...]).astype(BF16)

    mine = pl.BlockSpec((1, tr, cdim), lambda i, j, c_ref: (i, c_ref[0] * nb + j, 0))
    spec = pl.BlockSpec((1, tr, cdim), lambda i, j, c_ref: (i, j, 0))
    return pl.pallas_call(
        body, name="add_half",
        grid_spec=pltpu.PrefetchScalarGridSpec(num_scalar_prefetch=1, grid=(n, nb), in_specs=[mine, spec],
                                               out_specs=spec),
        out_shape=jax.ShapeDtypeStruct(rb.shape, BF16),
        compiler_params=_params(("parallel", "parallel")),
    )(core, full, rb)


def _sum_slots(full, rb, r3, shard_core, tr):
    _, hr, cdim = rb.shape
    nb = hr // tr

    def body(jc_ref, a_ref, b_ref, r_ref, o_ref):
        own = a_ref[0] + b_ref[0]
        o_ref[...] = ((own + r_ref[0].astype(F32)) + r_ref[1].astype(F32)) + r_ref[2].astype(F32)

    return pl.pallas_call(
        body, name="sum_slots",
        grid_spec=pltpu.PrefetchScalarGridSpec(
            num_scalar_prefetch=1, grid=(nb,),
            in_specs=[pl.BlockSpec((1, tr, cdim), lambda i, jc: (jc[0], jc[1] * nb + i, 0)),
                      pl.BlockSpec((1, tr, cdim), lambda i, jc: (jc[0], i, 0)),
                      pl.BlockSpec((3, tr, cdim), lambda i, jc: (0, i, 0))],
            out_specs=pl.BlockSpec((tr, cdim), lambda i, jc: (i, 0))),
        out_shape=jax.ShapeDtypeStruct((hr, cdim), F32),
        compiler_params=_params(("parallel",)),
    )(shard_core, full, rb, r3)


def _allreduce_phase(phase, p_refs, o_refs, rbufs, s1, r1, s2, r2, out_refs=None):
    me = _my_index()

    def chunk(t, d):
        ch = p_refs[t].shape[0] // NDEV
        return pl.ds(pl.multiple_of(d * ch, 8), ch)

    def scatter(t, k):
        e = 7 * t + k - 1
        return _rcopy(p_refs[t].at[chunk(t, me ^ k)], rbufs[t].at[me], s1.at[e], r1.at[e], _peer(k))

    def gather(t, k):
        e = 7 * t + k - 1
        return _rcopy(o_refs[t].at[chunk(t, me)], o_refs[t].at[chunk(t, me)], s2.at[e], r2.at[e], _peer(k))

    for t in range(len(p_refs)):
        if phase == 0:
            for k in range(1, NDEV):
                scatter(t, k).start()
            rbufs[t][me] = p_refs[t][chunk(t, me), :]
        elif phase == 1:
            for k in range(1, NDEV):
                e = 7 * t + k - 1
                _rcopy(p_refs[t].at[chunk(t, me)], rbufs[t].at[me ^ k], s1.at[e], r1.at[e], _peer(k)).wait_recv()
            tot = rbufs[t][0]
            for d in range(1, NDEV):
                tot = tot + rbufs[t][d]
            o_refs[t][chunk(t, me), :] = tot
            for k in range(1, NDEV):
                gather(t, k).start()
        else:
            for k in range(1, NDEV):
                e = 7 * t + k - 1
                _rcopy(o_refs[t].at[chunk(t, me)], o_refs[t].at[chunk(t, me ^ k)], s2.at[e], r2.at[e],
                       _peer(k)).wait_recv()
            for k in range(1, NDEV):
                scatter(t, k).wait_send()
                gather(t, k).wait_send()
            if out_refs is not None:
                out_refs[t][...] = o_refs[t][...]


def _adamw_update(w, g, m, v):
    nm = B1 * m + (1.0 - B1) * g
    nv = B2 * v + (1.0 - B2) * (g * g)
    m_hat = nm / (1.0 - B1 ** STEP)
    v_hat = nv / (1.0 - B2 ** STEP)
    return -LR * (m_hat / (jnp.sqrt(v_hat) + ADAM_EPS) + WD * w), nm, nv


def _adamw(w, g, m, v, tr):
    r, cdim = w.shape

    def body(w_ref, g_ref, m_ref, v_ref, d_ref, nm_ref, nv_ref):
        d_ref[...], nm_ref[...], nv_ref[...] = _adamw_update(w_ref[...], g_ref[...], m_ref[...], v_ref[...])

    spec = pl.BlockSpec((tr, cdim), lambda i: (i, 0))
    sd = jax.ShapeDtypeStruct((r, cdim), F32)
    return pl.pallas_call(
        body, name="adamw", grid=(r // tr,), in_specs=[spec] * 4, out_specs=(spec,) * 3, out_shape=(sd,) * 3,
        compiler_params=_params(("parallel",)),
    )(w, g, m, v)


def _adamw_halves(w, mine, theirs, m, v, core, tr):
    r, cdim = w.shape
    nbh = r // 2 // tr

    def body(c_ref, w_ref, a_ref, b_ref, m_ref, v_ref, g_ref, d_ref, nm_ref, nv_ref):
        is_mine = pl.program_id(0) // nbh == c_ref[0]
        g = jnp.where(is_mine, a_ref[...], b_ref[...])
        g_ref[...] = g
        d_ref[...], nm_ref[...], nv_ref[...] = _adamw_update(w_ref[...], g, m_ref[...], v_ref[...])

    spec = pl.BlockSpec((tr, cdim), lambda i, c: (i, 0))
    half = lambda own: pl.BlockSpec(
        (tr, cdim), lambda i, c: (jnp.clip(i - (c[0] if own else 1 - c[0]) * nbh, 0, nbh - 1), 0))
    sd = jax.ShapeDtypeStruct((r, cdim), F32)
    return pl.pallas_call(
        body, name="adamw_halves",
        grid_spec=pltpu.PrefetchScalarGridSpec(num_scalar_prefetch=1, grid=(r // tr,),
                                               in_specs=[spec, half(True), half(False), spec, spec],
                                               out_specs=(spec,) * 4),
        out_shape=(sd,) * 4,
        compiler_params=_params(("parallel",)),
    )(core, w, mine, theirs, m, v)


V_B_GATE, V_CONV_B, V_LAM, V_G_FINAL, V_CONV_W, V_LOSS, V_ROWS = 0, 2, 3, 4, 5, 9, 64
M_W_A, M_W_X, M_B_A, M_B_X, M_ROWS = 0, H * DH, 2 * H * DH, 2 * H * DH + H, 2112
SMALL = ("g_norm", "b_mod", "b_gate", "conv_b", "lam", "g_final", "conv_w", "w_a", "w_x", "b_a", "b_x")


def _adamw_small(redv, redm, g_conv, g_gnorm, g_bmod, wmv):
    def grad(name, rv, rm, gc, gg, gb):
        if name == "g_norm":
            return gg[...]
        if name == "b_mod":
            return gb[...]
        if name == "b_gate":
            return jnp.concatenate([rv[V_B_GATE + t:V_B_GATE + t + 1, :] for t in range(2)], axis=1)
        if name == "conv_b":
            return rv[V_CONV_B:V_CONV_B + 1, :]
        if name == "lam":
            return rv[V_LAM:V_LAM + 1, :]
        if name == "g_final":
            return rv[V_G_FINAL:V_G_FINAL + 1, :]
        if name == "conv_w":
            return gc[...]
        if name == "w_a":
            return rm[M_W_A:M_W_A + H * DH, :]
        if name == "w_x":
            return rm[M_W_X:M_W_X + H * DH, :]
        if name == "b_a":
            return rm[M_B_A:M_B_A + H, :]
        return rm[M_B_X:M_B_X + H, :]

    n = len(SMALL)

    def body(*refs):
        rv, rm, gc, gg, gb = refs[:5]
        ins, outs = refs[5:5 + 3 * n], refs[5 + 3 * n:]
        for t, name in enumerate(SMALL):
            w_ref, m_ref, v_ref = ins[3 * t:3 * t + 3]
            g_out, d_out, m_out, v_out = outs[4 * t:4 * t + 4]
            g = grad(name, rv, rm, gc, gg, gb)
            g_out[...] = g
            d_out[...], m_out[...], v_out[...] = _adamw_update(w_ref[...], g, m_ref[...], v_ref[...])

    vm = pl.BlockSpec(memory_space=pltpu.VMEM)
    flat = [a for name in SMALL for a in wmv[name]]
    shapes = [jax.ShapeDtypeStruct(wmv[name][0].shape, F32) for name in SMALL for _ in range(4)]
    outs = pl.pallas_call(
        body, name="adamw_small", out_shape=tuple(shapes),
        in_specs=[vm] * (5 + len(flat)), out_specs=tuple([vm] * len(shapes)),
        compiler_params=_params(),
    )(redv, redm, g_conv, g_gnorm, g_bmod, *flat)
    return {name: outs[4 * t:4 * t + 4] for t, name in enumerate(SMALL)}


def _rope_tables(positions):
    inv_freq = ROPE_THETA ** (-jnp.arange(0, ROT, 2, dtype=F32) / ROT)
    ang = positions.astype(F32)[:, None] * inv_freq
    cos, sin = lax.optimization_barrier((jnp.cos(ang), jnp.sin(ang)))
    n = positions.shape[0]
    half = ROT // 2
    rc = jnp.concatenate([cos, cos, jnp.ones((n, DH - ROT), F32)], axis=1)
    rsa = jnp.concatenate([-sin, jnp.zeros((n, DH - half), F32)], axis=1)
    rsb = jnp.concatenate([jnp.zeros((n, half), F32), sin, jnp.zeros((n, DH - ROT), F32)], axis=1)
    return rc, rsa, rsb


def kernel(x, c, positions, g_norm, w_mod, b_mod, w_in, b_gate, conv_w, conv_b, w_a, b_a, w_x, b_x, lam, w_out_rnn, w_out_attn, w_o, g_final, loss_target, m_g_norm, m_w_mod, m_b_mod, m_w_in, m_b_gate, m_conv_w, m_conv_b, m_w_a, m_b_a, m_w_x, m_b_x, m_lam, m_w_out_rnn, m_w_out_attn, m_w_o, m_g_final, v_g_norm, v_w_mod, v_b_mod, v_w_in, v_b_gate, v_conv_w, v_conv_b, v_w_a, v_b_a, v_w_x, v_b_x, v_lam, v_w_out_rnn, v_w_out_attn, v_w_o, v_g_final):
    s = x.shape[1]
    xi = lax.axis_index("x")
    yi = lax.axis_index("y")
    ci = lax.axis_index("c")
    shard = 2 * xi + yi
    x2d = x[0]
    tgt = loss_target[0]
    pos = positions[0]

    c_all, mod4, conv_all = _mod_fwd(c, w_mod[0], b_mod.reshape(4, 1, 768), conv_w[0])
    mod = mod4.reshape(1, 3 * D)
    shift, scale, gate = mod[:, :D], mod[:, D:2 * D], mod[:, 2 * D:]
    w3sh = jnp.concatenate([w_out_rnn[0], w_out_attn[0], w_o[0]], axis=0).astype(BF16)
    wsh = w_in[0].astype(BF16)
    conv_full = conv_all[0::2].transpose(1, 0, 2).reshape(4, D)

    order = jnp.stack([shard, shard ^ 2, shard ^ 1, shard ^ 3]).astype(jnp.int32)
    p, hbf, wg = _gather_norm_inproj(x2d, g_norm, shift, scale, wsh, order)
    rc, rsa, rsb = _rope_tables(pos)
    pos_col = jnp.broadcast_to((pos == 0).astype(F32)[:, None], (s, DH))
    b_a3, b_x3 = b_a.reshape(H, 1, DH), b_x.reshape(H, 1, DH)
    hr, gr = _rnn_fwd(p, pos_col, conv_full, conv_b, w_a[0], b_a3, w_x[0], b_x3, lam)
    o, lse, ga, q_rot, k_rot, w3g = _attn_fwd(p, rc, rsa, rsb, w3sh)
    w3g = lax.dynamic_update_slice(w3g, w3sh[None], (shard, 0, 0))
    w3 = w3g.reshape(4, 3, 256, D).transpose(1, 0, 2, 3).reshape(3, D, D)

    dgr, dga, dc, dx2, vec_t, g_out = _tail(gr, ga, p, x2d, tgt, w3, b_gate, gate, g_final.reshape(1, D))

    dxr, dzr, g_wa, g_ba, g_wx, g_bx, g_lam, g_cw, g_cb = _rnn_bwd(
        p, hr, dgr, pos_col, conv_full, conv_b, w_a[0], b_a3, w_x[0], b_x3, lam)
    vpack = jnp.concatenate([
        vec_t[2:4],
        g_cb.reshape(1, D),
        g_lam.reshape(1, D),
        vec_t[0:1],
        g_cw.transpose(1, 0, 2).reshape(4, D),
        vec_t[4:5],
        jnp.zeros((V_ROWS - 10, D), F32)], axis=0)
    mpack = jnp.concatenate([
        g_wa.reshape(H * DH, DH), g_wx.reshape(H * DH, DH), g_ba.reshape(H, DH), g_bx.reshape(H, DH),
        jnp.zeros((M_ROWS - 2 * H * DH - 2 * H, DH), F32)], axis=0)
    (dq, dk, dv, dza), (redv, redm) = _attn_bwd(p, q_rot, k_rot, o, lse, dga, rc, rsa, rsb, [vpack, mpack])

    pieces = [dxr, dzr, dq, dk, dv, dza, dc]
    g_win, rb_a, rb_b = _inproj_bwd_w(pieces, hbf, g_out)

    core = ci.reshape(1)
    shard_core = jnp.stack([shard, ci]).astype(jnp.int32)
    qh_a, qh_b = _add_half(g_win, rb_a, core, tr=256), _add_half(g_out, rb_b, core, tr=384)
    grad_x, vec_n, (r_a, r_b) = _inproj_bwd_x(pieces, wg, wsh, x2d, dx2, g_norm, scale, [qh_a, qh_b])
    f_a = _sum_slots(g_win, rb_a, r_a, shard_core, tr=256)
    f_b = _sum_slots(g_out, rb_b, r_b, shard_core, tr=384)
    s_a, s_b = _pair_swap([f_a, f_b])

    dmod_row = jnp.concatenate([vec_n[0:1], vec_n[1:2], vec_t[1:2]], axis=1)
    loss = redv[V_LOSS, 0]
    grad_w_mod, g_bmod4, g_gnorm = _mod_bwd(dmod_row.reshape(4, 1, 768), vec_n[2:3], c_all)
    g_conv_sh = lax.dynamic_slice_in_dim(redv[V_CONV_W:V_CONV_W + 4], shard * 256, 256, axis=1)

    shape2d = dict(g_norm=(1, D), b_mod=(1, 3 * D), b_gate=(1, 2 * D), conv_b=(1, D), lam=(1, D), g_final=(1, D),
                   conv_w=(4, 256), w_a=(H * DH, DH), w_x=(H * DH, DH), b_a=(H, DH), b_x=(H, DH))
    given = dict(
        g_norm=(g_norm, m_g_norm, v_g_norm), b_mod=(b_mod, m_b_mod, v_b_mod), b_gate=(b_gate, m_b_gate, v_b_gate),
        conv_b=(conv_b, m_conv_b, v_conv_b), lam=(lam, m_lam, v_lam), g_final=(g_final, m_g_final, v_g_final),
        conv_w=(conv_w, m_conv_w, v_conv_w), w_a=(w_a, m_w_a, v_w_a), w_x=(w_x, m_w_x, v_w_x),
        b_a=(b_a, m_b_a, v_b_a), b_x=(b_x, m_b_x, v_b_x))
    small = _adamw_small(redv, redm, g_conv_sh, g_gnorm, g_bmod4.reshape(1, 3 * D),
                         {n: tuple(a.reshape(shape2d[n]) for a in given[n]) for n in SMALL})

    big_in = _adamw_halves(w_in[0], f_a, s_a, m_w_in[0], v_w_in[0], core, tr=256)
    big_mod = (grad_w_mod,) + tuple(_adamw(w_mod[0], grad_w_mod, m_w_mod[0], v_w_mod[0], tr=256))
    w3f = jnp.concatenate([w_out_rnn[0], w_out_attn[0], w_o[0]], axis=0)
    m3f = jnp.concatenate([m_w_out_rnn[0], m_w_out_attn[0], m_w_o[0]], axis=0)
    v3f = jnp.concatenate([v_w_out_rnn[0], v_w_out_attn[0], v_w_o[0]], axis=0)
    big_out = _adamw_halves(w3f, f_b, s_b, m3f, v3f, core, tr=384)

    names = ["g_norm", "w_mod", "b_mod", "w_in", "b_gate", "conv_w", "conv_b", "w_a", "b_a", "w_x", "b_x", "lam",
             "w_out_rnn", "w_out_attn", "w_o", "g_final"]
    outs = []
    for idx in range(4):
        d = {n: small[n][idx].reshape(given[n][0].shape) for n in SMALL}
        d.update(w_mod=big_mod[idx][None], w_in=big_in[idx][None],
                 w_out_rnn=big_out[idx][0:256][None], w_out_attn=big_out[idx][256:512][None],
                 w_o=big_out[idx][512:768][None])
        outs.append(d)
    flat = [d[n] for d in outs for n in names]
    return (loss, grad_x[None], *flat)
```
